```python
import jax, jax.numpy as jnp
from jax import lax
import numpy as np

D_MODEL = 2048
BATCH = 8
SEQ = 4096
DEPTH = 1

PLE_DIM = 256
ATTN_WIDTH = D_MODEL // 2
CONV_WIDTH = D_MODEL - ATTN_WIDTH
HEAD_DIM = 64
N_Q_HEADS = ATTN_WIDTH // HEAD_DIM
N_KV_HEADS = 4
GQA_GROUP = N_Q_HEADS // N_KV_HEADS
KV_WIDTH = N_KV_HEADS * HEAD_DIM
WINDOW = 128
BLOCK = 128
ROT_DIM = HEAD_DIM // 4
ROPE_THETA = 500000.0
CONV_K = 3
CONV_GROUPS = 16
EPS = 1e-6
NEG_INF = -1e30

SEG_WIDTHS = (ATTN_WIDTH, KV_WIDTH, KV_WIDTH, ATTN_WIDTH,
              CONV_WIDTH, CONV_WIDTH, CONV_WIDTH, CONV_WIDTH)
IN_WIDTH = sum(SEG_WIDTHS)
SPLITS = [int(v) for v in np.cumsum(SEG_WIDTHS)[:-1]]

kernel_name = "hymba_swa_sink_shortconv_ple"


def rms_norm(x, gain):
    xf = x.astype(jnp.float32)
    y = xf * lax.rsqrt(jnp.mean(xf * xf, axis=-1, keepdims=True) + EPS)
    return (y * gain.astype(jnp.float32)).astype(x.dtype)


def partial_rope(x, pos):
    half = ROT_DIM // 2
    inv_freq = jnp.power(jnp.float32(ROPE_THETA), -jnp.arange(half, dtype=jnp.float32) * 2.0 / ROT_DIM)
    ang = pos.astype(jnp.float32)[:, None] * inv_freq[None, :]
    cos = jnp.cos(ang)[None, :, None, :]
    sin = jnp.sin(ang)[None, :, None, :]
    xr = x[..., :ROT_DIM].astype(jnp.float32)
    x1, x2 = xr[..., :half], xr[..., half:]
    rot = jnp.concatenate([x1 * cos - x2 * sin, x2 * cos + x1 * sin], axis=-1).astype(x.dtype)
    return jnp.concatenate([rot, x[..., ROT_DIM:]], axis=-1)


def sliding_window_attention(q, k, v, sinks):
    b, s = q.shape[0], q.shape[1]
    nb = s // BLOCK
    qb = q.reshape(b, nb, BLOCK, N_KV_HEADS, GQA_GROUP, HEAD_DIM)

    def band(t):
        tb = t.reshape(b, nb, BLOCK, N_KV_HEADS, HEAD_DIM)
        prev = jnp.pad(tb, ((0, 0), (1, 0), (0, 0), (0, 0), (0, 0)))[:, :-1]
        return jnp.concatenate([prev, tb], axis=2)

    kb, vb = band(k), band(v)
    scores = jnp.einsum('bnqhgd,bnkhd->bnhgqk', qb, kb,
                        preferred_element_type=jnp.float32) * (HEAD_DIM ** -0.5)
    q_pos = jnp.arange(nb)[:, None] * BLOCK + jnp.arange(BLOCK)[None, :]
    k_pos = jnp.arange(nb)[:, None] * BLOCK - BLOCK + jnp.arange(2 * BLOCK)[None, :]
    diff = q_pos[:, :, None] - k_pos[:, None, :]
    mask = (diff >= 0) & (diff < WINDOW) & (k_pos[:, None, :] >= 0)
    scores = jnp.where(mask[None, :, None, None], scores, NEG_INF)
    sink = sinks.astype(jnp.float32).reshape(N_KV_HEADS, GQA_GROUP)[None, None, :, :, None, None]
    m = jnp.maximum(jnp.max(scores, axis=-1, keepdims=True), sink)
    e = jnp.exp(scores - m)
    probs = e / (jnp.sum(e, axis=-1, keepdims=True) + jnp.exp(sink - m))
    out = jnp.einsum('bnhgqk,bnkhd->bnqhgd', probs.astype(v.dtype), vb)
    return out.reshape(b, s, N_Q_HEADS * HEAD_DIM)


def short_conv(u, w):
    return lax.conv_general_dilated(
        u, w[:, None, :].astype(u.dtype), window_strides=(1,),
        padding=[(CONV_K - 1, 0)], dimension_numbers=('NWC', 'WIO', 'NWC'),
        feature_group_count=u.shape[-1])


def _fwd_setup_inputs(seed: int = 0) -> dict:
    key = jax.random.key(seed)
    ks = jax.random.split(key, 16)
    f32 = jnp.float32
    nrm = lambda k, shape, scale: jax.random.normal(k, shape, f32) * scale
    return {
        "x": nrm(ks[0], (BATCH, SEQ, D_MODEL), 1.0),
        "p": nrm(ks[1], (DEPTH, BATCH, SEQ, PLE_DIM), 1.0),
        "norm_gain": 1.0 + nrm(ks[2], (DEPTH, D_MODEL), 0.02),
        "w_in": nrm(ks[3], (DEPTH, D_MODEL, IN_WIDTH), D_MODEL ** -0.5),
        "q_norm_gain": 1.0 + nrm(ks[4], (DEPTH, HEAD_DIM), 0.02),
        "k_norm_gain": 1.0 + nrm(ks[5], (DEPTH, HEAD_DIM), 0.02),
        "attn_sinks": nrm(ks[6], (DEPTH, N_Q_HEADS), 0.5),
        "conv_w": nrm(ks[7], (DEPTH, CONV_K, CONV_WIDTH), CONV_K ** -0.5),
        "w_out": nrm(ks[8], (DEPTH, D_MODEL, D_MODEL), D_MODEL ** -0.5),
        "ple_gate_norm_gain": 1.0 + nrm(ks[9], (DEPTH, D_MODEL), 0.02),
        "w_ple_gate": nrm(ks[10], (DEPTH, D_MODEL, D_MODEL), D_MODEL ** -0.5),
        "b_ple_gate": nrm(ks[11], (DEPTH, D_MODEL), 0.01),
        "w_ple_proj": nrm(ks[12], (DEPTH, PLE_DIM, D_MODEL), PLE_DIM ** -0.5),
        "ple_norm_gain": 1.0 + nrm(ks[13], (DEPTH, D_MODEL), 0.02),
    }


def _fwd_reference(x, p, norm_gain, w_in, q_norm_gain, k_norm_gain, attn_sinks, conv_w, w_out,
              ple_gate_norm_gain, w_ple_gate, b_ple_gate, w_ple_proj, ple_norm_gain):
    b, s = x.shape[0], x.shape[1]
    pos = jnp.arange(s)
    for i in range(DEPTH):
        h = rms_norm(x, norm_gain[i])
        z = h @ w_in[i]
        q, k, v, g_attn, c_b, c_c, c_h, g_conv = jnp.split(z, SPLITS, axis=-1)

        q = q.reshape(b, s, N_Q_HEADS, HEAD_DIM)
        k = k.reshape(b, s, N_KV_HEADS, HEAD_DIM)
        v = v.reshape(b, s, N_KV_HEADS, HEAD_DIM)
        q = partial_rope(rms_norm(q, q_norm_gain[i]), pos)
        k = partial_rope(rms_norm(k, k_norm_gain[i]), pos)
        y_attn = sliding_window_attention(q, k, v, attn_sinks[i]) * jax.nn.silu(g_attn)

        y_conv = c_b * short_conv(c_c * c_h, conv_w[i]) * jax.nn.silu(g_conv)

        mix = jnp.concatenate([y_attn, y_conv], axis=-1)
        x = x + mix @ w_out[i]

        gate = jax.nn.sigmoid(rms_norm(x, ple_gate_norm_gain[i]) @ w_ple_gate[i] + b_ple_gate[i])
        e = rms_norm(p[i] @ w_ple_proj[i], ple_norm_gain[i])
        x = x + gate * e
    return x


import jax as _jax
import jax.numpy as _jnp

TWIN_FORMAT = 'train_step'
FWD_PARAMS = ['x', 'p', 'norm_gain', 'w_in', 'q_norm_gain', 'k_norm_gain', 'attn_sinks', 'conv_w', 'w_out', 'ple_gate_norm_gain', 'w_ple_gate', 'b_ple_gate', 'w_ple_proj', 'ple_norm_gain']
TWIN_WEIGHTS = ['norm_gain', 'w_in', 'q_norm_gain', 'k_norm_gain', 'attn_sinks', 'conv_w', 'w_out', 'ple_gate_norm_gain', 'w_ple_gate', 'b_ple_gate', 'w_ple_proj', 'ple_norm_gain']
TWIN_DIFF_INPUT = 'x'
TWIN_INPUTS = ['x', 'p', 'norm_gain', 'w_in', 'q_norm_gain', 'k_norm_gain', 'attn_sinks', 'conv_w', 'w_out', 'ple_gate_norm_gain', 'w_ple_gate', 'b_ple_gate', 'w_ple_proj', 'ple_norm_gain', 'loss_target', 'm_norm_gain', 'm_w_in', 'm_q_norm_gain', 'm_k_norm_gain', 'm_attn_sinks', 'm_conv_w', 'm_w_out', 'm_ple_gate_norm_gain', 'm_w_ple_gate', 'm_b_ple_gate', 'm_w_ple_proj', 'm_ple_norm_gain', 'v_norm_gain', 'v_w_in', 'v_q_norm_gain', 'v_k_norm_gain', 'v_attn_sinks', 'v_conv_w', 'v_w_out', 'v_ple_gate_norm_gain', 'v_w_ple_gate', 'v_b_ple_gate', 'v_w_ple_proj', 'v_ple_norm_gain']
TWIN_OUTPUTS = ['loss', 'grad_x', 'grad_norm_gain', 'grad_w_in', 'grad_q_norm_gain', 'grad_k_norm_gain', 'grad_attn_sinks', 'grad_conv_w', 'grad_w_out', 'grad_ple_gate_norm_gain', 'grad_w_ple_gate', 'grad_b_ple_gate', 'grad_w_ple_proj', 'grad_ple_norm_gain', 'delta_norm_gain', 'delta_w_in', 'delta_q_norm_gain', 'delta_k_norm_gain', 'delta_attn_sinks', 'delta_conv_w', 'delta_w_out', 'delta_ple_gate_norm_gain', 'delta_w_ple_gate', 'delta_b_ple_gate', 'delta_w_ple_proj', 'delta_ple_norm_gain', 'new_m_norm_gain', 'new_m_w_in', 'new_m_q_norm_gain', 'new_m_k_norm_gain', 'new_m_attn_sinks', 'new_m_conv_w', 'new_m_w_out', 'new_m_ple_gate_norm_gain', 'new_m_w_ple_gate', 'new_m_b_ple_gate', 'new_m_w_ple_proj', 'new_m_ple_norm_gain', 'new_v_norm_gain', 'new_v_w_in', 'new_v_q_norm_gain', 'new_v_k_norm_gain', 'new_v_attn_sinks', 'new_v_conv_w', 'new_v_w_out', 'new_v_ple_gate_norm_gain', 'new_v_w_ple_gate', 'new_v_b_ple_gate', 'new_v_w_ple_proj', 'new_v_ple_norm_gain']
TWIN_LEAF_KINDS = {'loss': 'loss', 'grad_x': 'grad_x', 'grad_norm_gain': 'grad_w', 'grad_w_in': 'grad_w', 'grad_q_norm_gain': 'grad_w', 'grad_k_norm_gain': 'grad_w', 'grad_attn_sinks': 'grad_w', 'grad_conv_w': 'grad_w', 'grad_w_out': 'grad_w', 'grad_ple_gate_norm_gain': 'grad_w', 'grad_w_ple_gate': 'grad_w', 'grad_b_ple_gate': 'grad_w', 'grad_w_ple_proj': 'grad_w', 'grad_ple_norm_gain': 'grad_w', 'delta_norm_gain': 'delta_w', 'delta_w_in': 'delta_w', 'delta_q_norm_gain': 'delta_w', 'delta_k_norm_gain': 'delta_w', 'delta_attn_sinks': 'delta_w', 'delta_conv_w': 'delta_w', 'delta_w_out': 'delta_w', 'delta_ple_gate_norm_gain': 'delta_w', 'delta_w_ple_gate': 'delta_w', 'delta_b_ple_gate': 'delta_w', 'delta_w_ple_proj': 'delta_w', 'delta_ple_norm_gain': 'delta_w', 'new_m_norm_gain': 'new_m', 'new_m_w_in': 'new_m', 'new_m_q_norm_gain': 'new_m', 'new_m_k_norm_gain': 'new_m', 'new_m_attn_sinks': 'new_m', 'new_m_conv_w': 'new_m', 'new_m_w_out': 'new_m', 'new_m_ple_gate_norm_gain': 'new_m', 'new_m_w_ple_gate': 'new_m', 'new_m_b_ple_gate': 'new_m', 'new_m_w_ple_proj': 'new_m', 'new_m_ple_norm_gain': 'new_m', 'new_v_norm_gain': 'new_v', 'new_v_w_in': 'new_v', 'new_v_q_norm_gain': 'new_v', 'new_v_k_norm_gain': 'new_v', 'new_v_attn_sinks': 'new_v', 'new_v_conv_w': 'new_v', 'new_v_w_out': 'new_v', 'new_v_ple_gate_norm_gain': 'new_v', 'new_v_w_ple_gate': 'new_v', 'new_v_b_ple_gate': 'new_v', 'new_v_w_ple_proj': 'new_v', 'new_v_ple_norm_gain': 'new_v'}


def _forward(args):
    return _fwd_reference(*[args[k] for k in FWD_PARAMS])


def _output_shape():
    out = _jax.eval_shape(lambda: _forward(_fwd_setup_inputs(0)))
    return out.shape, out.dtype

N_MICROBATCH = 1
ADAM_LR = 0.001
ADAM_B1 = 0.9
ADAM_B2 = 0.999
ADAM_EPS = 1e-08
ADAM_WD = 0.01
ADAM_STEP = 10
PER_EXAMPLE_BATCH_AXIS = {'x': 0, 'p': 1, 'loss_target': 0}
SHARED_INPUTS = []
_WEIGHT_DTYPES = {'norm_gain': _jnp.float32, 'w_in': _jnp.float32, 'q_norm_gain': _jnp.float32, 'k_norm_gain': _jnp.float32, 'attn_sinks': _jnp.float32, 'conv_w': _jnp.float32, 'w_out': _jnp.float32, 'ple_gate_norm_gain': _jnp.float32, 'w_ple_gate': _jnp.float32, 'b_ple_gate': _jnp.float32, 'w_ple_proj': _jnp.float32, 'ple_norm_gain': _jnp.float32}
MOMENT_SCALE = {'norm_gain': 1.116165e+01, 'w_in': 2.723996e-01, 'q_norm_gain': 1.009766e+00, 'k_norm_gain': 1.000861e+00, 'attn_sinks': 1.378021e-01, 'conv_w': 2.821063e+00, 'w_out': 8.664800e-02, 'ple_gate_norm_gain': 4.716764e-01, 'w_ple_gate': 3.301166e-02, 'b_ple_gate': 1.658786e+00, 'w_ple_proj': 7.347662e-02, 'ple_norm_gain': 4.697232e+00}


def _to_microbatches(a, axis):
    t = _jnp.moveaxis(a, axis, 0)
    t = t.reshape((N_MICROBATCH, t.shape[0] // N_MICROBATCH) + t.shape[1:])
    return _jnp.moveaxis(t, 1, axis + 1)


def setup_inputs(seed: int = 0) -> dict:
    inp = _fwd_setup_inputs(seed)
    key = _jax.random.fold_in(_jax.random.key(seed), 7919)
    shape, _ = _output_shape()
    out = dict(inp)
    out["loss_target"] = _jax.random.normal(_jax.random.fold_in(key, 0), shape, _jnp.float32)
    for i, name in enumerate(TWIN_WEIGHTS):
        w = inp[name].astype(_jnp.float32)
        if MOMENT_SCALE is None:
            s = _jnp.sqrt(_jnp.mean(_jnp.square(w)) + 1e-30)
        else:
            s = MOMENT_SCALE[name]
        km, kv = _jax.random.split(_jax.random.fold_in(key, i + 1))
        out[name] = w
        out["m_" + name] = s * _jax.random.normal(km, w.shape, _jnp.float32)
        out["v_" + name] = (s * s) * _jax.random.uniform(kv, w.shape, _jnp.float32, 0.5, 1.5)
    if N_MICROBATCH > 1:
        for name, axis in PER_EXAMPLE_BATCH_AXIS.items():
            out[name] = _to_microbatches(out[name], axis)
    return {'x': out['x'], 'p': out['p'], 'norm_gain': out['norm_gain'], 'w_in': out['w_in'], 'q_norm_gain': out['q_norm_gain'], 'k_norm_gain': out['k_norm_gain'], 'attn_sinks': out['attn_sinks'], 'conv_w': out['conv_w'], 'w_out': out['w_out'], 'ple_gate_norm_gain': out['ple_gate_norm_gain'], 'w_ple_gate': out['w_ple_gate'], 'b_ple_gate': out['b_ple_gate'], 'w_ple_proj': out['w_ple_proj'], 'ple_norm_gain': out['ple_norm_gain'], 'loss_target': out['loss_target'], 'm_norm_gain': out['m_norm_gain'], 'm_w_in': out['m_w_in'], 'm_q_norm_gain': out['m_q_norm_gain'], 'm_k_norm_gain': out['m_k_norm_gain'], 'm_attn_sinks': out['m_attn_sinks'], 'm_conv_w': out['m_conv_w'], 'm_w_out': out['m_w_out'], 'm_ple_gate_norm_gain': out['m_ple_gate_norm_gain'], 'm_w_ple_gate': out['m_w_ple_gate'], 'm_b_ple_gate': out['m_b_ple_gate'], 'm_w_ple_proj': out['m_w_ple_proj'], 'm_ple_norm_gain': out['m_ple_norm_gain'], 'v_norm_gain': out['v_norm_gain'], 'v_w_in': out['v_w_in'], 'v_q_norm_gain': out['v_q_norm_gain'], 'v_k_norm_gain': out['v_k_norm_gain'], 'v_attn_sinks': out['v_attn_sinks'], 'v_conv_w': out['v_conv_w'], 'v_w_out': out['v_w_out'], 'v_ple_gate_norm_gain': out['v_ple_gate_norm_gain'], 'v_w_ple_gate': out['v_w_ple_gate'], 'v_b_ple_gate': out['v_b_ple_gate'], 'v_w_ple_proj': out['v_w_ple_proj'], 'v_ple_norm_gain': out['v_ple_norm_gain']}


def _loss(weights, diff, rest, loss_target):
    with _jax.named_scope("forward"):
        args = {**rest, TWIN_DIFF_INPUT: diff, **{k: w.astype(_WEIGHT_DTYPES[k]) for k, w in weights.items()}}
        y = _forward(args)
    with _jax.named_scope("loss_head"):
        err = _jnp.square(y.astype(_jnp.float32) - loss_target)
        return 0.5 * _jnp.sum(_jnp.mean(err, axis=-1)) if err.ndim else 0.5 * err


def _adamw(w, g, m, v):
    m = ADAM_B1 * m + (1.0 - ADAM_B1) * g
    v = ADAM_B2 * v + (1.0 - ADAM_B2) * _jnp.square(g)
    m_hat = m / (1.0 - ADAM_B1 ** ADAM_STEP)
    v_hat = v / (1.0 - ADAM_B2 ** ADAM_STEP)
    delta = -ADAM_LR * (m_hat / (_jnp.sqrt(v_hat) + ADAM_EPS) + ADAM_WD * w)
    return delta, m, v


def reference(x, p, norm_gain, w_in, q_norm_gain, k_norm_gain, attn_sinks, conv_w, w_out, ple_gate_norm_gain, w_ple_gate, b_ple_gate, w_ple_proj, ple_norm_gain, loss_target, m_norm_gain, m_w_in, m_q_norm_gain, m_k_norm_gain, m_attn_sinks, m_conv_w, m_w_out, m_ple_gate_norm_gain, m_w_ple_gate, m_b_ple_gate, m_w_ple_proj, m_ple_norm_gain, v_norm_gain, v_w_in, v_q_norm_gain, v_k_norm_gain, v_attn_sinks, v_conv_w, v_w_out, v_ple_gate_norm_gain, v_w_ple_gate, v_b_ple_gate, v_w_ple_proj, v_ple_norm_gain):
    given = dict(x=x, p=p, norm_gain=norm_gain, w_in=w_in, q_norm_gain=q_norm_gain, k_norm_gain=k_norm_gain, attn_sinks=attn_sinks, conv_w=conv_w, w_out=w_out, ple_gate_norm_gain=ple_gate_norm_gain, w_ple_gate=w_ple_gate, b_ple_gate=b_ple_gate, w_ple_proj=w_ple_proj, ple_norm_gain=ple_norm_gain, loss_target=loss_target, m_norm_gain=m_norm_gain, m_w_in=m_w_in, m_q_norm_gain=m_q_norm_gain, m_k_norm_gain=m_k_norm_gain, m_attn_sinks=m_attn_sinks, m_conv_w=m_conv_w, m_w_out=m_w_out, m_ple_gate_norm_gain=m_ple_gate_norm_gain, m_w_ple_gate=m_w_ple_gate, m_b_ple_gate=m_b_ple_gate, m_w_ple_proj=m_w_ple_proj, m_ple_norm_gain=m_ple_norm_gain, v_norm_gain=v_norm_gain, v_w_in=v_w_in, v_q_norm_gain=v_q_norm_gain, v_k_norm_gain=v_k_norm_gain, v_attn_sinks=v_attn_sinks, v_conv_w=v_conv_w, v_w_out=v_w_out, v_ple_gate_norm_gain=v_ple_gate_norm_gain, v_w_ple_gate=v_w_ple_gate, v_b_ple_gate=v_b_ple_gate, v_w_ple_proj=v_w_ple_proj, v_ple_norm_gain=v_ple_norm_gain)
    weights = {n: given[n] for n in TWIN_WEIGHTS}
    shared = {n: given[n] for n in SHARED_INPUTS}
    per_example = {n: given[n] for n in ['x', 'p']}
    grad_fn = _jax.value_and_grad(_loss, argnums=(0, 1))

    def one_microbatch(ex, loss_target):
        ex = dict(ex)
        diff = ex.pop(TWIN_DIFF_INPUT)
        return grad_fn(weights, diff, {**shared, **ex}, loss_target)

    if N_MICROBATCH == 1:
        loss, (grad_w, grad_x) = one_microbatch(per_example, given["loss_target"])
    else:
        def body(carry, xs):
            loss_sum, grad_sum = carry
            l_k, (gw_k, gx_k) = one_microbatch(xs[0], xs[1])
            with _jax.named_scope("update"):
                return (loss_sum + l_k, _jax.tree.map(_jnp.add, grad_sum, gw_k)), gx_k

        init = (_jnp.zeros((), _jnp.float32), _jax.tree.map(_jnp.zeros_like, weights))
        (loss, grad_w), grad_x = _jax.lax.scan(body, init, (per_example, given["loss_target"]))
    with _jax.named_scope("update"):
        delta_w, new_m, new_v = {}, {}, {}
        for n in TWIN_WEIGHTS:
            delta_w[n], new_m[n], new_v[n] = _adamw(weights[n], grad_w[n], given["m_" + n], given["v_" + n])
    return (loss, grad_x, *[grad_w[n] for n in TWIN_WEIGHTS], *[delta_w[n] for n in TWIN_WEIGHTS],
            *[new_m[n] for n in TWIN_WEIGHTS], *[new_v[n] for n in TWIN_WEIGHTS])
```

```python
import functools

import jax
import jax.numpy as jnp
from jax import lax
from jax.experimental import pallas as pl
from jax.experimental.pallas import tpu as pltpu

F32 = jnp.float32
BF16 = jnp.bfloat16
MESH = pl.DeviceIdType.MESH

HEAD_DIM = 64
N_KV_HEADS = 4
KV_WIDTH = N_KV_HEADS * HEAD_DIM
BLOCK = 128
ROT_DIM = 16
ROPE_THETA = 500000.0
EPS = 1e-6
NEG_INF = -1e30
N_CHIPS = 4
LANES = 128
SUBLANES = 8
COLT = 512
SEG = 256
VMEM_LIMIT = 48 * 1024 * 1024

ADAM_LR = 0.001
ADAM_B1 = 0.9
ADAM_B2 = 0.999
ADAM_EPS = 1e-08
ADAM_WD = 0.01
ADAM_STEP = 10

NN = (((1,), (0,)), ((), ()))
NT = (((1,), (1,)), ((), ()))
TN = (((0,), (0,)), ((), ()))


def _call(body, **kw):
    return pl.pallas_call(body, **kw)


def _params(sem):
    return pltpu.CompilerParams(dimension_semantics=sem, vmem_limit_bytes=VMEM_LIMIT)


def _tile(n, pref):
    return pref if n % pref == 0 else n


def _sigmoid(v):
    return 1.0 / (1.0 + jnp.exp(-v))


def _hbm():
    return pl.BlockSpec(memory_space=pl.ANY)


def cast_bf16(a, name):
    r, c = a.shape
    tr = _tile(r, 256)

    def body(a_ref, o_ref):
        o_ref[...] = a_ref[...].astype(BF16)

    return _call(body, name=name, grid=(r // tr,),
                 in_specs=[pl.BlockSpec((tr, c), lambda i: (i, 0))],
                 out_specs=pl.BlockSpec((tr, c), lambda i: (i, 0)),
                 out_shape=jax.ShapeDtypeStruct((r, c), BF16),
                 compiler_params=_params(("parallel",)))(a)


def rms_fwd(x, gain, name):
    s, d = x.shape
    tm = _tile(s, 256)

    def body(x_ref, g_ref, o_ref):
        xf = x_ref[...]
        r = lax.rsqrt(jnp.mean(xf * xf, axis=-1, keepdims=True) + EPS)
        o_ref[...] = ((xf * r) * g_ref[...]).astype(BF16)

    return _call(body, name=name, grid=(s // tm,),
                 in_specs=[pl.BlockSpec((tm, d), lambda i: (i, 0)),
                           pl.BlockSpec((1, d), lambda i: (0, 0))],
                 out_specs=pl.BlockSpec((tm, d), lambda i: (i, 0)),
                 out_shape=jax.ShapeDtypeStruct((s, d), BF16),
                 compiler_params=_params(("parallel",)))(x, gain)


def rms_bwd_add(dyn, xin, add, gain, name, want_bf16):
    s, d = xin.shape
    tm = _tile(s, 256)

    def body(dy_ref, x_ref, a_ref, g_ref, *outs):
        i = pl.program_id(0)
        dx_ref, acc_ref = outs[0], outs[-1]
        xf = x_ref[...]
        r = lax.rsqrt(jnp.mean(xf * xf, axis=-1, keepdims=True) + EPS)
        xhat = xf * r
        dyv = dy_ref[...]
        gd = dyv * g_ref[...]
        dx = a_ref[...] + r * (gd - xhat * jnp.mean(xhat * gd, axis=-1, keepdims=True))
        dx_ref[...] = dx
        if want_bf16:
            outs[1][...] = dx.astype(BF16)

        @pl.when(i == 0)
        def _():
            acc_ref[...] = jnp.zeros_like(acc_ref)

        acc_ref[0:1, :] += jnp.sum(dyv * xhat, axis=0, keepdims=True)

    row = pl.BlockSpec((tm, d), lambda i: (i, 0))
    out_specs = [row] + ([row] if want_bf16 else []) + [pl.BlockSpec((SUBLANES, d), lambda i: (0, 0))]
    out_shape = ([jax.ShapeDtypeStruct((s, d), F32)]
                 + ([jax.ShapeDtypeStruct((s, d), BF16)] if want_bf16 else [])
                 + [jax.ShapeDtypeStruct((SUBLANES, d), F32)])
    return _call(body, name=name, grid=(s // tm,),
                 in_specs=[row, row, row, pl.BlockSpec((1, d), lambda i: (0, 0))],
                 out_specs=out_specs, out_shape=out_shape,
                 compiler_params=_params(("arbitrary",)))(dyn, xin, add, gain)


def head_fwd_bwd(x1, gl, pe, tgt, bias, ple_gain):
    s, d = x1.shape
    tm = _tile(s, 128)

    def body(x1_ref, gl_ref, pe_ref, t_ref, b_ref, g_ref, dgl_ref, dpe_ref, dy_ref, acc_ref):
        i = pl.program_id(0)
        gate = _sigmoid(gl_ref[...] + b_ref[...])
        pev = pe_ref[...]
        r = lax.rsqrt(jnp.mean(pev * pev, axis=-1, keepdims=True) + EPS)
        pehat = pev * r
        gain = g_ref[...]
        e = pehat * gain
        diff = (x1_ref[...] + gate * e) - t_ref[...]
        dy = diff * (1.0 / d)
        dy_ref[...] = dy
        d_gl = (dy * e) * (gate * (1.0 - gate))
        dgl_ref[...] = d_gl.astype(BF16)
        d_e = dy * gate
        gd = d_e * gain
        d_pe = r * (gd - pehat * jnp.mean(pehat * gd, axis=-1, keepdims=True))
        dpe_ref[...] = d_pe.astype(BF16)

        @pl.when(i == 0)
        def _():
            acc_ref[...] = jnp.zeros_like(acc_ref)

        acc_ref[0:1, :] += jnp.sum(d_gl, axis=0, keepdims=True)
        acc_ref[1:2, :] += jnp.sum(d_e * pehat, axis=0, keepdims=True)
        acc_ref[2:3, :] += jnp.sum(diff * diff, axis=0, keepdims=True) * (0.5 / d)

    row = pl.BlockSpec((tm, d), lambda i: (i, 0))
    vec = pl.BlockSpec((1, d), lambda i: (0, 0))
    return _call(body, name="head_fwd_bwd", grid=(s // tm,),
                 in_specs=[row, row, row, row, vec, vec],
                 out_specs=[row, row, row, pl.BlockSpec((SUBLANES, d), lambda i: (0, 0))],
                 out_shape=[jax.ShapeDtypeStruct((s, d), BF16), jax.ShapeDtypeStruct((s, d), BF16),
                            jax.ShapeDtypeStruct((s, d), F32), jax.ShapeDtypeStruct((SUBLANES, d), F32)],
                 compiler_params=_params(("arbitrary",)))(x1, gl, pe, tgt, bias, ple_gain)


def adamw(g, w, m, v, name):
    r, c = g.shape
    tr = _tile(r, 128)

    def body(g_ref, w_ref, m_ref, v_ref, go_ref, d_ref, mo_ref, vo_ref):
        gv = g_ref[...]
        mn = ADAM_B1 * m_ref[...] + (1.0 - ADAM_B1) * gv
        vn = ADAM_B2 * v_ref[...] + (1.0 - ADAM_B2) * (gv * gv)
        m_hat = mn / (1.0 - ADAM_B1 ** ADAM_STEP)
        v_hat = vn / (1.0 - ADAM_B2 ** ADAM_STEP)
        go_ref[...] = gv
        d_ref[...] = -ADAM_LR * (m_hat / (jnp.sqrt(v_hat) + ADAM_EPS) + ADAM_WD * w_ref[...])
        mo_ref[...] = mn
        vo_ref[...] = vn

    blk = pl.BlockSpec((tr, c), lambda i: (i, 0))
    shp = jax.ShapeDtypeStruct((r, c), F32)
    return _call(body, name=name, grid=(r // tr,), in_specs=[blk] * 4, out_specs=[blk] * 4,
                 out_shape=[shp] * 4, compiler_params=_params(("parallel",)))(g, w, m, v)


def matmul(a, b, *, grid, a_spec, b_spec, o_spec, out_shape, dims, name, res=None, res_spec=None):
    nk = grid[2]
    acc_shape = tuple(d for d in o_spec.block_shape if d is not None)

    def body(*refs):
        if res is None:
            a_ref, b_ref, o_ref, acc_ref = refs
        else:
            a_ref, b_ref, r_ref, o_ref, acc_ref = refs
        k = pl.program_id(2)

        @pl.when(k == 0)
        def _():
            acc_ref[...] = jnp.zeros_like(acc_ref)

        acc_ref[...] += lax.dot_general(a_ref[...].astype(BF16), b_ref[...].astype(BF16), dims,
                                        preferred_element_type=F32)

        @pl.when(k == nk - 1)
        def _():
            out = acc_ref[...]
            if res is not None:
                out = r_ref[...] + out
            o_ref[...] = out.astype(o_ref.dtype)

    in_specs = [a_spec, b_spec] + ([res_spec] if res is not None else [])
    args = (a, b) + ((res,) if res is not None else ())
    return _call(body, name=name, grid=grid, in_specs=in_specs, out_specs=o_spec, out_shape=out_shape,
                 scratch_shapes=[pltpu.VMEM(acc_shape, F32)],
                 compiler_params=_params(("parallel", "parallel", "arbitrary")))(*args)


def _seg_mean(sq, segb):
    parts = []
    for cgrp in range(sq.shape[1] // SEG):
        blk = sq[:, cgrp * SEG:(cgrp + 1) * SEG]
        hi = blk.astype(BF16)
        r1 = blk - hi.astype(F32)
        mid = r1.astype(BF16)
        lo = (r1 - mid.astype(F32)).astype(BF16)
        acc = jnp.dot(hi, segb, preferred_element_type=F32)
        acc += jnp.dot(mid, segb, preferred_element_type=F32)
        acc += jnp.dot(lo, segb, preferred_element_type=F32)
        parts.append(acc)
    out = parts[0] if len(parts) == 1 else jnp.concatenate(parts, axis=1)
    return out * (1.0 / HEAD_DIM)


def _rope(v, cos, sa, sb):
    parts = []
    for cgrp in range(v.shape[1] // LANES):
        blk = v[:, cgrp * LANES:(cgrp + 1) * LANES]
        parts.append(blk * cos + pltpu.roll(blk, LANES - 8, 1) * sa + pltpu.roll(blk, 8, 1) * sb)
    return parts[0] if len(parts) == 1 else jnp.concatenate(parts, axis=1)


def _rope_t(dv, cos, sa, sb):
    parts = []
    for cgrp in range(dv.shape[1] // LANES):
        blk = dv[:, cgrp * LANES:(cgrp + 1) * LANES]
        parts.append(blk * cos + pltpu.roll(blk * sa, 8, 1) + pltpu.roll(blk * sb, LANES - 8, 1))
    return parts[0] if len(parts) == 1 else jnp.concatenate(parts, axis=1)


def _tile_lanes(vec, width):
    reps = width // LANES
    return vec if reps == 1 else jnp.tile(vec, (1, reps))


def qk_prep_fwd(z, tabs, qg, kg, segb, aw):
    s = z.shape[0]
    tm = _tile(s, 256)
    wq = aw + 2 * KV_WIDTH

    def body(z_ref, cos_ref, sa_ref, sb_ref, qg_ref, kg_ref, seg_ref, qs_ref, ks_ref, vb_ref):
        zz = z_ref[...]
        q, k, v = zz[:, :aw], zz[:, aw:aw + KV_WIDTH], zz[:, aw + KV_WIDTH:]
        cos, sa, sb, segm = cos_ref[...], sa_ref[...], sb_ref[...], seg_ref[...]
        rq = lax.rsqrt(_seg_mean(q * q, segm) + EPS)
        qn = (q * rq) * _tile_lanes(qg_ref[...], aw)
        qs_ref[...] = (_rope(qn, cos, sa, sb) * (HEAD_DIM ** -0.5)).astype(BF16)
        rk = lax.rsqrt(_seg_mean(k * k, segm) + EPS)
        kn = (k * rk) * _tile_lanes(kg_ref[...], KV_WIDTH)
        ks_ref[...] = _rope(kn, cos, sa, sb).astype(BF16)
        vb_ref[...] = v.astype(BF16)

    tab = pl.BlockSpec((tm, LANES), lambda i: (i, 0))
    vec = pl.BlockSpec((1, LANES), lambda i: (0, 0))
    return _call(body, name="qk_prep_fwd", grid=(s // tm,),
                 in_specs=[pl.BlockSpec((tm, wq), lambda i: (i, 0)), tab, tab, tab, vec, vec,
                           pl.BlockSpec((SEG, SEG), lambda i: (0, 0))],
                 out_specs=[pl.BlockSpec((tm, aw), lambda i: (i, 0)),
                            pl.BlockSpec((tm, KV_WIDTH), lambda i: (i, 0)),
                            pl.BlockSpec((tm, KV_WIDTH), lambda i: (i, 0))],
                 out_shape=[jax.ShapeDtypeStruct((s, aw), BF16), jax.ShapeDtypeStruct((s, KV_WIDTH), BF16),
                            jax.ShapeDtypeStruct((s, KV_WIDTH), BF16)],
                 compiler_params=_params(("parallel",)))(z, *tabs, qg, kg, segb)


def qk_prep_bwd(z, dqs, dks, dvs, tabs, qg, kg, segb, dz, aw):
    s = z.shape[0]
    tm = _tile(s, 256)
    wq = aw + 2 * KV_WIDTH

    def body(z_ref, dq_ref, dk_ref, dv_ref, cos_ref, sa_ref, sb_ref, qg_ref, kg_ref, seg_ref, dz_in,
             dz_ref, acc_ref):
        i = pl.program_id(0)
        zz = z_ref[...]
        q, k = zz[:, :aw], zz[:, aw:aw + KV_WIDTH]
        cos, sa, sb, segm = cos_ref[...], sa_ref[...], sb_ref[...], seg_ref[...]

        def one(xv, dout, gvec, width):
            g = _tile_lanes(gvec, width)
            r = lax.rsqrt(_seg_mean(xv * xv, segm) + EPS)
            xhat = xv * r
            dn = _rope_t(dout, cos, sa, sb)
            gd = dn * g
            dx = r * (gd - xhat * _seg_mean(xhat * gd, segm))
            contrib = jnp.sum(dn * xhat, axis=0, keepdims=True)
            folded = contrib[:, :LANES]
            for cgrp in range(1, width // LANES):
                folded = folded + contrib[:, cgrp * LANES:(cgrp + 1) * LANES]
            return dx, folded + pltpu.roll(folded, HEAD_DIM, 1)

        dq, gq = one(q, dq_ref[...] * (HEAD_DIM ** -0.5), qg_ref[...], aw)
        dk, gk = one(k, dk_ref[...], kg_ref[...], KV_WIDTH)
        dz_ref[:, :aw] = dq.astype(BF16)
        dz_ref[:, aw:aw + KV_WIDTH] = dk.astype(BF16)
        dz_ref[:, aw + KV_WIDTH:] = dv_ref[...].astype(BF16)

        @pl.when(i == 0)
        def _():
            acc_ref[...] = jnp.zeros_like(acc_ref)

        acc_ref[0:1, :] += gq
        acc_ref[1:2, :] += gk

    tab = pl.BlockSpec((tm, LANES), lambda i: (i, 0))
    vec = pl.BlockSpec((1, LANES), lambda i: (0, 0))
    kvb = pl.BlockSpec((tm, KV_WIDTH), lambda i: (i, 0))
    return _call(body, name="qk_prep_bwd", grid=(s // tm,),
                 in_specs=[pl.BlockSpec((tm, wq), lambda i: (i, 0)),
                           pl.BlockSpec((tm, aw), lambda i: (i, 0)), kvb, kvb, tab, tab, tab, vec, vec,
                           pl.BlockSpec((SEG, SEG), lambda i: (0, 0)), _hbm()],
                 out_specs=[pl.BlockSpec((tm, wq), lambda i: (i, 0)),
                            pl.BlockSpec((SUBLANES, LANES), lambda i: (0, 0))],
                 out_shape=[jax.ShapeDtypeStruct(dz.shape, BF16), jax.ShapeDtypeStruct((SUBLANES, LANES), F32)],
                 input_output_aliases={10: 0},
                 compiler_params=_params(("arbitrary",)))(z, dqs, dks, dvs, *tabs, qg, kg, segb, dz)


def _placed(band, lane_idx):
    out = {}
    for kh in range(N_KV_HEADS):
        grp = band[:, (kh // 2) * LANES:(kh // 2 + 1) * LANES]
        for half in (0, 1):
            t = grp if half == kh % 2 else pltpu.roll(grp, HEAD_DIM, 1)
            keep = (lane_idx >= half * HEAD_DIM) & (lane_idx < (half + 1) * HEAD_DIM)
            out[kh, half] = jnp.where(keep, t, jnp.zeros_like(t))
    return out


def _scores(qgrp, kpl, valid, sink):
    sc = lax.dot_general(qgrp, kpl, NT, preferred_element_type=F32)
    sc = jnp.where(valid, sc, NEG_INF)
    m = jnp.maximum(jnp.max(sc, axis=-1, keepdims=True), sink)
    e = jnp.exp(sc - m)
    es = jnp.exp(sink - m)
    den = jnp.sum(e, axis=-1, keepdims=True) + es
    return e / den, es / den


def _valid_mask(n):
    r_i = lax.broadcasted_iota(jnp.int32, (BLOCK, 2 * BLOCK), 0)
    j_i = lax.broadcasted_iota(jnp.int32, (BLOCK, 2 * BLOCK), 1)
    return (j_i > r_i) & (j_i <= r_i + BLOCK) & ((n > 0) | (j_i >= BLOCK))


def attn_fwd(qs, ks, vb, z, sinks, aw, d, ga_off):
    s = qs.shape[0]
    nb = s // BLOCK
    nq = aw // HEAD_DIM
    grp_sz = nq // N_KV_HEADS
    n_ga = aw // COLT

    def body(q_ref, ko_ref, kp_ref, vo_ref, vp_ref, *rest):
        ga_refs = rest[:n_ga]
        sink_ref, attn_ref, mix_ref = rest[n_ga:]
        n = pl.program_id(0)
        lane_idx = lax.broadcasted_iota(jnp.int32, (2 * BLOCK, LANES), 1)
        kpl = _placed(jnp.concatenate([kp_ref[...], ko_ref[...]], axis=0), lane_idx)
        vpl = _placed(jnp.concatenate([vp_ref[...], vo_ref[...]], axis=0), lane_idx)
        valid = _valid_mask(n)
        for cgrp in range(nq // 2):
            qgrp = q_ref[:, cgrp * LANES:(cgrp + 1) * LANES]
            acc = jnp.zeros((BLOCK, LANES), F32)
            for half in (0, 1):
                h = 2 * cgrp + half
                kh = h // grp_sz
                p, _ = _scores(qgrp, kpl[kh, half], valid, sink_ref[0, h])
                acc += jnp.dot(p.astype(BF16), vpl[kh, half], preferred_element_type=F32)
            attn_ref[:, cgrp * LANES:(cgrp + 1) * LANES] = acc
            col = cgrp * LANES
            ga = ga_refs[col // COLT][:, col % COLT:col % COLT + LANES]
            mix_ref[:, cgrp * LANES:(cgrp + 1) * LANES] = (acc * (ga * _sigmoid(ga))).astype(BF16)

    own = lambda n: (n, 0)
    prev = lambda n: (jnp.maximum(n - 1, 0), 0)
    kvs = lambda imap: pl.BlockSpec((BLOCK, KV_WIDTH), imap)
    ga_specs = [pl.BlockSpec((BLOCK, COLT), functools.partial(lambda n, j: (n, ga_off + j), j=j))
                for j in range(n_ga)]
    return _call(body, name="attn_fwd", grid=(nb,),
                 in_specs=[pl.BlockSpec((BLOCK, aw), own), kvs(own), kvs(prev), kvs(own), kvs(prev)]
                 + ga_specs + [pl.BlockSpec(memory_space=pltpu.SMEM)],
                 out_specs=[pl.BlockSpec((BLOCK, aw), own), pl.BlockSpec((BLOCK, aw), own)],
                 out_shape=[jax.ShapeDtypeStruct((s, aw), F32), jax.ShapeDtypeStruct((s, d), BF16)],
                 compiler_params=_params(("parallel",)))(qs, ks, ks, vb, vb, *([z] * n_ga), sinks)


def gate_bwd(d_mix, attn, z, aw, ga_off):
    s, in_w = z.shape
    tm = _tile(s, 256)

    def body(dm_ref, at_ref, ga_ref, do_ref, dz_ref):
        ga = ga_ref[...]
        sig = _sigmoid(ga)
        dm = dm_ref[...]
        do_ref[...] = (dm * (ga * sig)).astype(BF16)
        dz_ref[...] = ((dm * at_ref[...]) * (sig * (1.0 + ga * (1.0 - sig)))).astype(BF16)

    blk = pl.BlockSpec((tm, COLT), lambda i, j: (i, j))
    gab = pl.BlockSpec((tm, COLT), lambda i, j: (i, ga_off + j))
    return _call(body, name="gate_bwd", grid=(s // tm, aw // COLT),
                 in_specs=[blk, blk, gab], out_specs=[blk, gab],
                 out_shape=[jax.ShapeDtypeStruct((s, aw), BF16), jax.ShapeDtypeStruct((s, in_w), BF16)],
                 compiler_params=_params(("parallel", "parallel")))(d_mix, attn, z)


def attn_bwd(qs, ks, vb, d_o, sinks, aw):
    s = qs.shape[0]
    nb = s // BLOCK
    nq = aw // HEAD_DIM
    grp_sz = nq // N_KV_HEADS

    def body(q_ref, ko_ref, kp_ref, vo_ref, vp_ref, do_ref, sink_ref, dq_ref, dk_ref, dv_ref, ds_ref,
             ck_ref, cv_ref):
        n = pl.program_id(0)

        @pl.when(n == 0)
        def _():
            ds_ref[...] = jnp.zeros_like(ds_ref)
            dk_ref[...] = jnp.zeros_like(dk_ref)
            dv_ref[...] = jnp.zeros_like(dv_ref)

        @pl.when(n < nb)
        def _():
            lane_idx = lax.broadcasted_iota(jnp.int32, (2 * BLOCK, LANES), 1)
            lane_row = lax.broadcasted_iota(jnp.int32, (1, LANES), 1)
            kpl = _placed(jnp.concatenate([kp_ref[...], ko_ref[...]], axis=0), lane_idx)
            vpl = _placed(jnp.concatenate([vp_ref[...], vo_ref[...]], axis=0), lane_idx)
            valid = _valid_mask(n)
            dk_acc = [jnp.zeros((2 * BLOCK, LANES), F32) for _ in range(N_KV_HEADS)]
            dv_acc = [jnp.zeros((2 * BLOCK, LANES), F32) for _ in range(N_KV_HEADS)]
            ds_row = jnp.zeros((1, LANES), F32)
            for cgrp in range(nq // 2):
                qgrp = q_ref[:, cgrp * LANES:(cgrp + 1) * LANES]
                dog = do_ref[:, cgrp * LANES:(cgrp + 1) * LANES]
                dq_acc = jnp.zeros((BLOCK, LANES), F32)
                for half in (0, 1):
                    h = 2 * cgrp + half
                    kh = h // grp_sz
                    p, p_sink = _scores(qgrp, kpl[kh, half], valid, sink_ref[0, h])
                    dp = lax.dot_general(dog, vpl[kh, half], NT, preferred_element_type=F32)
                    delta = jnp.sum(p * dp, axis=-1, keepdims=True)
                    dsb = (p * (dp - delta)).astype(BF16)
                    dq_acc += jnp.dot(dsb, kpl[kh, half], preferred_element_type=F32)
                    keep = (lane_idx >= half * HEAD_DIM) & (lane_idx < (half + 1) * HEAD_DIM)
                    dkp = jnp.where(keep, lax.dot_general(dsb, qgrp, TN, preferred_element_type=F32), 0.0)
                    dvp = jnp.where(keep, lax.dot_general(p.astype(BF16), dog, TN, preferred_element_type=F32), 0.0)
                    if half != kh % 2:
                        dkp = pltpu.roll(dkp, HEAD_DIM, 1)
                        dvp = pltpu.roll(dvp, HEAD_DIM, 1)
                    dk_acc[kh] += dkp
                    dv_acc[kh] += dvp
                    dsink = -jnp.sum(p_sink * delta, axis=0, keepdims=True)
                    ds_row += jnp.where(lane_row == h, dsink, 0.0)
                dq_ref[:, cgrp * LANES:(cgrp + 1) * LANES] = dq_acc
            ds_ref[0:1, :] += ds_row
            dk_band = jnp.concatenate([dk_acc[0] + dk_acc[1], dk_acc[2] + dk_acc[3]], axis=1)
            dv_band = jnp.concatenate([dv_acc[0] + dv_acc[1], dv_acc[2] + dv_acc[3]], axis=1)

            @pl.when(n > 0)
            def _():
                dk_ref[...] = ck_ref[...] + dk_band[:BLOCK]
                dv_ref[...] = cv_ref[...] + dv_band[:BLOCK]

            ck_ref[...] = dk_band[BLOCK:]
            cv_ref[...] = dv_band[BLOCK:]

        @pl.when(n == nb)
        def _():
            dk_ref[...] = ck_ref[...]
            dv_ref[...] = cv_ref[...]

    own = lambda n: (jnp.minimum(n, nb - 1), 0)
    prev = lambda n: (jnp.clip(n - 1, 0, nb - 1), 0)
    done = lambda n: (jnp.maximum(n - 1, 0), 0)
    kvs = lambda imap: pl.BlockSpec((BLOCK, KV_WIDTH), imap)
    return _call(body, name="attn_bwd", grid=(nb + 1,),
                 in_specs=[pl.BlockSpec((BLOCK, aw), own), kvs(own), kvs(prev), kvs(own), kvs(prev),
                           pl.BlockSpec((BLOCK, aw), own), pl.BlockSpec(memory_space=pltpu.SMEM)],
                 out_specs=[pl.BlockSpec((BLOCK, aw), own), kvs(done), kvs(done),
                            pl.BlockSpec((SUBLANES, LANES), lambda n: (0, 0))],
                 out_shape=[jax.ShapeDtypeStruct((s, aw), F32), jax.ShapeDtypeStruct((s, KV_WIDTH), F32),
                            jax.ShapeDtypeStruct((s, KV_WIDTH), F32), jax.ShapeDtypeStruct((SUBLANES, LANES), F32)],
                 scratch_shapes=[pltpu.VMEM((BLOCK, KV_WIDTH), F32), pltpu.VMEM((BLOCK, KV_WIDTH), F32)],
                 compiler_params=_params(("arbitrary",)))(qs, ks, ks, vb, vb, d_o, sinks)


def conv_fwd(z, conv_w, mix, aw, cw, b_off):
    s = z.shape[0]
    tm = _tile(s, 256)
    nseg = cw // COLT
    hb = tm // SUBLANES

    def body(b_ref, c_ref, h_ref, g_ref, cp_ref, hp_ref, w_ref, mix_in, mix_ref):
        i = pl.program_id(0)
        u = c_ref[...] * h_ref[...]
        up = jnp.where(i > 0, cp_ref[...] * hp_ref[...], 0.0)
        ext = jnp.concatenate([up, u], axis=0)
        um1 = pltpu.roll(ext, 1, 0)[SUBLANES:]
        um2 = pltpu.roll(ext, 2, 0)[SUBLANES:]
        w = w_ref[...]
        cv = w[0:1] * um2 + w[1:2] * um1 + w[2:3] * u
        g = g_ref[...]
        mix_ref[...] = ((b_ref[...] * cv) * (g * _sigmoid(g))).astype(BF16)

    seg = lambda k: pl.BlockSpec((tm, COLT), functools.partial(lambda i, j, k: (i, b_off + k * nseg + j), k=k))
    halo = lambda k: pl.BlockSpec(
        (SUBLANES, COLT), functools.partial(lambda i, j, k: (jnp.maximum(i * hb - 1, 0), b_off + k * nseg + j), k=k))
    return _call(body, name="conv_fwd", grid=(s // tm, nseg),
                 in_specs=[seg(0), seg(1), seg(2), seg(3), halo(1), halo(2),
                           pl.BlockSpec((SUBLANES, COLT), lambda i, j: (0, j)), _hbm()],
                 out_specs=pl.BlockSpec((tm, COLT), lambda i, j: (i, aw // COLT + j)),
                 out_shape=jax.ShapeDtypeStruct(mix.shape, BF16),
                 input_output_aliases={7: 0},
                 compiler_params=_params(("parallel", "parallel")))(z, z, z, z, z, z, conv_w, mix)


def conv_bwd(z, d_mix, conv_w, dz, aw, cw, b_off):
    s = z.shape[0]
    tm = _tile(s, 256)
    nseg = cw // COLT
    hb = tm // SUBLANES
    n_row = s // tm
    last_h = s // SUBLANES - 1

    def body(b_ref, c_ref, h_ref, g_ref, cp_ref, hp_ref, bn_ref, gn_ref, dm_ref, dmn_ref, w_ref, dz_in,
             dz_ref, acc_ref, stash_ref):
        i = pl.program_id(1)
        k = pl.program_id(2)

        @pl.when(k == 0)
        def _():
            cc, hh, bb, g = c_ref[...], h_ref[...], b_ref[...], g_ref[...]
            w = w_ref[...]
            u = cc * hh
            up = jnp.where(i > 0, cp_ref[...] * hp_ref[...], 0.0)
            ext = jnp.concatenate([up, u], axis=0)
            um1 = pltpu.roll(ext, 1, 0)[SUBLANES:]
            um2 = pltpu.roll(ext, 2, 0)[SUBLANES:]
            cv = w[0:1] * um2 + w[1:2] * um1 + w[2:3] * u
            sig = _sigmoid(g)
            sg = g * sig
            dm = dm_ref[...]
            d_cv = (dm * bb) * sg
            gn = gn_ref[...]
            d_cv_next = jnp.where(i < n_row - 1, (dmn_ref[...] * bn_ref[...]) * (gn * _sigmoid(gn)), 0.0)
            ext2 = jnp.concatenate([d_cv, d_cv_next], axis=0)
            dp1 = pltpu.roll(ext2, tm + SUBLANES - 1, 0)[:tm]
            dp2 = pltpu.roll(ext2, tm + SUBLANES - 2, 0)[:tm]
            d_u = w[2:3] * d_cv + w[1:2] * dp1 + w[0:1] * dp2
            stash_ref[0] = ((dm * cv) * sg).astype(BF16)
            stash_ref[1] = (d_u * hh).astype(BF16)
            stash_ref[2] = (d_u * cc).astype(BF16)
            stash_ref[3] = (((dm * bb) * cv) * (sig * (1.0 + g * (1.0 - sig)))).astype(BF16)

            @pl.when(i == 0)
            def _():
                acc_ref[...] = jnp.zeros_like(acc_ref)

            acc_ref[0:1, :] += jnp.sum(d_cv * um2, axis=0, keepdims=True)
            acc_ref[1:2, :] += jnp.sum(d_cv * um1, axis=0, keepdims=True)
            acc_ref[2:3, :] += jnp.sum(d_cv * u, axis=0, keepdims=True)

        dz_ref[...] = stash_ref[k]

    seg = lambda q: pl.BlockSpec((tm, COLT), functools.partial(lambda j, i, k, q: (i, b_off + q * nseg + j), q=q))
    halo_p = lambda q: pl.BlockSpec(
        (SUBLANES, COLT),
        functools.partial(lambda j, i, k, q: (jnp.maximum(i * hb - 1, 0), b_off + q * nseg + j), q=q))
    halo_n = lambda q: pl.BlockSpec(
        (SUBLANES, COLT),
        functools.partial(lambda j, i, k, q: (jnp.minimum((i + 1) * hb, last_h), b_off + q * nseg + j), q=q))
    return _call(body, name="conv_bwd", grid=(nseg, n_row, 4),
                 in_specs=[seg(0), seg(1), seg(2), seg(3), halo_p(1), halo_p(2), halo_n(0), halo_n(3),
                           pl.BlockSpec((tm, COLT), lambda j, i, k: (i, aw // COLT + j)),
                           pl.BlockSpec((SUBLANES, COLT),
                                        lambda j, i, k: (jnp.minimum((i + 1) * hb, last_h), aw // COLT + j)),
                           pl.BlockSpec((SUBLANES, COLT), lambda j, i, k: (0, j)), _hbm()],
                 out_specs=[pl.BlockSpec((tm, COLT), lambda j, i, k: (i, b_off + k * nseg + j)),
                            pl.BlockSpec((SUBLANES, COLT), lambda j, i, k: (0, j))],
                 out_shape=[jax.ShapeDtypeStruct(dz.shape, BF16), jax.ShapeDtypeStruct((SUBLANES, cw), F32)],
                 input_output_aliases={11: 0},
                 scratch_shapes=[pltpu.VMEM((4, tm, COLT), BF16)],
                 compiler_params=_params(("arbitrary", "arbitrary", "arbitrary")))(
                     z, z, z, z, z, z, z, z, d_mix, d_mix, conv_w, dz)


def _place():
    x, y, c = lax.axis_index("x"), lax.axis_index("y"), lax.axis_index("c")
    chips = [(1 - x, y), (x, 1 - y), (1 - x, 1 - y)]
    return x, y, c, chips


def allgather_weights(shards, conv_pad):
    nw = len(shards)

    def body(*refs):
        ins, conv_in = refs[:nw], refs[nw]
        outs, conv_out = refs[nw + 1:2 * nw + 1], refs[2 * nw + 1]
        send, recv, fsend, frecv, csend, crecv, loc = refs[2 * nw + 2:]
        x, y, c, chips = _place()
        me = 2 * x + y
        local = [pltpu.make_async_copy(ins[w], outs[w].at[me], loc.at[w]) for w in range(nw)]
        local.append(pltpu.make_async_copy(conv_in, conv_out.at[me], loc.at[nw]))
        for cp in local:
            cp.start()

        def half(w):
            hr = ins[w].shape[0] // 2
            return pl.ds(c * hr, hr)

        def ici(w, j):
            cx, cy = chips[j]
            return pltpu.make_async_remote_copy(
                src_ref=ins[w].at[half(w)], dst_ref=outs[w].at[me, half(w)],
                send_sem=send.at[w, j], recv_sem=recv.at[w, j], device_id=(cx, cy, c), device_id_type=MESH)

        def arrived(w, j):
            cx, cy = chips[j]
            return pltpu.make_async_remote_copy(
                src_ref=ins[w].at[half(w)], dst_ref=outs[w].at[2 * cx + cy, half(w)],
                send_sem=send.at[w, j], recv_sem=recv.at[w, j], device_id=(cx, cy, c), device_id_type=MESH)

        def passed(w, j, from_core):
            cx, cy = chips[j]
            hr = ins[w].shape[0] // 2
            rows = outs[w].at[2 * cx + cy, pl.ds(from_core * hr, hr)]
            return pltpu.make_async_remote_copy(
                src_ref=rows, dst_ref=rows, send_sem=fsend.at[w, j], recv_sem=frecv.at[w, j],
                device_id=(x, y, 1 - c), device_id_type=MESH)

        def conv(j, slot):
            cx, cy = chips[j]
            return pltpu.make_async_remote_copy(
                src_ref=conv_in, dst_ref=conv_out.at[slot], send_sem=csend.at[j], recv_sem=crecv.at[j],
                device_id=(cx, cy, c), device_id_type=MESH)

        for w in range(nw):
            for j in range(3):
                ici(w, j).start()
        for j in range(3):
            conv(j, me).start()
        for w in range(nw):
            for j in range(3):
                arrived(w, j).wait_recv()
                passed(w, j, c).start()
        for w in range(nw):
            for j in range(3):
                passed(w, j, 1 - c).wait_recv()
        for j in range(3):
            cx, cy = chips[j]
            conv(j, 2 * cx + cy).wait_recv()
        for w in range(nw):
            for j in range(3):
                ici(w, j).wait_send()
                passed(w, j, c).wait_send()
        for j in range(3):
            conv(j, me).wait_send()
        for cp in local:
            cp.wait()

    out_shape = [jax.ShapeDtypeStruct((N_CHIPS,) + a.shape, a.dtype) for a in shards]
    out_shape.append(jax.ShapeDtypeStruct((N_CHIPS,) + conv_pad.shape, conv_pad.dtype))
    return _call(body, name="allgather_weights", in_specs=[_hbm()] * (nw + 1), out_specs=[_hbm()] * (nw + 1),
                 out_shape=out_shape,
                 scratch_shapes=[pltpu.SemaphoreType.DMA((nw, 3)), pltpu.SemaphoreType.DMA((nw, 3)),
                                 pltpu.SemaphoreType.DMA((nw, 3)), pltpu.SemaphoreType.DMA((nw, 3)),
                                 pltpu.SemaphoreType.DMA((3,)), pltpu.SemaphoreType.DMA((3,)),
                                 pltpu.SemaphoreType.DMA((nw + 1,))])(*shards, conv_pad)


def sibling_swap_halves(grads):
    nw = len(grads)

    def body(*refs):
        ins, outs = refs[:nw], refs[nw:2 * nw]
        send, recv = refs[2 * nw:]
        x, y, c, _ = _place()
        cps = []
        for w in range(nw):
            hr = ins[w].shape[1] // 2
            cps.append(pltpu.make_async_remote_copy(
                src_ref=ins[w].at[:, pl.ds((1 - c) * hr, hr), :], dst_ref=outs[w],
                send_sem=send.at[w], recv_sem=recv.at[w], device_id=(x, y, 1 - c), device_id_type=MESH))
        for cp in cps:
            cp.start()
        for cp in cps:
            cp.wait()

    out_shape = [jax.ShapeDtypeStruct((N_CHIPS, a.shape[1] // 2, a.shape[2]), a.dtype) for a in grads]
    return _call(body, name="sibling_swap_halves", in_specs=[_hbm()] * nw, out_specs=[_hbm()] * nw,
                 out_shape=out_shape,
                 scratch_shapes=[pltpu.SemaphoreType.DMA((nw,)), pltpu.SemaphoreType.DMA((nw,))])(*grads)


def scatter_to_owner(parts):
    nw = len(parts)

    def body(*refs):
        ins, outs = refs[:nw], refs[nw:2 * nw]
        send, recv, loc = refs[2 * nw:]
        x, y, c, chips = _place()
        me = 2 * x + y
        local = [pltpu.make_async_copy(ins[w].at[me], outs[w].at[me], loc.at[w]) for w in range(nw)]
        for cp in local:
            cp.start()

        def ici(w, j, slot_src, slot_dst):
            cx, cy = chips[j]
            return pltpu.make_async_remote_copy(
                src_ref=ins[w].at[slot_src], dst_ref=outs[w].at[slot_dst],
                send_sem=send.at[w, j], recv_sem=recv.at[w, j], device_id=(cx, cy, c), device_id_type=MESH)

        for w in range(nw):
            for j in range(3):
                cx, cy = chips[j]
                ici(w, j, 2 * cx + cy, me).start()
        for w in range(nw):
            for j in range(3):
                cx, cy = chips[j]
                ici(w, j, me, 2 * cx + cy).wait_recv()
        for w in range(nw):
            for j in range(3):
                cx, cy = chips[j]
                ici(w, j, 2 * cx + cy, me).wait_send()
        for cp in local:
            cp.wait()

    out_shape = [jax.ShapeDtypeStruct(a.shape, a.dtype) for a in parts]
    return _call(body, name="scatter_to_owner", in_specs=[_hbm()] * nw, out_specs=[_hbm()] * nw,
                 out_shape=out_shape,
                 scratch_shapes=[pltpu.SemaphoreType.DMA((nw, 3)), pltpu.SemaphoreType.DMA((nw, 3)),
                                 pltpu.SemaphoreType.DMA((nw,))])(*parts)


def sibling_join_halves(halves):
    nw = len(halves)

    def body(*refs):
        ins, outs = refs[:nw], refs[nw:2 * nw]
        send, recv, loc = refs[2 * nw:]
        x, y, c, _ = _place()
        local, cps = [], []
        for w in range(nw):
            hr = ins[w].shape[0]
            rows = outs[w].at[pl.ds(c * hr, hr)]
            local.append(pltpu.make_async_copy(ins[w], rows, loc.at[w]))
            cps.append(pltpu.make_async_remote_copy(
                src_ref=ins[w], dst_ref=rows, send_sem=send.at[w], recv_sem=recv.at[w],
                device_id=(x, y, 1 - c), device_id_type=MESH))
        for cp in local + cps:
            cp.start()
        for w in range(nw):
            hr = ins[w].shape[0]
            other = outs[w].at[pl.ds((1 - c) * hr, hr)]
            pltpu.make_async_remote_copy(
                src_ref=ins[w], dst_ref=other, send_sem=send.at[w], recv_sem=recv.at[w],
                device_id=(x, y, 1 - c), device_id_type=MESH).wait_recv()
        for cp in cps:
            cp.wait_send()
        for cp in local:
            cp.wait()

    out_shape = [jax.ShapeDtypeStruct((2 * a.shape[0], a.shape[1]), a.dtype) for a in halves]
    return _call(body, name="sibling_join_halves", in_specs=[_hbm()] * nw, out_specs=[_hbm()] * nw,
                 out_shape=out_shape,
                 scratch_shapes=[pltpu.SemaphoreType.DMA((nw,)), pltpu.SemaphoreType.DMA((nw,)),
                                 pltpu.SemaphoreType.DMA((nw,))])(*halves)


def allgather_small(small):
    rows, width = small.shape

    def body(in_ref, out_ref, send, recv, loc):
        x, y, c, _ = _place()
        me = 4 * x + 2 * y + c
        local = pltpu.make_async_copy(in_ref, out_ref.at[me], loc)
        local.start()
        flips = [(fx, fy, fc) for fx in (0, 1) for fy in (0, 1) for fc in (0, 1)][1:]

        def cp(k, slot):
            fx, fy, fc = flips[k]
            return pltpu.make_async_remote_copy(
                src_ref=in_ref, dst_ref=out_ref.at[slot], send_sem=send.at[k], recv_sem=recv.at[k],
                device_id=(x ^ fx, y ^ fy, c ^ fc), device_id_type=MESH)

        for k in range(7):
            cp(k, me).start()
        for k in range(7):
            fx, fy, fc = flips[k]
            cp(k, 4 * (x ^ fx) + 2 * (y ^ fy) + (c ^ fc)).wait_recv()
        for k in range(7):
            cp(k, me).wait_send()
        local.wait()

    return _call(body, name="allgather_small",
                 in_specs=[pl.BlockSpec(memory_space=pltpu.VMEM)],
                 out_specs=pl.BlockSpec(memory_space=pltpu.VMEM),
                 out_shape=jax.ShapeDtypeStruct((8, rows, width), F32),
                 scratch_shapes=[pltpu.SemaphoreType.DMA((7,)), pltpu.SemaphoreType.DMA((7,)),
                                 pltpu.SemaphoreType.DMA])(small)


def add_sibling(grad, got, core, name):
    _, r, c = grad.shape
    hr = r // 2
    tr = _tile(hr, 128)
    nblk = hr // tr

    def body(core_ref, g_ref, o_ref, out_ref):
        out_ref[...] = (g_ref[...].astype(F32) + o_ref[...].astype(F32)).astype(BF16)

    grid_spec = pltpu.PrefetchScalarGridSpec(
        num_scalar_prefetch=1, grid=(N_CHIPS, nblk),
        in_specs=[pl.BlockSpec((None, tr, c), lambda t, i, core_ref: (t, core_ref[0] * nblk + i, 0)),
                  pl.BlockSpec((None, tr, c), lambda t, i, core_ref: (t, i, 0))],
        out_specs=pl.BlockSpec((None, tr, c), lambda t, i, core_ref: (t, i, 0)))
    return _call(body, name=name, grid_spec=grid_spec,
                 out_shape=jax.ShapeDtypeStruct((N_CHIPS, hr, c), BF16),
                 compiler_params=_params(("parallel", "parallel")))(core, grad, got)


def sum_chips(parts, name):
    _, hr, c = parts.shape
    tr = _tile(hr, 128)

    def body(p_ref, out_ref):
        acc = p_ref[0].astype(F32)
        for t in range(1, N_CHIPS):
            acc = acc + p_ref[t].astype(F32)
        out_ref[...] = acc

    return _call(body, name=name, grid=(hr // tr,),
                 in_specs=[pl.BlockSpec((N_CHIPS, tr, c), lambda i: (0, i, 0))],
                 out_specs=pl.BlockSpec((tr, c), lambda i: (i, 0)),
                 out_shape=jax.ShapeDtypeStruct((hr, c), F32),
                 compiler_params=_params(("parallel",)))(parts)


def sum_devices(gathered):
    _, rows, width = gathered.shape

    def body(g_ref, out_ref):
        acc = g_ref[0]
        for dev in range(1, 8):
            acc = acc + g_ref[dev]
        out_ref[...] = acc
        tail = acc[SUBLANES:]
        out_ref[SUBLANES:, :] = jnp.broadcast_to(jnp.sum(tail, axis=1, keepdims=True), tail.shape)

    return _call(body, name="sum_devices",
                 in_specs=[pl.BlockSpec(memory_space=pltpu.VMEM)],
                 out_specs=pl.BlockSpec(memory_space=pltpu.VMEM),
                 out_shape=jax.ShapeDtypeStruct((rows, width), F32))(gathered)


def _rope_tables(s):
    half = ROT_DIM // 2
    inv_freq = jnp.power(jnp.float32(ROPE_THETA), -jnp.arange(half, dtype=F32) * 2.0 / ROT_DIM)
    ang = jnp.arange(s).astype(F32)[:, None] * inv_freq[None, :]
    cos, sin = jnp.cos(ang), jnp.sin(ang)
    zeros = lambda n: jnp.zeros((s, n), F32)
    cos64 = jnp.concatenate([cos, cos, jnp.ones((s, HEAD_DIM - ROT_DIM), F32)], axis=1)
    sa64 = jnp.concatenate([-sin, zeros(HEAD_DIM - half)], axis=1)
    sb64 = jnp.concatenate([zeros(half), sin, zeros(HEAD_DIM - ROT_DIM)], axis=1)
    return tuple(jnp.concatenate([t, t], axis=1) for t in (cos64, sa64, sb64))


def _pad_rows(a, rows):
    return jnp.pad(a, ((0, rows - a.shape[0]), (0, 0)))


def _pad_cols(a, cols):
    return jnp.pad(a, ((0, 0), (0, cols - a.shape[1])))


def kernel(x, p, norm_gain, w_in, q_norm_gain, k_norm_gain, attn_sinks, conv_w, w_out, ple_gate_norm_gain, w_ple_gate, b_ple_gate, w_ple_proj, ple_norm_gain, loss_target, m_norm_gain, m_w_in, m_q_norm_gain, m_k_norm_gain, m_attn_sinks, m_conv_w, m_w_out, m_ple_gate_norm_gain, m_w_ple_gate, m_b_ple_gate, m_w_ple_proj, m_ple_norm_gain, v_norm_gain, v_w_in, v_q_norm_gain, v_k_norm_gain, v_attn_sinks, v_conv_w, v_w_out, v_ple_gate_norm_gain, v_w_ple_gate, v_b_ple_gate, v_w_ple_proj, v_ple_norm_gain):
    x2, p2, tgt = x[0], p[0, 0], loss_target[0]
    s, d = x2.shape
    ple = p2.shape[1]
    aw = d // 2
    cw = d - aw
    nq = aw // HEAD_DIM
    sh = w_in.shape[2]
    in_w = N_CHIPS * sh
    dq = d // N_CHIPS
    cq = cw // N_CHIPS
    ga_off = (aw + 2 * KV_WIDTH) // COLT
    b_off = (2 * aw + 2 * KV_WIDTH) // COLT
    assert in_w == 2 * aw + 2 * KV_WIDTH + 4 * cw and aw % COLT == 0 and cw % COLT == 0
    assert s % BLOCK == 0 and sh % LANES == 0 and nq % (2 * N_KV_HEADS) == 0

    core = lax.axis_index("c").astype(jnp.int32).reshape(1)
    chip = 2 * lax.axis_index("x") + lax.axis_index("y")

    shards = [cast_bf16(w_in[0], "cast_w_in"), cast_bf16(w_out[0], "cast_w_out"),
              cast_bf16(w_ple_gate[0], "cast_w_pg"), cast_bf16(w_ple_proj[0], "cast_w_pp")]
    wi_all, wo_all, wg_all, wp_all, conv_all = allgather_weights(shards, _pad_rows(conv_w[0], SUBLANES))
    wo_full = wo_all.reshape(d, d)
    wg_full = wg_all.reshape(d, d)
    conv_full = conv_all.transpose(1, 0, 2).reshape(SUBLANES, cw)

    tm = _tile(s, 1024)
    tn = _tile(d, 1024)
    tk = _tile(d, 1024)
    ts = _tile(s, 1024)
    h = rms_fwd(x2, norm_gain, "rms_fwd_x")
    z = matmul(h, wi_all, grid=(s // tm, N_CHIPS, d // tk),
               a_spec=pl.BlockSpec((tm, tk), lambda i, j, k: (i, k)),
               b_spec=pl.BlockSpec((None, tk, sh), lambda i, j, k: (j, k, 0)),
               o_spec=pl.BlockSpec((tm, sh), lambda i, j, k: (i, j)),
               out_shape=jax.ShapeDtypeStruct((s, in_w), F32), dims=NN, name="mm_z")
    tabs = _rope_tables(s)
    qg = jnp.tile(q_norm_gain, (1, LANES // HEAD_DIM))
    kg = jnp.tile(k_norm_gain, (1, LANES // HEAD_DIM))
    seg_i = jnp.arange(SEG) // HEAD_DIM
    segb = (seg_i[:, None] == seg_i[None, :]).astype(BF16)
    qs, ks, vb = qk_prep_fwd(z, tabs, qg, kg, segb, aw)
    attn, mix = attn_fwd(qs, ks, vb, z, attn_sinks, aw, d, ga_off)
    mix = conv_fwd(z, conv_full, mix, aw, cw, b_off)
    sq = lambda shape: dict(
        a_spec=pl.BlockSpec((tm, tk), lambda i, j, k: (i, k)),
        o_spec=pl.BlockSpec((tm, tn), lambda i, j, k: (i, j)),
        out_shape=jax.ShapeDtypeStruct(shape, F32))
    x1 = matmul(mix, wo_full, grid=(s // tm, d // tn, d // tk),
                b_spec=pl.BlockSpec((tk, tn), lambda i, j, k: (k, j)), dims=NN, name="mm_x1",
                res=x2, res_spec=pl.BlockSpec((tm, tn), lambda i, j, k: (i, j)), **sq((s, d)))
    hg = rms_fwd(x1, ple_gate_norm_gain, "rms_fwd_x1")
    gl = matmul(hg, wg_full, grid=(s // tm, d // tn, d // tk),
                b_spec=pl.BlockSpec((tk, tn), lambda i, j, k: (k, j)), dims=NN, name="mm_gl", **sq((s, d)))
    pq = d // N_CHIPS
    pe = matmul(p2, wp_all, grid=(s // tm, N_CHIPS, 1),
                a_spec=pl.BlockSpec((tm, ple), lambda i, j, k: (i, 0)),
                b_spec=pl.BlockSpec((None, ple, pq), lambda i, j, k: (j, 0, 0)),
                o_spec=pl.BlockSpec((tm, pq), lambda i, j, k: (i, j)),
                out_shape=jax.ShapeDtypeStruct((s, d), F32), dims=NN, name="mm_pe")

    d_gl, d_pe, dy, acc_head = head_fwd_bwd(x1, gl, pe, tgt, b_ple_gate, ple_norm_gain)
    d_hg = matmul(d_gl, wg_full, grid=(s // tm, d // tn, d // tk),
                  b_spec=pl.BlockSpec((tn, tk), lambda i, j, k: (j, k)), dims=NT, name="mm_d_hg", **sq((s, d)))
    wgrad = lambda a_cols, shape3, o_spec, b_cols, name, a, b, grid: matmul(
        a, b, grid=grid,
        a_spec=pl.BlockSpec((ts, a_cols), lambda i, j, k: (k, i)),
        b_spec=pl.BlockSpec((ts, b_cols), lambda i, j, k: (k, j)),
        o_spec=o_spec, out_shape=jax.ShapeDtypeStruct(shape3, BF16), dims=TN, name=name)
    g_wg = wgrad(tn, (d, d), pl.BlockSpec((tn, tn), lambda i, j, k: (i, j)), tn, "mm_g_wg", hg, d_gl,
                 (d // tn, d // tn, s // ts))
    g_wp = wgrad(ple, (N_CHIPS, ple, pq), pl.BlockSpec((None, ple, pq), lambda i, j, k: (j, 0, 0)), pq,
                 "mm_g_wp", p2, d_pe, (1, N_CHIPS, s // ts))
    d_x1, d_x1b, acc_g = rms_bwd_add(d_hg, x1, dy, ple_gate_norm_gain, "rms_bwd_x1", True)
    d_mix = matmul(d_x1b, wo_full, grid=(s // tm, d // tn, d // tk),
                   b_spec=pl.BlockSpec((tn, tk), lambda i, j, k: (j, k)), dims=NT, name="mm_d_mix", **sq((s, d)))
    g_wo = wgrad(tn, (d, d), pl.BlockSpec((tn, tn), lambda i, j, k: (i, j)), tn, "mm_g_wo", mix, d_x1b,
                 (d // tn, d // tn, s // ts))
    d_o, dz = gate_bwd(d_mix, attn, z, aw, ga_off)
    dqs, dks, dvs, acc_sink = attn_bwd(qs, ks, vb, d_o, attn_sinks, aw)
    dz, acc_qk = qk_prep_bwd(z, dqs, dks, dvs, tabs, qg, kg, segb, dz, aw)
    dz, acc_conv = conv_bwd(z, d_mix, conv_full, dz, aw, cw, b_off)
    d_h = matmul(dz, wi_all, grid=(s // tm, d // tn, N_CHIPS),
                 a_spec=pl.BlockSpec((tm, sh), lambda i, j, k: (i, k)),
                 b_spec=pl.BlockSpec((None, tn, sh), lambda i, j, k: (k, j, 0)),
                 o_spec=pl.BlockSpec((tm, tn), lambda i, j, k: (i, j)),
                 out_shape=jax.ShapeDtypeStruct((s, d), F32), dims=NT, name="mm_d_h")
    g_wi = wgrad(tn, (N_CHIPS, d, sh), pl.BlockSpec((None, tn, sh), lambda i, j, k: (j, i, 0)), sh,
                 "mm_g_wi", h, dz, (d // tn, N_CHIPS, s // ts))
    grad_x, acc_x = rms_bwd_add(d_h, x2, d_x1, norm_gain, "rms_bwd_x", False)

    grads = [g_wi, g_wo.reshape(N_CHIPS, dq, d), g_wg.reshape(N_CHIPS, dq, d), g_wp]
    got = sibling_swap_halves(grads)
    names = ("wi", "wo", "wg", "wp")
    parts = [add_sibling(g, o, core, "add_sibling_" + nm) for g, o, nm in zip(grads, got, names)]
    owned = scatter_to_owner(parts)
    halves = [sum_chips(o, "sum_chips_" + nm) for o, nm in zip(owned, names)]
    full = sibling_join_halves(halves)
    big = {}
    for nm, g, w, m, v in (("w_in", full[0], w_in, m_w_in, v_w_in), ("w_out", full[1], w_out, m_w_out, v_w_out),
                           ("w_ple_gate", full[2], w_ple_gate, m_w_ple_gate, v_w_ple_gate),
                           ("w_ple_proj", full[3], w_ple_proj, m_w_ple_proj, v_w_ple_proj)):
        big[nm] = [o[None] for o in adamw(g, w[0], m[0], v[0], "adamw_" + nm)]

    wsm = max(d, cw)
    misc = jnp.concatenate([acc_qk[0:1, :HEAD_DIM], acc_qk[1:2, :HEAD_DIM], acc_sink[0:1, :nq]], axis=1)
    small = jnp.concatenate([
        _pad_cols(acc_x[0:1], wsm), _pad_cols(acc_g[0:1], wsm), _pad_cols(acc_head[0:1], wsm),
        _pad_cols(acc_head[1:2], wsm), _pad_cols(misc, wsm), _pad_cols(acc_conv[0:3], wsm)], axis=0)
    both = jnp.concatenate([small, _pad_rows(_pad_cols(acc_head[2:3], wsm), SUBLANES)], axis=0)
    tot = sum_devices(allgather_small(both))
    loss = tot[SUBLANES, 0]

    def pack(vals):
        ng, pg, bg, eg, qgv, kgv, sk, cv = vals
        misc_v = jnp.concatenate([qgv, kgv, sk], axis=1)
        return jnp.concatenate([_pad_cols(ng, wsm), _pad_cols(pg, wsm), _pad_cols(bg, wsm), _pad_cols(eg, wsm),
                                _pad_cols(misc_v, wsm), _pad_cols(cv[0], wsm)], axis=0)

    g_small = jnp.concatenate(
        [tot[0:5], _pad_cols(lax.dynamic_slice(tot[5:8], (0, chip * cq), (3, cq)), wsm)], axis=0)
    w_small = pack((norm_gain, ple_gate_norm_gain, b_ple_gate, ple_norm_gain, q_norm_gain, k_norm_gain,
                    attn_sinks, conv_w))
    m_small = pack((m_norm_gain, m_ple_gate_norm_gain, m_b_ple_gate, m_ple_norm_gain, m_q_norm_gain,
                    m_k_norm_gain, m_attn_sinks, m_conv_w))
    v_small = pack((v_norm_gain, v_ple_gate_norm_gain, v_b_ple_gate, v_ple_norm_gain, v_q_norm_gain,
                    v_k_norm_gain, v_attn_sinks, v_conv_w))
    sm = adamw(g_small, w_small, m_small, v_small, "adamw_small")

    def unpack(a):
        return {"norm_gain": a[0:1, :d], "ple_gate_norm_gain": a[1:2, :d], "b_ple_gate": a[2:3, :d],
                "ple_norm_gain": a[3:4, :d], "q_norm_gain": a[4:5, :HEAD_DIM],
                "k_norm_gain": a[4:5, HEAD_DIM:2 * HEAD_DIM],
                "attn_sinks": a[4:5, 2 * HEAD_DIM:2 * HEAD_DIM + nq], "conv_w": a[5:8, :cq][None]}

    order = ("norm_gain", "w_in", "q_norm_gain", "k_norm_gain", "attn_sinks", "conv_w", "w_out",
             "ple_gate_norm_gain", "w_ple_gate", "b_ple_gate", "w_ple_proj", "ple_norm_gain")
    outs = [loss, grad_x[None]]
    for kind in range(4):
        table = unpack(sm[kind])
        for nm in order:
            outs.append(big[nm][kind] if nm in big else table[nm])
    return tuple(outs)
```

```python
import functools

import jax
import jax.numpy as jnp
from jax import lax
from jax.experimental import pallas as pl
from jax.experimental.pallas import tpu as pltpu

F32 = jnp.float32
BF16 = jnp.bfloat16
MESH = pl.DeviceIdType.MESH

HEAD_DIM = 64
N_KV_HEADS = 4
KV_WIDTH = N_KV_HEADS * HEAD_DIM
BLOCK = 128
ROT_DIM = 16
ROPE_THETA = 500000.0
EPS = 1e-6
NEG_INF = -1e30
N_CHIPS = 4
LANES = 128
SUBLANES = 8
COLT = 512
SEG = 256
VMEM_LIMIT = 48 * 1024 * 1024

ADAM_LR = 0.001
ADAM_B1 = 0.9
ADAM_B2 = 0.999
ADAM_EPS = 1e-08
ADAM_WD = 0.01
ADAM_STEP = 10

NN = (((1,), (0,)), ((), ()))
NT = (((1,), (1,)), ((), ()))
TN = (((0,), (0,)), ((), ()))


def _call(body, **kw):
    return pl.pallas_call(body, **kw)


def _params(sem):
    return pltpu.CompilerParams(dimension_semantics=sem, vmem_limit_bytes=VMEM_LIMIT)


def _tile(n, pref):
    return pref if n % pref == 0 else n


def _sigmoid(v):
    return 1.0 / (1.0 + jnp.exp(-v))


def _hbm():
    return pl.BlockSpec(memory_space=pl.ANY)


def cast_into_slot(a, chip, name):
    r, c = a.shape
    tr = _tile(r, 256)

    def body(chip_ref, a_ref, o_ref):
        o_ref[...] = a_ref[...].astype(BF16)

    grid_spec = pltpu.PrefetchScalarGridSpec(
        num_scalar_prefetch=1, grid=(r // tr,),
        in_specs=[pl.BlockSpec((tr, c), lambda i, chip_ref: (i, 0))],
        out_specs=pl.BlockSpec((None, tr, c), lambda i, chip_ref: (chip_ref[0], i, 0)))
    return _call(body, name=name, grid_spec=grid_spec,
                 out_shape=jax.ShapeDtypeStruct((N_CHIPS, r, c), BF16),
                 compiler_params=_params(("parallel",)))(chip, a)


def rms_fwd(x, gain, name):
    s, d = x.shape
    tm = _tile(s, 256)

    def body(x_ref, g_ref, o_ref):
        xf = x_ref[...]
        r = lax.rsqrt(jnp.mean(xf * xf, axis=-1, keepdims=True) + EPS)
        o_ref[...] = ((xf * r) * g_ref[...]).astype(BF16)

    return _call(body, name=name, grid=(s // tm,),
                 in_specs=[pl.BlockSpec((tm, d), lambda i: (i, 0)),
                           pl.BlockSpec((1, d), lambda i: (0, 0))],
                 out_specs=pl.BlockSpec((tm, d), lambda i: (i, 0)),
                 out_shape=jax.ShapeDtypeStruct((s, d), BF16),
                 compiler_params=_params(("parallel",)))(x, gain)


def rms_bwd_add(dyn, xin, add, gain, name, want_bf16):
    s, d = xin.shape
    tm = _tile(s, 256)

    def body(dy_ref, x_ref, a_ref, g_ref, *outs):
        i = pl.program_id(0)
        dx_ref, acc_ref = outs[0], outs[-1]
        xf = x_ref[...]
        r = lax.rsqrt(jnp.mean(xf * xf, axis=-1, keepdims=True) + EPS)
        xhat = xf * r
        dyv = dy_ref[...]
        gd = dyv * g_ref[...]
        dx = a_ref[...] + r * (gd - xhat * jnp.mean(xhat * gd, axis=-1, keepdims=True))
        dx_ref[...] = dx
        if want_bf16:
            outs[1][...] = dx.astype(BF16)

        @pl.when(i == 0)
        def _():
            acc_ref[...] = jnp.zeros_like(acc_ref)

        acc_ref[0:1, :] += jnp.sum(dyv * xhat, axis=0, keepdims=True)

    row = pl.BlockSpec((tm, d), lambda i: (i, 0))
    out_specs = [row] + ([row] if want_bf16 else []) + [pl.BlockSpec((SUBLANES, d), lambda i: (0, 0))]
    out_shape = ([jax.ShapeDtypeStruct((s, d), F32)]
                 + ([jax.ShapeDtypeStruct((s, d), BF16)] if want_bf16 else [])
                 + [jax.ShapeDtypeStruct((SUBLANES, d), F32)])
    return _call(body, name=name, grid=(s // tm,),
                 in_specs=[row, row, row, pl.BlockSpec((1, d), lambda i: (0, 0))],
                 out_specs=out_specs, out_shape=out_shape,
                 compiler_params=_params(("arbitrary",)))(dyn, xin, add, gain)


def head_fwd_bwd(x1, gl, pe, tgt, bias, ple_gain):
    s, d = x1.shape
    tm = _tile(s, 128)

    def body(x1_ref, gl_ref, pe_ref, t_ref, b_ref, g_ref, dgl_ref, dpe_ref, dy_ref, acc_ref):
        i = pl.program_id(0)
        gate = _sigmoid(gl_ref[...] + b_ref[...])
        pev = pe_ref[...]
        r = lax.rsqrt(jnp.mean(pev * pev, axis=-1, keepdims=True) + EPS)
        pehat = pev * r
        gain = g_ref[...]
        e = pehat * gain
        diff = (x1_ref[...] + gate * e) - t_ref[...]
        dy = diff * (1.0 / d)
        dy_ref[...] = dy
        d_gl = (dy * e) * (gate * (1.0 - gate))
        dgl_ref[...] = d_gl.astype(BF16)
        d_e = dy * gate
        gd = d_e * gain
        d_pe = r * (gd - pehat * jnp.mean(pehat * gd, axis=-1, keepdims=True))
        dpe_ref[...] = d_pe.astype(BF16)

        @pl.when(i == 0)
        def _():
            acc_ref[...] = jnp.zeros_like(acc_ref)

        acc_ref[0:1, :] += jnp.sum(d_gl, axis=0, keepdims=True)
        acc_ref[1:2, :] += jnp.sum(d_e * pehat, axis=0, keepdims=True)
        acc_ref[2:3, :] += jnp.sum(diff * diff, axis=0, keepdims=True) * (0.5 / d)

    row = pl.BlockSpec((tm, d), lambda i: (i, 0))
    vec = pl.BlockSpec((1, d), lambda i: (0, 0))
    return _call(body, name="head_fwd_bwd", grid=(s // tm,),
                 in_specs=[row, row, row, row, vec, vec],
                 out_specs=[row, row, row, pl.BlockSpec((SUBLANES, d), lambda i: (0, 0))],
                 out_shape=[jax.ShapeDtypeStruct((s, d), BF16), jax.ShapeDtypeStruct((s, d), BF16),
                            jax.ShapeDtypeStruct((s, d), F32), jax.ShapeDtypeStruct((SUBLANES, d), F32)],
                 compiler_params=_params(("arbitrary",)))(x1, gl, pe, tgt, bias, ple_gain)


def adamw(g, w, m, v, name):
    r, c = g.shape
    tr = _tile(r, 128)

    def body(g_ref, w_ref, m_ref, v_ref, go_ref, d_ref, mo_ref, vo_ref):
        gv = g_ref[...]
        mn = ADAM_B1 * m_ref[...] + (1.0 - ADAM_B1) * gv
        vn = ADAM_B2 * v_ref[...] + (1.0 - ADAM_B2) * (gv * gv)
        m_hat = mn / (1.0 - ADAM_B1 ** ADAM_STEP)
        v_hat = vn / (1.0 - ADAM_B2 ** ADAM_STEP)
        go_ref[...] = gv
        d_ref[...] = -ADAM_LR * (m_hat / (jnp.sqrt(v_hat) + ADAM_EPS) + ADAM_WD * w_ref[...])
        mo_ref[...] = mn
        vo_ref[...] = vn

    blk = pl.BlockSpec((tr, c), lambda i: (i, 0))
    shp = jax.ShapeDtypeStruct((r, c), F32)
    return _call(body, name=name, grid=(r // tr,), in_specs=[blk] * 4, out_specs=[blk] * 4,
                 out_shape=[shp] * 4, compiler_params=_params(("parallel",)))(g, w, m, v)


def matmul(a, b, *, grid, a_spec, b_spec, o_spec, out_shape, dims, name, res=None, res_spec=None):
    nk = grid[2]
    acc_shape = tuple(d for d in o_spec.block_shape if d is not None)

    def body(*refs):
        if res is None:
            a_ref, b_ref, o_ref, acc_ref = refs
        else:
            a_ref, b_ref, r_ref, o_ref, acc_ref = refs
        k = pl.program_id(2)

        @pl.when(k == 0)
        def _():
            acc_ref[...] = jnp.zeros_like(acc_ref)

        acc_ref[...] += lax.dot_general(a_ref[...].astype(BF16), b_ref[...].astype(BF16), dims,
                                        preferred_element_type=F32)

        @pl.when(k == nk - 1)
        def _():
            out = acc_ref[...]
            if res is not None:
                out = r_ref[...] + out
            o_ref[...] = out.astype(o_ref.dtype)

    in_specs = [a_spec, b_spec] + ([res_spec] if res is not None else [])
    args = (a, b) + ((res,) if res is not None else ())
    return _call(body, name=name, grid=grid, in_specs=in_specs, out_specs=o_spec, out_shape=out_shape,
                 scratch_shapes=[pltpu.VMEM(acc_shape, F32)],
                 compiler_params=_params(("parallel", "parallel", "arbitrary")))(*args)


def _seg_mean(sq, segb):
    parts = []
    for cgrp in range(sq.shape[1] // SEG):
        blk = sq[:, cgrp * SEG:(cgrp + 1) * SEG]
        hi = blk.astype(BF16)
        r1 = blk - hi.astype(F32)
        mid = r1.astype(BF16)
        lo = (r1 - mid.astype(F32)).astype(BF16)
        acc = jnp.dot(hi, segb, preferred_element_type=F32)
        acc += jnp.dot(mid, segb, preferred_element_type=F32)
        acc += jnp.dot(lo, segb, preferred_element_type=F32)
        parts.append(acc)
    out = parts[0] if len(parts) == 1 else jnp.concatenate(parts, axis=1)
    return out * (1.0 / HEAD_DIM)


def _rope(v, cos, sa, sb):
    parts = []
    for cgrp in range(v.shape[1] // LANES):
        blk = v[:, cgrp * LANES:(cgrp + 1) * LANES]
        parts.append(blk * cos + pltpu.roll(blk, LANES - 8, 1) * sa + pltpu.roll(blk, 8, 1) * sb)
    return parts[0] if len(parts) == 1 else jnp.concatenate(parts, axis=1)


def _rope_t(dv, cos, sa, sb):
    parts = []
    for cgrp in range(dv.shape[1] // LANES):
        blk = dv[:, cgrp * LANES:(cgrp + 1) * LANES]
        parts.append(blk * cos + pltpu.roll(blk * sa, 8, 1) + pltpu.roll(blk * sb, LANES - 8, 1))
    return parts[0] if len(parts) == 1 else jnp.concatenate(parts, axis=1)


def _tile_lanes(vec, width):
    reps = width // LANES
    return vec if reps == 1 else jnp.tile(vec, (1, reps))


def qk_prep_fwd(z, tabs, qg, kg, segb, aw):
    s = z.shape[0]
    tm = _tile(s, 256)
    wq = aw + 2 * KV_WIDTH

    def body(z_ref, cos_ref, sa_ref, sb_ref, qg_ref, kg_ref, seg_ref, qs_ref, ks_ref, vb_ref):
        zz = z_ref[...]
        q, k, v = zz[:, :aw], zz[:, aw:aw + KV_WIDTH], zz[:, aw + KV_WIDTH:]
        cos, sa, sb, segm = cos_ref[...], sa_ref[...], sb_ref[...], seg_ref[...]
        rq = lax.rsqrt(_seg_mean(q * q, segm) + EPS)
        qn = (q * rq) * _tile_lanes(qg_ref[...], aw)
        qs_ref[...] = (_rope(qn, cos, sa, sb) * (HEAD_DIM ** -0.5)).astype(BF16)
        rk = lax.rsqrt(_seg_mean(k * k, segm) + EPS)
        kn = (k * rk) * _tile_lanes(kg_ref[...], KV_WIDTH)
        ks_ref[...] = _rope(kn, cos, sa, sb).astype(BF16)
        vb_ref[...] = v.astype(BF16)

    tab = pl.BlockSpec((tm, LANES), lambda i: (i, 0))
    vec = pl.BlockSpec((1, LANES), lambda i: (0, 0))
    return _call(body, name="qk_prep_fwd", grid=(s // tm,),
                 in_specs=[pl.BlockSpec((tm, wq), lambda i: (i, 0)), tab, tab, tab, vec, vec,
                           pl.BlockSpec((SEG, SEG), lambda i: (0, 0))],
                 out_specs=[pl.BlockSpec((tm, aw), lambda i: (i, 0)),
                            pl.BlockSpec((tm, KV_WIDTH), lambda i: (i, 0)),
                            pl.BlockSpec((tm, KV_WIDTH), lambda i: (i, 0))],
                 out_shape=[jax.ShapeDtypeStruct((s, aw), BF16), jax.ShapeDtypeStruct((s, KV_WIDTH), BF16),
                            jax.ShapeDtypeStruct((s, KV_WIDTH), BF16)],
                 compiler_params=_params(("parallel",)))(z, *tabs, qg, kg, segb)


def qk_prep_bwd(z, dqs, dks, dvs, tabs, qg, kg, segb, dz, aw):
    s = z.shape[0]
    tm = _tile(s, 256)
    wq = aw + 2 * KV_WIDTH

    def body(z_ref, dq_ref, dk_ref, dv_ref, cos_ref, sa_ref, sb_ref, qg_ref, kg_ref, seg_ref, dz_in,
             dz_ref, acc_ref):
        i = pl.program_id(0)
        zz = z_ref[...]
        q, k = zz[:, :aw], zz[:, aw:aw + KV_WIDTH]
        cos, sa, sb, segm = cos_ref[...], sa_ref[...], sb_ref[...], seg_ref[...]

        def one(xv, dout, gvec, width):
            g = _tile_lanes(gvec, width)
            r = lax.rsqrt(_seg_mean(xv * xv, segm) + EPS)
            xhat = xv * r
            dn = _rope_t(dout, cos, sa, sb)
            gd = dn * g
            dx = r * (gd - xhat * _seg_mean(xhat * gd, segm))
            contrib = jnp.sum(dn * xhat, axis=0, keepdims=True)
            folded = contrib[:, :LANES]
            for cgrp in range(1, width // LANES):
                folded = folded + contrib[:, cgrp * LANES:(cgrp + 1) * LANES]
            return dx, folded + pltpu.roll(folded, HEAD_DIM, 1)

        dq, gq = one(q, dq_ref[...] * (HEAD_DIM ** -0.5), qg_ref[...], aw)
        dk, gk = one(k, dk_ref[...], kg_ref[...], KV_WIDTH)
        dz_ref[:, :aw] = dq.astype(BF16)
        dz_ref[:, aw:aw + KV_WIDTH] = dk.astype(BF16)
        dz_ref[:, aw + KV_WIDTH:] = dv_ref[...].astype(BF16)

        @pl.when(i == 0)
        def _():
            acc_ref[...] = jnp.zeros_like(acc_ref)

        acc_ref[0:1, :] += gq
        acc_ref[1:2, :] += gk

    tab = pl.BlockSpec((tm, LANES), lambda i: (i, 0))
    vec = pl.BlockSpec((1, LANES), lambda i: (0, 0))
    kvb = pl.BlockSpec((tm, KV_WIDTH), lambda i: (i, 0))
    return _call(body, name="qk_prep_bwd", grid=(s // tm,),
                 in_specs=[pl.BlockSpec((tm, wq), lambda i: (i, 0)),
                           pl.BlockSpec((tm, aw), lambda i: (i, 0)), kvb, kvb, tab, tab, tab, vec, vec,
                           pl.BlockSpec((SEG, SEG), lambda i: (0, 0)), _hbm()],
                 out_specs=[pl.BlockSpec((tm, wq), lambda i: (i, 0)),
                            pl.BlockSpec((SUBLANES, LANES), lambda i: (0, 0))],
                 out_shape=[jax.ShapeDtypeStruct(dz.shape, BF16), jax.ShapeDtypeStruct((SUBLANES, LANES), F32)],
                 input_output_aliases={10: 0},
                 compiler_params=_params(("arbitrary",)))(z, dqs, dks, dvs, *tabs, qg, kg, segb, dz)


def _placed(band, lane_idx):
    out = {}
    for kh in range(N_KV_HEADS):
        grp = band[:, (kh // 2) * LANES:(kh // 2 + 1) * LANES]
        for half in (0, 1):
            t = grp if half == kh % 2 else pltpu.roll(grp, HEAD_DIM, 1)
            keep = (lane_idx >= half * HEAD_DIM) & (lane_idx < (half + 1) * HEAD_DIM)
            out[kh, half] = jnp.where(keep, t, jnp.zeros_like(t))
    return out


def _scores(qgrp, kpl, valid, sink):
    sc = lax.dot_general(qgrp, kpl, NT, preferred_element_type=F32)
    sc = jnp.where(valid, sc, NEG_INF)
    m = jnp.maximum(jnp.max(sc, axis=-1, keepdims=True), sink)
    e = jnp.exp(sc - m)
    es = jnp.exp(sink - m)
    den = jnp.sum(e, axis=-1, keepdims=True) + es
    return e / den, es / den


def _valid_mask(n):
    r_i = lax.broadcasted_iota(jnp.int32, (BLOCK, 2 * BLOCK), 0)
    j_i = lax.broadcasted_iota(jnp.int32, (BLOCK, 2 * BLOCK), 1)
    return (j_i > r_i) & (j_i <= r_i + BLOCK) & ((n > 0) | (j_i >= BLOCK))


def attn_fwd(qs, ks, vb, z, sinks, aw, d, ga_off):
    s = qs.shape[0]
    nb = s // BLOCK
    nq = aw // HEAD_DIM
    grp_sz = nq // N_KV_HEADS
    n_ga = aw // COLT

    def body(q_ref, ko_ref, kp_ref, vo_ref, vp_ref, *rest):
        ga_refs = rest[:n_ga]
        sink_ref, attn_ref, mix_ref = rest[n_ga:]
        n = pl.program_id(0)
        lane_idx = lax.broadcasted_iota(jnp.int32, (2 * BLOCK, LANES), 1)
        kpl = _placed(jnp.concatenate([kp_ref[...], ko_ref[...]], axis=0), lane_idx)
        vpl = _placed(jnp.concatenate([vp_ref[...], vo_ref[...]], axis=0), lane_idx)
        valid = _valid_mask(n)
        for cgrp in range(nq // 2):
            qgrp = q_ref[:, cgrp * LANES:(cgrp + 1) * LANES]
            acc = jnp.zeros((BLOCK, LANES), F32)
            for half in (0, 1):
                h = 2 * cgrp + half
                kh = h // grp_sz
                p, _ = _scores(qgrp, kpl[kh, half], valid, sink_ref[0, h])
                acc += jnp.dot(p.astype(BF16), vpl[kh, half], preferred_element_type=F32)
            attn_ref[:, cgrp * LANES:(cgrp + 1) * LANES] = acc
            col = cgrp * LANES
            ga = ga_refs[col // COLT][:, col % COLT:col % COLT + LANES]
            mix_ref[:, cgrp * LANES:(cgrp + 1) * LANES] = (acc * (ga * _sigmoid(ga))).astype(BF16)

    own = lambda n: (n, 0)
    prev = lambda n: (jnp.maximum(n - 1, 0), 0)
    kvs = lambda imap: pl.BlockSpec((BLOCK, KV_WIDTH), imap)
    ga_specs = [pl.BlockSpec((BLOCK, COLT), functools.partial(lambda n, j: (n, ga_off + j), j=j))
                for j in range(n_ga)]
    return _call(body, name="attn_fwd", grid=(nb,),
                 in_specs=[pl.BlockSpec((BLOCK, aw), own), kvs(own), kvs(prev), kvs(own), kvs(prev)]
                 + ga_specs + [pl.BlockSpec(memory_space=pltpu.SMEM)],
                 out_specs=[pl.BlockSpec((BLOCK, aw), own), pl.BlockSpec((BLOCK, aw), own)],
                 out_shape=[jax.ShapeDtypeStruct((s, aw), F32), jax.ShapeDtypeStruct((s, d), BF16)],
                 compiler_params=_params(("parallel",)))(qs, ks, ks, vb, vb, *([z] * n_ga), sinks)


def gate_bwd(d_mix, attn, z, aw, ga_off):
    s, in_w = z.shape
    tm = _tile(s, 256)

    def body(dm_ref, at_ref, ga_ref, do_ref, dz_ref):
        ga = ga_ref[...]
        sig = _sigmoid(ga)
        dm = dm_ref[...]
        do_ref[...] = (dm * (ga * sig)).astype(BF16)
        dz_ref[...] = ((dm * at_ref[...]) * (sig * (1.0 + ga * (1.0 - sig)))).astype(BF16)

    blk = pl.BlockSpec((tm, COLT), lambda i, j: (i, j))
    gab = pl.BlockSpec((tm, COLT), lambda i, j: (i, ga_off + j))
    return _call(body, name="gate_bwd", grid=(s // tm, aw // COLT),
                 in_specs=[blk, blk, gab], out_specs=[blk, gab],
                 out_shape=[jax.ShapeDtypeStruct((s, aw), BF16), jax.ShapeDtypeStruct((s, in_w), BF16)],
                 compiler_params=_params(("parallel", "parallel")))(d_mix, attn, z)


def attn_bwd(qs, ks, vb, d_o, sinks, aw):
    s = qs.shape[0]
    nb = s // BLOCK
    nq = aw // HEAD_DIM
    grp_sz = nq // N_KV_HEADS

    def body(q_ref, ko_ref, kp_ref, vo_ref, vp_ref, do_ref, sink_ref, dq_ref, dk_ref, dv_ref, ds_ref,
             ck_ref, cv_ref):
        n = pl.program_id(0)

        @pl.when(n == 0)
        def _():
            ds_ref[...] = jnp.zeros_like(ds_ref)
            dk_ref[...] = jnp.zeros_like(dk_ref)
            dv_ref[...] = jnp.zeros_like(dv_ref)

        @pl.when(n < nb)
        def _():
            lane_idx = lax.broadcasted_iota(jnp.int32, (2 * BLOCK, LANES), 1)
            lane_row = lax.broadcasted_iota(jnp.int32, (1, LANES), 1)
            kpl = _placed(jnp.concatenate([kp_ref[...], ko_ref[...]], axis=0), lane_idx)
            vpl = _placed(jnp.concatenate([vp_ref[...], vo_ref[...]], axis=0), lane_idx)
            valid = _valid_mask(n)
            dk_acc = [jnp.zeros((2 * BLOCK, LANES), F32) for _ in range(N_KV_HEADS)]
            dv_acc = [jnp.zeros((2 * BLOCK, LANES), F32) for _ in range(N_KV_HEADS)]
            ds_row = jnp.zeros((1, LANES), F32)
            for cgrp in range(nq // 2):
                qgrp = q_ref[:, cgrp * LANES:(cgrp + 1) * LANES]
                dog = do_ref[:, cgrp * LANES:(cgrp + 1) * LANES]
                dq_acc = jnp.zeros((BLOCK, LANES), F32)
                for half in (0, 1):
                    h = 2 * cgrp + half
                    kh = h // grp_sz
                    p, p_sink = _scores(qgrp, kpl[kh, half], valid, sink_ref[0, h])
                    dp = lax.dot_general(dog, vpl[kh, half], NT, preferred_element_type=F32)
                    delta = jnp.sum(p * dp, axis=-1, keepdims=True)
                    dsb = (p * (dp - delta)).astype(BF16)
                    dq_acc += jnp.dot(dsb, kpl[kh, half], preferred_element_type=F32)
                    keep = (lane_idx >= half * HEAD_DIM) & (lane_idx < (half + 1) * HEAD_DIM)
                    dkp = jnp.where(keep, lax.dot_general(dsb, qgrp, TN, preferred_element_type=F32), 0.0)
                    dvp = jnp.where(keep, lax.dot_general(p.astype(BF16), dog, TN, preferred_element_type=F32), 0.0)
                    if half != kh % 2:
                        dkp = pltpu.roll(dkp, HEAD_DIM, 1)
                        dvp = pltpu.roll(dvp, HEAD_DIM, 1)
                    dk_acc[kh] += dkp
                    dv_acc[kh] += dvp
                    dsink = -jnp.sum(p_sink * delta, axis=0, keepdims=True)
                    ds_row += jnp.where(lane_row == h, dsink, 0.0)
                dq_ref[:, cgrp * LANES:(cgrp + 1) * LANES] = dq_acc
            ds_ref[0:1, :] += ds_row
            dk_band = jnp.concatenate([dk_acc[0] + dk_acc[1], dk_acc[2] + dk_acc[3]], axis=1)
            dv_band = jnp.concatenate([dv_acc[0] + dv_acc[1], dv_acc[2] + dv_acc[3]], axis=1)

            @pl.when(n > 0)
            def _():
                dk_ref[...] = ck_ref[...] + dk_band[:BLOCK]
                dv_ref[...] = cv_ref[...] + dv_band[:BLOCK]

            ck_ref[...] = dk_band[BLOCK:]
            cv_ref[...] = dv_band[BLOCK:]

        @pl.when(n == nb)
        def _():
            dk_ref[...] = ck_ref[...]
            dv_ref[...] = cv_ref[...]

    own = lambda n: (jnp.minimum(n, nb - 1), 0)
    prev = lambda n: (jnp.clip(n - 1, 0, nb - 1), 0)
    done = lambda n: (jnp.maximum(n - 1, 0), 0)
    kvs = lambda imap: pl.BlockSpec((BLOCK, KV_WIDTH), imap)
    return _call(body, name="attn_bwd", grid=(nb + 1,),
                 in_specs=[pl.BlockSpec((BLOCK, aw), own), kvs(own), kvs(prev), kvs(own), kvs(prev),
                           pl.BlockSpec((BLOCK, aw), own), pl.BlockSpec(memory_space=pltpu.SMEM)],
                 out_specs=[pl.BlockSpec((BLOCK, aw), own), kvs(done), kvs(done),
                            pl.BlockSpec((SUBLANES, LANES), lambda n: (0, 0))],
                 out_shape=[jax.ShapeDtypeStruct((s, aw), F32), jax.ShapeDtypeStruct((s, KV_WIDTH), F32),
                            jax.ShapeDtypeStruct((s, KV_WIDTH), F32), jax.ShapeDtypeStruct((SUBLANES, LANES), F32)],
                 scratch_shapes=[pltpu.VMEM((BLOCK, KV_WIDTH), F32), pltpu.VMEM((BLOCK, KV_WIDTH), F32)],
                 compiler_params=_params(("arbitrary",)))(qs, ks, ks, vb, vb, d_o, sinks)


def conv_fwd(z, conv_w, mix, aw, cw, b_off):
    s = z.shape[0]
    tm = _tile(s, 256)
    nseg = cw // COLT
    hb = tm // SUBLANES

    def body(b_ref, c_ref, h_ref, g_ref, cp_ref, hp_ref, w_ref, mix_in, mix_ref):
        i = pl.program_id(0)
        u = c_ref[...] * h_ref[...]
        up = jnp.where(i > 0, cp_ref[...] * hp_ref[...], 0.0)
        ext = jnp.concatenate([up, u], axis=0)
        um1 = pltpu.roll(ext, 1, 0)[SUBLANES:]
        um2 = pltpu.roll(ext, 2, 0)[SUBLANES:]
        w = w_ref[...]
        cv = w[0:1] * um2 + w[1:2] * um1 + w[2:3] * u
        g = g_ref[...]
        mix_ref[...] = ((b_ref[...] * cv) * (g * _sigmoid(g))).astype(BF16)

    seg = lambda k: pl.BlockSpec((tm, COLT), functools.partial(lambda i, j, k: (i, b_off + k * nseg + j), k=k))
    halo = lambda k: pl.BlockSpec(
        (SUBLANES, COLT), functools.partial(lambda i, j, k: (jnp.maximum(i * hb - 1, 0), b_off + k * nseg + j), k=k))
    return _call(body, name="conv_fwd", grid=(s // tm, nseg),
                 in_specs=[seg(0), seg(1), seg(2), seg(3), halo(1), halo(2),
                           pl.BlockSpec((SUBLANES, COLT), lambda i, j: (0, j)), _hbm()],
                 out_specs=pl.BlockSpec((tm, COLT), lambda i, j: (i, aw // COLT + j)),
                 out_shape=jax.ShapeDtypeStruct(mix.shape, BF16),
                 input_output_aliases={7: 0},
                 compiler_params=_params(("parallel", "parallel")))(z, z, z, z, z, z, conv_w, mix)


def conv_bwd(z, d_mix, conv_w, dz, aw, cw, b_off):
    s = z.shape[0]
    tm = _tile(s, 256)
    nseg = cw // COLT
    hb = tm // SUBLANES
    n_row = s // tm
    last_h = s // SUBLANES - 1

    def body(b_ref, c_ref, h_ref, g_ref, cp_ref, hp_ref, bn_ref, gn_ref, dm_ref, dmn_ref, w_ref, dz_in,
             dz_ref, acc_ref, stash_ref):
        i = pl.program_id(1)
        k = pl.program_id(2)

        @pl.when(k == 0)
        def _():
            cc, hh, bb, g = c_ref[...], h_ref[...], b_ref[...], g_ref[...]
            w = w_ref[...]
            u = cc * hh
            up = jnp.where(i > 0, cp_ref[...] * hp_ref[...], 0.0)
            ext = jnp.concatenate([up, u], axis=0)
            um1 = pltpu.roll(ext, 1, 0)[SUBLANES:]
            um2 = pltpu.roll(ext, 2, 0)[SUBLANES:]
            cv = w[0:1] * um2 + w[1:2] * um1 + w[2:3] * u
            sig = _sigmoid(g)
            sg = g * sig
            dm = dm_ref[...]
            d_cv = (dm * bb) * sg
            gn = gn_ref[...]
            d_cv_next = jnp.where(i < n_row - 1, (dmn_ref[...] * bn_ref[...]) * (gn * _sigmoid(gn)), 0.0)
            ext2 = jnp.concatenate([d_cv, d_cv_next], axis=0)
            dp1 = pltpu.roll(ext2, tm + SUBLANES - 1, 0)[:tm]
            dp2 = pltpu.roll(ext2, tm + SUBLANES - 2, 0)[:tm]
            d_u = w[2:3] * d_cv + w[1:2] * dp1 + w[0:1] * dp2
            stash_ref[0] = ((dm * cv) * sg).astype(BF16)
            stash_ref[1] = (d_u * hh).astype(BF16)
            stash_ref[2] = (d_u * cc).astype(BF16)
            stash_ref[3] = (((dm * bb) * cv) * (sig * (1.0 + g * (1.0 - sig)))).astype(BF16)

            @pl.when(i == 0)
            def _():
                acc_ref[...] = jnp.zeros_like(acc_ref)

            acc_ref[0:1, :] += jnp.sum(d_cv * um2, axis=0, keepdims=True)
            acc_ref[1:2, :] += jnp.sum(d_cv * um1, axis=0, keepdims=True)
            acc_ref[2:3, :] += jnp.sum(d_cv * u, axis=0, keepdims=True)

        dz_ref[...] = stash_ref[k]

    seg = lambda q: pl.BlockSpec((tm, COLT), functools.partial(lambda j, i, k, q: (i, b_off + q * nseg + j), q=q))
    halo_p = lambda q: pl.BlockSpec(
        (SUBLANES, COLT),
        functools.partial(lambda j, i, k, q: (jnp.maximum(i * hb - 1, 0), b_off + q * nseg + j), q=q))
    halo_n = lambda q: pl.BlockSpec(
        (SUBLANES, COLT),
        functools.partial(lambda j, i, k, q: (jnp.minimum((i + 1) * hb, last_h), b_off + q * nseg + j), q=q))
    return _call(body, name="conv_bwd", grid=(nseg, n_row, 4),
                 in_specs=[seg(0), seg(1), seg(2), seg(3), halo_p(1), halo_p(2), halo_n(0), halo_n(3),
                           pl.BlockSpec((tm, COLT), lambda j, i, k: (i, aw // COLT + j)),
                           pl.BlockSpec((SUBLANES, COLT),
                                        lambda j, i, k: (jnp.minimum((i + 1) * hb, last_h), aw // COLT + j)),
                           pl.BlockSpec((SUBLANES, COLT), lambda j, i, k: (0, j)), _hbm()],
                 out_specs=[pl.BlockSpec((tm, COLT), lambda j, i, k: (i, b_off + k * nseg + j)),
                            pl.BlockSpec((SUBLANES, COLT), lambda j, i, k: (0, j))],
                 out_shape=[jax.ShapeDtypeStruct(dz.shape, BF16), jax.ShapeDtypeStruct((SUBLANES, cw), F32)],
                 input_output_aliases={11: 0},
                 scratch_shapes=[pltpu.VMEM((4, tm, COLT), BF16)],
                 compiler_params=_params(("arbitrary", "arbitrary", "arbitrary")))(
                     z, z, z, z, z, z, z, z, d_mix, d_mix, conv_w, dz)


def _place():
    x, y, c = lax.axis_index("x"), lax.axis_index("y"), lax.axis_index("c")
    chips = [(1 - x, y), (x, 1 - y), (1 - x, 1 - y)]
    return x, y, c, chips


def allgather_weights(bufs, conv_buf):
    nw = len(bufs)

    def body(*refs):
        ins, conv_in = refs[:nw], refs[nw]
        outs, conv_out = refs[nw + 1:2 * nw + 1], refs[2 * nw + 1]
        send, recv, fsend, frecv, csend, crecv = refs[2 * nw + 2:]
        x, y, c, chips = _place()
        me = 2 * x + y

        def rows(w, slot, core):
            hr = ins[w].shape[1] // 2
            return (slot, pl.ds(core * hr, hr))

        def ici(w, j, slot):
            cx, cy = chips[j]
            return pltpu.make_async_remote_copy(
                src_ref=ins[w].at[rows(w, slot, c)], dst_ref=outs[w].at[rows(w, slot, c)],
                send_sem=send.at[w, j], recv_sem=recv.at[w, j], device_id=(cx, cy, c), device_id_type=MESH)

        def passed(w, j, from_core):
            cx, cy = chips[j]
            half = outs[w].at[rows(w, 2 * cx + cy, from_core)]
            return pltpu.make_async_remote_copy(
                src_ref=half, dst_ref=half, send_sem=fsend.at[w, j], recv_sem=frecv.at[w, j],
                device_id=(x, y, 1 - c), device_id_type=MESH)

        def conv(j, slot):
            cx, cy = chips[j]
            return pltpu.make_async_remote_copy(
                src_ref=conv_in.at[slot], dst_ref=conv_out.at[slot], send_sem=csend.at[j], recv_sem=crecv.at[j],
                device_id=(cx, cy, c), device_id_type=MESH)

        for w in range(nw):
            for j in range(3):
                ici(w, j, me).start()
        for j in range(3):
            conv(j, me).start()
        for w in range(nw):
            for j in range(3):
                cx, cy = chips[j]
                ici(w, j, 2 * cx + cy).wait_recv()
                passed(w, j, c).start()
        for w in range(nw):
            for j in range(3):
                passed(w, j, 1 - c).wait_recv()
        for j in range(3):
            cx, cy = chips[j]
            conv(j, 2 * cx + cy).wait_recv()
        for w in range(nw):
            for j in range(3):
                ici(w, j, me).wait_send()
                passed(w, j, c).wait_send()
        for j in range(3):
            conv(j, me).wait_send()

    every = list(bufs) + [conv_buf]
    return _call(body, name="allgather_weights", in_specs=[_hbm()] * (nw + 1), out_specs=[_hbm()] * (nw + 1),
                 out_shape=[jax.ShapeDtypeStruct(a.shape, a.dtype) for a in every],
                 input_output_aliases={i: i for i in range(nw + 1)},
                 scratch_shapes=[pltpu.SemaphoreType.DMA((nw, 3)), pltpu.SemaphoreType.DMA((nw, 3)),
                                 pltpu.SemaphoreType.DMA((nw, 3)), pltpu.SemaphoreType.DMA((nw, 3)),
                                 pltpu.SemaphoreType.DMA((3,)), pltpu.SemaphoreType.DMA((3,))])(*every)


def sibling_swap_halves(grads):
    nw = len(grads)

    def body(*refs):
        ins, outs = refs[:nw], refs[nw:2 * nw]
        send, recv = refs[2 * nw:]
        x, y, c, _ = _place()
        cps = []
        for w in range(nw):
            hr = ins[w].shape[1] // 2
            cps.append(pltpu.make_async_remote_copy(
                src_ref=ins[w].at[:, pl.ds((1 - c) * hr, hr), :], dst_ref=outs[w],
                send_sem=send.at[w], recv_sem=recv.at[w], device_id=(x, y, 1 - c), device_id_type=MESH))
        for cp in cps:
            cp.start()
        for cp in cps:
            cp.wait()

    out_shape = [jax.ShapeDtypeStruct((N_CHIPS, a.shape[1] // 2, a.shape[2]), a.dtype) for a in grads]
    return _call(body, name="sibling_swap_halves", in_specs=[_hbm()] * nw, out_specs=[_hbm()] * nw,
                 out_shape=out_shape,
                 scratch_shapes=[pltpu.SemaphoreType.DMA((nw,)), pltpu.SemaphoreType.DMA((nw,))])(*grads)


def scatter_to_owner(parts):
    nw = len(parts)

    def body(*refs):
        ins, outs = refs[:nw], refs[nw:2 * nw]
        send, recv = refs[2 * nw:]
        x, y, c, chips = _place()
        me = 2 * x + y

        def ici(w, j, slot_src, slot_dst):
            cx, cy = chips[j]
            return pltpu.make_async_remote_copy(
                src_ref=ins[w].at[slot_src], dst_ref=outs[w].at[slot_dst],
                send_sem=send.at[w, j], recv_sem=recv.at[w, j], device_id=(cx, cy, c), device_id_type=MESH)

        for w in range(nw):
            for j in range(3):
                cx, cy = chips[j]
                ici(w, j, 2 * cx + cy, me).start()
        for w in range(nw):
            for j in range(3):
                cx, cy = chips[j]
                ici(w, j, me, 2 * cx + cy).wait_recv()
        for w in range(nw):
            for j in range(3):
                cx, cy = chips[j]
                ici(w, j, 2 * cx + cy, me).wait_send()

    out_shape = [jax.ShapeDtypeStruct(a.shape, a.dtype) for a in parts]
    return _call(body, name="scatter_to_owner", in_specs=[_hbm()] * nw, out_specs=[_hbm()] * nw,
                 out_shape=out_shape,
                 scratch_shapes=[pltpu.SemaphoreType.DMA((nw, 3)), pltpu.SemaphoreType.DMA((nw, 3))])(*parts)


def sibling_join_halves(fulls):
    nw = len(fulls)

    def body(*refs):
        ins, outs = refs[:nw], refs[nw:2 * nw]
        send, recv = refs[2 * nw:]
        x, y, c, _ = _place()

        def half(w, core):
            hr = ins[w].shape[0] // 2
            return pltpu.make_async_remote_copy(
                src_ref=ins[w].at[pl.ds(core * hr, hr)], dst_ref=outs[w].at[pl.ds(core * hr, hr)],
                send_sem=send.at[w], recv_sem=recv.at[w], device_id=(x, y, 1 - c), device_id_type=MESH)

        for w in range(nw):
            half(w, c).start()
        for w in range(nw):
            half(w, 1 - c).wait_recv()
        for w in range(nw):
            half(w, c).wait_send()

    return _call(body, name="sibling_join_halves", in_specs=[_hbm()] * nw, out_specs=[_hbm()] * nw,
                 out_shape=[jax.ShapeDtypeStruct(a.shape, a.dtype) for a in fulls],
                 input_output_aliases={i: i for i in range(nw)},
                 scratch_shapes=[pltpu.SemaphoreType.DMA((nw,)), pltpu.SemaphoreType.DMA((nw,))])(*fulls)


def allgather_small(small):
    rows, width = small.shape

    def body(in_ref, out_ref, send, recv, loc):
        x, y, c, _ = _place()
        me = 4 * x + 2 * y + c
        local = pltpu.make_async_copy(in_ref, out_ref.at[me], loc)
        local.start()
        flips = [(fx, fy, fc) for fx in (0, 1) for fy in (0, 1) for fc in (0, 1)][1:]

        def cp(k, slot):
            fx, fy, fc = flips[k]
            return pltpu.make_async_remote_copy(
                src_ref=in_ref, dst_ref=out_ref.at[slot], send_sem=send.at[k], recv_sem=recv.at[k],
                device_id=(x ^ fx, y ^ fy, c ^ fc), device_id_type=MESH)

        for k in range(7):
            cp(k, me).start()
        for k in range(7):
            fx, fy, fc = flips[k]
            cp(k, 4 * (x ^ fx) + 2 * (y ^ fy) + (c ^ fc)).wait_recv()
        for k in range(7):
            cp(k, me).wait_send()
        local.wait()

    return _call(body, name="allgather_small",
                 in_specs=[pl.BlockSpec(memory_space=pltpu.VMEM)],
                 out_specs=pl.BlockSpec(memory_space=pltpu.VMEM),
                 out_shape=jax.ShapeDtypeStruct((8, rows, width), F32),
                 scratch_shapes=[pltpu.SemaphoreType.DMA((7,)), pltpu.SemaphoreType.DMA((7,)),
                                 pltpu.SemaphoreType.DMA])(small)


def add_sibling(grad, got, core, name):
    _, r, c = grad.shape
    hr = r // 2
    tr = _tile(hr, 128)
    nblk = hr // tr

    def body(core_ref, g_ref, o_ref, out_ref):
        out_ref[...] = (g_ref[...].astype(F32) + o_ref[...].astype(F32)).astype(BF16)

    grid_spec = pltpu.PrefetchScalarGridSpec(
        num_scalar_prefetch=1, grid=(N_CHIPS, nblk),
        in_specs=[pl.BlockSpec((None, tr, c), lambda t, i, core_ref: (t, core_ref[0] * nblk + i, 0)),
                  pl.BlockSpec((None, tr, c), lambda t, i, core_ref: (t, i, 0))],
        out_specs=pl.BlockSpec((None, tr, c), lambda t, i, core_ref: (t, i, 0)))
    return _call(body, name=name, grid_spec=grid_spec,
                 out_shape=jax.ShapeDtypeStruct((N_CHIPS, hr, c), BF16),
                 compiler_params=_params(("parallel", "parallel")))(core, grad, got)


def sum_chips(mine, owned, place, name):
    _, hr, c = mine.shape
    tr = _tile(hr, 128)
    nblk = hr // tr

    def body(place_ref, m_ref, o1_ref, o2_ref, o3_ref, out_ref):
        acc = m_ref[...].astype(F32)
        for o_ref in (o1_ref, o2_ref, o3_ref):
            acc = acc + o_ref[...].astype(F32)
        out_ref[...] = acc

    other = lambda k: pl.BlockSpec((None, tr, c), lambda i, place_ref: ((place_ref[0] + k) % N_CHIPS, i, 0))
    grid_spec = pltpu.PrefetchScalarGridSpec(
        num_scalar_prefetch=1, grid=(nblk,),
        in_specs=[other(0), other(1), other(2), other(3)],
        out_specs=pl.BlockSpec((tr, c), lambda i, place_ref: (place_ref[1] * nblk + i, 0)))
    return _call(body, name=name, grid_spec=grid_spec,
                 out_shape=jax.ShapeDtypeStruct((2 * hr, c), F32),
                 compiler_params=_params(("parallel",)))(place, mine, owned, owned, owned)


def sum_devices(gathered):
    _, rows, width = gathered.shape

    def body(g_ref, out_ref):
        acc = g_ref[0]
        for dev in range(1, 8):
            acc = acc + g_ref[dev]
        out_ref[...] = acc
        tail = acc[SUBLANES:]
        out_ref[SUBLANES:, :] = jnp.broadcast_to(jnp.sum(tail, axis=1, keepdims=True), tail.shape)

    return _call(body, name="sum_devices",
                 in_specs=[pl.BlockSpec(memory_space=pltpu.VMEM)],
                 out_specs=pl.BlockSpec(memory_space=pltpu.VMEM),
                 out_shape=jax.ShapeDtypeStruct((rows, width), F32))(gathered)


def _rope_tables(s):
    half = ROT_DIM // 2
    inv_freq = jnp.power(jnp.float32(ROPE_THETA), -jnp.arange(half, dtype=F32) * 2.0 / ROT_DIM)
    ang = jnp.arange(s).astype(F32)[:, None] * inv_freq[None, :]
    cos, sin = jnp.cos(ang), jnp.sin(ang)
    zeros = lambda n: jnp.zeros((s, n), F32)
    cos64 = jnp.concatenate([cos, cos, jnp.ones((s, HEAD_DIM - ROT_DIM), F32)], axis=1)
    sa64 = jnp.concatenate([-sin, zeros(HEAD_DIM - half)], axis=1)
    sb64 = jnp.concatenate([zeros(half), sin, zeros(HEAD_DIM - ROT_DIM)], axis=1)
    return tuple(jnp.concatenate([t, t], axis=1) for t in (cos64, sa64, sb64))


def _pad_rows(a, rows):
    return jnp.pad(a, ((0, rows - a.shape[0]), (0, 0)))


def _pad_cols(a, cols):
    return jnp.pad(a, ((0, 0), (0, cols - a.shape[1])))


def kernel(x, p, norm_gain, w_in, q_norm_gain, k_norm_gain, attn_sinks, conv_w, w_out, ple_gate_norm_gain, w_ple_gate, b_ple_gate, w_ple_proj, ple_norm_gain, loss_target, m_norm_gain, m_w_in, m_q_norm_gain, m_k_norm_gain, m_attn_sinks, m_conv_w, m_w_out, m_ple_gate_norm_gain, m_w_ple_gate, m_b_ple_gate, m_w_ple_proj, m_ple_norm_gain, v_norm_gain, v_w_in, v_q_norm_gain, v_k_norm_gain, v_attn_sinks, v_conv_w, v_w_out, v_ple_gate_norm_gain, v_w_ple_gate, v_b_ple_gate, v_w_ple_proj, v_ple_norm_gain):
    x2, p2, tgt = x[0], p[0, 0], loss_target[0]
    s, d = x2.shape
    ple = p2.shape[1]
    aw = d // 2
    cw = d - aw
    nq = aw // HEAD_DIM
    sh = w_in.shape[2]
    in_w = N_CHIPS * sh
    dq = d // N_CHIPS
    cq = cw // N_CHIPS
    ga_off = (aw + 2 * KV_WIDTH) // COLT
    b_off = (2 * aw + 2 * KV_WIDTH) // COLT
    assert in_w == 2 * aw + 2 * KV_WIDTH + 4 * cw and aw % COLT == 0 and cw % COLT == 0
    assert s % BLOCK == 0 and sh % LANES == 0 and nq % (2 * N_KV_HEADS) == 0

    core = lax.axis_index("c").astype(jnp.int32).reshape(1)
    chip = 2 * lax.axis_index("x") + lax.axis_index("y")

    chip1 = chip.astype(jnp.int32).reshape(1)
    place = jnp.concatenate([chip1, core])
    bufs = [cast_into_slot(w_in[0], chip1, "cast_w_in"), cast_into_slot(w_out[0], chip1, "cast_w_out"),
            cast_into_slot(w_ple_gate[0], chip1, "cast_w_pg"), cast_into_slot(w_ple_proj[0], chip1, "cast_w_pp")]
    conv_buf = lax.dynamic_update_slice(jnp.zeros((N_CHIPS, SUBLANES, cq), F32),
                                        _pad_rows(conv_w[0], SUBLANES)[None], (chip, 0, 0))
    wi_all, wo_all, wg_all, wp_all, conv_all = allgather_weights(bufs, conv_buf)
    wo_full = wo_all.reshape(d, d)
    wg_full = wg_all.reshape(d, d)
    conv_full = conv_all.transpose(1, 0, 2).reshape(SUBLANES, cw)

    tm = _tile(s, 1024)
    tn = _tile(d, 1024)
    tk = _tile(d, 1024)
    ts = _tile(s, 1024)
    h = rms_fwd(x2, norm_gain, "rms_fwd_x")
    z = matmul(h, wi_all, grid=(s // tm, N_CHIPS, d // tk),
               a_spec=pl.BlockSpec((tm, tk), lambda i, j, k: (i, k)),
               b_spec=pl.BlockSpec((None, tk, sh), lambda i, j, k: (j, k, 0)),
               o_spec=pl.BlockSpec((tm, sh), lambda i, j, k: (i, j)),
               out_shape=jax.ShapeDtypeStruct((s, in_w), F32), dims=NN, name="mm_z")
    tabs = _rope_tables(s)
    qg = jnp.tile(q_norm_gain, (1, LANES // HEAD_DIM))
    kg = jnp.tile(k_norm_gain, (1, LANES // HEAD_DIM))
    seg_i = jnp.arange(SEG) // HEAD_DIM
    segb = (seg_i[:, None] == seg_i[None, :]).astype(BF16)
    qs, ks, vb = qk_prep_fwd(z, tabs, qg, kg, segb, aw)
    attn, mix = attn_fwd(qs, ks, vb, z, attn_sinks, aw, d, ga_off)
    mix = conv_fwd(z, conv_full, mix, aw, cw, b_off)
    sq = lambda shape: dict(
        a_spec=pl.BlockSpec((tm, tk), lambda i, j, k: (i, k)),
        o_spec=pl.BlockSpec((tm, tn), lambda i, j, k: (i, j)),
        out_shape=jax.ShapeDtypeStruct(shape, F32))
    x1 = matmul(mix, wo_full, grid=(s // tm, d // tn, d // tk),
                b_spec=pl.BlockSpec((tk, tn), lambda i, j, k: (k, j)), dims=NN, name="mm_x1",
                res=x2, res_spec=pl.BlockSpec((tm, tn), lambda i, j, k: (i, j)), **sq((s, d)))
    hg = rms_fwd(x1, ple_gate_norm_gain, "rms_fwd_x1")
    gl = matmul(hg, wg_full, grid=(s // tm, d // tn, d // tk),
                b_spec=pl.BlockSpec((tk, tn), lambda i, j, k: (k, j)), dims=NN, name="mm_gl", **sq((s, d)))
    pq = d // N_CHIPS
    pe = matmul(p2, wp_all, grid=(s // tm, N_CHIPS, 1),
                a_spec=pl.BlockSpec((tm, ple), lambda i, j, k: (i, 0)),
                b_spec=pl.BlockSpec((None, ple, pq), lambda i, j, k: (j, 0, 0)),
                o_spec=pl.BlockSpec((tm, pq), lambda i, j, k: (i, j)),
                out_shape=jax.ShapeDtypeStruct((s, d), F32), dims=NN, name="mm_pe")

    d_gl, d_pe, dy, acc_head = head_fwd_bwd(x1, gl, pe, tgt, b_ple_gate, ple_norm_gain)
    d_hg = matmul(d_gl, wg_full, grid=(s // tm, d // tn, d // tk),
                  b_spec=pl.BlockSpec((tn, tk), lambda i, j, k: (j, k)), dims=NT, name="mm_d_hg", **sq((s, d)))
    wgrad = lambda a_cols, shape3, o_spec, b_cols, name, a, b, grid: matmul(
        a, b, grid=grid,
        a_spec=pl.BlockSpec((ts, a_cols), lambda i, j, k: (k, i)),
        b_spec=pl.BlockSpec((ts, b_cols), lambda i, j, k: (k, j)),
        o_spec=o_spec, out_shape=jax.ShapeDtypeStruct(shape3, BF16), dims=TN, name=name)
    g_wg = wgrad(tn, (d, d), pl.BlockSpec((tn, tn), lambda i, j, k: (i, j)), tn, "mm_g_wg", hg, d_gl,
                 (d // tn, d // tn, s // ts))
    g_wp = wgrad(ple, (N_CHIPS, ple, pq), pl.BlockSpec((None, ple, pq), lambda i, j, k: (j, 0, 0)), pq,
                 "mm_g_wp", p2, d_pe, (1, N_CHIPS, s // ts))
    d_x1, d_x1b, acc_g = rms_bwd_add(d_hg, x1, dy, ple_gate_norm_gain, "rms_bwd_x1", True)
    d_mix = matmul(d_x1b, wo_full, grid=(s // tm, d // tn, d // tk),
                   b_spec=pl.BlockSpec((tn, tk), lambda i, j, k: (j, k)), dims=NT, name="mm_d_mix", **sq((s, d)))
    g_wo = wgrad(tn, (d, d), pl.BlockSpec((tn, tn), lambda i, j, k: (i, j)), tn, "mm_g_wo", mix, d_x1b,
                 (d // tn, d // tn, s // ts))
    d_o, dz = gate_bwd(d_mix, attn, z, aw, ga_off)
    dqs, dks, dvs, acc_sink = attn_bwd(qs, ks, vb, d_o, attn_sinks, aw)
    dz, acc_qk = qk_prep_bwd(z, dqs, dks, dvs, tabs, qg, kg, segb, dz, aw)
    dz, acc_conv = conv_bwd(z, d_mix, conv_full, dz, aw, cw, b_off)
    d_h = matmul(dz, wi_all, grid=(s // tm, d // tn, N_CHIPS),
                 a_spec=pl.BlockSpec((tm, sh), lambda i, j, k: (i, k)),
                 b_spec=pl.BlockSpec((None, tn, sh), lambda i, j, k: (k, j, 0)),
                 o_spec=pl.BlockSpec((tm, tn), lambda i, j, k: (i, j)),
                 out_shape=jax.ShapeDtypeStruct((s, d), F32), dims=NT, name="mm_d_h")
    g_wi = wgrad(tn, (N_CHIPS, d, sh), pl.BlockSpec((None, tn, sh), lambda i, j, k: (j, i, 0)), sh,
                 "mm_g_wi", h, dz, (d // tn, N_CHIPS, s // ts))
    grad_x, acc_x = rms_bwd_add(d_h, x2, d_x1, norm_gain, "rms_bwd_x", False)

    grads = [g_wi, g_wo.reshape(N_CHIPS, dq, d), g_wg.reshape(N_CHIPS, dq, d), g_wp]
    got = sibling_swap_halves(grads)
    names = ("wi", "wo", "wg", "wp")
    parts = [add_sibling(g, o, core, "add_sibling_" + nm) for g, o, nm in zip(grads, got, names)]
    owned = scatter_to_owner(parts)
    halves = [sum_chips(pt, o, place, "sum_chips_" + nm) for pt, o, nm in zip(parts, owned, names)]
    full = sibling_join_halves(halves)
    big = {}
    for nm, g, w, m, v in (("w_in", full[0], w_in, m_w_in, v_w_in), ("w_out", full[1], w_out, m_w_out, v_w_out),
                           ("w_ple_gate", full[2], w_ple_gate, m_w_ple_gate, v_w_ple_gate),
                           ("w_ple_proj", full[3], w_ple_proj, m_w_ple_proj, v_w_ple_proj)):
        big[nm] = [o[None] for o in adamw(g, w[0], m[0], v[0], "adamw_" + nm)]

    wsm = max(d, cw)
    misc = jnp.concatenate([acc_qk[0:1, :HEAD_DIM], acc_qk[1:2, :HEAD_DIM], acc_sink[0:1, :nq]], axis=1)
    small = jnp.concatenate([
        _pad_cols(acc_x[0:1], wsm), _pad_cols(acc_g[0:1], wsm), _pad_cols(acc_head[0:1], wsm),
        _pad_cols(acc_head[1:2], wsm), _pad_cols(misc, wsm), _pad_cols(acc_conv[0:3], wsm)], axis=0)
    both = jnp.concatenate([small, _pad_rows(_pad_cols(acc_head[2:3], wsm), SUBLANES)], axis=0)
    tot = sum_devices(allgather_small(both))
    loss = tot[SUBLANES, 0]

    def pack(vals):
        ng, pg, bg, eg, qgv, kgv, sk, cv = vals
        misc_v = jnp.concatenate([qgv, kgv, sk], axis=1)
        return jnp.concatenate([_pad_cols(ng, wsm), _pad_cols(pg, wsm), _pad_cols(bg, wsm), _pad_cols(eg, wsm),
                                _pad_cols(misc_v, wsm), _pad_cols(cv[0], wsm)], axis=0)

    g_small = jnp.concatenate(
        [tot[0:5], _pad_cols(lax.dynamic_slice(tot[5:8], (0, chip * cq), (3, cq)), wsm)], axis=0)
    w_small = pack((norm_gain, ple_gate_norm_gain, b_ple_gate, ple_norm_gain, q_norm_gain, k_norm_gain,
                    attn_sinks, conv_w))
    m_small = pack((m_norm_gain, m_ple_gate_norm_gain, m_b_ple_gate, m_ple_norm_gain, m_q_norm_gain,
                    m_k_norm_gain, m_attn_sinks, m_conv_w))
    v_small = pack((v_norm_gain, v_ple_gate_norm_gain, v_b_ple_gate, v_ple_norm_gain, v_q_norm_gain,
                    v_k_norm_gain, v_attn_sinks, v_conv_w))
    sm = adamw(g_small, w_small, m_small, v_small, "adamw_small")

    def unpack(a):
        return {"norm_gain": a[0:1, :d], "ple_gate_norm_gain": a[1:2, :d], "b_ple_gate": a[2:3, :d],
                "ple_norm_gain": a[3:4, :d], "q_norm_gain": a[4:5, :HEAD_DIM],
                "k_norm_gain": a[4:5, HEAD_DIM:2 * HEAD_DIM],
                "attn_sinks": a[4:5, 2 * HEAD_DIM:2 * HEAD_DIM + nq], "conv_w": a[5:8, :cq][None]}

    order = ("norm_gain", "w_in", "q_norm_gain", "k_norm_gain", "attn_sinks", "conv_w", "w_out",
             "ple_gate_norm_gain", "w_ple_gate", "b_ple_gate", "w_ple_proj", "ple_norm_gain")
    outs = [loss, grad_x[None]]
    for kind in range(4):
        table = unpack(sm[kind])
        for nm in order:
            outs.append(big[nm][kind] if nm in big else table[nm])
    return tuple(outs)
```

```python
import functools

import jax
import jax.numpy as jnp
from jax import lax
from jax.experimental import pallas as pl
from jax.experimental.pallas import tpu as pltpu

F32 = jnp.float32
BF16 = jnp.bfloat16
MESH = pl.DeviceIdType.MESH

HEAD_DIM = 64
N_KV_HEADS = 4
KV_WIDTH = N_KV_HEADS * HEAD_DIM
BLOCK = 128
ROT_DIM = 16
ROPE_THETA = 500000.0
EPS = 1e-6
NEG_INF = -1e30
N_CHIPS = 4
LANES = 128
SUBLANES = 8
COLT = 512
SEG = 256
VMEM_LIMIT = 48 * 1024 * 1024

ADAM_LR = 0.001
ADAM_B1 = 0.9
ADAM_B2 = 0.999
ADAM_EPS = 1e-08
ADAM_WD = 0.01
ADAM_STEP = 10

NN = (((1,), (0,)), ((), ()))
NT = (((1,), (1,)), ((), ()))
TN = (((0,), (0,)), ((), ()))


def _call(body, **kw):
    return pl.pallas_call(body, **kw)


def _params(sem):
    return pltpu.CompilerParams(dimension_semantics=sem, vmem_limit_bytes=VMEM_LIMIT)


def _tile(n, pref):
    return pref if n % pref == 0 else n


def _sigmoid(v):
    return 1.0 / (1.0 + jnp.exp(-v))


def _hbm():
    return pl.BlockSpec(memory_space=pl.ANY)


def cast_into_slot(a, chip, name):
    r, c = a.shape
    tr = _tile(r, 256)

    def body(chip_ref, a_ref, o_ref):
        o_ref[...] = a_ref[...].astype(BF16)

    grid_spec = pltpu.PrefetchScalarGridSpec(
        num_scalar_prefetch=1, grid=(r // tr,),
        in_specs=[pl.BlockSpec((tr, c), lambda i, chip_ref: (i, 0))],
        out_specs=pl.BlockSpec((None, tr, c), lambda i, chip_ref: (chip_ref[0], i, 0)))
    return _call(body, name=name, grid_spec=grid_spec,
                 out_shape=jax.ShapeDtypeStruct((N_CHIPS, r, c), BF16),
                 compiler_params=_params(("parallel",)))(chip, a)


def rms_fwd(x, gain, name):
    s, d = x.shape
    tm = _tile(s, 256)

    def body(x_ref, g_ref, o_ref):
        xf = x_ref[...]
        r = lax.rsqrt(jnp.mean(xf * xf, axis=-1, keepdims=True) + EPS)
        o_ref[...] = ((xf * r) * g_ref[...]).astype(BF16)

    return _call(body, name=name, grid=(s // tm,),
                 in_specs=[pl.BlockSpec((tm, d), lambda i: (i, 0)),
                           pl.BlockSpec((1, d), lambda i: (0, 0))],
                 out_specs=pl.BlockSpec((tm, d), lambda i: (i, 0)),
                 out_shape=jax.ShapeDtypeStruct((s, d), BF16),
                 compiler_params=_params(("parallel",)))(x, gain)


def rms_bwd_add(dyn, xin, add, gain, name, want_bf16):
    s, d = xin.shape
    tm = _tile(s, 256)

    def body(dy_ref, x_ref, a_ref, g_ref, *outs):
        i = pl.program_id(0)
        dx_ref, acc_ref = outs[0], outs[-1]
        xf = x_ref[...]
        r = lax.rsqrt(jnp.mean(xf * xf, axis=-1, keepdims=True) + EPS)
        xhat = xf * r
        dyv = dy_ref[...]
        gd = dyv * g_ref[...]
        dx = a_ref[...] + r * (gd - xhat * jnp.mean(xhat * gd, axis=-1, keepdims=True))
        dx_ref[...] = dx
        if want_bf16:
            outs[1][...] = dx.astype(BF16)

        @pl.when(i == 0)
        def _():
            acc_ref[...] = jnp.zeros_like(acc_ref)

        acc_ref[0:1, :] += jnp.sum(dyv * xhat, axis=0, keepdims=True)

    row = pl.BlockSpec((tm, d), lambda i: (i, 0))
    out_specs = [row] + ([row] if want_bf16 else []) + [pl.BlockSpec((SUBLANES, d), lambda i: (0, 0))]
    out_shape = ([jax.ShapeDtypeStruct((s, d), F32)]
                 + ([jax.ShapeDtypeStruct((s, d), BF16)] if want_bf16 else [])
                 + [jax.ShapeDtypeStruct((SUBLANES, d), F32)])
    return _call(body, name=name, grid=(s // tm,),
                 in_specs=[row, row, row, pl.BlockSpec((1, d), lambda i: (0, 0))],
                 out_specs=out_specs, out_shape=out_shape,
                 compiler_params=_params(("arbitrary",)))(dyn, xin, add, gain)


def head_fwd_bwd(x1, gl, pe, tgt, bias, ple_gain):
    s, d = x1.shape
    tm = _tile(s, 128)

    def body(x1_ref, gl_ref, pe_ref, t_ref, b_ref, g_ref, dgl_ref, dpe_ref, dy_ref, acc_ref):
        i = pl.program_id(0)
        gate = _sigmoid(gl_ref[...] + b_ref[...])
        pev = pe_ref[...]
        r = lax.rsqrt(jnp.mean(pev * pev, axis=-1, keepdims=True) + EPS)
        pehat = pev * r
        gain = g_ref[...]
        e = pehat * gain
        diff = (x1_ref[...] + gate * e) - t_ref[...]
        dy = diff * (1.0 / d)
        dy_ref[...] = dy
        d_gl = (dy * e) * (gate * (1.0 - gate))
        dgl_ref[...] = d_gl.astype(BF16)
        d_e = dy * gate
        gd = d_e * gain
        d_pe = r * (gd - pehat * jnp.mean(pehat * gd, axis=-1, keepdims=True))
        dpe_ref[...] = d_pe.astype(BF16)

        @pl.when(i == 0)
        def _():
            acc_ref[...] = jnp.zeros_like(acc_ref)

        acc_ref[0:1, :] += jnp.sum(d_gl, axis=0, keepdims=True)
        acc_ref[1:2, :] += jnp.sum(d_e * pehat, axis=0, keepdims=True)
        acc_ref[2:3, :] += jnp.sum(diff * diff, axis=0, keepdims=True) * (0.5 / d)

    row = pl.BlockSpec((tm, d), lambda i: (i, 0))
    vec = pl.BlockSpec((1, d), lambda i: (0, 0))
    return _call(body, name="head_fwd_bwd", grid=(s // tm,),
                 in_specs=[row, row, row, row, vec, vec],
                 out_specs=[row, row, row, pl.BlockSpec((SUBLANES, d), lambda i: (0, 0))],
                 out_shape=[jax.ShapeDtypeStruct((s, d), BF16), jax.ShapeDtypeStruct((s, d), BF16),
                            jax.ShapeDtypeStruct((s, d), F32), jax.ShapeDtypeStruct((SUBLANES, d), F32)],
                 compiler_params=_params(("arbitrary",)))(x1, gl, pe, tgt, bias, ple_gain)


def adamw(g, w, m, v, name):
    r, c = g.shape
    tr = _tile(r, 128)

    def body(g_ref, w_ref, m_ref, v_ref, go_ref, d_ref, mo_ref, vo_ref):
        gv = g_ref[...]
        mn = ADAM_B1 * m_ref[...] + (1.0 - ADAM_B1) * gv
        vn = ADAM_B2 * v_ref[...] + (1.0 - ADAM_B2) * (gv * gv)
        m_hat = mn / (1.0 - ADAM_B1 ** ADAM_STEP)
        v_hat = vn / (1.0 - ADAM_B2 ** ADAM_STEP)
        go_ref[...] = gv
        d_ref[...] = -ADAM_LR * (m_hat / (jnp.sqrt(v_hat) + ADAM_EPS) + ADAM_WD * w_ref[...])
        mo_ref[...] = mn
        vo_ref[...] = vn

    blk = pl.BlockSpec((tr, c), lambda i: (i, 0))
    shp = jax.ShapeDtypeStruct((r, c), F32)
    return _call(body, name=name, grid=(r // tr,), in_specs=[blk] * 4, out_specs=[blk] * 4,
                 out_shape=[shp] * 4, compiler_params=_params(("parallel",)))(g, w, m, v)


def matmul(a, b, *, grid, a_spec, b_spec, o_spec, out_shape, dims, name, res=None, res_spec=None):
    nk = grid[2]
    acc_shape = tuple(d for d in o_spec.block_shape if d is not None)

    def body(*refs):
        if res is None:
            a_ref, b_ref, o_ref, acc_ref = refs
        else:
            a_ref, b_ref, r_ref, o_ref, acc_ref = refs
        k = pl.program_id(2)

        @pl.when(k == 0)
        def _():
            acc_ref[...] = jnp.zeros_like(acc_ref)

        acc_ref[...] += lax.dot_general(a_ref[...].astype(BF16), b_ref[...].astype(BF16), dims,
                                        preferred_element_type=F32)

        @pl.when(k == nk - 1)
        def _():
            out = acc_ref[...]
            if res is not None:
                out = r_ref[...] + out
            o_ref[...] = out.astype(o_ref.dtype)

    in_specs = [a_spec, b_spec] + ([res_spec] if res is not None else [])
    args = (a, b) + ((res,) if res is not None else ())
    return _call(body, name=name, grid=grid, in_specs=in_specs, out_specs=o_spec, out_shape=out_shape,
                 scratch_shapes=[pltpu.VMEM(acc_shape, F32)],
                 compiler_params=_params(("parallel", "parallel", "arbitrary")))(*args)


def _seg_mean(sq, segb):
    parts = []
    for cgrp in range(sq.shape[1] // SEG):
        blk = sq[:, cgrp * SEG:(cgrp + 1) * SEG]
        hi = blk.astype(BF16)
        r1 = blk - hi.astype(F32)
        mid = r1.astype(BF16)
        lo = (r1 - mid.astype(F32)).astype(BF16)
        acc = jnp.dot(hi, segb, preferred_element_type=F32)
        acc += jnp.dot(mid, segb, preferred_element_type=F32)
        acc += jnp.dot(lo, segb, preferred_element_type=F32)
        parts.append(acc)
    out = parts[0] if len(parts) == 1 else jnp.concatenate(parts, axis=1)
    return out * (1.0 / HEAD_DIM)


def _rope(v, cos, sa, sb):
    parts = []
    for cgrp in range(v.shape[1] // LANES):
        blk = v[:, cgrp * LANES:(cgrp + 1) * LANES]
        parts.append(blk * cos + pltpu.roll(blk, LANES - 8, 1) * sa + pltpu.roll(blk, 8, 1) * sb)
    return parts[0] if len(parts) == 1 else jnp.concatenate(parts, axis=1)


def _rope_t(dv, cos, sa, sb):
    parts = []
    for cgrp in range(dv.shape[1] // LANES):
        blk = dv[:, cgrp * LANES:(cgrp + 1) * LANES]
        parts.append(blk * cos + pltpu.roll(blk * sa, 8, 1) + pltpu.roll(blk * sb, LANES - 8, 1))
    return parts[0] if len(parts) == 1 else jnp.concatenate(parts, axis=1)


def _tile_lanes(vec, width):
    reps = width // LANES
    return vec if reps == 1 else jnp.tile(vec, (1, reps))


def qk_prep_fwd(z, tabs, qg, kg, segb, aw):
    s = z.shape[0]
    tm = _tile(s, 256)
    wq = aw + 2 * KV_WIDTH

    def body(z_ref, cos_ref, sa_ref, sb_ref, qg_ref, kg_ref, seg_ref, qs_ref, ks_ref, vb_ref):
        zz = z_ref[...]
        q, k, v = zz[:, :aw], zz[:, aw:aw + KV_WIDTH], zz[:, aw + KV_WIDTH:]
        cos, sa, sb, segm = cos_ref[...], sa_ref[...], sb_ref[...], seg_ref[...]
        rq = lax.rsqrt(_seg_mean(q * q, segm) + EPS)
        qn = (q * rq) * _tile_lanes(qg_ref[...], aw)
        qs_ref[...] = (_rope(qn, cos, sa, sb) * (HEAD_DIM ** -0.5)).astype(BF16)
        rk = lax.rsqrt(_seg_mean(k * k, segm) + EPS)
        kn = (k * rk) * _tile_lanes(kg_ref[...], KV_WIDTH)
        ks_ref[...] = _rope(kn, cos, sa, sb).astype(BF16)
        vb_ref[...] = v.astype(BF16)

    tab = pl.BlockSpec((tm, LANES), lambda i: (i, 0))
    vec = pl.BlockSpec((1, LANES), lambda i: (0, 0))
    return _call(body, name="qk_prep_fwd", grid=(s // tm,),
                 in_specs=[pl.BlockSpec((tm, wq), lambda i: (i, 0)), tab, tab, tab, vec, vec,
                           pl.BlockSpec((SEG, SEG), lambda i: (0, 0))],
                 out_specs=[pl.BlockSpec((tm, aw), lambda i: (i, 0)),
                            pl.BlockSpec((tm, KV_WIDTH), lambda i: (i, 0)),
                            pl.BlockSpec((tm, KV_WIDTH), lambda i: (i, 0))],
                 out_shape=[jax.ShapeDtypeStruct((s, aw), BF16), jax.ShapeDtypeStruct((s, KV_WIDTH), BF16),
                            jax.ShapeDtypeStruct((s, KV_WIDTH), BF16)],
                 compiler_params=_params(("parallel",)))(z, *tabs, qg, kg, segb)


def qk_prep_bwd(z, dqs, dks, dvs, tabs, qg, kg, segb, dz, aw):
    s = z.shape[0]
    tm = _tile(s, 256)
    wq = aw + 2 * KV_WIDTH

    def body(z_ref, dq_ref, dk_ref, dv_ref, cos_ref, sa_ref, sb_ref, qg_ref, kg_ref, seg_ref, dz_in,
             dz_ref, acc_ref):
        i = pl.program_id(0)
        zz = z_ref[...]
        q, k = zz[:, :aw], zz[:, aw:aw + KV_WIDTH]
        cos, sa, sb, segm = cos_ref[...], sa_ref[...], sb_ref[...], seg_ref[...]

        def one(xv, dout, gvec, width):
            g = _tile_lanes(gvec, width)
            r = lax.rsqrt(_seg_mean(xv * xv, segm) + EPS)
            xhat = xv * r
            dn = _rope_t(dout, cos, sa, sb)
            gd = dn * g
            dx = r * (gd - xhat * _seg_mean(xhat * gd, segm))
            contrib = jnp.sum(dn * xhat, axis=0, keepdims=True)
            folded = contrib[:, :LANES]
            for cgrp in range(1, width // LANES):
                folded = folded + contrib[:, cgrp * LANES:(cgrp + 1) * LANES]
            return dx, folded + pltpu.roll(folded, HEAD_DIM, 1)

        dq, gq = one(q, dq_ref[...] * (HEAD_DIM ** -0.5), qg_ref[...], aw)
        dk, gk = one(k, dk_ref[...], kg_ref[...], KV_WIDTH)
        dz_ref[:, :aw] = dq.astype(BF16)
        dz_ref[:, aw:aw + KV_WIDTH] = dk.astype(BF16)
        dz_ref[:, aw + KV_WIDTH:] = dv_ref[...].astype(BF16)

        @pl.when(i == 0)
        def _():
            acc_ref[...] = jnp.zeros_like(acc_ref)

        acc_ref[0:1, :] += gq
        acc_ref[1:2, :] += gk

    tab = pl.BlockSpec((tm, LANES), lambda i: (i, 0))
    vec = pl.BlockSpec((1, LANES), lambda i: (0, 0))
    kvb = pl.BlockSpec((tm, KV_WIDTH), lambda i: (i, 0))
    return _call(body, name="qk_prep_bwd", grid=(s // tm,),
                 in_specs=[pl.BlockSpec((tm, wq), lambda i: (i, 0)),
                           pl.BlockSpec((tm, aw), lambda i: (i, 0)), kvb, kvb, tab, tab, tab, vec, vec,
                           pl.BlockSpec((SEG, SEG), lambda i: (0, 0)), _hbm()],
                 out_specs=[pl.BlockSpec((tm, wq), lambda i: (i, 0)),
                            pl.BlockSpec((SUBLANES, LANES), lambda i: (0, 0))],
                 out_shape=[jax.ShapeDtypeStruct(dz.shape, BF16), jax.ShapeDtypeStruct((SUBLANES, LANES), F32)],
                 input_output_aliases={10: 0},
                 compiler_params=_params(("arbitrary",)))(z, dqs, dks, dvs, *tabs, qg, kg, segb, dz)


def _placed(band, lane_idx):
    out = {}
    for kh in range(N_KV_HEADS):
        grp = band[:, (kh // 2) * LANES:(kh // 2 + 1) * LANES]
        for half in (0, 1):
            t = grp if half == kh % 2 else pltpu.roll(grp, HEAD_DIM, 1)
            keep = (lane_idx >= half * HEAD_DIM) & (lane_idx < (half + 1) * HEAD_DIM)
            out[kh, half] = jnp.where(keep, t, jnp.zeros_like(t))
    return out


def _scores(qgrp, kpl, valid, sink):
    sc = lax.dot_general(qgrp, kpl, NT, preferred_element_type=F32)
    sc = jnp.where(valid, sc, NEG_INF)
    m = jnp.maximum(jnp.max(sc, axis=-1, keepdims=True), sink)
    e = jnp.exp(sc - m)
    es = jnp.exp(sink - m)
    den = jnp.sum(e, axis=-1, keepdims=True) + es
    return e / den, es / den


def _valid_mask(n):
    r_i = lax.broadcasted_iota(jnp.int32, (BLOCK, 2 * BLOCK), 0)
    j_i = lax.broadcasted_iota(jnp.int32, (BLOCK, 2 * BLOCK), 1)
    return (j_i > r_i) & (j_i <= r_i + BLOCK) & ((n > 0) | (j_i >= BLOCK))


def attn_fwd(qs, ks, vb, z, sinks, aw, d, ga_off):
    s = qs.shape[0]
    nb = s // BLOCK
    nq = aw // HEAD_DIM
    grp_sz = nq // N_KV_HEADS
    n_ga = aw // COLT

    def body(q_ref, ko_ref, kp_ref, vo_ref, vp_ref, *rest):
        ga_refs = rest[:n_ga]
        sink_ref, attn_ref, mix_ref = rest[n_ga:]
        n = pl.program_id(0)
        lane_idx = lax.broadcasted_iota(jnp.int32, (2 * BLOCK, LANES), 1)
        kpl = _placed(jnp.concatenate([kp_ref[...], ko_ref[...]], axis=0), lane_idx)
        vpl = _placed(jnp.concatenate([vp_ref[...], vo_ref[...]], axis=0), lane_idx)
        valid = _valid_mask(n)
        for cgrp in range(nq // 2):
            qgrp = q_ref[:, cgrp * LANES:(cgrp + 1) * LANES]
            acc = jnp.zeros((BLOCK, LANES), F32)
            for half in (0, 1):
                h = 2 * cgrp + half
                kh = h // grp_sz
                p, _ = _scores(qgrp, kpl[kh, half], valid, sink_ref[0, h])
                acc += jnp.dot(p.astype(BF16), vpl[kh, half], preferred_element_type=F32)
            attn_ref[:, cgrp * LANES:(cgrp + 1) * LANES] = acc
            col = cgrp * LANES
            ga = ga_refs[col // COLT][:, col % COLT:col % COLT + LANES]
            mix_ref[:, cgrp * LANES:(cgrp + 1) * LANES] = (acc * (ga * _sigmoid(ga))).astype(BF16)

    own = lambda n: (n, 0)
    prev = lambda n: (jnp.maximum(n - 1, 0), 0)
    kvs = lambda imap: pl.BlockSpec((BLOCK, KV_WIDTH), imap)
    ga_specs = [pl.BlockSpec((BLOCK, COLT), functools.partial(lambda n, j: (n, ga_off + j), j=j))
                for j in range(n_ga)]
    return _call(body, name="attn_fwd", grid=(nb,),
                 in_specs=[pl.BlockSpec((BLOCK, aw), own), kvs(own), kvs(prev), kvs(own), kvs(prev)]
                 + ga_specs + [pl.BlockSpec(memory_space=pltpu.SMEM)],
                 out_specs=[pl.BlockSpec((BLOCK, aw), own), pl.BlockSpec((BLOCK, aw), own)],
                 out_shape=[jax.ShapeDtypeStruct((s, aw), F32), jax.ShapeDtypeStruct((s, d), BF16)],
                 compiler_params=_params(("parallel",)))(qs, ks, ks, vb, vb, *([z] * n_ga), sinks)


def gate_bwd(d_mix, attn, z, aw, ga_off):
    s, in_w = z.shape
    tm = _tile(s, 256)

    def body(dm_ref, at_ref, ga_ref, do_ref, dz_ref):
        ga = ga_ref[...]
        sig = _sigmoid(ga)
        dm = dm_ref[...]
        do_ref[...] = (dm * (ga * sig)).astype(BF16)
        dz_ref[...] = ((dm * at_ref[...]) * (sig * (1.0 + ga * (1.0 - sig)))).astype(BF16)

    blk = pl.BlockSpec((tm, COLT), lambda i, j: (i, j))
    gab = pl.BlockSpec((tm, COLT), lambda i, j: (i, ga_off + j))
    return _call(body, name="gate_bwd", grid=(s // tm, aw // COLT),
                 in_specs=[blk, blk, gab], out_specs=[blk, gab],
                 out_shape=[jax.ShapeDtypeStruct((s, aw), BF16), jax.ShapeDtypeStruct((s, in_w), BF16)],
                 compiler_params=_params(("parallel", "parallel")))(d_mix, attn, z)


def attn_bwd(qs, ks, vb, d_o, sinks, aw):
    s = qs.shape[0]
    nb = s // BLOCK
    nq = aw // HEAD_DIM
    grp_sz = nq // N_KV_HEADS

    def body(q_ref, ko_ref, kp_ref, vo_ref, vp_ref, do_ref, sink_ref, dq_ref, dk_ref, dv_ref, ds_ref,
             ck_ref, cv_ref):
        n = pl.program_id(0)

        @pl.when(n == 0)
        def _():
            ds_ref[...] = jnp.zeros_like(ds_ref)
            dk_ref[...] = jnp.zeros_like(dk_ref)
            dv_ref[...] = jnp.zeros_like(dv_ref)

        @pl.when(n < nb)
        def _():
            lane_idx = lax.broadcasted_iota(jnp.int32, (2 * BLOCK, LANES), 1)
            lane_row = lax.broadcasted_iota(jnp.int32, (1, LANES), 1)
            kpl = _placed(jnp.concatenate([kp_ref[...], ko_ref[...]], axis=0), lane_idx)
            vpl = _placed(jnp.concatenate([vp_ref[...], vo_ref[...]], axis=0), lane_idx)
            valid = _valid_mask(n)
            dk_acc = [jnp.zeros((2 * BLOCK, LANES), F32) for _ in range(N_KV_HEADS)]
            dv_acc = [jnp.zeros((2 * BLOCK, LANES), F32) for _ in range(N_KV_HEADS)]
            ds_row = jnp.zeros((1, LANES), F32)
            for cgrp in range(nq // 2):
                qgrp = q_ref[:, cgrp * LANES:(cgrp + 1) * LANES]
                dog = do_ref[:, cgrp * LANES:(cgrp + 1) * LANES]
                dq_acc = jnp.zeros((BLOCK, LANES), F32)
                for half in (0, 1):
                    h = 2 * cgrp + half
                    kh = h // grp_sz
                    p, p_sink = _scores(qgrp, kpl[kh, half], valid, sink_ref[0, h])
                    dp = lax.dot_general(dog, vpl[kh, half], NT, preferred_element_type=F32)
                    delta = jnp.sum(p * dp, axis=-1, keepdims=True)
                    dsb = (p * (dp - delta)).astype(BF16)
                    dq_acc += jnp.dot(dsb, kpl[kh, half], preferred_element_type=F32)
                    keep = (lane_idx >= half * HEAD_DIM) & (lane_idx < (half + 1) * HEAD_DIM)
                    dkp = jnp.where(keep, lax.dot_general(dsb, qgrp, TN, preferred_element_type=F32), 0.0)
                    dvp = jnp.where(keep, lax.dot_general(p.astype(BF16), dog, TN, preferred_element_type=F32), 0.0)
                    if half != kh % 2:
                        dkp = pltpu.roll(dkp, HEAD_DIM, 1)
                        dvp = pltpu.roll(dvp, HEAD_DIM, 1)
                    dk_acc[kh] += dkp
                    dv_acc[kh] += dvp
                    dsink = -jnp.sum(p_sink * delta, axis=0, keepdims=True)
                    ds_row += jnp.where(lane_row == h, dsink, 0.0)
                dq_ref[:, cgrp * LANES:(cgrp + 1) * LANES] = dq_acc
            ds_ref[0:1, :] += ds_row
            dk_band = jnp.concatenate([dk_acc[0] + dk_acc[1], dk_acc[2] + dk_acc[3]], axis=1)
            dv_band = jnp.concatenate([dv_acc[0] + dv_acc[1], dv_acc[2] + dv_acc[3]], axis=1)

            @pl.when(n > 0)
            def _():
                dk_ref[...] = ck_ref[...] + dk_band[:BLOCK]
                dv_ref[...] = cv_ref[...] + dv_band[:BLOCK]

            ck_ref[...] = dk_band[BLOCK:]
            cv_ref[...] = dv_band[BLOCK:]

        @pl.when(n == nb)
        def _():
            dk_ref[...] = ck_ref[...]
            dv_ref[...] = cv_ref[...]

    own = lambda n: (jnp.minimum(n, nb - 1), 0)
    prev = lambda n: (jnp.clip(n - 1, 0, nb - 1), 0)
    done = lambda n: (jnp.maximum(n - 1, 0), 0)
    kvs = lambda imap: pl.BlockSpec((BLOCK, KV_WIDTH), imap)
    return _call(body, name="attn_bwd", grid=(nb + 1,),
                 in_specs=[pl.BlockSpec((BLOCK, aw), own), kvs(own), kvs(prev), kvs(own), kvs(prev),
                           pl.BlockSpec((BLOCK, aw), own), pl.BlockSpec(memory_space=pltpu.SMEM)],
                 out_specs=[pl.BlockSpec((BLOCK, aw), own), kvs(done), kvs(done),
                            pl.BlockSpec((SUBLANES, LANES), lambda n: (0, 0))],
                 out_shape=[jax.ShapeDtypeStruct((s, aw), F32), jax.ShapeDtypeStruct((s, KV_WIDTH), F32),
                            jax.ShapeDtypeStruct((s, KV_WIDTH), F32), jax.ShapeDtypeStruct((SUBLANES, LANES), F32)],
                 scratch_shapes=[pltpu.VMEM((BLOCK, KV_WIDTH), F32), pltpu.VMEM((BLOCK, KV_WIDTH), F32)],
                 compiler_params=_params(("arbitrary",)))(qs, ks, ks, vb, vb, d_o, sinks)


def conv_fwd(z, conv_w, mix, aw, cw, b_off):
    s = z.shape[0]
    tm = _tile(s, 256)
    nseg = cw // COLT
    hb = tm // SUBLANES

    def body(b_ref, c_ref, h_ref, g_ref, cp_ref, hp_ref, w_ref, mix_in, mix_ref):
        i = pl.program_id(0)
        u = c_ref[...] * h_ref[...]
        up = jnp.where(i > 0, cp_ref[...] * hp_ref[...], 0.0)
        ext = jnp.concatenate([up, u], axis=0)
        um1 = pltpu.roll(ext, 1, 0)[SUBLANES:]
        um2 = pltpu.roll(ext, 2, 0)[SUBLANES:]
        w = w_ref[...]
        cv = w[0:1] * um2 + w[1:2] * um1 + w[2:3] * u
        g = g_ref[...]
        mix_ref[...] = ((b_ref[...] * cv) * (g * _sigmoid(g))).astype(BF16)

    seg = lambda k: pl.BlockSpec((tm, COLT), functools.partial(lambda i, j, k: (i, b_off + k * nseg + j), k=k))
    halo = lambda k: pl.BlockSpec(
        (SUBLANES, COLT), functools.partial(lambda i, j, k: (jnp.maximum(i * hb - 1, 0), b_off + k * nseg + j), k=k))
    return _call(body, name="conv_fwd", grid=(s // tm, nseg),
                 in_specs=[seg(0), seg(1), seg(2), seg(3), halo(1), halo(2),
                           pl.BlockSpec((SUBLANES, COLT), lambda i, j: (0, j)), _hbm()],
                 out_specs=pl.BlockSpec((tm, COLT), lambda i, j: (i, aw // COLT + j)),
                 out_shape=jax.ShapeDtypeStruct(mix.shape, BF16),
                 input_output_aliases={7: 0},
                 compiler_params=_params(("parallel", "parallel")))(z, z, z, z, z, z, conv_w, mix)


def conv_bwd(z, d_mix, conv_w, dz, aw, cw, b_off):
    s = z.shape[0]
    tm = _tile(s, 256)
    nseg = cw // COLT
    hb = tm // SUBLANES
    n_row = s // tm
    last_h = s // SUBLANES - 1

    def body(b_ref, c_ref, h_ref, g_ref, cp_ref, hp_ref, bn_ref, gn_ref, dm_ref, dmn_ref, w_ref, dz_in,
             dz_ref, acc_ref, stash_ref):
        i = pl.program_id(1)
        k = pl.program_id(2)

        @pl.when(k == 0)
        def _():
            cc, hh, bb, g = c_ref[...], h_ref[...], b_ref[...], g_ref[...]
            w = w_ref[...]
            u = cc * hh
            up = jnp.where(i > 0, cp_ref[...] * hp_ref[...], 0.0)
            ext = jnp.concatenate([up, u], axis=0)
            um1 = pltpu.roll(ext, 1, 0)[SUBLANES:]
            um2 = pltpu.roll(ext, 2, 0)[SUBLANES:]
            cv = w[0:1] * um2 + w[1:2] * um1 + w[2:3] * u
            sig = _sigmoid(g)
            sg = g * sig
            dm = dm_ref[...]
            d_cv = (dm * bb) * sg
            gn = gn_ref[...]
            d_cv_next = jnp.where(i < n_row - 1, (dmn_ref[...] * bn_ref[...]) * (gn * _sigmoid(gn)), 0.0)
            ext2 = jnp.concatenate([d_cv, d_cv_next], axis=0)
            dp1 = pltpu.roll(ext2, tm + SUBLANES - 1, 0)[:tm]
            dp2 = pltpu.roll(ext2, tm + SUBLANES - 2, 0)[:tm]
            d_u = w[2:3] * d_cv + w[1:2] * dp1 + w[0:1] * dp2
            stash_ref[0] = ((dm * cv) * sg).astype(BF16)
            stash_ref[1] = (d_u * hh).astype(BF16)
            stash_ref[2] = (d_u * cc).astype(BF16)
            stash_ref[3] = (((dm * bb) * cv) * (sig * (1.0 + g * (1.0 - sig)))).astype(BF16)

            @pl.when(i == 0)
            def _():
                acc_ref[...] = jnp.zeros_like(acc_ref)

            acc_ref[0:1, :] += jnp.sum(d_cv * um2, axis=0, keepdims=True)
            acc_ref[1:2, :] += jnp.sum(d_cv * um1, axis=0, keepdims=True)
            acc_ref[2:3, :] += jnp.sum(d_cv * u, axis=0, keepdims=True)

        dz_ref[...] = stash_ref[k]

    seg = lambda q: pl.BlockSpec((tm, COLT), functools.partial(lambda j, i, k, q: (i, b_off + q * nseg + j), q=q))
    halo_p = lambda q: pl.BlockSpec(
        (SUBLANES, COLT),
        functools.partial(lambda j, i, k, q: (jnp.maximum(i * hb - 1, 0), b_off + q * nseg + j), q=q))
    halo_n = lambda q: pl.BlockSpec(
        (SUBLANES, COLT),
        functools.partial(lambda j, i, k, q: (jnp.minimum((i + 1) * hb, last_h), b_off + q * nseg + j), q=q))
    return _call(body, name="conv_bwd", grid=(nseg, n_row, 4),
                 in_specs=[seg(0), seg(1), seg(2), seg(3), halo_p(1), halo_p(2), halo_n(0), halo_n(3),
                           pl.BlockSpec((tm, COLT), lambda j, i, k: (i, aw // COLT + j)),
                           pl.BlockSpec((SUBLANES, COLT),
                                        lambda j, i, k: (jnp.minimum((i + 1) * hb, last_h), aw // COLT + j)),
                           pl.BlockSpec((SUBLANES, COLT), lambda j, i, k: (0, j)), _hbm()],
                 out_specs=[pl.BlockSpec((tm, COLT), lambda j, i, k: (i, b_off + k * nseg + j)),
                            pl.BlockSpec((SUBLANES, COLT), lambda j, i, k: (0, j))],
                 out_shape=[jax.ShapeDtypeStruct(dz.shape, BF16), jax.ShapeDtypeStruct((SUBLANES, cw), F32)],
                 input_output_aliases={11: 0},
                 scratch_shapes=[pltpu.VMEM((4, tm, COLT), BF16)],
                 compiler_params=_params(("arbitrary", "arbitrary", "arbitrary")))(
                     z, z, z, z, z, z, z, z, d_mix, d_mix, conv_w, dz)


def _place():
    x, y, c = lax.axis_index("x"), lax.axis_index("y"), lax.axis_index("c")
    chips = [(1 - x, y), (x, 1 - y), (1 - x, 1 - y)]
    return x, y, c, chips


def _remote(src, dst, send_sem, recv_sem, device):
    return pltpu.make_async_remote_copy(src_ref=src, dst_ref=dst, send_sem=send_sem, recv_sem=recv_sem,
                                        device_id=device, device_id_type=MESH)


def plan_gather_ici(n_split):
    def plan(refs, send, recv):
        x, y, c, chips = _place()
        me = 2 * x + y
        mine, theirs = [], []
        for w, ref in enumerate(refs):
            for j, (cx, cy) in enumerate(chips):
                def part(slot):
                    if w >= n_split:
                        return ref.at[slot]
                    hr = ref.shape[1] // 2
                    return ref.at[slot, pl.ds(c * hr, hr)]
                k = 3 * w + j
                mine.append(_remote(part(me), part(me), send.at[k], recv.at[k], (cx, cy, c)))
                theirs.append(_remote(part(2 * cx + cy), part(2 * cx + cy), send.at[k], recv.at[k], (cx, cy, c)))
        return mine, theirs
    return plan


def plan_gather_pass(refs, send, recv):
    x, y, c, chips = _place()
    mine, theirs = [], []
    for w, ref in enumerate(refs):
        hr = ref.shape[1] // 2
        for j, (cx, cy) in enumerate(chips):
            half = lambda core: ref.at[2 * cx + cy, pl.ds(core * hr, hr)]
            k = 3 * w + j
            mine.append(_remote(half(c), half(c), send.at[k], recv.at[k], (x, y, 1 - c)))
            theirs.append(_remote(half(1 - c), half(1 - c), send.at[k], recv.at[k], (x, y, 1 - c)))
    return mine, theirs


def plan_swap(refs, send, recv):
    x, y, c, _ = _place()
    nw = len(refs) // 2
    mine = []
    for w in range(nw):
        hr = refs[w].shape[1] // 2
        mine.append(_remote(refs[w].at[:, pl.ds((1 - c) * hr, hr), :], refs[nw + w], send.at[w], recv.at[w],
                            (x, y, 1 - c)))
    return mine, mine


def plan_scatter(refs, send, recv):
    x, y, c, chips = _place()
    me = 2 * x + y
    nw = len(refs) // 2
    mine, theirs = [], []
    for w in range(nw):
        for j, (cx, cy) in enumerate(chips):
            k = 3 * w + j
            mine.append(_remote(refs[w].at[2 * cx + cy], refs[nw + w].at[me], send.at[k], recv.at[k], (cx, cy, c)))
            theirs.append(_remote(refs[w].at[me], refs[nw + w].at[2 * cx + cy], send.at[k], recv.at[k], (cx, cy, c)))
    return mine, theirs


def plan_join(refs, send, recv):
    x, y, c, _ = _place()
    mine, theirs = [], []
    for w, ref in enumerate(refs):
        hr = ref.shape[0] // 2
        half = lambda core: ref.at[pl.ds(core * hr, hr)]
        mine.append(_remote(half(c), half(c), send.at[w], recv.at[w], (x, y, 1 - c)))
        theirs.append(_remote(half(1 - c), half(1 - c), send.at[w], recv.at[w], (x, y, 1 - c)))
    return mine, theirs


def exchange(name, plan, arrays, n):
    na = len(arrays)

    def body(*refs):
        mine, theirs = plan(refs[:na], refs[2 * na], refs[2 * na + 1])
        for cp in mine:
            cp.start()
        for cp in theirs:
            cp.wait_recv()
        for cp in mine:
            cp.wait_send()

    return _call(body, name=name, in_specs=[_hbm()] * na, out_specs=[_hbm()] * na,
                 out_shape=[jax.ShapeDtypeStruct(a.shape, a.dtype) for a in arrays],
                 input_output_aliases={i: i for i in range(na)},
                 scratch_shapes=[pltpu.SemaphoreType.DMA((n,)), pltpu.SemaphoreType.DMA((n,))])(*arrays)


def exchange_start(name, plan, arrays, n):
    na = len(arrays)

    def body(*refs):
        mine, _ = plan(refs[:na], refs[na], refs[na + 1])
        for cp in mine:
            cp.start()
        refs[-1][...] = jnp.zeros_like(refs[-1])

    hbm = pl.BlockSpec(memory_space=pltpu.HBM)
    sem = pl.BlockSpec(memory_space=pltpu.SEMAPHORE)
    return _call(body, name=name, in_specs=[hbm] * na,
                 out_specs=[sem, sem] + [hbm] * na + [pl.BlockSpec(memory_space=pltpu.VMEM)],
                 out_shape=[pltpu.SemaphoreType.DMA((n,)), pltpu.SemaphoreType.DMA((n,))]
                 + [pltpu.HBM(a.shape, a.dtype) for a in arrays] + [jax.ShapeDtypeStruct((SUBLANES, LANES), F32)],
                 input_output_aliases={i: i + 2 for i in range(na)},
                 compiler_params=pltpu.CompilerParams(
                     has_side_effects=pltpu.SideEffectType.DATAFLOW_SIDE_EFFECTING))(
                         *[pltpu.with_memory_space_constraint(a, pltpu.HBM) for a in arrays])


def exchange_wait(name, plan, handle, after):
    send_sems, recv_sems, arrays = handle[0], handle[1], handle[2:-1]
    na = len(arrays)

    def body(*refs):
        mine, theirs = plan(refs[:na], refs[na], refs[na + 1])
        for cp in mine:
            cp.wait_send()
        for cp in theirs:
            cp.wait_recv()

    hbm = pl.BlockSpec(memory_space=pltpu.HBM)
    sem = pl.BlockSpec(memory_space=pltpu.SEMAPHORE)
    return _call(body, name=name, in_specs=[hbm] * na + [sem, sem, _hbm()], out_specs=[hbm] * na,
                 out_shape=[pltpu.HBM(a.shape, a.dtype) for a in arrays],
                 input_output_aliases={i: i for i in range(na)},
                 compiler_params=pltpu.CompilerParams(
                     has_side_effects=pltpu.SideEffectType.DATAFLOW_SIDE_EFFECTING))(
                         *arrays, send_sems, recv_sems, after)


def tie(value, token):
    return lax.optimization_barrier((value, token))[0]


def allgather_small(small):
    rows, width = small.shape

    def body(in_ref, out_ref, send, recv, loc):
        x, y, c, _ = _place()
        me = 4 * x + 2 * y + c
        local = pltpu.make_async_copy(in_ref, out_ref.at[me], loc)
        local.start()
        flips = [(fx, fy, fc) for fx in (0, 1) for fy in (0, 1) for fc in (0, 1)][1:]

        def cp(k, slot):
            fx, fy, fc = flips[k]
            return pltpu.make_async_remote_copy(
                src_ref=in_ref, dst_ref=out_ref.at[slot], send_sem=send.at[k], recv_sem=recv.at[k],
                device_id=(x ^ fx, y ^ fy, c ^ fc), device_id_type=MESH)

        for k in range(7):
            cp(k, me).start()
        for k in range(7):
            fx, fy, fc = flips[k]
            cp(k, 4 * (x ^ fx) + 2 * (y ^ fy) + (c ^ fc)).wait_recv()
        for k in range(7):
            cp(k, me).wait_send()
        local.wait()

    return _call(body, name="allgather_small",
                 in_specs=[pl.BlockSpec(memory_space=pltpu.VMEM)],
                 out_specs=pl.BlockSpec(memory_space=pltpu.VMEM),
                 out_shape=jax.ShapeDtypeStruct((8, rows, width), F32),
                 scratch_shapes=[pltpu.SemaphoreType.DMA((7,)), pltpu.SemaphoreType.DMA((7,)),
                                 pltpu.SemaphoreType.DMA])(small)


def add_sibling(grad, got, core, name):
    _, r, c = grad.shape
    hr = r // 2
    tr = _tile(hr, 128)
    nblk = hr // tr

    def body(core_ref, g_ref, o_ref, out_ref):
        out_ref[...] = (g_ref[...].astype(F32) + o_ref[...].astype(F32)).astype(BF16)

    grid_spec = pltpu.PrefetchScalarGridSpec(
        num_scalar_prefetch=1, grid=(N_CHIPS, nblk),
        in_specs=[pl.BlockSpec((None, tr, c), lambda t, i, core_ref: (t, core_ref[0] * nblk + i, 0)),
                  pl.BlockSpec((None, tr, c), lambda t, i, core_ref: (t, i, 0))],
        out_specs=pl.BlockSpec((None, tr, c), lambda t, i, core_ref: (t, i, 0)))
    return _call(body, name=name, grid_spec=grid_spec,
                 out_shape=jax.ShapeDtypeStruct((N_CHIPS, hr, c), BF16),
                 compiler_params=_params(("parallel", "parallel")))(core, grad, got)


def sum_chips(mine, owned, place, name):
    _, hr, c = mine.shape
    tr = _tile(hr, 128)
    nblk = hr // tr

    def body(place_ref, m_ref, o1_ref, o2_ref, o3_ref, out_ref):
        acc = m_ref[...].astype(F32)
        for o_ref in (o1_ref, o2_ref, o3_ref):
            acc = acc + o_ref[...].astype(F32)
        out_ref[...] = acc

    other = lambda k: pl.BlockSpec((None, tr, c), lambda i, place_ref: ((place_ref[0] + k) % N_CHIPS, i, 0))
    grid_spec = pltpu.PrefetchScalarGridSpec(
        num_scalar_prefetch=1, grid=(nblk,),
        in_specs=[other(0), other(1), other(2), other(3)],
        out_specs=pl.BlockSpec((tr, c), lambda i, place_ref: (place_ref[1] * nblk + i, 0)))
    return _call(body, name=name, grid_spec=grid_spec,
                 out_shape=jax.ShapeDtypeStruct((2 * hr, c), F32),
                 compiler_params=_params(("parallel",)))(place, mine, owned, owned, owned)


def sum_devices(gathered):
    _, rows, width = gathered.shape

    def body(g_ref, out_ref):
        acc = g_ref[0]
        for dev in range(1, 8):
            acc = acc + g_ref[dev]
        out_ref[...] = acc
        tail = acc[SUBLANES:]
        out_ref[SUBLANES:, :] = jnp.broadcast_to(jnp.sum(tail, axis=1, keepdims=True), tail.shape)

    return _call(body, name="sum_devices",
                 in_specs=[pl.BlockSpec(memory_space=pltpu.VMEM)],
                 out_specs=pl.BlockSpec(memory_space=pltpu.VMEM),
                 out_shape=jax.ShapeDtypeStruct((rows, width), F32))(gathered)


def _rope_tables(s):
    half = ROT_DIM // 2
    inv_freq = jnp.power(jnp.float32(ROPE_THETA), -jnp.arange(half, dtype=F32) * 2.0 / ROT_DIM)
    ang = jnp.arange(s).astype(F32)[:, None] * inv_freq[None, :]
    cos, sin = jnp.cos(ang), jnp.sin(ang)
    zeros = lambda n: jnp.zeros((s, n), F32)
    cos64 = jnp.concatenate([cos, cos, jnp.ones((s, HEAD_DIM - ROT_DIM), F32)], axis=1)
    sa64 = jnp.concatenate([-sin, zeros(HEAD_DIM - half)], axis=1)
    sb64 = jnp.concatenate([zeros(half), sin, zeros(HEAD_DIM - ROT_DIM)], axis=1)
    return tuple(jnp.concatenate([t, t], axis=1) for t in (cos64, sa64, sb64))


def _pad_rows(a, rows):
    return jnp.pad(a, ((0, rows - a.shape[0]), (0, 0)))


def _pad_cols(a, cols):
    return jnp.pad(a, ((0, 0), (0, cols - a.shape[1])))


def kernel(x, p, norm_gain, w_in, q_norm_gain, k_norm_gain, attn_sinks, conv_w, w_out, ple_gate_norm_gain, w_ple_gate, b_ple_gate, w_ple_proj, ple_norm_gain, loss_target, m_norm_gain, m_w_in, m_q_norm_gain, m_k_norm_gain, m_attn_sinks, m_conv_w, m_w_out, m_ple_gate_norm_gain, m_w_ple_gate, m_b_ple_gate, m_w_ple_proj, m_ple_norm_gain, v_norm_gain, v_w_in, v_q_norm_gain, v_k_norm_gain, v_attn_sinks, v_conv_w, v_w_out, v_ple_gate_norm_gain, v_w_ple_gate, v_b_ple_gate, v_w_ple_proj, v_ple_norm_gain):
    x2, p2, tgt = x[0], p[0, 0], loss_target[0]
    s, d = x2.shape
    ple = p2.shape[1]
    aw = d // 2
    cw = d - aw
    nq = aw // HEAD_DIM
    sh = w_in.shape[2]
    in_w = N_CHIPS * sh
    dq = d // N_CHIPS
    cq = cw // N_CHIPS
    ga_off = (aw + 2 * KV_WIDTH) // COLT
    b_off = (2 * aw + 2 * KV_WIDTH) // COLT
    assert in_w == 2 * aw + 2 * KV_WIDTH + 4 * cw and aw % COLT == 0 and cw % COLT == 0
    assert s % BLOCK == 0 and sh % LANES == 0 and nq % (2 * N_KV_HEADS) == 0

    core = lax.axis_index("c").astype(jnp.int32).reshape(1)
    chip = 2 * lax.axis_index("x") + lax.axis_index("y")

    chip1 = chip.astype(jnp.int32).reshape(1)
    place = jnp.concatenate([chip1, core])
    bufs = [cast_into_slot(w_in[0], chip1, "cast_w_in"), cast_into_slot(w_out[0], chip1, "cast_w_out"),
            cast_into_slot(w_ple_gate[0], chip1, "cast_w_pg"), cast_into_slot(w_ple_proj[0], chip1, "cast_w_pp")]
    conv_buf = lax.dynamic_update_slice(jnp.zeros((N_CHIPS, SUBLANES, cq), F32),
                                        _pad_rows(conv_w[0], SUBLANES)[None], (chip, 0, 0))
    ici_wi = exchange_start("gather_wi_start", plan_gather_ici(1), bufs[:1], 3)
    ici_rest = exchange_start("gather_rest_start", plan_gather_ici(3), bufs[1:] + [conv_buf], 12)

    tm = _tile(s, 1024)
    tn = _tile(d, 1024)
    tk = _tile(d, 1024)
    ts = _tile(s, 1024)
    h = rms_fwd(tie(x2, ici_rest[-1]), norm_gain, "rms_fwd_x")
    tabs = _rope_tables(s)
    qg = jnp.tile(q_norm_gain, (1, LANES // HEAD_DIM))
    kg = jnp.tile(k_norm_gain, (1, LANES // HEAD_DIM))
    seg_i = jnp.arange(SEG) // HEAD_DIM
    segb = (seg_i[:, None] == seg_i[None, :]).astype(BF16)
    wi_all, = exchange_wait("gather_wi_wait", plan_gather_ici(1), ici_wi, h)
    wi_all, = exchange("gather_wi_pass", plan_gather_pass, [wi_all], 3)
    z = matmul(h, wi_all, grid=(s // tm, N_CHIPS, d // tk),
               a_spec=pl.BlockSpec((tm, tk), lambda i, j, k: (i, k)),
               b_spec=pl.BlockSpec((None, tk, sh), lambda i, j, k: (j, k, 0)),
               o_spec=pl.BlockSpec((tm, sh), lambda i, j, k: (i, j)),
               out_shape=jax.ShapeDtypeStruct((s, in_w), F32), dims=NN, name="mm_z")
    wo_all, wg_all, wp_all, conv_all = exchange_wait("gather_rest_wait", plan_gather_ici(3), ici_rest, z)
    wo_all, wg_all, wp_all = exchange("gather_rest_pass", plan_gather_pass, [wo_all, wg_all, wp_all], 9)
    wo_full = wo_all.reshape(d, d)
    wg_full = wg_all.reshape(d, d)
    conv_full = conv_all.transpose(1, 0, 2).reshape(SUBLANES, cw)
    qs, ks, vb = qk_prep_fwd(z, tabs, qg, kg, segb, aw)
    attn, mix = attn_fwd(qs, ks, vb, z, attn_sinks, aw, d, ga_off)
    mix = conv_fwd(z, conv_full, mix, aw, cw, b_off)
    sq = lambda shape: dict(
        a_spec=pl.BlockSpec((tm, tk), lambda i, j, k: (i, k)),
        o_spec=pl.BlockSpec((tm, tn), lambda i, j, k: (i, j)),
        out_shape=jax.ShapeDtypeStruct(shape, F32))
    x1 = matmul(mix, wo_full, grid=(s // tm, d // tn, d // tk),
                b_spec=pl.BlockSpec((tk, tn), lambda i, j, k: (k, j)), dims=NN, name="mm_x1",
                res=x2, res_spec=pl.BlockSpec((tm, tn), lambda i, j, k: (i, j)), **sq((s, d)))
    hg = rms_fwd(x1, ple_gate_norm_gain, "rms_fwd_x1")
    gl = matmul(hg, wg_full, grid=(s // tm, d // tn, d // tk),
                b_spec=pl.BlockSpec((tk, tn), lambda i, j, k: (k, j)), dims=NN, name="mm_gl", **sq((s, d)))
    pq = d // N_CHIPS
    pe = matmul(p2, wp_all, grid=(s // tm, N_CHIPS, 1),
                a_spec=pl.BlockSpec((tm, ple), lambda i, j, k: (i, 0)),
                b_spec=pl.BlockSpec((None, ple, pq), lambda i, j, k: (j, 0, 0)),
                o_spec=pl.BlockSpec((tm, pq), lambda i, j, k: (i, j)),
                out_shape=jax.ShapeDtypeStruct((s, d), F32), dims=NN, name="mm_pe")

    d_gl, d_pe, dy, acc_head = head_fwd_bwd(x1, gl, pe, tgt, b_ple_gate, ple_norm_gain)
    d_hg = matmul(d_gl, wg_full, grid=(s // tm, d // tn, d // tk),
                  b_spec=pl.BlockSpec((tn, tk), lambda i, j, k: (j, k)), dims=NT, name="mm_d_hg", **sq((s, d)))
    wgrad = lambda a_cols, shape3, o_spec, b_cols, name, a, b, grid: matmul(
        a, b, grid=grid,
        a_spec=pl.BlockSpec((ts, a_cols), lambda i, j, k: (k, i)),
        b_spec=pl.BlockSpec((ts, b_cols), lambda i, j, k: (k, j)),
        o_spec=o_spec, out_shape=jax.ShapeDtypeStruct(shape3, BF16), dims=TN, name=name)
    g_wg = wgrad(tn, (d, d), pl.BlockSpec((tn, tn), lambda i, j, k: (i, j)), tn, "mm_g_wg", hg, d_gl,
                 (d // tn, d // tn, s // ts))
    g_wp = wgrad(ple, (N_CHIPS, ple, pq), pl.BlockSpec((None, ple, pq), lambda i, j, k: (j, 0, 0)), pq,
                 "mm_g_wp", p2, d_pe, (1, N_CHIPS, s // ts))
    d_x1, d_x1b, acc_g = rms_bwd_add(d_hg, x1, dy, ple_gate_norm_gain, "rms_bwd_x1", True)
    d_mix = matmul(d_x1b, wo_full, grid=(s // tm, d // tn, d // tk),
                   b_spec=pl.BlockSpec((tn, tk), lambda i, j, k: (j, k)), dims=NT, name="mm_d_mix", **sq((s, d)))
    g_wo = wgrad(tn, (d, d), pl.BlockSpec((tn, tn), lambda i, j, k: (i, j)), tn, "mm_g_wo", mix, d_x1b,
                 (d // tn, d // tn, s // ts))
    d_o, dz = gate_bwd(d_mix, attn, z, aw, ga_off)
    dqs, dks, dvs, acc_sink = attn_bwd(qs, ks, vb, d_o, attn_sinks, aw)
    dz, acc_qk = qk_prep_bwd(z, dqs, dks, dvs, tabs, qg, kg, segb, dz, aw)
    dz, acc_conv = conv_bwd(z, d_mix, conv_full, dz, aw, cw, b_off)
    g_wi = wgrad(tn, (N_CHIPS, d, sh), pl.BlockSpec((None, tn, sh), lambda i, j, k: (j, i, 0)), sh,
                 "mm_g_wi", h, dz, (d // tn, N_CHIPS, s // ts))

    grads = [g_wi, g_wo.reshape(N_CHIPS, dq, d), g_wg.reshape(N_CHIPS, dq, d), g_wp]
    half_like = lambda a: lax.empty((N_CHIPS, a.shape[1] // 2, a.shape[2]), BF16)
    swapped = exchange("swap_halves", plan_swap, grads + [half_like(a) for a in grads], 4)
    grads, got = swapped[:4], swapped[4:]
    names = ("wi", "wo", "wg", "wp")
    parts = [add_sibling(g, o, core, "add_sibling_" + nm) for g, o, nm in zip(grads, got, names)]
    ici_grads = exchange_start("scatter_start", plan_scatter, parts + [lax.empty(a.shape, BF16) for a in parts], 12)
    d_h = matmul(tie(dz, ici_grads[-1]), wi_all, grid=(s // tm, d // tn, N_CHIPS),
                 a_spec=pl.BlockSpec((tm, sh), lambda i, j, k: (i, k)),
                 b_spec=pl.BlockSpec((None, tn, sh), lambda i, j, k: (k, j, 0)),
                 o_spec=pl.BlockSpec((tm, tn), lambda i, j, k: (i, j)),
                 out_shape=jax.ShapeDtypeStruct((s, d), F32), dims=NT, name="mm_d_h")
    grad_x, acc_x = rms_bwd_add(d_h, x2, d_x1, norm_gain, "rms_bwd_x", False)
    scattered = exchange_wait("scatter_wait", plan_scatter, ici_grads, grad_x)
    parts, owned = scattered[:4], scattered[4:]
    halves = [sum_chips(pt, o, place, "sum_chips_" + nm) for pt, o, nm in zip(parts, owned, names)]
    full = exchange("join_halves", plan_join, halves, 4)
    big = {}
    for nm, g, w, m, v in (("w_in", full[0], w_in, m_w_in, v_w_in), ("w_out", full[1], w_out, m_w_out, v_w_out),
                           ("w_ple_gate", full[2], w_ple_gate, m_w_ple_gate, v_w_ple_gate),
                           ("w_ple_proj", full[3], w_ple_proj, m_w_ple_proj, v_w_ple_proj)):
        big[nm] = [o[None] for o in adamw(g, w[0], m[0], v[0], "adamw_" + nm)]

    wsm = max(d, cw)
    misc = jnp.concatenate([acc_qk[0:1, :HEAD_DIM], acc_qk[1:2, :HEAD_DIM], acc_sink[0:1, :nq]], axis=1)
    small = jnp.concatenate([
        _pad_cols(acc_x[0:1], wsm), _pad_cols(acc_g[0:1], wsm), _pad_cols(acc_head[0:1], wsm),
        _pad_cols(acc_head[1:2], wsm), _pad_cols(misc, wsm), _pad_cols(acc_conv[0:3], wsm)], axis=0)
    both = jnp.concatenate([small, _pad_rows(_pad_cols(acc_head[2:3], wsm), SUBLANES)], axis=0)
    tot = sum_devices(allgather_small(both))
    loss = tot[SUBLANES, 0]

    def pack(vals):
        ng, pg, bg, eg, qgv, kgv, sk, cv = vals
        misc_v = jnp.concatenate([qgv, kgv, sk], axis=1)
        return jnp.concatenate([_pad_cols(ng, wsm), _pad_cols(pg, wsm), _pad_cols(bg, wsm), _pad_cols(eg, wsm),
                                _pad_cols(misc_v, wsm), _pad_cols(cv[0], wsm)], axis=0)

    g_small = jnp.concatenate(
        [tot[0:5], _pad_cols(lax.dynamic_slice(tot[5:8], (0, chip * cq), (3, cq)), wsm)], axis=0)
    w_small = pack((norm_gain, ple_gate_norm_gain, b_ple_gate, ple_norm_gain, q_norm_gain, k_norm_gain,
                    attn_sinks, conv_w))
    m_small = pack((m_norm_gain, m_ple_gate_norm_gain, m_b_ple_gate, m_ple_norm_gain, m_q_norm_gain,
                    m_k_norm_gain, m_attn_sinks, m_conv_w))
    v_small = pack((v_norm_gain, v_ple_gate_norm_gain, v_b_ple_gate, v_ple_norm_gain, v_q_norm_gain,
                    v_k_norm_gain, v_attn_sinks, v_conv_w))
    sm = adamw(g_small, w_small, m_small, v_small, "adamw_small")

    def unpack(a):
        return {"norm_gain": a[0:1, :d], "ple_gate_norm_gain": a[1:2, :d], "b_ple_gate": a[2:3, :d],
                "ple_norm_gain": a[3:4, :d], "q_norm_gain": a[4:5, :HEAD_DIM],
                "k_norm_gain": a[4:5, HEAD_DIM:2 * HEAD_DIM],
                "attn_sinks": a[4:5, 2 * HEAD_DIM:2 * HEAD_DIM + nq], "conv_w": a[5:8, :cq][None]}

    order = ("norm_gain", "w_in", "q_norm_gain", "k_norm_gain", "attn_sinks", "conv_w", "w_out",
             "ple_gate_norm_gain", "w_ple_gate", "b_ple_gate", "w_ple_proj", "ple_norm_gain")
    outs = [loss, grad_x[None]]
    for kind in range(4):
        table = unpack(sm[kind])
        for nm in order:
            outs.append(big[nm][kind] if nm in big else table[nm])
    return tuple(outs)
```

```python
import functools

import jax
import jax.numpy as jnp
from jax import lax
from jax.experimental import pallas as pl
from jax.experimental.pallas import tpu as pltpu

F32 = jnp.float32
BF16 = jnp.bfloat16
MESH = pl.DeviceIdType.MESH

HEAD_DIM = 64
N_KV_HEADS = 4
KV_WIDTH = N_KV_HEADS * HEAD_DIM
BLOCK = 128
ROT_DIM = 16
ROPE_THETA = 500000.0
EPS = 1e-6
NEG_INF = -1e30
N_CHIPS = 4
LANES = 128
SUBLANES = 8
COLT = 512
SEG = 256
VMEM_LIMIT = 48 * 1024 * 1024

ADAM_LR = 0.001
ADAM_B1 = 0.9
ADAM_B2 = 0.999
ADAM_EPS = 1e-08
ADAM_WD = 0.01
ADAM_STEP = 10

NN = (((1,), (0,)), ((), ()))
NT = (((1,), (1,)), ((), ()))
TN = (((0,), (0,)), ((), ()))


def _call(body, **kw):
    return pl.pallas_call(body, **kw)


def _call_after(body, after, **kw):
    n_in, n_after = len(kw["in_specs"]), len(after)
    kw["in_specs"] = list(kw["in_specs"]) + [pl.BlockSpec(memory_space=pl.ANY)] * n_after

    def body_after(*refs):
        body(*refs[:n_in], *refs[n_in + n_after:])

    call = _call(body_after, **kw)
    return lambda *args: call(*args, *after)


def _params(sem):
    return pltpu.CompilerParams(dimension_semantics=sem, vmem_limit_bytes=VMEM_LIMIT)


def _tile(n, pref):
    return pref if n % pref == 0 else n


def _sigmoid(v):
    return 1.0 / (1.0 + jnp.exp(-v))


def _hbm():
    return pl.BlockSpec(memory_space=pl.ANY)


def cast_into_slot(a, chip, name):
    r, c = a.shape
    tr = _tile(r, 256)

    def body(chip_ref, a_ref, o_ref):
        o_ref[...] = a_ref[...].astype(BF16)

    grid_spec = pltpu.PrefetchScalarGridSpec(
        num_scalar_prefetch=1, grid=(r // tr,),
        in_specs=[pl.BlockSpec((tr, c), lambda i, chip_ref: (i, 0))],
        out_specs=pl.BlockSpec((None, tr, c), lambda i, chip_ref: (chip_ref[0], i, 0)))
    return _call(body, name=name, grid_spec=grid_spec,
                 out_shape=jax.ShapeDtypeStruct((N_CHIPS, r, c), BF16),
                 compiler_params=_params(("parallel",)))(chip, a)


def rms_fwd(x, gain, name, after=()):
    s, d = x.shape
    tm = _tile(s, 256)

    def body(x_ref, g_ref, o_ref):
        xf = x_ref[...]
        r = lax.rsqrt(jnp.mean(xf * xf, axis=-1, keepdims=True) + EPS)
        o_ref[...] = ((xf * r) * g_ref[...]).astype(BF16)

    return _call_after(body, after, name=name, grid=(s // tm,),
                       in_specs=[pl.BlockSpec((tm, d), lambda i: (i, 0)),
                                 pl.BlockSpec((1, d), lambda i: (0, 0))],
                       out_specs=pl.BlockSpec((tm, d), lambda i: (i, 0)),
                       out_shape=jax.ShapeDtypeStruct((s, d), BF16),
                       compiler_params=_params(("parallel",)))(x, gain)


def rms_bwd_add(dyn, xin, add, gain, name, want_bf16):
    s, d = xin.shape
    tm = _tile(s, 256)

    def body(dy_ref, x_ref, a_ref, g_ref, *outs):
        i = pl.program_id(0)
        dx_ref, acc_ref = outs[0], outs[-1]
        xf = x_ref[...]
        r = lax.rsqrt(jnp.mean(xf * xf, axis=-1, keepdims=True) + EPS)
        xhat = xf * r
        dyv = dy_ref[...]
        gd = dyv * g_ref[...]
        dx = a_ref[...] + r * (gd - xhat * jnp.mean(xhat * gd, axis=-1, keepdims=True))
        dx_ref[...] = dx
        if want_bf16:
            outs[1][...] = dx.astype(BF16)

        @pl.when(i == 0)
        def _():
            acc_ref[...] = jnp.zeros_like(acc_ref)

        acc_ref[0:1, :] += jnp.sum(dyv * xhat, axis=0, keepdims=True)

    row = pl.BlockSpec((tm, d), lambda i: (i, 0))
    out_specs = [row] + ([row] if want_bf16 else []) + [pl.BlockSpec((SUBLANES, d), lambda i: (0, 0))]
    out_shape = ([jax.ShapeDtypeStruct((s, d), F32)]
                 + ([jax.ShapeDtypeStruct((s, d), BF16)] if want_bf16 else [])
                 + [jax.ShapeDtypeStruct((SUBLANES, d), F32)])
    return _call(body, name=name, grid=(s // tm,),
                 in_specs=[row, row, row, pl.BlockSpec((1, d), lambda i: (0, 0))],
                 out_specs=out_specs, out_shape=out_shape,
                 compiler_params=_params(("arbitrary",)))(dyn, xin, add, gain)


def head_fwd_bwd(x1, gl, pe, tgt, bias, ple_gain):
    s, d = x1.shape
    tm = _tile(s, 128)

    def body(x1_ref, gl_ref, pe_ref, t_ref, b_ref, g_ref, dgl_ref, dpe_ref, dy_ref, acc_ref):
        i = pl.program_id(0)
        gate = _sigmoid(gl_ref[...] + b_ref[...])
        pev = pe_ref[...]
        r = lax.rsqrt(jnp.mean(pev * pev, axis=-1, keepdims=True) + EPS)
        pehat = pev * r
        gain = g_ref[...]
        e = pehat * gain
        diff = (x1_ref[...] + gate * e) - t_ref[...]
        dy = diff * (1.0 / d)
        dy_ref[...] = dy
        d_gl = (dy * e) * (gate * (1.0 - gate))
        dgl_ref[...] = d_gl.astype(BF16)
        d_e = dy * gate
        gd = d_e * gain
        d_pe = r * (gd - pehat * jnp.mean(pehat * gd, axis=-1, keepdims=True))
        dpe_ref[...] = d_pe.astype(BF16)

        @pl.when(i == 0)
        def _():
            acc_ref[...] = jnp.zeros_like(acc_ref)

        acc_ref[0:1, :] += jnp.sum(d_gl, axis=0, keepdims=True)
        acc_ref[1:2, :] += jnp.sum(d_e * pehat, axis=0, keepdims=True)
        acc_ref[2:3, :] += jnp.sum(diff * diff, axis=0, keepdims=True) * (0.5 / d)

    row = pl.BlockSpec((tm, d), lambda i: (i, 0))
    vec = pl.BlockSpec((1, d), lambda i: (0, 0))
    return _call(body, name="head_fwd_bwd", grid=(s // tm,),
                 in_specs=[row, row, row, row, vec, vec],
                 out_specs=[row, row, row, pl.BlockSpec((SUBLANES, d), lambda i: (0, 0))],
                 out_shape=[jax.ShapeDtypeStruct((s, d), BF16), jax.ShapeDtypeStruct((s, d), BF16),
                            jax.ShapeDtypeStruct((s, d), F32), jax.ShapeDtypeStruct((SUBLANES, d), F32)],
                 compiler_params=_params(("arbitrary",)))(x1, gl, pe, tgt, bias, ple_gain)


def adamw(g, w, m, v, name):
    r, c = g.shape
    tr = _tile(r, 128)

    def body(g_ref, w_ref, m_ref, v_ref, go_ref, d_ref, mo_ref, vo_ref):
        gv = g_ref[...]
        mn = ADAM_B1 * m_ref[...] + (1.0 - ADAM_B1) * gv
        vn = ADAM_B2 * v_ref[...] + (1.0 - ADAM_B2) * (gv * gv)
        m_hat = mn / (1.0 - ADAM_B1 ** ADAM_STEP)
        v_hat = vn / (1.0 - ADAM_B2 ** ADAM_STEP)
        go_ref[...] = gv
        d_ref[...] = -ADAM_LR * (m_hat / (jnp.sqrt(v_hat) + ADAM_EPS) + ADAM_WD * w_ref[...])
        mo_ref[...] = mn
        vo_ref[...] = vn

    blk = pl.BlockSpec((tr, c), lambda i: (i, 0))
    shp = jax.ShapeDtypeStruct((r, c), F32)
    return _call(body, name=name, grid=(r // tr,), in_specs=[blk] * 4, out_specs=[blk] * 4,
                 out_shape=[shp] * 4, compiler_params=_params(("parallel",)))(g, w, m, v)


def matmul(a, b, *, grid, a_spec, b_spec, o_spec, out_shape, dims, name, res=None, res_spec=None, after=()):
    nk = grid[2]
    acc_shape = tuple(d for d in o_spec.block_shape if d is not None)

    def body(*refs):
        if res is None:
            a_ref, b_ref, o_ref, acc_ref = refs
        else:
            a_ref, b_ref, r_ref, o_ref, acc_ref = refs
        k = pl.program_id(2)

        @pl.when(k == 0)
        def _():
            acc_ref[...] = jnp.zeros_like(acc_ref)

        acc_ref[...] += lax.dot_general(a_ref[...].astype(BF16), b_ref[...].astype(BF16), dims,
                                        preferred_element_type=F32)

        @pl.when(k == nk - 1)
        def _():
            out = acc_ref[...]
            if res is not None:
                out = r_ref[...] + out
            o_ref[...] = out.astype(o_ref.dtype)

    in_specs = [a_spec, b_spec] + ([res_spec] if res is not None else [])
    args = (a, b) + ((res,) if res is not None else ())
    return _call_after(body, after, name=name, grid=grid, in_specs=in_specs, out_specs=o_spec,
                       out_shape=out_shape, scratch_shapes=[pltpu.VMEM(acc_shape, F32)],
                       compiler_params=_params(("parallel", "parallel", "arbitrary")))(*args)


def _seg_mean(sq, segb):
    parts = []
    for cgrp in range(sq.shape[1] // SEG):
        blk = sq[:, cgrp * SEG:(cgrp + 1) * SEG]
        hi = blk.astype(BF16)
        r1 = blk - hi.astype(F32)
        mid = r1.astype(BF16)
        lo = (r1 - mid.astype(F32)).astype(BF16)
        acc = jnp.dot(hi, segb, preferred_element_type=F32)
        acc += jnp.dot(mid, segb, preferred_element_type=F32)
        acc += jnp.dot(lo, segb, preferred_element_type=F32)
        parts.append(acc)
    out = parts[0] if len(parts) == 1 else jnp.concatenate(parts, axis=1)
    return out * (1.0 / HEAD_DIM)


def _rope(v, cos, sa, sb):
    parts = []
    for cgrp in range(v.shape[1] // LANES):
        blk = v[:, cgrp * LANES:(cgrp + 1) * LANES]
        parts.append(blk * cos + pltpu.roll(blk, LANES - 8, 1) * sa + pltpu.roll(blk, 8, 1) * sb)
    return parts[0] if len(parts) == 1 else jnp.concatenate(parts, axis=1)


def _rope_t(dv, cos, sa, sb):
    parts = []
    for cgrp in range(dv.shape[1] // LANES):
        blk = dv[:, cgrp * LANES:(cgrp + 1) * LANES]
        parts.append(blk * cos + pltpu.roll(blk * sa, 8, 1) + pltpu.roll(blk * sb, LANES - 8, 1))
    return parts[0] if len(parts) == 1 else jnp.concatenate(parts, axis=1)


def _tile_lanes(vec, width):
    reps = width // LANES
    return vec if reps == 1 else jnp.tile(vec, (1, reps))


def qk_prep_fwd(z, tabs, qg, kg, segb, aw):
    s = z.shape[0]
    tm = _tile(s, 256)
    wq = aw + 2 * KV_WIDTH

    def body(z_ref, cos_ref, sa_ref, sb_ref, qg_ref, kg_ref, seg_ref, qs_ref, ks_ref, vb_ref):
        zz = z_ref[...]
        q, k, v = zz[:, :aw], zz[:, aw:aw + KV_WIDTH], zz[:, aw + KV_WIDTH:]
        cos, sa, sb, segm = cos_ref[...], sa_ref[...], sb_ref[...], seg_ref[...]
        rq = lax.rsqrt(_seg_mean(q * q, segm) + EPS)
        qn = (q * rq) * _tile_lanes(qg_ref[...], aw)
        qs_ref[...] = (_rope(qn, cos, sa, sb) * (HEAD_DIM ** -0.5)).astype(BF16)
        rk = lax.rsqrt(_seg_mean(k * k, segm) + EPS)
        kn = (k * rk) * _tile_lanes(kg_ref[...], KV_WIDTH)
        ks_ref[...] = _rope(kn, cos, sa, sb).astype(BF16)
        vb_ref[...] = v.astype(BF16)

    tab = pl.BlockSpec((tm, LANES), lambda i: (i, 0))
    vec = pl.BlockSpec((1, LANES), lambda i: (0, 0))
    return _call(body, name="qk_prep_fwd", grid=(s // tm,),
                 in_specs=[pl.BlockSpec((tm, wq), lambda i: (i, 0)), tab, tab, tab, vec, vec,
                           pl.BlockSpec((SEG, SEG), lambda i: (0, 0))],
                 out_specs=[pl.BlockSpec((tm, aw), lambda i: (i, 0)),
                            pl.BlockSpec((tm, KV_WIDTH), lambda i: (i, 0)),
                            pl.BlockSpec((tm, KV_WIDTH), lambda i: (i, 0))],
                 out_shape=[jax.ShapeDtypeStruct((s, aw), BF16), jax.ShapeDtypeStruct((s, KV_WIDTH), BF16),
                            jax.ShapeDtypeStruct((s, KV_WIDTH), BF16)],
                 compiler_params=_params(("parallel",)))(z, *tabs, qg, kg, segb)


def qk_prep_bwd(z, dqs, dks, dvs, tabs, qg, kg, segb, dz, aw):
    s = z.shape[0]
    tm = _tile(s, 256)
    wq = aw + 2 * KV_WIDTH

    def body(z_ref, dq_ref, dk_ref, dv_ref, cos_ref, sa_ref, sb_ref, qg_ref, kg_ref, seg_ref, dz_in,
             dz_ref, acc_ref):
        i = pl.program_id(0)
        zz = z_ref[...]
        q, k = zz[:, :aw], zz[:, aw:aw + KV_WIDTH]
        cos, sa, sb, segm = cos_ref[...], sa_ref[...], sb_ref[...], seg_ref[...]

        def one(xv, dout, gvec, width):
            g = _tile_lanes(gvec, width)
            r = lax.rsqrt(_seg_mean(xv * xv, segm) + EPS)
            xhat = xv * r
            dn = _rope_t(dout, cos, sa, sb)
            gd = dn * g
            dx = r * (gd - xhat * _seg_mean(xhat * gd, segm))
            contrib = jnp.sum(dn * xhat, axis=0, keepdims=True)
            folded = contrib[:, :LANES]
            for cgrp in range(1, width // LANES):
                folded = folded + contrib[:, cgrp * LANES:(cgrp + 1) * LANES]
            return dx, folded + pltpu.roll(folded, HEAD_DIM, 1)

        dq, gq = one(q, dq_ref[...] * (HEAD_DIM ** -0.5), qg_ref[...], aw)
        dk, gk = one(k, dk_ref[...], kg_ref[...], KV_WIDTH)
        dz_ref[:, :aw] = dq.astype(BF16)
        dz_ref[:, aw:aw + KV_WIDTH] = dk.astype(BF16)
        dz_ref[:, aw + KV_WIDTH:] = dv_ref[...].astype(BF16)

        @pl.when(i == 0)
        def _():
            acc_ref[...] = jnp.zeros_like(acc_ref)

        acc_ref[0:1, :] += gq
        acc_ref[1:2, :] += gk

    tab = pl.BlockSpec((tm, LANES), lambda i: (i, 0))
    vec = pl.BlockSpec((1, LANES), lambda i: (0, 0))
    kvb = pl.BlockSpec((tm, KV_WIDTH), lambda i: (i, 0))
    return _call(body, name="qk_prep_bwd", grid=(s // tm,),
                 in_specs=[pl.BlockSpec((tm, wq), lambda i: (i, 0)),
                           pl.BlockSpec((tm, aw), lambda i: (i, 0)), kvb, kvb, tab, tab, tab, vec, vec,
                           pl.BlockSpec((SEG, SEG), lambda i: (0, 0)), _hbm()],
                 out_specs=[pl.BlockSpec((tm, wq), lambda i: (i, 0)),
                            pl.BlockSpec((SUBLANES, LANES), lambda i: (0, 0))],
                 out_shape=[jax.ShapeDtypeStruct(dz.shape, BF16), jax.ShapeDtypeStruct((SUBLANES, LANES), F32)],
                 input_output_aliases={10: 0},
                 compiler_params=_params(("arbitrary",)))(z, dqs, dks, dvs, *tabs, qg, kg, segb, dz)


def _placed(band, lane_idx):
    out = {}
    for kh in range(N_KV_HEADS):
        grp = band[:, (kh // 2) * LANES:(kh // 2 + 1) * LANES]
        for half in (0, 1):
            t = grp if half == kh % 2 else pltpu.roll(grp, HEAD_DIM, 1)
            keep = (lane_idx >= half * HEAD_DIM) & (lane_idx < (half + 1) * HEAD_DIM)
            out[kh, half] = jnp.where(keep, t, jnp.zeros_like(t))
    return out


def _scores(qgrp, kpl, valid, sink):
    sc = lax.dot_general(qgrp, kpl, NT, preferred_element_type=F32)
    sc = jnp.where(valid, sc, NEG_INF)
    m = jnp.maximum(jnp.max(sc, axis=-1, keepdims=True), sink)
    e = jnp.exp(sc - m)
    es = jnp.exp(sink - m)
    den = jnp.sum(e, axis=-1, keepdims=True) + es
    return e / den, es / den


def _valid_mask(n):
    r_i = lax.broadcasted_iota(jnp.int32, (BLOCK, 2 * BLOCK), 0)
    j_i = lax.broadcasted_iota(jnp.int32, (BLOCK, 2 * BLOCK), 1)
    return (j_i > r_i) & (j_i <= r_i + BLOCK) & ((n > 0) | (j_i >= BLOCK))


def attn_fwd(qs, ks, vb, z, sinks, aw, d, ga_off):
    s = qs.shape[0]
    nb = s // BLOCK
    nq = aw // HEAD_DIM
    grp_sz = nq // N_KV_HEADS
    n_ga = aw // COLT

    def body(q_ref, ko_ref, kp_ref, vo_ref, vp_ref, *rest):
        ga_refs = rest[:n_ga]
        sink_ref, attn_ref, mix_ref = rest[n_ga:]
        n = pl.program_id(0)
        lane_idx = lax.broadcasted_iota(jnp.int32, (2 * BLOCK, LANES), 1)
        kpl = _placed(jnp.concatenate([kp_ref[...], ko_ref[...]], axis=0), lane_idx)
        vpl = _placed(jnp.concatenate([vp_ref[...], vo_ref[...]], axis=0), lane_idx)
        valid = _valid_mask(n)
        for cgrp in range(nq // 2):
            qgrp = q_ref[:, cgrp * LANES:(cgrp + 1) * LANES]
            acc = jnp.zeros((BLOCK, LANES), F32)
            for half in (0, 1):
                h = 2 * cgrp + half
                kh = h // grp_sz
                p, _ = _scores(qgrp, kpl[kh, half], valid, sink_ref[0, h])
                acc += jnp.dot(p.astype(BF16), vpl[kh, half], preferred_element_type=F32)
            attn_ref[:, cgrp * LANES:(cgrp + 1) * LANES] = acc
            col = cgrp * LANES
            ga = ga_refs[col // COLT][:, col % COLT:col % COLT + LANES]
            mix_ref[:, cgrp * LANES:(cgrp + 1) * LANES] = (acc * (ga * _sigmoid(ga))).astype(BF16)

    own = lambda n: (n, 0)
    prev = lambda n: (jnp.maximum(n - 1, 0), 0)
    kvs = lambda imap: pl.BlockSpec((BLOCK, KV_WIDTH), imap)
    ga_specs = [pl.BlockSpec((BLOCK, COLT), functools.partial(lambda n, j: (n, ga_off + j), j=j))
                for j in range(n_ga)]
    return _call(body, name="attn_fwd", grid=(nb,),
                 in_specs=[pl.BlockSpec((BLOCK, aw), own), kvs(own), kvs(prev), kvs(own), kvs(prev)]
                 + ga_specs + [pl.BlockSpec(memory_space=pltpu.SMEM)],
                 out_specs=[pl.BlockSpec((BLOCK, aw), own), pl.BlockSpec((BLOCK, aw), own)],
                 out_shape=[jax.ShapeDtypeStruct((s, aw), F32), jax.ShapeDtypeStruct((s, d), BF16)],
                 compiler_params=_params(("parallel",)))(qs, ks, ks, vb, vb, *([z] * n_ga), sinks)


def gate_bwd(d_mix, attn, z, aw, ga_off):
    s, in_w = z.shape
    tm = _tile(s, 256)

    def body(dm_ref, at_ref, ga_ref, do_ref, dz_ref):
        ga = ga_ref[...]
        sig = _sigmoid(ga)
        dm = dm_ref[...]
        do_ref[...] = (dm * (ga * sig)).astype(BF16)
        dz_ref[...] = ((dm * at_ref[...]) * (sig * (1.0 + ga * (1.0 - sig)))).astype(BF16)

    blk = pl.BlockSpec((tm, COLT), lambda i, j: (i, j))
    gab = pl.BlockSpec((tm, COLT), lambda i, j: (i, ga_off + j))
    return _call(body, name="gate_bwd", grid=(s // tm, aw // COLT),
                 in_specs=[blk, blk, gab], out_specs=[blk, gab],
                 out_shape=[jax.ShapeDtypeStruct((s, aw), BF16), jax.ShapeDtypeStruct((s, in_w), BF16)],
                 compiler_params=_params(("parallel", "parallel")))(d_mix, attn, z)


def attn_bwd(qs, ks, vb, d_o, sinks, aw):
    s = qs.shape[0]
    nb = s // BLOCK
    nq = aw // HEAD_DIM
    grp_sz = nq // N_KV_HEADS

    def body(q_ref, ko_ref, kp_ref, vo_ref, vp_ref, do_ref, sink_ref, dq_ref, dk_ref, dv_ref, ds_ref,
             ck_ref, cv_ref):
        n = pl.program_id(0)

        @pl.when(n == 0)
        def _():
            ds_ref[...] = jnp.zeros_like(ds_ref)
            dk_ref[...] = jnp.zeros_like(dk_ref)
            dv_ref[...] = jnp.zeros_like(dv_ref)

        @pl.when(n < nb)
        def _():
            lane_idx = lax.broadcasted_iota(jnp.int32, (2 * BLOCK, LANES), 1)
            lane_row = lax.broadcasted_iota(jnp.int32, (1, LANES), 1)
            kpl = _placed(jnp.concatenate([kp_ref[...], ko_ref[...]], axis=0), lane_idx)
            vpl = _placed(jnp.concatenate([vp_ref[...], vo_ref[...]], axis=0), lane_idx)
            valid = _valid_mask(n)
            dk_acc = [jnp.zeros((2 * BLOCK, LANES), F32) for _ in range(N_KV_HEADS)]
            dv_acc = [jnp.zeros((2 * BLOCK, LANES), F32) for _ in range(N_KV_HEADS)]
            ds_row = jnp.zeros((1, LANES), F32)
            for cgrp in range(nq // 2):
                qgrp = q_ref[:, cgrp * LANES:(cgrp + 1) * LANES]
                dog = do_ref[:, cgrp * LANES:(cgrp + 1) * LANES]
                dq_acc = jnp.zeros((BLOCK, LANES), F32)
                for half in (0, 1):
                    h = 2 * cgrp + half
                    kh = h // grp_sz
                    p, p_sink = _scores(qgrp, kpl[kh, half], valid, sink_ref[0, h])
                    dp = lax.dot_general(dog, vpl[kh, half], NT, preferred_element_type=F32)
                    delta = jnp.sum(p * dp, axis=-1, keepdims=True)
                    dsb = (p * (dp - delta)).astype(BF16)
                    dq_acc += jnp.dot(dsb, kpl[kh, half], preferred_element_type=F32)
                    keep = (lane_idx >= half * HEAD_DIM) & (lane_idx < (half + 1) * HEAD_DIM)
                    dkp = jnp.where(keep, lax.dot_general(dsb, qgrp, TN, preferred_element_type=F32), 0.0)
                    dvp = jnp.where(keep, lax.dot_general(p.astype(BF16), dog, TN, preferred_element_type=F32), 0.0)
                    if half != kh % 2:
                        dkp = pltpu.roll(dkp, HEAD_DIM, 1)
                        dvp = pltpu.roll(dvp, HEAD_DIM, 1)
                    dk_acc[kh] += dkp
                    dv_acc[kh] += dvp
                    dsink = -jnp.sum(p_sink * delta, axis=0, keepdims=True)
                    ds_row += jnp.where(lane_row == h, dsink, 0.0)
                dq_ref[:, cgrp * LANES:(cgrp + 1) * LANES] = dq_acc
            ds_ref[0:1, :] += ds_row
            dk_band = jnp.concatenate([dk_acc[0] + dk_acc[1], dk_acc[2] + dk_acc[3]], axis=1)
            dv_band = jnp.concatenate([dv_acc[0] + dv_acc[1], dv_acc[2] + dv_acc[3]], axis=1)

            @pl.when(n > 0)
            def _():
                dk_ref[...] = ck_ref[...] + dk_band[:BLOCK]
                dv_ref[...] = cv_ref[...] + dv_band[:BLOCK]

            ck_ref[...] = dk_band[BLOCK:]
            cv_ref[...] = dv_band[BLOCK:]

        @pl.when(n == nb)
        def _():
            dk_ref[...] = ck_ref[...]
            dv_ref[...] = cv_ref[...]

    own = lambda n: (jnp.minimum(n, nb - 1), 0)
    prev = lambda n: (jnp.clip(n - 1, 0, nb - 1), 0)
    done = lambda n: (jnp.maximum(n - 1, 0), 0)
    kvs = lambda imap: pl.BlockSpec((BLOCK, KV_WIDTH), imap)
    return _call(body, name="attn_bwd", grid=(nb + 1,),
                 in_specs=[pl.BlockSpec((BLOCK, aw), own), kvs(own), kvs(prev), kvs(own), kvs(prev),
                           pl.BlockSpec((BLOCK, aw), own), pl.BlockSpec(memory_space=pltpu.SMEM)],
                 out_specs=[pl.BlockSpec((BLOCK, aw), own), kvs(done), kvs(done),
                            pl.BlockSpec((SUBLANES, LANES), lambda n: (0, 0))],
                 out_shape=[jax.ShapeDtypeStruct((s, aw), F32), jax.ShapeDtypeStruct((s, KV_WIDTH), F32),
                            jax.ShapeDtypeStruct((s, KV_WIDTH), F32), jax.ShapeDtypeStruct((SUBLANES, LANES), F32)],
                 scratch_shapes=[pltpu.VMEM((BLOCK, KV_WIDTH), F32), pltpu.VMEM((BLOCK, KV_WIDTH), F32)],
                 compiler_params=_params(("arbitrary",)))(qs, ks, ks, vb, vb, d_o, sinks)


def conv_fwd(z, conv_w, mix, aw, cw, b_off):
    s = z.shape[0]
    tm = _tile(s, 256)
    nseg = cw // COLT
    hb = tm // SUBLANES

    def body(b_ref, c_ref, h_ref, g_ref, cp_ref, hp_ref, w_ref, mix_in, mix_ref):
        i = pl.program_id(0)
        u = c_ref[...] * h_ref[...]
        up = jnp.where(i > 0, cp_ref[...] * hp_ref[...], 0.0)
        ext = jnp.concatenate([up, u], axis=0)
        um1 = pltpu.roll(ext, 1, 0)[SUBLANES:]
        um2 = pltpu.roll(ext, 2, 0)[SUBLANES:]
        w = w_ref[...]
        cv = w[0:1] * um2 + w[1:2] * um1 + w[2:3] * u
        g = g_ref[...]
        mix_ref[...] = ((b_ref[...] * cv) * (g * _sigmoid(g))).astype(BF16)

    seg = lambda k: pl.BlockSpec((tm, COLT), functools.partial(lambda i, j, k: (i, b_off + k * nseg + j), k=k))
    halo = lambda k: pl.BlockSpec(
        (SUBLANES, COLT), functools.partial(lambda i, j, k: (jnp.maximum(i * hb - 1, 0), b_off + k * nseg + j), k=k))
    return _call(body, name="conv_fwd", grid=(s // tm, nseg),
                 in_specs=[seg(0), seg(1), seg(2), seg(3), halo(1), halo(2),
                           pl.BlockSpec((SUBLANES, COLT), lambda i, j: (0, j)), _hbm()],
                 out_specs=pl.BlockSpec((tm, COLT), lambda i, j: (i, aw // COLT + j)),
                 out_shape=jax.ShapeDtypeStruct(mix.shape, BF16),
                 input_output_aliases={7: 0},
                 compiler_params=_params(("parallel", "parallel")))(z, z, z, z, z, z, conv_w, mix)


def conv_bwd(z, d_mix, conv_w, dz, aw, cw, b_off):
    s = z.shape[0]
    tm = _tile(s, 256)
    nseg = cw // COLT
    hb = tm // SUBLANES
    n_row = s // tm
    last_h = s // SUBLANES - 1

    def body(b_ref, c_ref, h_ref, g_ref, cp_ref, hp_ref, bn_ref, gn_ref, dm_ref, dmn_ref, w_ref, dz_in,
             dz_ref, acc_ref, stash_ref):
        i = pl.program_id(1)
        k = pl.program_id(2)

        @pl.when(k == 0)
        def _():
            cc, hh, bb, g = c_ref[...], h_ref[...], b_ref[...], g_ref[...]
            w = w_ref[...]
            u = cc * hh
            up = jnp.where(i > 0, cp_ref[...] * hp_ref[...], 0.0)
            ext = jnp.concatenate([up, u], axis=0)
            um1 = pltpu.roll(ext, 1, 0)[SUBLANES:]
            um2 = pltpu.roll(ext, 2, 0)[SUBLANES:]
            cv = w[0:1] * um2 + w[1:2] * um1 + w[2:3] * u
            sig = _sigmoid(g)
            sg = g * sig
            dm = dm_ref[...]
            d_cv = (dm * bb) * sg
            gn = gn_ref[...]
            d_cv_next = jnp.where(i < n_row - 1, (dmn_ref[...] * bn_ref[...]) * (gn * _sigmoid(gn)), 0.0)
            ext2 = jnp.concatenate([d_cv, d_cv_next], axis=0)
            dp1 = pltpu.roll(ext2, tm + SUBLANES - 1, 0)[:tm]
            dp2 = pltpu.roll(ext2, tm + SUBLANES - 2, 0)[:tm]
            d_u = w[2:3] * d_cv + w[1:2] * dp1 + w[0:1] * dp2
            stash_ref[0] = ((dm * cv) * sg).astype(BF16)
            stash_ref[1] = (d_u * hh).astype(BF16)
            stash_ref[2] = (d_u * cc).astype(BF16)
            stash_ref[3] = (((dm * bb) * cv) * (sig * (1.0 + g * (1.0 - sig)))).astype(BF16)

            @pl.when(i == 0)
            def _():
                acc_ref[...] = jnp.zeros_like(acc_ref)

            acc_ref[0:1, :] += jnp.sum(d_cv * um2, axis=0, keepdims=True)
            acc_ref[1:2, :] += jnp.sum(d_cv * um1, axis=0, keepdims=True)
            acc_ref[2:3, :] += jnp.sum(d_cv * u, axis=0, keepdims=True)

        dz_ref[...] = stash_ref[k]

    seg = lambda q: pl.BlockSpec((tm, COLT), functools.partial(lambda j, i, k, q: (i, b_off + q * nseg + j), q=q))
    halo_p = lambda q: pl.BlockSpec(
        (SUBLANES, COLT),
        functools.partial(lambda j, i, k, q: (jnp.maximum(i * hb - 1, 0), b_off + q * nseg + j), q=q))
    halo_n = lambda q: pl.BlockSpec(
        (SUBLANES, COLT),
        functools.partial(lambda j, i, k, q: (jnp.minimum((i + 1) * hb, last_h), b_off + q * nseg + j), q=q))
    return _call(body, name="conv_bwd", grid=(nseg, n_row, 4),
                 in_specs=[seg(0), seg(1), seg(2), seg(3), halo_p(1), halo_p(2), halo_n(0), halo_n(3),
                           pl.BlockSpec((tm, COLT), lambda j, i, k: (i, aw // COLT + j)),
                           pl.BlockSpec((SUBLANES, COLT),
                                        lambda j, i, k: (jnp.minimum((i + 1) * hb, last_h), aw // COLT + j)),
                           pl.BlockSpec((SUBLANES, COLT), lambda j, i, k: (0, j)), _hbm()],
                 out_specs=[pl.BlockSpec((tm, COLT), lambda j, i, k: (i, b_off + k * nseg + j)),
                            pl.BlockSpec((SUBLANES, COLT), lambda j, i, k: (0, j))],
                 out_shape=[jax.ShapeDtypeStruct(dz.shape, BF16), jax.ShapeDtypeStruct((SUBLANES, cw), F32)],
                 input_output_aliases={11: 0},
                 scratch_shapes=[pltpu.VMEM((4, tm, COLT), BF16)],
                 compiler_params=_params(("arbitrary", "arbitrary", "arbitrary")))(
                     z, z, z, z, z, z, z, z, d_mix, d_mix, conv_w, dz)


def _place():
    x, y, c = lax.axis_index("x"), lax.axis_index("y"), lax.axis_index("c")
    chips = [(1 - x, y), (x, 1 - y), (1 - x, 1 - y)]
    return x, y, c, chips


def _remote(src, dst, send_sem, recv_sem, device):
    return pltpu.make_async_remote_copy(src_ref=src, dst_ref=dst, send_sem=send_sem, recv_sem=recv_sem,
                                        device_id=device, device_id_type=MESH)


def plan_gather_ici(n_split):
    def plan(refs, send, recv):
        x, y, c, chips = _place()
        me = 2 * x + y
        mine, theirs = [], []
        for w, ref in enumerate(refs):
            for j, (cx, cy) in enumerate(chips):
                def part(slot):
                    if w >= n_split:
                        return ref.at[slot]
                    hr = ref.shape[1] // 2
                    return ref.at[slot, pl.ds(c * hr, hr)]
                k = 3 * w + j
                mine.append(_remote(part(me), part(me), send.at[k], recv.at[k], (cx, cy, c)))
                theirs.append(_remote(part(2 * cx + cy), part(2 * cx + cy), send.at[k], recv.at[k], (cx, cy, c)))
        return mine, theirs
    return plan


def plan_gather_pass(refs, send, recv):
    x, y, c, chips = _place()
    mine, theirs = [], []
    for w, ref in enumerate(refs):
        hr = ref.shape[1] // 2
        for j, (cx, cy) in enumerate(chips):
            half = lambda core: ref.at[2 * cx + cy, pl.ds(core * hr, hr)]
            k = 3 * w + j
            mine.append(_remote(half(c), half(c), send.at[k], recv.at[k], (x, y, 1 - c)))
            theirs.append(_remote(half(1 - c), half(1 - c), send.at[k], recv.at[k], (x, y, 1 - c)))
    return mine, theirs


def plan_swap(refs, send, recv):
    x, y, c, _ = _place()
    nw = len(refs) // 2
    mine = []
    for w in range(nw):
        hr = refs[w].shape[1] // 2
        mine.append(_remote(refs[w].at[:, pl.ds((1 - c) * hr, hr), :], refs[nw + w], send.at[w], recv.at[w],
                            (x, y, 1 - c)))
    return mine, mine


def plan_scatter(refs, send, recv):
    x, y, c, chips = _place()
    me = 2 * x + y
    nw = len(refs) // 2
    mine, theirs = [], []
    for w in range(nw):
        for j, (cx, cy) in enumerate(chips):
            k = 3 * w + j
            mine.append(_remote(refs[w].at[2 * cx + cy], refs[nw + w].at[me], send.at[k], recv.at[k], (cx, cy, c)))
            theirs.append(_remote(refs[w].at[me], refs[nw + w].at[2 * cx + cy], send.at[k], recv.at[k], (cx, cy, c)))
    return mine, theirs


def plan_join(refs, send, recv):
    x, y, c, _ = _place()
    mine, theirs = [], []
    for w, ref in enumerate(refs):
        hr = ref.shape[0] // 2
        half = lambda core: ref.at[pl.ds(core * hr, hr)]
        mine.append(_remote(half(c), half(c), send.at[w], recv.at[w], (x, y, 1 - c)))
        theirs.append(_remote(half(1 - c), half(1 - c), send.at[w], recv.at[w], (x, y, 1 - c)))
    return mine, theirs


def exchange(name, plan, arrays, n):
    na = len(arrays)

    def body(*refs):
        mine, theirs = plan(refs[:na], refs[2 * na], refs[2 * na + 1])
        for cp in mine:
            cp.start()
        for cp in theirs:
            cp.wait_recv()
        for cp in mine:
            cp.wait_send()

    return _call(body, name=name, in_specs=[_hbm()] * na, out_specs=[_hbm()] * na,
                 out_shape=[jax.ShapeDtypeStruct(a.shape, a.dtype) for a in arrays],
                 input_output_aliases={i: i for i in range(na)},
                 scratch_shapes=[pltpu.SemaphoreType.DMA((n,)), pltpu.SemaphoreType.DMA((n,))])(*arrays)


def exchange_start(name, plan, arrays, n, after=()):
    na = len(arrays)

    def body(*refs):
        mine, _ = plan(refs[:na], refs[na], refs[na + 1])
        for cp in mine:
            cp.start()
        refs[-1][...] = jnp.zeros_like(refs[-1])

    hbm = pl.BlockSpec(memory_space=pltpu.HBM)
    sem = pl.BlockSpec(memory_space=pltpu.SEMAPHORE)
    return _call_after(body, after, name=name, in_specs=[hbm] * na,
                       out_specs=[sem, sem] + [hbm] * na + [pl.BlockSpec(memory_space=pltpu.VMEM)],
                       out_shape=[pltpu.SemaphoreType.DMA((n,)), pltpu.SemaphoreType.DMA((n,))]
                       + [pltpu.HBM(a.shape, a.dtype) for a in arrays]
                       + [jax.ShapeDtypeStruct((SUBLANES, LANES), F32)],
                       input_output_aliases={i: i + 2 for i in range(na)},
                       compiler_params=pltpu.CompilerParams(
                           has_side_effects=pltpu.SideEffectType.DATAFLOW_SIDE_EFFECTING))(
                               *[pltpu.with_memory_space_constraint(a, pltpu.HBM) for a in arrays])


def exchange_wait(name, plan, handle, after):
    send_sems, recv_sems, arrays = handle[0], handle[1], handle[2:-1]
    na = len(arrays)

    def body(*refs):
        mine, theirs = plan(refs[:na], refs[na], refs[na + 1])
        for cp in mine:
            cp.wait_send()
        for cp in theirs:
            cp.wait_recv()

    hbm = pl.BlockSpec(memory_space=pltpu.HBM)
    sem = pl.BlockSpec(memory_space=pltpu.SEMAPHORE)
    return _call(body, name=name, in_specs=[hbm] * na + [sem, sem, _hbm()], out_specs=[hbm] * na,
                 out_shape=[pltpu.HBM(a.shape, a.dtype) for a in arrays],
                 input_output_aliases={i: i for i in range(na)},
                 compiler_params=pltpu.CompilerParams(
                     has_side_effects=pltpu.SideEffectType.DATAFLOW_SIDE_EFFECTING))(
                         *arrays, send_sems, recv_sems, after)


def allgather_small(small):
    rows, width = small.shape

    def body(in_ref, out_ref, send, recv, loc):
        x, y, c, _ = _place()
        me = 4 * x + 2 * y + c
        local = pltpu.make_async_copy(in_ref, out_ref.at[me], loc)
        local.start()
        flips = [(fx, fy, fc) for fx in (0, 1) for fy in (0, 1) for fc in (0, 1)][1:]

        def cp(k, slot):
            fx, fy, fc = flips[k]
            return pltpu.make_async_remote_copy(
                src_ref=in_ref, dst_ref=out_ref.at[slot], send_sem=send.at[k], recv_sem=recv.at[k],
                device_id=(x ^ fx, y ^ fy, c ^ fc), device_id_type=MESH)

        for k in range(7):
            cp(k, me).start()
        for k in range(7):
            fx, fy, fc = flips[k]
            cp(k, 4 * (x ^ fx) + 2 * (y ^ fy) + (c ^ fc)).wait_recv()
        for k in range(7):
            cp(k, me).wait_send()
        local.wait()

    return _call(body, name="allgather_small",
                 in_specs=[pl.BlockSpec(memory_space=pltpu.VMEM)],
                 out_specs=pl.BlockSpec(memory_space=pltpu.VMEM),
                 out_shape=jax.ShapeDtypeStruct((8, rows, width), F32),
                 scratch_shapes=[pltpu.SemaphoreType.DMA((7,)), pltpu.SemaphoreType.DMA((7,)),
                                 pltpu.SemaphoreType.DMA])(small)


def add_sibling(grad, got, core, name):
    _, r, c = grad.shape
    hr = r // 2
    tr = _tile(hr, 128)
    nblk = hr // tr

    def body(core_ref, g_ref, o_ref, out_ref):
        out_ref[...] = (g_ref[...].astype(F32) + o_ref[...].astype(F32)).astype(BF16)

    grid_spec = pltpu.PrefetchScalarGridSpec(
        num_scalar_prefetch=1, grid=(N_CHIPS, nblk),
        in_specs=[pl.BlockSpec((None, tr, c), lambda t, i, core_ref: (t, core_ref[0] * nblk + i, 0)),
                  pl.BlockSpec((None, tr, c), lambda t, i, core_ref: (t, i, 0))],
        out_specs=pl.BlockSpec((None, tr, c), lambda t, i, core_ref: (t, i, 0)))
    return _call(body, name=name, grid_spec=grid_spec,
                 out_shape=jax.ShapeDtypeStruct((N_CHIPS, hr, c), BF16),
                 compiler_params=_params(("parallel", "parallel")))(core, grad, got)


def sum_chips(mine, owned, place, name):
    _, hr, c = mine.shape
    tr = _tile(hr, 128)
    nblk = hr // tr

    def body(place_ref, m_ref, o1_ref, o2_ref, o3_ref, out_ref):
        acc = m_ref[...].astype(F32)
        for o_ref in (o1_ref, o2_ref, o3_ref):
            acc = acc + o_ref[...].astype(F32)
        out_ref[...] = acc

    other = lambda k: pl.BlockSpec((None, tr, c), lambda i, place_ref: ((place_ref[0] + k) % N_CHIPS, i, 0))
    grid_spec = pltpu.PrefetchScalarGridSpec(
        num_scalar_prefetch=1, grid=(nblk,),
        in_specs=[other(0), other(1), other(2), other(3)],
        out_specs=pl.BlockSpec((tr, c), lambda i, place_ref: (place_ref[1] * nblk + i, 0)))
    return _call(body, name=name, grid_spec=grid_spec,
                 out_shape=jax.ShapeDtypeStruct((2 * hr, c), F32),
                 compiler_params=_params(("parallel",)))(place, mine, owned, owned, owned)


def sum_devices(gathered):
    _, rows, width = gathered.shape

    def body(g_ref, out_ref):
        acc = g_ref[0]
        for dev in range(1, 8):
            acc = acc + g_ref[dev]
        out_ref[...] = acc
        tail = acc[SUBLANES:]
        out_ref[SUBLANES:, :] = jnp.broadcast_to(jnp.sum(tail, axis=1, keepdims=True), tail.shape)

    return _call(body, name="sum_devices",
                 in_specs=[pl.BlockSpec(memory_space=pltpu.VMEM)],
                 out_specs=pl.BlockSpec(memory_space=pltpu.VMEM),
                 out_shape=jax.ShapeDtypeStruct((rows, width), F32))(gathered)


def _rope_tables(s):
    half = ROT_DIM // 2
    inv_freq = jnp.power(jnp.float32(ROPE_THETA), -jnp.arange(half, dtype=F32) * 2.0 / ROT_DIM)
    ang = jnp.arange(s).astype(F32)[:, None] * inv_freq[None, :]
    cos, sin = jnp.cos(ang), jnp.sin(ang)
    zeros = lambda n: jnp.zeros((s, n), F32)
    cos64 = jnp.concatenate([cos, cos, jnp.ones((s, HEAD_DIM - ROT_DIM), F32)], axis=1)
    sa64 = jnp.concatenate([-sin, zeros(HEAD_DIM - half)], axis=1)
    sb64 = jnp.concatenate([zeros(half), sin, zeros(HEAD_DIM - ROT_DIM)], axis=1)
    return tuple(jnp.concatenate([t, t], axis=1) for t in (cos64, sa64, sb64))


def _pad_rows(a, rows):
    return jnp.pad(a, ((0, rows - a.shape[0]), (0, 0)))


def _pad_cols(a, cols):
    return jnp.pad(a, ((0, 0), (0, cols - a.shape[1])))


def kernel(x, p, norm_gain, w_in, q_norm_gain, k_norm_gain, attn_sinks, conv_w, w_out, ple_gate_norm_gain, w_ple_gate, b_ple_gate, w_ple_proj, ple_norm_gain, loss_target, m_norm_gain, m_w_in, m_q_norm_gain, m_k_norm_gain, m_attn_sinks, m_conv_w, m_w_out, m_ple_gate_norm_gain, m_w_ple_gate, m_b_ple_gate, m_w_ple_proj, m_ple_norm_gain, v_norm_gain, v_w_in, v_q_norm_gain, v_k_norm_gain, v_attn_sinks, v_conv_w, v_w_out, v_ple_gate_norm_gain, v_w_ple_gate, v_b_ple_gate, v_w_ple_proj, v_ple_norm_gain):
    x2, p2, tgt = x[0], p[0, 0], loss_target[0]
    s, d = x2.shape
    ple = p2.shape[1]
    aw = d // 2
    cw = d - aw
    nq = aw // HEAD_DIM
    sh = w_in.shape[2]
    in_w = N_CHIPS * sh
    dq = d // N_CHIPS
    cq = cw // N_CHIPS
    ga_off = (aw + 2 * KV_WIDTH) // COLT
    b_off = (2 * aw + 2 * KV_WIDTH) // COLT
    assert in_w == 2 * aw + 2 * KV_WIDTH + 4 * cw and aw % COLT == 0 and cw % COLT == 0
    assert s % BLOCK == 0 and sh % LANES == 0 and nq % (2 * N_KV_HEADS) == 0

    core = lax.axis_index("c").astype(jnp.int32).reshape(1)
    chip = 2 * lax.axis_index("x") + lax.axis_index("y")

    chip1 = chip.astype(jnp.int32).reshape(1)
    place = jnp.concatenate([chip1, core])
    bufs = [cast_into_slot(w_in[0], chip1, "cast_w_in"), cast_into_slot(w_out[0], chip1, "cast_w_out"),
            cast_into_slot(w_ple_gate[0], chip1, "cast_w_pg"), cast_into_slot(w_ple_proj[0], chip1, "cast_w_pp")]
    conv_buf = lax.dynamic_update_slice(jnp.zeros((N_CHIPS, SUBLANES, cq), F32),
                                        _pad_rows(conv_w[0], SUBLANES)[None], (chip, 0, 0))
    ici_wi = exchange_start("gather_wi_start", plan_gather_ici(1), bufs[:1], 3)
    ici_rest = exchange_start("gather_rest_start", plan_gather_ici(3), bufs[1:] + [conv_buf], 12,
                              after=(ici_wi[-1],))

    tm = _tile(s, 1024)
    tn = _tile(d, 1024)
    tk = _tile(d, 1024)
    ts = _tile(s, 1024)
    h = rms_fwd(x2, norm_gain, "rms_fwd_x", after=(ici_rest[-1],))
    tabs = _rope_tables(s)
    qg = jnp.tile(q_norm_gain, (1, LANES // HEAD_DIM))
    kg = jnp.tile(k_norm_gain, (1, LANES // HEAD_DIM))
    seg_i = jnp.arange(SEG) // HEAD_DIM
    segb = (seg_i[:, None] == seg_i[None, :]).astype(BF16)
    wi_all, = exchange_wait("gather_wi_wait", plan_gather_ici(1), ici_wi, h)
    wi_all, = exchange("gather_wi_pass", plan_gather_pass, [wi_all], 3)
    z = matmul(h, wi_all, grid=(s // tm, N_CHIPS, d // tk),
               a_spec=pl.BlockSpec((tm, tk), lambda i, j, k: (i, k)),
               b_spec=pl.BlockSpec((None, tk, sh), lambda i, j, k: (j, k, 0)),
               o_spec=pl.BlockSpec((tm, sh), lambda i, j, k: (i, j)),
               out_shape=jax.ShapeDtypeStruct((s, in_w), F32), dims=NN, name="mm_z")
    wo_all, wg_all, wp_all, conv_all = exchange_wait("gather_rest_wait", plan_gather_ici(3), ici_rest, z)
    wo_all, wg_all, wp_all = exchange("gather_rest_pass", plan_gather_pass, [wo_all, wg_all, wp_all], 9)
    wo_full = wo_all.reshape(d, d)
    wg_full = wg_all.reshape(d, d)
    conv_full = conv_all.transpose(1, 0, 2).reshape(SUBLANES, cw)
    qs, ks, vb = qk_prep_fwd(z, tabs, qg, kg, segb, aw)
    attn, mix = attn_fwd(qs, ks, vb, z, attn_sinks, aw, d, ga_off)
    mix = conv_fwd(z, conv_full, mix, aw, cw, b_off)
    sq = lambda shape: dict(
        a_spec=pl.BlockSpec((tm, tk), lambda i, j, k: (i, k)),
        o_spec=pl.BlockSpec((tm, tn), lambda i, j, k: (i, j)),
        out_shape=jax.ShapeDtypeStruct(shape, F32))
    x1 = matmul(mix, wo_full, grid=(s // tm, d // tn, d // tk),
                b_spec=pl.BlockSpec((tk, tn), lambda i, j, k: (k, j)), dims=NN, name="mm_x1",
                res=x2, res_spec=pl.BlockSpec((tm, tn), lambda i, j, k: (i, j)), **sq((s, d)))
    hg = rms_fwd(x1, ple_gate_norm_gain, "rms_fwd_x1")
    gl = matmul(hg, wg_full, grid=(s // tm, d // tn, d // tk),
                b_spec=pl.BlockSpec((tk, tn), lambda i, j, k: (k, j)), dims=NN, name="mm_gl", **sq((s, d)))
    pq = d // N_CHIPS
    pe = matmul(p2, wp_all, grid=(s // tm, N_CHIPS, 1),
                a_spec=pl.BlockSpec((tm, ple), lambda i, j, k: (i, 0)),
                b_spec=pl.BlockSpec((None, ple, pq), lambda i, j, k: (j, 0, 0)),
                o_spec=pl.BlockSpec((tm, pq), lambda i, j, k: (i, j)),
                out_shape=jax.ShapeDtypeStruct((s, d), F32), dims=NN, name="mm_pe")

    d_gl, d_pe, dy, acc_head = head_fwd_bwd(x1, gl, pe, tgt, b_ple_gate, ple_norm_gain)
    d_hg = matmul(d_gl, wg_full, grid=(s // tm, d // tn, d // tk),
                  b_spec=pl.BlockSpec((tn, tk), lambda i, j, k: (j, k)), dims=NT, name="mm_d_hg", **sq((s, d)))
    wgrad = lambda a_cols, shape3, o_spec, b_cols, name, a, b, grid: matmul(
        a, b, grid=grid,
        a_spec=pl.BlockSpec((ts, a_cols), lambda i, j, k: (k, i)),
        b_spec=pl.BlockSpec((ts, b_cols), lambda i, j, k: (k, j)),
        o_spec=o_spec, out_shape=jax.ShapeDtypeStruct(shape3, BF16), dims=TN, name=name)
    g_wg = wgrad(tn, (d, d), pl.BlockSpec((tn, tn), lambda i, j, k: (i, j)), tn, "mm_g_wg", hg, d_gl,
                 (d // tn, d // tn, s // ts))
    g_wp = wgrad(ple, (N_CHIPS, ple, pq), pl.BlockSpec((None, ple, pq), lambda i, j, k: (j, 0, 0)), pq,
                 "mm_g_wp", p2, d_pe, (1, N_CHIPS, s // ts))
    d_x1, d_x1b, acc_g = rms_bwd_add(d_hg, x1, dy, ple_gate_norm_gain, "rms_bwd_x1", True)
    d_mix = matmul(d_x1b, wo_full, grid=(s // tm, d // tn, d // tk),
                   b_spec=pl.BlockSpec((tn, tk), lambda i, j, k: (j, k)), dims=NT, name="mm_d_mix", **sq((s, d)))
    g_wo = wgrad(tn, (d, d), pl.BlockSpec((tn, tn), lambda i, j, k: (i, j)), tn, "mm_g_wo", mix, d_x1b,
                 (d // tn, d // tn, s // ts))
    d_o, dz = gate_bwd(d_mix, attn, z, aw, ga_off)
    dqs, dks, dvs, acc_sink = attn_bwd(qs, ks, vb, d_o, attn_sinks, aw)
    dz, acc_qk = qk_prep_bwd(z, dqs, dks, dvs, tabs, qg, kg, segb, dz, aw)
    dz, acc_conv = conv_bwd(z, d_mix, conv_full, dz, aw, cw, b_off)
    g_wi = wgrad(tn, (N_CHIPS, d, sh), pl.BlockSpec((None, tn, sh), lambda i, j, k: (j, i, 0)), sh,
                 "mm_g_wi", h, dz, (d // tn, N_CHIPS, s // ts))

    grads = [g_wi, g_wo.reshape(N_CHIPS, dq, d), g_wg.reshape(N_CHIPS, dq, d), g_wp]
    half_like = lambda a: lax.empty((N_CHIPS, a.shape[1] // 2, a.shape[2]), BF16)
    swapped = exchange("swap_halves", plan_swap, grads + [half_like(a) for a in grads], 4)
    grads, got = swapped[:4], swapped[4:]
    names = ("wi", "wo", "wg", "wp")
    parts = [add_sibling(g, o, core, "add_sibling_" + nm) for g, o, nm in zip(grads, got, names)]
    ici_grads = exchange_start("scatter_start", plan_scatter, parts + [lax.empty(a.shape, BF16) for a in parts], 12)
    d_h = matmul(dz, wi_all, grid=(s // tm, d // tn, N_CHIPS),
                 a_spec=pl.BlockSpec((tm, sh), lambda i, j, k: (i, k)),
                 b_spec=pl.BlockSpec((None, tn, sh), lambda i, j, k: (k, j, 0)),
                 o_spec=pl.BlockSpec((tm, tn), lambda i, j, k: (i, j)),
                 out_shape=jax.ShapeDtypeStruct((s, d), F32), dims=NT, name="mm_d_h", after=(ici_grads[-1],))
    grad_x, acc_x = rms_bwd_add(d_h, x2, d_x1, norm_gain, "rms_bwd_x", False)
    scattered = exchange_wait("scatter_wait", plan_scatter, ici_grads, grad_x)
    parts, owned = scattered[:4], scattered[4:]
    halves = [sum_chips(pt, o, place, "sum_chips_" + nm) for pt, o, nm in zip(parts, owned, names)]
    full = exchange("join_halves", plan_join, halves, 4)
    big = {}
    for nm, g, w, m, v in (("w_in", full[0], w_in, m_w_in, v_w_in), ("w_out", full[1], w_out, m_w_out, v_w_out),
                           ("w_ple_gate", full[2], w_ple_gate, m_w_ple_gate, v_w_ple_gate),
                           ("w_ple_proj", full[3], w_ple_proj, m_w_ple_proj, v_w_ple_proj)):
        big[nm] = [o[None] for o in adamw(g, w[0], m[0], v[0], "adamw_" + nm)]

    wsm = max(d, cw)
    misc = jnp.concatenate([acc_qk[0:1, :HEAD_DIM], acc_qk[1:2, :HEAD_DIM], acc_sink[0:1, :nq]], axis=1)
    small = jnp.concatenate([
        _pad_cols(acc_x[0:1], wsm), _pad_cols(acc_g[0:1], wsm), _pad_cols(acc_head[0:1], wsm),
        _pad_cols(acc_head[1:2], wsm), _pad_cols(misc, wsm), _pad_cols(acc_conv[0:3], wsm)], axis=0)
    both = jnp.concatenate([small, _pad_rows(_pad_cols(acc_head[2:3], wsm), SUBLANES)], axis=0)
    tot = sum_devices(allgather_small(both))
    loss = tot[SUBLANES, 0]

    def pack(vals):
        ng, pg, bg, eg, qgv, kgv, sk, cv = vals
        misc_v = jnp.concatenate([qgv, kgv, sk], axis=1)
        return jnp.concatenate([_pad_cols(ng, wsm), _pad_cols(pg, wsm), _pad_cols(bg, wsm), _pad_cols(eg, wsm),
                                _pad_cols(misc_v, wsm), _pad_cols(cv[0], wsm)], axis=0)

    g_small = jnp.concatenate(
        [tot[0:5], _pad_cols(lax.dynamic_slice(tot[5:8], (0, chip * cq), (3, cq)), wsm)], axis=0)
    w_small = pack((norm_gain, ple_gate_norm_gain, b_ple_gate, ple_norm_gain, q_norm_gain, k_norm_gain,
                    attn_sinks, conv_w))
    m_small = pack((m_norm_gain, m_ple_gate_norm_gain, m_b_ple_gate, m_ple_norm_gain, m_q_norm_gain,
                    m_k_norm_gain, m_attn_sinks, m_conv_w))
    v_small = pack((v_norm_gain, v_ple_gate_norm_gain, v_b_ple_gate, v_ple_norm_gain, v_q_norm_gain,
                    v_k_norm_gain, v_attn_sinks, v_conv_w))
    sm = adamw(g_small, w_small, m_small, v_small, "adamw_small")

    def unpack(a):
        return {"norm_gain": a[0:1, :d], "ple_gate_norm_gain": a[1:2, :d], "b_ple_gate": a[2:3, :d],
                "ple_norm_gain": a[3:4, :d], "q_norm_gain": a[4:5, :HEAD_DIM],
                "k_norm_gain": a[4:5, HEAD_DIM:2 * HEAD_DIM],
                "attn_sinks": a[4:5, 2 * HEAD_DIM:2 * HEAD_DIM + nq], "conv_w": a[5:8, :cq][None]}

    order = ("norm_gain", "w_in", "q_norm_gain", "k_norm_gain", "attn_sinks", "conv_w", "w_out",
             "ple_gate_norm_gain", "w_ple_gate", "b_ple_gate", "w_ple_proj", "ple_norm_gain")
    outs = [loss, grad_x[None]]
    for kind in range(4):
        table = unpack(sm[kind])
        for nm in order:
            outs.append(big[nm][kind] if nm in big else table[nm])
    return tuple(outs)
```

```python
import functools

import jax
import jax.numpy as jnp
from jax import lax
from jax.experimental import pallas as pl
from jax.experimental.pallas import tpu as pltpu

F32 = jnp.float32
BF16 = jnp.bfloat16
MESH = pl.DeviceIdType.MESH

HEAD_DIM = 64
N_KV_HEADS = 4
KV_WIDTH = N_KV_HEADS * HEAD_DIM
BLOCK = 128
ROT_DIM = 16
ROPE_THETA = 500000.0
EPS = 1e-6
NEG_INF = -1e30
N_CHIPS = 4
LANES = 128
SUBLANES = 8
COLT = 512
SEG = 256
VMEM_LIMIT = 48 * 1024 * 1024

ADAM_LR = 0.001
ADAM_B1 = 0.9
ADAM_B2 = 0.999
ADAM_EPS = 1e-08
ADAM_WD = 0.01
ADAM_STEP = 10

NN = (((1,), (0,)), ((), ()))
NT = (((1,), (1,)), ((), ()))
TN = (((0,), (0,)), ((), ()))


def _call(body, **kw):
    return pl.pallas_call(body, **kw)


def _call_after(body, after, **kw):
    n_in, n_after = len(kw["in_specs"]), len(after)
    kw["in_specs"] = list(kw["in_specs"]) + [pl.BlockSpec(memory_space=pl.ANY)] * n_after

    def body_after(*refs):
        body(*refs[:n_in], *refs[n_in + n_after:])

    call = _call(body_after, **kw)
    return lambda *args: call(*args, *after)


def _params(sem):
    return pltpu.CompilerParams(dimension_semantics=sem, vmem_limit_bytes=VMEM_LIMIT)


def _tile(n, pref):
    return pref if n % pref == 0 else n


def _sigmoid(v):
    return 1.0 / (1.0 + jnp.exp(-v))


def _hbm():
    return pl.BlockSpec(memory_space=pl.ANY)


def cast_into_slot(a, chip, name):
    r, c = a.shape
    tr = _tile(r, 256)

    def body(chip_ref, a_ref, o_ref):
        o_ref[...] = a_ref[...].astype(BF16)

    grid_spec = pltpu.PrefetchScalarGridSpec(
        num_scalar_prefetch=1, grid=(r // tr,),
        in_specs=[pl.BlockSpec((tr, c), lambda i, chip_ref: (i, 0))],
        out_specs=pl.BlockSpec((None, tr, c), lambda i, chip_ref: (chip_ref[0], i, 0)))
    return _call(body, name=name, grid_spec=grid_spec,
                 out_shape=jax.ShapeDtypeStruct((N_CHIPS, r, c), BF16),
                 compiler_params=_params(("parallel",)))(chip, a)


def rms_fwd(x, gain, name, after=()):
    s, d = x.shape
    tm = _tile(s, 256)

    def body(x_ref, g_ref, o_ref):
        xf = x_ref[...]
        r = lax.rsqrt(jnp.mean(xf * xf, axis=-1, keepdims=True) + EPS)
        o_ref[...] = ((xf * r) * g_ref[...]).astype(BF16)

    return _call_after(body, after, name=name, grid=(s // tm,),
                       in_specs=[pl.BlockSpec((tm, d), lambda i: (i, 0)),
                                 pl.BlockSpec((1, d), lambda i: (0, 0))],
                       out_specs=pl.BlockSpec((tm, d), lambda i: (i, 0)),
                       out_shape=jax.ShapeDtypeStruct((s, d), BF16),
                       compiler_params=_params(("parallel",)))(x, gain)


def rms_bwd_add(dyn, xin, add, gain, name, want_bf16):
    s, d = xin.shape
    tm = _tile(s, 256)

    def body(dy_ref, x_ref, a_ref, g_ref, *outs):
        i = pl.program_id(0)
        dx_ref, acc_ref = outs[0], outs[-1]
        xf = x_ref[...]
        r = lax.rsqrt(jnp.mean(xf * xf, axis=-1, keepdims=True) + EPS)
        xhat = xf * r
        dyv = dy_ref[...]
        gd = dyv * g_ref[...]
        dx = a_ref[...] + r * (gd - xhat * jnp.mean(xhat * gd, axis=-1, keepdims=True))
        dx_ref[...] = dx
        if want_bf16:
            outs[1][...] = dx.astype(BF16)

        @pl.when(i == 0)
        def _():
            acc_ref[...] = jnp.zeros_like(acc_ref)

        acc_ref[0:1, :] += jnp.sum(dyv * xhat, axis=0, keepdims=True)

    row = pl.BlockSpec((tm, d), lambda i: (i, 0))
    out_specs = [row] + ([row] if want_bf16 else []) + [pl.BlockSpec((SUBLANES, d), lambda i: (0, 0))]
    out_shape = ([jax.ShapeDtypeStruct((s, d), F32)]
                 + ([jax.ShapeDtypeStruct((s, d), BF16)] if want_bf16 else [])
                 + [jax.ShapeDtypeStruct((SUBLANES, d), F32)])
    return _call(body, name=name, grid=(s // tm,),
                 in_specs=[row, row, row, pl.BlockSpec((1, d), lambda i: (0, 0))],
                 out_specs=out_specs, out_shape=out_shape,
                 compiler_params=_params(("arbitrary",)))(dyn, xin, add, gain)


def head_fwd_bwd(x1, gl, pe, tgt, bias, ple_gain):
    s, d = x1.shape
    tm = _tile(s, 128)

    def body(x1_ref, gl_ref, pe_ref, t_ref, b_ref, g_ref, dgl_ref, dpe_ref, dy_ref, acc_ref):
        i = pl.program_id(0)
        gate = _sigmoid(gl_ref[...] + b_ref[...])
        pev = pe_ref[...]
        r = lax.rsqrt(jnp.mean(pev * pev, axis=-1, keepdims=True) + EPS)
        pehat = pev * r
        gain = g_ref[...]
        e = pehat * gain
        diff = (x1_ref[...] + gate * e) - t_ref[...]
        dy = diff * (1.0 / d)
        dy_ref[...] = dy
        d_gl = (dy * e) * (gate * (1.0 - gate))
        dgl_ref[...] = d_gl.astype(BF16)
        d_e = dy * gate
        gd = d_e * gain
        d_pe = r * (gd - pehat * jnp.mean(pehat * gd, axis=-1, keepdims=True))
        dpe_ref[...] = d_pe.astype(BF16)

        @pl.when(i == 0)
        def _():
            acc_ref[...] = jnp.zeros_like(acc_ref)

        acc_ref[0:1, :] += jnp.sum(d_gl, axis=0, keepdims=True)
        acc_ref[1:2, :] += jnp.sum(d_e * pehat, axis=0, keepdims=True)
        acc_ref[2:3, :] += jnp.sum(diff * diff, axis=0, keepdims=True) * (0.5 / d)

    row = pl.BlockSpec((tm, d), lambda i: (i, 0))
    vec = pl.BlockSpec((1, d), lambda i: (0, 0))
    return _call(body, name="head_fwd_bwd", grid=(s // tm,),
                 in_specs=[row, row, row, row, vec, vec],
                 out_specs=[row, row, row, pl.BlockSpec((SUBLANES, d), lambda i: (0, 0))],
                 out_shape=[jax.ShapeDtypeStruct((s, d), BF16), jax.ShapeDtypeStruct((s, d), BF16),
                            jax.ShapeDtypeStruct((s, d), F32), jax.ShapeDtypeStruct((SUBLANES, d), F32)],
                 compiler_params=_params(("arbitrary",)))(x1, gl, pe, tgt, bias, ple_gain)


def adamw(g, w, m, v, name):
    r, c = g.shape
    tr = _tile(r, 128)

    def body(g_ref, w_ref, m_ref, v_ref, go_ref, d_ref, mo_ref, vo_ref):
        gv = g_ref[...]
        mn = ADAM_B1 * m_ref[...] + (1.0 - ADAM_B1) * gv
        vn = ADAM_B2 * v_ref[...] + (1.0 - ADAM_B2) * (gv * gv)
        m_hat = mn / (1.0 - ADAM_B1 ** ADAM_STEP)
        v_hat = vn / (1.0 - ADAM_B2 ** ADAM_STEP)
        go_ref[...] = gv
        d_ref[...] = -ADAM_LR * (m_hat / (jnp.sqrt(v_hat) + ADAM_EPS) + ADAM_WD * w_ref[...])
        mo_ref[...] = mn
        vo_ref[...] = vn

    blk = pl.BlockSpec((tr, c), lambda i: (i, 0))
    shp = jax.ShapeDtypeStruct((r, c), F32)
    return _call(body, name=name, grid=(r // tr,), in_specs=[blk] * 4, out_specs=[blk] * 4,
                 out_shape=[shp] * 4, compiler_params=_params(("parallel",)))(g, w, m, v)


def matmul(a, b, *, grid, a_spec, b_spec, o_spec, out_shape, dims, name, res=None, res_spec=None, after=()):
    nk = grid[2]
    acc_shape = tuple(d for d in o_spec.block_shape if d is not None)

    def body(*refs):
        if res is None:
            a_ref, b_ref, o_ref, acc_ref = refs
        else:
            a_ref, b_ref, r_ref, o_ref, acc_ref = refs
        k = pl.program_id(2)

        @pl.when(k == 0)
        def _():
            acc_ref[...] = jnp.zeros_like(acc_ref)

        acc_ref[...] += lax.dot_general(a_ref[...].astype(BF16), b_ref[...].astype(BF16), dims,
                                        preferred_element_type=F32)

        @pl.when(k == nk - 1)
        def _():
            out = acc_ref[...]
            if res is not None:
                out = r_ref[...] + out
            o_ref[...] = out.astype(o_ref.dtype)

    in_specs = [a_spec, b_spec] + ([res_spec] if res is not None else [])
    args = (a, b) + ((res,) if res is not None else ())
    return _call_after(body, after, name=name, grid=grid, in_specs=in_specs, out_specs=o_spec,
                       out_shape=out_shape, scratch_shapes=[pltpu.VMEM(acc_shape, F32)],
                       compiler_params=_params(("parallel", "parallel", "arbitrary")))(*args)


def _seg_mean(sq, segb):
    parts = []
    for cgrp in range(sq.shape[1] // SEG):
        blk = sq[:, cgrp * SEG:(cgrp + 1) * SEG]
        hi = blk.astype(BF16)
        r1 = blk - hi.astype(F32)
        mid = r1.astype(BF16)
        lo = (r1 - mid.astype(F32)).astype(BF16)
        acc = jnp.dot(hi, segb, preferred_element_type=F32)
        acc += jnp.dot(mid, segb, preferred_element_type=F32)
        acc += jnp.dot(lo, segb, preferred_element_type=F32)
        parts.append(acc)
    out = parts[0] if len(parts) == 1 else jnp.concatenate(parts, axis=1)
    return out * (1.0 / HEAD_DIM)


def _rope(v, cos, sa, sb):
    parts = []
    for cgrp in range(v.shape[1] // LANES):
        blk = v[:, cgrp * LANES:(cgrp + 1) * LANES]
        parts.append(blk * cos + pltpu.roll(blk, LANES - 8, 1) * sa + pltpu.roll(blk, 8, 1) * sb)
    return parts[0] if len(parts) == 1 else jnp.concatenate(parts, axis=1)


def _rope_t(dv, cos, sa, sb):
    parts = []
    for cgrp in range(dv.shape[1] // LANES):
        blk = dv[:, cgrp * LANES:(cgrp + 1) * LANES]
        parts.append(blk * cos + pltpu.roll(blk * sa, 8, 1) + pltpu.roll(blk * sb, LANES - 8, 1))
    return parts[0] if len(parts) == 1 else jnp.concatenate(parts, axis=1)


def _tile_lanes(vec, width):
    reps = width // LANES
    return vec if reps == 1 else jnp.tile(vec, (1, reps))


def qk_prep_fwd(z, tabs, qg, kg, segb, aw):
    s = z.shape[0]
    tm = _tile(s, 256)
    wq = aw + 2 * KV_WIDTH

    def body(z_ref, cos_ref, sa_ref, sb_ref, qg_ref, kg_ref, seg_ref, qs_ref, ks_ref, vb_ref):
        zz = z_ref[...]
        q, k, v = zz[:, :aw], zz[:, aw:aw + KV_WIDTH], zz[:, aw + KV_WIDTH:]
        cos, sa, sb, segm = cos_ref[...], sa_ref[...], sb_ref[...], seg_ref[...]
        rq = lax.rsqrt(_seg_mean(q * q, segm) + EPS)
        qn = (q * rq) * _tile_lanes(qg_ref[...], aw)
        qs_ref[...] = (_rope(qn, cos, sa, sb) * (HEAD_DIM ** -0.5)).astype(BF16)
        rk = lax.rsqrt(_seg_mean(k * k, segm) + EPS)
        kn = (k * rk) * _tile_lanes(kg_ref[...], KV_WIDTH)
        ks_ref[...] = _rope(kn, cos, sa, sb).astype(BF16)
        vb_ref[...] = v.astype(BF16)

    tab = pl.BlockSpec((tm, LANES), lambda i: (i, 0))
    vec = pl.BlockSpec((1, LANES), lambda i: (0, 0))
    return _call(body, name="qk_prep_fwd", grid=(s // tm,),
                 in_specs=[pl.BlockSpec((tm, wq), lambda i: (i, 0)), tab, tab, tab, vec, vec,
                           pl.BlockSpec((SEG, SEG), lambda i: (0, 0))],
                 out_specs=[pl.BlockSpec((tm, aw), lambda i: (i, 0)),
                            pl.BlockSpec((tm, KV_WIDTH), lambda i: (i, 0)),
                            pl.BlockSpec((tm, KV_WIDTH), lambda i: (i, 0))],
                 out_shape=[jax.ShapeDtypeStruct((s, aw), BF16), jax.ShapeDtypeStruct((s, KV_WIDTH), BF16),
                            jax.ShapeDtypeStruct((s, KV_WIDTH), BF16)],
                 compiler_params=_params(("parallel",)))(z, *tabs, qg, kg, segb)


def qk_prep_bwd(z, dqs, dks, dvs, tabs, qg, kg, segb, dz, aw):
    s = z.shape[0]
    tm = _tile(s, 256)
    wq = aw + 2 * KV_WIDTH

    def body(z_ref, dq_ref, dk_ref, dv_ref, cos_ref, sa_ref, sb_ref, qg_ref, kg_ref, seg_ref, dz_in,
             dz_ref, acc_ref):
        i = pl.program_id(0)
        zz = z_ref[...]
        q, k = zz[:, :aw], zz[:, aw:aw + KV_WIDTH]
        cos, sa, sb, segm = cos_ref[...], sa_ref[...], sb_ref[...], seg_ref[...]

        def one(xv, dout, gvec, width):
            g = _tile_lanes(gvec, width)
            r = lax.rsqrt(_seg_mean(xv * xv, segm) + EPS)
            xhat = xv * r
            dn = _rope_t(dout, cos, sa, sb)
            gd = dn * g
            dx = r * (gd - xhat * _seg_mean(xhat * gd, segm))
            contrib = jnp.sum(dn * xhat, axis=0, keepdims=True)
            folded = contrib[:, :LANES]
            for cgrp in range(1, width // LANES):
                folded = folded + contrib[:, cgrp * LANES:(cgrp + 1) * LANES]
            return dx, folded + pltpu.roll(folded, HEAD_DIM, 1)

        dq, gq = one(q, dq_ref[...] * (HEAD_DIM ** -0.5), qg_ref[...], aw)
        dk, gk = one(k, dk_ref[...], kg_ref[...], KV_WIDTH)
        dz_ref[:, :aw] = dq.astype(BF16)
        dz_ref[:, aw:aw + KV_WIDTH] = dk.astype(BF16)
        dz_ref[:, aw + KV_WIDTH:] = dv_ref[...].astype(BF16)

        @pl.when(i == 0)
        def _():
            acc_ref[...] = jnp.zeros_like(acc_ref)

        acc_ref[0:1, :] += gq
        acc_ref[1:2, :] += gk

    tab = pl.BlockSpec((tm, LANES), lambda i: (i, 0))
    vec = pl.BlockSpec((1, LANES), lambda i: (0, 0))
    kvb = pl.BlockSpec((tm, KV_WIDTH), lambda i: (i, 0))
    return _call(body, name="qk_prep_bwd", grid=(s // tm,),
                 in_specs=[pl.BlockSpec((tm, wq), lambda i: (i, 0)),
                           pl.BlockSpec((tm, aw), lambda i: (i, 0)), kvb, kvb, tab, tab, tab, vec, vec,
                           pl.BlockSpec((SEG, SEG), lambda i: (0, 0)), _hbm()],
                 out_specs=[pl.BlockSpec((tm, wq), lambda i: (i, 0)),
                            pl.BlockSpec((SUBLANES, LANES), lambda i: (0, 0))],
                 out_shape=[jax.ShapeDtypeStruct(dz.shape, BF16), jax.ShapeDtypeStruct((SUBLANES, LANES), F32)],
                 input_output_aliases={10: 0},
                 compiler_params=_params(("arbitrary",)))(z, dqs, dks, dvs, *tabs, qg, kg, segb, dz)


def _placed(band, lane_idx):
    out = {}
    for kh in range(N_KV_HEADS):
        grp = band[:, (kh // 2) * LANES:(kh // 2 + 1) * LANES]
        for half in (0, 1):
            t = grp if half == kh % 2 else pltpu.roll(grp, HEAD_DIM, 1)
            keep = (lane_idx >= half * HEAD_DIM) & (lane_idx < (half + 1) * HEAD_DIM)
            out[kh, half] = jnp.where(keep, t, jnp.zeros_like(t))
    return out


def _scores(qgrp, kpl, valid, sink):
    sc = lax.dot_general(qgrp, kpl, NT, preferred_element_type=F32)
    sc = jnp.where(valid, sc, NEG_INF)
    m = jnp.maximum(jnp.max(sc, axis=-1, keepdims=True), sink)
    e = jnp.exp(sc - m)
    es = jnp.exp(sink - m)
    den = jnp.sum(e, axis=-1, keepdims=True) + es
    return e / den, es / den


def _valid_mask(n):
    r_i = lax.broadcasted_iota(jnp.int32, (BLOCK, 2 * BLOCK), 0)
    j_i = lax.broadcasted_iota(jnp.int32, (BLOCK, 2 * BLOCK), 1)
    return (j_i > r_i) & (j_i <= r_i + BLOCK) & ((n > 0) | (j_i >= BLOCK))


def attn_fwd(qs, ks, vb, z, sinks, aw, d, ga_off):
    s = qs.shape[0]
    nb = s // BLOCK
    nq = aw // HEAD_DIM
    grp_sz = nq // N_KV_HEADS
    n_ga = aw // COLT

    def body(q_ref, ko_ref, kp_ref, vo_ref, vp_ref, *rest):
        ga_refs = rest[:n_ga]
        sink_ref, attn_ref, mix_ref = rest[n_ga:]
        n = pl.program_id(0)
        lane_idx = lax.broadcasted_iota(jnp.int32, (2 * BLOCK, LANES), 1)
        kpl = _placed(jnp.concatenate([kp_ref[...], ko_ref[...]], axis=0), lane_idx)
        vpl = _placed(jnp.concatenate([vp_ref[...], vo_ref[...]], axis=0), lane_idx)
        valid = _valid_mask(n)
        for cgrp in range(nq // 2):
            qgrp = q_ref[:, cgrp * LANES:(cgrp + 1) * LANES]
            acc = jnp.zeros((BLOCK, LANES), F32)
            for half in (0, 1):
                h = 2 * cgrp + half
                kh = h // grp_sz
                p, _ = _scores(qgrp, kpl[kh, half], valid, sink_ref[0, h])
                acc += jnp.dot(p.astype(BF16), vpl[kh, half], preferred_element_type=F32)
            attn_ref[:, cgrp * LANES:(cgrp + 1) * LANES] = acc
            col = cgrp * LANES
            ga = ga_refs[col // COLT][:, col % COLT:col % COLT + LANES]
            mix_ref[:, cgrp * LANES:(cgrp + 1) * LANES] = (acc * (ga * _sigmoid(ga))).astype(BF16)

    own = lambda n: (n, 0)
    prev = lambda n: (jnp.maximum(n - 1, 0), 0)
    kvs = lambda imap: pl.BlockSpec((BLOCK, KV_WIDTH), imap)
    ga_specs = [pl.BlockSpec((BLOCK, COLT), functools.partial(lambda n, j: (n, ga_off + j), j=j))
                for j in range(n_ga)]
    return _call(body, name="attn_fwd", grid=(nb,),
                 in_specs=[pl.BlockSpec((BLOCK, aw), own), kvs(own), kvs(prev), kvs(own), kvs(prev)]
                 + ga_specs + [pl.BlockSpec(memory_space=pltpu.SMEM)],
                 out_specs=[pl.BlockSpec((BLOCK, aw), own), pl.BlockSpec((BLOCK, aw), own)],
                 out_shape=[jax.ShapeDtypeStruct((s, aw), F32), jax.ShapeDtypeStruct((s, d), BF16)],
                 compiler_params=_params(("parallel",)))(qs, ks, ks, vb, vb, *([z] * n_ga), sinks)


def gate_bwd(d_mix, attn, z, aw, ga_off, after=()):
    s, in_w = z.shape
    tm = _tile(s, 256)

    def body(dm_ref, at_ref, ga_ref, do_ref, dz_ref):
        ga = ga_ref[...]
        sig = _sigmoid(ga)
        dm = dm_ref[...]
        do_ref[...] = (dm * (ga * sig)).astype(BF16)
        dz_ref[...] = ((dm * at_ref[...]) * (sig * (1.0 + ga * (1.0 - sig)))).astype(BF16)

    blk = pl.BlockSpec((tm, COLT), lambda i, j: (i, j))
    gab = pl.BlockSpec((tm, COLT), lambda i, j: (i, ga_off + j))
    return _call_after(body, after, name="gate_bwd", grid=(s // tm, aw // COLT),
                       in_specs=[blk, blk, gab], out_specs=[blk, gab],
                       out_shape=[jax.ShapeDtypeStruct((s, aw), BF16), jax.ShapeDtypeStruct((s, in_w), BF16)],
                       compiler_params=_params(("parallel", "parallel")))(d_mix, attn, z)


def attn_bwd(qs, ks, vb, d_o, sinks, aw):
    s = qs.shape[0]
    nb = s // BLOCK
    nq = aw // HEAD_DIM
    grp_sz = nq // N_KV_HEADS

    def body(q_ref, ko_ref, kp_ref, vo_ref, vp_ref, do_ref, sink_ref, dq_ref, dk_ref, dv_ref, ds_ref,
             ck_ref, cv_ref):
        n = pl.program_id(0)

        @pl.when(n == 0)
        def _():
            ds_ref[...] = jnp.zeros_like(ds_ref)
            dk_ref[...] = jnp.zeros_like(dk_ref)
            dv_ref[...] = jnp.zeros_like(dv_ref)

        @pl.when(n < nb)
        def _():
            lane_idx = lax.broadcasted_iota(jnp.int32, (2 * BLOCK, LANES), 1)
            lane_row = lax.broadcasted_iota(jnp.int32, (1, LANES), 1)
            kpl = _placed(jnp.concatenate([kp_ref[...], ko_ref[...]], axis=0), lane_idx)
            vpl = _placed(jnp.concatenate([vp_ref[...], vo_ref[...]], axis=0), lane_idx)
            valid = _valid_mask(n)
            dk_acc = [jnp.zeros((2 * BLOCK, LANES), F32) for _ in range(N_KV_HEADS)]
            dv_acc = [jnp.zeros((2 * BLOCK, LANES), F32) for _ in range(N_KV_HEADS)]
            ds_row = jnp.zeros((1, LANES), F32)
            for cgrp in range(nq // 2):
                qgrp = q_ref[:, cgrp * LANES:(cgrp + 1) * LANES]
                dog = do_ref[:, cgrp * LANES:(cgrp + 1) * LANES]
                dq_acc = jnp.zeros((BLOCK, LANES), F32)
                for half in (0, 1):
                    h = 2 * cgrp + half
                    kh = h // grp_sz
                    p, p_sink = _scores(qgrp, kpl[kh, half], valid, sink_ref[0, h])
                    dp = lax.dot_general(dog, vpl[kh, half], NT, preferred_element_type=F32)
                    delta = jnp.sum(p * dp, axis=-1, keepdims=True)
                    dsb = (p * (dp - delta)).astype(BF16)
                    dq_acc += jnp.dot(dsb, kpl[kh, half], preferred_element_type=F32)
                    keep = (lane_idx >= half * HEAD_DIM) & (lane_idx < (half + 1) * HEAD_DIM)
                    dkp = jnp.where(keep, lax.dot_general(dsb, qgrp, TN, preferred_element_type=F32), 0.0)
                    dvp = jnp.where(keep, lax.dot_general(p.astype(BF16), dog, TN, preferred_element_type=F32), 0.0)
                    if half != kh % 2:
                        dkp = pltpu.roll(dkp, HEAD_DIM, 1)
                        dvp = pltpu.roll(dvp, HEAD_DIM, 1)
                    dk_acc[kh] += dkp
                    dv_acc[kh] += dvp
                    dsink = -jnp.sum(p_sink * delta, axis=0, keepdims=True)
                    ds_row += jnp.where(lane_row == h, dsink, 0.0)
                dq_ref[:, cgrp * LANES:(cgrp + 1) * LANES] = dq_acc
            ds_ref[0:1, :] += ds_row
            dk_band = jnp.concatenate([dk_acc[0] + dk_acc[1], dk_acc[2] + dk_acc[3]], axis=1)
            dv_band = jnp.concatenate([dv_acc[0] + dv_acc[1], dv_acc[2] + dv_acc[3]], axis=1)

            @pl.when(n > 0)
            def _():
                dk_ref[...] = ck_ref[...] + dk_band[:BLOCK]
                dv_ref[...] = cv_ref[...] + dv_band[:BLOCK]

            ck_ref[...] = dk_band[BLOCK:]
            cv_ref[...] = dv_band[BLOCK:]

        @pl.when(n == nb)
        def _():
            dk_ref[...] = ck_ref[...]
            dv_ref[...] = cv_ref[...]

    own = lambda n: (jnp.minimum(n, nb - 1), 0)
    prev = lambda n: (jnp.clip(n - 1, 0, nb - 1), 0)
    done = lambda n: (jnp.maximum(n - 1, 0), 0)
    kvs = lambda imap: pl.BlockSpec((BLOCK, KV_WIDTH), imap)
    return _call(body, name="attn_bwd", grid=(nb + 1,),
                 in_specs=[pl.BlockSpec((BLOCK, aw), own), kvs(own), kvs(prev), kvs(own), kvs(prev),
                           pl.BlockSpec((BLOCK, aw), own), pl.BlockSpec(memory_space=pltpu.SMEM)],
                 out_specs=[pl.BlockSpec((BLOCK, aw), own), kvs(done), kvs(done),
                            pl.BlockSpec((SUBLANES, LANES), lambda n: (0, 0))],
                 out_shape=[jax.ShapeDtypeStruct((s, aw), F32), jax.ShapeDtypeStruct((s, KV_WIDTH), F32),
                            jax.ShapeDtypeStruct((s, KV_WIDTH), F32), jax.ShapeDtypeStruct((SUBLANES, LANES), F32)],
                 scratch_shapes=[pltpu.VMEM((BLOCK, KV_WIDTH), F32), pltpu.VMEM((BLOCK, KV_WIDTH), F32)],
                 compiler_params=_params(("arbitrary",)))(qs, ks, ks, vb, vb, d_o, sinks)


def conv_fwd(z, conv_w, mix, aw, cw, b_off):
    s = z.shape[0]
    tm = _tile(s, 256)
    nseg = cw // COLT
    hb = tm // SUBLANES

    def body(b_ref, c_ref, h_ref, g_ref, cp_ref, hp_ref, w_ref, mix_in, mix_ref):
        i = pl.program_id(0)
        u = c_ref[...] * h_ref[...]
        up = jnp.where(i > 0, cp_ref[...] * hp_ref[...], 0.0)
        ext = jnp.concatenate([up, u], axis=0)
        um1 = pltpu.roll(ext, 1, 0)[SUBLANES:]
        um2 = pltpu.roll(ext, 2, 0)[SUBLANES:]
        w = w_ref[...]
        cv = w[0:1] * um2 + w[1:2] * um1 + w[2:3] * u
        g = g_ref[...]
        mix_ref[...] = ((b_ref[...] * cv) * (g * _sigmoid(g))).astype(BF16)

    seg = lambda k: pl.BlockSpec((tm, COLT), functools.partial(lambda i, j, k: (i, b_off + k * nseg + j), k=k))
    halo = lambda k: pl.BlockSpec(
        (SUBLANES, COLT), functools.partial(lambda i, j, k: (jnp.maximum(i * hb - 1, 0), b_off + k * nseg + j), k=k))
    return _call(body, name="conv_fwd", grid=(s // tm, nseg),
                 in_specs=[seg(0), seg(1), seg(2), seg(3), halo(1), halo(2),
                           pl.BlockSpec((SUBLANES, COLT), lambda i, j: (0, j)), _hbm()],
                 out_specs=pl.BlockSpec((tm, COLT), lambda i, j: (i, aw // COLT + j)),
                 out_shape=jax.ShapeDtypeStruct(mix.shape, BF16),
                 input_output_aliases={7: 0},
                 compiler_params=_params(("parallel", "parallel")))(z, z, z, z, z, z, conv_w, mix)


def conv_bwd(z, d_mix, conv_w, dz, aw, cw, b_off):
    s = z.shape[0]
    tm = _tile(s, 256)
    nseg = cw // COLT
    hb = tm // SUBLANES
    n_row = s // tm
    last_h = s // SUBLANES - 1

    def body(b_ref, c_ref, h_ref, g_ref, cp_ref, hp_ref, bn_ref, gn_ref, dm_ref, dmn_ref, w_ref, dz_in,
             dz_ref, acc_ref, stash_ref):
        i = pl.program_id(1)
        k = pl.program_id(2)

        @pl.when(k == 0)
        def _():
            cc, hh, bb, g = c_ref[...], h_ref[...], b_ref[...], g_ref[...]
            w = w_ref[...]
            u = cc * hh
            up = jnp.where(i > 0, cp_ref[...] * hp_ref[...], 0.0)
            ext = jnp.concatenate([up, u], axis=0)
            um1 = pltpu.roll(ext, 1, 0)[SUBLANES:]
            um2 = pltpu.roll(ext, 2, 0)[SUBLANES:]
            cv = w[0:1] * um2 + w[1:2] * um1 + w[2:3] * u
            sig = _sigmoid(g)
            sg = g * sig
            dm = dm_ref[...]
            d_cv = (dm * bb) * sg
            gn = gn_ref[...]
            d_cv_next = jnp.where(i < n_row - 1, (dmn_ref[...] * bn_ref[...]) * (gn * _sigmoid(gn)), 0.0)
            ext2 = jnp.concatenate([d_cv, d_cv_next], axis=0)
            dp1 = pltpu.roll(ext2, tm + SUBLANES - 1, 0)[:tm]
            dp2 = pltpu.roll(ext2, tm + SUBLANES - 2, 0)[:tm]
            d_u = w[2:3] * d_cv + w[1:2] * dp1 + w[0:1] * dp2
            stash_ref[0] = ((dm * cv) * sg).astype(BF16)
            stash_ref[1] = (d_u * hh).astype(BF16)
            stash_ref[2] = (d_u * cc).astype(BF16)
            stash_ref[3] = (((dm * bb) * cv) * (sig * (1.0 + g * (1.0 - sig)))).astype(BF16)

            @pl.when(i == 0)
            def _():
                acc_ref[...] = jnp.zeros_like(acc_ref)

            acc_ref[0:1, :] += jnp.sum(d_cv * um2, axis=0, keepdims=True)
            acc_ref[1:2, :] += jnp.sum(d_cv * um1, axis=0, keepdims=True)
            acc_ref[2:3, :] += jnp.sum(d_cv * u, axis=0, keepdims=True)

        dz_ref[...] = stash_ref[k]

    seg = lambda q: pl.BlockSpec((tm, COLT), functools.partial(lambda j, i, k, q: (i, b_off + q * nseg + j), q=q))
    halo_p = lambda q: pl.BlockSpec(
        (SUBLANES, COLT),
        functools.partial(lambda j, i, k, q: (jnp.maximum(i * hb - 1, 0), b_off + q * nseg + j), q=q))
    halo_n = lambda q: pl.BlockSpec(
        (SUBLANES, COLT),
        functools.partial(lambda j, i, k, q: (jnp.minimum((i + 1) * hb, last_h), b_off + q * nseg + j), q=q))
    return _call(body, name="conv_bwd", grid=(nseg, n_row, 4),
                 in_specs=[seg(0), seg(1), seg(2), seg(3), halo_p(1), halo_p(2), halo_n(0), halo_n(3),
                           pl.BlockSpec((tm, COLT), lambda j, i, k: (i, aw // COLT + j)),
                           pl.BlockSpec((SUBLANES, COLT),
                                        lambda j, i, k: (jnp.minimum((i + 1) * hb, last_h), aw // COLT + j)),
                           pl.BlockSpec((SUBLANES, COLT), lambda j, i, k: (0, j)), _hbm()],
                 out_specs=[pl.BlockSpec((tm, COLT), lambda j, i, k: (i, b_off + k * nseg + j)),
                            pl.BlockSpec((SUBLANES, COLT), lambda j, i, k: (0, j))],
                 out_shape=[jax.ShapeDtypeStruct(dz.shape, BF16), jax.ShapeDtypeStruct((SUBLANES, cw), F32)],
                 input_output_aliases={11: 0},
                 scratch_shapes=[pltpu.VMEM((4, tm, COLT), BF16)],
                 compiler_params=_params(("arbitrary", "arbitrary", "arbitrary")))(
                     z, z, z, z, z, z, z, z, d_mix, d_mix, conv_w, dz)


def _place():
    x, y, c = lax.axis_index("x"), lax.axis_index("y"), lax.axis_index("c")
    chips = [(1 - x, y), (x, 1 - y), (1 - x, 1 - y)]
    return x, y, c, chips


def _remote(src, dst, send_sem, recv_sem, device):
    return pltpu.make_async_remote_copy(src_ref=src, dst_ref=dst, send_sem=send_sem, recv_sem=recv_sem,
                                        device_id=device, device_id_type=MESH)


def plan_gather_ici(n_split):
    def plan(refs, send, recv):
        x, y, c, chips = _place()
        me = 2 * x + y
        mine, theirs = [], []
        for w, ref in enumerate(refs):
            for j, (cx, cy) in enumerate(chips):
                def part(slot):
                    if w >= n_split:
                        return ref.at[slot]
                    hr = ref.shape[1] // 2
                    return ref.at[slot, pl.ds(c * hr, hr)]
                k = 3 * w + j
                mine.append(_remote(part(me), part(me), send.at[k], recv.at[k], (cx, cy, c)))
                theirs.append(_remote(part(2 * cx + cy), part(2 * cx + cy), send.at[k], recv.at[k], (cx, cy, c)))
        return mine, theirs
    return plan


def plan_gather_pass(refs, send, recv):
    x, y, c, chips = _place()
    mine, theirs = [], []
    for w, ref in enumerate(refs):
        hr = ref.shape[1] // 2
        for j, (cx, cy) in enumerate(chips):
            half = lambda core: ref.at[2 * cx + cy, pl.ds(core * hr, hr)]
            k = 3 * w + j
            mine.append(_remote(half(c), half(c), send.at[k], recv.at[k], (x, y, 1 - c)))
            theirs.append(_remote(half(1 - c), half(1 - c), send.at[k], recv.at[k], (x, y, 1 - c)))
    return mine, theirs


def plan_swap(refs, send, recv):
    x, y, c, _ = _place()
    nw = len(refs) // 2
    mine = []
    for w in range(nw):
        hr = refs[w].shape[1] // 2
        mine.append(_remote(refs[w].at[:, pl.ds((1 - c) * hr, hr), :], refs[nw + w], send.at[w], recv.at[w],
                            (x, y, 1 - c)))
    return mine, mine


def plan_scatter(refs, send, recv):
    x, y, c, chips = _place()
    me = 2 * x + y
    nw = len(refs) // 2
    mine, theirs = [], []
    for w in range(nw):
        for j, (cx, cy) in enumerate(chips):
            k = 3 * w + j
            mine.append(_remote(refs[w].at[2 * cx + cy], refs[nw + w].at[me], send.at[k], recv.at[k], (cx, cy, c)))
            theirs.append(_remote(refs[w].at[me], refs[nw + w].at[2 * cx + cy], send.at[k], recv.at[k], (cx, cy, c)))
    return mine, theirs


def plan_join(refs, send, recv):
    x, y, c, _ = _place()
    mine, theirs = [], []
    for w, ref in enumerate(refs):
        hr = ref.shape[0] // 2
        half = lambda core: ref.at[pl.ds(core * hr, hr)]
        mine.append(_remote(half(c), half(c), send.at[w], recv.at[w], (x, y, 1 - c)))
        theirs.append(_remote(half(1 - c), half(1 - c), send.at[w], recv.at[w], (x, y, 1 - c)))
    return mine, theirs


def exchange(name, plan, arrays, n):
    na = len(arrays)

    def body(*refs):
        mine, theirs = plan(refs[:na], refs[2 * na], refs[2 * na + 1])
        for cp in mine:
            cp.start()
        for cp in theirs:
            cp.wait_recv()
        for cp in mine:
            cp.wait_send()

    return _call(body, name=name, in_specs=[_hbm()] * na, out_specs=[_hbm()] * na,
                 out_shape=[jax.ShapeDtypeStruct(a.shape, a.dtype) for a in arrays],
                 input_output_aliases={i: i for i in range(na)},
                 scratch_shapes=[pltpu.SemaphoreType.DMA((n,)), pltpu.SemaphoreType.DMA((n,))])(*arrays)


def exchange_start(name, plan, arrays, n, after=()):
    na = len(arrays)

    def body(*refs):
        mine, _ = plan(refs[:na], refs[na], refs[na + 1])
        for cp in mine:
            cp.start()
        refs[-1][...] = jnp.zeros_like(refs[-1])

    hbm = pl.BlockSpec(memory_space=pltpu.HBM)
    sem = pl.BlockSpec(memory_space=pltpu.SEMAPHORE)
    return _call_after(body, after, name=name, in_specs=[hbm] * na,
                       out_specs=[sem, sem] + [hbm] * na + [pl.BlockSpec(memory_space=pltpu.VMEM)],
                       out_shape=[pltpu.SemaphoreType.DMA((n,)), pltpu.SemaphoreType.DMA((n,))]
                       + [pltpu.HBM(a.shape, a.dtype) for a in arrays]
                       + [jax.ShapeDtypeStruct((SUBLANES, LANES), F32)],
                       input_output_aliases={i: i + 2 for i in range(na)},
                       compiler_params=pltpu.CompilerParams(
                           has_side_effects=pltpu.SideEffectType.DATAFLOW_SIDE_EFFECTING))(
                               *[pltpu.with_memory_space_constraint(a, pltpu.HBM) for a in arrays])


def exchange_wait(name, plan, handle, after):
    send_sems, recv_sems, arrays = handle[0], handle[1], handle[2:-1]
    na = len(arrays)

    def body(*refs):
        mine, theirs = plan(refs[:na], refs[na], refs[na + 1])
        for cp in mine:
            cp.wait_send()
        for cp in theirs:
            cp.wait_recv()

    hbm = pl.BlockSpec(memory_space=pltpu.HBM)
    sem = pl.BlockSpec(memory_space=pltpu.SEMAPHORE)
    return _call(body, name=name, in_specs=[hbm] * na + [sem, sem, _hbm()], out_specs=[hbm] * na,
                 out_shape=[pltpu.HBM(a.shape, a.dtype) for a in arrays],
                 input_output_aliases={i: i for i in range(na)},
                 compiler_params=pltpu.CompilerParams(
                     has_side_effects=pltpu.SideEffectType.DATAFLOW_SIDE_EFFECTING))(
                         *arrays, send_sems, recv_sems, after)


def allgather_small(small):
    rows, width = small.shape

    def body(in_ref, out_ref, send, recv, loc):
        x, y, c, _ = _place()
        me = 4 * x + 2 * y + c
        local = pltpu.make_async_copy(in_ref, out_ref.at[me], loc)
        local.start()
        flips = [(fx, fy, fc) for fx in (0, 1) for fy in (0, 1) for fc in (0, 1)][1:]

        def cp(k, slot):
            fx, fy, fc = flips[k]
            return pltpu.make_async_remote_copy(
                src_ref=in_ref, dst_ref=out_ref.at[slot], send_sem=send.at[k], recv_sem=recv.at[k],
                device_id=(x ^ fx, y ^ fy, c ^ fc), device_id_type=MESH)

        for k in range(7):
            cp(k, me).start()
        for k in range(7):
            fx, fy, fc = flips[k]
            cp(k, 4 * (x ^ fx) + 2 * (y ^ fy) + (c ^ fc)).wait_recv()
        for k in range(7):
            cp(k, me).wait_send()
        local.wait()

    return _call(body, name="allgather_small",
                 in_specs=[pl.BlockSpec(memory_space=pltpu.VMEM)],
                 out_specs=pl.BlockSpec(memory_space=pltpu.VMEM),
                 out_shape=jax.ShapeDtypeStruct((8, rows, width), F32),
                 scratch_shapes=[pltpu.SemaphoreType.DMA((7,)), pltpu.SemaphoreType.DMA((7,)),
                                 pltpu.SemaphoreType.DMA])(small)


def add_sibling(grad, got, core, name):
    _, r, c = grad.shape
    hr = r // 2
    tr = _tile(hr, 128)
    nblk = hr // tr

    def body(core_ref, g_ref, o_ref, out_ref):
        out_ref[...] = (g_ref[...].astype(F32) + o_ref[...].astype(F32)).astype(BF16)

    grid_spec = pltpu.PrefetchScalarGridSpec(
        num_scalar_prefetch=1, grid=(N_CHIPS, nblk),
        in_specs=[pl.BlockSpec((None, tr, c), lambda t, i, core_ref: (t, core_ref[0] * nblk + i, 0)),
                  pl.BlockSpec((None, tr, c), lambda t, i, core_ref: (t, i, 0))],
        out_specs=pl.BlockSpec((None, tr, c), lambda t, i, core_ref: (t, i, 0)))
    return _call(body, name=name, grid_spec=grid_spec,
                 out_shape=jax.ShapeDtypeStruct((N_CHIPS, hr, c), BF16),
                 compiler_params=_params(("parallel", "parallel")))(core, grad, got)


def sum_chips(mine, owned, place, name):
    _, hr, c = mine.shape
    tr = _tile(hr, 128)
    nblk = hr // tr

    def body(place_ref, m_ref, o1_ref, o2_ref, o3_ref, out_ref):
        acc = m_ref[...].astype(F32)
        for o_ref in (o1_ref, o2_ref, o3_ref):
            acc = acc + o_ref[...].astype(F32)
        out_ref[...] = acc

    other = lambda k: pl.BlockSpec((None, tr, c), lambda i, place_ref: ((place_ref[0] + k) % N_CHIPS, i, 0))
    grid_spec = pltpu.PrefetchScalarGridSpec(
        num_scalar_prefetch=1, grid=(nblk,),
        in_specs=[other(0), other(1), other(2), other(3)],
        out_specs=pl.BlockSpec((tr, c), lambda i, place_ref: (place_ref[1] * nblk + i, 0)))
    return _call(body, name=name, grid_spec=grid_spec,
                 out_shape=jax.ShapeDtypeStruct((2 * hr, c), F32),
                 compiler_params=_params(("parallel",)))(place, mine, owned, owned, owned)


def sum_devices(gathered):
    _, rows, width = gathered.shape

    def body(g_ref, out_ref):
        acc = g_ref[0]
        for dev in range(1, 8):
            acc = acc + g_ref[dev]
        out_ref[...] = acc
        tail = acc[SUBLANES:]
        out_ref[SUBLANES:, :] = jnp.broadcast_to(jnp.sum(tail, axis=1, keepdims=True), tail.shape)

    return _call(body, name="sum_devices",
                 in_specs=[pl.BlockSpec(memory_space=pltpu.VMEM)],
                 out_specs=pl.BlockSpec(memory_space=pltpu.VMEM),
                 out_shape=jax.ShapeDtypeStruct((rows, width), F32))(gathered)


def _rope_tables(s):
    half = ROT_DIM // 2
    inv_freq = jnp.power(jnp.float32(ROPE_THETA), -jnp.arange(half, dtype=F32) * 2.0 / ROT_DIM)
    ang = jnp.arange(s).astype(F32)[:, None] * inv_freq[None, :]
    cos, sin = jnp.cos(ang), jnp.sin(ang)
    zeros = lambda n: jnp.zeros((s, n), F32)
    cos64 = jnp.concatenate([cos, cos, jnp.ones((s, HEAD_DIM - ROT_DIM), F32)], axis=1)
    sa64 = jnp.concatenate([-sin, zeros(HEAD_DIM - half)], axis=1)
    sb64 = jnp.concatenate([zeros(half), sin, zeros(HEAD_DIM - ROT_DIM)], axis=1)
    return tuple(jnp.concatenate([t, t], axis=1) for t in (cos64, sa64, sb64))


def _pad_rows(a, rows):
    return jnp.pad(a, ((0, rows - a.shape[0]), (0, 0)))


def _pad_cols(a, cols):
    return jnp.pad(a, ((0, 0), (0, cols - a.shape[1])))


def kernel(x, p, norm_gain, w_in, q_norm_gain, k_norm_gain, attn_sinks, conv_w, w_out, ple_gate_norm_gain, w_ple_gate, b_ple_gate, w_ple_proj, ple_norm_gain, loss_target, m_norm_gain, m_w_in, m_q_norm_gain, m_k_norm_gain, m_attn_sinks, m_conv_w, m_w_out, m_ple_gate_norm_gain, m_w_ple_gate, m_b_ple_gate, m_w_ple_proj, m_ple_norm_gain, v_norm_gain, v_w_in, v_q_norm_gain, v_k_norm_gain, v_attn_sinks, v_conv_w, v_w_out, v_ple_gate_norm_gain, v_w_ple_gate, v_b_ple_gate, v_w_ple_proj, v_ple_norm_gain):
    x2, p2, tgt = x[0], p[0, 0], loss_target[0]
    s, d = x2.shape
    ple = p2.shape[1]
    aw = d // 2
    cw = d - aw
    nq = aw // HEAD_DIM
    sh = w_in.shape[2]
    in_w = N_CHIPS * sh
    dq = d // N_CHIPS
    cq = cw // N_CHIPS
    ga_off = (aw + 2 * KV_WIDTH) // COLT
    b_off = (2 * aw + 2 * KV_WIDTH) // COLT
    assert in_w == 2 * aw + 2 * KV_WIDTH + 4 * cw and aw % COLT == 0 and cw % COLT == 0
    assert s % BLOCK == 0 and sh % LANES == 0 and nq % (2 * N_KV_HEADS) == 0

    core = lax.axis_index("c").astype(jnp.int32).reshape(1)
    chip = 2 * lax.axis_index("x") + lax.axis_index("y")

    chip1 = chip.astype(jnp.int32).reshape(1)
    place = jnp.concatenate([chip1, core])
    bufs = [cast_into_slot(w_in[0], chip1, "cast_w_in"), cast_into_slot(w_out[0], chip1, "cast_w_out"),
            cast_into_slot(w_ple_gate[0], chip1, "cast_w_pg"), cast_into_slot(w_ple_proj[0], chip1, "cast_w_pp")]
    conv_buf = lax.dynamic_update_slice(jnp.zeros((N_CHIPS, SUBLANES, cq), F32),
                                        _pad_rows(conv_w[0], SUBLANES)[None], (chip, 0, 0))
    ici_wi = exchange_start("gather_wi_start", plan_gather_ici(1), bufs[:1], 3)
    ici_rest = exchange_start("gather_rest_start", plan_gather_ici(3), bufs[1:] + [conv_buf], 12,
                              after=(ici_wi[-1],))

    tm = _tile(s, 1024)
    tn = _tile(d, 1024)
    tk = _tile(d, 1024)
    ts = _tile(s, 1024)
    h = rms_fwd(x2, norm_gain, "rms_fwd_x", after=(ici_rest[-1],))
    tabs = _rope_tables(s)
    qg = jnp.tile(q_norm_gain, (1, LANES // HEAD_DIM))
    kg = jnp.tile(k_norm_gain, (1, LANES // HEAD_DIM))
    seg_i = jnp.arange(SEG) // HEAD_DIM
    segb = (seg_i[:, None] == seg_i[None, :]).astype(BF16)
    wi_all, = exchange_wait("gather_wi_wait", plan_gather_ici(1), ici_wi, h)
    wi_all, = exchange("gather_wi_pass", plan_gather_pass, [wi_all], 3)
    z = matmul(h, wi_all, grid=(s // tm, N_CHIPS, d // tk),
               a_spec=pl.BlockSpec((tm, tk), lambda i, j, k: (i, k)),
               b_spec=pl.BlockSpec((None, tk, sh), lambda i, j, k: (j, k, 0)),
               o_spec=pl.BlockSpec((tm, sh), lambda i, j, k: (i, j)),
               out_shape=jax.ShapeDtypeStruct((s, in_w), F32), dims=NN, name="mm_z")
    wo_all, wg_all, wp_all, conv_all = exchange_wait("gather_rest_wait", plan_gather_ici(3), ici_rest, z)
    wo_all, wg_all, wp_all = exchange("gather_rest_pass", plan_gather_pass, [wo_all, wg_all, wp_all], 9)
    wo_full = wo_all.reshape(d, d)
    wg_full = wg_all.reshape(d, d)
    conv_full = conv_all.transpose(1, 0, 2).reshape(SUBLANES, cw)
    qs, ks, vb = qk_prep_fwd(z, tabs, qg, kg, segb, aw)
    attn, mix = attn_fwd(qs, ks, vb, z, attn_sinks, aw, d, ga_off)
    mix = conv_fwd(z, conv_full, mix, aw, cw, b_off)
    sq = lambda shape: dict(
        a_spec=pl.BlockSpec((tm, tk), lambda i, j, k: (i, k)),
        o_spec=pl.BlockSpec((tm, tn), lambda i, j, k: (i, j)),
        out_shape=jax.ShapeDtypeStruct(shape, F32))
    x1 = matmul(mix, wo_full, grid=(s // tm, d // tn, d // tk),
                b_spec=pl.BlockSpec((tk, tn), lambda i, j, k: (k, j)), dims=NN, name="mm_x1",
                res=x2, res_spec=pl.BlockSpec((tm, tn), lambda i, j, k: (i, j)), **sq((s, d)))
    hg = rms_fwd(x1, ple_gate_norm_gain, "rms_fwd_x1")
    gl = matmul(hg, wg_full, grid=(s // tm, d // tn, d // tk),
                b_spec=pl.BlockSpec((tk, tn), lambda i, j, k: (k, j)), dims=NN, name="mm_gl", **sq((s, d)))
    pq = d // N_CHIPS
    pe = matmul(p2, wp_all, grid=(s // tm, N_CHIPS, 1),
                a_spec=pl.BlockSpec((tm, ple), lambda i, j, k: (i, 0)),
                b_spec=pl.BlockSpec((None, ple, pq), lambda i, j, k: (j, 0, 0)),
                o_spec=pl.BlockSpec((tm, pq), lambda i, j, k: (i, j)),
                out_shape=jax.ShapeDtypeStruct((s, d), F32), dims=NN, name="mm_pe")

    d_gl, d_pe, dy, acc_head = head_fwd_bwd(x1, gl, pe, tgt, b_ple_gate, ple_norm_gain)
    d_hg = matmul(d_gl, wg_full, grid=(s // tm, d // tn, d // tk),
                  b_spec=pl.BlockSpec((tn, tk), lambda i, j, k: (j, k)), dims=NT, name="mm_d_hg", **sq((s, d)))
    wgrad = lambda a_cols, shape3, o_spec, b_cols, name, a, b, grid: matmul(
        a, b, grid=grid,
        a_spec=pl.BlockSpec((ts, a_cols), lambda i, j, k: (k, i)),
        b_spec=pl.BlockSpec((ts, b_cols), lambda i, j, k: (k, j)),
        o_spec=o_spec, out_shape=jax.ShapeDtypeStruct(shape3, BF16), dims=TN, name=name)
    g_wg = wgrad(tn, (d, d), pl.BlockSpec((tn, tn), lambda i, j, k: (i, j)), tn, "mm_g_wg", hg, d_gl,
                 (d // tn, d // tn, s // ts))
    g_wp = wgrad(ple, (N_CHIPS, ple, pq), pl.BlockSpec((None, ple, pq), lambda i, j, k: (j, 0, 0)), pq,
                 "mm_g_wp", p2, d_pe, (1, N_CHIPS, s // ts))
    d_x1, d_x1b, acc_g = rms_bwd_add(d_hg, x1, dy, ple_gate_norm_gain, "rms_bwd_x1", True)
    d_mix = matmul(d_x1b, wo_full, grid=(s // tm, d // tn, d // tk),
                   b_spec=pl.BlockSpec((tn, tk), lambda i, j, k: (j, k)), dims=NT, name="mm_d_mix", **sq((s, d)))
    g_wo = wgrad(tn, (d, d), pl.BlockSpec((tn, tn), lambda i, j, k: (i, j)), tn, "mm_g_wo", mix, d_x1b,
                 (d // tn, d // tn, s // ts))
    def reduce_start(tag, grads):
        n = len(grads)
        lands = [lax.empty((N_CHIPS, a.shape[1] // 2, a.shape[2]), BF16) for a in grads]
        swapped = exchange("swap_" + tag, plan_swap, list(grads) + lands, n)
        parts = [add_sibling(g, o, core, "add_sibling_%s%d" % (tag, i))
                 for i, (g, o) in enumerate(zip(swapped[:n], swapped[n:]))]
        return exchange_start("scatter_%s_start" % tag, plan_scatter,
                              parts + [lax.empty(a.shape, BF16) for a in parts], 3 * n)

    def reduce_finish(tag, handle, after):
        n = (len(handle) - 3) // 2
        got = exchange_wait("scatter_%s_wait" % tag, plan_scatter, handle, after)
        halves = [sum_chips(pt, o, place, "sum_chips_%s%d" % (tag, i))
                  for i, (pt, o) in enumerate(zip(got[:n], got[n:]))]
        return exchange("join_" + tag, plan_join, halves, n)

    early = reduce_start("early", [g_wo.reshape(N_CHIPS, dq, d), g_wg.reshape(N_CHIPS, dq, d), g_wp])
    d_o, dz = gate_bwd(d_mix, attn, z, aw, ga_off, after=(early[-1],))
    dqs, dks, dvs, acc_sink = attn_bwd(qs, ks, vb, d_o, attn_sinks, aw)
    dz, acc_qk = qk_prep_bwd(z, dqs, dks, dvs, tabs, qg, kg, segb, dz, aw)
    dz, acc_conv = conv_bwd(z, d_mix, conv_full, dz, aw, cw, b_off)
    full_wo, full_wg, full_wp = reduce_finish("early", early, dz)
    g_wi = wgrad(tn, (N_CHIPS, d, sh), pl.BlockSpec((None, tn, sh), lambda i, j, k: (j, i, 0)), sh,
                 "mm_g_wi", h, dz, (d // tn, N_CHIPS, s // ts))
    late = reduce_start("late", [g_wi])
    d_h = matmul(dz, wi_all, grid=(s // tm, d // tn, N_CHIPS),
                 a_spec=pl.BlockSpec((tm, sh), lambda i, j, k: (i, k)),
                 b_spec=pl.BlockSpec((None, tn, sh), lambda i, j, k: (k, j, 0)),
                 o_spec=pl.BlockSpec((tm, tn), lambda i, j, k: (i, j)),
                 out_shape=jax.ShapeDtypeStruct((s, d), F32), dims=NT, name="mm_d_h", after=(late[-1],))
    grad_x, acc_x = rms_bwd_add(d_h, x2, d_x1, norm_gain, "rms_bwd_x", False)
    full_wi, = reduce_finish("late", late, grad_x)
    big = {}
    for nm, g, w, m, v in (("w_in", full_wi, w_in, m_w_in, v_w_in), ("w_out", full_wo, w_out, m_w_out, v_w_out),
                           ("w_ple_gate", full_wg, w_ple_gate, m_w_ple_gate, v_w_ple_gate),
                           ("w_ple_proj", full_wp, w_ple_proj, m_w_ple_proj, v_w_ple_proj)):
        big[nm] = [o[None] for o in adamw(g, w[0], m[0], v[0], "adamw_" + nm)]

    wsm = max(d, cw)
    misc = jnp.concatenate([acc_qk[0:1, :HEAD_DIM], acc_qk[1:2, :HEAD_DIM], acc_sink[0:1, :nq]], axis=1)
    small = jnp.concatenate([
        _pad_cols(acc_x[0:1], wsm), _pad_cols(acc_g[0:1], wsm), _pad_cols(acc_head[0:1], wsm),
        _pad_cols(acc_head[1:2], wsm), _pad_cols(misc, wsm), _pad_cols(acc_conv[0:3], wsm)], axis=0)
    both = jnp.concatenate([small, _pad_rows(_pad_cols(acc_head[2:3], wsm), SUBLANES)], axis=0)
    tot = sum_devices(allgather_small(both))
    loss = tot[SUBLANES, 0]

    def pack(vals):
        ng, pg, bg, eg, qgv, kgv, sk, cv = vals
        misc_v = jnp.concatenate([qgv, kgv, sk], axis=1)
        return jnp.concatenate([_pad_cols(ng, wsm), _pad_cols(pg, wsm), _pad_cols(bg, wsm), _pad_cols(eg, wsm),
                                _pad_cols(misc_v, wsm), _pad_cols(cv[0], wsm)], axis=0)

    g_small = jnp.concatenate(
        [tot[0:5], _pad_cols(lax.dynamic_slice(tot[5:8], (0, chip * cq), (3, cq)), wsm)], axis=0)
    w_small = pack((norm_gain, ple_gate_norm_gain, b_ple_gate, ple_norm_gain, q_norm_gain, k_norm_gain,
                    attn_sinks, conv_w))
    m_small = pack((m_norm_gain, m_ple_gate_norm_gain, m_b_ple_gate, m_ple_norm_gain, m_q_norm_gain,
                    m_k_norm_gain, m_attn_sinks, m_conv_w))
    v_small = pack((v_norm_gain, v_ple_gate_norm_gain, v_b_ple_gate, v_ple_norm_gain, v_q_norm_gain,
                    v_k_norm_gain, v_attn_sinks, v_conv_w))
    sm = adamw(g_small, w_small, m_small, v_small, "adamw_small")

    def unpack(a):
        return {"norm_gain": a[0:1, :d], "ple_gate_norm_gain": a[1:2, :d], "b_ple_gate": a[2:3, :d],
                "ple_norm_gain": a[3:4, :d], "q_norm_gain": a[4:5, :HEAD_DIM],
                "k_norm_gain": a[4:5, HEAD_DIM:2 * HEAD_DIM],
                "attn_sinks": a[4:5, 2 * HEAD_DIM:2 * HEAD_DIM + nq], "conv_w": a[5:8, :cq][None]}

    order = ("norm_gain", "w_in", "q_norm_gain", "k_norm_gain", "attn_sinks", "conv_w", "w_out",
             "ple_gate_norm_gain", "w_ple_gate", "b_ple_gate", "w_ple_proj", "ple_norm_gain")
    outs = [loss, grad_x[None]]
    for kind in range(4):
        table = unpack(sm[kind])
        for nm in order:
            outs.append(big[nm][kind] if nm in big else table[nm])
    return tuple(outs)
```

```python
import functools

import jax
import jax.numpy as jnp
from jax import lax
from jax.experimental import pallas as pl
from jax.experimental.pallas import tpu as pltpu

F32 = jnp.float32
BF16 = jnp.bfloat16
MESH = pl.DeviceIdType.MESH

HEAD_DIM = 64
N_KV_HEADS = 4
KV_WIDTH = N_KV_HEADS * HEAD_DIM
BLOCK = 128
ROT_DIM = 16
ROPE_THETA = 500000.0
EPS = 1e-6
NEG_INF = -1e30
N_CHIPS = 4
LANES = 128
SUBLANES = 8
COLT = 512
SEG = 256
VMEM_LIMIT = 48 * 1024 * 1024

ADAM_LR = 0.001
ADAM_B1 = 0.9
ADAM_B2 = 0.999
ADAM_EPS = 1e-08
ADAM_WD = 0.01
ADAM_STEP = 10

NN = (((1,), (0,)), ((), ()))
NT = (((1,), (1,)), ((), ()))
TN = (((0,), (0,)), ((), ()))


def _call(body, **kw):
    return pl.pallas_call(body, **kw)


def _call_after(body, after, **kw):
    n_in, n_after = len(kw["in_specs"]), len(after)
    kw["in_specs"] = list(kw["in_specs"]) + [pl.BlockSpec(memory_space=pl.ANY)] * n_after

    def body_after(*refs):
        body(*refs[:n_in], *refs[n_in + n_after:])

    call = _call(body_after, **kw)
    return lambda *args: call(*args, *after)


def _params(sem):
    return pltpu.CompilerParams(dimension_semantics=sem, vmem_limit_bytes=VMEM_LIMIT)


def _tile(n, pref):
    return pref if n % pref == 0 else n


def _sigmoid(v):
    return 1.0 / (1.0 + jnp.exp(-v))


def _hbm():
    return pl.BlockSpec(memory_space=pl.ANY)


def cast_into_slot(a, chip, name):
    r, c = a.shape
    tr = _tile(r, 256)

    def body(chip_ref, a_ref, o_ref):
        o_ref[...] = a_ref[...].astype(BF16)

    grid_spec = pltpu.PrefetchScalarGridSpec(
        num_scalar_prefetch=1, grid=(r // tr,),
        in_specs=[pl.BlockSpec((tr, c), lambda i, chip_ref: (i, 0))],
        out_specs=pl.BlockSpec((None, tr, c), lambda i, chip_ref: (chip_ref[0], i, 0)))
    return _call(body, name=name, grid_spec=grid_spec,
                 out_shape=jax.ShapeDtypeStruct((N_CHIPS, r, c), BF16),
                 compiler_params=_params(("parallel",)))(chip, a)


def rms_fwd(x, gain, name, after=()):
    s, d = x.shape
    tm = _tile(s, 256)

    def body(x_ref, g_ref, o_ref):
        xf = x_ref[...]
        r = lax.rsqrt(jnp.mean(xf * xf, axis=-1, keepdims=True) + EPS)
        o_ref[...] = ((xf * r) * g_ref[...]).astype(BF16)

    return _call_after(body, after, name=name, grid=(s // tm,),
                       in_specs=[pl.BlockSpec((tm, d), lambda i: (i, 0)),
                                 pl.BlockSpec((1, d), lambda i: (0, 0))],
                       out_specs=pl.BlockSpec((tm, d), lambda i: (i, 0)),
                       out_shape=jax.ShapeDtypeStruct((s, d), BF16),
                       compiler_params=_params(("parallel",)))(x, gain)


def rms_bwd_add(dyn, xin, add, gain, name, want_bf16):
    s, d = xin.shape
    tm = _tile(s, 256)

    def body(dy_ref, x_ref, a_ref, g_ref, *outs):
        i = pl.program_id(0)
        dx_ref, acc_ref = outs[0], outs[-1]
        xf = x_ref[...]
        r = lax.rsqrt(jnp.mean(xf * xf, axis=-1, keepdims=True) + EPS)
        xhat = xf * r
        dyv = dy_ref[...]
        gd = dyv * g_ref[...]
        dx = a_ref[...] + r * (gd - xhat * jnp.mean(xhat * gd, axis=-1, keepdims=True))
        dx_ref[...] = dx
        if want_bf16:
            outs[1][...] = dx.astype(BF16)

        @pl.when(i == 0)
        def _():
            acc_ref[...] = jnp.zeros_like(acc_ref)

        acc_ref[0:1, :] += jnp.sum(dyv * xhat, axis=0, keepdims=True)

    row = pl.BlockSpec((tm, d), lambda i: (i, 0))
    out_specs = [row] + ([row] if want_bf16 else []) + [pl.BlockSpec((SUBLANES, d), lambda i: (0, 0))]
    out_shape = ([jax.ShapeDtypeStruct((s, d), F32)]
                 + ([jax.ShapeDtypeStruct((s, d), BF16)] if want_bf16 else [])
                 + [jax.ShapeDtypeStruct((SUBLANES, d), F32)])
    return _call(body, name=name, grid=(s // tm,),
                 in_specs=[row, row, row, pl.BlockSpec((1, d), lambda i: (0, 0))],
                 out_specs=out_specs, out_shape=out_shape,
                 compiler_params=_params(("arbitrary",)))(dyn, xin, add, gain)


def head_fwd_bwd(x1, gl, pe, tgt, bias, ple_gain):
    s, d = x1.shape
    tm = _tile(s, 128)

    def body(x1_ref, gl_ref, pe_ref, t_ref, b_ref, g_ref, dgl_ref, dpe_ref, dy_ref, acc_ref):
        i = pl.program_id(0)
        gate = _sigmoid(gl_ref[...] + b_ref[...])
        pev = pe_ref[...]
        r = lax.rsqrt(jnp.mean(pev * pev, axis=-1, keepdims=True) + EPS)
        pehat = pev * r
        gain = g_ref[...]
        e = pehat * gain
        diff = (x1_ref[...] + gate * e) - t_ref[...]
        dy = diff * (1.0 / d)
        dy_ref[...] = dy
        d_gl = (dy * e) * (gate * (1.0 - gate))
        dgl_ref[...] = d_gl.astype(BF16)
        d_e = dy * gate
        gd = d_e * gain
        d_pe = r * (gd - pehat * jnp.mean(pehat * gd, axis=-1, keepdims=True))
        dpe_ref[...] = d_pe.astype(BF16)

        @pl.when(i == 0)
        def _():
            acc_ref[...] = jnp.zeros_like(acc_ref)

        acc_ref[0:1, :] += jnp.sum(d_gl, axis=0, keepdims=True)
        acc_ref[1:2, :] += jnp.sum(d_e * pehat, axis=0, keepdims=True)
        acc_ref[2:3, :] += jnp.sum(diff * diff, axis=0, keepdims=True) * (0.5 / d)

    row = pl.BlockSpec((tm, d), lambda i: (i, 0))
    vec = pl.BlockSpec((1, d), lambda i: (0, 0))
    return _call(body, name="head_fwd_bwd", grid=(s // tm,),
                 in_specs=[row, row, row, row, vec, vec],
                 out_specs=[row, row, row, pl.BlockSpec((SUBLANES, d), lambda i: (0, 0))],
                 out_shape=[jax.ShapeDtypeStruct((s, d), BF16), jax.ShapeDtypeStruct((s, d), BF16),
                            jax.ShapeDtypeStruct((s, d), F32), jax.ShapeDtypeStruct((SUBLANES, d), F32)],
                 compiler_params=_params(("arbitrary",)))(x1, gl, pe, tgt, bias, ple_gain)


def adamw(g, w, m, v, name):
    r, c = g.shape
    tr = _tile(r, 128)

    def body(g_ref, w_ref, m_ref, v_ref, go_ref, d_ref, mo_ref, vo_ref):
        gv = g_ref[...]
        mn = ADAM_B1 * m_ref[...] + (1.0 - ADAM_B1) * gv
        vn = ADAM_B2 * v_ref[...] + (1.0 - ADAM_B2) * (gv * gv)
        m_hat = mn / (1.0 - ADAM_B1 ** ADAM_STEP)
        v_hat = vn / (1.0 - ADAM_B2 ** ADAM_STEP)
        go_ref[...] = gv
        d_ref[...] = -ADAM_LR * (m_hat / (jnp.sqrt(v_hat) + ADAM_EPS) + ADAM_WD * w_ref[...])
        mo_ref[...] = mn
        vo_ref[...] = vn

    blk = pl.BlockSpec((tr, c), lambda i: (i, 0))
    shp = jax.ShapeDtypeStruct((r, c), F32)
    return _call(body, name=name, grid=(r // tr,), in_specs=[blk] * 4, out_specs=[blk] * 4,
                 out_shape=[shp] * 4, compiler_params=_params(("parallel",)))(g, w, m, v)


def matmul(a, b, *, grid, a_spec, b_spec, o_spec, out_shape, dims, name, res=None, res_spec=None, after=()):
    nk = grid[2]
    acc_shape = tuple(d for d in o_spec.block_shape if d is not None)

    def body(*refs):
        a_ref, b_ref = refs[:2]
        r_ref = refs[2] if res is not None else None
        o_ref = refs[3] if res is not None else refs[2]
        part = lax.dot_general(a_ref[...].astype(BF16), b_ref[...].astype(BF16), dims, preferred_element_type=F32)

        def finish(out):
            if res is not None:
                out = r_ref[...] + out
            o_ref[...] = out.astype(o_ref.dtype)

        if nk == 1:
            finish(part)
            return
        acc_ref = refs[-1]
        k = pl.program_id(2)

        @pl.when(k == 0)
        def _():
            acc_ref[...] = part

        @pl.when((k > 0) & (k < nk - 1))
        def _():
            acc_ref[...] += part

        @pl.when(k == nk - 1)
        def _():
            finish(acc_ref[...] + part)

    in_specs = [a_spec, b_spec] + ([res_spec] if res is not None else [])
    args = (a, b) + ((res,) if res is not None else ())
    return _call_after(body, after, name=name, grid=grid, in_specs=in_specs, out_specs=o_spec,
                       out_shape=out_shape, scratch_shapes=[pltpu.VMEM(acc_shape, F32)] if nk > 1 else [],
                       compiler_params=_params(("parallel", "parallel", "arbitrary")))(*args)


def _seg_mean(sq, segb):
    parts = []
    for cgrp in range(sq.shape[1] // SEG):
        blk = sq[:, cgrp * SEG:(cgrp + 1) * SEG]
        hi = blk.astype(BF16)
        r1 = blk - hi.astype(F32)
        mid = r1.astype(BF16)
        lo = (r1 - mid.astype(F32)).astype(BF16)
        acc = jnp.dot(hi, segb, preferred_element_type=F32)
        acc += jnp.dot(mid, segb, preferred_element_type=F32)
        acc += jnp.dot(lo, segb, preferred_element_type=F32)
        parts.append(acc)
    out = parts[0] if len(parts) == 1 else jnp.concatenate(parts, axis=1)
    return out * (1.0 / HEAD_DIM)


def _rope(v, cos, sa, sb):
    parts = []
    for cgrp in range(v.shape[1] // LANES):
        blk = v[:, cgrp * LANES:(cgrp + 1) * LANES]
        parts.append(blk * cos + pltpu.roll(blk, LANES - 8, 1) * sa + pltpu.roll(blk, 8, 1) * sb)
    return parts[0] if len(parts) == 1 else jnp.concatenate(parts, axis=1)


def _rope_t(dv, cos, sa, sb):
    parts = []
    for cgrp in range(dv.shape[1] // LANES):
        blk = dv[:, cgrp * LANES:(cgrp + 1) * LANES]
        parts.append(blk * cos + pltpu.roll(blk * sa, 8, 1) + pltpu.roll(blk * sb, LANES - 8, 1))
    return parts[0] if len(parts) == 1 else jnp.concatenate(parts, axis=1)


def _tile_lanes(vec, width):
    reps = width // LANES
    return vec if reps == 1 else jnp.tile(vec, (1, reps))


def qk_prep_fwd(z, tabs, qg, kg, segb, aw):
    s = z.shape[0]
    tm = _tile(s, 256)
    wq = aw + 2 * KV_WIDTH

    def body(z_ref, cos_ref, sa_ref, sb_ref, qg_ref, kg_ref, seg_ref, qs_ref, ks_ref, vb_ref):
        zz = z_ref[...]
        q, k, v = zz[:, :aw], zz[:, aw:aw + KV_WIDTH], zz[:, aw + KV_WIDTH:]
        cos, sa, sb, segm = cos_ref[...], sa_ref[...], sb_ref[...], seg_ref[...]
        rq = lax.rsqrt(_seg_mean(q * q, segm) + EPS)
        qn = (q * rq) * _tile_lanes(qg_ref[...], aw)
        qs_ref[...] = (_rope(qn, cos, sa, sb) * (HEAD_DIM ** -0.5)).astype(BF16)
        rk = lax.rsqrt(_seg_mean(k * k, segm) + EPS)
        kn = (k * rk) * _tile_lanes(kg_ref[...], KV_WIDTH)
        ks_ref[...] = _rope(kn, cos, sa, sb).astype(BF16)
        vb_ref[...] = v.astype(BF16)

    tab = pl.BlockSpec((tm, LANES), lambda i: (i, 0))
    vec = pl.BlockSpec((1, LANES), lambda i: (0, 0))
    return _call(body, name="qk_prep_fwd", grid=(s // tm,),
                 in_specs=[pl.BlockSpec((tm, wq), lambda i: (i, 0)), tab, tab, tab, vec, vec,
                           pl.BlockSpec((SEG, SEG), lambda i: (0, 0))],
                 out_specs=[pl.BlockSpec((tm, aw), lambda i: (i, 0)),
                            pl.BlockSpec((tm, KV_WIDTH), lambda i: (i, 0)),
                            pl.BlockSpec((tm, KV_WIDTH), lambda i: (i, 0))],
                 out_shape=[jax.ShapeDtypeStruct((s, aw), BF16), jax.ShapeDtypeStruct((s, KV_WIDTH), BF16),
                            jax.ShapeDtypeStruct((s, KV_WIDTH), BF16)],
                 compiler_params=_params(("parallel",)))(z, *tabs, qg, kg, segb)


def qk_prep_bwd(z, dqs, dks, dvs, tabs, qg, kg, segb, dz, aw):
    s = z.shape[0]
    tm = _tile(s, 256)
    wq = aw + 2 * KV_WIDTH

    def body(z_ref, dq_ref, dk_ref, dv_ref, cos_ref, sa_ref, sb_ref, qg_ref, kg_ref, seg_ref, dz_in,
             dz_ref, acc_ref):
        i = pl.program_id(0)
        zz = z_ref[...]
        q, k = zz[:, :aw], zz[:, aw:aw + KV_WIDTH]
        cos, sa, sb, segm = cos_ref[...], sa_ref[...], sb_ref[...], seg_ref[...]

        def one(xv, dout, gvec, width):
            g = _tile_lanes(gvec, width)
            r = lax.rsqrt(_seg_mean(xv * xv, segm) + EPS)
            xhat = xv * r
            dn = _rope_t(dout, cos, sa, sb)
            gd = dn * g
            dx = r * (gd - xhat * _seg_mean(xhat * gd, segm))
            contrib = jnp.sum(dn * xhat, axis=0, keepdims=True)
            folded = contrib[:, :LANES]
            for cgrp in range(1, width // LANES):
                folded = folded + contrib[:, cgrp * LANES:(cgrp + 1) * LANES]
            return dx, folded + pltpu.roll(folded, HEAD_DIM, 1)

        dq, gq = one(q, dq_ref[...] * (HEAD_DIM ** -0.5), qg_ref[...], aw)
        dk, gk = one(k, dk_ref[...], kg_ref[...], KV_WIDTH)
        dz_ref[:, :aw] = dq.astype(BF16)
        dz_ref[:, aw:aw + KV_WIDTH] = dk.astype(BF16)
        dz_ref[:, aw + KV_WIDTH:] = dv_ref[...].astype(BF16)

        @pl.when(i == 0)
        def _():
            acc_ref[...] = jnp.zeros_like(acc_ref)

        acc_ref[0:1, :] += gq
        acc_ref[1:2, :] += gk

    tab = pl.BlockSpec((tm, LANES), lambda i: (i, 0))
    vec = pl.BlockSpec((1, LANES), lambda i: (0, 0))
    kvb = pl.BlockSpec((tm, KV_WIDTH), lambda i: (i, 0))
    return _call(body, name="qk_prep_bwd", grid=(s // tm,),
                 in_specs=[pl.BlockSpec((tm, wq), lambda i: (i, 0)),
                           pl.BlockSpec((tm, aw), lambda i: (i, 0)), kvb, kvb, tab, tab, tab, vec, vec,
                           pl.BlockSpec((SEG, SEG), lambda i: (0, 0)), _hbm()],
                 out_specs=[pl.BlockSpec((tm, wq), lambda i: (i, 0)),
                            pl.BlockSpec((SUBLANES, LANES), lambda i: (0, 0))],
                 out_shape=[jax.ShapeDtypeStruct(dz.shape, BF16), jax.ShapeDtypeStruct((SUBLANES, LANES), F32)],
                 input_output_aliases={10: 0},
                 compiler_params=_params(("arbitrary",)))(z, dqs, dks, dvs, *tabs, qg, kg, segb, dz)


def _placed(band, lane_idx):
    out = {}
    for kh in range(N_KV_HEADS):
        grp = band[:, (kh // 2) * LANES:(kh // 2 + 1) * LANES]
        for half in (0, 1):
            t = grp if half == kh % 2 else pltpu.roll(grp, HEAD_DIM, 1)
            keep = (lane_idx >= half * HEAD_DIM) & (lane_idx < (half + 1) * HEAD_DIM)
            out[kh, half] = jnp.where(keep, t, jnp.zeros_like(t))
    return out


def _scores(qgrp, kpl, valid, sink):
    sc = lax.dot_general(qgrp, kpl, NT, preferred_element_type=F32)
    sc = jnp.where(valid, sc, NEG_INF)
    m = jnp.maximum(jnp.max(sc, axis=-1, keepdims=True), sink)
    e = jnp.exp(sc - m)
    es = jnp.exp(sink - m)
    den = jnp.sum(e, axis=-1, keepdims=True) + es
    return e / den, es / den


def _valid_mask(n):
    r_i = lax.broadcasted_iota(jnp.int32, (BLOCK, 2 * BLOCK), 0)
    j_i = lax.broadcasted_iota(jnp.int32, (BLOCK, 2 * BLOCK), 1)
    return (j_i > r_i) & (j_i <= r_i + BLOCK) & ((n > 0) | (j_i >= BLOCK))


def attn_fwd(qs, ks, vb, z, sinks, aw, d, ga_off):
    s = qs.shape[0]
    nb = s // BLOCK
    nq = aw // HEAD_DIM
    grp_sz = nq // N_KV_HEADS
    n_ga = aw // COLT

    def body(q_ref, ko_ref, kp_ref, vo_ref, vp_ref, *rest):
        ga_refs = rest[:n_ga]
        sink_ref, attn_ref, mix_ref = rest[n_ga:]
        n = pl.program_id(0)
        lane_idx = lax.broadcasted_iota(jnp.int32, (2 * BLOCK, LANES), 1)
        kpl = _placed(jnp.concatenate([kp_ref[...], ko_ref[...]], axis=0), lane_idx)
        vpl = _placed(jnp.concatenate([vp_ref[...], vo_ref[...]], axis=0), lane_idx)
        valid = _valid_mask(n)
        for cgrp in range(nq // 2):
            qgrp = q_ref[:, cgrp * LANES:(cgrp + 1) * LANES]
            acc = jnp.zeros((BLOCK, LANES), F32)
            for half in (0, 1):
                h = 2 * cgrp + half
                kh = h // grp_sz
                p, _ = _scores(qgrp, kpl[kh, half], valid, sink_ref[0, h])
                acc += jnp.dot(p.astype(BF16), vpl[kh, half], preferred_element_type=F32)
            attn_ref[:, cgrp * LANES:(cgrp + 1) * LANES] = acc
            col = cgrp * LANES
            ga = ga_refs[col // COLT][:, col % COLT:col % COLT + LANES]
            mix_ref[:, cgrp * LANES:(cgrp + 1) * LANES] = (acc * (ga * _sigmoid(ga))).astype(BF16)

    own = lambda n: (n, 0)
    prev = lambda n: (jnp.maximum(n - 1, 0), 0)
    kvs = lambda imap: pl.BlockSpec((BLOCK, KV_WIDTH), imap)
    ga_specs = [pl.BlockSpec((BLOCK, COLT), functools.partial(lambda n, j: (n, ga_off + j), j=j))
                for j in range(n_ga)]
    return _call(body, name="attn_fwd", grid=(nb,),
                 in_specs=[pl.BlockSpec((BLOCK, aw), own), kvs(own), kvs(prev), kvs(own), kvs(prev)]
                 + ga_specs + [pl.BlockSpec(memory_space=pltpu.SMEM)],
                 out_specs=[pl.BlockSpec((BLOCK, aw), own), pl.BlockSpec((BLOCK, aw), own)],
                 out_shape=[jax.ShapeDtypeStruct((s, aw), F32), jax.ShapeDtypeStruct((s, d), BF16)],
                 compiler_params=_params(("parallel",)))(qs, ks, ks, vb, vb, *([z] * n_ga), sinks)


def gate_bwd(d_mix, attn, z, aw, ga_off, after=()):
    s, in_w = z.shape
    tm = _tile(s, 256)

    def body(dm_ref, at_ref, ga_ref, do_ref, dz_ref):
        ga = ga_ref[...]
        sig = _sigmoid(ga)
        dm = dm_ref[...]
        do_ref[...] = (dm * (ga * sig)).astype(BF16)
        dz_ref[...] = ((dm * at_ref[...]) * (sig * (1.0 + ga * (1.0 - sig)))).astype(BF16)

    blk = pl.BlockSpec((tm, COLT), lambda i, j: (i, j))
    gab = pl.BlockSpec((tm, COLT), lambda i, j: (i, ga_off + j))
    return _call_after(body, after, name="gate_bwd", grid=(s // tm, aw // COLT),
                       in_specs=[blk, blk, gab], out_specs=[blk, gab],
                       out_shape=[jax.ShapeDtypeStruct((s, aw), BF16), jax.ShapeDtypeStruct((s, in_w), BF16)],
                       compiler_params=_params(("parallel", "parallel")))(d_mix, attn, z)


def attn_bwd(qs, ks, vb, d_o, sinks, aw):
    s = qs.shape[0]
    nb = s // BLOCK
    nq = aw // HEAD_DIM
    grp_sz = nq // N_KV_HEADS

    def body(q_ref, ko_ref, kp_ref, vo_ref, vp_ref, do_ref, sink_ref, dq_ref, dk_ref, dv_ref, ds_ref,
             ck_ref, cv_ref):
        n = pl.program_id(0)

        @pl.when(n == 0)
        def _():
            ds_ref[...] = jnp.zeros_like(ds_ref)
            dk_ref[...] = jnp.zeros_like(dk_ref)
            dv_ref[...] = jnp.zeros_like(dv_ref)

        @pl.when(n < nb)
        def _():
            lane_idx = lax.broadcasted_iota(jnp.int32, (2 * BLOCK, LANES), 1)
            lane_row = lax.broadcasted_iota(jnp.int32, (1, LANES), 1)
            kpl = _placed(jnp.concatenate([kp_ref[...], ko_ref[...]], axis=0), lane_idx)
            vpl = _placed(jnp.concatenate([vp_ref[...], vo_ref[...]], axis=0), lane_idx)
            valid = _valid_mask(n)
            dk_acc = [jnp.zeros((2 * BLOCK, LANES), F32) for _ in range(N_KV_HEADS)]
            dv_acc = [jnp.zeros((2 * BLOCK, LANES), F32) for _ in range(N_KV_HEADS)]
            ds_row = jnp.zeros((1, LANES), F32)
            for cgrp in range(nq // 2):
                qgrp = q_ref[:, cgrp * LANES:(cgrp + 1) * LANES]
                dog = do_ref[:, cgrp * LANES:(cgrp + 1) * LANES]
                dq_acc = jnp.zeros((BLOCK, LANES), F32)
                for half in (0, 1):
                    h = 2 * cgrp + half
                    kh = h // grp_sz
                    p, p_sink = _scores(qgrp, kpl[kh, half], valid, sink_ref[0, h])
                    dp = lax.dot_general(dog, vpl[kh, half], NT, preferred_element_type=F32)
                    delta = jnp.sum(p * dp, axis=-1, keepdims=True)
                    dsb = (p * (dp - delta)).astype(BF16)
                    dq_acc += jnp.dot(dsb, kpl[kh, half], preferred_element_type=F32)
                    keep = (lane_idx >= half * HEAD_DIM) & (lane_idx < (half + 1) * HEAD_DIM)
                    dkp = jnp.where(keep, lax.dot_general(dsb, qgrp, TN, preferred_element_type=F32), 0.0)
                    dvp = jnp.where(keep, lax.dot_general(p.astype(BF16), dog, TN, preferred_element_type=F32), 0.0)
                    if half != kh % 2:
                        dkp = pltpu.roll(dkp, HEAD_DIM, 1)
                        dvp = pltpu.roll(dvp, HEAD_DIM, 1)
                    dk_acc[kh] += dkp
                    dv_acc[kh] += dvp
                    dsink = -jnp.sum(p_sink * delta, axis=0, keepdims=True)
                    ds_row += jnp.where(lane_row == h, dsink, 0.0)
                dq_ref[:, cgrp * LANES:(cgrp + 1) * LANES] = dq_acc
            ds_ref[0:1, :] += ds_row
            dk_band = jnp.concatenate([dk_acc[0] + dk_acc[1], dk_acc[2] + dk_acc[3]], axis=1)
            dv_band = jnp.concatenate([dv_acc[0] + dv_acc[1], dv_acc[2] + dv_acc[3]], axis=1)

            @pl.when(n > 0)
            def _():
                dk_ref[...] = ck_ref[...] + dk_band[:BLOCK]
                dv_ref[...] = cv_ref[...] + dv_band[:BLOCK]

            ck_ref[...] = dk_band[BLOCK:]
            cv_ref[...] = dv_band[BLOCK:]

        @pl.when(n == nb)
        def _():
            dk_ref[...] = ck_ref[...]
            dv_ref[...] = cv_ref[...]

    own = lambda n: (jnp.minimum(n, nb - 1), 0)
    prev = lambda n: (jnp.clip(n - 1, 0, nb - 1), 0)
    done = lambda n: (jnp.maximum(n - 1, 0), 0)
    kvs = lambda imap: pl.BlockSpec((BLOCK, KV_WIDTH), imap)
    return _call(body, name="attn_bwd", grid=(nb + 1,),
                 in_specs=[pl.BlockSpec((BLOCK, aw), own), kvs(own), kvs(prev), kvs(own), kvs(prev),
                           pl.BlockSpec((BLOCK, aw), own), pl.BlockSpec(memory_space=pltpu.SMEM)],
                 out_specs=[pl.BlockSpec((BLOCK, aw), own), kvs(done), kvs(done),
                            pl.BlockSpec((SUBLANES, LANES), lambda n: (0, 0))],
                 out_shape=[jax.ShapeDtypeStruct((s, aw), F32), jax.ShapeDtypeStruct((s, KV_WIDTH), F32),
                            jax.ShapeDtypeStruct((s, KV_WIDTH), F32), jax.ShapeDtypeStruct((SUBLANES, LANES), F32)],
                 scratch_shapes=[pltpu.VMEM((BLOCK, KV_WIDTH), F32), pltpu.VMEM((BLOCK, KV_WIDTH), F32)],
                 compiler_params=_params(("arbitrary",)))(qs, ks, ks, vb, vb, d_o, sinks)


def conv_fwd(z, conv_w, mix, aw, cw, b_off):
    s = z.shape[0]
    tm = _tile(s, 256)
    nseg = cw // COLT
    hb = tm // SUBLANES

    def body(b_ref, c_ref, h_ref, g_ref, cp_ref, hp_ref, w_ref, mix_in, mix_ref):
        i = pl.program_id(0)
        u = c_ref[...] * h_ref[...]
        up = jnp.where(i > 0, cp_ref[...] * hp_ref[...], 0.0)
        ext = jnp.concatenate([up, u], axis=0)
        um1 = pltpu.roll(ext, 1, 0)[SUBLANES:]
        um2 = pltpu.roll(ext, 2, 0)[SUBLANES:]
        w = w_ref[...]
        cv = w[0:1] * um2 + w[1:2] * um1 + w[2:3] * u
        g = g_ref[...]
        mix_ref[...] = ((b_ref[...] * cv) * (g * _sigmoid(g))).astype(BF16)

    seg = lambda k: pl.BlockSpec((tm, COLT), functools.partial(lambda i, j, k: (i, b_off + k * nseg + j), k=k))
    halo = lambda k: pl.BlockSpec(
        (SUBLANES, COLT), functools.partial(lambda i, j, k: (jnp.maximum(i * hb - 1, 0), b_off + k * nseg + j), k=k))
    return _call(body, name="conv_fwd", grid=(s // tm, nseg),
                 in_specs=[seg(0), seg(1), seg(2), seg(3), halo(1), halo(2),
                           pl.BlockSpec((SUBLANES, COLT), lambda i, j: (0, j)), _hbm()],
                 out_specs=pl.BlockSpec((tm, COLT), lambda i, j: (i, aw // COLT + j)),
                 out_shape=jax.ShapeDtypeStruct(mix.shape, BF16),
                 input_output_aliases={7: 0},
                 compiler_params=_params(("parallel", "parallel")))(z, z, z, z, z, z, conv_w, mix)


def conv_bwd(z, d_mix, conv_w, dz, aw, cw, b_off):
    s = z.shape[0]
    tm = _tile(s, 256)
    nseg = cw // COLT
    hb = tm // SUBLANES
    n_row = s // tm
    last_h = s // SUBLANES - 1

    def body(b_ref, c_ref, h_ref, g_ref, cp_ref, hp_ref, bn_ref, gn_ref, dm_ref, dmn_ref, w_ref, dz_in,
             dz_ref, acc_ref, stash_ref):
        i = pl.program_id(1)
        k = pl.program_id(2)

        @pl.when(k == 0)
        def _():
            cc, hh, bb, g = c_ref[...], h_ref[...], b_ref[...], g_ref[...]
            w = w_ref[...]
            u = cc * hh
            up = jnp.where(i > 0, cp_ref[...] * hp_ref[...], 0.0)
            ext = jnp.concatenate([up, u], axis=0)
            um1 = pltpu.roll(ext, 1, 0)[SUBLANES:]
            um2 = pltpu.roll(ext, 2, 0)[SUBLANES:]
            cv = w[0:1] * um2 + w[1:2] * um1 + w[2:3] * u
            sig = _sigmoid(g)
            sg = g * sig
            dm = dm_ref[...]
            d_cv = (dm * bb) * sg
            gn = gn_ref[...]
            d_cv_next = jnp.where(i < n_row - 1, (dmn_ref[...] * bn_ref[...]) * (gn * _sigmoid(gn)), 0.0)
            ext2 = jnp.concatenate([d_cv, d_cv_next], axis=0)
            dp1 = pltpu.roll(ext2, tm + SUBLANES - 1, 0)[:tm]
            dp2 = pltpu.roll(ext2, tm + SUBLANES - 2, 0)[:tm]
            d_u = w[2:3] * d_cv + w[1:2] * dp1 + w[0:1] * dp2
            stash_ref[0] = ((dm * cv) * sg).astype(BF16)
            stash_ref[1] = (d_u * hh).astype(BF16)
            stash_ref[2] = (d_u * cc).astype(BF16)
            stash_ref[3] = (((dm * bb) * cv) * (sig * (1.0 + g * (1.0 - sig)))).astype(BF16)

            @pl.when(i == 0)
            def _():
                acc_ref[...] = jnp.zeros_like(acc_ref)

            acc_ref[0:1, :] += jnp.sum(d_cv * um2, axis=0, keepdims=True)
            acc_ref[1:2, :] += jnp.sum(d_cv * um1, axis=0, keepdims=True)
            acc_ref[2:3, :] += jnp.sum(d_cv * u, axis=0, keepdims=True)

        dz_ref[...] = stash_ref[k]

    seg = lambda q: pl.BlockSpec((tm, COLT), functools.partial(lambda j, i, k, q: (i, b_off + q * nseg + j), q=q))
    halo_p = lambda q: pl.BlockSpec(
        (SUBLANES, COLT),
        functools.partial(lambda j, i, k, q: (jnp.maximum(i * hb - 1, 0), b_off + q * nseg + j), q=q))
    halo_n = lambda q: pl.BlockSpec(
        (SUBLANES, COLT),
        functools.partial(lambda j, i, k, q: (jnp.minimum((i + 1) * hb, last_h), b_off + q * nseg + j), q=q))
    return _call(body, name="conv_bwd", grid=(nseg, n_row, 4),
                 in_specs=[seg(0), seg(1), seg(2), seg(3), halo_p(1), halo_p(2), halo_n(0), halo_n(3),
                           pl.BlockSpec((tm, COLT), lambda j, i, k: (i, aw // COLT + j)),
                           pl.BlockSpec((SUBLANES, COLT),
                                        lambda j, i, k: (jnp.minimum((i + 1) * hb, last_h), aw // COLT + j)),
                           pl.BlockSpec((SUBLANES, COLT), lambda j, i, k: (0, j)), _hbm()],
                 out_specs=[pl.BlockSpec((tm, COLT), lambda j, i, k: (i, b_off + k * nseg + j)),
                            pl.BlockSpec((SUBLANES, COLT), lambda j, i, k: (0, j))],
                 out_shape=[jax.ShapeDtypeStruct(dz.shape, BF16), jax.ShapeDtypeStruct((SUBLANES, cw), F32)],
                 input_output_aliases={11: 0},
                 scratch_shapes=[pltpu.VMEM((4, tm, COLT), BF16)],
                 compiler_params=_params(("arbitrary", "arbitrary", "arbitrary")))(
                     z, z, z, z, z, z, z, z, d_mix, d_mix, conv_w, dz)


def _place():
    x, y, c = lax.axis_index("x"), lax.axis_index("y"), lax.axis_index("c")
    chips = [(1 - x, y), (x, 1 - y), (1 - x, 1 - y)]
    return x, y, c, chips


def _remote(src, dst, send_sem, recv_sem, device):
    return pltpu.make_async_remote_copy(src_ref=src, dst_ref=dst, send_sem=send_sem, recv_sem=recv_sem,
                                        device_id=device, device_id_type=MESH)


def plan_gather_ici(n_split):
    def plan(refs, send, recv):
        x, y, c, chips = _place()
        me = 2 * x + y
        mine, theirs = [], []
        for w, ref in enumerate(refs):
            for j, (cx, cy) in enumerate(chips):
                def part(slot):
                    if w >= n_split:
                        return ref.at[slot]
                    hr = ref.shape[1] // 2
                    return ref.at[slot, pl.ds(c * hr, hr)]
                k = 3 * w + j
                mine.append(_remote(part(me), part(me), send.at[k], recv.at[k], (cx, cy, c)))
                theirs.append(_remote(part(2 * cx + cy), part(2 * cx + cy), send.at[k], recv.at[k], (cx, cy, c)))
        return mine, theirs
    return plan


def plan_gather_pass(refs, send, recv):
    x, y, c, chips = _place()
    mine, theirs = [], []
    for w, ref in enumerate(refs):
        hr = ref.shape[1] // 2
        for j, (cx, cy) in enumerate(chips):
            half = lambda core: ref.at[2 * cx + cy, pl.ds(core * hr, hr)]
            k = 3 * w + j
            mine.append(_remote(half(c), half(c), send.at[k], recv.at[k], (x, y, 1 - c)))
            theirs.append(_remote(half(1 - c), half(1 - c), send.at[k], recv.at[k], (x, y, 1 - c)))
    return mine, theirs


def plan_swap(refs, send, recv):
    x, y, c, _ = _place()
    nw = len(refs) // 2
    mine = []
    for w in range(nw):
        hr = refs[w].shape[1] // 2
        mine.append(_remote(refs[w].at[:, pl.ds((1 - c) * hr, hr), :], refs[nw + w], send.at[w], recv.at[w],
                            (x, y, 1 - c)))
    return mine, mine


def plan_scatter(refs, send, recv):
    x, y, c, chips = _place()
    me = 2 * x + y
    nw = len(refs) // 2
    mine, theirs = [], []
    for w in range(nw):
        for j, (cx, cy) in enumerate(chips):
            k = 3 * w + j
            mine.append(_remote(refs[w].at[2 * cx + cy], refs[nw + w].at[me], send.at[k], recv.at[k], (cx, cy, c)))
            theirs.append(_remote(refs[w].at[me], refs[nw + w].at[2 * cx + cy], send.at[k], recv.at[k], (cx, cy, c)))
    return mine, theirs


def plan_join(refs, send, recv):
    x, y, c, _ = _place()
    mine, theirs = [], []
    for w, ref in enumerate(refs):
        hr = ref.shape[0] // 2
        half = lambda core: ref.at[pl.ds(core * hr, hr)]
        mine.append(_remote(half(c), half(c), send.at[w], recv.at[w], (x, y, 1 - c)))
        theirs.append(_remote(half(1 - c), half(1 - c), send.at[w], recv.at[w], (x, y, 1 - c)))
    return mine, theirs


def exchange(name, plan, arrays, n):
    na = len(arrays)

    def body(*refs):
        mine, theirs = plan(refs[:na], refs[2 * na], refs[2 * na + 1])
        for cp in mine:
            cp.start()
        for cp in theirs:
            cp.wait_recv()
        for cp in mine:
            cp.wait_send()

    return _call(body, name=name, in_specs=[_hbm()] * na, out_specs=[_hbm()] * na,
                 out_shape=[jax.ShapeDtypeStruct(a.shape, a.dtype) for a in arrays],
                 input_output_aliases={i: i for i in range(na)},
                 scratch_shapes=[pltpu.SemaphoreType.DMA((n,)), pltpu.SemaphoreType.DMA((n,))])(*arrays)


def exchange_start(name, plan, arrays, n, after=()):
    na = len(arrays)

    def body(*refs):
        mine, _ = plan(refs[:na], refs[na], refs[na + 1])
        for cp in mine:
            cp.start()
        refs[-1][...] = jnp.zeros_like(refs[-1])

    hbm = pl.BlockSpec(memory_space=pltpu.HBM)
    sem = pl.BlockSpec(memory_space=pltpu.SEMAPHORE)
    return _call_after(body, after, name=name, in_specs=[hbm] * na,
                       out_specs=[sem, sem] + [hbm] * na + [pl.BlockSpec(memory_space=pltpu.VMEM)],
                       out_shape=[pltpu.SemaphoreType.DMA((n,)), pltpu.SemaphoreType.DMA((n,))]
                       + [pltpu.HBM(a.shape, a.dtype) for a in arrays]
                       + [jax.ShapeDtypeStruct((SUBLANES, LANES), F32)],
                       input_output_aliases={i: i + 2 for i in range(na)},
                       compiler_params=pltpu.CompilerParams(
                           has_side_effects=pltpu.SideEffectType.DATAFLOW_SIDE_EFFECTING))(
                               *[pltpu.with_memory_space_constraint(a, pltpu.HBM) for a in arrays])


def exchange_wait(name, plan, handle, after):
    send_sems, recv_sems, arrays = handle[0], handle[1], handle[2:-1]
    na = len(arrays)

    def body(*refs):
        mine, theirs = plan(refs[:na], refs[na], refs[na + 1])
        for cp in mine:
            cp.wait_send()
        for cp in theirs:
            cp.wait_recv()

    hbm = pl.BlockSpec(memory_space=pltpu.HBM)
    sem = pl.BlockSpec(memory_space=pltpu.SEMAPHORE)
    return _call(body, name=name, in_specs=[hbm] * na + [sem, sem, _hbm()], out_specs=[hbm] * na,
                 out_shape=[pltpu.HBM(a.shape, a.dtype) for a in arrays],
                 input_output_aliases={i: i for i in range(na)},
                 compiler_params=pltpu.CompilerParams(
                     has_side_effects=pltpu.SideEffectType.DATAFLOW_SIDE_EFFECTING))(
                         *arrays, send_sems, recv_sems, after)


def allgather_small(small):
    rows, width = small.shape

    def body(in_ref, out_ref, send, recv, loc):
        x, y, c, _ = _place()
        me = 4 * x + 2 * y + c
        local = pltpu.make_async_copy(in_ref, out_ref.at[me], loc)
        local.start()
        flips = [(fx, fy, fc) for fx in (0, 1) for fy in (0, 1) for fc in (0, 1)][1:]

        def cp(k, slot):
            fx, fy, fc = flips[k]
            return pltpu.make_async_remote_copy(
                src_ref=in_ref, dst_ref=out_ref.at[slot], send_sem=send.at[k], recv_sem=recv.at[k],
                device_id=(x ^ fx, y ^ fy, c ^ fc), device_id_type=MESH)

        for k in range(7):
            cp(k, me).start()
        for k in range(7):
            fx, fy, fc = flips[k]
            cp(k, 4 * (x ^ fx) + 2 * (y ^ fy) + (c ^ fc)).wait_recv()
        for k in range(7):
            cp(k, me).wait_send()
        local.wait()

    return _call(body, name="allgather_small",
                 in_specs=[pl.BlockSpec(memory_space=pltpu.VMEM)],
                 out_specs=pl.BlockSpec(memory_space=pltpu.VMEM),
                 out_shape=jax.ShapeDtypeStruct((8, rows, width), F32),
                 scratch_shapes=[pltpu.SemaphoreType.DMA((7,)), pltpu.SemaphoreType.DMA((7,)),
                                 pltpu.SemaphoreType.DMA])(small)


def add_sibling(grad, got, core, name):
    _, r, c = grad.shape
    hr = r // 2
    tr = _tile(hr, 128)
    nblk = hr // tr

    def body(core_ref, g_ref, o_ref, out_ref):
        out_ref[...] = (g_ref[...].astype(F32) + o_ref[...].astype(F32)).astype(BF16)

    grid_spec = pltpu.PrefetchScalarGridSpec(
        num_scalar_prefetch=1, grid=(N_CHIPS, nblk),
        in_specs=[pl.BlockSpec((None, tr, c), lambda t, i, core_ref: (t, core_ref[0] * nblk + i, 0)),
                  pl.BlockSpec((None, tr, c), lambda t, i, core_ref: (t, i, 0))],
        out_specs=pl.BlockSpec((None, tr, c), lambda t, i, core_ref: (t, i, 0)))
    return _call(body, name=name, grid_spec=grid_spec,
                 out_shape=jax.ShapeDtypeStruct((N_CHIPS, hr, c), BF16),
                 compiler_params=_params(("parallel", "parallel")))(core, grad, got)


def sum_chips(mine, owned, place, name):
    _, hr, c = mine.shape
    tr = _tile(hr, 128)
    nblk = hr // tr

    def body(place_ref, m_ref, o1_ref, o2_ref, o3_ref, out_ref):
        acc = m_ref[...].astype(F32)
        for o_ref in (o1_ref, o2_ref, o3_ref):
            acc = acc + o_ref[...].astype(F32)
        out_ref[...] = acc

    other = lambda k: pl.BlockSpec((None, tr, c), lambda i, place_ref: ((place_ref[0] + k) % N_CHIPS, i, 0))
    grid_spec = pltpu.PrefetchScalarGridSpec(
        num_scalar_prefetch=1, grid=(nblk,),
        in_specs=[other(0), other(1), other(2), other(3)],
        out_specs=pl.BlockSpec((tr, c), lambda i, place_ref: (place_ref[1] * nblk + i, 0)))
    return _call(body, name=name, grid_spec=grid_spec,
                 out_shape=jax.ShapeDtypeStruct((2 * hr, c), F32),
                 compiler_params=_params(("parallel",)))(place, mine, owned, owned, owned)


def sum_devices(gathered):
    _, rows, width = gathered.shape

    def body(g_ref, out_ref):
        acc = g_ref[0]
        for dev in range(1, 8):
            acc = acc + g_ref[dev]
        out_ref[...] = acc
        tail = acc[SUBLANES:]
        out_ref[SUBLANES:, :] = jnp.broadcast_to(jnp.sum(tail, axis=1, keepdims=True), tail.shape)

    return _call(body, name="sum_devices",
                 in_specs=[pl.BlockSpec(memory_space=pltpu.VMEM)],
                 out_specs=pl.BlockSpec(memory_space=pltpu.VMEM),
                 out_shape=jax.ShapeDtypeStruct((rows, width), F32))(gathered)


def _rope_tables(s):
    half = ROT_DIM // 2
    inv_freq = jnp.power(jnp.float32(ROPE_THETA), -jnp.arange(half, dtype=F32) * 2.0 / ROT_DIM)
    ang = jnp.arange(s).astype(F32)[:, None] * inv_freq[None, :]
    cos, sin = jnp.cos(ang), jnp.sin(ang)
    zeros = lambda n: jnp.zeros((s, n), F32)
    cos64 = jnp.concatenate([cos, cos, jnp.ones((s, HEAD_DIM - ROT_DIM), F32)], axis=1)
    sa64 = jnp.concatenate([-sin, zeros(HEAD_DIM - half)], axis=1)
    sb64 = jnp.concatenate([zeros(half), sin, zeros(HEAD_DIM - ROT_DIM)], axis=1)
    return tuple(jnp.concatenate([t, t], axis=1) for t in (cos64, sa64, sb64))


def _pad_rows(a, rows):
    return jnp.pad(a, ((0, rows - a.shape[0]), (0, 0)))


def _pad_cols(a, cols):
    return jnp.pad(a, ((0, 0), (0, cols - a.shape[1])))


def kernel(x, p, norm_gain, w_in, q_norm_gain, k_norm_gain, attn_sinks, conv_w, w_out, ple_gate_norm_gain, w_ple_gate, b_ple_gate, w_ple_proj, ple_norm_gain, loss_target, m_norm_gain, m_w_in, m_q_norm_gain, m_k_norm_gain, m_attn_sinks, m_conv_w, m_w_out, m_ple_gate_norm_gain, m_w_ple_gate, m_b_ple_gate, m_w_ple_proj, m_ple_norm_gain, v_norm_gain, v_w_in, v_q_norm_gain, v_k_norm_gain, v_attn_sinks, v_conv_w, v_w_out, v_ple_gate_norm_gain, v_w_ple_gate, v_b_ple_gate, v_w_ple_proj, v_ple_norm_gain):
    x2, p2, tgt = x[0], p[0, 0], loss_target[0]
    s, d = x2.shape
    ple = p2.shape[1]
    aw = d // 2
    cw = d - aw
    nq = aw // HEAD_DIM
    sh = w_in.shape[2]
    in_w = N_CHIPS * sh
    dq = d // N_CHIPS
    cq = cw // N_CHIPS
    ga_off = (aw + 2 * KV_WIDTH) // COLT
    b_off = (2 * aw + 2 * KV_WIDTH) // COLT
    assert in_w == 2 * aw + 2 * KV_WIDTH + 4 * cw and aw % COLT == 0 and cw % COLT == 0
    assert s % BLOCK == 0 and sh % LANES == 0 and nq % (2 * N_KV_HEADS) == 0

    core = lax.axis_index("c").astype(jnp.int32).reshape(1)
    chip = 2 * lax.axis_index("x") + lax.axis_index("y")

    chip1 = chip.astype(jnp.int32).reshape(1)
    place = jnp.concatenate([chip1, core])
    bufs = [cast_into_slot(w_in[0], chip1, "cast_w_in"), cast_into_slot(w_out[0], chip1, "cast_w_out"),
            cast_into_slot(w_ple_gate[0], chip1, "cast_w_pg"), cast_into_slot(w_ple_proj[0], chip1, "cast_w_pp")]
    conv_buf = lax.dynamic_update_slice(jnp.zeros((N_CHIPS, SUBLANES, cq), F32),
                                        _pad_rows(conv_w[0], SUBLANES)[None], (chip, 0, 0))
    ici_wi = exchange_start("gather_wi_start", plan_gather_ici(1), bufs[:1], 3)
    ici_rest = exchange_start("gather_rest_start", plan_gather_ici(3), bufs[1:] + [conv_buf], 12,
                              after=(ici_wi[-1],))

    tm = _tile(s, 1024)
    tn = _tile(d, 1024)
    tk = _tile(d, 2048)
    ts = _tile(s, 2048)
    h = rms_fwd(x2, norm_gain, "rms_fwd_x", after=(ici_rest[-1],))
    tabs = _rope_tables(s)
    qg = jnp.tile(q_norm_gain, (1, LANES // HEAD_DIM))
    kg = jnp.tile(k_norm_gain, (1, LANES // HEAD_DIM))
    seg_i = jnp.arange(SEG) // HEAD_DIM
    segb = (seg_i[:, None] == seg_i[None, :]).astype(BF16)
    wi_all, = exchange_wait("gather_wi_wait", plan_gather_ici(1), ici_wi, h)
    wi_all, = exchange("gather_wi_pass", plan_gather_pass, [wi_all], 3)
    z = matmul(h, wi_all, grid=(s // tm, N_CHIPS, d // tk),
               a_spec=pl.BlockSpec((tm, tk), lambda i, j, k: (i, k)),
               b_spec=pl.BlockSpec((None, tk, sh), lambda i, j, k: (j, k, 0)),
               o_spec=pl.BlockSpec((tm, sh), lambda i, j, k: (i, j)),
               out_shape=jax.ShapeDtypeStruct((s, in_w), F32), dims=NN, name="mm_z")
    wo_all, wg_all, wp_all, conv_all = exchange_wait("gather_rest_wait", plan_gather_ici(3), ici_rest, z)
    wo_all, wg_all, wp_all = exchange("gather_rest_pass", plan_gather_pass, [wo_all, wg_all, wp_all], 9)
    wo_full = wo_all.reshape(d, d)
    wg_full = wg_all.reshape(d, d)
    conv_full = conv_all.transpose(1, 0, 2).reshape(SUBLANES, cw)
    qs, ks, vb = qk_prep_fwd(z, tabs, qg, kg, segb, aw)
    attn, mix = attn_fwd(qs, ks, vb, z, attn_sinks, aw, d, ga_off)
    mix = conv_fwd(z, conv_full, mix, aw, cw, b_off)
    sq = lambda shape: dict(
        a_spec=pl.BlockSpec((tm, tk), lambda i, j, k: (i, k)),
        o_spec=pl.BlockSpec((tm, tn), lambda i, j, k: (i, j)),
        out_shape=jax.ShapeDtypeStruct(shape, F32))
    x1 = matmul(mix, wo_full, grid=(s // tm, d // tn, d // tk),
                b_spec=pl.BlockSpec((tk, tn), lambda i, j, k: (k, j)), dims=NN, name="mm_x1",
                res=x2, res_spec=pl.BlockSpec((tm, tn), lambda i, j, k: (i, j)), **sq((s, d)))
    hg = rms_fwd(x1, ple_gate_norm_gain, "rms_fwd_x1")
    gl = matmul(hg, wg_full, grid=(s // tm, d // tn, d // tk),
                b_spec=pl.BlockSpec((tk, tn), lambda i, j, k: (k, j)), dims=NN, name="mm_gl", **sq((s, d)))
    pq = d // N_CHIPS
    pe = matmul(p2, wp_all, grid=(s // tm, N_CHIPS, 1),
                a_spec=pl.BlockSpec((tm, ple), lambda i, j, k: (i, 0)),
                b_spec=pl.BlockSpec((None, ple, pq), lambda i, j, k: (j, 0, 0)),
                o_spec=pl.BlockSpec((tm, pq), lambda i, j, k: (i, j)),
                out_shape=jax.ShapeDtypeStruct((s, d), F32), dims=NN, name="mm_pe")

    d_gl, d_pe, dy, acc_head = head_fwd_bwd(x1, gl, pe, tgt, b_ple_gate, ple_norm_gain)
    d_hg = matmul(d_gl, wg_full, grid=(s // tm, d // tn, d // tk),
                  b_spec=pl.BlockSpec((tn, tk), lambda i, j, k: (j, k)), dims=NT, name="mm_d_hg", **sq((s, d)))
    wgrad = lambda a_cols, shape3, o_spec, b_cols, name, a, b, grid: matmul(
        a, b, grid=grid,
        a_spec=pl.BlockSpec((ts, a_cols), lambda i, j, k: (k, i)),
        b_spec=pl.BlockSpec((ts, b_cols), lambda i, j, k: (k, j)),
        o_spec=o_spec, out_shape=jax.ShapeDtypeStruct(shape3, BF16), dims=TN, name=name)
    g_wg = wgrad(tn, (d, d), pl.BlockSpec((tn, tn), lambda i, j, k: (i, j)), tn, "mm_g_wg", hg, d_gl,
                 (d // tn, d // tn, s // ts))
    g_wp = wgrad(ple, (N_CHIPS, ple, pq), pl.BlockSpec((None, ple, pq), lambda i, j, k: (j, 0, 0)), pq,
                 "mm_g_wp", p2, d_pe, (1, N_CHIPS, s // ts))
    d_x1, d_x1b, acc_g = rms_bwd_add(d_hg, x1, dy, ple_gate_norm_gain, "rms_bwd_x1", True)
    d_mix = matmul(d_x1b, wo_full, grid=(s // tm, d // tn, d // tk),
                   b_spec=pl.BlockSpec((tn, tk), lambda i, j, k: (j, k)), dims=NT, name="mm_d_mix", **sq((s, d)))
    g_wo = wgrad(tn, (d, d), pl.BlockSpec((tn, tn), lambda i, j, k: (i, j)), tn, "mm_g_wo", mix, d_x1b,
                 (d // tn, d // tn, s // ts))
    def reduce_start(tag, grads):
        n = len(grads)
        lands = [lax.empty((N_CHIPS, a.shape[1] // 2, a.shape[2]), BF16) for a in grads]
        swapped = exchange("swap_" + tag, plan_swap, list(grads) + lands, n)
        parts = [add_sibling(g, o, core, "add_sibling_%s%d" % (tag, i))
                 for i, (g, o) in enumerate(zip(swapped[:n], swapped[n:]))]
        return exchange_start("scatter_%s_start" % tag, plan_scatter,
                              parts + [lax.empty(a.shape, BF16) for a in parts], 3 * n)

    def reduce_finish(tag, handle, after):
        n = (len(handle) - 3) // 2
        got = exchange_wait("scatter_%s_wait" % tag, plan_scatter, handle, after)
        halves = [sum_chips(pt, o, place, "sum_chips_%s%d" % (tag, i))
                  for i, (pt, o) in enumerate(zip(got[:n], got[n:]))]
        return exchange("join_" + tag, plan_join, halves, n)

    early = reduce_start("early", [g_wo.reshape(N_CHIPS, dq, d), g_wg.reshape(N_CHIPS, dq, d), g_wp])
    d_o, dz = gate_bwd(d_mix, attn, z, aw, ga_off, after=(early[-1],))
    dqs, dks, dvs, acc_sink = attn_bwd(qs, ks, vb, d_o, attn_sinks, aw)
    dz, acc_qk = qk_prep_bwd(z, dqs, dks, dvs, tabs, qg, kg, segb, dz, aw)
    dz, acc_conv = conv_bwd(z, d_mix, conv_full, dz, aw, cw, b_off)
    full_wo, full_wg, full_wp = reduce_finish("early", early, dz)
    g_wi = wgrad(tn, (N_CHIPS, d, sh), pl.BlockSpec((None, tn, sh), lambda i, j, k: (j, i, 0)), sh,
                 "mm_g_wi", h, dz, (d // tn, N_CHIPS, s // ts))
    late = reduce_start("late", [g_wi])
    d_h = matmul(dz, wi_all, grid=(s // tm, d // tn, N_CHIPS),
                 a_spec=pl.BlockSpec((tm, sh), lambda i, j, k: (i, k)),
                 b_spec=pl.BlockSpec((None, tn, sh), lambda i, j, k: (k, j, 0)),
                 o_spec=pl.BlockSpec((tm, tn), lambda i, j, k: (i, j)),
                 out_shape=jax.ShapeDtypeStruct((s, d), F32), dims=NT, name="mm_d_h", after=(late[-1],))
    grad_x, acc_x = rms_bwd_add(d_h, x2, d_x1, norm_gain, "rms_bwd_x", False)
    full_wi, = reduce_finish("late", late, grad_x)
    big = {}
    for nm, g, w, m, v in (("w_in", full_wi, w_in, m_w_in, v_w_in), ("w_out", full_wo, w_out, m_w_out, v_w_out),
                           ("w_ple_gate", full_wg, w_ple_gate, m_w_ple_gate, v_w_ple_gate),
                           ("w_ple_proj", full_wp, w_ple_proj, m_w_ple_proj, v_w_ple_proj)):
        big[nm] = [o[None] for o in adamw(g, w[0], m[0], v[0], "adamw_" + nm)]

    wsm = max(d, cw)
    misc = jnp.concatenate([acc_qk[0:1, :HEAD_DIM], acc_qk[1:2, :HEAD_DIM], acc_sink[0:1, :nq]], axis=1)
    small = jnp.concatenate([
        _pad_cols(acc_x[0:1], wsm), _pad_cols(acc_g[0:1], wsm), _pad_cols(acc_head[0:1], wsm),
        _pad_cols(acc_head[1:2], wsm), _pad_cols(misc, wsm), _pad_cols(acc_conv[0:3], wsm)], axis=0)
    both = jnp.concatenate([small, _pad_rows(_pad_cols(acc_head[2:3], wsm), SUBLANES)], axis=0)
    tot = sum_devices(allgather_small(both))
    loss = tot[SUBLANES, 0]

    def pack(vals):
        ng, pg, bg, eg, qgv, kgv, sk, cv = vals
        misc_v = jnp.concatenate([qgv, kgv, sk], axis=1)
        return jnp.concatenate([_pad_cols(ng, wsm), _pad_cols(pg, wsm), _pad_cols(bg, wsm), _pad_cols(eg, wsm),
                                _pad_cols(misc_v, wsm), _pad_cols(cv[0], wsm)], axis=0)

    g_small = jnp.concatenate(
        [tot[0:5], _pad_cols(lax.dynamic_slice(tot[5:8], (0, chip * cq), (3, cq)), wsm)], axis=0)
    w_small = pack((norm_gain, ple_gate_norm_gain, b_ple_gate, ple_norm_gain, q_norm_gain, k_norm_gain,
                    attn_sinks, conv_w))
    m_small = pack((m_norm_gain, m_ple_gate_norm_gain, m_b_ple_gate, m_ple_norm_gain, m_q_norm_gain,
                    m_k_norm_gain, m_attn_sinks, m_conv_w))
    v_small = pack((v_norm_gain, v_ple_gate_norm_gain, v_b_ple_gate, v_ple_norm_gain, v_q_norm_gain,
                    v_k_norm_gain, v_attn_sinks, v_conv_w))
    sm = adamw(g_small, w_small, m_small, v_small, "adamw_small")

    def unpack(a):
        return {"norm_gain": a[0:1, :d], "ple_gate_norm_gain": a[1:2, :d], "b_ple_gate": a[2:3, :d],
                "ple_norm_gain": a[3:4, :d], "q_norm_gain": a[4:5, :HEAD_DIM],
                "k_norm_gain": a[4:5, HEAD_DIM:2 * HEAD_DIM],
                "attn_sinks": a[4:5, 2 * HEAD_DIM:2 * HEAD_DIM + nq], "conv_w": a[5:8, :cq][None]}

    order = ("norm_gain", "w_in", "q_norm_gain", "k_norm_gain", "attn_sinks", "conv_w", "w_out",
             "ple_gate_norm_gain", "w_ple_gate", "b_ple_gate", "w_ple_proj", "ple_norm_gain")
    outs = [loss, grad_x[None]]
    for kind in range(4):
        table = unpack(sm[kind])
        for nm in order:
            outs.append(big[nm][kind] if nm in big else table[nm])
    return tuple(outs)
```

```python
import functools

import jax
import jax.numpy as jnp
from jax import lax
from jax.experimental import pallas as pl
from jax.experimental.pallas import tpu as pltpu

F32 = jnp.float32
BF16 = jnp.bfloat16
MESH = pl.DeviceIdType.MESH

HEAD_DIM = 64
N_KV_HEADS = 4
KV_WIDTH = N_KV_HEADS * HEAD_DIM
BLOCK = 128
ROT_DIM = 16
ROPE_THETA = 500000.0
EPS = 1e-6
NEG_INF = -1e30
N_CHIPS = 4
LANES = 128
SUBLANES = 8
COLT = 512
SEG = 256
VMEM_LIMIT = 48 * 1024 * 1024

ADAM_LR = 0.001
ADAM_B1 = 0.9
ADAM_B2 = 0.999
ADAM_EPS = 1e-08
ADAM_WD = 0.01
ADAM_STEP = 10

NN = (((1,), (0,)), ((), ()))
NT = (((1,), (1,)), ((), ()))
TN = (((0,), (0,)), ((), ()))


def _call(body, **kw):
    return pl.pallas_call(body, **kw)


def _call_after(body, after, **kw):
    n_in, n_after = len(kw["in_specs"]), len(after)
    kw["in_specs"] = list(kw["in_specs"]) + [pl.BlockSpec(memory_space=pl.ANY)] * n_after

    def body_after(*refs):
        body(*refs[:n_in], *refs[n_in + n_after:])

    call = _call(body_after, **kw)
    return lambda *args: call(*args, *after)


def _params(sem):
    return pltpu.CompilerParams(dimension_semantics=sem, vmem_limit_bytes=VMEM_LIMIT)


def _tile(n, pref):
    return pref if n % pref == 0 else n


def _sigmoid(v):
    return 1.0 / (1.0 + jnp.exp(-v))


def _hbm():
    return pl.BlockSpec(memory_space=pl.ANY)


def cast_bf16(a, name):
    r, c = a.shape
    tr = _tile(r, 256)

    def body(a_ref, o_ref):
        o_ref[...] = a_ref[...].astype(BF16)

    return _call(body, name=name, grid=(r // tr,),
                 in_specs=[pl.BlockSpec((tr, c), lambda i: (i, 0))],
                 out_specs=pl.BlockSpec((tr, c), lambda i: (i, 0)),
                 out_shape=jax.ShapeDtypeStruct((r, c), BF16),
                 compiler_params=_params(("parallel",)))(a)


def cast_into_slot(a, chip, name):
    r, c = a.shape
    tr = _tile(r, 256)

    def body(chip_ref, a_ref, o_ref):
        o_ref[...] = a_ref[...].astype(BF16)

    grid_spec = pltpu.PrefetchScalarGridSpec(
        num_scalar_prefetch=1, grid=(r // tr,),
        in_specs=[pl.BlockSpec((tr, c), lambda i, chip_ref: (i, 0))],
        out_specs=pl.BlockSpec((None, tr, c), lambda i, chip_ref: (chip_ref[0], i, 0)))
    return _call(body, name=name, grid_spec=grid_spec,
                 out_shape=jax.ShapeDtypeStruct((N_CHIPS, r, c), BF16),
                 compiler_params=_params(("parallel",)))(chip, a)


def rms_fwd(x, gain, name, after=()):
    s, d = x.shape
    tm = _tile(s, 256)

    def body(x_ref, g_ref, o_ref):
        xf = x_ref[...]
        r = lax.rsqrt(jnp.mean(xf * xf, axis=-1, keepdims=True) + EPS)
        o_ref[...] = ((xf * r) * g_ref[...]).astype(BF16)

    return _call_after(body, after, name=name, grid=(s // tm,),
                       in_specs=[pl.BlockSpec((tm, d), lambda i: (i, 0)),
                                 pl.BlockSpec((1, d), lambda i: (0, 0))],
                       out_specs=pl.BlockSpec((tm, d), lambda i: (i, 0)),
                       out_shape=jax.ShapeDtypeStruct((s, d), BF16),
                       compiler_params=_params(("parallel",)))(x, gain)


def rms_bwd_add(dyn, xin, add, gain, name, want_bf16):
    s, d = xin.shape
    tm = _tile(s, 256)

    def body(dy_ref, x_ref, a_ref, g_ref, *outs):
        i = pl.program_id(0)
        dx_ref, acc_ref = outs[0], outs[-1]
        xf = x_ref[...]
        r = lax.rsqrt(jnp.mean(xf * xf, axis=-1, keepdims=True) + EPS)
        xhat = xf * r
        dyv = dy_ref[...]
        gd = dyv * g_ref[...]
        dx = a_ref[...] + r * (gd - xhat * jnp.mean(xhat * gd, axis=-1, keepdims=True))
        dx_ref[...] = dx
        if want_bf16:
            outs[1][...] = dx.astype(BF16)

        @pl.when(i == 0)
        def _():
            acc_ref[...] = jnp.zeros_like(acc_ref)

        acc_ref[0:1, :] += jnp.sum(dyv * xhat, axis=0, keepdims=True)

    row = pl.BlockSpec((tm, d), lambda i: (i, 0))
    out_specs = [row] + ([row] if want_bf16 else []) + [pl.BlockSpec((SUBLANES, d), lambda i: (0, 0))]
    out_shape = ([jax.ShapeDtypeStruct((s, d), F32)]
                 + ([jax.ShapeDtypeStruct((s, d), BF16)] if want_bf16 else [])
                 + [jax.ShapeDtypeStruct((SUBLANES, d), F32)])
    return _call(body, name=name, grid=(s // tm,),
                 in_specs=[row, row, row, pl.BlockSpec((1, d), lambda i: (0, 0))],
                 out_specs=out_specs, out_shape=out_shape,
                 compiler_params=_params(("arbitrary",)))(dyn, xin, add, gain)


def head_fwd_bwd(x1, gl, pe, tgt, bias, ple_gain):
    s, d = x1.shape
    tm = _tile(s, 128)

    def body(x1_ref, gl_ref, pe_ref, t_ref, b_ref, g_ref, dgl_ref, dpe_ref, dy_ref, acc_ref):
        i = pl.program_id(0)
        gate = _sigmoid(gl_ref[...] + b_ref[...])
        pev = pe_ref[...]
        r = lax.rsqrt(jnp.mean(pev * pev, axis=-1, keepdims=True) + EPS)
        pehat = pev * r
        gain = g_ref[...]
        e = pehat * gain
        diff = (x1_ref[...] + gate * e) - t_ref[...]
        dy = diff * (1.0 / d)
        dy_ref[...] = dy
        d_gl = (dy * e) * (gate * (1.0 - gate))
        dgl_ref[...] = d_gl.astype(BF16)
        d_e = dy * gate
        gd = d_e * gain
        d_pe = r * (gd - pehat * jnp.mean(pehat * gd, axis=-1, keepdims=True))
        dpe_ref[...] = d_pe.astype(BF16)

        @pl.when(i == 0)
        def _():
            acc_ref[...] = jnp.zeros_like(acc_ref)

        acc_ref[0:1, :] += jnp.sum(d_gl, axis=0, keepdims=True)
        acc_ref[1:2, :] += jnp.sum(d_e * pehat, axis=0, keepdims=True)
        acc_ref[2:3, :] += jnp.sum(diff * diff, axis=0, keepdims=True) * (0.5 / d)

    row = pl.BlockSpec((tm, d), lambda i: (i, 0))
    vec = pl.BlockSpec((1, d), lambda i: (0, 0))
    return _call(body, name="head_fwd_bwd", grid=(s // tm,),
                 in_specs=[row, row, row, row, vec, vec],
                 out_specs=[row, row, row, pl.BlockSpec((SUBLANES, d), lambda i: (0, 0))],
                 out_shape=[jax.ShapeDtypeStruct((s, d), BF16), jax.ShapeDtypeStruct((s, d), BF16),
                            jax.ShapeDtypeStruct((s, d), F32), jax.ShapeDtypeStruct((SUBLANES, d), F32)],
                 compiler_params=_params(("arbitrary",)))(x1, gl, pe, tgt, bias, ple_gain)


def adamw(g, w, m, v, name):
    r, c = g.shape
    tr = _tile(r, 128)

    def body(g_ref, w_ref, m_ref, v_ref, go_ref, d_ref, mo_ref, vo_ref):
        gv = g_ref[...]
        mn = ADAM_B1 * m_ref[...] + (1.0 - ADAM_B1) * gv
        vn = ADAM_B2 * v_ref[...] + (1.0 - ADAM_B2) * (gv * gv)
        m_hat = mn / (1.0 - ADAM_B1 ** ADAM_STEP)
        v_hat = vn / (1.0 - ADAM_B2 ** ADAM_STEP)
        go_ref[...] = gv
        d_ref[...] = -ADAM_LR * (m_hat / (jnp.sqrt(v_hat) + ADAM_EPS) + ADAM_WD * w_ref[...])
        mo_ref[...] = mn
        vo_ref[...] = vn

    blk = pl.BlockSpec((tr, c), lambda i: (i, 0))
    shp = jax.ShapeDtypeStruct((r, c), F32)
    return _call(body, name=name, grid=(r // tr,), in_specs=[blk] * 4, out_specs=[blk] * 4,
                 out_shape=[shp] * 4, compiler_params=_params(("parallel",)))(g, w, m, v)


def matmul(a, b, *, grid, a_spec, b_spec, o_spec, out_shape, dims, name, res=None, res_spec=None, after=()):
    nk = grid[2]
    acc_shape = tuple(d for d in o_spec.block_shape if d is not None)

    def body(*refs):
        a_ref, b_ref = refs[:2]
        r_ref = refs[2] if res is not None else None
        o_ref = refs[3] if res is not None else refs[2]
        part = lax.dot_general(a_ref[...].astype(BF16), b_ref[...].astype(BF16), dims, preferred_element_type=F32)

        def finish(out):
            if res is not None:
                out = r_ref[...] + out
            o_ref[...] = out.astype(o_ref.dtype)

        if nk == 1:
            finish(part)
            return
        acc_ref = refs[-1]
        k = pl.program_id(2)

        @pl.when(k == 0)
        def _():
            acc_ref[...] = part

        @pl.when((k > 0) & (k < nk - 1))
        def _():
            acc_ref[...] += part

        @pl.when(k == nk - 1)
        def _():
            finish(acc_ref[...] + part)

    in_specs = [a_spec, b_spec] + ([res_spec] if res is not None else [])
    args = (a, b) + ((res,) if res is not None else ())
    return _call_after(body, after, name=name, grid=grid, in_specs=in_specs, out_specs=o_spec,
                       out_shape=out_shape, scratch_shapes=[pltpu.VMEM(acc_shape, F32)] if nk > 1 else [],
                       compiler_params=_params(("parallel", "parallel", "arbitrary")))(*args)


def in_proj_part(h, w, z_prev, order, q, in_w):
    s, d = h.shape
    sh = w.shape[1]
    tm = _tile(s, 1024)

    def body(order_ref, h_ref, w_ref, *rest):
        rest[-1][...] = jnp.dot(h_ref[...], w_ref[...], preferred_element_type=F32)

    in_specs = [pl.BlockSpec((tm, d), lambda i, order_ref: (i, 0)),
                pl.BlockSpec((d, sh), lambda i, order_ref: (0, 0))]
    args = [order, h, w]
    if z_prev is not None:
        in_specs.append(_hbm())
        args.append(z_prev)
    grid_spec = pltpu.PrefetchScalarGridSpec(
        num_scalar_prefetch=1, grid=(s // tm,), in_specs=in_specs,
        out_specs=pl.BlockSpec((tm, sh), lambda i, order_ref: (i, order_ref[q])))
    return _call(body, name="mm_z%d" % q, grid_spec=grid_spec,
                 out_shape=jax.ShapeDtypeStruct((s, in_w), F32),
                 input_output_aliases={3: 0} if z_prev is not None else {},
                 compiler_params=_params(("parallel",)))(*args)


def in_proj_bwd(dz, ws, order, d, after):
    s = dz.shape[0]
    sh = ws[0].shape[1]
    tm, tn = _tile(s, 512), _tile(d, 512)
    nq, n_after = len(ws), len(after)

    def body(order_ref, *refs):
        a_refs, b_refs, o_ref = refs[:nq], refs[nq:2 * nq], refs[2 * nq + n_after]
        acc = lax.dot_general(a_refs[0][...], b_refs[0][...], NT, preferred_element_type=F32)
        for q in range(1, nq):
            acc += lax.dot_general(a_refs[q][...], b_refs[q][...], NT, preferred_element_type=F32)
        o_ref[...] = acc

    a_spec = lambda q: pl.BlockSpec((tm, sh), functools.partial(lambda i, j, order_ref, q: (i, order_ref[q]), q=q))
    grid_spec = pltpu.PrefetchScalarGridSpec(
        num_scalar_prefetch=1, grid=(s // tm, d // tn),
        in_specs=[a_spec(q) for q in range(nq)]
        + [pl.BlockSpec((tn, sh), lambda i, j, order_ref: (j, 0))] * nq + [_hbm()] * n_after,
        out_specs=pl.BlockSpec((tm, tn), lambda i, j, order_ref: (i, j)))
    return _call(body, name="mm_d_h", grid_spec=grid_spec, out_shape=jax.ShapeDtypeStruct((s, d), F32),
                 compiler_params=_params(("parallel", "parallel")))(order, *([dz] * nq), *ws, *after)


def _seg_mean(sq, segb):
    parts = []
    for cgrp in range(sq.shape[1] // SEG):
        blk = sq[:, cgrp * SEG:(cgrp + 1) * SEG]
        hi = blk.astype(BF16)
        r1 = blk - hi.astype(F32)
        mid = r1.astype(BF16)
        lo = (r1 - mid.astype(F32)).astype(BF16)
        acc = jnp.dot(hi, segb, preferred_element_type=F32)
        acc += jnp.dot(mid, segb, preferred_element_type=F32)
        acc += jnp.dot(lo, segb, preferred_element_type=F32)
        parts.append(acc)
    out = parts[0] if len(parts) == 1 else jnp.concatenate(parts, axis=1)
    return out * (1.0 / HEAD_DIM)


def _rope(v, cos, sa, sb):
    parts = []
    for cgrp in range(v.shape[1] // LANES):
        blk = v[:, cgrp * LANES:(cgrp + 1) * LANES]
        parts.append(blk * cos + pltpu.roll(blk, LANES - 8, 1) * sa + pltpu.roll(blk, 8, 1) * sb)
    return parts[0] if len(parts) == 1 else jnp.concatenate(parts, axis=1)


def _rope_t(dv, cos, sa, sb):
    parts = []
    for cgrp in range(dv.shape[1] // LANES):
        blk = dv[:, cgrp * LANES:(cgrp + 1) * LANES]
        parts.append(blk * cos + pltpu.roll(blk * sa, 8, 1) + pltpu.roll(blk * sb, LANES - 8, 1))
    return parts[0] if len(parts) == 1 else jnp.concatenate(parts, axis=1)


def _tile_lanes(vec, width):
    reps = width // LANES
    return vec if reps == 1 else jnp.tile(vec, (1, reps))


def qk_prep_fwd(z, tabs, qg, kg, segb, aw):
    s = z.shape[0]
    tm = _tile(s, 256)
    wq = aw + 2 * KV_WIDTH

    def body(z_ref, cos_ref, sa_ref, sb_ref, qg_ref, kg_ref, seg_ref, qs_ref, ks_ref, vb_ref):
        zz = z_ref[...]
        q, k, v = zz[:, :aw], zz[:, aw:aw + KV_WIDTH], zz[:, aw + KV_WIDTH:]
        cos, sa, sb, segm = cos_ref[...], sa_ref[...], sb_ref[...], seg_ref[...]
        rq = lax.rsqrt(_seg_mean(q * q, segm) + EPS)
        qn = (q * rq) * _tile_lanes(qg_ref[...], aw)
        qs_ref[...] = (_rope(qn, cos, sa, sb) * (HEAD_DIM ** -0.5)).astype(BF16)
        rk = lax.rsqrt(_seg_mean(k * k, segm) + EPS)
        kn = (k * rk) * _tile_lanes(kg_ref[...], KV_WIDTH)
        ks_ref[...] = _rope(kn, cos, sa, sb).astype(BF16)
        vb_ref[...] = v.astype(BF16)

    tab = pl.BlockSpec((tm, LANES), lambda i: (i, 0))
    vec = pl.BlockSpec((1, LANES), lambda i: (0, 0))
    return _call(body, name="qk_prep_fwd", grid=(s // tm,),
                 in_specs=[pl.BlockSpec((tm, wq), lambda i: (i, 0)), tab, tab, tab, vec, vec,
                           pl.BlockSpec((SEG, SEG), lambda i: (0, 0))],
                 out_specs=[pl.BlockSpec((tm, aw), lambda i: (i, 0)),
                            pl.BlockSpec((tm, KV_WIDTH), lambda i: (i, 0)),
                            pl.BlockSpec((tm, KV_WIDTH), lambda i: (i, 0))],
                 out_shape=[jax.ShapeDtypeStruct((s, aw), BF16), jax.ShapeDtypeStruct((s, KV_WIDTH), BF16),
                            jax.ShapeDtypeStruct((s, KV_WIDTH), BF16)],
                 compiler_params=_params(("parallel",)))(z, *tabs, qg, kg, segb)


def qk_prep_bwd(z, dqs, dks, dvs, tabs, qg, kg, segb, dz, aw):
    s = z.shape[0]
    tm = _tile(s, 256)
    wq = aw + 2 * KV_WIDTH

    def body(z_ref, dq_ref, dk_ref, dv_ref, cos_ref, sa_ref, sb_ref, qg_ref, kg_ref, seg_ref, dz_in,
             dz_ref, acc_ref):
        i = pl.program_id(0)
        zz = z_ref[...]
        q, k = zz[:, :aw], zz[:, aw:aw + KV_WIDTH]
        cos, sa, sb, segm = cos_ref[...], sa_ref[...], sb_ref[...], seg_ref[...]

        def one(xv, dout, gvec, width):
            g = _tile_lanes(gvec, width)
            r = lax.rsqrt(_seg_mean(xv * xv, segm) + EPS)
            xhat = xv * r
            dn = _rope_t(dout, cos, sa, sb)
            gd = dn * g
            dx = r * (gd - xhat * _seg_mean(xhat * gd, segm))
            contrib = jnp.sum(dn * xhat, axis=0, keepdims=True)
            folded = contrib[:, :LANES]
            for cgrp in range(1, width // LANES):
                folded = folded + contrib[:, cgrp * LANES:(cgrp + 1) * LANES]
            return dx, folded + pltpu.roll(folded, HEAD_DIM, 1)

        dq, gq = one(q, dq_ref[...] * (HEAD_DIM ** -0.5), qg_ref[...], aw)
        dk, gk = one(k, dk_ref[...], kg_ref[...], KV_WIDTH)
        dz_ref[:, :aw] = dq.astype(BF16)
        dz_ref[:, aw:aw + KV_WIDTH] = dk.astype(BF16)
        dz_ref[:, aw + KV_WIDTH:] = dv_ref[...].astype(BF16)

        @pl.when(i == 0)
        def _():
            acc_ref[...] = jnp.zeros_like(acc_ref)

        acc_ref[0:1, :] += gq
        acc_ref[1:2, :] += gk

    tab = pl.BlockSpec((tm, LANES), lambda i: (i, 0))
    vec = pl.BlockSpec((1, LANES), lambda i: (0, 0))
    kvb = pl.BlockSpec((tm, KV_WIDTH), lambda i: (i, 0))
    return _call(body, name="qk_prep_bwd", grid=(s // tm,),
                 in_specs=[pl.BlockSpec((tm, wq), lambda i: (i, 0)),
                           pl.BlockSpec((tm, aw), lambda i: (i, 0)), kvb, kvb, tab, tab, tab, vec, vec,
                           pl.BlockSpec((SEG, SEG), lambda i: (0, 0)), _hbm()],
                 out_specs=[pl.BlockSpec((tm, wq), lambda i: (i, 0)),
                            pl.BlockSpec((SUBLANES, LANES), lambda i: (0, 0))],
                 out_shape=[jax.ShapeDtypeStruct(dz.shape, BF16), jax.ShapeDtypeStruct((SUBLANES, LANES), F32)],
                 input_output_aliases={10: 0},
                 compiler_params=_params(("arbitrary",)))(z, dqs, dks, dvs, *tabs, qg, kg, segb, dz)


def _placed(band, lane_idx):
    out = {}
    for kh in range(N_KV_HEADS):
        grp = band[:, (kh // 2) * LANES:(kh // 2 + 1) * LANES]
        for half in (0, 1):
            t = grp if half == kh % 2 else pltpu.roll(grp, HEAD_DIM, 1)
            keep = (lane_idx >= half * HEAD_DIM) & (lane_idx < (half + 1) * HEAD_DIM)
            out[kh, half] = jnp.where(keep, t, jnp.zeros_like(t))
    return out


def _scores(qgrp, kpl, valid, sink):
    sc = lax.dot_general(qgrp, kpl, NT, preferred_element_type=F32)
    sc = jnp.where(valid, sc, NEG_INF)
    m = jnp.maximum(jnp.max(sc, axis=-1, keepdims=True), sink)
    e = jnp.exp(sc - m)
    es = jnp.exp(sink - m)
    den = jnp.sum(e, axis=-1, keepdims=True) + es
    return e / den, es / den


def _valid_mask(n):
    r_i = lax.broadcasted_iota(jnp.int32, (BLOCK, 2 * BLOCK), 0)
    j_i = lax.broadcasted_iota(jnp.int32, (BLOCK, 2 * BLOCK), 1)
    return (j_i > r_i) & (j_i <= r_i + BLOCK) & ((n > 0) | (j_i >= BLOCK))


def attn_fwd(qs, ks, vb, z, sinks, aw, d, ga_off):
    s = qs.shape[0]
    nb = s // BLOCK
    nq = aw // HEAD_DIM
    grp_sz = nq // N_KV_HEADS
    n_ga = aw // COLT

    def body(q_ref, ko_ref, kp_ref, vo_ref, vp_ref, *rest):
        ga_refs = rest[:n_ga]
        sink_ref, attn_ref, mix_ref = rest[n_ga:]
        n = pl.program_id(0)
        lane_idx = lax.broadcasted_iota(jnp.int32, (2 * BLOCK, LANES), 1)
        kpl = _placed(jnp.concatenate([kp_ref[...], ko_ref[...]], axis=0), lane_idx)
        vpl = _placed(jnp.concatenate([vp_ref[...], vo_ref[...]], axis=0), lane_idx)
        valid = _valid_mask(n)
        for cgrp in range(nq // 2):
            qgrp = q_ref[:, cgrp * LANES:(cgrp + 1) * LANES]
            acc = jnp.zeros((BLOCK, LANES), F32)
            for half in (0, 1):
                h = 2 * cgrp + half
                kh = h // grp_sz
                p, _ = _scores(qgrp, kpl[kh, half], valid, sink_ref[0, h])
                acc += jnp.dot(p.astype(BF16), vpl[kh, half], preferred_element_type=F32)
            attn_ref[:, cgrp * LANES:(cgrp + 1) * LANES] = acc
            col = cgrp * LANES
            ga = ga_refs[col // COLT][:, col % COLT:col % COLT + LANES]
            mix_ref[:, cgrp * LANES:(cgrp + 1) * LANES] = (acc * (ga * _sigmoid(ga))).astype(BF16)

    own = lambda n: (n, 0)
    prev = lambda n: (jnp.maximum(n - 1, 0), 0)
    kvs = lambda imap: pl.BlockSpec((BLOCK, KV_WIDTH), imap)
    ga_specs = [pl.BlockSpec((BLOCK, COLT), functools.partial(lambda n, j: (n, ga_off + j), j=j))
                for j in range(n_ga)]
    return _call(body, name="attn_fwd", grid=(nb,),
                 in_specs=[pl.BlockSpec((BLOCK, aw), own), kvs(own), kvs(prev), kvs(own), kvs(prev)]
                 + ga_specs + [pl.BlockSpec(memory_space=pltpu.SMEM)],
                 out_specs=[pl.BlockSpec((BLOCK, aw), own), pl.BlockSpec((BLOCK, aw), own)],
                 out_shape=[jax.ShapeDtypeStruct((s, aw), F32), jax.ShapeDtypeStruct((s, d), BF16)],
                 compiler_params=_params(("parallel",)))(qs, ks, ks, vb, vb, *([z] * n_ga), sinks)


def gate_bwd(d_mix, attn, z, aw, ga_off, after=()):
    s, in_w = z.shape
    tm = _tile(s, 256)

    def body(dm_ref, at_ref, ga_ref, do_ref, dz_ref):
        ga = ga_ref[...]
        sig = _sigmoid(ga)
        dm = dm_ref[...]
        do_ref[...] = (dm * (ga * sig)).astype(BF16)
        dz_ref[...] = ((dm * at_ref[...]) * (sig * (1.0 + ga * (1.0 - sig)))).astype(BF16)

    blk = pl.BlockSpec((tm, COLT), lambda i, j: (i, j))
    gab = pl.BlockSpec((tm, COLT), lambda i, j: (i, ga_off + j))
    return _call_after(body, after, name="gate_bwd", grid=(s // tm, aw // COLT),
                       in_specs=[blk, blk, gab], out_specs=[blk, gab],
                       out_shape=[jax.ShapeDtypeStruct((s, aw), BF16), jax.ShapeDtypeStruct((s, in_w), BF16)],
                       compiler_params=_params(("parallel", "parallel")))(d_mix, attn, z)


def attn_bwd(qs, ks, vb, d_o, sinks, aw):
    s = qs.shape[0]
    nb = s // BLOCK
    nq = aw // HEAD_DIM
    grp_sz = nq // N_KV_HEADS

    def body(q_ref, ko_ref, kp_ref, vo_ref, vp_ref, do_ref, sink_ref, dq_ref, dk_ref, dv_ref, ds_ref,
             ck_ref, cv_ref):
        n = pl.program_id(0)

        @pl.when(n == 0)
        def _():
            ds_ref[...] = jnp.zeros_like(ds_ref)
            dk_ref[...] = jnp.zeros_like(dk_ref)
            dv_ref[...] = jnp.zeros_like(dv_ref)

        @pl.when(n < nb)
        def _():
            lane_idx = lax.broadcasted_iota(jnp.int32, (2 * BLOCK, LANES), 1)
            lane_row = lax.broadcasted_iota(jnp.int32, (1, LANES), 1)
            kpl = _placed(jnp.concatenate([kp_ref[...], ko_ref[...]], axis=0), lane_idx)
            vpl = _placed(jnp.concatenate([vp_ref[...], vo_ref[...]], axis=0), lane_idx)
            valid = _valid_mask(n)
            dk_acc = [jnp.zeros((2 * BLOCK, LANES), F32) for _ in range(N_KV_HEADS)]
            dv_acc = [jnp.zeros((2 * BLOCK, LANES), F32) for _ in range(N_KV_HEADS)]
            ds_row = jnp.zeros((1, LANES), F32)
            for cgrp in range(nq // 2):
                qgrp = q_ref[:, cgrp * LANES:(cgrp + 1) * LANES]
                dog = do_ref[:, cgrp * LANES:(cgrp + 1) * LANES]
                dq_acc = jnp.zeros((BLOCK, LANES), F32)
                for half in (0, 1):
                    h = 2 * cgrp + half
                    kh = h // grp_sz
                    p, p_sink = _scores(qgrp, kpl[kh, half], valid, sink_ref[0, h])
                    dp = lax.dot_general(dog, vpl[kh, half], NT, preferred_element_type=F32)
                    delta = jnp.sum(p * dp, axis=-1, keepdims=True)
                    dsb = (p * (dp - delta)).astype(BF16)
                    dq_acc += jnp.dot(dsb, kpl[kh, half], preferred_element_type=F32)
                    keep = (lane_idx >= half * HEAD_DIM) & (lane_idx < (half + 1) * HEAD_DIM)
                    dkp = jnp.where(keep, lax.dot_general(dsb, qgrp, TN, preferred_element_type=F32), 0.0)
                    dvp = jnp.where(keep, lax.dot_general(p.astype(BF16), dog, TN, preferred_element_type=F32), 0.0)
                    if half != kh % 2:
                        dkp = pltpu.roll(dkp, HEAD_DIM, 1)
                        dvp = pltpu.roll(dvp, HEAD_DIM, 1)
                    dk_acc[kh] += dkp
                    dv_acc[kh] += dvp
                    dsink = -jnp.sum(p_sink * delta, axis=0, keepdims=True)
                    ds_row += jnp.where(lane_row == h, dsink, 0.0)
                dq_ref[:, cgrp * LANES:(cgrp + 1) * LANES] = dq_acc
            ds_ref[0:1, :] += ds_row
            dk_band = jnp.concatenate([dk_acc[0] + dk_acc[1], dk_acc[2] + dk_acc[3]], axis=1)
            dv_band = jnp.concatenate([dv_acc[0] + dv_acc[1], dv_acc[2] + dv_acc[3]], axis=1)

            @pl.when(n > 0)
            def _():
                dk_ref[...] = ck_ref[...] + dk_band[:BLOCK]
                dv_ref[...] = cv_ref[...] + dv_band[:BLOCK]

            ck_ref[...] = dk_band[BLOCK:]
            cv_ref[...] = dv_band[BLOCK:]

        @pl.when(n == nb)
        def _():
            dk_ref[...] = ck_ref[...]
            dv_ref[...] = cv_ref[...]

    own = lambda n: (jnp.minimum(n, nb - 1), 0)
    prev = lambda n: (jnp.clip(n - 1, 0, nb - 1), 0)
    done = lambda n: (jnp.maximum(n - 1, 0), 0)
    kvs = lambda imap: pl.BlockSpec((BLOCK, KV_WIDTH), imap)
    return _call(body, name="attn_bwd", grid=(nb + 1,),
                 in_specs=[pl.BlockSpec((BLOCK, aw), own), kvs(own), kvs(prev), kvs(own), kvs(prev),
                           pl.BlockSpec((BLOCK, aw), own), pl.BlockSpec(memory_space=pltpu.SMEM)],
                 out_specs=[pl.BlockSpec((BLOCK, aw), own), kvs(done), kvs(done),
                            pl.BlockSpec((SUBLANES, LANES), lambda n: (0, 0))],
                 out_shape=[jax.ShapeDtypeStruct((s, aw), F32), jax.ShapeDtypeStruct((s, KV_WIDTH), F32),
                            jax.ShapeDtypeStruct((s, KV_WIDTH), F32), jax.ShapeDtypeStruct((SUBLANES, LANES), F32)],
                 scratch_shapes=[pltpu.VMEM((BLOCK, KV_WIDTH), F32), pltpu.VMEM((BLOCK, KV_WIDTH), F32)],
                 compiler_params=_params(("arbitrary",)))(qs, ks, ks, vb, vb, d_o, sinks)


def conv_fwd(z, conv_w, mix, aw, cw, b_off):
    s = z.shape[0]
    tm = _tile(s, 256)
    nseg = cw // COLT
    hb = tm // SUBLANES

    def body(b_ref, c_ref, h_ref, g_ref, cp_ref, hp_ref, w_ref, mix_in, mix_ref):
        i = pl.program_id(0)
        u = c_ref[...] * h_ref[...]
        up = jnp.where(i > 0, cp_ref[...] * hp_ref[...], 0.0)
        ext = jnp.concatenate([up, u], axis=0)
        um1 = pltpu.roll(ext, 1, 0)[SUBLANES:]
        um2 = pltpu.roll(ext, 2, 0)[SUBLANES:]
        w = w_ref[...]
        cv = w[0:1] * um2 + w[1:2] * um1 + w[2:3] * u
        g = g_ref[...]
        mix_ref[...] = ((b_ref[...] * cv) * (g * _sigmoid(g))).astype(BF16)

    seg = lambda k: pl.BlockSpec((tm, COLT), functools.partial(lambda i, j, k: (i, b_off + k * nseg + j), k=k))
    halo = lambda k: pl.BlockSpec(
        (SUBLANES, COLT), functools.partial(lambda i, j, k: (jnp.maximum(i * hb - 1, 0), b_off + k * nseg + j), k=k))
    return _call(body, name="conv_fwd", grid=(s // tm, nseg),
                 in_specs=[seg(0), seg(1), seg(2), seg(3), halo(1), halo(2),
                           pl.BlockSpec((SUBLANES, COLT), lambda i, j: (0, j)), _hbm()],
                 out_specs=pl.BlockSpec((tm, COLT), lambda i, j: (i, aw // COLT + j)),
                 out_shape=jax.ShapeDtypeStruct(mix.shape, BF16),
                 input_output_aliases={7: 0},
                 compiler_params=_params(("parallel", "parallel")))(z, z, z, z, z, z, conv_w, mix)


def conv_bwd(z, d_mix, conv_w, dz, aw, cw, b_off):
    s = z.shape[0]
    tm = _tile(s, 256)
    nseg = cw // COLT
    hb = tm // SUBLANES
    n_row = s // tm
    last_h = s // SUBLANES - 1

    def body(b_ref, c_ref, h_ref, g_ref, cp_ref, hp_ref, bn_ref, gn_ref, dm_ref, dmn_ref, w_ref, dz_in,
             dz_ref, acc_ref, stash_ref):
        i = pl.program_id(1)
        k = pl.program_id(2)

        @pl.when(k == 0)
        def _():
            cc, hh, bb, g = c_ref[...], h_ref[...], b_ref[...], g_ref[...]
            w = w_ref[...]
            u = cc * hh
            up = jnp.where(i > 0, cp_ref[...] * hp_ref[...], 0.0)
            ext = jnp.concatenate([up, u], axis=0)
            um1 = pltpu.roll(ext, 1, 0)[SUBLANES:]
            um2 = pltpu.roll(ext, 2, 0)[SUBLANES:]
            cv = w[0:1] * um2 + w[1:2] * um1 + w[2:3] * u
            sig = _sigmoid(g)
            sg = g * sig
            dm = dm_ref[...]
            d_cv = (dm * bb) * sg
            gn = gn_ref[...]
            d_cv_next = jnp.where(i < n_row - 1, (dmn_ref[...] * bn_ref[...]) * (gn * _sigmoid(gn)), 0.0)
            ext2 = jnp.concatenate([d_cv, d_cv_next], axis=0)
            dp1 = pltpu.roll(ext2, tm + SUBLANES - 1, 0)[:tm]
            dp2 = pltpu.roll(ext2, tm + SUBLANES - 2, 0)[:tm]
            d_u = w[2:3] * d_cv + w[1:2] * dp1 + w[0:1] * dp2
            stash_ref[0] = ((dm * cv) * sg).astype(BF16)
            stash_ref[1] = (d_u * hh).astype(BF16)
            stash_ref[2] = (d_u * cc).astype(BF16)
            stash_ref[3] = (((dm * bb) * cv) * (sig * (1.0 + g * (1.0 - sig)))).astype(BF16)

            @pl.when(i == 0)
            def _():
                acc_ref[...] = jnp.zeros_like(acc_ref)

            acc_ref[0:1, :] += jnp.sum(d_cv * um2, axis=0, keepdims=True)
            acc_ref[1:2, :] += jnp.sum(d_cv * um1, axis=0, keepdims=True)
            acc_ref[2:3, :] += jnp.sum(d_cv * u, axis=0, keepdims=True)

        dz_ref[...] = stash_ref[k]

    seg = lambda q: pl.BlockSpec((tm, COLT), functools.partial(lambda j, i, k, q: (i, b_off + q * nseg + j), q=q))
    halo_p = lambda q: pl.BlockSpec(
        (SUBLANES, COLT),
        functools.partial(lambda j, i, k, q: (jnp.maximum(i * hb - 1, 0), b_off + q * nseg + j), q=q))
    halo_n = lambda q: pl.BlockSpec(
        (SUBLANES, COLT),
        functools.partial(lambda j, i, k, q: (jnp.minimum((i + 1) * hb, last_h), b_off + q * nseg + j), q=q))
    return _call(body, name="conv_bwd", grid=(nseg, n_row, 4),
                 in_specs=[seg(0), seg(1), seg(2), seg(3), halo_p(1), halo_p(2), halo_n(0), halo_n(3),
                           pl.BlockSpec((tm, COLT), lambda j, i, k: (i, aw // COLT + j)),
                           pl.BlockSpec((SUBLANES, COLT),
                                        lambda j, i, k: (jnp.minimum((i + 1) * hb, last_h), aw // COLT + j)),
                           pl.BlockSpec((SUBLANES, COLT), lambda j, i, k: (0, j)), _hbm()],
                 out_specs=[pl.BlockSpec((tm, COLT), lambda j, i, k: (i, b_off + k * nseg + j)),
                            pl.BlockSpec((SUBLANES, COLT), lambda j, i, k: (0, j))],
                 out_shape=[jax.ShapeDtypeStruct(dz.shape, BF16), jax.ShapeDtypeStruct((SUBLANES, cw), F32)],
                 input_output_aliases={11: 0},
                 scratch_shapes=[pltpu.VMEM((4, tm, COLT), BF16)],
                 compiler_params=_params(("arbitrary", "arbitrary", "arbitrary")))(
                     z, z, z, z, z, z, z, z, d_mix, d_mix, conv_w, dz)


def _place():
    x, y, c = lax.axis_index("x"), lax.axis_index("y"), lax.axis_index("c")
    chips = [(1 - x, y), (x, 1 - y), (1 - x, 1 - y)]
    return x, y, c, chips


def _remote(src, dst, send_sem, recv_sem, device):
    return pltpu.make_async_remote_copy(src_ref=src, dst_ref=dst, send_sem=send_sem, recv_sem=recv_sem,
                                        device_id=device, device_id_type=MESH)


def plan_gather_ici(n_split):
    def plan(refs, send, recv):
        x, y, c, chips = _place()
        me = 2 * x + y
        mine, theirs = [], []
        for w, ref in enumerate(refs):
            for j, (cx, cy) in enumerate(chips):
                def part(slot):
                    if w >= n_split:
                        return ref.at[slot]
                    hr = ref.shape[1] // 2
                    return ref.at[slot, pl.ds(c * hr, hr)]
                k = 3 * w + j
                mine.append(_remote(part(me), part(me), send.at[k], recv.at[k], (cx, cy, c)))
                theirs.append(_remote(part(2 * cx + cy), part(2 * cx + cy), send.at[k], recv.at[k], (cx, cy, c)))
        return mine, theirs
    return plan


def plan_shard_ici(j):
    def plan(refs, send, recv):
        _, _, c, chips = _place()
        own, land = refs
        hr = own.shape[0] // 2
        rows = pl.ds(c * hr, hr)
        cp = _remote(own.at[rows], land.at[rows], send.at[0], recv.at[0], (*chips[j], c))
        return [cp], [cp]
    return plan


def plan_shard_pass(refs, send, recv):
    x, y, c, _ = _place()
    land, = refs
    hr = land.shape[0] // 2
    half = lambda core: land.at[pl.ds(core * hr, hr)]
    return ([_remote(half(c), half(c), send.at[0], recv.at[0], (x, y, 1 - c))],
            [_remote(half(1 - c), half(1 - c), send.at[0], recv.at[0], (x, y, 1 - c))])


def plan_gather_pass(refs, send, recv):
    x, y, c, chips = _place()
    mine, theirs = [], []
    for w, ref in enumerate(refs):
        hr = ref.shape[1] // 2
        for j, (cx, cy) in enumerate(chips):
            half = lambda core: ref.at[2 * cx + cy, pl.ds(core * hr, hr)]
            k = 3 * w + j
            mine.append(_remote(half(c), half(c), send.at[k], recv.at[k], (x, y, 1 - c)))
            theirs.append(_remote(half(1 - c), half(1 - c), send.at[k], recv.at[k], (x, y, 1 - c)))
    return mine, theirs


def plan_swap(refs, send, recv):
    x, y, c, _ = _place()
    nw = len(refs) // 2
    mine = []
    for w in range(nw):
        hr = refs[w].shape[1] // 2
        mine.append(_remote(refs[w].at[:, pl.ds((1 - c) * hr, hr), :], refs[nw + w], send.at[w], recv.at[w],
                            (x, y, 1 - c)))
    return mine, mine


def plan_scatter(refs, send, recv):
    x, y, c, chips = _place()
    me = 2 * x + y
    nw = len(refs) // 2
    mine, theirs = [], []
    for w in range(nw):
        for j, (cx, cy) in enumerate(chips):
            k = 3 * w + j
            mine.append(_remote(refs[w].at[2 * cx + cy], refs[nw + w].at[me], send.at[k], recv.at[k], (cx, cy, c)))
            theirs.append(_remote(refs[w].at[me], refs[nw + w].at[2 * cx + cy], send.at[k], recv.at[k], (cx, cy, c)))
    return mine, theirs


def plan_join(refs, send, recv):
    x, y, c, _ = _place()
    mine, theirs = [], []
    for w, ref in enumerate(refs):
        hr = ref.shape[0] // 2
        half = lambda core: ref.at[pl.ds(core * hr, hr)]
        mine.append(_remote(half(c), half(c), send.at[w], recv.at[w], (x, y, 1 - c)))
        theirs.append(_remote(half(1 - c), half(1 - c), send.at[w], recv.at[w], (x, y, 1 - c)))
    return mine, theirs


def exchange(name, plan, arrays, n):
    na = len(arrays)

    def body(*refs):
        mine, theirs = plan(refs[:na], refs[2 * na], refs[2 * na + 1])
        for cp in mine:
            cp.start()
        for cp in theirs:
            cp.wait_recv()
        for cp in mine:
            cp.wait_send()

    return _call(body, name=name, in_specs=[_hbm()] * na, out_specs=[_hbm()] * na,
                 out_shape=[jax.ShapeDtypeStruct(a.shape, a.dtype) for a in arrays],
                 input_output_aliases={i: i for i in range(na)},
                 scratch_shapes=[pltpu.SemaphoreType.DMA((n,)), pltpu.SemaphoreType.DMA((n,))])(*arrays)


def exchange_start(name, groups, arrays, after=()):
    na, ng = len(arrays), len(groups)

    def body(*refs):
        for g, (plan, idx, _) in enumerate(groups):
            mine, _ = plan([refs[i] for i in idx], refs[na + 2 * g], refs[na + 2 * g + 1])
            for cp in mine:
                cp.start()
        refs[-1][...] = jnp.zeros_like(refs[-1])

    hbm = pl.BlockSpec(memory_space=pltpu.HBM)
    sem = pl.BlockSpec(memory_space=pltpu.SEMAPHORE)
    sem_types = [pltpu.SemaphoreType.DMA((n,)) for _, _, n in groups for _ in (0, 1)]
    outs = _call_after(body, after, name=name, in_specs=[hbm] * na,
                       out_specs=[sem] * (2 * ng) + [hbm] * na + [pl.BlockSpec(memory_space=pltpu.VMEM)],
                       out_shape=sem_types + [pltpu.HBM(a.shape, a.dtype) for a in arrays]
                       + [jax.ShapeDtypeStruct((SUBLANES, LANES), F32)],
                       input_output_aliases={i: i + 2 * ng for i in range(na)},
                       compiler_params=pltpu.CompilerParams(
                           has_side_effects=pltpu.SideEffectType.DATAFLOW_SIDE_EFFECTING))(
                               *[pltpu.with_memory_space_constraint(a, pltpu.HBM) for a in arrays])
    sems = [(outs[2 * g], outs[2 * g + 1]) for g in range(ng)]
    return sems, list(outs[2 * ng:-1]), outs[-1]


def exchange_wait(name, plan, sems, arrays, after):
    send_sems, recv_sems = sems
    na = len(arrays)

    def body(*refs):
        mine, theirs = plan(refs[:na], refs[na], refs[na + 1])
        for cp in mine:
            cp.wait_send()
        for cp in theirs:
            cp.wait_recv()

    hbm = pl.BlockSpec(memory_space=pltpu.HBM)
    sem = pl.BlockSpec(memory_space=pltpu.SEMAPHORE)
    return _call(body, name=name, in_specs=[hbm] * na + [sem, sem, _hbm()], out_specs=[hbm] * na,
                 out_shape=[pltpu.HBM(a.shape, a.dtype) for a in arrays],
                 input_output_aliases={i: i for i in range(na)},
                 compiler_params=pltpu.CompilerParams(
                     has_side_effects=pltpu.SideEffectType.DATAFLOW_SIDE_EFFECTING))(
                         *arrays, send_sems, recv_sems, after)


def allgather_small(small):
    rows, width = small.shape

    def body(in_ref, out_ref, send, recv, loc):
        x, y, c, _ = _place()
        me = 4 * x + 2 * y + c
        local = pltpu.make_async_copy(in_ref, out_ref.at[me], loc)
        local.start()
        flips = [(fx, fy, fc) for fx in (0, 1) for fy in (0, 1) for fc in (0, 1)][1:]

        def cp(k, slot):
            fx, fy, fc = flips[k]
            return pltpu.make_async_remote_copy(
                src_ref=in_ref, dst_ref=out_ref.at[slot], send_sem=send.at[k], recv_sem=recv.at[k],
                device_id=(x ^ fx, y ^ fy, c ^ fc), device_id_type=MESH)

        for k in range(7):
            cp(k, me).start()
        for k in range(7):
            fx, fy, fc = flips[k]
            cp(k, 4 * (x ^ fx) + 2 * (y ^ fy) + (c ^ fc)).wait_recv()
        for k in range(7):
            cp(k, me).wait_send()
        local.wait()

    return _call(body, name="allgather_small",
                 in_specs=[pl.BlockSpec(memory_space=pltpu.VMEM)],
                 out_specs=pl.BlockSpec(memory_space=pltpu.VMEM),
                 out_shape=jax.ShapeDtypeStruct((8, rows, width), F32),
                 scratch_shapes=[pltpu.SemaphoreType.DMA((7,)), pltpu.SemaphoreType.DMA((7,)),
                                 pltpu.SemaphoreType.DMA])(small)


def add_sibling(grad, got, core, name):
    _, r, c = grad.shape
    hr = r // 2
    tr = _tile(hr, 128)
    nblk = hr // tr

    def body(core_ref, g_ref, o_ref, out_ref):
        out_ref[...] = (g_ref[...].astype(F32) + o_ref[...].astype(F32)).astype(BF16)

    grid_spec = pltpu.PrefetchScalarGridSpec(
        num_scalar_prefetch=1, grid=(N_CHIPS, nblk),
        in_specs=[pl.BlockSpec((None, tr, c), lambda t, i, core_ref: (t, core_ref[0] * nblk + i, 0)),
                  pl.BlockSpec((None, tr, c), lambda t, i, core_ref: (t, i, 0))],
        out_specs=pl.BlockSpec((None, tr, c), lambda t, i, core_ref: (t, i, 0)))
    return _call(body, name=name, grid_spec=grid_spec,
                 out_shape=jax.ShapeDtypeStruct((N_CHIPS, hr, c), BF16),
                 compiler_params=_params(("parallel", "parallel")))(core, grad, got)


def sum_chips(mine, owned, place, name):
    _, hr, c = mine.shape
    tr = _tile(hr, 128)
    nblk = hr // tr

    def body(place_ref, m_ref, o1_ref, o2_ref, o3_ref, out_ref):
        acc = m_ref[...].astype(F32)
        for o_ref in (o1_ref, o2_ref, o3_ref):
            acc = acc + o_ref[...].astype(F32)
        out_ref[...] = acc

    other = lambda k: pl.BlockSpec((None, tr, c), lambda i, place_ref: ((place_ref[0] + k) % N_CHIPS, i, 0))
    grid_spec = pltpu.PrefetchScalarGridSpec(
        num_scalar_prefetch=1, grid=(nblk,),
        in_specs=[other(0), other(1), other(2), other(3)],
        out_specs=pl.BlockSpec((tr, c), lambda i, place_ref: (place_ref[1] * nblk + i, 0)))
    return _call(body, name=name, grid_spec=grid_spec,
                 out_shape=jax.ShapeDtypeStruct((2 * hr, c), F32),
                 compiler_params=_params(("parallel",)))(place, mine, owned, owned, owned)


def sum_devices(gathered):
    _, rows, width = gathered.shape

    def body(g_ref, out_ref):
        acc = g_ref[0]
        for dev in range(1, 8):
            acc = acc + g_ref[dev]
        out_ref[...] = acc
        tail = acc[SUBLANES:]
        out_ref[SUBLANES:, :] = jnp.broadcast_to(jnp.sum(tail, axis=1, keepdims=True), tail.shape)

    return _call(body, name="sum_devices",
                 in_specs=[pl.BlockSpec(memory_space=pltpu.VMEM)],
                 out_specs=pl.BlockSpec(memory_space=pltpu.VMEM),
                 out_shape=jax.ShapeDtypeStruct((rows, width), F32))(gathered)


def _rope_tables(s):
    half = ROT_DIM // 2
    inv_freq = jnp.power(jnp.float32(ROPE_THETA), -jnp.arange(half, dtype=F32) * 2.0 / ROT_DIM)
    ang = jnp.arange(s).astype(F32)[:, None] * inv_freq[None, :]
    cos, sin = jnp.cos(ang), jnp.sin(ang)
    zeros = lambda n: jnp.zeros((s, n), F32)
    cos64 = jnp.concatenate([cos, cos, jnp.ones((s, HEAD_DIM - ROT_DIM), F32)], axis=1)
    sa64 = jnp.concatenate([-sin, zeros(HEAD_DIM - half)], axis=1)
    sb64 = jnp.concatenate([zeros(half), sin, zeros(HEAD_DIM - ROT_DIM)], axis=1)
    return tuple(jnp.concatenate([t, t], axis=1) for t in (cos64, sa64, sb64))


def _pad_rows(a, rows):
    return jnp.pad(a, ((0, rows - a.shape[0]), (0, 0)))


def _pad_cols(a, cols):
    return jnp.pad(a, ((0, 0), (0, cols - a.shape[1])))


def kernel(x, p, norm_gain, w_in, q_norm_gain, k_norm_gain, attn_sinks, conv_w, w_out, ple_gate_norm_gain, w_ple_gate, b_ple_gate, w_ple_proj, ple_norm_gain, loss_target, m_norm_gain, m_w_in, m_q_norm_gain, m_k_norm_gain, m_attn_sinks, m_conv_w, m_w_out, m_ple_gate_norm_gain, m_w_ple_gate, m_b_ple_gate, m_w_ple_proj, m_ple_norm_gain, v_norm_gain, v_w_in, v_q_norm_gain, v_k_norm_gain, v_attn_sinks, v_conv_w, v_w_out, v_ple_gate_norm_gain, v_w_ple_gate, v_b_ple_gate, v_w_ple_proj, v_ple_norm_gain):
    x2, p2, tgt = x[0], p[0, 0], loss_target[0]
    s, d = x2.shape
    ple = p2.shape[1]
    aw = d // 2
    cw = d - aw
    nq = aw // HEAD_DIM
    sh = w_in.shape[2]
    in_w = N_CHIPS * sh
    dq = d // N_CHIPS
    cq = cw // N_CHIPS
    ga_off = (aw + 2 * KV_WIDTH) // COLT
    b_off = (2 * aw + 2 * KV_WIDTH) // COLT
    assert in_w == 2 * aw + 2 * KV_WIDTH + 4 * cw and aw % COLT == 0 and cw % COLT == 0
    assert s % BLOCK == 0 and sh % LANES == 0 and nq % (2 * N_KV_HEADS) == 0

    core = lax.axis_index("c").astype(jnp.int32).reshape(1)
    chip = 2 * lax.axis_index("x") + lax.axis_index("y")

    chip1 = chip.astype(jnp.int32).reshape(1)
    place = jnp.concatenate([chip1, core])
    w_in_own = cast_bf16(w_in[0], "cast_w_in")
    rest = [cast_into_slot(w_out[0], chip1, "cast_w_out"), cast_into_slot(w_ple_gate[0], chip1, "cast_w_pg"),
            cast_into_slot(w_ple_proj[0], chip1, "cast_w_pp"),
            lax.dynamic_update_slice(jnp.zeros((N_CHIPS, SUBLANES, cq), F32),
                                     _pad_rows(conv_w[0], SUBLANES)[None], (chip, 0, 0))]
    order = jnp.stack([chip, chip ^ 2, chip ^ 1, chip ^ 3]).astype(jnp.int32)
    wi_sems, wi_arrs, wi_token = exchange_start(
        "gather_wi_start", [(plan_shard_ici(j), [0, 1 + j], 1) for j in range(3)],
        [w_in_own] + [lax.empty((d, sh), BF16) for _ in range(3)])
    rest_sems, rest, rest_token = exchange_start(
        "gather_rest_start", [(plan_gather_ici(3), [0, 1, 2, 3], 12)], rest, after=(wi_token,))

    tm = _tile(s, 1024)
    tn = _tile(d, 1024)
    tk = _tile(d, 2048)
    ts = _tile(s, 2048)
    h = rms_fwd(x2, norm_gain, "rms_fwd_x", after=(rest_token,))
    tabs = _rope_tables(s)
    qg = jnp.tile(q_norm_gain, (1, LANES // HEAD_DIM))
    kg = jnp.tile(k_norm_gain, (1, LANES // HEAD_DIM))
    seg_i = jnp.arange(SEG) // HEAD_DIM
    segb = (seg_i[:, None] == seg_i[None, :]).astype(BF16)
    z = in_proj_part(h, wi_arrs[0], None, order, 0, in_w)
    w_shards = [wi_arrs[0]]
    for j in range(3):
        w_shards[0], land = exchange_wait("gather_wi_wait%d" % j, plan_shard_ici(j), wi_sems[j],
                                          [w_shards[0], wi_arrs[1 + j]], z)
        land, = exchange("gather_wi_pass%d" % j, plan_shard_pass, [land], 1)
        z = in_proj_part(h, land, z, order, 1 + j, in_w)
        w_shards.append(land)
    wo_all, wg_all, wp_all, conv_all = exchange_wait("gather_rest_wait", plan_gather_ici(3), rest_sems[0], rest, z)
    wo_all, wg_all, wp_all = exchange("gather_rest_pass", plan_gather_pass, [wo_all, wg_all, wp_all], 9)
    wo_full = wo_all.reshape(d, d)
    wg_full = wg_all.reshape(d, d)
    conv_full = conv_all.transpose(1, 0, 2).reshape(SUBLANES, cw)
    qs, ks, vb = qk_prep_fwd(z, tabs, qg, kg, segb, aw)
    attn, mix = attn_fwd(qs, ks, vb, z, attn_sinks, aw, d, ga_off)
    mix = conv_fwd(z, conv_full, mix, aw, cw, b_off)
    sq = lambda shape: dict(
        a_spec=pl.BlockSpec((tm, tk), lambda i, j, k: (i, k)),
        o_spec=pl.BlockSpec((tm, tn), lambda i, j, k: (i, j)),
        out_shape=jax.ShapeDtypeStruct(shape, F32))
    x1 = matmul(mix, wo_full, grid=(s // tm, d // tn, d // tk),
                b_spec=pl.BlockSpec((tk, tn), lambda i, j, k: (k, j)), dims=NN, name="mm_x1",
                res=x2, res_spec=pl.BlockSpec((tm, tn), lambda i, j, k: (i, j)), **sq((s, d)))
    hg = rms_fwd(x1, ple_gate_norm_gain, "rms_fwd_x1")
    gl = matmul(hg, wg_full, grid=(s // tm, d // tn, d // tk),
                b_spec=pl.BlockSpec((tk, tn), lambda i, j, k: (k, j)), dims=NN, name="mm_gl", **sq((s, d)))
    pq = d // N_CHIPS
    pe = matmul(p2, wp_all, grid=(s // tm, N_CHIPS, 1),
                a_spec=pl.BlockSpec((tm, ple), lambda i, j, k: (i, 0)),
                b_spec=pl.BlockSpec((None, ple, pq), lambda i, j, k: (j, 0, 0)),
                o_spec=pl.BlockSpec((tm, pq), lambda i, j, k: (i, j)),
                out_shape=jax.ShapeDtypeStruct((s, d), F32), dims=NN, name="mm_pe")

    d_gl, d_pe, dy, acc_head = head_fwd_bwd(x1, gl, pe, tgt, b_ple_gate, ple_norm_gain)
    d_hg = matmul(d_gl, wg_full, grid=(s // tm, d // tn, d // tk),
                  b_spec=pl.BlockSpec((tn, tk), lambda i, j, k: (j, k)), dims=NT, name="mm_d_hg", **sq((s, d)))
    wgrad = lambda a_cols, shape3, o_spec, b_cols, name, a, b, grid: matmul(
        a, b, grid=grid,
        a_spec=pl.BlockSpec((ts, a_cols), lambda i, j, k: (k, i)),
        b_spec=pl.BlockSpec((ts, b_cols), lambda i, j, k: (k, j)),
        o_spec=o_spec, out_shape=jax.ShapeDtypeStruct(shape3, BF16), dims=TN, name=name)
    g_wg = wgrad(tn, (d, d), pl.BlockSpec((tn, tn), lambda i, j, k: (i, j)), tn, "mm_g_wg", hg, d_gl,
                 (d // tn, d // tn, s // ts))
    g_wp = wgrad(ple, (N_CHIPS, ple, pq), pl.BlockSpec((None, ple, pq), lambda i, j, k: (j, 0, 0)), pq,
                 "mm_g_wp", p2, d_pe, (1, N_CHIPS, s // ts))
    d_x1, d_x1b, acc_g = rms_bwd_add(d_hg, x1, dy, ple_gate_norm_gain, "rms_bwd_x1", True)
    d_mix = matmul(d_x1b, wo_full, grid=(s // tm, d // tn, d // tk),
                   b_spec=pl.BlockSpec((tn, tk), lambda i, j, k: (j, k)), dims=NT, name="mm_d_mix", **sq((s, d)))
    g_wo = wgrad(tn, (d, d), pl.BlockSpec((tn, tn), lambda i, j, k: (i, j)), tn, "mm_g_wo", mix, d_x1b,
                 (d // tn, d // tn, s // ts))
    def reduce_start(tag, grads):
        n = len(grads)
        lands = [lax.empty((N_CHIPS, a.shape[1] // 2, a.shape[2]), BF16) for a in grads]
        swapped = exchange("swap_" + tag, plan_swap, list(grads) + lands, n)
        parts = [add_sibling(g, o, core, "add_sibling_%s%d" % (tag, i))
                 for i, (g, o) in enumerate(zip(swapped[:n], swapped[n:]))]
        return exchange_start("scatter_%s_start" % tag, [(plan_scatter, list(range(2 * n)), 3 * n)],
                              parts + [lax.empty(a.shape, BF16) for a in parts])

    def reduce_finish(tag, handle, after):
        sems, arrays, _ = handle
        n = len(arrays) // 2
        got = exchange_wait("scatter_%s_wait" % tag, plan_scatter, sems[0], arrays, after)
        halves = [sum_chips(pt, o, place, "sum_chips_%s%d" % (tag, i))
                  for i, (pt, o) in enumerate(zip(got[:n], got[n:]))]
        return exchange("join_" + tag, plan_join, halves, n)

    early = reduce_start("early", [g_wo.reshape(N_CHIPS, dq, d), g_wg.reshape(N_CHIPS, dq, d), g_wp])
    d_o, dz = gate_bwd(d_mix, attn, z, aw, ga_off, after=(early[-1],))
    dqs, dks, dvs, acc_sink = attn_bwd(qs, ks, vb, d_o, attn_sinks, aw)
    dz, acc_qk = qk_prep_bwd(z, dqs, dks, dvs, tabs, qg, kg, segb, dz, aw)
    dz, acc_conv = conv_bwd(z, d_mix, conv_full, dz, aw, cw, b_off)
    full_wo, full_wg, full_wp = reduce_finish("early", early, dz)
    g_wi = wgrad(tn, (N_CHIPS, d, sh), pl.BlockSpec((None, tn, sh), lambda i, j, k: (j, i, 0)), sh,
                 "mm_g_wi", h, dz, (d // tn, N_CHIPS, s // ts))
    late = reduce_start("late", [g_wi])
    d_h = in_proj_bwd(dz, w_shards, order, d, after=(late[-1],))
    grad_x, acc_x = rms_bwd_add(d_h, x2, d_x1, norm_gain, "rms_bwd_x", False)
    full_wi, = reduce_finish("late", late, grad_x)
    big = {}
    for nm, g, w, m, v in (("w_in", full_wi, w_in, m_w_in, v_w_in), ("w_out", full_wo, w_out, m_w_out, v_w_out),
                           ("w_ple_gate", full_wg, w_ple_gate, m_w_ple_gate, v_w_ple_gate),
                           ("w_ple_proj", full_wp, w_ple_proj, m_w_ple_proj, v_w_ple_proj)):
        big[nm] = [o[None] for o in adamw(g, w[0], m[0], v[0], "adamw_" + nm)]

    wsm = max(d, cw)
    misc = jnp.concatenate([acc_qk[0:1, :HEAD_DIM], acc_qk[1:2, :HEAD_DIM], acc_sink[0:1, :nq]], axis=1)
    small = jnp.concatenate([
        _pad_cols(acc_x[0:1], wsm), _pad_cols(acc_g[0:1], wsm), _pad_cols(acc_head[0:1], wsm),
        _pad_cols(acc_head[1:2], wsm), _pad_cols(misc, wsm), _pad_cols(acc_conv[0:3], wsm)], axis=0)
    both = jnp.concatenate([small, _pad_rows(_pad_cols(acc_head[2:3], wsm), SUBLANES)], axis=0)
    tot = sum_devices(allgather_small(both))
    loss = tot[SUBLANES, 0]

    def pack(vals):
        ng, pg, bg, eg, qgv, kgv, sk, cv = vals
        misc_v = jnp.concatenate([qgv, kgv, sk], axis=1)
        return jnp.concatenate([_pad_cols(ng, wsm), _pad_cols(pg, wsm), _pad_cols(bg, wsm), _pad_cols(eg, wsm),
                                _pad_cols(misc_v, wsm), _pad_cols(cv[0], wsm)], axis=0)

    g_small = jnp.concatenate(
        [tot[0:5], _pad_cols(lax.dynamic_slice(tot[5:8], (0, chip * cq), (3, cq)), wsm)], axis=0)
    w_small = pack((norm_gain, ple_gate_norm_gain, b_ple_gate, ple_norm_gain, q_norm_gain, k_norm_gain,
                    attn_sinks, conv_w))
    m_small = pack((m_norm_gain, m_ple_gate_norm_gain, m_b_ple_gate, m_ple_norm_gain, m_q_norm_gain,
                    m_k_norm_gain, m_attn_sinks, m_conv_w))
    v_small = pack((v_norm_gain, v_ple_gate_norm_gain, v_b_ple_gate, v_ple_norm_gain, v_q_norm_gain,
                    v_k_norm_gain, v_attn_sinks, v_conv_w))
    sm = adamw(g_small, w_small, m_small, v_small, "adamw_small")

    def unpack(a):
        return {"norm_gain": a[0:1, :d], "ple_gate_norm_gain": a[1:2, :d], "b_ple_gate": a[2:3, :d],
                "ple_norm_gain": a[3:4, :d], "q_norm_gain": a[4:5, :HEAD_DIM],
                "k_norm_gain": a[4:5, HEAD_DIM:2 * HEAD_DIM],
                "attn_sinks": a[4:5, 2 * HEAD_DIM:2 * HEAD_DIM + nq], "conv_w": a[5:8, :cq][None]}

    order = ("norm_gain", "w_in", "q_norm_gain", "k_norm_gain", "attn_sinks", "conv_w", "w_out",
             "ple_gate_norm_gain", "w_ple_gate", "b_ple_gate", "w_ple_proj", "ple_norm_gain")
    outs = [loss, grad_x[None]]
    for kind in range(4):
        table = unpack(sm[kind])
        for nm in order:
            outs.append(big[nm][kind] if nm in big else table[nm])
    return tuple(outs)
```

```python
import functools

import jax
import jax.numpy as jnp
from jax import lax
from jax.experimental import pallas as pl
from jax.experimental.pallas import tpu as pltpu

F32 = jnp.float32
BF16 = jnp.bfloat16
MESH = pl.DeviceIdType.MESH

HEAD_DIM = 64
N_KV_HEADS = 4
KV_WIDTH = N_KV_HEADS * HEAD_DIM
BLOCK = 128
ROT_DIM = 16
ROPE_THETA = 500000.0
EPS = 1e-6
NEG_INF = -1e30
N_CHIPS = 4
LANES = 128
SUBLANES = 8
COLT = 512
SEG = 256
VMEM_LIMIT = 48 * 1024 * 1024

ADAM_LR = 0.001
ADAM_B1 = 0.9
ADAM_B2 = 0.999
ADAM_EPS = 1e-08
ADAM_WD = 0.01
ADAM_STEP = 10

NN = (((1,), (0,)), ((), ()))
NT = (((1,), (1,)), ((), ()))
TN = (((0,), (0,)), ((), ()))


def _call(body, **kw):
    return pl.pallas_call(body, **kw)


def _call_after(body, after, **kw):
    n_in, n_after = len(kw["in_specs"]), len(after)
    kw["in_specs"] = list(kw["in_specs"]) + [pl.BlockSpec(memory_space=pl.ANY)] * n_after

    def body_after(*refs):
        body(*refs[:n_in], *refs[n_in + n_after:])

    call = _call(body_after, **kw)
    return lambda *args: call(*args, *after)


def _params(sem):
    return pltpu.CompilerParams(dimension_semantics=sem, vmem_limit_bytes=VMEM_LIMIT)


def _tile(n, pref):
    return pref if n % pref == 0 else n


def _sigmoid(v):
    return 1.0 / (1.0 + jnp.exp(-v))


def _hbm():
    return pl.BlockSpec(memory_space=pl.ANY)


def cast_bf16(a, name):
    r, c = a.shape
    tr = _tile(r, 256)

    def body(a_ref, o_ref):
        o_ref[...] = a_ref[...].astype(BF16)

    return _call(body, name=name, grid=(r // tr,),
                 in_specs=[pl.BlockSpec((tr, c), lambda i: (i, 0))],
                 out_specs=pl.BlockSpec((tr, c), lambda i: (i, 0)),
                 out_shape=jax.ShapeDtypeStruct((r, c), BF16),
                 compiler_params=_params(("parallel",)))(a)


def cast_into_slot(a, chip, name):
    r, c = a.shape
    tr = _tile(r, 256)

    def body(chip_ref, a_ref, o_ref):
        o_ref[...] = a_ref[...].astype(BF16)

    grid_spec = pltpu.PrefetchScalarGridSpec(
        num_scalar_prefetch=1, grid=(r // tr,),
        in_specs=[pl.BlockSpec((tr, c), lambda i, chip_ref: (i, 0))],
        out_specs=pl.BlockSpec((None, tr, c), lambda i, chip_ref: (chip_ref[0], i, 0)))
    return _call(body, name=name, grid_spec=grid_spec,
                 out_shape=jax.ShapeDtypeStruct((N_CHIPS, r, c), BF16),
                 compiler_params=_params(("parallel",)))(chip, a)


def rms_fwd(x, gain, name, after=()):
    s, d = x.shape
    tm = _tile(s, 256)

    def body(x_ref, g_ref, o_ref):
        xf = x_ref[...]
        r = lax.rsqrt(jnp.mean(xf * xf, axis=-1, keepdims=True) + EPS)
        o_ref[...] = ((xf * r) * g_ref[...]).astype(BF16)

    return _call_after(body, after, name=name, grid=(s // tm,),
                       in_specs=[pl.BlockSpec((tm, d), lambda i: (i, 0)),
                                 pl.BlockSpec((1, d), lambda i: (0, 0))],
                       out_specs=pl.BlockSpec((tm, d), lambda i: (i, 0)),
                       out_shape=jax.ShapeDtypeStruct((s, d), BF16),
                       compiler_params=_params(("parallel",)))(x, gain)


def rms_bwd_add(dyn, xin, add, gain, name, want_bf16):
    s, d = xin.shape
    tm = _tile(s, 256)

    def body(dy_ref, x_ref, a_ref, g_ref, *outs):
        i = pl.program_id(0)
        dx_ref, acc_ref = outs[0], outs[-1]
        xf = x_ref[...]
        r = lax.rsqrt(jnp.mean(xf * xf, axis=-1, keepdims=True) + EPS)
        xhat = xf * r
        dyv = dy_ref[...]
        gd = dyv * g_ref[...]
        dx = a_ref[...] + r * (gd - xhat * jnp.mean(xhat * gd, axis=-1, keepdims=True))
        dx_ref[...] = dx
        if want_bf16:
            outs[1][...] = dx.astype(BF16)

        @pl.when(i == 0)
        def _():
            acc_ref[...] = jnp.zeros_like(acc_ref)

        acc_ref[0:1, :] += jnp.sum(dyv * xhat, axis=0, keepdims=True)

    row = pl.BlockSpec((tm, d), lambda i: (i, 0))
    out_specs = [row] + ([row] if want_bf16 else []) + [pl.BlockSpec((SUBLANES, d), lambda i: (0, 0))]
    out_shape = ([jax.ShapeDtypeStruct((s, d), F32)]
                 + ([jax.ShapeDtypeStruct((s, d), BF16)] if want_bf16 else [])
                 + [jax.ShapeDtypeStruct((SUBLANES, d), F32)])
    return _call(body, name=name, grid=(s // tm,),
                 in_specs=[row, row, row, pl.BlockSpec((1, d), lambda i: (0, 0))],
                 out_specs=out_specs, out_shape=out_shape,
                 compiler_params=_params(("arbitrary",)))(dyn, xin, add, gain)


def head_fwd_bwd(x1, gl, pe, tgt, bias, ple_gain):
    s, d = x1.shape
    tm = _tile(s, 128)

    def body(x1_ref, gl_ref, pe_ref, t_ref, b_ref, g_ref, dgl_ref, dpe_ref, dy_ref, acc_ref):
        i = pl.program_id(0)
        gate = _sigmoid(gl_ref[...] + b_ref[...])
        pev = pe_ref[...]
        r = lax.rsqrt(jnp.mean(pev * pev, axis=-1, keepdims=True) + EPS)
        pehat = pev * r
        gain = g_ref[...]
        e = pehat * gain
        diff = (x1_ref[...] + gate * e) - t_ref[...]
        dy = diff * (1.0 / d)
        dy_ref[...] = dy
        d_gl = (dy * e) * (gate * (1.0 - gate))
        dgl_ref[...] = d_gl.astype(BF16)
        d_e = dy * gate
        gd = d_e * gain
        d_pe = r * (gd - pehat * jnp.mean(pehat * gd, axis=-1, keepdims=True))
        dpe_ref[...] = d_pe.astype(BF16)

        @pl.when(i == 0)
        def _():
            acc_ref[...] = jnp.zeros_like(acc_ref)

        acc_ref[0:1, :] += jnp.sum(d_gl, axis=0, keepdims=True)
        acc_ref[1:2, :] += jnp.sum(d_e * pehat, axis=0, keepdims=True)
        acc_ref[2:3, :] += jnp.sum(diff * diff, axis=0, keepdims=True) * (0.5 / d)

    row = pl.BlockSpec((tm, d), lambda i: (i, 0))
    vec = pl.BlockSpec((1, d), lambda i: (0, 0))
    return _call(body, name="head_fwd_bwd", grid=(s // tm,),
                 in_specs=[row, row, row, row, vec, vec],
                 out_specs=[row, row, row, pl.BlockSpec((SUBLANES, d), lambda i: (0, 0))],
                 out_shape=[jax.ShapeDtypeStruct((s, d), BF16), jax.ShapeDtypeStruct((s, d), BF16),
                            jax.ShapeDtypeStruct((s, d), F32), jax.ShapeDtypeStruct((SUBLANES, d), F32)],
                 compiler_params=_params(("arbitrary",)))(x1, gl, pe, tgt, bias, ple_gain)


def adamw(g, w, m, v, name):
    r, c = g.shape
    tr = _tile(r, 128)

    def body(g_ref, w_ref, m_ref, v_ref, go_ref, d_ref, mo_ref, vo_ref):
        gv = g_ref[...]
        mn = ADAM_B1 * m_ref[...] + (1.0 - ADAM_B1) * gv
        vn = ADAM_B2 * v_ref[...] + (1.0 - ADAM_B2) * (gv * gv)
        m_hat = mn / (1.0 - ADAM_B1 ** ADAM_STEP)
        v_hat = vn / (1.0 - ADAM_B2 ** ADAM_STEP)
        go_ref[...] = gv
        d_ref[...] = -ADAM_LR * (m_hat / (jnp.sqrt(v_hat) + ADAM_EPS) + ADAM_WD * w_ref[...])
        mo_ref[...] = mn
        vo_ref[...] = vn

    blk = pl.BlockSpec((tr, c), lambda i: (i, 0))
    shp = jax.ShapeDtypeStruct((r, c), F32)
    return _call(body, name=name, grid=(r // tr,), in_specs=[blk] * 4, out_specs=[blk] * 4,
                 out_shape=[shp] * 4, compiler_params=_params(("parallel",)))(g, w, m, v)


def matmul(a, b, *, grid, a_spec, b_spec, o_spec, out_shape, dims, name, res=None, res_spec=None, after=()):
    nk = grid[2]
    acc_shape = tuple(d for d in o_spec.block_shape if d is not None)

    def body(*refs):
        a_ref, b_ref = refs[:2]
        r_ref = refs[2] if res is not None else None
        o_ref = refs[3] if res is not None else refs[2]
        part = lax.dot_general(a_ref[...].astype(BF16), b_ref[...].astype(BF16), dims, preferred_element_type=F32)

        def finish(out):
            if res is not None:
                out = r_ref[...] + out
            o_ref[...] = out.astype(o_ref.dtype)

        if nk == 1:
            finish(part)
            return
        acc_ref = refs[-1]
        k = pl.program_id(2)

        @pl.when(k == 0)
        def _():
            acc_ref[...] = part

        @pl.when((k > 0) & (k < nk - 1))
        def _():
            acc_ref[...] += part

        @pl.when(k == nk - 1)
        def _():
            finish(acc_ref[...] + part)

    in_specs = [a_spec, b_spec] + ([res_spec] if res is not None else [])
    args = (a, b) + ((res,) if res is not None else ())
    return _call_after(body, after, name=name, grid=grid, in_specs=in_specs, out_specs=o_spec,
                       out_shape=out_shape, scratch_shapes=[pltpu.VMEM(acc_shape, F32)] if nk > 1 else [],
                       compiler_params=_params(("parallel", "parallel", "arbitrary")))(*args)


def in_proj_part(h, w, z_prev, order, q, in_w):
    s, d = h.shape
    sh = w.shape[1]
    tm = _tile(s, 1024)

    def body(order_ref, h_ref, w_ref, *rest):
        rest[-1][...] = jnp.dot(h_ref[...], w_ref[...], preferred_element_type=F32)

    in_specs = [pl.BlockSpec((tm, d), lambda i, order_ref: (i, 0)),
                pl.BlockSpec((d, sh), lambda i, order_ref: (0, 0))]
    args = [order, h, w]
    if z_prev is not None:
        in_specs.append(_hbm())
        args.append(z_prev)
    grid_spec = pltpu.PrefetchScalarGridSpec(
        num_scalar_prefetch=1, grid=(s // tm,), in_specs=in_specs,
        out_specs=pl.BlockSpec((tm, sh), lambda i, order_ref: (i, order_ref[q])))
    return _call(body, name="mm_z%d" % q, grid_spec=grid_spec,
                 out_shape=jax.ShapeDtypeStruct((s, in_w), F32),
                 input_output_aliases={3: 0} if z_prev is not None else {},
                 compiler_params=_params(("parallel",)))(*args)


def in_proj_bwd(dz, ws, order, d, after):
    s = dz.shape[0]
    sh = ws[0].shape[1]
    tm, tn = _tile(s, 512), _tile(d, 512)
    nq, n_after = len(ws), len(after)

    def body(order_ref, *refs):
        a_refs, b_refs, o_ref = refs[:nq], refs[nq:2 * nq], refs[2 * nq + n_after]
        acc = lax.dot_general(a_refs[0][...], b_refs[0][...], NT, preferred_element_type=F32)
        for q in range(1, nq):
            acc += lax.dot_general(a_refs[q][...], b_refs[q][...], NT, preferred_element_type=F32)
        o_ref[...] = acc

    a_spec = lambda q: pl.BlockSpec((tm, sh), functools.partial(lambda i, j, order_ref, q: (i, order_ref[q]), q=q))
    grid_spec = pltpu.PrefetchScalarGridSpec(
        num_scalar_prefetch=1, grid=(s // tm, d // tn),
        in_specs=[a_spec(q) for q in range(nq)]
        + [pl.BlockSpec((tn, sh), lambda i, j, order_ref: (j, 0))] * nq + [_hbm()] * n_after,
        out_specs=pl.BlockSpec((tm, tn), lambda i, j, order_ref: (i, j)))
    return _call(body, name="mm_d_h", grid_spec=grid_spec, out_shape=jax.ShapeDtypeStruct((s, d), F32),
                 compiler_params=_params(("parallel", "parallel")))(order, *([dz] * nq), *ws, *after)


def _seg_mean(sq, segb):
    parts = []
    for cgrp in range(sq.shape[1] // SEG):
        blk = sq[:, cgrp * SEG:(cgrp + 1) * SEG]
        hi = blk.astype(BF16)
        r1 = blk - hi.astype(F32)
        mid = r1.astype(BF16)
        lo = (r1 - mid.astype(F32)).astype(BF16)
        acc = jnp.dot(hi, segb, preferred_element_type=F32)
        acc += jnp.dot(mid, segb, preferred_element_type=F32)
        acc += jnp.dot(lo, segb, preferred_element_type=F32)
        parts.append(acc)
    out = parts[0] if len(parts) == 1 else jnp.concatenate(parts, axis=1)
    return out * (1.0 / HEAD_DIM)


def _rope(v, cos, sa, sb):
    parts = []
    for cgrp in range(v.shape[1] // LANES):
        blk = v[:, cgrp * LANES:(cgrp + 1) * LANES]
        parts.append(blk * cos + pltpu.roll(blk, LANES - 8, 1) * sa + pltpu.roll(blk, 8, 1) * sb)
    return parts[0] if len(parts) == 1 else jnp.concatenate(parts, axis=1)


def _rope_t(dv, cos, sa, sb):
    parts = []
    for cgrp in range(dv.shape[1] // LANES):
        blk = dv[:, cgrp * LANES:(cgrp + 1) * LANES]
        parts.append(blk * cos + pltpu.roll(blk * sa, 8, 1) + pltpu.roll(blk * sb, LANES - 8, 1))
    return parts[0] if len(parts) == 1 else jnp.concatenate(parts, axis=1)


def _tile_lanes(vec, width):
    reps = width // LANES
    return vec if reps == 1 else jnp.tile(vec, (1, reps))


def qk_prep_fwd(z, tabs, qg, kg, segb, aw):
    s = z.shape[0]
    tm = _tile(s, 256)
    wq = aw + 2 * KV_WIDTH

    def body(z_ref, cos_ref, sa_ref, sb_ref, qg_ref, kg_ref, seg_ref, qs_ref, ks_ref, vb_ref):
        zz = z_ref[...]
        q, k, v = zz[:, :aw], zz[:, aw:aw + KV_WIDTH], zz[:, aw + KV_WIDTH:]
        cos, sa, sb, segm = cos_ref[...], sa_ref[...], sb_ref[...], seg_ref[...]
        rq = lax.rsqrt(_seg_mean(q * q, segm) + EPS)
        qn = (q * rq) * _tile_lanes(qg_ref[...], aw)
        qs_ref[...] = (_rope(qn, cos, sa, sb) * (HEAD_DIM ** -0.5)).astype(BF16)
        rk = lax.rsqrt(_seg_mean(k * k, segm) + EPS)
        kn = (k * rk) * _tile_lanes(kg_ref[...], KV_WIDTH)
        ks_ref[...] = _rope(kn, cos, sa, sb).astype(BF16)
        vb_ref[...] = v.astype(BF16)

    tab = pl.BlockSpec((tm, LANES), lambda i: (i, 0))
    vec = pl.BlockSpec((1, LANES), lambda i: (0, 0))
    return _call(body, name="qk_prep_fwd", grid=(s // tm,),
                 in_specs=[pl.BlockSpec((tm, wq), lambda i: (i, 0)), tab, tab, tab, vec, vec,
                           pl.BlockSpec((SEG, SEG), lambda i: (0, 0))],
                 out_specs=[pl.BlockSpec((tm, aw), lambda i: (i, 0)),
                            pl.BlockSpec((tm, KV_WIDTH), lambda i: (i, 0)),
                            pl.BlockSpec((tm, KV_WIDTH), lambda i: (i, 0))],
                 out_shape=[jax.ShapeDtypeStruct((s, aw), BF16), jax.ShapeDtypeStruct((s, KV_WIDTH), BF16),
                            jax.ShapeDtypeStruct((s, KV_WIDTH), BF16)],
                 compiler_params=_params(("parallel",)))(z, *tabs, qg, kg, segb)


def qk_prep_bwd(z, dqs, dks, dvs, tabs, qg, kg, segb, dz, aw):
    s = z.shape[0]
    tm = _tile(s, 256)
    wq = aw + 2 * KV_WIDTH

    def body(z_ref, dq_ref, dk_ref, dv_ref, cos_ref, sa_ref, sb_ref, qg_ref, kg_ref, seg_ref, dz_in,
             dz_ref, acc_ref):
        i = pl.program_id(0)
        zz = z_ref[...]
        q, k = zz[:, :aw], zz[:, aw:aw + KV_WIDTH]
        cos, sa, sb, segm = cos_ref[...], sa_ref[...], sb_ref[...], seg_ref[...]

        def one(xv, dout, gvec, width):
            g = _tile_lanes(gvec, width)
            r = lax.rsqrt(_seg_mean(xv * xv, segm) + EPS)
            xhat = xv * r
            dn = _rope_t(dout, cos, sa, sb)
            gd = dn * g
            dx = r * (gd - xhat * _seg_mean(xhat * gd, segm))
            contrib = jnp.sum(dn * xhat, axis=0, keepdims=True)
            folded = contrib[:, :LANES]
            for cgrp in range(1, width // LANES):
                folded = folded + contrib[:, cgrp * LANES:(cgrp + 1) * LANES]
            return dx, folded + pltpu.roll(folded, HEAD_DIM, 1)

        dq, gq = one(q, dq_ref[...] * (HEAD_DIM ** -0.5), qg_ref[...], aw)
        dk, gk = one(k, dk_ref[...], kg_ref[...], KV_WIDTH)
        dz_ref[:, :aw] = dq.astype(BF16)
        dz_ref[:, aw:aw + KV_WIDTH] = dk.astype(BF16)
        dz_ref[:, aw + KV_WIDTH:] = dv_ref[...].astype(BF16)

        @pl.when(i == 0)
        def _():
            acc_ref[...] = jnp.zeros_like(acc_ref)

        acc_ref[0:1, :] += gq
        acc_ref[1:2, :] += gk

    tab = pl.BlockSpec((tm, LANES), lambda i: (i, 0))
    vec = pl.BlockSpec((1, LANES), lambda i: (0, 0))
    kvb = pl.BlockSpec((tm, KV_WIDTH), lambda i: (i, 0))
    return _call(body, name="qk_prep_bwd", grid=(s // tm,),
                 in_specs=[pl.BlockSpec((tm, wq), lambda i: (i, 0)),
                           pl.BlockSpec((tm, aw), lambda i: (i, 0)), kvb, kvb, tab, tab, tab, vec, vec,
                           pl.BlockSpec((SEG, SEG), lambda i: (0, 0)), _hbm()],
                 out_specs=[pl.BlockSpec((tm, wq), lambda i: (i, 0)),
                            pl.BlockSpec((SUBLANES, LANES), lambda i: (0, 0))],
                 out_shape=[jax.ShapeDtypeStruct(dz.shape, BF16), jax.ShapeDtypeStruct((SUBLANES, LANES), F32)],
                 input_output_aliases={10: 0},
                 compiler_params=_params(("arbitrary",)))(z, dqs, dks, dvs, *tabs, qg, kg, segb, dz)


def _placed(band, lane_idx):
    out = {}
    for kh in range(N_KV_HEADS):
        grp = band[:, (kh // 2) * LANES:(kh // 2 + 1) * LANES]
        for half in (0, 1):
            t = grp if half == kh % 2 else pltpu.roll(grp, HEAD_DIM, 1)
            keep = (lane_idx >= half * HEAD_DIM) & (lane_idx < (half + 1) * HEAD_DIM)
            out[kh, half] = jnp.where(keep, t, jnp.zeros_like(t))
    return out


def _softmax_with_sink(sc, valid, sink):
    sc = jnp.where(valid, sc, NEG_INF)
    m = jnp.maximum(jnp.max(sc, axis=-1, keepdims=True), sink)
    e = jnp.exp(sc - m)
    es = jnp.exp(sink - m)
    den = jnp.sum(e, axis=-1, keepdims=True) + es
    return e / den, es / den


def _stack_halves(placed, kh):
    return jnp.concatenate([placed[kh, 0], placed[kh, 1]], axis=0)


def _valid_mask(n):
    r_i = lax.broadcasted_iota(jnp.int32, (BLOCK, 2 * BLOCK), 0)
    j_i = lax.broadcasted_iota(jnp.int32, (BLOCK, 2 * BLOCK), 1)
    return (j_i > r_i) & (j_i <= r_i + BLOCK) & ((n > 0) | (j_i >= BLOCK))


def attn_fwd(qs, ks, vb, z, sinks, aw, d, ga_off):
    s = qs.shape[0]
    nb = s // BLOCK
    nq = aw // HEAD_DIM
    grp_sz = nq // N_KV_HEADS
    n_ga = aw // COLT

    def body(q_ref, ko_ref, kp_ref, vo_ref, vp_ref, *rest):
        ga_refs = rest[:n_ga]
        sink_ref, attn_ref, mix_ref = rest[n_ga:]
        n = pl.program_id(0)
        lane_idx = lax.broadcasted_iota(jnp.int32, (2 * BLOCK, LANES), 1)
        kpl = _placed(jnp.concatenate([kp_ref[...], ko_ref[...]], axis=0), lane_idx)
        vpl = _placed(jnp.concatenate([vp_ref[...], vo_ref[...]], axis=0), lane_idx)
        valid = _valid_mask(n)
        for cgrp in range(nq // 2):
            kh = 2 * cgrp // grp_sz
            qgrp = q_ref[:, cgrp * LANES:(cgrp + 1) * LANES]
            sc = lax.dot_general(qgrp, _stack_halves(kpl, kh), NT, preferred_element_type=F32)
            probs = [_softmax_with_sink(sc[:, half * 2 * BLOCK:(half + 1) * 2 * BLOCK], valid,
                                        sink_ref[0, 2 * cgrp + half])[0].astype(BF16) for half in (0, 1)]
            acc = jnp.dot(jnp.concatenate(probs, axis=1), _stack_halves(vpl, kh), preferred_element_type=F32)
            attn_ref[:, cgrp * LANES:(cgrp + 1) * LANES] = acc
            col = cgrp * LANES
            ga = ga_refs[col // COLT][:, col % COLT:col % COLT + LANES]
            mix_ref[:, cgrp * LANES:(cgrp + 1) * LANES] = (acc * (ga * _sigmoid(ga))).astype(BF16)

    own = lambda n: (n, 0)
    prev = lambda n: (jnp.maximum(n - 1, 0), 0)
    kvs = lambda imap: pl.BlockSpec((BLOCK, KV_WIDTH), imap)
    ga_specs = [pl.BlockSpec((BLOCK, COLT), functools.partial(lambda n, j: (n, ga_off + j), j=j))
                for j in range(n_ga)]
    return _call(body, name="attn_fwd", grid=(nb,),
                 in_specs=[pl.BlockSpec((BLOCK, aw), own), kvs(own), kvs(prev), kvs(own), kvs(prev)]
                 + ga_specs + [pl.BlockSpec(memory_space=pltpu.SMEM)],
                 out_specs=[pl.BlockSpec((BLOCK, aw), own), pl.BlockSpec((BLOCK, aw), own)],
                 out_shape=[jax.ShapeDtypeStruct((s, aw), F32), jax.ShapeDtypeStruct((s, d), BF16)],
                 compiler_params=_params(("parallel",)))(qs, ks, ks, vb, vb, *([z] * n_ga), sinks)


def gate_bwd(d_mix, attn, z, aw, ga_off, after=()):
    s, in_w = z.shape
    tm = _tile(s, 256)

    def body(dm_ref, at_ref, ga_ref, do_ref, dz_ref):
        ga = ga_ref[...]
        sig = _sigmoid(ga)
        dm = dm_ref[...]
        do_ref[...] = (dm * (ga * sig)).astype(BF16)
        dz_ref[...] = ((dm * at_ref[...]) * (sig * (1.0 + ga * (1.0 - sig)))).astype(BF16)

    blk = pl.BlockSpec((tm, COLT), lambda i, j: (i, j))
    gab = pl.BlockSpec((tm, COLT), lambda i, j: (i, ga_off + j))
    return _call_after(body, after, name="gate_bwd", grid=(s // tm, aw // COLT),
                       in_specs=[blk, blk, gab], out_specs=[blk, gab],
                       out_shape=[jax.ShapeDtypeStruct((s, aw), BF16), jax.ShapeDtypeStruct((s, in_w), BF16)],
                       compiler_params=_params(("parallel", "parallel")))(d_mix, attn, z)


def attn_bwd(qs, ks, vb, d_o, sinks, aw):
    s = qs.shape[0]
    nb = s // BLOCK
    nq = aw // HEAD_DIM
    grp_sz = nq // N_KV_HEADS

    def body(q_ref, ko_ref, kp_ref, vo_ref, vp_ref, do_ref, sink_ref, dq_ref, dk_ref, dv_ref, ds_ref,
             ck_ref, cv_ref):
        n = pl.program_id(0)

        @pl.when(n == 0)
        def _():
            ds_ref[...] = jnp.zeros_like(ds_ref)
            dk_ref[...] = jnp.zeros_like(dk_ref)
            dv_ref[...] = jnp.zeros_like(dv_ref)

        @pl.when(n < nb)
        def _():
            lane_idx = lax.broadcasted_iota(jnp.int32, (2 * BLOCK, LANES), 1)
            lane_row = lax.broadcasted_iota(jnp.int32, (1, LANES), 1)
            kpl = _placed(jnp.concatenate([kp_ref[...], ko_ref[...]], axis=0), lane_idx)
            vpl = _placed(jnp.concatenate([vp_ref[...], vo_ref[...]], axis=0), lane_idx)
            valid = _valid_mask(n)
            ds_row = jnp.zeros((1, LANES), F32)
            wide = 2 * BLOCK
            dkt, dvt = [], []
            for kh in range(N_KV_HEADS):
                kcat, vcat = _stack_halves(kpl, kh), _stack_halves(vpl, kh)
                dkt_sum = jnp.zeros((LANES, 2 * wide), F32)
                dvt_sum = jnp.zeros((LANES, 2 * wide), F32)
                for cgrp in range(kh * grp_sz // 2, (kh + 1) * grp_sz // 2):
                    qgrp = q_ref[:, cgrp * LANES:(cgrp + 1) * LANES]
                    dog = do_ref[:, cgrp * LANES:(cgrp + 1) * LANES]
                    sc = lax.dot_general(qgrp, kcat, NT, preferred_element_type=F32)
                    dp = lax.dot_general(dog, vcat, NT, preferred_element_type=F32)
                    probs, dss = [], []
                    for half in (0, 1):
                        h = 2 * cgrp + half
                        cols = slice(half * wide, (half + 1) * wide)
                        p, p_sink = _softmax_with_sink(sc[:, cols], valid, sink_ref[0, h])
                        delta = jnp.sum(p * dp[:, cols], axis=-1, keepdims=True)
                        dss.append((p * (dp[:, cols] - delta)).astype(BF16))
                        probs.append(p.astype(BF16))
                        dsink = -jnp.sum(p_sink * delta, axis=0, keepdims=True)
                        ds_row += jnp.where(lane_row == h, dsink, 0.0)
                    ds_pair = jnp.concatenate(dss, axis=1)
                    dq_ref[:, cgrp * LANES:(cgrp + 1) * LANES] = jnp.dot(ds_pair, kcat, preferred_element_type=F32)
                    dkt_sum += lax.dot_general(qgrp, ds_pair, TN, preferred_element_type=F32)
                    dvt_sum += lax.dot_general(dog, jnp.concatenate(probs, axis=1), TN, preferred_element_type=F32)
                dkt.append(dkt_sum[:HEAD_DIM, :wide] + dkt_sum[HEAD_DIM:, wide:])
                dvt.append(dvt_sum[:HEAD_DIM, :wide] + dvt_sum[HEAD_DIM:, wide:])
            ds_ref[0:1, :] += ds_row
            back = lambda t: jnp.concatenate(
                [jnp.concatenate(t[2 * g:2 * g + 2], axis=0).T for g in range(N_KV_HEADS // 2)], axis=1)
            dk_band, dv_band = back(dkt), back(dvt)

            @pl.when(n > 0)
            def _():
                dk_ref[...] = ck_ref[...] + dk_band[:BLOCK]
                dv_ref[...] = cv_ref[...] + dv_band[:BLOCK]

            ck_ref[...] = dk_band[BLOCK:]
            cv_ref[...] = dv_band[BLOCK:]

        @pl.when(n == nb)
        def _():
            dk_ref[...] = ck_ref[...]
            dv_ref[...] = cv_ref[...]

    own = lambda n: (jnp.minimum(n, nb - 1), 0)
    prev = lambda n: (jnp.clip(n - 1, 0, nb - 1), 0)
    done = lambda n: (jnp.maximum(n - 1, 0), 0)
    kvs = lambda imap: pl.BlockSpec((BLOCK, KV_WIDTH), imap)
    return _call(body, name="attn_bwd", grid=(nb + 1,),
                 in_specs=[pl.BlockSpec((BLOCK, aw), own), kvs(own), kvs(prev), kvs(own), kvs(prev),
                           pl.BlockSpec((BLOCK, aw), own), pl.BlockSpec(memory_space=pltpu.SMEM)],
                 out_specs=[pl.BlockSpec((BLOCK, aw), own), kvs(done), kvs(done),
                            pl.BlockSpec((SUBLANES, LANES), lambda n: (0, 0))],
                 out_shape=[jax.ShapeDtypeStruct((s, aw), F32), jax.ShapeDtypeStruct((s, KV_WIDTH), F32),
                            jax.ShapeDtypeStruct((s, KV_WIDTH), F32), jax.ShapeDtypeStruct((SUBLANES, LANES), F32)],
                 scratch_shapes=[pltpu.VMEM((BLOCK, KV_WIDTH), F32), pltpu.VMEM((BLOCK, KV_WIDTH), F32)],
                 compiler_params=_params(("arbitrary",)))(qs, ks, ks, vb, vb, d_o, sinks)


def conv_fwd(z, conv_w, mix, aw, cw, b_off):
    s = z.shape[0]
    tm = _tile(s, 256)
    nseg = cw // COLT
    hb = tm // SUBLANES

    def body(b_ref, c_ref, h_ref, g_ref, cp_ref, hp_ref, w_ref, mix_in, mix_ref):
        i = pl.program_id(0)
        u = c_ref[...] * h_ref[...]
        up = jnp.where(i > 0, cp_ref[...] * hp_ref[...], 0.0)
        ext = jnp.concatenate([up, u], axis=0)
        um1 = pltpu.roll(ext, 1, 0)[SUBLANES:]
        um2 = pltpu.roll(ext, 2, 0)[SUBLANES:]
        w = w_ref[...]
        cv = w[0:1] * um2 + w[1:2] * um1 + w[2:3] * u
        g = g_ref[...]
        mix_ref[...] = ((b_ref[...] * cv) * (g * _sigmoid(g))).astype(BF16)

    seg = lambda k: pl.BlockSpec((tm, COLT), functools.partial(lambda i, j, k: (i, b_off + k * nseg + j), k=k))
    halo = lambda k: pl.BlockSpec(
        (SUBLANES, COLT), functools.partial(lambda i, j, k: (jnp.maximum(i * hb - 1, 0), b_off + k * nseg + j), k=k))
    return _call(body, name="conv_fwd", grid=(s // tm, nseg),
                 in_specs=[seg(0), seg(1), seg(2), seg(3), halo(1), halo(2),
                           pl.BlockSpec((SUBLANES, COLT), lambda i, j: (0, j)), _hbm()],
                 out_specs=pl.BlockSpec((tm, COLT), lambda i, j: (i, aw // COLT + j)),
                 out_shape=jax.ShapeDtypeStruct(mix.shape, BF16),
                 input_output_aliases={7: 0},
                 compiler_params=_params(("parallel", "parallel")))(z, z, z, z, z, z, conv_w, mix)


def conv_bwd(z, d_mix, conv_w, dz, aw, cw, b_off):
    s = z.shape[0]
    tm = _tile(s, 256)
    nseg = cw // COLT
    hb = tm // SUBLANES
    n_row = s // tm
    last_h = s // SUBLANES - 1
    n_step = nseg * n_row

    def body(b_ref, c_ref, h_ref, g_ref, cp_ref, hp_ref, bn_ref, gn_ref, dm_ref, dmn_ref, w_ref, dz_in,
             dz_ref, acc_ref, stash_ref, sems):
        j = pl.program_id(0)
        i = pl.program_id(1)
        step = j * n_row + i
        slot = step % 2

        def copies(from_slot, at_step):
            jj, ii = at_step // n_row, at_step % n_row
            rows = pl.ds(pl.multiple_of(ii * tm, tm), tm)
            return [pltpu.make_async_copy(
                stash_ref.at[from_slot, q],
                dz_ref.at[rows, pl.ds(pl.multiple_of((b_off + q * nseg + jj) * COLT, COLT), COLT)],
                sems.at[from_slot, q]) for q in range(4)]

        @pl.when(step >= 2)
        def _():
            for cp in copies(slot, step - 2):
                cp.wait()

        cc, hh, bb, g = c_ref[...], h_ref[...], b_ref[...], g_ref[...]
        w = w_ref[...]
        u = cc * hh
        up = jnp.where(i > 0, cp_ref[...] * hp_ref[...], 0.0)
        ext = jnp.concatenate([up, u], axis=0)
        um1 = pltpu.roll(ext, 1, 0)[SUBLANES:]
        um2 = pltpu.roll(ext, 2, 0)[SUBLANES:]
        cv = w[0:1] * um2 + w[1:2] * um1 + w[2:3] * u
        sig = _sigmoid(g)
        sg = g * sig
        dm = dm_ref[...]
        d_cv = (dm * bb) * sg
        gn = gn_ref[...]
        d_cv_next = jnp.where(i < n_row - 1, (dmn_ref[...] * bn_ref[...]) * (gn * _sigmoid(gn)), 0.0)
        ext2 = jnp.concatenate([d_cv, d_cv_next], axis=0)
        dp1 = pltpu.roll(ext2, tm + SUBLANES - 1, 0)[:tm]
        dp2 = pltpu.roll(ext2, tm + SUBLANES - 2, 0)[:tm]
        d_u = w[2:3] * d_cv + w[1:2] * dp1 + w[0:1] * dp2
        stash_ref[slot, 0] = ((dm * cv) * sg).astype(BF16)
        stash_ref[slot, 1] = (d_u * hh).astype(BF16)
        stash_ref[slot, 2] = (d_u * cc).astype(BF16)
        stash_ref[slot, 3] = (((dm * bb) * cv) * (sig * (1.0 + g * (1.0 - sig)))).astype(BF16)
        for cp in copies(slot, step):
            cp.start()

        @pl.when(i == 0)
        def _():
            acc_ref[...] = jnp.zeros_like(acc_ref)

        acc_ref[0:1, :] += jnp.sum(d_cv * um2, axis=0, keepdims=True)
        acc_ref[1:2, :] += jnp.sum(d_cv * um1, axis=0, keepdims=True)
        acc_ref[2:3, :] += jnp.sum(d_cv * u, axis=0, keepdims=True)

        @pl.when(step == n_step - 1)
        def _():
            if n_step >= 2:
                for cp in copies(1 - slot, step - 1):
                    cp.wait()
            for cp in copies(slot, step):
                cp.wait()

    seg = lambda q: pl.BlockSpec((tm, COLT), functools.partial(lambda j, i, q: (i, b_off + q * nseg + j), q=q))
    halo_p = lambda q: pl.BlockSpec(
        (SUBLANES, COLT),
        functools.partial(lambda j, i, q: (jnp.maximum(i * hb - 1, 0), b_off + q * nseg + j), q=q))
    halo_n = lambda q: pl.BlockSpec(
        (SUBLANES, COLT),
        functools.partial(lambda j, i, q: (jnp.minimum((i + 1) * hb, last_h), b_off + q * nseg + j), q=q))
    return _call(body, name="conv_bwd", grid=(nseg, n_row),
                 in_specs=[seg(0), seg(1), seg(2), seg(3), halo_p(1), halo_p(2), halo_n(0), halo_n(3),
                           pl.BlockSpec((tm, COLT), lambda j, i: (i, aw // COLT + j)),
                           pl.BlockSpec((SUBLANES, COLT),
                                        lambda j, i: (jnp.minimum((i + 1) * hb, last_h), aw // COLT + j)),
                           pl.BlockSpec((SUBLANES, COLT), lambda j, i: (0, j)), _hbm()],
                 out_specs=[_hbm(), pl.BlockSpec((SUBLANES, COLT), lambda j, i: (0, j))],
                 out_shape=[jax.ShapeDtypeStruct(dz.shape, BF16), jax.ShapeDtypeStruct((SUBLANES, cw), F32)],
                 input_output_aliases={11: 0},
                 scratch_shapes=[pltpu.VMEM((2, 4, tm, COLT), BF16), pltpu.SemaphoreType.DMA((2, 4))],
                 compiler_params=_params(("arbitrary", "arbitrary")))(
                     z, z, z, z, z, z, z, z, d_mix, d_mix, conv_w, dz)


def _place():
    x, y, c = lax.axis_index("x"), lax.axis_index("y"), lax.axis_index("c")
    chips = [(1 - x, y), (x, 1 - y), (1 - x, 1 - y)]
    return x, y, c, chips


def _remote(src, dst, send_sem, recv_sem, device):
    return pltpu.make_async_remote_copy(src_ref=src, dst_ref=dst, send_sem=send_sem, recv_sem=recv_sem,
                                        device_id=device, device_id_type=MESH)


def plan_gather_ici(n_split):
    def plan(refs, send, recv):
        x, y, c, chips = _place()
        me = 2 * x + y
        mine, theirs = [], []
        for w, ref in enumerate(refs):
            for j, (cx, cy) in enumerate(chips):
                def part(slot):
                    if w >= n_split:
                        return ref.at[slot]
                    hr = ref.shape[1] // 2
                    return ref.at[slot, pl.ds(c * hr, hr)]
                k = 3 * w + j
                mine.append(_remote(part(me), part(me), send.at[k], recv.at[k], (cx, cy, c)))
                theirs.append(_remote(part(2 * cx + cy), part(2 * cx + cy), send.at[k], recv.at[k], (cx, cy, c)))
        return mine, theirs
    return plan


def plan_shard_ici(j):
    def plan(refs, send, recv):
        _, _, c, chips = _place()
        own, land = refs
        hr = own.shape[0] // 2
        rows = pl.ds(c * hr, hr)
        cp = _remote(own.at[rows], land.at[rows], send.at[0], recv.at[0], (*chips[j], c))
        return [cp], [cp]
    return plan


def plan_shard_pass(refs, send, recv):
    x, y, c, _ = _place()
    land, = refs
    hr = land.shape[0] // 2
    half = lambda core: land.at[pl.ds(core * hr, hr)]
    return ([_remote(half(c), half(c), send.at[0], recv.at[0], (x, y, 1 - c))],
            [_remote(half(1 - c), half(1 - c), send.at[0], recv.at[0], (x, y, 1 - c))])


def plan_gather_pass(refs, send, recv):
    x, y, c, chips = _place()
    mine, theirs = [], []
    for w, ref in enumerate(refs):
        hr = ref.shape[1] // 2
        for j, (cx, cy) in enumerate(chips):
            half = lambda core: ref.at[2 * cx + cy, pl.ds(core * hr, hr)]
            k = 3 * w + j
            mine.append(_remote(half(c), half(c), send.at[k], recv.at[k], (x, y, 1 - c)))
            theirs.append(_remote(half(1 - c), half(1 - c), send.at[k], recv.at[k], (x, y, 1 - c)))
    return mine, theirs


def plan_swap(refs, send, recv):
    x, y, c, _ = _place()
    nw = len(refs) // 2
    mine = []
    for w in range(nw):
        hr = refs[w].shape[1] // 2
        mine.append(_remote(refs[w].at[:, pl.ds((1 - c) * hr, hr), :], refs[nw + w], send.at[w], recv.at[w],
                            (x, y, 1 - c)))
    return mine, mine


def plan_scatter(refs, send, recv):
    x, y, c, chips = _place()
    me = 2 * x + y
    nw = len(refs) // 2
    mine, theirs = [], []
    for w in range(nw):
        for j, (cx, cy) in enumerate(chips):
            k = 3 * w + j
            mine.append(_remote(refs[w].at[2 * cx + cy], refs[nw + w].at[me], send.at[k], recv.at[k], (cx, cy, c)))
            theirs.append(_remote(refs[w].at[me], refs[nw + w].at[2 * cx + cy], send.at[k], recv.at[k], (cx, cy, c)))
    return mine, theirs


def plan_join(refs, send, recv):
    x, y, c, _ = _place()
    mine, theirs = [], []
    for w, ref in enumerate(refs):
        hr = ref.shape[0] // 2
        half = lambda core: ref.at[pl.ds(core * hr, hr)]
        mine.append(_remote(half(c), half(c), send.at[w], recv.at[w], (x, y, 1 - c)))
        theirs.append(_remote(half(1 - c), half(1 - c), send.at[w], recv.at[w], (x, y, 1 - c)))
    return mine, theirs


def exchange(name, plan, arrays, n):
    na = len(arrays)

    def body(*refs):
        mine, theirs = plan(refs[:na], refs[2 * na], refs[2 * na + 1])
        for cp in mine:
            cp.start()
        for cp in theirs:
            cp.wait_recv()
        for cp in mine:
            cp.wait_send()

    return _call(body, name=name, in_specs=[_hbm()] * na, out_specs=[_hbm()] * na,
                 out_shape=[jax.ShapeDtypeStruct(a.shape, a.dtype) for a in arrays],
                 input_output_aliases={i: i for i in range(na)},
                 scratch_shapes=[pltpu.SemaphoreType.DMA((n,)), pltpu.SemaphoreType.DMA((n,))])(*arrays)


def exchange_start(name, groups, arrays, after=()):
    na, ng = len(arrays), len(groups)

    def body(*refs):
        for g, (plan, idx, _) in enumerate(groups):
            mine, _ = plan([refs[i] for i in idx], refs[na + 2 * g], refs[na + 2 * g + 1])
            for cp in mine:
                cp.start()
        refs[-1][...] = jnp.zeros_like(refs[-1])

    hbm = pl.BlockSpec(memory_space=pltpu.HBM)
    sem = pl.BlockSpec(memory_space=pltpu.SEMAPHORE)
    sem_types = [pltpu.SemaphoreType.DMA((n,)) for _, _, n in groups for _ in (0, 1)]
    outs = _call_after(body, after, name=name, in_specs=[hbm] * na,
                       out_specs=[sem] * (2 * ng) + [hbm] * na + [pl.BlockSpec(memory_space=pltpu.VMEM)],
                       out_shape=sem_types + [pltpu.HBM(a.shape, a.dtype) for a in arrays]
                       + [jax.ShapeDtypeStruct((SUBLANES, LANES), F32)],
                       input_output_aliases={i: i + 2 * ng for i in range(na)},
                       compiler_params=pltpu.CompilerParams(
                           has_side_effects=pltpu.SideEffectType.DATAFLOW_SIDE_EFFECTING))(
                               *[pltpu.with_memory_space_constraint(a, pltpu.HBM) for a in arrays])
    sems = [(outs[2 * g], outs[2 * g + 1]) for g in range(ng)]
    return sems, list(outs[2 * ng:-1]), outs[-1]


def exchange_wait(name, plan, sems, arrays, after):
    send_sems, recv_sems = sems
    na = len(arrays)

    def body(*refs):
        mine, theirs = plan(refs[:na], refs[na], refs[na + 1])
        for cp in mine:
            cp.wait_send()
        for cp in theirs:
            cp.wait_recv()

    hbm = pl.BlockSpec(memory_space=pltpu.HBM)
    sem = pl.BlockSpec(memory_space=pltpu.SEMAPHORE)
    return _call(body, name=name, in_specs=[hbm] * na + [sem, sem, _hbm()], out_specs=[hbm] * na,
                 out_shape=[pltpu.HBM(a.shape, a.dtype) for a in arrays],
                 input_output_aliases={i: i for i in range(na)},
                 compiler_params=pltpu.CompilerParams(
                     has_side_effects=pltpu.SideEffectType.DATAFLOW_SIDE_EFFECTING))(
                         *arrays, send_sems, recv_sems, after)


def allgather_small(small):
    rows, width = small.shape

    def body(in_ref, out_ref, send, recv, loc):
        x, y, c, _ = _place()
        me = 4 * x + 2 * y + c
        local = pltpu.make_async_copy(in_ref, out_ref.at[me], loc)
        local.start()
        flips = [(fx, fy, fc) for fx in (0, 1) for fy in (0, 1) for fc in (0, 1)][1:]

        def cp(k, slot):
            fx, fy, fc = flips[k]
            return pltpu.make_async_remote_copy(
                src_ref=in_ref, dst_ref=out_ref.at[slot], send_sem=send.at[k], recv_sem=recv.at[k],
                device_id=(x ^ fx, y ^ fy, c ^ fc), device_id_type=MESH)

        for k in range(7):
            cp(k, me).start()
        for k in range(7):
            fx, fy, fc = flips[k]
            cp(k, 4 * (x ^ fx) + 2 * (y ^ fy) + (c ^ fc)).wait_recv()
        for k in range(7):
            cp(k, me).wait_send()
        local.wait()

    return _call(body, name="allgather_small",
                 in_specs=[pl.BlockSpec(memory_space=pltpu.VMEM)],
                 out_specs=pl.BlockSpec(memory_space=pltpu.VMEM),
                 out_shape=jax.ShapeDtypeStruct((8, rows, width), F32),
                 scratch_shapes=[pltpu.SemaphoreType.DMA((7,)), pltpu.SemaphoreType.DMA((7,)),
                                 pltpu.SemaphoreType.DMA])(small)


def add_sibling(grad, got, core, name):
    _, r, c = grad.shape
    hr = r // 2
    tr = _tile(hr, 128)
    nblk = hr // tr

    def body(core_ref, g_ref, o_ref, out_ref):
        out_ref[...] = (g_ref[...].astype(F32) + o_ref[...].astype(F32)).astype(BF16)

    grid_spec = pltpu.PrefetchScalarGridSpec(
        num_scalar_prefetch=1, grid=(N_CHIPS, nblk),
        in_specs=[pl.BlockSpec((None, tr, c), lambda t, i, core_ref: (t, core_ref[0] * nblk + i, 0)),
                  pl.BlockSpec((None, tr, c), lambda t, i, core_ref: (t, i, 0))],
        out_specs=pl.BlockSpec((None, tr, c), lambda t, i, core_ref: (t, i, 0)))
    return _call(body, name=name, grid_spec=grid_spec,
                 out_shape=jax.ShapeDtypeStruct((N_CHIPS, hr, c), BF16),
                 compiler_params=_params(("parallel", "parallel")))(core, grad, got)


def sum_chips(mine, owned, place, name):
    _, hr, c = mine.shape
    tr = _tile(hr, 128)
    nblk = hr // tr

    def body(place_ref, m_ref, o1_ref, o2_ref, o3_ref, out_ref):
        acc = m_ref[...].astype(F32)
        for o_ref in (o1_ref, o2_ref, o3_ref):
            acc = acc + o_ref[...].astype(F32)
        out_ref[...] = acc

    other = lambda k: pl.BlockSpec((None, tr, c), lambda i, place_ref: ((place_ref[0] + k) % N_CHIPS, i, 0))
    grid_spec = pltpu.PrefetchScalarGridSpec(
        num_scalar_prefetch=1, grid=(nblk,),
        in_specs=[other(0), other(1), other(2), other(3)],
        out_specs=pl.BlockSpec((tr, c), lambda i, place_ref: (place_ref[1] * nblk + i, 0)))
    return _call(body, name=name, grid_spec=grid_spec,
                 out_shape=jax.ShapeDtypeStruct((2 * hr, c), F32),
                 compiler_params=_params(("parallel",)))(place, mine, owned, owned, owned)


def sum_devices(gathered):
    _, rows, width = gathered.shape

    def body(g_ref, out_ref):
        acc = g_ref[0]
        for dev in range(1, 8):
            acc = acc + g_ref[dev]
        out_ref[...] = acc
        tail = acc[SUBLANES:]
        out_ref[SUBLANES:, :] = jnp.broadcast_to(jnp.sum(tail, axis=1, keepdims=True), tail.shape)

    return _call(body, name="sum_devices",
                 in_specs=[pl.BlockSpec(memory_space=pltpu.VMEM)],
                 out_specs=pl.BlockSpec(memory_space=pltpu.VMEM),
                 out_shape=jax.ShapeDtypeStruct((rows, width), F32))(gathered)


def _rope_tables(s):
    half = ROT_DIM // 2
    inv_freq = jnp.power(jnp.float32(ROPE_THETA), -jnp.arange(half, dtype=F32) * 2.0 / ROT_DIM)
    freq64 = jnp.concatenate([inv_freq, inv_freq, jnp.zeros((HEAD_DIM - ROT_DIM,), F32)])
    freq = jnp.concatenate([freq64, freq64])[None, :]
    dim = (jnp.arange(LANES) % HEAD_DIM)[None, :]
    ang = jnp.arange(s).astype(F32)[:, None] * freq
    cos, sin = jnp.cos(ang), jnp.sin(ang)
    return cos, jnp.where(dim < half, -sin, 0.0), jnp.where((dim >= half) & (dim < ROT_DIM), sin, 0.0)


def _pad_rows(a, rows):
    return jnp.pad(a, ((0, rows - a.shape[0]), (0, 0)))


def _pad_cols(a, cols):
    return jnp.pad(a, ((0, 0), (0, cols - a.shape[1])))


def kernel(x, p, norm_gain, w_in, q_norm_gain, k_norm_gain, attn_sinks, conv_w, w_out, ple_gate_norm_gain, w_ple_gate, b_ple_gate, w_ple_proj, ple_norm_gain, loss_target, m_norm_gain, m_w_in, m_q_norm_gain, m_k_norm_gain, m_attn_sinks, m_conv_w, m_w_out, m_ple_gate_norm_gain, m_w_ple_gate, m_b_ple_gate, m_w_ple_proj, m_ple_norm_gain, v_norm_gain, v_w_in, v_q_norm_gain, v_k_norm_gain, v_attn_sinks, v_conv_w, v_w_out, v_ple_gate_norm_gain, v_w_ple_gate, v_b_ple_gate, v_w_ple_proj, v_ple_norm_gain):
    x2, p2, tgt = x[0], p[0, 0], loss_target[0]
    s, d = x2.shape
    ple = p2.shape[1]
    aw = d // 2
    cw = d - aw
    nq = aw // HEAD_DIM
    sh = w_in.shape[2]
    in_w = N_CHIPS * sh
    dq = d // N_CHIPS
    cq = cw // N_CHIPS
    ga_off = (aw + 2 * KV_WIDTH) // COLT
    b_off = (2 * aw + 2 * KV_WIDTH) // COLT
    assert in_w == 2 * aw + 2 * KV_WIDTH + 4 * cw and aw % COLT == 0 and cw % COLT == 0
    assert s % BLOCK == 0 and sh % LANES == 0 and nq % (2 * N_KV_HEADS) == 0

    core = lax.axis_index("c").astype(jnp.int32).reshape(1)
    chip = 2 * lax.axis_index("x") + lax.axis_index("y")

    chip1 = chip.astype(jnp.int32).reshape(1)
    place = jnp.concatenate([chip1, core])
    w_in_own = cast_bf16(w_in[0], "cast_w_in")
    rest = [cast_into_slot(w_out[0], chip1, "cast_w_out"), cast_into_slot(w_ple_gate[0], chip1, "cast_w_pg"),
            cast_into_slot(w_ple_proj[0], chip1, "cast_w_pp"),
            lax.dynamic_update_slice(jnp.zeros((N_CHIPS, SUBLANES, cq), F32),
                                     _pad_rows(conv_w[0], SUBLANES)[None], (chip, 0, 0))]
    order = jnp.stack([chip, chip ^ 2, chip ^ 1, chip ^ 3]).astype(jnp.int32)
    wi_sems, wi_arrs, wi_token = exchange_start(
        "gather_wi_start", [(plan_shard_ici(j), [0, 1 + j], 1) for j in range(3)],
        [w_in_own] + [lax.empty((d, sh), BF16) for _ in range(3)])
    rest_sems, rest, rest_token = exchange_start(
        "gather_rest_start", [(plan_gather_ici(3), [0, 1, 2, 3], 12)], rest, after=(wi_token,))

    tm = _tile(s, 1024)
    tn = _tile(d, 1024)
    tk = _tile(d, 2048)
    ts = _tile(s, 2048)
    h = rms_fwd(x2, norm_gain, "rms_fwd_x", after=(rest_token,))
    tabs = _rope_tables(s)
    qg = jnp.tile(q_norm_gain, (1, LANES // HEAD_DIM))
    kg = jnp.tile(k_norm_gain, (1, LANES // HEAD_DIM))
    seg_i = jnp.arange(SEG) // HEAD_DIM
    segb = (seg_i[:, None] == seg_i[None, :]).astype(BF16)
    z = in_proj_part(h, wi_arrs[0], None, order, 0, in_w)
    w_shards = [wi_arrs[0]]
    for j in range(3):
        w_shards[0], land = exchange_wait("gather_wi_wait%d" % j, plan_shard_ici(j), wi_sems[j],
                                          [w_shards[0], wi_arrs[1 + j]], z)
        land, = exchange("gather_wi_pass%d" % j, plan_shard_pass, [land], 1)
        z = in_proj_part(h, land, z, order, 1 + j, in_w)
        w_shards.append(land)
    wo_all, wg_all, wp_all, conv_all = exchange_wait("gather_rest_wait", plan_gather_ici(3), rest_sems[0], rest, z)
    wo_all, wg_all, wp_all = exchange("gather_rest_pass", plan_gather_pass, [wo_all, wg_all, wp_all], 9)
    wo_full = wo_all.reshape(d, d)
    wg_full = wg_all.reshape(d, d)
    conv_full = conv_all.transpose(1, 0, 2).reshape(SUBLANES, cw)
    qs, ks, vb = qk_prep_fwd(z, tabs, qg, kg, segb, aw)
    attn, mix = attn_fwd(qs, ks, vb, z, attn_sinks, aw, d, ga_off)
    mix = conv_fwd(z, conv_full, mix, aw, cw, b_off)
    sq = lambda shape: dict(
        a_spec=pl.BlockSpec((tm, tk), lambda i, j, k: (i, k)),
        o_spec=pl.BlockSpec((tm, tn), lambda i, j, k: (i, j)),
        out_shape=jax.ShapeDtypeStruct(shape, F32))
    x1 = matmul(mix, wo_full, grid=(s // tm, d // tn, d // tk),
                b_spec=pl.BlockSpec((tk, tn), lambda i, j, k: (k, j)), dims=NN, name="mm_x1",
                res=x2, res_spec=pl.BlockSpec((tm, tn), lambda i, j, k: (i, j)), **sq((s, d)))
    hg = rms_fwd(x1, ple_gate_norm_gain, "rms_fwd_x1")
    gl = matmul(hg, wg_full, grid=(s // tm, d // tn, d // tk),
                b_spec=pl.BlockSpec((tk, tn), lambda i, j, k: (k, j)), dims=NN, name="mm_gl", **sq((s, d)))
    pq = d // N_CHIPS
    pe = matmul(p2, wp_all, grid=(s // tm, N_CHIPS, 1),
                a_spec=pl.BlockSpec((tm, ple), lambda i, j, k: (i, 0)),
                b_spec=pl.BlockSpec((None, ple, pq), lambda i, j, k: (j, 0, 0)),
                o_spec=pl.BlockSpec((tm, pq), lambda i, j, k: (i, j)),
                out_shape=jax.ShapeDtypeStruct((s, d), F32), dims=NN, name="mm_pe")

    d_gl, d_pe, dy, acc_head = head_fwd_bwd(x1, gl, pe, tgt, b_ple_gate, ple_norm_gain)
    d_hg = matmul(d_gl, wg_full, grid=(s // tm, d // tn, d // tk),
                  b_spec=pl.BlockSpec((tn, tk), lambda i, j, k: (j, k)), dims=NT, name="mm_d_hg", **sq((s, d)))
    wgrad = lambda a_cols, shape3, o_spec, b_cols, name, a, b, grid: matmul(
        a, b, grid=grid,
        a_spec=pl.BlockSpec((ts, a_cols), lambda i, j, k: (k, i)),
        b_spec=pl.BlockSpec((ts, b_cols), lambda i, j, k: (k, j)),
        o_spec=o_spec, out_shape=jax.ShapeDtypeStruct(shape3, BF16), dims=TN, name=name)
    g_wg = wgrad(tn, (d, d), pl.BlockSpec((tn, tn), lambda i, j, k: (i, j)), tn, "mm_g_wg", hg, d_gl,
                 (d // tn, d // tn, s // ts))
    g_wp = wgrad(ple, (N_CHIPS, ple, pq), pl.BlockSpec((None, ple, pq), lambda i, j, k: (j, 0, 0)), pq,
                 "mm_g_wp", p2, d_pe, (1, N_CHIPS, s // ts))
    d_x1, d_x1b, acc_g = rms_bwd_add(d_hg, x1, dy, ple_gate_norm_gain, "rms_bwd_x1", True)
    d_mix = matmul(d_x1b, wo_full, grid=(s // tm, d // tn, d // tk),
                   b_spec=pl.BlockSpec((tn, tk), lambda i, j, k: (j, k)), dims=NT, name="mm_d_mix", **sq((s, d)))
    g_wo = wgrad(tn, (d, d), pl.BlockSpec((tn, tn), lambda i, j, k: (i, j)), tn, "mm_g_wo", mix, d_x1b,
                 (d // tn, d // tn, s // ts))
    def reduce_start(tag, grads):
        n = len(grads)
        lands = [lax.empty((N_CHIPS, a.shape[1] // 2, a.shape[2]), BF16) for a in grads]
        swapped = exchange("swap_" + tag, plan_swap, list(grads) + lands, n)
        parts = [add_sibling(g, o, core, "add_sibling_%s%d" % (tag, i))
                 for i, (g, o) in enumerate(zip(swapped[:n], swapped[n:]))]
        return exchange_start("scatter_%s_start" % tag, [(plan_scatter, list(range(2 * n)), 3 * n)],
                              parts + [lax.empty(a.shape, BF16) for a in parts])

    def reduce_finish(tag, handle, after):
        sems, arrays, _ = handle
        n = len(arrays) // 2
        got = exchange_wait("scatter_%s_wait" % tag, plan_scatter, sems[0], arrays, after)
        halves = [sum_chips(pt, o, place, "sum_chips_%s%d" % (tag, i))
                  for i, (pt, o) in enumerate(zip(got[:n], got[n:]))]
        return exchange("join_" + tag, plan_join, halves, n)

    early = reduce_start("early", [g_wo.reshape(N_CHIPS, dq, d), g_wg.reshape(N_CHIPS, dq, d), g_wp])
    d_o, dz = gate_bwd(d_mix, attn, z, aw, ga_off, after=(early[-1],))
    dqs, dks, dvs, acc_sink = attn_bwd(qs, ks, vb, d_o, attn_sinks, aw)
    dz, acc_qk = qk_prep_bwd(z, dqs, dks, dvs, tabs, qg, kg, segb, dz, aw)
    dz, acc_conv = conv_bwd(z, d_mix, conv_full, dz, aw, cw, b_off)
    full_wo, full_wg, full_wp = reduce_finish("early", early, dz)
    g_wi = wgrad(tn, (N_CHIPS, d, sh), pl.BlockSpec((None, tn, sh), lambda i, j, k: (j, i, 0)), sh,
                 "mm_g_wi", h, dz, (d // tn, N_CHIPS, s // ts))
    late = reduce_start("late", [g_wi])
    d_h = in_proj_bwd(dz, w_shards, order, d, after=(late[-1],))
    grad_x, acc_x = rms_bwd_add(d_h, x2, d_x1, norm_gain, "rms_bwd_x", False)
    full_wi, = reduce_finish("late", late, grad_x)
    big = {}
    for nm, g, w, m, v in (("w_in", full_wi, w_in, m_w_in, v_w_in), ("w_out", full_wo, w_out, m_w_out, v_w_out),
                           ("w_ple_gate", full_wg, w_ple_gate, m_w_ple_gate, v_w_ple_gate),
                           ("w_ple_proj", full_wp, w_ple_proj, m_w_ple_proj, v_w_ple_proj)):
        big[nm] = [o[None] for o in adamw(g, w[0], m[0], v[0], "adamw_" + nm)]

    wsm = max(d, cw)
    misc = jnp.concatenate([acc_qk[0:1, :HEAD_DIM], acc_qk[1:2, :HEAD_DIM], acc_sink[0:1, :nq]], axis=1)
    small = jnp.concatenate([
        _pad_cols(acc_x[0:1], wsm), _pad_cols(acc_g[0:1], wsm), _pad_cols(acc_head[0:1], wsm),
        _pad_cols(acc_head[1:2], wsm), _pad_cols(misc, wsm), _pad_cols(acc_conv[0:3], wsm)], axis=0)
    both = jnp.concatenate([small, _pad_rows(_pad_cols(acc_head[2:3], wsm), SUBLANES)], axis=0)
    tot = sum_devices(allgather_small(both))
    loss = tot[SUBLANES, 0]

    def pack(vals):
        ng, pg, bg, eg, qgv, kgv, sk, cv = vals
        misc_v = jnp.concatenate([qgv, kgv, sk], axis=1)
        return jnp.concatenate([_pad_cols(ng, wsm), _pad_cols(pg, wsm), _pad_cols(bg, wsm), _pad_cols(eg, wsm),
                                _pad_cols(misc_v, wsm), _pad_cols(cv[0], wsm)], axis=0)

    g_small = jnp.concatenate(
        [tot[0:5], _pad_cols(lax.dynamic_slice(tot[5:8], (0, chip * cq), (3, cq)), wsm)], axis=0)
    w_small = pack((norm_gain, ple_gate_norm_gain, b_ple_gate, ple_norm_gain, q_norm_gain, k_norm_gain,
                    attn_sinks, conv_w))
    m_small = pack((m_norm_gain, m_ple_gate_norm_gain, m_b_ple_gate, m_ple_norm_gain, m_q_norm_gain,
                    m_k_norm_gain, m_attn_sinks, m_conv_w))
    v_small = pack((v_norm_gain, v_ple_gate_norm_gain, v_b_ple_gate, v_ple_norm_gain, v_q_norm_gain,
                    v_k_norm_gain, v_attn_sinks, v_conv_w))
    sm = adamw(g_small, w_small, m_small, v_small, "adamw_small")

    def unpack(a):
        return {"norm_gain": a[0:1, :d], "ple_gate_norm_gain": a[1:2, :d], "b_ple_gate": a[2:3, :d],
                "ple_norm_gain": a[3:4, :d], "q_norm_gain": a[4:5, :HEAD_DIM],
                "k_norm_gain": a[4:5, HEAD_DIM:2 * HEAD_DIM],
                "attn_sinks": a[4:5, 2 * HEAD_DIM:2 * HEAD_DIM + nq], "conv_w": a[5:8, :cq][None]}

    order = ("norm_gain", "w_in", "q_norm_gain", "k_norm_gain", "attn_sinks", "conv_w", "w_out",
             "ple_gate_norm_gain", "w_ple_gate", "b_ple_gate", "w_ple_proj", "ple_norm_gain")
    outs = [loss, grad_x[None]]
    for kind in range(4):
        table = unpack(sm[kind])
        for nm in order:
            outs.append(big[nm][kind] if nm in big else table[nm])
    return tuple(outs)
```

```python
import functools

import jax
import jax.numpy as jnp
from jax import lax
from jax.experimental import pallas as pl
from jax.experimental.pallas import tpu as pltpu

F32 = jnp.float32
BF16 = jnp.bfloat16
MESH = pl.DeviceIdType.MESH

HEAD_DIM = 64
N_KV_HEADS = 4
KV_WIDTH = N_KV_HEADS * HEAD_DIM
BLOCK = 128
ROT_DIM = 16
ROPE_THETA = 500000.0
EPS = 1e-6
NEG_INF = -1e30
N_CHIPS = 4
LANES = 128
SUBLANES = 8
COLT = 512
SEG = 256
VMEM_LIMIT = 48 * 1024 * 1024

ADAM_LR = 0.001
ADAM_B1 = 0.9
ADAM_B2 = 0.999
ADAM_EPS = 1e-08
ADAM_WD = 0.01
ADAM_STEP = 10

NN = (((1,), (0,)), ((), ()))
NT = (((1,), (1,)), ((), ()))
TN = (((0,), (0,)), ((), ()))


def _call(body, **kw):
    return pl.pallas_call(body, **kw)


def _call_after(body, after, **kw):
    n_in, n_after = len(kw["in_specs"]), len(after)
    kw["in_specs"] = list(kw["in_specs"]) + [pl.BlockSpec(memory_space=pl.ANY)] * n_after

    def body_after(*refs):
        body(*refs[:n_in], *refs[n_in + n_after:])

    call = _call(body_after, **kw)
    return lambda *args: call(*args, *after)


def _params(sem):
    return pltpu.CompilerParams(dimension_semantics=sem, vmem_limit_bytes=VMEM_LIMIT)


def _tile(n, pref):
    return pref if n % pref == 0 else n


def _sigmoid(v):
    return 1.0 / (1.0 + jnp.exp(-v))


def _hbm():
    return pl.BlockSpec(memory_space=pl.ANY)


def cast_bf16(a, name):
    r, c = a.shape
    tr = _tile(r, 256)

    def body(a_ref, o_ref):
        o_ref[...] = a_ref[...].astype(BF16)

    return _call(body, name=name, grid=(r // tr,),
                 in_specs=[pl.BlockSpec((tr, c), lambda i: (i, 0))],
                 out_specs=pl.BlockSpec((tr, c), lambda i: (i, 0)),
                 out_shape=jax.ShapeDtypeStruct((r, c), BF16),
                 compiler_params=_params(("parallel",)))(a)


def cast_into_slot(a, chip, name):
    r, c = a.shape
    tr = _tile(r, 256)

    def body(chip_ref, a_ref, o_ref):
        o_ref[...] = a_ref[...].astype(BF16)

    grid_spec = pltpu.PrefetchScalarGridSpec(
        num_scalar_prefetch=1, grid=(r // tr,),
        in_specs=[pl.BlockSpec((tr, c), lambda i, chip_ref: (i, 0))],
        out_specs=pl.BlockSpec((None, tr, c), lambda i, chip_ref: (chip_ref[0], i, 0)))
    return _call(body, name=name, grid_spec=grid_spec,
                 out_shape=jax.ShapeDtypeStruct((N_CHIPS, r, c), BF16),
                 compiler_params=_params(("parallel",)))(chip, a)


def rms_fwd(x, gain, name, after=()):
    s, d = x.shape
    tm = _tile(s, 256)

    def body(x_ref, g_ref, o_ref):
        xf = x_ref[...]
        r = lax.rsqrt(jnp.mean(xf * xf, axis=-1, keepdims=True) + EPS)
        o_ref[...] = ((xf * r) * g_ref[...]).astype(BF16)

    return _call_after(body, after, name=name, grid=(s // tm,),
                       in_specs=[pl.BlockSpec((tm, d), lambda i: (i, 0)),
                                 pl.BlockSpec((1, d), lambda i: (0, 0))],
                       out_specs=pl.BlockSpec((tm, d), lambda i: (i, 0)),
                       out_shape=jax.ShapeDtypeStruct((s, d), BF16),
                       compiler_params=_params(("parallel",)))(x, gain)


def rms_bwd_add(dyn, xin, add, gain, name, want_bf16):
    s, d = xin.shape
    tm = _tile(s, 256)

    def body(dy_ref, x_ref, a_ref, g_ref, *outs):
        i = pl.program_id(0)
        dx_ref, acc_ref = outs[0], outs[-1]
        xf = x_ref[...]
        r = lax.rsqrt(jnp.mean(xf * xf, axis=-1, keepdims=True) + EPS)
        xhat = xf * r
        dyv = dy_ref[...]
        gd = dyv * g_ref[...]
        dx = a_ref[...] + r * (gd - xhat * jnp.mean(xhat * gd, axis=-1, keepdims=True))
        dx_ref[...] = dx
        if want_bf16:
            outs[1][...] = dx.astype(BF16)

        @pl.when(i == 0)
        def _():
            acc_ref[...] = jnp.zeros_like(acc_ref)

        acc_ref[0:1, :] += jnp.sum(dyv * xhat, axis=0, keepdims=True)

    row = pl.BlockSpec((tm, d), lambda i: (i, 0))
    out_specs = [row] + ([row] if want_bf16 else []) + [pl.BlockSpec((SUBLANES, d), lambda i: (0, 0))]
    out_shape = ([jax.ShapeDtypeStruct((s, d), F32)]
                 + ([jax.ShapeDtypeStruct((s, d), BF16)] if want_bf16 else [])
                 + [jax.ShapeDtypeStruct((SUBLANES, d), F32)])
    return _call(body, name=name, grid=(s // tm,),
                 in_specs=[row, row, row, pl.BlockSpec((1, d), lambda i: (0, 0))],
                 out_specs=out_specs, out_shape=out_shape,
                 compiler_params=_params(("arbitrary",)))(dyn, xin, add, gain)


def head_fwd_bwd(x1, gl, pe, tgt, bias, ple_gain):
    s, d = x1.shape
    tm = _tile(s, 128)

    def body(x1_ref, gl_ref, pe_ref, t_ref, b_ref, g_ref, dgl_ref, dpe_ref, dy_ref, acc_ref):
        i = pl.program_id(0)
        gate = _sigmoid(gl_ref[...] + b_ref[...])
        pev = pe_ref[...]
        r = lax.rsqrt(jnp.mean(pev * pev, axis=-1, keepdims=True) + EPS)
        pehat = pev * r
        gain = g_ref[...]
        e = pehat * gain
        diff = (x1_ref[...] + gate * e) - t_ref[...]
        dy = diff * (1.0 / d)
        dy_ref[...] = dy
        d_gl = (dy * e) * (gate * (1.0 - gate))
        dgl_ref[...] = d_gl.astype(BF16)
        d_e = dy * gate
        gd = d_e * gain
        d_pe = r * (gd - pehat * jnp.mean(pehat * gd, axis=-1, keepdims=True))
        dpe_ref[...] = d_pe.astype(BF16)

        @pl.when(i == 0)
        def _():
            acc_ref[...] = jnp.zeros_like(acc_ref)

        acc_ref[0:1, :] += jnp.sum(d_gl, axis=0, keepdims=True)
        acc_ref[1:2, :] += jnp.sum(d_e * pehat, axis=0, keepdims=True)
        acc_ref[2:3, :] += jnp.sum(diff * diff, axis=0, keepdims=True) * (0.5 / d)

    row = pl.BlockSpec((tm, d), lambda i: (i, 0))
    vec = pl.BlockSpec((1, d), lambda i: (0, 0))
    return _call(body, name="head_fwd_bwd", grid=(s // tm,),
                 in_specs=[row, row, row, row, vec, vec],
                 out_specs=[row, row, row, pl.BlockSpec((SUBLANES, d), lambda i: (0, 0))],
                 out_shape=[jax.ShapeDtypeStruct((s, d), BF16), jax.ShapeDtypeStruct((s, d), BF16),
                            jax.ShapeDtypeStruct((s, d), F32), jax.ShapeDtypeStruct((SUBLANES, d), F32)],
                 compiler_params=_params(("arbitrary",)))(x1, gl, pe, tgt, bias, ple_gain)


def adamw(g, w, m, v, name):
    r, c = g.shape
    tr = _tile(r, 128)

    def body(g_ref, w_ref, m_ref, v_ref, go_ref, d_ref, mo_ref, vo_ref):
        gv = g_ref[...]
        mn = ADAM_B1 * m_ref[...] + (1.0 - ADAM_B1) * gv
        vn = ADAM_B2 * v_ref[...] + (1.0 - ADAM_B2) * (gv * gv)
        m_hat = mn / (1.0 - ADAM_B1 ** ADAM_STEP)
        v_hat = vn / (1.0 - ADAM_B2 ** ADAM_STEP)
        go_ref[...] = gv
        d_ref[...] = -ADAM_LR * (m_hat / (jnp.sqrt(v_hat) + ADAM_EPS) + ADAM_WD * w_ref[...])
        mo_ref[...] = mn
        vo_ref[...] = vn

    blk = pl.BlockSpec((tr, c), lambda i: (i, 0))
    shp = jax.ShapeDtypeStruct((r, c), F32)
    return _call(body, name=name, grid=(r // tr,), in_specs=[blk] * 4, out_specs=[blk] * 4,
                 out_shape=[shp] * 4, compiler_params=_params(("parallel",)))(g, w, m, v)


def matmul(a, b, *, grid, a_spec, b_spec, o_spec, out_shape, dims, name, res=None, res_spec=None, after=()):
    nk = grid[2]
    acc_shape = tuple(d for d in o_spec.block_shape if d is not None)

    def body(*refs):
        a_ref, b_ref = refs[:2]
        r_ref = refs[2] if res is not None else None
        o_ref = refs[3] if res is not None else refs[2]
        part = lax.dot_general(a_ref[...].astype(BF16), b_ref[...].astype(BF16), dims, preferred_element_type=F32)

        def finish(out):
            if res is not None:
                out = r_ref[...] + out
            o_ref[...] = out.astype(o_ref.dtype)

        if nk == 1:
            finish(part)
            return
        acc_ref = refs[-1]
        k = pl.program_id(2)

        @pl.when(k == 0)
        def _():
            acc_ref[...] = part

        @pl.when((k > 0) & (k < nk - 1))
        def _():
            acc_ref[...] += part

        @pl.when(k == nk - 1)
        def _():
            finish(acc_ref[...] + part)

    in_specs = [a_spec, b_spec] + ([res_spec] if res is not None else [])
    args = (a, b) + ((res,) if res is not None else ())
    return _call_after(body, after, name=name, grid=grid, in_specs=in_specs, out_specs=o_spec,
                       out_shape=out_shape, scratch_shapes=[pltpu.VMEM(acc_shape, F32)] if nk > 1 else [],
                       compiler_params=_params(("parallel", "parallel", "arbitrary")))(*args)


def in_proj_part(h, w, z_prev, order, q, in_w):
    s, d = h.shape
    sh = w.shape[1]
    tm = _tile(s, 1024)

    def body(order_ref, h_ref, w_ref, *rest):
        rest[-1][...] = jnp.dot(h_ref[...], w_ref[...], preferred_element_type=F32)

    in_specs = [pl.BlockSpec((tm, d), lambda i, order_ref: (i, 0)),
                pl.BlockSpec((d, sh), lambda i, order_ref: (0, 0))]
    args = [order, h, w]
    if z_prev is not None:
        in_specs.append(_hbm())
        args.append(z_prev)
    grid_spec = pltpu.PrefetchScalarGridSpec(
        num_scalar_prefetch=1, grid=(s // tm,), in_specs=in_specs,
        out_specs=pl.BlockSpec((tm, sh), lambda i, order_ref: (i, order_ref[q])))
    return _call(body, name="mm_z%d" % q, grid_spec=grid_spec,
                 out_shape=jax.ShapeDtypeStruct((s, in_w), F32),
                 input_output_aliases={3: 0} if z_prev is not None else {},
                 compiler_params=_params(("parallel",)))(*args)


def in_proj_bwd(dz, ws, order, d, after):
    s = dz.shape[0]
    sh = ws[0].shape[1]
    tm, tn = _tile(s, 512), _tile(d, 512)
    nq, n_after = len(ws), len(after)

    def body(order_ref, *refs):
        a_refs, b_refs, o_ref = refs[:nq], refs[nq:2 * nq], refs[2 * nq + n_after]
        acc = lax.dot_general(a_refs[0][...], b_refs[0][...], NT, preferred_element_type=F32)
        for q in range(1, nq):
            acc += lax.dot_general(a_refs[q][...], b_refs[q][...], NT, preferred_element_type=F32)
        o_ref[...] = acc

    a_spec = lambda q: pl.BlockSpec((tm, sh), functools.partial(lambda i, j, order_ref, q: (i, order_ref[q]), q=q))
    grid_spec = pltpu.PrefetchScalarGridSpec(
        num_scalar_prefetch=1, grid=(s // tm, d // tn),
        in_specs=[a_spec(q) for q in range(nq)]
        + [pl.BlockSpec((tn, sh), lambda i, j, order_ref: (j, 0))] * nq + [_hbm()] * n_after,
        out_specs=pl.BlockSpec((tm, tn), lambda i, j, order_ref: (i, j)))
    return _call(body, name="mm_d_h", grid_spec=grid_spec, out_shape=jax.ShapeDtypeStruct((s, d), F32),
                 compiler_params=_params(("parallel", "parallel")))(order, *([dz] * nq), *ws, *after)


def _seg_mean(sq, segb):
    parts = []
    for cgrp in range(sq.shape[1] // SEG):
        blk = sq[:, cgrp * SEG:(cgrp + 1) * SEG]
        hi = blk.astype(BF16)
        r1 = blk - hi.astype(F32)
        mid = r1.astype(BF16)
        lo = (r1 - mid.astype(F32)).astype(BF16)
        acc = jnp.dot(hi, segb, preferred_element_type=F32)
        acc += jnp.dot(mid, segb, preferred_element_type=F32)
        acc += jnp.dot(lo, segb, preferred_element_type=F32)
        parts.append(acc)
    out = parts[0] if len(parts) == 1 else jnp.concatenate(parts, axis=1)
    return out * (1.0 / HEAD_DIM)


def _rope(v, cos, sa, sb):
    parts = []
    for cgrp in range(v.shape[1] // LANES):
        blk = v[:, cgrp * LANES:(cgrp + 1) * LANES]
        parts.append(blk * cos + pltpu.roll(blk, LANES - 8, 1) * sa + pltpu.roll(blk, 8, 1) * sb)
    return parts[0] if len(parts) == 1 else jnp.concatenate(parts, axis=1)


def _rope_t(dv, cos, sa, sb):
    parts = []
    for cgrp in range(dv.shape[1] // LANES):
        blk = dv[:, cgrp * LANES:(cgrp + 1) * LANES]
        parts.append(blk * cos + pltpu.roll(blk * sa, 8, 1) + pltpu.roll(blk * sb, LANES - 8, 1))
    return parts[0] if len(parts) == 1 else jnp.concatenate(parts, axis=1)


def _tile_lanes(vec, width):
    reps = width // LANES
    return vec if reps == 1 else jnp.tile(vec, (1, reps))


def qk_prep_fwd(z, tabs, qg, kg, segb, aw):
    s = z.shape[0]
    tm = _tile(s, 256)
    wq = aw + 2 * KV_WIDTH

    def body(z_ref, cos_ref, sa_ref, sb_ref, qg_ref, kg_ref, seg_ref, qs_ref, ks_ref, vb_ref):
        zz = z_ref[...]
        q, k, v = zz[:, :aw], zz[:, aw:aw + KV_WIDTH], zz[:, aw + KV_WIDTH:]
        cos, sa, sb, segm = cos_ref[...], sa_ref[...], sb_ref[...], seg_ref[...]
        rq = lax.rsqrt(_seg_mean(q * q, segm) + EPS)
        qn = (q * rq) * _tile_lanes(qg_ref[...], aw)
        qs_ref[...] = (_rope(qn, cos, sa, sb) * (HEAD_DIM ** -0.5)).astype(BF16)
        rk = lax.rsqrt(_seg_mean(k * k, segm) + EPS)
        kn = (k * rk) * _tile_lanes(kg_ref[...], KV_WIDTH)
        ks_ref[...] = _rope(kn, cos, sa, sb).astype(BF16)
        vb_ref[...] = v.astype(BF16)

    tab = pl.BlockSpec((tm, LANES), lambda i: (i, 0))
    vec = pl.BlockSpec((1, LANES), lambda i: (0, 0))
    return _call(body, name="qk_prep_fwd", grid=(s // tm,),
                 in_specs=[pl.BlockSpec((tm, wq), lambda i: (i, 0)), tab, tab, tab, vec, vec,
                           pl.BlockSpec((SEG, SEG), lambda i: (0, 0))],
                 out_specs=[pl.BlockSpec((tm, aw), lambda i: (i, 0)),
                            pl.BlockSpec((tm, KV_WIDTH), lambda i: (i, 0)),
                            pl.BlockSpec((tm, KV_WIDTH), lambda i: (i, 0))],
                 out_shape=[jax.ShapeDtypeStruct((s, aw), BF16), jax.ShapeDtypeStruct((s, KV_WIDTH), BF16),
                            jax.ShapeDtypeStruct((s, KV_WIDTH), BF16)],
                 compiler_params=_params(("parallel",)))(z, *tabs, qg, kg, segb)


def qk_prep_bwd(z, dqs, dks, dvs, tabs, qg, kg, segb, dz, aw):
    s = z.shape[0]
    tm = _tile(s, 256)
    wq = aw + 2 * KV_WIDTH

    def body(z_ref, dq_ref, dk_ref, dv_ref, cos_ref, sa_ref, sb_ref, qg_ref, kg_ref, seg_ref, dz_in,
             dz_ref, acc_ref):
        i = pl.program_id(0)
        zz = z_ref[...]
        q, k = zz[:, :aw], zz[:, aw:aw + KV_WIDTH]
        cos, sa, sb, segm = cos_ref[...], sa_ref[...], sb_ref[...], seg_ref[...]

        def one(xv, dout, gvec, width):
            g = _tile_lanes(gvec, width)
            r = lax.rsqrt(_seg_mean(xv * xv, segm) + EPS)
            xhat = xv * r
            dn = _rope_t(dout, cos, sa, sb)
            gd = dn * g
            dx = r * (gd - xhat * _seg_mean(xhat * gd, segm))
            contrib = jnp.sum(dn * xhat, axis=0, keepdims=True)
            folded = contrib[:, :LANES]
            for cgrp in range(1, width // LANES):
                folded = folded + contrib[:, cgrp * LANES:(cgrp + 1) * LANES]
            return dx, folded + pltpu.roll(folded, HEAD_DIM, 1)

        dq, gq = one(q, dq_ref[...] * (HEAD_DIM ** -0.5), qg_ref[...], aw)
        dk, gk = one(k, dk_ref[...], kg_ref[...], KV_WIDTH)
        dz_ref[:, :aw] = dq.astype(BF16)
        dz_ref[:, aw:aw + KV_WIDTH] = dk.astype(BF16)
        dz_ref[:, aw + KV_WIDTH:] = dv_ref[...].astype(BF16)

        @pl.when(i == 0)
        def _():
            acc_ref[...] = jnp.zeros_like(acc_ref)

        acc_ref[0:1, :] += gq
        acc_ref[1:2, :] += gk

    tab = pl.BlockSpec((tm, LANES), lambda i: (i, 0))
    vec = pl.BlockSpec((1, LANES), lambda i: (0, 0))
    kvb = pl.BlockSpec((tm, KV_WIDTH), lambda i: (i, 0))
    return _call(body, name="qk_prep_bwd", grid=(s // tm,),
                 in_specs=[pl.BlockSpec((tm, wq), lambda i: (i, 0)),
                           pl.BlockSpec((tm, aw), lambda i: (i, 0)), kvb, kvb, tab, tab, tab, vec, vec,
                           pl.BlockSpec((SEG, SEG), lambda i: (0, 0)), _hbm()],
                 out_specs=[pl.BlockSpec((tm, wq), lambda i: (i, 0)),
                            pl.BlockSpec((SUBLANES, LANES), lambda i: (0, 0))],
                 out_shape=[jax.ShapeDtypeStruct(dz.shape, BF16), jax.ShapeDtypeStruct((SUBLANES, LANES), F32)],
                 input_output_aliases={10: 0},
                 compiler_params=_params(("arbitrary",)))(z, dqs, dks, dvs, *tabs, qg, kg, segb, dz)


def _placed(band, lane_idx):
    out = {}
    for kh in range(N_KV_HEADS):
        grp = band[:, (kh // 2) * LANES:(kh // 2 + 1) * LANES]
        for half in (0, 1):
            t = grp if half == kh % 2 else pltpu.roll(grp, HEAD_DIM, 1)
            keep = (lane_idx >= half * HEAD_DIM) & (lane_idx < (half + 1) * HEAD_DIM)
            out[kh, half] = jnp.where(keep, t, jnp.zeros_like(t))
    return out


def _softmax_with_sink(sc, valid, sink):
    sc = jnp.where(valid, sc, NEG_INF)
    m = jnp.maximum(jnp.max(sc, axis=-1, keepdims=True), sink)
    e = jnp.exp(sc - m)
    es = jnp.exp(sink - m)
    den = jnp.sum(e, axis=-1, keepdims=True) + es
    return e / den, es / den


def _stack_halves(placed, kh):
    return jnp.concatenate([placed[kh, 0], placed[kh, 1]], axis=0)


def _valid_mask(n):
    r_i = lax.broadcasted_iota(jnp.int32, (BLOCK, 2 * BLOCK), 0)
    j_i = lax.broadcasted_iota(jnp.int32, (BLOCK, 2 * BLOCK), 1)
    return (j_i > r_i) & (j_i <= r_i + BLOCK) & ((n > 0) | (j_i >= BLOCK))


def attn_fwd(qs, ks, vb, z, sinks, aw, d, ga_off):
    s = qs.shape[0]
    nb = s // BLOCK
    nq = aw // HEAD_DIM
    grp_sz = nq // N_KV_HEADS
    n_ga = aw // COLT

    def body(q_ref, ko_ref, kp_ref, vo_ref, vp_ref, *rest):
        ga_refs = rest[:n_ga]
        sink_ref, attn_ref, mix_ref = rest[n_ga:]
        n = pl.program_id(0)
        lane_idx = lax.broadcasted_iota(jnp.int32, (2 * BLOCK, LANES), 1)
        kpl = _placed(jnp.concatenate([kp_ref[...], ko_ref[...]], axis=0), lane_idx)
        vpl = _placed(jnp.concatenate([vp_ref[...], vo_ref[...]], axis=0), lane_idx)
        valid = _valid_mask(n)
        groups = range(nq // 2)
        kcat = [_stack_halves(kpl, kh) for kh in range(N_KV_HEADS)]
        vcat = [_stack_halves(vpl, kh) for kh in range(N_KV_HEADS)]
        scs = [lax.dot_general(q_ref[:, c * LANES:(c + 1) * LANES], kcat[2 * c // grp_sz], NT,
                               preferred_element_type=F32) for c in groups]
        probs = [jnp.concatenate(
            [_softmax_with_sink(scs[c][:, half * 2 * BLOCK:(half + 1) * 2 * BLOCK], valid,
                                sink_ref[0, 2 * c + half])[0].astype(BF16) for half in (0, 1)], axis=1)
                 for c in groups]
        for c in groups:
            acc = jnp.dot(probs[c], vcat[2 * c // grp_sz], preferred_element_type=F32)
            attn_ref[:, c * LANES:(c + 1) * LANES] = acc
            col = c * LANES
            ga = ga_refs[col // COLT][:, col % COLT:col % COLT + LANES]
            mix_ref[:, c * LANES:(c + 1) * LANES] = (acc * (ga * _sigmoid(ga))).astype(BF16)

    own = lambda n: (n, 0)
    prev = lambda n: (jnp.maximum(n - 1, 0), 0)
    kvs = lambda imap: pl.BlockSpec((BLOCK, KV_WIDTH), imap)
    ga_specs = [pl.BlockSpec((BLOCK, COLT), functools.partial(lambda n, j: (n, ga_off + j), j=j))
                for j in range(n_ga)]
    return _call(body, name="attn_fwd", grid=(nb,),
                 in_specs=[pl.BlockSpec((BLOCK, aw), own), kvs(own), kvs(prev), kvs(own), kvs(prev)]
                 + ga_specs + [pl.BlockSpec(memory_space=pltpu.SMEM)],
                 out_specs=[pl.BlockSpec((BLOCK, aw), own), pl.BlockSpec((BLOCK, aw), own)],
                 out_shape=[jax.ShapeDtypeStruct((s, aw), F32), jax.ShapeDtypeStruct((s, d), BF16)],
                 compiler_params=_params(("parallel",)))(qs, ks, ks, vb, vb, *([z] * n_ga), sinks)


def gate_bwd(d_mix, attn, z, aw, ga_off, after=()):
    s, in_w = z.shape
    tm = _tile(s, 256)

    def body(dm_ref, at_ref, ga_ref, do_ref, dz_ref):
        ga = ga_ref[...]
        sig = _sigmoid(ga)
        dm = dm_ref[...]
        do_ref[...] = (dm * (ga * sig)).astype(BF16)
        dz_ref[...] = ((dm * at_ref[...]) * (sig * (1.0 + ga * (1.0 - sig)))).astype(BF16)

    blk = pl.BlockSpec((tm, COLT), lambda i, j: (i, j))
    gab = pl.BlockSpec((tm, COLT), lambda i, j: (i, ga_off + j))
    return _call_after(body, after, name="gate_bwd", grid=(s // tm, aw // COLT),
                       in_specs=[blk, blk, gab], out_specs=[blk, gab],
                       out_shape=[jax.ShapeDtypeStruct((s, aw), BF16), jax.ShapeDtypeStruct((s, in_w), BF16)],
                       compiler_params=_params(("parallel", "parallel")))(d_mix, attn, z)


def attn_bwd(qs, ks, vb, d_o, sinks, aw):
    s = qs.shape[0]
    nb = s // BLOCK
    nq = aw // HEAD_DIM
    grp_sz = nq // N_KV_HEADS

    def body(q_ref, ko_ref, kp_ref, vo_ref, vp_ref, do_ref, sink_ref, dq_ref, dk_ref, dv_ref, ds_ref,
             ck_ref, cv_ref):
        n = pl.program_id(0)

        @pl.when(n == 0)
        def _():
            ds_ref[...] = jnp.zeros_like(ds_ref)
            dk_ref[...] = jnp.zeros_like(dk_ref)
            dv_ref[...] = jnp.zeros_like(dv_ref)

        @pl.when(n < nb)
        def _():
            lane_idx = lax.broadcasted_iota(jnp.int32, (2 * BLOCK, LANES), 1)
            lane_row = lax.broadcasted_iota(jnp.int32, (1, LANES), 1)
            kpl = _placed(jnp.concatenate([kp_ref[...], ko_ref[...]], axis=0), lane_idx)
            vpl = _placed(jnp.concatenate([vp_ref[...], vo_ref[...]], axis=0), lane_idx)
            valid = _valid_mask(n)
            ds_row = jnp.zeros((1, LANES), F32)
            wide = 2 * BLOCK
            groups = range(nq // 2)
            per_kv = grp_sz // 2
            kcat = [_stack_halves(kpl, kh) for kh in range(N_KV_HEADS)]
            vcat = [_stack_halves(vpl, kh) for kh in range(N_KV_HEADS)]
            qg = [q_ref[:, c * LANES:(c + 1) * LANES] for c in groups]
            dog = [do_ref[:, c * LANES:(c + 1) * LANES] for c in groups]
            scs = [lax.dot_general(qg[c], kcat[c // per_kv], NT, preferred_element_type=F32) for c in groups]
            dps = [lax.dot_general(dog[c], vcat[c // per_kv], NT, preferred_element_type=F32) for c in groups]
            ds_pairs, p_pairs = [], []
            for c in groups:
                probs, dss = [], []
                for half in (0, 1):
                    h = 2 * c + half
                    cols = slice(half * wide, (half + 1) * wide)
                    p, p_sink = _softmax_with_sink(scs[c][:, cols], valid, sink_ref[0, h])
                    delta = jnp.sum(p * dps[c][:, cols], axis=-1, keepdims=True)
                    dss.append((p * (dps[c][:, cols] - delta)).astype(BF16))
                    probs.append(p.astype(BF16))
                    dsink = -jnp.sum(p_sink * delta, axis=0, keepdims=True)
                    ds_row += jnp.where(lane_row == h, dsink, 0.0)
                ds_pairs.append(jnp.concatenate(dss, axis=1))
                p_pairs.append(jnp.concatenate(probs, axis=1))
            for c in groups:
                dq_ref[:, c * LANES:(c + 1) * LANES] = jnp.dot(ds_pairs[c], kcat[c // per_kv],
                                                               preferred_element_type=F32)
            dkts = [lax.dot_general(qg[c], ds_pairs[c], TN, preferred_element_type=F32) for c in groups]
            dvts = [lax.dot_general(dog[c], p_pairs[c], TN, preferred_element_type=F32) for c in groups]
            dkt, dvt = [], []
            for kh in range(N_KV_HEADS):
                for parts, out in ((dkts, dkt), (dvts, dvt)):
                    tot = parts[kh * per_kv]
                    for c in range(kh * per_kv + 1, (kh + 1) * per_kv):
                        tot = tot + parts[c]
                    out.append(tot[:HEAD_DIM, :wide] + tot[HEAD_DIM:, wide:])
            ds_ref[0:1, :] += ds_row
            back = lambda t: jnp.concatenate(
                [jnp.concatenate(t[2 * g:2 * g + 2], axis=0).T for g in range(N_KV_HEADS // 2)], axis=1)
            dk_band, dv_band = back(dkt), back(dvt)

            @pl.when(n > 0)
            def _():
                dk_ref[...] = ck_ref[...] + dk_band[:BLOCK]
                dv_ref[...] = cv_ref[...] + dv_band[:BLOCK]

            ck_ref[...] = dk_band[BLOCK:]
            cv_ref[...] = dv_band[BLOCK:]

        @pl.when(n == nb)
        def _():
            dk_ref[...] = ck_ref[...]
            dv_ref[...] = cv_ref[...]

    own = lambda n: (jnp.minimum(n, nb - 1), 0)
    prev = lambda n: (jnp.clip(n - 1, 0, nb - 1), 0)
    done = lambda n: (jnp.maximum(n - 1, 0), 0)
    kvs = lambda imap: pl.BlockSpec((BLOCK, KV_WIDTH), imap)
    return _call(body, name="attn_bwd", grid=(nb + 1,),
                 in_specs=[pl.BlockSpec((BLOCK, aw), own), kvs(own), kvs(prev), kvs(own), kvs(prev),
                           pl.BlockSpec((BLOCK, aw), own), pl.BlockSpec(memory_space=pltpu.SMEM)],
                 out_specs=[pl.BlockSpec((BLOCK, aw), own), kvs(done), kvs(done),
                            pl.BlockSpec((SUBLANES, LANES), lambda n: (0, 0))],
                 out_shape=[jax.ShapeDtypeStruct((s, aw), F32), jax.ShapeDtypeStruct((s, KV_WIDTH), F32),
                            jax.ShapeDtypeStruct((s, KV_WIDTH), F32), jax.ShapeDtypeStruct((SUBLANES, LANES), F32)],
                 scratch_shapes=[pltpu.VMEM((BLOCK, KV_WIDTH), F32), pltpu.VMEM((BLOCK, KV_WIDTH), F32)],
                 compiler_params=_params(("arbitrary",)))(qs, ks, ks, vb, vb, d_o, sinks)


def conv_fwd(z, conv_w, mix, aw, cw, b_off):
    s = z.shape[0]
    tm = _tile(s, 256)
    nseg = cw // COLT
    hb = tm // SUBLANES

    def body(b_ref, c_ref, h_ref, g_ref, cp_ref, hp_ref, w_ref, mix_in, mix_ref):
        i = pl.program_id(0)
        u = c_ref[...] * h_ref[...]
        up = jnp.where(i > 0, cp_ref[...] * hp_ref[...], 0.0)
        ext = jnp.concatenate([up, u], axis=0)
        um1 = pltpu.roll(ext, 1, 0)[SUBLANES:]
        um2 = pltpu.roll(ext, 2, 0)[SUBLANES:]
        w = w_ref[...]
        cv = w[0:1] * um2 + w[1:2] * um1 + w[2:3] * u
        g = g_ref[...]
        mix_ref[...] = ((b_ref[...] * cv) * (g * _sigmoid(g))).astype(BF16)

    seg = lambda k: pl.BlockSpec((tm, COLT), functools.partial(lambda i, j, k: (i, b_off + k * nseg + j), k=k))
    halo = lambda k: pl.BlockSpec(
        (SUBLANES, COLT), functools.partial(lambda i, j, k: (jnp.maximum(i * hb - 1, 0), b_off + k * nseg + j), k=k))
    return _call(body, name="conv_fwd", grid=(s // tm, nseg),
                 in_specs=[seg(0), seg(1), seg(2), seg(3), halo(1), halo(2),
                           pl.BlockSpec((SUBLANES, COLT), lambda i, j: (0, j)), _hbm()],
                 out_specs=pl.BlockSpec((tm, COLT), lambda i, j: (i, aw // COLT + j)),
                 out_shape=jax.ShapeDtypeStruct(mix.shape, BF16),
                 input_output_aliases={7: 0},
                 compiler_params=_params(("parallel", "parallel")))(z, z, z, z, z, z, conv_w, mix)


def conv_bwd(z, d_mix, conv_w, dz, aw, cw, b_off):
    s = z.shape[0]
    tm = _tile(s, 256)
    nseg = cw // COLT
    hb = tm // SUBLANES
    n_row = s // tm
    last_h = s // SUBLANES - 1
    n_step = nseg * n_row

    def body(b_ref, c_ref, h_ref, g_ref, cp_ref, hp_ref, bn_ref, gn_ref, dm_ref, dmn_ref, w_ref, dz_in,
             dz_ref, acc_ref, stash_ref, sems):
        j = pl.program_id(0)
        i = pl.program_id(1)
        step = j * n_row + i
        slot = step % 2

        def copies(from_slot, at_step):
            jj, ii = at_step // n_row, at_step % n_row
            rows = pl.ds(pl.multiple_of(ii * tm, tm), tm)
            return [pltpu.make_async_copy(
                stash_ref.at[from_slot, q],
                dz_ref.at[rows, pl.ds(pl.multiple_of((b_off + q * nseg + jj) * COLT, COLT), COLT)],
                sems.at[from_slot, q]) for q in range(4)]

        @pl.when(step >= 2)
        def _():
            for cp in copies(slot, step - 2):
                cp.wait()

        cc, hh, bb, g = c_ref[...], h_ref[...], b_ref[...], g_ref[...]
        w = w_ref[...]
        u = cc * hh
        up = jnp.where(i > 0, cp_ref[...] * hp_ref[...], 0.0)
        ext = jnp.concatenate([up, u], axis=0)
        um1 = pltpu.roll(ext, 1, 0)[SUBLANES:]
        um2 = pltpu.roll(ext, 2, 0)[SUBLANES:]
        cv = w[0:1] * um2 + w[1:2] * um1 + w[2:3] * u
        sig = _sigmoid(g)
        sg = g * sig
        dm = dm_ref[...]
        d_cv = (dm * bb) * sg
        gn = gn_ref[...]
        d_cv_next = jnp.where(i < n_row - 1, (dmn_ref[...] * bn_ref[...]) * (gn * _sigmoid(gn)), 0.0)
        ext2 = jnp.concatenate([d_cv, d_cv_next], axis=0)
        dp1 = pltpu.roll(ext2, tm + SUBLANES - 1, 0)[:tm]
        dp2 = pltpu.roll(ext2, tm + SUBLANES - 2, 0)[:tm]
        d_u = w[2:3] * d_cv + w[1:2] * dp1 + w[0:1] * dp2
        stash_ref[slot, 0] = ((dm * cv) * sg).astype(BF16)
        stash_ref[slot, 1] = (d_u * hh).astype(BF16)
        stash_ref[slot, 2] = (d_u * cc).astype(BF16)
        stash_ref[slot, 3] = (((dm * bb) * cv) * (sig * (1.0 + g * (1.0 - sig)))).astype(BF16)
        for cp in copies(slot, step):
            cp.start()

        @pl.when(i == 0)
        def _():
            acc_ref[...] = jnp.zeros_like(acc_ref)

        acc_ref[0:1, :] += jnp.sum(d_cv * um2, axis=0, keepdims=True)
        acc_ref[1:2, :] += jnp.sum(d_cv * um1, axis=0, keepdims=True)
        acc_ref[2:3, :] += jnp.sum(d_cv * u, axis=0, keepdims=True)

        @pl.when(step == n_step - 1)
        def _():
            if n_step >= 2:
                for cp in copies(1 - slot, step - 1):
                    cp.wait()
            for cp in copies(slot, step):
                cp.wait()

    seg = lambda q: pl.BlockSpec((tm, COLT), functools.partial(lambda j, i, q: (i, b_off + q * nseg + j), q=q))
    halo_p = lambda q: pl.BlockSpec(
        (SUBLANES, COLT),
        functools.partial(lambda j, i, q: (jnp.maximum(i * hb - 1, 0), b_off + q * nseg + j), q=q))
    halo_n = lambda q: pl.BlockSpec(
        (SUBLANES, COLT),
        functools.partial(lambda j, i, q: (jnp.minimum((i + 1) * hb, last_h), b_off + q * nseg + j), q=q))
    return _call(body, name="conv_bwd", grid=(nseg, n_row),
                 in_specs=[seg(0), seg(1), seg(2), seg(3), halo_p(1), halo_p(2), halo_n(0), halo_n(3),
                           pl.BlockSpec((tm, COLT), lambda j, i: (i, aw // COLT + j)),
                           pl.BlockSpec((SUBLANES, COLT),
                                        lambda j, i: (jnp.minimum((i + 1) * hb, last_h), aw // COLT + j)),
                           pl.BlockSpec((SUBLANES, COLT), lambda j, i: (0, j)), _hbm()],
                 out_specs=[_hbm(), pl.BlockSpec((SUBLANES, COLT), lambda j, i: (0, j))],
                 out_shape=[jax.ShapeDtypeStruct(dz.shape, BF16), jax.ShapeDtypeStruct((SUBLANES, cw), F32)],
                 input_output_aliases={11: 0},
                 scratch_shapes=[pltpu.VMEM((2, 4, tm, COLT), BF16), pltpu.SemaphoreType.DMA((2, 4))],
                 compiler_params=_params(("arbitrary", "arbitrary")))(
                     z, z, z, z, z, z, z, z, d_mix, d_mix, conv_w, dz)


def _place():
    x, y, c = lax.axis_index("x"), lax.axis_index("y"), lax.axis_index("c")
    chips = [(1 - x, y), (x, 1 - y), (1 - x, 1 - y)]
    return x, y, c, chips


def _remote(src, dst, send_sem, recv_sem, device):
    return pltpu.make_async_remote_copy(src_ref=src, dst_ref=dst, send_sem=send_sem, recv_sem=recv_sem,
                                        device_id=device, device_id_type=MESH)


def plan_gather_ici(n_split):
    def plan(refs, send, recv):
        x, y, c, chips = _place()
        me = 2 * x + y
        mine, theirs = [], []
        for w, ref in enumerate(refs):
            for j, (cx, cy) in enumerate(chips):
                def part(slot):
                    if w >= n_split:
                        return ref.at[slot]
                    hr = ref.shape[1] // 2
                    return ref.at[slot, pl.ds(c * hr, hr)]
                k = 3 * w + j
                mine.append(_remote(part(me), part(me), send.at[k], recv.at[k], (cx, cy, c)))
                theirs.append(_remote(part(2 * cx + cy), part(2 * cx + cy), send.at[k], recv.at[k], (cx, cy, c)))
        return mine, theirs
    return plan


def plan_shard_ici(j):
    def plan(refs, send, recv):
        _, _, c, chips = _place()
        own, land = refs
        hr = own.shape[0] // 2
        rows = pl.ds(c * hr, hr)
        cp = _remote(own.at[rows], land.at[rows], send.at[0], recv.at[0], (*chips[j], c))
        return [cp], [cp]
    return plan


def plan_shard_pass(refs, send, recv):
    x, y, c, _ = _place()
    land, = refs
    hr = land.shape[0] // 2
    half = lambda core: land.at[pl.ds(core * hr, hr)]
    return ([_remote(half(c), half(c), send.at[0], recv.at[0], (x, y, 1 - c))],
            [_remote(half(1 - c), half(1 - c), send.at[0], recv.at[0], (x, y, 1 - c))])


def plan_gather_pass(refs, send, recv):
    x, y, c, chips = _place()
    mine, theirs = [], []
    for w, ref in enumerate(refs):
        hr = ref.shape[1] // 2
        for j, (cx, cy) in enumerate(chips):
            half = lambda core: ref.at[2 * cx + cy, pl.ds(core * hr, hr)]
            k = 3 * w + j
            mine.append(_remote(half(c), half(c), send.at[k], recv.at[k], (x, y, 1 - c)))
            theirs.append(_remote(half(1 - c), half(1 - c), send.at[k], recv.at[k], (x, y, 1 - c)))
    return mine, theirs


def plan_swap(refs, send, recv):
    x, y, c, _ = _place()
    nw = len(refs) // 2
    mine = []
    for w in range(nw):
        hr = refs[w].shape[1] // 2
        mine.append(_remote(refs[w].at[:, pl.ds((1 - c) * hr, hr), :], refs[nw + w], send.at[w], recv.at[w],
                            (x, y, 1 - c)))
    return mine, mine


def plan_scatter(refs, send, recv):
    x, y, c, chips = _place()
    me = 2 * x + y
    nw = len(refs) // 2
    mine, theirs = [], []
    for w in range(nw):
        for j, (cx, cy) in enumerate(chips):
            k = 3 * w + j
            mine.append(_remote(refs[w].at[2 * cx + cy], refs[nw + w].at[me], send.at[k], recv.at[k], (cx, cy, c)))
            theirs.append(_remote(refs[w].at[me], refs[nw + w].at[2 * cx + cy], send.at[k], recv.at[k], (cx, cy, c)))
    return mine, theirs


def plan_join(refs, send, recv):
    x, y, c, _ = _place()
    mine, theirs = [], []
    for w, ref in enumerate(refs):
        hr = ref.shape[0] // 2
        half = lambda core: ref.at[pl.ds(core * hr, hr)]
        mine.append(_remote(half(c), half(c), send.at[w], recv.at[w], (x, y, 1 - c)))
        theirs.append(_remote(half(1 - c), half(1 - c), send.at[w], recv.at[w], (x, y, 1 - c)))
    return mine, theirs


def exchange(name, plan, arrays, n):
    na = len(arrays)

    def body(*refs):
        mine, theirs = plan(refs[:na], refs[2 * na], refs[2 * na + 1])
        for cp in mine:
            cp.start()
        for cp in theirs:
            cp.wait_recv()
        for cp in mine:
            cp.wait_send()

    return _call(body, name=name, in_specs=[_hbm()] * na, out_specs=[_hbm()] * na,
                 out_shape=[jax.ShapeDtypeStruct(a.shape, a.dtype) for a in arrays],
                 input_output_aliases={i: i for i in range(na)},
                 scratch_shapes=[pltpu.SemaphoreType.DMA((n,)), pltpu.SemaphoreType.DMA((n,))])(*arrays)


def exchange_start(name, groups, arrays, after=()):
    na, ng = len(arrays), len(groups)

    def body(*refs):
        for g, (plan, idx, _) in enumerate(groups):
            mine, _ = plan([refs[i] for i in idx], refs[na + 2 * g], refs[na + 2 * g + 1])
            for cp in mine:
                cp.start()
        refs[-1][...] = jnp.zeros_like(refs[-1])

    hbm = pl.BlockSpec(memory_space=pltpu.HBM)
    sem = pl.BlockSpec(memory_space=pltpu.SEMAPHORE)
    sem_types = [pltpu.SemaphoreType.DMA((n,)) for _, _, n in groups for _ in (0, 1)]
    outs = _call_after(body, after, name=name, in_specs=[hbm] * na,
                       out_specs=[sem] * (2 * ng) + [hbm] * na + [pl.BlockSpec(memory_space=pltpu.VMEM)],
                       out_shape=sem_types + [pltpu.HBM(a.shape, a.dtype) for a in arrays]
                       + [jax.ShapeDtypeStruct((SUBLANES, LANES), F32)],
                       input_output_aliases={i: i + 2 * ng for i in range(na)},
                       compiler_params=pltpu.CompilerParams(
                           has_side_effects=pltpu.SideEffectType.DATAFLOW_SIDE_EFFECTING))(
                               *[pltpu.with_memory_space_constraint(a, pltpu.HBM) for a in arrays])
    sems = [(outs[2 * g], outs[2 * g + 1]) for g in range(ng)]
    return sems, list(outs[2 * ng:-1]), outs[-1]


def exchange_wait(name, plan, sems, arrays, after):
    send_sems, recv_sems = sems
    na = len(arrays)

    def body(*refs):
        mine, theirs = plan(refs[:na], refs[na], refs[na + 1])
        for cp in mine:
            cp.wait_send()
        for cp in theirs:
            cp.wait_recv()

    hbm = pl.BlockSpec(memory_space=pltpu.HBM)
    sem = pl.BlockSpec(memory_space=pltpu.SEMAPHORE)
    return _call(body, name=name, in_specs=[hbm] * na + [sem, sem, _hbm()], out_specs=[hbm] * na,
                 out_shape=[pltpu.HBM(a.shape, a.dtype) for a in arrays],
                 input_output_aliases={i: i for i in range(na)},
                 compiler_params=pltpu.CompilerParams(
                     has_side_effects=pltpu.SideEffectType.DATAFLOW_SIDE_EFFECTING))(
                         *arrays, send_sems, recv_sems, after)


def allgather_small(small):
    rows, width = small.shape

    def body(in_ref, out_ref, send, recv, loc):
        x, y, c, _ = _place()
        me = 4 * x + 2 * y + c
        local = pltpu.make_async_copy(in_ref, out_ref.at[me], loc)
        local.start()
        flips = [(fx, fy, fc) for fx in (0, 1) for fy in (0, 1) for fc in (0, 1)][1:]

        def cp(k, slot):
            fx, fy, fc = flips[k]
            return pltpu.make_async_remote_copy(
                src_ref=in_ref, dst_ref=out_ref.at[slot], send_sem=send.at[k], recv_sem=recv.at[k],
                device_id=(x ^ fx, y ^ fy, c ^ fc), device_id_type=MESH)

        for k in range(7):
            cp(k, me).start()
        for k in range(7):
            fx, fy, fc = flips[k]
            cp(k, 4 * (x ^ fx) + 2 * (y ^ fy) + (c ^ fc)).wait_recv()
        for k in range(7):
            cp(k, me).wait_send()
        local.wait()

    return _call(body, name="allgather_small",
                 in_specs=[pl.BlockSpec(memory_space=pltpu.VMEM)],
                 out_specs=pl.BlockSpec(memory_space=pltpu.VMEM),
                 out_shape=jax.ShapeDtypeStruct((8, rows, width), F32),
                 scratch_shapes=[pltpu.SemaphoreType.DMA((7,)), pltpu.SemaphoreType.DMA((7,)),
                                 pltpu.SemaphoreType.DMA])(small)


def add_sibling(grad, got, core, name):
    _, r, c = grad.shape
    hr = r // 2
    tr = _tile(hr, 128)
    nblk = hr // tr

    def body(core_ref, g_ref, o_ref, out_ref):
        out_ref[...] = (g_ref[...].astype(F32) + o_ref[...].astype(F32)).astype(BF16)

    grid_spec = pltpu.PrefetchScalarGridSpec(
        num_scalar_prefetch=1, grid=(N_CHIPS, nblk),
        in_specs=[pl.BlockSpec((None, tr, c), lambda t, i, core_ref: (t, core_ref[0] * nblk + i, 0)),
                  pl.BlockSpec((None, tr, c), lambda t, i, core_ref: (t, i, 0))],
        out_specs=pl.BlockSpec((None, tr, c), lambda t, i, core_ref: (t, i, 0)))
    return _call(body, name=name, grid_spec=grid_spec,
                 out_shape=jax.ShapeDtypeStruct((N_CHIPS, hr, c), BF16),
                 compiler_params=_params(("parallel", "parallel")))(core, grad, got)


def sum_chips(mine, owned, place, name):
    _, hr, c = mine.shape
    tr = _tile(hr, 128)
    nblk = hr // tr

    def body(place_ref, m_ref, o1_ref, o2_ref, o3_ref, out_ref):
        acc = m_ref[...].astype(F32)
        for o_ref in (o1_ref, o2_ref, o3_ref):
            acc = acc + o_ref[...].astype(F32)
        out_ref[...] = acc

    other = lambda k: pl.BlockSpec((None, tr, c), lambda i, place_ref: ((place_ref[0] + k) % N_CHIPS, i, 0))
    grid_spec = pltpu.PrefetchScalarGridSpec(
        num_scalar_prefetch=1, grid=(nblk,),
        in_specs=[other(0), other(1), other(2), other(3)],
        out_specs=pl.BlockSpec((tr, c), lambda i, place_ref: (place_ref[1] * nblk + i, 0)))
    return _call(body, name=name, grid_spec=grid_spec,
                 out_shape=jax.ShapeDtypeStruct((2 * hr, c), F32),
                 compiler_params=_params(("parallel",)))(place, mine, owned, owned, owned)


def sum_devices(gathered):
    _, rows, width = gathered.shape

    def body(g_ref, out_ref):
        acc = g_ref[0]
        for dev in range(1, 8):
            acc = acc + g_ref[dev]
        out_ref[...] = acc
        tail = acc[SUBLANES:]
        out_ref[SUBLANES:, :] = jnp.broadcast_to(jnp.sum(tail, axis=1, keepdims=True), tail.shape)

    return _call(body, name="sum_devices",
                 in_specs=[pl.BlockSpec(memory_space=pltpu.VMEM)],
                 out_specs=pl.BlockSpec(memory_space=pltpu.VMEM),
                 out_shape=jax.ShapeDtypeStruct((rows, width), F32))(gathered)


def _rope_tables(s):
    half = ROT_DIM // 2
    inv_freq = jnp.power(jnp.float32(ROPE_THETA), -jnp.arange(half, dtype=F32) * 2.0 / ROT_DIM)
    freq64 = jnp.concatenate([inv_freq, inv_freq, jnp.zeros((HEAD_DIM - ROT_DIM,), F32)])
    freq = jnp.concatenate([freq64, freq64])[None, :]
    dim = (jnp.arange(LANES) % HEAD_DIM)[None, :]
    ang = jnp.arange(s).astype(F32)[:, None] * freq
    cos, sin = jnp.cos(ang), jnp.sin(ang)
    return cos, jnp.where(dim < half, -sin, 0.0), jnp.where((dim >= half) & (dim < ROT_DIM), sin, 0.0)


def _pad_rows(a, rows):
    return jnp.pad(a, ((0, rows - a.shape[0]), (0, 0)))


def _pad_cols(a, cols):
    return jnp.pad(a, ((0, 0), (0, cols - a.shape[1])))


def kernel(x, p, norm_gain, w_in, q_norm_gain, k_norm_gain, attn_sinks, conv_w, w_out, ple_gate_norm_gain, w_ple_gate, b_ple_gate, w_ple_proj, ple_norm_gain, loss_target, m_norm_gain, m_w_in, m_q_norm_gain, m_k_norm_gain, m_attn_sinks, m_conv_w, m_w_out, m_ple_gate_norm_gain, m_w_ple_gate, m_b_ple_gate, m_w_ple_proj, m_ple_norm_gain, v_norm_gain, v_w_in, v_q_norm_gain, v_k_norm_gain, v_attn_sinks, v_conv_w, v_w_out, v_ple_gate_norm_gain, v_w_ple_gate, v_b_ple_gate, v_w_ple_proj, v_ple_norm_gain):
    x2, p2, tgt = x[0], p[0, 0], loss_target[0]
    s, d = x2.shape
    ple = p2.shape[1]
    aw = d // 2
    cw = d - aw
    nq = aw // HEAD_DIM
    sh = w_in.shape[2]
    in_w = N_CHIPS * sh
    dq = d // N_CHIPS
    cq = cw // N_CHIPS
    ga_off = (aw + 2 * KV_WIDTH) // COLT
    b_off = (2 * aw + 2 * KV_WIDTH) // COLT
    assert in_w == 2 * aw + 2 * KV_WIDTH + 4 * cw and aw % COLT == 0 and cw % COLT == 0
    assert s % BLOCK == 0 and sh % LANES == 0 and nq % (2 * N_KV_HEADS) == 0

    core = lax.axis_index("c").astype(jnp.int32).reshape(1)
    chip = 2 * lax.axis_index("x") + lax.axis_index("y")

    chip1 = chip.astype(jnp.int32).reshape(1)
    place = jnp.concatenate([chip1, core])
    w_in_own = cast_bf16(w_in[0], "cast_w_in")
    rest = [cast_into_slot(w_out[0], chip1, "cast_w_out"), cast_into_slot(w_ple_gate[0], chip1, "cast_w_pg"),
            cast_into_slot(w_ple_proj[0], chip1, "cast_w_pp"),
            lax.dynamic_update_slice(jnp.zeros((N_CHIPS, SUBLANES, cq), F32),
                                     _pad_rows(conv_w[0], SUBLANES)[None], (chip, 0, 0))]
    order = jnp.stack([chip, chip ^ 2, chip ^ 1, chip ^ 3]).astype(jnp.int32)
    wi_sems, wi_arrs, wi_token = exchange_start(
        "gather_wi_start", [(plan_shard_ici(j), [0, 1 + j], 1) for j in range(3)],
        [w_in_own] + [lax.empty((d, sh), BF16) for _ in range(3)])
    rest_sems, rest, rest_token = exchange_start(
        "gather_rest_start", [(plan_gather_ici(3), [0, 1, 2, 3], 12)], rest, after=(wi_token,))

    tm = _tile(s, 1024)
    tn = _tile(d, 1024)
    tk = _tile(d, 2048)
    ts = _tile(s, 2048)
    h = rms_fwd(x2, norm_gain, "rms_fwd_x", after=(rest_token,))
    tabs = _rope_tables(s)
    qg = jnp.tile(q_norm_gain, (1, LANES // HEAD_DIM))
    kg = jnp.tile(k_norm_gain, (1, LANES // HEAD_DIM))
    seg_i = jnp.arange(SEG) // HEAD_DIM
    segb = (seg_i[:, None] == seg_i[None, :]).astype(BF16)
    z = in_proj_part(h, wi_arrs[0], None, order, 0, in_w)
    w_shards = [wi_arrs[0]]
    for j in range(3):
        w_shards[0], land = exchange_wait("gather_wi_wait%d" % j, plan_shard_ici(j), wi_sems[j],
                                          [w_shards[0], wi_arrs[1 + j]], z)
        land, = exchange("gather_wi_pass%d" % j, plan_shard_pass, [land], 1)
        z = in_proj_part(h, land, z, order, 1 + j, in_w)
        w_shards.append(land)
    wo_all, wg_all, wp_all, conv_all = exchange_wait("gather_rest_wait", plan_gather_ici(3), rest_sems[0], rest, z)
    wo_all, wg_all, wp_all = exchange("gather_rest_pass", plan_gather_pass, [wo_all, wg_all, wp_all], 9)
    wo_full = wo_all.reshape(d, d)
    wg_full = wg_all.reshape(d, d)
    conv_full = conv_all.transpose(1, 0, 2).reshape(SUBLANES, cw)
    qs, ks, vb = qk_prep_fwd(z, tabs, qg, kg, segb, aw)
    attn, mix = attn_fwd(qs, ks, vb, z, attn_sinks, aw, d, ga_off)
    mix = conv_fwd(z, conv_full, mix, aw, cw, b_off)
    sq = lambda shape: dict(
        a_spec=pl.BlockSpec((tm, tk), lambda i, j, k: (i, k)),
        o_spec=pl.BlockSpec((tm, tn), lambda i, j, k: (i, j)),
        out_shape=jax.ShapeDtypeStruct(shape, F32))
    x1 = matmul(mix, wo_full, grid=(s // tm, d // tn, d // tk),
                b_spec=pl.BlockSpec((tk, tn), lambda i, j, k: (k, j)), dims=NN, name="mm_x1",
                res=x2, res_spec=pl.BlockSpec((tm, tn), lambda i, j, k: (i, j)), **sq((s, d)))
    hg = rms_fwd(x1, ple_gate_norm_gain, "rms_fwd_x1")
    gl = matmul(hg, wg_full, grid=(s // tm, d // tn, d // tk),
                b_spec=pl.BlockSpec((tk, tn), lambda i, j, k: (k, j)), dims=NN, name="mm_gl", **sq((s, d)))
    pq = d // N_CHIPS
    pe = matmul(p2, wp_all, grid=(s // tm, N_CHIPS, 1),
                a_spec=pl.BlockSpec((tm, ple), lambda i, j, k: (i, 0)),
                b_spec=pl.BlockSpec((None, ple, pq), lambda i, j, k: (j, 0, 0)),
                o_spec=pl.BlockSpec((tm, pq), lambda i, j, k: (i, j)),
                out_shape=jax.ShapeDtypeStruct((s, d), F32), dims=NN, name="mm_pe")

    d_gl, d_pe, dy, acc_head = head_fwd_bwd(x1, gl, pe, tgt, b_ple_gate, ple_norm_gain)
    d_hg = matmul(d_gl, wg_full, grid=(s // tm, d // tn, d // tk),
                  b_spec=pl.BlockSpec((tn, tk), lambda i, j, k: (j, k)), dims=NT, name="mm_d_hg", **sq((s, d)))
    wgrad = lambda a_cols, shape3, o_spec, b_cols, name, a, b, grid: matmul(
        a, b, grid=grid,
        a_spec=pl.BlockSpec((ts, a_cols), lambda i, j, k: (k, i)),
        b_spec=pl.BlockSpec((ts, b_cols), lambda i, j, k: (k, j)),
        o_spec=o_spec, out_shape=jax.ShapeDtypeStruct(shape3, BF16), dims=TN, name=name)
    g_wg = wgrad(tn, (d, d), pl.BlockSpec((tn, tn), lambda i, j, k: (i, j)), tn, "mm_g_wg", hg, d_gl,
                 (d // tn, d // tn, s // ts))
    g_wp = wgrad(ple, (N_CHIPS, ple, pq), pl.BlockSpec((None, ple, pq), lambda i, j, k: (j, 0, 0)), pq,
                 "mm_g_wp", p2, d_pe, (1, N_CHIPS, s // ts))
    d_x1, d_x1b, acc_g = rms_bwd_add(d_hg, x1, dy, ple_gate_norm_gain, "rms_bwd_x1", True)
    d_mix = matmul(d_x1b, wo_full, grid=(s // tm, d // tn, d // tk),
                   b_spec=pl.BlockSpec((tn, tk), lambda i, j, k: (j, k)), dims=NT, name="mm_d_mix", **sq((s, d)))
    g_wo = wgrad(tn, (d, d), pl.BlockSpec((tn, tn), lambda i, j, k: (i, j)), tn, "mm_g_wo", mix, d_x1b,
                 (d // tn, d // tn, s // ts))
    def reduce_start(tag, grads):
        n = len(grads)
        lands = [lax.empty((N_CHIPS, a.shape[1] // 2, a.shape[2]), BF16) for a in grads]
        swapped = exchange("swap_" + tag, plan_swap, list(grads) + lands, n)
        parts = [add_sibling(g, o, core, "add_sibling_%s%d" % (tag, i))
                 for i, (g, o) in enumerate(zip(swapped[:n], swapped[n:]))]
        return exchange_start("scatter_%s_start" % tag, [(plan_scatter, list(range(2 * n)), 3 * n)],
                              parts + [lax.empty(a.shape, BF16) for a in parts])

    def reduce_finish(tag, handle, after):
        sems, arrays, _ = handle
        n = len(arrays) // 2
        got = exchange_wait("scatter_%s_wait" % tag, plan_scatter, sems[0], arrays, after)
        halves = [sum_chips(pt, o, place, "sum_chips_%s%d" % (tag, i))
                  for i, (pt, o) in enumerate(zip(got[:n], got[n:]))]
        return exchange("join_" + tag, plan_join, halves, n)

    early = reduce_start("early", [g_wo.reshape(N_CHIPS, dq, d), g_wg.reshape(N_CHIPS, dq, d), g_wp])
    d_o, dz = gate_bwd(d_mix, attn, z, aw, ga_off, after=(early[-1],))
    dqs, dks, dvs, acc_sink = attn_bwd(qs, ks, vb, d_o, attn_sinks, aw)
    dz, acc_qk = qk_prep_bwd(z, dqs, dks, dvs, tabs, qg, kg, segb, dz, aw)
    dz, acc_conv = conv_bwd(z, d_mix, conv_full, dz, aw, cw, b_off)
    full_wo, full_wg, full_wp = reduce_finish("early", early, dz)
    g_wi = wgrad(tn, (N_CHIPS, d, sh), pl.BlockSpec((None, tn, sh), lambda i, j, k: (j, i, 0)), sh,
                 "mm_g_wi", h, dz, (d // tn, N_CHIPS, s // ts))
    late = reduce_start("late", [g_wi])
    d_h = in_proj_bwd(dz, w_shards, order, d, after=(late[-1],))
    grad_x, acc_x = rms_bwd_add(d_h, x2, d_x1, norm_gain, "rms_bwd_x", False)
    full_wi, = reduce_finish("late", late, grad_x)
    big = {}
    for nm, g, w, m, v in (("w_in", full_wi, w_in, m_w_in, v_w_in), ("w_out", full_wo, w_out, m_w_out, v_w_out),
                           ("w_ple_gate", full_wg, w_ple_gate, m_w_ple_gate, v_w_ple_gate),
                           ("w_ple_proj", full_wp, w_ple_proj, m_w_ple_proj, v_w_ple_proj)):
        big[nm] = [o[None] for o in adamw(g, w[0], m[0], v[0], "adamw_" + nm)]

    wsm = max(d, cw)
    misc = jnp.concatenate([acc_qk[0:1, :HEAD_DIM], acc_qk[1:2, :HEAD_DIM], acc_sink[0:1, :nq]], axis=1)
    small = jnp.concatenate([
        _pad_cols(acc_x[0:1], wsm), _pad_cols(acc_g[0:1], wsm), _pad_cols(acc_head[0:1], wsm),
        _pad_cols(acc_head[1:2], wsm), _pad_cols(misc, wsm), _pad_cols(acc_conv[0:3], wsm)], axis=0)
    both = jnp.concatenate([small, _pad_rows(_pad_cols(acc_head[2:3], wsm), SUBLANES)], axis=0)
    tot = sum_devices(allgather_small(both))
    loss = tot[SUBLANES, 0]

    def pack(vals):
        ng, pg, bg, eg, qgv, kgv, sk, cv = vals
        misc_v = jnp.concatenate([qgv, kgv, sk], axis=1)
        return jnp.concatenate([_pad_cols(ng, wsm), _pad_cols(pg, wsm), _pad_cols(bg, wsm), _pad_cols(eg, wsm),
                                _pad_cols(misc_v, wsm), _pad_cols(cv[0], wsm)], axis=0)

    g_small = jnp.concatenate(
        [tot[0:5], _pad_cols(lax.dynamic_slice(tot[5:8], (0, chip * cq), (3, cq)), wsm)], axis=0)
    w_small = pack((norm_gain, ple_gate_norm_gain, b_ple_gate, ple_norm_gain, q_norm_gain, k_norm_gain,
                    attn_sinks, conv_w))
    m_small = pack((m_norm_gain, m_ple_gate_norm_gain, m_b_ple_gate, m_ple_norm_gain, m_q_norm_gain,
                    m_k_norm_gain, m_attn_sinks, m_conv_w))
    v_small = pack((v_norm_gain, v_ple_gate_norm_gain, v_b_ple_gate, v_ple_norm_gain, v_q_norm_gain,
                    v_k_norm_gain, v_attn_sinks, v_conv_w))
    sm = adamw(g_small, w_small, m_small, v_small, "adamw_small")

    def unpack(a):
        return {"norm_gain": a[0:1, :d], "ple_gate_norm_gain": a[1:2, :d], "b_ple_gate": a[2:3, :d],
                "ple_norm_gain": a[3:4, :d], "q_norm_gain": a[4:5, :HEAD_DIM],
                "k_norm_gain": a[4:5, HEAD_DIM:2 * HEAD_DIM],
                "attn_sinks": a[4:5, 2 * HEAD_DIM:2 * HEAD_DIM + nq], "conv_w": a[5:8, :cq][None]}

    order = ("norm_gain", "w_in", "q_norm_gain", "k_norm_gain", "attn_sinks", "conv_w", "w_out",
             "ple_gate_norm_gain", "w_ple_gate", "b_ple_gate", "w_ple_proj", "ple_norm_gain")
    outs = [loss, grad_x[None]]
    for kind in range(4):
        table = unpack(sm[kind])
        for nm in order:
            outs.append(big[nm][kind] if nm in big else table[nm])
    return tuple(outs)
```

```python
import functools

import jax
import jax.numpy as jnp
from jax import lax
from jax.experimental import pallas as pl
from jax.experimental.pallas import tpu as pltpu

F32 = jnp.float32
BF16 = jnp.bfloat16
MESH = pl.DeviceIdType.MESH

HEAD_DIM = 64
N_KV_HEADS = 4
KV_WIDTH = N_KV_HEADS * HEAD_DIM
BLOCK = 128
ROT_DIM = 16
ROPE_THETA = 500000.0
EPS = 1e-6
NEG_INF = -1e30
N_CHIPS = 4
LANES = 128
SUBLANES = 8
COLT = 512
SEG = 256
VMEM_LIMIT = 48 * 1024 * 1024

ADAM_LR = 0.001
ADAM_B1 = 0.9
ADAM_B2 = 0.999
ADAM_EPS = 1e-08
ADAM_WD = 0.01
ADAM_STEP = 10

NN = (((1,), (0,)), ((), ()))
NT = (((1,), (1,)), ((), ()))
TN = (((0,), (0,)), ((), ()))


def _call(body, **kw):
    return pl.pallas_call(body, **kw)


def _call_after(body, after, **kw):
    n_in, n_after = len(kw["in_specs"]), len(after)
    kw["in_specs"] = list(kw["in_specs"]) + [pl.BlockSpec(memory_space=pl.ANY)] * n_after

    def body_after(*refs):
        body(*refs[:n_in], *refs[n_in + n_after:])

    call = _call(body_after, **kw)
    return lambda *args: call(*args, *after)


def _params(sem):
    return pltpu.CompilerParams(dimension_semantics=sem, vmem_limit_bytes=VMEM_LIMIT)


def _tile(n, pref):
    return pref if n % pref == 0 else n


def _sigmoid(v):
    return 1.0 / (1.0 + jnp.exp(-v))


def _hbm():
    return pl.BlockSpec(memory_space=pl.ANY)


def cast_bf16(a, name):
    r, c = a.shape
    tr = _tile(r, 256)

    def body(a_ref, o_ref):
        o_ref[...] = a_ref[...].astype(BF16)

    return _call(body, name=name, grid=(r // tr,),
                 in_specs=[pl.BlockSpec((tr, c), lambda i: (i, 0))],
                 out_specs=pl.BlockSpec((tr, c), lambda i: (i, 0)),
                 out_shape=jax.ShapeDtypeStruct((r, c), BF16),
                 compiler_params=_params(("parallel",)))(a)


def cast_into_slot(a, chip, name):
    r, c = a.shape
    tr = _tile(r, 256)

    def body(chip_ref, a_ref, o_ref):
        o_ref[...] = a_ref[...].astype(BF16)

    grid_spec = pltpu.PrefetchScalarGridSpec(
        num_scalar_prefetch=1, grid=(r // tr,),
        in_specs=[pl.BlockSpec((tr, c), lambda i, chip_ref: (i, 0))],
        out_specs=pl.BlockSpec((None, tr, c), lambda i, chip_ref: (chip_ref[0], i, 0)))
    return _call(body, name=name, grid_spec=grid_spec,
                 out_shape=jax.ShapeDtypeStruct((N_CHIPS, r, c), BF16),
                 compiler_params=_params(("parallel",)))(chip, a)


def rms_fwd(x, gain, name, after=()):
    s, d = x.shape
    tm = _tile(s, 256)

    def body(x_ref, g_ref, o_ref):
        xf = x_ref[...]
        r = lax.rsqrt(jnp.mean(xf * xf, axis=-1, keepdims=True) + EPS)
        o_ref[...] = ((xf * r) * g_ref[...]).astype(BF16)

    return _call_after(body, after, name=name, grid=(s // tm,),
                       in_specs=[pl.BlockSpec((tm, d), lambda i: (i, 0)),
                                 pl.BlockSpec((1, d), lambda i: (0, 0))],
                       out_specs=pl.BlockSpec((tm, d), lambda i: (i, 0)),
                       out_shape=jax.ShapeDtypeStruct((s, d), BF16),
                       compiler_params=_params(("parallel",)))(x, gain)


def rms_bwd_add(dyn, xin, add, gain, name, want_bf16):
    s, d = xin.shape
    tm = _tile(s, 256)

    def body(dy_ref, x_ref, a_ref, g_ref, *outs):
        i = pl.program_id(0)
        dx_ref, acc_ref = outs[0], outs[-1]
        xf = x_ref[...]
        r = lax.rsqrt(jnp.mean(xf * xf, axis=-1, keepdims=True) + EPS)
        xhat = xf * r
        dyv = dy_ref[...]
        gd = dyv * g_ref[...]
        dx = a_ref[...] + r * (gd - xhat * jnp.mean(xhat * gd, axis=-1, keepdims=True))
        dx_ref[...] = dx
        if want_bf16:
            outs[1][...] = dx.astype(BF16)

        @pl.when(i == 0)
        def _():
            acc_ref[...] = jnp.zeros_like(acc_ref)

        acc_ref[0:1, :] += jnp.sum(dyv * xhat, axis=0, keepdims=True)

    row = pl.BlockSpec((tm, d), lambda i: (i, 0))
    out_specs = [row] + ([row] if want_bf16 else []) + [pl.BlockSpec((SUBLANES, d), lambda i: (0, 0))]
    out_shape = ([jax.ShapeDtypeStruct((s, d), F32)]
                 + ([jax.ShapeDtypeStruct((s, d), BF16)] if want_bf16 else [])
                 + [jax.ShapeDtypeStruct((SUBLANES, d), F32)])
    return _call(body, name=name, grid=(s // tm,),
                 in_specs=[row, row, row, pl.BlockSpec((1, d), lambda i: (0, 0))],
                 out_specs=out_specs, out_shape=out_shape,
                 compiler_params=_params(("arbitrary",)))(dyn, xin, add, gain)


def head_fwd_bwd(x1, gl, pe, tgt, bias, ple_gain):
    s, d = x1.shape
    tm = _tile(s, 128)

    def body(x1_ref, gl_ref, pe_ref, t_ref, b_ref, g_ref, dgl_ref, dpe_ref, dy_ref, acc_ref):
        i = pl.program_id(0)
        gate = _sigmoid(gl_ref[...] + b_ref[...])
        pev = pe_ref[...]
        r = lax.rsqrt(jnp.mean(pev * pev, axis=-1, keepdims=True) + EPS)
        pehat = pev * r
        gain = g_ref[...]
        e = pehat * gain
        diff = (x1_ref[...] + gate * e) - t_ref[...]
        dy = diff * (1.0 / d)
        dy_ref[...] = dy
        d_gl = (dy * e) * (gate * (1.0 - gate))
        dgl_ref[...] = d_gl.astype(BF16)
        d_e = dy * gate
        gd = d_e * gain
        d_pe = r * (gd - pehat * jnp.mean(pehat * gd, axis=-1, keepdims=True))
        dpe_ref[...] = d_pe.astype(BF16)

        @pl.when(i == 0)
        def _():
            acc_ref[...] = jnp.zeros_like(acc_ref)

        acc_ref[0:1, :] += jnp.sum(d_gl, axis=0, keepdims=True)
        acc_ref[1:2, :] += jnp.sum(d_e * pehat, axis=0, keepdims=True)
        acc_ref[2:3, :] += jnp.sum(diff * diff, axis=0, keepdims=True) * (0.5 / d)

    row = pl.BlockSpec((tm, d), lambda i: (i, 0))
    vec = pl.BlockSpec((1, d), lambda i: (0, 0))
    return _call(body, name="head_fwd_bwd", grid=(s // tm,),
                 in_specs=[row, row, row, row, vec, vec],
                 out_specs=[row, row, row, pl.BlockSpec((SUBLANES, d), lambda i: (0, 0))],
                 out_shape=[jax.ShapeDtypeStruct((s, d), BF16), jax.ShapeDtypeStruct((s, d), BF16),
                            jax.ShapeDtypeStruct((s, d), F32), jax.ShapeDtypeStruct((SUBLANES, d), F32)],
                 compiler_params=_params(("arbitrary",)))(x1, gl, pe, tgt, bias, ple_gain)


def adamw(g, w, m, v, name):
    r, c = g.shape
    tr = _tile(r, 128)

    def body(g_ref, w_ref, m_ref, v_ref, go_ref, d_ref, mo_ref, vo_ref):
        gv = g_ref[...]
        mn = ADAM_B1 * m_ref[...] + (1.0 - ADAM_B1) * gv
        vn = ADAM_B2 * v_ref[...] + (1.0 - ADAM_B2) * (gv * gv)
        m_hat = mn / (1.0 - ADAM_B1 ** ADAM_STEP)
        v_hat = vn / (1.0 - ADAM_B2 ** ADAM_STEP)
        go_ref[...] = gv
        d_ref[...] = -ADAM_LR * (m_hat / (jnp.sqrt(v_hat) + ADAM_EPS) + ADAM_WD * w_ref[...])
        mo_ref[...] = mn
        vo_ref[...] = vn

    blk = pl.BlockSpec((tr, c), lambda i: (i, 0))
    shp = jax.ShapeDtypeStruct((r, c), F32)
    return _call(body, name=name, grid=(r // tr,), in_specs=[blk] * 4, out_specs=[blk] * 4,
                 out_shape=[shp] * 4, compiler_params=_params(("parallel",)))(g, w, m, v)


def matmul(a, b, *, grid, a_spec, b_spec, o_spec, out_shape, dims, name, res=None, res_spec=None, after=()):
    nk = grid[2]
    acc_shape = tuple(d for d in o_spec.block_shape if d is not None)

    def body(*refs):
        a_ref, b_ref = refs[:2]
        r_ref = refs[2] if res is not None else None
        o_ref = refs[3] if res is not None else refs[2]
        part = lax.dot_general(a_ref[...].astype(BF16), b_ref[...].astype(BF16), dims, preferred_element_type=F32)

        def finish(out):
            if res is not None:
                out = r_ref[...] + out
            o_ref[...] = out.astype(o_ref.dtype)

        if nk == 1:
            finish(part)
            return
        acc_ref = refs[-1]
        k = pl.program_id(2)

        @pl.when(k == 0)
        def _():
            acc_ref[...] = part

        @pl.when((k > 0) & (k < nk - 1))
        def _():
            acc_ref[...] += part

        @pl.when(k == nk - 1)
        def _():
            finish(acc_ref[...] + part)

    in_specs = [a_spec, b_spec] + ([res_spec] if res is not None else [])
    args = (a, b) + ((res,) if res is not None else ())
    return _call_after(body, after, name=name, grid=grid, in_specs=in_specs, out_specs=o_spec,
                       out_shape=out_shape, scratch_shapes=[pltpu.VMEM(acc_shape, F32)] if nk > 1 else [],
                       compiler_params=_params(("parallel", "parallel", "arbitrary")))(*args)


def in_proj_part(h, w, z_prev, order, q, in_w):
    s, d = h.shape
    sh = w.shape[1]
    tm = _tile(s, 1024)

    def body(order_ref, h_ref, w_ref, *rest):
        rest[-1][...] = jnp.dot(h_ref[...], w_ref[...], preferred_element_type=F32)

    in_specs = [pl.BlockSpec((tm, d), lambda i, order_ref: (i, 0)),
                pl.BlockSpec((d, sh), lambda i, order_ref: (0, 0))]
    args = [order, h, w]
    if z_prev is not None:
        in_specs.append(_hbm())
        args.append(z_prev)
    grid_spec = pltpu.PrefetchScalarGridSpec(
        num_scalar_prefetch=1, grid=(s // tm,), in_specs=in_specs,
        out_specs=pl.BlockSpec((tm, sh), lambda i, order_ref: (i, order_ref[q])))
    return _call(body, name="mm_z%d" % q, grid_spec=grid_spec,
                 out_shape=jax.ShapeDtypeStruct((s, in_w), F32),
                 input_output_aliases={3: 0} if z_prev is not None else {},
                 compiler_params=_params(("parallel",)))(*args)


def in_proj_bwd(dz, ws, order, d, after):
    s = dz.shape[0]
    sh = ws[0].shape[1]
    tm, tn = _tile(s, 512), _tile(d, 512)
    nq, n_after = len(ws), len(after)

    def body(order_ref, *refs):
        a_refs, b_refs, o_ref = refs[:nq], refs[nq:2 * nq], refs[2 * nq + n_after]
        acc = lax.dot_general(a_refs[0][...], b_refs[0][...], NT, preferred_element_type=F32)
        for q in range(1, nq):
            acc += lax.dot_general(a_refs[q][...], b_refs[q][...], NT, preferred_element_type=F32)
        o_ref[...] = acc

    a_spec = lambda q: pl.BlockSpec((tm, sh), functools.partial(lambda i, j, order_ref, q: (i, order_ref[q]), q=q))
    grid_spec = pltpu.PrefetchScalarGridSpec(
        num_scalar_prefetch=1, grid=(s // tm, d // tn),
        in_specs=[a_spec(q) for q in range(nq)]
        + [pl.BlockSpec((tn, sh), lambda i, j, order_ref: (j, 0))] * nq + [_hbm()] * n_after,
        out_specs=pl.BlockSpec((tm, tn), lambda i, j, order_ref: (i, j)))
    return _call(body, name="mm_d_h", grid_spec=grid_spec, out_shape=jax.ShapeDtypeStruct((s, d), F32),
                 compiler_params=_params(("parallel", "parallel")))(order, *([dz] * nq), *ws, *after)


def _seg_mean(sq, segb):
    parts = []
    for cgrp in range(sq.shape[1] // SEG):
        blk = sq[:, cgrp * SEG:(cgrp + 1) * SEG]
        hi = blk.astype(BF16)
        r1 = blk - hi.astype(F32)
        mid = r1.astype(BF16)
        lo = (r1 - mid.astype(F32)).astype(BF16)
        acc = jnp.dot(hi, segb, preferred_element_type=F32)
        acc += jnp.dot(mid, segb, preferred_element_type=F32)
        acc += jnp.dot(lo, segb, preferred_element_type=F32)
        parts.append(acc)
    out = parts[0] if len(parts) == 1 else jnp.concatenate(parts, axis=1)
    return out * (1.0 / HEAD_DIM)


def _rope(v, cos, sa, sb):
    parts = []
    for cgrp in range(v.shape[1] // LANES):
        blk = v[:, cgrp * LANES:(cgrp + 1) * LANES]
        parts.append(blk * cos + pltpu.roll(blk, LANES - 8, 1) * sa + pltpu.roll(blk, 8, 1) * sb)
    return parts[0] if len(parts) == 1 else jnp.concatenate(parts, axis=1)


def _rope_t(dv, cos, sa, sb):
    parts = []
    for cgrp in range(dv.shape[1] // LANES):
        blk = dv[:, cgrp * LANES:(cgrp + 1) * LANES]
        parts.append(blk * cos + pltpu.roll(blk * sa, 8, 1) + pltpu.roll(blk * sb, LANES - 8, 1))
    return parts[0] if len(parts) == 1 else jnp.concatenate(parts, axis=1)


def _tile_lanes(vec, width):
    reps = width // LANES
    return vec if reps == 1 else jnp.tile(vec, (1, reps))


def qk_prep_fwd(z, tabs, qg, kg, segb, aw, after=()):
    s = z.shape[0]
    tm = _tile(s, 256)
    wq = aw + 2 * KV_WIDTH

    def body(z_ref, cos_ref, sa_ref, sb_ref, qg_ref, kg_ref, seg_ref, qs_ref, ks_ref, vb_ref):
        zz = z_ref[...]
        q, k, v = zz[:, :aw], zz[:, aw:aw + KV_WIDTH], zz[:, aw + KV_WIDTH:]
        cos, sa, sb, segm = cos_ref[...], sa_ref[...], sb_ref[...], seg_ref[...]
        rq = lax.rsqrt(_seg_mean(q * q, segm) + EPS)
        qn = (q * rq) * _tile_lanes(qg_ref[...], aw)
        qs_ref[...] = (_rope(qn, cos, sa, sb) * (HEAD_DIM ** -0.5)).astype(BF16)
        rk = lax.rsqrt(_seg_mean(k * k, segm) + EPS)
        kn = (k * rk) * _tile_lanes(kg_ref[...], KV_WIDTH)
        ks_ref[...] = _rope(kn, cos, sa, sb).astype(BF16)
        vb_ref[...] = v.astype(BF16)

    tab = pl.BlockSpec((tm, LANES), lambda i: (i, 0))
    vec = pl.BlockSpec((1, LANES), lambda i: (0, 0))
    return _call_after(body, after, name="qk_prep_fwd", grid=(s // tm,),
                       in_specs=[pl.BlockSpec((tm, wq), lambda i: (i, 0)), tab, tab, tab, vec, vec,
                                 pl.BlockSpec((SEG, SEG), lambda i: (0, 0))],
                       out_specs=[pl.BlockSpec((tm, aw), lambda i: (i, 0)),
                                  pl.BlockSpec((tm, KV_WIDTH), lambda i: (i, 0)),
                                  pl.BlockSpec((tm, KV_WIDTH), lambda i: (i, 0))],
                       out_shape=[jax.ShapeDtypeStruct((s, aw), BF16), jax.ShapeDtypeStruct((s, KV_WIDTH), BF16),
                                  jax.ShapeDtypeStruct((s, KV_WIDTH), BF16)],
                       compiler_params=_params(("parallel",)))(z, *tabs, qg, kg, segb)


def qk_prep_bwd(z, dqs, dks, dvs, tabs, qg, kg, segb, dz, aw):
    s = z.shape[0]
    tm = _tile(s, 256)
    wq = aw + 2 * KV_WIDTH

    def body(z_ref, dq_ref, dk_ref, dv_ref, cos_ref, sa_ref, sb_ref, qg_ref, kg_ref, seg_ref, dz_in,
             dz_ref, acc_ref):
        i = pl.program_id(0)
        zz = z_ref[...]
        q, k = zz[:, :aw], zz[:, aw:aw + KV_WIDTH]
        cos, sa, sb, segm = cos_ref[...], sa_ref[...], sb_ref[...], seg_ref[...]

        def one(xv, dout, gvec, width):
            g = _tile_lanes(gvec, width)
            r = lax.rsqrt(_seg_mean(xv * xv, segm) + EPS)
            xhat = xv * r
            dn = _rope_t(dout, cos, sa, sb)
            gd = dn * g
            dx = r * (gd - xhat * _seg_mean(xhat * gd, segm))
            contrib = jnp.sum(dn * xhat, axis=0, keepdims=True)
            folded = contrib[:, :LANES]
            for cgrp in range(1, width // LANES):
                folded = folded + contrib[:, cgrp * LANES:(cgrp + 1) * LANES]
            return dx, folded + pltpu.roll(folded, HEAD_DIM, 1)

        dq, gq = one(q, dq_ref[...] * (HEAD_DIM ** -0.5), qg_ref[...], aw)
        dk, gk = one(k, dk_ref[...], kg_ref[...], KV_WIDTH)
        dz_ref[:, :aw] = dq.astype(BF16)
        dz_ref[:, aw:aw + KV_WIDTH] = dk.astype(BF16)
        dz_ref[:, aw + KV_WIDTH:] = dv_ref[...].astype(BF16)

        @pl.when(i == 0)
        def _():
            acc_ref[...] = jnp.zeros_like(acc_ref)

        acc_ref[0:1, :] += gq
        acc_ref[1:2, :] += gk

    tab = pl.BlockSpec((tm, LANES), lambda i: (i, 0))
    vec = pl.BlockSpec((1, LANES), lambda i: (0, 0))
    kvb = pl.BlockSpec((tm, KV_WIDTH), lambda i: (i, 0))
    return _call(body, name="qk_prep_bwd", grid=(s // tm,),
                 in_specs=[pl.BlockSpec((tm, wq), lambda i: (i, 0)),
                           pl.BlockSpec((tm, aw), lambda i: (i, 0)), kvb, kvb, tab, tab, tab, vec, vec,
                           pl.BlockSpec((SEG, SEG), lambda i: (0, 0)), _hbm()],
                 out_specs=[pl.BlockSpec((tm, wq), lambda i: (i, 0)),
                            pl.BlockSpec((SUBLANES, LANES), lambda i: (0, 0))],
                 out_shape=[jax.ShapeDtypeStruct(dz.shape, BF16), jax.ShapeDtypeStruct((SUBLANES, LANES), F32)],
                 input_output_aliases={10: 0},
                 compiler_params=_params(("arbitrary",)))(z, dqs, dks, dvs, *tabs, qg, kg, segb, dz)


def _placed(band, lane_idx):
    out = {}
    for kh in range(N_KV_HEADS):
        grp = band[:, (kh // 2) * LANES:(kh // 2 + 1) * LANES]
        for half in (0, 1):
            t = grp if half == kh % 2 else pltpu.roll(grp, HEAD_DIM, 1)
            keep = (lane_idx >= half * HEAD_DIM) & (lane_idx < (half + 1) * HEAD_DIM)
            out[kh, half] = jnp.where(keep, t, jnp.zeros_like(t))
    return out


def _softmax_with_sink(sc, valid, sink):
    sc = jnp.where(valid, sc, NEG_INF)
    m = jnp.maximum(jnp.max(sc, axis=-1, keepdims=True), sink)
    e = jnp.exp(sc - m)
    es = jnp.exp(sink - m)
    den = jnp.sum(e, axis=-1, keepdims=True) + es
    return e / den, es / den


def _stack_halves(placed, kh):
    return jnp.concatenate([placed[kh, 0], placed[kh, 1]], axis=0)


def _valid_mask(n):
    r_i = lax.broadcasted_iota(jnp.int32, (BLOCK, 2 * BLOCK), 0)
    j_i = lax.broadcasted_iota(jnp.int32, (BLOCK, 2 * BLOCK), 1)
    return (j_i > r_i) & (j_i <= r_i + BLOCK) & ((n > 0) | (j_i >= BLOCK))


def attn_fwd(qs, ks, vb, z, sinks, aw, d, ga_off):
    s = qs.shape[0]
    nb = s // BLOCK
    nq = aw // HEAD_DIM
    grp_sz = nq // N_KV_HEADS
    n_ga = aw // COLT

    def body(q_ref, ko_ref, kp_ref, vo_ref, vp_ref, *rest):
        ga_refs = rest[:n_ga]
        sink_ref, attn_ref, mix_ref = rest[n_ga:]
        n = pl.program_id(0)
        lane_idx = lax.broadcasted_iota(jnp.int32, (2 * BLOCK, LANES), 1)
        kpl = _placed(jnp.concatenate([kp_ref[...], ko_ref[...]], axis=0), lane_idx)
        vpl = _placed(jnp.concatenate([vp_ref[...], vo_ref[...]], axis=0), lane_idx)
        valid = _valid_mask(n)
        groups = range(nq // 2)
        kcat = [_stack_halves(kpl, kh) for kh in range(N_KV_HEADS)]
        vcat = [_stack_halves(vpl, kh) for kh in range(N_KV_HEADS)]
        scs = [lax.dot_general(q_ref[:, c * LANES:(c + 1) * LANES], kcat[2 * c // grp_sz], NT,
                               preferred_element_type=F32) for c in groups]
        probs = [jnp.concatenate(
            [_softmax_with_sink(scs[c][:, half * 2 * BLOCK:(half + 1) * 2 * BLOCK], valid,
                                sink_ref[0, 2 * c + half])[0].astype(BF16) for half in (0, 1)], axis=1)
                 for c in groups]
        for c in groups:
            acc = jnp.dot(probs[c], vcat[2 * c // grp_sz], preferred_element_type=F32)
            attn_ref[:, c * LANES:(c + 1) * LANES] = acc
            col = c * LANES
            ga = ga_refs[col // COLT][:, col % COLT:col % COLT + LANES]
            mix_ref[:, c * LANES:(c + 1) * LANES] = (acc * (ga * _sigmoid(ga))).astype(BF16)

    own = lambda n: (n, 0)
    prev = lambda n: (jnp.maximum(n - 1, 0), 0)
    kvs = lambda imap: pl.BlockSpec((BLOCK, KV_WIDTH), imap)
    ga_specs = [pl.BlockSpec((BLOCK, COLT), functools.partial(lambda n, j: (n, ga_off + j), j=j))
                for j in range(n_ga)]
    return _call(body, name="attn_fwd", grid=(nb,),
                 in_specs=[pl.BlockSpec((BLOCK, aw), own), kvs(own), kvs(prev), kvs(own), kvs(prev)]
                 + ga_specs + [pl.BlockSpec(memory_space=pltpu.SMEM)],
                 out_specs=[pl.BlockSpec((BLOCK, aw), own), pl.BlockSpec((BLOCK, aw), own)],
                 out_shape=[jax.ShapeDtypeStruct((s, aw), F32), jax.ShapeDtypeStruct((s, d), BF16)],
                 compiler_params=_params(("parallel",)))(qs, ks, ks, vb, vb, *([z] * n_ga), sinks)


def gate_bwd(d_mix, attn, z, aw, ga_off, after=()):
    s, in_w = z.shape
    tm = _tile(s, 256)

    def body(dm_ref, at_ref, ga_ref, do_ref, dz_ref):
        ga = ga_ref[...]
        sig = _sigmoid(ga)
        dm = dm_ref[...]
        do_ref[...] = (dm * (ga * sig)).astype(BF16)
        dz_ref[...] = ((dm * at_ref[...]) * (sig * (1.0 + ga * (1.0 - sig)))).astype(BF16)

    blk = pl.BlockSpec((tm, COLT), lambda i, j: (i, j))
    gab = pl.BlockSpec((tm, COLT), lambda i, j: (i, ga_off + j))
    return _call_after(body, after, name="gate_bwd", grid=(s // tm, aw // COLT),
                       in_specs=[blk, blk, gab], out_specs=[blk, gab],
                       out_shape=[jax.ShapeDtypeStruct((s, aw), BF16), jax.ShapeDtypeStruct((s, in_w), BF16)],
                       compiler_params=_params(("parallel", "parallel")))(d_mix, attn, z)


def attn_bwd(qs, ks, vb, d_o, sinks, aw):
    s = qs.shape[0]
    nb = s // BLOCK
    nq = aw // HEAD_DIM
    grp_sz = nq // N_KV_HEADS

    def body(q_ref, ko_ref, kp_ref, vo_ref, vp_ref, do_ref, sink_ref, dq_ref, dk_ref, dv_ref, ds_ref,
             ck_ref, cv_ref):
        n = pl.program_id(0)

        @pl.when(n == 0)
        def _():
            ds_ref[...] = jnp.zeros_like(ds_ref)
            dk_ref[...] = jnp.zeros_like(dk_ref)
            dv_ref[...] = jnp.zeros_like(dv_ref)

        @pl.when(n < nb)
        def _():
            lane_idx = lax.broadcasted_iota(jnp.int32, (2 * BLOCK, LANES), 1)
            lane_row = lax.broadcasted_iota(jnp.int32, (1, LANES), 1)
            kpl = _placed(jnp.concatenate([kp_ref[...], ko_ref[...]], axis=0), lane_idx)
            vpl = _placed(jnp.concatenate([vp_ref[...], vo_ref[...]], axis=0), lane_idx)
            valid = _valid_mask(n)
            ds_row = jnp.zeros((1, LANES), F32)
            wide = 2 * BLOCK
            groups = range(nq // 2)
            per_kv = grp_sz // 2
            kcat = [_stack_halves(kpl, kh) for kh in range(N_KV_HEADS)]
            vcat = [_stack_halves(vpl, kh) for kh in range(N_KV_HEADS)]
            qg = [q_ref[:, c * LANES:(c + 1) * LANES] for c in groups]
            dog = [do_ref[:, c * LANES:(c + 1) * LANES] for c in groups]
            scs = [lax.dot_general(qg[c], kcat[c // per_kv], NT, preferred_element_type=F32) for c in groups]
            dps = [lax.dot_general(dog[c], vcat[c // per_kv], NT, preferred_element_type=F32) for c in groups]
            ds_pairs, p_pairs = [], []
            for c in groups:
                probs, dss = [], []
                for half in (0, 1):
                    h = 2 * c + half
                    cols = slice(half * wide, (half + 1) * wide)
                    p, p_sink = _softmax_with_sink(scs[c][:, cols], valid, sink_ref[0, h])
                    delta = jnp.sum(p * dps[c][:, cols], axis=-1, keepdims=True)
                    dss.append((p * (dps[c][:, cols] - delta)).astype(BF16))
                    probs.append(p.astype(BF16))
                    dsink = -jnp.sum(p_sink * delta, axis=0, keepdims=True)
                    ds_row += jnp.where(lane_row == h, dsink, 0.0)
                ds_pairs.append(jnp.concatenate(dss, axis=1))
                p_pairs.append(jnp.concatenate(probs, axis=1))
            for c in groups:
                dq_ref[:, c * LANES:(c + 1) * LANES] = jnp.dot(ds_pairs[c], kcat[c // per_kv],
                                                               preferred_element_type=F32)
            dkts = [lax.dot_general(qg[c], ds_pairs[c], TN, preferred_element_type=F32) for c in groups]
            dvts = [lax.dot_general(dog[c], p_pairs[c], TN, preferred_element_type=F32) for c in groups]
            dkt, dvt = [], []
            for kh in range(N_KV_HEADS):
                for parts, out in ((dkts, dkt), (dvts, dvt)):
                    tot = parts[kh * per_kv]
                    for c in range(kh * per_kv + 1, (kh + 1) * per_kv):
                        tot = tot + parts[c]
                    out.append(tot[:HEAD_DIM, :wide] + tot[HEAD_DIM:, wide:])
            ds_ref[0:1, :] += ds_row
            back = lambda t: jnp.concatenate(
                [jnp.concatenate(t[2 * g:2 * g + 2], axis=0).T for g in range(N_KV_HEADS // 2)], axis=1)
            dk_band, dv_band = back(dkt), back(dvt)

            @pl.when(n > 0)
            def _():
                dk_ref[...] = ck_ref[...] + dk_band[:BLOCK]
                dv_ref[...] = cv_ref[...] + dv_band[:BLOCK]

            ck_ref[...] = dk_band[BLOCK:]
            cv_ref[...] = dv_band[BLOCK:]

        @pl.when(n == nb)
        def _():
            dk_ref[...] = ck_ref[...]
            dv_ref[...] = cv_ref[...]

    own = lambda n: (jnp.minimum(n, nb - 1), 0)
    prev = lambda n: (jnp.clip(n - 1, 0, nb - 1), 0)
    done = lambda n: (jnp.maximum(n - 1, 0), 0)
    kvs = lambda imap: pl.BlockSpec((BLOCK, KV_WIDTH), imap)
    return _call(body, name="attn_bwd", grid=(nb + 1,),
                 in_specs=[pl.BlockSpec((BLOCK, aw), own), kvs(own), kvs(prev), kvs(own), kvs(prev),
                           pl.BlockSpec((BLOCK, aw), own), pl.BlockSpec(memory_space=pltpu.SMEM)],
                 out_specs=[pl.BlockSpec((BLOCK, aw), own), kvs(done), kvs(done),
                            pl.BlockSpec((SUBLANES, LANES), lambda n: (0, 0))],
                 out_shape=[jax.ShapeDtypeStruct((s, aw), F32), jax.ShapeDtypeStruct((s, KV_WIDTH), F32),
                            jax.ShapeDtypeStruct((s, KV_WIDTH), F32), jax.ShapeDtypeStruct((SUBLANES, LANES), F32)],
                 scratch_shapes=[pltpu.VMEM((BLOCK, KV_WIDTH), F32), pltpu.VMEM((BLOCK, KV_WIDTH), F32)],
                 compiler_params=_params(("arbitrary",)))(qs, ks, ks, vb, vb, d_o, sinks)


def conv_fwd(z, conv_w, mix, aw, cw, b_off):
    s = z.shape[0]
    tm = _tile(s, 256)
    nseg = cw // COLT
    hb = tm // SUBLANES

    def body(b_ref, c_ref, h_ref, g_ref, cp_ref, hp_ref, w_ref, mix_in, mix_ref):
        i = pl.program_id(0)
        u = c_ref[...] * h_ref[...]
        up = jnp.where(i > 0, cp_ref[...] * hp_ref[...], 0.0)
        ext = jnp.concatenate([up, u], axis=0)
        um1 = pltpu.roll(ext, 1, 0)[SUBLANES:]
        um2 = pltpu.roll(ext, 2, 0)[SUBLANES:]
        w = w_ref[...]
        cv = w[0:1] * um2 + w[1:2] * um1 + w[2:3] * u
        g = g_ref[...]
        mix_ref[...] = ((b_ref[...] * cv) * (g * _sigmoid(g))).astype(BF16)

    seg = lambda k: pl.BlockSpec((tm, COLT), functools.partial(lambda i, j, k: (i, b_off + k * nseg + j), k=k))
    halo = lambda k: pl.BlockSpec(
        (SUBLANES, COLT), functools.partial(lambda i, j, k: (jnp.maximum(i * hb - 1, 0), b_off + k * nseg + j), k=k))
    return _call(body, name="conv_fwd", grid=(s // tm, nseg),
                 in_specs=[seg(0), seg(1), seg(2), seg(3), halo(1), halo(2),
                           pl.BlockSpec((SUBLANES, COLT), lambda i, j: (0, j)), _hbm()],
                 out_specs=pl.BlockSpec((tm, COLT), lambda i, j: (i, aw // COLT + j)),
                 out_shape=jax.ShapeDtypeStruct(mix.shape, BF16),
                 input_output_aliases={7: 0},
                 compiler_params=_params(("parallel", "parallel")))(z, z, z, z, z, z, conv_w, mix)


def conv_bwd(z, d_mix, conv_w, dz, aw, cw, b_off):
    s = z.shape[0]
    tm = _tile(s, 256)
    nseg = cw // COLT
    hb = tm // SUBLANES
    n_row = s // tm
    last_h = s // SUBLANES - 1
    n_step = nseg * n_row

    def body(b_ref, c_ref, h_ref, g_ref, cp_ref, hp_ref, bn_ref, gn_ref, dm_ref, dmn_ref, w_ref, dz_in,
             dz_ref, acc_ref, stash_ref, sems):
        j = pl.program_id(0)
        i = pl.program_id(1)
        step = j * n_row + i
        slot = step % 2

        def copies(from_slot, at_step):
            jj, ii = at_step // n_row, at_step % n_row
            rows = pl.ds(pl.multiple_of(ii * tm, tm), tm)
            return [pltpu.make_async_copy(
                stash_ref.at[from_slot, q],
                dz_ref.at[rows, pl.ds(pl.multiple_of((b_off + q * nseg + jj) * COLT, COLT), COLT)],
                sems.at[from_slot, q]) for q in range(4)]

        @pl.when(step >= 2)
        def _():
            for cp in copies(slot, step - 2):
                cp.wait()

        cc, hh, bb, g = c_ref[...], h_ref[...], b_ref[...], g_ref[...]
        w = w_ref[...]
        u = cc * hh
        up = jnp.where(i > 0, cp_ref[...] * hp_ref[...], 0.0)
        ext = jnp.concatenate([up, u], axis=0)
        um1 = pltpu.roll(ext, 1, 0)[SUBLANES:]
        um2 = pltpu.roll(ext, 2, 0)[SUBLANES:]
        cv = w[0:1] * um2 + w[1:2] * um1 + w[2:3] * u
        sig = _sigmoid(g)
        sg = g * sig
        dm = dm_ref[...]
        d_cv = (dm * bb) * sg
        gn = gn_ref[...]
        d_cv_next = jnp.where(i < n_row - 1, (dmn_ref[...] * bn_ref[...]) * (gn * _sigmoid(gn)), 0.0)
        ext2 = jnp.concatenate([d_cv, d_cv_next], axis=0)
        dp1 = pltpu.roll(ext2, tm + SUBLANES - 1, 0)[:tm]
        dp2 = pltpu.roll(ext2, tm + SUBLANES - 2, 0)[:tm]
        d_u = w[2:3] * d_cv + w[1:2] * dp1 + w[0:1] * dp2
        stash_ref[slot, 0] = ((dm * cv) * sg).astype(BF16)
        stash_ref[slot, 1] = (d_u * hh).astype(BF16)
        stash_ref[slot, 2] = (d_u * cc).astype(BF16)
        stash_ref[slot, 3] = (((dm * bb) * cv) * (sig * (1.0 + g * (1.0 - sig)))).astype(BF16)
        for cp in copies(slot, step):
            cp.start()

        @pl.when(i == 0)
        def _():
            acc_ref[...] = jnp.zeros_like(acc_ref)

        acc_ref[0:1, :] += jnp.sum(d_cv * um2, axis=0, keepdims=True)
        acc_ref[1:2, :] += jnp.sum(d_cv * um1, axis=0, keepdims=True)
        acc_ref[2:3, :] += jnp.sum(d_cv * u, axis=0, keepdims=True)

        @pl.when(step == n_step - 1)
        def _():
            if n_step >= 2:
                for cp in copies(1 - slot, step - 1):
                    cp.wait()
            for cp in copies(slot, step):
                cp.wait()

    seg = lambda q: pl.BlockSpec((tm, COLT), functools.partial(lambda j, i, q: (i, b_off + q * nseg + j), q=q))
    halo_p = lambda q: pl.BlockSpec(
        (SUBLANES, COLT),
        functools.partial(lambda j, i, q: (jnp.maximum(i * hb - 1, 0), b_off + q * nseg + j), q=q))
    halo_n = lambda q: pl.BlockSpec(
        (SUBLANES, COLT),
        functools.partial(lambda j, i, q: (jnp.minimum((i + 1) * hb, last_h), b_off + q * nseg + j), q=q))
    return _call(body, name="conv_bwd", grid=(nseg, n_row),
                 in_specs=[seg(0), seg(1), seg(2), seg(3), halo_p(1), halo_p(2), halo_n(0), halo_n(3),
                           pl.BlockSpec((tm, COLT), lambda j, i: (i, aw // COLT + j)),
                           pl.BlockSpec((SUBLANES, COLT),
                                        lambda j, i: (jnp.minimum((i + 1) * hb, last_h), aw // COLT + j)),
                           pl.BlockSpec((SUBLANES, COLT), lambda j, i: (0, j)), _hbm()],
                 out_specs=[_hbm(), pl.BlockSpec((SUBLANES, COLT), lambda j, i: (0, j))],
                 out_shape=[jax.ShapeDtypeStruct(dz.shape, BF16), jax.ShapeDtypeStruct((SUBLANES, cw), F32)],
                 input_output_aliases={11: 0},
                 scratch_shapes=[pltpu.VMEM((2, 4, tm, COLT), BF16), pltpu.SemaphoreType.DMA((2, 4))],
                 compiler_params=_params(("arbitrary", "arbitrary")))(
                     z, z, z, z, z, z, z, z, d_mix, d_mix, conv_w, dz)


def _place():
    x, y, c = lax.axis_index("x"), lax.axis_index("y"), lax.axis_index("c")
    chips = [(1 - x, y), (x, 1 - y), (1 - x, 1 - y)]
    return x, y, c, chips


def _remote(src, dst, send_sem, recv_sem, device):
    return pltpu.make_async_remote_copy(src_ref=src, dst_ref=dst, send_sem=send_sem, recv_sem=recv_sem,
                                        device_id=device, device_id_type=MESH)


def plan_gather_ici(n_split):
    def plan(refs, send, recv):
        x, y, c, chips = _place()
        me = 2 * x + y
        mine, theirs = [], []
        for w, ref in enumerate(refs):
            for j, (cx, cy) in enumerate(chips):
                def part(slot):
                    if w >= n_split:
                        return ref.at[slot]
                    hr = ref.shape[1] // 2
                    return ref.at[slot, pl.ds(c * hr, hr)]
                k = 3 * w + j
                mine.append(_remote(part(me), part(me), send.at[k], recv.at[k], (cx, cy, c)))
                theirs.append(_remote(part(2 * cx + cy), part(2 * cx + cy), send.at[k], recv.at[k], (cx, cy, c)))
        return mine, theirs
    return plan


def plan_shard_ici(j):
    def plan(refs, send, recv):
        _, _, c, chips = _place()
        own, land = refs
        hr = own.shape[0] // 2
        rows = pl.ds(c * hr, hr)
        cp = _remote(own.at[rows], land.at[rows], send.at[0], recv.at[0], (*chips[j], c))
        return [cp], [cp]
    return plan


def plan_shard_pass(refs, send, recv):
    x, y, c, _ = _place()
    land, = refs
    hr = land.shape[0] // 2
    half = lambda core: land.at[pl.ds(core * hr, hr)]
    return ([_remote(half(c), half(c), send.at[0], recv.at[0], (x, y, 1 - c))],
            [_remote(half(1 - c), half(1 - c), send.at[0], recv.at[0], (x, y, 1 - c))])


def plan_gather_pass(refs, send, recv):
    x, y, c, chips = _place()
    mine, theirs = [], []
    for w, ref in enumerate(refs):
        hr = ref.shape[1] // 2
        for j, (cx, cy) in enumerate(chips):
            half = lambda core: ref.at[2 * cx + cy, pl.ds(core * hr, hr)]
            k = 3 * w + j
            mine.append(_remote(half(c), half(c), send.at[k], recv.at[k], (x, y, 1 - c)))
            theirs.append(_remote(half(1 - c), half(1 - c), send.at[k], recv.at[k], (x, y, 1 - c)))
    return mine, theirs


def plan_swap(refs, send, recv):
    x, y, c, _ = _place()
    nw = len(refs) // 2
    mine = []
    for w in range(nw):
        hr = refs[w].shape[1] // 2
        mine.append(_remote(refs[w].at[:, pl.ds((1 - c) * hr, hr), :], refs[nw + w], send.at[w], recv.at[w],
                            (x, y, 1 - c)))
    return mine, mine


def plan_scatter(refs, send, recv):
    x, y, c, chips = _place()
    me = 2 * x + y
    nw = len(refs) // 2
    mine, theirs = [], []
    for w in range(nw):
        for j, (cx, cy) in enumerate(chips):
            k = 3 * w + j
            mine.append(_remote(refs[w].at[2 * cx + cy], refs[nw + w].at[me], send.at[k], recv.at[k], (cx, cy, c)))
            theirs.append(_remote(refs[w].at[me], refs[nw + w].at[2 * cx + cy], send.at[k], recv.at[k], (cx, cy, c)))
    return mine, theirs


def plan_join(refs, send, recv):
    x, y, c, _ = _place()
    mine, theirs = [], []
    for w, ref in enumerate(refs):
        hr = ref.shape[0] // 2
        half = lambda core: ref.at[pl.ds(core * hr, hr)]
        mine.append(_remote(half(c), half(c), send.at[w], recv.at[w], (x, y, 1 - c)))
        theirs.append(_remote(half(1 - c), half(1 - c), send.at[w], recv.at[w], (x, y, 1 - c)))
    return mine, theirs


def exchange(name, plan, arrays, n):
    na = len(arrays)

    def body(*refs):
        mine, theirs = plan(refs[:na], refs[2 * na], refs[2 * na + 1])
        for cp in mine:
            cp.start()
        for cp in theirs:
            cp.wait_recv()
        for cp in mine:
            cp.wait_send()

    return _call(body, name=name, in_specs=[_hbm()] * na, out_specs=[_hbm()] * na,
                 out_shape=[jax.ShapeDtypeStruct(a.shape, a.dtype) for a in arrays],
                 input_output_aliases={i: i for i in range(na)},
                 scratch_shapes=[pltpu.SemaphoreType.DMA((n,)), pltpu.SemaphoreType.DMA((n,))])(*arrays)


def exchange_start(name, groups, arrays, after=()):
    na, ng = len(arrays), len(groups)

    def body(*refs):
        for g, (plan, idx, _) in enumerate(groups):
            mine, _ = plan([refs[i] for i in idx], refs[na + 2 * g], refs[na + 2 * g + 1])
            for cp in mine:
                cp.start()
        refs[-1][...] = jnp.zeros_like(refs[-1])

    hbm = pl.BlockSpec(memory_space=pltpu.HBM)
    sem = pl.BlockSpec(memory_space=pltpu.SEMAPHORE)
    sem_types = [pltpu.SemaphoreType.DMA((n,)) for _, _, n in groups for _ in (0, 1)]
    outs = _call_after(body, after, name=name, in_specs=[hbm] * na,
                       out_specs=[sem] * (2 * ng) + [hbm] * na + [pl.BlockSpec(memory_space=pltpu.VMEM)],
                       out_shape=sem_types + [pltpu.HBM(a.shape, a.dtype) for a in arrays]
                       + [jax.ShapeDtypeStruct((SUBLANES, LANES), F32)],
                       input_output_aliases={i: i + 2 * ng for i in range(na)},
                       compiler_params=pltpu.CompilerParams(
                           has_side_effects=pltpu.SideEffectType.DATAFLOW_SIDE_EFFECTING))(
                               *[pltpu.with_memory_space_constraint(a, pltpu.HBM) for a in arrays])
    sems = [(outs[2 * g], outs[2 * g + 1]) for g in range(ng)]
    return sems, list(outs[2 * ng:-1]), outs[-1]


def exchange_wait(name, plan, sems, arrays, after):
    send_sems, recv_sems = sems
    na = len(arrays)

    def body(*refs):
        mine, theirs = plan(refs[:na], refs[na], refs[na + 1])
        for cp in mine:
            cp.wait_send()
        for cp in theirs:
            cp.wait_recv()

    hbm = pl.BlockSpec(memory_space=pltpu.HBM)
    sem = pl.BlockSpec(memory_space=pltpu.SEMAPHORE)
    return _call(body, name=name, in_specs=[hbm] * na + [sem, sem, _hbm()], out_specs=[hbm] * na,
                 out_shape=[pltpu.HBM(a.shape, a.dtype) for a in arrays],
                 input_output_aliases={i: i for i in range(na)},
                 compiler_params=pltpu.CompilerParams(
                     has_side_effects=pltpu.SideEffectType.DATAFLOW_SIDE_EFFECTING))(
                         *arrays, send_sems, recv_sems, after)


def plan_allgather_small(refs, send, recv):
    x, y, c, _ = _place()
    buf, = refs
    mine, theirs = [], []
    flips = [(fx, fy, fc) for fx in (0, 1) for fy in (0, 1) for fc in (0, 1)][1:]
    for k, (fx, fy, fc) in enumerate(flips):
        peer = (x ^ fx, y ^ fy, c ^ fc)
        slot = lambda px, py, pc: buf.at[4 * px + 2 * py + pc]
        mine.append(_remote(slot(x, y, c), slot(x, y, c), send.at[k], recv.at[k], peer))
        theirs.append(_remote(slot(*peer), slot(*peer), send.at[k], recv.at[k], peer))
    return mine, theirs


def add_sibling(grad, got, core, name):
    _, r, c = grad.shape
    hr = r // 2
    tr = _tile(hr, 128)
    nblk = hr // tr

    def body(core_ref, g_ref, o_ref, out_ref):
        out_ref[...] = (g_ref[...].astype(F32) + o_ref[...].astype(F32)).astype(BF16)

    grid_spec = pltpu.PrefetchScalarGridSpec(
        num_scalar_prefetch=1, grid=(N_CHIPS, nblk),
        in_specs=[pl.BlockSpec((None, tr, c), lambda t, i, core_ref: (t, core_ref[0] * nblk + i, 0)),
                  pl.BlockSpec((None, tr, c), lambda t, i, core_ref: (t, i, 0))],
        out_specs=pl.BlockSpec((None, tr, c), lambda t, i, core_ref: (t, i, 0)))
    return _call(body, name=name, grid_spec=grid_spec,
                 out_shape=jax.ShapeDtypeStruct((N_CHIPS, hr, c), BF16),
                 compiler_params=_params(("parallel", "parallel")))(core, grad, got)


def sum_chips(mine, owned, place, name):
    _, hr, c = mine.shape
    tr = _tile(hr, 128)
    nblk = hr // tr

    def body(place_ref, m_ref, o1_ref, o2_ref, o3_ref, out_ref):
        acc = m_ref[...].astype(F32)
        for o_ref in (o1_ref, o2_ref, o3_ref):
            acc = acc + o_ref[...].astype(F32)
        out_ref[...] = acc

    other = lambda k: pl.BlockSpec((None, tr, c), lambda i, place_ref: ((place_ref[0] + k) % N_CHIPS, i, 0))
    grid_spec = pltpu.PrefetchScalarGridSpec(
        num_scalar_prefetch=1, grid=(nblk,),
        in_specs=[other(0), other(1), other(2), other(3)],
        out_specs=pl.BlockSpec((tr, c), lambda i, place_ref: (place_ref[1] * nblk + i, 0)))
    return _call(body, name=name, grid_spec=grid_spec,
                 out_shape=jax.ShapeDtypeStruct((2 * hr, c), F32),
                 compiler_params=_params(("parallel",)))(place, mine, owned, owned, owned)


def sum_devices(gathered):
    _, rows, width = gathered.shape

    def body(g_ref, out_ref):
        acc = g_ref[0]
        for dev in range(1, 8):
            acc = acc + g_ref[dev]
        out_ref[...] = acc
        tail = acc[SUBLANES:]
        out_ref[SUBLANES:, :] = jnp.broadcast_to(jnp.sum(tail, axis=1, keepdims=True), tail.shape)

    return _call(body, name="sum_devices",
                 in_specs=[pl.BlockSpec(memory_space=pltpu.VMEM)],
                 out_specs=pl.BlockSpec(memory_space=pltpu.VMEM),
                 out_shape=jax.ShapeDtypeStruct((rows, width), F32))(gathered)


def _rope_tables(s):
    half = ROT_DIM // 2
    inv_freq = jnp.power(jnp.float32(ROPE_THETA), -jnp.arange(half, dtype=F32) * 2.0 / ROT_DIM)
    freq64 = jnp.concatenate([inv_freq, inv_freq, jnp.zeros((HEAD_DIM - ROT_DIM,), F32)])
    freq = jnp.concatenate([freq64, freq64])[None, :]
    dim = (jnp.arange(LANES) % HEAD_DIM)[None, :]
    ang = jnp.arange(s).astype(F32)[:, None] * freq
    cos, sin = jnp.cos(ang), jnp.sin(ang)
    return cos, jnp.where(dim < half, -sin, 0.0), jnp.where((dim >= half) & (dim < ROT_DIM), sin, 0.0)


def _pad_rows(a, rows):
    return jnp.pad(a, ((0, rows - a.shape[0]), (0, 0)))


def _pad_cols(a, cols):
    return jnp.pad(a, ((0, 0), (0, cols - a.shape[1])))


def kernel(x, p, norm_gain, w_in, q_norm_gain, k_norm_gain, attn_sinks, conv_w, w_out, ple_gate_norm_gain, w_ple_gate, b_ple_gate, w_ple_proj, ple_norm_gain, loss_target, m_norm_gain, m_w_in, m_q_norm_gain, m_k_norm_gain, m_attn_sinks, m_conv_w, m_w_out, m_ple_gate_norm_gain, m_w_ple_gate, m_b_ple_gate, m_w_ple_proj, m_ple_norm_gain, v_norm_gain, v_w_in, v_q_norm_gain, v_k_norm_gain, v_attn_sinks, v_conv_w, v_w_out, v_ple_gate_norm_gain, v_w_ple_gate, v_b_ple_gate, v_w_ple_proj, v_ple_norm_gain):
    x2, p2, tgt = x[0], p[0, 0], loss_target[0]
    s, d = x2.shape
    ple = p2.shape[1]
    aw = d // 2
    cw = d - aw
    nq = aw // HEAD_DIM
    sh = w_in.shape[2]
    in_w = N_CHIPS * sh
    dq = d // N_CHIPS
    cq = cw // N_CHIPS
    ga_off = (aw + 2 * KV_WIDTH) // COLT
    b_off = (2 * aw + 2 * KV_WIDTH) // COLT
    assert in_w == 2 * aw + 2 * KV_WIDTH + 4 * cw and aw % COLT == 0 and cw % COLT == 0
    assert s % BLOCK == 0 and sh % LANES == 0 and nq % (2 * N_KV_HEADS) == 0

    core = lax.axis_index("c").astype(jnp.int32).reshape(1)
    chip = 2 * lax.axis_index("x") + lax.axis_index("y")

    chip1 = chip.astype(jnp.int32).reshape(1)
    place = jnp.concatenate([chip1, core])
    w_in_own = cast_bf16(w_in[0], "cast_w_in")
    rest = [cast_into_slot(w_out[0], chip1, "cast_w_out"), cast_into_slot(w_ple_gate[0], chip1, "cast_w_pg"),
            cast_into_slot(w_ple_proj[0], chip1, "cast_w_pp"),
            lax.dynamic_update_slice(jnp.zeros((N_CHIPS, SUBLANES, cq), F32),
                                     _pad_rows(conv_w[0], SUBLANES)[None], (chip, 0, 0))]
    order = jnp.stack([chip, chip ^ 2, chip ^ 1, chip ^ 3]).astype(jnp.int32)
    wi_sems, wi_arrs, wi_token = exchange_start(
        "gather_wi_start", [(plan_shard_ici(j), [0, 1 + j], 1) for j in range(3)],
        [w_in_own] + [lax.empty((d, sh), BF16) for _ in range(3)])
    rest_sems, rest, rest_token = exchange_start(
        "gather_rest_start", [(plan_gather_ici(3), [0, 1, 2, 3], 12)], rest, after=(wi_token,))

    tm = _tile(s, 1024)
    tn = _tile(d, 1024)
    tk = _tile(d, 2048)
    ts = _tile(s, 2048)
    h = rms_fwd(x2, norm_gain, "rms_fwd_x", after=(rest_token,))
    tabs = _rope_tables(s)
    qg = jnp.tile(q_norm_gain, (1, LANES // HEAD_DIM))
    kg = jnp.tile(k_norm_gain, (1, LANES // HEAD_DIM))
    seg_i = jnp.arange(SEG) // HEAD_DIM
    segb = (seg_i[:, None] == seg_i[None, :]).astype(BF16)
    z = in_proj_part(h, wi_arrs[0], None, order, 0, in_w)
    w_shards = [wi_arrs[0]]
    for j in range(3):
        w_shards[0], land = exchange_wait("gather_wi_wait%d" % j, plan_shard_ici(j), wi_sems[j],
                                          [w_shards[0], wi_arrs[1 + j]], z)
        land, = exchange("gather_wi_pass%d" % j, plan_shard_pass, [land], 1)
        z = in_proj_part(h, land, z, order, 1 + j, in_w)
        w_shards.append(land)
    wo_all, wg_all, wp_all, conv_all = exchange_wait("gather_rest_wait", plan_gather_ici(3), rest_sems[0], rest, z)
    pass_sems, passed, pass_token = exchange_start(
        "gather_rest_pass_start", [(plan_gather_pass, [0, 1, 2], 9)], [wo_all, wg_all, wp_all])
    conv_full = conv_all.transpose(1, 0, 2).reshape(SUBLANES, cw)
    qs, ks, vb = qk_prep_fwd(z, tabs, qg, kg, segb, aw, after=(pass_token,))
    attn, mix = attn_fwd(qs, ks, vb, z, attn_sinks, aw, d, ga_off)
    mix = conv_fwd(z, conv_full, mix, aw, cw, b_off)
    wo_all, wg_all, wp_all = exchange_wait("gather_rest_pass_wait", plan_gather_pass, pass_sems[0], passed, mix)
    wo_full = wo_all.reshape(d, d)
    wg_full = wg_all.reshape(d, d)
    sq = lambda shape: dict(
        a_spec=pl.BlockSpec((tm, tk), lambda i, j, k: (i, k)),
        o_spec=pl.BlockSpec((tm, tn), lambda i, j, k: (i, j)),
        out_shape=jax.ShapeDtypeStruct(shape, F32))
    x1 = matmul(mix, wo_full, grid=(s // tm, d // tn, d // tk),
                b_spec=pl.BlockSpec((tk, tn), lambda i, j, k: (k, j)), dims=NN, name="mm_x1",
                res=x2, res_spec=pl.BlockSpec((tm, tn), lambda i, j, k: (i, j)), **sq((s, d)))
    hg = rms_fwd(x1, ple_gate_norm_gain, "rms_fwd_x1")
    gl = matmul(hg, wg_full, grid=(s // tm, d // tn, d // tk),
                b_spec=pl.BlockSpec((tk, tn), lambda i, j, k: (k, j)), dims=NN, name="mm_gl", **sq((s, d)))
    pq = d // N_CHIPS
    pe = matmul(p2, wp_all, grid=(s // tm, N_CHIPS, 1),
                a_spec=pl.BlockSpec((tm, ple), lambda i, j, k: (i, 0)),
                b_spec=pl.BlockSpec((None, ple, pq), lambda i, j, k: (j, 0, 0)),
                o_spec=pl.BlockSpec((tm, pq), lambda i, j, k: (i, j)),
                out_shape=jax.ShapeDtypeStruct((s, d), F32), dims=NN, name="mm_pe")

    d_gl, d_pe, dy, acc_head = head_fwd_bwd(x1, gl, pe, tgt, b_ple_gate, ple_norm_gain)
    d_hg = matmul(d_gl, wg_full, grid=(s // tm, d // tn, d // tk),
                  b_spec=pl.BlockSpec((tn, tk), lambda i, j, k: (j, k)), dims=NT, name="mm_d_hg", **sq((s, d)))
    wgrad = lambda a_cols, shape3, o_spec, b_cols, name, a, b, grid: matmul(
        a, b, grid=grid,
        a_spec=pl.BlockSpec((ts, a_cols), lambda i, j, k: (k, i)),
        b_spec=pl.BlockSpec((ts, b_cols), lambda i, j, k: (k, j)),
        o_spec=o_spec, out_shape=jax.ShapeDtypeStruct(shape3, BF16), dims=TN, name=name)
    g_wg = wgrad(tn, (d, d), pl.BlockSpec((tn, tn), lambda i, j, k: (i, j)), tn, "mm_g_wg", hg, d_gl,
                 (d // tn, d // tn, s // ts))
    g_wp = wgrad(ple, (N_CHIPS, ple, pq), pl.BlockSpec((None, ple, pq), lambda i, j, k: (j, 0, 0)), pq,
                 "mm_g_wp", p2, d_pe, (1, N_CHIPS, s // ts))
    d_x1, d_x1b, acc_g = rms_bwd_add(d_hg, x1, dy, ple_gate_norm_gain, "rms_bwd_x1", True)
    d_mix = matmul(d_x1b, wo_full, grid=(s // tm, d // tn, d // tk),
                   b_spec=pl.BlockSpec((tn, tk), lambda i, j, k: (j, k)), dims=NT, name="mm_d_mix", **sq((s, d)))
    g_wo = wgrad(tn, (d, d), pl.BlockSpec((tn, tn), lambda i, j, k: (i, j)), tn, "mm_g_wo", mix, d_x1b,
                 (d // tn, d // tn, s // ts))
    def reduce_start(tag, grads):
        n = len(grads)
        lands = [lax.empty((N_CHIPS, a.shape[1] // 2, a.shape[2]), BF16) for a in grads]
        swapped = exchange("swap_" + tag, plan_swap, list(grads) + lands, n)
        parts = [add_sibling(g, o, core, "add_sibling_%s%d" % (tag, i))
                 for i, (g, o) in enumerate(zip(swapped[:n], swapped[n:]))]
        return exchange_start("scatter_%s_start" % tag, [(plan_scatter, list(range(2 * n)), 3 * n)],
                              parts + [lax.empty(a.shape, BF16) for a in parts])

    def reduce_sum(tag, handle, after):
        sems, arrays, _ = handle
        n = len(arrays) // 2
        got = exchange_wait("scatter_%s_wait" % tag, plan_scatter, sems[0], arrays, after)
        return [sum_chips(pt, o, place, "sum_chips_%s%d" % (tag, i))
                for i, (pt, o) in enumerate(zip(got[:n], got[n:]))]

    early = reduce_start("early", [g_wo.reshape(N_CHIPS, dq, d), g_wg.reshape(N_CHIPS, dq, d), g_wp])
    d_o, dz = gate_bwd(d_mix, attn, z, aw, ga_off, after=(early[-1],))
    dqs, dks, dvs, acc_sink = attn_bwd(qs, ks, vb, d_o, attn_sinks, aw)
    dz, acc_qk = qk_prep_bwd(z, dqs, dks, dvs, tabs, qg, kg, segb, dz, aw)
    dz, acc_conv = conv_bwd(z, d_mix, conv_full, dz, aw, cw, b_off)
    join_sems, joining, join_token = exchange_start(
        "join_early_start", [(plan_join, [0, 1, 2], 3)], reduce_sum("early", early, dz))
    g_wi = matmul(h, dz, grid=(d // tn, N_CHIPS, s // ts),
                  a_spec=pl.BlockSpec((ts, tn), lambda i, j, k: (k, i)),
                  b_spec=pl.BlockSpec((ts, sh), lambda i, j, k: (k, j)),
                  o_spec=pl.BlockSpec((None, tn, sh), lambda i, j, k: (j, i, 0)),
                  out_shape=jax.ShapeDtypeStruct((N_CHIPS, d, sh), BF16), dims=TN, name="mm_g_wi",
                  after=(join_token,))
    full_wo, full_wg, full_wp = exchange_wait("join_early_wait", plan_join, join_sems[0], joining, g_wi)
    late = reduce_start("late", [g_wi])
    d_h = in_proj_bwd(dz, w_shards, order, d, after=(late[-1],))
    grad_x, acc_x = rms_bwd_add(d_h, x2, d_x1, norm_gain, "rms_bwd_x", False)

    wsm = max(d, cw)
    misc = jnp.concatenate([acc_qk[0:1, :HEAD_DIM], acc_qk[1:2, :HEAD_DIM], acc_sink[0:1, :nq]], axis=1)
    small = jnp.concatenate([
        _pad_cols(acc_x[0:1], wsm), _pad_cols(acc_g[0:1], wsm), _pad_cols(acc_head[0:1], wsm),
        _pad_cols(acc_head[1:2], wsm), _pad_cols(misc, wsm), _pad_cols(acc_conv[0:3], wsm),
        _pad_rows(_pad_cols(acc_head[2:3], wsm), SUBLANES)], axis=0)
    device = 2 * chip + lax.axis_index("c")
    small_sems, small_bufs, small_token = exchange_start(
        "small_start", [(plan_allgather_small, [0], 7)],
        [lax.dynamic_update_slice(jnp.zeros((8,) + small.shape, F32), small[None], (device, 0, 0))])
    full_wi, = exchange("join_late", plan_join, reduce_sum("late", late, small_token), 1)
    big = {}
    for nm, g, w, m, v in (("w_in", full_wi, w_in, m_w_in, v_w_in), ("w_out", full_wo, w_out, m_w_out, v_w_out),
                           ("w_ple_gate", full_wg, w_ple_gate, m_w_ple_gate, v_w_ple_gate),
                           ("w_ple_proj", full_wp, w_ple_proj, m_w_ple_proj, v_w_ple_proj)):
        big[nm] = [o[None] for o in adamw(g, w[0], m[0], v[0], "adamw_" + nm)]

    gathered, = exchange_wait("small_wait", plan_allgather_small, small_sems[0], small_bufs, big["w_in"][1])
    tot = sum_devices(gathered)
    loss = tot[SUBLANES, 0]

    def pack(vals):
        ng, pg, bg, eg, qgv, kgv, sk, cv = vals
        misc_v = jnp.concatenate([qgv, kgv, sk], axis=1)
        return jnp.concatenate([_pad_cols(ng, wsm), _pad_cols(pg, wsm), _pad_cols(bg, wsm), _pad_cols(eg, wsm),
                                _pad_cols(misc_v, wsm), _pad_cols(cv[0], wsm)], axis=0)

    g_small = jnp.concatenate(
        [tot[0:5], _pad_cols(lax.dynamic_slice(tot[5:8], (0, chip * cq), (3, cq)), wsm)], axis=0)
    w_small = pack((norm_gain, ple_gate_norm_gain, b_ple_gate, ple_norm_gain, q_norm_gain, k_norm_gain,
                    attn_sinks, conv_w))
    m_small = pack((m_norm_gain, m_ple_gate_norm_gain, m_b_ple_gate, m_ple_norm_gain, m_q_norm_gain,
                    m_k_norm_gain, m_attn_sinks, m_conv_w))
    v_small = pack((v_norm_gain, v_ple_gate_norm_gain, v_b_ple_gate, v_ple_norm_gain, v_q_norm_gain,
                    v_k_norm_gain, v_attn_sinks, v_conv_w))
    sm = adamw(g_small, w_small, m_small, v_small, "adamw_small")

    def unpack(a):
        return {"norm_gain": a[0:1, :d], "ple_gate_norm_gain": a[1:2, :d], "b_ple_gate": a[2:3, :d],
                "ple_norm_gain": a[3:4, :d], "q_norm_gain": a[4:5, :HEAD_DIM],
                "k_norm_gain": a[4:5, HEAD_DIM:2 * HEAD_DIM],
                "attn_sinks": a[4:5, 2 * HEAD_DIM:2 * HEAD_DIM + nq], "conv_w": a[5:8, :cq][None]}

    order = ("norm_gain", "w_in", "q_norm_gain", "k_norm_gain", "attn_sinks", "conv_w", "w_out",
             "ple_gate_norm_gain", "w_ple_gate", "b_ple_gate", "w_ple_proj", "ple_norm_gain")
    outs = [loss, grad_x[None]]
    for kind in range(4):
        table = unpack(sm[kind])
        for nm in order:
            outs.append(big[nm][kind] if nm in big else table[nm])
    return tuple(outs)
```

```python
import functools

import jax
import jax.numpy as jnp
from jax import lax
from jax.experimental import pallas as pl
from jax.experimental.pallas import tpu as pltpu

F32 = jnp.float32
BF16 = jnp.bfloat16
MESH = pl.DeviceIdType.MESH

HEAD_DIM = 64
N_KV_HEADS = 4
KV_WIDTH = N_KV_HEADS * HEAD_DIM
BLOCK = 128
ROT_DIM = 16
ROPE_THETA = 500000.0
EPS = 1e-6
NEG_INF = -1e30
N_CHIPS = 4
LANES = 128
SUBLANES = 8
COLT = 512
SEG = 256
VMEM_LIMIT = 48 * 1024 * 1024

ADAM_LR = 0.001
ADAM_B1 = 0.9
ADAM_B2 = 0.999
ADAM_EPS = 1e-08
ADAM_WD = 0.01
ADAM_STEP = 10

NN = (((1,), (0,)), ((), ()))
NT = (((1,), (1,)), ((), ()))
TN = (((0,), (0,)), ((), ()))


def _call(body, **kw):
    return pl.pallas_call(body, **kw)


def _call_after(body, after, **kw):
    n_in, n_after = len(kw["in_specs"]), len(after)
    kw["in_specs"] = list(kw["in_specs"]) + [pl.BlockSpec(memory_space=pl.ANY)] * n_after

    def body_after(*refs):
        body(*refs[:n_in], *refs[n_in + n_after:])

    call = _call(body_after, **kw)
    return lambda *args: call(*args, *after)


def _params(sem):
    return pltpu.CompilerParams(dimension_semantics=sem, vmem_limit_bytes=VMEM_LIMIT)


def _tile(n, pref):
    return pref if n % pref == 0 else n


def _sigmoid(v):
    return 1.0 / (1.0 + jnp.exp(-v))


def _hbm():
    return pl.BlockSpec(memory_space=pl.ANY)


def cast_bf16(a, name):
    r, c = a.shape
    tr = _tile(r, 256)

    def body(a_ref, o_ref):
        o_ref[...] = a_ref[...].astype(BF16)

    return _call(body, name=name, grid=(r // tr,),
                 in_specs=[pl.BlockSpec((tr, c), lambda i: (i, 0))],
                 out_specs=pl.BlockSpec((tr, c), lambda i: (i, 0)),
                 out_shape=jax.ShapeDtypeStruct((r, c), BF16),
                 compiler_params=_params(("parallel",)))(a)


def cast_into_slot(a, chip, name):
    r, c = a.shape
    tr = _tile(r, 256)

    def body(chip_ref, a_ref, o_ref):
        o_ref[...] = a_ref[...].astype(BF16)

    grid_spec = pltpu.PrefetchScalarGridSpec(
        num_scalar_prefetch=1, grid=(r // tr,),
        in_specs=[pl.BlockSpec((tr, c), lambda i, chip_ref: (i, 0))],
        out_specs=pl.BlockSpec((None, tr, c), lambda i, chip_ref: (chip_ref[0], i, 0)))
    return _call(body, name=name, grid_spec=grid_spec,
                 out_shape=jax.ShapeDtypeStruct((N_CHIPS, r, c), BF16),
                 compiler_params=_params(("parallel",)))(chip, a)


def rms_fwd(x, gain, name, after=()):
    s, d = x.shape
    tm = _tile(s, 256)

    def body(x_ref, g_ref, o_ref):
        xf = x_ref[...]
        r = lax.rsqrt(jnp.mean(xf * xf, axis=-1, keepdims=True) + EPS)
        o_ref[...] = ((xf * r) * g_ref[...]).astype(BF16)

    return _call_after(body, after, name=name, grid=(s // tm,),
                       in_specs=[pl.BlockSpec((tm, d), lambda i: (i, 0)),
                                 pl.BlockSpec((1, d), lambda i: (0, 0))],
                       out_specs=pl.BlockSpec((tm, d), lambda i: (i, 0)),
                       out_shape=jax.ShapeDtypeStruct((s, d), BF16),
                       compiler_params=_params(("parallel",)))(x, gain)


def rms_bwd_add(dyn, xin, add, gain, name, want_bf16):
    s, d = xin.shape
    tm = _tile(s, 256)

    def body(dy_ref, x_ref, a_ref, g_ref, *outs):
        i = pl.program_id(0)
        dx_ref, acc_ref = outs[0], outs[-1]
        xf = x_ref[...]
        r = lax.rsqrt(jnp.mean(xf * xf, axis=-1, keepdims=True) + EPS)
        xhat = xf * r
        dyv = dy_ref[...]
        gd = dyv * g_ref[...]
        dx = a_ref[...] + r * (gd - xhat * jnp.mean(xhat * gd, axis=-1, keepdims=True))
        dx_ref[...] = dx
        if want_bf16:
            outs[1][...] = dx.astype(BF16)

        @pl.when(i == 0)
        def _():
            acc_ref[...] = jnp.zeros_like(acc_ref)

        acc_ref[0:1, :] += jnp.sum(dyv * xhat, axis=0, keepdims=True)

    row = pl.BlockSpec((tm, d), lambda i: (i, 0))
    out_specs = [row] + ([row] if want_bf16 else []) + [pl.BlockSpec((SUBLANES, d), lambda i: (0, 0))]
    out_shape = ([jax.ShapeDtypeStruct((s, d), F32)]
                 + ([jax.ShapeDtypeStruct((s, d), BF16)] if want_bf16 else [])
                 + [jax.ShapeDtypeStruct((SUBLANES, d), F32)])
    return _call(body, name=name, grid=(s // tm,),
                 in_specs=[row, row, row, pl.BlockSpec((1, d), lambda i: (0, 0))],
                 out_specs=out_specs, out_shape=out_shape,
                 compiler_params=_params(("arbitrary",)))(dyn, xin, add, gain)


def head_fwd_bwd(x1, gl, p, wp, tgt, bias, ple_gain):
    s, d = x1.shape
    tm = _tile(s, 128)

    def body(x1_ref, gl_ref, p_ref, wp_ref, t_ref, b_ref, g_ref, dgl_ref, dpe_ref, dy_ref, acc_ref):
        i = pl.program_id(0)
        gate = _sigmoid(gl_ref[...] + b_ref[...])
        pb = p_ref[...].astype(BF16)
        pev = jnp.concatenate([jnp.dot(pb, wp_ref[q], preferred_element_type=F32) for q in range(N_CHIPS)],
                              axis=1)
        r = lax.rsqrt(jnp.mean(pev * pev, axis=-1, keepdims=True) + EPS)
        pehat = pev * r
        gain = g_ref[...]
        e = pehat * gain
        diff = (x1_ref[...] + gate * e) - t_ref[...]
        dy = diff * (1.0 / d)
        dy_ref[...] = dy
        d_gl = (dy * e) * (gate * (1.0 - gate))
        dgl_ref[...] = d_gl.astype(BF16)
        d_e = dy * gate
        gd = d_e * gain
        d_pe = r * (gd - pehat * jnp.mean(pehat * gd, axis=-1, keepdims=True))
        dpe_ref[...] = d_pe.astype(BF16)

        @pl.when(i == 0)
        def _():
            acc_ref[...] = jnp.zeros_like(acc_ref)

        acc_ref[0:1, :] += jnp.sum(d_gl, axis=0, keepdims=True)
        acc_ref[1:2, :] += jnp.sum(d_e * pehat, axis=0, keepdims=True)
        acc_ref[2:3, :] += jnp.sum(diff * diff, axis=0, keepdims=True) * (0.5 / d)

    row = pl.BlockSpec((tm, d), lambda i: (i, 0))
    vec = pl.BlockSpec((1, d), lambda i: (0, 0))
    return _call(body, name="head_fwd_bwd", grid=(s // tm,),
                 in_specs=[row, row, pl.BlockSpec((tm, p.shape[1]), lambda i: (i, 0)),
                           pl.BlockSpec(wp.shape, lambda i: (0, 0, 0)), row, vec, vec],
                 out_specs=[row, row, row, pl.BlockSpec((SUBLANES, d), lambda i: (0, 0))],
                 out_shape=[jax.ShapeDtypeStruct((s, d), BF16), jax.ShapeDtypeStruct((s, d), BF16),
                            jax.ShapeDtypeStruct((s, d), F32), jax.ShapeDtypeStruct((SUBLANES, d), F32)],
                 compiler_params=_params(("arbitrary",)))(x1, gl, p, wp, tgt, bias, ple_gain)


def adamw(g, w, m, v, name):
    r, c = g.shape
    tr = _tile(r, 128)

    def body(g_ref, w_ref, m_ref, v_ref, go_ref, d_ref, mo_ref, vo_ref):
        gv = g_ref[...]
        mn = ADAM_B1 * m_ref[...] + (1.0 - ADAM_B1) * gv
        vn = ADAM_B2 * v_ref[...] + (1.0 - ADAM_B2) * (gv * gv)
        m_hat = mn / (1.0 - ADAM_B1 ** ADAM_STEP)
        v_hat = vn / (1.0 - ADAM_B2 ** ADAM_STEP)
        go_ref[...] = gv
        d_ref[...] = -ADAM_LR * (m_hat / (jnp.sqrt(v_hat) + ADAM_EPS) + ADAM_WD * w_ref[...])
        mo_ref[...] = mn
        vo_ref[...] = vn

    blk = pl.BlockSpec((tr, c), lambda i: (i, 0))
    shp = jax.ShapeDtypeStruct((r, c), F32)
    return _call(body, name=name, grid=(r // tr,), in_specs=[blk] * 4, out_specs=[blk] * 4,
                 out_shape=[shp] * 4, compiler_params=_params(("parallel",)))(g, w, m, v)


def matmul(a, b, *, grid, a_spec, b_spec, o_spec, out_shape, dims, name, res=None, res_spec=None, after=()):
    nk = grid[2]
    acc_shape = tuple(d for d in o_spec.block_shape if d is not None)

    def body(*refs):
        a_ref, b_ref = refs[:2]
        r_ref = refs[2] if res is not None else None
        o_ref = refs[3] if res is not None else refs[2]
        part = lax.dot_general(a_ref[...].astype(BF16), b_ref[...].astype(BF16), dims, preferred_element_type=F32)

        def finish(out):
            if res is not None:
                out = r_ref[...] + out
            o_ref[...] = out.astype(o_ref.dtype)

        if nk == 1:
            finish(part)
            return
        acc_ref = refs[-1]
        k = pl.program_id(2)

        @pl.when(k == 0)
        def _():
            acc_ref[...] = part

        @pl.when((k > 0) & (k < nk - 1))
        def _():
            acc_ref[...] += part

        @pl.when(k == nk - 1)
        def _():
            finish(acc_ref[...] + part)

    in_specs = [a_spec, b_spec] + ([res_spec] if res is not None else [])
    args = (a, b) + ((res,) if res is not None else ())
    return _call_after(body, after, name=name, grid=grid, in_specs=in_specs, out_specs=o_spec,
                       out_shape=out_shape, scratch_shapes=[pltpu.VMEM(acc_shape, F32)] if nk > 1 else [],
                       compiler_params=_params(("parallel", "parallel", "arbitrary")))(*args)


def in_proj_part(h, w, z_prev, order, q, in_w):
    s, d = h.shape
    sh = w.shape[1]
    tm = _tile(s, 1024)

    def body(order_ref, h_ref, w_ref, *rest):
        rest[-1][...] = jnp.dot(h_ref[...], w_ref[...], preferred_element_type=F32)

    in_specs = [pl.BlockSpec((tm, d), lambda i, order_ref: (i, 0)),
                pl.BlockSpec((d, sh), lambda i, order_ref: (0, 0))]
    args = [order, h, w]
    if z_prev is not None:
        in_specs.append(_hbm())
        args.append(z_prev)
    grid_spec = pltpu.PrefetchScalarGridSpec(
        num_scalar_prefetch=1, grid=(s // tm,), in_specs=in_specs,
        out_specs=pl.BlockSpec((tm, sh), lambda i, order_ref: (i, order_ref[q])))
    return _call(body, name="mm_z%d" % q, grid_spec=grid_spec,
                 out_shape=jax.ShapeDtypeStruct((s, in_w), F32),
                 input_output_aliases={3: 0} if z_prev is not None else {},
                 compiler_params=_params(("parallel",)))(*args)


def in_proj_bwd(dz, ws, order, d, after):
    s = dz.shape[0]
    sh = ws[0].shape[1]
    tm, tn = _tile(s, 512), _tile(d, 512)
    nq, n_after = len(ws), len(after)

    def body(order_ref, *refs):
        a_refs, b_refs, o_ref = refs[:nq], refs[nq:2 * nq], refs[2 * nq + n_after]
        acc = lax.dot_general(a_refs[0][...], b_refs[0][...], NT, preferred_element_type=F32)
        for q in range(1, nq):
            acc += lax.dot_general(a_refs[q][...], b_refs[q][...], NT, preferred_element_type=F32)
        o_ref[...] = acc

    a_spec = lambda q: pl.BlockSpec((tm, sh), functools.partial(lambda i, j, order_ref, q: (i, order_ref[q]), q=q))
    grid_spec = pltpu.PrefetchScalarGridSpec(
        num_scalar_prefetch=1, grid=(s // tm, d // tn),
        in_specs=[a_spec(q) for q in range(nq)]
        + [pl.BlockSpec((tn, sh), lambda i, j, order_ref: (j, 0))] * nq + [_hbm()] * n_after,
        out_specs=pl.BlockSpec((tm, tn), lambda i, j, order_ref: (i, j)))
    return _call(body, name="mm_d_h", grid_spec=grid_spec, out_shape=jax.ShapeDtypeStruct((s, d), F32),
                 compiler_params=_params(("parallel", "parallel")))(order, *([dz] * nq), *ws, *after)


def _seg_mean(sq, segb):
    parts = []
    for cgrp in range(sq.shape[1] // SEG):
        blk = sq[:, cgrp * SEG:(cgrp + 1) * SEG]
        hi = blk.astype(BF16)
        r1 = blk - hi.astype(F32)
        mid = r1.astype(BF16)
        lo = (r1 - mid.astype(F32)).astype(BF16)
        acc = jnp.dot(hi, segb, preferred_element_type=F32)
        acc += jnp.dot(mid, segb, preferred_element_type=F32)
        acc += jnp.dot(lo, segb, preferred_element_type=F32)
        parts.append(acc)
    out = parts[0] if len(parts) == 1 else jnp.concatenate(parts, axis=1)
    return out * (1.0 / HEAD_DIM)


def _rope(v, cos, sa, sb):
    parts = []
    for cgrp in range(v.shape[1] // LANES):
        blk = v[:, cgrp * LANES:(cgrp + 1) * LANES]
        parts.append(blk * cos + pltpu.roll(blk, LANES - 8, 1) * sa + pltpu.roll(blk, 8, 1) * sb)
    return parts[0] if len(parts) == 1 else jnp.concatenate(parts, axis=1)


def _rope_t(dv, cos, sa, sb):
    parts = []
    for cgrp in range(dv.shape[1] // LANES):
        blk = dv[:, cgrp * LANES:(cgrp + 1) * LANES]
        parts.append(blk * cos + pltpu.roll(blk * sa, 8, 1) + pltpu.roll(blk * sb, LANES - 8, 1))
    return parts[0] if len(parts) == 1 else jnp.concatenate(parts, axis=1)


def _tile_lanes(vec, width):
    reps = width // LANES
    return vec if reps == 1 else jnp.tile(vec, (1, reps))


def qk_prep_fwd(z, tabs, qg, kg, segb, aw, after=()):
    s = z.shape[0]
    tm = _tile(s, 256)
    wq = aw + 2 * KV_WIDTH

    def body(z_ref, cos_ref, sa_ref, sb_ref, qg_ref, kg_ref, seg_ref, qs_ref, ks_ref, vb_ref):
        zz = z_ref[...]
        q, k, v = zz[:, :aw], zz[:, aw:aw + KV_WIDTH], zz[:, aw + KV_WIDTH:]
        cos, sa, sb, segm = cos_ref[...], sa_ref[...], sb_ref[...], seg_ref[...]
        rq = lax.rsqrt(_seg_mean(q * q, segm) + EPS)
        qn = (q * rq) * _tile_lanes(qg_ref[...], aw)
        qs_ref[...] = (_rope(qn, cos, sa, sb) * (HEAD_DIM ** -0.5)).astype(BF16)
        rk = lax.rsqrt(_seg_mean(k * k, segm) + EPS)
        kn = (k * rk) * _tile_lanes(kg_ref[...], KV_WIDTH)
        ks_ref[...] = _rope(kn, cos, sa, sb).astype(BF16)
        vb_ref[...] = v.astype(BF16)

    tab = pl.BlockSpec((tm, LANES), lambda i: (i, 0))
    vec = pl.BlockSpec((1, LANES), lambda i: (0, 0))
    return _call_after(body, after, name="qk_prep_fwd", grid=(s // tm,),
                       in_specs=[pl.BlockSpec((tm, wq), lambda i: (i, 0)), tab, tab, tab, vec, vec,
                                 pl.BlockSpec((SEG, SEG), lambda i: (0, 0))],
                       out_specs=[pl.BlockSpec((tm, aw), lambda i: (i, 0)),
                                  pl.BlockSpec((tm, KV_WIDTH), lambda i: (i, 0)),
                                  pl.BlockSpec((tm, KV_WIDTH), lambda i: (i, 0))],
                       out_shape=[jax.ShapeDtypeStruct((s, aw), BF16), jax.ShapeDtypeStruct((s, KV_WIDTH), BF16),
                                  jax.ShapeDtypeStruct((s, KV_WIDTH), BF16)],
                       compiler_params=_params(("parallel",)))(z, *tabs, qg, kg, segb)


def qk_prep_bwd(z, dqs, dks, dvs, tabs, qg, kg, segb, dz, aw):
    s = z.shape[0]
    tm = _tile(s, 256)
    wq = aw + 2 * KV_WIDTH

    def body(z_ref, dq_ref, dk_ref, dv_ref, cos_ref, sa_ref, sb_ref, qg_ref, kg_ref, seg_ref, dz_in,
             dz_ref, acc_ref):
        i = pl.program_id(0)
        zz = z_ref[...]
        q, k = zz[:, :aw], zz[:, aw:aw + KV_WIDTH]
        cos, sa, sb, segm = cos_ref[...], sa_ref[...], sb_ref[...], seg_ref[...]

        def one(xv, dout, gvec, width):
            g = _tile_lanes(gvec, width)
            r = lax.rsqrt(_seg_mean(xv * xv, segm) + EPS)
            xhat = xv * r
            dn = _rope_t(dout, cos, sa, sb)
            gd = dn * g
            dx = r * (gd - xhat * _seg_mean(xhat * gd, segm))
            contrib = jnp.sum(dn * xhat, axis=0, keepdims=True)
            folded = contrib[:, :LANES]
            for cgrp in range(1, width // LANES):
                folded = folded + contrib[:, cgrp * LANES:(cgrp + 1) * LANES]
            return dx, folded + pltpu.roll(folded, HEAD_DIM, 1)

        dq, gq = one(q, dq_ref[...] * (HEAD_DIM ** -0.5), qg_ref[...], aw)
        dk, gk = one(k, dk_ref[...], kg_ref[...], KV_WIDTH)
        dz_ref[:, :aw] = dq.astype(BF16)
        dz_ref[:, aw:aw + KV_WIDTH] = dk.astype(BF16)
        dz_ref[:, aw + KV_WIDTH:] = dv_ref[...].astype(BF16)

        @pl.when(i == 0)
        def _():
            acc_ref[...] = jnp.zeros_like(acc_ref)

        acc_ref[0:1, :] += gq
        acc_ref[1:2, :] += gk

    tab = pl.BlockSpec((tm, LANES), lambda i: (i, 0))
    vec = pl.BlockSpec((1, LANES), lambda i: (0, 0))
    kvb = pl.BlockSpec((tm, KV_WIDTH), lambda i: (i, 0))
    return _call(body, name="qk_prep_bwd", grid=(s // tm,),
                 in_specs=[pl.BlockSpec((tm, wq), lambda i: (i, 0)),
                           pl.BlockSpec((tm, aw), lambda i: (i, 0)), kvb, kvb, tab, tab, tab, vec, vec,
                           pl.BlockSpec((SEG, SEG), lambda i: (0, 0)), _hbm()],
                 out_specs=[pl.BlockSpec((tm, wq), lambda i: (i, 0)),
                            pl.BlockSpec((SUBLANES, LANES), lambda i: (0, 0))],
                 out_shape=[jax.ShapeDtypeStruct(dz.shape, BF16), jax.ShapeDtypeStruct((SUBLANES, LANES), F32)],
                 input_output_aliases={10: 0},
                 compiler_params=_params(("arbitrary",)))(z, dqs, dks, dvs, *tabs, qg, kg, segb, dz)


def _placed(band, lane_idx):
    out = {}
    for kh in range(N_KV_HEADS):
        grp = band[:, (kh // 2) * LANES:(kh // 2 + 1) * LANES]
        for half in (0, 1):
            t = grp if half == kh % 2 else pltpu.roll(grp, HEAD_DIM, 1)
            keep = (lane_idx >= half * HEAD_DIM) & (lane_idx < (half + 1) * HEAD_DIM)
            out[kh, half] = jnp.where(keep, t, jnp.zeros_like(t))
    return out


def _softmax_with_sink(sc, valid, sink):
    sc = jnp.where(valid, sc, NEG_INF)
    m = jnp.maximum(jnp.max(sc, axis=-1, keepdims=True), sink)
    e = jnp.exp(sc - m)
    es = jnp.exp(sink - m)
    den = jnp.sum(e, axis=-1, keepdims=True) + es
    return e / den, es / den


def _stack_halves(placed, kh):
    return jnp.concatenate([placed[kh, 0], placed[kh, 1]], axis=0)


def _valid_mask(n):
    r_i = lax.broadcasted_iota(jnp.int32, (BLOCK, 2 * BLOCK), 0)
    j_i = lax.broadcasted_iota(jnp.int32, (BLOCK, 2 * BLOCK), 1)
    return (j_i > r_i) & (j_i <= r_i + BLOCK) & ((n > 0) | (j_i >= BLOCK))


def attn_fwd(qs, ks, vb, z, sinks, aw, d, ga_off):
    s = qs.shape[0]
    nb = s // BLOCK
    nq = aw // HEAD_DIM
    grp_sz = nq // N_KV_HEADS
    n_ga = aw // COLT

    def body(q_ref, ko_ref, kp_ref, vo_ref, vp_ref, *rest):
        ga_refs = rest[:n_ga]
        sink_ref, attn_ref, mix_ref = rest[n_ga:]
        n = pl.program_id(0)
        lane_idx = lax.broadcasted_iota(jnp.int32, (2 * BLOCK, LANES), 1)
        kpl = _placed(jnp.concatenate([kp_ref[...], ko_ref[...]], axis=0), lane_idx)
        vpl = _placed(jnp.concatenate([vp_ref[...], vo_ref[...]], axis=0), lane_idx)
        valid = _valid_mask(n)
        groups = range(nq // 2)
        kcat = [_stack_halves(kpl, kh) for kh in range(N_KV_HEADS)]
        vcat = [_stack_halves(vpl, kh) for kh in range(N_KV_HEADS)]
        scs = [lax.dot_general(q_ref[:, c * LANES:(c + 1) * LANES], kcat[2 * c // grp_sz], NT,
                               preferred_element_type=F32) for c in groups]
        probs = [jnp.concatenate(
            [_softmax_with_sink(scs[c][:, half * 2 * BLOCK:(half + 1) * 2 * BLOCK], valid,
                                sink_ref[0, 2 * c + half])[0].astype(BF16) for half in (0, 1)], axis=1)
                 for c in groups]
        for c in groups:
            acc = jnp.dot(probs[c], vcat[2 * c // grp_sz], preferred_element_type=F32)
            attn_ref[:, c * LANES:(c + 1) * LANES] = acc
            col = c * LANES
            ga = ga_refs[col // COLT][:, col % COLT:col % COLT + LANES]
            mix_ref[:, c * LANES:(c + 1) * LANES] = (acc * (ga * _sigmoid(ga))).astype(BF16)

    own = lambda n: (n, 0)
    prev = lambda n: (jnp.maximum(n - 1, 0), 0)
    kvs = lambda imap: pl.BlockSpec((BLOCK, KV_WIDTH), imap)
    ga_specs = [pl.BlockSpec((BLOCK, COLT), functools.partial(lambda n, j: (n, ga_off + j), j=j))
                for j in range(n_ga)]
    return _call(body, name="attn_fwd", grid=(nb,),
                 in_specs=[pl.BlockSpec((BLOCK, aw), own), kvs(own), kvs(prev), kvs(own), kvs(prev)]
                 + ga_specs + [pl.BlockSpec(memory_space=pltpu.SMEM)],
                 out_specs=[pl.BlockSpec((BLOCK, aw), own), pl.BlockSpec((BLOCK, aw), own)],
                 out_shape=[jax.ShapeDtypeStruct((s, aw), F32), jax.ShapeDtypeStruct((s, d), BF16)],
                 compiler_params=_params(("parallel",)))(qs, ks, ks, vb, vb, *([z] * n_ga), sinks)


def gate_bwd(d_mix, attn, z, aw, ga_off, after=()):
    s, in_w = z.shape
    tm = _tile(s, 256)

    def body(dm_ref, at_ref, ga_ref, do_ref, dz_ref):
        ga = ga_ref[...]
        sig = _sigmoid(ga)
        dm = dm_ref[...]
        do_ref[...] = (dm * (ga * sig)).astype(BF16)
        dz_ref[...] = ((dm * at_ref[...]) * (sig * (1.0 + ga * (1.0 - sig)))).astype(BF16)

    blk = pl.BlockSpec((tm, COLT), lambda i, j: (i, j))
    gab = pl.BlockSpec((tm, COLT), lambda i, j: (i, ga_off + j))
    return _call_after(body, after, name="gate_bwd", grid=(s // tm, aw // COLT),
                       in_specs=[blk, blk, gab], out_specs=[blk, gab],
                       out_shape=[jax.ShapeDtypeStruct((s, aw), BF16), jax.ShapeDtypeStruct((s, in_w), BF16)],
                       compiler_params=_params(("parallel", "parallel")))(d_mix, attn, z)


def attn_bwd(qs, ks, vb, d_o, sinks, aw):
    s = qs.shape[0]
    nb = s // BLOCK
    nq = aw // HEAD_DIM
    grp_sz = nq // N_KV_HEADS

    def body(q_ref, ko_ref, kp_ref, vo_ref, vp_ref, do_ref, sink_ref, dq_ref, dk_ref, dv_ref, ds_ref,
             ck_ref, cv_ref):
        n = pl.program_id(0)

        @pl.when(n == 0)
        def _():
            ds_ref[...] = jnp.zeros_like(ds_ref)
            dk_ref[...] = jnp.zeros_like(dk_ref)
            dv_ref[...] = jnp.zeros_like(dv_ref)

        @pl.when(n < nb)
        def _():
            lane_idx = lax.broadcasted_iota(jnp.int32, (2 * BLOCK, LANES), 1)
            lane_row = lax.broadcasted_iota(jnp.int32, (1, LANES), 1)
            kpl = _placed(jnp.concatenate([kp_ref[...], ko_ref[...]], axis=0), lane_idx)
            vpl = _placed(jnp.concatenate([vp_ref[...], vo_ref[...]], axis=0), lane_idx)
            valid = _valid_mask(n)
            ds_row = jnp.zeros((1, LANES), F32)
            wide = 2 * BLOCK
            groups = range(nq // 2)
            per_kv = grp_sz // 2
            kcat = [_stack_halves(kpl, kh) for kh in range(N_KV_HEADS)]
            vcat = [_stack_halves(vpl, kh) for kh in range(N_KV_HEADS)]
            qg = [q_ref[:, c * LANES:(c + 1) * LANES] for c in groups]
            dog = [do_ref[:, c * LANES:(c + 1) * LANES] for c in groups]
            scs = [lax.dot_general(qg[c], kcat[c // per_kv], NT, preferred_element_type=F32) for c in groups]
            dps = [lax.dot_general(dog[c], vcat[c // per_kv], NT, preferred_element_type=F32) for c in groups]
            ds_pairs, p_pairs = [], []
            for c in groups:
                probs, dss = [], []
                for half in (0, 1):
                    h = 2 * c + half
                    cols = slice(half * wide, (half + 1) * wide)
                    p, p_sink = _softmax_with_sink(scs[c][:, cols], valid, sink_ref[0, h])
                    delta = jnp.sum(p * dps[c][:, cols], axis=-1, keepdims=True)
                    dss.append((p * (dps[c][:, cols] - delta)).astype(BF16))
                    probs.append(p.astype(BF16))
                    dsink = -jnp.sum(p_sink * delta, axis=0, keepdims=True)
                    ds_row += jnp.where(lane_row == h, dsink, 0.0)
                ds_pairs.append(jnp.concatenate(dss, axis=1))
                p_pairs.append(jnp.concatenate(probs, axis=1))
            for c in groups:
                dq_ref[:, c * LANES:(c + 1) * LANES] = jnp.dot(ds_pairs[c], kcat[c // per_kv],
                                                               preferred_element_type=F32)
            dkts = [lax.dot_general(qg[c], ds_pairs[c], TN, preferred_element_type=F32) for c in groups]
            dvts = [lax.dot_general(dog[c], p_pairs[c], TN, preferred_element_type=F32) for c in groups]
            dkt, dvt = [], []
            for kh in range(N_KV_HEADS):
                for parts, out in ((dkts, dkt), (dvts, dvt)):
                    tot = parts[kh * per_kv]
                    for c in range(kh * per_kv + 1, (kh + 1) * per_kv):
                        tot = tot + parts[c]
                    out.append(tot[:HEAD_DIM, :wide] + tot[HEAD_DIM:, wide:])
            ds_ref[0:1, :] += ds_row
            back = lambda t: jnp.concatenate(
                [jnp.concatenate(t[2 * g:2 * g + 2], axis=0).T for g in range(N_KV_HEADS // 2)], axis=1)
            dk_band, dv_band = back(dkt), back(dvt)

            @pl.when(n > 0)
            def _():
                dk_ref[...] = ck_ref[...] + dk_band[:BLOCK]
                dv_ref[...] = cv_ref[...] + dv_band[:BLOCK]

            ck_ref[...] = dk_band[BLOCK:]
            cv_ref[...] = dv_band[BLOCK:]

        @pl.when(n == nb)
        def _():
            dk_ref[...] = ck_ref[...]
            dv_ref[...] = cv_ref[...]

    own = lambda n: (jnp.minimum(n, nb - 1), 0)
    prev = lambda n: (jnp.clip(n - 1, 0, nb - 1), 0)
    done = lambda n: (jnp.maximum(n - 1, 0), 0)
    kvs = lambda imap: pl.BlockSpec((BLOCK, KV_WIDTH), imap)
    return _call(body, name="attn_bwd", grid=(nb + 1,),
                 in_specs=[pl.BlockSpec((BLOCK, aw), own), kvs(own), kvs(prev), kvs(own), kvs(prev),
                           pl.BlockSpec((BLOCK, aw), own), pl.BlockSpec(memory_space=pltpu.SMEM)],
                 out_specs=[pl.BlockSpec((BLOCK, aw), own), kvs(done), kvs(done),
                            pl.BlockSpec((SUBLANES, LANES), lambda n: (0, 0))],
                 out_shape=[jax.ShapeDtypeStruct((s, aw), F32), jax.ShapeDtypeStruct((s, KV_WIDTH), F32),
                            jax.ShapeDtypeStruct((s, KV_WIDTH), F32), jax.ShapeDtypeStruct((SUBLANES, LANES), F32)],
                 scratch_shapes=[pltpu.VMEM((BLOCK, KV_WIDTH), F32), pltpu.VMEM((BLOCK, KV_WIDTH), F32)],
                 compiler_params=_params(("arbitrary",)))(qs, ks, ks, vb, vb, d_o, sinks)


def conv_fwd(z, conv_w, mix, aw, cw, b_off):
    s = z.shape[0]
    tm = _tile(s, 256)
    nseg = cw // COLT
    hb = tm // SUBLANES

    def body(b_ref, c_ref, h_ref, g_ref, cp_ref, hp_ref, w_ref, mix_in, mix_ref):
        i = pl.program_id(0)
        u = c_ref[...] * h_ref[...]
        up = jnp.where(i > 0, cp_ref[...] * hp_ref[...], 0.0)
        ext = jnp.concatenate([up, u], axis=0)
        um1 = pltpu.roll(ext, 1, 0)[SUBLANES:]
        um2 = pltpu.roll(ext, 2, 0)[SUBLANES:]
        w = w_ref[...]
        cv = w[0:1] * um2 + w[1:2] * um1 + w[2:3] * u
        g = g_ref[...]
        mix_ref[...] = ((b_ref[...] * cv) * (g * _sigmoid(g))).astype(BF16)

    seg = lambda k: pl.BlockSpec((tm, COLT), functools.partial(lambda i, j, k: (i, b_off + k * nseg + j), k=k))
    halo = lambda k: pl.BlockSpec(
        (SUBLANES, COLT), functools.partial(lambda i, j, k: (jnp.maximum(i * hb - 1, 0), b_off + k * nseg + j), k=k))
    return _call(body, name="conv_fwd", grid=(s // tm, nseg),
                 in_specs=[seg(0), seg(1), seg(2), seg(3), halo(1), halo(2),
                           pl.BlockSpec((SUBLANES, COLT), lambda i, j: (0, j)), _hbm()],
                 out_specs=pl.BlockSpec((tm, COLT), lambda i, j: (i, aw // COLT + j)),
                 out_shape=jax.ShapeDtypeStruct(mix.shape, BF16),
                 input_output_aliases={7: 0},
                 compiler_params=_params(("parallel", "parallel")))(z, z, z, z, z, z, conv_w, mix)


def conv_bwd(z, d_mix, conv_w, dz, aw, cw, b_off):
    s = z.shape[0]
    tm = _tile(s, 256)
    nseg = cw // COLT
    hb = tm // SUBLANES
    n_row = s // tm
    last_h = s // SUBLANES - 1
    n_step = nseg * n_row

    def body(b_ref, c_ref, h_ref, g_ref, cp_ref, hp_ref, bn_ref, gn_ref, dm_ref, dmn_ref, w_ref, dz_in,
             dz_ref, acc_ref, stash_ref, sems):
        j = pl.program_id(0)
        i = pl.program_id(1)
        step = j * n_row + i
        slot = step % 2

        def copies(from_slot, at_step):
            jj, ii = at_step // n_row, at_step % n_row
            rows = pl.ds(pl.multiple_of(ii * tm, tm), tm)
            return [pltpu.make_async_copy(
                stash_ref.at[from_slot, q],
                dz_ref.at[rows, pl.ds(pl.multiple_of((b_off + q * nseg + jj) * COLT, COLT), COLT)],
                sems.at[from_slot, q]) for q in range(4)]

        @pl.when(step >= 2)
        def _():
            for cp in copies(slot, step - 2):
                cp.wait()

        cc, hh, bb, g = c_ref[...], h_ref[...], b_ref[...], g_ref[...]
        w = w_ref[...]
        u = cc * hh
        up = jnp.where(i > 0, cp_ref[...] * hp_ref[...], 0.0)
        ext = jnp.concatenate([up, u], axis=0)
        um1 = pltpu.roll(ext, 1, 0)[SUBLANES:]
        um2 = pltpu.roll(ext, 2, 0)[SUBLANES:]
        cv = w[0:1] * um2 + w[1:2] * um1 + w[2:3] * u
        sig = _sigmoid(g)
        sg = g * sig
        dm = dm_ref[...]
        d_cv = (dm * bb) * sg
        gn = gn_ref[...]
        d_cv_next = jnp.where(i < n_row - 1, (dmn_ref[...] * bn_ref[...]) * (gn * _sigmoid(gn)), 0.0)
        ext2 = jnp.concatenate([d_cv, d_cv_next], axis=0)
        dp1 = pltpu.roll(ext2, tm + SUBLANES - 1, 0)[:tm]
        dp2 = pltpu.roll(ext2, tm + SUBLANES - 2, 0)[:tm]
        d_u = w[2:3] * d_cv + w[1:2] * dp1 + w[0:1] * dp2
        stash_ref[slot, 0] = ((dm * cv) * sg).astype(BF16)
        stash_ref[slot, 1] = (d_u * hh).astype(BF16)
        stash_ref[slot, 2] = (d_u * cc).astype(BF16)
        stash_ref[slot, 3] = (((dm * bb) * cv) * (sig * (1.0 + g * (1.0 - sig)))).astype(BF16)
        for cp in copies(slot, step):
            cp.start()

        @pl.when(i == 0)
        def _():
            acc_ref[...] = jnp.zeros_like(acc_ref)

        acc_ref[0:1, :] += jnp.sum(d_cv * um2, axis=0, keepdims=True)
        acc_ref[1:2, :] += jnp.sum(d_cv * um1, axis=0, keepdims=True)
        acc_ref[2:3, :] += jnp.sum(d_cv * u, axis=0, keepdims=True)

        @pl.when(step == n_step - 1)
        def _():
            if n_step >= 2:
                for cp in copies(1 - slot, step - 1):
                    cp.wait()
            for cp in copies(slot, step):
                cp.wait()

    seg = lambda q: pl.BlockSpec((tm, COLT), functools.partial(lambda j, i, q: (i, b_off + q * nseg + j), q=q))
    halo_p = lambda q: pl.BlockSpec(
        (SUBLANES, COLT),
        functools.partial(lambda j, i, q: (jnp.maximum(i * hb - 1, 0), b_off + q * nseg + j), q=q))
    halo_n = lambda q: pl.BlockSpec(
        (SUBLANES, COLT),
        functools.partial(lambda j, i, q: (jnp.minimum((i + 1) * hb, last_h), b_off + q * nseg + j), q=q))
    return _call(body, name="conv_bwd", grid=(nseg, n_row),
                 in_specs=[seg(0), seg(1), seg(2), seg(3), halo_p(1), halo_p(2), halo_n(0), halo_n(3),
                           pl.BlockSpec((tm, COLT), lambda j, i: (i, aw // COLT + j)),
                           pl.BlockSpec((SUBLANES, COLT),
                                        lambda j, i: (jnp.minimum((i + 1) * hb, last_h), aw // COLT + j)),
                           pl.BlockSpec((SUBLANES, COLT), lambda j, i: (0, j)), _hbm()],
                 out_specs=[_hbm(), pl.BlockSpec((SUBLANES, COLT), lambda j, i: (0, j))],
                 out_shape=[jax.ShapeDtypeStruct(dz.shape, BF16), jax.ShapeDtypeStruct((SUBLANES, cw), F32)],
                 input_output_aliases={11: 0},
                 scratch_shapes=[pltpu.VMEM((2, 4, tm, COLT), BF16), pltpu.SemaphoreType.DMA((2, 4))],
                 compiler_params=_params(("arbitrary", "arbitrary")))(
                     z, z, z, z, z, z, z, z, d_mix, d_mix, conv_w, dz)


def _place():
    x, y, c = lax.axis_index("x"), lax.axis_index("y"), lax.axis_index("c")
    chips = [(1 - x, y), (x, 1 - y), (1 - x, 1 - y)]
    return x, y, c, chips


def _remote(src, dst, send_sem, recv_sem, device):
    return pltpu.make_async_remote_copy(src_ref=src, dst_ref=dst, send_sem=send_sem, recv_sem=recv_sem,
                                        device_id=device, device_id_type=MESH)


def plan_gather_ici(n_split):
    def plan(refs, send, recv):
        x, y, c, chips = _place()
        me = 2 * x + y
        mine, theirs = [], []
        for w, ref in enumerate(refs):
            for j, (cx, cy) in enumerate(chips):
                def part(slot):
                    if w >= n_split:
                        return ref.at[slot]
                    hr = ref.shape[1] // 2
                    return ref.at[slot, pl.ds(c * hr, hr)]
                k = 3 * w + j
                mine.append(_remote(part(me), part(me), send.at[k], recv.at[k], (cx, cy, c)))
                theirs.append(_remote(part(2 * cx + cy), part(2 * cx + cy), send.at[k], recv.at[k], (cx, cy, c)))
        return mine, theirs
    return plan


def plan_shard_ici(j):
    def plan(refs, send, recv):
        _, _, c, chips = _place()
        own, land = refs
        hr = own.shape[0] // 2
        rows = pl.ds(c * hr, hr)
        cp = _remote(own.at[rows], land.at[rows], send.at[0], recv.at[0], (*chips[j], c))
        return [cp], [cp]
    return plan


def plan_shard_pass(refs, send, recv):
    x, y, c, _ = _place()
    land, = refs
    hr = land.shape[0] // 2
    half = lambda core: land.at[pl.ds(core * hr, hr)]
    return ([_remote(half(c), half(c), send.at[0], recv.at[0], (x, y, 1 - c))],
            [_remote(half(1 - c), half(1 - c), send.at[0], recv.at[0], (x, y, 1 - c))])


def plan_gather_pass(refs, send, recv):
    x, y, c, chips = _place()
    mine, theirs = [], []
    for w, ref in enumerate(refs):
        hr = ref.shape[1] // 2
        for j, (cx, cy) in enumerate(chips):
            half = lambda core: ref.at[2 * cx + cy, pl.ds(core * hr, hr)]
            k = 3 * w + j
            mine.append(_remote(half(c), half(c), send.at[k], recv.at[k], (x, y, 1 - c)))
            theirs.append(_remote(half(1 - c), half(1 - c), send.at[k], recv.at[k], (x, y, 1 - c)))
    return mine, theirs


def plan_swap(refs, send, recv):
    x, y, c, _ = _place()
    nw = len(refs) // 2
    mine = []
    for w in range(nw):
        hr = refs[w].shape[1] // 2
        mine.append(_remote(refs[w].at[:, pl.ds((1 - c) * hr, hr), :], refs[nw + w], send.at[w], recv.at[w],
                            (x, y, 1 - c)))
    return mine, mine


def plan_scatter(refs, send, recv):
    x, y, c, chips = _place()
    me = 2 * x + y
    nw = len(refs) // 2
    mine, theirs = [], []
    for w in range(nw):
        for j, (cx, cy) in enumerate(chips):
            k = 3 * w + j
            mine.append(_remote(refs[w].at[2 * cx + cy], refs[nw + w].at[me], send.at[k], recv.at[k], (cx, cy, c)))
            theirs.append(_remote(refs[w].at[me], refs[nw + w].at[2 * cx + cy], send.at[k], recv.at[k], (cx, cy, c)))
    return mine, theirs


def plan_join(refs, send, recv):
    x, y, c, _ = _place()
    mine, theirs = [], []
    for w, ref in enumerate(refs):
        hr = ref.shape[0] // 2
        half = lambda core: ref.at[pl.ds(core * hr, hr)]
        mine.append(_remote(half(c), half(c), send.at[w], recv.at[w], (x, y, 1 - c)))
        theirs.append(_remote(half(1 - c), half(1 - c), send.at[w], recv.at[w], (x, y, 1 - c)))
    return mine, theirs


def exchange(name, plan, arrays, n):
    na = len(arrays)

    def body(*refs):
        mine, theirs = plan(refs[:na], refs[2 * na], refs[2 * na + 1])
        for cp in mine:
            cp.start()
        for cp in theirs:
            cp.wait_recv()
        for cp in mine:
            cp.wait_send()

    return _call(body, name=name, in_specs=[_hbm()] * na, out_specs=[_hbm()] * na,
                 out_shape=[jax.ShapeDtypeStruct(a.shape, a.dtype) for a in arrays],
                 input_output_aliases={i: i for i in range(na)},
                 scratch_shapes=[pltpu.SemaphoreType.DMA((n,)), pltpu.SemaphoreType.DMA((n,))])(*arrays)


def exchange_start(name, groups, arrays, after=()):
    na, ng = len(arrays), len(groups)

    def body(*refs):
        for g, (plan, idx, _) in enumerate(groups):
            mine, _ = plan([refs[i] for i in idx], refs[na + 2 * g], refs[na + 2 * g + 1])
            for cp in mine:
                cp.start()
        refs[-1][...] = jnp.zeros_like(refs[-1])

    hbm = pl.BlockSpec(memory_space=pltpu.HBM)
    sem = pl.BlockSpec(memory_space=pltpu.SEMAPHORE)
    sem_types = [pltpu.SemaphoreType.DMA((n,)) for _, _, n in groups for _ in (0, 1)]
    outs = _call_after(body, after, name=name, in_specs=[hbm] * na,
                       out_specs=[sem] * (2 * ng) + [hbm] * na + [pl.BlockSpec(memory_space=pltpu.VMEM)],
                       out_shape=sem_types + [pltpu.HBM(a.shape, a.dtype) for a in arrays]
                       + [jax.ShapeDtypeStruct((SUBLANES, LANES), F32)],
                       input_output_aliases={i: i + 2 * ng for i in range(na)},
                       compiler_params=pltpu.CompilerParams(
                           has_side_effects=pltpu.SideEffectType.DATAFLOW_SIDE_EFFECTING))(
                               *[pltpu.with_memory_space_constraint(a, pltpu.HBM) for a in arrays])
    sems = [(outs[2 * g], outs[2 * g + 1]) for g in range(ng)]
    return sems, list(outs[2 * ng:-1]), outs[-1]


def exchange_wait(name, plan, sems, arrays, after):
    send_sems, recv_sems = sems
    na = len(arrays)

    def body(*refs):
        mine, theirs = plan(refs[:na], refs[na], refs[na + 1])
        for cp in mine:
            cp.wait_send()
        for cp in theirs:
            cp.wait_recv()

    hbm = pl.BlockSpec(memory_space=pltpu.HBM)
    sem = pl.BlockSpec(memory_space=pltpu.SEMAPHORE)
    return _call(body, name=name, in_specs=[hbm] * na + [sem, sem, _hbm()], out_specs=[hbm] * na,
                 out_shape=[pltpu.HBM(a.shape, a.dtype) for a in arrays],
                 input_output_aliases={i: i for i in range(na)},
                 compiler_params=pltpu.CompilerParams(
                     has_side_effects=pltpu.SideEffectType.DATAFLOW_SIDE_EFFECTING))(
                         *arrays, send_sems, recv_sems, after)


def plan_allgather_small(refs, send, recv):
    x, y, c, _ = _place()
    buf, = refs
    mine, theirs = [], []
    flips = [(fx, fy, fc) for fx in (0, 1) for fy in (0, 1) for fc in (0, 1)][1:]
    for k, (fx, fy, fc) in enumerate(flips):
        peer = (x ^ fx, y ^ fy, c ^ fc)
        slot = lambda px, py, pc: buf.at[4 * px + 2 * py + pc]
        mine.append(_remote(slot(x, y, c), slot(x, y, c), send.at[k], recv.at[k], peer))
        theirs.append(_remote(slot(*peer), slot(*peer), send.at[k], recv.at[k], peer))
    return mine, theirs


def add_sibling(grad, got, core, name):
    _, r, c = grad.shape
    hr = r // 2
    tr = _tile(hr, 128)
    nblk = hr // tr

    def body(core_ref, g_ref, o_ref, out_ref):
        out_ref[...] = (g_ref[...].astype(F32) + o_ref[...].astype(F32)).astype(BF16)

    grid_spec = pltpu.PrefetchScalarGridSpec(
        num_scalar_prefetch=1, grid=(N_CHIPS, nblk),
        in_specs=[pl.BlockSpec((None, tr, c), lambda t, i, core_ref: (t, core_ref[0] * nblk + i, 0)),
                  pl.BlockSpec((None, tr, c), lambda t, i, core_ref: (t, i, 0))],
        out_specs=pl.BlockSpec((None, tr, c), lambda t, i, core_ref: (t, i, 0)))
    return _call(body, name=name, grid_spec=grid_spec,
                 out_shape=jax.ShapeDtypeStruct((N_CHIPS, hr, c), BF16),
                 compiler_params=_params(("parallel", "parallel")))(core, grad, got)


def sum_chips(mine, owned, place, name):
    _, hr, c = mine.shape
    tr = _tile(hr, 128)
    nblk = hr // tr

    def body(place_ref, m_ref, o1_ref, o2_ref, o3_ref, out_ref):
        acc = m_ref[...].astype(F32)
        for o_ref in (o1_ref, o2_ref, o3_ref):
            acc = acc + o_ref[...].astype(F32)
        out_ref[...] = acc

    other = lambda k: pl.BlockSpec((None, tr, c), lambda i, place_ref: ((place_ref[0] + k) % N_CHIPS, i, 0))
    grid_spec = pltpu.PrefetchScalarGridSpec(
        num_scalar_prefetch=1, grid=(nblk,),
        in_specs=[other(0), other(1), other(2), other(3)],
        out_specs=pl.BlockSpec((tr, c), lambda i, place_ref: (place_ref[1] * nblk + i, 0)))
    return _call(body, name=name, grid_spec=grid_spec,
                 out_shape=jax.ShapeDtypeStruct((2 * hr, c), F32),
                 compiler_params=_params(("parallel",)))(place, mine, owned, owned, owned)


def sum_devices(gathered):
    _, rows, width = gathered.shape

    def body(g_ref, out_ref):
        acc = g_ref[0]
        for dev in range(1, 8):
            acc = acc + g_ref[dev]
        out_ref[...] = acc
        tail = acc[SUBLANES:]
        out_ref[SUBLANES:, :] = jnp.broadcast_to(jnp.sum(tail, axis=1, keepdims=True), tail.shape)

    return _call(body, name="sum_devices",
                 in_specs=[pl.BlockSpec(memory_space=pltpu.VMEM)],
                 out_specs=pl.BlockSpec(memory_space=pltpu.VMEM),
                 out_shape=jax.ShapeDtypeStruct((rows, width), F32))(gathered)


def _rope_tables(s):
    half = ROT_DIM // 2
    inv_freq = jnp.power(jnp.float32(ROPE_THETA), -jnp.arange(half, dtype=F32) * 2.0 / ROT_DIM)
    freq64 = jnp.concatenate([inv_freq, inv_freq, jnp.zeros((HEAD_DIM - ROT_DIM,), F32)])
    freq = jnp.concatenate([freq64, freq64])[None, :]
    dim = (jnp.arange(LANES) % HEAD_DIM)[None, :]
    ang = jnp.arange(s).astype(F32)[:, None] * freq
    cos, sin = jnp.cos(ang), jnp.sin(ang)
    return cos, jnp.where(dim < half, -sin, 0.0), jnp.where((dim >= half) & (dim < ROT_DIM), sin, 0.0)


def _pad_rows(a, rows):
    return jnp.pad(a, ((0, rows - a.shape[0]), (0, 0)))


def _pad_cols(a, cols):
    return jnp.pad(a, ((0, 0), (0, cols - a.shape[1])))


def kernel(x, p, norm_gain, w_in, q_norm_gain, k_norm_gain, attn_sinks, conv_w, w_out, ple_gate_norm_gain, w_ple_gate, b_ple_gate, w_ple_proj, ple_norm_gain, loss_target, m_norm_gain, m_w_in, m_q_norm_gain, m_k_norm_gain, m_attn_sinks, m_conv_w, m_w_out, m_ple_gate_norm_gain, m_w_ple_gate, m_b_ple_gate, m_w_ple_proj, m_ple_norm_gain, v_norm_gain, v_w_in, v_q_norm_gain, v_k_norm_gain, v_attn_sinks, v_conv_w, v_w_out, v_ple_gate_norm_gain, v_w_ple_gate, v_b_ple_gate, v_w_ple_proj, v_ple_norm_gain):
    x2, p2, tgt = x[0], p[0, 0], loss_target[0]
    s, d = x2.shape
    ple = p2.shape[1]
    aw = d // 2
    cw = d - aw
    nq = aw // HEAD_DIM
    sh = w_in.shape[2]
    in_w = N_CHIPS * sh
    dq = d // N_CHIPS
    cq = cw // N_CHIPS
    ga_off = (aw + 2 * KV_WIDTH) // COLT
    b_off = (2 * aw + 2 * KV_WIDTH) // COLT
    assert in_w == 2 * aw + 2 * KV_WIDTH + 4 * cw and aw % COLT == 0 and cw % COLT == 0
    assert s % BLOCK == 0 and sh % LANES == 0 and nq % (2 * N_KV_HEADS) == 0

    core = lax.axis_index("c").astype(jnp.int32).reshape(1)
    chip = 2 * lax.axis_index("x") + lax.axis_index("y")

    chip1 = chip.astype(jnp.int32).reshape(1)
    place = jnp.concatenate([chip1, core])
    w_in_own = cast_bf16(w_in[0], "cast_w_in")
    rest = [cast_into_slot(w_out[0], chip1, "cast_w_out"), cast_into_slot(w_ple_gate[0], chip1, "cast_w_pg"),
            cast_into_slot(w_ple_proj[0], chip1, "cast_w_pp"),
            lax.dynamic_update_slice(jnp.zeros((N_CHIPS, SUBLANES, cq), F32),
                                     _pad_rows(conv_w[0], SUBLANES)[None], (chip, 0, 0))]
    order = jnp.stack([chip, chip ^ 2, chip ^ 1, chip ^ 3]).astype(jnp.int32)
    wi_sems, wi_arrs, wi_token = exchange_start(
        "gather_wi_start", [(plan_shard_ici(j), [0, 1 + j], 1) for j in range(3)],
        [w_in_own] + [lax.empty((d, sh), BF16) for _ in range(3)])
    rest_sems, rest, rest_token = exchange_start(
        "gather_rest_start", [(plan_gather_ici(3), [0, 1, 2, 3], 12)], rest, after=(wi_token,))

    tm = _tile(s, 1024)
    tn = _tile(d, 1024)
    tk = _tile(d, 2048)
    ts = _tile(s, 2048)
    h = rms_fwd(x2, norm_gain, "rms_fwd_x", after=(rest_token,))
    tabs = _rope_tables(s)
    qg = jnp.tile(q_norm_gain, (1, LANES // HEAD_DIM))
    kg = jnp.tile(k_norm_gain, (1, LANES // HEAD_DIM))
    seg_i = jnp.arange(SEG) // HEAD_DIM
    segb = (seg_i[:, None] == seg_i[None, :]).astype(BF16)
    z = in_proj_part(h, wi_arrs[0], None, order, 0, in_w)
    w_shards = [wi_arrs[0]]
    for j in range(3):
        w_shards[0], land = exchange_wait("gather_wi_wait%d" % j, plan_shard_ici(j), wi_sems[j],
                                          [w_shards[0], wi_arrs[1 + j]], z)
        land, = exchange("gather_wi_pass%d" % j, plan_shard_pass, [land], 1)
        z = in_proj_part(h, land, z, order, 1 + j, in_w)
        w_shards.append(land)
    wo_all, wg_all, wp_all, conv_all = exchange_wait("gather_rest_wait", plan_gather_ici(3), rest_sems[0], rest, z)
    pass_sems, passed, pass_token = exchange_start(
        "gather_rest_pass_start", [(plan_gather_pass, [0, 1, 2], 9)], [wo_all, wg_all, wp_all])
    conv_full = conv_all.transpose(1, 0, 2).reshape(SUBLANES, cw)
    qs, ks, vb = qk_prep_fwd(z, tabs, qg, kg, segb, aw, after=(pass_token,))
    attn, mix = attn_fwd(qs, ks, vb, z, attn_sinks, aw, d, ga_off)
    mix = conv_fwd(z, conv_full, mix, aw, cw, b_off)
    wo_all, wg_all, wp_all = exchange_wait("gather_rest_pass_wait", plan_gather_pass, pass_sems[0], passed, mix)
    wo_full = wo_all.reshape(d, d)
    wg_full = wg_all.reshape(d, d)
    sq = lambda shape: dict(
        a_spec=pl.BlockSpec((tm, tk), lambda i, j, k: (i, k)),
        o_spec=pl.BlockSpec((tm, tn), lambda i, j, k: (i, j)),
        out_shape=jax.ShapeDtypeStruct(shape, F32))
    x1 = matmul(mix, wo_full, grid=(s // tm, d // tn, d // tk),
                b_spec=pl.BlockSpec((tk, tn), lambda i, j, k: (k, j)), dims=NN, name="mm_x1",
                res=x2, res_spec=pl.BlockSpec((tm, tn), lambda i, j, k: (i, j)), **sq((s, d)))
    hg = rms_fwd(x1, ple_gate_norm_gain, "rms_fwd_x1")
    gl = matmul(hg, wg_full, grid=(s // tm, d // tn, d // tk),
                b_spec=pl.BlockSpec((tk, tn), lambda i, j, k: (k, j)), dims=NN, name="mm_gl", **sq((s, d)))
    pq = d // N_CHIPS

    d_gl, d_pe, dy, acc_head = head_fwd_bwd(x1, gl, p2, wp_all, tgt, b_ple_gate, ple_norm_gain)
    d_hg = matmul(d_gl, wg_full, grid=(s // tm, d // tn, d // tk),
                  b_spec=pl.BlockSpec((tn, tk), lambda i, j, k: (j, k)), dims=NT, name="mm_d_hg", **sq((s, d)))
    wgrad = lambda a_cols, shape3, o_spec, b_cols, name, a, b, grid: matmul(
        a, b, grid=grid,
        a_spec=pl.BlockSpec((ts, a_cols), lambda i, j, k: (k, i)),
        b_spec=pl.BlockSpec((ts, b_cols), lambda i, j, k: (k, j)),
        o_spec=o_spec, out_shape=jax.ShapeDtypeStruct(shape3, BF16), dims=TN, name=name)
    g_wg = wgrad(tn, (d, d), pl.BlockSpec((tn, tn), lambda i, j, k: (i, j)), tn, "mm_g_wg", hg, d_gl,
                 (d // tn, d // tn, s // ts))
    g_wp = wgrad(ple, (N_CHIPS, ple, pq), pl.BlockSpec((None, ple, pq), lambda i, j, k: (j, 0, 0)), pq,
                 "mm_g_wp", p2, d_pe, (1, N_CHIPS, s // ts))
    d_x1, d_x1b, acc_g = rms_bwd_add(d_hg, x1, dy, ple_gate_norm_gain, "rms_bwd_x1", True)
    d_mix = matmul(d_x1b, wo_full, grid=(s // tm, d // tn, d // tk),
                   b_spec=pl.BlockSpec((tn, tk), lambda i, j, k: (j, k)), dims=NT, name="mm_d_mix", **sq((s, d)))
    g_wo = wgrad(tn, (d, d), pl.BlockSpec((tn, tn), lambda i, j, k: (i, j)), tn, "mm_g_wo", mix, d_x1b,
                 (d // tn, d // tn, s // ts))
    def reduce_start(tag, grads):
        n = len(grads)
        lands = [lax.empty((N_CHIPS, a.shape[1] // 2, a.shape[2]), BF16) for a in grads]
        swapped = exchange("swap_" + tag, plan_swap, list(grads) + lands, n)
        parts = [add_sibling(g, o, core, "add_sibling_%s%d" % (tag, i))
                 for i, (g, o) in enumerate(zip(swapped[:n], swapped[n:]))]
        return exchange_start("scatter_%s_start" % tag, [(plan_scatter, list(range(2 * n)), 3 * n)],
                              parts + [lax.empty(a.shape, BF16) for a in parts])

    def reduce_sum(tag, handle, after):
        sems, arrays, _ = handle
        n = len(arrays) // 2
        got = exchange_wait("scatter_%s_wait" % tag, plan_scatter, sems[0], arrays, after)
        return [sum_chips(pt, o, place, "sum_chips_%s%d" % (tag, i))
                for i, (pt, o) in enumerate(zip(got[:n], got[n:]))]

    early = reduce_start("early", [g_wo.reshape(N_CHIPS, dq, d), g_wg.reshape(N_CHIPS, dq, d), g_wp])
    d_o, dz = gate_bwd(d_mix, attn, z, aw, ga_off, after=(early[-1],))
    dqs, dks, dvs, acc_sink = attn_bwd(qs, ks, vb, d_o, attn_sinks, aw)
    dz, acc_qk = qk_prep_bwd(z, dqs, dks, dvs, tabs, qg, kg, segb, dz, aw)
    dz, acc_conv = conv_bwd(z, d_mix, conv_full, dz, aw, cw, b_off)
    join_sems, joining, join_token = exchange_start(
        "join_early_start", [(plan_join, [0, 1, 2], 3)], reduce_sum("early", early, dz))
    g_wi = matmul(h, dz, grid=(d // tn, N_CHIPS, s // ts),
                  a_spec=pl.BlockSpec((ts, tn), lambda i, j, k: (k, i)),
                  b_spec=pl.BlockSpec((ts, sh), lambda i, j, k: (k, j)),
                  o_spec=pl.BlockSpec((None, tn, sh), lambda i, j, k: (j, i, 0)),
                  out_shape=jax.ShapeDtypeStruct((N_CHIPS, d, sh), BF16), dims=TN, name="mm_g_wi",
                  after=(join_token,))
    full_wo, full_wg, full_wp = exchange_wait("join_early_wait", plan_join, join_sems[0], joining, g_wi)
    late = reduce_start("late", [g_wi])
    d_h = in_proj_bwd(dz, w_shards, order, d, after=(late[-1],))
    grad_x, acc_x = rms_bwd_add(d_h, x2, d_x1, norm_gain, "rms_bwd_x", False)

    wsm = max(d, cw)
    misc = jnp.concatenate([acc_qk[0:1, :HEAD_DIM], acc_qk[1:2, :HEAD_DIM], acc_sink[0:1, :nq]], axis=1)
    small = jnp.concatenate([
        _pad_cols(acc_x[0:1], wsm), _pad_cols(acc_g[0:1], wsm), _pad_cols(acc_head[0:1], wsm),
        _pad_cols(acc_head[1:2], wsm), _pad_cols(misc, wsm), _pad_cols(acc_conv[0:3], wsm),
        _pad_rows(_pad_cols(acc_head[2:3], wsm), SUBLANES)], axis=0)
    device = 2 * chip + lax.axis_index("c")
    small_sems, small_bufs, small_token = exchange_start(
        "small_start", [(plan_allgather_small, [0], 7)],
        [lax.dynamic_update_slice(jnp.zeros((8,) + small.shape, F32), small[None], (device, 0, 0))])
    full_wi, = exchange("join_late", plan_join, reduce_sum("late", late, small_token), 1)
    big = {}
    for nm, g, w, m, v in (("w_in", full_wi, w_in, m_w_in, v_w_in), ("w_out", full_wo, w_out, m_w_out, v_w_out),
                           ("w_ple_gate", full_wg, w_ple_gate, m_w_ple_gate, v_w_ple_gate),
                           ("w_ple_proj", full_wp, w_ple_proj, m_w_ple_proj, v_w_ple_proj)):
        big[nm] = [o[None] for o in adamw(g, w[0], m[0], v[0], "adamw_" + nm)]

    gathered, = exchange_wait("small_wait", plan_allgather_small, small_sems[0], small_bufs, big["w_in"][1])
    tot = sum_devices(gathered)
    loss = tot[SUBLANES, 0]

    def pack(vals):
        ng, pg, bg, eg, qgv, kgv, sk, cv = vals
        misc_v = jnp.concatenate([qgv, kgv, sk], axis=1)
        return jnp.concatenate([_pad_cols(ng, wsm), _pad_cols(pg, wsm), _pad_cols(bg, wsm), _pad_cols(eg, wsm),
                                _pad_cols(misc_v, wsm), _pad_cols(cv[0], wsm)], axis=0)

    g_small = jnp.concatenate(
        [tot[0:5], _pad_cols(lax.dynamic_slice(tot[5:8], (0, chip * cq), (3, cq)), wsm)], axis=0)
    w_small = pack((norm_gain, ple_gate_norm_gain, b_ple_gate, ple_norm_gain, q_norm_gain, k_norm_gain,
                    attn_sinks, conv_w))
    m_small = pack((m_norm_gain, m_ple_gate_norm_gain, m_b_ple_gate, m_ple_norm_gain, m_q_norm_gain,
                    m_k_norm_gain, m_attn_sinks, m_conv_w))
    v_small = pack((v_norm_gain, v_ple_gate_norm_gain, v_b_ple_gate, v_ple_norm_gain, v_q_norm_gain,
                    v_k_norm_gain, v_attn_sinks, v_conv_w))
    sm = adamw(g_small, w_small, m_small, v_small, "adamw_small")

    def unpack(a):
        return {"norm_gain": a[0:1, :d], "ple_gate_norm_gain": a[1:2, :d], "b_ple_gate": a[2:3, :d],
                "ple_norm_gain": a[3:4, :d], "q_norm_gain": a[4:5, :HEAD_DIM],
                "k_norm_gain": a[4:5, HEAD_DIM:2 * HEAD_DIM],
                "attn_sinks": a[4:5, 2 * HEAD_DIM:2 * HEAD_DIM + nq], "conv_w": a[5:8, :cq][None]}

    order = ("norm_gain", "w_in", "q_norm_gain", "k_norm_gain", "attn_sinks", "conv_w", "w_out",
             "ple_gate_norm_gain", "w_ple_gate", "b_ple_gate", "w_ple_proj", "ple_norm_gain")
    outs = [loss, grad_x[None]]
    for kind in range(4):
        table = unpack(sm[kind])
        for nm in order:
            outs.append(big[nm][kind] if nm in big else table[nm])
    return tuple(outs)
```

```python
import functools

import jax
import jax.numpy as jnp
from jax import lax
from jax.experimental import pallas as pl
from jax.experimental.pallas import tpu as pltpu

F32 = jnp.float32
BF16 = jnp.bfloat16
MESH = pl.DeviceIdType.MESH

HEAD_DIM = 64
N_KV_HEADS = 4
KV_WIDTH = N_KV_HEADS * HEAD_DIM
BLOCK = 128
ROT_DIM = 16
ROPE_THETA = 500000.0
EPS = 1e-6
NEG_INF = -1e30
N_CHIPS = 4
LANES = 128
SUBLANES = 8
HALO = 16
COLT = 512
SEG = 256
VMEM_LIMIT = 48 * 1024 * 1024

ADAM_LR = 0.001
ADAM_B1 = 0.9
ADAM_B2 = 0.999
ADAM_EPS = 1e-08
ADAM_WD = 0.01
ADAM_STEP = 10

NN = (((1,), (0,)), ((), ()))
NT = (((1,), (1,)), ((), ()))
TN = (((0,), (0,)), ((), ()))


def _call(body, **kw):
    return pl.pallas_call(body, **kw)


def _call_after(body, after, **kw):
    n_in, n_after = len(kw["in_specs"]), len(after)
    kw["in_specs"] = list(kw["in_specs"]) + [pl.BlockSpec(memory_space=pl.ANY)] * n_after

    def body_after(*refs):
        body(*refs[:n_in], *refs[n_in + n_after:])

    call = _call(body_after, **kw)
    return lambda *args: call(*args, *after)


def _params(sem):
    return pltpu.CompilerParams(dimension_semantics=sem, vmem_limit_bytes=VMEM_LIMIT)


def _tile(n, pref):
    return pref if n % pref == 0 else n


def _sigmoid(v):
    return 1.0 / (1.0 + jnp.exp(-v))


def _hbm():
    return pl.BlockSpec(memory_space=pl.ANY)


def cast_bf16(a, name):
    r, c = a.shape
    tr = _tile(r, 256)

    def body(a_ref, o_ref):
        o_ref[...] = a_ref[...].astype(BF16)

    return _call(body, name=name, grid=(r // tr,),
                 in_specs=[pl.BlockSpec((tr, c), lambda i: (i, 0))],
                 out_specs=pl.BlockSpec((tr, c), lambda i: (i, 0)),
                 out_shape=jax.ShapeDtypeStruct((r, c), BF16),
                 compiler_params=_params(("parallel",)))(a)


def cast_into_slot(a, chip, name):
    r, c = a.shape
    tr = _tile(r, 256)

    def body(chip_ref, a_ref, o_ref):
        o_ref[...] = a_ref[...].astype(BF16)

    grid_spec = pltpu.PrefetchScalarGridSpec(
        num_scalar_prefetch=1, grid=(r // tr,),
        in_specs=[pl.BlockSpec((tr, c), lambda i, chip_ref: (i, 0))],
        out_specs=pl.BlockSpec((None, tr, c), lambda i, chip_ref: (chip_ref[0], i, 0)))
    return _call(body, name=name, grid_spec=grid_spec,
                 out_shape=jax.ShapeDtypeStruct((N_CHIPS, r, c), BF16),
                 compiler_params=_params(("parallel",)))(chip, a)


def rms_fwd(x, gain, name, after=()):
    s, d = x.shape
    tm = _tile(s, 256)

    def body(x_ref, g_ref, o_ref):
        xf = x_ref[...]
        r = lax.rsqrt(jnp.mean(xf * xf, axis=-1, keepdims=True) + EPS)
        o_ref[...] = ((xf * r) * g_ref[...]).astype(BF16)

    return _call_after(body, after, name=name, grid=(s // tm,),
                       in_specs=[pl.BlockSpec((tm, d), lambda i: (i, 0)),
                                 pl.BlockSpec((1, d), lambda i: (0, 0))],
                       out_specs=pl.BlockSpec((tm, d), lambda i: (i, 0)),
                       out_shape=jax.ShapeDtypeStruct((s, d), BF16),
                       compiler_params=_params(("parallel",)))(x, gain)


def rms_bwd_add(dyn, xin, add, gain, name, want_bf16):
    s, d = xin.shape
    tm = _tile(s, 256)

    def body(dy_ref, x_ref, a_ref, g_ref, *outs):
        i = pl.program_id(0)
        dx_ref, acc_ref = outs[0], outs[-1]
        xf = x_ref[...]
        r = lax.rsqrt(jnp.mean(xf * xf, axis=-1, keepdims=True) + EPS)
        xhat = xf * r
        dyv = dy_ref[...]
        gd = dyv * g_ref[...]
        dx = a_ref[...] + r * (gd - xhat * jnp.mean(xhat * gd, axis=-1, keepdims=True))
        dx_ref[...] = dx
        if want_bf16:
            outs[1][...] = dx.astype(BF16)

        @pl.when(i == 0)
        def _():
            acc_ref[...] = jnp.zeros_like(acc_ref)

        acc_ref[0:1, :] += jnp.sum(dyv * xhat, axis=0, keepdims=True)

    row = pl.BlockSpec((tm, d), lambda i: (i, 0))
    out_specs = [row] + ([row] if want_bf16 else []) + [pl.BlockSpec((SUBLANES, d), lambda i: (0, 0))]
    out_shape = ([jax.ShapeDtypeStruct((s, d), F32)]
                 + ([jax.ShapeDtypeStruct((s, d), BF16)] if want_bf16 else [])
                 + [jax.ShapeDtypeStruct((SUBLANES, d), F32)])
    return _call(body, name=name, grid=(s // tm,),
                 in_specs=[row, row, row, pl.BlockSpec((1, d), lambda i: (0, 0))],
                 out_specs=out_specs, out_shape=out_shape,
                 compiler_params=_params(("arbitrary",)))(dyn, xin, add, gain)


def head_fwd_bwd(x1, gl, p, wp, tgt, bias, ple_gain):
    s, d = x1.shape
    tm = _tile(s, 128)

    def body(x1_ref, gl_ref, p_ref, wp_ref, t_ref, b_ref, g_ref, dgl_ref, dpe_ref, dy_ref, acc_ref):
        i = pl.program_id(0)
        gate = _sigmoid(gl_ref[...] + b_ref[...])
        pb = p_ref[...].astype(BF16)
        pev = jnp.concatenate([jnp.dot(pb, wp_ref[q], preferred_element_type=F32) for q in range(N_CHIPS)],
                              axis=1)
        r = lax.rsqrt(jnp.mean(pev * pev, axis=-1, keepdims=True) + EPS)
        pehat = pev * r
        gain = g_ref[...]
        e = pehat * gain
        diff = (x1_ref[...] + gate * e) - t_ref[...]
        dy = diff * (1.0 / d)
        dy_ref[...] = dy
        d_gl = (dy * e) * (gate * (1.0 - gate))
        dgl_ref[...] = d_gl.astype(BF16)
        d_e = dy * gate
        gd = d_e * gain
        d_pe = r * (gd - pehat * jnp.mean(pehat * gd, axis=-1, keepdims=True))
        dpe_ref[...] = d_pe.astype(BF16)

        @pl.when(i == 0)
        def _():
            acc_ref[...] = jnp.zeros_like(acc_ref)

        acc_ref[0:1, :] += jnp.sum(d_gl, axis=0, keepdims=True)
        acc_ref[1:2, :] += jnp.sum(d_e * pehat, axis=0, keepdims=True)
        acc_ref[2:3, :] += jnp.sum(diff * diff, axis=0, keepdims=True) * (0.5 / d)

    row = pl.BlockSpec((tm, d), lambda i: (i, 0))
    vec = pl.BlockSpec((1, d), lambda i: (0, 0))
    return _call(body, name="head_fwd_bwd", grid=(s // tm,),
                 in_specs=[row, row, pl.BlockSpec((tm, p.shape[1]), lambda i: (i, 0)),
                           pl.BlockSpec(wp.shape, lambda i: (0, 0, 0)), row, vec, vec],
                 out_specs=[row, row, row, pl.BlockSpec((SUBLANES, d), lambda i: (0, 0))],
                 out_shape=[jax.ShapeDtypeStruct((s, d), BF16), jax.ShapeDtypeStruct((s, d), BF16),
                            jax.ShapeDtypeStruct((s, d), F32), jax.ShapeDtypeStruct((SUBLANES, d), F32)],
                 compiler_params=_params(("arbitrary",)))(x1, gl, p, wp, tgt, bias, ple_gain)


def adamw(g, w, m, v, name):
    r, c = g.shape
    tr = _tile(r, 128)

    def body(g_ref, w_ref, m_ref, v_ref, go_ref, d_ref, mo_ref, vo_ref):
        gv = g_ref[...]
        mn = ADAM_B1 * m_ref[...] + (1.0 - ADAM_B1) * gv
        vn = ADAM_B2 * v_ref[...] + (1.0 - ADAM_B2) * (gv * gv)
        m_hat = mn / (1.0 - ADAM_B1 ** ADAM_STEP)
        v_hat = vn / (1.0 - ADAM_B2 ** ADAM_STEP)
        go_ref[...] = gv
        d_ref[...] = -ADAM_LR * (m_hat / (jnp.sqrt(v_hat) + ADAM_EPS) + ADAM_WD * w_ref[...])
        mo_ref[...] = mn
        vo_ref[...] = vn

    blk = pl.BlockSpec((tr, c), lambda i: (i, 0))
    shp = jax.ShapeDtypeStruct((r, c), F32)
    return _call(body, name=name, grid=(r // tr,), in_specs=[blk] * 4, out_specs=[blk] * 4,
                 out_shape=[shp] * 4, compiler_params=_params(("parallel",)))(g, w, m, v)


def matmul(a, b, *, grid, a_spec, b_spec, o_spec, out_shape, dims, name, res=None, res_spec=None, after=()):
    nk = grid[2]
    acc_shape = tuple(d for d in o_spec.block_shape if d is not None)

    def body(*refs):
        a_ref, b_ref = refs[:2]
        r_ref = refs[2] if res is not None else None
        o_ref = refs[3] if res is not None else refs[2]
        part = lax.dot_general(a_ref[...].astype(BF16), b_ref[...].astype(BF16), dims, preferred_element_type=F32)

        def finish(out):
            if res is not None:
                out = r_ref[...] + out
            o_ref[...] = out.astype(o_ref.dtype)

        if nk == 1:
            finish(part)
            return
        acc_ref = refs[-1]
        k = pl.program_id(2)

        @pl.when(k == 0)
        def _():
            acc_ref[...] = part

        @pl.when((k > 0) & (k < nk - 1))
        def _():
            acc_ref[...] += part

        @pl.when(k == nk - 1)
        def _():
            finish(acc_ref[...] + part)

    in_specs = [a_spec, b_spec] + ([res_spec] if res is not None else [])
    args = (a, b) + ((res,) if res is not None else ())
    return _call_after(body, after, name=name, grid=grid, in_specs=in_specs, out_specs=o_spec,
                       out_shape=out_shape, scratch_shapes=[pltpu.VMEM(acc_shape, F32)] if nk > 1 else [],
                       compiler_params=_params(("parallel", "parallel", "arbitrary")))(*args)


def in_proj_part(h, w, z_prev, order, q, in_w):
    s, d = h.shape
    sh = w.shape[1]
    tm = _tile(s, 1024)

    def body(order_ref, h_ref, w_ref, *rest):
        rest[-1][...] = jnp.dot(h_ref[...], w_ref[...], preferred_element_type=F32).astype(BF16)

    in_specs = [pl.BlockSpec((tm, d), lambda i, order_ref: (i, 0)),
                pl.BlockSpec((d, sh), lambda i, order_ref: (0, 0))]
    args = [order, h, w]
    if z_prev is not None:
        in_specs.append(_hbm())
        args.append(z_prev)
    grid_spec = pltpu.PrefetchScalarGridSpec(
        num_scalar_prefetch=1, grid=(s // tm,), in_specs=in_specs,
        out_specs=pl.BlockSpec((tm, sh), lambda i, order_ref: (i, order_ref[q])))
    return _call(body, name="mm_z%d" % q, grid_spec=grid_spec,
                 out_shape=jax.ShapeDtypeStruct((s, in_w), BF16),
                 input_output_aliases={3: 0} if z_prev is not None else {},
                 compiler_params=_params(("parallel",)))(*args)


def in_proj_bwd(dz, ws, order, d, after):
    s = dz.shape[0]
    sh = ws[0].shape[1]
    tm, tn = _tile(s, 512), _tile(d, 512)
    nq, n_after = len(ws), len(after)

    def body(order_ref, *refs):
        a_refs, b_refs, o_ref = refs[:nq], refs[nq:2 * nq], refs[2 * nq + n_after]
        acc = lax.dot_general(a_refs[0][...], b_refs[0][...], NT, preferred_element_type=F32)
        for q in range(1, nq):
            acc += lax.dot_general(a_refs[q][...], b_refs[q][...], NT, preferred_element_type=F32)
        o_ref[...] = acc

    a_spec = lambda q: pl.BlockSpec((tm, sh), functools.partial(lambda i, j, order_ref, q: (i, order_ref[q]), q=q))
    grid_spec = pltpu.PrefetchScalarGridSpec(
        num_scalar_prefetch=1, grid=(s // tm, d // tn),
        in_specs=[a_spec(q) for q in range(nq)]
        + [pl.BlockSpec((tn, sh), lambda i, j, order_ref: (j, 0))] * nq + [_hbm()] * n_after,
        out_specs=pl.BlockSpec((tm, tn), lambda i, j, order_ref: (i, j)))
    return _call(body, name="mm_d_h", grid_spec=grid_spec, out_shape=jax.ShapeDtypeStruct((s, d), F32),
                 compiler_params=_params(("parallel", "parallel")))(order, *([dz] * nq), *ws, *after)


def _seg_mean(sq, segb):
    parts = []
    for cgrp in range(sq.shape[1] // SEG):
        blk = sq[:, cgrp * SEG:(cgrp + 1) * SEG]
        hi = blk.astype(BF16)
        r1 = blk - hi.astype(F32)
        mid = r1.astype(BF16)
        lo = (r1 - mid.astype(F32)).astype(BF16)
        acc = jnp.dot(hi, segb, preferred_element_type=F32)
        acc += jnp.dot(mid, segb, preferred_element_type=F32)
        acc += jnp.dot(lo, segb, preferred_element_type=F32)
        parts.append(acc)
    out = parts[0] if len(parts) == 1 else jnp.concatenate(parts, axis=1)
    return out * (1.0 / HEAD_DIM)


def _rope(v, cos, sa, sb):
    parts = []
    for cgrp in range(v.shape[1] // LANES):
        blk = v[:, cgrp * LANES:(cgrp + 1) * LANES]
        parts.append(blk * cos + pltpu.roll(blk, LANES - 8, 1) * sa + pltpu.roll(blk, 8, 1) * sb)
    return parts[0] if len(parts) == 1 else jnp.concatenate(parts, axis=1)


def _rope_t(dv, cos, sa, sb):
    parts = []
    for cgrp in range(dv.shape[1] // LANES):
        blk = dv[:, cgrp * LANES:(cgrp + 1) * LANES]
        parts.append(blk * cos + pltpu.roll(blk * sa, 8, 1) + pltpu.roll(blk * sb, LANES - 8, 1))
    return parts[0] if len(parts) == 1 else jnp.concatenate(parts, axis=1)


def _tile_lanes(vec, width):
    reps = width // LANES
    return vec if reps == 1 else jnp.tile(vec, (1, reps))


def qk_prep_fwd(z, tabs, qg, kg, segb, aw, after=()):
    s = z.shape[0]
    tm = _tile(s, 256)
    wq = aw + 2 * KV_WIDTH

    def body(z_ref, cos_ref, sa_ref, sb_ref, qg_ref, kg_ref, seg_ref, qs_ref, ks_ref, vb_ref):
        zz = z_ref[...].astype(F32)
        q, k, v = zz[:, :aw], zz[:, aw:aw + KV_WIDTH], zz[:, aw + KV_WIDTH:]
        cos, sa, sb, segm = cos_ref[...], sa_ref[...], sb_ref[...], seg_ref[...]
        rq = lax.rsqrt(_seg_mean(q * q, segm) + EPS)
        qn = (q * rq) * _tile_lanes(qg_ref[...], aw)
        qs_ref[...] = (_rope(qn, cos, sa, sb) * (HEAD_DIM ** -0.5)).astype(BF16)
        rk = lax.rsqrt(_seg_mean(k * k, segm) + EPS)
        kn = (k * rk) * _tile_lanes(kg_ref[...], KV_WIDTH)
        ks_ref[...] = _rope(kn, cos, sa, sb).astype(BF16)
        vb_ref[...] = v.astype(BF16)

    tab = pl.BlockSpec((tm, LANES), lambda i: (i, 0))
    vec = pl.BlockSpec((1, LANES), lambda i: (0, 0))
    return _call_after(body, after, name="qk_prep_fwd", grid=(s // tm,),
                       in_specs=[pl.BlockSpec((tm, wq), lambda i: (i, 0)), tab, tab, tab, vec, vec,
                                 pl.BlockSpec((SEG, SEG), lambda i: (0, 0))],
                       out_specs=[pl.BlockSpec((tm, aw), lambda i: (i, 0)),
                                  pl.BlockSpec((tm, KV_WIDTH), lambda i: (i, 0)),
                                  pl.BlockSpec((tm, KV_WIDTH), lambda i: (i, 0))],
                       out_shape=[jax.ShapeDtypeStruct((s, aw), BF16), jax.ShapeDtypeStruct((s, KV_WIDTH), BF16),
                                  jax.ShapeDtypeStruct((s, KV_WIDTH), BF16)],
                       compiler_params=_params(("parallel",)))(z, *tabs, qg, kg, segb)


def qk_prep_bwd(z, dqs, dks, dvs, tabs, qg, kg, segb, dz, aw):
    s = z.shape[0]
    tm = _tile(s, 256)
    wq = aw + 2 * KV_WIDTH

    def body(z_ref, dq_ref, dk_ref, dv_ref, cos_ref, sa_ref, sb_ref, qg_ref, kg_ref, seg_ref, dz_in,
             dz_ref, acc_ref):
        i = pl.program_id(0)
        zz = z_ref[...].astype(F32)
        q, k = zz[:, :aw], zz[:, aw:aw + KV_WIDTH]
        cos, sa, sb, segm = cos_ref[...], sa_ref[...], sb_ref[...], seg_ref[...]

        def one(xv, dout, gvec, width):
            g = _tile_lanes(gvec, width)
            r = lax.rsqrt(_seg_mean(xv * xv, segm) + EPS)
            xhat = xv * r
            dn = _rope_t(dout, cos, sa, sb)
            gd = dn * g
            dx = r * (gd - xhat * _seg_mean(xhat * gd, segm))
            contrib = jnp.sum(dn * xhat, axis=0, keepdims=True)
            folded = contrib[:, :LANES]
            for cgrp in range(1, width // LANES):
                folded = folded + contrib[:, cgrp * LANES:(cgrp + 1) * LANES]
            return dx, folded + pltpu.roll(folded, HEAD_DIM, 1)

        dq, gq = one(q, dq_ref[...] * (HEAD_DIM ** -0.5), qg_ref[...], aw)
        dk, gk = one(k, dk_ref[...], kg_ref[...], KV_WIDTH)
        dz_ref[:, :aw] = dq.astype(BF16)
        dz_ref[:, aw:aw + KV_WIDTH] = dk.astype(BF16)
        dz_ref[:, aw + KV_WIDTH:] = dv_ref[...].astype(BF16)

        @pl.when(i == 0)
        def _():
            acc_ref[...] = jnp.zeros_like(acc_ref)

        acc_ref[0:1, :] += gq
        acc_ref[1:2, :] += gk

    tab = pl.BlockSpec((tm, LANES), lambda i: (i, 0))
    vec = pl.BlockSpec((1, LANES), lambda i: (0, 0))
    kvb = pl.BlockSpec((tm, KV_WIDTH), lambda i: (i, 0))
    return _call(body, name="qk_prep_bwd", grid=(s // tm,),
                 in_specs=[pl.BlockSpec((tm, wq), lambda i: (i, 0)),
                           pl.BlockSpec((tm, aw), lambda i: (i, 0)), kvb, kvb, tab, tab, tab, vec, vec,
                           pl.BlockSpec((SEG, SEG), lambda i: (0, 0)), _hbm()],
                 out_specs=[pl.BlockSpec((tm, wq), lambda i: (i, 0)),
                            pl.BlockSpec((SUBLANES, LANES), lambda i: (0, 0))],
                 out_shape=[jax.ShapeDtypeStruct(dz.shape, BF16), jax.ShapeDtypeStruct((SUBLANES, LANES), F32)],
                 input_output_aliases={10: 0},
                 compiler_params=_params(("arbitrary",)))(z, dqs, dks, dvs, *tabs, qg, kg, segb, dz)


def _placed(band, lane_idx):
    out = {}
    for kh in range(N_KV_HEADS):
        grp = band[:, (kh // 2) * LANES:(kh // 2 + 1) * LANES]
        for half in (0, 1):
            t = grp if half == kh % 2 else pltpu.roll(grp, HEAD_DIM, 1)
            keep = (lane_idx >= half * HEAD_DIM) & (lane_idx < (half + 1) * HEAD_DIM)
            out[kh, half] = jnp.where(keep, t, jnp.zeros_like(t))
    return out


def _softmax_with_sink(sc, valid, sink):
    sc = jnp.where(valid, sc, NEG_INF)
    m = jnp.maximum(jnp.max(sc, axis=-1, keepdims=True), sink)
    e = jnp.exp(sc - m)
    es = jnp.exp(sink - m)
    den = jnp.sum(e, axis=-1, keepdims=True) + es
    return e / den, es / den


def _stack_halves(placed, kh):
    return jnp.concatenate([placed[kh, 0], placed[kh, 1]], axis=0)


def _valid_mask(n):
    r_i = lax.broadcasted_iota(jnp.int32, (BLOCK, 2 * BLOCK), 0)
    j_i = lax.broadcasted_iota(jnp.int32, (BLOCK, 2 * BLOCK), 1)
    return (j_i > r_i) & (j_i <= r_i + BLOCK) & ((n > 0) | (j_i >= BLOCK))


def attn_fwd(qs, ks, vb, z, sinks, aw, d, ga_off):
    s = qs.shape[0]
    nb = s // BLOCK
    nq = aw // HEAD_DIM
    grp_sz = nq // N_KV_HEADS
    n_ga = aw // COLT

    def body(q_ref, ko_ref, kp_ref, vo_ref, vp_ref, *rest):
        ga_refs = rest[:n_ga]
        sink_ref, attn_ref, mix_ref = rest[n_ga:]
        n = pl.program_id(0)
        lane_idx = lax.broadcasted_iota(jnp.int32, (2 * BLOCK, LANES), 1)
        kpl = _placed(jnp.concatenate([kp_ref[...], ko_ref[...]], axis=0), lane_idx)
        vpl = _placed(jnp.concatenate([vp_ref[...], vo_ref[...]], axis=0), lane_idx)
        valid = _valid_mask(n)
        groups = range(nq // 2)
        kcat = [_stack_halves(kpl, kh) for kh in range(N_KV_HEADS)]
        vcat = [_stack_halves(vpl, kh) for kh in range(N_KV_HEADS)]
        scs = [lax.dot_general(q_ref[:, c * LANES:(c + 1) * LANES], kcat[2 * c // grp_sz], NT,
                               preferred_element_type=F32) for c in groups]
        probs = [jnp.concatenate(
            [_softmax_with_sink(scs[c][:, half * 2 * BLOCK:(half + 1) * 2 * BLOCK], valid,
                                sink_ref[0, 2 * c + half])[0].astype(BF16) for half in (0, 1)], axis=1)
                 for c in groups]
        for c in groups:
            acc = jnp.dot(probs[c], vcat[2 * c // grp_sz], preferred_element_type=F32)
            attn_ref[:, c * LANES:(c + 1) * LANES] = acc
            col = c * LANES
            ga = ga_refs[col // COLT][:, col % COLT:col % COLT + LANES].astype(F32)
            mix_ref[:, c * LANES:(c + 1) * LANES] = (acc * (ga * _sigmoid(ga))).astype(BF16)

    own = lambda n: (n, 0)
    prev = lambda n: (jnp.maximum(n - 1, 0), 0)
    kvs = lambda imap: pl.BlockSpec((BLOCK, KV_WIDTH), imap)
    ga_specs = [pl.BlockSpec((BLOCK, COLT), functools.partial(lambda n, j: (n, ga_off + j), j=j))
                for j in range(n_ga)]
    return _call(body, name="attn_fwd", grid=(nb,),
                 in_specs=[pl.BlockSpec((BLOCK, aw), own), kvs(own), kvs(prev), kvs(own), kvs(prev)]
                 + ga_specs + [pl.BlockSpec(memory_space=pltpu.SMEM)],
                 out_specs=[pl.BlockSpec((BLOCK, aw), own), pl.BlockSpec((BLOCK, aw), own)],
                 out_shape=[jax.ShapeDtypeStruct((s, aw), F32), jax.ShapeDtypeStruct((s, d), BF16)],
                 compiler_params=_params(("parallel",)))(qs, ks, ks, vb, vb, *([z] * n_ga), sinks)


def gate_bwd(d_mix, attn, z, aw, ga_off, after=()):
    s, in_w = z.shape
    tm = _tile(s, 256)

    def body(dm_ref, at_ref, ga_ref, do_ref, dz_ref):
        ga = ga_ref[...].astype(F32)
        sig = _sigmoid(ga)
        dm = dm_ref[...]
        do_ref[...] = (dm * (ga * sig)).astype(BF16)
        dz_ref[...] = ((dm * at_ref[...]) * (sig * (1.0 + ga * (1.0 - sig)))).astype(BF16)

    blk = pl.BlockSpec((tm, COLT), lambda i, j: (i, j))
    gab = pl.BlockSpec((tm, COLT), lambda i, j: (i, ga_off + j))
    return _call_after(body, after, name="gate_bwd", grid=(s // tm, aw // COLT),
                       in_specs=[blk, blk, gab], out_specs=[blk, gab],
                       out_shape=[jax.ShapeDtypeStruct((s, aw), BF16), jax.ShapeDtypeStruct((s, in_w), BF16)],
                       compiler_params=_params(("parallel", "parallel")))(d_mix, attn, z)


def attn_bwd(qs, ks, vb, d_o, sinks, aw):
    s = qs.shape[0]
    nb = s // BLOCK
    nq = aw // HEAD_DIM
    grp_sz = nq // N_KV_HEADS

    def body(q_ref, ko_ref, kp_ref, vo_ref, vp_ref, do_ref, sink_ref, dq_ref, dk_ref, dv_ref, ds_ref,
             ck_ref, cv_ref):
        n = pl.program_id(0)

        @pl.when(n == 0)
        def _():
            ds_ref[...] = jnp.zeros_like(ds_ref)
            dk_ref[...] = jnp.zeros_like(dk_ref)
            dv_ref[...] = jnp.zeros_like(dv_ref)

        @pl.when(n < nb)
        def _():
            lane_idx = lax.broadcasted_iota(jnp.int32, (2 * BLOCK, LANES), 1)
            lane_row = lax.broadcasted_iota(jnp.int32, (1, LANES), 1)
            kpl = _placed(jnp.concatenate([kp_ref[...], ko_ref[...]], axis=0), lane_idx)
            vpl = _placed(jnp.concatenate([vp_ref[...], vo_ref[...]], axis=0), lane_idx)
            valid = _valid_mask(n)
            ds_row = jnp.zeros((1, LANES), F32)
            wide = 2 * BLOCK
            groups = range(nq // 2)
            per_kv = grp_sz // 2
            kcat = [_stack_halves(kpl, kh) for kh in range(N_KV_HEADS)]
            vcat = [_stack_halves(vpl, kh) for kh in range(N_KV_HEADS)]
            qg = [q_ref[:, c * LANES:(c + 1) * LANES] for c in groups]
            dog = [do_ref[:, c * LANES:(c + 1) * LANES] for c in groups]
            scs = [lax.dot_general(qg[c], kcat[c // per_kv], NT, preferred_element_type=F32) for c in groups]
            dps = [lax.dot_general(dog[c], vcat[c // per_kv], NT, preferred_element_type=F32) for c in groups]
            ds_pairs, p_pairs = [], []
            for c in groups:
                probs, dss = [], []
                for half in (0, 1):
                    h = 2 * c + half
                    cols = slice(half * wide, (half + 1) * wide)
                    p, p_sink = _softmax_with_sink(scs[c][:, cols], valid, sink_ref[0, h])
                    delta = jnp.sum(p * dps[c][:, cols], axis=-1, keepdims=True)
                    dss.append((p * (dps[c][:, cols] - delta)).astype(BF16))
                    probs.append(p.astype(BF16))
                    dsink = -jnp.sum(p_sink * delta, axis=0, keepdims=True)
                    ds_row += jnp.where(lane_row == h, dsink, 0.0)
                ds_pairs.append(jnp.concatenate(dss, axis=1))
                p_pairs.append(jnp.concatenate(probs, axis=1))
            for c in groups:
                dq_ref[:, c * LANES:(c + 1) * LANES] = jnp.dot(ds_pairs[c], kcat[c // per_kv],
                                                               preferred_element_type=F32)
            dkts = [lax.dot_general(qg[c], ds_pairs[c], TN, preferred_element_type=F32) for c in groups]
            dvts = [lax.dot_general(dog[c], p_pairs[c], TN, preferred_element_type=F32) for c in groups]
            dkt, dvt = [], []
            for kh in range(N_KV_HEADS):
                for parts, out in ((dkts, dkt), (dvts, dvt)):
                    tot = parts[kh * per_kv]
                    for c in range(kh * per_kv + 1, (kh + 1) * per_kv):
                        tot = tot + parts[c]
                    out.append(tot[:HEAD_DIM, :wide] + tot[HEAD_DIM:, wide:])
            ds_ref[0:1, :] += ds_row
            back = lambda t: jnp.concatenate(
                [jnp.concatenate(t[2 * g:2 * g + 2], axis=0).T for g in range(N_KV_HEADS // 2)], axis=1)
            dk_band, dv_band = back(dkt), back(dvt)

            @pl.when(n > 0)
            def _():
                dk_ref[...] = ck_ref[...] + dk_band[:BLOCK]
                dv_ref[...] = cv_ref[...] + dv_band[:BLOCK]

            ck_ref[...] = dk_band[BLOCK:]
            cv_ref[...] = dv_band[BLOCK:]

        @pl.when(n == nb)
        def _():
            dk_ref[...] = ck_ref[...]
            dv_ref[...] = cv_ref[...]

    own = lambda n: (jnp.minimum(n, nb - 1), 0)
    prev = lambda n: (jnp.clip(n - 1, 0, nb - 1), 0)
    done = lambda n: (jnp.maximum(n - 1, 0), 0)
    kvs = lambda imap: pl.BlockSpec((BLOCK, KV_WIDTH), imap)
    return _call(body, name="attn_bwd", grid=(nb + 1,),
                 in_specs=[pl.BlockSpec((BLOCK, aw), own), kvs(own), kvs(prev), kvs(own), kvs(prev),
                           pl.BlockSpec((BLOCK, aw), own), pl.BlockSpec(memory_space=pltpu.SMEM)],
                 out_specs=[pl.BlockSpec((BLOCK, aw), own), kvs(done), kvs(done),
                            pl.BlockSpec((SUBLANES, LANES), lambda n: (0, 0))],
                 out_shape=[jax.ShapeDtypeStruct((s, aw), F32), jax.ShapeDtypeStruct((s, KV_WIDTH), F32),
                            jax.ShapeDtypeStruct((s, KV_WIDTH), F32), jax.ShapeDtypeStruct((SUBLANES, LANES), F32)],
                 scratch_shapes=[pltpu.VMEM((BLOCK, KV_WIDTH), F32), pltpu.VMEM((BLOCK, KV_WIDTH), F32)],
                 compiler_params=_params(("arbitrary",)))(qs, ks, ks, vb, vb, d_o, sinks)


def conv_fwd(z, conv_w, mix, aw, cw, b_off):
    s = z.shape[0]
    tm = _tile(s, 256)
    nseg = cw // COLT
    hb = tm // HALO

    def body(b_ref, c_ref, h_ref, g_ref, cp_ref, hp_ref, w_ref, mix_in, mix_ref):
        i = pl.program_id(0)
        f32 = lambda ref: ref[...].astype(F32)
        u = f32(c_ref) * f32(h_ref)
        up = jnp.where(i > 0, f32(cp_ref) * f32(hp_ref), 0.0)
        ext = jnp.concatenate([up, u], axis=0)
        um1 = pltpu.roll(ext, 1, 0)[HALO:]
        um2 = pltpu.roll(ext, 2, 0)[HALO:]
        w = w_ref[...]
        cv = w[0:1] * um2 + w[1:2] * um1 + w[2:3] * u
        g = f32(g_ref)
        mix_ref[...] = ((f32(b_ref) * cv) * (g * _sigmoid(g))).astype(BF16)

    seg = lambda k: pl.BlockSpec((tm, COLT), functools.partial(lambda i, j, k: (i, b_off + k * nseg + j), k=k))
    halo = lambda k: pl.BlockSpec(
        (HALO, COLT), functools.partial(lambda i, j, k: (jnp.maximum(i * hb - 1, 0), b_off + k * nseg + j), k=k))
    return _call(body, name="conv_fwd", grid=(s // tm, nseg),
                 in_specs=[seg(0), seg(1), seg(2), seg(3), halo(1), halo(2),
                           pl.BlockSpec((SUBLANES, COLT), lambda i, j: (0, j)), _hbm()],
                 out_specs=pl.BlockSpec((tm, COLT), lambda i, j: (i, aw // COLT + j)),
                 out_shape=jax.ShapeDtypeStruct(mix.shape, BF16),
                 input_output_aliases={7: 0},
                 compiler_params=_params(("parallel", "parallel")))(z, z, z, z, z, z, conv_w, mix)


def conv_bwd(z, d_mix, conv_w, dz, aw, cw, b_off):
    s = z.shape[0]
    tm = _tile(s, 256)
    nseg = cw // COLT
    hb = tm // HALO
    n_row = s // tm
    last_h = s // HALO - 1
    n_step = nseg * n_row

    def body(b_ref, c_ref, h_ref, g_ref, cp_ref, hp_ref, bn_ref, gn_ref, dm_ref, dmn_ref, w_ref, dz_in,
             dz_ref, acc_ref, stash_ref, sems):
        j = pl.program_id(0)
        i = pl.program_id(1)
        step = j * n_row + i
        slot = step % 2

        def copies(from_slot, at_step):
            jj, ii = at_step // n_row, at_step % n_row
            rows = pl.ds(pl.multiple_of(ii * tm, tm), tm)
            return [pltpu.make_async_copy(
                stash_ref.at[from_slot, q],
                dz_ref.at[rows, pl.ds(pl.multiple_of((b_off + q * nseg + jj) * COLT, COLT), COLT)],
                sems.at[from_slot, q]) for q in range(4)]

        @pl.when(step >= 2)
        def _():
            for cp in copies(slot, step - 2):
                cp.wait()

        f32 = lambda ref: ref[...].astype(F32)
        cc, hh, bb, g = f32(c_ref), f32(h_ref), f32(b_ref), f32(g_ref)
        w = w_ref[...]
        u = cc * hh
        up = jnp.where(i > 0, f32(cp_ref) * f32(hp_ref), 0.0)
        ext = jnp.concatenate([up, u], axis=0)
        um1 = pltpu.roll(ext, 1, 0)[HALO:]
        um2 = pltpu.roll(ext, 2, 0)[HALO:]
        cv = w[0:1] * um2 + w[1:2] * um1 + w[2:3] * u
        sig = _sigmoid(g)
        sg = g * sig
        dm = dm_ref[...]
        d_cv = (dm * bb) * sg
        gn = f32(gn_ref)
        d_cv_next = jnp.where(i < n_row - 1, (dmn_ref[...] * f32(bn_ref)) * (gn * _sigmoid(gn)), 0.0)
        ext2 = jnp.concatenate([d_cv, d_cv_next], axis=0)
        dp1 = pltpu.roll(ext2, tm + HALO - 1, 0)[:tm]
        dp2 = pltpu.roll(ext2, tm + HALO - 2, 0)[:tm]
        d_u = w[2:3] * d_cv + w[1:2] * dp1 + w[0:1] * dp2
        stash_ref[slot, 0] = ((dm * cv) * sg).astype(BF16)
        stash_ref[slot, 1] = (d_u * hh).astype(BF16)
        stash_ref[slot, 2] = (d_u * cc).astype(BF16)
        stash_ref[slot, 3] = (((dm * bb) * cv) * (sig * (1.0 + g * (1.0 - sig)))).astype(BF16)
        for cp in copies(slot, step):
            cp.start()

        @pl.when(i == 0)
        def _():
            acc_ref[...] = jnp.zeros_like(acc_ref)

        acc_ref[0:1, :] += jnp.sum(d_cv * um2, axis=0, keepdims=True)
        acc_ref[1:2, :] += jnp.sum(d_cv * um1, axis=0, keepdims=True)
        acc_ref[2:3, :] += jnp.sum(d_cv * u, axis=0, keepdims=True)

        @pl.when(step == n_step - 1)
        def _():
            if n_step >= 2:
                for cp in copies(1 - slot, step - 1):
                    cp.wait()
            for cp in copies(slot, step):
                cp.wait()

    seg = lambda q: pl.BlockSpec((tm, COLT), functools.partial(lambda j, i, q: (i, b_off + q * nseg + j), q=q))
    halo_p = lambda q: pl.BlockSpec(
        (HALO, COLT),
        functools.partial(lambda j, i, q: (jnp.maximum(i * hb - 1, 0), b_off + q * nseg + j), q=q))
    halo_n = lambda q: pl.BlockSpec(
        (HALO, COLT),
        functools.partial(lambda j, i, q: (jnp.minimum((i + 1) * hb, last_h), b_off + q * nseg + j), q=q))
    return _call(body, name="conv_bwd", grid=(nseg, n_row),
                 in_specs=[seg(0), seg(1), seg(2), seg(3), halo_p(1), halo_p(2), halo_n(0), halo_n(3),
                           pl.BlockSpec((tm, COLT), lambda j, i: (i, aw // COLT + j)),
                           pl.BlockSpec((HALO, COLT),
                                        lambda j, i: (jnp.minimum((i + 1) * hb, last_h), aw // COLT + j)),
                           pl.BlockSpec((SUBLANES, COLT), lambda j, i: (0, j)), _hbm()],
                 out_specs=[_hbm(), pl.BlockSpec((SUBLANES, COLT), lambda j, i: (0, j))],
                 out_shape=[jax.ShapeDtypeStruct(dz.shape, BF16), jax.ShapeDtypeStruct((SUBLANES, cw), F32)],
                 input_output_aliases={11: 0},
                 scratch_shapes=[pltpu.VMEM((2, 4, tm, COLT), BF16), pltpu.SemaphoreType.DMA((2, 4))],
                 compiler_params=_params(("arbitrary", "arbitrary")))(
                     z, z, z, z, z, z, z, z, d_mix, d_mix, conv_w, dz)


def _place():
    x, y, c = lax.axis_index("x"), lax.axis_index("y"), lax.axis_index("c")
    chips = [(1 - x, y), (x, 1 - y), (1 - x, 1 - y)]
    return x, y, c, chips


def _remote(src, dst, send_sem, recv_sem, device):
    return pltpu.make_async_remote_copy(src_ref=src, dst_ref=dst, send_sem=send_sem, recv_sem=recv_sem,
                                        device_id=device, device_id_type=MESH)


def plan_gather_ici(n_split):
    def plan(refs, send, recv):
        x, y, c, chips = _place()
        me = 2 * x + y
        mine, theirs = [], []
        for w, ref in enumerate(refs):
            for j, (cx, cy) in enumerate(chips):
                def part(slot):
                    if w >= n_split:
                        return ref.at[slot]
                    hr = ref.shape[1] // 2
                    return ref.at[slot, pl.ds(c * hr, hr)]
                k = 3 * w + j
                mine.append(_remote(part(me), part(me), send.at[k], recv.at[k], (cx, cy, c)))
                theirs.append(_remote(part(2 * cx + cy), part(2 * cx + cy), send.at[k], recv.at[k], (cx, cy, c)))
        return mine, theirs
    return plan


def plan_shard_ici(j):
    def plan(refs, send, recv):
        _, _, c, chips = _place()
        own, land = refs
        hr = own.shape[0] // 2
        rows = pl.ds(c * hr, hr)
        cp = _remote(own.at[rows], land.at[rows], send.at[0], recv.at[0], (*chips[j], c))
        return [cp], [cp]
    return plan


def plan_shard_pass(refs, send, recv):
    x, y, c, _ = _place()
    land, = refs
    hr = land.shape[0] // 2
    half = lambda core: land.at[pl.ds(core * hr, hr)]
    return ([_remote(half(c), half(c), send.at[0], recv.at[0], (x, y, 1 - c))],
            [_remote(half(1 - c), half(1 - c), send.at[0], recv.at[0], (x, y, 1 - c))])


def plan_gather_pass(refs, send, recv):
    x, y, c, chips = _place()
    mine, theirs = [], []
    for w, ref in enumerate(refs):
        hr = ref.shape[1] // 2
        for j, (cx, cy) in enumerate(chips):
            half = lambda core: ref.at[2 * cx + cy, pl.ds(core * hr, hr)]
            k = 3 * w + j
            mine.append(_remote(half(c), half(c), send.at[k], recv.at[k], (x, y, 1 - c)))
            theirs.append(_remote(half(1 - c), half(1 - c), send.at[k], recv.at[k], (x, y, 1 - c)))
    return mine, theirs


def plan_swap(refs, send, recv):
    x, y, c, _ = _place()
    nw = len(refs) // 2
    mine = []
    for w in range(nw):
        hr = refs[w].shape[1] // 2
        mine.append(_remote(refs[w].at[:, pl.ds((1 - c) * hr, hr), :], refs[nw + w], send.at[w], recv.at[w],
                            (x, y, 1 - c)))
    return mine, mine


def plan_scatter(refs, send, recv):
    x, y, c, chips = _place()
    me = 2 * x + y
    nw = len(refs) // 2
    mine, theirs = [], []
    for w in range(nw):
        for j, (cx, cy) in enumerate(chips):
            k = 3 * w + j
            mine.append(_remote(refs[w].at[2 * cx + cy], refs[nw + w].at[me], send.at[k], recv.at[k], (cx, cy, c)))
            theirs.append(_remote(refs[w].at[me], refs[nw + w].at[2 * cx + cy], send.at[k], recv.at[k], (cx, cy, c)))
    return mine, theirs


def plan_join(refs, send, recv):
    x, y, c, _ = _place()
    mine, theirs = [], []
    for w, ref in enumerate(refs):
        hr = ref.shape[0] // 2
        half = lambda core: ref.at[pl.ds(core * hr, hr)]
        mine.append(_remote(half(c), half(c), send.at[w], recv.at[w], (x, y, 1 - c)))
        theirs.append(_remote(half(1 - c), half(1 - c), send.at[w], recv.at[w], (x, y, 1 - c)))
    return mine, theirs


def exchange(name, plan, arrays, n):
    na = len(arrays)

    def body(*refs):
        mine, theirs = plan(refs[:na], refs[2 * na], refs[2 * na + 1])
        for cp in mine:
            cp.start()
        for cp in theirs:
            cp.wait_recv()
        for cp in mine:
            cp.wait_send()

    return _call(body, name=name, in_specs=[_hbm()] * na, out_specs=[_hbm()] * na,
                 out_shape=[jax.ShapeDtypeStruct(a.shape, a.dtype) for a in arrays],
                 input_output_aliases={i: i for i in range(na)},
                 scratch_shapes=[pltpu.SemaphoreType.DMA((n,)), pltpu.SemaphoreType.DMA((n,))])(*arrays)


def exchange_start(name, groups, arrays, after=()):
    na, ng = len(arrays), len(groups)

    def body(*refs):
        for g, (plan, idx, _) in enumerate(groups):
            mine, _ = plan([refs[i] for i in idx], refs[na + 2 * g], refs[na + 2 * g + 1])
            for cp in mine:
                cp.start()
        refs[-1][...] = jnp.zeros_like(refs[-1])

    hbm = pl.BlockSpec(memory_space=pltpu.HBM)
    sem = pl.BlockSpec(memory_space=pltpu.SEMAPHORE)
    sem_types = [pltpu.SemaphoreType.DMA((n,)) for _, _, n in groups for _ in (0, 1)]
    outs = _call_after(body, after, name=name, in_specs=[hbm] * na,
                       out_specs=[sem] * (2 * ng) + [hbm] * na + [pl.BlockSpec(memory_space=pltpu.VMEM)],
                       out_shape=sem_types + [pltpu.HBM(a.shape, a.dtype) for a in arrays]
                       + [jax.ShapeDtypeStruct((SUBLANES, LANES), F32)],
                       input_output_aliases={i: i + 2 * ng for i in range(na)},
                       compiler_params=pltpu.CompilerParams(
                           has_side_effects=pltpu.SideEffectType.DATAFLOW_SIDE_EFFECTING))(
                               *[pltpu.with_memory_space_constraint(a, pltpu.HBM) for a in arrays])
    sems = [(outs[2 * g], outs[2 * g + 1]) for g in range(ng)]
    return sems, list(outs[2 * ng:-1]), outs[-1]


def exchange_wait(name, plan, sems, arrays, after):
    send_sems, recv_sems = sems
    na = len(arrays)

    def body(*refs):
        mine, theirs = plan(refs[:na], refs[na], refs[na + 1])
        for cp in mine:
            cp.wait_send()
        for cp in theirs:
            cp.wait_recv()

    hbm = pl.BlockSpec(memory_space=pltpu.HBM)
    sem = pl.BlockSpec(memory_space=pltpu.SEMAPHORE)
    return _call(body, name=name, in_specs=[hbm] * na + [sem, sem, _hbm()], out_specs=[hbm] * na,
                 out_shape=[pltpu.HBM(a.shape, a.dtype) for a in arrays],
                 input_output_aliases={i: i for i in range(na)},
                 compiler_params=pltpu.CompilerParams(
                     has_side_effects=pltpu.SideEffectType.DATAFLOW_SIDE_EFFECTING))(
                         *arrays, send_sems, recv_sems, after)


def plan_allgather_small(refs, send, recv):
    x, y, c, _ = _place()
    buf, = refs
    mine, theirs = [], []
    flips = [(fx, fy, fc) for fx in (0, 1) for fy in (0, 1) for fc in (0, 1)][1:]
    for k, (fx, fy, fc) in enumerate(flips):
        peer = (x ^ fx, y ^ fy, c ^ fc)
        slot = lambda px, py, pc: buf.at[4 * px + 2 * py + pc]
        mine.append(_remote(slot(x, y, c), slot(x, y, c), send.at[k], recv.at[k], peer))
        theirs.append(_remote(slot(*peer), slot(*peer), send.at[k], recv.at[k], peer))
    return mine, theirs


def add_sibling(grad, got, core, name):
    _, r, c = grad.shape
    hr = r // 2
    tr = _tile(hr, 128)
    nblk = hr // tr

    def body(core_ref, g_ref, o_ref, out_ref):
        out_ref[...] = (g_ref[...].astype(F32) + o_ref[...].astype(F32)).astype(BF16)

    grid_spec = pltpu.PrefetchScalarGridSpec(
        num_scalar_prefetch=1, grid=(N_CHIPS, nblk),
        in_specs=[pl.BlockSpec((None, tr, c), lambda t, i, core_ref: (t, core_ref[0] * nblk + i, 0)),
                  pl.BlockSpec((None, tr, c), lambda t, i, core_ref: (t, i, 0))],
        out_specs=pl.BlockSpec((None, tr, c), lambda t, i, core_ref: (t, i, 0)))
    return _call(body, name=name, grid_spec=grid_spec,
                 out_shape=jax.ShapeDtypeStruct((N_CHIPS, hr, c), BF16),
                 compiler_params=_params(("parallel", "parallel")))(core, grad, got)


def sum_chips(mine, owned, place, name):
    _, hr, c = mine.shape
    tr = _tile(hr, 128)
    nblk = hr // tr

    def body(place_ref, m_ref, o1_ref, o2_ref, o3_ref, out_ref):
        acc = m_ref[...].astype(F32)
        for o_ref in (o1_ref, o2_ref, o3_ref):
            acc = acc + o_ref[...].astype(F32)
        out_ref[...] = acc

    other = lambda k: pl.BlockSpec((None, tr, c), lambda i, place_ref: ((place_ref[0] + k) % N_CHIPS, i, 0))
    grid_spec = pltpu.PrefetchScalarGridSpec(
        num_scalar_prefetch=1, grid=(nblk,),
        in_specs=[other(0), other(1), other(2), other(3)],
        out_specs=pl.BlockSpec((tr, c), lambda i, place_ref: (place_ref[1] * nblk + i, 0)))
    return _call(body, name=name, grid_spec=grid_spec,
                 out_shape=jax.ShapeDtypeStruct((2 * hr, c), F32),
                 compiler_params=_params(("parallel",)))(place, mine, owned, owned, owned)


def sum_devices(gathered):
    _, rows, width = gathered.shape

    def body(g_ref, out_ref):
        acc = g_ref[0]
        for dev in range(1, 8):
            acc = acc + g_ref[dev]
        out_ref[...] = acc
        tail = acc[SUBLANES:]
        out_ref[SUBLANES:, :] = jnp.broadcast_to(jnp.sum(tail, axis=1, keepdims=True), tail.shape)

    return _call(body, name="sum_devices",
                 in_specs=[pl.BlockSpec(memory_space=pltpu.VMEM)],
                 out_specs=pl.BlockSpec(memory_space=pltpu.VMEM),
                 out_shape=jax.ShapeDtypeStruct((rows, width), F32))(gathered)


def _rope_tables(s):
    half = ROT_DIM // 2
    inv_freq = jnp.power(jnp.float32(ROPE_THETA), -jnp.arange(half, dtype=F32) * 2.0 / ROT_DIM)
    freq64 = jnp.concatenate([inv_freq, inv_freq, jnp.zeros((HEAD_DIM - ROT_DIM,), F32)])
    freq = jnp.concatenate([freq64, freq64])[None, :]
    dim = (jnp.arange(LANES) % HEAD_DIM)[None, :]
    ang = jnp.arange(s).astype(F32)[:, None] * freq
    cos, sin = jnp.cos(ang), jnp.sin(ang)
    return cos, jnp.where(dim < half, -sin, 0.0), jnp.where((dim >= half) & (dim < ROT_DIM), sin, 0.0)


def _pad_rows(a, rows):
    return jnp.pad(a, ((0, rows - a.shape[0]), (0, 0)))


def _pad_cols(a, cols):
    return jnp.pad(a, ((0, 0), (0, cols - a.shape[1])))


def kernel(x, p, norm_gain, w_in, q_norm_gain, k_norm_gain, attn_sinks, conv_w, w_out, ple_gate_norm_gain, w_ple_gate, b_ple_gate, w_ple_proj, ple_norm_gain, loss_target, m_norm_gain, m_w_in, m_q_norm_gain, m_k_norm_gain, m_attn_sinks, m_conv_w, m_w_out, m_ple_gate_norm_gain, m_w_ple_gate, m_b_ple_gate, m_w_ple_proj, m_ple_norm_gain, v_norm_gain, v_w_in, v_q_norm_gain, v_k_norm_gain, v_attn_sinks, v_conv_w, v_w_out, v_ple_gate_norm_gain, v_w_ple_gate, v_b_ple_gate, v_w_ple_proj, v_ple_norm_gain):
    x2, p2, tgt = x[0], p[0, 0], loss_target[0]
    s, d = x2.shape
    ple = p2.shape[1]
    aw = d // 2
    cw = d - aw
    nq = aw // HEAD_DIM
    sh = w_in.shape[2]
    in_w = N_CHIPS * sh
    dq = d // N_CHIPS
    cq = cw // N_CHIPS
    ga_off = (aw + 2 * KV_WIDTH) // COLT
    b_off = (2 * aw + 2 * KV_WIDTH) // COLT
    assert in_w == 2 * aw + 2 * KV_WIDTH + 4 * cw and aw % COLT == 0 and cw % COLT == 0
    assert s % BLOCK == 0 and sh % LANES == 0 and nq % (2 * N_KV_HEADS) == 0

    core = lax.axis_index("c").astype(jnp.int32).reshape(1)
    chip = 2 * lax.axis_index("x") + lax.axis_index("y")

    chip1 = chip.astype(jnp.int32).reshape(1)
    place = jnp.concatenate([chip1, core])
    w_in_own = cast_bf16(w_in[0], "cast_w_in")
    rest = [cast_into_slot(w_out[0], chip1, "cast_w_out"), cast_into_slot(w_ple_gate[0], chip1, "cast_w_pg"),
            cast_into_slot(w_ple_proj[0], chip1, "cast_w_pp"),
            lax.dynamic_update_slice(jnp.zeros((N_CHIPS, SUBLANES, cq), F32),
                                     _pad_rows(conv_w[0], SUBLANES)[None], (chip, 0, 0))]
    order = jnp.stack([chip, chip ^ 2, chip ^ 1, chip ^ 3]).astype(jnp.int32)
    wi_sems, wi_arrs, wi_token = exchange_start(
        "gather_wi_start", [(plan_shard_ici(j), [0, 1 + j], 1) for j in range(3)],
        [w_in_own] + [lax.empty((d, sh), BF16) for _ in range(3)])
    rest_sems, rest, rest_token = exchange_start(
        "gather_rest_start", [(plan_gather_ici(3), [0, 1, 2, 3], 12)], rest, after=(wi_token,))

    tm = _tile(s, 1024)
    tn = _tile(d, 1024)
    tk = _tile(d, 2048)
    ts = _tile(s, 2048)
    h = rms_fwd(x2, norm_gain, "rms_fwd_x", after=(rest_token,))
    tabs = _rope_tables(s)
    qg = jnp.tile(q_norm_gain, (1, LANES // HEAD_DIM))
    kg = jnp.tile(k_norm_gain, (1, LANES // HEAD_DIM))
    seg_i = jnp.arange(SEG) // HEAD_DIM
    segb = (seg_i[:, None] == seg_i[None, :]).astype(BF16)
    z = in_proj_part(h, wi_arrs[0], None, order, 0, in_w)
    w_shards = [wi_arrs[0]]
    for j in range(3):
        w_shards[0], land = exchange_wait("gather_wi_wait%d" % j, plan_shard_ici(j), wi_sems[j],
                                          [w_shards[0], wi_arrs[1 + j]], z)
        land, = exchange("gather_wi_pass%d" % j, plan_shard_pass, [land], 1)
        z = in_proj_part(h, land, z, order, 1 + j, in_w)
        w_shards.append(land)
    wo_all, wg_all, wp_all, conv_all = exchange_wait("gather_rest_wait", plan_gather_ici(3), rest_sems[0], rest, z)
    pass_sems, passed, pass_token = exchange_start(
        "gather_rest_pass_start", [(plan_gather_pass, [0, 1, 2], 9)], [wo_all, wg_all, wp_all])
    conv_full = conv_all.transpose(1, 0, 2).reshape(SUBLANES, cw)
    qs, ks, vb = qk_prep_fwd(z, tabs, qg, kg, segb, aw, after=(pass_token,))
    attn, mix = attn_fwd(qs, ks, vb, z, attn_sinks, aw, d, ga_off)
    mix = conv_fwd(z, conv_full, mix, aw, cw, b_off)
    wo_all, wg_all, wp_all = exchange_wait("gather_rest_pass_wait", plan_gather_pass, pass_sems[0], passed, mix)
    wo_full = wo_all.reshape(d, d)
    wg_full = wg_all.reshape(d, d)
    sq = lambda shape: dict(
        a_spec=pl.BlockSpec((tm, tk), lambda i, j, k: (i, k)),
        o_spec=pl.BlockSpec((tm, tn), lambda i, j, k: (i, j)),
        out_shape=jax.ShapeDtypeStruct(shape, F32))
    x1 = matmul(mix, wo_full, grid=(s // tm, d // tn, d // tk),
                b_spec=pl.BlockSpec((tk, tn), lambda i, j, k: (k, j)), dims=NN, name="mm_x1",
                res=x2, res_spec=pl.BlockSpec((tm, tn), lambda i, j, k: (i, j)), **sq((s, d)))
    hg = rms_fwd(x1, ple_gate_norm_gain, "rms_fwd_x1")
    gl = matmul(hg, wg_full, grid=(s // tm, d // tn, d // tk),
                b_spec=pl.BlockSpec((tk, tn), lambda i, j, k: (k, j)), dims=NN, name="mm_gl", **sq((s, d)))
    pq = d // N_CHIPS

    d_gl, d_pe, dy, acc_head = head_fwd_bwd(x1, gl, p2, wp_all, tgt, b_ple_gate, ple_norm_gain)
    d_hg = matmul(d_gl, wg_full, grid=(s // tm, d // tn, d // tk),
                  b_spec=pl.BlockSpec((tn, tk), lambda i, j, k: (j, k)), dims=NT, name="mm_d_hg", **sq((s, d)))
    wgrad = lambda a_cols, shape3, o_spec, b_cols, name, a, b, grid: matmul(
        a, b, grid=grid,
        a_spec=pl.BlockSpec((ts, a_cols), lambda i, j, k: (k, i)),
        b_spec=pl.BlockSpec((ts, b_cols), lambda i, j, k: (k, j)),
        o_spec=o_spec, out_shape=jax.ShapeDtypeStruct(shape3, BF16), dims=TN, name=name)
    g_wg = wgrad(tn, (d, d), pl.BlockSpec((tn, tn), lambda i, j, k: (i, j)), tn, "mm_g_wg", hg, d_gl,
                 (d // tn, d // tn, s // ts))
    g_wp = wgrad(ple, (N_CHIPS, ple, pq), pl.BlockSpec((None, ple, pq), lambda i, j, k: (j, 0, 0)), pq,
                 "mm_g_wp", p2, d_pe, (1, N_CHIPS, s // ts))
    d_x1, d_x1b, acc_g = rms_bwd_add(d_hg, x1, dy, ple_gate_norm_gain, "rms_bwd_x1", True)
    d_mix = matmul(d_x1b, wo_full, grid=(s // tm, d // tn, d // tk),
                   b_spec=pl.BlockSpec((tn, tk), lambda i, j, k: (j, k)), dims=NT, name="mm_d_mix", **sq((s, d)))
    g_wo = wgrad(tn, (d, d), pl.BlockSpec((tn, tn), lambda i, j, k: (i, j)), tn, "mm_g_wo", mix, d_x1b,
                 (d // tn, d // tn, s // ts))
    def reduce_start(tag, grads):
        n = len(grads)
        lands = [lax.empty((N_CHIPS, a.shape[1] // 2, a.shape[2]), BF16) for a in grads]
        swapped = exchange("swap_" + tag, plan_swap, list(grads) + lands, n)
        parts = [add_sibling(g, o, core, "add_sibling_%s%d" % (tag, i))
                 for i, (g, o) in enumerate(zip(swapped[:n], swapped[n:]))]
        return exchange_start("scatter_%s_start" % tag, [(plan_scatter, list(range(2 * n)), 3 * n)],
                              parts + [lax.empty(a.shape, BF16) for a in parts])

    def reduce_sum(tag, handle, after):
        sems, arrays, _ = handle
        n = len(arrays) // 2
        got = exchange_wait("scatter_%s_wait" % tag, plan_scatter, sems[0], arrays, after)
        return [sum_chips(pt, o, place, "sum_chips_%s%d" % (tag, i))
                for i, (pt, o) in enumerate(zip(got[:n], got[n:]))]

    early = reduce_start("early", [g_wo.reshape(N_CHIPS, dq, d), g_wg.reshape(N_CHIPS, dq, d), g_wp])
    d_o, dz = gate_bwd(d_mix, attn, z, aw, ga_off, after=(early[-1],))
    dqs, dks, dvs, acc_sink = attn_bwd(qs, ks, vb, d_o, attn_sinks, aw)
    dz, acc_qk = qk_prep_bwd(z, dqs, dks, dvs, tabs, qg, kg, segb, dz, aw)
    dz, acc_conv = conv_bwd(z, d_mix, conv_full, dz, aw, cw, b_off)
    join_sems, joining, join_token = exchange_start(
        "join_early_start", [(plan_join, [0, 1, 2], 3)], reduce_sum("early", early, dz))
    g_wi = matmul(h, dz, grid=(d // tn, N_CHIPS, s // ts),
                  a_spec=pl.BlockSpec((ts, tn), lambda i, j, k: (k, i)),
                  b_spec=pl.BlockSpec((ts, sh), lambda i, j, k: (k, j)),
                  o_spec=pl.BlockSpec((None, tn, sh), lambda i, j, k: (j, i, 0)),
                  out_shape=jax.ShapeDtypeStruct((N_CHIPS, d, sh), BF16), dims=TN, name="mm_g_wi",
                  after=(join_token,))
    full_wo, full_wg, full_wp = exchange_wait("join_early_wait", plan_join, join_sems[0], joining, g_wi)
    late = reduce_start("late", [g_wi])
    d_h = in_proj_bwd(dz, w_shards, order, d, after=(late[-1],))
    grad_x, acc_x = rms_bwd_add(d_h, x2, d_x1, norm_gain, "rms_bwd_x", False)

    wsm = max(d, cw)
    misc = jnp.concatenate([acc_qk[0:1, :HEAD_DIM], acc_qk[1:2, :HEAD_DIM], acc_sink[0:1, :nq]], axis=1)
    small = jnp.concatenate([
        _pad_cols(acc_x[0:1], wsm), _pad_cols(acc_g[0:1], wsm), _pad_cols(acc_head[0:1], wsm),
        _pad_cols(acc_head[1:2], wsm), _pad_cols(misc, wsm), _pad_cols(acc_conv[0:3], wsm),
        _pad_rows(_pad_cols(acc_head[2:3], wsm), SUBLANES)], axis=0)
    device = 2 * chip + lax.axis_index("c")
    small_sems, small_bufs, small_token = exchange_start(
        "small_start", [(plan_allgather_small, [0], 7)],
        [lax.dynamic_update_slice(jnp.zeros((8,) + small.shape, F32), small[None], (device, 0, 0))])
    full_wi, = exchange("join_late", plan_join, reduce_sum("late", late, small_token), 1)
    big = {}
    for nm, g, w, m, v in (("w_in", full_wi, w_in, m_w_in, v_w_in), ("w_out", full_wo, w_out, m_w_out, v_w_out),
                           ("w_ple_gate", full_wg, w_ple_gate, m_w_ple_gate, v_w_ple_gate),
                           ("w_ple_proj", full_wp, w_ple_proj, m_w_ple_proj, v_w_ple_proj)):
        big[nm] = [o[None] for o in adamw(g, w[0], m[0], v[0], "adamw_" + nm)]

    gathered, = exchange_wait("small_wait", plan_allgather_small, small_sems[0], small_bufs, big["w_in"][1])
    tot = sum_devices(gathered)
    loss = tot[SUBLANES, 0]

    def pack(vals):
        ng, pg, bg, eg, qgv, kgv, sk, cv = vals
        misc_v = jnp.concatenate([qgv, kgv, sk], axis=1)
        return jnp.concatenate([_pad_cols(ng, wsm), _pad_cols(pg, wsm), _pad_cols(bg, wsm), _pad_cols(eg, wsm),
                                _pad_cols(misc_v, wsm), _pad_cols(cv[0], wsm)], axis=0)

    g_small = jnp.concatenate(
        [tot[0:5], _pad_cols(lax.dynamic_slice(tot[5:8], (0, chip * cq), (3, cq)), wsm)], axis=0)
    w_small = pack((norm_gain, ple_gate_norm_gain, b_ple_gate, ple_norm_gain, q_norm_gain, k_norm_gain,
                    attn_sinks, conv_w))
    m_small = pack((m_norm_gain, m_ple_gate_norm_gain, m_b_ple_gate, m_ple_norm_gain, m_q_norm_gain,
                    m_k_norm_gain, m_attn_sinks, m_conv_w))
    v_small = pack((v_norm_gain, v_ple_gate_norm_gain, v_b_ple_gate, v_ple_norm_gain, v_q_norm_gain,
                    v_k_norm_gain, v_attn_sinks, v_conv_w))
    sm = adamw(g_small, w_small, m_small, v_small, "adamw_small")

    def unpack(a):
        return {"norm_gain": a[0:1, :d], "ple_gate_norm_gain": a[1:2, :d], "b_ple_gate": a[2:3, :d],
                "ple_norm_gain": a[3:4, :d], "q_norm_gain": a[4:5, :HEAD_DIM],
                "k_norm_gain": a[4:5, HEAD_DIM:2 * HEAD_DIM],
                "attn_sinks": a[4:5, 2 * HEAD_DIM:2 * HEAD_DIM + nq], "conv_w": a[5:8, :cq][None]}

    order = ("norm_gain", "w_in", "q_norm_gain", "k_norm_gain", "attn_sinks", "conv_w", "w_out",
             "ple_gate_norm_gain", "w_ple_gate", "b_ple_gate", "w_ple_proj", "ple_norm_gain")
    outs = [loss, grad_x[None]]
    for kind in range(4):
        table = unpack(sm[kind])
        for nm in order:
            outs.append(big[nm][kind] if nm in big else table[nm])
    return tuple(outs)
```

```python
import functools

import jax
import jax.numpy as jnp
from jax import lax
from jax.experimental import pallas as pl
from jax.experimental.pallas import tpu as pltpu

F32 = jnp.float32
BF16 = jnp.bfloat16
MESH = pl.DeviceIdType.MESH

HEAD_DIM = 64
N_KV_HEADS = 4
KV_WIDTH = N_KV_HEADS * HEAD_DIM
BLOCK = 128
ROT_DIM = 16
ROPE_THETA = 500000.0
EPS = 1e-6
NEG_INF = -1e30
N_CHIPS = 4
LANES = 128
SUBLANES = 8
COLT = 512
SEG = 256
VMEM_LIMIT = 48 * 1024 * 1024

ADAM_LR = 0.001
ADAM_B1 = 0.9
ADAM_B2 = 0.999
ADAM_EPS = 1e-08
ADAM_WD = 0.01
ADAM_STEP = 10

NN = (((1,), (0,)), ((), ()))
NT = (((1,), (1,)), ((), ()))
TN = (((0,), (0,)), ((), ()))


def _call(body, **kw):
    return pl.pallas_call(body, **kw)


def _call_after(body, after, **kw):
    n_in, n_after = len(kw["in_specs"]), len(after)
    kw["in_specs"] = list(kw["in_specs"]) + [pl.BlockSpec(memory_space=pl.ANY)] * n_after

    def body_after(*refs):
        body(*refs[:n_in], *refs[n_in + n_after:])

    call = _call(body_after, **kw)
    return lambda *args: call(*args, *after)


def _params(sem):
    return pltpu.CompilerParams(dimension_semantics=sem, vmem_limit_bytes=VMEM_LIMIT)


def _tile(n, pref):
    return pref if n % pref == 0 else n


def _sigmoid(v):
    return 1.0 / (1.0 + jnp.exp(-v))


def _hbm():
    return pl.BlockSpec(memory_space=pl.ANY)


def cast_bf16(a, name):
    r, c = a.shape
    tr = _tile(r, 256)

    def body(a_ref, o_ref):
        o_ref[...] = a_ref[...].astype(BF16)

    return _call(body, name=name, grid=(r // tr,),
                 in_specs=[pl.BlockSpec((tr, c), lambda i: (i, 0))],
                 out_specs=pl.BlockSpec((tr, c), lambda i: (i, 0)),
                 out_shape=jax.ShapeDtypeStruct((r, c), BF16),
                 compiler_params=_params(("parallel",)))(a)


def cast_into_slot(a, chip, name):
    r, c = a.shape
    tr = _tile(r, 256)

    def body(chip_ref, a_ref, o_ref):
        o_ref[...] = a_ref[...].astype(BF16)

    grid_spec = pltpu.PrefetchScalarGridSpec(
        num_scalar_prefetch=1, grid=(r // tr,),
        in_specs=[pl.BlockSpec((tr, c), lambda i, chip_ref: (i, 0))],
        out_specs=pl.BlockSpec((None, tr, c), lambda i, chip_ref: (chip_ref[0], i, 0)))
    return _call(body, name=name, grid_spec=grid_spec,
                 out_shape=jax.ShapeDtypeStruct((N_CHIPS, r, c), BF16),
                 compiler_params=_params(("parallel",)))(chip, a)


def rms_fwd(x, gain, name, after=()):
    s, d = x.shape
    tm = _tile(s, 256)

    def body(x_ref, g_ref, o_ref):
        xf = x_ref[...]
        r = lax.rsqrt(jnp.mean(xf * xf, axis=-1, keepdims=True) + EPS)
        o_ref[...] = ((xf * r) * g_ref[...]).astype(BF16)

    return _call_after(body, after, name=name, grid=(s // tm,),
                       in_specs=[pl.BlockSpec((tm, d), lambda i: (i, 0)),
                                 pl.BlockSpec((1, d), lambda i: (0, 0))],
                       out_specs=pl.BlockSpec((tm, d), lambda i: (i, 0)),
                       out_shape=jax.ShapeDtypeStruct((s, d), BF16),
                       compiler_params=_params(("parallel",)))(x, gain)


def rms_bwd_add(dyn, xin, add, gain, name, want_bf16):
    s, d = xin.shape
    tm = _tile(s, 256)

    def body(dy_ref, x_ref, a_ref, g_ref, *outs):
        i = pl.program_id(0)
        dx_ref, acc_ref = outs[0], outs[-1]
        xf = x_ref[...]
        r = lax.rsqrt(jnp.mean(xf * xf, axis=-1, keepdims=True) + EPS)
        xhat = xf * r
        dyv = dy_ref[...]
        gd = dyv * g_ref[...]
        dx = a_ref[...] + r * (gd - xhat * jnp.mean(xhat * gd, axis=-1, keepdims=True))
        dx_ref[...] = dx
        if want_bf16:
            outs[1][...] = dx.astype(BF16)

        @pl.when(i == 0)
        def _():
            acc_ref[...] = jnp.zeros_like(acc_ref)

        acc_ref[0:1, :] += jnp.sum(dyv * xhat, axis=0, keepdims=True)

    row = pl.BlockSpec((tm, d), lambda i: (i, 0))
    out_specs = [row] + ([row] if want_bf16 else []) + [pl.BlockSpec((SUBLANES, d), lambda i: (0, 0))]
    out_shape = ([jax.ShapeDtypeStruct((s, d), F32)]
                 + ([jax.ShapeDtypeStruct((s, d), BF16)] if want_bf16 else [])
                 + [jax.ShapeDtypeStruct((SUBLANES, d), F32)])
    return _call(body, name=name, grid=(s // tm,),
                 in_specs=[row, row, row, pl.BlockSpec((1, d), lambda i: (0, 0))],
                 out_specs=out_specs, out_shape=out_shape,
                 compiler_params=_params(("arbitrary",)))(dyn, xin, add, gain)


def head_fwd_bwd(x1, gl, p, wp, tgt, bias, ple_gain):
    s, d = x1.shape
    tm = _tile(s, 256)

    def body(x1_ref, gl_ref, p_ref, wp_ref, t_ref, b_ref, g_ref, dgl_ref, dpe_ref, dy_ref, acc_ref):
        i = pl.program_id(0)
        gate = _sigmoid(gl_ref[...] + b_ref[...])
        pb = p_ref[...].astype(BF16)
        pev = jnp.concatenate([jnp.dot(pb, wp_ref[q], preferred_element_type=F32) for q in range(N_CHIPS)],
                              axis=1)
        r = lax.rsqrt(jnp.mean(pev * pev, axis=-1, keepdims=True) + EPS)
        pehat = pev * r
        gain = g_ref[...]
        e = pehat * gain
        diff = (x1_ref[...] + gate * e) - t_ref[...]
        dy = diff * (1.0 / d)
        dy_ref[...] = dy
        d_gl = (dy * e) * (gate * (1.0 - gate))
        dgl_ref[...] = d_gl.astype(BF16)
        d_e = dy * gate
        gd = d_e * gain
        d_pe = r * (gd - pehat * jnp.mean(pehat * gd, axis=-1, keepdims=True))
        dpe_ref[...] = d_pe.astype(BF16)

        @pl.when(i == 0)
        def _():
            acc_ref[...] = jnp.zeros_like(acc_ref)

        acc_ref[0:1, :] += jnp.sum(d_gl, axis=0, keepdims=True)
        acc_ref[1:2, :] += jnp.sum(d_e * pehat, axis=0, keepdims=True)
        acc_ref[2:3, :] += jnp.sum(diff * diff, axis=0, keepdims=True) * (0.5 / d)

    row = pl.BlockSpec((tm, d), lambda i: (i, 0))
    vec = pl.BlockSpec((1, d), lambda i: (0, 0))
    return _call(body, name="head_fwd_bwd", grid=(s // tm,),
                 in_specs=[row, row, pl.BlockSpec((tm, p.shape[1]), lambda i: (i, 0)),
                           pl.BlockSpec(wp.shape, lambda i: (0, 0, 0)), row, vec, vec],
                 out_specs=[row, row, row, pl.BlockSpec((SUBLANES, d), lambda i: (0, 0))],
                 out_shape=[jax.ShapeDtypeStruct((s, d), BF16), jax.ShapeDtypeStruct((s, d), BF16),
                            jax.ShapeDtypeStruct((s, d), F32), jax.ShapeDtypeStruct((SUBLANES, d), F32)],
                 compiler_params=_params(("arbitrary",)))(x1, gl, p, wp, tgt, bias, ple_gain)


def adamw(g, w, m, v, name, after=()):
    r, c = g.shape
    tr = _tile(r, 256)

    def body(g_ref, w_ref, m_ref, v_ref, go_ref, d_ref, mo_ref, vo_ref):
        gv = g_ref[...]
        mn = ADAM_B1 * m_ref[...] + (1.0 - ADAM_B1) * gv
        vn = ADAM_B2 * v_ref[...] + (1.0 - ADAM_B2) * (gv * gv)
        m_hat = mn / (1.0 - ADAM_B1 ** ADAM_STEP)
        v_hat = vn / (1.0 - ADAM_B2 ** ADAM_STEP)
        go_ref[...] = gv
        d_ref[...] = -ADAM_LR * (m_hat / (jnp.sqrt(v_hat) + ADAM_EPS) + ADAM_WD * w_ref[...])
        mo_ref[...] = mn
        vo_ref[...] = vn

    blk = pl.BlockSpec((tr, c), lambda i: (i, 0))
    shp = jax.ShapeDtypeStruct((r, c), F32)
    return _call_after(body, after, name=name, grid=(r // tr,), in_specs=[blk] * 4, out_specs=[blk] * 4,
                       out_shape=[shp] * 4, compiler_params=_params(("parallel",)))(g, w, m, v)


def matmul(a, b, *, grid, a_spec, b_spec, o_spec, out_shape, dims, name, res=None, res_spec=None, after=()):
    nk = grid[2]
    acc_shape = tuple(d for d in o_spec.block_shape if d is not None)

    def body(*refs):
        a_ref, b_ref = refs[:2]
        r_ref = refs[2] if res is not None else None
        o_ref = refs[3] if res is not None else refs[2]
        part = lax.dot_general(a_ref[...].astype(BF16), b_ref[...].astype(BF16), dims, preferred_element_type=F32)

        def finish(out):
            if res is not None:
                out = r_ref[...] + out
            o_ref[...] = out.astype(o_ref.dtype)

        if nk == 1:
            finish(part)
            return
        acc_ref = refs[-1]
        k = pl.program_id(2)

        @pl.when(k == 0)
        def _():
            acc_ref[...] = part

        @pl.when((k > 0) & (k < nk - 1))
        def _():
            acc_ref[...] += part

        @pl.when(k == nk - 1)
        def _():
            finish(acc_ref[...] + part)

    in_specs = [a_spec, b_spec] + ([res_spec] if res is not None else [])
    args = (a, b) + ((res,) if res is not None else ())
    return _call_after(body, after, name=name, grid=grid, in_specs=in_specs, out_specs=o_spec,
                       out_shape=out_shape, scratch_shapes=[pltpu.VMEM(acc_shape, F32)] if nk > 1 else [],
                       compiler_params=_params(("parallel", "parallel", "arbitrary")))(*args)


def in_proj_part(h, w, z_prev, order, q, in_w):
    s, d = h.shape
    sh = w.shape[1]
    tm = _tile(s, 1024)

    def body(order_ref, h_ref, w_ref, *rest):
        rest[-1][...] = jnp.dot(h_ref[...], w_ref[...], preferred_element_type=F32)

    in_specs = [pl.BlockSpec((tm, d), lambda i, order_ref: (i, 0)),
                pl.BlockSpec((d, sh), lambda i, order_ref: (0, 0))]
    args = [order, h, w]
    if z_prev is not None:
        in_specs.append(_hbm())
        args.append(z_prev)
    grid_spec = pltpu.PrefetchScalarGridSpec(
        num_scalar_prefetch=1, grid=(s // tm,), in_specs=in_specs,
        out_specs=pl.BlockSpec((tm, sh), lambda i, order_ref: (i, order_ref[q])))
    return _call(body, name="mm_z%d" % q, grid_spec=grid_spec,
                 out_shape=jax.ShapeDtypeStruct((s, in_w), F32),
                 input_output_aliases={3: 0} if z_prev is not None else {},
                 compiler_params=_params(("parallel",)))(*args)


def in_proj_bwd(dz, ws, order, d, after):
    s = dz.shape[0]
    sh = ws[0].shape[1]
    tm, tn = _tile(s, 512), _tile(d, 512)
    nq, n_after = len(ws), len(after)

    def body(order_ref, *refs):
        a_refs, b_refs, o_ref = refs[:nq], refs[nq:2 * nq], refs[2 * nq + n_after]
        acc = lax.dot_general(a_refs[0][...], b_refs[0][...], NT, preferred_element_type=F32)
        for q in range(1, nq):
            acc += lax.dot_general(a_refs[q][...], b_refs[q][...], NT, preferred_element_type=F32)
        o_ref[...] = acc

    a_spec = lambda q: pl.BlockSpec((tm, sh), functools.partial(lambda i, j, order_ref, q: (i, order_ref[q]), q=q))
    grid_spec = pltpu.PrefetchScalarGridSpec(
        num_scalar_prefetch=1, grid=(s // tm, d // tn),
        in_specs=[a_spec(q) for q in range(nq)]
        + [pl.BlockSpec((tn, sh), lambda i, j, order_ref: (j, 0))] * nq + [_hbm()] * n_after,
        out_specs=pl.BlockSpec((tm, tn), lambda i, j, order_ref: (i, j)))
    return _call(body, name="mm_d_h", grid_spec=grid_spec, out_shape=jax.ShapeDtypeStruct((s, d), F32),
                 compiler_params=_params(("parallel", "parallel")))(order, *([dz] * nq), *ws, *after)


def _seg_mean(sq, segb):
    parts = []
    for cgrp in range(sq.shape[1] // SEG):
        blk = sq[:, cgrp * SEG:(cgrp + 1) * SEG]
        hi = blk.astype(BF16)
        r1 = blk - hi.astype(F32)
        mid = r1.astype(BF16)
        lo = (r1 - mid.astype(F32)).astype(BF16)
        acc = jnp.dot(hi, segb, preferred_element_type=F32)
        acc += jnp.dot(mid, segb, preferred_element_type=F32)
        acc += jnp.dot(lo, segb, preferred_element_type=F32)
        parts.append(acc)
    out = parts[0] if len(parts) == 1 else jnp.concatenate(parts, axis=1)
    return out * (1.0 / HEAD_DIM)


def _rope(v, cos, sa, sb):
    parts = []
    for cgrp in range(v.shape[1] // LANES):
        blk = v[:, cgrp * LANES:(cgrp + 1) * LANES]
        parts.append(blk * cos + pltpu.roll(blk, LANES - 8, 1) * sa + pltpu.roll(blk, 8, 1) * sb)
    return parts[0] if len(parts) == 1 else jnp.concatenate(parts, axis=1)


def _rope_t(dv, cos, sa, sb):
    parts = []
    for cgrp in range(dv.shape[1] // LANES):
        blk = dv[:, cgrp * LANES:(cgrp + 1) * LANES]
        parts.append(blk * cos + pltpu.roll(blk * sa, 8, 1) + pltpu.roll(blk * sb, LANES - 8, 1))
    return parts[0] if len(parts) == 1 else jnp.concatenate(parts, axis=1)


def _tile_lanes(vec, width):
    reps = width // LANES
    return vec if reps == 1 else jnp.tile(vec, (1, reps))


def qk_prep_fwd(z, tabs, qg, kg, segb, aw, after=()):
    s = z.shape[0]
    tm = _tile(s, 256)
    wq = aw + 2 * KV_WIDTH

    def body(z_ref, cos_ref, sa_ref, sb_ref, qg_ref, kg_ref, seg_ref, qs_ref, ks_ref, vb_ref):
        zz = z_ref[...]
        q, k, v = zz[:, :aw], zz[:, aw:aw + KV_WIDTH], zz[:, aw + KV_WIDTH:]
        cos, sa, sb, segm = cos_ref[...], sa_ref[...], sb_ref[...], seg_ref[...]
        rq = lax.rsqrt(_seg_mean(q * q, segm) + EPS)
        qn = (q * rq) * _tile_lanes(qg_ref[...], aw)
        qs_ref[...] = (_rope(qn, cos, sa, sb) * (HEAD_DIM ** -0.5)).astype(BF16)
        rk = lax.rsqrt(_seg_mean(k * k, segm) + EPS)
        kn = (k * rk) * _tile_lanes(kg_ref[...], KV_WIDTH)
        ks_ref[...] = _rope(kn, cos, sa, sb).astype(BF16)
        vb_ref[...] = v.astype(BF16)

    tab = pl.BlockSpec((tm, LANES), lambda i: (i, 0))
    vec = pl.BlockSpec((1, LANES), lambda i: (0, 0))
    return _call_after(body, after, name="qk_prep_fwd", grid=(s // tm,),
                       in_specs=[pl.BlockSpec((tm, wq), lambda i: (i, 0)), tab, tab, tab, vec, vec,
                                 pl.BlockSpec((SEG, SEG), lambda i: (0, 0))],
                       out_specs=[pl.BlockSpec((tm, aw), lambda i: (i, 0)),
                                  pl.BlockSpec((tm, KV_WIDTH), lambda i: (i, 0)),
                                  pl.BlockSpec((tm, KV_WIDTH), lambda i: (i, 0))],
                       out_shape=[jax.ShapeDtypeStruct((s, aw), BF16), jax.ShapeDtypeStruct((s, KV_WIDTH), BF16),
                                  jax.ShapeDtypeStruct((s, KV_WIDTH), BF16)],
                       compiler_params=_params(("parallel",)))(z, *tabs, qg, kg, segb)


def qk_prep_bwd(z, dqs, dks, dvs, tabs, qg, kg, segb, dz, aw):
    s = z.shape[0]
    tm = _tile(s, 256)
    wq = aw + 2 * KV_WIDTH

    def body(z_ref, dq_ref, dk_ref, dv_ref, cos_ref, sa_ref, sb_ref, qg_ref, kg_ref, seg_ref, dz_in,
             dz_ref, acc_ref):
        i = pl.program_id(0)
        zz = z_ref[...]
        q, k = zz[:, :aw], zz[:, aw:aw + KV_WIDTH]
        cos, sa, sb, segm = cos_ref[...], sa_ref[...], sb_ref[...], seg_ref[...]

        def one(xv, dout, gvec, width):
            g = _tile_lanes(gvec, width)
            r = lax.rsqrt(_seg_mean(xv * xv, segm) + EPS)
            xhat = xv * r
            dn = _rope_t(dout, cos, sa, sb)
            gd = dn * g
            dx = r * (gd - xhat * _seg_mean(xhat * gd, segm))
            contrib = jnp.sum(dn * xhat, axis=0, keepdims=True)
            folded = contrib[:, :LANES]
            for cgrp in range(1, width // LANES):
                folded = folded + contrib[:, cgrp * LANES:(cgrp + 1) * LANES]
            return dx, folded + pltpu.roll(folded, HEAD_DIM, 1)

        dq, gq = one(q, dq_ref[...] * (HEAD_DIM ** -0.5), qg_ref[...], aw)
        dk, gk = one(k, dk_ref[...], kg_ref[...], KV_WIDTH)
        dz_ref[:, :aw] = dq.astype(BF16)
        dz_ref[:, aw:aw + KV_WIDTH] = dk.astype(BF16)
        dz_ref[:, aw + KV_WIDTH:] = dv_ref[...].astype(BF16)

        @pl.when(i == 0)
        def _():
            acc_ref[...] = jnp.zeros_like(acc_ref)

        acc_ref[0:1, :] += gq
        acc_ref[1:2, :] += gk

    tab = pl.BlockSpec((tm, LANES), lambda i: (i, 0))
    vec = pl.BlockSpec((1, LANES), lambda i: (0, 0))
    kvb = pl.BlockSpec((tm, KV_WIDTH), lambda i: (i, 0))
    return _call(body, name="qk_prep_bwd", grid=(s // tm,),
                 in_specs=[pl.BlockSpec((tm, wq), lambda i: (i, 0)),
                           pl.BlockSpec((tm, aw), lambda i: (i, 0)), kvb, kvb, tab, tab, tab, vec, vec,
                           pl.BlockSpec((SEG, SEG), lambda i: (0, 0)), _hbm()],
                 out_specs=[pl.BlockSpec((tm, wq), lambda i: (i, 0)),
                            pl.BlockSpec((SUBLANES, LANES), lambda i: (0, 0))],
                 out_shape=[jax.ShapeDtypeStruct(dz.shape, BF16), jax.ShapeDtypeStruct((SUBLANES, LANES), F32)],
                 input_output_aliases={10: 0},
                 compiler_params=_params(("arbitrary",)))(z, dqs, dks, dvs, *tabs, qg, kg, segb, dz)


def _placed(band, lane_idx):
    out = {}
    for kh in range(N_KV_HEADS):
        grp = band[:, (kh // 2) * LANES:(kh // 2 + 1) * LANES]
        for half in (0, 1):
            t = grp if half == kh % 2 else pltpu.roll(grp, HEAD_DIM, 1)
            keep = (lane_idx >= half * HEAD_DIM) & (lane_idx < (half + 1) * HEAD_DIM)
            out[kh, half] = jnp.where(keep, t, jnp.zeros_like(t))
    return out


def _softmax_with_sink(sc, valid, sink):
    sc = jnp.where(valid, sc, NEG_INF)
    m = jnp.maximum(jnp.max(sc, axis=-1, keepdims=True), sink)
    e = jnp.exp(sc - m)
    es = jnp.exp(sink - m)
    den = jnp.sum(e, axis=-1, keepdims=True) + es
    return e / den, es / den


def _stack_halves(placed, kh):
    return jnp.concatenate([placed[kh, 0], placed[kh, 1]], axis=0)


def _valid_mask(n):
    r_i = lax.broadcasted_iota(jnp.int32, (BLOCK, 2 * BLOCK), 0)
    j_i = lax.broadcasted_iota(jnp.int32, (BLOCK, 2 * BLOCK), 1)
    return (j_i > r_i) & (j_i <= r_i + BLOCK) & ((n > 0) | (j_i >= BLOCK))


def attn_fwd(qs, ks, vb, z, sinks, aw, d, ga_off):
    s = qs.shape[0]
    nb = s // BLOCK
    nq = aw // HEAD_DIM
    grp_sz = nq // N_KV_HEADS
    n_ga = aw // COLT

    def body(q_ref, ko_ref, kp_ref, vo_ref, vp_ref, *rest):
        ga_refs = rest[:n_ga]
        sink_ref, attn_ref, mix_ref = rest[n_ga:]
        n = pl.program_id(0)
        lane_idx = lax.broadcasted_iota(jnp.int32, (2 * BLOCK, LANES), 1)
        kpl = _placed(jnp.concatenate([kp_ref[...], ko_ref[...]], axis=0), lane_idx)
        vpl = _placed(jnp.concatenate([vp_ref[...], vo_ref[...]], axis=0), lane_idx)
        valid = _valid_mask(n)
        groups = range(nq // 2)
        kcat = [_stack_halves(kpl, kh) for kh in range(N_KV_HEADS)]
        vcat = [_stack_halves(vpl, kh) for kh in range(N_KV_HEADS)]
        scs = [lax.dot_general(q_ref[:, c * LANES:(c + 1) * LANES], kcat[2 * c // grp_sz], NT,
                               preferred_element_type=F32) for c in groups]
        probs = [jnp.concatenate(
            [_softmax_with_sink(scs[c][:, half * 2 * BLOCK:(half + 1) * 2 * BLOCK], valid,
                                sink_ref[0, 2 * c + half])[0].astype(BF16) for half in (0, 1)], axis=1)
                 for c in groups]
        for c in groups:
            acc = jnp.dot(probs[c], vcat[2 * c // grp_sz], preferred_element_type=F32)
            attn_ref[:, c * LANES:(c + 1) * LANES] = acc
            col = c * LANES
            ga = ga_refs[col // COLT][:, col % COLT:col % COLT + LANES]
            mix_ref[:, c * LANES:(c + 1) * LANES] = (acc * (ga * _sigmoid(ga))).astype(BF16)

    own = lambda n: (n, 0)
    prev = lambda n: (jnp.maximum(n - 1, 0), 0)
    kvs = lambda imap: pl.BlockSpec((BLOCK, KV_WIDTH), imap)
    ga_specs = [pl.BlockSpec((BLOCK, COLT), functools.partial(lambda n, j: (n, ga_off + j), j=j))
                for j in range(n_ga)]
    return _call(body, name="attn_fwd", grid=(nb,),
                 in_specs=[pl.BlockSpec((BLOCK, aw), own), kvs(own), kvs(prev), kvs(own), kvs(prev)]
                 + ga_specs + [pl.BlockSpec(memory_space=pltpu.SMEM)],
                 out_specs=[pl.BlockSpec((BLOCK, aw), own), pl.BlockSpec((BLOCK, aw), own)],
                 out_shape=[jax.ShapeDtypeStruct((s, aw), F32), jax.ShapeDtypeStruct((s, d), BF16)],
                 compiler_params=_params(("parallel",)))(qs, ks, ks, vb, vb, *([z] * n_ga), sinks)


def gate_bwd(d_mix, attn, z, aw, ga_off, after=()):
    s, in_w = z.shape
    tm = _tile(s, 512)

    def body(dm_ref, at_ref, ga_ref, do_ref, dz_ref):
        ga = ga_ref[...]
        sig = _sigmoid(ga)
        dm = dm_ref[...]
        do_ref[...] = (dm * (ga * sig)).astype(BF16)
        dz_ref[...] = ((dm * at_ref[...]) * (sig * (1.0 + ga * (1.0 - sig)))).astype(BF16)

    blk = pl.BlockSpec((tm, COLT), lambda i, j: (i, j))
    gab = pl.BlockSpec((tm, COLT), lambda i, j: (i, ga_off + j))
    return _call_after(body, after, name="gate_bwd", grid=(s // tm, aw // COLT),
                       in_specs=[blk, blk, gab], out_specs=[blk, gab],
                       out_shape=[jax.ShapeDtypeStruct((s, aw), BF16), jax.ShapeDtypeStruct((s, in_w), BF16)],
                       compiler_params=_params(("parallel", "parallel")))(d_mix, attn, z)


def attn_bwd(qs, ks, vb, d_o, sinks, aw):
    s = qs.shape[0]
    nb = s // BLOCK
    nq = aw // HEAD_DIM
    grp_sz = nq // N_KV_HEADS

    def body(q_ref, ko_ref, kp_ref, vo_ref, vp_ref, do_ref, sink_ref, dq_ref, dk_ref, dv_ref, ds_ref,
             ck_ref, cv_ref):
        n = pl.program_id(0)

        @pl.when(n == 0)
        def _():
            ds_ref[...] = jnp.zeros_like(ds_ref)
            dk_ref[...] = jnp.zeros_like(dk_ref)
            dv_ref[...] = jnp.zeros_like(dv_ref)

        @pl.when(n < nb)
        def _():
            lane_idx = lax.broadcasted_iota(jnp.int32, (2 * BLOCK, LANES), 1)
            lane_row = lax.broadcasted_iota(jnp.int32, (1, LANES), 1)
            kpl = _placed(jnp.concatenate([kp_ref[...], ko_ref[...]], axis=0), lane_idx)
            vpl = _placed(jnp.concatenate([vp_ref[...], vo_ref[...]], axis=0), lane_idx)
            valid = _valid_mask(n)
            ds_row = jnp.zeros((1, LANES), F32)
            wide = 2 * BLOCK
            groups = range(nq // 2)
            per_kv = grp_sz // 2
            kcat = [_stack_halves(kpl, kh) for kh in range(N_KV_HEADS)]
            vcat = [_stack_halves(vpl, kh) for kh in range(N_KV_HEADS)]
            qg = [q_ref[:, c * LANES:(c + 1) * LANES] for c in groups]
            dog = [do_ref[:, c * LANES:(c + 1) * LANES] for c in groups]
            scs = [lax.dot_general(qg[c], kcat[c // per_kv], NT, preferred_element_type=F32) for c in groups]
            dps = [lax.dot_general(dog[c], vcat[c // per_kv], NT, preferred_element_type=F32) for c in groups]
            ds_pairs, p_pairs = [], []
            for c in groups:
                probs, dss = [], []
                for half in (0, 1):
                    h = 2 * c + half
                    cols = slice(half * wide, (half + 1) * wide)
                    p, p_sink = _softmax_with_sink(scs[c][:, cols], valid, sink_ref[0, h])
                    delta = jnp.sum(p * dps[c][:, cols], axis=-1, keepdims=True)
                    dss.append((p * (dps[c][:, cols] - delta)).astype(BF16))
                    probs.append(p.astype(BF16))
                    dsink = -jnp.sum(p_sink * delta, axis=0, keepdims=True)
                    ds_row += jnp.where(lane_row == h, dsink, 0.0)
                ds_pairs.append(jnp.concatenate(dss, axis=1))
                p_pairs.append(jnp.concatenate(probs, axis=1))
            for c in groups:
                dq_ref[:, c * LANES:(c + 1) * LANES] = jnp.dot(ds_pairs[c], kcat[c // per_kv],
                                                               preferred_element_type=F32)
            dkts = [lax.dot_general(qg[c], ds_pairs[c], TN, preferred_element_type=F32) for c in groups]
            dvts = [lax.dot_general(dog[c], p_pairs[c], TN, preferred_element_type=F32) for c in groups]
            dkt, dvt = [], []
            for kh in range(N_KV_HEADS):
                for parts, out in ((dkts, dkt), (dvts, dvt)):
                    tot = parts[kh * per_kv]
                    for c in range(kh * per_kv + 1, (kh + 1) * per_kv):
                        tot = tot + parts[c]
                    out.append(tot[:HEAD_DIM, :wide] + tot[HEAD_DIM:, wide:])
            ds_ref[0:1, :] += ds_row
            back = lambda t: jnp.concatenate(
                [jnp.concatenate(t[2 * g:2 * g + 2], axis=0).T for g in range(N_KV_HEADS // 2)], axis=1)
            dk_band, dv_band = back(dkt), back(dvt)

            @pl.when(n > 0)
            def _():
                dk_ref[...] = ck_ref[...] + dk_band[:BLOCK]
                dv_ref[...] = cv_ref[...] + dv_band[:BLOCK]

            ck_ref[...] = dk_band[BLOCK:]
            cv_ref[...] = dv_band[BLOCK:]

        @pl.when(n == nb)
        def _():
            dk_ref[...] = ck_ref[...]
            dv_ref[...] = cv_ref[...]

    own = lambda n: (jnp.minimum(n, nb - 1), 0)
    prev = lambda n: (jnp.clip(n - 1, 0, nb - 1), 0)
    done = lambda n: (jnp.maximum(n - 1, 0), 0)
    kvs = lambda imap: pl.BlockSpec((BLOCK, KV_WIDTH), imap)
    return _call(body, name="attn_bwd", grid=(nb + 1,),
                 in_specs=[pl.BlockSpec((BLOCK, aw), own), kvs(own), kvs(prev), kvs(own), kvs(prev),
                           pl.BlockSpec((BLOCK, aw), own), pl.BlockSpec(memory_space=pltpu.SMEM)],
                 out_specs=[pl.BlockSpec((BLOCK, aw), own), kvs(done), kvs(done),
                            pl.BlockSpec((SUBLANES, LANES), lambda n: (0, 0))],
                 out_shape=[jax.ShapeDtypeStruct((s, aw), F32), jax.ShapeDtypeStruct((s, KV_WIDTH), F32),
                            jax.ShapeDtypeStruct((s, KV_WIDTH), F32), jax.ShapeDtypeStruct((SUBLANES, LANES), F32)],
                 scratch_shapes=[pltpu.VMEM((BLOCK, KV_WIDTH), F32), pltpu.VMEM((BLOCK, KV_WIDTH), F32)],
                 compiler_params=_params(("arbitrary",)))(qs, ks, ks, vb, vb, d_o, sinks)


def conv_fwd(z, conv_w, mix, aw, cw, b_off):
    s = z.shape[0]
    tm = _tile(s, 512)
    nseg = cw // COLT
    hb = tm // SUBLANES

    def body(b_ref, c_ref, h_ref, g_ref, cp_ref, hp_ref, w_ref, mix_in, mix_ref):
        i = pl.program_id(0)
        u = c_ref[...] * h_ref[...]
        up = jnp.where(i > 0, cp_ref[...] * hp_ref[...], 0.0)
        ext = jnp.concatenate([up, u], axis=0)
        um1 = pltpu.roll(ext, 1, 0)[SUBLANES:]
        um2 = pltpu.roll(ext, 2, 0)[SUBLANES:]
        w = w_ref[...]
        cv = w[0:1] * um2 + w[1:2] * um1 + w[2:3] * u
        g = g_ref[...]
        mix_ref[...] = ((b_ref[...] * cv) * (g * _sigmoid(g))).astype(BF16)

    seg = lambda k: pl.BlockSpec((tm, COLT), functools.partial(lambda i, j, k: (i, b_off + k * nseg + j), k=k))
    halo = lambda k: pl.BlockSpec(
        (SUBLANES, COLT), functools.partial(lambda i, j, k: (jnp.maximum(i * hb - 1, 0), b_off + k * nseg + j), k=k))
    return _call(body, name="conv_fwd", grid=(s // tm, nseg),
                 in_specs=[seg(0), seg(1), seg(2), seg(3), halo(1), halo(2),
                           pl.BlockSpec((SUBLANES, COLT), lambda i, j: (0, j)), _hbm()],
                 out_specs=pl.BlockSpec((tm, COLT), lambda i, j: (i, aw // COLT + j)),
                 out_shape=jax.ShapeDtypeStruct(mix.shape, BF16),
                 input_output_aliases={7: 0},
                 compiler_params=_params(("parallel", "parallel")))(z, z, z, z, z, z, conv_w, mix)


def conv_bwd(z, d_mix, conv_w, dz, aw, cw, b_off):
    s = z.shape[0]
    tm = _tile(s, 512)
    nseg = cw // COLT
    hb = tm // SUBLANES
    n_row = s // tm
    last_h = s // SUBLANES - 1
    n_step = nseg * n_row

    def body(b_ref, c_ref, h_ref, g_ref, cp_ref, hp_ref, bn_ref, gn_ref, dm_ref, dmn_ref, w_ref, dz_in,
             dz_ref, acc_ref, stash_ref, sems):
        j = pl.program_id(0)
        i = pl.program_id(1)
        step = j * n_row + i
        slot = step % 2

        def copies(from_slot, at_step):
            jj, ii = at_step // n_row, at_step % n_row
            rows = pl.ds(pl.multiple_of(ii * tm, tm), tm)
            return [pltpu.make_async_copy(
                stash_ref.at[from_slot, q],
                dz_ref.at[rows, pl.ds(pl.multiple_of((b_off + q * nseg + jj) * COLT, COLT), COLT)],
                sems.at[from_slot, q]) for q in range(4)]

        @pl.when(step >= 2)
        def _():
            for cp in copies(slot, step - 2):
                cp.wait()

        cc, hh, bb, g = c_ref[...], h_ref[...], b_ref[...], g_ref[...]
        w = w_ref[...]
        u = cc * hh
        up = jnp.where(i > 0, cp_ref[...] * hp_ref[...], 0.0)
        ext = jnp.concatenate([up, u], axis=0)
        um1 = pltpu.roll(ext, 1, 0)[SUBLANES:]
        um2 = pltpu.roll(ext, 2, 0)[SUBLANES:]
        cv = w[0:1] * um2 + w[1:2] * um1 + w[2:3] * u
        sig = _sigmoid(g)
        sg = g * sig
        dm = dm_ref[...]
        d_cv = (dm * bb) * sg
        gn = gn_ref[...]
        d_cv_next = jnp.where(i < n_row - 1, (dmn_ref[...] * bn_ref[...]) * (gn * _sigmoid(gn)), 0.0)
        ext2 = jnp.concatenate([d_cv, d_cv_next], axis=0)
        dp1 = pltpu.roll(ext2, tm + SUBLANES - 1, 0)[:tm]
        dp2 = pltpu.roll(ext2, tm + SUBLANES - 2, 0)[:tm]
        d_u = w[2:3] * d_cv + w[1:2] * dp1 + w[0:1] * dp2
        stash_ref[slot, 0] = ((dm * cv) * sg).astype(BF16)
        stash_ref[slot, 1] = (d_u * hh).astype(BF16)
        stash_ref[slot, 2] = (d_u * cc).astype(BF16)
        stash_ref[slot, 3] = (((dm * bb) * cv) * (sig * (1.0 + g * (1.0 - sig)))).astype(BF16)
        for cp in copies(slot, step):
            cp.start()

        @pl.when(i == 0)
        def _():
            acc_ref[...] = jnp.zeros_like(acc_ref)

        acc_ref[0:1, :] += jnp.sum(d_cv * um2, axis=0, keepdims=True)
        acc_ref[1:2, :] += jnp.sum(d_cv * um1, axis=0, keepdims=True)
        acc_ref[2:3, :] += jnp.sum(d_cv * u, axis=0, keepdims=True)

        @pl.when(step == n_step - 1)
        def _():
            if n_step >= 2:
                for cp in copies(1 - slot, step - 1):
                    cp.wait()
            for cp in copies(slot, step):
                cp.wait()

    seg = lambda q: pl.BlockSpec((tm, COLT), functools.partial(lambda j, i, q: (i, b_off + q * nseg + j), q=q))
    halo_p = lambda q: pl.BlockSpec(
        (SUBLANES, COLT),
        functools.partial(lambda j, i, q: (jnp.maximum(i * hb - 1, 0), b_off + q * nseg + j), q=q))
    halo_n = lambda q: pl.BlockSpec(
        (SUBLANES, COLT),
        functools.partial(lambda j, i, q: (jnp.minimum((i + 1) * hb, last_h), b_off + q * nseg + j), q=q))
    return _call(body, name="conv_bwd", grid=(nseg, n_row),
                 in_specs=[seg(0), seg(1), seg(2), seg(3), halo_p(1), halo_p(2), halo_n(0), halo_n(3),
                           pl.BlockSpec((tm, COLT), lambda j, i: (i, aw // COLT + j)),
                           pl.BlockSpec((SUBLANES, COLT),
                                        lambda j, i: (jnp.minimum((i + 1) * hb, last_h), aw // COLT + j)),
                           pl.BlockSpec((SUBLANES, COLT), lambda j, i: (0, j)), _hbm()],
                 out_specs=[_hbm(), pl.BlockSpec((SUBLANES, COLT), lambda j, i: (0, j))],
                 out_shape=[jax.ShapeDtypeStruct(dz.shape, BF16), jax.ShapeDtypeStruct((SUBLANES, cw), F32)],
                 input_output_aliases={11: 0},
                 scratch_shapes=[pltpu.VMEM((2, 4, tm, COLT), BF16), pltpu.SemaphoreType.DMA((2, 4))],
                 compiler_params=_params(("arbitrary", "arbitrary")))(
                     z, z, z, z, z, z, z, z, d_mix, d_mix, conv_w, dz)


def _place():
    x, y, c = lax.axis_index("x"), lax.axis_index("y"), lax.axis_index("c")
    chips = [(1 - x, y), (x, 1 - y), (1 - x, 1 - y)]
    return x, y, c, chips


def _remote(src, dst, send_sem, recv_sem, device):
    return pltpu.make_async_remote_copy(src_ref=src, dst_ref=dst, send_sem=send_sem, recv_sem=recv_sem,
                                        device_id=device, device_id_type=MESH)


def plan_gather_ici(n_split):
    def plan(refs, send, recv):
        x, y, c, chips = _place()
        me = 2 * x + y
        mine, theirs = [], []
        for w, ref in enumerate(refs):
            for j, (cx, cy) in enumerate(chips):
                def part(slot):
                    if w >= n_split:
                        return ref.at[slot]
                    hr = ref.shape[1] // 2
                    return ref.at[slot, pl.ds(c * hr, hr)]
                k = 3 * w + j
                mine.append(_remote(part(me), part(me), send.at[k], recv.at[k], (cx, cy, c)))
                theirs.append(_remote(part(2 * cx + cy), part(2 * cx + cy), send.at[k], recv.at[k], (cx, cy, c)))
        return mine, theirs
    return plan


def plan_shard_ici(j):
    def plan(refs, send, recv):
        _, _, c, chips = _place()
        own, land = refs
        hr = own.shape[0] // 2
        rows = pl.ds(c * hr, hr)
        cp = _remote(own.at[rows], land.at[rows], send.at[0], recv.at[0], (*chips[j], c))
        return [cp], [cp]
    return plan


def plan_shard_pass(refs, send, recv):
    x, y, c, _ = _place()
    land, = refs
    hr = land.shape[0] // 2
    half = lambda core: land.at[pl.ds(core * hr, hr)]
    return ([_remote(half(c), half(c), send.at[0], recv.at[0], (x, y, 1 - c))],
            [_remote(half(1 - c), half(1 - c), send.at[0], recv.at[0], (x, y, 1 - c))])


def plan_gather_pass(refs, send, recv):
    x, y, c, chips = _place()
    mine, theirs = [], []
    for w, ref in enumerate(refs):
        hr = ref.shape[1] // 2
        for j, (cx, cy) in enumerate(chips):
            half = lambda core: ref.at[2 * cx + cy, pl.ds(core * hr, hr)]
            k = 3 * w + j
            mine.append(_remote(half(c), half(c), send.at[k], recv.at[k], (x, y, 1 - c)))
            theirs.append(_remote(half(1 - c), half(1 - c), send.at[k], recv.at[k], (x, y, 1 - c)))
    return mine, theirs


def plan_swap(refs, send, recv):
    x, y, c, _ = _place()
    nw = len(refs) // 2
    mine = []
    for w in range(nw):
        hr = refs[w].shape[1] // 2
        mine.append(_remote(refs[w].at[:, pl.ds((1 - c) * hr, hr), :], refs[nw + w], send.at[w], recv.at[w],
                            (x, y, 1 - c)))
    return mine, mine


def plan_scatter(refs, send, recv):
    x, y, c, chips = _place()
    me = 2 * x + y
    nw = len(refs) // 2
    mine, theirs = [], []
    for w in range(nw):
        for j, (cx, cy) in enumerate(chips):
            k = 3 * w + j
            mine.append(_remote(refs[w].at[2 * cx + cy], refs[nw + w].at[me], send.at[k], recv.at[k], (cx, cy, c)))
            theirs.append(_remote(refs[w].at[me], refs[nw + w].at[2 * cx + cy], send.at[k], recv.at[k], (cx, cy, c)))
    return mine, theirs


def plan_join(refs, send, recv):
    x, y, c, _ = _place()
    mine, theirs = [], []
    for w, ref in enumerate(refs):
        hr = ref.shape[0] // 2
        half = lambda core: ref.at[pl.ds(core * hr, hr)]
        mine.append(_remote(half(c), half(c), send.at[w], recv.at[w], (x, y, 1 - c)))
        theirs.append(_remote(half(1 - c), half(1 - c), send.at[w], recv.at[w], (x, y, 1 - c)))
    return mine, theirs


def exchange(name, plan, arrays, n):
    na = len(arrays)

    def body(*refs):
        mine, theirs = plan(refs[:na], refs[2 * na], refs[2 * na + 1])
        for cp in mine:
            cp.start()
        for cp in theirs:
            cp.wait_recv()
        for cp in mine:
            cp.wait_send()

    return _call(body, name=name, in_specs=[_hbm()] * na, out_specs=[_hbm()] * na,
                 out_shape=[jax.ShapeDtypeStruct(a.shape, a.dtype) for a in arrays],
                 input_output_aliases={i: i for i in range(na)},
                 scratch_shapes=[pltpu.SemaphoreType.DMA((n,)), pltpu.SemaphoreType.DMA((n,))])(*arrays)


def exchange_start(name, groups, arrays, after=()):
    na, ng = len(arrays), len(groups)

    def body(*refs):
        for g, (plan, idx, _) in enumerate(groups):
            mine, _ = plan([refs[i] for i in idx], refs[na + 2 * g], refs[na + 2 * g + 1])
            for cp in mine:
                cp.start()
        refs[-1][...] = jnp.zeros_like(refs[-1])

    hbm = pl.BlockSpec(memory_space=pltpu.HBM)
    sem = pl.BlockSpec(memory_space=pltpu.SEMAPHORE)
    sem_types = [pltpu.SemaphoreType.DMA((n,)) for _, _, n in groups for _ in (0, 1)]
    outs = _call_after(body, after, name=name, in_specs=[hbm] * na,
                       out_specs=[sem] * (2 * ng) + [hbm] * na + [pl.BlockSpec(memory_space=pltpu.VMEM)],
                       out_shape=sem_types + [pltpu.HBM(a.shape, a.dtype) for a in arrays]
                       + [jax.ShapeDtypeStruct((SUBLANES, LANES), F32)],
                       input_output_aliases={i: i + 2 * ng for i in range(na)},
                       compiler_params=pltpu.CompilerParams(
                           has_side_effects=pltpu.SideEffectType.DATAFLOW_SIDE_EFFECTING))(
                               *[pltpu.with_memory_space_constraint(a, pltpu.HBM) for a in arrays])
    sems = [(outs[2 * g], outs[2 * g + 1]) for g in range(ng)]
    return sems, list(outs[2 * ng:-1]), outs[-1]


def exchange_wait(name, plan, sems, arrays, after):
    send_sems, recv_sems = sems
    na = len(arrays)

    def body(*refs):
        mine, theirs = plan(refs[:na], refs[na], refs[na + 1])
        for cp in mine:
            cp.wait_send()
        for cp in theirs:
            cp.wait_recv()

    hbm = pl.BlockSpec(memory_space=pltpu.HBM)
    sem = pl.BlockSpec(memory_space=pltpu.SEMAPHORE)
    return _call(body, name=name, in_specs=[hbm] * na + [sem, sem, _hbm()], out_specs=[hbm] * na,
                 out_shape=[pltpu.HBM(a.shape, a.dtype) for a in arrays],
                 input_output_aliases={i: i for i in range(na)},
                 compiler_params=pltpu.CompilerParams(
                     has_side_effects=pltpu.SideEffectType.DATAFLOW_SIDE_EFFECTING))(
                         *arrays, send_sems, recv_sems, after)


def plan_allgather_small(refs, send, recv):
    x, y, c, _ = _place()
    buf, = refs
    mine, theirs = [], []
    flips = [(fx, fy, fc) for fx in (0, 1) for fy in (0, 1) for fc in (0, 1)][1:]
    for k, (fx, fy, fc) in enumerate(flips):
        peer = (x ^ fx, y ^ fy, c ^ fc)
        slot = lambda px, py, pc: buf.at[4 * px + 2 * py + pc]
        mine.append(_remote(slot(x, y, c), slot(x, y, c), send.at[k], recv.at[k], peer))
        theirs.append(_remote(slot(*peer), slot(*peer), send.at[k], recv.at[k], peer))
    return mine, theirs


def add_sibling(grad, got, core, name):
    _, r, c = grad.shape
    hr = r // 2
    tr = _tile(hr, 256)
    nblk = hr // tr

    def body(core_ref, g_ref, o_ref, out_ref):
        out_ref[...] = (g_ref[...].astype(F32) + o_ref[...].astype(F32)).astype(BF16)

    grid_spec = pltpu.PrefetchScalarGridSpec(
        num_scalar_prefetch=1, grid=(N_CHIPS, nblk),
        in_specs=[pl.BlockSpec((None, tr, c), lambda t, i, core_ref: (t, core_ref[0] * nblk + i, 0)),
                  pl.BlockSpec((None, tr, c), lambda t, i, core_ref: (t, i, 0))],
        out_specs=pl.BlockSpec((None, tr, c), lambda t, i, core_ref: (t, i, 0)))
    return _call(body, name=name, grid_spec=grid_spec,
                 out_shape=jax.ShapeDtypeStruct((N_CHIPS, hr, c), BF16),
                 compiler_params=_params(("parallel", "parallel")))(core, grad, got)


def sum_chips(mine, owned, place, name):
    _, hr, c = mine.shape
    tr = _tile(hr, 256)
    nblk = hr // tr

    def body(place_ref, m_ref, o1_ref, o2_ref, o3_ref, out_ref):
        acc = m_ref[...].astype(F32)
        for o_ref in (o1_ref, o2_ref, o3_ref):
            acc = acc + o_ref[...].astype(F32)
        out_ref[...] = acc

    other = lambda k: pl.BlockSpec((None, tr, c), lambda i, place_ref: ((place_ref[0] + k) % N_CHIPS, i, 0))
    grid_spec = pltpu.PrefetchScalarGridSpec(
        num_scalar_prefetch=1, grid=(nblk,),
        in_specs=[other(0), other(1), other(2), other(3)],
        out_specs=pl.BlockSpec((tr, c), lambda i, place_ref: (place_ref[1] * nblk + i, 0)))
    return _call(body, name=name, grid_spec=grid_spec,
                 out_shape=jax.ShapeDtypeStruct((2 * hr, c), F32),
                 compiler_params=_params(("parallel",)))(place, mine, owned, owned, owned)


def sum_devices(gathered):
    _, rows, width = gathered.shape

    def body(g_ref, out_ref):
        acc = g_ref[0]
        for dev in range(1, 8):
            acc = acc + g_ref[dev]
        out_ref[...] = acc
        tail = acc[SUBLANES:]
        out_ref[SUBLANES:, :] = jnp.broadcast_to(jnp.sum(tail, axis=1, keepdims=True), tail.shape)

    return _call(body, name="sum_devices",
                 in_specs=[pl.BlockSpec(memory_space=pltpu.VMEM)],
                 out_specs=pl.BlockSpec(memory_space=pltpu.VMEM),
                 out_shape=jax.ShapeDtypeStruct((rows, width), F32))(gathered)


def _rope_tables(s):
    half = ROT_DIM // 2
    inv_freq = jnp.power(jnp.float32(ROPE_THETA), -jnp.arange(half, dtype=F32) * 2.0 / ROT_DIM)
    freq64 = jnp.concatenate([inv_freq, inv_freq, jnp.zeros((HEAD_DIM - ROT_DIM,), F32)])
    freq = jnp.concatenate([freq64, freq64])[None, :]
    dim = (jnp.arange(LANES) % HEAD_DIM)[None, :]
    ang = jnp.arange(s).astype(F32)[:, None] * freq
    cos, sin = jnp.cos(ang), jnp.sin(ang)
    return cos, jnp.where(dim < half, -sin, 0.0), jnp.where((dim >= half) & (dim < ROT_DIM), sin, 0.0)


def _pad_rows(a, rows):
    return jnp.pad(a, ((0, rows - a.shape[0]), (0, 0)))


def _pad_cols(a, cols):
    return jnp.pad(a, ((0, 0), (0, cols - a.shape[1])))


def kernel(x, p, norm_gain, w_in, q_norm_gain, k_norm_gain, attn_sinks, conv_w, w_out, ple_gate_norm_gain, w_ple_gate, b_ple_gate, w_ple_proj, ple_norm_gain, loss_target, m_norm_gain, m_w_in, m_q_norm_gain, m_k_norm_gain, m_attn_sinks, m_conv_w, m_w_out, m_ple_gate_norm_gain, m_w_ple_gate, m_b_ple_gate, m_w_ple_proj, m_ple_norm_gain, v_norm_gain, v_w_in, v_q_norm_gain, v_k_norm_gain, v_attn_sinks, v_conv_w, v_w_out, v_ple_gate_norm_gain, v_w_ple_gate, v_b_ple_gate, v_w_ple_proj, v_ple_norm_gain):
    x2, p2, tgt = x[0], p[0, 0], loss_target[0]
    s, d = x2.shape
    ple = p2.shape[1]
    aw = d // 2
    cw = d - aw
    nq = aw // HEAD_DIM
    sh = w_in.shape[2]
    in_w = N_CHIPS * sh
    dq = d // N_CHIPS
    cq = cw // N_CHIPS
    ga_off = (aw + 2 * KV_WIDTH) // COLT
    b_off = (2 * aw + 2 * KV_WIDTH) // COLT
    assert in_w == 2 * aw + 2 * KV_WIDTH + 4 * cw and aw % COLT == 0 and cw % COLT == 0
    assert s % BLOCK == 0 and sh % LANES == 0 and nq % (2 * N_KV_HEADS) == 0

    core = lax.axis_index("c").astype(jnp.int32).reshape(1)
    chip = 2 * lax.axis_index("x") + lax.axis_index("y")

    chip1 = chip.astype(jnp.int32).reshape(1)
    place = jnp.concatenate([chip1, core])
    w_in_own = cast_bf16(w_in[0], "cast_w_in")
    rest = [cast_into_slot(w_out[0], chip1, "cast_w_out"), cast_into_slot(w_ple_gate[0], chip1, "cast_w_pg"),
            cast_into_slot(w_ple_proj[0], chip1, "cast_w_pp"),
            lax.dynamic_update_slice(jnp.zeros((N_CHIPS, SUBLANES, cq), F32),
                                     _pad_rows(conv_w[0], SUBLANES)[None], (chip, 0, 0))]
    order = jnp.stack([chip, chip ^ 2, chip ^ 1, chip ^ 3]).astype(jnp.int32)
    wi_sems, wi_arrs, wi_token = exchange_start(
        "gather_wi_start", [(plan_shard_ici(j), [0, 1 + j], 1) for j in range(3)],
        [w_in_own] + [lax.empty((d, sh), BF16) for _ in range(3)])
    rest_sems, rest, rest_token = exchange_start(
        "gather_rest_start", [(plan_gather_ici(3), [0, 1, 2, 3], 12)], rest, after=(wi_token,))

    tm = _tile(s, 1024)
    tn = _tile(d, 1024)
    tk = _tile(d, 2048)
    ts = _tile(s, 2048)
    h = rms_fwd(x2, norm_gain, "rms_fwd_x", after=(rest_token,))
    tabs = _rope_tables(s)
    qg = jnp.tile(q_norm_gain, (1, LANES // HEAD_DIM))
    kg = jnp.tile(k_norm_gain, (1, LANES // HEAD_DIM))
    seg_i = jnp.arange(SEG) // HEAD_DIM
    segb = (seg_i[:, None] == seg_i[None, :]).astype(BF16)
    z = in_proj_part(h, wi_arrs[0], None, order, 0, in_w)
    w_shards = [wi_arrs[0]]
    for j in range(3):
        w_shards[0], land = exchange_wait("gather_wi_wait%d" % j, plan_shard_ici(j), wi_sems[j],
                                          [w_shards[0], wi_arrs[1 + j]], z)
        land, = exchange("gather_wi_pass%d" % j, plan_shard_pass, [land], 1)
        z = in_proj_part(h, land, z, order, 1 + j, in_w)
        w_shards.append(land)
    wo_all, wg_all, wp_all, conv_all = exchange_wait("gather_rest_wait", plan_gather_ici(3), rest_sems[0], rest, z)
    pass_sems, passed, pass_token = exchange_start(
        "gather_rest_pass_start", [(plan_gather_pass, [0, 1, 2], 9)], [wo_all, wg_all, wp_all])
    conv_full = conv_all.transpose(1, 0, 2).reshape(SUBLANES, cw)
    qs, ks, vb = qk_prep_fwd(z, tabs, qg, kg, segb, aw, after=(pass_token,))
    attn, mix = attn_fwd(qs, ks, vb, z, attn_sinks, aw, d, ga_off)
    mix = conv_fwd(z, conv_full, mix, aw, cw, b_off)
    wo_all, wg_all, wp_all = exchange_wait("gather_rest_pass_wait", plan_gather_pass, pass_sems[0], passed, mix)
    wo_full = wo_all.reshape(d, d)
    wg_full = wg_all.reshape(d, d)
    sq = lambda shape: dict(
        a_spec=pl.BlockSpec((tm, tk), lambda i, j, k: (i, k)),
        o_spec=pl.BlockSpec((tm, tn), lambda i, j, k: (i, j)),
        out_shape=jax.ShapeDtypeStruct(shape, F32))
    x1 = matmul(mix, wo_full, grid=(s // tm, d // tn, d // tk),
                b_spec=pl.BlockSpec((tk, tn), lambda i, j, k: (k, j)), dims=NN, name="mm_x1",
                res=x2, res_spec=pl.BlockSpec((tm, tn), lambda i, j, k: (i, j)), **sq((s, d)))
    hg = rms_fwd(x1, ple_gate_norm_gain, "rms_fwd_x1")
    gl = matmul(hg, wg_full, grid=(s // tm, d // tn, d // tk),
                b_spec=pl.BlockSpec((tk, tn), lambda i, j, k: (k, j)), dims=NN, name="mm_gl", **sq((s, d)))
    pq = d // N_CHIPS

    d_gl, d_pe, dy, acc_head = head_fwd_bwd(x1, gl, p2, wp_all, tgt, b_ple_gate, ple_norm_gain)
    d_hg = matmul(d_gl, wg_full, grid=(s // tm, d // tn, d // tk),
                  b_spec=pl.BlockSpec((tn, tk), lambda i, j, k: (j, k)), dims=NT, name="mm_d_hg", **sq((s, d)))
    wgrad = lambda a_cols, shape3, o_spec, b_cols, name, a, b, grid: matmul(
        a, b, grid=grid,
        a_spec=pl.BlockSpec((ts, a_cols), lambda i, j, k: (k, i)),
        b_spec=pl.BlockSpec((ts, b_cols), lambda i, j, k: (k, j)),
        o_spec=o_spec, out_shape=jax.ShapeDtypeStruct(shape3, BF16), dims=TN, name=name)
    g_wg = wgrad(tn, (d, d), pl.BlockSpec((tn, tn), lambda i, j, k: (i, j)), tn, "mm_g_wg", hg, d_gl,
                 (d // tn, d // tn, s // ts))
    g_wp = wgrad(ple, (N_CHIPS, ple, pq), pl.BlockSpec((None, ple, pq), lambda i, j, k: (j, 0, 0)), pq,
                 "mm_g_wp", p2, d_pe, (1, N_CHIPS, s // ts))
    d_x1, d_x1b, acc_g = rms_bwd_add(d_hg, x1, dy, ple_gate_norm_gain, "rms_bwd_x1", True)
    d_mix = matmul(d_x1b, wo_full, grid=(s // tm, d // tn, d // tk),
                   b_spec=pl.BlockSpec((tn, tk), lambda i, j, k: (j, k)), dims=NT, name="mm_d_mix", **sq((s, d)))
    g_wo = wgrad(tn, (d, d), pl.BlockSpec((tn, tn), lambda i, j, k: (i, j)), tn, "mm_g_wo", mix, d_x1b,
                 (d // tn, d // tn, s // ts))
    def reduce_start(tag, grads):
        n = len(grads)
        lands = [lax.empty((N_CHIPS, a.shape[1] // 2, a.shape[2]), BF16) for a in grads]
        swapped = exchange("swap_" + tag, plan_swap, list(grads) + lands, n)
        parts = [add_sibling(g, o, core, "add_sibling_%s%d" % (tag, i))
                 for i, (g, o) in enumerate(zip(swapped[:n], swapped[n:]))]
        return exchange_start("scatter_%s_start" % tag, [(plan_scatter, list(range(2 * n)), 3 * n)],
                              parts + [lax.empty(a.shape, BF16) for a in parts])

    def reduce_sum(tag, handle, after):
        sems, arrays, _ = handle
        n = len(arrays) // 2
        got = exchange_wait("scatter_%s_wait" % tag, plan_scatter, sems[0], arrays, after)
        return [sum_chips(pt, o, place, "sum_chips_%s%d" % (tag, i))
                for i, (pt, o) in enumerate(zip(got[:n], got[n:]))]

    early = reduce_start("early", [g_wo.reshape(N_CHIPS, dq, d), g_wg.reshape(N_CHIPS, dq, d), g_wp])
    d_o, dz = gate_bwd(d_mix, attn, z, aw, ga_off, after=(early[-1],))
    dqs, dks, dvs, acc_sink = attn_bwd(qs, ks, vb, d_o, attn_sinks, aw)
    dz, acc_qk = qk_prep_bwd(z, dqs, dks, dvs, tabs, qg, kg, segb, dz, aw)
    dz, acc_conv = conv_bwd(z, d_mix, conv_full, dz, aw, cw, b_off)
    join_sems, joining, join_token = exchange_start(
        "join_early_start", [(plan_join, [0, 1, 2], 3)], reduce_sum("early", early, dz))
    g_wi = matmul(h, dz, grid=(d // tn, N_CHIPS, s // ts),
                  a_spec=pl.BlockSpec((ts, tn), lambda i, j, k: (k, i)),
                  b_spec=pl.BlockSpec((ts, sh), lambda i, j, k: (k, j)),
                  o_spec=pl.BlockSpec((None, tn, sh), lambda i, j, k: (j, i, 0)),
                  out_shape=jax.ShapeDtypeStruct((N_CHIPS, d, sh), BF16), dims=TN, name="mm_g_wi",
                  after=(join_token,))
    full_wo, full_wg, full_wp = exchange_wait("join_early_wait", plan_join, join_sems[0], joining, g_wi)
    late = reduce_start("late", [g_wi])
    d_h = in_proj_bwd(dz, w_shards, order, d, after=(late[-1],))
    grad_x, acc_x = rms_bwd_add(d_h, x2, d_x1, norm_gain, "rms_bwd_x", False)

    wsm = max(d, cw)
    misc = jnp.concatenate([acc_qk[0:1, :HEAD_DIM], acc_qk[1:2, :HEAD_DIM], acc_sink[0:1, :nq]], axis=1)
    small = jnp.concatenate([
        _pad_cols(acc_x[0:1], wsm), _pad_cols(acc_g[0:1], wsm), _pad_cols(acc_head[0:1], wsm),
        _pad_cols(acc_head[1:2], wsm), _pad_cols(misc, wsm), _pad_cols(acc_conv[0:3], wsm),
        _pad_rows(_pad_cols(acc_head[2:3], wsm), SUBLANES)], axis=0)
    device = 2 * chip + lax.axis_index("c")
    small_sems, small_bufs, small_token = exchange_start(
        "small_start", [(plan_allgather_small, [0], 7)],
        [lax.dynamic_update_slice(jnp.zeros((8,) + small.shape, F32), small[None], (device, 0, 0))])
    last_sems, last_halves, last_token = exchange_start(
        "join_late_start", [(plan_join, [0], 1)], reduce_sum("late", late, small_token))
    big, after = {}, (last_token,)
    for nm, g, w, m, v in (("w_out", full_wo, w_out, m_w_out, v_w_out),
                           ("w_ple_gate", full_wg, w_ple_gate, m_w_ple_gate, v_w_ple_gate),
                           ("w_ple_proj", full_wp, w_ple_proj, m_w_ple_proj, v_w_ple_proj)):
        stepped = adamw(g, w[0], m[0], v[0], "adamw_" + nm, after=after)
        big[nm], after = [o[None] for o in stepped], (stepped[1],)
    full_wi, = exchange_wait("join_late_wait", plan_join, last_sems[0], last_halves, after[0])
    big["w_in"] = [o[None] for o in adamw(full_wi, w_in[0], m_w_in[0], v_w_in[0], "adamw_w_in")]

    gathered, = exchange_wait("small_wait", plan_allgather_small, small_sems[0], small_bufs, big["w_in"][1])
    tot = sum_devices(gathered)
    loss = tot[SUBLANES, 0]

    def pack(vals):
        ng, pg, bg, eg, qgv, kgv, sk, cv = vals
        misc_v = jnp.concatenate([qgv, kgv, sk], axis=1)
        return jnp.concatenate([_pad_cols(ng, wsm), _pad_cols(pg, wsm), _pad_cols(bg, wsm), _pad_cols(eg, wsm),
                                _pad_cols(misc_v, wsm), _pad_cols(cv[0], wsm)], axis=0)

    g_small = jnp.concatenate(
        [tot[0:5], _pad_cols(lax.dynamic_slice(tot[5:8], (0, chip * cq), (3, cq)), wsm)], axis=0)
    w_small = pack((norm_gain, ple_gate_norm_gain, b_ple_gate, ple_norm_gain, q_norm_gain, k_norm_gain,
                    attn_sinks, conv_w))
    m_small = pack((m_norm_gain, m_ple_gate_norm_gain, m_b_ple_gate, m_ple_norm_gain, m_q_norm_gain,
                    m_k_norm_gain, m_attn_sinks, m_conv_w))
    v_small = pack((v_norm_gain, v_ple_gate_norm_gain, v_b_ple_gate, v_ple_norm_gain, v_q_norm_gain,
                    v_k_norm_gain, v_attn_sinks, v_conv_w))
    sm = adamw(g_small, w_small, m_small, v_small, "adamw_small")

    def unpack(a):
        return {"norm_gain": a[0:1, :d], "ple_gate_norm_gain": a[1:2, :d], "b_ple_gate": a[2:3, :d],
                "ple_norm_gain": a[3:4, :d], "q_norm_gain": a[4:5, :HEAD_DIM],
                "k_norm_gain": a[4:5, HEAD_DIM:2 * HEAD_DIM],
                "attn_sinks": a[4:5, 2 * HEAD_DIM:2 * HEAD_DIM + nq], "conv_w": a[5:8, :cq][None]}

    order = ("norm_gain", "w_in", "q_norm_gain", "k_norm_gain", "attn_sinks", "conv_w", "w_out",
             "ple_gate_norm_gain", "w_ple_gate", "b_ple_gate", "w_ple_proj", "ple_norm_gain")
    outs = [loss, grad_x[None]]
    for kind in range(4):
        table = unpack(sm[kind])
        for nm in order:
            outs.append(big[nm][kind] if nm in big else table[nm])
    return tuple(outs)
```

```python
import functools

import jax
import jax.numpy as jnp
from jax import lax
from jax.experimental import pallas as pl
from jax.experimental.pallas import tpu as pltpu

F32 = jnp.float32
BF16 = jnp.bfloat16
MESH = pl.DeviceIdType.MESH

HEAD_DIM = 64
N_KV_HEADS = 4
KV_WIDTH = N_KV_HEADS * HEAD_DIM
BLOCK = 128
ROT_DIM = 16
ROPE_THETA = 500000.0
EPS = 1e-6
NEG_INF = -1e30
N_CHIPS = 4
LANES = 128
SUBLANES = 8
COLT = 512
SEG = 256
VMEM_LIMIT = 48 * 1024 * 1024

ADAM_LR = 0.001
ADAM_B1 = 0.9
ADAM_B2 = 0.999
ADAM_EPS = 1e-08
ADAM_WD = 0.01
ADAM_STEP = 10

NN = (((1,), (0,)), ((), ()))
NT = (((1,), (1,)), ((), ()))
TN = (((0,), (0,)), ((), ()))


def _call(body, **kw):
    return pl.pallas_call(body, **kw)


def _call_after(body, after, **kw):
    n_in, n_after = len(kw["in_specs"]), len(after)
    kw["in_specs"] = list(kw["in_specs"]) + [pl.BlockSpec(memory_space=pl.ANY)] * n_after

    def body_after(*refs):
        body(*refs[:n_in], *refs[n_in + n_after:])

    call = _call(body_after, **kw)
    return lambda *args: call(*args, *after)


def _params(sem):
    return pltpu.CompilerParams(dimension_semantics=sem, vmem_limit_bytes=VMEM_LIMIT)


def _tile(n, pref):
    return pref if n % pref == 0 else n


def _sigmoid(v):
    return 1.0 / (1.0 + jnp.exp(-v))


def _hbm():
    return pl.BlockSpec(memory_space=pl.ANY)


def cast_bf16(a, name):
    r, c = a.shape
    tr = _tile(r, 512)

    def body(a_ref, o_ref):
        o_ref[...] = a_ref[...].astype(BF16)

    return _call(body, name=name, grid=(r // tr,),
                 in_specs=[pl.BlockSpec((tr, c), lambda i: (i, 0))],
                 out_specs=pl.BlockSpec((tr, c), lambda i: (i, 0)),
                 out_shape=jax.ShapeDtypeStruct((r, c), BF16),
                 compiler_params=_params(("parallel",)))(a)


def cast_into_slot(a, chip, name):
    r, c = a.shape
    tr = _tile(r, 512)

    def body(chip_ref, a_ref, o_ref):
        o_ref[...] = a_ref[...].astype(BF16)

    grid_spec = pltpu.PrefetchScalarGridSpec(
        num_scalar_prefetch=1, grid=(r // tr,),
        in_specs=[pl.BlockSpec((tr, c), lambda i, chip_ref: (i, 0))],
        out_specs=pl.BlockSpec((None, tr, c), lambda i, chip_ref: (chip_ref[0], i, 0)))
    return _call(body, name=name, grid_spec=grid_spec,
                 out_shape=jax.ShapeDtypeStruct((N_CHIPS, r, c), BF16),
                 compiler_params=_params(("parallel",)))(chip, a)


def rms_fwd(x, gain, name, after=()):
    s, d = x.shape
    tm = _tile(s, 512)

    def body(x_ref, g_ref, o_ref):
        xf = x_ref[...]
        r = lax.rsqrt(jnp.mean(xf * xf, axis=-1, keepdims=True) + EPS)
        o_ref[...] = ((xf * r) * g_ref[...]).astype(BF16)

    return _call_after(body, after, name=name, grid=(s // tm,),
                       in_specs=[pl.BlockSpec((tm, d), lambda i: (i, 0)),
                                 pl.BlockSpec((1, d), lambda i: (0, 0))],
                       out_specs=pl.BlockSpec((tm, d), lambda i: (i, 0)),
                       out_shape=jax.ShapeDtypeStruct((s, d), BF16),
                       compiler_params=_params(("parallel",)))(x, gain)


def rms_bwd_add(dyn, xin, add, gain, name, want_bf16):
    s, d = xin.shape
    tm = _tile(s, 256)

    def body(dy_ref, x_ref, a_ref, g_ref, *outs):
        i = pl.program_id(0)
        dx_ref, acc_ref = outs[0], outs[-1]
        xf = x_ref[...]
        r = lax.rsqrt(jnp.mean(xf * xf, axis=-1, keepdims=True) + EPS)
        xhat = xf * r
        dyv = dy_ref[...]
        gd = dyv * g_ref[...]
        dx = a_ref[...] + r * (gd - xhat * jnp.mean(xhat * gd, axis=-1, keepdims=True))
        dx_ref[...] = dx
        if want_bf16:
            outs[1][...] = dx.astype(BF16)

        @pl.when(i == 0)
        def _():
            acc_ref[...] = jnp.zeros_like(acc_ref)

        acc_ref[0:1, :] += jnp.sum(dyv * xhat, axis=0, keepdims=True)

    row = pl.BlockSpec((tm, d), lambda i: (i, 0))
    out_specs = [row] + ([row] if want_bf16 else []) + [pl.BlockSpec((SUBLANES, d), lambda i: (0, 0))]
    out_shape = ([jax.ShapeDtypeStruct((s, d), F32)]
                 + ([jax.ShapeDtypeStruct((s, d), BF16)] if want_bf16 else [])
                 + [jax.ShapeDtypeStruct((SUBLANES, d), F32)])
    return _call(body, name=name, grid=(s // tm,),
                 in_specs=[row, row, row, pl.BlockSpec((1, d), lambda i: (0, 0))],
                 out_specs=out_specs, out_shape=out_shape,
                 compiler_params=_params(("arbitrary",)))(dyn, xin, add, gain)


def head_fwd_bwd(x1, gl, p, wp, tgt, bias, ple_gain):
    s, d = x1.shape
    tm = _tile(s, 256)

    def body(x1_ref, gl_ref, p_ref, wp_ref, t_ref, b_ref, g_ref, dgl_ref, dpe_ref, dy_ref, acc_ref):
        i = pl.program_id(0)
        gate = _sigmoid(gl_ref[...] + b_ref[...])
        pb = p_ref[...].astype(BF16)
        pev = jnp.concatenate([jnp.dot(pb, wp_ref[q], preferred_element_type=F32) for q in range(N_CHIPS)],
                              axis=1)
        r = lax.rsqrt(jnp.mean(pev * pev, axis=-1, keepdims=True) + EPS)
        pehat = pev * r
        gain = g_ref[...]
        e = pehat * gain
        diff = (x1_ref[...] + gate * e) - t_ref[...]
        dy = diff * (1.0 / d)
        dy_ref[...] = dy
        d_gl = (dy * e) * (gate * (1.0 - gate))
        dgl_ref[...] = d_gl.astype(BF16)
        d_e = dy * gate
        gd = d_e * gain
        d_pe = r * (gd - pehat * jnp.mean(pehat * gd, axis=-1, keepdims=True))
        dpe_ref[...] = d_pe.astype(BF16)

        @pl.when(i == 0)
        def _():
            acc_ref[...] = jnp.zeros_like(acc_ref)

        acc_ref[0:1, :] += jnp.sum(d_gl, axis=0, keepdims=True)
        acc_ref[1:2, :] += jnp.sum(d_e * pehat, axis=0, keepdims=True)
        acc_ref[2:3, :] += jnp.sum(diff * diff, axis=0, keepdims=True) * (0.5 / d)

    row = pl.BlockSpec((tm, d), lambda i: (i, 0))
    vec = pl.BlockSpec((1, d), lambda i: (0, 0))
    return _call(body, name="head_fwd_bwd", grid=(s // tm,),
                 in_specs=[row, row, pl.BlockSpec((tm, p.shape[1]), lambda i: (i, 0)),
                           pl.BlockSpec(wp.shape, lambda i: (0, 0, 0)), row, vec, vec],
                 out_specs=[row, row, row, pl.BlockSpec((SUBLANES, d), lambda i: (0, 0))],
                 out_shape=[jax.ShapeDtypeStruct((s, d), BF16), jax.ShapeDtypeStruct((s, d), BF16),
                            jax.ShapeDtypeStruct((s, d), F32), jax.ShapeDtypeStruct((SUBLANES, d), F32)],
                 compiler_params=_params(("arbitrary",)))(x1, gl, p, wp, tgt, bias, ple_gain)


def adamw(g, w, m, v, name, after=()):
    r, c = g.shape
    tr = _tile(r, 256)

    def body(g_ref, w_ref, m_ref, v_ref, go_ref, d_ref, mo_ref, vo_ref):
        gv = g_ref[...]
        mn = ADAM_B1 * m_ref[...] + (1.0 - ADAM_B1) * gv
        vn = ADAM_B2 * v_ref[...] + (1.0 - ADAM_B2) * (gv * gv)
        m_hat = mn / (1.0 - ADAM_B1 ** ADAM_STEP)
        v_hat = vn / (1.0 - ADAM_B2 ** ADAM_STEP)
        go_ref[...] = gv
        d_ref[...] = -ADAM_LR * (m_hat / (jnp.sqrt(v_hat) + ADAM_EPS) + ADAM_WD * w_ref[...])
        mo_ref[...] = mn
        vo_ref[...] = vn

    blk = pl.BlockSpec((tr, c), lambda i: (i, 0))
    shp = jax.ShapeDtypeStruct((r, c), F32)
    return _call_after(body, after, name=name, grid=(r // tr,), in_specs=[blk] * 4, out_specs=[blk] * 4,
                       out_shape=[shp] * 4, compiler_params=_params(("parallel",)))(g, w, m, v)


def matmul(a, b, *, grid, a_spec, b_spec, o_spec, out_shape, dims, name, res=None, res_spec=None, after=()):
    nk = grid[2]
    acc_shape = tuple(d for d in o_spec.block_shape if d is not None)

    def body(*refs):
        a_ref, b_ref = refs[:2]
        r_ref = refs[2] if res is not None else None
        o_ref = refs[3] if res is not None else refs[2]
        part = lax.dot_general(a_ref[...].astype(BF16), b_ref[...].astype(BF16), dims, preferred_element_type=F32)

        def finish(out):
            if res is not None:
                out = r_ref[...] + out
            o_ref[...] = out.astype(o_ref.dtype)

        if nk == 1:
            finish(part)
            return
        acc_ref = refs[-1]
        k = pl.program_id(2)

        @pl.when(k == 0)
        def _():
            acc_ref[...] = part

        @pl.when((k > 0) & (k < nk - 1))
        def _():
            acc_ref[...] += part

        @pl.when(k == nk - 1)
        def _():
            finish(acc_ref[...] + part)

    in_specs = [a_spec, b_spec] + ([res_spec] if res is not None else [])
    args = (a, b) + ((res,) if res is not None else ())
    return _call_after(body, after, name=name, grid=grid, in_specs=in_specs, out_specs=o_spec,
                       out_shape=out_shape, scratch_shapes=[pltpu.VMEM(acc_shape, F32)] if nk > 1 else [],
                       compiler_params=_params(("parallel", "parallel", "arbitrary")))(*args)


def in_proj_part(h, w, z_prev, order, q, in_w):
    s, d = h.shape
    sh = w.shape[1]
    tm = _tile(s, 1024)

    def body(order_ref, h_ref, w_ref, *rest):
        rest[-1][...] = jnp.dot(h_ref[...], w_ref[...], preferred_element_type=F32)

    in_specs = [pl.BlockSpec((tm, d), lambda i, order_ref: (i, 0)),
                pl.BlockSpec((d, sh), lambda i, order_ref: (0, 0))]
    args = [order, h, w]
    if z_prev is not None:
        in_specs.append(_hbm())
        args.append(z_prev)
    grid_spec = pltpu.PrefetchScalarGridSpec(
        num_scalar_prefetch=1, grid=(s // tm,), in_specs=in_specs,
        out_specs=pl.BlockSpec((tm, sh), lambda i, order_ref: (i, order_ref[q])))
    return _call(body, name="mm_z%d" % q, grid_spec=grid_spec,
                 out_shape=jax.ShapeDtypeStruct((s, in_w), F32),
                 input_output_aliases={3: 0} if z_prev is not None else {},
                 compiler_params=_params(("parallel",)))(*args)


def in_proj_bwd(dz, ws, order, d, after):
    s = dz.shape[0]
    sh = ws[0].shape[1]
    tm, tn = _tile(s, 512), _tile(d, 512)
    nq, n_after = len(ws), len(after)

    def body(order_ref, *refs):
        a_refs, b_refs, o_ref = refs[:nq], refs[nq:2 * nq], refs[2 * nq + n_after]
        acc = lax.dot_general(a_refs[0][...], b_refs[0][...], NT, preferred_element_type=F32)
        for q in range(1, nq):
            acc += lax.dot_general(a_refs[q][...], b_refs[q][...], NT, preferred_element_type=F32)
        o_ref[...] = acc

    a_spec = lambda q: pl.BlockSpec((tm, sh), functools.partial(lambda i, j, order_ref, q: (i, order_ref[q]), q=q))
    grid_spec = pltpu.PrefetchScalarGridSpec(
        num_scalar_prefetch=1, grid=(s // tm, d // tn),
        in_specs=[a_spec(q) for q in range(nq)]
        + [pl.BlockSpec((tn, sh), lambda i, j, order_ref: (j, 0))] * nq + [_hbm()] * n_after,
        out_specs=pl.BlockSpec((tm, tn), lambda i, j, order_ref: (i, j)))
    return _call(body, name="mm_d_h", grid_spec=grid_spec, out_shape=jax.ShapeDtypeStruct((s, d), F32),
                 compiler_params=_params(("parallel", "parallel")))(order, *([dz] * nq), *ws, *after)


def _seg_mean(sq, segb):
    parts = []
    for cgrp in range(sq.shape[1] // SEG):
        blk = sq[:, cgrp * SEG:(cgrp + 1) * SEG]
        hi = blk.astype(BF16)
        r1 = blk - hi.astype(F32)
        mid = r1.astype(BF16)
        lo = (r1 - mid.astype(F32)).astype(BF16)
        acc = jnp.dot(hi, segb, preferred_element_type=F32)
        acc += jnp.dot(mid, segb, preferred_element_type=F32)
        acc += jnp.dot(lo, segb, preferred_element_type=F32)
        parts.append(acc)
    out = parts[0] if len(parts) == 1 else jnp.concatenate(parts, axis=1)
    return out * (1.0 / HEAD_DIM)


def _rope(v, cos, sa, sb):
    parts = []
    for cgrp in range(v.shape[1] // LANES):
        blk = v[:, cgrp * LANES:(cgrp + 1) * LANES]
        parts.append(blk * cos + pltpu.roll(blk, LANES - 8, 1) * sa + pltpu.roll(blk, 8, 1) * sb)
    return parts[0] if len(parts) == 1 else jnp.concatenate(parts, axis=1)


def _rope_t(dv, cos, sa, sb):
    parts = []
    for cgrp in range(dv.shape[1] // LANES):
        blk = dv[:, cgrp * LANES:(cgrp + 1) * LANES]
        parts.append(blk * cos + pltpu.roll(blk * sa, 8, 1) + pltpu.roll(blk * sb, LANES - 8, 1))
    return parts[0] if len(parts) == 1 else jnp.concatenate(parts, axis=1)


def _tile_lanes(vec, width):
    reps = width // LANES
    return vec if reps == 1 else jnp.tile(vec, (1, reps))


def qk_prep_fwd(z, tabs, qg, kg, segb, aw, after=()):
    s = z.shape[0]
    tm = _tile(s, 512)
    wq = aw + 2 * KV_WIDTH

    def body(z_ref, cos_ref, sa_ref, sb_ref, qg_ref, kg_ref, seg_ref, qs_ref, ks_ref, vb_ref):
        zz = z_ref[...]
        q, k, v = zz[:, :aw], zz[:, aw:aw + KV_WIDTH], zz[:, aw + KV_WIDTH:]
        cos, sa, sb, segm = cos_ref[...], sa_ref[...], sb_ref[...], seg_ref[...]
        rq = lax.rsqrt(_seg_mean(q * q, segm) + EPS)
        qn = (q * rq) * _tile_lanes(qg_ref[...], aw)
        qs_ref[...] = (_rope(qn, cos, sa, sb) * (HEAD_DIM ** -0.5)).astype(BF16)
        rk = lax.rsqrt(_seg_mean(k * k, segm) + EPS)
        kn = (k * rk) * _tile_lanes(kg_ref[...], KV_WIDTH)
        ks_ref[...] = _rope(kn, cos, sa, sb).astype(BF16)
        vb_ref[...] = v.astype(BF16)

    tab = pl.BlockSpec((tm, LANES), lambda i: (i, 0))
    vec = pl.BlockSpec((1, LANES), lambda i: (0, 0))
    return _call_after(body, after, name="qk_prep_fwd", grid=(s // tm,),
                       in_specs=[pl.BlockSpec((tm, wq), lambda i: (i, 0)), tab, tab, tab, vec, vec,
                                 pl.BlockSpec((SEG, SEG), lambda i: (0, 0))],
                       out_specs=[pl.BlockSpec((tm, aw), lambda i: (i, 0)),
                                  pl.BlockSpec((tm, KV_WIDTH), lambda i: (i, 0)),
                                  pl.BlockSpec((tm, KV_WIDTH), lambda i: (i, 0))],
                       out_shape=[jax.ShapeDtypeStruct((s, aw), BF16), jax.ShapeDtypeStruct((s, KV_WIDTH), BF16),
                                  jax.ShapeDtypeStruct((s, KV_WIDTH), BF16)],
                       compiler_params=_params(("parallel",)))(z, *tabs, qg, kg, segb)


def qk_prep_bwd(z, dqs, dks, dvs, tabs, qg, kg, segb, dz, aw):
    s = z.shape[0]
    tm = _tile(s, 512)
    wq = aw + 2 * KV_WIDTH

    def body(z_ref, dq_ref, dk_ref, dv_ref, cos_ref, sa_ref, sb_ref, qg_ref, kg_ref, seg_ref, dz_in,
             dz_ref, acc_ref):
        i = pl.program_id(0)
        zz = z_ref[...]
        q, k = zz[:, :aw], zz[:, aw:aw + KV_WIDTH]
        cos, sa, sb, segm = cos_ref[...], sa_ref[...], sb_ref[...], seg_ref[...]

        def one(xv, dout, gvec, width):
            g = _tile_lanes(gvec, width)
            r = lax.rsqrt(_seg_mean(xv * xv, segm) + EPS)
            xhat = xv * r
            dn = _rope_t(dout, cos, sa, sb)
            gd = dn * g
            dx = r * (gd - xhat * _seg_mean(xhat * gd, segm))
            contrib = jnp.sum(dn * xhat, axis=0, keepdims=True)
            folded = contrib[:, :LANES]
            for cgrp in range(1, width // LANES):
                folded = folded + contrib[:, cgrp * LANES:(cgrp + 1) * LANES]
            return dx, folded + pltpu.roll(folded, HEAD_DIM, 1)

        dq, gq = one(q, dq_ref[...] * (HEAD_DIM ** -0.5), qg_ref[...], aw)
        dk, gk = one(k, dk_ref[...], kg_ref[...], KV_WIDTH)
        dz_ref[:, :aw] = dq.astype(BF16)
        dz_ref[:, aw:aw + KV_WIDTH] = dk.astype(BF16)
        dz_ref[:, aw + KV_WIDTH:] = dv_ref[...].astype(BF16)

        @pl.when(i == 0)
        def _():
            acc_ref[...] = jnp.zeros_like(acc_ref)

        acc_ref[0:1, :] += gq
        acc_ref[1:2, :] += gk

    tab = pl.BlockSpec((tm, LANES), lambda i: (i, 0))
    vec = pl.BlockSpec((1, LANES), lambda i: (0, 0))
    kvb = pl.BlockSpec((tm, KV_WIDTH), lambda i: (i, 0))
    return _call(body, name="qk_prep_bwd", grid=(s // tm,),
                 in_specs=[pl.BlockSpec((tm, wq), lambda i: (i, 0)),
                           pl.BlockSpec((tm, aw), lambda i: (i, 0)), kvb, kvb, tab, tab, tab, vec, vec,
                           pl.BlockSpec((SEG, SEG), lambda i: (0, 0)), _hbm()],
                 out_specs=[pl.BlockSpec((tm, wq), lambda i: (i, 0)),
                            pl.BlockSpec((SUBLANES, LANES), lambda i: (0, 0))],
                 out_shape=[jax.ShapeDtypeStruct(dz.shape, BF16), jax.ShapeDtypeStruct((SUBLANES, LANES), F32)],
                 input_output_aliases={10: 0},
                 compiler_params=_params(("arbitrary",)))(z, dqs, dks, dvs, *tabs, qg, kg, segb, dz)


def _placed(band, lane_idx):
    out = {}
    for kh in range(N_KV_HEADS):
        grp = band[:, (kh // 2) * LANES:(kh // 2 + 1) * LANES]
        for half in (0, 1):
            t = grp if half == kh % 2 else pltpu.roll(grp, HEAD_DIM, 1)
            keep = (lane_idx >= half * HEAD_DIM) & (lane_idx < (half + 1) * HEAD_DIM)
            out[kh, half] = jnp.where(keep, t, jnp.zeros_like(t))
    return out


def _softmax_with_sink(sc, valid, sink):
    sc = jnp.where(valid, sc, NEG_INF)
    m = jnp.maximum(jnp.max(sc, axis=-1, keepdims=True), sink)
    e = jnp.exp(sc - m)
    es = jnp.exp(sink - m)
    den = jnp.sum(e, axis=-1, keepdims=True) + es
    return e / den, es / den


def _stack_halves(placed, kh):
    return jnp.concatenate([placed[kh, 0], placed[kh, 1]], axis=0)


def _valid_mask(n):
    r_i = lax.broadcasted_iota(jnp.int32, (BLOCK, 2 * BLOCK), 0)
    j_i = lax.broadcasted_iota(jnp.int32, (BLOCK, 2 * BLOCK), 1)
    return (j_i > r_i) & (j_i <= r_i + BLOCK) & ((n > 0) | (j_i >= BLOCK))


def attn_fwd(qs, ks, vb, z, sinks, aw, d, ga_off):
    s = qs.shape[0]
    nb = s // BLOCK
    nq = aw // HEAD_DIM
    grp_sz = nq // N_KV_HEADS
    n_ga = aw // COLT

    def body(q_ref, ko_ref, kp_ref, vo_ref, vp_ref, *rest):
        ga_refs = rest[:n_ga]
        sink_ref, attn_ref, mix_ref = rest[n_ga:]
        n = pl.program_id(0)
        lane_idx = lax.broadcasted_iota(jnp.int32, (2 * BLOCK, LANES), 1)
        kpl = _placed(jnp.concatenate([kp_ref[...], ko_ref[...]], axis=0), lane_idx)
        vpl = _placed(jnp.concatenate([vp_ref[...], vo_ref[...]], axis=0), lane_idx)
        valid = _valid_mask(n)
        groups = range(nq // 2)
        kcat = [_stack_halves(kpl, kh) for kh in range(N_KV_HEADS)]
        vcat = [_stack_halves(vpl, kh) for kh in range(N_KV_HEADS)]
        scs = [lax.dot_general(q_ref[:, c * LANES:(c + 1) * LANES], kcat[2 * c // grp_sz], NT,
                               preferred_element_type=F32) for c in groups]
        probs = [jnp.concatenate(
            [_softmax_with_sink(scs[c][:, half * 2 * BLOCK:(half + 1) * 2 * BLOCK], valid,
                                sink_ref[0, 2 * c + half])[0].astype(BF16) for half in (0, 1)], axis=1)
                 for c in groups]
        for c in groups:
            acc = jnp.dot(probs[c], vcat[2 * c // grp_sz], preferred_element_type=F32)
            attn_ref[:, c * LANES:(c + 1) * LANES] = acc
            col = c * LANES
            ga = ga_refs[col // COLT][:, col % COLT:col % COLT + LANES]
            mix_ref[:, c * LANES:(c + 1) * LANES] = (acc * (ga * _sigmoid(ga))).astype(BF16)

    own = lambda n: (n, 0)
    prev = lambda n: (jnp.maximum(n - 1, 0), 0)
    kvs = lambda imap: pl.BlockSpec((BLOCK, KV_WIDTH), imap)
    ga_specs = [pl.BlockSpec((BLOCK, COLT), functools.partial(lambda n, j: (n, ga_off + j), j=j))
                for j in range(n_ga)]
    return _call(body, name="attn_fwd", grid=(nb,),
                 in_specs=[pl.BlockSpec((BLOCK, aw), own), kvs(own), kvs(prev), kvs(own), kvs(prev)]
                 + ga_specs + [pl.BlockSpec(memory_space=pltpu.SMEM)],
                 out_specs=[pl.BlockSpec((BLOCK, aw), own), pl.BlockSpec((BLOCK, aw), own)],
                 out_shape=[jax.ShapeDtypeStruct((s, aw), F32), jax.ShapeDtypeStruct((s, d), BF16)],
                 compiler_params=_params(("parallel",)))(qs, ks, ks, vb, vb, *([z] * n_ga), sinks)


def gate_bwd(d_mix, attn, z, aw, ga_off, after=()):
    s, in_w = z.shape
    tm = _tile(s, 1024)

    def body(dm_ref, at_ref, ga_ref, do_ref, dz_ref):
        ga = ga_ref[...]
        sig = _sigmoid(ga)
        dm = dm_ref[...]
        do_ref[...] = (dm * (ga * sig)).astype(BF16)
        dz_ref[...] = ((dm * at_ref[...]) * (sig * (1.0 + ga * (1.0 - sig)))).astype(BF16)

    blk = pl.BlockSpec((tm, COLT), lambda i, j: (i, j))
    gab = pl.BlockSpec((tm, COLT), lambda i, j: (i, ga_off + j))
    return _call_after(body, after, name="gate_bwd", grid=(s // tm, aw // COLT),
                       in_specs=[blk, blk, gab], out_specs=[blk, gab],
                       out_shape=[jax.ShapeDtypeStruct((s, aw), BF16), jax.ShapeDtypeStruct((s, in_w), BF16)],
                       compiler_params=_params(("parallel", "parallel")))(d_mix, attn, z)


def attn_bwd(qs, ks, vb, d_o, sinks, aw):
    s = qs.shape[0]
    nb = s // BLOCK
    nq = aw // HEAD_DIM
    grp_sz = nq // N_KV_HEADS

    def body(q_ref, ko_ref, kp_ref, vo_ref, vp_ref, do_ref, sink_ref, dq_ref, dk_ref, dv_ref, ds_ref,
             ck_ref, cv_ref):
        n = pl.program_id(0)

        @pl.when(n == 0)
        def _():
            ds_ref[...] = jnp.zeros_like(ds_ref)
            dk_ref[...] = jnp.zeros_like(dk_ref)
            dv_ref[...] = jnp.zeros_like(dv_ref)

        @pl.when(n < nb)
        def _():
            lane_idx = lax.broadcasted_iota(jnp.int32, (2 * BLOCK, LANES), 1)
            lane_row = lax.broadcasted_iota(jnp.int32, (1, LANES), 1)
            kpl = _placed(jnp.concatenate([kp_ref[...], ko_ref[...]], axis=0), lane_idx)
            vpl = _placed(jnp.concatenate([vp_ref[...], vo_ref[...]], axis=0), lane_idx)
            valid = _valid_mask(n)
            ds_row = jnp.zeros((1, LANES), F32)
            wide = 2 * BLOCK
            groups = range(nq // 2)
            per_kv = grp_sz // 2
            kcat = [_stack_halves(kpl, kh) for kh in range(N_KV_HEADS)]
            vcat = [_stack_halves(vpl, kh) for kh in range(N_KV_HEADS)]
            qg = [q_ref[:, c * LANES:(c + 1) * LANES] for c in groups]
            dog = [do_ref[:, c * LANES:(c + 1) * LANES] for c in groups]
            scs = [lax.dot_general(qg[c], kcat[c // per_kv], NT, preferred_element_type=F32) for c in groups]
            dps = [lax.dot_general(dog[c], vcat[c // per_kv], NT, preferred_element_type=F32) for c in groups]
            ds_pairs, p_pairs = [], []
            for c in groups:
                probs, dss = [], []
                for half in (0, 1):
                    h = 2 * c + half
                    cols = slice(half * wide, (half + 1) * wide)
                    p, p_sink = _softmax_with_sink(scs[c][:, cols], valid, sink_ref[0, h])
                    delta = jnp.sum(p * dps[c][:, cols], axis=-1, keepdims=True)
                    dss.append((p * (dps[c][:, cols] - delta)).astype(BF16))
                    probs.append(p.astype(BF16))
                    dsink = -jnp.sum(p_sink * delta, axis=0, keepdims=True)
                    ds_row += jnp.where(lane_row == h, dsink, 0.0)
                ds_pairs.append(jnp.concatenate(dss, axis=1))
                p_pairs.append(jnp.concatenate(probs, axis=1))
            for c in groups:
                dq_ref[:, c * LANES:(c + 1) * LANES] = jnp.dot(ds_pairs[c], kcat[c // per_kv],
                                                               preferred_element_type=F32)
            dkts = [lax.dot_general(qg[c], ds_pairs[c], TN, preferred_element_type=F32) for c in groups]
            dvts = [lax.dot_general(dog[c], p_pairs[c], TN, preferred_element_type=F32) for c in groups]
            dkt, dvt = [], []
            for kh in range(N_KV_HEADS):
                for parts, out in ((dkts, dkt), (dvts, dvt)):
                    tot = parts[kh * per_kv]
                    for c in range(kh * per_kv + 1, (kh + 1) * per_kv):
                        tot = tot + parts[c]
                    out.append(tot[:HEAD_DIM, :wide] + tot[HEAD_DIM:, wide:])
            ds_ref[0:1, :] += ds_row
            back = lambda t: jnp.concatenate(
                [jnp.concatenate(t[2 * g:2 * g + 2], axis=0).T for g in range(N_KV_HEADS // 2)], axis=1)
            dk_band, dv_band = back(dkt), back(dvt)

            @pl.when(n > 0)
            def _():
                dk_ref[...] = ck_ref[...] + dk_band[:BLOCK]
                dv_ref[...] = cv_ref[...] + dv_band[:BLOCK]

            ck_ref[...] = dk_band[BLOCK:]
            cv_ref[...] = dv_band[BLOCK:]

        @pl.when(n == nb)
        def _():
            dk_ref[...] = ck_ref[...]
            dv_ref[...] = cv_ref[...]

    own = lambda n: (jnp.minimum(n, nb - 1), 0)
    prev = lambda n: (jnp.clip(n - 1, 0, nb - 1), 0)
    done = lambda n: (jnp.maximum(n - 1, 0), 0)
    kvs = lambda imap: pl.BlockSpec((BLOCK, KV_WIDTH), imap)
    return _call(body, name="attn_bwd", grid=(nb + 1,),
                 in_specs=[pl.BlockSpec((BLOCK, aw), own), kvs(own), kvs(prev), kvs(own), kvs(prev),
                           pl.BlockSpec((BLOCK, aw), own), pl.BlockSpec(memory_space=pltpu.SMEM)],
                 out_specs=[pl.BlockSpec((BLOCK, aw), own), kvs(done), kvs(done),
                            pl.BlockSpec((SUBLANES, LANES), lambda n: (0, 0))],
                 out_shape=[jax.ShapeDtypeStruct((s, aw), F32), jax.ShapeDtypeStruct((s, KV_WIDTH), F32),
                            jax.ShapeDtypeStruct((s, KV_WIDTH), F32), jax.ShapeDtypeStruct((SUBLANES, LANES), F32)],
                 scratch_shapes=[pltpu.VMEM((BLOCK, KV_WIDTH), F32), pltpu.VMEM((BLOCK, KV_WIDTH), F32)],
                 compiler_params=_params(("arbitrary",)))(qs, ks, ks, vb, vb, d_o, sinks)


def conv_fwd(z, conv_w, mix, aw, cw, b_off):
    s = z.shape[0]
    tm = _tile(s, 1024)
    nseg = cw // COLT
    hb = tm // SUBLANES

    def body(b_ref, c_ref, h_ref, g_ref, cp_ref, hp_ref, w_ref, mix_in, mix_ref):
        i = pl.program_id(0)
        u = c_ref[...] * h_ref[...]
        up = jnp.where(i > 0, cp_ref[...] * hp_ref[...], 0.0)
        ext = jnp.concatenate([up, u], axis=0)
        um1 = pltpu.roll(ext, 1, 0)[SUBLANES:]
        um2 = pltpu.roll(ext, 2, 0)[SUBLANES:]
        w = w_ref[...]
        cv = w[0:1] * um2 + w[1:2] * um1 + w[2:3] * u
        g = g_ref[...]
        mix_ref[...] = ((b_ref[...] * cv) * (g * _sigmoid(g))).astype(BF16)

    seg = lambda k: pl.BlockSpec((tm, COLT), functools.partial(lambda i, j, k: (i, b_off + k * nseg + j), k=k))
    halo = lambda k: pl.BlockSpec(
        (SUBLANES, COLT), functools.partial(lambda i, j, k: (jnp.maximum(i * hb - 1, 0), b_off + k * nseg + j), k=k))
    return _call(body, name="conv_fwd", grid=(s // tm, nseg),
                 in_specs=[seg(0), seg(1), seg(2), seg(3), halo(1), halo(2),
                           pl.BlockSpec((SUBLANES, COLT), lambda i, j: (0, j)), _hbm()],
                 out_specs=pl.BlockSpec((tm, COLT), lambda i, j: (i, aw // COLT + j)),
                 out_shape=jax.ShapeDtypeStruct(mix.shape, BF16),
                 input_output_aliases={7: 0},
                 compiler_params=_params(("parallel", "parallel")))(z, z, z, z, z, z, conv_w, mix)


def conv_bwd(z, d_mix, conv_w, dz, aw, cw, b_off):
    s = z.shape[0]
    tm = _tile(s, 512)
    nseg = cw // COLT
    hb = tm // SUBLANES
    n_row = s // tm
    last_h = s // SUBLANES - 1
    n_step = nseg * n_row

    def body(b_ref, c_ref, h_ref, g_ref, cp_ref, hp_ref, bn_ref, gn_ref, dm_ref, dmn_ref, w_ref, dz_in,
             dz_ref, acc_ref, stash_ref, sems):
        j = pl.program_id(0)
        i = pl.program_id(1)
        step = j * n_row + i
        slot = step % 2

        def copies(from_slot, at_step):
            jj, ii = at_step // n_row, at_step % n_row
            rows = pl.ds(pl.multiple_of(ii * tm, tm), tm)
            return [pltpu.make_async_copy(
                stash_ref.at[from_slot, q],
                dz_ref.at[rows, pl.ds(pl.multiple_of((b_off + q * nseg + jj) * COLT, COLT), COLT)],
                sems.at[from_slot, q]) for q in range(4)]

        @pl.when(step >= 2)
        def _():
            for cp in copies(slot, step - 2):
                cp.wait()

        cc, hh, bb, g = c_ref[...], h_ref[...], b_ref[...], g_ref[...]
        w = w_ref[...]
        u = cc * hh
        up = jnp.where(i > 0, cp_ref[...] * hp_ref[...], 0.0)
        ext = jnp.concatenate([up, u], axis=0)
        um1 = pltpu.roll(ext, 1, 0)[SUBLANES:]
        um2 = pltpu.roll(ext, 2, 0)[SUBLANES:]
        cv = w[0:1] * um2 + w[1:2] * um1 + w[2:3] * u
        sig = _sigmoid(g)
        sg = g * sig
        dm = dm_ref[...]
        d_cv = (dm * bb) * sg
        gn = gn_ref[...]
        d_cv_next = jnp.where(i < n_row - 1, (dmn_ref[...] * bn_ref[...]) * (gn * _sigmoid(gn)), 0.0)
        ext2 = jnp.concatenate([d_cv, d_cv_next], axis=0)
        dp1 = pltpu.roll(ext2, tm + SUBLANES - 1, 0)[:tm]
        dp2 = pltpu.roll(ext2, tm + SUBLANES - 2, 0)[:tm]
        d_u = w[2:3] * d_cv + w[1:2] * dp1 + w[0:1] * dp2
        stash_ref[slot, 0] = ((dm * cv) * sg).astype(BF16)
        stash_ref[slot, 1] = (d_u * hh).astype(BF16)
        stash_ref[slot, 2] = (d_u * cc).astype(BF16)
        stash_ref[slot, 3] = (((dm * bb) * cv) * (sig * (1.0 + g * (1.0 - sig)))).astype(BF16)
        for cp in copies(slot, step):
            cp.start()

        @pl.when(i == 0)
        def _():
            acc_ref[...] = jnp.zeros_like(acc_ref)

        acc_ref[0:1, :] += jnp.sum(d_cv * um2, axis=0, keepdims=True)
        acc_ref[1:2, :] += jnp.sum(d_cv * um1, axis=0, keepdims=True)
        acc_ref[2:3, :] += jnp.sum(d_cv * u, axis=0, keepdims=True)

        @pl.when(step == n_step - 1)
        def _():
            if n_step >= 2:
                for cp in copies(1 - slot, step - 1):
                    cp.wait()
            for cp in copies(slot, step):
                cp.wait()

    seg = lambda q: pl.BlockSpec((tm, COLT), functools.partial(lambda j, i, q: (i, b_off + q * nseg + j), q=q))
    halo_p = lambda q: pl.BlockSpec(
        (SUBLANES, COLT),
        functools.partial(lambda j, i, q: (jnp.maximum(i * hb - 1, 0), b_off + q * nseg + j), q=q))
    halo_n = lambda q: pl.BlockSpec(
        (SUBLANES, COLT),
        functools.partial(lambda j, i, q: (jnp.minimum((i + 1) * hb, last_h), b_off + q * nseg + j), q=q))
    return _call(body, name="conv_bwd", grid=(nseg, n_row),
                 in_specs=[seg(0), seg(1), seg(2), seg(3), halo_p(1), halo_p(2), halo_n(0), halo_n(3),
                           pl.BlockSpec((tm, COLT), lambda j, i: (i, aw // COLT + j)),
                           pl.BlockSpec((SUBLANES, COLT),
                                        lambda j, i: (jnp.minimum((i + 1) * hb, last_h), aw // COLT + j)),
                           pl.BlockSpec((SUBLANES, COLT), lambda j, i: (0, j)), _hbm()],
                 out_specs=[_hbm(), pl.BlockSpec((SUBLANES, COLT), lambda j, i: (0, j))],
                 out_shape=[jax.ShapeDtypeStruct(dz.shape, BF16), jax.ShapeDtypeStruct((SUBLANES, cw), F32)],
                 input_output_aliases={11: 0},
                 scratch_shapes=[pltpu.VMEM((2, 4, tm, COLT), BF16), pltpu.SemaphoreType.DMA((2, 4))],
                 compiler_params=_params(("arbitrary", "arbitrary")))(
                     z, z, z, z, z, z, z, z, d_mix, d_mix, conv_w, dz)


def _place():
    x, y, c = lax.axis_index("x"), lax.axis_index("y"), lax.axis_index("c")
    chips = [(1 - x, y), (x, 1 - y), (1 - x, 1 - y)]
    return x, y, c, chips


def _remote(src, dst, send_sem, recv_sem, device):
    return pltpu.make_async_remote_copy(src_ref=src, dst_ref=dst, send_sem=send_sem, recv_sem=recv_sem,
                                        device_id=device, device_id_type=MESH)


def plan_gather_ici(n_split):
    def plan(refs, send, recv):
        x, y, c, chips = _place()
        me = 2 * x + y
        mine, theirs = [], []
        for w, ref in enumerate(refs):
            for j, (cx, cy) in enumerate(chips):
                def part(slot):
                    if w >= n_split:
                        return ref.at[slot]
                    hr = ref.shape[1] // 2
                    return ref.at[slot, pl.ds(c * hr, hr)]
                k = 3 * w + j
                mine.append(_remote(part(me), part(me), send.at[k], recv.at[k], (cx, cy, c)))
                theirs.append(_remote(part(2 * cx + cy), part(2 * cx + cy), send.at[k], recv.at[k], (cx, cy, c)))
        return mine, theirs
    return plan


def plan_shard_ici(j):
    def plan(refs, send, recv):
        _, _, c, chips = _place()
        own, land = refs
        hr = own.shape[0] // 2
        rows = pl.ds(c * hr, hr)
        cp = _remote(own.at[rows], land.at[rows], send.at[0], recv.at[0], (*chips[j], c))
        return [cp], [cp]
    return plan


def plan_shard_pass(refs, send, recv):
    x, y, c, _ = _place()
    land, = refs
    hr = land.shape[0] // 2
    half = lambda core: land.at[pl.ds(core * hr, hr)]
    return ([_remote(half(c), half(c), send.at[0], recv.at[0], (x, y, 1 - c))],
            [_remote(half(1 - c), half(1 - c), send.at[0], recv.at[0], (x, y, 1 - c))])


def plan_gather_pass(refs, send, recv):
    x, y, c, chips = _place()
    mine, theirs = [], []
    for w, ref in enumerate(refs):
        hr = ref.shape[1] // 2
        for j, (cx, cy) in enumerate(chips):
            half = lambda core: ref.at[2 * cx + cy, pl.ds(core * hr, hr)]
            k = 3 * w + j
            mine.append(_remote(half(c), half(c), send.at[k], recv.at[k], (x, y, 1 - c)))
            theirs.append(_remote(half(1 - c), half(1 - c), send.at[k], recv.at[k], (x, y, 1 - c)))
    return mine, theirs


def plan_swap(refs, send, recv):
    x, y, c, _ = _place()
    nw = len(refs) // 2
    mine = []
    for w in range(nw):
        hr = refs[w].shape[1] // 2
        mine.append(_remote(refs[w].at[:, pl.ds((1 - c) * hr, hr), :], refs[nw + w], send.at[w], recv.at[w],
                            (x, y, 1 - c)))
    return mine, mine


def plan_scatter(refs, send, recv):
    x, y, c, chips = _place()
    me = 2 * x + y
    nw = len(refs) // 2
    mine, theirs = [], []
    for w in range(nw):
        for j, (cx, cy) in enumerate(chips):
            k = 3 * w + j
            mine.append(_remote(refs[w].at[2 * cx + cy], refs[nw + w].at[me], send.at[k], recv.at[k], (cx, cy, c)))
            theirs.append(_remote(refs[w].at[me], refs[nw + w].at[2 * cx + cy], send.at[k], recv.at[k], (cx, cy, c)))
    return mine, theirs


def plan_join(refs, send, recv):
    x, y, c, _ = _place()
    mine, theirs = [], []
    for w, ref in enumerate(refs):
        hr = ref.shape[0] // 2
        half = lambda core: ref.at[pl.ds(core * hr, hr)]
        mine.append(_remote(half(c), half(c), send.at[w], recv.at[w], (x, y, 1 - c)))
        theirs.append(_remote(half(1 - c), half(1 - c), send.at[w], recv.at[w], (x, y, 1 - c)))
    return mine, theirs


def exchange(name, plan, arrays, n):
    na = len(arrays)

    def body(*refs):
        mine, theirs = plan(refs[:na], refs[2 * na], refs[2 * na + 1])
        for cp in mine:
            cp.start()
        for cp in theirs:
            cp.wait_recv()
        for cp in mine:
            cp.wait_send()

    return _call(body, name=name, in_specs=[_hbm()] * na, out_specs=[_hbm()] * na,
                 out_shape=[jax.ShapeDtypeStruct(a.shape, a.dtype) for a in arrays],
                 input_output_aliases={i: i for i in range(na)},
                 scratch_shapes=[pltpu.SemaphoreType.DMA((n,)), pltpu.SemaphoreType.DMA((n,))])(*arrays)


def exchange_start(name, groups, arrays, after=()):
    na, ng = len(arrays), len(groups)

    def body(*refs):
        for g, (plan, idx, _) in enumerate(groups):
            mine, _ = plan([refs[i] for i in idx], refs[na + 2 * g], refs[na + 2 * g + 1])
            for cp in mine:
                cp.start()
        refs[-1][...] = jnp.zeros_like(refs[-1])

    hbm = pl.BlockSpec(memory_space=pltpu.HBM)
    sem = pl.BlockSpec(memory_space=pltpu.SEMAPHORE)
    sem_types = [pltpu.SemaphoreType.DMA((n,)) for _, _, n in groups for _ in (0, 1)]
    outs = _call_after(body, after, name=name, in_specs=[hbm] * na,
                       out_specs=[sem] * (2 * ng) + [hbm] * na + [pl.BlockSpec(memory_space=pltpu.VMEM)],
                       out_shape=sem_types + [pltpu.HBM(a.shape, a.dtype) for a in arrays]
                       + [jax.ShapeDtypeStruct((SUBLANES, LANES), F32)],
                       input_output_aliases={i: i + 2 * ng for i in range(na)},
                       compiler_params=pltpu.CompilerParams(
                           has_side_effects=pltpu.SideEffectType.DATAFLOW_SIDE_EFFECTING))(
                               *[pltpu.with_memory_space_constraint(a, pltpu.HBM) for a in arrays])
    sems = [(outs[2 * g], outs[2 * g + 1]) for g in range(ng)]
    return sems, list(outs[2 * ng:-1]), outs[-1]


def exchange_wait(name, plan, sems, arrays, after):
    send_sems, recv_sems = sems
    na = len(arrays)

    def body(*refs):
        mine, theirs = plan(refs[:na], refs[na], refs[na + 1])
        for cp in mine:
            cp.wait_send()
        for cp in theirs:
            cp.wait_recv()

    hbm = pl.BlockSpec(memory_space=pltpu.HBM)
    sem = pl.BlockSpec(memory_space=pltpu.SEMAPHORE)
    return _call(body, name=name, in_specs=[hbm] * na + [sem, sem, _hbm()], out_specs=[hbm] * na,
                 out_shape=[pltpu.HBM(a.shape, a.dtype) for a in arrays],
                 input_output_aliases={i: i for i in range(na)},
                 compiler_params=pltpu.CompilerParams(
                     has_side_effects=pltpu.SideEffectType.DATAFLOW_SIDE_EFFECTING))(
                         *arrays, send_sems, recv_sems, after)


def plan_allgather_small(refs, send, recv):
    x, y, c, _ = _place()
    buf, = refs
    mine, theirs = [], []
    flips = [(fx, fy, fc) for fx in (0, 1) for fy in (0, 1) for fc in (0, 1)][1:]
    for k, (fx, fy, fc) in enumerate(flips):
        peer = (x ^ fx, y ^ fy, c ^ fc)
        slot = lambda px, py, pc: buf.at[4 * px + 2 * py + pc]
        mine.append(_remote(slot(x, y, c), slot(x, y, c), send.at[k], recv.at[k], peer))
        theirs.append(_remote(slot(*peer), slot(*peer), send.at[k], recv.at[k], peer))
    return mine, theirs


def add_sibling(grad, got, core, name):
    _, r, c = grad.shape
    hr = r // 2
    tr = _tile(hr, 512)
    nblk = hr // tr

    def body(core_ref, g_ref, o_ref, out_ref):
        out_ref[...] = (g_ref[...].astype(F32) + o_ref[...].astype(F32)).astype(BF16)

    grid_spec = pltpu.PrefetchScalarGridSpec(
        num_scalar_prefetch=1, grid=(N_CHIPS, nblk),
        in_specs=[pl.BlockSpec((None, tr, c), lambda t, i, core_ref: (t, core_ref[0] * nblk + i, 0)),
                  pl.BlockSpec((None, tr, c), lambda t, i, core_ref: (t, i, 0))],
        out_specs=pl.BlockSpec((None, tr, c), lambda t, i, core_ref: (t, i, 0)))
    return _call(body, name=name, grid_spec=grid_spec,
                 out_shape=jax.ShapeDtypeStruct((N_CHIPS, hr, c), BF16),
                 compiler_params=_params(("parallel", "parallel")))(core, grad, got)


def sum_chips(mine, owned, place, name):
    _, hr, c = mine.shape
    tr = _tile(hr, 512)
    nblk = hr // tr

    def body(place_ref, m_ref, o1_ref, o2_ref, o3_ref, out_ref):
        acc = m_ref[...].astype(F32)
        for o_ref in (o1_ref, o2_ref, o3_ref):
            acc = acc + o_ref[...].astype(F32)
        out_ref[...] = acc

    other = lambda k: pl.BlockSpec((None, tr, c), lambda i, place_ref: ((place_ref[0] + k) % N_CHIPS, i, 0))
    grid_spec = pltpu.PrefetchScalarGridSpec(
        num_scalar_prefetch=1, grid=(nblk,),
        in_specs=[other(0), other(1), other(2), other(3)],
        out_specs=pl.BlockSpec((tr, c), lambda i, place_ref: (place_ref[1] * nblk + i, 0)))
    return _call(body, name=name, grid_spec=grid_spec,
                 out_shape=jax.ShapeDtypeStruct((2 * hr, c), F32),
                 compiler_params=_params(("parallel",)))(place, mine, owned, owned, owned)


def sum_devices(gathered):
    _, rows, width = gathered.shape

    def body(g_ref, out_ref):
        acc = g_ref[0]
        for dev in range(1, 8):
            acc = acc + g_ref[dev]
        out_ref[...] = acc
        tail = acc[SUBLANES:]
        out_ref[SUBLANES:, :] = jnp.broadcast_to(jnp.sum(tail, axis=1, keepdims=True), tail.shape)

    return _call(body, name="sum_devices",
                 in_specs=[pl.BlockSpec(memory_space=pltpu.VMEM)],
                 out_specs=pl.BlockSpec(memory_space=pltpu.VMEM),
                 out_shape=jax.ShapeDtypeStruct((rows, width), F32))(gathered)


def _rope_tables(s):
    half = ROT_DIM // 2
    inv_freq = jnp.power(jnp.float32(ROPE_THETA), -jnp.arange(half, dtype=F32) * 2.0 / ROT_DIM)
    freq64 = jnp.concatenate([inv_freq, inv_freq, jnp.zeros((HEAD_DIM - ROT_DIM,), F32)])
    freq = jnp.concatenate([freq64, freq64])[None, :]
    dim = (jnp.arange(LANES) % HEAD_DIM)[None, :]
    ang = jnp.arange(s).astype(F32)[:, None] * freq
    cos, sin = jnp.cos(ang), jnp.sin(ang)
    return cos, jnp.where(dim < half, -sin, 0.0), jnp.where((dim >= half) & (dim < ROT_DIM), sin, 0.0)


def _pad_rows(a, rows):
    return jnp.pad(a, ((0, rows - a.shape[0]), (0, 0)))


def _pad_cols(a, cols):
    return jnp.pad(a, ((0, 0), (0, cols - a.shape[1])))


def kernel(x, p, norm_gain, w_in, q_norm_gain, k_norm_gain, attn_sinks, conv_w, w_out, ple_gate_norm_gain, w_ple_gate, b_ple_gate, w_ple_proj, ple_norm_gain, loss_target, m_norm_gain, m_w_in, m_q_norm_gain, m_k_norm_gain, m_attn_sinks, m_conv_w, m_w_out, m_ple_gate_norm_gain, m_w_ple_gate, m_b_ple_gate, m_w_ple_proj, m_ple_norm_gain, v_norm_gain, v_w_in, v_q_norm_gain, v_k_norm_gain, v_attn_sinks, v_conv_w, v_w_out, v_ple_gate_norm_gain, v_w_ple_gate, v_b_ple_gate, v_w_ple_proj, v_ple_norm_gain):
    x2, p2, tgt = x[0], p[0, 0], loss_target[0]
    s, d = x2.shape
    ple = p2.shape[1]
    aw = d // 2
    cw = d - aw
    nq = aw // HEAD_DIM
    sh = w_in.shape[2]
    in_w = N_CHIPS * sh
    dq = d // N_CHIPS
    cq = cw // N_CHIPS
    ga_off = (aw + 2 * KV_WIDTH) // COLT
    b_off = (2 * aw + 2 * KV_WIDTH) // COLT
    assert in_w == 2 * aw + 2 * KV_WIDTH + 4 * cw and aw % COLT == 0 and cw % COLT == 0
    assert s % BLOCK == 0 and sh % LANES == 0 and nq % (2 * N_KV_HEADS) == 0

    core = lax.axis_index("c").astype(jnp.int32).reshape(1)
    chip = 2 * lax.axis_index("x") + lax.axis_index("y")

    chip1 = chip.astype(jnp.int32).reshape(1)
    place = jnp.concatenate([chip1, core])
    w_in_own = cast_bf16(w_in[0], "cast_w_in")
    rest = [cast_into_slot(w_out[0], chip1, "cast_w_out"), cast_into_slot(w_ple_gate[0], chip1, "cast_w_pg"),
            cast_into_slot(w_ple_proj[0], chip1, "cast_w_pp"),
            lax.dynamic_update_slice(jnp.zeros((N_CHIPS, SUBLANES, cq), F32),
                                     _pad_rows(conv_w[0], SUBLANES)[None], (chip, 0, 0))]
    order = jnp.stack([chip, chip ^ 2, chip ^ 1, chip ^ 3]).astype(jnp.int32)
    wi_sems, wi_arrs, wi_token = exchange_start(
        "gather_wi_start", [(plan_shard_ici(j), [0, 1 + j], 1) for j in range(3)],
        [w_in_own] + [lax.empty((d, sh), BF16) for _ in range(3)])
    rest_sems, rest, rest_token = exchange_start(
        "gather_rest_start", [(plan_gather_ici(3), [0, 1, 2, 3], 12)], rest, after=(wi_token,))

    tm = _tile(s, 1024)
    tn = _tile(d, 1024)
    tk = _tile(d, 2048)
    ts = _tile(s, 2048)
    h = rms_fwd(x2, norm_gain, "rms_fwd_x", after=(rest_token,))
    tabs = _rope_tables(s)
    qg = jnp.tile(q_norm_gain, (1, LANES // HEAD_DIM))
    kg = jnp.tile(k_norm_gain, (1, LANES // HEAD_DIM))
    seg_i = jnp.arange(SEG) // HEAD_DIM
    segb = (seg_i[:, None] == seg_i[None, :]).astype(BF16)
    z = in_proj_part(h, wi_arrs[0], None, order, 0, in_w)
    w_shards = [wi_arrs[0]]
    for j in range(3):
        w_shards[0], land = exchange_wait("gather_wi_wait%d" % j, plan_shard_ici(j), wi_sems[j],
                                          [w_shards[0], wi_arrs[1 + j]], z)
        land, = exchange("gather_wi_pass%d" % j, plan_shard_pass, [land], 1)
        z = in_proj_part(h, land, z, order, 1 + j, in_w)
        w_shards.append(land)
    wo_all, wg_all, wp_all, conv_all = exchange_wait("gather_rest_wait", plan_gather_ici(3), rest_sems[0], rest, z)
    pass_sems, passed, pass_token = exchange_start(
        "gather_rest_pass_start", [(plan_gather_pass, [0, 1, 2], 9)], [wo_all, wg_all, wp_all])
    conv_full = conv_all.transpose(1, 0, 2).reshape(SUBLANES, cw)
    qs, ks, vb = qk_prep_fwd(z, tabs, qg, kg, segb, aw, after=(pass_token,))
    attn, mix = attn_fwd(qs, ks, vb, z, attn_sinks, aw, d, ga_off)
    mix = conv_fwd(z, conv_full, mix, aw, cw, b_off)
    wo_all, wg_all, wp_all = exchange_wait("gather_rest_pass_wait", plan_gather_pass, pass_sems[0], passed, mix)
    wo_full = wo_all.reshape(d, d)
    wg_full = wg_all.reshape(d, d)
    sq = lambda shape: dict(
        a_spec=pl.BlockSpec((tm, tk), lambda i, j, k: (i, k)),
        o_spec=pl.BlockSpec((tm, tn), lambda i, j, k: (i, j)),
        out_shape=jax.ShapeDtypeStruct(shape, F32))
    x1 = matmul(mix, wo_full, grid=(s // tm, d // tn, d // tk),
                b_spec=pl.BlockSpec((tk, tn), lambda i, j, k: (k, j)), dims=NN, name="mm_x1",
                res=x2, res_spec=pl.BlockSpec((tm, tn), lambda i, j, k: (i, j)), **sq((s, d)))
    hg = rms_fwd(x1, ple_gate_norm_gain, "rms_fwd_x1")
    gl = matmul(hg, wg_full, grid=(s // tm, d // tn, d // tk),
                b_spec=pl.BlockSpec((tk, tn), lambda i, j, k: (k, j)), dims=NN, name="mm_gl", **sq((s, d)))
    pq = d // N_CHIPS

    d_gl, d_pe, dy, acc_head = head_fwd_bwd(x1, gl, p2, wp_all, tgt, b_ple_gate, ple_norm_gain)
    d_hg = matmul(d_gl, wg_full, grid=(s // tm, d // tn, d // tk),
                  b_spec=pl.BlockSpec((tn, tk), lambda i, j, k: (j, k)), dims=NT, name="mm_d_hg", **sq((s, d)))
    wgrad = lambda a_cols, shape3, o_spec, b_cols, name, a, b, grid: matmul(
        a, b, grid=grid,
        a_spec=pl.BlockSpec((ts, a_cols), lambda i, j, k: (k, i)),
        b_spec=pl.BlockSpec((ts, b_cols), lambda i, j, k: (k, j)),
        o_spec=o_spec, out_shape=jax.ShapeDtypeStruct(shape3, BF16), dims=TN, name=name)
    g_wg = wgrad(tn, (d, d), pl.BlockSpec((tn, tn), lambda i, j, k: (i, j)), tn, "mm_g_wg", hg, d_gl,
                 (d // tn, d // tn, s // ts))
    g_wp = wgrad(ple, (N_CHIPS, ple, pq), pl.BlockSpec((None, ple, pq), lambda i, j, k: (j, 0, 0)), pq,
                 "mm_g_wp", p2, d_pe, (1, N_CHIPS, s // ts))
    d_x1, d_x1b, acc_g = rms_bwd_add(d_hg, x1, dy, ple_gate_norm_gain, "rms_bwd_x1", True)
    d_mix = matmul(d_x1b, wo_full, grid=(s // tm, d // tn, d // tk),
                   b_spec=pl.BlockSpec((tn, tk), lambda i, j, k: (j, k)), dims=NT, name="mm_d_mix", **sq((s, d)))
    g_wo = wgrad(tn, (d, d), pl.BlockSpec((tn, tn), lambda i, j, k: (i, j)), tn, "mm_g_wo", mix, d_x1b,
                 (d // tn, d // tn, s // ts))
    def reduce_start(tag, grads):
        n = len(grads)
        lands = [lax.empty((N_CHIPS, a.shape[1] // 2, a.shape[2]), BF16) for a in grads]
        swapped = exchange("swap_" + tag, plan_swap, list(grads) + lands, n)
        parts = [add_sibling(g, o, core, "add_sibling_%s%d" % (tag, i))
                 for i, (g, o) in enumerate(zip(swapped[:n], swapped[n:]))]
        return exchange_start("scatter_%s_start" % tag, [(plan_scatter, list(range(2 * n)), 3 * n)],
                              parts + [lax.empty(a.shape, BF16) for a in parts])

    def reduce_sum(tag, handle, after):
        sems, arrays, _ = handle
        n = len(arrays) // 2
        got = exchange_wait("scatter_%s_wait" % tag, plan_scatter, sems[0], arrays, after)
        return [sum_chips(pt, o, place, "sum_chips_%s%d" % (tag, i))
                for i, (pt, o) in enumerate(zip(got[:n], got[n:]))]

    early = reduce_start("early", [g_wo.reshape(N_CHIPS, dq, d), g_wg.reshape(N_CHIPS, dq, d), g_wp])
    d_o, dz = gate_bwd(d_mix, attn, z, aw, ga_off, after=(early[-1],))
    dqs, dks, dvs, acc_sink = attn_bwd(qs, ks, vb, d_o, attn_sinks, aw)
    dz, acc_qk = qk_prep_bwd(z, dqs, dks, dvs, tabs, qg, kg, segb, dz, aw)
    dz, acc_conv = conv_bwd(z, d_mix, conv_full, dz, aw, cw, b_off)
    join_sems, joining, join_token = exchange_start(
        "join_early_start", [(plan_join, [0, 1, 2], 3)], reduce_sum("early", early, dz))
    g_wi = matmul(h, dz, grid=(d // tn, N_CHIPS, s // ts),
                  a_spec=pl.BlockSpec((ts, tn), lambda i, j, k: (k, i)),
                  b_spec=pl.BlockSpec((ts, sh), lambda i, j, k: (k, j)),
                  o_spec=pl.BlockSpec((None, tn, sh), lambda i, j, k: (j, i, 0)),
                  out_shape=jax.ShapeDtypeStruct((N_CHIPS, d, sh), BF16), dims=TN, name="mm_g_wi",
                  after=(join_token,))
    full_wo, full_wg, full_wp = exchange_wait("join_early_wait", plan_join, join_sems[0], joining, g_wi)
    late = reduce_start("late", [g_wi])
    d_h = in_proj_bwd(dz, w_shards, order, d, after=(late[-1],))
    grad_x, acc_x = rms_bwd_add(d_h, x2, d_x1, norm_gain, "rms_bwd_x", False)

    wsm = max(d, cw)
    misc = jnp.concatenate([acc_qk[0:1, :HEAD_DIM], acc_qk[1:2, :HEAD_DIM], acc_sink[0:1, :nq]], axis=1)
    small = jnp.concatenate([
        _pad_cols(acc_x[0:1], wsm), _pad_cols(acc_g[0:1], wsm), _pad_cols(acc_head[0:1], wsm),
        _pad_cols(acc_head[1:2], wsm), _pad_cols(misc, wsm), _pad_cols(acc_conv[0:3], wsm),
        _pad_rows(_pad_cols(acc_head[2:3], wsm), SUBLANES)], axis=0)
    device = 2 * chip + lax.axis_index("c")
    small_sems, small_bufs, small_token = exchange_start(
        "small_start", [(plan_allgather_small, [0], 7)],
        [lax.dynamic_update_slice(jnp.zeros((8,) + small.shape, F32), small[None], (device, 0, 0))])
    last_sems, last_halves, last_token = exchange_start(
        "join_late_start", [(plan_join, [0], 1)], reduce_sum("late", late, small_token))
    big, after = {}, (last_token,)
    for nm, g, w, m, v in (("w_out", full_wo, w_out, m_w_out, v_w_out),
                           ("w_ple_gate", full_wg, w_ple_gate, m_w_ple_gate, v_w_ple_gate),
                           ("w_ple_proj", full_wp, w_ple_proj, m_w_ple_proj, v_w_ple_proj)):
        stepped = adamw(g, w[0], m[0], v[0], "adamw_" + nm, after=after)
        big[nm], after = [o[None] for o in stepped], (stepped[1],)
    full_wi, = exchange_wait("join_late_wait", plan_join, last_sems[0], last_halves, after[0])
    big["w_in"] = [o[None] for o in adamw(full_wi, w_in[0], m_w_in[0], v_w_in[0], "adamw_w_in")]

    gathered, = exchange_wait("small_wait", plan_allgather_small, small_sems[0], small_bufs, big["w_in"][1])
    tot = sum_devices(gathered)
    loss = tot[SUBLANES, 0]

    def pack(vals):
        ng, pg, bg, eg, qgv, kgv, sk, cv = vals
        misc_v = jnp.concatenate([qgv, kgv, sk], axis=1)
        return jnp.concatenate([_pad_cols(ng, wsm), _pad_cols(pg, wsm), _pad_cols(bg, wsm), _pad_cols(eg, wsm),
                                _pad_cols(misc_v, wsm), _pad_cols(cv[0], wsm)], axis=0)

    g_small = jnp.concatenate(
        [tot[0:5], _pad_cols(lax.dynamic_slice(tot[5:8], (0, chip * cq), (3, cq)), wsm)], axis=0)
    w_small = pack((norm_gain, ple_gate_norm_gain, b_ple_gate, ple_norm_gain, q_norm_gain, k_norm_gain,
                    attn_sinks, conv_w))
    m_small = pack((m_norm_gain, m_ple_gate_norm_gain, m_b_ple_gate, m_ple_norm_gain, m_q_norm_gain,
                    m_k_norm_gain, m_attn_sinks, m_conv_w))
    v_small = pack((v_norm_gain, v_ple_gate_norm_gain, v_b_ple_gate, v_ple_norm_gain, v_q_norm_gain,
                    v_k_norm_gain, v_attn_sinks, v_conv_w))
    sm = adamw(g_small, w_small, m_small, v_small, "adamw_small")

    def unpack(a):
        return {"norm_gain": a[0:1, :d], "ple_gate_norm_gain": a[1:2, :d], "b_ple_gate": a[2:3, :d],
                "ple_norm_gain": a[3:4, :d], "q_norm_gain": a[4:5, :HEAD_DIM],
                "k_norm_gain": a[4:5, HEAD_DIM:2 * HEAD_DIM],
                "attn_sinks": a[4:5, 2 * HEAD_DIM:2 * HEAD_DIM + nq], "conv_w": a[5:8, :cq][None]}

    order = ("norm_gain", "w_in", "q_norm_gain", "k_norm_gain", "attn_sinks", "conv_w", "w_out",
             "ple_gate_norm_gain", "w_ple_gate", "b_ple_gate", "w_ple_proj", "ple_norm_gain")
    outs = [loss, grad_x[None]]
    for kind in range(4):
        table = unpack(sm[kind])
        for nm in order:
            outs.append(big[nm][kind] if nm in big else table[nm])
    return tuple(outs)
```

```python
import functools

import jax
import jax.numpy as jnp
from jax import lax
from jax.experimental import pallas as pl
from jax.experimental.pallas import tpu as pltpu

F32 = jnp.float32
BF16 = jnp.bfloat16
MESH = pl.DeviceIdType.MESH

HEAD_DIM = 64
N_KV_HEADS = 4
KV_WIDTH = N_KV_HEADS * HEAD_DIM
BLOCK = 128
ROT_DIM = 16
ROPE_THETA = 500000.0
EPS = 1e-6
NEG_INF = -1e30
N_CHIPS = 4
LANES = 128
SUBLANES = 8
COLT = 512
SEG = 256
VMEM_LIMIT = 48 * 1024 * 1024
VMEM_LIMIT_BIG = 60 * 1024 * 1024

ADAM_LR = 0.001
ADAM_B1 = 0.9
ADAM_B2 = 0.999
ADAM_EPS = 1e-08
ADAM_WD = 0.01
ADAM_STEP = 10

NN = (((1,), (0,)), ((), ()))
NT = (((1,), (1,)), ((), ()))
TN = (((0,), (0,)), ((), ()))


def _call(body, **kw):
    return pl.pallas_call(body, **kw)


def _call_after(body, after, **kw):
    n_in, n_after = len(kw["in_specs"]), len(after)
    kw["in_specs"] = list(kw["in_specs"]) + [pl.BlockSpec(memory_space=pl.ANY)] * n_after

    def body_after(*refs):
        body(*refs[:n_in], *refs[n_in + n_after:])

    call = _call(body_after, **kw)
    return lambda *args: call(*args, *after)


def _params(sem, vmem=VMEM_LIMIT):
    return pltpu.CompilerParams(dimension_semantics=sem, vmem_limit_bytes=vmem)


def _tile(n, pref):
    return pref if n % pref == 0 else n


def _sigmoid(v):
    return 1.0 / (1.0 + jnp.exp(-v))


def _hbm():
    return pl.BlockSpec(memory_space=pl.ANY)


def cast_bf16(a, name):
    r, c = a.shape
    tr = _tile(r, 512)

    def body(a_ref, o_ref):
        o_ref[...] = a_ref[...].astype(BF16)

    return _call(body, name=name, grid=(r // tr,),
                 in_specs=[pl.BlockSpec((tr, c), lambda i: (i, 0))],
                 out_specs=pl.BlockSpec((tr, c), lambda i: (i, 0)),
                 out_shape=jax.ShapeDtypeStruct((r, c), BF16),
                 compiler_params=_params(("parallel",)))(a)


def cast_into_slot(a, chip, name):
    r, c = a.shape
    tr = _tile(r, 512)

    def body(chip_ref, a_ref, o_ref):
        o_ref[...] = a_ref[...].astype(BF16)

    grid_spec = pltpu.PrefetchScalarGridSpec(
        num_scalar_prefetch=1, grid=(r // tr,),
        in_specs=[pl.BlockSpec((tr, c), lambda i, chip_ref: (i, 0))],
        out_specs=pl.BlockSpec((None, tr, c), lambda i, chip_ref: (chip_ref[0], i, 0)))
    return _call(body, name=name, grid_spec=grid_spec,
                 out_shape=jax.ShapeDtypeStruct((N_CHIPS, r, c), BF16),
                 compiler_params=_params(("parallel",)))(chip, a)


def rms_fwd(x, gain, name, after=()):
    s, d = x.shape
    tm = _tile(s, 512)

    def body(x_ref, g_ref, o_ref):
        xf = x_ref[...]
        r = lax.rsqrt(jnp.mean(xf * xf, axis=-1, keepdims=True) + EPS)
        o_ref[...] = ((xf * r) * g_ref[...]).astype(BF16)

    return _call_after(body, after, name=name, grid=(s // tm,),
                       in_specs=[pl.BlockSpec((tm, d), lambda i: (i, 0)),
                                 pl.BlockSpec((1, d), lambda i: (0, 0))],
                       out_specs=pl.BlockSpec((tm, d), lambda i: (i, 0)),
                       out_shape=jax.ShapeDtypeStruct((s, d), BF16),
                       compiler_params=_params(("parallel",)))(x, gain)


def rms_bwd_add(dyn, xin, add, gain, name, want_bf16):
    s, d = xin.shape
    tm = _tile(s, 256)

    def body(dy_ref, x_ref, a_ref, g_ref, *outs):
        i = pl.program_id(0)
        dx_ref, acc_ref = outs[0], outs[-1]
        xf = x_ref[...]
        r = lax.rsqrt(jnp.mean(xf * xf, axis=-1, keepdims=True) + EPS)
        xhat = xf * r
        dyv = dy_ref[...]
        gd = dyv * g_ref[...]
        dx = a_ref[...] + r * (gd - xhat * jnp.mean(xhat * gd, axis=-1, keepdims=True))
        dx_ref[...] = dx
        if want_bf16:
            outs[1][...] = dx.astype(BF16)

        @pl.when(i == 0)
        def _():
            acc_ref[...] = jnp.zeros_like(acc_ref)

        acc_ref[0:1, :] += jnp.sum(dyv * xhat, axis=0, keepdims=True)

    row = pl.BlockSpec((tm, d), lambda i: (i, 0))
    out_specs = [row] + ([row] if want_bf16 else []) + [pl.BlockSpec((SUBLANES, d), lambda i: (0, 0))]
    out_shape = ([jax.ShapeDtypeStruct((s, d), F32)]
                 + ([jax.ShapeDtypeStruct((s, d), BF16)] if want_bf16 else [])
                 + [jax.ShapeDtypeStruct((SUBLANES, d), F32)])
    return _call(body, name=name, grid=(s // tm,),
                 in_specs=[row, row, row, pl.BlockSpec((1, d), lambda i: (0, 0))],
                 out_specs=out_specs, out_shape=out_shape,
                 compiler_params=_params(("arbitrary",)))(dyn, xin, add, gain)


def head_fwd_bwd(x1, gl, p, wp, tgt, bias, ple_gain):
    s, d = x1.shape
    tm = _tile(s, 256)

    def body(x1_ref, gl_ref, p_ref, wp_ref, t_ref, b_ref, g_ref, dgl_ref, dpe_ref, dy_ref, acc_ref):
        i = pl.program_id(0)
        gate = _sigmoid(gl_ref[...] + b_ref[...])
        pb = p_ref[...].astype(BF16)
        pev = jnp.concatenate([jnp.dot(pb, wp_ref[q], preferred_element_type=F32) for q in range(N_CHIPS)],
                              axis=1)
        r = lax.rsqrt(jnp.mean(pev * pev, axis=-1, keepdims=True) + EPS)
        pehat = pev * r
        gain = g_ref[...]
        e = pehat * gain
        diff = (x1_ref[...] + gate * e) - t_ref[...]
        dy = diff * (1.0 / d)
        dy_ref[...] = dy
        d_gl = (dy * e) * (gate * (1.0 - gate))
        dgl_ref[...] = d_gl.astype(BF16)
        d_e = dy * gate
        gd = d_e * gain
        d_pe = r * (gd - pehat * jnp.mean(pehat * gd, axis=-1, keepdims=True))
        dpe_ref[...] = d_pe.astype(BF16)

        @pl.when(i == 0)
        def _():
            acc_ref[...] = jnp.zeros_like(acc_ref)

        acc_ref[0:1, :] += jnp.sum(d_gl, axis=0, keepdims=True)
        acc_ref[1:2, :] += jnp.sum(d_e * pehat, axis=0, keepdims=True)
        acc_ref[2:3, :] += jnp.sum(diff * diff, axis=0, keepdims=True) * (0.5 / d)

    row = pl.BlockSpec((tm, d), lambda i: (i, 0))
    vec = pl.BlockSpec((1, d), lambda i: (0, 0))
    return _call(body, name="head_fwd_bwd", grid=(s // tm,),
                 in_specs=[row, row, pl.BlockSpec((tm, p.shape[1]), lambda i: (i, 0)),
                           pl.BlockSpec(wp.shape, lambda i: (0, 0, 0)), row, vec, vec],
                 out_specs=[row, row, row, pl.BlockSpec((SUBLANES, d), lambda i: (0, 0))],
                 out_shape=[jax.ShapeDtypeStruct((s, d), BF16), jax.ShapeDtypeStruct((s, d), BF16),
                            jax.ShapeDtypeStruct((s, d), F32), jax.ShapeDtypeStruct((SUBLANES, d), F32)],
                 compiler_params=_params(("arbitrary",)))(x1, gl, p, wp, tgt, bias, ple_gain)


def adamw(g, w, m, v, name, after=()):
    r, c = g.shape
    tr = _tile(r, 256)

    def body(g_ref, w_ref, m_ref, v_ref, go_ref, d_ref, mo_ref, vo_ref):
        gv = g_ref[...]
        mn = ADAM_B1 * m_ref[...] + (1.0 - ADAM_B1) * gv
        vn = ADAM_B2 * v_ref[...] + (1.0 - ADAM_B2) * (gv * gv)
        m_hat = mn / (1.0 - ADAM_B1 ** ADAM_STEP)
        v_hat = vn / (1.0 - ADAM_B2 ** ADAM_STEP)
        go_ref[...] = gv
        d_ref[...] = -ADAM_LR * (m_hat / (jnp.sqrt(v_hat) + ADAM_EPS) + ADAM_WD * w_ref[...])
        mo_ref[...] = mn
        vo_ref[...] = vn

    blk = pl.BlockSpec((tr, c), lambda i: (i, 0))
    shp = jax.ShapeDtypeStruct((r, c), F32)
    return _call_after(body, after, name=name, grid=(r // tr,), in_specs=[blk] * 4, out_specs=[blk] * 4,
                       out_shape=[shp] * 4, compiler_params=_params(("parallel",)))(g, w, m, v)


def matmul(a, b, *, grid, a_spec, b_spec, o_spec, out_shape, dims, name, res=None, res_spec=None, after=(),
           vmem=VMEM_LIMIT):
    nk = grid[2]
    acc_shape = tuple(d for d in o_spec.block_shape if d is not None)

    def body(*refs):
        a_ref, b_ref = refs[:2]
        r_ref = refs[2] if res is not None else None
        o_ref = refs[3] if res is not None else refs[2]
        part = lax.dot_general(a_ref[...].astype(BF16), b_ref[...].astype(BF16), dims, preferred_element_type=F32)

        def finish(out):
            if res is not None:
                out = r_ref[...] + out
            o_ref[...] = out.astype(o_ref.dtype)

        if nk == 1:
            finish(part)
            return
        acc_ref = refs[-1]
        k = pl.program_id(2)

        @pl.when(k == 0)
        def _():
            acc_ref[...] = part

        @pl.when((k > 0) & (k < nk - 1))
        def _():
            acc_ref[...] += part

        @pl.when(k == nk - 1)
        def _():
            finish(acc_ref[...] + part)

    in_specs = [a_spec, b_spec] + ([res_spec] if res is not None else [])
    args = (a, b) + ((res,) if res is not None else ())
    return _call_after(body, after, name=name, grid=grid, in_specs=in_specs, out_specs=o_spec,
                       out_shape=out_shape, scratch_shapes=[pltpu.VMEM(acc_shape, F32)] if nk > 1 else [],
                       compiler_params=_params(("parallel", "parallel", "arbitrary"), vmem))(*args)


def in_proj_part(h, w, z_prev, order, q, in_w):
    s, d = h.shape
    sh = w.shape[1]
    tm = _tile(s, 1024)

    def body(order_ref, h_ref, w_ref, *rest):
        rest[-1][...] = jnp.dot(h_ref[...], w_ref[...], preferred_element_type=F32)

    in_specs = [pl.BlockSpec((tm, d), lambda i, order_ref: (i, 0)),
                pl.BlockSpec((d, sh), lambda i, order_ref: (0, 0))]
    args = [order, h, w]
    if z_prev is not None:
        in_specs.append(_hbm())
        args.append(z_prev)
    grid_spec = pltpu.PrefetchScalarGridSpec(
        num_scalar_prefetch=1, grid=(s // tm,), in_specs=in_specs,
        out_specs=pl.BlockSpec((tm, sh), lambda i, order_ref: (i, order_ref[q])))
    return _call(body, name="mm_z%d" % q, grid_spec=grid_spec,
                 out_shape=jax.ShapeDtypeStruct((s, in_w), F32),
                 input_output_aliases={3: 0} if z_prev is not None else {},
                 compiler_params=_params(("parallel",)))(*args)


def in_proj_bwd(dz, ws, order, d, after):
    s = dz.shape[0]
    sh = ws[0].shape[1]
    tm, tn = _tile(s, 512), _tile(d, 1024)
    nq, n_after = len(ws), len(after)

    def body(order_ref, *refs):
        a_refs, b_refs, o_ref = refs[:nq], refs[nq:2 * nq], refs[2 * nq + n_after]
        acc = lax.dot_general(a_refs[0][...], b_refs[0][...], NT, preferred_element_type=F32)
        for q in range(1, nq):
            acc += lax.dot_general(a_refs[q][...], b_refs[q][...], NT, preferred_element_type=F32)
        o_ref[...] = acc

    a_spec = lambda q: pl.BlockSpec((tm, sh), functools.partial(lambda i, j, order_ref, q: (i, order_ref[q]), q=q))
    grid_spec = pltpu.PrefetchScalarGridSpec(
        num_scalar_prefetch=1, grid=(s // tm, d // tn),
        in_specs=[a_spec(q) for q in range(nq)]
        + [pl.BlockSpec((tn, sh), lambda i, j, order_ref: (j, 0))] * nq + [_hbm()] * n_after,
        out_specs=pl.BlockSpec((tm, tn), lambda i, j, order_ref: (i, j)))
    return _call(body, name="mm_d_h", grid_spec=grid_spec, out_shape=jax.ShapeDtypeStruct((s, d), F32),
                 compiler_params=_params(("parallel", "parallel"), VMEM_LIMIT_BIG))(
                     order, *([dz] * nq), *ws, *after)


def _seg_mean(sq, segb):
    parts = []
    for cgrp in range(sq.shape[1] // SEG):
        blk = sq[:, cgrp * SEG:(cgrp + 1) * SEG]
        hi = blk.astype(BF16)
        r1 = blk - hi.astype(F32)
        mid = r1.astype(BF16)
        lo = (r1 - mid.astype(F32)).astype(BF16)
        acc = jnp.dot(hi, segb, preferred_element_type=F32)
        acc += jnp.dot(mid, segb, preferred_element_type=F32)
        acc += jnp.dot(lo, segb, preferred_element_type=F32)
        parts.append(acc)
    out = parts[0] if len(parts) == 1 else jnp.concatenate(parts, axis=1)
    return out * (1.0 / HEAD_DIM)


def _rope(v, cos, sa, sb):
    parts = []
    for cgrp in range(v.shape[1] // LANES):
        blk = v[:, cgrp * LANES:(cgrp + 1) * LANES]
        parts.append(blk * cos + pltpu.roll(blk, LANES - 8, 1) * sa + pltpu.roll(blk, 8, 1) * sb)
    return parts[0] if len(parts) == 1 else jnp.concatenate(parts, axis=1)


def _rope_t(dv, cos, sa, sb):
    parts = []
    for cgrp in range(dv.shape[1] // LANES):
        blk = dv[:, cgrp * LANES:(cgrp + 1) * LANES]
        parts.append(blk * cos + pltpu.roll(blk * sa, 8, 1) + pltpu.roll(blk * sb, LANES - 8, 1))
    return parts[0] if len(parts) == 1 else jnp.concatenate(parts, axis=1)


def _tile_lanes(vec, width):
    reps = width // LANES
    return vec if reps == 1 else jnp.tile(vec, (1, reps))


def qk_prep_fwd(z, tabs, qg, kg, segb, aw, after=()):
    s = z.shape[0]
    tm = _tile(s, 512)
    wq = aw + 2 * KV_WIDTH

    def body(z_ref, cos_ref, sa_ref, sb_ref, qg_ref, kg_ref, seg_ref, qs_ref, ks_ref, vb_ref):
        zz = z_ref[...]
        q, k, v = zz[:, :aw], zz[:, aw:aw + KV_WIDTH], zz[:, aw + KV_WIDTH:]
        cos, sa, sb, segm = cos_ref[...], sa_ref[...], sb_ref[...], seg_ref[...]
        rq = lax.rsqrt(_seg_mean(q * q, segm) + EPS)
        qn = (q * rq) * _tile_lanes(qg_ref[...], aw)
        qs_ref[...] = (_rope(qn, cos, sa, sb) * (HEAD_DIM ** -0.5)).astype(BF16)
        rk = lax.rsqrt(_seg_mean(k * k, segm) + EPS)
        kn = (k * rk) * _tile_lanes(kg_ref[...], KV_WIDTH)
        ks_ref[...] = _rope(kn, cos, sa, sb).astype(BF16)
        vb_ref[...] = v.astype(BF16)

    tab = pl.BlockSpec((tm, LANES), lambda i: (i, 0))
    vec = pl.BlockSpec((1, LANES), lambda i: (0, 0))
    return _call_after(body, after, name="qk_prep_fwd", grid=(s // tm,),
                       in_specs=[pl.BlockSpec((tm, wq), lambda i: (i, 0)), tab, tab, tab, vec, vec,
                                 pl.BlockSpec((SEG, SEG), lambda i: (0, 0))],
                       out_specs=[pl.BlockSpec((tm, aw), lambda i: (i, 0)),
                                  pl.BlockSpec((tm, KV_WIDTH), lambda i: (i, 0)),
                                  pl.BlockSpec((tm, KV_WIDTH), lambda i: (i, 0))],
                       out_shape=[jax.ShapeDtypeStruct((s, aw), BF16), jax.ShapeDtypeStruct((s, KV_WIDTH), BF16),
                                  jax.ShapeDtypeStruct((s, KV_WIDTH), BF16)],
                       compiler_params=_params(("parallel",)))(z, *tabs, qg, kg, segb)


def qk_prep_bwd(z, dqs, dks, dvs, tabs, qg, kg, segb, dz, aw):
    s = z.shape[0]
    tm = _tile(s, 512)
    wq = aw + 2 * KV_WIDTH

    def body(z_ref, dq_ref, dk_ref, dv_ref, cos_ref, sa_ref, sb_ref, qg_ref, kg_ref, seg_ref, dz_in,
             dz_ref, acc_ref):
        i = pl.program_id(0)
        zz = z_ref[...]
        q, k = zz[:, :aw], zz[:, aw:aw + KV_WIDTH]
        cos, sa, sb, segm = cos_ref[...], sa_ref[...], sb_ref[...], seg_ref[...]

        def one(xv, dout, gvec, width):
            g = _tile_lanes(gvec, width)
            r = lax.rsqrt(_seg_mean(xv * xv, segm) + EPS)
            xhat = xv * r
            dn = _rope_t(dout, cos, sa, sb)
            gd = dn * g
            dx = r * (gd - xhat * _seg_mean(xhat * gd, segm))
            contrib = jnp.sum(dn * xhat, axis=0, keepdims=True)
            folded = contrib[:, :LANES]
            for cgrp in range(1, width // LANES):
                folded = folded + contrib[:, cgrp * LANES:(cgrp + 1) * LANES]
            return dx, folded + pltpu.roll(folded, HEAD_DIM, 1)

        dq, gq = one(q, dq_ref[...] * (HEAD_DIM ** -0.5), qg_ref[...], aw)
        dk, gk = one(k, dk_ref[...], kg_ref[...], KV_WIDTH)
        dz_ref[:, :aw] = dq.astype(BF16)
        dz_ref[:, aw:aw + KV_WIDTH] = dk.astype(BF16)
        dz_ref[:, aw + KV_WIDTH:] = dv_ref[...].astype(BF16)

        @pl.when(i == 0)
        def _():
            acc_ref[...] = jnp.zeros_like(acc_ref)

        acc_ref[0:1, :] += gq
        acc_ref[1:2, :] += gk

    tab = pl.BlockSpec((tm, LANES), lambda i: (i, 0))
    vec = pl.BlockSpec((1, LANES), lambda i: (0, 0))
    kvb = pl.BlockSpec((tm, KV_WIDTH), lambda i: (i, 0))
    return _call(body, name="qk_prep_bwd", grid=(s // tm,),
                 in_specs=[pl.BlockSpec((tm, wq), lambda i: (i, 0)),
                           pl.BlockSpec((tm, aw), lambda i: (i, 0)), kvb, kvb, tab, tab, tab, vec, vec,
                           pl.BlockSpec((SEG, SEG), lambda i: (0, 0)), _hbm()],
                 out_specs=[pl.BlockSpec((tm, wq), lambda i: (i, 0)),
                            pl.BlockSpec((SUBLANES, LANES), lambda i: (0, 0))],
                 out_shape=[jax.ShapeDtypeStruct(dz.shape, BF16), jax.ShapeDtypeStruct((SUBLANES, LANES), F32)],
                 input_output_aliases={10: 0},
                 compiler_params=_params(("arbitrary",)))(z, dqs, dks, dvs, *tabs, qg, kg, segb, dz)


def _placed(band, lane_idx):
    out = {}
    for kh in range(N_KV_HEADS):
        grp = band[:, (kh // 2) * LANES:(kh // 2 + 1) * LANES]
        for half in (0, 1):
            t = grp if half == kh % 2 else pltpu.roll(grp, HEAD_DIM, 1)
            keep = (lane_idx >= half * HEAD_DIM) & (lane_idx < (half + 1) * HEAD_DIM)
            out[kh, half] = jnp.where(keep, t, jnp.zeros_like(t))
    return out


def _softmax_with_sink(sc, valid, sink):
    sc = jnp.where(valid, sc, NEG_INF)
    m = jnp.maximum(jnp.max(sc, axis=-1, keepdims=True), sink)
    e = jnp.exp(sc - m)
    es = jnp.exp(sink - m)
    den = jnp.sum(e, axis=-1, keepdims=True) + es
    return e / den, es / den


def _stack_halves(placed, kh):
    return jnp.concatenate([placed[kh, 0], placed[kh, 1]], axis=0)


def _valid_mask(n):
    r_i = lax.broadcasted_iota(jnp.int32, (BLOCK, 2 * BLOCK), 0)
    j_i = lax.broadcasted_iota(jnp.int32, (BLOCK, 2 * BLOCK), 1)
    return (j_i > r_i) & (j_i <= r_i + BLOCK) & ((n > 0) | (j_i >= BLOCK))


def attn_fwd(qs, ks, vb, z, sinks, aw, d, ga_off):
    s = qs.shape[0]
    nb = s // BLOCK
    nq = aw // HEAD_DIM
    grp_sz = nq // N_KV_HEADS
    n_ga = aw // COLT

    def body(q_ref, ko_ref, kp_ref, vo_ref, vp_ref, *rest):
        ga_refs = rest[:n_ga]
        sink_ref, attn_ref, mix_ref = rest[n_ga:]
        n = pl.program_id(0)
        lane_idx = lax.broadcasted_iota(jnp.int32, (2 * BLOCK, LANES), 1)
        kpl = _placed(jnp.concatenate([kp_ref[...], ko_ref[...]], axis=0), lane_idx)
        vpl = _placed(jnp.concatenate([vp_ref[...], vo_ref[...]], axis=0), lane_idx)
        valid = _valid_mask(n)
        groups = range(nq // 2)
        kcat = [_stack_halves(kpl, kh) for kh in range(N_KV_HEADS)]
        vcat = [_stack_halves(vpl, kh) for kh in range(N_KV_HEADS)]
        scs = [lax.dot_general(q_ref[:, c * LANES:(c + 1) * LANES], kcat[2 * c // grp_sz], NT,
                               preferred_element_type=F32) for c in groups]
        probs = [jnp.concatenate(
            [_softmax_with_sink(scs[c][:, half * 2 * BLOCK:(half + 1) * 2 * BLOCK], valid,
                                sink_ref[0, 2 * c + half])[0].astype(BF16) for half in (0, 1)], axis=1)
                 for c in groups]
        for c in groups:
            acc = jnp.dot(probs[c], vcat[2 * c // grp_sz], preferred_element_type=F32)
            attn_ref[:, c * LANES:(c + 1) * LANES] = acc
            col = c * LANES
            ga = ga_refs[col // COLT][:, col % COLT:col % COLT + LANES]
            mix_ref[:, c * LANES:(c + 1) * LANES] = (acc * (ga * _sigmoid(ga))).astype(BF16)

    own = lambda n: (n, 0)
    prev = lambda n: (jnp.maximum(n - 1, 0), 0)
    kvs = lambda imap: pl.BlockSpec((BLOCK, KV_WIDTH), imap)
    ga_specs = [pl.BlockSpec((BLOCK, COLT), functools.partial(lambda n, j: (n, ga_off + j), j=j))
                for j in range(n_ga)]
    return _call(body, name="attn_fwd", grid=(nb,),
                 in_specs=[pl.BlockSpec((BLOCK, aw), own), kvs(own), kvs(prev), kvs(own), kvs(prev)]
                 + ga_specs + [pl.BlockSpec(memory_space=pltpu.SMEM)],
                 out_specs=[pl.BlockSpec((BLOCK, aw), own), pl.BlockSpec((BLOCK, aw), own)],
                 out_shape=[jax.ShapeDtypeStruct((s, aw), F32), jax.ShapeDtypeStruct((s, d), BF16)],
                 compiler_params=_params(("parallel",)))(qs, ks, ks, vb, vb, *([z] * n_ga), sinks)


def gate_bwd(d_mix, attn, z, aw, ga_off, after=()):
    s, in_w = z.shape
    tm = _tile(s, 1024)

    def body(dm_ref, at_ref, ga_ref, do_ref, dz_ref):
        ga = ga_ref[...]
        sig = _sigmoid(ga)
        dm = dm_ref[...]
        do_ref[...] = (dm * (ga * sig)).astype(BF16)
        dz_ref[...] = ((dm * at_ref[...]) * (sig * (1.0 + ga * (1.0 - sig)))).astype(BF16)

    blk = pl.BlockSpec((tm, COLT), lambda i, j: (i, j))
    gab = pl.BlockSpec((tm, COLT), lambda i, j: (i, ga_off + j))
    return _call_after(body, after, name="gate_bwd", grid=(s // tm, aw // COLT),
                       in_specs=[blk, blk, gab], out_specs=[blk, gab],
                       out_shape=[jax.ShapeDtypeStruct((s, aw), BF16), jax.ShapeDtypeStruct((s, in_w), BF16)],
                       compiler_params=_params(("parallel", "parallel")))(d_mix, attn, z)


def attn_bwd(qs, ks, vb, d_o, sinks, aw):
    s = qs.shape[0]
    nb = s // BLOCK
    nq = aw // HEAD_DIM
    grp_sz = nq // N_KV_HEADS

    def body(q_ref, ko_ref, kp_ref, vo_ref, vp_ref, do_ref, sink_ref, dq_ref, dk_ref, dv_ref, ds_ref,
             ck_ref, cv_ref):
        n = pl.program_id(0)

        @pl.when(n == 0)
        def _():
            ds_ref[...] = jnp.zeros_like(ds_ref)
            dk_ref[...] = jnp.zeros_like(dk_ref)
            dv_ref[...] = jnp.zeros_like(dv_ref)

        @pl.when(n < nb)
        def _():
            lane_idx = lax.broadcasted_iota(jnp.int32, (2 * BLOCK, LANES), 1)
            lane_row = lax.broadcasted_iota(jnp.int32, (1, LANES), 1)
            kpl = _placed(jnp.concatenate([kp_ref[...], ko_ref[...]], axis=0), lane_idx)
            vpl = _placed(jnp.concatenate([vp_ref[...], vo_ref[...]], axis=0), lane_idx)
            valid = _valid_mask(n)
            ds_row = jnp.zeros((1, LANES), F32)
            wide = 2 * BLOCK
            groups = range(nq // 2)
            per_kv = grp_sz // 2
            kcat = [_stack_halves(kpl, kh) for kh in range(N_KV_HEADS)]
            vcat = [_stack_halves(vpl, kh) for kh in range(N_KV_HEADS)]
            qg = [q_ref[:, c * LANES:(c + 1) * LANES] for c in groups]
            dog = [do_ref[:, c * LANES:(c + 1) * LANES] for c in groups]
            scs = [lax.dot_general(qg[c], kcat[c // per_kv], NT, preferred_element_type=F32) for c in groups]
            dps = [lax.dot_general(dog[c], vcat[c // per_kv], NT, preferred_element_type=F32) for c in groups]
            ds_pairs, p_pairs = [], []
            for c in groups:
                probs, dss = [], []
                for half in (0, 1):
                    h = 2 * c + half
                    cols = slice(half * wide, (half + 1) * wide)
                    p, p_sink = _softmax_with_sink(scs[c][:, cols], valid, sink_ref[0, h])
                    delta = jnp.sum(p * dps[c][:, cols], axis=-1, keepdims=True)
                    dss.append((p * (dps[c][:, cols] - delta)).astype(BF16))
                    probs.append(p.astype(BF16))
                    dsink = -jnp.sum(p_sink * delta, axis=0, keepdims=True)
                    ds_row += jnp.where(lane_row == h, dsink, 0.0)
                ds_pairs.append(jnp.concatenate(dss, axis=1))
                p_pairs.append(jnp.concatenate(probs, axis=1))
            for c in groups:
                dq_ref[:, c * LANES:(c + 1) * LANES] = jnp.dot(ds_pairs[c], kcat[c // per_kv],
                                                               preferred_element_type=F32)
            dkts = [lax.dot_general(qg[c], ds_pairs[c], TN, preferred_element_type=F32) for c in groups]
            dvts = [lax.dot_general(dog[c], p_pairs[c], TN, preferred_element_type=F32) for c in groups]
            dkt, dvt = [], []
            for kh in range(N_KV_HEADS):
                for parts, out in ((dkts, dkt), (dvts, dvt)):
                    tot = parts[kh * per_kv]
                    for c in range(kh * per_kv + 1, (kh + 1) * per_kv):
                        tot = tot + parts[c]
                    out.append(tot[:HEAD_DIM, :wide] + tot[HEAD_DIM:, wide:])
            ds_ref[0:1, :] += ds_row
            back = lambda t: jnp.concatenate(
                [jnp.concatenate(t[2 * g:2 * g + 2], axis=0).T for g in range(N_KV_HEADS // 2)], axis=1)
            dk_band, dv_band = back(dkt), back(dvt)

            @pl.when(n > 0)
            def _():
                dk_ref[...] = ck_ref[...] + dk_band[:BLOCK]
                dv_ref[...] = cv_ref[...] + dv_band[:BLOCK]

            ck_ref[...] = dk_band[BLOCK:]
            cv_ref[...] = dv_band[BLOCK:]

        @pl.when(n == nb)
        def _():
            dk_ref[...] = ck_ref[...]
            dv_ref[...] = cv_ref[...]

    own = lambda n: (jnp.minimum(n, nb - 1), 0)
    prev = lambda n: (jnp.clip(n - 1, 0, nb - 1), 0)
    done = lambda n: (jnp.maximum(n - 1, 0), 0)
    kvs = lambda imap: pl.BlockSpec((BLOCK, KV_WIDTH), imap)
    return _call(body, name="attn_bwd", grid=(nb + 1,),
                 in_specs=[pl.BlockSpec((BLOCK, aw), own), kvs(own), kvs(prev), kvs(own), kvs(prev),
                           pl.BlockSpec((BLOCK, aw), own), pl.BlockSpec(memory_space=pltpu.SMEM)],
                 out_specs=[pl.BlockSpec((BLOCK, aw), own), kvs(done), kvs(done),
                            pl.BlockSpec((SUBLANES, LANES), lambda n: (0, 0))],
                 out_shape=[jax.ShapeDtypeStruct((s, aw), F32), jax.ShapeDtypeStruct((s, KV_WIDTH), F32),
                            jax.ShapeDtypeStruct((s, KV_WIDTH), F32), jax.ShapeDtypeStruct((SUBLANES, LANES), F32)],
                 scratch_shapes=[pltpu.VMEM((BLOCK, KV_WIDTH), F32), pltpu.VMEM((BLOCK, KV_WIDTH), F32)],
                 compiler_params=_params(("arbitrary",)))(qs, ks, ks, vb, vb, d_o, sinks)


def conv_fwd(z, conv_w, mix, aw, cw, b_off):
    s = z.shape[0]
    tm = _tile(s, 1024)
    nseg = cw // COLT
    hb = tm // SUBLANES

    def body(b_ref, c_ref, h_ref, g_ref, cp_ref, hp_ref, w_ref, mix_in, mix_ref):
        i = pl.program_id(0)
        u = c_ref[...] * h_ref[...]
        up = jnp.where(i > 0, cp_ref[...] * hp_ref[...], 0.0)
        ext = jnp.concatenate([up, u], axis=0)
        um1 = pltpu.roll(ext, 1, 0)[SUBLANES:]
        um2 = pltpu.roll(ext, 2, 0)[SUBLANES:]
        w = w_ref[...]
        cv = w[0:1] * um2 + w[1:2] * um1 + w[2:3] * u
        g = g_ref[...]
        mix_ref[...] = ((b_ref[...] * cv) * (g * _sigmoid(g))).astype(BF16)

    seg = lambda k: pl.BlockSpec((tm, COLT), functools.partial(lambda i, j, k: (i, b_off + k * nseg + j), k=k))
    halo = lambda k: pl.BlockSpec(
        (SUBLANES, COLT), functools.partial(lambda i, j, k: (jnp.maximum(i * hb - 1, 0), b_off + k * nseg + j), k=k))
    return _call(body, name="conv_fwd", grid=(s // tm, nseg),
                 in_specs=[seg(0), seg(1), seg(2), seg(3), halo(1), halo(2),
                           pl.BlockSpec((SUBLANES, COLT), lambda i, j: (0, j)), _hbm()],
                 out_specs=pl.BlockSpec((tm, COLT), lambda i, j: (i, aw // COLT + j)),
                 out_shape=jax.ShapeDtypeStruct(mix.shape, BF16),
                 input_output_aliases={7: 0},
                 compiler_params=_params(("parallel", "parallel")))(z, z, z, z, z, z, conv_w, mix)


def conv_bwd(z, d_mix, conv_w, dz, aw, cw, b_off):
    s = z.shape[0]
    tm = _tile(s, 512)
    nseg = cw // COLT
    hb = tm // SUBLANES
    n_row = s // tm
    last_h = s // SUBLANES - 1
    n_step = nseg * n_row

    def body(b_ref, c_ref, h_ref, g_ref, cp_ref, hp_ref, bn_ref, gn_ref, dm_ref, dmn_ref, w_ref, dz_in,
             dz_ref, acc_ref, stash_ref, sems):
        j = pl.program_id(0)
        i = pl.program_id(1)
        step = j * n_row + i
        slot = step % 2

        def copies(from_slot, at_step):
            jj, ii = at_step // n_row, at_step % n_row
            rows = pl.ds(pl.multiple_of(ii * tm, tm), tm)
            return [pltpu.make_async_copy(
                stash_ref.at[from_slot, q],
                dz_ref.at[rows, pl.ds(pl.multiple_of((b_off + q * nseg + jj) * COLT, COLT), COLT)],
                sems.at[from_slot, q]) for q in range(4)]

        @pl.when(step >= 2)
        def _():
            for cp in copies(slot, step - 2):
                cp.wait()

        cc, hh, bb, g = c_ref[...], h_ref[...], b_ref[...], g_ref[...]
        w = w_ref[...]
        u = cc * hh
        up = jnp.where(i > 0, cp_ref[...] * hp_ref[...], 0.0)
        ext = jnp.concatenate([up, u], axis=0)
        um1 = pltpu.roll(ext, 1, 0)[SUBLANES:]
        um2 = pltpu.roll(ext, 2, 0)[SUBLANES:]
        cv = w[0:1] * um2 + w[1:2] * um1 + w[2:3] * u
        sig = _sigmoid(g)
        sg = g * sig
        dm = dm_ref[...]
        d_cv = (dm * bb) * sg
        gn = gn_ref[...]
        d_cv_next = jnp.where(i < n_row - 1, (dmn_ref[...] * bn_ref[...]) * (gn * _sigmoid(gn)), 0.0)
        ext2 = jnp.concatenate([d_cv, d_cv_next], axis=0)
        dp1 = pltpu.roll(ext2, tm + SUBLANES - 1, 0)[:tm]
        dp2 = pltpu.roll(ext2, tm + SUBLANES - 2, 0)[:tm]
        d_u = w[2:3] * d_cv + w[1:2] * dp1 + w[0:1] * dp2
        stash_ref[slot, 0] = ((dm * cv) * sg).astype(BF16)
        stash_ref[slot, 1] = (d_u * hh).astype(BF16)
        stash_ref[slot, 2] = (d_u * cc).astype(BF16)
        stash_ref[slot, 3] = (((dm * bb) * cv) * (sig * (1.0 + g * (1.0 - sig)))).astype(BF16)
        for cp in copies(slot, step):
            cp.start()

        @pl.when(i == 0)
        def _():
            acc_ref[...] = jnp.zeros_like(acc_ref)

        acc_ref[0:1, :] += jnp.sum(d_cv * um2, axis=0, keepdims=True)
        acc_ref[1:2, :] += jnp.sum(d_cv * um1, axis=0, keepdims=True)
        acc_ref[2:3, :] += jnp.sum(d_cv * u, axis=0, keepdims=True)

        @pl.when(step == n_step - 1)
        def _():
            if n_step >= 2:
                for cp in copies(1 - slot, step - 1):
                    cp.wait()
            for cp in copies(slot, step):
                cp.wait()

    seg = lambda q: pl.BlockSpec((tm, COLT), functools.partial(lambda j, i, q: (i, b_off + q * nseg + j), q=q))
    halo_p = lambda q: pl.BlockSpec(
        (SUBLANES, COLT),
        functools.partial(lambda j, i, q: (jnp.maximum(i * hb - 1, 0), b_off + q * nseg + j), q=q))
    halo_n = lambda q: pl.BlockSpec(
        (SUBLANES, COLT),
        functools.partial(lambda j, i, q: (jnp.minimum((i + 1) * hb, last_h), b_off + q * nseg + j), q=q))
    return _call(body, name="conv_bwd", grid=(nseg, n_row),
                 in_specs=[seg(0), seg(1), seg(2), seg(3), halo_p(1), halo_p(2), halo_n(0), halo_n(3),
                           pl.BlockSpec((tm, COLT), lambda j, i: (i, aw // COLT + j)),
                           pl.BlockSpec((SUBLANES, COLT),
                                        lambda j, i: (jnp.minimum((i + 1) * hb, last_h), aw // COLT + j)),
                           pl.BlockSpec((SUBLANES, COLT), lambda j, i: (0, j)), _hbm()],
                 out_specs=[_hbm(), pl.BlockSpec((SUBLANES, COLT), lambda j, i: (0, j))],
                 out_shape=[jax.ShapeDtypeStruct(dz.shape, BF16), jax.ShapeDtypeStruct((SUBLANES, cw), F32)],
                 input_output_aliases={11: 0},
                 scratch_shapes=[pltpu.VMEM((2, 4, tm, COLT), BF16), pltpu.SemaphoreType.DMA((2, 4))],
                 compiler_params=_params(("arbitrary", "arbitrary")))(
                     z, z, z, z, z, z, z, z, d_mix, d_mix, conv_w, dz)


def _place():
    x, y, c = lax.axis_index("x"), lax.axis_index("y"), lax.axis_index("c")
    chips = [(1 - x, y), (x, 1 - y), (1 - x, 1 - y)]
    return x, y, c, chips


def _remote(src, dst, send_sem, recv_sem, device):
    return pltpu.make_async_remote_copy(src_ref=src, dst_ref=dst, send_sem=send_sem, recv_sem=recv_sem,
                                        device_id=device, device_id_type=MESH)


def plan_gather_ici(n_split):
    def plan(refs, send, recv):
        x, y, c, chips = _place()
        me = 2 * x + y
        mine, theirs = [], []
        for w, ref in enumerate(refs):
            for j, (cx, cy) in enumerate(chips):
                def part(slot):
                    if w >= n_split:
                        return ref.at[slot]
                    hr = ref.shape[1] // 2
                    return ref.at[slot, pl.ds(c * hr, hr)]
                k = 3 * w + j
                mine.append(_remote(part(me), part(me), send.at[k], recv.at[k], (cx, cy, c)))
                theirs.append(_remote(part(2 * cx + cy), part(2 * cx + cy), send.at[k], recv.at[k], (cx, cy, c)))
        return mine, theirs
    return plan


def plan_shard_ici(j):
    def plan(refs, send, recv):
        _, _, c, chips = _place()
        own, land = refs
        hr = own.shape[0] // 2
        rows = pl.ds(c * hr, hr)
        cp = _remote(own.at[rows], land.at[rows], send.at[0], recv.at[0], (*chips[j], c))
        return [cp], [cp]
    return plan


def plan_shard_pass(refs, send, recv):
    x, y, c, _ = _place()
    land, = refs
    hr = land.shape[0] // 2
    half = lambda core: land.at[pl.ds(core * hr, hr)]
    return ([_remote(half(c), half(c), send.at[0], recv.at[0], (x, y, 1 - c))],
            [_remote(half(1 - c), half(1 - c), send.at[0], recv.at[0], (x, y, 1 - c))])


def plan_gather_pass(refs, send, recv):
    x, y, c, chips = _place()
    mine, theirs = [], []
    for w, ref in enumerate(refs):
        hr = ref.shape[1] // 2
        for j, (cx, cy) in enumerate(chips):
            half = lambda core: ref.at[2 * cx + cy, pl.ds(core * hr, hr)]
            k = 3 * w + j
            mine.append(_remote(half(c), half(c), send.at[k], recv.at[k], (x, y, 1 - c)))
            theirs.append(_remote(half(1 - c), half(1 - c), send.at[k], recv.at[k], (x, y, 1 - c)))
    return mine, theirs


def plan_swap(refs, send, recv):
    x, y, c, _ = _place()
    nw = len(refs) // 2
    mine = []
    for w in range(nw):
        hr = refs[w].shape[1] // 2
        mine.append(_remote(refs[w].at[:, pl.ds((1 - c) * hr, hr), :], refs[nw + w], send.at[w], recv.at[w],
                            (x, y, 1 - c)))
    return mine, mine


def plan_scatter(refs, send, recv):
    x, y, c, chips = _place()
    me = 2 * x + y
    nw = len(refs) // 2
    mine, theirs = [], []
    for w in range(nw):
        for j, (cx, cy) in enumerate(chips):
            k = 3 * w + j
            mine.append(_remote(refs[w].at[2 * cx + cy], refs[nw + w].at[me], send.at[k], recv.at[k], (cx, cy, c)))
            theirs.append(_remote(refs[w].at[me], refs[nw + w].at[2 * cx + cy], send.at[k], recv.at[k], (cx, cy, c)))
    return mine, theirs


def plan_join(refs, send, recv):
    x, y, c, _ = _place()
    mine, theirs = [], []
    for w, ref in enumerate(refs):
        hr = ref.shape[0] // 2
        half = lambda core: ref.at[pl.ds(core * hr, hr)]
        mine.append(_remote(half(c), half(c), send.at[w], recv.at[w], (x, y, 1 - c)))
        theirs.append(_remote(half(1 - c), half(1 - c), send.at[w], recv.at[w], (x, y, 1 - c)))
    return mine, theirs


def exchange(name, plan, arrays, n):
    na = len(arrays)

    def body(*refs):
        mine, theirs = plan(refs[:na], refs[2 * na], refs[2 * na + 1])
        for cp in mine:
            cp.start()
        for cp in theirs:
            cp.wait_recv()
        for cp in mine:
            cp.wait_send()

    return _call(body, name=name, in_specs=[_hbm()] * na, out_specs=[_hbm()] * na,
                 out_shape=[jax.ShapeDtypeStruct(a.shape, a.dtype) for a in arrays],
                 input_output_aliases={i: i for i in range(na)},
                 scratch_shapes=[pltpu.SemaphoreType.DMA((n,)), pltpu.SemaphoreType.DMA((n,))])(*arrays)


def exchange_start(name, groups, arrays, after=()):
    na, ng = len(arrays), len(groups)

    def body(*refs):
        for g, (plan, idx, _) in enumerate(groups):
            mine, _ = plan([refs[i] for i in idx], refs[na + 2 * g], refs[na + 2 * g + 1])
            for cp in mine:
                cp.start()
        refs[-1][...] = jnp.zeros_like(refs[-1])

    hbm = pl.BlockSpec(memory_space=pltpu.HBM)
    sem = pl.BlockSpec(memory_space=pltpu.SEMAPHORE)
    sem_types = [pltpu.SemaphoreType.DMA((n,)) for _, _, n in groups for _ in (0, 1)]
    outs = _call_after(body, after, name=name, in_specs=[hbm] * na,
                       out_specs=[sem] * (2 * ng) + [hbm] * na + [pl.BlockSpec(memory_space=pltpu.VMEM)],
                       out_shape=sem_types + [pltpu.HBM(a.shape, a.dtype) for a in arrays]
                       + [jax.ShapeDtypeStruct((SUBLANES, LANES), F32)],
                       input_output_aliases={i: i + 2 * ng for i in range(na)},
                       compiler_params=pltpu.CompilerParams(
                           has_side_effects=pltpu.SideEffectType.DATAFLOW_SIDE_EFFECTING))(
                               *[pltpu.with_memory_space_constraint(a, pltpu.HBM) for a in arrays])
    sems = [(outs[2 * g], outs[2 * g + 1]) for g in range(ng)]
    return sems, list(outs[2 * ng:-1]), outs[-1]


def exchange_wait(name, plan, sems, arrays, after):
    send_sems, recv_sems = sems
    na = len(arrays)

    def body(*refs):
        mine, theirs = plan(refs[:na], refs[na], refs[na + 1])
        for cp in mine:
            cp.wait_send()
        for cp in theirs:
            cp.wait_recv()

    hbm = pl.BlockSpec(memory_space=pltpu.HBM)
    sem = pl.BlockSpec(memory_space=pltpu.SEMAPHORE)
    return _call(body, name=name, in_specs=[hbm] * na + [sem, sem, _hbm()], out_specs=[hbm] * na,
                 out_shape=[pltpu.HBM(a.shape, a.dtype) for a in arrays],
                 input_output_aliases={i: i for i in range(na)},
                 compiler_params=pltpu.CompilerParams(
                     has_side_effects=pltpu.SideEffectType.DATAFLOW_SIDE_EFFECTING))(
                         *arrays, send_sems, recv_sems, after)


def plan_allgather_small(refs, send, recv):
    x, y, c, _ = _place()
    buf, = refs
    mine, theirs = [], []
    flips = [(fx, fy, fc) for fx in (0, 1) for fy in (0, 1) for fc in (0, 1)][1:]
    for k, (fx, fy, fc) in enumerate(flips):
        peer = (x ^ fx, y ^ fy, c ^ fc)
        slot = lambda px, py, pc: buf.at[4 * px + 2 * py + pc]
        mine.append(_remote(slot(x, y, c), slot(x, y, c), send.at[k], recv.at[k], peer))
        theirs.append(_remote(slot(*peer), slot(*peer), send.at[k], recv.at[k], peer))
    return mine, theirs


def add_sibling(grad, got, core, name):
    _, r, c = grad.shape
    hr = r // 2
    tr = _tile(hr, 512)
    nblk = hr // tr

    def body(core_ref, g_ref, o_ref, out_ref):
        out_ref[...] = (g_ref[...].astype(F32) + o_ref[...].astype(F32)).astype(BF16)

    grid_spec = pltpu.PrefetchScalarGridSpec(
        num_scalar_prefetch=1, grid=(N_CHIPS, nblk),
        in_specs=[pl.BlockSpec((None, tr, c), lambda t, i, core_ref: (t, core_ref[0] * nblk + i, 0)),
                  pl.BlockSpec((None, tr, c), lambda t, i, core_ref: (t, i, 0))],
        out_specs=pl.BlockSpec((None, tr, c), lambda t, i, core_ref: (t, i, 0)))
    return _call(body, name=name, grid_spec=grid_spec,
                 out_shape=jax.ShapeDtypeStruct((N_CHIPS, hr, c), BF16),
                 compiler_params=_params(("parallel", "parallel")))(core, grad, got)


def sum_chips(mine, owned, place, name):
    _, hr, c = mine.shape
    tr = _tile(hr, 512)
    nblk = hr // tr

    def body(place_ref, m_ref, o1_ref, o2_ref, o3_ref, out_ref):
        acc = m_ref[...].astype(F32)
        for o_ref in (o1_ref, o2_ref, o3_ref):
            acc = acc + o_ref[...].astype(F32)
        out_ref[...] = acc

    other = lambda k: pl.BlockSpec((None, tr, c), lambda i, place_ref: ((place_ref[0] + k) % N_CHIPS, i, 0))
    grid_spec = pltpu.PrefetchScalarGridSpec(
        num_scalar_prefetch=1, grid=(nblk,),
        in_specs=[other(0), other(1), other(2), other(3)],
        out_specs=pl.BlockSpec((tr, c), lambda i, place_ref: (place_ref[1] * nblk + i, 0)))
    return _call(body, name=name, grid_spec=grid_spec,
                 out_shape=jax.ShapeDtypeStruct((2 * hr, c), F32),
                 compiler_params=_params(("parallel",)))(place, mine, owned, owned, owned)


def sum_devices(gathered):
    _, rows, width = gathered.shape

    def body(g_ref, out_ref):
        acc = g_ref[0]
        for dev in range(1, 8):
            acc = acc + g_ref[dev]
        out_ref[...] = acc
        tail = acc[SUBLANES:]
        out_ref[SUBLANES:, :] = jnp.broadcast_to(jnp.sum(tail, axis=1, keepdims=True), tail.shape)

    return _call(body, name="sum_devices",
                 in_specs=[pl.BlockSpec(memory_space=pltpu.VMEM)],
                 out_specs=pl.BlockSpec(memory_space=pltpu.VMEM),
                 out_shape=jax.ShapeDtypeStruct((rows, width), F32))(gathered)


def _rope_tables(s):
    half = ROT_DIM // 2
    inv_freq = jnp.power(jnp.float32(ROPE_THETA), -jnp.arange(half, dtype=F32) * 2.0 / ROT_DIM)
    freq64 = jnp.concatenate([inv_freq, inv_freq, jnp.zeros((HEAD_DIM - ROT_DIM,), F32)])
    freq = jnp.concatenate([freq64, freq64])[None, :]
    dim = (jnp.arange(LANES) % HEAD_DIM)[None, :]
    ang = jnp.arange(s).astype(F32)[:, None] * freq
    cos, sin = jnp.cos(ang), jnp.sin(ang)
    return cos, jnp.where(dim < half, -sin, 0.0), jnp.where((dim >= half) & (dim < ROT_DIM), sin, 0.0)


def _pad_rows(a, rows):
    return jnp.pad(a, ((0, rows - a.shape[0]), (0, 0)))


def _pad_cols(a, cols):
    return jnp.pad(a, ((0, 0), (0, cols - a.shape[1])))


def kernel(x, p, norm_gain, w_in, q_norm_gain, k_norm_gain, attn_sinks, conv_w, w_out, ple_gate_norm_gain, w_ple_gate, b_ple_gate, w_ple_proj, ple_norm_gain, loss_target, m_norm_gain, m_w_in, m_q_norm_gain, m_k_norm_gain, m_attn_sinks, m_conv_w, m_w_out, m_ple_gate_norm_gain, m_w_ple_gate, m_b_ple_gate, m_w_ple_proj, m_ple_norm_gain, v_norm_gain, v_w_in, v_q_norm_gain, v_k_norm_gain, v_attn_sinks, v_conv_w, v_w_out, v_ple_gate_norm_gain, v_w_ple_gate, v_b_ple_gate, v_w_ple_proj, v_ple_norm_gain):
    x2, p2, tgt = x[0], p[0, 0], loss_target[0]
    s, d = x2.shape
    ple = p2.shape[1]
    aw = d // 2
    cw = d - aw
    nq = aw // HEAD_DIM
    sh = w_in.shape[2]
    in_w = N_CHIPS * sh
    dq = d // N_CHIPS
    cq = cw // N_CHIPS
    ga_off = (aw + 2 * KV_WIDTH) // COLT
    b_off = (2 * aw + 2 * KV_WIDTH) // COLT
    assert in_w == 2 * aw + 2 * KV_WIDTH + 4 * cw and aw % COLT == 0 and cw % COLT == 0
    assert s % BLOCK == 0 and sh % LANES == 0 and nq % (2 * N_KV_HEADS) == 0

    core = lax.axis_index("c").astype(jnp.int32).reshape(1)
    chip = 2 * lax.axis_index("x") + lax.axis_index("y")

    chip1 = chip.astype(jnp.int32).reshape(1)
    place = jnp.concatenate([chip1, core])
    w_in_own = cast_bf16(w_in[0], "cast_w_in")
    rest = [cast_into_slot(w_out[0], chip1, "cast_w_out"), cast_into_slot(w_ple_gate[0], chip1, "cast_w_pg"),
            cast_into_slot(w_ple_proj[0], chip1, "cast_w_pp"),
            lax.dynamic_update_slice(jnp.zeros((N_CHIPS, SUBLANES, cq), F32),
                                     _pad_rows(conv_w[0], SUBLANES)[None], (chip, 0, 0))]
    order = jnp.stack([chip, chip ^ 2, chip ^ 1, chip ^ 3]).astype(jnp.int32)
    wi_sems, wi_arrs, wi_token = exchange_start(
        "gather_wi_start", [(plan_shard_ici(j), [0, 1 + j], 1) for j in range(3)],
        [w_in_own] + [lax.empty((d, sh), BF16) for _ in range(3)])
    rest_sems, rest, rest_token = exchange_start(
        "gather_rest_start", [(plan_gather_ici(3), [0, 1, 2, 3], 12)], rest, after=(wi_token,))

    tm = _tile(s, 1024)
    tn = _tile(d, 1024)
    tk = _tile(d, 2048)
    ts = _tile(s, 2048)
    h = rms_fwd(x2, norm_gain, "rms_fwd_x", after=(rest_token,))
    tabs = _rope_tables(s)
    qg = jnp.tile(q_norm_gain, (1, LANES // HEAD_DIM))
    kg = jnp.tile(k_norm_gain, (1, LANES // HEAD_DIM))
    seg_i = jnp.arange(SEG) // HEAD_DIM
    segb = (seg_i[:, None] == seg_i[None, :]).astype(BF16)
    z = in_proj_part(h, wi_arrs[0], None, order, 0, in_w)
    w_shards = [wi_arrs[0]]
    for j in range(3):
        w_shards[0], land = exchange_wait("gather_wi_wait%d" % j, plan_shard_ici(j), wi_sems[j],
                                          [w_shards[0], wi_arrs[1 + j]], z)
        land, = exchange("gather_wi_pass%d" % j, plan_shard_pass, [land], 1)
        z = in_proj_part(h, land, z, order, 1 + j, in_w)
        w_shards.append(land)
    wo_all, wg_all, wp_all, conv_all = exchange_wait("gather_rest_wait", plan_gather_ici(3), rest_sems[0], rest, z)
    pass_sems, passed, pass_token = exchange_start(
        "gather_rest_pass_start", [(plan_gather_pass, [0, 1, 2], 9)], [wo_all, wg_all, wp_all])
    conv_full = conv_all.transpose(1, 0, 2).reshape(SUBLANES, cw)
    qs, ks, vb = qk_prep_fwd(z, tabs, qg, kg, segb, aw, after=(pass_token,))
    attn, mix = attn_fwd(qs, ks, vb, z, attn_sinks, aw, d, ga_off)
    mix = conv_fwd(z, conv_full, mix, aw, cw, b_off)
    wo_all, wg_all, wp_all = exchange_wait("gather_rest_pass_wait", plan_gather_pass, pass_sems[0], passed, mix)
    wo_full = wo_all.reshape(d, d)
    wg_full = wg_all.reshape(d, d)
    sq = lambda shape: dict(
        a_spec=pl.BlockSpec((tm, tk), lambda i, j, k: (i, k)),
        o_spec=pl.BlockSpec((tm, tn), lambda i, j, k: (i, j)),
        out_shape=jax.ShapeDtypeStruct(shape, F32))
    x1 = matmul(mix, wo_full, grid=(s // tm, d // tn, d // tk),
                b_spec=pl.BlockSpec((tk, tn), lambda i, j, k: (k, j)), dims=NN, name="mm_x1",
                res=x2, res_spec=pl.BlockSpec((tm, tn), lambda i, j, k: (i, j)), **sq((s, d)))
    hg = rms_fwd(x1, ple_gate_norm_gain, "rms_fwd_x1")
    gl = matmul(hg, wg_full, grid=(s // tm, d // tn, d // tk),
                b_spec=pl.BlockSpec((tk, tn), lambda i, j, k: (k, j)), dims=NN, name="mm_gl", **sq((s, d)))
    pq = d // N_CHIPS

    d_gl, d_pe, dy, acc_head = head_fwd_bwd(x1, gl, p2, wp_all, tgt, b_ple_gate, ple_norm_gain)
    d_hg = matmul(d_gl, wg_full, grid=(s // tm, d // tn, d // tk),
                  b_spec=pl.BlockSpec((tn, tk), lambda i, j, k: (j, k)), dims=NT, name="mm_d_hg", **sq((s, d)))
    wgrad = lambda a_cols, shape3, o_spec, b_cols, name, a, b, grid: matmul(
        a, b, grid=grid,
        a_spec=pl.BlockSpec((ts, a_cols), lambda i, j, k: (k, i)),
        b_spec=pl.BlockSpec((ts, b_cols), lambda i, j, k: (k, j)),
        o_spec=o_spec, out_shape=jax.ShapeDtypeStruct(shape3, BF16), dims=TN, name=name)
    g_wg = wgrad(tn, (d, d), pl.BlockSpec((tn, tn), lambda i, j, k: (i, j)), tn, "mm_g_wg", hg, d_gl,
                 (d // tn, d // tn, s // ts))
    g_wp = wgrad(ple, (N_CHIPS, ple, pq), pl.BlockSpec((None, ple, pq), lambda i, j, k: (j, 0, 0)), pq,
                 "mm_g_wp", p2, d_pe, (1, N_CHIPS, s // ts))
    d_x1, d_x1b, acc_g = rms_bwd_add(d_hg, x1, dy, ple_gate_norm_gain, "rms_bwd_x1", True)
    d_mix = matmul(d_x1b, wo_full, grid=(s // tm, d // tn, d // tk),
                   b_spec=pl.BlockSpec((tn, tk), lambda i, j, k: (j, k)), dims=NT, name="mm_d_mix", **sq((s, d)))
    g_wo = wgrad(tn, (d, d), pl.BlockSpec((tn, tn), lambda i, j, k: (i, j)), tn, "mm_g_wo", mix, d_x1b,
                 (d // tn, d // tn, s // ts))
    def reduce_start(tag, grads):
        n = len(grads)
        lands = [lax.empty((N_CHIPS, a.shape[1] // 2, a.shape[2]), BF16) for a in grads]
        swapped = exchange("swap_" + tag, plan_swap, list(grads) + lands, n)
        parts = [add_sibling(g, o, core, "add_sibling_%s%d" % (tag, i))
                 for i, (g, o) in enumerate(zip(swapped[:n], swapped[n:]))]
        return exchange_start("scatter_%s_start" % tag, [(plan_scatter, list(range(2 * n)), 3 * n)],
                              parts + [lax.empty(a.shape, BF16) for a in parts])

    def reduce_sum(tag, handle, after):
        sems, arrays, _ = handle
        n = len(arrays) // 2
        got = exchange_wait("scatter_%s_wait" % tag, plan_scatter, sems[0], arrays, after)
        return [sum_chips(pt, o, place, "sum_chips_%s%d" % (tag, i))
                for i, (pt, o) in enumerate(zip(got[:n], got[n:]))]

    early = reduce_start("early", [g_wo.reshape(N_CHIPS, dq, d), g_wg.reshape(N_CHIPS, dq, d), g_wp])
    d_o, dz = gate_bwd(d_mix, attn, z, aw, ga_off, after=(early[-1],))
    dqs, dks, dvs, acc_sink = attn_bwd(qs, ks, vb, d_o, attn_sinks, aw)
    dz, acc_qk = qk_prep_bwd(z, dqs, dks, dvs, tabs, qg, kg, segb, dz, aw)
    dz, acc_conv = conv_bwd(z, d_mix, conv_full, dz, aw, cw, b_off)
    join_sems, joining, join_token = exchange_start(
        "join_early_start", [(plan_join, [0, 1, 2], 3)], reduce_sum("early", early, dz))
    g_wi = matmul(h, dz, grid=(d // tn, N_CHIPS, 1),
                  a_spec=pl.BlockSpec((s, tn), lambda i, j, k: (0, i), pipeline_mode=pl.Buffered(1)),
                  b_spec=pl.BlockSpec((s, sh), lambda i, j, k: (0, j)),
                  o_spec=pl.BlockSpec((None, tn, sh), lambda i, j, k: (j, i, 0)),
                  out_shape=jax.ShapeDtypeStruct((N_CHIPS, d, sh), BF16), dims=TN, name="mm_g_wi",
                  after=(join_token,), vmem=VMEM_LIMIT_BIG)
    full_wo, full_wg, full_wp = exchange_wait("join_early_wait", plan_join, join_sems[0], joining, g_wi)
    late = reduce_start("late", [g_wi])
    d_h = in_proj_bwd(dz, w_shards, order, d, after=(late[-1],))
    grad_x, acc_x = rms_bwd_add(d_h, x2, d_x1, norm_gain, "rms_bwd_x", False)

    wsm = max(d, cw)
    misc = jnp.concatenate([acc_qk[0:1, :HEAD_DIM], acc_qk[1:2, :HEAD_DIM], acc_sink[0:1, :nq]], axis=1)
    small = jnp.concatenate([
        _pad_cols(acc_x[0:1], wsm), _pad_cols(acc_g[0:1], wsm), _pad_cols(acc_head[0:1], wsm),
        _pad_cols(acc_head[1:2], wsm), _pad_cols(misc, wsm), _pad_cols(acc_conv[0:3], wsm),
        _pad_rows(_pad_cols(acc_head[2:3], wsm), SUBLANES)], axis=0)
    device = 2 * chip + lax.axis_index("c")
    small_sems, small_bufs, small_token = exchange_start(
        "small_start", [(plan_allgather_small, [0], 7)],
        [lax.dynamic_update_slice(jnp.zeros((8,) + small.shape, F32), small[None], (device, 0, 0))])
    last_sems, last_halves, last_token = exchange_start(
        "join_late_start", [(plan_join, [0], 1)], reduce_sum("late", late, small_token))
    big, after = {}, (last_token,)
    for nm, g, w, m, v in (("w_out", full_wo, w_out, m_w_out, v_w_out),
                           ("w_ple_gate", full_wg, w_ple_gate, m_w_ple_gate, v_w_ple_gate),
                           ("w_ple_proj", full_wp, w_ple_proj, m_w_ple_proj, v_w_ple_proj)):
        stepped = adamw(g, w[0], m[0], v[0], "adamw_" + nm, after=after)
        big[nm], after = [o[None] for o in stepped], (stepped[1],)
    full_wi, = exchange_wait("join_late_wait", plan_join, last_sems[0], last_halves, after[0])
    big["w_in"] = [o[None] for o in adamw(full_wi, w_in[0], m_w_in[0], v_w_in[0], "adamw_w_in")]

    gathered, = exchange_wait("small_wait", plan_allgather_small, small_sems[0], small_bufs, big["w_in"][1])
    tot = sum_devices(gathered)
    loss = tot[SUBLANES, 0]

    def pack(vals):
        ng, pg, bg, eg, qgv, kgv, sk, cv = vals
        misc_v = jnp.concatenate([qgv, kgv, sk], axis=1)
        return jnp.concatenate([_pad_cols(ng, wsm), _pad_cols(pg, wsm), _pad_cols(bg, wsm), _pad_cols(eg, wsm),
                                _pad_cols(misc_v, wsm), _pad_cols(cv[0], wsm)], axis=0)

    g_small = jnp.concatenate(
        [tot[0:5], _pad_cols(lax.dynamic_slice(tot[5:8], (0, chip * cq), (3, cq)), wsm)], axis=0)
    w_small = pack((norm_gain, ple_gate_norm_gain, b_ple_gate, ple_norm_gain, q_norm_gain, k_norm_gain,
                    attn_sinks, conv_w))
    m_small = pack((m_norm_gain, m_ple_gate_norm_gain, m_b_ple_gate, m_ple_norm_gain, m_q_norm_gain,
                    m_k_norm_gain, m_attn_sinks, m_conv_w))
    v_small = pack((v_norm_gain, v_ple_gate_norm_gain, v_b_ple_gate, v_ple_norm_gain, v_q_norm_gain,
                    v_k_norm_gain, v_attn_sinks, v_conv_w))
    sm = adamw(g_small, w_small, m_small, v_small, "adamw_small")

    def unpack(a):
        return {"norm_gain": a[0:1, :d], "ple_gate_norm_gain": a[1:2, :d], "b_ple_gate": a[2:3, :d],
                "ple_norm_gain": a[3:4, :d], "q_norm_gain": a[4:5, :HEAD_DIM],
                "k_norm_gain": a[4:5, HEAD_DIM:2 * HEAD_DIM],
                "attn_sinks": a[4:5, 2 * HEAD_DIM:2 * HEAD_DIM + nq], "conv_w": a[5:8, :cq][None]}

    order = ("norm_gain", "w_in", "q_norm_gain", "k_norm_gain", "attn_sinks", "conv_w", "w_out",
             "ple_gate_norm_gain", "w_ple_gate", "b_ple_gate", "w_ple_proj", "ple_norm_gain")
    outs = [loss, grad_x[None]]
    for kind in range(4):
        table = unpack(sm[kind])
        for nm in order:
            outs.append(big[nm][kind] if nm in big else table[nm])
    return tuple(outs)
```

```python
import functools

import jax
import jax.numpy as jnp
from jax import lax
from jax.experimental import pallas as pl
from jax.experimental.pallas import tpu as pltpu

F32 = jnp.float32
BF16 = jnp.bfloat16
MESH = pl.DeviceIdType.MESH

HEAD_DIM = 64
N_KV_HEADS = 4
KV_WIDTH = N_KV_HEADS * HEAD_DIM
BLOCK = 128
ROT_DIM = 16
ROPE_THETA = 500000.0
EPS = 1e-6
NEG_INF = -1e30
N_CHIPS = 4
LANES = 128
SUBLANES = 8
COLT = 512
SEG = 256
VMEM_LIMIT = 48 * 1024 * 1024
VMEM_LIMIT_BIG = 60 * 1024 * 1024

ADAM_LR = 0.001
ADAM_B1 = 0.9
ADAM_B2 = 0.999
ADAM_EPS = 1e-08
ADAM_WD = 0.01
ADAM_STEP = 10

NN = (((1,), (0,)), ((), ()))
NT = (((1,), (1,)), ((), ()))
TN = (((0,), (0,)), ((), ()))


def _call(body, **kw):
    return pl.pallas_call(body, **kw)


def _call_after(body, after, **kw):
    n_in, n_after = len(kw["in_specs"]), len(after)
    kw["in_specs"] = list(kw["in_specs"]) + [pl.BlockSpec(memory_space=pl.ANY)] * n_after

    def body_after(*refs):
        body(*refs[:n_in], *refs[n_in + n_after:])

    call = _call(body_after, **kw)
    return lambda *args: call(*args, *after)


def _params(sem, vmem=VMEM_LIMIT):
    return pltpu.CompilerParams(dimension_semantics=sem, vmem_limit_bytes=vmem)


def _tile(n, pref):
    return pref if n % pref == 0 else n


def _sigmoid(v):
    return 1.0 / (1.0 + jnp.exp(-v))


def _hbm():
    return pl.BlockSpec(memory_space=pl.ANY)


def cast_bf16(a, name):
    r, c = a.shape
    tr = _tile(r, 512)

    def body(a_ref, o_ref):
        o_ref[...] = a_ref[...].astype(BF16)

    return _call(body, name=name, grid=(r // tr,),
                 in_specs=[pl.BlockSpec((tr, c), lambda i: (i, 0))],
                 out_specs=pl.BlockSpec((tr, c), lambda i: (i, 0)),
                 out_shape=jax.ShapeDtypeStruct((r, c), BF16),
                 compiler_params=_params(("parallel",)))(a)


def cast_into_slot(a, chip, name):
    r, c = a.shape
    tr = _tile(r, 512)

    def body(chip_ref, a_ref, o_ref):
        o_ref[...] = a_ref[...].astype(BF16)

    grid_spec = pltpu.PrefetchScalarGridSpec(
        num_scalar_prefetch=1, grid=(r // tr,),
        in_specs=[pl.BlockSpec((tr, c), lambda i, chip_ref: (i, 0))],
        out_specs=pl.BlockSpec((None, tr, c), lambda i, chip_ref: (chip_ref[0], i, 0)))
    return _call(body, name=name, grid_spec=grid_spec,
                 out_shape=jax.ShapeDtypeStruct((N_CHIPS, r, c), BF16),
                 compiler_params=_params(("parallel",)))(chip, a)


def rms_fwd(x, gain, name, after=()):
    s, d = x.shape
    tm = _tile(s, 512)

    def body(x_ref, g_ref, o_ref):
        xf = x_ref[...]
        r = lax.rsqrt(jnp.mean(xf * xf, axis=-1, keepdims=True) + EPS)
        o_ref[...] = ((xf * r) * g_ref[...]).astype(BF16)

    return _call_after(body, after, name=name, grid=(s // tm,),
                       in_specs=[pl.BlockSpec((tm, d), lambda i: (i, 0)),
                                 pl.BlockSpec((1, d), lambda i: (0, 0))],
                       out_specs=pl.BlockSpec((tm, d), lambda i: (i, 0)),
                       out_shape=jax.ShapeDtypeStruct((s, d), BF16),
                       compiler_params=_params(("parallel",)))(x, gain)


def rms_bwd_add(dyn, xin, add, gain, name, want_bf16):
    s, d = xin.shape
    tm = _tile(s, 512)

    def body(dy_ref, x_ref, a_ref, g_ref, *outs):
        i = pl.program_id(0)
        dx_ref, acc_ref = outs[0], outs[-1]
        xf = x_ref[...]
        r = lax.rsqrt(jnp.mean(xf * xf, axis=-1, keepdims=True) + EPS)
        xhat = xf * r
        dyv = dy_ref[...]
        gd = dyv * g_ref[...]
        dx = a_ref[...] + r * (gd - xhat * jnp.mean(xhat * gd, axis=-1, keepdims=True))
        dx_ref[...] = dx
        if want_bf16:
            outs[1][...] = dx.astype(BF16)

        @pl.when(i == 0)
        def _():
            acc_ref[...] = jnp.zeros_like(acc_ref)

        acc_ref[0:1, :] += jnp.sum(dyv * xhat, axis=0, keepdims=True)

    row = pl.BlockSpec((tm, d), lambda i: (i, 0))
    out_specs = [row] + ([row] if want_bf16 else []) + [pl.BlockSpec((SUBLANES, d), lambda i: (0, 0))]
    out_shape = ([jax.ShapeDtypeStruct((s, d), F32)]
                 + ([jax.ShapeDtypeStruct((s, d), BF16)] if want_bf16 else [])
                 + [jax.ShapeDtypeStruct((SUBLANES, d), F32)])
    return _call(body, name=name, grid=(s // tm,),
                 in_specs=[row, row, row, pl.BlockSpec((1, d), lambda i: (0, 0))],
                 out_specs=out_specs, out_shape=out_shape,
                 compiler_params=_params(("arbitrary",), VMEM_LIMIT_BIG))(dyn, xin, add, gain)


def head_fwd_bwd(x1, gl, p, wp, tgt, bias, ple_gain):
    s, d = x1.shape
    tm = _tile(s, 256)

    def body(x1_ref, gl_ref, p_ref, wp_ref, t_ref, b_ref, g_ref, dgl_ref, dpe_ref, dy_ref, acc_ref):
        i = pl.program_id(0)
        gate = _sigmoid(gl_ref[...] + b_ref[...])
        pb = p_ref[...].astype(BF16)
        pev = jnp.concatenate([jnp.dot(pb, wp_ref[q], preferred_element_type=F32) for q in range(N_CHIPS)],
                              axis=1)
        r = lax.rsqrt(jnp.mean(pev * pev, axis=-1, keepdims=True) + EPS)
        pehat = pev * r
        gain = g_ref[...]
        e = pehat * gain
        diff = (x1_ref[...] + gate * e) - t_ref[...]
        dy = diff * (1.0 / d)
        dy_ref[...] = dy
        d_gl = (dy * e) * (gate * (1.0 - gate))
        dgl_ref[...] = d_gl.astype(BF16)
        d_e = dy * gate
        gd = d_e * gain
        d_pe = r * (gd - pehat * jnp.mean(pehat * gd, axis=-1, keepdims=True))
        dpe_ref[...] = d_pe.astype(BF16)

        @pl.when(i == 0)
        def _():
            acc_ref[...] = jnp.zeros_like(acc_ref)

        acc_ref[0:1, :] += jnp.sum(d_gl, axis=0, keepdims=True)
        acc_ref[1:2, :] += jnp.sum(d_e * pehat, axis=0, keepdims=True)
        acc_ref[2:3, :] += jnp.sum(diff * diff, axis=0, keepdims=True) * (0.5 / d)

    row = pl.BlockSpec((tm, d), lambda i: (i, 0))
    vec = pl.BlockSpec((1, d), lambda i: (0, 0))
    return _call(body, name="head_fwd_bwd", grid=(s // tm,),
                 in_specs=[row, row, pl.BlockSpec((tm, p.shape[1]), lambda i: (i, 0)),
                           pl.BlockSpec(wp.shape, lambda i: (0, 0, 0)), row, vec, vec],
                 out_specs=[row, row, row, pl.BlockSpec((SUBLANES, d), lambda i: (0, 0))],
                 out_shape=[jax.ShapeDtypeStruct((s, d), BF16), jax.ShapeDtypeStruct((s, d), BF16),
                            jax.ShapeDtypeStruct((s, d), F32), jax.ShapeDtypeStruct((SUBLANES, d), F32)],
                 compiler_params=_params(("arbitrary",)))(x1, gl, p, wp, tgt, bias, ple_gain)


def adamw(g, w, m, v, name, after=()):
    r, c = g.shape
    tr = _tile(r, 256)

    def body(g_ref, w_ref, m_ref, v_ref, go_ref, d_ref, mo_ref, vo_ref):
        gv = g_ref[...]
        mn = ADAM_B1 * m_ref[...] + (1.0 - ADAM_B1) * gv
        vn = ADAM_B2 * v_ref[...] + (1.0 - ADAM_B2) * (gv * gv)
        m_hat = mn / (1.0 - ADAM_B1 ** ADAM_STEP)
        v_hat = vn / (1.0 - ADAM_B2 ** ADAM_STEP)
        go_ref[...] = gv
        d_ref[...] = -ADAM_LR * (m_hat / (jnp.sqrt(v_hat) + ADAM_EPS) + ADAM_WD * w_ref[...])
        mo_ref[...] = mn
        vo_ref[...] = vn

    blk = pl.BlockSpec((tr, c), lambda i: (i, 0))
    shp = jax.ShapeDtypeStruct((r, c), F32)
    return _call_after(body, after, name=name, grid=(r // tr,), in_specs=[blk] * 4, out_specs=[blk] * 4,
                       out_shape=[shp] * 4, compiler_params=_params(("parallel",)))(g, w, m, v)


def matmul(a, b, *, grid, a_spec, b_spec, o_spec, out_shape, dims, name, res=None, res_spec=None, after=(),
           vmem=VMEM_LIMIT):
    nk = grid[2]
    acc_shape = tuple(d for d in o_spec.block_shape if d is not None)

    def body(*refs):
        a_ref, b_ref = refs[:2]
        r_ref = refs[2] if res is not None else None
        o_ref = refs[3] if res is not None else refs[2]
        part = lax.dot_general(a_ref[...].astype(BF16), b_ref[...].astype(BF16), dims, preferred_element_type=F32)

        def finish(out):
            if res is not None:
                out = r_ref[...] + out
            o_ref[...] = out.astype(o_ref.dtype)

        if nk == 1:
            finish(part)
            return
        acc_ref = refs[-1]
        k = pl.program_id(2)

        @pl.when(k == 0)
        def _():
            acc_ref[...] = part

        @pl.when((k > 0) & (k < nk - 1))
        def _():
            acc_ref[...] += part

        @pl.when(k == nk - 1)
        def _():
            finish(acc_ref[...] + part)

    in_specs = [a_spec, b_spec] + ([res_spec] if res is not None else [])
    args = (a, b) + ((res,) if res is not None else ())
    return _call_after(body, after, name=name, grid=grid, in_specs=in_specs, out_specs=o_spec,
                       out_shape=out_shape, scratch_shapes=[pltpu.VMEM(acc_shape, F32)] if nk > 1 else [],
                       compiler_params=_params(("parallel", "parallel", "arbitrary"), vmem))(*args)


def in_proj_part(h, w, z_prev, order, q, in_w):
    s, d = h.shape
    sh = w.shape[1]
    tm = _tile(s, 512)

    def body(order_ref, h_ref, w_ref, *rest):
        rest[-1][...] = jnp.dot(h_ref[...], w_ref[...], preferred_element_type=F32)

    in_specs = [pl.BlockSpec((tm, d), lambda i, order_ref: (i, 0)),
                pl.BlockSpec((d, sh), lambda i, order_ref: (0, 0))]
    args = [order, h, w]
    if z_prev is not None:
        in_specs.append(_hbm())
        args.append(z_prev)
    grid_spec = pltpu.PrefetchScalarGridSpec(
        num_scalar_prefetch=1, grid=(s // tm,), in_specs=in_specs,
        out_specs=pl.BlockSpec((tm, sh), lambda i, order_ref: (i, order_ref[q])))
    return _call(body, name="mm_z%d" % q, grid_spec=grid_spec,
                 out_shape=jax.ShapeDtypeStruct((s, in_w), F32),
                 input_output_aliases={3: 0} if z_prev is not None else {},
                 compiler_params=_params(("parallel",)))(*args)


def in_proj_bwd(dz, ws, order, d, after):
    s = dz.shape[0]
    sh = ws[0].shape[1]
    tm, tn = _tile(s, 512), _tile(d, 1024)
    nq, n_after = len(ws), len(after)

    def body(order_ref, *refs):
        a_refs, b_refs, o_ref = refs[:nq], refs[nq:2 * nq], refs[2 * nq + n_after]
        acc = lax.dot_general(a_refs[0][...], b_refs[0][...], NT, preferred_element_type=F32)
        for q in range(1, nq):
            acc += lax.dot_general(a_refs[q][...], b_refs[q][...], NT, preferred_element_type=F32)
        o_ref[...] = acc

    a_spec = lambda q: pl.BlockSpec((tm, sh), functools.partial(lambda i, j, order_ref, q: (i, order_ref[q]), q=q))
    grid_spec = pltpu.PrefetchScalarGridSpec(
        num_scalar_prefetch=1, grid=(s // tm, d // tn),
        in_specs=[a_spec(q) for q in range(nq)]
        + [pl.BlockSpec((tn, sh), lambda i, j, order_ref: (j, 0))] * nq + [_hbm()] * n_after,
        out_specs=pl.BlockSpec((tm, tn), lambda i, j, order_ref: (i, j)))
    return _call(body, name="mm_d_h", grid_spec=grid_spec, out_shape=jax.ShapeDtypeStruct((s, d), F32),
                 compiler_params=_params(("parallel", "parallel"), VMEM_LIMIT_BIG))(
                     order, *([dz] * nq), *ws, *after)


def _seg_mean(sq, segb):
    parts = []
    for cgrp in range(sq.shape[1] // SEG):
        blk = sq[:, cgrp * SEG:(cgrp + 1) * SEG]
        hi = blk.astype(BF16)
        r1 = blk - hi.astype(F32)
        mid = r1.astype(BF16)
        lo = (r1 - mid.astype(F32)).astype(BF16)
        acc = jnp.dot(hi, segb, preferred_element_type=F32)
        acc += jnp.dot(mid, segb, preferred_element_type=F32)
        acc += jnp.dot(lo, segb, preferred_element_type=F32)
        parts.append(acc)
    out = parts[0] if len(parts) == 1 else jnp.concatenate(parts, axis=1)
    return out * (1.0 / HEAD_DIM)


def _rope(v, cos, sa, sb):
    parts = []
    for cgrp in range(v.shape[1] // LANES):
        blk = v[:, cgrp * LANES:(cgrp + 1) * LANES]
        parts.append(blk * cos + pltpu.roll(blk, LANES - 8, 1) * sa + pltpu.roll(blk, 8, 1) * sb)
    return parts[0] if len(parts) == 1 else jnp.concatenate(parts, axis=1)


def _rope_t(dv, cos, sa, sb):
    parts = []
    for cgrp in range(dv.shape[1] // LANES):
        blk = dv[:, cgrp * LANES:(cgrp + 1) * LANES]
        parts.append(blk * cos + pltpu.roll(blk * sa, 8, 1) + pltpu.roll(blk * sb, LANES - 8, 1))
    return parts[0] if len(parts) == 1 else jnp.concatenate(parts, axis=1)


def _tile_lanes(vec, width):
    reps = width // LANES
    return vec if reps == 1 else jnp.tile(vec, (1, reps))


def qk_prep_fwd(z, tabs, qg, kg, segb, aw, after=()):
    s = z.shape[0]
    tm = _tile(s, 512)
    wq = aw + 2 * KV_WIDTH

    def body(z_ref, cos_ref, sa_ref, sb_ref, qg_ref, kg_ref, seg_ref, qs_ref, ks_ref, vb_ref):
        zz = z_ref[...]
        q, k, v = zz[:, :aw], zz[:, aw:aw + KV_WIDTH], zz[:, aw + KV_WIDTH:]
        cos, sa, sb, segm = cos_ref[...], sa_ref[...], sb_ref[...], seg_ref[...]
        rq = lax.rsqrt(_seg_mean(q * q, segm) + EPS)
        qn = (q * rq) * _tile_lanes(qg_ref[...], aw)
        qs_ref[...] = (_rope(qn, cos, sa, sb) * (HEAD_DIM ** -0.5)).astype(BF16)
        rk = lax.rsqrt(_seg_mean(k * k, segm) + EPS)
        kn = (k * rk) * _tile_lanes(kg_ref[...], KV_WIDTH)
        ks_ref[...] = _rope(kn, cos, sa, sb).astype(BF16)
        vb_ref[...] = v.astype(BF16)

    tab = pl.BlockSpec((tm, LANES), lambda i: (i, 0))
    vec = pl.BlockSpec((1, LANES), lambda i: (0, 0))
    return _call_after(body, after, name="qk_prep_fwd", grid=(s // tm,),
                       in_specs=[pl.BlockSpec((tm, wq), lambda i: (i, 0)), tab, tab, tab, vec, vec,
                                 pl.BlockSpec((SEG, SEG), lambda i: (0, 0))],
                       out_specs=[pl.BlockSpec((tm, aw), lambda i: (i, 0)),
                                  pl.BlockSpec((tm, KV_WIDTH), lambda i: (i, 0)),
                                  pl.BlockSpec((tm, KV_WIDTH), lambda i: (i, 0))],
                       out_shape=[jax.ShapeDtypeStruct((s, aw), BF16), jax.ShapeDtypeStruct((s, KV_WIDTH), BF16),
                                  jax.ShapeDtypeStruct((s, KV_WIDTH), BF16)],
                       compiler_params=_params(("parallel",)))(z, *tabs, qg, kg, segb)


def qk_prep_bwd(z, dqs, dks, dvs, tabs, qg, kg, segb, dz, aw):
    s = z.shape[0]
    tm = _tile(s, 512)
    wq = aw + 2 * KV_WIDTH

    def body(z_ref, dq_ref, dk_ref, dv_ref, cos_ref, sa_ref, sb_ref, qg_ref, kg_ref, seg_ref, dz_in,
             dz_ref, acc_ref):
        i = pl.program_id(0)
        zz = z_ref[...]
        q, k = zz[:, :aw], zz[:, aw:aw + KV_WIDTH]
        cos, sa, sb, segm = cos_ref[...], sa_ref[...], sb_ref[...], seg_ref[...]

        def one(xv, dout, gvec, width):
            g = _tile_lanes(gvec, width)
            r = lax.rsqrt(_seg_mean(xv * xv, segm) + EPS)
            xhat = xv * r
            dn = _rope_t(dout, cos, sa, sb)
            gd = dn * g
            dx = r * (gd - xhat * _seg_mean(xhat * gd, segm))
            contrib = jnp.sum(dn * xhat, axis=0, keepdims=True)
            folded = contrib[:, :LANES]
            for cgrp in range(1, width // LANES):
                folded = folded + contrib[:, cgrp * LANES:(cgrp + 1) * LANES]
            return dx, folded + pltpu.roll(folded, HEAD_DIM, 1)

        dq, gq = one(q, dq_ref[...] * (HEAD_DIM ** -0.5), qg_ref[...], aw)
        dk, gk = one(k, dk_ref[...], kg_ref[...], KV_WIDTH)
        dz_ref[:, :aw] = dq.astype(BF16)
        dz_ref[:, aw:aw + KV_WIDTH] = dk.astype(BF16)
        dz_ref[:, aw + KV_WIDTH:] = dv_ref[...].astype(BF16)

        @pl.when(i == 0)
        def _():
            acc_ref[...] = jnp.zeros_like(acc_ref)

        acc_ref[0:1, :] += gq
        acc_ref[1:2, :] += gk

    tab = pl.BlockSpec((tm, LANES), lambda i: (i, 0))
    vec = pl.BlockSpec((1, LANES), lambda i: (0, 0))
    kvb = pl.BlockSpec((tm, KV_WIDTH), lambda i: (i, 0))
    return _call(body, name="qk_prep_bwd", grid=(s // tm,),
                 in_specs=[pl.BlockSpec((tm, wq), lambda i: (i, 0)),
                           pl.BlockSpec((tm, aw), lambda i: (i, 0)), kvb, kvb, tab, tab, tab, vec, vec,
                           pl.BlockSpec((SEG, SEG), lambda i: (0, 0)), _hbm()],
                 out_specs=[pl.BlockSpec((tm, wq), lambda i: (i, 0)),
                            pl.BlockSpec((SUBLANES, LANES), lambda i: (0, 0))],
                 out_shape=[jax.ShapeDtypeStruct(dz.shape, BF16), jax.ShapeDtypeStruct((SUBLANES, LANES), F32)],
                 input_output_aliases={10: 0},
                 compiler_params=_params(("arbitrary",)))(z, dqs, dks, dvs, *tabs, qg, kg, segb, dz)


def _placed(band, lane_idx):
    out = {}
    for kh in range(N_KV_HEADS):
        grp = band[:, (kh // 2) * LANES:(kh // 2 + 1) * LANES]
        for half in (0, 1):
            t = grp if half == kh % 2 else pltpu.roll(grp, HEAD_DIM, 1)
            keep = (lane_idx >= half * HEAD_DIM) & (lane_idx < (half + 1) * HEAD_DIM)
            out[kh, half] = jnp.where(keep, t, jnp.zeros_like(t))
    return out


def _softmax_with_sink(sc, valid, sink):
    sc = jnp.where(valid, sc, NEG_INF)
    m = jnp.maximum(jnp.max(sc, axis=-1, keepdims=True), sink)
    e = jnp.exp(sc - m)
    es = jnp.exp(sink - m)
    den = jnp.sum(e, axis=-1, keepdims=True) + es
    return e / den, es / den


def _stack_halves(placed, kh):
    return jnp.concatenate([placed[kh, 0], placed[kh, 1]], axis=0)


def _valid_mask(n):
    r_i = lax.broadcasted_iota(jnp.int32, (BLOCK, 2 * BLOCK), 0)
    j_i = lax.broadcasted_iota(jnp.int32, (BLOCK, 2 * BLOCK), 1)
    return (j_i > r_i) & (j_i <= r_i + BLOCK) & ((n > 0) | (j_i >= BLOCK))


def attn_fwd(qs, ks, vb, z, sinks, aw, d, ga_off):
    s = qs.shape[0]
    nb = s // BLOCK
    nq = aw // HEAD_DIM
    grp_sz = nq // N_KV_HEADS
    n_ga = aw // COLT

    def body(q_ref, ko_ref, kp_ref, vo_ref, vp_ref, *rest):
        ga_refs = rest[:n_ga]
        sink_ref, attn_ref, mix_ref = rest[n_ga:]
        n = pl.program_id(0)
        lane_idx = lax.broadcasted_iota(jnp.int32, (2 * BLOCK, LANES), 1)
        kpl = _placed(jnp.concatenate([kp_ref[...], ko_ref[...]], axis=0), lane_idx)
        vpl = _placed(jnp.concatenate([vp_ref[...], vo_ref[...]], axis=0), lane_idx)
        valid = _valid_mask(n)
        groups = range(nq // 2)
        kcat = [_stack_halves(kpl, kh) for kh in range(N_KV_HEADS)]
        vcat = [_stack_halves(vpl, kh) for kh in range(N_KV_HEADS)]
        scs = [lax.dot_general(q_ref[:, c * LANES:(c + 1) * LANES], kcat[2 * c // grp_sz], NT,
                               preferred_element_type=F32) for c in groups]
        probs = [jnp.concatenate(
            [_softmax_with_sink(scs[c][:, half * 2 * BLOCK:(half + 1) * 2 * BLOCK], valid,
                                sink_ref[0, 2 * c + half])[0].astype(BF16) for half in (0, 1)], axis=1)
                 for c in groups]
        for c in groups:
            acc = jnp.dot(probs[c], vcat[2 * c // grp_sz], preferred_element_type=F32)
            attn_ref[:, c * LANES:(c + 1) * LANES] = acc
            col = c * LANES
            ga = ga_refs[col // COLT][:, col % COLT:col % COLT + LANES]
            mix_ref[:, c * LANES:(c + 1) * LANES] = (acc * (ga * _sigmoid(ga))).astype(BF16)

    own = lambda n: (n, 0)
    prev = lambda n: (jnp.maximum(n - 1, 0), 0)
    kvs = lambda imap: pl.BlockSpec((BLOCK, KV_WIDTH), imap)
    ga_specs = [pl.BlockSpec((BLOCK, COLT), functools.partial(lambda n, j: (n, ga_off + j), j=j))
                for j in range(n_ga)]
    return _call(body, name="attn_fwd", grid=(nb,),
                 in_specs=[pl.BlockSpec((BLOCK, aw), own), kvs(own), kvs(prev), kvs(own), kvs(prev)]
                 + ga_specs + [pl.BlockSpec(memory_space=pltpu.SMEM)],
                 out_specs=[pl.BlockSpec((BLOCK, aw), own), pl.BlockSpec((BLOCK, aw), own)],
                 out_shape=[jax.ShapeDtypeStruct((s, aw), F32), jax.ShapeDtypeStruct((s, d), BF16)],
                 compiler_params=_params(("parallel",)))(qs, ks, ks, vb, vb, *([z] * n_ga), sinks)


def gate_bwd(d_mix, attn, z, aw, ga_off, after=()):
    s, in_w = z.shape
    tm = _tile(s, 1024)

    def body(dm_ref, at_ref, ga_ref, do_ref, dz_ref):
        ga = ga_ref[...]
        sig = _sigmoid(ga)
        dm = dm_ref[...]
        do_ref[...] = (dm * (ga * sig)).astype(BF16)
        dz_ref[...] = ((dm * at_ref[...]) * (sig * (1.0 + ga * (1.0 - sig)))).astype(BF16)

    blk = pl.BlockSpec((tm, COLT), lambda i, j: (i, j))
    gab = pl.BlockSpec((tm, COLT), lambda i, j: (i, ga_off + j))
    return _call_after(body, after, name="gate_bwd", grid=(s // tm, aw // COLT),
                       in_specs=[blk, blk, gab], out_specs=[blk, gab],
                       out_shape=[jax.ShapeDtypeStruct((s, aw), BF16), jax.ShapeDtypeStruct((s, in_w), BF16)],
                       compiler_params=_params(("parallel", "parallel")))(d_mix, attn, z)


def attn_bwd(qs, ks, vb, d_o, sinks, aw):
    s = qs.shape[0]
    nb = s // BLOCK
    nq = aw // HEAD_DIM
    grp_sz = nq // N_KV_HEADS

    def body(q_ref, ko_ref, kp_ref, vo_ref, vp_ref, do_ref, sink_ref, dq_ref, dk_ref, dv_ref, ds_ref,
             ck_ref, cv_ref):
        n = pl.program_id(0)

        @pl.when(n == 0)
        def _():
            ds_ref[...] = jnp.zeros_like(ds_ref)
            dk_ref[...] = jnp.zeros_like(dk_ref)
            dv_ref[...] = jnp.zeros_like(dv_ref)

        @pl.when(n < nb)
        def _():
            lane_idx = lax.broadcasted_iota(jnp.int32, (2 * BLOCK, LANES), 1)
            lane_row = lax.broadcasted_iota(jnp.int32, (1, LANES), 1)
            kpl = _placed(jnp.concatenate([kp_ref[...], ko_ref[...]], axis=0), lane_idx)
            vpl = _placed(jnp.concatenate([vp_ref[...], vo_ref[...]], axis=0), lane_idx)
            valid = _valid_mask(n)
            ds_row = jnp.zeros((1, LANES), F32)
            wide = 2 * BLOCK
            groups = range(nq // 2)
            per_kv = grp_sz // 2
            kcat = [_stack_halves(kpl, kh) for kh in range(N_KV_HEADS)]
            vcat = [_stack_halves(vpl, kh) for kh in range(N_KV_HEADS)]
            qg = [q_ref[:, c * LANES:(c + 1) * LANES] for c in groups]
            dog = [do_ref[:, c * LANES:(c + 1) * LANES] for c in groups]
            scs = [lax.dot_general(qg[c], kcat[c // per_kv], NT, preferred_element_type=F32) for c in groups]
            dps = [lax.dot_general(dog[c], vcat[c // per_kv], NT, preferred_element_type=F32) for c in groups]
            ds_pairs, p_pairs = [], []
            for c in groups:
                probs, dss = [], []
                for half in (0, 1):
                    h = 2 * c + half
                    cols = slice(half * wide, (half + 1) * wide)
                    p, p_sink = _softmax_with_sink(scs[c][:, cols], valid, sink_ref[0, h])
                    delta = jnp.sum(p * dps[c][:, cols], axis=-1, keepdims=True)
                    dss.append((p * (dps[c][:, cols] - delta)).astype(BF16))
                    probs.append(p.astype(BF16))
                    dsink = -jnp.sum(p_sink * delta, axis=0, keepdims=True)
                    ds_row += jnp.where(lane_row == h, dsink, 0.0)
                ds_pairs.append(jnp.concatenate(dss, axis=1))
                p_pairs.append(jnp.concatenate(probs, axis=1))
            for c in groups:
                dq_ref[:, c * LANES:(c + 1) * LANES] = jnp.dot(ds_pairs[c], kcat[c // per_kv],
                                                               preferred_element_type=F32)
            dkts = [lax.dot_general(qg[c], ds_pairs[c], TN, preferred_element_type=F32) for c in groups]
            dvts = [lax.dot_general(dog[c], p_pairs[c], TN, preferred_element_type=F32) for c in groups]
            dkt, dvt = [], []
            for kh in range(N_KV_HEADS):
                for parts, out in ((dkts, dkt), (dvts, dvt)):
                    tot = parts[kh * per_kv]
                    for c in range(kh * per_kv + 1, (kh + 1) * per_kv):
                        tot = tot + parts[c]
                    out.append(tot[:HEAD_DIM, :wide] + tot[HEAD_DIM:, wide:])
            ds_ref[0:1, :] += ds_row
            back = lambda t: jnp.concatenate(
                [jnp.concatenate(t[2 * g:2 * g + 2], axis=0).T for g in range(N_KV_HEADS // 2)], axis=1)
            dk_band, dv_band = back(dkt), back(dvt)

            @pl.when(n > 0)
            def _():
                dk_ref[...] = ck_ref[...] + dk_band[:BLOCK]
                dv_ref[...] = cv_ref[...] + dv_band[:BLOCK]

            ck_ref[...] = dk_band[BLOCK:]
            cv_ref[...] = dv_band[BLOCK:]

        @pl.when(n == nb)
        def _():
            dk_ref[...] = ck_ref[...]
            dv_ref[...] = cv_ref[...]

    own = lambda n: (jnp.minimum(n, nb - 1), 0)
    prev = lambda n: (jnp.clip(n - 1, 0, nb - 1), 0)
    done = lambda n: (jnp.maximum(n - 1, 0), 0)
    kvs = lambda imap: pl.BlockSpec((BLOCK, KV_WIDTH), imap)
    return _call(body, name="attn_bwd", grid=(nb + 1,),
                 in_specs=[pl.BlockSpec((BLOCK, aw), own), kvs(own), kvs(prev), kvs(own), kvs(prev),
                           pl.BlockSpec((BLOCK, aw), own), pl.BlockSpec(memory_space=pltpu.SMEM)],
                 out_specs=[pl.BlockSpec((BLOCK, aw), own), kvs(done), kvs(done),
                            pl.BlockSpec((SUBLANES, LANES), lambda n: (0, 0))],
                 out_shape=[jax.ShapeDtypeStruct((s, aw), F32), jax.ShapeDtypeStruct((s, KV_WIDTH), F32),
                            jax.ShapeDtypeStruct((s, KV_WIDTH), F32), jax.ShapeDtypeStruct((SUBLANES, LANES), F32)],
                 scratch_shapes=[pltpu.VMEM((BLOCK, KV_WIDTH), F32), pltpu.VMEM((BLOCK, KV_WIDTH), F32)],
                 compiler_params=_params(("arbitrary",)))(qs, ks, ks, vb, vb, d_o, sinks)


def conv_fwd(z, conv_w, mix, aw, cw, b_off):
    s = z.shape[0]
    tm = _tile(s, 1024)
    nseg = cw // COLT
    hb = tm // SUBLANES

    def body(b_ref, c_ref, h_ref, g_ref, cp_ref, hp_ref, w_ref, mix_in, mix_ref):
        i = pl.program_id(0)
        u = c_ref[...] * h_ref[...]
        up = jnp.where(i > 0, cp_ref[...] * hp_ref[...], 0.0)
        ext = jnp.concatenate([up, u], axis=0)
        um1 = pltpu.roll(ext, 1, 0)[SUBLANES:]
        um2 = pltpu.roll(ext, 2, 0)[SUBLANES:]
        w = w_ref[...]
        cv = w[0:1] * um2 + w[1:2] * um1 + w[2:3] * u
        g = g_ref[...]
        mix_ref[...] = ((b_ref[...] * cv) * (g * _sigmoid(g))).astype(BF16)

    seg = lambda k: pl.BlockSpec((tm, COLT), functools.partial(lambda i, j, k: (i, b_off + k * nseg + j), k=k))
    halo = lambda k: pl.BlockSpec(
        (SUBLANES, COLT), functools.partial(lambda i, j, k: (jnp.maximum(i * hb - 1, 0), b_off + k * nseg + j), k=k))
    return _call(body, name="conv_fwd", grid=(s // tm, nseg),
                 in_specs=[seg(0), seg(1), seg(2), seg(3), halo(1), halo(2),
                           pl.BlockSpec((SUBLANES, COLT), lambda i, j: (0, j)), _hbm()],
                 out_specs=pl.BlockSpec((tm, COLT), lambda i, j: (i, aw // COLT + j)),
                 out_shape=jax.ShapeDtypeStruct(mix.shape, BF16),
                 input_output_aliases={7: 0},
                 compiler_params=_params(("parallel", "parallel")))(z, z, z, z, z, z, conv_w, mix)


def conv_bwd(z, d_mix, conv_w, dz, aw, cw, b_off):
    s = z.shape[0]
    tm = _tile(s, 512)
    nseg = cw // COLT
    hb = tm // SUBLANES
    n_row = s // tm
    last_h = s // SUBLANES - 1
    n_step = nseg * n_row

    def body(b_ref, c_ref, h_ref, g_ref, cp_ref, hp_ref, bn_ref, gn_ref, dm_ref, dmn_ref, w_ref, dz_in,
             dz_ref, acc_ref, stash_ref, sems):
        j = pl.program_id(0)
        i = pl.program_id(1)
        step = j * n_row + i
        slot = step % 2

        def copies(from_slot, at_step):
            jj, ii = at_step // n_row, at_step % n_row
            rows = pl.ds(pl.multiple_of(ii * tm, tm), tm)
            return [pltpu.make_async_copy(
                stash_ref.at[from_slot, q],
                dz_ref.at[rows, pl.ds(pl.multiple_of((b_off + q * nseg + jj) * COLT, COLT), COLT)],
                sems.at[from_slot, q]) for q in range(4)]

        @pl.when(step >= 2)
        def _():
            for cp in copies(slot, step - 2):
                cp.wait()

        cc, hh, bb, g = c_ref[...], h_ref[...], b_ref[...], g_ref[...]
        w = w_ref[...]
        u = cc * hh
        up = jnp.where(i > 0, cp_ref[...] * hp_ref[...], 0.0)
        ext = jnp.concatenate([up, u], axis=0)
        um1 = pltpu.roll(ext, 1, 0)[SUBLANES:]
        um2 = pltpu.roll(ext, 2, 0)[SUBLANES:]
        cv = w[0:1] * um2 + w[1:2] * um1 + w[2:3] * u
        sig = _sigmoid(g)
        sg = g * sig
        dm = dm_ref[...]
        d_cv = (dm * bb) * sg
        gn = gn_ref[...]
        d_cv_next = jnp.where(i < n_row - 1, (dmn_ref[...] * bn_ref[...]) * (gn * _sigmoid(gn)), 0.0)
        ext2 = jnp.concatenate([d_cv, d_cv_next], axis=0)
        dp1 = pltpu.roll(ext2, tm + SUBLANES - 1, 0)[:tm]
        dp2 = pltpu.roll(ext2, tm + SUBLANES - 2, 0)[:tm]
        d_u = w[2:3] * d_cv + w[1:2] * dp1 + w[0:1] * dp2
        stash_ref[slot, 0] = ((dm * cv) * sg).astype(BF16)
        stash_ref[slot, 1] = (d_u * hh).astype(BF16)
        stash_ref[slot, 2] = (d_u * cc).astype(BF16)
        stash_ref[slot, 3] = (((dm * bb) * cv) * (sig * (1.0 + g * (1.0 - sig)))).astype(BF16)
        for cp in copies(slot, step):
            cp.start()

        @pl.when(i == 0)
        def _():
            acc_ref[...] = jnp.zeros_like(acc_ref)

        acc_ref[0:1, :] += jnp.sum(d_cv * um2, axis=0, keepdims=True)
        acc_ref[1:2, :] += jnp.sum(d_cv * um1, axis=0, keepdims=True)
        acc_ref[2:3, :] += jnp.sum(d_cv * u, axis=0, keepdims=True)

        @pl.when(step == n_step - 1)
        def _():
            if n_step >= 2:
                for cp in copies(1 - slot, step - 1):
                    cp.wait()
            for cp in copies(slot, step):
                cp.wait()

    seg = lambda q: pl.BlockSpec((tm, COLT), functools.partial(lambda j, i, q: (i, b_off + q * nseg + j), q=q))
    halo_p = lambda q: pl.BlockSpec(
        (SUBLANES, COLT),
        functools.partial(lambda j, i, q: (jnp.maximum(i * hb - 1, 0), b_off + q * nseg + j), q=q))
    halo_n = lambda q: pl.BlockSpec(
        (SUBLANES, COLT),
        functools.partial(lambda j, i, q: (jnp.minimum((i + 1) * hb, last_h), b_off + q * nseg + j), q=q))
    return _call(body, name="conv_bwd", grid=(nseg, n_row),
                 in_specs=[seg(0), seg(1), seg(2), seg(3), halo_p(1), halo_p(2), halo_n(0), halo_n(3),
                           pl.BlockSpec((tm, COLT), lambda j, i: (i, aw // COLT + j)),
                           pl.BlockSpec((SUBLANES, COLT),
                                        lambda j, i: (jnp.minimum((i + 1) * hb, last_h), aw // COLT + j)),
                           pl.BlockSpec((SUBLANES, COLT), lambda j, i: (0, j)), _hbm()],
                 out_specs=[_hbm(), pl.BlockSpec((SUBLANES, COLT), lambda j, i: (0, j))],
                 out_shape=[jax.ShapeDtypeStruct(dz.shape, BF16), jax.ShapeDtypeStruct((SUBLANES, cw), F32)],
                 input_output_aliases={11: 0},
                 scratch_shapes=[pltpu.VMEM((2, 4, tm, COLT), BF16), pltpu.SemaphoreType.DMA((2, 4))],
                 compiler_params=_params(("arbitrary", "arbitrary")))(
                     z, z, z, z, z, z, z, z, d_mix, d_mix, conv_w, dz)


def _place():
    x, y, c = lax.axis_index("x"), lax.axis_index("y"), lax.axis_index("c")
    chips = [(1 - x, y), (x, 1 - y), (1 - x, 1 - y)]
    return x, y, c, chips


def _remote(src, dst, send_sem, recv_sem, device):
    return pltpu.make_async_remote_copy(src_ref=src, dst_ref=dst, send_sem=send_sem, recv_sem=recv_sem,
                                        device_id=device, device_id_type=MESH)


def plan_gather_ici(n_split):
    def plan(refs, send, recv):
        x, y, c, chips = _place()
        me = 2 * x + y
        mine, theirs = [], []
        for w, ref in enumerate(refs):
            for j, (cx, cy) in enumerate(chips):
                def part(slot):
                    if w >= n_split:
                        return ref.at[slot]
                    hr = ref.shape[1] // 2
                    return ref.at[slot, pl.ds(c * hr, hr)]
                k = 3 * w + j
                mine.append(_remote(part(me), part(me), send.at[k], recv.at[k], (cx, cy, c)))
                theirs.append(_remote(part(2 * cx + cy), part(2 * cx + cy), send.at[k], recv.at[k], (cx, cy, c)))
        return mine, theirs
    return plan


def plan_shard_ici(j):
    def plan(refs, send, recv):
        _, _, c, chips = _place()
        own, land = refs
        hr = own.shape[0] // 2
        rows = pl.ds(c * hr, hr)
        cp = _remote(own.at[rows], land.at[rows], send.at[0], recv.at[0], (*chips[j], c))
        return [cp], [cp]
    return plan


def plan_shard_pass(refs, send, recv):
    x, y, c, _ = _place()
    land, = refs
    hr = land.shape[0] // 2
    half = lambda core: land.at[pl.ds(core * hr, hr)]
    return ([_remote(half(c), half(c), send.at[0], recv.at[0], (x, y, 1 - c))],
            [_remote(half(1 - c), half(1 - c), send.at[0], recv.at[0], (x, y, 1 - c))])


def plan_gather_pass(refs, send, recv):
    x, y, c, chips = _place()
    mine, theirs = [], []
    for w, ref in enumerate(refs):
        hr = ref.shape[1] // 2
        for j, (cx, cy) in enumerate(chips):
            half = lambda core: ref.at[2 * cx + cy, pl.ds(core * hr, hr)]
            k = 3 * w + j
            mine.append(_remote(half(c), half(c), send.at[k], recv.at[k], (x, y, 1 - c)))
            theirs.append(_remote(half(1 - c), half(1 - c), send.at[k], recv.at[k], (x, y, 1 - c)))
    return mine, theirs


def plan_swap(refs, send, recv):
    x, y, c, _ = _place()
    nw = len(refs) // 2
    mine = []
    for w in range(nw):
        hr = refs[w].shape[1] // 2
        mine.append(_remote(refs[w].at[:, pl.ds((1 - c) * hr, hr), :], refs[nw + w], send.at[w], recv.at[w],
                            (x, y, 1 - c)))
    return mine, mine


def plan_scatter(refs, send, recv):
    x, y, c, chips = _place()
    me = 2 * x + y
    nw = len(refs) // 2
    mine, theirs = [], []
    for w in range(nw):
        for j, (cx, cy) in enumerate(chips):
            k = 3 * w + j
            mine.append(_remote(refs[w].at[2 * cx + cy], refs[nw + w].at[me], send.at[k], recv.at[k], (cx, cy, c)))
            theirs.append(_remote(refs[w].at[me], refs[nw + w].at[2 * cx + cy], send.at[k], recv.at[k], (cx, cy, c)))
    return mine, theirs


def plan_join(refs, send, recv):
    x, y, c, _ = _place()
    mine, theirs = [], []
    for w, ref in enumerate(refs):
        hr = ref.shape[0] // 2
        half = lambda core: ref.at[pl.ds(core * hr, hr)]
        mine.append(_remote(half(c), half(c), send.at[w], recv.at[w], (x, y, 1 - c)))
        theirs.append(_remote(half(1 - c), half(1 - c), send.at[w], recv.at[w], (x, y, 1 - c)))
    return mine, theirs


def exchange(name, plan, arrays, n):
    na = len(arrays)

    def body(*refs):
        mine, theirs = plan(refs[:na], refs[2 * na], refs[2 * na + 1])
        for cp in mine:
            cp.start()
        for cp in theirs:
            cp.wait_recv()
        for cp in mine:
            cp.wait_send()

    return _call(body, name=name, in_specs=[_hbm()] * na, out_specs=[_hbm()] * na,
                 out_shape=[jax.ShapeDtypeStruct(a.shape, a.dtype) for a in arrays],
                 input_output_aliases={i: i for i in range(na)},
                 scratch_shapes=[pltpu.SemaphoreType.DMA((n,)), pltpu.SemaphoreType.DMA((n,))])(*arrays)


def exchange_start(name, groups, arrays, after=()):
    na, ng = len(arrays), len(groups)

    def body(*refs):
        for g, (plan, idx, _) in enumerate(groups):
            mine, _ = plan([refs[i] for i in idx], refs[na + 2 * g], refs[na + 2 * g + 1])
            for cp in mine:
                cp.start()
        refs[-1][...] = jnp.zeros_like(refs[-1])

    hbm = pl.BlockSpec(memory_space=pltpu.HBM)
    sem = pl.BlockSpec(memory_space=pltpu.SEMAPHORE)
    sem_types = [pltpu.SemaphoreType.DMA((n,)) for _, _, n in groups for _ in (0, 1)]
    outs = _call_after(body, after, name=name, in_specs=[hbm] * na,
                       out_specs=[sem] * (2 * ng) + [hbm] * na + [pl.BlockSpec(memory_space=pltpu.VMEM)],
                       out_shape=sem_types + [pltpu.HBM(a.shape, a.dtype) for a in arrays]
                       + [jax.ShapeDtypeStruct((SUBLANES, LANES), F32)],
                       input_output_aliases={i: i + 2 * ng for i in range(na)},
                       compiler_params=pltpu.CompilerParams(
                           has_side_effects=pltpu.SideEffectType.DATAFLOW_SIDE_EFFECTING))(
                               *[pltpu.with_memory_space_constraint(a, pltpu.HBM) for a in arrays])
    sems = [(outs[2 * g], outs[2 * g + 1]) for g in range(ng)]
    return sems, list(outs[2 * ng:-1]), outs[-1]


def exchange_wait(name, plan, sems, arrays, after):
    send_sems, recv_sems = sems
    na = len(arrays)

    def body(*refs):
        mine, theirs = plan(refs[:na], refs[na], refs[na + 1])
        for cp in mine:
            cp.wait_send()
        for cp in theirs:
            cp.wait_recv()

    hbm = pl.BlockSpec(memory_space=pltpu.HBM)
    sem = pl.BlockSpec(memory_space=pltpu.SEMAPHORE)
    return _call(body, name=name, in_specs=[hbm] * na + [sem, sem, _hbm()], out_specs=[hbm] * na,
                 out_shape=[pltpu.HBM(a.shape, a.dtype) for a in arrays],
                 input_output_aliases={i: i for i in range(na)},
                 compiler_params=pltpu.CompilerParams(
                     has_side_effects=pltpu.SideEffectType.DATAFLOW_SIDE_EFFECTING))(
                         *arrays, send_sems, recv_sems, after)


def plan_allgather_small(refs, send, recv):
    x, y, c, _ = _place()
    buf, = refs
    mine, theirs = [], []
    flips = [(fx, fy, fc) for fx in (0, 1) for fy in (0, 1) for fc in (0, 1)][1:]
    for k, (fx, fy, fc) in enumerate(flips):
        peer = (x ^ fx, y ^ fy, c ^ fc)
        slot = lambda px, py, pc: buf.at[4 * px + 2 * py + pc]
        mine.append(_remote(slot(x, y, c), slot(x, y, c), send.at[k], recv.at[k], peer))
        theirs.append(_remote(slot(*peer), slot(*peer), send.at[k], recv.at[k], peer))
    return mine, theirs


def add_sibling(grad, got, core, name):
    _, r, c = grad.shape
    hr = r // 2
    tr = _tile(hr, 512)
    nblk = hr // tr

    def body(core_ref, g_ref, o_ref, out_ref):
        out_ref[...] = (g_ref[...].astype(F32) + o_ref[...].astype(F32)).astype(BF16)

    grid_spec = pltpu.PrefetchScalarGridSpec(
        num_scalar_prefetch=1, grid=(N_CHIPS, nblk),
        in_specs=[pl.BlockSpec((None, tr, c), lambda t, i, core_ref: (t, core_ref[0] * nblk + i, 0)),
                  pl.BlockSpec((None, tr, c), lambda t, i, core_ref: (t, i, 0))],
        out_specs=pl.BlockSpec((None, tr, c), lambda t, i, core_ref: (t, i, 0)))
    return _call(body, name=name, grid_spec=grid_spec,
                 out_shape=jax.ShapeDtypeStruct((N_CHIPS, hr, c), BF16),
                 compiler_params=_params(("parallel", "parallel")))(core, grad, got)


def sum_chips(mine, owned, place, name):
    _, hr, c = mine.shape
    tr = _tile(hr, 512)
    nblk = hr // tr

    def body(place_ref, m_ref, o1_ref, o2_ref, o3_ref, out_ref):
        acc = m_ref[...].astype(F32)
        for o_ref in (o1_ref, o2_ref, o3_ref):
            acc = acc + o_ref[...].astype(F32)
        out_ref[...] = acc

    other = lambda k: pl.BlockSpec((None, tr, c), lambda i, place_ref: ((place_ref[0] + k) % N_CHIPS, i, 0))
    grid_spec = pltpu.PrefetchScalarGridSpec(
        num_scalar_prefetch=1, grid=(nblk,),
        in_specs=[other(0), other(1), other(2), other(3)],
        out_specs=pl.BlockSpec((tr, c), lambda i, place_ref: (place_ref[1] * nblk + i, 0)))
    return _call(body, name=name, grid_spec=grid_spec,
                 out_shape=jax.ShapeDtypeStruct((2 * hr, c), F32),
                 compiler_params=_params(("parallel",)))(place, mine, owned, owned, owned)


def sum_devices(gathered):
    _, rows, width = gathered.shape

    def body(g_ref, out_ref):
        acc = g_ref[0]
        for dev in range(1, 8):
            acc = acc + g_ref[dev]
        out_ref[...] = acc
        tail = acc[SUBLANES:]
        out_ref[SUBLANES:, :] = jnp.broadcast_to(jnp.sum(tail, axis=1, keepdims=True), tail.shape)

    return _call(body, name="sum_devices",
                 in_specs=[pl.BlockSpec(memory_space=pltpu.VMEM)],
                 out_specs=pl.BlockSpec(memory_space=pltpu.VMEM),
                 out_shape=jax.ShapeDtypeStruct((rows, width), F32))(gathered)


def _rope_tables(s):
    half = ROT_DIM // 2
    inv_freq = jnp.power(jnp.float32(ROPE_THETA), -jnp.arange(half, dtype=F32) * 2.0 / ROT_DIM)
    freq64 = jnp.concatenate([inv_freq, inv_freq, jnp.zeros((HEAD_DIM - ROT_DIM,), F32)])
    freq = jnp.concatenate([freq64, freq64])[None, :]
    dim = (jnp.arange(LANES) % HEAD_DIM)[None, :]
    ang = jnp.arange(s).astype(F32)[:, None] * freq
    cos, sin = jnp.cos(ang), jnp.sin(ang)
    return cos, jnp.where(dim < half, -sin, 0.0), jnp.where((dim >= half) & (dim < ROT_DIM), sin, 0.0)


def _pad_rows(a, rows):
    return jnp.pad(a, ((0, rows - a.shape[0]), (0, 0)))


def _pad_cols(a, cols):
    return jnp.pad(a, ((0, 0), (0, cols - a.shape[1])))


def kernel(x, p, norm_gain, w_in, q_norm_gain, k_norm_gain, attn_sinks, conv_w, w_out, ple_gate_norm_gain, w_ple_gate, b_ple_gate, w_ple_proj, ple_norm_gain, loss_target, m_norm_gain, m_w_in, m_q_norm_gain, m_k_norm_gain, m_attn_sinks, m_conv_w, m_w_out, m_ple_gate_norm_gain, m_w_ple_gate, m_b_ple_gate, m_w_ple_proj, m_ple_norm_gain, v_norm_gain, v_w_in, v_q_norm_gain, v_k_norm_gain, v_attn_sinks, v_conv_w, v_w_out, v_ple_gate_norm_gain, v_w_ple_gate, v_b_ple_gate, v_w_ple_proj, v_ple_norm_gain):
    x2, p2, tgt = x[0], p[0, 0], loss_target[0]
    s, d = x2.shape
    ple = p2.shape[1]
    aw = d // 2
    cw = d - aw
    nq = aw // HEAD_DIM
    sh = w_in.shape[2]
    in_w = N_CHIPS * sh
    dq = d // N_CHIPS
    cq = cw // N_CHIPS
    ga_off = (aw + 2 * KV_WIDTH) // COLT
    b_off = (2 * aw + 2 * KV_WIDTH) // COLT
    assert in_w == 2 * aw + 2 * KV_WIDTH + 4 * cw and aw % COLT == 0 and cw % COLT == 0
    assert s % BLOCK == 0 and sh % LANES == 0 and nq % (2 * N_KV_HEADS) == 0

    core = lax.axis_index("c").astype(jnp.int32).reshape(1)
    chip = 2 * lax.axis_index("x") + lax.axis_index("y")

    chip1 = chip.astype(jnp.int32).reshape(1)
    place = jnp.concatenate([chip1, core])
    w_in_own = cast_bf16(w_in[0], "cast_w_in")
    rest = [cast_into_slot(w_out[0], chip1, "cast_w_out"), cast_into_slot(w_ple_gate[0], chip1, "cast_w_pg"),
            cast_into_slot(w_ple_proj[0], chip1, "cast_w_pp"),
            lax.dynamic_update_slice(jnp.zeros((N_CHIPS, SUBLANES, cq), F32),
                                     _pad_rows(conv_w[0], SUBLANES)[None], (chip, 0, 0))]
    order = jnp.stack([chip, chip ^ 2, chip ^ 1, chip ^ 3]).astype(jnp.int32)
    wi_sems, wi_arrs, wi_token = exchange_start(
        "gather_wi_start", [(plan_shard_ici(j), [0, 1 + j], 1) for j in range(3)],
        [w_in_own] + [lax.empty((d, sh), BF16) for _ in range(3)])
    rest_sems, rest, rest_token = exchange_start(
        "gather_rest_start", [(plan_gather_ici(3), [0, 1, 2, 3], 12)], rest, after=(wi_token,))

    tm = _tile(s, 1024)
    tn = _tile(d, 1024)
    tk = _tile(d, 2048)
    ts = _tile(s, 2048)
    h = rms_fwd(x2, norm_gain, "rms_fwd_x", after=(rest_token,))
    tabs = _rope_tables(s)
    qg = jnp.tile(q_norm_gain, (1, LANES // HEAD_DIM))
    kg = jnp.tile(k_norm_gain, (1, LANES // HEAD_DIM))
    seg_i = jnp.arange(SEG) // HEAD_DIM
    segb = (seg_i[:, None] == seg_i[None, :]).astype(BF16)
    z = in_proj_part(h, wi_arrs[0], None, order, 0, in_w)
    w_shards = [wi_arrs[0]]
    for j in range(3):
        w_shards[0], land = exchange_wait("gather_wi_wait%d" % j, plan_shard_ici(j), wi_sems[j],
                                          [w_shards[0], wi_arrs[1 + j]], z)
        land, = exchange("gather_wi_pass%d" % j, plan_shard_pass, [land], 1)
        z = in_proj_part(h, land, z, order, 1 + j, in_w)
        w_shards.append(land)
    wo_all, wg_all, wp_all, conv_all = exchange_wait("gather_rest_wait", plan_gather_ici(3), rest_sems[0], rest, z)
    pass_sems, passed, pass_token = exchange_start(
        "gather_rest_pass_start", [(plan_gather_pass, [0, 1, 2], 9)], [wo_all, wg_all, wp_all])
    conv_full = conv_all.transpose(1, 0, 2).reshape(SUBLANES, cw)
    qs, ks, vb = qk_prep_fwd(z, tabs, qg, kg, segb, aw, after=(pass_token,))
    attn, mix = attn_fwd(qs, ks, vb, z, attn_sinks, aw, d, ga_off)
    mix = conv_fwd(z, conv_full, mix, aw, cw, b_off)
    wo_all, wg_all, wp_all = exchange_wait("gather_rest_pass_wait", plan_gather_pass, pass_sems[0], passed, mix)
    wo_full = wo_all.reshape(d, d)
    wg_full = wg_all.reshape(d, d)
    sq = lambda shape: dict(
        a_spec=pl.BlockSpec((tm, tk), lambda i, j, k: (i, k)),
        o_spec=pl.BlockSpec((tm, tn), lambda i, j, k: (i, j)),
        out_shape=jax.ShapeDtypeStruct(shape, F32))
    x1 = matmul(mix, wo_full, grid=(s // tm, d // tn, d // tk),
                b_spec=pl.BlockSpec((tk, tn), lambda i, j, k: (k, j)), dims=NN, name="mm_x1",
                res=x2, res_spec=pl.BlockSpec((tm, tn), lambda i, j, k: (i, j)), **sq((s, d)))
    hg = rms_fwd(x1, ple_gate_norm_gain, "rms_fwd_x1")
    gl = matmul(hg, wg_full, grid=(s // tm, d // tn, d // tk),
                b_spec=pl.BlockSpec((tk, tn), lambda i, j, k: (k, j)), dims=NN, name="mm_gl", **sq((s, d)))
    pq = d // N_CHIPS

    d_gl, d_pe, dy, acc_head = head_fwd_bwd(x1, gl, p2, wp_all, tgt, b_ple_gate, ple_norm_gain)
    d_hg = matmul(d_gl, wg_full, grid=(s // tm, d // tn, d // tk),
                  b_spec=pl.BlockSpec((tn, tk), lambda i, j, k: (j, k)), dims=NT, name="mm_d_hg", **sq((s, d)))
    wgrad = lambda a_cols, shape3, o_spec, b_cols, name, a, b, grid: matmul(
        a, b, grid=grid,
        a_spec=pl.BlockSpec((ts, a_cols), lambda i, j, k: (k, i)),
        b_spec=pl.BlockSpec((ts, b_cols), lambda i, j, k: (k, j)),
        o_spec=o_spec, out_shape=jax.ShapeDtypeStruct(shape3, BF16), dims=TN, name=name)
    g_wg = wgrad(tn, (d, d), pl.BlockSpec((tn, tn), lambda i, j, k: (i, j)), tn, "mm_g_wg", hg, d_gl,
                 (d // tn, d // tn, s // ts))
    g_wp = wgrad(ple, (N_CHIPS, ple, pq), pl.BlockSpec((None, ple, pq), lambda i, j, k: (j, 0, 0)), pq,
                 "mm_g_wp", p2, d_pe, (1, N_CHIPS, s // ts))
    d_x1, d_x1b, acc_g = rms_bwd_add(d_hg, x1, dy, ple_gate_norm_gain, "rms_bwd_x1", True)
    d_mix = matmul(d_x1b, wo_full, grid=(s // tm, d // tn, d // tk),
                   b_spec=pl.BlockSpec((tn, tk), lambda i, j, k: (j, k)), dims=NT, name="mm_d_mix", **sq((s, d)))
    g_wo = wgrad(tn, (d, d), pl.BlockSpec((tn, tn), lambda i, j, k: (i, j)), tn, "mm_g_wo", mix, d_x1b,
                 (d // tn, d // tn, s // ts))
    def reduce_start(tag, grads):
        n = len(grads)
        lands = [lax.empty((N_CHIPS, a.shape[1] // 2, a.shape[2]), BF16) for a in grads]
        swapped = exchange("swap_" + tag, plan_swap, list(grads) + lands, n)
        parts = [add_sibling(g, o, core, "add_sibling_%s%d" % (tag, i))
                 for i, (g, o) in enumerate(zip(swapped[:n], swapped[n:]))]
        return exchange_start("scatter_%s_start" % tag, [(plan_scatter, list(range(2 * n)), 3 * n)],
                              parts + [lax.empty(a.shape, BF16) for a in parts])

    def reduce_sum(tag, handle, after):
        sems, arrays, _ = handle
        n = len(arrays) // 2
        got = exchange_wait("scatter_%s_wait" % tag, plan_scatter, sems[0], arrays, after)
        return [sum_chips(pt, o, place, "sum_chips_%s%d" % (tag, i))
                for i, (pt, o) in enumerate(zip(got[:n], got[n:]))]

    early = reduce_start("early", [g_wo.reshape(N_CHIPS, dq, d), g_wg.reshape(N_CHIPS, dq, d), g_wp])
    d_o, dz = gate_bwd(d_mix, attn, z, aw, ga_off, after=(early[-1],))
    dqs, dks, dvs, acc_sink = attn_bwd(qs, ks, vb, d_o, attn_sinks, aw)
    dz, acc_qk = qk_prep_bwd(z, dqs, dks, dvs, tabs, qg, kg, segb, dz, aw)
    dz, acc_conv = conv_bwd(z, d_mix, conv_full, dz, aw, cw, b_off)
    join_sems, joining, join_token = exchange_start(
        "join_early_start", [(plan_join, [0, 1, 2], 3)], reduce_sum("early", early, dz))
    g_wi = matmul(h, dz, grid=(d // tn, N_CHIPS, 1),
                  a_spec=pl.BlockSpec((s, tn), lambda i, j, k: (0, i), pipeline_mode=pl.Buffered(1)),
                  b_spec=pl.BlockSpec((s, sh), lambda i, j, k: (0, j)),
                  o_spec=pl.BlockSpec((None, tn, sh), lambda i, j, k: (j, i, 0)),
                  out_shape=jax.ShapeDtypeStruct((N_CHIPS, d, sh), BF16), dims=TN, name="mm_g_wi",
                  after=(join_token,), vmem=VMEM_LIMIT_BIG)
    full_wo, full_wg, full_wp = exchange_wait("join_early_wait", plan_join, join_sems[0], joining, g_wi)
    late = reduce_start("late", [g_wi])
    d_h = in_proj_bwd(dz, w_shards, order, d, after=(late[-1],))
    grad_x, acc_x = rms_bwd_add(d_h, x2, d_x1, norm_gain, "rms_bwd_x", False)

    wsm = max(d, cw)
    misc = jnp.concatenate([acc_qk[0:1, :HEAD_DIM], acc_qk[1:2, :HEAD_DIM], acc_sink[0:1, :nq]], axis=1)
    small = jnp.concatenate([
        _pad_cols(acc_x[0:1], wsm), _pad_cols(acc_g[0:1], wsm), _pad_cols(acc_head[0:1], wsm),
        _pad_cols(acc_head[1:2], wsm), _pad_cols(misc, wsm), _pad_cols(acc_conv[0:3], wsm),
        _pad_rows(_pad_cols(acc_head[2:3], wsm), SUBLANES)], axis=0)
    device = 2 * chip + lax.axis_index("c")
    small_sems, small_bufs, small_token = exchange_start(
        "small_start", [(plan_allgather_small, [0], 7)],
        [lax.dynamic_update_slice(jnp.zeros((8,) + small.shape, F32), small[None], (device, 0, 0))])
    last_sems, last_halves, last_token = exchange_start(
        "join_late_start", [(plan_join, [0], 1)], reduce_sum("late", late, small_token))
    big, after = {}, (last_token,)
    for nm, g, w, m, v in (("w_out", full_wo, w_out, m_w_out, v_w_out),
                           ("w_ple_gate", full_wg, w_ple_gate, m_w_ple_gate, v_w_ple_gate),
                           ("w_ple_proj", full_wp, w_ple_proj, m_w_ple_proj, v_w_ple_proj)):
        stepped = adamw(g, w[0], m[0], v[0], "adamw_" + nm, after=after)
        big[nm], after = [o[None] for o in stepped], (stepped[1],)
    full_wi, = exchange_wait("join_late_wait", plan_join, last_sems[0], last_halves, after[0])
    big["w_in"] = [o[None] for o in adamw(full_wi, w_in[0], m_w_in[0], v_w_in[0], "adamw_w_in")]

    gathered, = exchange_wait("small_wait", plan_allgather_small, small_sems[0], small_bufs, big["w_in"][1])
    tot = sum_devices(gathered)
    loss = tot[SUBLANES, 0]

    def pack(vals):
        ng, pg, bg, eg, qgv, kgv, sk, cv = vals
        misc_v = jnp.concatenate([qgv, kgv, sk], axis=1)
        return jnp.concatenate([_pad_cols(ng, wsm), _pad_cols(pg, wsm), _pad_cols(bg, wsm), _pad_cols(eg, wsm),
                                _pad_cols(misc_v, wsm), _pad_cols(cv[0], wsm)], axis=0)

    g_small = jnp.concatenate(
        [tot[0:5], _pad_cols(lax.dynamic_slice(tot[5:8], (0, chip * cq), (3, cq)), wsm)], axis=0)
    w_small = pack((norm_gain, ple_gate_norm_gain, b_ple_gate, ple_norm_gain, q_norm_gain, k_norm_gain,
                    attn_sinks, conv_w))
    m_small = pack((m_norm_gain, m_ple_gate_norm_gain, m_b_ple_gate, m_ple_norm_gain, m_q_norm_gain,
                    m_k_norm_gain, m_attn_sinks, m_conv_w))
    v_small = pack((v_norm_gain, v_ple_gate_norm_gain, v_b_ple_gate, v_ple_norm_gain, v_q_norm_gain,
                    v_k_norm_gain, v_attn_sinks, v_conv_w))
    sm = adamw(g_small, w_small, m_small, v_small, "adamw_small")

    def unpack(a):
        return {"norm_gain": a[0:1, :d], "ple_gate_norm_gain": a[1:2, :d], "b_ple_gate": a[2:3, :d],
                "ple_norm_gain": a[3:4, :d], "q_norm_gain": a[4:5, :HEAD_DIM],
                "k_norm_gain": a[4:5, HEAD_DIM:2 * HEAD_DIM],
                "attn_sinks": a[4:5, 2 * HEAD_DIM:2 * HEAD_DIM + nq], "conv_w": a[5:8, :cq][None]}

    order = ("norm_gain", "w_in", "q_norm_gain", "k_norm_gain", "attn_sinks", "conv_w", "w_out",
             "ple_gate_norm_gain", "w_ple_gate", "b_ple_gate", "w_ple_proj", "ple_norm_gain")
    outs = [loss, grad_x[None]]
    for kind in range(4):
        table = unpack(sm[kind])
        for nm in order:
            outs.append(big[nm][kind] if nm in big else table[nm])
    return tuple(outs)
```

```python
import functools

import jax
import jax.numpy as jnp
from jax import lax
from jax.experimental import pallas as pl
from jax.experimental.pallas import tpu as pltpu

F32 = jnp.float32
BF16 = jnp.bfloat16
MESH = pl.DeviceIdType.MESH

HEAD_DIM = 64
N_KV_HEADS = 4
KV_WIDTH = N_KV_HEADS * HEAD_DIM
BLOCK = 128
ROT_DIM = 16
ROPE_THETA = 500000.0
EPS = 1e-6
NEG_INF = -1e30
N_CHIPS = 4
LANES = 128
SUBLANES = 8
COLT = 512
SEG = 256
VMEM_LIMIT = 48 * 1024 * 1024
VMEM_LIMIT_BIG = 60 * 1024 * 1024

ADAM_LR = 0.001
ADAM_B1 = 0.9
ADAM_B2 = 0.999
ADAM_EPS = 1e-08
ADAM_WD = 0.01
ADAM_STEP = 10

NN = (((1,), (0,)), ((), ()))
NT = (((1,), (1,)), ((), ()))
TN = (((0,), (0,)), ((), ()))


def _call(body, **kw):
    return pl.pallas_call(body, **kw)


def _call_after(body, after, **kw):
    n_in, n_after = len(kw["in_specs"]), len(after)
    kw["in_specs"] = list(kw["in_specs"]) + [pl.BlockSpec(memory_space=pl.ANY)] * n_after

    def body_after(*refs):
        body(*refs[:n_in], *refs[n_in + n_after:])

    call = _call(body_after, **kw)
    return lambda *args: call(*args, *after)


def _params(sem, vmem=VMEM_LIMIT):
    return pltpu.CompilerParams(dimension_semantics=sem, vmem_limit_bytes=vmem)


def _tile(n, pref):
    return pref if n % pref == 0 else n


def _sigmoid(v):
    return 1.0 / (1.0 + jnp.exp(-v))


def _hbm():
    return pl.BlockSpec(memory_space=pl.ANY)


def cast_bf16(a, name):
    r, c = a.shape
    tr = _tile(r, 512)

    def body(a_ref, o_ref):
        o_ref[...] = a_ref[...].astype(BF16)

    return _call(body, name=name, grid=(r // tr,),
                 in_specs=[pl.BlockSpec((tr, c), lambda i: (i, 0))],
                 out_specs=pl.BlockSpec((tr, c), lambda i: (i, 0)),
                 out_shape=jax.ShapeDtypeStruct((r, c), BF16),
                 compiler_params=_params(("parallel",)))(a)


def cast_into_slot(a, chip, name):
    r, c = a.shape
    tr = _tile(r, 512)

    def body(chip_ref, a_ref, o_ref):
        o_ref[...] = a_ref[...].astype(BF16)

    grid_spec = pltpu.PrefetchScalarGridSpec(
        num_scalar_prefetch=1, grid=(r // tr,),
        in_specs=[pl.BlockSpec((tr, c), lambda i, chip_ref: (i, 0))],
        out_specs=pl.BlockSpec((None, tr, c), lambda i, chip_ref: (chip_ref[0], i, 0)))
    return _call(body, name=name, grid_spec=grid_spec,
                 out_shape=jax.ShapeDtypeStruct((N_CHIPS, r, c), BF16),
                 compiler_params=_params(("parallel",)))(chip, a)


def rms_fwd(x, gain, name, after=()):
    s, d = x.shape
    tm = _tile(s, 512)

    def body(x_ref, g_ref, o_ref):
        xf = x_ref[...]
        r = lax.rsqrt(jnp.mean(xf * xf, axis=-1, keepdims=True) + EPS)
        o_ref[...] = ((xf * r) * g_ref[...]).astype(BF16)

    return _call_after(body, after, name=name, grid=(s // tm,),
                       in_specs=[pl.BlockSpec((tm, d), lambda i: (i, 0)),
                                 pl.BlockSpec((1, d), lambda i: (0, 0))],
                       out_specs=pl.BlockSpec((tm, d), lambda i: (i, 0)),
                       out_shape=jax.ShapeDtypeStruct((s, d), BF16),
                       compiler_params=_params(("parallel",)))(x, gain)


def rms_bwd_add(dyn, xin, add, gain, name, want_bf16):
    s, d = xin.shape
    tm = _tile(s, 512)

    def body(dy_ref, x_ref, a_ref, g_ref, *outs):
        i = pl.program_id(0)
        dx_ref, acc_ref = outs[0], outs[-1]
        xf = x_ref[...]
        r = lax.rsqrt(jnp.mean(xf * xf, axis=-1, keepdims=True) + EPS)
        xhat = xf * r
        dyv = dy_ref[...]
        gd = dyv * g_ref[...]
        dx = a_ref[...] + r * (gd - xhat * jnp.mean(xhat * gd, axis=-1, keepdims=True))
        dx_ref[...] = dx
        if want_bf16:
            outs[1][...] = dx.astype(BF16)

        @pl.when(i == 0)
        def _():
            acc_ref[...] = jnp.zeros_like(acc_ref)

        acc_ref[0:1, :] += jnp.sum(dyv * xhat, axis=0, keepdims=True)

    row = pl.BlockSpec((tm, d), lambda i: (i, 0))
    out_specs = [row] + ([row] if want_bf16 else []) + [pl.BlockSpec((SUBLANES, d), lambda i: (0, 0))]
    out_shape = ([jax.ShapeDtypeStruct((s, d), F32)]
                 + ([jax.ShapeDtypeStruct((s, d), BF16)] if want_bf16 else [])
                 + [jax.ShapeDtypeStruct((SUBLANES, d), F32)])
    return _call(body, name=name, grid=(s // tm,),
                 in_specs=[row, row, row, pl.BlockSpec((1, d), lambda i: (0, 0))],
                 out_specs=out_specs, out_shape=out_shape,
                 compiler_params=_params(("arbitrary",), VMEM_LIMIT_BIG))(dyn, xin, add, gain)


def head_fwd_bwd(x1, gl, p, wp, tgt, bias, ple_gain):
    s, d = x1.shape
    tm = _tile(s, 256)

    def body(x1_ref, gl_ref, p_ref, wp_ref, t_ref, b_ref, g_ref, dgl_ref, dpe_ref, dy_ref, acc_ref):
        i = pl.program_id(0)
        gate = _sigmoid(gl_ref[...] + b_ref[...])
        pb = p_ref[...].astype(BF16)
        pev = jnp.concatenate([jnp.dot(pb, wp_ref[q], preferred_element_type=F32) for q in range(N_CHIPS)],
                              axis=1)
        r = lax.rsqrt(jnp.mean(pev * pev, axis=-1, keepdims=True) + EPS)
        pehat = pev * r
        gain = g_ref[...]
        e = pehat * gain
        diff = (x1_ref[...] + gate * e) - t_ref[...]
        dy = diff * (1.0 / d)
        dy_ref[...] = dy
        d_gl = (dy * e) * (gate * (1.0 - gate))
        dgl_ref[...] = d_gl.astype(BF16)
        d_e = dy * gate
        gd = d_e * gain
        d_pe = r * (gd - pehat * jnp.mean(pehat * gd, axis=-1, keepdims=True))
        dpe_ref[...] = d_pe.astype(BF16)

        @pl.when(i == 0)
        def _():
            acc_ref[...] = jnp.zeros_like(acc_ref)

        acc_ref[0:1, :] += jnp.sum(d_gl, axis=0, keepdims=True)
        acc_ref[1:2, :] += jnp.sum(d_e * pehat, axis=0, keepdims=True)
        acc_ref[2:3, :] += jnp.sum(diff * diff, axis=0, keepdims=True) * (0.5 / d)

    row = pl.BlockSpec((tm, d), lambda i: (i, 0))
    vec = pl.BlockSpec((1, d), lambda i: (0, 0))
    return _call(body, name="head_fwd_bwd", grid=(s // tm,),
                 in_specs=[row, row, pl.BlockSpec((tm, p.shape[1]), lambda i: (i, 0)),
                           pl.BlockSpec(wp.shape, lambda i: (0, 0, 0)), row, vec, vec],
                 out_specs=[row, row, row, pl.BlockSpec((SUBLANES, d), lambda i: (0, 0))],
                 out_shape=[jax.ShapeDtypeStruct((s, d), BF16), jax.ShapeDtypeStruct((s, d), BF16),
                            jax.ShapeDtypeStruct((s, d), F32), jax.ShapeDtypeStruct((SUBLANES, d), F32)],
                 compiler_params=_params(("arbitrary",)))(x1, gl, p, wp, tgt, bias, ple_gain)


def adamw(g, w, m, v, name, after=()):
    r, c = g.shape
    tr = _tile(r, 256)

    def body(g_ref, w_ref, m_ref, v_ref, go_ref, d_ref, mo_ref, vo_ref):
        gv = g_ref[...]
        mn = ADAM_B1 * m_ref[...] + (1.0 - ADAM_B1) * gv
        vn = ADAM_B2 * v_ref[...] + (1.0 - ADAM_B2) * (gv * gv)
        m_hat = mn / (1.0 - ADAM_B1 ** ADAM_STEP)
        v_hat = vn / (1.0 - ADAM_B2 ** ADAM_STEP)
        go_ref[...] = gv
        d_ref[...] = -ADAM_LR * (m_hat / (jnp.sqrt(v_hat) + ADAM_EPS) + ADAM_WD * w_ref[...])
        mo_ref[...] = mn
        vo_ref[...] = vn

    blk = pl.BlockSpec((tr, c), lambda i: (i, 0))
    shp = jax.ShapeDtypeStruct((r, c), F32)
    return _call_after(body, after, name=name, grid=(r // tr,), in_specs=[blk] * 4, out_specs=[blk] * 4,
                       out_shape=[shp] * 4, compiler_params=_params(("parallel",)))(g, w, m, v)


def matmul(a, b, *, grid, a_spec, b_spec, o_spec, out_shape, dims, name, res=None, res_spec=None, after=(),
           vmem=VMEM_LIMIT):
    nk = grid[2]
    acc_shape = tuple(d for d in o_spec.block_shape if d is not None)

    def body(*refs):
        a_ref, b_ref = refs[:2]
        r_ref = refs[2] if res is not None else None
        o_ref = refs[3] if res is not None else refs[2]
        part = lax.dot_general(a_ref[...].astype(BF16), b_ref[...].astype(BF16), dims, preferred_element_type=F32)

        def finish(out):
            if res is not None:
                out = r_ref[...] + out
            o_ref[...] = out.astype(o_ref.dtype)

        if nk == 1:
            finish(part)
            return
        acc_ref = refs[-1]
        k = pl.program_id(2)

        @pl.when(k == 0)
        def _():
            acc_ref[...] = part

        @pl.when((k > 0) & (k < nk - 1))
        def _():
            acc_ref[...] += part

        @pl.when(k == nk - 1)
        def _():
            finish(acc_ref[...] + part)

    in_specs = [a_spec, b_spec] + ([res_spec] if res is not None else [])
    args = (a, b) + ((res,) if res is not None else ())
    return _call_after(body, after, name=name, grid=grid, in_specs=in_specs, out_specs=o_spec,
                       out_shape=out_shape, scratch_shapes=[pltpu.VMEM(acc_shape, F32)] if nk > 1 else [],
                       compiler_params=_params(("parallel", "parallel", "arbitrary"), vmem))(*args)


def in_proj_part(h, w, z_prev, order, q, in_w):
    s, d = h.shape
    sh = w.shape[1]
    tm = _tile(s, 512)

    def body(order_ref, h_ref, w_ref, *rest):
        rest[-1][...] = jnp.dot(h_ref[...], w_ref[...], preferred_element_type=F32)

    in_specs = [pl.BlockSpec((tm, d), lambda i, order_ref: (i, 0)),
                pl.BlockSpec((d, sh), lambda i, order_ref: (0, 0))]
    args = [order, h, w]
    if z_prev is not None:
        in_specs.append(_hbm())
        args.append(z_prev)
    grid_spec = pltpu.PrefetchScalarGridSpec(
        num_scalar_prefetch=1, grid=(s // tm,), in_specs=in_specs,
        out_specs=pl.BlockSpec((tm, sh), lambda i, order_ref: (i, order_ref[q])))
    return _call(body, name="mm_z%d" % q, grid_spec=grid_spec,
                 out_shape=jax.ShapeDtypeStruct((s, in_w), F32),
                 input_output_aliases={3: 0} if z_prev is not None else {},
                 compiler_params=_params(("parallel",)))(*args)


def in_proj_bwd(dz, ws, order, d, after):
    s = dz.shape[0]
    sh = ws[0].shape[1]
    tm, tn = _tile(s, 512), _tile(d, 1024)
    nq, n_after = len(ws), len(after)

    def body(order_ref, *refs):
        a_refs, b_refs, o_ref = refs[:nq], refs[nq:2 * nq], refs[2 * nq + n_after]
        acc = lax.dot_general(a_refs[0][...], b_refs[0][...], NT, preferred_element_type=F32)
        for q in range(1, nq):
            acc += lax.dot_general(a_refs[q][...], b_refs[q][...], NT, preferred_element_type=F32)
        o_ref[...] = acc

    a_spec = lambda q: pl.BlockSpec((tm, sh), functools.partial(lambda i, j, order_ref, q: (i, order_ref[q]), q=q))
    grid_spec = pltpu.PrefetchScalarGridSpec(
        num_scalar_prefetch=1, grid=(s // tm, d // tn),
        in_specs=[a_spec(q) for q in range(nq)]
        + [pl.BlockSpec((tn, sh), lambda i, j, order_ref: (j, 0))] * nq + [_hbm()] * n_after,
        out_specs=pl.BlockSpec((tm, tn), lambda i, j, order_ref: (i, j)))
    return _call(body, name="mm_d_h", grid_spec=grid_spec, out_shape=jax.ShapeDtypeStruct((s, d), F32),
                 compiler_params=_params(("parallel", "parallel"), VMEM_LIMIT_BIG))(
                     order, *([dz] * nq), *ws, *after)


def _seg_mean(sq, segb):
    parts = []
    for cgrp in range(sq.shape[1] // SEG):
        blk = sq[:, cgrp * SEG:(cgrp + 1) * SEG]
        hi = blk.astype(BF16)
        r1 = blk - hi.astype(F32)
        mid = r1.astype(BF16)
        lo = (r1 - mid.astype(F32)).astype(BF16)
        acc = jnp.dot(hi, segb, preferred_element_type=F32)
        acc += jnp.dot(mid, segb, preferred_element_type=F32)
        acc += jnp.dot(lo, segb, preferred_element_type=F32)
        parts.append(acc)
    out = parts[0] if len(parts) == 1 else jnp.concatenate(parts, axis=1)
    return out * (1.0 / HEAD_DIM)


def _rope(v, cos, sa, sb):
    parts = []
    for cgrp in range(v.shape[1] // LANES):
        blk = v[:, cgrp * LANES:(cgrp + 1) * LANES]
        parts.append(blk * cos + pltpu.roll(blk, LANES - 8, 1) * sa + pltpu.roll(blk, 8, 1) * sb)
    return parts[0] if len(parts) == 1 else jnp.concatenate(parts, axis=1)


def _rope_t(dv, cos, sa, sb):
    parts = []
    for cgrp in range(dv.shape[1] // LANES):
        blk = dv[:, cgrp * LANES:(cgrp + 1) * LANES]
        parts.append(blk * cos + pltpu.roll(blk * sa, 8, 1) + pltpu.roll(blk * sb, LANES - 8, 1))
    return parts[0] if len(parts) == 1 else jnp.concatenate(parts, axis=1)


def _tile_lanes(vec, width):
    reps = width // LANES
    return vec if reps == 1 else jnp.tile(vec, (1, reps))


def qk_prep_fwd(z, tabs, qg, kg, segb, aw, after=()):
    s = z.shape[0]
    tm = _tile(s, 512)
    wq = aw + 2 * KV_WIDTH

    def body(z_ref, cos_ref, sa_ref, sb_ref, qg_ref, kg_ref, seg_ref, qs_ref, ks_ref, vb_ref):
        zz = z_ref[...]
        q, k, v = zz[:, :aw], zz[:, aw:aw + KV_WIDTH], zz[:, aw + KV_WIDTH:]
        cos, sa, sb, segm = cos_ref[...], sa_ref[...], sb_ref[...], seg_ref[...]
        rq = lax.rsqrt(_seg_mean(q * q, segm) + EPS)
        qn = (q * rq) * _tile_lanes(qg_ref[...], aw)
        qs_ref[...] = (_rope(qn, cos, sa, sb) * (HEAD_DIM ** -0.5)).astype(BF16)
        rk = lax.rsqrt(_seg_mean(k * k, segm) + EPS)
        kn = (k * rk) * _tile_lanes(kg_ref[...], KV_WIDTH)
        ks_ref[...] = _rope(kn, cos, sa, sb).astype(BF16)
        vb_ref[...] = v.astype(BF16)

    tab = pl.BlockSpec((tm, LANES), lambda i: (i, 0))
    vec = pl.BlockSpec((1, LANES), lambda i: (0, 0))
    return _call_after(body, after, name="qk_prep_fwd", grid=(s // tm,),
                       in_specs=[pl.BlockSpec((tm, wq), lambda i: (i, 0)), tab, tab, tab, vec, vec,
                                 pl.BlockSpec((SEG, SEG), lambda i: (0, 0))],
                       out_specs=[pl.BlockSpec((tm, aw), lambda i: (i, 0)),
                                  pl.BlockSpec((tm, KV_WIDTH), lambda i: (i, 0)),
                                  pl.BlockSpec((tm, KV_WIDTH), lambda i: (i, 0))],
                       out_shape=[jax.ShapeDtypeStruct((s, aw), BF16), jax.ShapeDtypeStruct((s, KV_WIDTH), BF16),
                                  jax.ShapeDtypeStruct((s, KV_WIDTH), BF16)],
                       compiler_params=_params(("parallel",)))(z, *tabs, qg, kg, segb)


def qk_prep_bwd(z, dqs, dks, dvs, tabs, qg, kg, segb, dz, aw):
    s = z.shape[0]
    tm = _tile(s, 512)
    wq = aw + 2 * KV_WIDTH

    def body(z_ref, dq_ref, dk_ref, dv_ref, cos_ref, sa_ref, sb_ref, qg_ref, kg_ref, seg_ref, dz_in,
             dz_ref, acc_ref):
        i = pl.program_id(0)
        zz = z_ref[...]
        q, k = zz[:, :aw], zz[:, aw:aw + KV_WIDTH]
        cos, sa, sb, segm = cos_ref[...], sa_ref[...], sb_ref[...], seg_ref[...]

        def one(xv, dout, gvec, width):
            g = _tile_lanes(gvec, width)
            r = lax.rsqrt(_seg_mean(xv * xv, segm) + EPS)
            xhat = xv * r
            dn = _rope_t(dout, cos, sa, sb)
            gd = dn * g
            dx = r * (gd - xhat * _seg_mean(xhat * gd, segm))
            contrib = jnp.sum(dn * xhat, axis=0, keepdims=True)
            folded = contrib[:, :LANES]
            for cgrp in range(1, width // LANES):
                folded = folded + contrib[:, cgrp * LANES:(cgrp + 1) * LANES]
            return dx, folded + pltpu.roll(folded, HEAD_DIM, 1)

        dq, gq = one(q, dq_ref[...] * (HEAD_DIM ** -0.5), qg_ref[...], aw)
        dk, gk = one(k, dk_ref[...], kg_ref[...], KV_WIDTH)
        dz_ref[:, :aw] = dq.astype(BF16)
        dz_ref[:, aw:aw + KV_WIDTH] = dk.astype(BF16)
        dz_ref[:, aw + KV_WIDTH:] = dv_ref[...].astype(BF16)

        @pl.when(i == 0)
        def _():
            acc_ref[...] = jnp.zeros_like(acc_ref)

        acc_ref[0:1, :] += gq
        acc_ref[1:2, :] += gk

    tab = pl.BlockSpec((tm, LANES), lambda i: (i, 0))
    vec = pl.BlockSpec((1, LANES), lambda i: (0, 0))
    kvb = pl.BlockSpec((tm, KV_WIDTH), lambda i: (i, 0))
    return _call(body, name="qk_prep_bwd", grid=(s // tm,),
                 in_specs=[pl.BlockSpec((tm, wq), lambda i: (i, 0)),
                           pl.BlockSpec((tm, aw), lambda i: (i, 0)), kvb, kvb, tab, tab, tab, vec, vec,
                           pl.BlockSpec((SEG, SEG), lambda i: (0, 0)), _hbm()],
                 out_specs=[pl.BlockSpec((tm, wq), lambda i: (i, 0)),
                            pl.BlockSpec((SUBLANES, LANES), lambda i: (0, 0))],
                 out_shape=[jax.ShapeDtypeStruct(dz.shape, BF16), jax.ShapeDtypeStruct((SUBLANES, LANES), F32)],
                 input_output_aliases={10: 0},
                 compiler_params=_params(("arbitrary",)))(z, dqs, dks, dvs, *tabs, qg, kg, segb, dz)


def _placed(band, lane_idx):
    out = {}
    for kh in range(N_KV_HEADS):
        grp = band[:, (kh // 2) * LANES:(kh // 2 + 1) * LANES]
        for half in (0, 1):
            t = grp if half == kh % 2 else pltpu.roll(grp, HEAD_DIM, 1)
            keep = (lane_idx >= half * HEAD_DIM) & (lane_idx < (half + 1) * HEAD_DIM)
            out[kh, half] = jnp.where(keep, t, jnp.zeros_like(t))
    return out


def _softmax_with_sink(sc, valid, sink):
    sc = jnp.where(valid, sc, NEG_INF)
    m = jnp.maximum(jnp.max(sc, axis=-1, keepdims=True), sink)
    e = jnp.exp(sc - m)
    es = jnp.exp(sink - m)
    den = jnp.sum(e, axis=-1, keepdims=True) + es
    return e / den, es / den


def _stack_halves(placed, kh):
    return jnp.concatenate([placed[kh, 0], placed[kh, 1]], axis=0)


def _valid_mask(n):
    r_i = lax.broadcasted_iota(jnp.int32, (BLOCK, 2 * BLOCK), 0)
    j_i = lax.broadcasted_iota(jnp.int32, (BLOCK, 2 * BLOCK), 1)
    return (j_i > r_i) & (j_i <= r_i + BLOCK) & ((n > 0) | (j_i >= BLOCK))


def attn_fwd(qs, ks, vb, z, sinks, aw, d, ga_off):
    s = qs.shape[0]
    nb = s // BLOCK
    nq = aw // HEAD_DIM
    grp_sz = nq // N_KV_HEADS
    n_ga = aw // COLT

    def body(q_ref, ko_ref, kp_ref, vo_ref, vp_ref, *rest):
        ga_refs = rest[:n_ga]
        sink_ref, attn_ref, mix_ref = rest[n_ga:]
        n = pl.program_id(0)
        lane_idx = lax.broadcasted_iota(jnp.int32, (2 * BLOCK, LANES), 1)
        kpl = _placed(jnp.concatenate([kp_ref[...], ko_ref[...]], axis=0), lane_idx)
        vpl = _placed(jnp.concatenate([vp_ref[...], vo_ref[...]], axis=0), lane_idx)
        valid = _valid_mask(n)
        groups = range(nq // 2)
        kcat = [_stack_halves(kpl, kh) for kh in range(N_KV_HEADS)]
        vcat = [_stack_halves(vpl, kh) for kh in range(N_KV_HEADS)]
        scs = [lax.dot_general(q_ref[:, c * LANES:(c + 1) * LANES], kcat[2 * c // grp_sz], NT,
                               preferred_element_type=F32) for c in groups]
        probs = [jnp.concatenate(
            [_softmax_with_sink(scs[c][:, half * 2 * BLOCK:(half + 1) * 2 * BLOCK], valid,
                                sink_ref[0, 2 * c + half])[0].astype(BF16) for half in (0, 1)], axis=1)
                 for c in groups]
        for c in groups:
            acc = jnp.dot(probs[c], vcat[2 * c // grp_sz], preferred_element_type=F32)
            attn_ref[:, c * LANES:(c + 1) * LANES] = acc
            col = c * LANES
            ga = ga_refs[col // COLT][:, col % COLT:col % COLT + LANES]
            mix_ref[:, c * LANES:(c + 1) * LANES] = (acc * (ga * _sigmoid(ga))).astype(BF16)

    own = lambda n: (n, 0)
    prev = lambda n: (jnp.maximum(n - 1, 0), 0)
    kvs = lambda imap: pl.BlockSpec((BLOCK, KV_WIDTH), imap)
    ga_specs = [pl.BlockSpec((BLOCK, COLT), functools.partial(lambda n, j: (n, ga_off + j), j=j))
                for j in range(n_ga)]
    return _call(body, name="attn_fwd", grid=(nb,),
                 in_specs=[pl.BlockSpec((BLOCK, aw), own), kvs(own), kvs(prev), kvs(own), kvs(prev)]
                 + ga_specs + [pl.BlockSpec(memory_space=pltpu.SMEM)],
                 out_specs=[pl.BlockSpec((BLOCK, aw), own), pl.BlockSpec((BLOCK, aw), own)],
                 out_shape=[jax.ShapeDtypeStruct((s, aw), F32), jax.ShapeDtypeStruct((s, d), BF16)],
                 compiler_params=_params(("parallel",)))(qs, ks, ks, vb, vb, *([z] * n_ga), sinks)


def gate_bwd(d_mix, attn, z, aw, ga_off, after=()):
    s, in_w = z.shape
    tm = _tile(s, 1024)

    def body(dm_ref, at_ref, ga_ref, do_ref, dz_ref):
        ga = ga_ref[...]
        sig = _sigmoid(ga)
        dm = dm_ref[...]
        do_ref[...] = (dm * (ga * sig)).astype(BF16)
        dz_ref[...] = ((dm * at_ref[...]) * (sig * (1.0 + ga * (1.0 - sig)))).astype(BF16)

    blk = pl.BlockSpec((tm, COLT), lambda i, j: (i, j))
    gab = pl.BlockSpec((tm, COLT), lambda i, j: (i, ga_off + j))
    return _call_after(body, after, name="gate_bwd", grid=(s // tm, aw // COLT),
                       in_specs=[blk, blk, gab], out_specs=[blk, gab],
                       out_shape=[jax.ShapeDtypeStruct((s, aw), BF16), jax.ShapeDtypeStruct((s, in_w), BF16)],
                       compiler_params=_params(("parallel", "parallel")))(d_mix, attn, z)


def attn_bwd(qs, ks, vb, d_o, sinks, aw):
    s = qs.shape[0]
    nb = s // BLOCK
    nq = aw // HEAD_DIM
    grp_sz = nq // N_KV_HEADS

    def body(q_ref, ko_ref, kp_ref, vo_ref, vp_ref, do_ref, sink_ref, dq_ref, dk_ref, dv_ref, ds_ref,
             ck_ref, cv_ref):
        n = pl.program_id(0)

        @pl.when(n == 0)
        def _():
            ds_ref[...] = jnp.zeros_like(ds_ref)
            dk_ref[...] = jnp.zeros_like(dk_ref)
            dv_ref[...] = jnp.zeros_like(dv_ref)

        @pl.when(n < nb)
        def _():
            lane_idx = lax.broadcasted_iota(jnp.int32, (2 * BLOCK, LANES), 1)
            lane_row = lax.broadcasted_iota(jnp.int32, (1, LANES), 1)
            kpl = _placed(jnp.concatenate([kp_ref[...], ko_ref[...]], axis=0), lane_idx)
            vpl = _placed(jnp.concatenate([vp_ref[...], vo_ref[...]], axis=0), lane_idx)
            valid = _valid_mask(n)
            ds_row = jnp.zeros((1, LANES), F32)
            wide = 2 * BLOCK
            groups = range(nq // 2)
            per_kv = grp_sz // 2
            kcat = [_stack_halves(kpl, kh) for kh in range(N_KV_HEADS)]
            vcat = [_stack_halves(vpl, kh) for kh in range(N_KV_HEADS)]
            qg = [q_ref[:, c * LANES:(c + 1) * LANES] for c in groups]
            dog = [do_ref[:, c * LANES:(c + 1) * LANES] for c in groups]
            scs = [lax.dot_general(qg[c], kcat[c // per_kv], NT, preferred_element_type=F32) for c in groups]
            dps = [lax.dot_general(dog[c], vcat[c // per_kv], NT, preferred_element_type=F32) for c in groups]
            ds_pairs, p_pairs = [], []
            for c in groups:
                probs, dss = [], []
                for half in (0, 1):
                    h = 2 * c + half
                    cols = slice(half * wide, (half + 1) * wide)
                    p, p_sink = _softmax_with_sink(scs[c][:, cols], valid, sink_ref[0, h])
                    delta = jnp.sum(p * dps[c][:, cols], axis=-1, keepdims=True)
                    dss.append((p * (dps[c][:, cols] - delta)).astype(BF16))
                    probs.append(p.astype(BF16))
                    dsink = -jnp.sum(p_sink * delta, axis=0, keepdims=True)
                    ds_row += jnp.where(lane_row == h, dsink, 0.0)
                ds_pairs.append(jnp.concatenate(dss, axis=1))
                p_pairs.append(jnp.concatenate(probs, axis=1))
            for c in groups:
                dq_ref[:, c * LANES:(c + 1) * LANES] = jnp.dot(ds_pairs[c], kcat[c // per_kv],
                                                               preferred_element_type=F32)
            dkts = [lax.dot_general(qg[c], ds_pairs[c], TN, preferred_element_type=F32) for c in groups]
            dvts = [lax.dot_general(dog[c], p_pairs[c], TN, preferred_element_type=F32) for c in groups]
            dkt, dvt = [], []
            for kh in range(N_KV_HEADS):
                for parts, out in ((dkts, dkt), (dvts, dvt)):
                    tot = parts[kh * per_kv]
                    for c in range(kh * per_kv + 1, (kh + 1) * per_kv):
                        tot = tot + parts[c]
                    out.append(tot[:HEAD_DIM, :wide] + tot[HEAD_DIM:, wide:])
            ds_ref[0:1, :] += ds_row
            back = lambda t: jnp.concatenate(
                [jnp.concatenate(t[2 * g:2 * g + 2], axis=0).T for g in range(N_KV_HEADS // 2)], axis=1)
            dk_band, dv_band = back(dkt), back(dvt)

            @pl.when(n > 0)
            def _():
                dk_ref[...] = ck_ref[...] + dk_band[:BLOCK]
                dv_ref[...] = cv_ref[...] + dv_band[:BLOCK]

            ck_ref[...] = dk_band[BLOCK:]
            cv_ref[...] = dv_band[BLOCK:]

        @pl.when(n == nb)
        def _():
            dk_ref[...] = ck_ref[...]
            dv_ref[...] = cv_ref[...]

    own = lambda n: (jnp.minimum(n, nb - 1), 0)
    prev = lambda n: (jnp.clip(n - 1, 0, nb - 1), 0)
    done = lambda n: (jnp.maximum(n - 1, 0), 0)
    kvs = lambda imap: pl.BlockSpec((BLOCK, KV_WIDTH), imap)
    return _call(body, name="attn_bwd", grid=(nb + 1,),
                 in_specs=[pl.BlockSpec((BLOCK, aw), own), kvs(own), kvs(prev), kvs(own), kvs(prev),
                           pl.BlockSpec((BLOCK, aw), own), pl.BlockSpec(memory_space=pltpu.SMEM)],
                 out_specs=[pl.BlockSpec((BLOCK, aw), own), kvs(done), kvs(done),
                            pl.BlockSpec((SUBLANES, LANES), lambda n: (0, 0))],
                 out_shape=[jax.ShapeDtypeStruct((s, aw), F32), jax.ShapeDtypeStruct((s, KV_WIDTH), F32),
                            jax.ShapeDtypeStruct((s, KV_WIDTH), F32), jax.ShapeDtypeStruct((SUBLANES, LANES), F32)],
                 scratch_shapes=[pltpu.VMEM((BLOCK, KV_WIDTH), F32), pltpu.VMEM((BLOCK, KV_WIDTH), F32)],
                 compiler_params=_params(("arbitrary",)))(qs, ks, ks, vb, vb, d_o, sinks)


def conv_fwd(z, conv_w, mix, aw, cw, b_off):
    s = z.shape[0]
    tm = _tile(s, 1024)
    nseg = cw // COLT
    hb = tm // SUBLANES

    def body(b_ref, c_ref, h_ref, g_ref, cp_ref, hp_ref, w_ref, mix_in, mix_ref):
        i = pl.program_id(0)
        u = c_ref[...] * h_ref[...]
        up = jnp.where(i > 0, cp_ref[...] * hp_ref[...], 0.0)
        ext = jnp.concatenate([up, u], axis=0)
        um1 = pltpu.roll(ext, 1, 0)[SUBLANES:]
        um2 = pltpu.roll(ext, 2, 0)[SUBLANES:]
        w = w_ref[...]
        cv = w[0:1] * um2 + w[1:2] * um1 + w[2:3] * u
        g = g_ref[...]
        mix_ref[...] = ((b_ref[...] * cv) * (g * _sigmoid(g))).astype(BF16)

    seg = lambda k: pl.BlockSpec((tm, COLT), functools.partial(lambda i, j, k: (i, b_off + k * nseg + j), k=k))
    halo = lambda k: pl.BlockSpec(
        (SUBLANES, COLT), functools.partial(lambda i, j, k: (jnp.maximum(i * hb - 1, 0), b_off + k * nseg + j), k=k))
    return _call(body, name="conv_fwd", grid=(s // tm, nseg),
                 in_specs=[seg(0), seg(1), seg(2), seg(3), halo(1), halo(2),
                           pl.BlockSpec((SUBLANES, COLT), lambda i, j: (0, j)), _hbm()],
                 out_specs=pl.BlockSpec((tm, COLT), lambda i, j: (i, aw // COLT + j)),
                 out_shape=jax.ShapeDtypeStruct(mix.shape, BF16),
                 input_output_aliases={7: 0},
                 compiler_params=_params(("parallel", "parallel")))(z, z, z, z, z, z, conv_w, mix)


def conv_bwd(z, d_mix, conv_w, dz, aw, cw, b_off):
    s = z.shape[0]
    tm = _tile(s, 512)
    nseg = cw // COLT
    hb = tm // SUBLANES
    n_row = s // tm
    last_h = s // SUBLANES - 1
    n_step = nseg * n_row

    def body(b_ref, c_ref, h_ref, g_ref, cp_ref, hp_ref, bn_ref, gn_ref, dm_ref, dmn_ref, w_ref, dz_in,
             dz_ref, acc_ref, stash_ref, sems):
        j = pl.program_id(0)
        i = pl.program_id(1)
        step = j * n_row + i
        slot = step % 2

        def copies(from_slot, at_step):
            jj, ii = at_step // n_row, at_step % n_row
            rows = pl.ds(pl.multiple_of(ii * tm, tm), tm)
            return [pltpu.make_async_copy(
                stash_ref.at[from_slot, q],
                dz_ref.at[rows, pl.ds(pl.multiple_of((b_off + q * nseg + jj) * COLT, COLT), COLT)],
                sems.at[from_slot, q]) for q in range(4)]

        @pl.when(step >= 2)
        def _():
            for cp in copies(slot, step - 2):
                cp.wait()

        cc, hh, bb, g = c_ref[...], h_ref[...], b_ref[...], g_ref[...]
        w = w_ref[...]
        u = cc * hh
        up = jnp.where(i > 0, cp_ref[...] * hp_ref[...], 0.0)
        ext = jnp.concatenate([up, u], axis=0)
        um1 = pltpu.roll(ext, 1, 0)[SUBLANES:]
        um2 = pltpu.roll(ext, 2, 0)[SUBLANES:]
        cv = w[0:1] * um2 + w[1:2] * um1 + w[2:3] * u
        sig = _sigmoid(g)
        sg = g * sig
        dm = dm_ref[...]
        d_cv = (dm * bb) * sg
        gn = gn_ref[...]
        d_cv_next = jnp.where(i < n_row - 1, (dmn_ref[...] * bn_ref[...]) * (gn * _sigmoid(gn)), 0.0)
        ext2 = jnp.concatenate([d_cv, d_cv_next], axis=0)
        dp1 = pltpu.roll(ext2, tm + SUBLANES - 1, 0)[:tm]
        dp2 = pltpu.roll(ext2, tm + SUBLANES - 2, 0)[:tm]
        d_u = w[2:3] * d_cv + w[1:2] * dp1 + w[0:1] * dp2
        stash_ref[slot, 0] = ((dm * cv) * sg).astype(BF16)
        stash_ref[slot, 1] = (d_u * hh).astype(BF16)
        stash_ref[slot, 2] = (d_u * cc).astype(BF16)
        stash_ref[slot, 3] = (((dm * bb) * cv) * (sig * (1.0 + g * (1.0 - sig)))).astype(BF16)
        for cp in copies(slot, step):
            cp.start()

        @pl.when(i == 0)
        def _():
            acc_ref[...] = jnp.zeros_like(acc_ref)

        acc_ref[0:1, :] += jnp.sum(d_cv * um2, axis=0, keepdims=True)
        acc_ref[1:2, :] += jnp.sum(d_cv * um1, axis=0, keepdims=True)
        acc_ref[2:3, :] += jnp.sum(d_cv * u, axis=0, keepdims=True)

        @pl.when(step == n_step - 1)
        def _():
            if n_step >= 2:
                for cp in copies(1 - slot, step - 1):
                    cp.wait()
            for cp in copies(slot, step):
                cp.wait()

    seg = lambda q: pl.BlockSpec((tm, COLT), functools.partial(lambda j, i, q: (i, b_off + q * nseg + j), q=q))
    halo_p = lambda q: pl.BlockSpec(
        (SUBLANES, COLT),
        functools.partial(lambda j, i, q: (jnp.maximum(i * hb - 1, 0), b_off + q * nseg + j), q=q))
    halo_n = lambda q: pl.BlockSpec(
        (SUBLANES, COLT),
        functools.partial(lambda j, i, q: (jnp.minimum((i + 1) * hb, last_h), b_off + q * nseg + j), q=q))
    return _call(body, name="conv_bwd", grid=(nseg, n_row),
                 in_specs=[seg(0), seg(1), seg(2), seg(3), halo_p(1), halo_p(2), halo_n(0), halo_n(3),
                           pl.BlockSpec((tm, COLT), lambda j, i: (i, aw // COLT + j)),
                           pl.BlockSpec((SUBLANES, COLT),
                                        lambda j, i: (jnp.minimum((i + 1) * hb, last_h), aw // COLT + j)),
                           pl.BlockSpec((SUBLANES, COLT), lambda j, i: (0, j)), _hbm()],
                 out_specs=[_hbm(), pl.BlockSpec((SUBLANES, COLT), lambda j, i: (0, j))],
                 out_shape=[jax.ShapeDtypeStruct(dz.shape, BF16), jax.ShapeDtypeStruct((SUBLANES, cw), F32)],
                 input_output_aliases={11: 0},
                 scratch_shapes=[pltpu.VMEM((2, 4, tm, COLT), BF16), pltpu.SemaphoreType.DMA((2, 4))],
                 compiler_params=_params(("arbitrary", "arbitrary")))(
                     z, z, z, z, z, z, z, z, d_mix, d_mix, conv_w, dz)


def _place():
    x, y, c = lax.axis_index("x"), lax.axis_index("y"), lax.axis_index("c")
    chips = [(1 - x, y), (x, 1 - y), (1 - x, 1 - y)]
    return x, y, c, chips


def _remote(src, dst, send_sem, recv_sem, device):
    return pltpu.make_async_remote_copy(src_ref=src, dst_ref=dst, send_sem=send_sem, recv_sem=recv_sem,
                                        device_id=device, device_id_type=MESH)


def plan_gather_ici(n_split):
    def plan(refs, send, recv):
        x, y, c, chips = _place()
        me = 2 * x + y
        mine, theirs = [], []
        for w, ref in enumerate(refs):
            for j, (cx, cy) in enumerate(chips):
                def part(slot):
                    if w >= n_split:
                        return ref.at[slot]
                    hr = ref.shape[1] // 2
                    return ref.at[slot, pl.ds(c * hr, hr)]
                k = 3 * w + j
                mine.append(_remote(part(me), part(me), send.at[k], recv.at[k], (cx, cy, c)))
                theirs.append(_remote(part(2 * cx + cy), part(2 * cx + cy), send.at[k], recv.at[k], (cx, cy, c)))
        return mine, theirs
    return plan


def plan_shard_ici(j):
    def plan(refs, send, recv):
        _, _, c, chips = _place()
        own, land = refs
        hr = own.shape[0] // 2
        rows = pl.ds(c * hr, hr)
        cp = _remote(own.at[rows], land.at[rows], send.at[0], recv.at[0], (*chips[j], c))
        return [cp], [cp]
    return plan


def plan_shard_relay(refs, send, recv):
    _, _, c, chips = _place()
    land_x, land_y, land_far = refs
    hr = land_far.shape[0] // 2
    quarter = lambda q: pl.ds(c * hr + q * (hr // 2), hr // 2)
    mine = [_remote(land_y.at[quarter(0)], land_far.at[quarter(0)], send.at[0], recv.at[0], (*chips[0], c)),
            _remote(land_x.at[quarter(1)], land_far.at[quarter(1)], send.at[1], recv.at[1], (*chips[1], c))]
    return mine, mine


def plan_shard_pass(refs, send, recv):
    x, y, c, _ = _place()
    land, = refs
    hr = land.shape[0] // 2
    half = lambda core: land.at[pl.ds(core * hr, hr)]
    return ([_remote(half(c), half(c), send.at[0], recv.at[0], (x, y, 1 - c))],
            [_remote(half(1 - c), half(1 - c), send.at[0], recv.at[0], (x, y, 1 - c))])


def plan_gather_pass(refs, send, recv):
    x, y, c, chips = _place()
    mine, theirs = [], []
    for w, ref in enumerate(refs):
        hr = ref.shape[1] // 2
        for j, (cx, cy) in enumerate(chips):
            half = lambda core: ref.at[2 * cx + cy, pl.ds(core * hr, hr)]
            k = 3 * w + j
            mine.append(_remote(half(c), half(c), send.at[k], recv.at[k], (x, y, 1 - c)))
            theirs.append(_remote(half(1 - c), half(1 - c), send.at[k], recv.at[k], (x, y, 1 - c)))
    return mine, theirs


def plan_swap(refs, send, recv):
    x, y, c, _ = _place()
    nw = len(refs) // 2
    mine = []
    for w in range(nw):
        hr = refs[w].shape[1] // 2
        mine.append(_remote(refs[w].at[:, pl.ds((1 - c) * hr, hr), :], refs[nw + w], send.at[w], recv.at[w],
                            (x, y, 1 - c)))
    return mine, mine


def plan_scatter(refs, send, recv):
    x, y, c, chips = _place()
    me = 2 * x + y
    nw = len(refs) // 2
    mine, theirs = [], []
    for w in range(nw):
        for j, (cx, cy) in enumerate(chips):
            k = 3 * w + j
            mine.append(_remote(refs[w].at[2 * cx + cy], refs[nw + w].at[me], send.at[k], recv.at[k], (cx, cy, c)))
            theirs.append(_remote(refs[w].at[me], refs[nw + w].at[2 * cx + cy], send.at[k], recv.at[k], (cx, cy, c)))
    return mine, theirs


def plan_join(refs, send, recv):
    x, y, c, _ = _place()
    mine, theirs = [], []
    for w, ref in enumerate(refs):
        hr = ref.shape[0] // 2
        half = lambda core: ref.at[pl.ds(core * hr, hr)]
        mine.append(_remote(half(c), half(c), send.at[w], recv.at[w], (x, y, 1 - c)))
        theirs.append(_remote(half(1 - c), half(1 - c), send.at[w], recv.at[w], (x, y, 1 - c)))
    return mine, theirs


def exchange(name, plan, arrays, n):
    na = len(arrays)

    def body(*refs):
        mine, theirs = plan(refs[:na], refs[2 * na], refs[2 * na + 1])
        for cp in mine:
            cp.start()
        for cp in theirs:
            cp.wait_recv()
        for cp in mine:
            cp.wait_send()

    return _call(body, name=name, in_specs=[_hbm()] * na, out_specs=[_hbm()] * na,
                 out_shape=[jax.ShapeDtypeStruct(a.shape, a.dtype) for a in arrays],
                 input_output_aliases={i: i for i in range(na)},
                 scratch_shapes=[pltpu.SemaphoreType.DMA((n,)), pltpu.SemaphoreType.DMA((n,))])(*arrays)


def exchange_start(name, groups, arrays, after=()):
    na, ng = len(arrays), len(groups)

    def body(*refs):
        for g, (plan, idx, _) in enumerate(groups):
            mine, _ = plan([refs[i] for i in idx], refs[na + 2 * g], refs[na + 2 * g + 1])
            for cp in mine:
                cp.start()
        refs[-1][...] = jnp.zeros_like(refs[-1])

    hbm = pl.BlockSpec(memory_space=pltpu.HBM)
    sem = pl.BlockSpec(memory_space=pltpu.SEMAPHORE)
    sem_types = [pltpu.SemaphoreType.DMA((n,)) for _, _, n in groups for _ in (0, 1)]
    outs = _call_after(body, after, name=name, in_specs=[hbm] * na,
                       out_specs=[sem] * (2 * ng) + [hbm] * na + [pl.BlockSpec(memory_space=pltpu.VMEM)],
                       out_shape=sem_types + [pltpu.HBM(a.shape, a.dtype) for a in arrays]
                       + [jax.ShapeDtypeStruct((SUBLANES, LANES), F32)],
                       input_output_aliases={i: i + 2 * ng for i in range(na)},
                       compiler_params=pltpu.CompilerParams(
                           has_side_effects=pltpu.SideEffectType.DATAFLOW_SIDE_EFFECTING))(
                               *[pltpu.with_memory_space_constraint(a, pltpu.HBM) for a in arrays])
    sems = [(outs[2 * g], outs[2 * g + 1]) for g in range(ng)]
    return sems, list(outs[2 * ng:-1]), outs[-1]


def exchange_wait(name, plan, sems, arrays, after):
    send_sems, recv_sems = sems
    na = len(arrays)

    def body(*refs):
        mine, theirs = plan(refs[:na], refs[na], refs[na + 1])
        for cp in mine:
            cp.wait_send()
        for cp in theirs:
            cp.wait_recv()

    hbm = pl.BlockSpec(memory_space=pltpu.HBM)
    sem = pl.BlockSpec(memory_space=pltpu.SEMAPHORE)
    return _call(body, name=name, in_specs=[hbm] * na + [sem, sem, _hbm()], out_specs=[hbm] * na,
                 out_shape=[pltpu.HBM(a.shape, a.dtype) for a in arrays],
                 input_output_aliases={i: i for i in range(na)},
                 compiler_params=pltpu.CompilerParams(
                     has_side_effects=pltpu.SideEffectType.DATAFLOW_SIDE_EFFECTING))(
                         *arrays, send_sems, recv_sems, after)


def plan_allgather_small(refs, send, recv):
    x, y, c, _ = _place()
    buf, = refs
    mine, theirs = [], []
    flips = [(fx, fy, fc) for fx in (0, 1) for fy in (0, 1) for fc in (0, 1)][1:]
    for k, (fx, fy, fc) in enumerate(flips):
        peer = (x ^ fx, y ^ fy, c ^ fc)
        slot = lambda px, py, pc: buf.at[4 * px + 2 * py + pc]
        mine.append(_remote(slot(x, y, c), slot(x, y, c), send.at[k], recv.at[k], peer))
        theirs.append(_remote(slot(*peer), slot(*peer), send.at[k], recv.at[k], peer))
    return mine, theirs


def add_sibling(grad, got, core, name):
    _, r, c = grad.shape
    hr = r // 2
    tr = _tile(hr, 512)
    nblk = hr // tr

    def body(core_ref, g_ref, o_ref, out_ref):
        out_ref[...] = (g_ref[...].astype(F32) + o_ref[...].astype(F32)).astype(BF16)

    grid_spec = pltpu.PrefetchScalarGridSpec(
        num_scalar_prefetch=1, grid=(N_CHIPS, nblk),
        in_specs=[pl.BlockSpec((None, tr, c), lambda t, i, core_ref: (t, core_ref[0] * nblk + i, 0)),
                  pl.BlockSpec((None, tr, c), lambda t, i, core_ref: (t, i, 0))],
        out_specs=pl.BlockSpec((None, tr, c), lambda t, i, core_ref: (t, i, 0)))
    return _call(body, name=name, grid_spec=grid_spec,
                 out_shape=jax.ShapeDtypeStruct((N_CHIPS, hr, c), BF16),
                 compiler_params=_params(("parallel", "parallel")))(core, grad, got)


def sum_chips(mine, owned, place, name):
    _, hr, c = mine.shape
    tr = _tile(hr, 512)
    nblk = hr // tr

    def body(place_ref, m_ref, o1_ref, o2_ref, o3_ref, out_ref):
        acc = m_ref[...].astype(F32)
        for o_ref in (o1_ref, o2_ref, o3_ref):
            acc = acc + o_ref[...].astype(F32)
        out_ref[...] = acc

    other = lambda k: pl.BlockSpec((None, tr, c), lambda i, place_ref: ((place_ref[0] + k) % N_CHIPS, i, 0))
    grid_spec = pltpu.PrefetchScalarGridSpec(
        num_scalar_prefetch=1, grid=(nblk,),
        in_specs=[other(0), other(1), other(2), other(3)],
        out_specs=pl.BlockSpec((tr, c), lambda i, place_ref: (place_ref[1] * nblk + i, 0)))
    return _call(body, name=name, grid_spec=grid_spec,
                 out_shape=jax.ShapeDtypeStruct((2 * hr, c), F32),
                 compiler_params=_params(("parallel",)))(place, mine, owned, owned, owned)


def sum_devices(gathered):
    _, rows, width = gathered.shape

    def body(g_ref, out_ref):
        acc = g_ref[0]
        for dev in range(1, 8):
            acc = acc + g_ref[dev]
        out_ref[...] = acc
        tail = acc[SUBLANES:]
        out_ref[SUBLANES:, :] = jnp.broadcast_to(jnp.sum(tail, axis=1, keepdims=True), tail.shape)

    return _call(body, name="sum_devices",
                 in_specs=[pl.BlockSpec(memory_space=pltpu.VMEM)],
                 out_specs=pl.BlockSpec(memory_space=pltpu.VMEM),
                 out_shape=jax.ShapeDtypeStruct((rows, width), F32))(gathered)


def _rope_tables(s):
    half = ROT_DIM // 2
    inv_freq = jnp.power(jnp.float32(ROPE_THETA), -jnp.arange(half, dtype=F32) * 2.0 / ROT_DIM)
    freq64 = jnp.concatenate([inv_freq, inv_freq, jnp.zeros((HEAD_DIM - ROT_DIM,), F32)])
    freq = jnp.concatenate([freq64, freq64])[None, :]
    dim = (jnp.arange(LANES) % HEAD_DIM)[None, :]
    ang = jnp.arange(s).astype(F32)[:, None] * freq
    cos, sin = jnp.cos(ang), jnp.sin(ang)
    return cos, jnp.where(dim < half, -sin, 0.0), jnp.where((dim >= half) & (dim < ROT_DIM), sin, 0.0)


def _pad_rows(a, rows):
    return jnp.pad(a, ((0, rows - a.shape[0]), (0, 0)))


def _pad_cols(a, cols):
    return jnp.pad(a, ((0, 0), (0, cols - a.shape[1])))


def kernel(x, p, norm_gain, w_in, q_norm_gain, k_norm_gain, attn_sinks, conv_w, w_out, ple_gate_norm_gain, w_ple_gate, b_ple_gate, w_ple_proj, ple_norm_gain, loss_target, m_norm_gain, m_w_in, m_q_norm_gain, m_k_norm_gain, m_attn_sinks, m_conv_w, m_w_out, m_ple_gate_norm_gain, m_w_ple_gate, m_b_ple_gate, m_w_ple_proj, m_ple_norm_gain, v_norm_gain, v_w_in, v_q_norm_gain, v_k_norm_gain, v_attn_sinks, v_conv_w, v_w_out, v_ple_gate_norm_gain, v_w_ple_gate, v_b_ple_gate, v_w_ple_proj, v_ple_norm_gain):
    x2, p2, tgt = x[0], p[0, 0], loss_target[0]
    s, d = x2.shape
    ple = p2.shape[1]
    aw = d // 2
    cw = d - aw
    nq = aw // HEAD_DIM
    sh = w_in.shape[2]
    in_w = N_CHIPS * sh
    dq = d // N_CHIPS
    cq = cw // N_CHIPS
    ga_off = (aw + 2 * KV_WIDTH) // COLT
    b_off = (2 * aw + 2 * KV_WIDTH) // COLT
    assert in_w == 2 * aw + 2 * KV_WIDTH + 4 * cw and aw % COLT == 0 and cw % COLT == 0
    assert s % BLOCK == 0 and sh % LANES == 0 and nq % (2 * N_KV_HEADS) == 0

    core = lax.axis_index("c").astype(jnp.int32).reshape(1)
    chip = 2 * lax.axis_index("x") + lax.axis_index("y")

    chip1 = chip.astype(jnp.int32).reshape(1)
    place = jnp.concatenate([chip1, core])
    w_in_own = cast_bf16(w_in[0], "cast_w_in")
    rest = [cast_into_slot(w_out[0], chip1, "cast_w_out"), cast_into_slot(w_ple_gate[0], chip1, "cast_w_pg"),
            cast_into_slot(w_ple_proj[0], chip1, "cast_w_pp"),
            lax.dynamic_update_slice(jnp.zeros((N_CHIPS, SUBLANES, cq), F32),
                                     _pad_rows(conv_w[0], SUBLANES)[None], (chip, 0, 0))]
    order = jnp.stack([chip, chip ^ 2, chip ^ 1, chip ^ 3]).astype(jnp.int32)
    wi_sems, wi_arrs, wi_token = exchange_start(
        "gather_wi_start", [(plan_shard_ici(j), [0, 1 + j], 1) for j in range(2)],
        [w_in_own] + [lax.empty((d, sh), BF16) for _ in range(3)])

    tm = _tile(s, 1024)
    tn = _tile(d, 1024)
    tk = _tile(d, 2048)
    ts = _tile(s, 2048)
    h = rms_fwd(x2, norm_gain, "rms_fwd_x", after=(wi_token,))
    tabs = _rope_tables(s)
    qg = jnp.tile(q_norm_gain, (1, LANES // HEAD_DIM))
    kg = jnp.tile(k_norm_gain, (1, LANES // HEAD_DIM))
    seg_i = jnp.arange(SEG) // HEAD_DIM
    segb = (seg_i[:, None] == seg_i[None, :]).astype(BF16)
    z = in_proj_part(h, wi_arrs[0], None, order, 0, in_w)
    own, land_x = exchange_wait("gather_wi_wait0", plan_shard_ici(0), wi_sems[0], [wi_arrs[0], wi_arrs[1]], z)
    own, land_y = exchange_wait("gather_wi_wait1", plan_shard_ici(1), wi_sems[1], [own, wi_arrs[2]], land_x)
    relay_sems, relayed, relay_token = exchange_start(
        "gather_wi_relay_start", [(plan_shard_relay, [0, 1, 2], 2)], [land_x, land_y, wi_arrs[3]])
    rest_sems, rest, rest_token = exchange_start(
        "gather_rest_start", [(plan_gather_ici(3), [0, 1, 2, 3], 12)], rest, after=(relay_token,))
    land_x, = exchange("gather_wi_pass0", plan_shard_pass, [relayed[0]], 1)
    z = in_proj_part(h, land_x, z, order, 1, in_w)
    land_y, = exchange("gather_wi_pass1", plan_shard_pass, [relayed[1]], 1)
    z = in_proj_part(h, land_y, z, order, 2, in_w)
    land_x, land_y, land_far = exchange_wait("gather_wi_relay_wait", plan_shard_relay, relay_sems[0],
                                             [land_x, land_y, relayed[2]], z)
    land_far, = exchange("gather_wi_pass2", plan_shard_pass, [land_far], 1)
    z = in_proj_part(h, land_far, z, order, 3, in_w)
    w_shards = [own, land_x, land_y, land_far]
    wo_all, wg_all, wp_all, conv_all = exchange_wait("gather_rest_wait", plan_gather_ici(3), rest_sems[0], rest, z)
    pass_sems, passed, pass_token = exchange_start(
        "gather_rest_pass_start", [(plan_gather_pass, [0, 1, 2], 9)], [wo_all, wg_all, wp_all])
    conv_full = conv_all.transpose(1, 0, 2).reshape(SUBLANES, cw)
    qs, ks, vb = qk_prep_fwd(z, tabs, qg, kg, segb, aw, after=(pass_token,))
    attn, mix = attn_fwd(qs, ks, vb, z, attn_sinks, aw, d, ga_off)
    mix = conv_fwd(z, conv_full, mix, aw, cw, b_off)
    wo_all, wg_all, wp_all = exchange_wait("gather_rest_pass_wait", plan_gather_pass, pass_sems[0], passed, mix)
    wo_full = wo_all.reshape(d, d)
    wg_full = wg_all.reshape(d, d)
    sq = lambda shape: dict(
        a_spec=pl.BlockSpec((tm, tk), lambda i, j, k: (i, k)),
        o_spec=pl.BlockSpec((tm, tn), lambda i, j, k: (i, j)),
        out_shape=jax.ShapeDtypeStruct(shape, F32))
    x1 = matmul(mix, wo_full, grid=(s // tm, d // tn, d // tk),
                b_spec=pl.BlockSpec((tk, tn), lambda i, j, k: (k, j)), dims=NN, name="mm_x1",
                res=x2, res_spec=pl.BlockSpec((tm, tn), lambda i, j, k: (i, j)), **sq((s, d)))
    hg = rms_fwd(x1, ple_gate_norm_gain, "rms_fwd_x1")
    gl = matmul(hg, wg_full, grid=(s // tm, d // tn, d // tk),
                b_spec=pl.BlockSpec((tk, tn), lambda i, j, k: (k, j)), dims=NN, name="mm_gl", **sq((s, d)))
    pq = d // N_CHIPS

    d_gl, d_pe, dy, acc_head = head_fwd_bwd(x1, gl, p2, wp_all, tgt, b_ple_gate, ple_norm_gain)
    d_hg = matmul(d_gl, wg_full, grid=(s // tm, d // tn, d // tk),
                  b_spec=pl.BlockSpec((tn, tk), lambda i, j, k: (j, k)), dims=NT, name="mm_d_hg", **sq((s, d)))
    wgrad = lambda a_cols, shape3, o_spec, b_cols, name, a, b, grid: matmul(
        a, b, grid=grid,
        a_spec=pl.BlockSpec((ts, a_cols), lambda i, j, k: (k, i)),
        b_spec=pl.BlockSpec((ts, b_cols), lambda i, j, k: (k, j)),
        o_spec=o_spec, out_shape=jax.ShapeDtypeStruct(shape3, BF16), dims=TN, name=name)
    g_wg = wgrad(tn, (d, d), pl.BlockSpec((tn, tn), lambda i, j, k: (i, j)), tn, "mm_g_wg", hg, d_gl,
                 (d // tn, d // tn, s // ts))
    g_wp = wgrad(ple, (N_CHIPS, ple, pq), pl.BlockSpec((None, ple, pq), lambda i, j, k: (j, 0, 0)), pq,
                 "mm_g_wp", p2, d_pe, (1, N_CHIPS, s // ts))
    d_x1, d_x1b, acc_g = rms_bwd_add(d_hg, x1, dy, ple_gate_norm_gain, "rms_bwd_x1", True)
    d_mix = matmul(d_x1b, wo_full, grid=(s // tm, d // tn, d // tk),
                   b_spec=pl.BlockSpec((tn, tk), lambda i, j, k: (j, k)), dims=NT, name="mm_d_mix", **sq((s, d)))
    g_wo = wgrad(tn, (d, d), pl.BlockSpec((tn, tn), lambda i, j, k: (i, j)), tn, "mm_g_wo", mix, d_x1b,
                 (d // tn, d // tn, s // ts))
    def reduce_start(tag, grads):
        n = len(grads)
        lands = [lax.empty((N_CHIPS, a.shape[1] // 2, a.shape[2]), BF16) for a in grads]
        swapped = exchange("swap_" + tag, plan_swap, list(grads) + lands, n)
        parts = [add_sibling(g, o, core, "add_sibling_%s%d" % (tag, i))
                 for i, (g, o) in enumerate(zip(swapped[:n], swapped[n:]))]
        return exchange_start("scatter_%s_start" % tag, [(plan_scatter, list(range(2 * n)), 3 * n)],
                              parts + [lax.empty(a.shape, BF16) for a in parts])

    def reduce_sum(tag, handle, after):
        sems, arrays, _ = handle
        n = len(arrays) // 2
        got = exchange_wait("scatter_%s_wait" % tag, plan_scatter, sems[0], arrays, after)
        return [sum_chips(pt, o, place, "sum_chips_%s%d" % (tag, i))
                for i, (pt, o) in enumerate(zip(got[:n], got[n:]))]

    early = reduce_start("early", [g_wo.reshape(N_CHIPS, dq, d), g_wg.reshape(N_CHIPS, dq, d), g_wp])
    d_o, dz = gate_bwd(d_mix, attn, z, aw, ga_off, after=(early[-1],))
    dqs, dks, dvs, acc_sink = attn_bwd(qs, ks, vb, d_o, attn_sinks, aw)
    dz, acc_qk = qk_prep_bwd(z, dqs, dks, dvs, tabs, qg, kg, segb, dz, aw)
    dz, acc_conv = conv_bwd(z, d_mix, conv_full, dz, aw, cw, b_off)
    join_sems, joining, join_token = exchange_start(
        "join_early_start", [(plan_join, [0, 1, 2], 3)], reduce_sum("early", early, dz))
    g_wi = matmul(h, dz, grid=(d // tn, N_CHIPS, 1),
                  a_spec=pl.BlockSpec((s, tn), lambda i, j, k: (0, i), pipeline_mode=pl.Buffered(1)),
                  b_spec=pl.BlockSpec((s, sh), lambda i, j, k: (0, j)),
                  o_spec=pl.BlockSpec((None, tn, sh), lambda i, j, k: (j, i, 0)),
                  out_shape=jax.ShapeDtypeStruct((N_CHIPS, d, sh), BF16), dims=TN, name="mm_g_wi",
                  after=(join_token,), vmem=VMEM_LIMIT_BIG)
    full_wo, full_wg, full_wp = exchange_wait("join_early_wait", plan_join, join_sems[0], joining, g_wi)
    late = reduce_start("late", [g_wi])
    d_h = in_proj_bwd(dz, w_shards, order, d, after=(late[-1],))
    grad_x, acc_x = rms_bwd_add(d_h, x2, d_x1, norm_gain, "rms_bwd_x", False)

    wsm = max(d, cw)
    misc = jnp.concatenate([acc_qk[0:1, :HEAD_DIM], acc_qk[1:2, :HEAD_DIM], acc_sink[0:1, :nq]], axis=1)
    small = jnp.concatenate([
        _pad_cols(acc_x[0:1], wsm), _pad_cols(acc_g[0:1], wsm), _pad_cols(acc_head[0:1], wsm),
        _pad_cols(acc_head[1:2], wsm), _pad_cols(misc, wsm), _pad_cols(acc_conv[0:3], wsm),
        _pad_rows(_pad_cols(acc_head[2:3], wsm), SUBLANES)], axis=0)
    device = 2 * chip + lax.axis_index("c")
    small_sems, small_bufs, small_token = exchange_start(
        "small_start", [(plan_allgather_small, [0], 7)],
        [lax.dynamic_update_slice(jnp.zeros((8,) + small.shape, F32), small[None], (device, 0, 0))])
    last_sems, last_halves, last_token = exchange_start(
        "join_late_start", [(plan_join, [0], 1)], reduce_sum("late", late, small_token))
    big, after = {}, (last_token,)
    for nm, g, w, m, v in (("w_out", full_wo, w_out, m_w_out, v_w_out),
                           ("w_ple_gate", full_wg, w_ple_gate, m_w_ple_gate, v_w_ple_gate),
                           ("w_ple_proj", full_wp, w_ple_proj, m_w_ple_proj, v_w_ple_proj)):
        stepped = adamw(g, w[0], m[0], v[0], "adamw_" + nm, after=after)
        big[nm], after = [o[None] for o in stepped], (stepped[1],)
    full_wi, = exchange_wait("join_late_wait", plan_join, last_sems[0], last_halves, after[0])
    big["w_in"] = [o[None] for o in adamw(full_wi, w_in[0], m_w_in[0], v_w_in[0], "adamw_w_in")]

    gathered, = exchange_wait("small_wait", plan_allgather_small, small_sems[0], small_bufs, big["w_in"][1])
    tot = sum_devices(gathered)
    loss = tot[SUBLANES, 0]

    def pack(vals):
        ng, pg, bg, eg, qgv, kgv, sk, cv = vals
        misc_v = jnp.concatenate([qgv, kgv, sk], axis=1)
        return jnp.concatenate([_pad_cols(ng, wsm), _pad_cols(pg, wsm), _pad_cols(bg, wsm), _pad_cols(eg, wsm),
                                _pad_cols(misc_v, wsm), _pad_cols(cv[0], wsm)], axis=0)

    g_small = jnp.concatenate(
        [tot[0:5], _pad_cols(lax.dynamic_slice(tot[5:8], (0, chip * cq), (3, cq)), wsm)], axis=0)
    w_small = pack((norm_gain, ple_gate_norm_gain, b_ple_gate, ple_norm_gain, q_norm_gain, k_norm_gain,
                    attn_sinks, conv_w))
    m_small = pack((m_norm_gain, m_ple_gate_norm_gain, m_b_ple_gate, m_ple_norm_gain, m_q_norm_gain,
                    m_k_norm_gain, m_attn_sinks, m_conv_w))
    v_small = pack((v_norm_gain, v_ple_gate_norm_gain, v_b_ple_gate, v_ple_norm_gain, v_q_norm_gain,
                    v_k_norm_gain, v_attn_sinks, v_conv_w))
    sm = adamw(g_small, w_small, m_small, v_small, "adamw_small")

    def unpack(a):
        return {"norm_gain": a[0:1, :d], "ple_gate_norm_gain": a[1:2, :d], "b_ple_gate": a[2:3, :d],
                "ple_norm_gain": a[3:4, :d], "q_norm_gain": a[4:5, :HEAD_DIM],
                "k_norm_gain": a[4:5, HEAD_DIM:2 * HEAD_DIM],
                "attn_sinks": a[4:5, 2 * HEAD_DIM:2 * HEAD_DIM + nq], "conv_w": a[5:8, :cq][None]}

    order = ("norm_gain", "w_in", "q_norm_gain", "k_norm_gain", "attn_sinks", "conv_w", "w_out",
             "ple_gate_norm_gain", "w_ple_gate", "b_ple_gate", "w_ple_proj", "ple_norm_gain")
    outs = [loss, grad_x[None]]
    for kind in range(4):
        table = unpack(sm[kind])
        for nm in order:
            outs.append(big[nm][kind] if nm in big else table[nm])
    return tuple(outs)
```

```python
import functools

import jax
import jax.numpy as jnp
from jax import lax
from jax.experimental import pallas as pl
from jax.experimental.pallas import tpu as pltpu

F32 = jnp.float32
BF16 = jnp.bfloat16
MESH = pl.DeviceIdType.MESH

HEAD_DIM = 64
N_KV_HEADS = 4
KV_WIDTH = N_KV_HEADS * HEAD_DIM
BLOCK = 128
ROT_DIM = 16
ROPE_THETA = 500000.0
EPS = 1e-6
NEG_INF = -1e30
N_CHIPS = 4
LANES = 128
SUBLANES = 8
COLT = 512
SEG = 256
VMEM_LIMIT = 48 * 1024 * 1024
VMEM_LIMIT_BIG = 60 * 1024 * 1024

ADAM_LR = 0.001
ADAM_B1 = 0.9
ADAM_B2 = 0.999
ADAM_EPS = 1e-08
ADAM_WD = 0.01
ADAM_STEP = 10

NN = (((1,), (0,)), ((), ()))
NT = (((1,), (1,)), ((), ()))
TN = (((0,), (0,)), ((), ()))


def _call(body, **kw):
    return pl.pallas_call(body, **kw)


def _call_after(body, after, **kw):
    n_in, n_after = len(kw["in_specs"]), len(after)
    kw["in_specs"] = list(kw["in_specs"]) + [pl.BlockSpec(memory_space=pl.ANY)] * n_after

    def body_after(*refs):
        body(*refs[:n_in], *refs[n_in + n_after:])

    call = _call(body_after, **kw)
    return lambda *args: call(*args, *after)


def _params(sem, vmem=VMEM_LIMIT):
    return pltpu.CompilerParams(dimension_semantics=sem, vmem_limit_bytes=vmem)


def _tile(n, pref):
    return pref if n % pref == 0 else n


def _sigmoid(v):
    return 1.0 / (1.0 + jnp.exp(-v))


def _hbm():
    return pl.BlockSpec(memory_space=pl.ANY)


def cast_bf16(a, name):
    r, c = a.shape
    tr = _tile(r, 512)

    def body(a_ref, o_ref):
        o_ref[...] = a_ref[...].astype(BF16)

    return _call(body, name=name, grid=(r // tr,),
                 in_specs=[pl.BlockSpec((tr, c), lambda i: (i, 0))],
                 out_specs=pl.BlockSpec((tr, c), lambda i: (i, 0)),
                 out_shape=jax.ShapeDtypeStruct((r, c), BF16),
                 compiler_params=_params(("parallel",)))(a)


def cast_into_slot(a, chip, name):
    r, c = a.shape
    tr = _tile(r, 512)

    def body(chip_ref, a_ref, o_ref):
        o_ref[...] = a_ref[...].astype(BF16)

    grid_spec = pltpu.PrefetchScalarGridSpec(
        num_scalar_prefetch=1, grid=(r // tr,),
        in_specs=[pl.BlockSpec((tr, c), lambda i, chip_ref: (i, 0))],
        out_specs=pl.BlockSpec((None, tr, c), lambda i, chip_ref: (chip_ref[0], i, 0)))
    return _call(body, name=name, grid_spec=grid_spec,
                 out_shape=jax.ShapeDtypeStruct((N_CHIPS, r, c), BF16),
                 compiler_params=_params(("parallel",)))(chip, a)


def rms_fwd(x, gain, name, after=()):
    s, d = x.shape
    tm = _tile(s, 512)

    def body(x_ref, g_ref, o_ref):
        xf = x_ref[...]
        r = lax.rsqrt(jnp.mean(xf * xf, axis=-1, keepdims=True) + EPS)
        o_ref[...] = ((xf * r) * g_ref[...]).astype(BF16)

    return _call_after(body, after, name=name, grid=(s // tm,),
                       in_specs=[pl.BlockSpec((tm, d), lambda i: (i, 0)),
                                 pl.BlockSpec((1, d), lambda i: (0, 0))],
                       out_specs=pl.BlockSpec((tm, d), lambda i: (i, 0)),
                       out_shape=jax.ShapeDtypeStruct((s, d), BF16),
                       compiler_params=_params(("parallel",)))(x, gain)


def rms_bwd_add(dyn, xin, add, gain, name, want_bf16):
    s, d = xin.shape
    tm = _tile(s, 512)

    def body(dy_ref, x_ref, a_ref, g_ref, *outs):
        i = pl.program_id(0)
        dx_ref, acc_ref = outs[0], outs[-1]
        xf = x_ref[...]
        r = lax.rsqrt(jnp.mean(xf * xf, axis=-1, keepdims=True) + EPS)
        xhat = xf * r
        dyv = dy_ref[...]
        gd = dyv * g_ref[...]
        dx = a_ref[...] + r * (gd - xhat * jnp.mean(xhat * gd, axis=-1, keepdims=True))
        dx_ref[...] = dx
        if want_bf16:
            outs[1][...] = dx.astype(BF16)

        @pl.when(i == 0)
        def _():
            acc_ref[...] = jnp.zeros_like(acc_ref)

        acc_ref[0:1, :] += jnp.sum(dyv * xhat, axis=0, keepdims=True)

    row = pl.BlockSpec((tm, d), lambda i: (i, 0))
    out_specs = [row] + ([row] if want_bf16 else []) + [pl.BlockSpec((SUBLANES, d), lambda i: (0, 0))]
    out_shape = ([jax.ShapeDtypeStruct((s, d), F32)]
                 + ([jax.ShapeDtypeStruct((s, d), BF16)] if want_bf16 else [])
                 + [jax.ShapeDtypeStruct((SUBLANES, d), F32)])
    return _call(body, name=name, grid=(s // tm,),
                 in_specs=[row, row, row, pl.BlockSpec((1, d), lambda i: (0, 0))],
                 out_specs=out_specs, out_shape=out_shape,
                 compiler_params=_params(("arbitrary",), VMEM_LIMIT_BIG))(dyn, xin, add, gain)


def head_fwd_bwd(x1, gl, p, wp, tgt, bias, ple_gain):
    s, d = x1.shape
    tm = _tile(s, 256)

    def body(x1_ref, gl_ref, p_ref, wp_ref, t_ref, b_ref, g_ref, dgl_ref, dpe_ref, dy_ref, acc_ref):
        i = pl.program_id(0)
        gate = _sigmoid(gl_ref[...] + b_ref[...])
        pb = p_ref[...].astype(BF16)
        pev = jnp.concatenate([jnp.dot(pb, wp_ref[q], preferred_element_type=F32) for q in range(N_CHIPS)],
                              axis=1)
        r = lax.rsqrt(jnp.mean(pev * pev, axis=-1, keepdims=True) + EPS)
        pehat = pev * r
        gain = g_ref[...]
        e = pehat * gain
        diff = (x1_ref[...] + gate * e) - t_ref[...]
        dy = diff * (1.0 / d)
        dy_ref[...] = dy
        d_gl = (dy * e) * (gate * (1.0 - gate))
        dgl_ref[...] = d_gl.astype(BF16)
        d_e = dy * gate
        gd = d_e * gain
        d_pe = r * (gd - pehat * jnp.mean(pehat * gd, axis=-1, keepdims=True))
        dpe_ref[...] = d_pe.astype(BF16)

        @pl.when(i == 0)
        def _():
            acc_ref[...] = jnp.zeros_like(acc_ref)

        acc_ref[0:1, :] += jnp.sum(d_gl, axis=0, keepdims=True)
        acc_ref[1:2, :] += jnp.sum(d_e * pehat, axis=0, keepdims=True)
        acc_ref[2:3, :] += jnp.sum(diff * diff, axis=0, keepdims=True) * (0.5 / d)

    row = pl.BlockSpec((tm, d), lambda i: (i, 0))
    vec = pl.BlockSpec((1, d), lambda i: (0, 0))
    return _call(body, name="head_fwd_bwd", grid=(s // tm,),
                 in_specs=[row, row, pl.BlockSpec((tm, p.shape[1]), lambda i: (i, 0)),
                           pl.BlockSpec(wp.shape, lambda i: (0, 0, 0)), row, vec, vec],
                 out_specs=[row, row, row, pl.BlockSpec((SUBLANES, d), lambda i: (0, 0))],
                 out_shape=[jax.ShapeDtypeStruct((s, d), BF16), jax.ShapeDtypeStruct((s, d), BF16),
                            jax.ShapeDtypeStruct((s, d), F32), jax.ShapeDtypeStruct((SUBLANES, d), F32)],
                 compiler_params=_params(("arbitrary",)))(x1, gl, p, wp, tgt, bias, ple_gain)


def adamw(g, w, m, v, name, after=()):
    r, c = g.shape
    tr = _tile(r, 256)

    def body(g_ref, w_ref, m_ref, v_ref, go_ref, d_ref, mo_ref, vo_ref):
        gv = g_ref[...]
        mn = ADAM_B1 * m_ref[...] + (1.0 - ADAM_B1) * gv
        vn = ADAM_B2 * v_ref[...] + (1.0 - ADAM_B2) * (gv * gv)
        m_hat = mn / (1.0 - ADAM_B1 ** ADAM_STEP)
        v_hat = vn / (1.0 - ADAM_B2 ** ADAM_STEP)
        go_ref[...] = gv
        d_ref[...] = -ADAM_LR * (m_hat / (jnp.sqrt(v_hat) + ADAM_EPS) + ADAM_WD * w_ref[...])
        mo_ref[...] = mn
        vo_ref[...] = vn

    blk = pl.BlockSpec((tr, c), lambda i: (i, 0))
    shp = jax.ShapeDtypeStruct((r, c), F32)
    return _call_after(body, after, name=name, grid=(r // tr,), in_specs=[blk] * 4, out_specs=[blk] * 4,
                       out_shape=[shp] * 4, compiler_params=_params(("parallel",)))(g, w, m, v)


def matmul(a, b, *, grid, a_spec, b_spec, o_spec, out_shape, dims, name, res=None, res_spec=None, after=(),
           vmem=VMEM_LIMIT):
    nk = grid[2]
    acc_shape = tuple(d for d in o_spec.block_shape if d is not None)

    def body(*refs):
        a_ref, b_ref = refs[:2]
        r_ref = refs[2] if res is not None else None
        o_ref = refs[3] if res is not None else refs[2]
        part = lax.dot_general(a_ref[...].astype(BF16), b_ref[...].astype(BF16), dims, preferred_element_type=F32)

        def finish(out):
            if res is not None:
                out = r_ref[...] + out
            o_ref[...] = out.astype(o_ref.dtype)

        if nk == 1:
            finish(part)
            return
        acc_ref = refs[-1]
        k = pl.program_id(2)

        @pl.when(k == 0)
        def _():
            acc_ref[...] = part

        @pl.when((k > 0) & (k < nk - 1))
        def _():
            acc_ref[...] += part

        @pl.when(k == nk - 1)
        def _():
            finish(acc_ref[...] + part)

    in_specs = [a_spec, b_spec] + ([res_spec] if res is not None else [])
    args = (a, b) + ((res,) if res is not None else ())
    return _call_after(body, after, name=name, grid=grid, in_specs=in_specs, out_specs=o_spec,
                       out_shape=out_shape, scratch_shapes=[pltpu.VMEM(acc_shape, F32)] if nk > 1 else [],
                       compiler_params=_params(("parallel", "parallel", "arbitrary"), vmem))(*args)


def in_proj_part(h, w, z_prev, order, q, in_w):
    s, d = h.shape
    sh = w.shape[1]
    tm = _tile(s, 512)

    def body(order_ref, h_ref, w_ref, *rest):
        rest[-1][...] = jnp.dot(h_ref[...], w_ref[...], preferred_element_type=F32)

    in_specs = [pl.BlockSpec((tm, d), lambda i, order_ref: (i, 0)),
                pl.BlockSpec((d, sh), lambda i, order_ref: (0, 0))]
    args = [order, h, w]
    if z_prev is not None:
        in_specs.append(_hbm())
        args.append(z_prev)
    grid_spec = pltpu.PrefetchScalarGridSpec(
        num_scalar_prefetch=1, grid=(s // tm,), in_specs=in_specs,
        out_specs=pl.BlockSpec((tm, sh), lambda i, order_ref: (i, order_ref[q])))
    return _call(body, name="mm_z%d" % q, grid_spec=grid_spec,
                 out_shape=jax.ShapeDtypeStruct((s, in_w), F32),
                 input_output_aliases={3: 0} if z_prev is not None else {},
                 compiler_params=_params(("parallel",)))(*args)


def in_proj_bwd(dz, ws, order, d, after):
    s = dz.shape[0]
    sh = ws[0].shape[1]
    tm, tn = _tile(s, 512), _tile(d, 1024)
    nq, n_after = len(ws), len(after)

    def body(order_ref, *refs):
        a_refs, b_refs, o_ref = refs[:nq], refs[nq:2 * nq], refs[2 * nq + n_after]
        acc = lax.dot_general(a_refs[0][...], b_refs[0][...], NT, preferred_element_type=F32)
        for q in range(1, nq):
            acc += lax.dot_general(a_refs[q][...], b_refs[q][...], NT, preferred_element_type=F32)
        o_ref[...] = acc

    a_spec = lambda q: pl.BlockSpec((tm, sh), functools.partial(lambda i, j, order_ref, q: (i, order_ref[q]), q=q))
    grid_spec = pltpu.PrefetchScalarGridSpec(
        num_scalar_prefetch=1, grid=(s // tm, d // tn),
        in_specs=[a_spec(q) for q in range(nq)]
        + [pl.BlockSpec((tn, sh), lambda i, j, order_ref: (j, 0))] * nq + [_hbm()] * n_after,
        out_specs=pl.BlockSpec((tm, tn), lambda i, j, order_ref: (i, j)))
    return _call(body, name="mm_d_h", grid_spec=grid_spec, out_shape=jax.ShapeDtypeStruct((s, d), F32),
                 compiler_params=_params(("parallel", "parallel"), VMEM_LIMIT_BIG))(
                     order, *([dz] * nq), *ws, *after)


def _seg_mean(sq, segb):
    parts = []
    for cgrp in range(sq.shape[1] // SEG):
        blk = sq[:, cgrp * SEG:(cgrp + 1) * SEG]
        hi = blk.astype(BF16)
        r1 = blk - hi.astype(F32)
        mid = r1.astype(BF16)
        lo = (r1 - mid.astype(F32)).astype(BF16)
        acc = jnp.dot(hi, segb, preferred_element_type=F32)
        acc += jnp.dot(mid, segb, preferred_element_type=F32)
        acc += jnp.dot(lo, segb, preferred_element_type=F32)
        parts.append(acc)
    out = parts[0] if len(parts) == 1 else jnp.concatenate(parts, axis=1)
    return out * (1.0 / HEAD_DIM)


def _rope(v, cos, sa, sb):
    parts = []
    for cgrp in range(v.shape[1] // LANES):
        blk = v[:, cgrp * LANES:(cgrp + 1) * LANES]
        parts.append(blk * cos + pltpu.roll(blk, LANES - 8, 1) * sa + pltpu.roll(blk, 8, 1) * sb)
    return parts[0] if len(parts) == 1 else jnp.concatenate(parts, axis=1)


def _rope_t(dv, cos, sa, sb):
    parts = []
    for cgrp in range(dv.shape[1] // LANES):
        blk = dv[:, cgrp * LANES:(cgrp + 1) * LANES]
        parts.append(blk * cos + pltpu.roll(blk * sa, 8, 1) + pltpu.roll(blk * sb, LANES - 8, 1))
    return parts[0] if len(parts) == 1 else jnp.concatenate(parts, axis=1)


def _tile_lanes(vec, width):
    reps = width // LANES
    return vec if reps == 1 else jnp.tile(vec, (1, reps))


def qk_prep_fwd(z, tabs, qg, kg, segb, aw, after=()):
    s = z.shape[0]
    tm = _tile(s, 512)
    wq = aw + 2 * KV_WIDTH

    def body(z_ref, cos_ref, sa_ref, sb_ref, qg_ref, kg_ref, seg_ref, qs_ref, ks_ref, vb_ref):
        zz = z_ref[...]
        q, k, v = zz[:, :aw], zz[:, aw:aw + KV_WIDTH], zz[:, aw + KV_WIDTH:]
        cos, sa, sb, segm = cos_ref[...], sa_ref[...], sb_ref[...], seg_ref[...]
        rq = lax.rsqrt(_seg_mean(q * q, segm) + EPS)
        qn = (q * rq) * _tile_lanes(qg_ref[...], aw)
        qs_ref[...] = (_rope(qn, cos, sa, sb) * (HEAD_DIM ** -0.5)).astype(BF16)
        rk = lax.rsqrt(_seg_mean(k * k, segm) + EPS)
        kn = (k * rk) * _tile_lanes(kg_ref[...], KV_WIDTH)
        ks_ref[...] = _rope(kn, cos, sa, sb).astype(BF16)
        vb_ref[...] = v.astype(BF16)

    tab = pl.BlockSpec((tm, LANES), lambda i: (i, 0))
    vec = pl.BlockSpec((1, LANES), lambda i: (0, 0))
    return _call_after(body, after, name="qk_prep_fwd", grid=(s // tm,),
                       in_specs=[pl.BlockSpec((tm, wq), lambda i: (i, 0)), tab, tab, tab, vec, vec,
                                 pl.BlockSpec((SEG, SEG), lambda i: (0, 0))],
                       out_specs=[pl.BlockSpec((tm, aw), lambda i: (i, 0)),
                                  pl.BlockSpec((tm, KV_WIDTH), lambda i: (i, 0)),
                                  pl.BlockSpec((tm, KV_WIDTH), lambda i: (i, 0))],
                       out_shape=[jax.ShapeDtypeStruct((s, aw), BF16), jax.ShapeDtypeStruct((s, KV_WIDTH), BF16),
                                  jax.ShapeDtypeStruct((s, KV_WIDTH), BF16)],
                       compiler_params=_params(("parallel",)))(z, *tabs, qg, kg, segb)


def qk_prep_bwd(z, dqs, dks, dvs, tabs, qg, kg, segb, dz, aw):
    s = z.shape[0]
    tm = _tile(s, 512)
    wq = aw + 2 * KV_WIDTH

    def body(z_ref, dq_ref, dk_ref, dv_ref, cos_ref, sa_ref, sb_ref, qg_ref, kg_ref, seg_ref, dz_in,
             dz_ref, acc_ref):
        i = pl.program_id(0)
        zz = z_ref[...]
        q, k = zz[:, :aw], zz[:, aw:aw + KV_WIDTH]
        cos, sa, sb, segm = cos_ref[...], sa_ref[...], sb_ref[...], seg_ref[...]

        def one(xv, dout, gvec, width):
            g = _tile_lanes(gvec, width)
            r = lax.rsqrt(_seg_mean(xv * xv, segm) + EPS)
            xhat = xv * r
            dn = _rope_t(dout, cos, sa, sb)
            gd = dn * g
            dx = r * (gd - xhat * _seg_mean(xhat * gd, segm))
            contrib = jnp.sum(dn * xhat, axis=0, keepdims=True)
            folded = contrib[:, :LANES]
            for cgrp in range(1, width // LANES):
                folded = folded + contrib[:, cgrp * LANES:(cgrp + 1) * LANES]
            return dx, folded + pltpu.roll(folded, HEAD_DIM, 1)

        dq, gq = one(q, dq_ref[...] * (HEAD_DIM ** -0.5), qg_ref[...], aw)
        dk, gk = one(k, dk_ref[...], kg_ref[...], KV_WIDTH)
        dz_ref[:, :aw] = dq.astype(BF16)
        dz_ref[:, aw:aw + KV_WIDTH] = dk.astype(BF16)
        dz_ref[:, aw + KV_WIDTH:] = dv_ref[...].astype(BF16)

        @pl.when(i == 0)
        def _():
            acc_ref[...] = jnp.zeros_like(acc_ref)

        acc_ref[0:1, :] += gq
        acc_ref[1:2, :] += gk

    tab = pl.BlockSpec((tm, LANES), lambda i: (i, 0))
    vec = pl.BlockSpec((1, LANES), lambda i: (0, 0))
    kvb = pl.BlockSpec((tm, KV_WIDTH), lambda i: (i, 0))
    return _call(body, name="qk_prep_bwd", grid=(s // tm,),
                 in_specs=[pl.BlockSpec((tm, wq), lambda i: (i, 0)),
                           pl.BlockSpec((tm, aw), lambda i: (i, 0)), kvb, kvb, tab, tab, tab, vec, vec,
                           pl.BlockSpec((SEG, SEG), lambda i: (0, 0)), _hbm()],
                 out_specs=[pl.BlockSpec((tm, wq), lambda i: (i, 0)),
                            pl.BlockSpec((SUBLANES, LANES), lambda i: (0, 0))],
                 out_shape=[jax.ShapeDtypeStruct(dz.shape, BF16), jax.ShapeDtypeStruct((SUBLANES, LANES), F32)],
                 input_output_aliases={10: 0},
                 compiler_params=_params(("arbitrary",)))(z, dqs, dks, dvs, *tabs, qg, kg, segb, dz)


def _placed(band, lane_idx):
    out = {}
    for kh in range(N_KV_HEADS):
        grp = band[:, (kh // 2) * LANES:(kh // 2 + 1) * LANES]
        for half in (0, 1):
            t = grp if half == kh % 2 else pltpu.roll(grp, HEAD_DIM, 1)
            keep = (lane_idx >= half * HEAD_DIM) & (lane_idx < (half + 1) * HEAD_DIM)
            out[kh, half] = jnp.where(keep, t, jnp.zeros_like(t))
    return out


def _softmax_with_sink(sc, valid, sink):
    sc = jnp.where(valid, sc, NEG_INF)
    m = jnp.maximum(jnp.max(sc, axis=-1, keepdims=True), sink)
    e = jnp.exp(sc - m)
    es = jnp.exp(sink - m)
    den = jnp.sum(e, axis=-1, keepdims=True) + es
    return e / den, es / den


def _stack_halves(placed, kh):
    return jnp.concatenate([placed[kh, 0], placed[kh, 1]], axis=0)


def _valid_mask(n):
    r_i = lax.broadcasted_iota(jnp.int32, (BLOCK, 2 * BLOCK), 0)
    j_i = lax.broadcasted_iota(jnp.int32, (BLOCK, 2 * BLOCK), 1)
    return (j_i > r_i) & (j_i <= r_i + BLOCK) & ((n > 0) | (j_i >= BLOCK))


def attn_fwd(qs, ks, vb, z, sinks, aw, d, ga_off):
    s = qs.shape[0]
    nb = s // BLOCK
    nq = aw // HEAD_DIM
    grp_sz = nq // N_KV_HEADS
    n_ga = aw // COLT

    def body(q_ref, ko_ref, kp_ref, vo_ref, vp_ref, *rest):
        ga_refs = rest[:n_ga]
        sink_ref, attn_ref, mix_ref = rest[n_ga:]
        n = pl.program_id(0)
        lane_idx = lax.broadcasted_iota(jnp.int32, (2 * BLOCK, LANES), 1)
        kpl = _placed(jnp.concatenate([kp_ref[...], ko_ref[...]], axis=0), lane_idx)
        vpl = _placed(jnp.concatenate([vp_ref[...], vo_ref[...]], axis=0), lane_idx)
        valid = _valid_mask(n)
        groups = range(nq // 2)
        kcat = [_stack_halves(kpl, kh) for kh in range(N_KV_HEADS)]
        vcat = [_stack_halves(vpl, kh) for kh in range(N_KV_HEADS)]
        scs = [lax.dot_general(q_ref[:, c * LANES:(c + 1) * LANES], kcat[2 * c // grp_sz], NT,
                               preferred_element_type=F32) for c in groups]
        probs = [jnp.concatenate(
            [_softmax_with_sink(scs[c][:, half * 2 * BLOCK:(half + 1) * 2 * BLOCK], valid,
                                sink_ref[0, 2 * c + half])[0].astype(BF16) for half in (0, 1)], axis=1)
                 for c in groups]
        for c in groups:
            acc = jnp.dot(probs[c], vcat[2 * c // grp_sz], preferred_element_type=F32)
            attn_ref[:, c * LANES:(c + 1) * LANES] = acc
            col = c * LANES
            ga = ga_refs[col // COLT][:, col % COLT:col % COLT + LANES]
            mix_ref[:, c * LANES:(c + 1) * LANES] = (acc * (ga * _sigmoid(ga))).astype(BF16)

    own = lambda n: (n, 0)
    prev = lambda n: (jnp.maximum(n - 1, 0), 0)
    kvs = lambda imap: pl.BlockSpec((BLOCK, KV_WIDTH), imap)
    ga_specs = [pl.BlockSpec((BLOCK, COLT), functools.partial(lambda n, j: (n, ga_off + j), j=j))
                for j in range(n_ga)]
    return _call(body, name="attn_fwd", grid=(nb,),
                 in_specs=[pl.BlockSpec((BLOCK, aw), own), kvs(own), kvs(prev), kvs(own), kvs(prev)]
                 + ga_specs + [pl.BlockSpec(memory_space=pltpu.SMEM)],
                 out_specs=[pl.BlockSpec((BLOCK, aw), own), pl.BlockSpec((BLOCK, aw), own)],
                 out_shape=[jax.ShapeDtypeStruct((s, aw), F32), jax.ShapeDtypeStruct((s, d), BF16)],
                 compiler_params=_params(("parallel",)))(qs, ks, ks, vb, vb, *([z] * n_ga), sinks)


def gate_bwd(d_mix, attn, z, aw, ga_off, after=()):
    s, in_w = z.shape
    tm = _tile(s, 1024)

    def body(dm_ref, at_ref, ga_ref, do_ref, dz_ref):
        ga = ga_ref[...]
        sig = _sigmoid(ga)
        dm = dm_ref[...]
        do_ref[...] = (dm * (ga * sig)).astype(BF16)
        dz_ref[...] = ((dm * at_ref[...]) * (sig * (1.0 + ga * (1.0 - sig)))).astype(BF16)

    blk = pl.BlockSpec((tm, COLT), lambda i, j: (i, j))
    gab = pl.BlockSpec((tm, COLT), lambda i, j: (i, ga_off + j))
    return _call_after(body, after, name="gate_bwd", grid=(s // tm, aw // COLT),
                       in_specs=[blk, blk, gab], out_specs=[blk, gab],
                       out_shape=[jax.ShapeDtypeStruct((s, aw), BF16), jax.ShapeDtypeStruct((s, in_w), BF16)],
                       compiler_params=_params(("parallel", "parallel")))(d_mix, attn, z)


def attn_bwd(qs, ks, vb, d_o, sinks, aw):
    s = qs.shape[0]
    nb = s // BLOCK
    nq = aw // HEAD_DIM
    grp_sz = nq // N_KV_HEADS

    def body(q_ref, ko_ref, kp_ref, vo_ref, vp_ref, do_ref, sink_ref, dq_ref, dk_ref, dv_ref, ds_ref,
             ck_ref, cv_ref):
        n = pl.program_id(0)

        @pl.when(n == 0)
        def _():
            ds_ref[...] = jnp.zeros_like(ds_ref)
            dk_ref[...] = jnp.zeros_like(dk_ref)
            dv_ref[...] = jnp.zeros_like(dv_ref)

        @pl.when(n < nb)
        def _():
            lane_idx = lax.broadcasted_iota(jnp.int32, (2 * BLOCK, LANES), 1)
            lane_row = lax.broadcasted_iota(jnp.int32, (1, LANES), 1)
            kpl = _placed(jnp.concatenate([kp_ref[...], ko_ref[...]], axis=0), lane_idx)
            vpl = _placed(jnp.concatenate([vp_ref[...], vo_ref[...]], axis=0), lane_idx)
            valid = _valid_mask(n)
            ds_row = jnp.zeros((1, LANES), F32)
            wide = 2 * BLOCK
            groups = range(nq // 2)
            per_kv = grp_sz // 2
            kcat = [_stack_halves(kpl, kh) for kh in range(N_KV_HEADS)]
            vcat = [_stack_halves(vpl, kh) for kh in range(N_KV_HEADS)]
            qg = [q_ref[:, c * LANES:(c + 1) * LANES] for c in groups]
            dog = [do_ref[:, c * LANES:(c + 1) * LANES] for c in groups]
            scs = [lax.dot_general(qg[c], kcat[c // per_kv], NT, preferred_element_type=F32) for c in groups]
            dps = [lax.dot_general(dog[c], vcat[c // per_kv], NT, preferred_element_type=F32) for c in groups]
            ds_pairs, p_pairs = [], []
            for c in groups:
                probs, dss = [], []
                for half in (0, 1):
                    h = 2 * c + half
                    cols = slice(half * wide, (half + 1) * wide)
                    p, p_sink = _softmax_with_sink(scs[c][:, cols], valid, sink_ref[0, h])
                    delta = jnp.sum(p * dps[c][:, cols], axis=-1, keepdims=True)
                    dss.append((p * (dps[c][:, cols] - delta)).astype(BF16))
                    probs.append(p.astype(BF16))
                    dsink = -jnp.sum(p_sink * delta, axis=0, keepdims=True)
                    ds_row += jnp.where(lane_row == h, dsink, 0.0)
                ds_pairs.append(jnp.concatenate(dss, axis=1))
                p_pairs.append(jnp.concatenate(probs, axis=1))
            for c in groups:
                dq_ref[:, c * LANES:(c + 1) * LANES] = jnp.dot(ds_pairs[c], kcat[c // per_kv],
                                                               preferred_element_type=F32)
            dkts = [lax.dot_general(qg[c], ds_pairs[c], TN, preferred_element_type=F32) for c in groups]
            dvts = [lax.dot_general(dog[c], p_pairs[c], TN, preferred_element_type=F32) for c in groups]
            dkt, dvt = [], []
            for kh in range(N_KV_HEADS):
                for parts, out in ((dkts, dkt), (dvts, dvt)):
                    tot = parts[kh * per_kv]
                    for c in range(kh * per_kv + 1, (kh + 1) * per_kv):
                        tot = tot + parts[c]
                    out.append(tot[:HEAD_DIM, :wide] + tot[HEAD_DIM:, wide:])
            ds_ref[0:1, :] += ds_row
            back = lambda t: jnp.concatenate(
                [jnp.concatenate(t[2 * g:2 * g + 2], axis=0).T for g in range(N_KV_HEADS // 2)], axis=1)
            dk_band, dv_band = back(dkt), back(dvt)

            @pl.when(n > 0)
            def _():
                dk_ref[...] = ck_ref[...] + dk_band[:BLOCK]
                dv_ref[...] = cv_ref[...] + dv_band[:BLOCK]

            ck_ref[...] = dk_band[BLOCK:]
            cv_ref[...] = dv_band[BLOCK:]

        @pl.when(n == nb)
        def _():
            dk_ref[...] = ck_ref[...]
            dv_ref[...] = cv_ref[...]

    own = lambda n: (jnp.minimum(n, nb - 1), 0)
    prev = lambda n: (jnp.clip(n - 1, 0, nb - 1), 0)
    done = lambda n: (jnp.maximum(n - 1, 0), 0)
    kvs = lambda imap: pl.BlockSpec((BLOCK, KV_WIDTH), imap)
    return _call(body, name="attn_bwd", grid=(nb + 1,),
                 in_specs=[pl.BlockSpec((BLOCK, aw), own), kvs(own), kvs(prev), kvs(own), kvs(prev),
                           pl.BlockSpec((BLOCK, aw), own), pl.BlockSpec(memory_space=pltpu.SMEM)],
                 out_specs=[pl.BlockSpec((BLOCK, aw), own), kvs(done), kvs(done),
                            pl.BlockSpec((SUBLANES, LANES), lambda n: (0, 0))],
                 out_shape=[jax.ShapeDtypeStruct((s, aw), F32), jax.ShapeDtypeStruct((s, KV_WIDTH), F32),
                            jax.ShapeDtypeStruct((s, KV_WIDTH), F32), jax.ShapeDtypeStruct((SUBLANES, LANES), F32)],
                 scratch_shapes=[pltpu.VMEM((BLOCK, KV_WIDTH), F32), pltpu.VMEM((BLOCK, KV_WIDTH), F32)],
                 compiler_params=_params(("arbitrary",)))(qs, ks, ks, vb, vb, d_o, sinks)


def conv_fwd(z, conv_w, mix, aw, cw, b_off):
    s = z.shape[0]
    tm = _tile(s, 1024)
    nseg = cw // COLT
    hb = tm // SUBLANES

    def body(b_ref, c_ref, h_ref, g_ref, cp_ref, hp_ref, w_ref, mix_in, mix_ref):
        i = pl.program_id(0)
        u = c_ref[...] * h_ref[...]
        up = jnp.where(i > 0, cp_ref[...] * hp_ref[...], 0.0)
        ext = jnp.concatenate([up, u], axis=0)
        um1 = pltpu.roll(ext, 1, 0)[SUBLANES:]
        um2 = pltpu.roll(ext, 2, 0)[SUBLANES:]
        w = w_ref[...]
        cv = w[0:1] * um2 + w[1:2] * um1 + w[2:3] * u
        g = g_ref[...]
        mix_ref[...] = ((b_ref[...] * cv) * (g * _sigmoid(g))).astype(BF16)

    seg = lambda k: pl.BlockSpec((tm, COLT), functools.partial(lambda i, j, k: (i, b_off + k * nseg + j), k=k))
    halo = lambda k: pl.BlockSpec(
        (SUBLANES, COLT), functools.partial(lambda i, j, k: (jnp.maximum(i * hb - 1, 0), b_off + k * nseg + j), k=k))
    return _call(body, name="conv_fwd", grid=(s // tm, nseg),
                 in_specs=[seg(0), seg(1), seg(2), seg(3), halo(1), halo(2),
                           pl.BlockSpec((SUBLANES, COLT), lambda i, j: (0, j)), _hbm()],
                 out_specs=pl.BlockSpec((tm, COLT), lambda i, j: (i, aw // COLT + j)),
                 out_shape=jax.ShapeDtypeStruct(mix.shape, BF16),
                 input_output_aliases={7: 0},
                 compiler_params=_params(("parallel", "parallel")))(z, z, z, z, z, z, conv_w, mix)


def conv_bwd(z, d_mix, conv_w, dz, aw, cw, b_off):
    s = z.shape[0]
    tm = _tile(s, 512)
    nseg = cw // COLT
    hb = tm // SUBLANES
    n_row = s // tm
    last_h = s // SUBLANES - 1
    n_step = nseg * n_row

    def body(b_ref, c_ref, h_ref, g_ref, cp_ref, hp_ref, bn_ref, gn_ref, dm_ref, dmn_ref, w_ref, dz_in,
             dz_ref, acc_ref, stash_ref, sems):
        j = pl.program_id(0)
        i = pl.program_id(1)
        step = j * n_row + i
        slot = step % 2

        def copies(from_slot, at_step):
            jj, ii = at_step // n_row, at_step % n_row
            rows = pl.ds(pl.multiple_of(ii * tm, tm), tm)
            return [pltpu.make_async_copy(
                stash_ref.at[from_slot, q],
                dz_ref.at[rows, pl.ds(pl.multiple_of((b_off + q * nseg + jj) * COLT, COLT), COLT)],
                sems.at[from_slot, q]) for q in range(4)]

        @pl.when(step >= 2)
        def _():
            for cp in copies(slot, step - 2):
                cp.wait()

        cc, hh, bb, g = c_ref[...], h_ref[...], b_ref[...], g_ref[...]
        w = w_ref[...]
        u = cc * hh
        up = jnp.where(i > 0, cp_ref[...] * hp_ref[...], 0.0)
        ext = jnp.concatenate([up, u], axis=0)
        um1 = pltpu.roll(ext, 1, 0)[SUBLANES:]
        um2 = pltpu.roll(ext, 2, 0)[SUBLANES:]
        cv = w[0:1] * um2 + w[1:2] * um1 + w[2:3] * u
        sig = _sigmoid(g)
        sg = g * sig
        dm = dm_ref[...]
        d_cv = (dm * bb) * sg
        gn = gn_ref[...]
        d_cv_next = jnp.where(i < n_row - 1, (dmn_ref[...] * bn_ref[...]) * (gn * _sigmoid(gn)), 0.0)
        ext2 = jnp.concatenate([d_cv, d_cv_next], axis=0)
        dp1 = pltpu.roll(ext2, tm + SUBLANES - 1, 0)[:tm]
        dp2 = pltpu.roll(ext2, tm + SUBLANES - 2, 0)[:tm]
        d_u = w[2:3] * d_cv + w[1:2] * dp1 + w[0:1] * dp2
        stash_ref[slot, 0] = ((dm * cv) * sg).astype(BF16)
        stash_ref[slot, 1] = (d_u * hh).astype(BF16)
        stash_ref[slot, 2] = (d_u * cc).astype(BF16)
        stash_ref[slot, 3] = (((dm * bb) * cv) * (sig * (1.0 + g * (1.0 - sig)))).astype(BF16)
        for cp in copies(slot, step):
            cp.start()

        @pl.when(i == 0)
        def _():
            acc_ref[...] = jnp.zeros_like(acc_ref)

        acc_ref[0:1, :] += jnp.sum(d_cv * um2, axis=0, keepdims=True)
        acc_ref[1:2, :] += jnp.sum(d_cv * um1, axis=0, keepdims=True)
        acc_ref[2:3, :] += jnp.sum(d_cv * u, axis=0, keepdims=True)

        @pl.when(step == n_step - 1)
        def _():
            if n_step >= 2:
                for cp in copies(1 - slot, step - 1):
                    cp.wait()
            for cp in copies(slot, step):
                cp.wait()

    seg = lambda q: pl.BlockSpec((tm, COLT), functools.partial(lambda j, i, q: (i, b_off + q * nseg + j), q=q))
    halo_p = lambda q: pl.BlockSpec(
        (SUBLANES, COLT),
        functools.partial(lambda j, i, q: (jnp.maximum(i * hb - 1, 0), b_off + q * nseg + j), q=q))
    halo_n = lambda q: pl.BlockSpec(
        (SUBLANES, COLT),
        functools.partial(lambda j, i, q: (jnp.minimum((i + 1) * hb, last_h), b_off + q * nseg + j), q=q))
    return _call(body, name="conv_bwd", grid=(nseg, n_row),
                 in_specs=[seg(0), seg(1), seg(2), seg(3), halo_p(1), halo_p(2), halo_n(0), halo_n(3),
                           pl.BlockSpec((tm, COLT), lambda j, i: (i, aw // COLT + j)),
                           pl.BlockSpec((SUBLANES, COLT),
                                        lambda j, i: (jnp.minimum((i + 1) * hb, last_h), aw // COLT + j)),
                           pl.BlockSpec((SUBLANES, COLT), lambda j, i: (0, j)), _hbm()],
                 out_specs=[_hbm(), pl.BlockSpec((SUBLANES, COLT), lambda j, i: (0, j))],
                 out_shape=[jax.ShapeDtypeStruct(dz.shape, BF16), jax.ShapeDtypeStruct((SUBLANES, cw), F32)],
                 input_output_aliases={11: 0},
                 scratch_shapes=[pltpu.VMEM((2, 4, tm, COLT), BF16), pltpu.SemaphoreType.DMA((2, 4))],
                 compiler_params=_params(("arbitrary", "arbitrary")))(
                     z, z, z, z, z, z, z, z, d_mix, d_mix, conv_w, dz)


def _place():
    x, y, c = lax.axis_index("x"), lax.axis_index("y"), lax.axis_index("c")
    chips = [(1 - x, y), (x, 1 - y), (1 - x, 1 - y)]
    return x, y, c, chips


def _remote(src, dst, send_sem, recv_sem, device):
    return pltpu.make_async_remote_copy(src_ref=src, dst_ref=dst, send_sem=send_sem, recv_sem=recv_sem,
                                        device_id=device, device_id_type=MESH)


def plan_gather_ici(n_split):
    def plan(refs, send, recv):
        x, y, c, chips = _place()
        me = 2 * x + y
        mine, theirs = [], []
        for w, ref in enumerate(refs):
            for j, (cx, cy) in enumerate(chips):
                def part(slot):
                    if w >= n_split:
                        return ref.at[slot]
                    hr = ref.shape[1] // 2
                    return ref.at[slot, pl.ds(c * hr, hr)]
                k = 3 * w + j
                mine.append(_remote(part(me), part(me), send.at[k], recv.at[k], (cx, cy, c)))
                theirs.append(_remote(part(2 * cx + cy), part(2 * cx + cy), send.at[k], recv.at[k], (cx, cy, c)))
        return mine, theirs
    return plan


def plan_shard_ici(j):
    def plan(refs, send, recv):
        _, _, c, chips = _place()
        own, land = refs
        hr = own.shape[0] // 2
        rows = pl.ds(c * hr, hr)
        cp = _remote(own.at[rows], land.at[rows], send.at[0], recv.at[0], (*chips[j], c))
        return [cp], [cp]
    return plan


def plan_shard_relay(refs, send, recv):
    _, _, c, chips = _place()
    land_x, land_y, land_far = refs
    hr = land_far.shape[0] // 2
    quarter = lambda q: pl.ds(c * hr + q * (hr // 2), hr // 2)
    mine = [_remote(land_y.at[quarter(0)], land_far.at[quarter(0)], send.at[0], recv.at[0], (*chips[0], c)),
            _remote(land_x.at[quarter(1)], land_far.at[quarter(1)], send.at[1], recv.at[1], (*chips[1], c))]
    return mine, mine


def plan_shard_pass(refs, send, recv):
    x, y, c, _ = _place()
    land, = refs
    hr = land.shape[0] // 2
    half = lambda core: land.at[pl.ds(core * hr, hr)]
    return ([_remote(half(c), half(c), send.at[0], recv.at[0], (x, y, 1 - c))],
            [_remote(half(1 - c), half(1 - c), send.at[0], recv.at[0], (x, y, 1 - c))])


def plan_gather_pass(refs, send, recv):
    x, y, c, chips = _place()
    mine, theirs = [], []
    for w, ref in enumerate(refs):
        hr = ref.shape[1] // 2
        for j, (cx, cy) in enumerate(chips):
            half = lambda core: ref.at[2 * cx + cy, pl.ds(core * hr, hr)]
            k = 3 * w + j
            mine.append(_remote(half(c), half(c), send.at[k], recv.at[k], (x, y, 1 - c)))
            theirs.append(_remote(half(1 - c), half(1 - c), send.at[k], recv.at[k], (x, y, 1 - c)))
    return mine, theirs


def plan_swap(refs, send, recv):
    x, y, c, _ = _place()
    nw = len(refs) // 2
    mine = []
    for w in range(nw):
        hr = refs[w].shape[1] // 2
        mine.append(_remote(refs[w].at[:, pl.ds((1 - c) * hr, hr), :], refs[nw + w], send.at[w], recv.at[w],
                            (x, y, 1 - c)))
    return mine, mine


def plan_scatter(refs, send, recv):
    x, y, c, chips = _place()
    me = 2 * x + y
    nw = len(refs) // 2
    mine, theirs = [], []
    for w in range(nw):
        for j, (cx, cy) in enumerate(chips):
            k = 3 * w + j
            mine.append(_remote(refs[w].at[2 * cx + cy], refs[nw + w].at[me], send.at[k], recv.at[k], (cx, cy, c)))
            theirs.append(_remote(refs[w].at[me], refs[nw + w].at[2 * cx + cy], send.at[k], recv.at[k], (cx, cy, c)))
    return mine, theirs


def plan_join(refs, send, recv):
    x, y, c, _ = _place()
    mine, theirs = [], []
    for w, ref in enumerate(refs):
        hr = ref.shape[0] // 2
        half = lambda core: ref.at[pl.ds(core * hr, hr)]
        mine.append(_remote(half(c), half(c), send.at[w], recv.at[w], (x, y, 1 - c)))
        theirs.append(_remote(half(1 - c), half(1 - c), send.at[w], recv.at[w], (x, y, 1 - c)))
    return mine, theirs


def exchange(name, plan, arrays, n, after=()):
    na = len(arrays)

    def body(*refs):
        mine, theirs = plan(refs[:na], refs[2 * na], refs[2 * na + 1])
        for cp in mine:
            cp.start()
        for cp in theirs:
            cp.wait_recv()
        for cp in mine:
            cp.wait_send()

    return _call_after(body, after, name=name, in_specs=[_hbm()] * na, out_specs=[_hbm()] * na,
                       out_shape=[jax.ShapeDtypeStruct(a.shape, a.dtype) for a in arrays],
                       input_output_aliases={i: i for i in range(na)},
                       scratch_shapes=[pltpu.SemaphoreType.DMA((n,)), pltpu.SemaphoreType.DMA((n,))])(*arrays)


def exchange_start(name, groups, arrays, after=()):
    na, ng = len(arrays), len(groups)

    def body(*refs):
        for g, (plan, idx, _) in enumerate(groups):
            mine, _ = plan([refs[i] for i in idx], refs[na + 2 * g], refs[na + 2 * g + 1])
            for cp in mine:
                cp.start()
        refs[-1][...] = jnp.zeros_like(refs[-1])

    hbm = pl.BlockSpec(memory_space=pltpu.HBM)
    sem = pl.BlockSpec(memory_space=pltpu.SEMAPHORE)
    sem_types = [pltpu.SemaphoreType.DMA((n,)) for _, _, n in groups for _ in (0, 1)]
    outs = _call_after(body, after, name=name, in_specs=[hbm] * na,
                       out_specs=[sem] * (2 * ng) + [hbm] * na + [pl.BlockSpec(memory_space=pltpu.VMEM)],
                       out_shape=sem_types + [pltpu.HBM(a.shape, a.dtype) for a in arrays]
                       + [jax.ShapeDtypeStruct((SUBLANES, LANES), F32)],
                       input_output_aliases={i: i + 2 * ng for i in range(na)},
                       compiler_params=pltpu.CompilerParams(
                           has_side_effects=pltpu.SideEffectType.DATAFLOW_SIDE_EFFECTING))(
                               *[pltpu.with_memory_space_constraint(a, pltpu.HBM) for a in arrays])
    sems = [(outs[2 * g], outs[2 * g + 1]) for g in range(ng)]
    return sems, list(outs[2 * ng:-1]), outs[-1]


def exchange_wait(name, plan, sems, arrays, after):
    send_sems, recv_sems = sems
    na = len(arrays)

    def body(*refs):
        mine, theirs = plan(refs[:na], refs[na], refs[na + 1])
        for cp in mine:
            cp.wait_send()
        for cp in theirs:
            cp.wait_recv()

    hbm = pl.BlockSpec(memory_space=pltpu.HBM)
    sem = pl.BlockSpec(memory_space=pltpu.SEMAPHORE)
    return _call(body, name=name, in_specs=[hbm] * na + [sem, sem, _hbm()], out_specs=[hbm] * na,
                 out_shape=[pltpu.HBM(a.shape, a.dtype) for a in arrays],
                 input_output_aliases={i: i for i in range(na)},
                 compiler_params=pltpu.CompilerParams(
                     has_side_effects=pltpu.SideEffectType.DATAFLOW_SIDE_EFFECTING))(
                         *arrays, send_sems, recv_sems, after)


def plan_allgather_small(refs, send, recv):
    x, y, c, _ = _place()
    buf, = refs
    mine, theirs = [], []
    flips = [(fx, fy, fc) for fx in (0, 1) for fy in (0, 1) for fc in (0, 1)][1:]
    for k, (fx, fy, fc) in enumerate(flips):
        peer = (x ^ fx, y ^ fy, c ^ fc)
        slot = lambda px, py, pc: buf.at[4 * px + 2 * py + pc]
        mine.append(_remote(slot(x, y, c), slot(x, y, c), send.at[k], recv.at[k], peer))
        theirs.append(_remote(slot(*peer), slot(*peer), send.at[k], recv.at[k], peer))
    return mine, theirs


def add_sibling(grad, got, core, name):
    _, r, c = grad.shape
    hr = r // 2
    tr = _tile(hr, 512)
    nblk = hr // tr

    def body(core_ref, g_ref, o_ref, out_ref):
        out_ref[...] = (g_ref[...].astype(F32) + o_ref[...].astype(F32)).astype(BF16)

    grid_spec = pltpu.PrefetchScalarGridSpec(
        num_scalar_prefetch=1, grid=(N_CHIPS, nblk),
        in_specs=[pl.BlockSpec((None, tr, c), lambda t, i, core_ref: (t, core_ref[0] * nblk + i, 0)),
                  pl.BlockSpec((None, tr, c), lambda t, i, core_ref: (t, i, 0))],
        out_specs=pl.BlockSpec((None, tr, c), lambda t, i, core_ref: (t, i, 0)))
    return _call(body, name=name, grid_spec=grid_spec,
                 out_shape=jax.ShapeDtypeStruct((N_CHIPS, hr, c), BF16),
                 compiler_params=_params(("parallel", "parallel")))(core, grad, got)


def sum_chips(mine, owned, place, name):
    _, hr, c = mine.shape
    tr = _tile(hr, 512)
    nblk = hr // tr

    def body(place_ref, m_ref, o1_ref, o2_ref, o3_ref, out_ref):
        acc = m_ref[...].astype(F32)
        for o_ref in (o1_ref, o2_ref, o3_ref):
            acc = acc + o_ref[...].astype(F32)
        out_ref[...] = acc

    other = lambda k: pl.BlockSpec((None, tr, c), lambda i, place_ref: ((place_ref[0] + k) % N_CHIPS, i, 0))
    grid_spec = pltpu.PrefetchScalarGridSpec(
        num_scalar_prefetch=1, grid=(nblk,),
        in_specs=[other(0), other(1), other(2), other(3)],
        out_specs=pl.BlockSpec((tr, c), lambda i, place_ref: (place_ref[1] * nblk + i, 0)))
    return _call(body, name=name, grid_spec=grid_spec,
                 out_shape=jax.ShapeDtypeStruct((2 * hr, c), F32),
                 compiler_params=_params(("parallel",)))(place, mine, owned, owned, owned)


def sum_devices(gathered):
    _, rows, width = gathered.shape

    def body(g_ref, out_ref):
        acc = g_ref[0]
        for dev in range(1, 8):
            acc = acc + g_ref[dev]
        out_ref[...] = acc
        tail = acc[SUBLANES:]
        out_ref[SUBLANES:, :] = jnp.broadcast_to(jnp.sum(tail, axis=1, keepdims=True), tail.shape)

    return _call(body, name="sum_devices",
                 in_specs=[pl.BlockSpec(memory_space=pltpu.VMEM)],
                 out_specs=pl.BlockSpec(memory_space=pltpu.VMEM),
                 out_shape=jax.ShapeDtypeStruct((rows, width), F32))(gathered)


def _rope_tables(s):
    half = ROT_DIM // 2
    inv_freq = jnp.power(jnp.float32(ROPE_THETA), -jnp.arange(half, dtype=F32) * 2.0 / ROT_DIM)
    freq64 = jnp.concatenate([inv_freq, inv_freq, jnp.zeros((HEAD_DIM - ROT_DIM,), F32)])
    freq = jnp.concatenate([freq64, freq64])[None, :]
    dim = (jnp.arange(LANES) % HEAD_DIM)[None, :]
    ang = jnp.arange(s).astype(F32)[:, None] * freq
    cos, sin = jnp.cos(ang), jnp.sin(ang)
    return cos, jnp.where(dim < half, -sin, 0.0), jnp.where((dim >= half) & (dim < ROT_DIM), sin, 0.0)


def _pad_rows(a, rows):
    return jnp.pad(a, ((0, rows - a.shape[0]), (0, 0)))


def _pad_cols(a, cols):
    return jnp.pad(a, ((0, 0), (0, cols - a.shape[1])))


def kernel(x, p, norm_gain, w_in, q_norm_gain, k_norm_gain, attn_sinks, conv_w, w_out, ple_gate_norm_gain, w_ple_gate, b_ple_gate, w_ple_proj, ple_norm_gain, loss_target, m_norm_gain, m_w_in, m_q_norm_gain, m_k_norm_gain, m_attn_sinks, m_conv_w, m_w_out, m_ple_gate_norm_gain, m_w_ple_gate, m_b_ple_gate, m_w_ple_proj, m_ple_norm_gain, v_norm_gain, v_w_in, v_q_norm_gain, v_k_norm_gain, v_attn_sinks, v_conv_w, v_w_out, v_ple_gate_norm_gain, v_w_ple_gate, v_b_ple_gate, v_w_ple_proj, v_ple_norm_gain):
    x2, p2, tgt = x[0], p[0, 0], loss_target[0]
    s, d = x2.shape
    ple = p2.shape[1]
    aw = d // 2
    cw = d - aw
    nq = aw // HEAD_DIM
    sh = w_in.shape[2]
    in_w = N_CHIPS * sh
    dq = d // N_CHIPS
    cq = cw // N_CHIPS
    ga_off = (aw + 2 * KV_WIDTH) // COLT
    b_off = (2 * aw + 2 * KV_WIDTH) // COLT
    assert in_w == 2 * aw + 2 * KV_WIDTH + 4 * cw and aw % COLT == 0 and cw % COLT == 0
    assert s % BLOCK == 0 and sh % LANES == 0 and nq % (2 * N_KV_HEADS) == 0

    core = lax.axis_index("c").astype(jnp.int32).reshape(1)
    chip = 2 * lax.axis_index("x") + lax.axis_index("y")

    chip1 = chip.astype(jnp.int32).reshape(1)
    place = jnp.concatenate([chip1, core])
    w_in_own = cast_bf16(w_in[0], "cast_w_in")
    rest = [cast_into_slot(w_out[0], chip1, "cast_w_out"), cast_into_slot(w_ple_gate[0], chip1, "cast_w_pg"),
            cast_into_slot(w_ple_proj[0], chip1, "cast_w_pp"),
            lax.dynamic_update_slice(jnp.zeros((N_CHIPS, SUBLANES, cq), F32),
                                     _pad_rows(conv_w[0], SUBLANES)[None], (chip, 0, 0))]
    order = jnp.stack([chip, chip ^ 2, chip ^ 1, chip ^ 3]).astype(jnp.int32)
    wi_sems, wi_arrs, wi_token = exchange_start(
        "gather_wi_start", [(plan_shard_ici(j), [0, 1 + j], 1) for j in range(2)],
        [w_in_own] + [lax.empty((d, sh), BF16) for _ in range(3)])

    tm = _tile(s, 1024)
    tn = _tile(d, 1024)
    tk = _tile(d, 2048)
    ts = _tile(s, 2048)
    h = rms_fwd(x2, norm_gain, "rms_fwd_x", after=(wi_token,))
    tabs = _rope_tables(s)
    qg = jnp.tile(q_norm_gain, (1, LANES // HEAD_DIM))
    kg = jnp.tile(k_norm_gain, (1, LANES // HEAD_DIM))
    seg_i = jnp.arange(SEG) // HEAD_DIM
    segb = (seg_i[:, None] == seg_i[None, :]).astype(BF16)
    z = in_proj_part(h, wi_arrs[0], None, order, 0, in_w)
    own, land_x = exchange_wait("gather_wi_wait0", plan_shard_ici(0), wi_sems[0], [wi_arrs[0], wi_arrs[1]], z)
    own, land_y = exchange_wait("gather_wi_wait1", plan_shard_ici(1), wi_sems[1], [own, wi_arrs[2]], land_x)
    relay_sems, relayed, relay_token = exchange_start(
        "gather_wi_relay_start", [(plan_shard_relay, [0, 1, 2], 2)], [land_x, land_y, wi_arrs[3]])
    rest_sems, rest, rest_token = exchange_start(
        "gather_rest_start", [(plan_gather_ici(3), [0, 1, 2, 3], 12)], rest, after=(relay_token,))
    land_x, = exchange("gather_wi_pass0", plan_shard_pass, [relayed[0]], 1, after=(rest_token,))
    z = in_proj_part(h, land_x, z, order, 1, in_w)
    land_y, = exchange("gather_wi_pass1", plan_shard_pass, [relayed[1]], 1)
    z = in_proj_part(h, land_y, z, order, 2, in_w)
    land_x, land_y, land_far = exchange_wait("gather_wi_relay_wait", plan_shard_relay, relay_sems[0],
                                             [land_x, land_y, relayed[2]], z)
    land_far, = exchange("gather_wi_pass2", plan_shard_pass, [land_far], 1)
    z = in_proj_part(h, land_far, z, order, 3, in_w)
    w_shards = [own, land_x, land_y, land_far]
    wo_all, wg_all, wp_all, conv_all = exchange_wait("gather_rest_wait", plan_gather_ici(3), rest_sems[0], rest, z)
    pass_sems, passed, pass_token = exchange_start(
        "gather_rest_pass_start", [(plan_gather_pass, [0, 1, 2], 9)], [wo_all, wg_all, wp_all])
    conv_full = conv_all.transpose(1, 0, 2).reshape(SUBLANES, cw)
    qs, ks, vb = qk_prep_fwd(z, tabs, qg, kg, segb, aw, after=(pass_token,))
    attn, mix = attn_fwd(qs, ks, vb, z, attn_sinks, aw, d, ga_off)
    mix = conv_fwd(z, conv_full, mix, aw, cw, b_off)
    wo_all, wg_all, wp_all = exchange_wait("gather_rest_pass_wait", plan_gather_pass, pass_sems[0], passed, mix)
    wo_full = wo_all.reshape(d, d)
    wg_full = wg_all.reshape(d, d)
    sq = lambda shape: dict(
        a_spec=pl.BlockSpec((tm, tk), lambda i, j, k: (i, k)),
        o_spec=pl.BlockSpec((tm, tn), lambda i, j, k: (i, j)),
        out_shape=jax.ShapeDtypeStruct(shape, F32))
    x1 = matmul(mix, wo_full, grid=(s // tm, d // tn, d // tk),
                b_spec=pl.BlockSpec((tk, tn), lambda i, j, k: (k, j)), dims=NN, name="mm_x1",
                res=x2, res_spec=pl.BlockSpec((tm, tn), lambda i, j, k: (i, j)), **sq((s, d)))
    hg = rms_fwd(x1, ple_gate_norm_gain, "rms_fwd_x1")
    gl = matmul(hg, wg_full, grid=(s // tm, d // tn, d // tk),
                b_spec=pl.BlockSpec((tk, tn), lambda i, j, k: (k, j)), dims=NN, name="mm_gl", **sq((s, d)))
    pq = d // N_CHIPS

    d_gl, d_pe, dy, acc_head = head_fwd_bwd(x1, gl, p2, wp_all, tgt, b_ple_gate, ple_norm_gain)
    d_hg = matmul(d_gl, wg_full, grid=(s // tm, d // tn, d // tk),
                  b_spec=pl.BlockSpec((tn, tk), lambda i, j, k: (j, k)), dims=NT, name="mm_d_hg", **sq((s, d)))
    wgrad = lambda a_cols, shape3, o_spec, b_cols, name, a, b, grid: matmul(
        a, b, grid=grid,
        a_spec=pl.BlockSpec((ts, a_cols), lambda i, j, k: (k, i)),
        b_spec=pl.BlockSpec((ts, b_cols), lambda i, j, k: (k, j)),
        o_spec=o_spec, out_shape=jax.ShapeDtypeStruct(shape3, BF16), dims=TN, name=name)
    g_wg = wgrad(tn, (d, d), pl.BlockSpec((tn, tn), lambda i, j, k: (i, j)), tn, "mm_g_wg", hg, d_gl,
                 (d // tn, d // tn, s // ts))
    g_wp = wgrad(ple, (N_CHIPS, ple, pq), pl.BlockSpec((None, ple, pq), lambda i, j, k: (j, 0, 0)), pq,
                 "mm_g_wp", p2, d_pe, (1, N_CHIPS, s // ts))
    d_x1, d_x1b, acc_g = rms_bwd_add(d_hg, x1, dy, ple_gate_norm_gain, "rms_bwd_x1", True)
    d_mix = matmul(d_x1b, wo_full, grid=(s // tm, d // tn, d // tk),
                   b_spec=pl.BlockSpec((tn, tk), lambda i, j, k: (j, k)), dims=NT, name="mm_d_mix", **sq((s, d)))
    g_wo = wgrad(tn, (d, d), pl.BlockSpec((tn, tn), lambda i, j, k: (i, j)), tn, "mm_g_wo", mix, d_x1b,
                 (d // tn, d // tn, s // ts))
    def reduce_start(tag, grads):
        n = len(grads)
        lands = [lax.empty((N_CHIPS, a.shape[1] // 2, a.shape[2]), BF16) for a in grads]
        swapped = exchange("swap_" + tag, plan_swap, list(grads) + lands, n)
        parts = [add_sibling(g, o, core, "add_sibling_%s%d" % (tag, i))
                 for i, (g, o) in enumerate(zip(swapped[:n], swapped[n:]))]
        return exchange_start("scatter_%s_start" % tag, [(plan_scatter, list(range(2 * n)), 3 * n)],
                              parts + [lax.empty(a.shape, BF16) for a in parts])

    def reduce_sum(tag, handle, after):
        sems, arrays, _ = handle
        n = len(arrays) // 2
        got = exchange_wait("scatter_%s_wait" % tag, plan_scatter, sems[0], arrays, after)
        return [sum_chips(pt, o, place, "sum_chips_%s%d" % (tag, i))
                for i, (pt, o) in enumerate(zip(got[:n], got[n:]))]

    early = reduce_start("early", [g_wo.reshape(N_CHIPS, dq, d), g_wg.reshape(N_CHIPS, dq, d), g_wp])
    d_o, dz = gate_bwd(d_mix, attn, z, aw, ga_off, after=(early[-1],))
    dqs, dks, dvs, acc_sink = attn_bwd(qs, ks, vb, d_o, attn_sinks, aw)
    dz, acc_qk = qk_prep_bwd(z, dqs, dks, dvs, tabs, qg, kg, segb, dz, aw)
    dz, acc_conv = conv_bwd(z, d_mix, conv_full, dz, aw, cw, b_off)
    join_sems, joining, join_token = exchange_start(
        "join_early_start", [(plan_join, [0, 1, 2], 3)], reduce_sum("early", early, dz))
    g_wi = matmul(h, dz, grid=(d // tn, N_CHIPS, 1),
                  a_spec=pl.BlockSpec((s, tn), lambda i, j, k: (0, i), pipeline_mode=pl.Buffered(1)),
                  b_spec=pl.BlockSpec((s, sh), lambda i, j, k: (0, j)),
                  o_spec=pl.BlockSpec((None, tn, sh), lambda i, j, k: (j, i, 0)),
                  out_shape=jax.ShapeDtypeStruct((N_CHIPS, d, sh), BF16), dims=TN, name="mm_g_wi",
                  after=(join_token,), vmem=VMEM_LIMIT_BIG)
    full_wo, full_wg, full_wp = exchange_wait("join_early_wait", plan_join, join_sems[0], joining, g_wi)
    late = reduce_start("late", [g_wi])
    d_h = in_proj_bwd(dz, w_shards, order, d, after=(late[-1],))
    grad_x, acc_x = rms_bwd_add(d_h, x2, d_x1, norm_gain, "rms_bwd_x", False)

    wsm = max(d, cw)
    misc = jnp.concatenate([acc_qk[0:1, :HEAD_DIM], acc_qk[1:2, :HEAD_DIM], acc_sink[0:1, :nq]], axis=1)
    small = jnp.concatenate([
        _pad_cols(acc_x[0:1], wsm), _pad_cols(acc_g[0:1], wsm), _pad_cols(acc_head[0:1], wsm),
        _pad_cols(acc_head[1:2], wsm), _pad_cols(misc, wsm), _pad_cols(acc_conv[0:3], wsm),
        _pad_rows(_pad_cols(acc_head[2:3], wsm), SUBLANES)], axis=0)
    device = 2 * chip + lax.axis_index("c")
    small_sems, small_bufs, small_token = exchange_start(
        "small_start", [(plan_allgather_small, [0], 7)],
        [lax.dynamic_update_slice(jnp.zeros((8,) + small.shape, F32), small[None], (device, 0, 0))])
    last_sems, last_halves, last_token = exchange_start(
        "join_late_start", [(plan_join, [0], 1)], reduce_sum("late", late, small_token))
    big, after = {}, (last_token,)
    for nm, g, w, m, v in (("w_out", full_wo, w_out, m_w_out, v_w_out),
                           ("w_ple_gate", full_wg, w_ple_gate, m_w_ple_gate, v_w_ple_gate),
                           ("w_ple_proj", full_wp, w_ple_proj, m_w_ple_proj, v_w_ple_proj)):
        stepped = adamw(g, w[0], m[0], v[0], "adamw_" + nm, after=after)
        big[nm], after = [o[None] for o in stepped], (stepped[1],)
    full_wi, = exchange_wait("join_late_wait", plan_join, last_sems[0], last_halves, after[0])
    big["w_in"] = [o[None] for o in adamw(full_wi, w_in[0], m_w_in[0], v_w_in[0], "adamw_w_in")]

    gathered, = exchange_wait("small_wait", plan_allgather_small, small_sems[0], small_bufs, big["w_in"][1])
    tot = sum_devices(gathered)
    loss = tot[SUBLANES, 0]

    def pack(vals):
        ng, pg, bg, eg, qgv, kgv, sk, cv = vals
        misc_v = jnp.concatenate([qgv, kgv, sk], axis=1)
        return jnp.concatenate([_pad_cols(ng, wsm), _pad_cols(pg, wsm), _pad_cols(bg, wsm), _pad_cols(eg, wsm),
                                _pad_cols(misc_v, wsm), _pad_cols(cv[0], wsm)], axis=0)

    g_small = jnp.concatenate(
        [tot[0:5], _pad_cols(lax.dynamic_slice(tot[5:8], (0, chip * cq), (3, cq)), wsm)], axis=0)
    w_small = pack((norm_gain, ple_gate_norm_gain, b_ple_gate, ple_norm_gain, q_norm_gain, k_norm_gain,
                    attn_sinks, conv_w))
    m_small = pack((m_norm_gain, m_ple_gate_norm_gain, m_b_ple_gate, m_ple_norm_gain, m_q_norm_gain,
                    m_k_norm_gain, m_attn_sinks, m_conv_w))
    v_small = pack((v_norm_gain, v_ple_gate_norm_gain, v_b_ple_gate, v_ple_norm_gain, v_q_norm_gain,
                    v_k_norm_gain, v_attn_sinks, v_conv_w))
    sm = adamw(g_small, w_small, m_small, v_small, "adamw_small")

    def unpack(a):
        return {"norm_gain": a[0:1, :d], "ple_gate_norm_gain": a[1:2, :d], "b_ple_gate": a[2:3, :d],
                "ple_norm_gain": a[3:4, :d], "q_norm_gain": a[4:5, :HEAD_DIM],
                "k_norm_gain": a[4:5, HEAD_DIM:2 * HEAD_DIM],
                "attn_sinks": a[4:5, 2 * HEAD_DIM:2 * HEAD_DIM + nq], "conv_w": a[5:8, :cq][None]}

    order = ("norm_gain", "w_in", "q_norm_gain", "k_norm_gain", "attn_sinks", "conv_w", "w_out",
             "ple_gate_norm_gain", "w_ple_gate", "b_ple_gate", "w_ple_proj", "ple_norm_gain")
    outs = [loss, grad_x[None]]
    for kind in range(4):
        table = unpack(sm[kind])
        for nm in order:
            outs.append(big[nm][kind] if nm in big else table[nm])
    return tuple(outs)
```

```python
import functools

import jax
import jax.numpy as jnp
from jax import lax
from jax.experimental import pallas as pl
from jax.experimental.pallas import tpu as pltpu

F32 = jnp.float32
BF16 = jnp.bfloat16
MESH = pl.DeviceIdType.MESH

HEAD_DIM = 64
N_KV_HEADS = 4
KV_WIDTH = N_KV_HEADS * HEAD_DIM
BLOCK = 128
ROT_DIM = 16
ROPE_THETA = 500000.0
EPS = 1e-6
NEG_INF = -1e30
N_CHIPS = 4
LANES = 128
SUBLANES = 8
COLT = 512
SEG = 256
VMEM_LIMIT = 48 * 1024 * 1024
VMEM_LIMIT_BIG = 60 * 1024 * 1024

ADAM_LR = 0.001
ADAM_B1 = 0.9
ADAM_B2 = 0.999
ADAM_EPS = 1e-08
ADAM_WD = 0.01
ADAM_STEP = 10

NN = (((1,), (0,)), ((), ()))
NT = (((1,), (1,)), ((), ()))
TN = (((0,), (0,)), ((), ()))


def _call(body, **kw):
    return pl.pallas_call(body, **kw)


def _call_after(body, after, **kw):
    n_in, n_after = len(kw["in_specs"]), len(after)
    kw["in_specs"] = list(kw["in_specs"]) + [pl.BlockSpec(memory_space=pl.ANY)] * n_after

    def body_after(*refs):
        body(*refs[:n_in], *refs[n_in + n_after:])

    call = _call(body_after, **kw)
    return lambda *args: call(*args, *after)


def _params(sem, vmem=VMEM_LIMIT):
    return pltpu.CompilerParams(dimension_semantics=sem, vmem_limit_bytes=vmem)


def _tile(n, pref):
    return pref if n % pref == 0 else n


def _sigmoid(v):
    return 1.0 / (1.0 + jnp.exp(-v))


def _hbm():
    return pl.BlockSpec(memory_space=pl.ANY)


def cast_bf16(a, name):
    r, c = a.shape
    tr = _tile(r, 512)

    def body(a_ref, o_ref):
        o_ref[...] = a_ref[...].astype(BF16)

    return _call(body, name=name, grid=(r // tr,),
                 in_specs=[pl.BlockSpec((tr, c), lambda i: (i, 0))],
                 out_specs=pl.BlockSpec((tr, c), lambda i: (i, 0)),
                 out_shape=jax.ShapeDtypeStruct((r, c), BF16),
                 compiler_params=_params(("parallel",)))(a)


def cast_into_slot(a, chip, name):
    r, c = a.shape
    tr = _tile(r, 512)

    def body(chip_ref, a_ref, o_ref):
        o_ref[...] = a_ref[...].astype(BF16)

    grid_spec = pltpu.PrefetchScalarGridSpec(
        num_scalar_prefetch=1, grid=(r // tr,),
        in_specs=[pl.BlockSpec((tr, c), lambda i, chip_ref: (i, 0))],
        out_specs=pl.BlockSpec((None, tr, c), lambda i, chip_ref: (chip_ref[0], i, 0)))
    return _call(body, name=name, grid_spec=grid_spec,
                 out_shape=jax.ShapeDtypeStruct((N_CHIPS, r, c), BF16),
                 compiler_params=_params(("parallel",)))(chip, a)


def rms_fwd(x, gain, name, after=()):
    s, d = x.shape
    tm = _tile(s, 512)

    def body(x_ref, g_ref, o_ref):
        xf = x_ref[...]
        r = lax.rsqrt(jnp.mean(xf * xf, axis=-1, keepdims=True) + EPS)
        o_ref[...] = ((xf * r) * g_ref[...]).astype(BF16)

    return _call_after(body, after, name=name, grid=(s // tm,),
                       in_specs=[pl.BlockSpec((tm, d), lambda i: (i, 0)),
                                 pl.BlockSpec((1, d), lambda i: (0, 0))],
                       out_specs=pl.BlockSpec((tm, d), lambda i: (i, 0)),
                       out_shape=jax.ShapeDtypeStruct((s, d), BF16),
                       compiler_params=_params(("parallel",)))(x, gain)


def rms_bwd_add(dyn, xin, add, gain, name, want_bf16):
    s, d = xin.shape
    tm = _tile(s, 512)

    def body(dy_ref, x_ref, a_ref, g_ref, *outs):
        i = pl.program_id(0)
        dx_ref, acc_ref = outs[0], outs[-1]
        xf = x_ref[...]
        r = lax.rsqrt(jnp.mean(xf * xf, axis=-1, keepdims=True) + EPS)
        xhat = xf * r
        dyv = dy_ref[...]
        gd = dyv * g_ref[...]
        dx = a_ref[...] + r * (gd - xhat * jnp.mean(xhat * gd, axis=-1, keepdims=True))
        dx_ref[...] = dx
        if want_bf16:
            outs[1][...] = dx.astype(BF16)

        @pl.when(i == 0)
        def _():
            acc_ref[...] = jnp.zeros_like(acc_ref)

        acc_ref[0:1, :] += jnp.sum(dyv * xhat, axis=0, keepdims=True)

    row = pl.BlockSpec((tm, d), lambda i: (i, 0))
    out_specs = [row] + ([row] if want_bf16 else []) + [pl.BlockSpec((SUBLANES, d), lambda i: (0, 0))]
    out_shape = ([jax.ShapeDtypeStruct((s, d), F32)]
                 + ([jax.ShapeDtypeStruct((s, d), BF16)] if want_bf16 else [])
                 + [jax.ShapeDtypeStruct((SUBLANES, d), F32)])
    return _call(body, name=name, grid=(s // tm,),
                 in_specs=[row, row, row, pl.BlockSpec((1, d), lambda i: (0, 0))],
                 out_specs=out_specs, out_shape=out_shape,
                 compiler_params=_params(("arbitrary",), VMEM_LIMIT_BIG))(dyn, xin, add, gain)


def head_fwd_bwd(x1, gl, p, wp, tgt, bias, ple_gain):
    s, d = x1.shape
    tm = _tile(s, 256)

    def body(x1_ref, gl_ref, p_ref, wp_ref, t_ref, b_ref, g_ref, dgl_ref, dpe_ref, dy_ref, acc_ref):
        i = pl.program_id(0)
        gate = _sigmoid(gl_ref[...] + b_ref[...])
        pb = p_ref[...].astype(BF16)
        pev = jnp.concatenate([jnp.dot(pb, wp_ref[q], preferred_element_type=F32) for q in range(N_CHIPS)],
                              axis=1)
        r = lax.rsqrt(jnp.mean(pev * pev, axis=-1, keepdims=True) + EPS)
        pehat = pev * r
        gain = g_ref[...]
        e = pehat * gain
        diff = (x1_ref[...] + gate * e) - t_ref[...]
        dy = diff * (1.0 / d)
        dy_ref[...] = dy
        d_gl = (dy * e) * (gate * (1.0 - gate))
        dgl_ref[...] = d_gl.astype(BF16)
        d_e = dy * gate
        gd = d_e * gain
        d_pe = r * (gd - pehat * jnp.mean(pehat * gd, axis=-1, keepdims=True))
        dpe_ref[...] = d_pe.astype(BF16)

        @pl.when(i == 0)
        def _():
            acc_ref[...] = jnp.zeros_like(acc_ref)

        acc_ref[0:1, :] += jnp.sum(d_gl, axis=0, keepdims=True)
        acc_ref[1:2, :] += jnp.sum(d_e * pehat, axis=0, keepdims=True)
        acc_ref[2:3, :] += jnp.sum(diff * diff, axis=0, keepdims=True) * (0.5 / d)

    row = pl.BlockSpec((tm, d), lambda i: (i, 0))
    vec = pl.BlockSpec((1, d), lambda i: (0, 0))
    return _call(body, name="head_fwd_bwd", grid=(s // tm,),
                 in_specs=[row, row, pl.BlockSpec((tm, p.shape[1]), lambda i: (i, 0)),
                           pl.BlockSpec(wp.shape, lambda i: (0, 0, 0)), row, vec, vec],
                 out_specs=[row, row, row, pl.BlockSpec((SUBLANES, d), lambda i: (0, 0))],
                 out_shape=[jax.ShapeDtypeStruct((s, d), BF16), jax.ShapeDtypeStruct((s, d), BF16),
                            jax.ShapeDtypeStruct((s, d), F32), jax.ShapeDtypeStruct((SUBLANES, d), F32)],
                 compiler_params=_params(("arbitrary",)))(x1, gl, p, wp, tgt, bias, ple_gain)


def adamw(g, w, m, v, name, after=()):
    r, c = g.shape
    tr = _tile(r, 256)

    def body(g_ref, w_ref, m_ref, v_ref, go_ref, d_ref, mo_ref, vo_ref):
        gv = g_ref[...]
        mn = ADAM_B1 * m_ref[...] + (1.0 - ADAM_B1) * gv
        vn = ADAM_B2 * v_ref[...] + (1.0 - ADAM_B2) * (gv * gv)
        m_hat = mn / (1.0 - ADAM_B1 ** ADAM_STEP)
        v_hat = vn / (1.0 - ADAM_B2 ** ADAM_STEP)
        go_ref[...] = gv
        d_ref[...] = -ADAM_LR * (m_hat / (jnp.sqrt(v_hat) + ADAM_EPS) + ADAM_WD * w_ref[...])
        mo_ref[...] = mn
        vo_ref[...] = vn

    blk = pl.BlockSpec((tr, c), lambda i: (i, 0))
    shp = jax.ShapeDtypeStruct((r, c), F32)
    return _call_after(body, after, name=name, grid=(r // tr,), in_specs=[blk] * 4, out_specs=[blk] * 4,
                       out_shape=[shp] * 4, compiler_params=_params(("parallel",)))(g, w, m, v)


def matmul(a, b, *, grid, a_spec, b_spec, o_spec, out_shape, dims, name, res=None, res_spec=None, after=(),
           vmem=VMEM_LIMIT):
    nk = grid[2]
    acc_shape = tuple(d for d in o_spec.block_shape if d is not None)

    def body(*refs):
        a_ref, b_ref = refs[:2]
        r_ref = refs[2] if res is not None else None
        o_ref = refs[3] if res is not None else refs[2]
        part = lax.dot_general(a_ref[...].astype(BF16), b_ref[...].astype(BF16), dims, preferred_element_type=F32)

        def finish(out):
            if res is not None:
                out = r_ref[...] + out
            o_ref[...] = out.astype(o_ref.dtype)

        if nk == 1:
            finish(part)
            return
        acc_ref = refs[-1]
        k = pl.program_id(2)

        @pl.when(k == 0)
        def _():
            acc_ref[...] = part

        @pl.when((k > 0) & (k < nk - 1))
        def _():
            acc_ref[...] += part

        @pl.when(k == nk - 1)
        def _():
            finish(acc_ref[...] + part)

    in_specs = [a_spec, b_spec] + ([res_spec] if res is not None else [])
    args = (a, b) + ((res,) if res is not None else ())
    return _call_after(body, after, name=name, grid=grid, in_specs=in_specs, out_specs=o_spec,
                       out_shape=out_shape, scratch_shapes=[pltpu.VMEM(acc_shape, F32)] if nk > 1 else [],
                       compiler_params=_params(("parallel", "parallel", "arbitrary"), vmem))(*args)


def in_proj_part(h, w, z_prev, order, q, in_w):
    s, d = h.shape
    sh = w.shape[1]
    tm = _tile(s, 512)

    def body(order_ref, h_ref, w_ref, *rest):
        rest[-1][...] = jnp.dot(h_ref[...], w_ref[...], preferred_element_type=F32)

    in_specs = [pl.BlockSpec((tm, d), lambda i, order_ref: (i, 0)),
                pl.BlockSpec((d, sh), lambda i, order_ref: (0, 0))]
    args = [order, h, w]
    if z_prev is not None:
        in_specs.append(_hbm())
        args.append(z_prev)
    grid_spec = pltpu.PrefetchScalarGridSpec(
        num_scalar_prefetch=1, grid=(s // tm,), in_specs=in_specs,
        out_specs=pl.BlockSpec((tm, sh), lambda i, order_ref: (i, order_ref[q])))
    return _call(body, name="mm_z%d" % q, grid_spec=grid_spec,
                 out_shape=jax.ShapeDtypeStruct((s, in_w), F32),
                 input_output_aliases={3: 0} if z_prev is not None else {},
                 compiler_params=_params(("parallel",)))(*args)


def in_proj_bwd(dz, ws, order, d, after):
    s = dz.shape[0]
    sh = ws[0].shape[1]
    tm, tn = _tile(s, 512), _tile(d, 1024)
    nq, n_after = len(ws), len(after)

    def body(order_ref, *refs):
        a_refs, b_refs, o_ref = refs[:nq], refs[nq:2 * nq], refs[2 * nq + n_after]
        acc = lax.dot_general(a_refs[0][...], b_refs[0][...], NT, preferred_element_type=F32)
        for q in range(1, nq):
            acc += lax.dot_general(a_refs[q][...], b_refs[q][...], NT, preferred_element_type=F32)
        o_ref[...] = acc

    a_spec = lambda q: pl.BlockSpec((tm, sh), functools.partial(lambda i, j, order_ref, q: (i, order_ref[q]), q=q))
    grid_spec = pltpu.PrefetchScalarGridSpec(
        num_scalar_prefetch=1, grid=(s // tm, d // tn),
        in_specs=[a_spec(q) for q in range(nq)]
        + [pl.BlockSpec((tn, sh), lambda i, j, order_ref: (j, 0))] * nq + [_hbm()] * n_after,
        out_specs=pl.BlockSpec((tm, tn), lambda i, j, order_ref: (i, j)))
    return _call(body, name="mm_d_h", grid_spec=grid_spec, out_shape=jax.ShapeDtypeStruct((s, d), F32),
                 compiler_params=_params(("parallel", "parallel"), VMEM_LIMIT_BIG))(
                     order, *([dz] * nq), *ws, *after)


def _seg_mean(sq, segb):
    parts = []
    for cgrp in range(sq.shape[1] // SEG):
        blk = sq[:, cgrp * SEG:(cgrp + 1) * SEG]
        hi = blk.astype(BF16)
        r1 = blk - hi.astype(F32)
        mid = r1.astype(BF16)
        lo = (r1 - mid.astype(F32)).astype(BF16)
        acc = jnp.dot(hi, segb, preferred_element_type=F32)
        acc += jnp.dot(mid, segb, preferred_element_type=F32)
        acc += jnp.dot(lo, segb, preferred_element_type=F32)
        parts.append(acc)
    out = parts[0] if len(parts) == 1 else jnp.concatenate(parts, axis=1)
    return out * (1.0 / HEAD_DIM)


def _rope(v, cos, sa, sb):
    parts = []
    for cgrp in range(v.shape[1] // LANES):
        blk = v[:, cgrp * LANES:(cgrp + 1) * LANES]
        parts.append(blk * cos + pltpu.roll(blk, LANES - 8, 1) * sa + pltpu.roll(blk, 8, 1) * sb)
    return parts[0] if len(parts) == 1 else jnp.concatenate(parts, axis=1)


def _rope_t(dv, cos, sa, sb):
    parts = []
    for cgrp in range(dv.shape[1] // LANES):
        blk = dv[:, cgrp * LANES:(cgrp + 1) * LANES]
        parts.append(blk * cos + pltpu.roll(blk * sa, 8, 1) + pltpu.roll(blk * sb, LANES - 8, 1))
    return parts[0] if len(parts) == 1 else jnp.concatenate(parts, axis=1)


def _tile_lanes(vec, width):
    reps = width // LANES
    return vec if reps == 1 else jnp.tile(vec, (1, reps))


def qk_prep_fwd(z, tabs, qg, kg, segb, aw, after=()):
    s = z.shape[0]
    tm = _tile(s, 512)
    wq = aw + 2 * KV_WIDTH

    def body(z_ref, cos_ref, sa_ref, sb_ref, qg_ref, kg_ref, seg_ref, qs_ref, ks_ref, vb_ref):
        zz = z_ref[...]
        q, k, v = zz[:, :aw], zz[:, aw:aw + KV_WIDTH], zz[:, aw + KV_WIDTH:]
        cos, sa, sb, segm = cos_ref[...], sa_ref[...], sb_ref[...], seg_ref[...]
        rq = lax.rsqrt(_seg_mean(q * q, segm) + EPS)
        qn = (q * rq) * _tile_lanes(qg_ref[...], aw)
        qs_ref[...] = (_rope(qn, cos, sa, sb) * (HEAD_DIM ** -0.5)).astype(BF16)
        rk = lax.rsqrt(_seg_mean(k * k, segm) + EPS)
        kn = (k * rk) * _tile_lanes(kg_ref[...], KV_WIDTH)
        ks_ref[...] = _rope(kn, cos, sa, sb).astype(BF16)
        vb_ref[...] = v.astype(BF16)

    tab = pl.BlockSpec((tm, LANES), lambda i: (i, 0))
    vec = pl.BlockSpec((1, LANES), lambda i: (0, 0))
    return _call_after(body, after, name="qk_prep_fwd", grid=(s // tm,),
                       in_specs=[pl.BlockSpec((tm, wq), lambda i: (i, 0)), tab, tab, tab, vec, vec,
                                 pl.BlockSpec((SEG, SEG), lambda i: (0, 0))],
                       out_specs=[pl.BlockSpec((tm, aw), lambda i: (i, 0)),
                                  pl.BlockSpec((tm, KV_WIDTH), lambda i: (i, 0)),
                                  pl.BlockSpec((tm, KV_WIDTH), lambda i: (i, 0))],
                       out_shape=[jax.ShapeDtypeStruct((s, aw), BF16), jax.ShapeDtypeStruct((s, KV_WIDTH), BF16),
                                  jax.ShapeDtypeStruct((s, KV_WIDTH), BF16)],
                       compiler_params=_params(("parallel",)))(z, *tabs, qg, kg, segb)


def qk_prep_bwd(z, dqs, dks, dvs, tabs, qg, kg, segb, dz, aw):
    s = z.shape[0]
    tm = _tile(s, 512)
    wq = aw + 2 * KV_WIDTH

    def body(z_ref, dq_ref, dk_ref, dv_ref, cos_ref, sa_ref, sb_ref, qg_ref, kg_ref, seg_ref, dz_in,
             dz_ref, acc_ref):
        i = pl.program_id(0)
        zz = z_ref[...]
        q, k = zz[:, :aw], zz[:, aw:aw + KV_WIDTH]
        cos, sa, sb, segm = cos_ref[...], sa_ref[...], sb_ref[...], seg_ref[...]

        def one(xv, dout, gvec, width):
            g = _tile_lanes(gvec, width)
            r = lax.rsqrt(_seg_mean(xv * xv, segm) + EPS)
            xhat = xv * r
            dn = _rope_t(dout, cos, sa, sb)
            gd = dn * g
            dx = r * (gd - xhat * _seg_mean(xhat * gd, segm))
            contrib = jnp.sum(dn * xhat, axis=0, keepdims=True)
            folded = contrib[:, :LANES]
            for cgrp in range(1, width // LANES):
                folded = folded + contrib[:, cgrp * LANES:(cgrp + 1) * LANES]
            return dx, folded + pltpu.roll(folded, HEAD_DIM, 1)

        dq, gq = one(q, dq_ref[...] * (HEAD_DIM ** -0.5), qg_ref[...], aw)
        dk, gk = one(k, dk_ref[...], kg_ref[...], KV_WIDTH)
        dz_ref[:, :aw] = dq.astype(BF16)
        dz_ref[:, aw:aw + KV_WIDTH] = dk.astype(BF16)
        dz_ref[:, aw + KV_WIDTH:] = dv_ref[...].astype(BF16)

        @pl.when(i == 0)
        def _():
            acc_ref[...] = jnp.zeros_like(acc_ref)

        acc_ref[0:1, :] += gq
        acc_ref[1:2, :] += gk

    tab = pl.BlockSpec((tm, LANES), lambda i: (i, 0))
    vec = pl.BlockSpec((1, LANES), lambda i: (0, 0))
    kvb = pl.BlockSpec((tm, KV_WIDTH), lambda i: (i, 0))
    return _call(body, name="qk_prep_bwd", grid=(s // tm,),
                 in_specs=[pl.BlockSpec((tm, wq), lambda i: (i, 0)),
                           pl.BlockSpec((tm, aw), lambda i: (i, 0)), kvb, kvb, tab, tab, tab, vec, vec,
                           pl.BlockSpec((SEG, SEG), lambda i: (0, 0)), _hbm()],
                 out_specs=[pl.BlockSpec((tm, wq), lambda i: (i, 0)),
                            pl.BlockSpec((SUBLANES, LANES), lambda i: (0, 0))],
                 out_shape=[jax.ShapeDtypeStruct(dz.shape, BF16), jax.ShapeDtypeStruct((SUBLANES, LANES), F32)],
                 input_output_aliases={10: 0},
                 compiler_params=_params(("arbitrary",)))(z, dqs, dks, dvs, *tabs, qg, kg, segb, dz)


def _placed(band, lane_idx):
    out = {}
    for kh in range(N_KV_HEADS):
        grp = band[:, (kh // 2) * LANES:(kh // 2 + 1) * LANES]
        for half in (0, 1):
            t = grp if half == kh % 2 else pltpu.roll(grp, HEAD_DIM, 1)
            keep = (lane_idx >= half * HEAD_DIM) & (lane_idx < (half + 1) * HEAD_DIM)
            out[kh, half] = jnp.where(keep, t, jnp.zeros_like(t))
    return out


def _softmax_with_sink(sc, valid, sink):
    sc = jnp.where(valid, sc, NEG_INF)
    m = jnp.maximum(jnp.max(sc, axis=-1, keepdims=True), sink)
    e = jnp.exp(sc - m)
    es = jnp.exp(sink - m)
    den = jnp.sum(e, axis=-1, keepdims=True) + es
    return e / den, es / den


def _stack_halves(placed, kh):
    return jnp.concatenate([placed[kh, 0], placed[kh, 1]], axis=0)


def _valid_mask(n):
    r_i = lax.broadcasted_iota(jnp.int32, (BLOCK, 2 * BLOCK), 0)
    j_i = lax.broadcasted_iota(jnp.int32, (BLOCK, 2 * BLOCK), 1)
    return (j_i > r_i) & (j_i <= r_i + BLOCK) & ((n > 0) | (j_i >= BLOCK))


def attn_fwd(qs, ks, vb, z, sinks, aw, d, ga_off):
    s = qs.shape[0]
    nb = s // BLOCK
    nq = aw // HEAD_DIM
    grp_sz = nq // N_KV_HEADS
    n_ga = aw // COLT

    def body(q_ref, ko_ref, kp_ref, vo_ref, vp_ref, *rest):
        ga_refs = rest[:n_ga]
        sink_ref, attn_ref, mix_ref = rest[n_ga:]
        n = pl.program_id(0)
        lane_idx = lax.broadcasted_iota(jnp.int32, (2 * BLOCK, LANES), 1)
        kpl = _placed(jnp.concatenate([kp_ref[...], ko_ref[...]], axis=0), lane_idx)
        vpl = _placed(jnp.concatenate([vp_ref[...], vo_ref[...]], axis=0), lane_idx)
        valid = _valid_mask(n)
        groups = range(nq // 2)
        kcat = [_stack_halves(kpl, kh) for kh in range(N_KV_HEADS)]
        vcat = [_stack_halves(vpl, kh) for kh in range(N_KV_HEADS)]
        scs = [lax.dot_general(q_ref[:, c * LANES:(c + 1) * LANES], kcat[2 * c // grp_sz], NT,
                               preferred_element_type=F32) for c in groups]
        probs = [jnp.concatenate(
            [_softmax_with_sink(scs[c][:, half * 2 * BLOCK:(half + 1) * 2 * BLOCK], valid,
                                sink_ref[0, 2 * c + half])[0].astype(BF16) for half in (0, 1)], axis=1)
                 for c in groups]
        for c in groups:
            acc = jnp.dot(probs[c], vcat[2 * c // grp_sz], preferred_element_type=F32)
            attn_ref[:, c * LANES:(c + 1) * LANES] = acc
            col = c * LANES
            ga = ga_refs[col // COLT][:, col % COLT:col % COLT + LANES]
            mix_ref[:, c * LANES:(c + 1) * LANES] = (acc * (ga * _sigmoid(ga))).astype(BF16)

    own = lambda n: (n, 0)
    prev = lambda n: (jnp.maximum(n - 1, 0), 0)
    kvs = lambda imap: pl.BlockSpec((BLOCK, KV_WIDTH), imap)
    ga_specs = [pl.BlockSpec((BLOCK, COLT), functools.partial(lambda n, j: (n, ga_off + j), j=j))
                for j in range(n_ga)]
    return _call(body, name="attn_fwd", grid=(nb,),
                 in_specs=[pl.BlockSpec((BLOCK, aw), own), kvs(own), kvs(prev), kvs(own), kvs(prev)]
                 + ga_specs + [pl.BlockSpec(memory_space=pltpu.SMEM)],
                 out_specs=[pl.BlockSpec((BLOCK, aw), own), pl.BlockSpec((BLOCK, aw), own)],
                 out_shape=[jax.ShapeDtypeStruct((s, aw), F32), jax.ShapeDtypeStruct((s, d), BF16)],
                 compiler_params=_params(("parallel",)))(qs, ks, ks, vb, vb, *([z] * n_ga), sinks)


def gate_bwd(d_mix, attn, z, aw, ga_off, after=()):
    s, in_w = z.shape
    tm = _tile(s, 1024)

    def body(dm_ref, at_ref, ga_ref, do_ref, dz_ref):
        ga = ga_ref[...]
        sig = _sigmoid(ga)
        dm = dm_ref[...]
        do_ref[...] = (dm * (ga * sig)).astype(BF16)
        dz_ref[...] = ((dm * at_ref[...]) * (sig * (1.0 + ga * (1.0 - sig)))).astype(BF16)

    blk = pl.BlockSpec((tm, COLT), lambda i, j: (i, j))
    gab = pl.BlockSpec((tm, COLT), lambda i, j: (i, ga_off + j))
    return _call_after(body, after, name="gate_bwd", grid=(s // tm, aw // COLT),
                       in_specs=[blk, blk, gab], out_specs=[blk, gab],
                       out_shape=[jax.ShapeDtypeStruct((s, aw), BF16), jax.ShapeDtypeStruct((s, in_w), BF16)],
                       compiler_params=_params(("parallel", "parallel")))(d_mix, attn, z)


def attn_bwd(qs, ks, vb, d_o, sinks, aw):
    s = qs.shape[0]
    nb = s // BLOCK
    nq = aw // HEAD_DIM
    grp_sz = nq // N_KV_HEADS

    def body(q_ref, ko_ref, kp_ref, vo_ref, vp_ref, do_ref, sink_ref, dq_ref, dk_ref, dv_ref, ds_ref,
             ck_ref, cv_ref):
        n = pl.program_id(0)

        @pl.when(n == 0)
        def _():
            ds_ref[...] = jnp.zeros_like(ds_ref)
            dk_ref[...] = jnp.zeros_like(dk_ref)
            dv_ref[...] = jnp.zeros_like(dv_ref)

        @pl.when(n < nb)
        def _():
            lane_idx = lax.broadcasted_iota(jnp.int32, (2 * BLOCK, LANES), 1)
            lane_row = lax.broadcasted_iota(jnp.int32, (1, LANES), 1)
            kpl = _placed(jnp.concatenate([kp_ref[...], ko_ref[...]], axis=0), lane_idx)
            vpl = _placed(jnp.concatenate([vp_ref[...], vo_ref[...]], axis=0), lane_idx)
            valid = _valid_mask(n)
            ds_row = jnp.zeros((1, LANES), F32)
            wide = 2 * BLOCK
            groups = range(nq // 2)
            per_kv = grp_sz // 2
            kcat = [_stack_halves(kpl, kh) for kh in range(N_KV_HEADS)]
            vcat = [_stack_halves(vpl, kh) for kh in range(N_KV_HEADS)]
            qg = [q_ref[:, c * LANES:(c + 1) * LANES] for c in groups]
            dog = [do_ref[:, c * LANES:(c + 1) * LANES] for c in groups]
            scs = [lax.dot_general(qg[c], kcat[c // per_kv], NT, preferred_element_type=F32) for c in groups]
            dps = [lax.dot_general(dog[c], vcat[c // per_kv], NT, preferred_element_type=F32) for c in groups]
            ds_pairs, p_pairs = [], []
            for c in groups:
                probs, dss = [], []
                for half in (0, 1):
                    h = 2 * c + half
                    cols = slice(half * wide, (half + 1) * wide)
                    p, p_sink = _softmax_with_sink(scs[c][:, cols], valid, sink_ref[0, h])
                    delta = jnp.sum(p * dps[c][:, cols], axis=-1, keepdims=True)
                    dss.append((p * (dps[c][:, cols] - delta)).astype(BF16))
                    probs.append(p.astype(BF16))
                    dsink = -jnp.sum(p_sink * delta, axis=0, keepdims=True)
                    ds_row += jnp.where(lane_row == h, dsink, 0.0)
                ds_pairs.append(jnp.concatenate(dss, axis=1))
                p_pairs.append(jnp.concatenate(probs, axis=1))
            for c in groups:
                dq_ref[:, c * LANES:(c + 1) * LANES] = jnp.dot(ds_pairs[c], kcat[c // per_kv],
                                                               preferred_element_type=F32)
            dkts = [lax.dot_general(qg[c], ds_pairs[c], TN, preferred_element_type=F32) for c in groups]
            dvts = [lax.dot_general(dog[c], p_pairs[c], TN, preferred_element_type=F32) for c in groups]
            dkt, dvt = [], []
            for kh in range(N_KV_HEADS):
                for parts, out in ((dkts, dkt), (dvts, dvt)):
                    tot = parts[kh * per_kv]
                    for c in range(kh * per_kv + 1, (kh + 1) * per_kv):
                        tot = tot + parts[c]
                    out.append(tot[:HEAD_DIM, :wide] + tot[HEAD_DIM:, wide:])
            ds_ref[0:1, :] += ds_row
            back = lambda t: jnp.concatenate(
                [jnp.concatenate(t[2 * g:2 * g + 2], axis=0).T for g in range(N_KV_HEADS // 2)], axis=1)
            dk_band, dv_band = back(dkt), back(dvt)

            @pl.when(n > 0)
            def _():
                dk_ref[...] = ck_ref[...] + dk_band[:BLOCK]
                dv_ref[...] = cv_ref[...] + dv_band[:BLOCK]

            ck_ref[...] = dk_band[BLOCK:]
            cv_ref[...] = dv_band[BLOCK:]

        @pl.when(n == nb)
        def _():
            dk_ref[...] = ck_ref[...]
            dv_ref[...] = cv_ref[...]

    own = lambda n: (jnp.minimum(n, nb - 1), 0)
    prev = lambda n: (jnp.clip(n - 1, 0, nb - 1), 0)
    done = lambda n: (jnp.maximum(n - 1, 0), 0)
    kvs = lambda imap: pl.BlockSpec((BLOCK, KV_WIDTH), imap)
    return _call(body, name="attn_bwd", grid=(nb + 1,),
                 in_specs=[pl.BlockSpec((BLOCK, aw), own), kvs(own), kvs(prev), kvs(own), kvs(prev),
                           pl.BlockSpec((BLOCK, aw), own), pl.BlockSpec(memory_space=pltpu.SMEM)],
                 out_specs=[pl.BlockSpec((BLOCK, aw), own), kvs(done), kvs(done),
                            pl.BlockSpec((SUBLANES, LANES), lambda n: (0, 0))],
                 out_shape=[jax.ShapeDtypeStruct((s, aw), F32), jax.ShapeDtypeStruct((s, KV_WIDTH), F32),
                            jax.ShapeDtypeStruct((s, KV_WIDTH), F32), jax.ShapeDtypeStruct((SUBLANES, LANES), F32)],
                 scratch_shapes=[pltpu.VMEM((BLOCK, KV_WIDTH), F32), pltpu.VMEM((BLOCK, KV_WIDTH), F32)],
                 compiler_params=_params(("arbitrary",)))(qs, ks, ks, vb, vb, d_o, sinks)


def conv_fwd(z, conv_w, mix, aw, cw, b_off):
    s = z.shape[0]
    tm = _tile(s, 1024)
    nseg = cw // COLT
    hb = tm // SUBLANES

    def body(b_ref, c_ref, h_ref, g_ref, cp_ref, hp_ref, w_ref, mix_in, mix_ref):
        i = pl.program_id(0)
        u = c_ref[...] * h_ref[...]
        up = jnp.where(i > 0, cp_ref[...] * hp_ref[...], 0.0)
        ext = jnp.concatenate([up, u], axis=0)
        um1 = pltpu.roll(ext, 1, 0)[SUBLANES:]
        um2 = pltpu.roll(ext, 2, 0)[SUBLANES:]
        w = w_ref[...]
        cv = w[0:1] * um2 + w[1:2] * um1 + w[2:3] * u
        g = g_ref[...]
        mix_ref[...] = ((b_ref[...] * cv) * (g * _sigmoid(g))).astype(BF16)

    seg = lambda k: pl.BlockSpec((tm, COLT), functools.partial(lambda i, j, k: (i, b_off + k * nseg + j), k=k))
    halo = lambda k: pl.BlockSpec(
        (SUBLANES, COLT), functools.partial(lambda i, j, k: (jnp.maximum(i * hb - 1, 0), b_off + k * nseg + j), k=k))
    return _call(body, name="conv_fwd", grid=(s // tm, nseg),
                 in_specs=[seg(0), seg(1), seg(2), seg(3), halo(1), halo(2),
                           pl.BlockSpec((SUBLANES, COLT), lambda i, j: (0, j)), _hbm()],
                 out_specs=pl.BlockSpec((tm, COLT), lambda i, j: (i, aw // COLT + j)),
                 out_shape=jax.ShapeDtypeStruct(mix.shape, BF16),
                 input_output_aliases={7: 0},
                 compiler_params=_params(("parallel", "parallel")))(z, z, z, z, z, z, conv_w, mix)


def conv_bwd(z, d_mix, conv_w, dz, aw, cw, b_off):
    s = z.shape[0]
    tm = _tile(s, 512)
    nseg = cw // COLT
    hb = tm // SUBLANES
    n_row = s // tm
    last_h = s // SUBLANES - 1
    n_step = nseg * n_row

    def body(b_ref, c_ref, h_ref, g_ref, cp_ref, hp_ref, bn_ref, gn_ref, dm_ref, dmn_ref, w_ref, dz_in,
             dz_ref, acc_ref, stash_ref, sems):
        j = pl.program_id(0)
        i = pl.program_id(1)
        step = j * n_row + i
        slot = step % 2

        def copies(from_slot, at_step):
            jj, ii = at_step // n_row, at_step % n_row
            rows = pl.ds(pl.multiple_of(ii * tm, tm), tm)
            return [pltpu.make_async_copy(
                stash_ref.at[from_slot, q],
                dz_ref.at[rows, pl.ds(pl.multiple_of((b_off + q * nseg + jj) * COLT, COLT), COLT)],
                sems.at[from_slot, q]) for q in range(4)]

        @pl.when(step >= 2)
        def _():
            for cp in copies(slot, step - 2):
                cp.wait()

        cc, hh, bb, g = c_ref[...], h_ref[...], b_ref[...], g_ref[...]
        w = w_ref[...]
        u = cc * hh
        up = jnp.where(i > 0, cp_ref[...] * hp_ref[...], 0.0)
        ext = jnp.concatenate([up, u], axis=0)
        um1 = pltpu.roll(ext, 1, 0)[SUBLANES:]
        um2 = pltpu.roll(ext, 2, 0)[SUBLANES:]
        cv = w[0:1] * um2 + w[1:2] * um1 + w[2:3] * u
        sig = _sigmoid(g)
        sg = g * sig
        dm = dm_ref[...]
        d_cv = (dm * bb) * sg
        gn = gn_ref[...]
        d_cv_next = jnp.where(i < n_row - 1, (dmn_ref[...] * bn_ref[...]) * (gn * _sigmoid(gn)), 0.0)
        ext2 = jnp.concatenate([d_cv, d_cv_next], axis=0)
        dp1 = pltpu.roll(ext2, tm + SUBLANES - 1, 0)[:tm]
        dp2 = pltpu.roll(ext2, tm + SUBLANES - 2, 0)[:tm]
        d_u = w[2:3] * d_cv + w[1:2] * dp1 + w[0:1] * dp2
        stash_ref[slot, 0] = ((dm * cv) * sg).astype(BF16)
        stash_ref[slot, 1] = (d_u * hh).astype(BF16)
        stash_ref[slot, 2] = (d_u * cc).astype(BF16)
        stash_ref[slot, 3] = (((dm * bb) * cv) * (sig * (1.0 + g * (1.0 - sig)))).astype(BF16)
        for cp in copies(slot, step):
            cp.start()

        @pl.when(i == 0)
        def _():
            acc_ref[...] = jnp.zeros_like(acc_ref)

        acc_ref[0:1, :] += jnp.sum(d_cv * um2, axis=0, keepdims=True)
        acc_ref[1:2, :] += jnp.sum(d_cv * um1, axis=0, keepdims=True)
        acc_ref[2:3, :] += jnp.sum(d_cv * u, axis=0, keepdims=True)

        @pl.when(step == n_step - 1)
        def _():
            if n_step >= 2:
                for cp in copies(1 - slot, step - 1):
                    cp.wait()
            for cp in copies(slot, step):
                cp.wait()

    seg = lambda q: pl.BlockSpec((tm, COLT), functools.partial(lambda j, i, q: (i, b_off + q * nseg + j), q=q))
    halo_p = lambda q: pl.BlockSpec(
        (SUBLANES, COLT),
        functools.partial(lambda j, i, q: (jnp.maximum(i * hb - 1, 0), b_off + q * nseg + j), q=q))
    halo_n = lambda q: pl.BlockSpec(
        (SUBLANES, COLT),
        functools.partial(lambda j, i, q: (jnp.minimum((i + 1) * hb, last_h), b_off + q * nseg + j), q=q))
    return _call(body, name="conv_bwd", grid=(nseg, n_row),
                 in_specs=[seg(0), seg(1), seg(2), seg(3), halo_p(1), halo_p(2), halo_n(0), halo_n(3),
                           pl.BlockSpec((tm, COLT), lambda j, i: (i, aw // COLT + j)),
                           pl.BlockSpec((SUBLANES, COLT),
                                        lambda j, i: (jnp.minimum((i + 1) * hb, last_h), aw // COLT + j)),
                           pl.BlockSpec((SUBLANES, COLT), lambda j, i: (0, j)), _hbm()],
                 out_specs=[_hbm(), pl.BlockSpec((SUBLANES, COLT), lambda j, i: (0, j))],
                 out_shape=[jax.ShapeDtypeStruct(dz.shape, BF16), jax.ShapeDtypeStruct((SUBLANES, cw), F32)],
                 input_output_aliases={11: 0},
                 scratch_shapes=[pltpu.VMEM((2, 4, tm, COLT), BF16), pltpu.SemaphoreType.DMA((2, 4))],
                 compiler_params=_params(("arbitrary", "arbitrary")))(
                     z, z, z, z, z, z, z, z, d_mix, d_mix, conv_w, dz)


def _place():
    x, y, c = lax.axis_index("x"), lax.axis_index("y"), lax.axis_index("c")
    chips = [(1 - x, y), (x, 1 - y), (1 - x, 1 - y)]
    return x, y, c, chips


def _remote(src, dst, send_sem, recv_sem, device):
    return pltpu.make_async_remote_copy(src_ref=src, dst_ref=dst, send_sem=send_sem, recv_sem=recv_sem,
                                        device_id=device, device_id_type=MESH)


def plan_gather_ici(n_split):
    def plan(refs, send, recv):
        x, y, c, chips = _place()
        me = 2 * x + y
        mine, theirs = [], []
        for w, ref in enumerate(refs):
            for j, (cx, cy) in enumerate(chips):
                def part(slot):
                    if w >= n_split:
                        return ref.at[slot]
                    hr = ref.shape[1] // 2
                    return ref.at[slot, pl.ds(c * hr, hr)]
                k = 3 * w + j
                mine.append(_remote(part(me), part(me), send.at[k], recv.at[k], (cx, cy, c)))
                theirs.append(_remote(part(2 * cx + cy), part(2 * cx + cy), send.at[k], recv.at[k], (cx, cy, c)))
        return mine, theirs
    return plan


def plan_shard_ici(j):
    def plan(refs, send, recv):
        _, _, c, chips = _place()
        own, land = refs
        hr = own.shape[0] // 2
        rows = pl.ds(c * hr, hr)
        cp = _remote(own.at[rows], land.at[rows], send.at[0], recv.at[0], (*chips[j], c))
        return [cp], [cp]
    return plan


def plan_shard_relay(refs, send, recv):
    _, _, c, chips = _place()
    land_x, land_y, land_far = refs
    hr = land_far.shape[0] // 2
    quarter = lambda q: pl.ds(c * hr + q * (hr // 2), hr // 2)
    mine = [_remote(land_y.at[quarter(0)], land_far.at[quarter(0)], send.at[0], recv.at[0], (*chips[0], c)),
            _remote(land_x.at[quarter(1)], land_far.at[quarter(1)], send.at[1], recv.at[1], (*chips[1], c))]
    return mine, mine


def plan_shard_pass(refs, send, recv):
    x, y, c, _ = _place()
    mine, theirs = [], []
    for w, land in enumerate(refs):
        hr = land.shape[0] // 2
        half = lambda core: land.at[pl.ds(core * hr, hr)]
        mine.append(_remote(half(c), half(c), send.at[w], recv.at[w], (x, y, 1 - c)))
        theirs.append(_remote(half(1 - c), half(1 - c), send.at[w], recv.at[w], (x, y, 1 - c)))
    return mine, theirs


def plan_gather_pass(refs, send, recv):
    x, y, c, chips = _place()
    mine, theirs = [], []
    for w, ref in enumerate(refs):
        hr = ref.shape[1] // 2
        for j, (cx, cy) in enumerate(chips):
            half = lambda core: ref.at[2 * cx + cy, pl.ds(core * hr, hr)]
            k = 3 * w + j
            mine.append(_remote(half(c), half(c), send.at[k], recv.at[k], (x, y, 1 - c)))
            theirs.append(_remote(half(1 - c), half(1 - c), send.at[k], recv.at[k], (x, y, 1 - c)))
    return mine, theirs


def plan_swap(refs, send, recv):
    x, y, c, _ = _place()
    nw = len(refs) // 2
    mine = []
    for w in range(nw):
        hr = refs[w].shape[1] // 2
        mine.append(_remote(refs[w].at[:, pl.ds((1 - c) * hr, hr), :], refs[nw + w], send.at[w], recv.at[w],
                            (x, y, 1 - c)))
    return mine, mine


def plan_scatter(refs, send, recv):
    x, y, c, chips = _place()
    me = 2 * x + y
    nw = len(refs) // 2
    mine, theirs = [], []
    for w in range(nw):
        for j, (cx, cy) in enumerate(chips):
            k = 3 * w + j
            mine.append(_remote(refs[w].at[2 * cx + cy], refs[nw + w].at[me], send.at[k], recv.at[k], (cx, cy, c)))
            theirs.append(_remote(refs[w].at[me], refs[nw + w].at[2 * cx + cy], send.at[k], recv.at[k], (cx, cy, c)))
    return mine, theirs


def plan_join(refs, send, recv):
    x, y, c, _ = _place()
    mine, theirs = [], []
    for w, ref in enumerate(refs):
        hr = ref.shape[0] // 2
        half = lambda core: ref.at[pl.ds(core * hr, hr)]
        mine.append(_remote(half(c), half(c), send.at[w], recv.at[w], (x, y, 1 - c)))
        theirs.append(_remote(half(1 - c), half(1 - c), send.at[w], recv.at[w], (x, y, 1 - c)))
    return mine, theirs


def exchange(name, plan, arrays, n, after=()):
    na = len(arrays)

    def body(*refs):
        mine, theirs = plan(refs[:na], refs[2 * na], refs[2 * na + 1])
        for cp in mine:
            cp.start()
        for cp in theirs:
            cp.wait_recv()
        for cp in mine:
            cp.wait_send()

    return _call_after(body, after, name=name, in_specs=[_hbm()] * na, out_specs=[_hbm()] * na,
                       out_shape=[jax.ShapeDtypeStruct(a.shape, a.dtype) for a in arrays],
                       input_output_aliases={i: i for i in range(na)},
                       scratch_shapes=[pltpu.SemaphoreType.DMA((n,)), pltpu.SemaphoreType.DMA((n,))])(*arrays)


def exchange_start(name, groups, arrays, after=()):
    na, ng = len(arrays), len(groups)

    def body(*refs):
        for g, (plan, idx, _) in enumerate(groups):
            mine, _ = plan([refs[i] for i in idx], refs[na + 2 * g], refs[na + 2 * g + 1])
            for cp in mine:
                cp.start()
        refs[-1][...] = jnp.zeros_like(refs[-1])

    hbm = pl.BlockSpec(memory_space=pltpu.HBM)
    sem = pl.BlockSpec(memory_space=pltpu.SEMAPHORE)
    sem_types = [pltpu.SemaphoreType.DMA((n,)) for _, _, n in groups for _ in (0, 1)]
    outs = _call_after(body, after, name=name, in_specs=[hbm] * na,
                       out_specs=[sem] * (2 * ng) + [hbm] * na + [pl.BlockSpec(memory_space=pltpu.VMEM)],
                       out_shape=sem_types + [pltpu.HBM(a.shape, a.dtype) for a in arrays]
                       + [jax.ShapeDtypeStruct((SUBLANES, LANES), F32)],
                       input_output_aliases={i: i + 2 * ng for i in range(na)},
                       compiler_params=pltpu.CompilerParams(
                           has_side_effects=pltpu.SideEffectType.DATAFLOW_SIDE_EFFECTING))(
                               *[pltpu.with_memory_space_constraint(a, pltpu.HBM) for a in arrays])
    sems = [(outs[2 * g], outs[2 * g + 1]) for g in range(ng)]
    return sems, list(outs[2 * ng:-1]), outs[-1]


def exchange_wait(name, plan, sems, arrays, after):
    send_sems, recv_sems = sems
    na = len(arrays)

    def body(*refs):
        mine, theirs = plan(refs[:na], refs[na], refs[na + 1])
        for cp in mine:
            cp.wait_send()
        for cp in theirs:
            cp.wait_recv()

    hbm = pl.BlockSpec(memory_space=pltpu.HBM)
    sem = pl.BlockSpec(memory_space=pltpu.SEMAPHORE)
    return _call(body, name=name, in_specs=[hbm] * na + [sem, sem, _hbm()], out_specs=[hbm] * na,
                 out_shape=[pltpu.HBM(a.shape, a.dtype) for a in arrays],
                 input_output_aliases={i: i for i in range(na)},
                 compiler_params=pltpu.CompilerParams(
                     has_side_effects=pltpu.SideEffectType.DATAFLOW_SIDE_EFFECTING))(
                         *arrays, send_sems, recv_sems, after)


def plan_allgather_small(refs, send, recv):
    x, y, c, _ = _place()
    buf, = refs
    mine, theirs = [], []
    flips = [(fx, fy, fc) for fx in (0, 1) for fy in (0, 1) for fc in (0, 1)][1:]
    for k, (fx, fy, fc) in enumerate(flips):
        peer = (x ^ fx, y ^ fy, c ^ fc)
        slot = lambda px, py, pc: buf.at[4 * px + 2 * py + pc]
        mine.append(_remote(slot(x, y, c), slot(x, y, c), send.at[k], recv.at[k], peer))
        theirs.append(_remote(slot(*peer), slot(*peer), send.at[k], recv.at[k], peer))
    return mine, theirs


def add_sibling(grad, got, core, name):
    _, r, c = grad.shape
    hr = r // 2
    tr = _tile(hr, 512)
    nblk = hr // tr

    def body(core_ref, g_ref, o_ref, out_ref):
        out_ref[...] = (g_ref[...].astype(F32) + o_ref[...].astype(F32)).astype(BF16)

    grid_spec = pltpu.PrefetchScalarGridSpec(
        num_scalar_prefetch=1, grid=(N_CHIPS, nblk),
        in_specs=[pl.BlockSpec((None, tr, c), lambda t, i, core_ref: (t, core_ref[0] * nblk + i, 0)),
                  pl.BlockSpec((None, tr, c), lambda t, i, core_ref: (t, i, 0))],
        out_specs=pl.BlockSpec((None, tr, c), lambda t, i, core_ref: (t, i, 0)))
    return _call(body, name=name, grid_spec=grid_spec,
                 out_shape=jax.ShapeDtypeStruct((N_CHIPS, hr, c), BF16),
                 compiler_params=_params(("parallel", "parallel")))(core, grad, got)


def sum_chips(mine, owned, place, name):
    _, hr, c = mine.shape
    tr = _tile(hr, 512)
    nblk = hr // tr

    def body(place_ref, m_ref, o1_ref, o2_ref, o3_ref, out_ref):
        acc = m_ref[...].astype(F32)
        for o_ref in (o1_ref, o2_ref, o3_ref):
            acc = acc + o_ref[...].astype(F32)
        out_ref[...] = acc

    other = lambda k: pl.BlockSpec((None, tr, c), lambda i, place_ref: ((place_ref[0] + k) % N_CHIPS, i, 0))
    grid_spec = pltpu.PrefetchScalarGridSpec(
        num_scalar_prefetch=1, grid=(nblk,),
        in_specs=[other(0), other(1), other(2), other(3)],
        out_specs=pl.BlockSpec((tr, c), lambda i, place_ref: (place_ref[1] * nblk + i, 0)))
    return _call(body, name=name, grid_spec=grid_spec,
                 out_shape=jax.ShapeDtypeStruct((2 * hr, c), F32),
                 compiler_params=_params(("parallel",)))(place, mine, owned, owned, owned)


def sum_devices(gathered):
    _, rows, width = gathered.shape

    def body(g_ref, out_ref):
        acc = g_ref[0]
        for dev in range(1, 8):
            acc = acc + g_ref[dev]
        out_ref[...] = acc
        tail = acc[SUBLANES:]
        out_ref[SUBLANES:, :] = jnp.broadcast_to(jnp.sum(tail, axis=1, keepdims=True), tail.shape)

    return _call(body, name="sum_devices",
                 in_specs=[pl.BlockSpec(memory_space=pltpu.VMEM)],
                 out_specs=pl.BlockSpec(memory_space=pltpu.VMEM),
                 out_shape=jax.ShapeDtypeStruct((rows, width), F32))(gathered)


def _rope_tables(s):
    half = ROT_DIM // 2
    inv_freq = jnp.power(jnp.float32(ROPE_THETA), -jnp.arange(half, dtype=F32) * 2.0 / ROT_DIM)
    freq64 = jnp.concatenate([inv_freq, inv_freq, jnp.zeros((HEAD_DIM - ROT_DIM,), F32)])
    freq = jnp.concatenate([freq64, freq64])[None, :]
    dim = (jnp.arange(LANES) % HEAD_DIM)[None, :]
    ang = jnp.arange(s).astype(F32)[:, None] * freq
    cos, sin = jnp.cos(ang), jnp.sin(ang)
    return cos, jnp.where(dim < half, -sin, 0.0), jnp.where((dim >= half) & (dim < ROT_DIM), sin, 0.0)


def _pad_rows(a, rows):
    return jnp.pad(a, ((0, rows - a.shape[0]), (0, 0)))


def _pad_cols(a, cols):
    return jnp.pad(a, ((0, 0), (0, cols - a.shape[1])))


def kernel(x, p, norm_gain, w_in, q_norm_gain, k_norm_gain, attn_sinks, conv_w, w_out, ple_gate_norm_gain, w_ple_gate, b_ple_gate, w_ple_proj, ple_norm_gain, loss_target, m_norm_gain, m_w_in, m_q_norm_gain, m_k_norm_gain, m_attn_sinks, m_conv_w, m_w_out, m_ple_gate_norm_gain, m_w_ple_gate, m_b_ple_gate, m_w_ple_proj, m_ple_norm_gain, v_norm_gain, v_w_in, v_q_norm_gain, v_k_norm_gain, v_attn_sinks, v_conv_w, v_w_out, v_ple_gate_norm_gain, v_w_ple_gate, v_b_ple_gate, v_w_ple_proj, v_ple_norm_gain):
    x2, p2, tgt = x[0], p[0, 0], loss_target[0]
    s, d = x2.shape
    ple = p2.shape[1]
    aw = d // 2
    cw = d - aw
    nq = aw // HEAD_DIM
    sh = w_in.shape[2]
    in_w = N_CHIPS * sh
    dq = d // N_CHIPS
    cq = cw // N_CHIPS
    ga_off = (aw + 2 * KV_WIDTH) // COLT
    b_off = (2 * aw + 2 * KV_WIDTH) // COLT
    assert in_w == 2 * aw + 2 * KV_WIDTH + 4 * cw and aw % COLT == 0 and cw % COLT == 0
    assert s % BLOCK == 0 and sh % LANES == 0 and nq % (2 * N_KV_HEADS) == 0

    core = lax.axis_index("c").astype(jnp.int32).reshape(1)
    chip = 2 * lax.axis_index("x") + lax.axis_index("y")

    chip1 = chip.astype(jnp.int32).reshape(1)
    place = jnp.concatenate([chip1, core])
    w_in_own = cast_bf16(w_in[0], "cast_w_in")
    rest = [cast_into_slot(w_out[0], chip1, "cast_w_out"), cast_into_slot(w_ple_gate[0], chip1, "cast_w_pg"),
            cast_into_slot(w_ple_proj[0], chip1, "cast_w_pp"),
            lax.dynamic_update_slice(jnp.zeros((N_CHIPS, SUBLANES, cq), F32),
                                     _pad_rows(conv_w[0], SUBLANES)[None], (chip, 0, 0))]
    order = jnp.stack([chip, chip ^ 2, chip ^ 1, chip ^ 3]).astype(jnp.int32)
    wi_sems, wi_arrs, wi_token = exchange_start(
        "gather_wi_start", [(plan_shard_ici(j), [0, 1 + j], 1) for j in range(2)],
        [w_in_own] + [lax.empty((d, sh), BF16) for _ in range(3)])

    tm = _tile(s, 1024)
    tn = _tile(d, 1024)
    tk = _tile(d, 2048)
    ts = _tile(s, 2048)
    h = rms_fwd(x2, norm_gain, "rms_fwd_x", after=(wi_token,))
    tabs = _rope_tables(s)
    qg = jnp.tile(q_norm_gain, (1, LANES // HEAD_DIM))
    kg = jnp.tile(k_norm_gain, (1, LANES // HEAD_DIM))
    seg_i = jnp.arange(SEG) // HEAD_DIM
    segb = (seg_i[:, None] == seg_i[None, :]).astype(BF16)
    z = in_proj_part(h, wi_arrs[0], None, order, 0, in_w)
    own, land_x = exchange_wait("gather_wi_wait0", plan_shard_ici(0), wi_sems[0], [wi_arrs[0], wi_arrs[1]], z)
    own, land_y = exchange_wait("gather_wi_wait1", plan_shard_ici(1), wi_sems[1], [own, wi_arrs[2]], land_x)
    relay_sems, relayed, relay_token = exchange_start(
        "gather_wi_relay_start", [(plan_shard_relay, [0, 1, 2], 2)], [land_x, land_y, wi_arrs[3]])
    rest_sems, rest, rest_token = exchange_start(
        "gather_rest_start", [(plan_gather_ici(3), [0, 1, 2, 3], 12)], rest, after=(relay_token,))
    land_x, land_y = exchange("gather_wi_pass01", plan_shard_pass, relayed[:2], 2, after=(rest_token,))
    z = in_proj_part(h, land_x, z, order, 1, in_w)
    z = in_proj_part(h, land_y, z, order, 2, in_w)
    land_x, land_y, land_far = exchange_wait("gather_wi_relay_wait", plan_shard_relay, relay_sems[0],
                                             [land_x, land_y, relayed[2]], z)
    land_far, = exchange("gather_wi_pass2", plan_shard_pass, [land_far], 1)
    z = in_proj_part(h, land_far, z, order, 3, in_w)
    w_shards = [own, land_x, land_y, land_far]
    wo_all, wg_all, wp_all, conv_all = exchange_wait("gather_rest_wait", plan_gather_ici(3), rest_sems[0], rest, z)
    pass_sems, passed, pass_token = exchange_start(
        "gather_rest_pass_start", [(plan_gather_pass, [0, 1, 2], 9)], [wo_all, wg_all, wp_all])
    conv_full = conv_all.transpose(1, 0, 2).reshape(SUBLANES, cw)
    qs, ks, vb = qk_prep_fwd(z, tabs, qg, kg, segb, aw, after=(pass_token,))
    attn, mix = attn_fwd(qs, ks, vb, z, attn_sinks, aw, d, ga_off)
    mix = conv_fwd(z, conv_full, mix, aw, cw, b_off)
    wo_all, wg_all, wp_all = exchange_wait("gather_rest_pass_wait", plan_gather_pass, pass_sems[0], passed, mix)
    wo_full = wo_all.reshape(d, d)
    wg_full = wg_all.reshape(d, d)
    sq = lambda shape: dict(
        a_spec=pl.BlockSpec((tm, tk), lambda i, j, k: (i, k)),
        o_spec=pl.BlockSpec((tm, tn), lambda i, j, k: (i, j)),
        out_shape=jax.ShapeDtypeStruct(shape, F32))
    x1 = matmul(mix, wo_full, grid=(s // tm, d // tn, d // tk),
                b_spec=pl.BlockSpec((tk, tn), lambda i, j, k: (k, j)), dims=NN, name="mm_x1",
                res=x2, res_spec=pl.BlockSpec((tm, tn), lambda i, j, k: (i, j)), **sq((s, d)))
    hg = rms_fwd(x1, ple_gate_norm_gain, "rms_fwd_x1")
    gl = matmul(hg, wg_full, grid=(s // tm, d // tn, d // tk),
                b_spec=pl.BlockSpec((tk, tn), lambda i, j, k: (k, j)), dims=NN, name="mm_gl", **sq((s, d)))
    pq = d // N_CHIPS

    d_gl, d_pe, dy, acc_head = head_fwd_bwd(x1, gl, p2, wp_all, tgt, b_ple_gate, ple_norm_gain)
    d_hg = matmul(d_gl, wg_full, grid=(s // tm, d // tn, d // tk),
                  b_spec=pl.BlockSpec((tn, tk), lambda i, j, k: (j, k)), dims=NT, name="mm_d_hg", **sq((s, d)))
    wgrad = lambda a_cols, shape3, o_spec, b_cols, name, a, b, grid: matmul(
        a, b, grid=grid,
        a_spec=pl.BlockSpec((ts, a_cols), lambda i, j, k: (k, i)),
        b_spec=pl.BlockSpec((ts, b_cols), lambda i, j, k: (k, j)),
        o_spec=o_spec, out_shape=jax.ShapeDtypeStruct(shape3, BF16), dims=TN, name=name)
    g_wg = wgrad(tn, (d, d), pl.BlockSpec((tn, tn), lambda i, j, k: (i, j)), tn, "mm_g_wg", hg, d_gl,
                 (d // tn, d // tn, s // ts))
    g_wp = wgrad(ple, (N_CHIPS, ple, pq), pl.BlockSpec((None, ple, pq), lambda i, j, k: (j, 0, 0)), pq,
                 "mm_g_wp", p2, d_pe, (1, N_CHIPS, s // ts))
    d_x1, d_x1b, acc_g = rms_bwd_add(d_hg, x1, dy, ple_gate_norm_gain, "rms_bwd_x1", True)
    d_mix = matmul(d_x1b, wo_full, grid=(s // tm, d // tn, d // tk),
                   b_spec=pl.BlockSpec((tn, tk), lambda i, j, k: (j, k)), dims=NT, name="mm_d_mix", **sq((s, d)))
    g_wo = wgrad(tn, (d, d), pl.BlockSpec((tn, tn), lambda i, j, k: (i, j)), tn, "mm_g_wo", mix, d_x1b,
                 (d // tn, d // tn, s // ts))
    def reduce_start(tag, grads):
        n = len(grads)
        lands = [lax.empty((N_CHIPS, a.shape[1] // 2, a.shape[2]), BF16) for a in grads]
        swapped = exchange("swap_" + tag, plan_swap, list(grads) + lands, n)
        parts = [add_sibling(g, o, core, "add_sibling_%s%d" % (tag, i))
                 for i, (g, o) in enumerate(zip(swapped[:n], swapped[n:]))]
        return exchange_start("scatter_%s_start" % tag, [(plan_scatter, list(range(2 * n)), 3 * n)],
                              parts + [lax.empty(a.shape, BF16) for a in parts])

    def reduce_sum(tag, handle, after):
        sems, arrays, _ = handle
        n = len(arrays) // 2
        got = exchange_wait("scatter_%s_wait" % tag, plan_scatter, sems[0], arrays, after)
        return [sum_chips(pt, o, place, "sum_chips_%s%d" % (tag, i))
                for i, (pt, o) in enumerate(zip(got[:n], got[n:]))]

    early = reduce_start("early", [g_wo.reshape(N_CHIPS, dq, d), g_wg.reshape(N_CHIPS, dq, d), g_wp])
    d_o, dz = gate_bwd(d_mix, attn, z, aw, ga_off, after=(early[-1],))
    dqs, dks, dvs, acc_sink = attn_bwd(qs, ks, vb, d_o, attn_sinks, aw)
    dz, acc_qk = qk_prep_bwd(z, dqs, dks, dvs, tabs, qg, kg, segb, dz, aw)
    dz, acc_conv = conv_bwd(z, d_mix, conv_full, dz, aw, cw, b_off)
    join_sems, joining, join_token = exchange_start(
        "join_early_start", [(plan_join, [0, 1, 2], 3)], reduce_sum("early", early, dz))
    g_wi = matmul(h, dz, grid=(d // tn, N_CHIPS, 1),
                  a_spec=pl.BlockSpec((s, tn), lambda i, j, k: (0, i), pipeline_mode=pl.Buffered(1)),
                  b_spec=pl.BlockSpec((s, sh), lambda i, j, k: (0, j)),
                  o_spec=pl.BlockSpec((None, tn, sh), lambda i, j, k: (j, i, 0)),
                  out_shape=jax.ShapeDtypeStruct((N_CHIPS, d, sh), BF16), dims=TN, name="mm_g_wi",
                  after=(join_token,), vmem=VMEM_LIMIT_BIG)
    full_wo, full_wg, full_wp = exchange_wait("join_early_wait", plan_join, join_sems[0], joining, g_wi)
    late = reduce_start("late", [g_wi])
    d_h = in_proj_bwd(dz, w_shards, order, d, after=(late[-1],))
    grad_x, acc_x = rms_bwd_add(d_h, x2, d_x1, norm_gain, "rms_bwd_x", False)

    wsm = max(d, cw)
    misc = jnp.concatenate([acc_qk[0:1, :HEAD_DIM], acc_qk[1:2, :HEAD_DIM], acc_sink[0:1, :nq]], axis=1)
    small = jnp.concatenate([
        _pad_cols(acc_x[0:1], wsm), _pad_cols(acc_g[0:1], wsm), _pad_cols(acc_head[0:1], wsm),
        _pad_cols(acc_head[1:2], wsm), _pad_cols(misc, wsm), _pad_cols(acc_conv[0:3], wsm),
        _pad_rows(_pad_cols(acc_head[2:3], wsm), SUBLANES)], axis=0)
    device = 2 * chip + lax.axis_index("c")
    small_sems, small_bufs, small_token = exchange_start(
        "small_start", [(plan_allgather_small, [0], 7)],
        [lax.dynamic_update_slice(jnp.zeros((8,) + small.shape, F32), small[None], (device, 0, 0))])
    last_sems, last_halves, last_token = exchange_start(
        "join_late_start", [(plan_join, [0], 1)], reduce_sum("late", late, small_token))
    big, after = {}, (last_token,)
    for nm, g, w, m, v in (("w_out", full_wo, w_out, m_w_out, v_w_out),
                           ("w_ple_gate", full_wg, w_ple_gate, m_w_ple_gate, v_w_ple_gate),
                           ("w_ple_proj", full_wp, w_ple_proj, m_w_ple_proj, v_w_ple_proj)):
        stepped = adamw(g, w[0], m[0], v[0], "adamw_" + nm, after=after)
        big[nm], after = [o[None] for o in stepped], (stepped[1],)
    full_wi, = exchange_wait("join_late_wait", plan_join, last_sems[0], last_halves, after[0])
    big["w_in"] = [o[None] for o in adamw(full_wi, w_in[0], m_w_in[0], v_w_in[0], "adamw_w_in")]

    gathered, = exchange_wait("small_wait", plan_allgather_small, small_sems[0], small_bufs, big["w_in"][1])
    tot = sum_devices(gathered)
    loss = tot[SUBLANES, 0]

    def pack(vals):
        ng, pg, bg, eg, qgv, kgv, sk, cv = vals
        misc_v = jnp.concatenate([qgv, kgv, sk], axis=1)
        return jnp.concatenate([_pad_cols(ng, wsm), _pad_cols(pg, wsm), _pad_cols(bg, wsm), _pad_cols(eg, wsm),
                                _pad_cols(misc_v, wsm), _pad_cols(cv[0], wsm)], axis=0)

    g_small = jnp.concatenate(
        [tot[0:5], _pad_cols(lax.dynamic_slice(tot[5:8], (0, chip * cq), (3, cq)), wsm)], axis=0)
    w_small = pack((norm_gain, ple_gate_norm_gain, b_ple_gate, ple_norm_gain, q_norm_gain, k_norm_gain,
                    attn_sinks, conv_w))
    m_small = pack((m_norm_gain, m_ple_gate_norm_gain, m_b_ple_gate, m_ple_norm_gain, m_q_norm_gain,
                    m_k_norm_gain, m_attn_sinks, m_conv_w))
    v_small = pack((v_norm_gain, v_ple_gate_norm_gain, v_b_ple_gate, v_ple_norm_gain, v_q_norm_gain,
                    v_k_norm_gain, v_attn_sinks, v_conv_w))
    sm = adamw(g_small, w_small, m_small, v_small, "adamw_small")

    def unpack(a):
        return {"norm_gain": a[0:1, :d], "ple_gate_norm_gain": a[1:2, :d], "b_ple_gate": a[2:3, :d],
                "ple_norm_gain": a[3:4, :d], "q_norm_gain": a[4:5, :HEAD_DIM],
                "k_norm_gain": a[4:5, HEAD_DIM:2 * HEAD_DIM],
                "attn_sinks": a[4:5, 2 * HEAD_DIM:2 * HEAD_DIM + nq], "conv_w": a[5:8, :cq][None]}

    order = ("norm_gain", "w_in", "q_norm_gain", "k_norm_gain", "attn_sinks", "conv_w", "w_out",
             "ple_gate_norm_gain", "w_ple_gate", "b_ple_gate", "w_ple_proj", "ple_norm_gain")
    outs = [loss, grad_x[None]]
    for kind in range(4):
        table = unpack(sm[kind])
        for nm in order:
            outs.append(big[nm][kind] if nm in big else table[nm])
    return tuple(outs)
```

```python
import functools

import jax
import jax.numpy as jnp
from jax import lax
from jax.experimental import pallas as pl
from jax.experimental.pallas import tpu as pltpu

F32 = jnp.float32
BF16 = jnp.bfloat16
MESH = pl.DeviceIdType.MESH

HEAD_DIM = 64
N_KV_HEADS = 4
KV_WIDTH = N_KV_HEADS * HEAD_DIM
BLOCK = 128
ROT_DIM = 16
ROPE_THETA = 500000.0
EPS = 1e-6
NEG_INF = -1e30
N_CHIPS = 4
LANES = 128
SUBLANES = 8
COLT = 512
SEG = 256
VMEM_LIMIT = 48 * 1024 * 1024
VMEM_LIMIT_BIG = 60 * 1024 * 1024

ADAM_LR = 0.001
ADAM_B1 = 0.9
ADAM_B2 = 0.999
ADAM_EPS = 1e-08
ADAM_WD = 0.01
ADAM_STEP = 10

NN = (((1,), (0,)), ((), ()))
NT = (((1,), (1,)), ((), ()))
TN = (((0,), (0,)), ((), ()))


def _call(body, **kw):
    return pl.pallas_call(body, **kw)


def _call_after(body, after, **kw):
    n_in, n_after = len(kw["in_specs"]), len(after)
    kw["in_specs"] = list(kw["in_specs"]) + [pl.BlockSpec(memory_space=pl.ANY)] * n_after

    def body_after(*refs):
        body(*refs[:n_in], *refs[n_in + n_after:])

    call = _call(body_after, **kw)
    return lambda *args: call(*args, *after)


def _params(sem, vmem=VMEM_LIMIT):
    return pltpu.CompilerParams(dimension_semantics=sem, vmem_limit_bytes=vmem)


def _tile(n, pref):
    return pref if n % pref == 0 else n


def _sigmoid(v):
    return 1.0 / (1.0 + jnp.exp(-v))


def _hbm():
    return pl.BlockSpec(memory_space=pl.ANY)


def cast_bf16(a, name):
    r, c = a.shape
    tr = _tile(r, 512)

    def body(a_ref, o_ref):
        o_ref[...] = a_ref[...].astype(BF16)

    return _call(body, name=name, grid=(r // tr,),
                 in_specs=[pl.BlockSpec((tr, c), lambda i: (i, 0))],
                 out_specs=pl.BlockSpec((tr, c), lambda i: (i, 0)),
                 out_shape=jax.ShapeDtypeStruct((r, c), BF16),
                 compiler_params=_params(("parallel",)))(a)


def cast_into_slot(a, chip, name):
    r, c = a.shape
    tr = _tile(r, 512)

    def body(chip_ref, a_ref, o_ref):
        o_ref[...] = a_ref[...].astype(BF16)

    grid_spec = pltpu.PrefetchScalarGridSpec(
        num_scalar_prefetch=1, grid=(r // tr,),
        in_specs=[pl.BlockSpec((tr, c), lambda i, chip_ref: (i, 0))],
        out_specs=pl.BlockSpec((None, tr, c), lambda i, chip_ref: (chip_ref[0], i, 0)))
    return _call(body, name=name, grid_spec=grid_spec,
                 out_shape=jax.ShapeDtypeStruct((N_CHIPS, r, c), BF16),
                 compiler_params=_params(("parallel",)))(chip, a)


def rms_fwd(x, gain, name, after=()):
    s, d = x.shape
    tm = _tile(s, 512)

    def body(x_ref, g_ref, o_ref):
        xf = x_ref[...]
        r = lax.rsqrt(jnp.mean(xf * xf, axis=-1, keepdims=True) + EPS)
        o_ref[...] = ((xf * r) * g_ref[...]).astype(BF16)

    return _call_after(body, after, name=name, grid=(s // tm,),
                       in_specs=[pl.BlockSpec((tm, d), lambda i: (i, 0)),
                                 pl.BlockSpec((1, d), lambda i: (0, 0))],
                       out_specs=pl.BlockSpec((tm, d), lambda i: (i, 0)),
                       out_shape=jax.ShapeDtypeStruct((s, d), BF16),
                       compiler_params=_params(("parallel",)))(x, gain)


def rms_bwd_add(dyn, xin, add, gain, name, want_bf16):
    s, d = xin.shape
    tm = _tile(s, 512)

    def body(dy_ref, x_ref, a_ref, g_ref, *outs):
        i = pl.program_id(0)
        dx_ref, acc_ref = outs[0], outs[-1]
        xf = x_ref[...]
        r = lax.rsqrt(jnp.mean(xf * xf, axis=-1, keepdims=True) + EPS)
        xhat = xf * r
        dyv = dy_ref[...]
        gd = dyv * g_ref[...]
        dx = a_ref[...] + r * (gd - xhat * jnp.mean(xhat * gd, axis=-1, keepdims=True))
        dx_ref[...] = dx
        if want_bf16:
            outs[1][...] = dx.astype(BF16)

        @pl.when(i == 0)
        def _():
            acc_ref[...] = jnp.zeros_like(acc_ref)

        acc_ref[0:1, :] += jnp.sum(dyv * xhat, axis=0, keepdims=True)

    row = pl.BlockSpec((tm, d), lambda i: (i, 0))
    out_specs = [row] + ([row] if want_bf16 else []) + [pl.BlockSpec((SUBLANES, d), lambda i: (0, 0))]
    out_shape = ([jax.ShapeDtypeStruct((s, d), F32)]
                 + ([jax.ShapeDtypeStruct((s, d), BF16)] if want_bf16 else [])
                 + [jax.ShapeDtypeStruct((SUBLANES, d), F32)])
    return _call(body, name=name, grid=(s // tm,),
                 in_specs=[row, row, row, pl.BlockSpec((1, d), lambda i: (0, 0))],
                 out_specs=out_specs, out_shape=out_shape,
                 compiler_params=_params(("arbitrary",), VMEM_LIMIT_BIG))(dyn, xin, add, gain)


def head_fwd_bwd(x1, gl, p, wp, tgt, bias, ple_gain):
    s, d = x1.shape
    tm = _tile(s, 256)

    def body(x1_ref, gl_ref, p_ref, wp_ref, t_ref, b_ref, g_ref, dgl_ref, dpe_ref, dy_ref, acc_ref):
        i = pl.program_id(0)
        gate = _sigmoid(gl_ref[...] + b_ref[...])
        pb = p_ref[...].astype(BF16)
        pev = jnp.concatenate([jnp.dot(pb, wp_ref[q], preferred_element_type=F32) for q in range(N_CHIPS)],
                              axis=1)
        r = lax.rsqrt(jnp.mean(pev * pev, axis=-1, keepdims=True) + EPS)
        pehat = pev * r
        gain = g_ref[...]
        e = pehat * gain
        diff = (x1_ref[...] + gate * e) - t_ref[...]
        dy = diff * (1.0 / d)
        dy_ref[...] = dy
        d_gl = (dy * e) * (gate * (1.0 - gate))
        dgl_ref[...] = d_gl.astype(BF16)
        d_e = dy * gate
        gd = d_e * gain
        d_pe = r * (gd - pehat * jnp.mean(pehat * gd, axis=-1, keepdims=True))
        dpe_ref[...] = d_pe.astype(BF16)

        @pl.when(i == 0)
        def _():
            acc_ref[...] = jnp.zeros_like(acc_ref)

        acc_ref[0:1, :] += jnp.sum(d_gl, axis=0, keepdims=True)
        acc_ref[1:2, :] += jnp.sum(d_e * pehat, axis=0, keepdims=True)
        acc_ref[2:3, :] += jnp.sum(diff * diff, axis=0, keepdims=True) * (0.5 / d)

    row = pl.BlockSpec((tm, d), lambda i: (i, 0))
    vec = pl.BlockSpec((1, d), lambda i: (0, 0))
    return _call(body, name="head_fwd_bwd", grid=(s // tm,),
                 in_specs=[row, row, pl.BlockSpec((tm, p.shape[1]), lambda i: (i, 0)),
                           pl.BlockSpec(wp.shape, lambda i: (0, 0, 0)), row, vec, vec],
                 out_specs=[row, row, row, pl.BlockSpec((SUBLANES, d), lambda i: (0, 0))],
                 out_shape=[jax.ShapeDtypeStruct((s, d), BF16), jax.ShapeDtypeStruct((s, d), BF16),
                            jax.ShapeDtypeStruct((s, d), F32), jax.ShapeDtypeStruct((SUBLANES, d), F32)],
                 compiler_params=_params(("arbitrary",)))(x1, gl, p, wp, tgt, bias, ple_gain)


def adamw(g, w, m, v, name, after=()):
    r, c = g.shape
    tr = _tile(r, 256)

    def body(g_ref, w_ref, m_ref, v_ref, go_ref, d_ref, mo_ref, vo_ref):
        gv = g_ref[...]
        mn = ADAM_B1 * m_ref[...] + (1.0 - ADAM_B1) * gv
        vn = ADAM_B2 * v_ref[...] + (1.0 - ADAM_B2) * (gv * gv)
        m_hat = mn / (1.0 - ADAM_B1 ** ADAM_STEP)
        v_hat = vn / (1.0 - ADAM_B2 ** ADAM_STEP)
        go_ref[...] = gv
        d_ref[...] = -ADAM_LR * (m_hat / (jnp.sqrt(v_hat) + ADAM_EPS) + ADAM_WD * w_ref[...])
        mo_ref[...] = mn
        vo_ref[...] = vn

    blk = pl.BlockSpec((tr, c), lambda i: (i, 0))
    shp = jax.ShapeDtypeStruct((r, c), F32)
    return _call_after(body, after, name=name, grid=(r // tr,), in_specs=[blk] * 4, out_specs=[blk] * 4,
                       out_shape=[shp] * 4, compiler_params=_params(("parallel",)))(g, w, m, v)


def matmul(a, b, *, grid, a_spec, b_spec, o_spec, out_shape, dims, name, res=None, res_spec=None, after=(),
           vmem=VMEM_LIMIT):
    nk = grid[2]
    acc_shape = tuple(d for d in o_spec.block_shape if d is not None)

    def body(*refs):
        a_ref, b_ref = refs[:2]
        r_ref = refs[2] if res is not None else None
        o_ref = refs[3] if res is not None else refs[2]
        part = lax.dot_general(a_ref[...].astype(BF16), b_ref[...].astype(BF16), dims, preferred_element_type=F32)

        def finish(out):
            if res is not None:
                out = r_ref[...] + out
            o_ref[...] = out.astype(o_ref.dtype)

        if nk == 1:
            finish(part)
            return
        acc_ref = refs[-1]
        k = pl.program_id(2)

        @pl.when(k == 0)
        def _():
            acc_ref[...] = part

        @pl.when((k > 0) & (k < nk - 1))
        def _():
            acc_ref[...] += part

        @pl.when(k == nk - 1)
        def _():
            finish(acc_ref[...] + part)

    in_specs = [a_spec, b_spec] + ([res_spec] if res is not None else [])
    args = (a, b) + ((res,) if res is not None else ())
    return _call_after(body, after, name=name, grid=grid, in_specs=in_specs, out_specs=o_spec,
                       out_shape=out_shape, scratch_shapes=[pltpu.VMEM(acc_shape, F32)] if nk > 1 else [],
                       compiler_params=_params(("parallel", "parallel", "arbitrary"), vmem))(*args)


def in_proj_part(h, ws, z_prev, order, q, in_w):
    s, d = h.shape
    sh = ws[0].shape[1]
    tm = _tile(s, 512)
    nw = len(ws)

    def body(order_ref, h_ref, *rest):
        for j in range(nw):
            @pl.when(pl.program_id(0) == j)
            def _():
                rest[-1][...] = jnp.dot(h_ref[...], rest[j][...], preferred_element_type=F32)

    in_specs = [pl.BlockSpec((tm, d), lambda j, i, order_ref: (i, 0))]
    in_specs += [pl.BlockSpec((d, sh), lambda j, i, order_ref: (0, 0))] * nw
    args = [order, h, *ws]
    if z_prev is not None:
        in_specs.append(_hbm())
        args.append(z_prev)
    grid_spec = pltpu.PrefetchScalarGridSpec(
        num_scalar_prefetch=1, grid=(nw, s // tm), in_specs=in_specs,
        out_specs=pl.BlockSpec((tm, sh), lambda j, i, order_ref: (i, order_ref[q + j])))
    return _call(body, name="mm_z%d" % q, grid_spec=grid_spec,
                 out_shape=jax.ShapeDtypeStruct((s, in_w), F32),
                 input_output_aliases={2 + nw: 0} if z_prev is not None else {},
                 compiler_params=_params(("arbitrary", "parallel")))(*args)


def in_proj_bwd(dz, ws, order, d, after):
    s = dz.shape[0]
    sh = ws[0].shape[1]
    tm, tn = _tile(s, 512), _tile(d, 1024)
    nq, n_after = len(ws), len(after)

    def body(order_ref, *refs):
        a_refs, b_refs, o_ref = refs[:nq], refs[nq:2 * nq], refs[2 * nq + n_after]
        acc = lax.dot_general(a_refs[0][...], b_refs[0][...], NT, preferred_element_type=F32)
        for q in range(1, nq):
            acc += lax.dot_general(a_refs[q][...], b_refs[q][...], NT, preferred_element_type=F32)
        o_ref[...] = acc

    a_spec = lambda q: pl.BlockSpec((tm, sh), functools.partial(lambda i, j, order_ref, q: (i, order_ref[q]), q=q))
    grid_spec = pltpu.PrefetchScalarGridSpec(
        num_scalar_prefetch=1, grid=(s // tm, d // tn),
        in_specs=[a_spec(q) for q in range(nq)]
        + [pl.BlockSpec((tn, sh), lambda i, j, order_ref: (j, 0))] * nq + [_hbm()] * n_after,
        out_specs=pl.BlockSpec((tm, tn), lambda i, j, order_ref: (i, j)))
    return _call(body, name="mm_d_h", grid_spec=grid_spec, out_shape=jax.ShapeDtypeStruct((s, d), F32),
                 compiler_params=_params(("parallel", "parallel"), VMEM_LIMIT_BIG))(
                     order, *([dz] * nq), *ws, *after)


def _seg_mean(sq, segb):
    parts = []
    for cgrp in range(sq.shape[1] // SEG):
        blk = sq[:, cgrp * SEG:(cgrp + 1) * SEG]
        hi = blk.astype(BF16)
        r1 = blk - hi.astype(F32)
        mid = r1.astype(BF16)
        lo = (r1 - mid.astype(F32)).astype(BF16)
        acc = jnp.dot(hi, segb, preferred_element_type=F32)
        acc += jnp.dot(mid, segb, preferred_element_type=F32)
        acc += jnp.dot(lo, segb, preferred_element_type=F32)
        parts.append(acc)
    out = parts[0] if len(parts) == 1 else jnp.concatenate(parts, axis=1)
    return out * (1.0 / HEAD_DIM)


def _rope(v, cos, sa, sb):
    parts = []
    for cgrp in range(v.shape[1] // LANES):
        blk = v[:, cgrp * LANES:(cgrp + 1) * LANES]
        parts.append(blk * cos + pltpu.roll(blk, LANES - 8, 1) * sa + pltpu.roll(blk, 8, 1) * sb)
    return parts[0] if len(parts) == 1 else jnp.concatenate(parts, axis=1)


def _rope_t(dv, cos, sa, sb):
    parts = []
    for cgrp in range(dv.shape[1] // LANES):
        blk = dv[:, cgrp * LANES:(cgrp + 1) * LANES]
        parts.append(blk * cos + pltpu.roll(blk * sa, 8, 1) + pltpu.roll(blk * sb, LANES - 8, 1))
    return parts[0] if len(parts) == 1 else jnp.concatenate(parts, axis=1)


def _tile_lanes(vec, width):
    reps = width // LANES
    return vec if reps == 1 else jnp.tile(vec, (1, reps))


def qk_prep_fwd(z, tabs, qg, kg, segb, aw, after=()):
    s = z.shape[0]
    tm = _tile(s, 512)
    wq = aw + 2 * KV_WIDTH

    def body(z_ref, cos_ref, sa_ref, sb_ref, qg_ref, kg_ref, seg_ref, qs_ref, ks_ref, vb_ref):
        zz = z_ref[...]
        q, k, v = zz[:, :aw], zz[:, aw:aw + KV_WIDTH], zz[:, aw + KV_WIDTH:]
        cos, sa, sb, segm = cos_ref[...], sa_ref[...], sb_ref[...], seg_ref[...]
        rq = lax.rsqrt(_seg_mean(q * q, segm) + EPS)
        qn = (q * rq) * _tile_lanes(qg_ref[...], aw)
        qs_ref[...] = (_rope(qn, cos, sa, sb) * (HEAD_DIM ** -0.5)).astype(BF16)
        rk = lax.rsqrt(_seg_mean(k * k, segm) + EPS)
        kn = (k * rk) * _tile_lanes(kg_ref[...], KV_WIDTH)
        ks_ref[...] = _rope(kn, cos, sa, sb).astype(BF16)
        vb_ref[...] = v.astype(BF16)

    tab = pl.BlockSpec((tm, LANES), lambda i: (i, 0))
    vec = pl.BlockSpec((1, LANES), lambda i: (0, 0))
    return _call_after(body, after, name="qk_prep_fwd", grid=(s // tm,),
                       in_specs=[pl.BlockSpec((tm, wq), lambda i: (i, 0)), tab, tab, tab, vec, vec,
                                 pl.BlockSpec((SEG, SEG), lambda i: (0, 0))],
                       out_specs=[pl.BlockSpec((tm, aw), lambda i: (i, 0)),
                                  pl.BlockSpec((tm, KV_WIDTH), lambda i: (i, 0)),
                                  pl.BlockSpec((tm, KV_WIDTH), lambda i: (i, 0))],
                       out_shape=[jax.ShapeDtypeStruct((s, aw), BF16), jax.ShapeDtypeStruct((s, KV_WIDTH), BF16),
                                  jax.ShapeDtypeStruct((s, KV_WIDTH), BF16)],
                       compiler_params=_params(("parallel",)))(z, *tabs, qg, kg, segb)


def qk_prep_bwd(z, dqs, dks, dvs, tabs, qg, kg, segb, dz, aw):
    s = z.shape[0]
    tm = _tile(s, 512)
    wq = aw + 2 * KV_WIDTH

    def body(z_ref, dq_ref, dk_ref, dv_ref, cos_ref, sa_ref, sb_ref, qg_ref, kg_ref, seg_ref, dz_in,
             dz_ref, acc_ref):
        i = pl.program_id(0)
        zz = z_ref[...]
        q, k = zz[:, :aw], zz[:, aw:aw + KV_WIDTH]
        cos, sa, sb, segm = cos_ref[...], sa_ref[...], sb_ref[...], seg_ref[...]

        def one(xv, dout, gvec, width):
            g = _tile_lanes(gvec, width)
            r = lax.rsqrt(_seg_mean(xv * xv, segm) + EPS)
            xhat = xv * r
            dn = _rope_t(dout, cos, sa, sb)
            gd = dn * g
            dx = r * (gd - xhat * _seg_mean(xhat * gd, segm))
            contrib = jnp.sum(dn * xhat, axis=0, keepdims=True)
            folded = contrib[:, :LANES]
            for cgrp in range(1, width // LANES):
                folded = folded + contrib[:, cgrp * LANES:(cgrp + 1) * LANES]
            return dx, folded + pltpu.roll(folded, HEAD_DIM, 1)

        dq, gq = one(q, dq_ref[...] * (HEAD_DIM ** -0.5), qg_ref[...], aw)
        dk, gk = one(k, dk_ref[...], kg_ref[...], KV_WIDTH)
        dz_ref[:, :aw] = dq.astype(BF16)
        dz_ref[:, aw:aw + KV_WIDTH] = dk.astype(BF16)
        dz_ref[:, aw + KV_WIDTH:] = dv_ref[...].astype(BF16)

        @pl.when(i == 0)
        def _():
            acc_ref[...] = jnp.zeros_like(acc_ref)

        acc_ref[0:1, :] += gq
        acc_ref[1:2, :] += gk

    tab = pl.BlockSpec((tm, LANES), lambda i: (i, 0))
    vec = pl.BlockSpec((1, LANES), lambda i: (0, 0))
    kvb = pl.BlockSpec((tm, KV_WIDTH), lambda i: (i, 0))
    return _call(body, name="qk_prep_bwd", grid=(s // tm,),
                 in_specs=[pl.BlockSpec((tm, wq), lambda i: (i, 0)),
                           pl.BlockSpec((tm, aw), lambda i: (i, 0)), kvb, kvb, tab, tab, tab, vec, vec,
                           pl.BlockSpec((SEG, SEG), lambda i: (0, 0)), _hbm()],
                 out_specs=[pl.BlockSpec((tm, wq), lambda i: (i, 0)),
                            pl.BlockSpec((SUBLANES, LANES), lambda i: (0, 0))],
                 out_shape=[jax.ShapeDtypeStruct(dz.shape, BF16), jax.ShapeDtypeStruct((SUBLANES, LANES), F32)],
                 input_output_aliases={10: 0},
                 compiler_params=_params(("arbitrary",)))(z, dqs, dks, dvs, *tabs, qg, kg, segb, dz)


def _placed(band, lane_idx):
    out = {}
    for kh in range(N_KV_HEADS):
        grp = band[:, (kh // 2) * LANES:(kh // 2 + 1) * LANES]
        for half in (0, 1):
            t = grp if half == kh % 2 else pltpu.roll(grp, HEAD_DIM, 1)
            keep = (lane_idx >= half * HEAD_DIM) & (lane_idx < (half + 1) * HEAD_DIM)
            out[kh, half] = jnp.where(keep, t, jnp.zeros_like(t))
    return out


def _softmax_with_sink(sc, valid, sink):
    sc = jnp.where(valid, sc, NEG_INF)
    m = jnp.maximum(jnp.max(sc, axis=-1, keepdims=True), sink)
    e = jnp.exp(sc - m)
    es = jnp.exp(sink - m)
    den = jnp.sum(e, axis=-1, keepdims=True) + es
    return e / den, es / den


def _stack_halves(placed, kh):
    return jnp.concatenate([placed[kh, 0], placed[kh, 1]], axis=0)


def _valid_mask(n):
    r_i = lax.broadcasted_iota(jnp.int32, (BLOCK, 2 * BLOCK), 0)
    j_i = lax.broadcasted_iota(jnp.int32, (BLOCK, 2 * BLOCK), 1)
    return (j_i > r_i) & (j_i <= r_i + BLOCK) & ((n > 0) | (j_i >= BLOCK))


def attn_fwd(qs, ks, vb, z, sinks, aw, d, ga_off):
    s = qs.shape[0]
    nb = s // BLOCK
    nq = aw // HEAD_DIM
    grp_sz = nq // N_KV_HEADS
    n_ga = aw // COLT

    def body(q_ref, ko_ref, kp_ref, vo_ref, vp_ref, *rest):
        ga_refs = rest[:n_ga]
        sink_ref, attn_ref, mix_ref = rest[n_ga:]
        n = pl.program_id(0)
        lane_idx = lax.broadcasted_iota(jnp.int32, (2 * BLOCK, LANES), 1)
        kpl = _placed(jnp.concatenate([kp_ref[...], ko_ref[...]], axis=0), lane_idx)
        vpl = _placed(jnp.concatenate([vp_ref[...], vo_ref[...]], axis=0), lane_idx)
        valid = _valid_mask(n)
        groups = range(nq // 2)
        kcat = [_stack_halves(kpl, kh) for kh in range(N_KV_HEADS)]
        vcat = [_stack_halves(vpl, kh) for kh in range(N_KV_HEADS)]
        scs = [lax.dot_general(q_ref[:, c * LANES:(c + 1) * LANES], kcat[2 * c // grp_sz], NT,
                               preferred_element_type=F32) for c in groups]
        probs = [jnp.concatenate(
            [_softmax_with_sink(scs[c][:, half * 2 * BLOCK:(half + 1) * 2 * BLOCK], valid,
                                sink_ref[0, 2 * c + half])[0].astype(BF16) for half in (0, 1)], axis=1)
                 for c in groups]
        for c in groups:
            acc = jnp.dot(probs[c], vcat[2 * c // grp_sz], preferred_element_type=F32)
            attn_ref[:, c * LANES:(c + 1) * LANES] = acc
            col = c * LANES
            ga = ga_refs[col // COLT][:, col % COLT:col % COLT + LANES]
            mix_ref[:, c * LANES:(c + 1) * LANES] = (acc * (ga * _sigmoid(ga))).astype(BF16)

    own = lambda n: (n, 0)
    prev = lambda n: (jnp.maximum(n - 1, 0), 0)
    kvs = lambda imap: pl.BlockSpec((BLOCK, KV_WIDTH), imap)
    ga_specs = [pl.BlockSpec((BLOCK, COLT), functools.partial(lambda n, j: (n, ga_off + j), j=j))
                for j in range(n_ga)]
    return _call(body, name="attn_fwd", grid=(nb,),
                 in_specs=[pl.BlockSpec((BLOCK, aw), own), kvs(own), kvs(prev), kvs(own), kvs(prev)]
                 + ga_specs + [pl.BlockSpec(memory_space=pltpu.SMEM)],
                 out_specs=[pl.BlockSpec((BLOCK, aw), own), pl.BlockSpec((BLOCK, aw), own)],
                 out_shape=[jax.ShapeDtypeStruct((s, aw), F32), jax.ShapeDtypeStruct((s, d), BF16)],
                 compiler_params=_params(("parallel",)))(qs, ks, ks, vb, vb, *([z] * n_ga), sinks)


def gate_bwd(d_mix, attn, z, aw, ga_off, after=()):
    s, in_w = z.shape
    tm = _tile(s, 1024)

    def body(dm_ref, at_ref, ga_ref, do_ref, dz_ref):
        ga = ga_ref[...]
        sig = _sigmoid(ga)
        dm = dm_ref[...]
        do_ref[...] = (dm * (ga * sig)).astype(BF16)
        dz_ref[...] = ((dm * at_ref[...]) * (sig * (1.0 + ga * (1.0 - sig)))).astype(BF16)

    blk = pl.BlockSpec((tm, COLT), lambda i, j: (i, j))
    gab = pl.BlockSpec((tm, COLT), lambda i, j: (i, ga_off + j))
    return _call_after(body, after, name="gate_bwd", grid=(s // tm, aw // COLT),
                       in_specs=[blk, blk, gab], out_specs=[blk, gab],
                       out_shape=[jax.ShapeDtypeStruct((s, aw), BF16), jax.ShapeDtypeStruct((s, in_w), BF16)],
                       compiler_params=_params(("parallel", "parallel")))(d_mix, attn, z)


def attn_bwd(qs, ks, vb, d_o, sinks, aw):
    s = qs.shape[0]
    nb = s // BLOCK
    nq = aw // HEAD_DIM
    grp_sz = nq // N_KV_HEADS

    def body(q_ref, ko_ref, kp_ref, vo_ref, vp_ref, do_ref, sink_ref, dq_ref, dk_ref, dv_ref, ds_ref,
             ck_ref, cv_ref):
        n = pl.program_id(0)

        @pl.when(n == 0)
        def _():
            ds_ref[...] = jnp.zeros_like(ds_ref)
            dk_ref[...] = jnp.zeros_like(dk_ref)
            dv_ref[...] = jnp.zeros_like(dv_ref)

        @pl.when(n < nb)
        def _():
            lane_idx = lax.broadcasted_iota(jnp.int32, (2 * BLOCK, LANES), 1)
            lane_row = lax.broadcasted_iota(jnp.int32, (1, LANES), 1)
            kpl = _placed(jnp.concatenate([kp_ref[...], ko_ref[...]], axis=0), lane_idx)
            vpl = _placed(jnp.concatenate([vp_ref[...], vo_ref[...]], axis=0), lane_idx)
            valid = _valid_mask(n)
            ds_row = jnp.zeros((1, LANES), F32)
            wide = 2 * BLOCK
            groups = range(nq // 2)
            per_kv = grp_sz // 2
            kcat = [_stack_halves(kpl, kh) for kh in range(N_KV_HEADS)]
            vcat = [_stack_halves(vpl, kh) for kh in range(N_KV_HEADS)]
            qg = [q_ref[:, c * LANES:(c + 1) * LANES] for c in groups]
            dog = [do_ref[:, c * LANES:(c + 1) * LANES] for c in groups]
            scs = [lax.dot_general(qg[c], kcat[c // per_kv], NT, preferred_element_type=F32) for c in groups]
            dps = [lax.dot_general(dog[c], vcat[c // per_kv], NT, preferred_element_type=F32) for c in groups]
            ds_pairs, p_pairs = [], []
            for c in groups:
                probs, dss = [], []
                for half in (0, 1):
                    h = 2 * c + half
                    cols = slice(half * wide, (half + 1) * wide)
                    p, p_sink = _softmax_with_sink(scs[c][:, cols], valid, sink_ref[0, h])
                    delta = jnp.sum(p * dps[c][:, cols], axis=-1, keepdims=True)
                    dss.append((p * (dps[c][:, cols] - delta)).astype(BF16))
                    probs.append(p.astype(BF16))
                    dsink = -jnp.sum(p_sink * delta, axis=0, keepdims=True)
                    ds_row += jnp.where(lane_row == h, dsink, 0.0)
                ds_pairs.append(jnp.concatenate(dss, axis=1))
                p_pairs.append(jnp.concatenate(probs, axis=1))
            for c in groups:
                dq_ref[:, c * LANES:(c + 1) * LANES] = jnp.dot(ds_pairs[c], kcat[c // per_kv],
                                                               preferred_element_type=F32)
            dkts = [lax.dot_general(qg[c], ds_pairs[c], TN, preferred_element_type=F32) for c in groups]
            dvts = [lax.dot_general(dog[c], p_pairs[c], TN, preferred_element_type=F32) for c in groups]
            dkt, dvt = [], []
            for kh in range(N_KV_HEADS):
                for parts, out in ((dkts, dkt), (dvts, dvt)):
                    tot = parts[kh * per_kv]
                    for c in range(kh * per_kv + 1, (kh + 1) * per_kv):
                        tot = tot + parts[c]
                    out.append(tot[:HEAD_DIM, :wide] + tot[HEAD_DIM:, wide:])
            ds_ref[0:1, :] += ds_row
            back = lambda t: jnp.concatenate(
                [jnp.concatenate(t[2 * g:2 * g + 2], axis=0).T for g in range(N_KV_HEADS // 2)], axis=1)
            dk_band, dv_band = back(dkt), back(dvt)

            @pl.when(n > 0)
            def _():
                dk_ref[...] = ck_ref[...] + dk_band[:BLOCK]
                dv_ref[...] = cv_ref[...] + dv_band[:BLOCK]

            ck_ref[...] = dk_band[BLOCK:]
            cv_ref[...] = dv_band[BLOCK:]

        @pl.when(n == nb)
        def _():
            dk_ref[...] = ck_ref[...]
            dv_ref[...] = cv_ref[...]

    own = lambda n: (jnp.minimum(n, nb - 1), 0)
    prev = lambda n: (jnp.clip(n - 1, 0, nb - 1), 0)
    done = lambda n: (jnp.maximum(n - 1, 0), 0)
    kvs = lambda imap: pl.BlockSpec((BLOCK, KV_WIDTH), imap)
    return _call(body, name="attn_bwd", grid=(nb + 1,),
                 in_specs=[pl.BlockSpec((BLOCK, aw), own), kvs(own), kvs(prev), kvs(own), kvs(prev),
                           pl.BlockSpec((BLOCK, aw), own), pl.BlockSpec(memory_space=pltpu.SMEM)],
                 out_specs=[pl.BlockSpec((BLOCK, aw), own), kvs(done), kvs(done),
                            pl.BlockSpec((SUBLANES, LANES), lambda n: (0, 0))],
                 out_shape=[jax.ShapeDtypeStruct((s, aw), F32), jax.ShapeDtypeStruct((s, KV_WIDTH), F32),
                            jax.ShapeDtypeStruct((s, KV_WIDTH), F32), jax.ShapeDtypeStruct((SUBLANES, LANES), F32)],
                 scratch_shapes=[pltpu.VMEM((BLOCK, KV_WIDTH), F32), pltpu.VMEM((BLOCK, KV_WIDTH), F32)],
                 compiler_params=_params(("arbitrary",)))(qs, ks, ks, vb, vb, d_o, sinks)


def conv_fwd(z, conv_w, mix, aw, cw, b_off):
    s = z.shape[0]
    tm = _tile(s, 1024)
    nseg = cw // COLT
    hb = tm // SUBLANES

    def body(b_ref, c_ref, h_ref, g_ref, cp_ref, hp_ref, w_ref, mix_in, mix_ref):
        i = pl.program_id(0)
        u = c_ref[...] * h_ref[...]
        up = jnp.where(i > 0, cp_ref[...] * hp_ref[...], 0.0)
        ext = jnp.concatenate([up, u], axis=0)
        um1 = pltpu.roll(ext, 1, 0)[SUBLANES:]
        um2 = pltpu.roll(ext, 2, 0)[SUBLANES:]
        w = w_ref[...]
        cv = w[0:1] * um2 + w[1:2] * um1 + w[2:3] * u
        g = g_ref[...]
        mix_ref[...] = ((b_ref[...] * cv) * (g * _sigmoid(g))).astype(BF16)

    seg = lambda k: pl.BlockSpec((tm, COLT), functools.partial(lambda i, j, k: (i, b_off + k * nseg + j), k=k))
    halo = lambda k: pl.BlockSpec(
        (SUBLANES, COLT), functools.partial(lambda i, j, k: (jnp.maximum(i * hb - 1, 0), b_off + k * nseg + j), k=k))
    return _call(body, name="conv_fwd", grid=(s // tm, nseg),
                 in_specs=[seg(0), seg(1), seg(2), seg(3), halo(1), halo(2),
                           pl.BlockSpec((SUBLANES, COLT), lambda i, j: (0, j)), _hbm()],
                 out_specs=pl.BlockSpec((tm, COLT), lambda i, j: (i, aw // COLT + j)),
                 out_shape=jax.ShapeDtypeStruct(mix.shape, BF16),
                 input_output_aliases={7: 0},
                 compiler_params=_params(("parallel", "parallel")))(z, z, z, z, z, z, conv_w, mix)


def conv_bwd(z, d_mix, conv_w, dz, aw, cw, b_off):
    s = z.shape[0]
    tm = _tile(s, 512)
    nseg = cw // COLT
    hb = tm // SUBLANES
    n_row = s // tm
    last_h = s // SUBLANES - 1
    n_step = nseg * n_row

    def body(b_ref, c_ref, h_ref, g_ref, cp_ref, hp_ref, bn_ref, gn_ref, dm_ref, dmn_ref, w_ref, dz_in,
             dz_ref, acc_ref, stash_ref, sems):
        j = pl.program_id(0)
        i = pl.program_id(1)
        step = j * n_row + i
        slot = step % 2

        def copies(from_slot, at_step):
            jj, ii = at_step // n_row, at_step % n_row
            rows = pl.ds(pl.multiple_of(ii * tm, tm), tm)
            return [pltpu.make_async_copy(
                stash_ref.at[from_slot, q],
                dz_ref.at[rows, pl.ds(pl.multiple_of((b_off + q * nseg + jj) * COLT, COLT), COLT)],
                sems.at[from_slot, q]) for q in range(4)]

        @pl.when(step >= 2)
        def _():
            for cp in copies(slot, step - 2):
                cp.wait()

        cc, hh, bb, g = c_ref[...], h_ref[...], b_ref[...], g_ref[...]
        w = w_ref[...]
        u = cc * hh
        up = jnp.where(i > 0, cp_ref[...] * hp_ref[...], 0.0)
        ext = jnp.concatenate([up, u], axis=0)
        um1 = pltpu.roll(ext, 1, 0)[SUBLANES:]
        um2 = pltpu.roll(ext, 2, 0)[SUBLANES:]
        cv = w[0:1] * um2 + w[1:2] * um1 + w[2:3] * u
        sig = _sigmoid(g)
        sg = g * sig
        dm = dm_ref[...]
        d_cv = (dm * bb) * sg
        gn = gn_ref[...]
        d_cv_next = jnp.where(i < n_row - 1, (dmn_ref[...] * bn_ref[...]) * (gn * _sigmoid(gn)), 0.0)
        ext2 = jnp.concatenate([d_cv, d_cv_next], axis=0)
        dp1 = pltpu.roll(ext2, tm + SUBLANES - 1, 0)[:tm]
        dp2 = pltpu.roll(ext2, tm + SUBLANES - 2, 0)[:tm]
        d_u = w[2:3] * d_cv + w[1:2] * dp1 + w[0:1] * dp2
        stash_ref[slot, 0] = ((dm * cv) * sg).astype(BF16)
        stash_ref[slot, 1] = (d_u * hh).astype(BF16)
        stash_ref[slot, 2] = (d_u * cc).astype(BF16)
        stash_ref[slot, 3] = (((dm * bb) * cv) * (sig * (1.0 + g * (1.0 - sig)))).astype(BF16)
        for cp in copies(slot, step):
            cp.start()

        @pl.when(i == 0)
        def _():
            acc_ref[...] = jnp.zeros_like(acc_ref)

        acc_ref[0:1, :] += jnp.sum(d_cv * um2, axis=0, keepdims=True)
        acc_ref[1:2, :] += jnp.sum(d_cv * um1, axis=0, keepdims=True)
        acc_ref[2:3, :] += jnp.sum(d_cv * u, axis=0, keepdims=True)

        @pl.when(step == n_step - 1)
        def _():
            if n_step >= 2:
                for cp in copies(1 - slot, step - 1):
                    cp.wait()
            for cp in copies(slot, step):
                cp.wait()

    seg = lambda q: pl.BlockSpec((tm, COLT), functools.partial(lambda j, i, q: (i, b_off + q * nseg + j), q=q))
    halo_p = lambda q: pl.BlockSpec(
        (SUBLANES, COLT),
        functools.partial(lambda j, i, q: (jnp.maximum(i * hb - 1, 0), b_off + q * nseg + j), q=q))
    halo_n = lambda q: pl.BlockSpec(
        (SUBLANES, COLT),
        functools.partial(lambda j, i, q: (jnp.minimum((i + 1) * hb, last_h), b_off + q * nseg + j), q=q))
    return _call(body, name="conv_bwd", grid=(nseg, n_row),
                 in_specs=[seg(0), seg(1), seg(2), seg(3), halo_p(1), halo_p(2), halo_n(0), halo_n(3),
                           pl.BlockSpec((tm, COLT), lambda j, i: (i, aw // COLT + j)),
                           pl.BlockSpec((SUBLANES, COLT),
                                        lambda j, i: (jnp.minimum((i + 1) * hb, last_h), aw // COLT + j)),
                           pl.BlockSpec((SUBLANES, COLT), lambda j, i: (0, j)), _hbm()],
                 out_specs=[_hbm(), pl.BlockSpec((SUBLANES, COLT), lambda j, i: (0, j))],
                 out_shape=[jax.ShapeDtypeStruct(dz.shape, BF16), jax.ShapeDtypeStruct((SUBLANES, cw), F32)],
                 input_output_aliases={11: 0},
                 scratch_shapes=[pltpu.VMEM((2, 4, tm, COLT), BF16), pltpu.SemaphoreType.DMA((2, 4))],
                 compiler_params=_params(("arbitrary", "arbitrary")))(
                     z, z, z, z, z, z, z, z, d_mix, d_mix, conv_w, dz)


def _place():
    x, y, c = lax.axis_index("x"), lax.axis_index("y"), lax.axis_index("c")
    chips = [(1 - x, y), (x, 1 - y), (1 - x, 1 - y)]
    return x, y, c, chips


def _remote(src, dst, send_sem, recv_sem, device):
    return pltpu.make_async_remote_copy(src_ref=src, dst_ref=dst, send_sem=send_sem, recv_sem=recv_sem,
                                        device_id=device, device_id_type=MESH)


def plan_gather_ici(n_split):
    def plan(refs, send, recv):
        x, y, c, chips = _place()
        me = 2 * x + y
        mine, theirs = [], []
        for w, ref in enumerate(refs):
            for j, (cx, cy) in enumerate(chips):
                def part(slot):
                    if w >= n_split:
                        return ref.at[slot]
                    hr = ref.shape[1] // 2
                    return ref.at[slot, pl.ds(c * hr, hr)]
                k = 3 * w + j
                mine.append(_remote(part(me), part(me), send.at[k], recv.at[k], (cx, cy, c)))
                theirs.append(_remote(part(2 * cx + cy), part(2 * cx + cy), send.at[k], recv.at[k], (cx, cy, c)))
        return mine, theirs
    return plan


def plan_shard_ici(j):
    def plan(refs, send, recv):
        _, _, c, chips = _place()
        own, land = refs
        hr = own.shape[0] // 2
        rows = pl.ds(c * hr, hr)
        cp = _remote(own.at[rows], land.at[rows], send.at[0], recv.at[0], (*chips[j], c))
        return [cp], [cp]
    return plan


def plan_shard_relay(refs, send, recv):
    _, _, c, chips = _place()
    land_x, land_y, land_far = refs
    hr = land_far.shape[0] // 2
    quarter = lambda q: pl.ds(c * hr + q * (hr // 2), hr // 2)
    mine = [_remote(land_y.at[quarter(0)], land_far.at[quarter(0)], send.at[0], recv.at[0], (*chips[0], c)),
            _remote(land_x.at[quarter(1)], land_far.at[quarter(1)], send.at[1], recv.at[1], (*chips[1], c))]
    return mine, mine


def plan_shard_pass(refs, send, recv):
    x, y, c, _ = _place()
    mine, theirs = [], []
    for w, land in enumerate(refs):
        hr = land.shape[0] // 2
        half = lambda core: land.at[pl.ds(core * hr, hr)]
        mine.append(_remote(half(c), half(c), send.at[w], recv.at[w], (x, y, 1 - c)))
        theirs.append(_remote(half(1 - c), half(1 - c), send.at[w], recv.at[w], (x, y, 1 - c)))
    return mine, theirs


def plan_gather_pass(refs, send, recv):
    x, y, c, chips = _place()
    mine, theirs = [], []
    for w, ref in enumerate(refs):
        hr = ref.shape[1] // 2
        for j, (cx, cy) in enumerate(chips):
            half = lambda core: ref.at[2 * cx + cy, pl.ds(core * hr, hr)]
            k = 3 * w + j
            mine.append(_remote(half(c), half(c), send.at[k], recv.at[k], (x, y, 1 - c)))
            theirs.append(_remote(half(1 - c), half(1 - c), send.at[k], recv.at[k], (x, y, 1 - c)))
    return mine, theirs


def plan_swap(refs, send, recv):
    x, y, c, _ = _place()
    nw = len(refs) // 2
    mine = []
    for w in range(nw):
        hr = refs[w].shape[1] // 2
        mine.append(_remote(refs[w].at[:, pl.ds((1 - c) * hr, hr), :], refs[nw + w], send.at[w], recv.at[w],
                            (x, y, 1 - c)))
    return mine, mine


def plan_scatter(refs, send, recv):
    x, y, c, chips = _place()
    me = 2 * x + y
    nw = len(refs) // 2
    mine, theirs = [], []
    for w in range(nw):
        for j, (cx, cy) in enumerate(chips):
            k = 3 * w + j
            mine.append(_remote(refs[w].at[2 * cx + cy], refs[nw + w].at[me], send.at[k], recv.at[k], (cx, cy, c)))
            theirs.append(_remote(refs[w].at[me], refs[nw + w].at[2 * cx + cy], send.at[k], recv.at[k], (cx, cy, c)))
    return mine, theirs


def plan_join(refs, send, recv):
    x, y, c, _ = _place()
    mine, theirs = [], []
    for w, ref in enumerate(refs):
        hr = ref.shape[0] // 2
        half = lambda core: ref.at[pl.ds(core * hr, hr)]
        mine.append(_remote(half(c), half(c), send.at[w], recv.at[w], (x, y, 1 - c)))
        theirs.append(_remote(half(1 - c), half(1 - c), send.at[w], recv.at[w], (x, y, 1 - c)))
    return mine, theirs


def exchange(name, plan, arrays, n, after=()):
    na = len(arrays)

    def body(*refs):
        mine, theirs = plan(refs[:na], refs[2 * na], refs[2 * na + 1])
        for cp in mine:
            cp.start()
        for cp in theirs:
            cp.wait_recv()
        for cp in mine:
            cp.wait_send()

    return _call_after(body, after, name=name, in_specs=[_hbm()] * na, out_specs=[_hbm()] * na,
                       out_shape=[jax.ShapeDtypeStruct(a.shape, a.dtype) for a in arrays],
                       input_output_aliases={i: i for i in range(na)},
                       scratch_shapes=[pltpu.SemaphoreType.DMA((n,)), pltpu.SemaphoreType.DMA((n,))])(*arrays)


def exchange_start(name, groups, arrays, after=()):
    na, ng = len(arrays), len(groups)

    def body(*refs):
        for g, (plan, idx, _) in enumerate(groups):
            mine, _ = plan([refs[i] for i in idx], refs[na + 2 * g], refs[na + 2 * g + 1])
            for cp in mine:
                cp.start()
        refs[-1][...] = jnp.zeros_like(refs[-1])

    hbm = pl.BlockSpec(memory_space=pltpu.HBM)
    sem = pl.BlockSpec(memory_space=pltpu.SEMAPHORE)
    sem_types = [pltpu.SemaphoreType.DMA((n,)) for _, _, n in groups for _ in (0, 1)]
    outs = _call_after(body, after, name=name, in_specs=[hbm] * na,
                       out_specs=[sem] * (2 * ng) + [hbm] * na + [pl.BlockSpec(memory_space=pltpu.VMEM)],
                       out_shape=sem_types + [pltpu.HBM(a.shape, a.dtype) for a in arrays]
                       + [jax.ShapeDtypeStruct((SUBLANES, LANES), F32)],
                       input_output_aliases={i: i + 2 * ng for i in range(na)},
                       compiler_params=pltpu.CompilerParams(
                           has_side_effects=pltpu.SideEffectType.DATAFLOW_SIDE_EFFECTING))(
                               *[pltpu.with_memory_space_constraint(a, pltpu.HBM) for a in arrays])
    sems = [(outs[2 * g], outs[2 * g + 1]) for g in range(ng)]
    return sems, list(outs[2 * ng:-1]), outs[-1]


def exchange_wait(name, plan, sems, arrays, after):
    send_sems, recv_sems = sems
    na = len(arrays)

    def body(*refs):
        mine, theirs = plan(refs[:na], refs[na], refs[na + 1])
        for cp in mine:
            cp.wait_send()
        for cp in theirs:
            cp.wait_recv()

    hbm = pl.BlockSpec(memory_space=pltpu.HBM)
    sem = pl.BlockSpec(memory_space=pltpu.SEMAPHORE)
    return _call(body, name=name, in_specs=[hbm] * na + [sem, sem, _hbm()], out_specs=[hbm] * na,
                 out_shape=[pltpu.HBM(a.shape, a.dtype) for a in arrays],
                 input_output_aliases={i: i for i in range(na)},
                 compiler_params=pltpu.CompilerParams(
                     has_side_effects=pltpu.SideEffectType.DATAFLOW_SIDE_EFFECTING))(
                         *arrays, send_sems, recv_sems, after)


def plan_allgather_small(refs, send, recv):
    x, y, c, _ = _place()
    buf, = refs
    mine, theirs = [], []
    flips = [(fx, fy, fc) for fx in (0, 1) for fy in (0, 1) for fc in (0, 1)][1:]
    for k, (fx, fy, fc) in enumerate(flips):
        peer = (x ^ fx, y ^ fy, c ^ fc)
        slot = lambda px, py, pc: buf.at[4 * px + 2 * py + pc]
        mine.append(_remote(slot(x, y, c), slot(x, y, c), send.at[k], recv.at[k], peer))
        theirs.append(_remote(slot(*peer), slot(*peer), send.at[k], recv.at[k], peer))
    return mine, theirs


def add_sibling(grad, got, core, name):
    _, r, c = grad.shape
    hr = r // 2
    tr = _tile(hr, 512)
    nblk = hr // tr

    def body(core_ref, g_ref, o_ref, out_ref):
        out_ref[...] = (g_ref[...].astype(F32) + o_ref[...].astype(F32)).astype(BF16)

    grid_spec = pltpu.PrefetchScalarGridSpec(
        num_scalar_prefetch=1, grid=(N_CHIPS, nblk),
        in_specs=[pl.BlockSpec((None, tr, c), lambda t, i, core_ref: (t, core_ref[0] * nblk + i, 0)),
                  pl.BlockSpec((None, tr, c), lambda t, i, core_ref: (t, i, 0))],
        out_specs=pl.BlockSpec((None, tr, c), lambda t, i, core_ref: (t, i, 0)))
    return _call(body, name=name, grid_spec=grid_spec,
                 out_shape=jax.ShapeDtypeStruct((N_CHIPS, hr, c), BF16),
                 compiler_params=_params(("parallel", "parallel")))(core, grad, got)


def sum_chips(mine, owned, place, name):
    _, hr, c = mine.shape
    tr = _tile(hr, 512)
    nblk = hr // tr

    def body(place_ref, m_ref, o1_ref, o2_ref, o3_ref, out_ref):
        acc = m_ref[...].astype(F32)
        for o_ref in (o1_ref, o2_ref, o3_ref):
            acc = acc + o_ref[...].astype(F32)
        out_ref[...] = acc

    other = lambda k: pl.BlockSpec((None, tr, c), lambda i, place_ref: ((place_ref[0] + k) % N_CHIPS, i, 0))
    grid_spec = pltpu.PrefetchScalarGridSpec(
        num_scalar_prefetch=1, grid=(nblk,),
        in_specs=[other(0), other(1), other(2), other(3)],
        out_specs=pl.BlockSpec((tr, c), lambda i, place_ref: (place_ref[1] * nblk + i, 0)))
    return _call(body, name=name, grid_spec=grid_spec,
                 out_shape=jax.ShapeDtypeStruct((2 * hr, c), F32),
                 compiler_params=_params(("parallel",)))(place, mine, owned, owned, owned)


def sum_devices(gathered):
    _, rows, width = gathered.shape

    def body(g_ref, out_ref):
        acc = g_ref[0]
        for dev in range(1, 8):
            acc = acc + g_ref[dev]
        out_ref[...] = acc
        tail = acc[SUBLANES:]
        out_ref[SUBLANES:, :] = jnp.broadcast_to(jnp.sum(tail, axis=1, keepdims=True), tail.shape)

    return _call(body, name="sum_devices",
                 in_specs=[pl.BlockSpec(memory_space=pltpu.VMEM)],
                 out_specs=pl.BlockSpec(memory_space=pltpu.VMEM),
                 out_shape=jax.ShapeDtypeStruct((rows, width), F32))(gathered)


def _rope_tables(s):
    half = ROT_DIM // 2
    inv_freq = jnp.power(jnp.float32(ROPE_THETA), -jnp.arange(half, dtype=F32) * 2.0 / ROT_DIM)
    freq64 = jnp.concatenate([inv_freq, inv_freq, jnp.zeros((HEAD_DIM - ROT_DIM,), F32)])
    freq = jnp.concatenate([freq64, freq64])[None, :]
    dim = (jnp.arange(LANES) % HEAD_DIM)[None, :]
    ang = jnp.arange(s).astype(F32)[:, None] * freq
    cos, sin = jnp.cos(ang), jnp.sin(ang)
    return cos, jnp.where(dim < half, -sin, 0.0), jnp.where((dim >= half) & (dim < ROT_DIM), sin, 0.0)


def _pad_rows(a, rows):
    return jnp.pad(a, ((0, rows - a.shape[0]), (0, 0)))


def _pad_cols(a, cols):
    return jnp.pad(a, ((0, 0), (0, cols - a.shape[1])))


def kernel(x, p, norm_gain, w_in, q_norm_gain, k_norm_gain, attn_sinks, conv_w, w_out, ple_gate_norm_gain, w_ple_gate, b_ple_gate, w_ple_proj, ple_norm_gain, loss_target, m_norm_gain, m_w_in, m_q_norm_gain, m_k_norm_gain, m_attn_sinks, m_conv_w, m_w_out, m_ple_gate_norm_gain, m_w_ple_gate, m_b_ple_gate, m_w_ple_proj, m_ple_norm_gain, v_norm_gain, v_w_in, v_q_norm_gain, v_k_norm_gain, v_attn_sinks, v_conv_w, v_w_out, v_ple_gate_norm_gain, v_w_ple_gate, v_b_ple_gate, v_w_ple_proj, v_ple_norm_gain):
    x2, p2, tgt = x[0], p[0, 0], loss_target[0]
    s, d = x2.shape
    ple = p2.shape[1]
    aw = d // 2
    cw = d - aw
    nq = aw // HEAD_DIM
    sh = w_in.shape[2]
    in_w = N_CHIPS * sh
    dq = d // N_CHIPS
    cq = cw // N_CHIPS
    ga_off = (aw + 2 * KV_WIDTH) // COLT
    b_off = (2 * aw + 2 * KV_WIDTH) // COLT
    assert in_w == 2 * aw + 2 * KV_WIDTH + 4 * cw and aw % COLT == 0 and cw % COLT == 0
    assert s % BLOCK == 0 and sh % LANES == 0 and nq % (2 * N_KV_HEADS) == 0

    core = lax.axis_index("c").astype(jnp.int32).reshape(1)
    chip = 2 * lax.axis_index("x") + lax.axis_index("y")

    chip1 = chip.astype(jnp.int32).reshape(1)
    place = jnp.concatenate([chip1, core])
    w_in_own = cast_bf16(w_in[0], "cast_w_in")
    rest = [cast_into_slot(w_out[0], chip1, "cast_w_out"), cast_into_slot(w_ple_gate[0], chip1, "cast_w_pg"),
            cast_into_slot(w_ple_proj[0], chip1, "cast_w_pp"),
            lax.dynamic_update_slice(jnp.zeros((N_CHIPS, SUBLANES, cq), F32),
                                     _pad_rows(conv_w[0], SUBLANES)[None], (chip, 0, 0))]
    order = jnp.stack([chip, chip ^ 2, chip ^ 1, chip ^ 3]).astype(jnp.int32)
    wi_sems, wi_arrs, wi_token = exchange_start(
        "gather_wi_start", [(plan_shard_ici(j), [0, 1 + j], 1) for j in range(2)],
        [w_in_own] + [lax.empty((d, sh), BF16) for _ in range(3)])

    tm = _tile(s, 1024)
    tn = _tile(d, 1024)
    tk = _tile(d, 2048)
    ts = _tile(s, 2048)
    h = rms_fwd(x2, norm_gain, "rms_fwd_x", after=(wi_token,))
    tabs = _rope_tables(s)
    qg = jnp.tile(q_norm_gain, (1, LANES // HEAD_DIM))
    kg = jnp.tile(k_norm_gain, (1, LANES // HEAD_DIM))
    seg_i = jnp.arange(SEG) // HEAD_DIM
    segb = (seg_i[:, None] == seg_i[None, :]).astype(BF16)
    z = in_proj_part(h, [wi_arrs[0]], None, order, 0, in_w)
    own, land_x = exchange_wait("gather_wi_wait0", plan_shard_ici(0), wi_sems[0], [wi_arrs[0], wi_arrs[1]], z)
    own, land_y = exchange_wait("gather_wi_wait1", plan_shard_ici(1), wi_sems[1], [own, wi_arrs[2]], land_x)
    relay_sems, relayed, relay_token = exchange_start(
        "gather_wi_relay_start", [(plan_shard_relay, [0, 1, 2], 2)], [land_x, land_y, wi_arrs[3]])
    rest_sems, rest, rest_token = exchange_start(
        "gather_rest_start", [(plan_gather_ici(3), [0, 1, 2, 3], 12)], rest, after=(relay_token,))
    land_x, land_y = exchange("gather_wi_pass01", plan_shard_pass, relayed[:2], 2, after=(rest_token,))
    z = in_proj_part(h, [land_x, land_y], z, order, 1, in_w)
    land_x, land_y, land_far = exchange_wait("gather_wi_relay_wait", plan_shard_relay, relay_sems[0],
                                             [land_x, land_y, relayed[2]], z)
    land_far, = exchange("gather_wi_pass2", plan_shard_pass, [land_far], 1)
    z = in_proj_part(h, [land_far], z, order, 3, in_w)
    w_shards = [own, land_x, land_y, land_far]
    wo_all, wg_all, wp_all, conv_all = exchange_wait("gather_rest_wait", plan_gather_ici(3), rest_sems[0], rest, z)
    pass_sems, passed, pass_token = exchange_start(
        "gather_rest_pass_start", [(plan_gather_pass, [0, 1, 2], 9)], [wo_all, wg_all, wp_all])
    conv_full = conv_all.transpose(1, 0, 2).reshape(SUBLANES, cw)
    qs, ks, vb = qk_prep_fwd(z, tabs, qg, kg, segb, aw, after=(pass_token,))
    attn, mix = attn_fwd(qs, ks, vb, z, attn_sinks, aw, d, ga_off)
    mix = conv_fwd(z, conv_full, mix, aw, cw, b_off)
    wo_all, wg_all, wp_all = exchange_wait("gather_rest_pass_wait", plan_gather_pass, pass_sems[0], passed, mix)
    wo_full = wo_all.reshape(d, d)
    wg_full = wg_all.reshape(d, d)
    sq = lambda shape: dict(
        a_spec=pl.BlockSpec((tm, tk), lambda i, j, k: (i, k)),
        o_spec=pl.BlockSpec((tm, tn), lambda i, j, k: (i, j)),
        out_shape=jax.ShapeDtypeStruct(shape, F32))
    x1 = matmul(mix, wo_full, grid=(s // tm, d // tn, d // tk),
                b_spec=pl.BlockSpec((tk, tn), lambda i, j, k: (k, j)), dims=NN, name="mm_x1",
                res=x2, res_spec=pl.BlockSpec((tm, tn), lambda i, j, k: (i, j)), **sq((s, d)))
    hg = rms_fwd(x1, ple_gate_norm_gain, "rms_fwd_x1")
    gl = matmul(hg, wg_full, grid=(s // tm, d // tn, d // tk),
                b_spec=pl.BlockSpec((tk, tn), lambda i, j, k: (k, j)), dims=NN, name="mm_gl", **sq((s, d)))
    pq = d // N_CHIPS

    d_gl, d_pe, dy, acc_head = head_fwd_bwd(x1, gl, p2, wp_all, tgt, b_ple_gate, ple_norm_gain)
    d_hg = matmul(d_gl, wg_full, grid=(s // tm, d // tn, d // tk),
                  b_spec=pl.BlockSpec((tn, tk), lambda i, j, k: (j, k)), dims=NT, name="mm_d_hg", **sq((s, d)))
    wgrad = lambda a_cols, shape3, o_spec, b_cols, name, a, b, grid: matmul(
        a, b, grid=grid,
        a_spec=pl.BlockSpec((ts, a_cols), lambda i, j, k: (k, i)),
        b_spec=pl.BlockSpec((ts, b_cols), lambda i, j, k: (k, j)),
        o_spec=o_spec, out_shape=jax.ShapeDtypeStruct(shape3, BF16), dims=TN, name=name)
    g_wg = wgrad(tn, (d, d), pl.BlockSpec((tn, tn), lambda i, j, k: (i, j)), tn, "mm_g_wg", hg, d_gl,
                 (d // tn, d // tn, s // ts))
    g_wp = wgrad(ple, (N_CHIPS, ple, pq), pl.BlockSpec((None, ple, pq), lambda i, j, k: (j, 0, 0)), pq,
                 "mm_g_wp", p2, d_pe, (1, N_CHIPS, s // ts))
    d_x1, d_x1b, acc_g = rms_bwd_add(d_hg, x1, dy, ple_gate_norm_gain, "rms_bwd_x1", True)
    d_mix = matmul(d_x1b, wo_full, grid=(s // tm, d // tn, d // tk),
                   b_spec=pl.BlockSpec((tn, tk), lambda i, j, k: (j, k)), dims=NT, name="mm_d_mix", **sq((s, d)))
    g_wo = wgrad(tn, (d, d), pl.BlockSpec((tn, tn), lambda i, j, k: (i, j)), tn, "mm_g_wo", mix, d_x1b,
                 (d // tn, d // tn, s // ts))
    def reduce_start(tag, grads):
        n = len(grads)
        lands = [lax.empty((N_CHIPS, a.shape[1] // 2, a.shape[2]), BF16) for a in grads]
        swapped = exchange("swap_" + tag, plan_swap, list(grads) + lands, n)
        parts = [add_sibling(g, o, core, "add_sibling_%s%d" % (tag, i))
                 for i, (g, o) in enumerate(zip(swapped[:n], swapped[n:]))]
        return exchange_start("scatter_%s_start" % tag, [(plan_scatter, list(range(2 * n)), 3 * n)],
                              parts + [lax.empty(a.shape, BF16) for a in parts])

    def reduce_sum(tag, handle, after):
        sems, arrays, _ = handle
        n = len(arrays) // 2
        got = exchange_wait("scatter_%s_wait" % tag, plan_scatter, sems[0], arrays, after)
        return [sum_chips(pt, o, place, "sum_chips_%s%d" % (tag, i))
                for i, (pt, o) in enumerate(zip(got[:n], got[n:]))]

    early = reduce_start("early", [g_wo.reshape(N_CHIPS, dq, d), g_wg.reshape(N_CHIPS, dq, d), g_wp])
    d_o, dz = gate_bwd(d_mix, attn, z, aw, ga_off, after=(early[-1],))
    dqs, dks, dvs, acc_sink = attn_bwd(qs, ks, vb, d_o, attn_sinks, aw)
    dz, acc_qk = qk_prep_bwd(z, dqs, dks, dvs, tabs, qg, kg, segb, dz, aw)
    dz, acc_conv = conv_bwd(z, d_mix, conv_full, dz, aw, cw, b_off)
    join_sems, joining, join_token = exchange_start(
        "join_early_start", [(plan_join, [0, 1, 2], 3)], reduce_sum("early", early, dz))
    g_wi = matmul(h, dz, grid=(d // tn, N_CHIPS, 1),
                  a_spec=pl.BlockSpec((s, tn), lambda i, j, k: (0, i), pipeline_mode=pl.Buffered(1)),
                  b_spec=pl.BlockSpec((s, sh), lambda i, j, k: (0, j)),
                  o_spec=pl.BlockSpec((None, tn, sh), lambda i, j, k: (j, i, 0)),
                  out_shape=jax.ShapeDtypeStruct((N_CHIPS, d, sh), BF16), dims=TN, name="mm_g_wi",
                  after=(join_token,), vmem=VMEM_LIMIT_BIG)
    full_wo, full_wg, full_wp = exchange_wait("join_early_wait", plan_join, join_sems[0], joining, g_wi)
    late = reduce_start("late", [g_wi])
    d_h = in_proj_bwd(dz, w_shards, order, d, after=(late[-1],))
    grad_x, acc_x = rms_bwd_add(d_h, x2, d_x1, norm_gain, "rms_bwd_x", False)

    wsm = max(d, cw)
    misc = jnp.concatenate([acc_qk[0:1, :HEAD_DIM], acc_qk[1:2, :HEAD_DIM], acc_sink[0:1, :nq]], axis=1)
    small = jnp.concatenate([
        _pad_cols(acc_x[0:1], wsm), _pad_cols(acc_g[0:1], wsm), _pad_cols(acc_head[0:1], wsm),
        _pad_cols(acc_head[1:2], wsm), _pad_cols(misc, wsm), _pad_cols(acc_conv[0:3], wsm),
        _pad_rows(_pad_cols(acc_head[2:3], wsm), SUBLANES)], axis=0)
    device = 2 * chip + lax.axis_index("c")
    small_sems, small_bufs, small_token = exchange_start(
        "small_start", [(plan_allgather_small, [0], 7)],
        [lax.dynamic_update_slice(jnp.zeros((8,) + small.shape, F32), small[None], (device, 0, 0))])
    last_sems, last_halves, last_token = exchange_start(
        "join_late_start", [(plan_join, [0], 1)], reduce_sum("late", late, small_token))
    big, after = {}, (last_token,)
    for nm, g, w, m, v in (("w_out", full_wo, w_out, m_w_out, v_w_out),
                           ("w_ple_gate", full_wg, w_ple_gate, m_w_ple_gate, v_w_ple_gate),
                           ("w_ple_proj", full_wp, w_ple_proj, m_w_ple_proj, v_w_ple_proj)):
        stepped = adamw(g, w[0], m[0], v[0], "adamw_" + nm, after=after)
        big[nm], after = [o[None] for o in stepped], (stepped[1],)
    full_wi, = exchange_wait("join_late_wait", plan_join, last_sems[0], last_halves, after[0])
    big["w_in"] = [o[None] for o in adamw(full_wi, w_in[0], m_w_in[0], v_w_in[0], "adamw_w_in")]

    gathered, = exchange_wait("small_wait", plan_allgather_small, small_sems[0], small_bufs, big["w_in"][1])
    tot = sum_devices(gathered)
    loss = tot[SUBLANES, 0]

    def pack(vals):
        ng, pg, bg, eg, qgv, kgv, sk, cv = vals
        misc_v = jnp.concatenate([qgv, kgv, sk], axis=1)
        return jnp.concatenate([_pad_cols(ng, wsm), _pad_cols(pg, wsm), _pad_cols(bg, wsm), _pad_cols(eg, wsm),
                                _pad_cols(misc_v, wsm), _pad_cols(cv[0], wsm)], axis=0)

    g_small = jnp.concatenate(
        [tot[0:5], _pad_cols(lax.dynamic_slice(tot[5:8], (0, chip * cq), (3, cq)), wsm)], axis=0)
    w_small = pack((norm_gain, ple_gate_norm_gain, b_ple_gate, ple_norm_gain, q_norm_gain, k_norm_gain,
                    attn_sinks, conv_w))
    m_small = pack((m_norm_gain, m_ple_gate_norm_gain, m_b_ple_gate, m_ple_norm_gain, m_q_norm_gain,
                    m_k_norm_gain, m_attn_sinks, m_conv_w))
    v_small = pack((v_norm_gain, v_ple_gate_norm_gain, v_b_ple_gate, v_ple_norm_gain, v_q_norm_gain,
                    v_k_norm_gain, v_attn_sinks, v_conv_w))
    sm = adamw(g_small, w_small, m_small, v_small, "adamw_small")

    def unpack(a):
        return {"norm_gain": a[0:1, :d], "ple_gate_norm_gain": a[1:2, :d], "b_ple_gate": a[2:3, :d],
                "ple_norm_gain": a[3:4, :d], "q_norm_gain": a[4:5, :HEAD_DIM],
                "k_norm_gain": a[4:5, HEAD_DIM:2 * HEAD_DIM],
                "attn_sinks": a[4:5, 2 * HEAD_DIM:2 * HEAD_DIM + nq], "conv_w": a[5:8, :cq][None]}

    order = ("norm_gain", "w_in", "q_norm_gain", "k_norm_gain", "attn_sinks", "conv_w", "w_out",
             "ple_gate_norm_gain", "w_ple_gate", "b_ple_gate", "w_ple_proj", "ple_norm_gain")
    outs = [loss, grad_x[None]]
    for kind in range(4):
        table = unpack(sm[kind])
        for nm in order:
            outs.append(big[nm][kind] if nm in big else table[nm])
    return tuple(outs)
```

```python
import functools

import jax
import jax.numpy as jnp
from jax import lax
from jax.experimental import pallas as pl
from jax.experimental.pallas import tpu as pltpu

F32 = jnp.float32
BF16 = jnp.bfloat16
MESH = pl.DeviceIdType.MESH

HEAD_DIM = 64
N_KV_HEADS = 4
KV_WIDTH = N_KV_HEADS * HEAD_DIM
BLOCK = 128
ROT_DIM = 16
ROPE_THETA = 500000.0
EPS = 1e-6
NEG_INF = -1e30
N_CHIPS = 4
LANES = 128
SUBLANES = 8
COLT = 512
SEG = 256
VMEM_LIMIT = 48 * 1024 * 1024
VMEM_LIMIT_BIG = 60 * 1024 * 1024

ADAM_LR = 0.001
ADAM_B1 = 0.9
ADAM_B2 = 0.999
ADAM_EPS = 1e-08
ADAM_WD = 0.01
ADAM_STEP = 10

NN = (((1,), (0,)), ((), ()))
NT = (((1,), (1,)), ((), ()))
TN = (((0,), (0,)), ((), ()))


def _call(body, **kw):
    return pl.pallas_call(body, **kw)


def _call_after(body, after, **kw):
    n_in, n_after = len(kw["in_specs"]), len(after)
    kw["in_specs"] = list(kw["in_specs"]) + [pl.BlockSpec(memory_space=pl.ANY)] * n_after

    def body_after(*refs):
        body(*refs[:n_in], *refs[n_in + n_after:])

    call = _call(body_after, **kw)
    return lambda *args: call(*args, *after)


def _params(sem, vmem=VMEM_LIMIT):
    return pltpu.CompilerParams(dimension_semantics=sem, vmem_limit_bytes=vmem)


def _tile(n, pref):
    return pref if n % pref == 0 else n


def _sigmoid(v):
    return 1.0 / (1.0 + jnp.exp(-v))


def _hbm():
    return pl.BlockSpec(memory_space=pl.ANY)


def cast_bf16(a, name):
    r, c = a.shape
    tr = _tile(r, 512)

    def body(a_ref, o_ref):
        o_ref[...] = a_ref[...].astype(BF16)

    return _call(body, name=name, grid=(r // tr,),
                 in_specs=[pl.BlockSpec((tr, c), lambda i: (i, 0))],
                 out_specs=pl.BlockSpec((tr, c), lambda i: (i, 0)),
                 out_shape=jax.ShapeDtypeStruct((r, c), BF16),
                 compiler_params=_params(("parallel",)))(a)


def cast_into_slot(a, chip, name):
    r, c = a.shape
    tr = _tile(r, 512)

    def body(chip_ref, a_ref, o_ref):
        o_ref[...] = a_ref[...].astype(BF16)

    grid_spec = pltpu.PrefetchScalarGridSpec(
        num_scalar_prefetch=1, grid=(r // tr,),
        in_specs=[pl.BlockSpec((tr, c), lambda i, chip_ref: (i, 0))],
        out_specs=pl.BlockSpec((None, tr, c), lambda i, chip_ref: (chip_ref[0], i, 0)))
    return _call(body, name=name, grid_spec=grid_spec,
                 out_shape=jax.ShapeDtypeStruct((N_CHIPS, r, c), BF16),
                 compiler_params=_params(("parallel",)))(chip, a)


def rms_fwd(x, gain, name, after=()):
    s, d = x.shape
    tm = _tile(s, 512)

    def body(x_ref, g_ref, o_ref):
        xf = x_ref[...]
        r = lax.rsqrt(jnp.mean(xf * xf, axis=-1, keepdims=True) + EPS)
        o_ref[...] = ((xf * r) * g_ref[...]).astype(BF16)

    return _call_after(body, after, name=name, grid=(s // tm,),
                       in_specs=[pl.BlockSpec((tm, d), lambda i: (i, 0)),
                                 pl.BlockSpec((1, d), lambda i: (0, 0))],
                       out_specs=pl.BlockSpec((tm, d), lambda i: (i, 0)),
                       out_shape=jax.ShapeDtypeStruct((s, d), BF16),
                       compiler_params=_params(("parallel",)))(x, gain)


def out_proj_norm(mix, wo, x, gain):
    s, d = x.shape
    tm = _tile(s, 512)

    def body(m_ref, w_ref, x_ref, g_ref, x1_ref, hg_ref):
        x1 = x_ref[...] + jnp.dot(m_ref[...], w_ref[...], preferred_element_type=F32)
        x1_ref[...] = x1
        r = lax.rsqrt(jnp.mean(x1 * x1, axis=-1, keepdims=True) + EPS)
        hg_ref[...] = ((x1 * r) * g_ref[...]).astype(BF16)

    row = pl.BlockSpec((tm, d), lambda i: (i, 0))
    return _call(body, name="out_proj_norm", grid=(s // tm,),
                 in_specs=[row, pl.BlockSpec((d, d), lambda i: (0, 0), pipeline_mode=pl.Buffered(1)), row,
                           pl.BlockSpec((1, d), lambda i: (0, 0))],
                 out_specs=[row, row],
                 out_shape=[jax.ShapeDtypeStruct((s, d), F32), jax.ShapeDtypeStruct((s, d), BF16)],
                 compiler_params=_params(("parallel",), VMEM_LIMIT_BIG))(mix, wo, x, gain)


def gate_proj_bwd_norm(d_gl, wg, xin, add, gain):
    s, d = xin.shape
    tm = _tile(s, 256)

    def body(dg_ref, w_ref, x_ref, a_ref, g_ref, dx_ref, dxb_ref, acc_ref):
        i = pl.program_id(0)
        dyv = lax.dot_general(dg_ref[...], w_ref[...], NT, preferred_element_type=F32)
        xf = x_ref[...]
        r = lax.rsqrt(jnp.mean(xf * xf, axis=-1, keepdims=True) + EPS)
        xhat = xf * r
        gd = dyv * g_ref[...]
        dx = a_ref[...] + r * (gd - xhat * jnp.mean(xhat * gd, axis=-1, keepdims=True))
        dx_ref[...] = dx
        dxb_ref[...] = dx.astype(BF16)

        @pl.when(i == 0)
        def _():
            acc_ref[...] = jnp.zeros_like(acc_ref)

        acc_ref[0:1, :] += jnp.sum(dyv * xhat, axis=0, keepdims=True)

    row = pl.BlockSpec((tm, d), lambda i: (i, 0))
    return _call(body, name="gate_proj_bwd_norm", grid=(s // tm,),
                 in_specs=[row, pl.BlockSpec((d, d), lambda i: (0, 0), pipeline_mode=pl.Buffered(1)), row, row,
                           pl.BlockSpec((1, d), lambda i: (0, 0))],
                 out_specs=[row, row, pl.BlockSpec((SUBLANES, d), lambda i: (0, 0))],
                 out_shape=[jax.ShapeDtypeStruct((s, d), F32), jax.ShapeDtypeStruct((s, d), BF16),
                            jax.ShapeDtypeStruct((SUBLANES, d), F32)],
                 compiler_params=_params(("arbitrary",), VMEM_LIMIT_BIG))(d_gl, wg, xin, add, gain)


def rms_bwd_add(dyn, xin, add, gain, name, want_bf16):
    s, d = xin.shape
    tm = _tile(s, 512)

    def body(dy_ref, x_ref, a_ref, g_ref, *outs):
        i = pl.program_id(0)
        dx_ref, acc_ref = outs[0], outs[-1]
        xf = x_ref[...]
        r = lax.rsqrt(jnp.mean(xf * xf, axis=-1, keepdims=True) + EPS)
        xhat = xf * r
        dyv = dy_ref[...]
        gd = dyv * g_ref[...]
        dx = a_ref[...] + r * (gd - xhat * jnp.mean(xhat * gd, axis=-1, keepdims=True))
        dx_ref[...] = dx
        if want_bf16:
            outs[1][...] = dx.astype(BF16)

        @pl.when(i == 0)
        def _():
            acc_ref[...] = jnp.zeros_like(acc_ref)

        acc_ref[0:1, :] += jnp.sum(dyv * xhat, axis=0, keepdims=True)

    row = pl.BlockSpec((tm, d), lambda i: (i, 0))
    out_specs = [row] + ([row] if want_bf16 else []) + [pl.BlockSpec((SUBLANES, d), lambda i: (0, 0))]
    out_shape = ([jax.ShapeDtypeStruct((s, d), F32)]
                 + ([jax.ShapeDtypeStruct((s, d), BF16)] if want_bf16 else [])
                 + [jax.ShapeDtypeStruct((SUBLANES, d), F32)])
    return _call(body, name=name, grid=(s // tm,),
                 in_specs=[row, row, row, pl.BlockSpec((1, d), lambda i: (0, 0))],
                 out_specs=out_specs, out_shape=out_shape,
                 compiler_params=_params(("arbitrary",), VMEM_LIMIT_BIG))(dyn, xin, add, gain)


def head_fwd_bwd(x1, hg, wg, p, wp, tgt, bias, ple_gain):
    s, d = x1.shape
    tm = _tile(s, 256)

    def body(x1_ref, hg_ref, wg_ref, p_ref, wp_ref, t_ref, b_ref, g_ref, dgl_ref, dpe_ref, dy_ref, acc_ref):
        i = pl.program_id(0)
        gl = jnp.dot(hg_ref[...], wg_ref[...], preferred_element_type=F32)
        pb = p_ref[...].astype(BF16)
        pev = jnp.concatenate([jnp.dot(pb, wp_ref[q], preferred_element_type=F32) for q in range(N_CHIPS)],
                              axis=1)
        r = lax.rsqrt(jnp.mean(pev * pev, axis=-1, keepdims=True) + EPS)
        pehat = pev * r
        gain = g_ref[...]
        e = pehat * gain
        gate = _sigmoid(gl + b_ref[...])
        diff = (x1_ref[...] + gate * e) - t_ref[...]
        dy = diff * (1.0 / d)
        dy_ref[...] = dy
        d_gl = (dy * e) * (gate * (1.0 - gate))
        dgl_ref[...] = d_gl.astype(BF16)
        d_e = dy * gate
        gd = d_e * gain
        d_pe = r * (gd - pehat * jnp.mean(pehat * gd, axis=-1, keepdims=True))
        dpe_ref[...] = d_pe.astype(BF16)

        @pl.when(i == 0)
        def _():
            acc_ref[...] = jnp.zeros_like(acc_ref)

        acc_ref[0:1, :] += jnp.sum(d_gl, axis=0, keepdims=True)
        acc_ref[1:2, :] += jnp.sum(d_e * pehat, axis=0, keepdims=True)
        acc_ref[2:3, :] += jnp.sum(diff * diff, axis=0, keepdims=True) * (0.5 / d)

    row = pl.BlockSpec((tm, d), lambda i: (i, 0))
    vec = pl.BlockSpec((1, d), lambda i: (0, 0))
    return _call(body, name="head_fwd_bwd", grid=(s // tm,),
                 in_specs=[row, row, pl.BlockSpec((d, d), lambda i: (0, 0), pipeline_mode=pl.Buffered(1)),
                           pl.BlockSpec((tm, p.shape[1]), lambda i: (i, 0)),
                           pl.BlockSpec(wp.shape, lambda i: (0, 0, 0)), row, vec, vec],
                 out_specs=[row, row, row, pl.BlockSpec((SUBLANES, d), lambda i: (0, 0))],
                 out_shape=[jax.ShapeDtypeStruct((s, d), BF16), jax.ShapeDtypeStruct((s, d), BF16),
                            jax.ShapeDtypeStruct((s, d), F32), jax.ShapeDtypeStruct((SUBLANES, d), F32)],
                 compiler_params=_params(("arbitrary",), VMEM_LIMIT_BIG))(
                     x1, hg, wg, p, wp, tgt, bias, ple_gain)


def adamw(g, w, m, v, name, after=()):
    r, c = g.shape
    tr = _tile(r, 256)

    def body(g_ref, w_ref, m_ref, v_ref, go_ref, d_ref, mo_ref, vo_ref):
        gv = g_ref[...]
        mn = ADAM_B1 * m_ref[...] + (1.0 - ADAM_B1) * gv
        vn = ADAM_B2 * v_ref[...] + (1.0 - ADAM_B2) * (gv * gv)
        m_hat = mn / (1.0 - ADAM_B1 ** ADAM_STEP)
        v_hat = vn / (1.0 - ADAM_B2 ** ADAM_STEP)
        go_ref[...] = gv
        d_ref[...] = -ADAM_LR * (m_hat / (jnp.sqrt(v_hat) + ADAM_EPS) + ADAM_WD * w_ref[...])
        mo_ref[...] = mn
        vo_ref[...] = vn

    blk = pl.BlockSpec((tr, c), lambda i: (i, 0))
    shp = jax.ShapeDtypeStruct((r, c), F32)
    return _call_after(body, after, name=name, grid=(r // tr,), in_specs=[blk] * 4, out_specs=[blk] * 4,
                       out_shape=[shp] * 4, compiler_params=_params(("parallel",)))(g, w, m, v)


def matmul(a, b, *, grid, a_spec, b_spec, o_spec, out_shape, dims, name, res=None, res_spec=None, after=(),
           vmem=VMEM_LIMIT):
    nk = grid[2]
    acc_shape = tuple(d for d in o_spec.block_shape if d is not None)

    def body(*refs):
        a_ref, b_ref = refs[:2]
        r_ref = refs[2] if res is not None else None
        o_ref = refs[3] if res is not None else refs[2]
        part = lax.dot_general(a_ref[...].astype(BF16), b_ref[...].astype(BF16), dims, preferred_element_type=F32)

        def finish(out):
            if res is not None:
                out = r_ref[...] + out
            o_ref[...] = out.astype(o_ref.dtype)

        if nk == 1:
            finish(part)
            return
        acc_ref = refs[-1]
        k = pl.program_id(2)

        @pl.when(k == 0)
        def _():
            acc_ref[...] = part

        @pl.when((k > 0) & (k < nk - 1))
        def _():
            acc_ref[...] += part

        @pl.when(k == nk - 1)
        def _():
            finish(acc_ref[...] + part)

    in_specs = [a_spec, b_spec] + ([res_spec] if res is not None else [])
    args = (a, b) + ((res,) if res is not None else ())
    return _call_after(body, after, name=name, grid=grid, in_specs=in_specs, out_specs=o_spec,
                       out_shape=out_shape, scratch_shapes=[pltpu.VMEM(acc_shape, F32)] if nk > 1 else [],
                       compiler_params=_params(("parallel", "parallel", "arbitrary"), vmem))(*args)


def in_proj_part(h, w, z_prev, order, q, in_w):
    s, d = h.shape
    sh = w.shape[1]
    tm = _tile(s, 512)

    def body(order_ref, h_ref, w_ref, *rest):
        rest[-1][...] = jnp.dot(h_ref[...], w_ref[...], preferred_element_type=F32)

    in_specs = [pl.BlockSpec((tm, d), lambda i, order_ref: (i, 0)),
                pl.BlockSpec((d, sh), lambda i, order_ref: (0, 0))]
    args = [order, h, w]
    if z_prev is not None:
        in_specs.append(_hbm())
        args.append(z_prev)
    grid_spec = pltpu.PrefetchScalarGridSpec(
        num_scalar_prefetch=1, grid=(s // tm,), in_specs=in_specs,
        out_specs=pl.BlockSpec((tm, sh), lambda i, order_ref: (i, order_ref[q])))
    return _call(body, name="mm_z%d" % q, grid_spec=grid_spec,
                 out_shape=jax.ShapeDtypeStruct((s, in_w), F32),
                 input_output_aliases={3: 0} if z_prev is not None else {},
                 compiler_params=_params(("parallel",)))(*args)


def in_proj_bwd(dz, ws, order, d, after):
    s = dz.shape[0]
    sh = ws[0].shape[1]
    tm, tn = _tile(s, 512), _tile(d, 1024)
    nq, n_after = len(ws), len(after)

    def body(order_ref, *refs):
        a_refs, b_refs, o_ref = refs[:nq], refs[nq:2 * nq], refs[2 * nq + n_after]
        acc = lax.dot_general(a_refs[0][...], b_refs[0][...], NT, preferred_element_type=F32)
        for q in range(1, nq):
            acc += lax.dot_general(a_refs[q][...], b_refs[q][...], NT, preferred_element_type=F32)
        o_ref[...] = acc

    a_spec = lambda q: pl.BlockSpec((tm, sh), functools.partial(lambda i, j, order_ref, q: (i, order_ref[q]), q=q))
    grid_spec = pltpu.PrefetchScalarGridSpec(
        num_scalar_prefetch=1, grid=(s // tm, d // tn),
        in_specs=[a_spec(q) for q in range(nq)]
        + [pl.BlockSpec((tn, sh), lambda i, j, order_ref: (j, 0))] * nq + [_hbm()] * n_after,
        out_specs=pl.BlockSpec((tm, tn), lambda i, j, order_ref: (i, j)))
    return _call(body, name="mm_d_h", grid_spec=grid_spec, out_shape=jax.ShapeDtypeStruct((s, d), F32),
                 compiler_params=_params(("parallel", "parallel"), VMEM_LIMIT_BIG))(
                     order, *([dz] * nq), *ws, *after)


def _seg_mean(sq, segb):
    parts = []
    for cgrp in range(sq.shape[1] // SEG):
        blk = sq[:, cgrp * SEG:(cgrp + 1) * SEG]
        hi = blk.astype(BF16)
        r1 = blk - hi.astype(F32)
        mid = r1.astype(BF16)
        lo = (r1 - mid.astype(F32)).astype(BF16)
        acc = jnp.dot(hi, segb, preferred_element_type=F32)
        acc += jnp.dot(mid, segb, preferred_element_type=F32)
        acc += jnp.dot(lo, segb, preferred_element_type=F32)
        parts.append(acc)
    out = parts[0] if len(parts) == 1 else jnp.concatenate(parts, axis=1)
    return out * (1.0 / HEAD_DIM)


def _rope(v, cos, sa, sb):
    parts = []
    for cgrp in range(v.shape[1] // LANES):
        blk = v[:, cgrp * LANES:(cgrp + 1) * LANES]
        parts.append(blk * cos + pltpu.roll(blk, LANES - 8, 1) * sa + pltpu.roll(blk, 8, 1) * sb)
    return parts[0] if len(parts) == 1 else jnp.concatenate(parts, axis=1)


def _rope_t(dv, cos, sa, sb):
    parts = []
    for cgrp in range(dv.shape[1] // LANES):
        blk = dv[:, cgrp * LANES:(cgrp + 1) * LANES]
        parts.append(blk * cos + pltpu.roll(blk * sa, 8, 1) + pltpu.roll(blk * sb, LANES - 8, 1))
    return parts[0] if len(parts) == 1 else jnp.concatenate(parts, axis=1)


def _tile_lanes(vec, width):
    reps = width // LANES
    return vec if reps == 1 else jnp.tile(vec, (1, reps))


def qk_prep_fwd(z, tabs, qg, kg, segb, aw, after=()):
    s = z.shape[0]
    tm = _tile(s, 512)
    wq = aw + 2 * KV_WIDTH

    def body(z_ref, cos_ref, sa_ref, sb_ref, qg_ref, kg_ref, seg_ref, qs_ref, ks_ref, vb_ref):
        zz = z_ref[...]
        q, k, v = zz[:, :aw], zz[:, aw:aw + KV_WIDTH], zz[:, aw + KV_WIDTH:]
        cos, sa, sb, segm = cos_ref[...], sa_ref[...], sb_ref[...], seg_ref[...]
        rq = lax.rsqrt(_seg_mean(q * q, segm) + EPS)
        qn = (q * rq) * _tile_lanes(qg_ref[...], aw)
        qs_ref[...] = (_rope(qn, cos, sa, sb) * (HEAD_DIM ** -0.5)).astype(BF16)
        rk = lax.rsqrt(_seg_mean(k * k, segm) + EPS)
        kn = (k * rk) * _tile_lanes(kg_ref[...], KV_WIDTH)
        ks_ref[...] = _rope(kn, cos, sa, sb).astype(BF16)
        vb_ref[...] = v.astype(BF16)

    tab = pl.BlockSpec((tm, LANES), lambda i: (i, 0))
    vec = pl.BlockSpec((1, LANES), lambda i: (0, 0))
    return _call_after(body, after, name="qk_prep_fwd", grid=(s // tm,),
                       in_specs=[pl.BlockSpec((tm, wq), lambda i: (i, 0)), tab, tab, tab, vec, vec,
                                 pl.BlockSpec((SEG, SEG), lambda i: (0, 0))],
                       out_specs=[pl.BlockSpec((tm, aw), lambda i: (i, 0)),
                                  pl.BlockSpec((tm, KV_WIDTH), lambda i: (i, 0)),
                                  pl.BlockSpec((tm, KV_WIDTH), lambda i: (i, 0))],
                       out_shape=[jax.ShapeDtypeStruct((s, aw), BF16), jax.ShapeDtypeStruct((s, KV_WIDTH), BF16),
                                  jax.ShapeDtypeStruct((s, KV_WIDTH), BF16)],
                       compiler_params=_params(("parallel",)))(z, *tabs, qg, kg, segb)


def qk_prep_bwd(z, dqs, dks, dvs, tabs, qg, kg, segb, dz, aw):
    s = z.shape[0]
    tm = _tile(s, 512)
    wq = aw + 2 * KV_WIDTH

    def body(z_ref, dq_ref, dk_ref, dv_ref, cos_ref, sa_ref, sb_ref, qg_ref, kg_ref, seg_ref, dz_in,
             dz_ref, acc_ref):
        i = pl.program_id(0)
        zz = z_ref[...]
        q, k = zz[:, :aw], zz[:, aw:aw + KV_WIDTH]
        cos, sa, sb, segm = cos_ref[...], sa_ref[...], sb_ref[...], seg_ref[...]

        def one(xv, dout, gvec, width):
            g = _tile_lanes(gvec, width)
            r = lax.rsqrt(_seg_mean(xv * xv, segm) + EPS)
            xhat = xv * r
            dn = _rope_t(dout, cos, sa, sb)
            gd = dn * g
            dx = r * (gd - xhat * _seg_mean(xhat * gd, segm))
            contrib = jnp.sum(dn * xhat, axis=0, keepdims=True)
            folded = contrib[:, :LANES]
            for cgrp in range(1, width // LANES):
                folded = folded + contrib[:, cgrp * LANES:(cgrp + 1) * LANES]
            return dx, folded + pltpu.roll(folded, HEAD_DIM, 1)

        dq, gq = one(q, dq_ref[...] * (HEAD_DIM ** -0.5), qg_ref[...], aw)
        dk, gk = one(k, dk_ref[...], kg_ref[...], KV_WIDTH)
        dz_ref[:, :aw] = dq.astype(BF16)
        dz_ref[:, aw:aw + KV_WIDTH] = dk.astype(BF16)
        dz_ref[:, aw + KV_WIDTH:] = dv_ref[...].astype(BF16)

        @pl.when(i == 0)
        def _():
            acc_ref[...] = jnp.zeros_like(acc_ref)

        acc_ref[0:1, :] += gq
        acc_ref[1:2, :] += gk

    tab = pl.BlockSpec((tm, LANES), lambda i: (i, 0))
    vec = pl.BlockSpec((1, LANES), lambda i: (0, 0))
    kvb = pl.BlockSpec((tm, KV_WIDTH), lambda i: (i, 0))
    return _call(body, name="qk_prep_bwd", grid=(s // tm,),
                 in_specs=[pl.BlockSpec((tm, wq), lambda i: (i, 0)),
                           pl.BlockSpec((tm, aw), lambda i: (i, 0)), kvb, kvb, tab, tab, tab, vec, vec,
                           pl.BlockSpec((SEG, SEG), lambda i: (0, 0)), _hbm()],
                 out_specs=[pl.BlockSpec((tm, wq), lambda i: (i, 0)),
                            pl.BlockSpec((SUBLANES, LANES), lambda i: (0, 0))],
                 out_shape=[jax.ShapeDtypeStruct(dz.shape, BF16), jax.ShapeDtypeStruct((SUBLANES, LANES), F32)],
                 input_output_aliases={10: 0},
                 compiler_params=_params(("arbitrary",)))(z, dqs, dks, dvs, *tabs, qg, kg, segb, dz)


def _placed(band, lane_idx):
    out = {}
    for kh in range(N_KV_HEADS):
        grp = band[:, (kh // 2) * LANES:(kh // 2 + 1) * LANES]
        for half in (0, 1):
            t = grp if half == kh % 2 else pltpu.roll(grp, HEAD_DIM, 1)
            keep = (lane_idx >= half * HEAD_DIM) & (lane_idx < (half + 1) * HEAD_DIM)
            out[kh, half] = jnp.where(keep, t, jnp.zeros_like(t))
    return out


def _softmax_with_sink(sc, valid, sink):
    sc = jnp.where(valid, sc, NEG_INF)
    m = jnp.maximum(jnp.max(sc, axis=-1, keepdims=True), sink)
    e = jnp.exp(sc - m)
    es = jnp.exp(sink - m)
    den = jnp.sum(e, axis=-1, keepdims=True) + es
    return e / den, es / den


def _stack_halves(placed, kh):
    return jnp.concatenate([placed[kh, 0], placed[kh, 1]], axis=0)


def _valid_mask(n):
    r_i = lax.broadcasted_iota(jnp.int32, (BLOCK, 2 * BLOCK), 0)
    j_i = lax.broadcasted_iota(jnp.int32, (BLOCK, 2 * BLOCK), 1)
    return (j_i > r_i) & (j_i <= r_i + BLOCK) & ((n > 0) | (j_i >= BLOCK))


def attn_fwd(qs, ks, vb, z, sinks, aw, d, ga_off):
    s = qs.shape[0]
    nb = s // BLOCK
    nq = aw // HEAD_DIM
    grp_sz = nq // N_KV_HEADS
    n_ga = aw // COLT

    def body(q_ref, ko_ref, kp_ref, vo_ref, vp_ref, *rest):
        ga_refs = rest[:n_ga]
        sink_ref, attn_ref, mix_ref = rest[n_ga:]
        n = pl.program_id(0)
        lane_idx = lax.broadcasted_iota(jnp.int32, (2 * BLOCK, LANES), 1)
        kpl = _placed(jnp.concatenate([kp_ref[...], ko_ref[...]], axis=0), lane_idx)
        vpl = _placed(jnp.concatenate([vp_ref[...], vo_ref[...]], axis=0), lane_idx)
        valid = _valid_mask(n)
        groups = range(nq // 2)
        kcat = [_stack_halves(kpl, kh) for kh in range(N_KV_HEADS)]
        vcat = [_stack_halves(vpl, kh) for kh in range(N_KV_HEADS)]
        scs = [lax.dot_general(q_ref[:, c * LANES:(c + 1) * LANES], kcat[2 * c // grp_sz], NT,
                               preferred_element_type=F32) for c in groups]
        probs = [jnp.concatenate(
            [_softmax_with_sink(scs[c][:, half * 2 * BLOCK:(half + 1) * 2 * BLOCK], valid,
                                sink_ref[0, 2 * c + half])[0].astype(BF16) for half in (0, 1)], axis=1)
                 for c in groups]
        for c in groups:
            acc = jnp.dot(probs[c], vcat[2 * c // grp_sz], preferred_element_type=F32)
            attn_ref[:, c * LANES:(c + 1) * LANES] = acc
            col = c * LANES
            ga = ga_refs[col // COLT][:, col % COLT:col % COLT + LANES]
            mix_ref[:, c * LANES:(c + 1) * LANES] = (acc * (ga * _sigmoid(ga))).astype(BF16)

    own = lambda n: (n, 0)
    prev = lambda n: (jnp.maximum(n - 1, 0), 0)
    kvs = lambda imap: pl.BlockSpec((BLOCK, KV_WIDTH), imap)
    ga_specs = [pl.BlockSpec((BLOCK, COLT), functools.partial(lambda n, j: (n, ga_off + j), j=j))
                for j in range(n_ga)]
    return _call(body, name="attn_fwd", grid=(nb,),
                 in_specs=[pl.BlockSpec((BLOCK, aw), own), kvs(own), kvs(prev), kvs(own), kvs(prev)]
                 + ga_specs + [pl.BlockSpec(memory_space=pltpu.SMEM)],
                 out_specs=[pl.BlockSpec((BLOCK, aw), own), pl.BlockSpec((BLOCK, aw), own)],
                 out_shape=[jax.ShapeDtypeStruct((s, aw), F32), jax.ShapeDtypeStruct((s, d), BF16)],
                 compiler_params=_params(("parallel",)))(qs, ks, ks, vb, vb, *([z] * n_ga), sinks)


def gate_bwd(d_mix, attn, z, aw, ga_off, after=()):
    s, in_w = z.shape
    tm = _tile(s, 1024)

    def body(dm_ref, at_ref, ga_ref, do_ref, dz_ref):
        ga = ga_ref[...]
        sig = _sigmoid(ga)
        dm = dm_ref[...]
        do_ref[...] = (dm * (ga * sig)).astype(BF16)
        dz_ref[...] = ((dm * at_ref[...]) * (sig * (1.0 + ga * (1.0 - sig)))).astype(BF16)

    blk = pl.BlockSpec((tm, COLT), lambda i, j: (i, j))
    gab = pl.BlockSpec((tm, COLT), lambda i, j: (i, ga_off + j))
    return _call_after(body, after, name="gate_bwd", grid=(s // tm, aw // COLT),
                       in_specs=[blk, blk, gab], out_specs=[blk, gab],
                       out_shape=[jax.ShapeDtypeStruct((s, aw), BF16), jax.ShapeDtypeStruct((s, in_w), BF16)],
                       compiler_params=_params(("parallel", "parallel")))(d_mix, attn, z)


def attn_bwd(qs, ks, vb, d_o, sinks, aw):
    s = qs.shape[0]
    nb = s // BLOCK
    nq = aw // HEAD_DIM
    grp_sz = nq // N_KV_HEADS

    def body(q_ref, ko_ref, kp_ref, vo_ref, vp_ref, do_ref, sink_ref, dq_ref, dk_ref, dv_ref, ds_ref,
             ck_ref, cv_ref):
        n = pl.program_id(0)

        @pl.when(n == 0)
        def _():
            ds_ref[...] = jnp.zeros_like(ds_ref)
            dk_ref[...] = jnp.zeros_like(dk_ref)
            dv_ref[...] = jnp.zeros_like(dv_ref)

        @pl.when(n < nb)
        def _():
            lane_idx = lax.broadcasted_iota(jnp.int32, (2 * BLOCK, LANES), 1)
            lane_row = lax.broadcasted_iota(jnp.int32, (1, LANES), 1)
            kpl = _placed(jnp.concatenate([kp_ref[...], ko_ref[...]], axis=0), lane_idx)
            vpl = _placed(jnp.concatenate([vp_ref[...], vo_ref[...]], axis=0), lane_idx)
            valid = _valid_mask(n)
            ds_row = jnp.zeros((1, LANES), F32)
            wide = 2 * BLOCK
            groups = range(nq // 2)
            per_kv = grp_sz // 2
            kcat = [_stack_halves(kpl, kh) for kh in range(N_KV_HEADS)]
            vcat = [_stack_halves(vpl, kh) for kh in range(N_KV_HEADS)]
            qg = [q_ref[:, c * LANES:(c + 1) * LANES] for c in groups]
            dog = [do_ref[:, c * LANES:(c + 1) * LANES] for c in groups]
            scs = [lax.dot_general(qg[c], kcat[c // per_kv], NT, preferred_element_type=F32) for c in groups]
            dps = [lax.dot_general(dog[c], vcat[c // per_kv], NT, preferred_element_type=F32) for c in groups]
            ds_pairs, p_pairs = [], []
            for c in groups:
                probs, dss = [], []
                for half in (0, 1):
                    h = 2 * c + half
                    cols = slice(half * wide, (half + 1) * wide)
                    p, p_sink = _softmax_with_sink(scs[c][:, cols], valid, sink_ref[0, h])
                    delta = jnp.sum(p * dps[c][:, cols], axis=-1, keepdims=True)
                    dss.append((p * (dps[c][:, cols] - delta)).astype(BF16))
                    probs.append(p.astype(BF16))
                    dsink = -jnp.sum(p_sink * delta, axis=0, keepdims=True)
                    ds_row += jnp.where(lane_row == h, dsink, 0.0)
                ds_pairs.append(jnp.concatenate(dss, axis=1))
                p_pairs.append(jnp.concatenate(probs, axis=1))
            for c in groups:
                dq_ref[:, c * LANES:(c + 1) * LANES] = jnp.dot(ds_pairs[c], kcat[c // per_kv],
                                                               preferred_element_type=F32)
            dkts = [lax.dot_general(qg[c], ds_pairs[c], TN, preferred_element_type=F32) for c in groups]
            dvts = [lax.dot_general(dog[c], p_pairs[c], TN, preferred_element_type=F32) for c in groups]
            dkt, dvt = [], []
            for kh in range(N_KV_HEADS):
                for parts, out in ((dkts, dkt), (dvts, dvt)):
                    tot = parts[kh * per_kv]
                    for c in range(kh * per_kv + 1, (kh + 1) * per_kv):
                        tot = tot + parts[c]
                    out.append(tot[:HEAD_DIM, :wide] + tot[HEAD_DIM:, wide:])
            ds_ref[0:1, :] += ds_row
            back = lambda t: jnp.concatenate(
                [jnp.concatenate(t[2 * g:2 * g + 2], axis=0).T for g in range(N_KV_HEADS // 2)], axis=1)
            dk_band, dv_band = back(dkt), back(dvt)

            @pl.when(n > 0)
            def _():
                dk_ref[...] = ck_ref[...] + dk_band[:BLOCK]
                dv_ref[...] = cv_ref[...] + dv_band[:BLOCK]

            ck_ref[...] = dk_band[BLOCK:]
            cv_ref[...] = dv_band[BLOCK:]

        @pl.when(n == nb)
        def _():
            dk_ref[...] = ck_ref[...]
            dv_ref[...] = cv_ref[...]

    own = lambda n: (jnp.minimum(n, nb - 1), 0)
    prev = lambda n: (jnp.clip(n - 1, 0, nb - 1), 0)
    done = lambda n: (jnp.maximum(n - 1, 0), 0)
    kvs = lambda imap: pl.BlockSpec((BLOCK, KV_WIDTH), imap)
    return _call(body, name="attn_bwd", grid=(nb + 1,),
                 in_specs=[pl.BlockSpec((BLOCK, aw), own), kvs(own), kvs(prev), kvs(own), kvs(prev),
                           pl.BlockSpec((BLOCK, aw), own), pl.BlockSpec(memory_space=pltpu.SMEM)],
                 out_specs=[pl.BlockSpec((BLOCK, aw), own), kvs(done), kvs(done),
                            pl.BlockSpec((SUBLANES, LANES), lambda n: (0, 0))],
                 out_shape=[jax.ShapeDtypeStruct((s, aw), F32), jax.ShapeDtypeStruct((s, KV_WIDTH), F32),
                            jax.ShapeDtypeStruct((s, KV_WIDTH), F32), jax.ShapeDtypeStruct((SUBLANES, LANES), F32)],
                 scratch_shapes=[pltpu.VMEM((BLOCK, KV_WIDTH), F32), pltpu.VMEM((BLOCK, KV_WIDTH), F32)],
                 compiler_params=_params(("arbitrary",)))(qs, ks, ks, vb, vb, d_o, sinks)


def conv_fwd(z, conv_w, mix, aw, cw, b_off):
    s = z.shape[0]
    tm = _tile(s, 1024)
    nseg = cw // COLT
    hb = tm // SUBLANES

    def body(b_ref, c_ref, h_ref, g_ref, cp_ref, hp_ref, w_ref, mix_in, mix_ref):
        i = pl.program_id(0)
        u = c_ref[...] * h_ref[...]
        up = jnp.where(i > 0, cp_ref[...] * hp_ref[...], 0.0)
        ext = jnp.concatenate([up, u], axis=0)
        um1 = pltpu.roll(ext, 1, 0)[SUBLANES:]
        um2 = pltpu.roll(ext, 2, 0)[SUBLANES:]
        w = w_ref[...]
        cv = w[0:1] * um2 + w[1:2] * um1 + w[2:3] * u
        g = g_ref[...]
        mix_ref[...] = ((b_ref[...] * cv) * (g * _sigmoid(g))).astype(BF16)

    seg = lambda k: pl.BlockSpec((tm, COLT), functools.partial(lambda i, j, k: (i, b_off + k * nseg + j), k=k))
    halo = lambda k: pl.BlockSpec(
        (SUBLANES, COLT), functools.partial(lambda i, j, k: (jnp.maximum(i * hb - 1, 0), b_off + k * nseg + j), k=k))
    return _call(body, name="conv_fwd", grid=(s // tm, nseg),
                 in_specs=[seg(0), seg(1), seg(2), seg(3), halo(1), halo(2),
                           pl.BlockSpec((SUBLANES, COLT), lambda i, j: (0, j)), _hbm()],
                 out_specs=pl.BlockSpec((tm, COLT), lambda i, j: (i, aw // COLT + j)),
                 out_shape=jax.ShapeDtypeStruct(mix.shape, BF16),
                 input_output_aliases={7: 0},
                 compiler_params=_params(("parallel", "parallel")))(z, z, z, z, z, z, conv_w, mix)


def conv_bwd(z, d_mix, conv_w, dz, aw, cw, b_off):
    s = z.shape[0]
    tm = _tile(s, 512)
    nseg = cw // COLT
    hb = tm // SUBLANES
    n_row = s // tm
    last_h = s // SUBLANES - 1
    n_step = nseg * n_row

    def body(b_ref, c_ref, h_ref, g_ref, cp_ref, hp_ref, bn_ref, gn_ref, dm_ref, dmn_ref, w_ref, dz_in,
             dz_ref, acc_ref, stash_ref, sems):
        j = pl.program_id(0)
        i = pl.program_id(1)
        step = j * n_row + i
        slot = step % 2

        def copies(from_slot, at_step):
            jj, ii = at_step // n_row, at_step % n_row
            rows = pl.ds(pl.multiple_of(ii * tm, tm), tm)
            return [pltpu.make_async_copy(
                stash_ref.at[from_slot, q],
                dz_ref.at[rows, pl.ds(pl.multiple_of((b_off + q * nseg + jj) * COLT, COLT), COLT)],
                sems.at[from_slot, q]) for q in range(4)]

        @pl.when(step >= 2)
        def _():
            for cp in copies(slot, step - 2):
                cp.wait()

        cc, hh, bb, g = c_ref[...], h_ref[...], b_ref[...], g_ref[...]
        w = w_ref[...]
        u = cc * hh
        up = jnp.where(i > 0, cp_ref[...] * hp_ref[...], 0.0)
        ext = jnp.concatenate([up, u], axis=0)
        um1 = pltpu.roll(ext, 1, 0)[SUBLANES:]
        um2 = pltpu.roll(ext, 2, 0)[SUBLANES:]
        cv = w[0:1] * um2 + w[1:2] * um1 + w[2:3] * u
        sig = _sigmoid(g)
        sg = g * sig
        dm = dm_ref[...]
        d_cv = (dm * bb) * sg
        gn = gn_ref[...]
        d_cv_next = jnp.where(i < n_row - 1, (dmn_ref[...] * bn_ref[...]) * (gn * _sigmoid(gn)), 0.0)
        ext2 = jnp.concatenate([d_cv, d_cv_next], axis=0)
        dp1 = pltpu.roll(ext2, tm + SUBLANES - 1, 0)[:tm]
        dp2 = pltpu.roll(ext2, tm + SUBLANES - 2, 0)[:tm]
        d_u = w[2:3] * d_cv + w[1:2] * dp1 + w[0:1] * dp2
        stash_ref[slot, 0] = ((dm * cv) * sg).astype(BF16)
        stash_ref[slot, 1] = (d_u * hh).astype(BF16)
        stash_ref[slot, 2] = (d_u * cc).astype(BF16)
        stash_ref[slot, 3] = (((dm * bb) * cv) * (sig * (1.0 + g * (1.0 - sig)))).astype(BF16)
        for cp in copies(slot, step):
            cp.start()

        @pl.when(i == 0)
        def _():
            acc_ref[...] = jnp.zeros_like(acc_ref)

        acc_ref[0:1, :] += jnp.sum(d_cv * um2, axis=0, keepdims=True)
        acc_ref[1:2, :] += jnp.sum(d_cv * um1, axis=0, keepdims=True)
        acc_ref[2:3, :] += jnp.sum(d_cv * u, axis=0, keepdims=True)

        @pl.when(step == n_step - 1)
        def _():
            if n_step >= 2:
                for cp in copies(1 - slot, step - 1):
                    cp.wait()
            for cp in copies(slot, step):
                cp.wait()

    seg = lambda q: pl.BlockSpec((tm, COLT), functools.partial(lambda j, i, q: (i, b_off + q * nseg + j), q=q))
    halo_p = lambda q: pl.BlockSpec(
        (SUBLANES, COLT),
        functools.partial(lambda j, i, q: (jnp.maximum(i * hb - 1, 0), b_off + q * nseg + j), q=q))
    halo_n = lambda q: pl.BlockSpec(
        (SUBLANES, COLT),
        functools.partial(lambda j, i, q: (jnp.minimum((i + 1) * hb, last_h), b_off + q * nseg + j), q=q))
    return _call(body, name="conv_bwd", grid=(nseg, n_row),
                 in_specs=[seg(0), seg(1), seg(2), seg(3), halo_p(1), halo_p(2), halo_n(0), halo_n(3),
                           pl.BlockSpec((tm, COLT), lambda j, i: (i, aw // COLT + j)),
                           pl.BlockSpec((SUBLANES, COLT),
                                        lambda j, i: (jnp.minimum((i + 1) * hb, last_h), aw // COLT + j)),
                           pl.BlockSpec((SUBLANES, COLT), lambda j, i: (0, j)), _hbm()],
                 out_specs=[_hbm(), pl.BlockSpec((SUBLANES, COLT), lambda j, i: (0, j))],
                 out_shape=[jax.ShapeDtypeStruct(dz.shape, BF16), jax.ShapeDtypeStruct((SUBLANES, cw), F32)],
                 input_output_aliases={11: 0},
                 scratch_shapes=[pltpu.VMEM((2, 4, tm, COLT), BF16), pltpu.SemaphoreType.DMA((2, 4))],
                 compiler_params=_params(("arbitrary", "arbitrary")))(
                     z, z, z, z, z, z, z, z, d_mix, d_mix, conv_w, dz)


def _place():
    x, y, c = lax.axis_index("x"), lax.axis_index("y"), lax.axis_index("c")
    chips = [(1 - x, y), (x, 1 - y), (1 - x, 1 - y)]
    return x, y, c, chips


def _remote(src, dst, send_sem, recv_sem, device):
    return pltpu.make_async_remote_copy(src_ref=src, dst_ref=dst, send_sem=send_sem, recv_sem=recv_sem,
                                        device_id=device, device_id_type=MESH)


def plan_gather_ici(n_split):
    def plan(refs, send, recv):
        x, y, c, chips = _place()
        me = 2 * x + y
        mine, theirs = [], []
        for w, ref in enumerate(refs):
            for j, (cx, cy) in enumerate(chips):
                def part(slot):
                    if w >= n_split:
                        return ref.at[slot]
                    hr = ref.shape[1] // 2
                    return ref.at[slot, pl.ds(c * hr, hr)]
                k = 3 * w + j
                mine.append(_remote(part(me), part(me), send.at[k], recv.at[k], (cx, cy, c)))
                theirs.append(_remote(part(2 * cx + cy), part(2 * cx + cy), send.at[k], recv.at[k], (cx, cy, c)))
        return mine, theirs
    return plan


def plan_shard_ici(j):
    def plan(refs, send, recv):
        _, _, c, chips = _place()
        own, land = refs
        hr = own.shape[0] // 2
        rows = pl.ds(c * hr, hr)
        cp = _remote(own.at[rows], land.at[rows], send.at[0], recv.at[0], (*chips[j], c))
        return [cp], [cp]
    return plan


def plan_shard_relay(refs, send, recv):
    _, _, c, chips = _place()
    land_x, land_y, land_far = refs
    hr = land_far.shape[0] // 2
    quarter = lambda q: pl.ds(c * hr + q * (hr // 2), hr // 2)
    mine = [_remote(land_y.at[quarter(0)], land_far.at[quarter(0)], send.at[0], recv.at[0], (*chips[0], c)),
            _remote(land_x.at[quarter(1)], land_far.at[quarter(1)], send.at[1], recv.at[1], (*chips[1], c))]
    return mine, mine


def plan_shard_pass(refs, send, recv):
    x, y, c, _ = _place()
    mine, theirs = [], []
    for w, land in enumerate(refs):
        hr = land.shape[0] // 2
        half = lambda core: land.at[pl.ds(core * hr, hr)]
        mine.append(_remote(half(c), half(c), send.at[w], recv.at[w], (x, y, 1 - c)))
        theirs.append(_remote(half(1 - c), half(1 - c), send.at[w], recv.at[w], (x, y, 1 - c)))
    return mine, theirs


def plan_gather_pass(refs, send, recv):
    x, y, c, chips = _place()
    mine, theirs = [], []
    for w, ref in enumerate(refs):
        hr = ref.shape[1] // 2
        for j, (cx, cy) in enumerate(chips):
            half = lambda core: ref.at[2 * cx + cy, pl.ds(core * hr, hr)]
            k = 3 * w + j
            mine.append(_remote(half(c), half(c), send.at[k], recv.at[k], (x, y, 1 - c)))
            theirs.append(_remote(half(1 - c), half(1 - c), send.at[k], recv.at[k], (x, y, 1 - c)))
    return mine, theirs


def plan_swap(refs, send, recv):
    x, y, c, _ = _place()
    nw = len(refs) // 2
    mine = []
    for w in range(nw):
        hr = refs[w].shape[1] // 2
        mine.append(_remote(refs[w].at[:, pl.ds((1 - c) * hr, hr), :], refs[nw + w], send.at[w], recv.at[w],
                            (x, y, 1 - c)))
    return mine, mine


def plan_scatter(refs, send, recv):
    x, y, c, chips = _place()
    me = 2 * x + y
    nw = len(refs) // 2
    mine, theirs = [], []
    for w in range(nw):
        for j, (cx, cy) in enumerate(chips):
            k = 3 * w + j
            mine.append(_remote(refs[w].at[2 * cx + cy], refs[nw + w].at[me], send.at[k], recv.at[k], (cx, cy, c)))
            theirs.append(_remote(refs[w].at[me], refs[nw + w].at[2 * cx + cy], send.at[k], recv.at[k], (cx, cy, c)))
    return mine, theirs


def plan_join(refs, send, recv):
    x, y, c, _ = _place()
    mine, theirs = [], []
    for w, ref in enumerate(refs):
        hr = ref.shape[0] // 2
        half = lambda core: ref.at[pl.ds(core * hr, hr)]
        mine.append(_remote(half(c), half(c), send.at[w], recv.at[w], (x, y, 1 - c)))
        theirs.append(_remote(half(1 - c), half(1 - c), send.at[w], recv.at[w], (x, y, 1 - c)))
    return mine, theirs


def exchange(name, plan, arrays, n, after=()):
    na = len(arrays)

    def body(*refs):
        mine, theirs = plan(refs[:na], refs[2 * na], refs[2 * na + 1])
        for cp in mine:
            cp.start()
        for cp in theirs:
            cp.wait_recv()
        for cp in mine:
            cp.wait_send()

    return _call_after(body, after, name=name, in_specs=[_hbm()] * na, out_specs=[_hbm()] * na,
                       out_shape=[jax.ShapeDtypeStruct(a.shape, a.dtype) for a in arrays],
                       input_output_aliases={i: i for i in range(na)},
                       scratch_shapes=[pltpu.SemaphoreType.DMA((n,)), pltpu.SemaphoreType.DMA((n,))])(*arrays)


def exchange_start(name, groups, arrays, after=()):
    na, ng = len(arrays), len(groups)

    def body(*refs):
        for g, (plan, idx, _) in enumerate(groups):
            mine, _ = plan([refs[i] for i in idx], refs[na + 2 * g], refs[na + 2 * g + 1])
            for cp in mine:
                cp.start()
        refs[-1][...] = jnp.zeros_like(refs[-1])

    hbm = pl.BlockSpec(memory_space=pltpu.HBM)
    sem = pl.BlockSpec(memory_space=pltpu.SEMAPHORE)
    sem_types = [pltpu.SemaphoreType.DMA((n,)) for _, _, n in groups for _ in (0, 1)]
    outs = _call_after(body, after, name=name, in_specs=[hbm] * na,
                       out_specs=[sem] * (2 * ng) + [hbm] * na + [pl.BlockSpec(memory_space=pltpu.VMEM)],
                       out_shape=sem_types + [pltpu.HBM(a.shape, a.dtype) for a in arrays]
                       + [jax.ShapeDtypeStruct((SUBLANES, LANES), F32)],
                       input_output_aliases={i: i + 2 * ng for i in range(na)},
                       compiler_params=pltpu.CompilerParams(
                           has_side_effects=pltpu.SideEffectType.DATAFLOW_SIDE_EFFECTING))(
                               *[pltpu.with_memory_space_constraint(a, pltpu.HBM) for a in arrays])
    sems = [(outs[2 * g], outs[2 * g + 1]) for g in range(ng)]
    return sems, list(outs[2 * ng:-1]), outs[-1]


def exchange_wait(name, plan, sems, arrays, after):
    send_sems, recv_sems = sems
    na = len(arrays)

    def body(*refs):
        mine, theirs = plan(refs[:na], refs[na], refs[na + 1])
        for cp in mine:
            cp.wait_send()
        for cp in theirs:
            cp.wait_recv()

    hbm = pl.BlockSpec(memory_space=pltpu.HBM)
    sem = pl.BlockSpec(memory_space=pltpu.SEMAPHORE)
    return _call(body, name=name, in_specs=[hbm] * na + [sem, sem, _hbm()], out_specs=[hbm] * na,
                 out_shape=[pltpu.HBM(a.shape, a.dtype) for a in arrays],
                 input_output_aliases={i: i for i in range(na)},
                 compiler_params=pltpu.CompilerParams(
                     has_side_effects=pltpu.SideEffectType.DATAFLOW_SIDE_EFFECTING))(
                         *arrays, send_sems, recv_sems, after)


def plan_allgather_small(refs, send, recv):
    x, y, c, _ = _place()
    buf, = refs
    mine, theirs = [], []
    flips = [(fx, fy, fc) for fx in (0, 1) for fy in (0, 1) for fc in (0, 1)][1:]
    for k, (fx, fy, fc) in enumerate(flips):
        peer = (x ^ fx, y ^ fy, c ^ fc)
        slot = lambda px, py, pc: buf.at[4 * px + 2 * py + pc]
        mine.append(_remote(slot(x, y, c), slot(x, y, c), send.at[k], recv.at[k], peer))
        theirs.append(_remote(slot(*peer), slot(*peer), send.at[k], recv.at[k], peer))
    return mine, theirs


def add_sibling(grad, got, core, name):
    _, r, c = grad.shape
    hr = r // 2
    tr = _tile(hr, 512)
    nblk = hr // tr

    def body(core_ref, g_ref, o_ref, out_ref):
        out_ref[...] = (g_ref[...].astype(F32) + o_ref[...].astype(F32)).astype(BF16)

    grid_spec = pltpu.PrefetchScalarGridSpec(
        num_scalar_prefetch=1, grid=(N_CHIPS, nblk),
        in_specs=[pl.BlockSpec((None, tr, c), lambda t, i, core_ref: (t, core_ref[0] * nblk + i, 0)),
                  pl.BlockSpec((None, tr, c), lambda t, i, core_ref: (t, i, 0))],
        out_specs=pl.BlockSpec((None, tr, c), lambda t, i, core_ref: (t, i, 0)))
    return _call(body, name=name, grid_spec=grid_spec,
                 out_shape=jax.ShapeDtypeStruct((N_CHIPS, hr, c), BF16),
                 compiler_params=_params(("parallel", "parallel")))(core, grad, got)


def sum_chips(mine, owned, place, name):
    _, hr, c = mine.shape
    tr = _tile(hr, 512)
    nblk = hr // tr

    def body(place_ref, m_ref, o1_ref, o2_ref, o3_ref, out_ref):
        acc = m_ref[...].astype(F32)
        for o_ref in (o1_ref, o2_ref, o3_ref):
            acc = acc + o_ref[...].astype(F32)
        out_ref[...] = acc

    other = lambda k: pl.BlockSpec((None, tr, c), lambda i, place_ref: ((place_ref[0] + k) % N_CHIPS, i, 0))
    grid_spec = pltpu.PrefetchScalarGridSpec(
        num_scalar_prefetch=1, grid=(nblk,),
        in_specs=[other(0), other(1), other(2), other(3)],
        out_specs=pl.BlockSpec((tr, c), lambda i, place_ref: (place_ref[1] * nblk + i, 0)))
    return _call(body, name=name, grid_spec=grid_spec,
                 out_shape=jax.ShapeDtypeStruct((2 * hr, c), F32),
                 compiler_params=_params(("parallel",)))(place, mine, owned, owned, owned)


def sum_devices(gathered):
    _, rows, width = gathered.shape

    def body(g_ref, out_ref):
        acc = g_ref[0]
        for dev in range(1, 8):
            acc = acc + g_ref[dev]
        out_ref[...] = acc
        tail = acc[SUBLANES:]
        out_ref[SUBLANES:, :] = jnp.broadcast_to(jnp.sum(tail, axis=1, keepdims=True), tail.shape)

    return _call(body, name="sum_devices",
                 in_specs=[pl.BlockSpec(memory_space=pltpu.VMEM)],
                 out_specs=pl.BlockSpec(memory_space=pltpu.VMEM),
                 out_shape=jax.ShapeDtypeStruct((rows, width), F32))(gathered)


def _rope_tables(s):
    half = ROT_DIM // 2
    inv_freq = jnp.power(jnp.float32(ROPE_THETA), -jnp.arange(half, dtype=F32) * 2.0 / ROT_DIM)
    freq64 = jnp.concatenate([inv_freq, inv_freq, jnp.zeros((HEAD_DIM - ROT_DIM,), F32)])
    freq = jnp.concatenate([freq64, freq64])[None, :]
    dim = (jnp.arange(LANES) % HEAD_DIM)[None, :]
    ang = jnp.arange(s).astype(F32)[:, None] * freq
    cos, sin = jnp.cos(ang), jnp.sin(ang)
    return cos, jnp.where(dim < half, -sin, 0.0), jnp.where((dim >= half) & (dim < ROT_DIM), sin, 0.0)


def _pad_rows(a, rows):
    return jnp.pad(a, ((0, rows - a.shape[0]), (0, 0)))


def _pad_cols(a, cols):
    return jnp.pad(a, ((0, 0), (0, cols - a.shape[1])))


def kernel(x, p, norm_gain, w_in, q_norm_gain, k_norm_gain, attn_sinks, conv_w, w_out, ple_gate_norm_gain, w_ple_gate, b_ple_gate, w_ple_proj, ple_norm_gain, loss_target, m_norm_gain, m_w_in, m_q_norm_gain, m_k_norm_gain, m_attn_sinks, m_conv_w, m_w_out, m_ple_gate_norm_gain, m_w_ple_gate, m_b_ple_gate, m_w_ple_proj, m_ple_norm_gain, v_norm_gain, v_w_in, v_q_norm_gain, v_k_norm_gain, v_attn_sinks, v_conv_w, v_w_out, v_ple_gate_norm_gain, v_w_ple_gate, v_b_ple_gate, v_w_ple_proj, v_ple_norm_gain):
    x2, p2, tgt = x[0], p[0, 0], loss_target[0]
    s, d = x2.shape
    ple = p2.shape[1]
    aw = d // 2
    cw = d - aw
    nq = aw // HEAD_DIM
    sh = w_in.shape[2]
    in_w = N_CHIPS * sh
    dq = d // N_CHIPS
    cq = cw // N_CHIPS
    ga_off = (aw + 2 * KV_WIDTH) // COLT
    b_off = (2 * aw + 2 * KV_WIDTH) // COLT
    assert in_w == 2 * aw + 2 * KV_WIDTH + 4 * cw and aw % COLT == 0 and cw % COLT == 0
    assert s % BLOCK == 0 and sh % LANES == 0 and nq % (2 * N_KV_HEADS) == 0

    core = lax.axis_index("c").astype(jnp.int32).reshape(1)
    chip = 2 * lax.axis_index("x") + lax.axis_index("y")

    chip1 = chip.astype(jnp.int32).reshape(1)
    place = jnp.concatenate([chip1, core])
    w_in_own = cast_bf16(w_in[0], "cast_w_in")
    rest = [cast_into_slot(w_out[0], chip1, "cast_w_out"), cast_into_slot(w_ple_gate[0], chip1, "cast_w_pg"),
            cast_into_slot(w_ple_proj[0], chip1, "cast_w_pp"),
            lax.dynamic_update_slice(jnp.zeros((N_CHIPS, SUBLANES, cq), F32),
                                     _pad_rows(conv_w[0], SUBLANES)[None], (chip, 0, 0))]
    order = jnp.stack([chip, chip ^ 2, chip ^ 1, chip ^ 3]).astype(jnp.int32)
    wi_sems, wi_arrs, wi_token = exchange_start(
        "gather_wi_start", [(plan_shard_ici(j), [0, 1 + j], 1) for j in range(2)],
        [w_in_own] + [lax.empty((d, sh), BF16) for _ in range(3)])

    tm = _tile(s, 1024)
    tn = _tile(d, 1024)
    tk = _tile(d, 2048)
    ts = _tile(s, 2048)
    h = rms_fwd(x2, norm_gain, "rms_fwd_x", after=(wi_token,))
    tabs = _rope_tables(s)
    qg = jnp.tile(q_norm_gain, (1, LANES // HEAD_DIM))
    kg = jnp.tile(k_norm_gain, (1, LANES // HEAD_DIM))
    seg_i = jnp.arange(SEG) // HEAD_DIM
    segb = (seg_i[:, None] == seg_i[None, :]).astype(BF16)
    z = in_proj_part(h, wi_arrs[0], None, order, 0, in_w)
    own, land_x = exchange_wait("gather_wi_wait0", plan_shard_ici(0), wi_sems[0], [wi_arrs[0], wi_arrs[1]], z)
    own, land_y = exchange_wait("gather_wi_wait1", plan_shard_ici(1), wi_sems[1], [own, wi_arrs[2]], land_x)
    relay_sems, relayed, relay_token = exchange_start(
        "gather_wi_relay_start", [(plan_shard_relay, [0, 1, 2], 2)], [land_x, land_y, wi_arrs[3]])
    rest_sems, rest, rest_token = exchange_start(
        "gather_rest_start", [(plan_gather_ici(3), [0, 1, 2, 3], 12)], rest, after=(relay_token,))
    land_x, land_y = exchange("gather_wi_pass01", plan_shard_pass, relayed[:2], 2, after=(rest_token,))
    z = in_proj_part(h, land_x, z, order, 1, in_w)
    z = in_proj_part(h, land_y, z, order, 2, in_w)
    land_x, land_y, land_far = exchange_wait("gather_wi_relay_wait", plan_shard_relay, relay_sems[0],
                                             [land_x, land_y, relayed[2]], z)
    land_far, = exchange("gather_wi_pass2", plan_shard_pass, [land_far], 1)
    z = in_proj_part(h, land_far, z, order, 3, in_w)
    w_shards = [own, land_x, land_y, land_far]
    wo_all, wg_all, wp_all, conv_all = exchange_wait("gather_rest_wait", plan_gather_ici(3), rest_sems[0], rest, z)
    pass_sems, passed, pass_token = exchange_start(
        "gather_rest_pass_start", [(plan_gather_pass, [0, 1, 2], 9)], [wo_all, wg_all, wp_all])
    conv_full = conv_all.transpose(1, 0, 2).reshape(SUBLANES, cw)
    qs, ks, vb = qk_prep_fwd(z, tabs, qg, kg, segb, aw, after=(pass_token,))
    attn, mix = attn_fwd(qs, ks, vb, z, attn_sinks, aw, d, ga_off)
    mix = conv_fwd(z, conv_full, mix, aw, cw, b_off)
    wo_all, wg_all, wp_all = exchange_wait("gather_rest_pass_wait", plan_gather_pass, pass_sems[0], passed, mix)
    wo_full = wo_all.reshape(d, d)
    wg_full = wg_all.reshape(d, d)
    sq = lambda shape: dict(
        a_spec=pl.BlockSpec((tm, tk), lambda i, j, k: (i, k)),
        o_spec=pl.BlockSpec((tm, tn), lambda i, j, k: (i, j)),
        out_shape=jax.ShapeDtypeStruct(shape, F32))
    x1, hg = out_proj_norm(mix, wo_full, x2, ple_gate_norm_gain)
    pq = d // N_CHIPS

    d_gl, d_pe, dy, acc_head = head_fwd_bwd(x1, hg, wg_full, p2, wp_all, tgt, b_ple_gate, ple_norm_gain)
    wgrad = lambda a_cols, shape3, o_spec, b_cols, name, a, b, grid: matmul(
        a, b, grid=grid,
        a_spec=pl.BlockSpec((ts, a_cols), lambda i, j, k: (k, i)),
        b_spec=pl.BlockSpec((ts, b_cols), lambda i, j, k: (k, j)),
        o_spec=o_spec, out_shape=jax.ShapeDtypeStruct(shape3, BF16), dims=TN, name=name)
    g_wg = wgrad(tn, (d, d), pl.BlockSpec((tn, tn), lambda i, j, k: (i, j)), tn, "mm_g_wg", hg, d_gl,
                 (d // tn, d // tn, s // ts))
    g_wp = wgrad(ple, (N_CHIPS, ple, pq), pl.BlockSpec((None, ple, pq), lambda i, j, k: (j, 0, 0)), pq,
                 "mm_g_wp", p2, d_pe, (1, N_CHIPS, s // ts))
    d_x1, d_x1b, acc_g = gate_proj_bwd_norm(d_gl, wg_full, x1, dy, ple_gate_norm_gain)
    d_mix = matmul(d_x1b, wo_full, grid=(s // tm, d // tn, d // tk),
                   b_spec=pl.BlockSpec((tn, tk), lambda i, j, k: (j, k)), dims=NT, name="mm_d_mix", **sq((s, d)))
    g_wo = wgrad(tn, (d, d), pl.BlockSpec((tn, tn), lambda i, j, k: (i, j)), tn, "mm_g_wo", mix, d_x1b,
                 (d // tn, d // tn, s // ts))
    def reduce_start(tag, grads):
        n = len(grads)
        lands = [lax.empty((N_CHIPS, a.shape[1] // 2, a.shape[2]), BF16) for a in grads]
        swapped = exchange("swap_" + tag, plan_swap, list(grads) + lands, n)
        parts = [add_sibling(g, o, core, "add_sibling_%s%d" % (tag, i))
                 for i, (g, o) in enumerate(zip(swapped[:n], swapped[n:]))]
        return exchange_start("scatter_%s_start" % tag, [(plan_scatter, list(range(2 * n)), 3 * n)],
                              parts + [lax.empty(a.shape, BF16) for a in parts])

    def reduce_sum(tag, handle, after):
        sems, arrays, _ = handle
        n = len(arrays) // 2
        got = exchange_wait("scatter_%s_wait" % tag, plan_scatter, sems[0], arrays, after)
        return [sum_chips(pt, o, place, "sum_chips_%s%d" % (tag, i))
                for i, (pt, o) in enumerate(zip(got[:n], got[n:]))]

    early = reduce_start("early", [g_wo.reshape(N_CHIPS, dq, d), g_wg.reshape(N_CHIPS, dq, d), g_wp])
    d_o, dz = gate_bwd(d_mix, attn, z, aw, ga_off, after=(early[-1],))
    dqs, dks, dvs, acc_sink = attn_bwd(qs, ks, vb, d_o, attn_sinks, aw)
    dz, acc_qk = qk_prep_bwd(z, dqs, dks, dvs, tabs, qg, kg, segb, dz, aw)
    dz, acc_conv = conv_bwd(z, d_mix, conv_full, dz, aw, cw, b_off)
    join_sems, joining, join_token = exchange_start(
        "join_early_start", [(plan_join, [0, 1, 2], 3)], reduce_sum("early", early, dz))
    g_wi = matmul(h, dz, grid=(d // tn, N_CHIPS, 1),
                  a_spec=pl.BlockSpec((s, tn), lambda i, j, k: (0, i), pipeline_mode=pl.Buffered(1)),
                  b_spec=pl.BlockSpec((s, sh), lambda i, j, k: (0, j)),
                  o_spec=pl.BlockSpec((None, tn, sh), lambda i, j, k: (j, i, 0)),
                  out_shape=jax.ShapeDtypeStruct((N_CHIPS, d, sh), BF16), dims=TN, name="mm_g_wi",
                  after=(join_token,), vmem=VMEM_LIMIT_BIG)
    full_wo, full_wg, full_wp = exchange_wait("join_early_wait", plan_join, join_sems[0], joining, g_wi)
    late = reduce_start("late", [g_wi])
    d_h = in_proj_bwd(dz, w_shards, order, d, after=(late[-1],))
    grad_x, acc_x = rms_bwd_add(d_h, x2, d_x1, norm_gain, "rms_bwd_x", False)

    wsm = max(d, cw)
    misc = jnp.concatenate([acc_qk[0:1, :HEAD_DIM], acc_qk[1:2, :HEAD_DIM], acc_sink[0:1, :nq]], axis=1)
    small = jnp.concatenate([
        _pad_cols(acc_x[0:1], wsm), _pad_cols(acc_g[0:1], wsm), _pad_cols(acc_head[0:1], wsm),
        _pad_cols(acc_head[1:2], wsm), _pad_cols(misc, wsm), _pad_cols(acc_conv[0:3], wsm),
        _pad_rows(_pad_cols(acc_head[2:3], wsm), SUBLANES)], axis=0)
    device = 2 * chip + lax.axis_index("c")
    small_sems, small_bufs, small_token = exchange_start(
        "small_start", [(plan_allgather_small, [0], 7)],
        [lax.dynamic_update_slice(jnp.zeros((8,) + small.shape, F32), small[None], (device, 0, 0))])
    last_sems, last_halves, last_token = exchange_start(
        "join_late_start", [(plan_join, [0], 1)], reduce_sum("late", late, small_token))
    big, after = {}, (last_token,)
    for nm, g, w, m, v in (("w_out", full_wo, w_out, m_w_out, v_w_out),
                           ("w_ple_gate", full_wg, w_ple_gate, m_w_ple_gate, v_w_ple_gate),
                           ("w_ple_proj", full_wp, w_ple_proj, m_w_ple_proj, v_w_ple_proj)):
        stepped = adamw(g, w[0], m[0], v[0], "adamw_" + nm, after=after)
        big[nm], after = [o[None] for o in stepped], (stepped[1],)
    full_wi, = exchange_wait("join_late_wait", plan_join, last_sems[0], last_halves, after[0])
    big["w_in"] = [o[None] for o in adamw(full_wi, w_in[0], m_w_in[0], v_w_in[0], "adamw_w_in")]

    gathered, = exchange_wait("small_wait", plan_allgather_small, small_sems[0], small_bufs, big["w_in"][1])
    tot = sum_devices(gathered)
    loss = tot[SUBLANES, 0]

    def pack(vals):
        ng, pg, bg, eg, qgv, kgv, sk, cv = vals
        misc_v = jnp.concatenate([qgv, kgv, sk], axis=1)
        return jnp.concatenate([_pad_cols(ng, wsm), _pad_cols(pg, wsm), _pad_cols(bg, wsm), _pad_cols(eg, wsm),
                                _pad_cols(misc_v, wsm), _pad_cols(cv[0], wsm)], axis=0)

    g_small = jnp.concatenate(
        [tot[0:5], _pad_cols(lax.dynamic_slice(tot[5:8], (0, chip * cq), (3, cq)), wsm)], axis=0)
    w_small = pack((norm_gain, ple_gate_norm_gain, b_ple_gate, ple_norm_gain, q_norm_gain, k_norm_gain,
                    attn_sinks, conv_w))
    m_small = pack((m_norm_gain, m_ple_gate_norm_gain, m_b_ple_gate, m_ple_norm_gain, m_q_norm_gain,
                    m_k_norm_gain, m_attn_sinks, m_conv_w))
    v_small = pack((v_norm_gain, v_ple_gate_norm_gain, v_b_ple_gate, v_ple_norm_gain, v_q_norm_gain,
                    v_k_norm_gain, v_attn_sinks, v_conv_w))
    sm = adamw(g_small, w_small, m_small, v_small, "adamw_small")

    def unpack(a):
        return {"norm_gain": a[0:1, :d], "ple_gate_norm_gain": a[1:2, :d], "b_ple_gate": a[2:3, :d],
                "ple_norm_gain": a[3:4, :d], "q_norm_gain": a[4:5, :HEAD_DIM],
                "k_norm_gain": a[4:5, HEAD_DIM:2 * HEAD_DIM],
                "attn_sinks": a[4:5, 2 * HEAD_DIM:2 * HEAD_DIM + nq], "conv_w": a[5:8, :cq][None]}

    order = ("norm_gain", "w_in", "q_norm_gain", "k_norm_gain", "attn_sinks", "conv_w", "w_out",
             "ple_gate_norm_gain", "w_ple_gate", "b_ple_gate", "w_ple_proj", "ple_norm_gain")
    outs = [loss, grad_x[None]]
    for kind in range(4):
        table = unpack(sm[kind])
        for nm in order:
            outs.append(big[nm][kind] if nm in big else table[nm])
    return tuple(outs)
```

```python
import functools

import jax
import jax.numpy as jnp
from jax import lax
from jax.experimental import pallas as pl
from jax.experimental.pallas import tpu as pltpu

F32 = jnp.float32
BF16 = jnp.bfloat16
MESH = pl.DeviceIdType.MESH

HEAD_DIM = 64
N_KV_HEADS = 4
KV_WIDTH = N_KV_HEADS * HEAD_DIM
BLOCK = 128
ROT_DIM = 16
ROPE_THETA = 500000.0
EPS = 1e-6
NEG_INF = -1e30
N_CHIPS = 4
LANES = 128
SUBLANES = 8
COLT = 512
SEG = 256
VMEM_LIMIT = 48 * 1024 * 1024
VMEM_LIMIT_BIG = 60 * 1024 * 1024

ADAM_LR = 0.001
ADAM_B1 = 0.9
ADAM_B2 = 0.999
ADAM_EPS = 1e-08
ADAM_WD = 0.01
ADAM_STEP = 10

NN = (((1,), (0,)), ((), ()))
NT = (((1,), (1,)), ((), ()))
TN = (((0,), (0,)), ((), ()))


def _call(body, **kw):
    return pl.pallas_call(body, **kw)


def _call_after(body, after, **kw):
    n_in, n_after = len(kw["in_specs"]), len(after)
    kw["in_specs"] = list(kw["in_specs"]) + [pl.BlockSpec(memory_space=pl.ANY)] * n_after

    def body_after(*refs):
        body(*refs[:n_in], *refs[n_in + n_after:])

    call = _call(body_after, **kw)
    return lambda *args: call(*args, *after)


def _params(sem, vmem=VMEM_LIMIT):
    return pltpu.CompilerParams(dimension_semantics=sem, vmem_limit_bytes=vmem)


def _tile(n, pref):
    return pref if n % pref == 0 else n


def _sigmoid(v):
    return 1.0 / (1.0 + jnp.exp(-v))


def _hbm():
    return pl.BlockSpec(memory_space=pl.ANY)


def cast_bf16(a, name):
    r, c = a.shape
    tr = _tile(r, 512)

    def body(a_ref, o_ref):
        o_ref[...] = a_ref[...].astype(BF16)

    return _call(body, name=name, grid=(r // tr,),
                 in_specs=[pl.BlockSpec((tr, c), lambda i: (i, 0))],
                 out_specs=pl.BlockSpec((tr, c), lambda i: (i, 0)),
                 out_shape=jax.ShapeDtypeStruct((r, c), BF16),
                 compiler_params=_params(("parallel",)))(a)


def cast_into_slot(a, chip, name):
    r, c = a.shape
    tr = _tile(r, 512)

    def body(chip_ref, a_ref, o_ref):
        o_ref[...] = a_ref[...].astype(BF16)

    grid_spec = pltpu.PrefetchScalarGridSpec(
        num_scalar_prefetch=1, grid=(r // tr,),
        in_specs=[pl.BlockSpec((tr, c), lambda i, chip_ref: (i, 0))],
        out_specs=pl.BlockSpec((None, tr, c), lambda i, chip_ref: (chip_ref[0], i, 0)))
    return _call(body, name=name, grid_spec=grid_spec,
                 out_shape=jax.ShapeDtypeStruct((N_CHIPS, r, c), BF16),
                 compiler_params=_params(("parallel",)))(chip, a)


def out_proj_norm(mix, wo, x, gain):
    s, d = x.shape
    tm = _tile(s, 512)

    def body(m_ref, w_ref, x_ref, g_ref, x1_ref, hg_ref):
        x1 = x_ref[...] + jnp.dot(m_ref[...], w_ref[...], preferred_element_type=F32)
        x1_ref[...] = x1
        r = lax.rsqrt(jnp.mean(x1 * x1, axis=-1, keepdims=True) + EPS)
        hg_ref[...] = ((x1 * r) * g_ref[...]).astype(BF16)

    row = pl.BlockSpec((tm, d), lambda i: (i, 0))
    return _call(body, name="out_proj_norm", grid=(s // tm,),
                 in_specs=[row, pl.BlockSpec((d, d), lambda i: (0, 0), pipeline_mode=pl.Buffered(1)), row,
                           pl.BlockSpec((1, d), lambda i: (0, 0))],
                 out_specs=[row, row],
                 out_shape=[jax.ShapeDtypeStruct((s, d), F32), jax.ShapeDtypeStruct((s, d), BF16)],
                 compiler_params=_params(("parallel",), VMEM_LIMIT_BIG))(mix, wo, x, gain)


def gate_proj_bwd_norm(d_gl, wg, xin, add, gain):
    s, d = xin.shape
    tm = _tile(s, 256)

    def body(dg_ref, w_ref, x_ref, a_ref, g_ref, dx_ref, dxb_ref, acc_ref):
        i = pl.program_id(0)
        dyv = lax.dot_general(dg_ref[...], w_ref[...], NT, preferred_element_type=F32)
        xf = x_ref[...]
        r = lax.rsqrt(jnp.mean(xf * xf, axis=-1, keepdims=True) + EPS)
        xhat = xf * r
        gd = dyv * g_ref[...]
        dx = a_ref[...] + r * (gd - xhat * jnp.mean(xhat * gd, axis=-1, keepdims=True))
        dx_ref[...] = dx
        dxb_ref[...] = dx.astype(BF16)

        @pl.when(i == 0)
        def _():
            acc_ref[...] = jnp.zeros_like(acc_ref)

        acc_ref[0:1, :] += jnp.sum(dyv * xhat, axis=0, keepdims=True)

    row = pl.BlockSpec((tm, d), lambda i: (i, 0))
    return _call(body, name="gate_proj_bwd_norm", grid=(s // tm,),
                 in_specs=[row, pl.BlockSpec((d, d), lambda i: (0, 0), pipeline_mode=pl.Buffered(1)), row, row,
                           pl.BlockSpec((1, d), lambda i: (0, 0))],
                 out_specs=[row, row, pl.BlockSpec((SUBLANES, d), lambda i: (0, 0))],
                 out_shape=[jax.ShapeDtypeStruct((s, d), F32), jax.ShapeDtypeStruct((s, d), BF16),
                            jax.ShapeDtypeStruct((SUBLANES, d), F32)],
                 compiler_params=_params(("arbitrary",), VMEM_LIMIT_BIG))(d_gl, wg, xin, add, gain)


def head_fwd_bwd(x1, hg, wg, p, wp, tgt, bias, ple_gain):
    s, d = x1.shape
    tm = _tile(s, 256)

    def body(x1_ref, hg_ref, wg_ref, p_ref, wp_ref, t_ref, b_ref, g_ref, dgl_ref, dpe_ref, dy_ref, acc_ref):
        i = pl.program_id(0)
        gl = jnp.dot(hg_ref[...], wg_ref[...], preferred_element_type=F32)
        pb = p_ref[...].astype(BF16)
        pev = jnp.concatenate([jnp.dot(pb, wp_ref[q], preferred_element_type=F32) for q in range(N_CHIPS)],
                              axis=1)
        r = lax.rsqrt(jnp.mean(pev * pev, axis=-1, keepdims=True) + EPS)
        pehat = pev * r
        gain = g_ref[...]
        e = pehat * gain
        gate = _sigmoid(gl + b_ref[...])
        diff = (x1_ref[...] + gate * e) - t_ref[...]
        dy = diff * (1.0 / d)
        dy_ref[...] = dy
        d_gl = (dy * e) * (gate * (1.0 - gate))
        dgl_ref[...] = d_gl.astype(BF16)
        d_e = dy * gate
        gd = d_e * gain
        d_pe = r * (gd - pehat * jnp.mean(pehat * gd, axis=-1, keepdims=True))
        dpe_ref[...] = d_pe.astype(BF16)

        @pl.when(i == 0)
        def _():
            acc_ref[...] = jnp.zeros_like(acc_ref)

        acc_ref[0:1, :] += jnp.sum(d_gl, axis=0, keepdims=True)
        acc_ref[1:2, :] += jnp.sum(d_e * pehat, axis=0, keepdims=True)
        acc_ref[2:3, :] += jnp.sum(diff * diff, axis=0, keepdims=True) * (0.5 / d)

    row = pl.BlockSpec((tm, d), lambda i: (i, 0))
    vec = pl.BlockSpec((1, d), lambda i: (0, 0))
    return _call(body, name="head_fwd_bwd", grid=(s // tm,),
                 in_specs=[row, row, pl.BlockSpec((d, d), lambda i: (0, 0), pipeline_mode=pl.Buffered(1)),
                           pl.BlockSpec((tm, p.shape[1]), lambda i: (i, 0)),
                           pl.BlockSpec(wp.shape, lambda i: (0, 0, 0)), row, vec, vec],
                 out_specs=[row, row, row, pl.BlockSpec((SUBLANES, d), lambda i: (0, 0))],
                 out_shape=[jax.ShapeDtypeStruct((s, d), BF16), jax.ShapeDtypeStruct((s, d), BF16),
                            jax.ShapeDtypeStruct((s, d), F32), jax.ShapeDtypeStruct((SUBLANES, d), F32)],
                 compiler_params=_params(("arbitrary",), VMEM_LIMIT_BIG))(
                     x1, hg, wg, p, wp, tgt, bias, ple_gain)


def adamw(g, w, m, v, name, after=()):
    r, c = g.shape
    tr = _tile(r, 256)

    def body(g_ref, w_ref, m_ref, v_ref, go_ref, d_ref, mo_ref, vo_ref):
        gv = g_ref[...]
        mn = ADAM_B1 * m_ref[...] + (1.0 - ADAM_B1) * gv
        vn = ADAM_B2 * v_ref[...] + (1.0 - ADAM_B2) * (gv * gv)
        m_hat = mn / (1.0 - ADAM_B1 ** ADAM_STEP)
        v_hat = vn / (1.0 - ADAM_B2 ** ADAM_STEP)
        go_ref[...] = gv
        d_ref[...] = -ADAM_LR * (m_hat / (jnp.sqrt(v_hat) + ADAM_EPS) + ADAM_WD * w_ref[...])
        mo_ref[...] = mn
        vo_ref[...] = vn

    blk = pl.BlockSpec((tr, c), lambda i: (i, 0))
    shp = jax.ShapeDtypeStruct((r, c), F32)
    return _call_after(body, after, name=name, grid=(r // tr,), in_specs=[blk] * 4, out_specs=[blk] * 4,
                       out_shape=[shp] * 4, compiler_params=_params(("parallel",)))(g, w, m, v)


def matmul(a, b, *, grid, a_spec, b_spec, o_spec, out_shape, dims, name, res=None, res_spec=None, after=(),
           vmem=VMEM_LIMIT):
    nk = grid[2]
    acc_shape = tuple(d for d in o_spec.block_shape if d is not None)

    def body(*refs):
        a_ref, b_ref = refs[:2]
        r_ref = refs[2] if res is not None else None
        o_ref = refs[3] if res is not None else refs[2]
        part = lax.dot_general(a_ref[...].astype(BF16), b_ref[...].astype(BF16), dims, preferred_element_type=F32)

        def finish(out):
            if res is not None:
                out = r_ref[...] + out
            o_ref[...] = out.astype(o_ref.dtype)

        if nk == 1:
            finish(part)
            return
        acc_ref = refs[-1]
        k = pl.program_id(2)

        @pl.when(k == 0)
        def _():
            acc_ref[...] = part

        @pl.when((k > 0) & (k < nk - 1))
        def _():
            acc_ref[...] += part

        @pl.when(k == nk - 1)
        def _():
            finish(acc_ref[...] + part)

    in_specs = [a_spec, b_spec] + ([res_spec] if res is not None else [])
    args = (a, b) + ((res,) if res is not None else ())
    return _call_after(body, after, name=name, grid=grid, in_specs=in_specs, out_specs=o_spec,
                       out_shape=out_shape, scratch_shapes=[pltpu.VMEM(acc_shape, F32)] if nk > 1 else [],
                       compiler_params=_params(("parallel", "parallel", "arbitrary"), vmem))(*args)


def norm_in_proj(x, gain, w, order, in_w):
    s, d = x.shape
    sh = w.shape[1]
    tm = _tile(s, 512)

    def body(order_ref, x_ref, g_ref, w_ref, h_ref, z_ref):
        xf = x_ref[...]
        r = lax.rsqrt(jnp.mean(xf * xf, axis=-1, keepdims=True) + EPS)
        hb = ((xf * r) * g_ref[...]).astype(BF16)
        h_ref[...] = hb
        z_ref[...] = jnp.dot(hb, w_ref[...], preferred_element_type=F32)

    grid_spec = pltpu.PrefetchScalarGridSpec(
        num_scalar_prefetch=1, grid=(s // tm,),
        in_specs=[pl.BlockSpec((tm, d), lambda i, order_ref: (i, 0)),
                  pl.BlockSpec((1, d), lambda i, order_ref: (0, 0)),
                  pl.BlockSpec((d, sh), lambda i, order_ref: (0, 0), pipeline_mode=pl.Buffered(1))],
        out_specs=[pl.BlockSpec((tm, d), lambda i, order_ref: (i, 0)),
                   pl.BlockSpec((tm, sh), lambda i, order_ref: (i, order_ref[0]))])
    return _call(body, name="norm_in_proj", grid_spec=grid_spec,
                 out_shape=[jax.ShapeDtypeStruct((s, d), BF16), jax.ShapeDtypeStruct((s, in_w), F32)],
                 compiler_params=_params(("parallel",)))(order, x, gain, w)


def in_proj_part(h, w, z_prev, order, q, in_w):
    s, d = h.shape
    sh = w.shape[1]
    tm = _tile(s, 512)

    def body(order_ref, h_ref, w_ref, *rest):
        rest[-1][...] = jnp.dot(h_ref[...], w_ref[...], preferred_element_type=F32)

    in_specs = [pl.BlockSpec((tm, d), lambda i, order_ref: (i, 0)),
                pl.BlockSpec((d, sh), lambda i, order_ref: (0, 0))]
    args = [order, h, w]
    if z_prev is not None:
        in_specs.append(_hbm())
        args.append(z_prev)
    grid_spec = pltpu.PrefetchScalarGridSpec(
        num_scalar_prefetch=1, grid=(s // tm,), in_specs=in_specs,
        out_specs=pl.BlockSpec((tm, sh), lambda i, order_ref: (i, order_ref[q])))
    return _call(body, name="mm_z%d" % q, grid_spec=grid_spec,
                 out_shape=jax.ShapeDtypeStruct((s, in_w), F32),
                 input_output_aliases={3: 0} if z_prev is not None else {},
                 compiler_params=_params(("parallel",)))(*args)


def in_proj_bwd_norm(dz, ws, order, xin, add, gain, after):
    s, d = xin.shape
    sh = ws[0].shape[1]
    tm = _tile(s, 256)
    nq, n_after = len(ws), len(after)

    def body(order_ref, *refs):
        a_refs, b_refs = refs[:nq], refs[nq:2 * nq]
        x_ref, add_ref, g_ref = refs[2 * nq:2 * nq + 3]
        dx_ref, acc_ref = refs[2 * nq + 3 + n_after:]
        i = pl.program_id(0)
        dyv = lax.dot_general(a_refs[0][...], b_refs[0][...], NT, preferred_element_type=F32)
        for q in range(1, nq):
            dyv += lax.dot_general(a_refs[q][...], b_refs[q][...], NT, preferred_element_type=F32)
        xf = x_ref[...]
        r = lax.rsqrt(jnp.mean(xf * xf, axis=-1, keepdims=True) + EPS)
        xhat = xf * r
        gd = dyv * g_ref[...]
        dx_ref[...] = add_ref[...] + r * (gd - xhat * jnp.mean(xhat * gd, axis=-1, keepdims=True))

        @pl.when(i == 0)
        def _():
            acc_ref[...] = jnp.zeros_like(acc_ref)

        acc_ref[0:1, :] += jnp.sum(dyv * xhat, axis=0, keepdims=True)

    a_spec = lambda q: pl.BlockSpec((tm, sh), functools.partial(lambda i, order_ref, q: (i, order_ref[q]), q=q))
    row = pl.BlockSpec((tm, d), lambda i, order_ref: (i, 0))
    grid_spec = pltpu.PrefetchScalarGridSpec(
        num_scalar_prefetch=1, grid=(s // tm,),
        in_specs=[a_spec(q) for q in range(nq)]
        + [pl.BlockSpec((d, sh), lambda i, order_ref: (0, 0), pipeline_mode=pl.Buffered(1))] * nq
        + [row, row, pl.BlockSpec((1, d), lambda i, order_ref: (0, 0))] + [_hbm()] * n_after,
        out_specs=[row, pl.BlockSpec((SUBLANES, d), lambda i, order_ref: (0, 0))])
    return _call(body, name="in_proj_bwd_norm", grid_spec=grid_spec,
                 out_shape=[jax.ShapeDtypeStruct((s, d), F32), jax.ShapeDtypeStruct((SUBLANES, d), F32)],
                 compiler_params=_params(("arbitrary",), VMEM_LIMIT_BIG))(
                     order, *([dz] * nq), *ws, xin, add, gain, *after)


def _seg_mean(sq, segb):
    parts = []
    for cgrp in range(sq.shape[1] // SEG):
        blk = sq[:, cgrp * SEG:(cgrp + 1) * SEG]
        hi = blk.astype(BF16)
        r1 = blk - hi.astype(F32)
        mid = r1.astype(BF16)
        lo = (r1 - mid.astype(F32)).astype(BF16)
        acc = jnp.dot(hi, segb, preferred_element_type=F32)
        acc += jnp.dot(mid, segb, preferred_element_type=F32)
        acc += jnp.dot(lo, segb, preferred_element_type=F32)
        parts.append(acc)
    out = parts[0] if len(parts) == 1 else jnp.concatenate(parts, axis=1)
    return out * (1.0 / HEAD_DIM)


def _rope(v, cos, sa, sb):
    parts = []
    for cgrp in range(v.shape[1] // LANES):
        blk = v[:, cgrp * LANES:(cgrp + 1) * LANES]
        parts.append(blk * cos + pltpu.roll(blk, LANES - 8, 1) * sa + pltpu.roll(blk, 8, 1) * sb)
    return parts[0] if len(parts) == 1 else jnp.concatenate(parts, axis=1)


def _rope_t(dv, cos, sa, sb):
    parts = []
    for cgrp in range(dv.shape[1] // LANES):
        blk = dv[:, cgrp * LANES:(cgrp + 1) * LANES]
        parts.append(blk * cos + pltpu.roll(blk * sa, 8, 1) + pltpu.roll(blk * sb, LANES - 8, 1))
    return parts[0] if len(parts) == 1 else jnp.concatenate(parts, axis=1)


def _tile_lanes(vec, width):
    reps = width // LANES
    return vec if reps == 1 else jnp.tile(vec, (1, reps))


def qk_prep_fwd(z, tabs, qg, kg, segb, aw, after=()):
    s = z.shape[0]
    tm = _tile(s, 512)
    wq = aw + 2 * KV_WIDTH

    def body(z_ref, cos_ref, sa_ref, sb_ref, qg_ref, kg_ref, seg_ref, qs_ref, ks_ref, vb_ref):
        zz = z_ref[...]
        q, k, v = zz[:, :aw], zz[:, aw:aw + KV_WIDTH], zz[:, aw + KV_WIDTH:]
        cos, sa, sb, segm = cos_ref[...], sa_ref[...], sb_ref[...], seg_ref[...]
        rq = lax.rsqrt(_seg_mean(q * q, segm) + EPS)
        qn = (q * rq) * _tile_lanes(qg_ref[...], aw)
        qs_ref[...] = (_rope(qn, cos, sa, sb) * (HEAD_DIM ** -0.5)).astype(BF16)
        rk = lax.rsqrt(_seg_mean(k * k, segm) + EPS)
        kn = (k * rk) * _tile_lanes(kg_ref[...], KV_WIDTH)
        ks_ref[...] = _rope(kn, cos, sa, sb).astype(BF16)
        vb_ref[...] = v.astype(BF16)

    tab = pl.BlockSpec((tm, LANES), lambda i: (i, 0))
    vec = pl.BlockSpec((1, LANES), lambda i: (0, 0))
    return _call_after(body, after, name="qk_prep_fwd", grid=(s // tm,),
                       in_specs=[pl.BlockSpec((tm, wq), lambda i: (i, 0)), tab, tab, tab, vec, vec,
                                 pl.BlockSpec((SEG, SEG), lambda i: (0, 0))],
                       out_specs=[pl.BlockSpec((tm, aw), lambda i: (i, 0)),
                                  pl.BlockSpec((tm, KV_WIDTH), lambda i: (i, 0)),
                                  pl.BlockSpec((tm, KV_WIDTH), lambda i: (i, 0))],
                       out_shape=[jax.ShapeDtypeStruct((s, aw), BF16), jax.ShapeDtypeStruct((s, KV_WIDTH), BF16),
                                  jax.ShapeDtypeStruct((s, KV_WIDTH), BF16)],
                       compiler_params=_params(("parallel",)))(z, *tabs, qg, kg, segb)


def qk_prep_bwd(z, dqs, dks, dvs, tabs, qg, kg, segb, dz, aw):
    s = z.shape[0]
    tm = _tile(s, 512)
    wq = aw + 2 * KV_WIDTH

    def body(z_ref, dq_ref, dk_ref, dv_ref, cos_ref, sa_ref, sb_ref, qg_ref, kg_ref, seg_ref, dz_in,
             dz_ref, acc_ref):
        i = pl.program_id(0)
        zz = z_ref[...]
        q, k = zz[:, :aw], zz[:, aw:aw + KV_WIDTH]
        cos, sa, sb, segm = cos_ref[...], sa_ref[...], sb_ref[...], seg_ref[...]

        def one(xv, dout, gvec, width):
            g = _tile_lanes(gvec, width)
            r = lax.rsqrt(_seg_mean(xv * xv, segm) + EPS)
            xhat = xv * r
            dn = _rope_t(dout, cos, sa, sb)
            gd = dn * g
            dx = r * (gd - xhat * _seg_mean(xhat * gd, segm))
            contrib = jnp.sum(dn * xhat, axis=0, keepdims=True)
            folded = contrib[:, :LANES]
            for cgrp in range(1, width // LANES):
                folded = folded + contrib[:, cgrp * LANES:(cgrp + 1) * LANES]
            return dx, folded + pltpu.roll(folded, HEAD_DIM, 1)

        dq, gq = one(q, dq_ref[...] * (HEAD_DIM ** -0.5), qg_ref[...], aw)
        dk, gk = one(k, dk_ref[...], kg_ref[...], KV_WIDTH)
        dz_ref[:, :aw] = dq.astype(BF16)
        dz_ref[:, aw:aw + KV_WIDTH] = dk.astype(BF16)
        dz_ref[:, aw + KV_WIDTH:] = dv_ref[...].astype(BF16)

        @pl.when(i == 0)
        def _():
            acc_ref[...] = jnp.zeros_like(acc_ref)

        acc_ref[0:1, :] += gq
        acc_ref[1:2, :] += gk

    tab = pl.BlockSpec((tm, LANES), lambda i: (i, 0))
    vec = pl.BlockSpec((1, LANES), lambda i: (0, 0))
    kvb = pl.BlockSpec((tm, KV_WIDTH), lambda i: (i, 0))
    return _call(body, name="qk_prep_bwd", grid=(s // tm,),
                 in_specs=[pl.BlockSpec((tm, wq), lambda i: (i, 0)),
                           pl.BlockSpec((tm, aw), lambda i: (i, 0)), kvb, kvb, tab, tab, tab, vec, vec,
                           pl.BlockSpec((SEG, SEG), lambda i: (0, 0)), _hbm()],
                 out_specs=[pl.BlockSpec((tm, wq), lambda i: (i, 0)),
                            pl.BlockSpec((SUBLANES, LANES), lambda i: (0, 0))],
                 out_shape=[jax.ShapeDtypeStruct(dz.shape, BF16), jax.ShapeDtypeStruct((SUBLANES, LANES), F32)],
                 input_output_aliases={10: 0},
                 compiler_params=_params(("arbitrary",)))(z, dqs, dks, dvs, *tabs, qg, kg, segb, dz)


def _placed(band, lane_idx):
    out = {}
    for kh in range(N_KV_HEADS):
        grp = band[:, (kh // 2) * LANES:(kh // 2 + 1) * LANES]
        for half in (0, 1):
            t = grp if half == kh % 2 else pltpu.roll(grp, HEAD_DIM, 1)
            keep = (lane_idx >= half * HEAD_DIM) & (lane_idx < (half + 1) * HEAD_DIM)
            out[kh, half] = jnp.where(keep, t, jnp.zeros_like(t))
    return out


def _softmax_with_sink(sc, valid, sink):
    sc = jnp.where(valid, sc, NEG_INF)
    m = jnp.maximum(jnp.max(sc, axis=-1, keepdims=True), sink)
    e = jnp.exp(sc - m)
    es = jnp.exp(sink - m)
    den = jnp.sum(e, axis=-1, keepdims=True) + es
    return e / den, es / den


def _stack_halves(placed, kh):
    return jnp.concatenate([placed[kh, 0], placed[kh, 1]], axis=0)


def _valid_mask(n):
    r_i = lax.broadcasted_iota(jnp.int32, (BLOCK, 2 * BLOCK), 0)
    j_i = lax.broadcasted_iota(jnp.int32, (BLOCK, 2 * BLOCK), 1)
    return (j_i > r_i) & (j_i <= r_i + BLOCK) & ((n > 0) | (j_i >= BLOCK))


def attn_fwd(qs, ks, vb, z, sinks, aw, d, ga_off):
    s = qs.shape[0]
    nb = s // BLOCK
    nq = aw // HEAD_DIM
    grp_sz = nq // N_KV_HEADS
    n_ga = aw // COLT

    def body(q_ref, ko_ref, kp_ref, vo_ref, vp_ref, *rest):
        ga_refs = rest[:n_ga]
        sink_ref, attn_ref, mix_ref = rest[n_ga:]
        n = pl.program_id(0)
        lane_idx = lax.broadcasted_iota(jnp.int32, (2 * BLOCK, LANES), 1)
        kpl = _placed(jnp.concatenate([kp_ref[...], ko_ref[...]], axis=0), lane_idx)
        vpl = _placed(jnp.concatenate([vp_ref[...], vo_ref[...]], axis=0), lane_idx)
        valid = _valid_mask(n)
        groups = range(nq // 2)
        kcat = [_stack_halves(kpl, kh) for kh in range(N_KV_HEADS)]
        vcat = [_stack_halves(vpl, kh) for kh in range(N_KV_HEADS)]
        scs = [lax.dot_general(q_ref[:, c * LANES:(c + 1) * LANES], kcat[2 * c // grp_sz], NT,
                               preferred_element_type=F32) for c in groups]
        probs = [jnp.concatenate(
            [_softmax_with_sink(scs[c][:, half * 2 * BLOCK:(half + 1) * 2 * BLOCK], valid,
                                sink_ref[0, 2 * c + half])[0].astype(BF16) for half in (0, 1)], axis=1)
                 for c in groups]
        for c in groups:
            acc = jnp.dot(probs[c], vcat[2 * c // grp_sz], preferred_element_type=F32)
            attn_ref[:, c * LANES:(c + 1) * LANES] = acc
            col = c * LANES
            ga = ga_refs[col // COLT][:, col % COLT:col % COLT + LANES]
            mix_ref[:, c * LANES:(c + 1) * LANES] = (acc * (ga * _sigmoid(ga))).astype(BF16)

    own = lambda n: (n, 0)
    prev = lambda n: (jnp.maximum(n - 1, 0), 0)
    kvs = lambda imap: pl.BlockSpec((BLOCK, KV_WIDTH), imap)
    ga_specs = [pl.BlockSpec((BLOCK, COLT), functools.partial(lambda n, j: (n, ga_off + j), j=j))
                for j in range(n_ga)]
    return _call(body, name="attn_fwd", grid=(nb,),
                 in_specs=[pl.BlockSpec((BLOCK, aw), own), kvs(own), kvs(prev), kvs(own), kvs(prev)]
                 + ga_specs + [pl.BlockSpec(memory_space=pltpu.SMEM)],
                 out_specs=[pl.BlockSpec((BLOCK, aw), own), pl.BlockSpec((BLOCK, aw), own)],
                 out_shape=[jax.ShapeDtypeStruct((s, aw), F32), jax.ShapeDtypeStruct((s, d), BF16)],
                 compiler_params=_params(("parallel",)))(qs, ks, ks, vb, vb, *([z] * n_ga), sinks)


def gate_bwd(d_mix, attn, z, aw, ga_off, after=()):
    s, in_w = z.shape
    tm = _tile(s, 1024)

    def body(dm_ref, at_ref, ga_ref, do_ref, dz_ref):
        ga = ga_ref[...]
        sig = _sigmoid(ga)
        dm = dm_ref[...]
        do_ref[...] = (dm * (ga * sig)).astype(BF16)
        dz_ref[...] = ((dm * at_ref[...]) * (sig * (1.0 + ga * (1.0 - sig)))).astype(BF16)

    blk = pl.BlockSpec((tm, COLT), lambda i, j: (i, j))
    gab = pl.BlockSpec((tm, COLT), lambda i, j: (i, ga_off + j))
    return _call_after(body, after, name="gate_bwd", grid=(s // tm, aw // COLT),
                       in_specs=[blk, blk, gab], out_specs=[blk, gab],
                       out_shape=[jax.ShapeDtypeStruct((s, aw), BF16), jax.ShapeDtypeStruct((s, in_w), BF16)],
                       compiler_params=_params(("parallel", "parallel")))(d_mix, attn, z)


def attn_bwd(qs, ks, vb, d_o, sinks, aw):
    s = qs.shape[0]
    nb = s // BLOCK
    nq = aw // HEAD_DIM
    grp_sz = nq // N_KV_HEADS

    def body(q_ref, ko_ref, kp_ref, vo_ref, vp_ref, do_ref, sink_ref, dq_ref, dk_ref, dv_ref, ds_ref,
             ck_ref, cv_ref):
        n = pl.program_id(0)

        @pl.when(n == 0)
        def _():
            ds_ref[...] = jnp.zeros_like(ds_ref)
            dk_ref[...] = jnp.zeros_like(dk_ref)
            dv_ref[...] = jnp.zeros_like(dv_ref)

        @pl.when(n < nb)
        def _():
            lane_idx = lax.broadcasted_iota(jnp.int32, (2 * BLOCK, LANES), 1)
            lane_row = lax.broadcasted_iota(jnp.int32, (1, LANES), 1)
            kpl = _placed(jnp.concatenate([kp_ref[...], ko_ref[...]], axis=0), lane_idx)
            vpl = _placed(jnp.concatenate([vp_ref[...], vo_ref[...]], axis=0), lane_idx)
            valid = _valid_mask(n)
            ds_row = jnp.zeros((1, LANES), F32)
            wide = 2 * BLOCK
            groups = range(nq // 2)
            per_kv = grp_sz // 2
            kcat = [_stack_halves(kpl, kh) for kh in range(N_KV_HEADS)]
            vcat = [_stack_halves(vpl, kh) for kh in range(N_KV_HEADS)]
            qg = [q_ref[:, c * LANES:(c + 1) * LANES] for c in groups]
            dog = [do_ref[:, c * LANES:(c + 1) * LANES] for c in groups]
            scs = [lax.dot_general(qg[c], kcat[c // per_kv], NT, preferred_element_type=F32) for c in groups]
            dps = [lax.dot_general(dog[c], vcat[c // per_kv], NT, preferred_element_type=F32) for c in groups]
            ds_pairs, p_pairs = [], []
            for c in groups:
                probs, dss = [], []
                for half in (0, 1):
                    h = 2 * c + half
                    cols = slice(half * wide, (half + 1) * wide)
                    p, p_sink = _softmax_with_sink(scs[c][:, cols], valid, sink_ref[0, h])
                    delta = jnp.sum(p * dps[c][:, cols], axis=-1, keepdims=True)
                    dss.append((p * (dps[c][:, cols] - delta)).astype(BF16))
                    probs.append(p.astype(BF16))
                    dsink = -jnp.sum(p_sink * delta, axis=0, keepdims=True)
                    ds_row += jnp.where(lane_row == h, dsink, 0.0)
                ds_pairs.append(jnp.concatenate(dss, axis=1))
                p_pairs.append(jnp.concatenate(probs, axis=1))
            for c in groups:
                dq_ref[:, c * LANES:(c + 1) * LANES] = jnp.dot(ds_pairs[c], kcat[c // per_kv],
                                                               preferred_element_type=F32)
            dkts = [lax.dot_general(qg[c], ds_pairs[c], TN, preferred_element_type=F32) for c in groups]
            dvts = [lax.dot_general(dog[c], p_pairs[c], TN, preferred_element_type=F32) for c in groups]
            dkt, dvt = [], []
            for kh in range(N_KV_HEADS):
                for parts, out in ((dkts, dkt), (dvts, dvt)):
                    tot = parts[kh * per_kv]
                    for c in range(kh * per_kv + 1, (kh + 1) * per_kv):
                        tot = tot + parts[c]
                    out.append(tot[:HEAD_DIM, :wide] + tot[HEAD_DIM:, wide:])
            ds_ref[0:1, :] += ds_row
            back = lambda t: jnp.concatenate(
                [jnp.concatenate(t[2 * g:2 * g + 2], axis=0).T for g in range(N_KV_HEADS // 2)], axis=1)
            dk_band, dv_band = back(dkt), back(dvt)

            @pl.when(n > 0)
            def _():
                dk_ref[...] = ck_ref[...] + dk_band[:BLOCK]
                dv_ref[...] = cv_ref[...] + dv_band[:BLOCK]

            ck_ref[...] = dk_band[BLOCK:]
            cv_ref[...] = dv_band[BLOCK:]

        @pl.when(n == nb)
        def _():
            dk_ref[...] = ck_ref[...]
            dv_ref[...] = cv_ref[...]

    own = lambda n: (jnp.minimum(n, nb - 1), 0)
    prev = lambda n: (jnp.clip(n - 1, 0, nb - 1), 0)
    done = lambda n: (jnp.maximum(n - 1, 0), 0)
    kvs = lambda imap: pl.BlockSpec((BLOCK, KV_WIDTH), imap)
    return _call(body, name="attn_bwd", grid=(nb + 1,),
                 in_specs=[pl.BlockSpec((BLOCK, aw), own), kvs(own), kvs(prev), kvs(own), kvs(prev),
                           pl.BlockSpec((BLOCK, aw), own), pl.BlockSpec(memory_space=pltpu.SMEM)],
                 out_specs=[pl.BlockSpec((BLOCK, aw), own), kvs(done), kvs(done),
                            pl.BlockSpec((SUBLANES, LANES), lambda n: (0, 0))],
                 out_shape=[jax.ShapeDtypeStruct((s, aw), F32), jax.ShapeDtypeStruct((s, KV_WIDTH), F32),
                            jax.ShapeDtypeStruct((s, KV_WIDTH), F32), jax.ShapeDtypeStruct((SUBLANES, LANES), F32)],
                 scratch_shapes=[pltpu.VMEM((BLOCK, KV_WIDTH), F32), pltpu.VMEM((BLOCK, KV_WIDTH), F32)],
                 compiler_params=_params(("arbitrary",)))(qs, ks, ks, vb, vb, d_o, sinks)


def conv_fwd(z, conv_w, mix, aw, cw, b_off):
    s = z.shape[0]
    tm = _tile(s, 1024)
    nseg = cw // COLT
    hb = tm // SUBLANES

    def body(b_ref, c_ref, h_ref, g_ref, cp_ref, hp_ref, w_ref, mix_in, mix_ref):
        i = pl.program_id(0)
        u = c_ref[...] * h_ref[...]
        up = jnp.where(i > 0, cp_ref[...] * hp_ref[...], 0.0)
        ext = jnp.concatenate([up, u], axis=0)
        um1 = pltpu.roll(ext, 1, 0)[SUBLANES:]
        um2 = pltpu.roll(ext, 2, 0)[SUBLANES:]
        w = w_ref[...]
        cv = w[0:1] * um2 + w[1:2] * um1 + w[2:3] * u
        g = g_ref[...]
        mix_ref[...] = ((b_ref[...] * cv) * (g * _sigmoid(g))).astype(BF16)

    seg = lambda k: pl.BlockSpec((tm, COLT), functools.partial(lambda i, j, k: (i, b_off + k * nseg + j), k=k))
    halo = lambda k: pl.BlockSpec(
        (SUBLANES, COLT), functools.partial(lambda i, j, k: (jnp.maximum(i * hb - 1, 0), b_off + k * nseg + j), k=k))
    return _call(body, name="conv_fwd", grid=(s // tm, nseg),
                 in_specs=[seg(0), seg(1), seg(2), seg(3), halo(1), halo(2),
                           pl.BlockSpec((SUBLANES, COLT), lambda i, j: (0, j)), _hbm()],
                 out_specs=pl.BlockSpec((tm, COLT), lambda i, j: (i, aw // COLT + j)),
                 out_shape=jax.ShapeDtypeStruct(mix.shape, BF16),
                 input_output_aliases={7: 0},
                 compiler_params=_params(("parallel", "parallel")))(z, z, z, z, z, z, conv_w, mix)


def conv_bwd(z, d_mix, conv_w, dz, aw, cw, b_off):
    s = z.shape[0]
    tm = _tile(s, 512)
    nseg = cw // COLT
    hb = tm // SUBLANES
    n_row = s // tm
    last_h = s // SUBLANES - 1
    n_step = nseg * n_row

    def body(b_ref, c_ref, h_ref, g_ref, cp_ref, hp_ref, bn_ref, gn_ref, dm_ref, dmn_ref, w_ref, dz_in,
             dz_ref, acc_ref, stash_ref, sems):
        j = pl.program_id(0)
        i = pl.program_id(1)
        step = j * n_row + i
        slot = step % 2

        def copies(from_slot, at_step):
            jj, ii = at_step // n_row, at_step % n_row
            rows = pl.ds(pl.multiple_of(ii * tm, tm), tm)
            return [pltpu.make_async_copy(
                stash_ref.at[from_slot, q],
                dz_ref.at[rows, pl.ds(pl.multiple_of((b_off + q * nseg + jj) * COLT, COLT), COLT)],
                sems.at[from_slot, q]) for q in range(4)]

        @pl.when(step >= 2)
        def _():
            for cp in copies(slot, step - 2):
                cp.wait()

        cc, hh, bb, g = c_ref[...], h_ref[...], b_ref[...], g_ref[...]
        w = w_ref[...]
        u = cc * hh
        up = jnp.where(i > 0, cp_ref[...] * hp_ref[...], 0.0)
        ext = jnp.concatenate([up, u], axis=0)
        um1 = pltpu.roll(ext, 1, 0)[SUBLANES:]
        um2 = pltpu.roll(ext, 2, 0)[SUBLANES:]
        cv = w[0:1] * um2 + w[1:2] * um1 + w[2:3] * u
        sig = _sigmoid(g)
        sg = g * sig
        dm = dm_ref[...]
        d_cv = (dm * bb) * sg
        gn = gn_ref[...]
        d_cv_next = jnp.where(i < n_row - 1, (dmn_ref[...] * bn_ref[...]) * (gn * _sigmoid(gn)), 0.0)
        ext2 = jnp.concatenate([d_cv, d_cv_next], axis=0)
        dp1 = pltpu.roll(ext2, tm + SUBLANES - 1, 0)[:tm]
        dp2 = pltpu.roll(ext2, tm + SUBLANES - 2, 0)[:tm]
        d_u = w[2:3] * d_cv + w[1:2] * dp1 + w[0:1] * dp2
        stash_ref[slot, 0] = ((dm * cv) * sg).astype(BF16)
        stash_ref[slot, 1] = (d_u * hh).astype(BF16)
        stash_ref[slot, 2] = (d_u * cc).astype(BF16)
        stash_ref[slot, 3] = (((dm * bb) * cv) * (sig * (1.0 + g * (1.0 - sig)))).astype(BF16)
        for cp in copies(slot, step):
            cp.start()

        @pl.when(i == 0)
        def _():
            acc_ref[...] = jnp.zeros_like(acc_ref)

        acc_ref[0:1, :] += jnp.sum(d_cv * um2, axis=0, keepdims=True)
        acc_ref[1:2, :] += jnp.sum(d_cv * um1, axis=0, keepdims=True)
        acc_ref[2:3, :] += jnp.sum(d_cv * u, axis=0, keepdims=True)

        @pl.when(step == n_step - 1)
        def _():
            if n_step >= 2:
                for cp in copies(1 - slot, step - 1):
                    cp.wait()
            for cp in copies(slot, step):
                cp.wait()

    seg = lambda q: pl.BlockSpec((tm, COLT), functools.partial(lambda j, i, q: (i, b_off + q * nseg + j), q=q))
    halo_p = lambda q: pl.BlockSpec(
        (SUBLANES, COLT),
        functools.partial(lambda j, i, q: (jnp.maximum(i * hb - 1, 0), b_off + q * nseg + j), q=q))
    halo_n = lambda q: pl.BlockSpec(
        (SUBLANES, COLT),
        functools.partial(lambda j, i, q: (jnp.minimum((i + 1) * hb, last_h), b_off + q * nseg + j), q=q))
    return _call(body, name="conv_bwd", grid=(nseg, n_row),
                 in_specs=[seg(0), seg(1), seg(2), seg(3), halo_p(1), halo_p(2), halo_n(0), halo_n(3),
                           pl.BlockSpec((tm, COLT), lambda j, i: (i, aw // COLT + j)),
                           pl.BlockSpec((SUBLANES, COLT),
                                        lambda j, i: (jnp.minimum((i + 1) * hb, last_h), aw // COLT + j)),
                           pl.BlockSpec((SUBLANES, COLT), lambda j, i: (0, j)), _hbm()],
                 out_specs=[_hbm(), pl.BlockSpec((SUBLANES, COLT), lambda j, i: (0, j))],
                 out_shape=[jax.ShapeDtypeStruct(dz.shape, BF16), jax.ShapeDtypeStruct((SUBLANES, cw), F32)],
                 input_output_aliases={11: 0},
                 scratch_shapes=[pltpu.VMEM((2, 4, tm, COLT), BF16), pltpu.SemaphoreType.DMA((2, 4))],
                 compiler_params=_params(("arbitrary", "arbitrary")))(
                     z, z, z, z, z, z, z, z, d_mix, d_mix, conv_w, dz)


def _place():
    x, y, c = lax.axis_index("x"), lax.axis_index("y"), lax.axis_index("c")
    chips = [(1 - x, y), (x, 1 - y), (1 - x, 1 - y)]
    return x, y, c, chips


def _remote(src, dst, send_sem, recv_sem, device):
    return pltpu.make_async_remote_copy(src_ref=src, dst_ref=dst, send_sem=send_sem, recv_sem=recv_sem,
                                        device_id=device, device_id_type=MESH)


def plan_gather_ici(n_split):
    def plan(refs, send, recv):
        x, y, c, chips = _place()
        me = 2 * x + y
        mine, theirs = [], []
        for w, ref in enumerate(refs):
            for j, (cx, cy) in enumerate(chips):
                def part(slot):
                    if w >= n_split:
                        return ref.at[slot]
                    hr = ref.shape[1] // 2
                    return ref.at[slot, pl.ds(c * hr, hr)]
                k = 3 * w + j
                mine.append(_remote(part(me), part(me), send.at[k], recv.at[k], (cx, cy, c)))
                theirs.append(_remote(part(2 * cx + cy), part(2 * cx + cy), send.at[k], recv.at[k], (cx, cy, c)))
        return mine, theirs
    return plan


def plan_shard_ici(j):
    def plan(refs, send, recv):
        _, _, c, chips = _place()
        own, land = refs
        hr = own.shape[0] // 2
        rows = pl.ds(c * hr, hr)
        cp = _remote(own.at[rows], land.at[rows], send.at[0], recv.at[0], (*chips[j], c))
        return [cp], [cp]
    return plan


def plan_shard_relay(refs, send, recv):
    _, _, c, chips = _place()
    land_x, land_y, land_far = refs
    hr = land_far.shape[0] // 2
    quarter = lambda q: pl.ds(c * hr + q * (hr // 2), hr // 2)
    mine = [_remote(land_y.at[quarter(0)], land_far.at[quarter(0)], send.at[0], recv.at[0], (*chips[0], c)),
            _remote(land_x.at[quarter(1)], land_far.at[quarter(1)], send.at[1], recv.at[1], (*chips[1], c))]
    return mine, mine


def plan_shard_pass(refs, send, recv):
    x, y, c, _ = _place()
    mine, theirs = [], []
    for w, land in enumerate(refs):
        hr = land.shape[0] // 2
        half = lambda core: land.at[pl.ds(core * hr, hr)]
        mine.append(_remote(half(c), half(c), send.at[w], recv.at[w], (x, y, 1 - c)))
        theirs.append(_remote(half(1 - c), half(1 - c), send.at[w], recv.at[w], (x, y, 1 - c)))
    return mine, theirs


def plan_gather_pass(refs, send, recv):
    x, y, c, chips = _place()
    mine, theirs = [], []
    for w, ref in enumerate(refs):
        hr = ref.shape[1] // 2
        for j, (cx, cy) in enumerate(chips):
            half = lambda core: ref.at[2 * cx + cy, pl.ds(core * hr, hr)]
            k = 3 * w + j
            mine.append(_remote(half(c), half(c), send.at[k], recv.at[k], (x, y, 1 - c)))
            theirs.append(_remote(half(1 - c), half(1 - c), send.at[k], recv.at[k], (x, y, 1 - c)))
    return mine, theirs


def plan_swap(refs, send, recv):
    x, y, c, _ = _place()
    nw = len(refs) // 2
    mine = []
    for w in range(nw):
        hr = refs[w].shape[1] // 2
        mine.append(_remote(refs[w].at[:, pl.ds((1 - c) * hr, hr), :], refs[nw + w], send.at[w], recv.at[w],
                            (x, y, 1 - c)))
    return mine, mine


def plan_scatter(refs, send, recv):
    x, y, c, chips = _place()
    me = 2 * x + y
    nw = len(refs) // 2
    mine, theirs = [], []
    for w in range(nw):
        for j, (cx, cy) in enumerate(chips):
            k = 3 * w + j
            mine.append(_remote(refs[w].at[2 * cx + cy], refs[nw + w].at[me], send.at[k], recv.at[k], (cx, cy, c)))
            theirs.append(_remote(refs[w].at[me], refs[nw + w].at[2 * cx + cy], send.at[k], recv.at[k], (cx, cy, c)))
    return mine, theirs


def plan_join(refs, send, recv):
    x, y, c, _ = _place()
    mine, theirs = [], []
    for w, ref in enumerate(refs):
        hr = ref.shape[0] // 2
        half = lambda core: ref.at[pl.ds(core * hr, hr)]
        mine.append(_remote(half(c), half(c), send.at[w], recv.at[w], (x, y, 1 - c)))
        theirs.append(_remote(half(1 - c), half(1 - c), send.at[w], recv.at[w], (x, y, 1 - c)))
    return mine, theirs


def exchange(name, plan, arrays, n, after=()):
    na = len(arrays)

    def body(*refs):
        mine, theirs = plan(refs[:na], refs[2 * na], refs[2 * na + 1])
        for cp in mine:
            cp.start()
        for cp in theirs:
            cp.wait_recv()
        for cp in mine:
            cp.wait_send()

    return _call_after(body, after, name=name, in_specs=[_hbm()] * na, out_specs=[_hbm()] * na,
                       out_shape=[jax.ShapeDtypeStruct(a.shape, a.dtype) for a in arrays],
                       input_output_aliases={i: i for i in range(na)},
                       scratch_shapes=[pltpu.SemaphoreType.DMA((n,)), pltpu.SemaphoreType.DMA((n,))])(*arrays)


def exchange_start(name, groups, arrays, after=()):
    na, ng = len(arrays), len(groups)

    def body(*refs):
        for g, (plan, idx, _) in enumerate(groups):
            mine, _ = plan([refs[i] for i in idx], refs[na + 2 * g], refs[na + 2 * g + 1])
            for cp in mine:
                cp.start()
        refs[-1][...] = jnp.zeros_like(refs[-1])

    hbm = pl.BlockSpec(memory_space=pltpu.HBM)
    sem = pl.BlockSpec(memory_space=pltpu.SEMAPHORE)
    sem_types = [pltpu.SemaphoreType.DMA((n,)) for _, _, n in groups for _ in (0, 1)]
    outs = _call_after(body, after, name=name, in_specs=[hbm] * na,
                       out_specs=[sem] * (2 * ng) + [hbm] * na + [pl.BlockSpec(memory_space=pltpu.VMEM)],
                       out_shape=sem_types + [pltpu.HBM(a.shape, a.dtype) for a in arrays]
                       + [jax.ShapeDtypeStruct((SUBLANES, LANES), F32)],
                       input_output_aliases={i: i + 2 * ng for i in range(na)},
                       compiler_params=pltpu.CompilerParams(
                           has_side_effects=pltpu.SideEffectType.DATAFLOW_SIDE_EFFECTING))(
                               *[pltpu.with_memory_space_constraint(a, pltpu.HBM) for a in arrays])
    sems = [(outs[2 * g], outs[2 * g + 1]) for g in range(ng)]
    return sems, list(outs[2 * ng:-1]), outs[-1]


def exchange_wait(name, plan, sems, arrays, after):
    send_sems, recv_sems = sems
    na = len(arrays)

    def body(*refs):
        mine, theirs = plan(refs[:na], refs[na], refs[na + 1])
        for cp in mine:
            cp.wait_send()
        for cp in theirs:
            cp.wait_recv()

    hbm = pl.BlockSpec(memory_space=pltpu.HBM)
    sem = pl.BlockSpec(memory_space=pltpu.SEMAPHORE)
    return _call(body, name=name, in_specs=[hbm] * na + [sem, sem, _hbm()], out_specs=[hbm] * na,
                 out_shape=[pltpu.HBM(a.shape, a.dtype) for a in arrays],
                 input_output_aliases={i: i for i in range(na)},
                 compiler_params=pltpu.CompilerParams(
                     has_side_effects=pltpu.SideEffectType.DATAFLOW_SIDE_EFFECTING))(
                         *arrays, send_sems, recv_sems, after)


def plan_allgather_small(refs, send, recv):
    x, y, c, _ = _place()
    buf, = refs
    mine, theirs = [], []
    flips = [(fx, fy, fc) for fx in (0, 1) for fy in (0, 1) for fc in (0, 1)][1:]
    for k, (fx, fy, fc) in enumerate(flips):
        peer = (x ^ fx, y ^ fy, c ^ fc)
        slot = lambda px, py, pc: buf.at[4 * px + 2 * py + pc]
        mine.append(_remote(slot(x, y, c), slot(x, y, c), send.at[k], recv.at[k], peer))
        theirs.append(_remote(slot(*peer), slot(*peer), send.at[k], recv.at[k], peer))
    return mine, theirs


def add_sibling(grad, got, core, name):
    _, r, c = grad.shape
    hr = r // 2
    tr = _tile(hr, 512)
    nblk = hr // tr

    def body(core_ref, g_ref, o_ref, out_ref):
        out_ref[...] = (g_ref[...].astype(F32) + o_ref[...].astype(F32)).astype(BF16)

    grid_spec = pltpu.PrefetchScalarGridSpec(
        num_scalar_prefetch=1, grid=(N_CHIPS, nblk),
        in_specs=[pl.BlockSpec((None, tr, c), lambda t, i, core_ref: (t, core_ref[0] * nblk + i, 0)),
                  pl.BlockSpec((None, tr, c), lambda t, i, core_ref: (t, i, 0))],
        out_specs=pl.BlockSpec((None, tr, c), lambda t, i, core_ref: (t, i, 0)))
    return _call(body, name=name, grid_spec=grid_spec,
                 out_shape=jax.ShapeDtypeStruct((N_CHIPS, hr, c), BF16),
                 compiler_params=_params(("parallel", "parallel")))(core, grad, got)


def sum_chips(mine, owned, place, name):
    _, hr, c = mine.shape
    tr = _tile(hr, 512)
    nblk = hr // tr

    def body(place_ref, m_ref, o1_ref, o2_ref, o3_ref, out_ref):
        acc = m_ref[...].astype(F32)
        for o_ref in (o1_ref, o2_ref, o3_ref):
            acc = acc + o_ref[...].astype(F32)
        out_ref[...] = acc

    other = lambda k: pl.BlockSpec((None, tr, c), lambda i, place_ref: ((place_ref[0] + k) % N_CHIPS, i, 0))
    grid_spec = pltpu.PrefetchScalarGridSpec(
        num_scalar_prefetch=1, grid=(nblk,),
        in_specs=[other(0), other(1), other(2), other(3)],
        out_specs=pl.BlockSpec((tr, c), lambda i, place_ref: (place_ref[1] * nblk + i, 0)))
    return _call(body, name=name, grid_spec=grid_spec,
                 out_shape=jax.ShapeDtypeStruct((2 * hr, c), F32),
                 compiler_params=_params(("parallel",)))(place, mine, owned, owned, owned)


def sum_devices(gathered):
    _, rows, width = gathered.shape

    def body(g_ref, out_ref):
        acc = g_ref[0]
        for dev in range(1, 8):
            acc = acc + g_ref[dev]
        out_ref[...] = acc
        tail = acc[SUBLANES:]
        out_ref[SUBLANES:, :] = jnp.broadcast_to(jnp.sum(tail, axis=1, keepdims=True), tail.shape)

    return _call(body, name="sum_devices",
                 in_specs=[pl.BlockSpec(memory_space=pltpu.VMEM)],
                 out_specs=pl.BlockSpec(memory_space=pltpu.VMEM),
                 out_shape=jax.ShapeDtypeStruct((rows, width), F32))(gathered)


def _rope_tables(s):
    half = ROT_DIM // 2
    inv_freq = jnp.power(jnp.float32(ROPE_THETA), -jnp.arange(half, dtype=F32) * 2.0 / ROT_DIM)
    freq64 = jnp.concatenate([inv_freq, inv_freq, jnp.zeros((HEAD_DIM - ROT_DIM,), F32)])
    freq = jnp.concatenate([freq64, freq64])[None, :]
    dim = (jnp.arange(LANES) % HEAD_DIM)[None, :]
    ang = jnp.arange(s).astype(F32)[:, None] * freq
    cos, sin = jnp.cos(ang), jnp.sin(ang)
    return cos, jnp.where(dim < half, -sin, 0.0), jnp.where((dim >= half) & (dim < ROT_DIM), sin, 0.0)


def _pad_rows(a, rows):
    return jnp.pad(a, ((0, rows - a.shape[0]), (0, 0)))


def _pad_cols(a, cols):
    return jnp.pad(a, ((0, 0), (0, cols - a.shape[1])))


def kernel(x, p, norm_gain, w_in, q_norm_gain, k_norm_gain, attn_sinks, conv_w, w_out, ple_gate_norm_gain, w_ple_gate, b_ple_gate, w_ple_proj, ple_norm_gain, loss_target, m_norm_gain, m_w_in, m_q_norm_gain, m_k_norm_gain, m_attn_sinks, m_conv_w, m_w_out, m_ple_gate_norm_gain, m_w_ple_gate, m_b_ple_gate, m_w_ple_proj, m_ple_norm_gain, v_norm_gain, v_w_in, v_q_norm_gain, v_k_norm_gain, v_attn_sinks, v_conv_w, v_w_out, v_ple_gate_norm_gain, v_w_ple_gate, v_b_ple_gate, v_w_ple_proj, v_ple_norm_gain):
    x2, p2, tgt = x[0], p[0, 0], loss_target[0]
    s, d = x2.shape
    ple = p2.shape[1]
    aw = d // 2
    cw = d - aw
    nq = aw // HEAD_DIM
    sh = w_in.shape[2]
    in_w = N_CHIPS * sh
    dq = d // N_CHIPS
    cq = cw // N_CHIPS
    ga_off = (aw + 2 * KV_WIDTH) // COLT
    b_off = (2 * aw + 2 * KV_WIDTH) // COLT
    assert in_w == 2 * aw + 2 * KV_WIDTH + 4 * cw and aw % COLT == 0 and cw % COLT == 0
    assert s % BLOCK == 0 and sh % LANES == 0 and nq % (2 * N_KV_HEADS) == 0

    core = lax.axis_index("c").astype(jnp.int32).reshape(1)
    chip = 2 * lax.axis_index("x") + lax.axis_index("y")

    chip1 = chip.astype(jnp.int32).reshape(1)
    place = jnp.concatenate([chip1, core])
    w_in_own = cast_bf16(w_in[0], "cast_w_in")
    rest = [cast_into_slot(w_out[0], chip1, "cast_w_out"), cast_into_slot(w_ple_gate[0], chip1, "cast_w_pg"),
            cast_into_slot(w_ple_proj[0], chip1, "cast_w_pp"),
            lax.dynamic_update_slice(jnp.zeros((N_CHIPS, SUBLANES, cq), F32),
                                     _pad_rows(conv_w[0], SUBLANES)[None], (chip, 0, 0))]
    order = jnp.stack([chip, chip ^ 2, chip ^ 1, chip ^ 3]).astype(jnp.int32)
    wi_sems, wi_arrs, wi_token = exchange_start(
        "gather_wi_start", [(plan_shard_ici(j), [0, 1 + j], 1) for j in range(2)],
        [w_in_own] + [lax.empty((d, sh), BF16) for _ in range(3)])

    tm = _tile(s, 1024)
    tn = _tile(d, 1024)
    tk = _tile(d, 2048)
    ts = _tile(s, 2048)
    tabs = _rope_tables(s)
    qg = jnp.tile(q_norm_gain, (1, LANES // HEAD_DIM))
    kg = jnp.tile(k_norm_gain, (1, LANES // HEAD_DIM))
    seg_i = jnp.arange(SEG) // HEAD_DIM
    segb = (seg_i[:, None] == seg_i[None, :]).astype(BF16)
    h, z = norm_in_proj(x2, norm_gain, wi_arrs[0], order, in_w)
    own, land_x = exchange_wait("gather_wi_wait0", plan_shard_ici(0), wi_sems[0], [wi_arrs[0], wi_arrs[1]], z)
    own, land_y = exchange_wait("gather_wi_wait1", plan_shard_ici(1), wi_sems[1], [own, wi_arrs[2]], land_x)
    relay_sems, relayed, relay_token = exchange_start(
        "gather_wi_relay_start", [(plan_shard_relay, [0, 1, 2], 2)], [land_x, land_y, wi_arrs[3]])
    rest_sems, rest, rest_token = exchange_start(
        "gather_rest_start", [(plan_gather_ici(3), [0, 1, 2, 3], 12)], rest, after=(relay_token,))
    land_x, land_y = exchange("gather_wi_pass01", plan_shard_pass, relayed[:2], 2, after=(rest_token,))
    z = in_proj_part(h, land_x, z, order, 1, in_w)
    z = in_proj_part(h, land_y, z, order, 2, in_w)
    land_x, land_y, land_far = exchange_wait("gather_wi_relay_wait", plan_shard_relay, relay_sems[0],
                                             [land_x, land_y, relayed[2]], z)
    land_far, = exchange("gather_wi_pass2", plan_shard_pass, [land_far], 1)
    z = in_proj_part(h, land_far, z, order, 3, in_w)
    w_shards = [own, land_x, land_y, land_far]
    wo_all, wg_all, wp_all, conv_all = exchange_wait("gather_rest_wait", plan_gather_ici(3), rest_sems[0], rest, z)
    pass_sems, passed, pass_token = exchange_start(
        "gather_rest_pass_start", [(plan_gather_pass, [0, 1, 2], 9)], [wo_all, wg_all, wp_all])
    conv_full = conv_all.transpose(1, 0, 2).reshape(SUBLANES, cw)
    qs, ks, vb = qk_prep_fwd(z, tabs, qg, kg, segb, aw, after=(pass_token,))
    attn, mix = attn_fwd(qs, ks, vb, z, attn_sinks, aw, d, ga_off)
    mix = conv_fwd(z, conv_full, mix, aw, cw, b_off)
    wo_all, wg_all, wp_all = exchange_wait("gather_rest_pass_wait", plan_gather_pass, pass_sems[0], passed, mix)
    wo_full = wo_all.reshape(d, d)
    wg_full = wg_all.reshape(d, d)
    sq = lambda shape: dict(
        a_spec=pl.BlockSpec((tm, tk), lambda i, j, k: (i, k)),
        o_spec=pl.BlockSpec((tm, tn), lambda i, j, k: (i, j)),
        out_shape=jax.ShapeDtypeStruct(shape, F32))
    x1, hg = out_proj_norm(mix, wo_full, x2, ple_gate_norm_gain)
    pq = d // N_CHIPS

    d_gl, d_pe, dy, acc_head = head_fwd_bwd(x1, hg, wg_full, p2, wp_all, tgt, b_ple_gate, ple_norm_gain)
    wgrad = lambda a_cols, shape3, o_spec, b_cols, name, a, b, grid: matmul(
        a, b, grid=grid,
        a_spec=pl.BlockSpec((ts, a_cols), lambda i, j, k: (k, i)),
        b_spec=pl.BlockSpec((ts, b_cols), lambda i, j, k: (k, j)),
        o_spec=o_spec, out_shape=jax.ShapeDtypeStruct(shape3, BF16), dims=TN, name=name)
    g_wg = wgrad(tn, (d, d), pl.BlockSpec((tn, tn), lambda i, j, k: (i, j)), tn, "mm_g_wg", hg, d_gl,
                 (d // tn, d // tn, s // ts))
    g_wp = wgrad(ple, (N_CHIPS, ple, pq), pl.BlockSpec((None, ple, pq), lambda i, j, k: (j, 0, 0)), pq,
                 "mm_g_wp", p2, d_pe, (1, N_CHIPS, s // ts))
    d_x1, d_x1b, acc_g = gate_proj_bwd_norm(d_gl, wg_full, x1, dy, ple_gate_norm_gain)
    d_mix = matmul(d_x1b, wo_full, grid=(s // tm, d // tn, d // tk),
                   b_spec=pl.BlockSpec((tn, tk), lambda i, j, k: (j, k)), dims=NT, name="mm_d_mix", **sq((s, d)))
    g_wo = wgrad(tn, (d, d), pl.BlockSpec((tn, tn), lambda i, j, k: (i, j)), tn, "mm_g_wo", mix, d_x1b,
                 (d // tn, d // tn, s // ts))
    def reduce_start(tag, grads):
        n = len(grads)
        lands = [lax.empty((N_CHIPS, a.shape[1] // 2, a.shape[2]), BF16) for a in grads]
        swapped = exchange("swap_" + tag, plan_swap, list(grads) + lands, n)
        parts = [add_sibling(g, o, core, "add_sibling_%s%d" % (tag, i))
                 for i, (g, o) in enumerate(zip(swapped[:n], swapped[n:]))]
        return exchange_start("scatter_%s_start" % tag, [(plan_scatter, list(range(2 * n)), 3 * n)],
                              parts + [lax.empty(a.shape, BF16) for a in parts])

    def reduce_sum(tag, handle, after):
        sems, arrays, _ = handle
        n = len(arrays) // 2
        got = exchange_wait("scatter_%s_wait" % tag, plan_scatter, sems[0], arrays, after)
        return [sum_chips(pt, o, place, "sum_chips_%s%d" % (tag, i))
                for i, (pt, o) in enumerate(zip(got[:n], got[n:]))]

    early = reduce_start("early", [g_wo.reshape(N_CHIPS, dq, d), g_wg.reshape(N_CHIPS, dq, d), g_wp])
    d_o, dz = gate_bwd(d_mix, attn, z, aw, ga_off, after=(early[-1],))
    dqs, dks, dvs, acc_sink = attn_bwd(qs, ks, vb, d_o, attn_sinks, aw)
    dz, acc_qk = qk_prep_bwd(z, dqs, dks, dvs, tabs, qg, kg, segb, dz, aw)
    dz, acc_conv = conv_bwd(z, d_mix, conv_full, dz, aw, cw, b_off)
    join_sems, joining, join_token = exchange_start(
        "join_early_start", [(plan_join, [0, 1, 2], 3)], reduce_sum("early", early, dz))
    g_wi = matmul(h, dz, grid=(d // tn, N_CHIPS, 1),
                  a_spec=pl.BlockSpec((s, tn), lambda i, j, k: (0, i), pipeline_mode=pl.Buffered(1)),
                  b_spec=pl.BlockSpec((s, sh), lambda i, j, k: (0, j)),
                  o_spec=pl.BlockSpec((None, tn, sh), lambda i, j, k: (j, i, 0)),
                  out_shape=jax.ShapeDtypeStruct((N_CHIPS, d, sh), BF16), dims=TN, name="mm_g_wi",
                  after=(join_token,), vmem=VMEM_LIMIT_BIG)
    full_wo, full_wg, full_wp = exchange_wait("join_early_wait", plan_join, join_sems[0], joining, g_wi)
    late = reduce_start("late", [g_wi])
    grad_x, acc_x = in_proj_bwd_norm(dz, w_shards, order, x2, d_x1, norm_gain, after=(late[-1],))

    wsm = max(d, cw)
    misc = jnp.concatenate([acc_qk[0:1, :HEAD_DIM], acc_qk[1:2, :HEAD_DIM], acc_sink[0:1, :nq]], axis=1)
    small = jnp.concatenate([
        _pad_cols(acc_x[0:1], wsm), _pad_cols(acc_g[0:1], wsm), _pad_cols(acc_head[0:1], wsm),
        _pad_cols(acc_head[1:2], wsm), _pad_cols(misc, wsm), _pad_cols(acc_conv[0:3], wsm),
        _pad_rows(_pad_cols(acc_head[2:3], wsm), SUBLANES)], axis=0)
    device = 2 * chip + lax.axis_index("c")
    small_sems, small_bufs, small_token = exchange_start(
        "small_start", [(plan_allgather_small, [0], 7)],
        [lax.dynamic_update_slice(jnp.zeros((8,) + small.shape, F32), small[None], (device, 0, 0))])
    last_sems, last_halves, last_token = exchange_start(
        "join_late_start", [(plan_join, [0], 1)], reduce_sum("late", late, small_token))
    big, after = {}, (last_token,)
    for nm, g, w, m, v in (("w_out", full_wo, w_out, m_w_out, v_w_out),
                           ("w_ple_gate", full_wg, w_ple_gate, m_w_ple_gate, v_w_ple_gate),
                           ("w_ple_proj", full_wp, w_ple_proj, m_w_ple_proj, v_w_ple_proj)):
        stepped = adamw(g, w[0], m[0], v[0], "adamw_" + nm, after=after)
        big[nm], after = [o[None] for o in stepped], (stepped[1],)
    full_wi, = exchange_wait("join_late_wait", plan_join, last_sems[0], last_halves, after[0])
    big["w_in"] = [o[None] for o in adamw(full_wi, w_in[0], m_w_in[0], v_w_in[0], "adamw_w_in")]

    gathered, = exchange_wait("small_wait", plan_allgather_small, small_sems[0], small_bufs, big["w_in"][1])
    tot = sum_devices(gathered)
    loss = tot[SUBLANES, 0]

    def pack(vals):
        ng, pg, bg, eg, qgv, kgv, sk, cv = vals
        misc_v = jnp.concatenate([qgv, kgv, sk], axis=1)
        return jnp.concatenate([_pad_cols(ng, wsm), _pad_cols(pg, wsm), _pad_cols(bg, wsm), _pad_cols(eg, wsm),
                                _pad_cols(misc_v, wsm), _pad_cols(cv[0], wsm)], axis=0)

    g_small = jnp.concatenate(
        [tot[0:5], _pad_cols(lax.dynamic_slice(tot[5:8], (0, chip * cq), (3, cq)), wsm)], axis=0)
    w_small = pack((norm_gain, ple_gate_norm_gain, b_ple_gate, ple_norm_gain, q_norm_gain, k_norm_gain,
                    attn_sinks, conv_w))
    m_small = pack((m_norm_gain, m_ple_gate_norm_gain, m_b_ple_gate, m_ple_norm_gain, m_q_norm_gain,
                    m_k_norm_gain, m_attn_sinks, m_conv_w))
    v_small = pack((v_norm_gain, v_ple_gate_norm_gain, v_b_ple_gate, v_ple_norm_gain, v_q_norm_gain,
                    v_k_norm_gain, v_attn_sinks, v_conv_w))
    sm = adamw(g_small, w_small, m_small, v_small, "adamw_small")

    def unpack(a):
        return {"norm_gain": a[0:1, :d], "ple_gate_norm_gain": a[1:2, :d], "b_ple_gate": a[2:3, :d],
                "ple_norm_gain": a[3:4, :d], "q_norm_gain": a[4:5, :HEAD_DIM],
                "k_norm_gain": a[4:5, HEAD_DIM:2 * HEAD_DIM],
                "attn_sinks": a[4:5, 2 * HEAD_DIM:2 * HEAD_DIM + nq], "conv_w": a[5:8, :cq][None]}

    order = ("norm_gain", "w_in", "q_norm_gain", "k_norm_gain", "attn_sinks", "conv_w", "w_out",
             "ple_gate_norm_gain", "w_ple_gate", "b_ple_gate", "w_ple_proj", "ple_norm_gain")
    outs = [loss, grad_x[None]]
    for kind in range(4):
        table = unpack(sm[kind])
        for nm in order:
            outs.append(big[nm][kind] if nm in big else table[nm])
    return tuple(outs)
```

```python
import functools

import jax
import jax.numpy as jnp
from jax import lax
from jax.experimental import pallas as pl
from jax.experimental.pallas import tpu as pltpu

F32 = jnp.float32
BF16 = jnp.bfloat16
MESH = pl.DeviceIdType.MESH

HEAD_DIM = 64
N_KV_HEADS = 4
KV_WIDTH = N_KV_HEADS * HEAD_DIM
BLOCK = 128
ROT_DIM = 16
ROPE_THETA = 500000.0
EPS = 1e-6
NEG_INF = -1e30
N_CHIPS = 4
LANES = 128
SUBLANES = 8
COLT = 512
SEG = 256
VMEM_LIMIT = 48 * 1024 * 1024
VMEM_LIMIT_BIG = 60 * 1024 * 1024

ADAM_LR = 0.001
ADAM_B1 = 0.9
ADAM_B2 = 0.999
ADAM_EPS = 1e-08
ADAM_WD = 0.01
ADAM_STEP = 10

NN = (((1,), (0,)), ((), ()))
NT = (((1,), (1,)), ((), ()))
TN = (((0,), (0,)), ((), ()))


def _call(body, **kw):
    return pl.pallas_call(body, **kw)


def _call_after(body, after, **kw):
    n_in, n_after = len(kw["in_specs"]), len(after)
    kw["in_specs"] = list(kw["in_specs"]) + [pl.BlockSpec(memory_space=pl.ANY)] * n_after

    def body_after(*refs):
        body(*refs[:n_in], *refs[n_in + n_after:])

    call = _call(body_after, **kw)
    return lambda *args: call(*args, *after)


def _params(sem, vmem=VMEM_LIMIT):
    return pltpu.CompilerParams(dimension_semantics=sem, vmem_limit_bytes=vmem)


def _tile(n, pref):
    return pref if n % pref == 0 else n


def _sigmoid(v):
    return 1.0 / (1.0 + jnp.exp(-v))


def _hbm():
    return pl.BlockSpec(memory_space=pl.ANY)


def cast_bf16(a, name):
    r, c = a.shape
    tr = _tile(r, 512)

    def body(a_ref, o_ref):
        o_ref[...] = a_ref[...].astype(BF16)

    return _call(body, name=name, grid=(r // tr,),
                 in_specs=[pl.BlockSpec((tr, c), lambda i: (i, 0))],
                 out_specs=pl.BlockSpec((tr, c), lambda i: (i, 0)),
                 out_shape=jax.ShapeDtypeStruct((r, c), BF16),
                 compiler_params=_params(("parallel",)))(a)


def cast_into_slot(a, chip, name):
    r, c = a.shape
    tr = _tile(r, 512)

    def body(chip_ref, a_ref, o_ref):
        o_ref[...] = a_ref[...].astype(BF16)

    grid_spec = pltpu.PrefetchScalarGridSpec(
        num_scalar_prefetch=1, grid=(r // tr,),
        in_specs=[pl.BlockSpec((tr, c), lambda i, chip_ref: (i, 0))],
        out_specs=pl.BlockSpec((None, tr, c), lambda i, chip_ref: (chip_ref[0], i, 0)))
    return _call(body, name=name, grid_spec=grid_spec,
                 out_shape=jax.ShapeDtypeStruct((N_CHIPS, r, c), BF16),
                 compiler_params=_params(("parallel",)))(chip, a)


def out_proj_norm(mix, wo, x, gain):
    s, d = x.shape
    tm = _tile(s, 512)

    def body(m_ref, w_ref, x_ref, g_ref, x1_ref, hg_ref):
        x1 = x_ref[...] + jnp.dot(m_ref[...], w_ref[...], preferred_element_type=F32)
        x1_ref[...] = x1
        r = lax.rsqrt(jnp.mean(x1 * x1, axis=-1, keepdims=True) + EPS)
        hg_ref[...] = ((x1 * r) * g_ref[...]).astype(BF16)

    row = pl.BlockSpec((tm, d), lambda i: (i, 0))
    return _call(body, name="out_proj_norm", grid=(s // tm,),
                 in_specs=[row, pl.BlockSpec((d, d), lambda i: (0, 0), pipeline_mode=pl.Buffered(1)), row,
                           pl.BlockSpec((1, d), lambda i: (0, 0))],
                 out_specs=[row, row],
                 out_shape=[jax.ShapeDtypeStruct((s, d), F32), jax.ShapeDtypeStruct((s, d), BF16)],
                 compiler_params=_params(("parallel",), VMEM_LIMIT_BIG))(mix, wo, x, gain)


def gate_proj_bwd_norm(d_gl, wg, xin, add, gain):
    s, d = xin.shape
    tm = _tile(s, 256)

    def body(dg_ref, w_ref, x_ref, a_ref, g_ref, dx_ref, dxb_ref, acc_ref):
        i = pl.program_id(0)
        dyv = lax.dot_general(dg_ref[...], w_ref[...], NT, preferred_element_type=F32)
        xf = x_ref[...]
        r = lax.rsqrt(jnp.mean(xf * xf, axis=-1, keepdims=True) + EPS)
        xhat = xf * r
        gd = dyv * g_ref[...]
        dx = a_ref[...] + r * (gd - xhat * jnp.mean(xhat * gd, axis=-1, keepdims=True))
        dx_ref[...] = dx
        dxb_ref[...] = dx.astype(BF16)

        @pl.when(i == 0)
        def _():
            acc_ref[...] = jnp.zeros_like(acc_ref)

        acc_ref[0:1, :] += jnp.sum(dyv * xhat, axis=0, keepdims=True)

    row = pl.BlockSpec((tm, d), lambda i: (i, 0))
    return _call(body, name="gate_proj_bwd_norm", grid=(s // tm,),
                 in_specs=[row, pl.BlockSpec((d, d), lambda i: (0, 0), pipeline_mode=pl.Buffered(1)), row, row,
                           pl.BlockSpec((1, d), lambda i: (0, 0))],
                 out_specs=[row, row, pl.BlockSpec((SUBLANES, d), lambda i: (0, 0))],
                 out_shape=[jax.ShapeDtypeStruct((s, d), F32), jax.ShapeDtypeStruct((s, d), BF16),
                            jax.ShapeDtypeStruct((SUBLANES, d), F32)],
                 compiler_params=_params(("arbitrary",), VMEM_LIMIT_BIG))(d_gl, wg, xin, add, gain)


def head_fwd_bwd(x1, hg, wg, p, wp, tgt, bias, ple_gain):
    s, d = x1.shape
    tm = _tile(s, 256)

    def body(x1_ref, hg_ref, wg_ref, p_ref, wp_ref, t_ref, b_ref, g_ref, dgl_ref, dpe_ref, dy_ref, acc_ref):
        i = pl.program_id(0)
        gl = jnp.dot(hg_ref[...], wg_ref[...], preferred_element_type=F32)
        pb = p_ref[...].astype(BF16)
        pev = jnp.concatenate([jnp.dot(pb, wp_ref[q], preferred_element_type=F32) for q in range(N_CHIPS)],
                              axis=1)
        r = lax.rsqrt(jnp.mean(pev * pev, axis=-1, keepdims=True) + EPS)
        pehat = pev * r
        gain = g_ref[...]
        e = pehat * gain
        gate = _sigmoid(gl + b_ref[...])
        diff = (x1_ref[...] + gate * e) - t_ref[...]
        dy = diff * (1.0 / d)
        dy_ref[...] = dy
        d_gl = (dy * e) * (gate * (1.0 - gate))
        dgl_ref[...] = d_gl.astype(BF16)
        d_e = dy * gate
        gd = d_e * gain
        d_pe = r * (gd - pehat * jnp.mean(pehat * gd, axis=-1, keepdims=True))
        dpe_ref[...] = d_pe.astype(BF16)

        @pl.when(i == 0)
        def _():
            acc_ref[...] = jnp.zeros_like(acc_ref)

        acc_ref[0:1, :] += jnp.sum(d_gl, axis=0, keepdims=True)
        acc_ref[1:2, :] += jnp.sum(d_e * pehat, axis=0, keepdims=True)
        acc_ref[2:3, :] += jnp.sum(diff * diff, axis=0, keepdims=True) * (0.5 / d)

    row = pl.BlockSpec((tm, d), lambda i: (i, 0))
    vec = pl.BlockSpec((1, d), lambda i: (0, 0))
    return _call(body, name="head_fwd_bwd", grid=(s // tm,),
                 in_specs=[row, row, pl.BlockSpec((d, d), lambda i: (0, 0), pipeline_mode=pl.Buffered(1)),
                           pl.BlockSpec((tm, p.shape[1]), lambda i: (i, 0)),
                           pl.BlockSpec(wp.shape, lambda i: (0, 0, 0)), row, vec, vec],
                 out_specs=[row, row, row, pl.BlockSpec((SUBLANES, d), lambda i: (0, 0))],
                 out_shape=[jax.ShapeDtypeStruct((s, d), BF16), jax.ShapeDtypeStruct((s, d), BF16),
                            jax.ShapeDtypeStruct((s, d), F32), jax.ShapeDtypeStruct((SUBLANES, d), F32)],
                 compiler_params=_params(("arbitrary",), VMEM_LIMIT_BIG))(
                     x1, hg, wg, p, wp, tgt, bias, ple_gain)


def adamw(g, w, m, v, name, after=()):
    r, c = g.shape
    tr = _tile(r, 256)

    def body(g_ref, w_ref, m_ref, v_ref, go_ref, d_ref, mo_ref, vo_ref):
        gv = g_ref[...]
        mn = ADAM_B1 * m_ref[...] + (1.0 - ADAM_B1) * gv
        vn = ADAM_B2 * v_ref[...] + (1.0 - ADAM_B2) * (gv * gv)
        m_hat = mn / (1.0 - ADAM_B1 ** ADAM_STEP)
        v_hat = vn / (1.0 - ADAM_B2 ** ADAM_STEP)
        go_ref[...] = gv
        d_ref[...] = -ADAM_LR * (m_hat / (jnp.sqrt(v_hat) + ADAM_EPS) + ADAM_WD * w_ref[...])
        mo_ref[...] = mn
        vo_ref[...] = vn

    blk = pl.BlockSpec((tr, c), lambda i: (i, 0))
    shp = jax.ShapeDtypeStruct((r, c), F32)
    return _call_after(body, after, name=name, grid=(r // tr,), in_specs=[blk] * 4, out_specs=[blk] * 4,
                       out_shape=[shp] * 4, compiler_params=_params(("parallel",)))(g, w, m, v)


def matmul(a, b, *, grid, a_spec, b_spec, o_spec, out_shape, dims, name, res=None, res_spec=None, after=(),
           vmem=VMEM_LIMIT):
    nk = grid[2]
    acc_shape = tuple(d for d in o_spec.block_shape if d is not None)

    def body(*refs):
        a_ref, b_ref = refs[:2]
        r_ref = refs[2] if res is not None else None
        o_ref = refs[3] if res is not None else refs[2]
        part = lax.dot_general(a_ref[...].astype(BF16), b_ref[...].astype(BF16), dims, preferred_element_type=F32)

        def finish(out):
            if res is not None:
                out = r_ref[...] + out
            o_ref[...] = out.astype(o_ref.dtype)

        if nk == 1:
            finish(part)
            return
        acc_ref = refs[-1]
        k = pl.program_id(2)

        @pl.when(k == 0)
        def _():
            acc_ref[...] = part

        @pl.when((k > 0) & (k < nk - 1))
        def _():
            acc_ref[...] += part

        @pl.when(k == nk - 1)
        def _():
            finish(acc_ref[...] + part)

    in_specs = [a_spec, b_spec] + ([res_spec] if res is not None else [])
    args = (a, b) + ((res,) if res is not None else ())
    return _call_after(body, after, name=name, grid=grid, in_specs=in_specs, out_specs=o_spec,
                       out_shape=out_shape, scratch_shapes=[pltpu.VMEM(acc_shape, F32)] if nk > 1 else [],
                       compiler_params=_params(("parallel", "parallel", "arbitrary"), vmem))(*args)


def norm_in_proj(x, gain, w, order, in_w):
    s, d = x.shape
    sh = w.shape[1]
    tm = _tile(s, 512)

    def body(order_ref, x_ref, g_ref, w_ref, h_ref, z_ref):
        xf = x_ref[...]
        r = lax.rsqrt(jnp.mean(xf * xf, axis=-1, keepdims=True) + EPS)
        hb = ((xf * r) * g_ref[...]).astype(BF16)
        h_ref[...] = hb
        z_ref[...] = jnp.dot(hb, w_ref[...], preferred_element_type=F32)

    grid_spec = pltpu.PrefetchScalarGridSpec(
        num_scalar_prefetch=1, grid=(s // tm,),
        in_specs=[pl.BlockSpec((tm, d), lambda i, order_ref: (i, 0)),
                  pl.BlockSpec((1, d), lambda i, order_ref: (0, 0)),
                  pl.BlockSpec((d, sh), lambda i, order_ref: (0, 0), pipeline_mode=pl.Buffered(1))],
        out_specs=[pl.BlockSpec((tm, d), lambda i, order_ref: (i, 0)),
                   pl.BlockSpec((tm, sh), lambda i, order_ref: (i, order_ref[0]))])
    return _call(body, name="norm_in_proj", grid_spec=grid_spec,
                 out_shape=[jax.ShapeDtypeStruct((s, d), BF16), jax.ShapeDtypeStruct((s, in_w), F32)],
                 compiler_params=_params(("parallel",)))(order, x, gain, w)


def in_proj_part(h, w, z_prev, order, q, in_w):
    s, d = h.shape
    sh = w.shape[1]
    tm = _tile(s, 512)

    def body(order_ref, h_ref, w_ref, *rest):
        rest[-1][...] = jnp.dot(h_ref[...], w_ref[...], preferred_element_type=F32)

    in_specs = [pl.BlockSpec((tm, d), lambda i, order_ref: (i, 0)),
                pl.BlockSpec((d, sh), lambda i, order_ref: (0, 0))]
    args = [order, h, w]
    if z_prev is not None:
        in_specs.append(_hbm())
        args.append(z_prev)
    grid_spec = pltpu.PrefetchScalarGridSpec(
        num_scalar_prefetch=1, grid=(s // tm,), in_specs=in_specs,
        out_specs=pl.BlockSpec((tm, sh), lambda i, order_ref: (i, order_ref[q])))
    return _call(body, name="mm_z%d" % q, grid_spec=grid_spec,
                 out_shape=jax.ShapeDtypeStruct((s, in_w), F32),
                 input_output_aliases={3: 0} if z_prev is not None else {},
                 compiler_params=_params(("parallel",)))(*args)


def in_proj_bwd_norm(dz, ws, order, xin, add, gain, after):
    s, d = xin.shape
    sh = ws[0].shape[1]
    tm = _tile(s, 256)
    nq, n_after = len(ws), len(after)

    def body(order_ref, *refs):
        a_refs, b_refs = refs[:nq], refs[nq:2 * nq]
        x_ref, add_ref, g_ref = refs[2 * nq:2 * nq + 3]
        dx_ref, acc_ref = refs[2 * nq + 3 + n_after:]
        i = pl.program_id(0)
        dyv = lax.dot_general(a_refs[0][...], b_refs[0][...], NT, preferred_element_type=F32)
        for q in range(1, nq):
            dyv += lax.dot_general(a_refs[q][...], b_refs[q][...], NT, preferred_element_type=F32)
        xf = x_ref[...]
        r = lax.rsqrt(jnp.mean(xf * xf, axis=-1, keepdims=True) + EPS)
        xhat = xf * r
        gd = dyv * g_ref[...]
        dx_ref[...] = add_ref[...] + r * (gd - xhat * jnp.mean(xhat * gd, axis=-1, keepdims=True))

        @pl.when(i == 0)
        def _():
            acc_ref[...] = jnp.zeros_like(acc_ref)

        acc_ref[0:1, :] += jnp.sum(dyv * xhat, axis=0, keepdims=True)

    a_spec = lambda q: pl.BlockSpec((tm, sh), functools.partial(lambda i, order_ref, q: (i, order_ref[q]), q=q))
    row = pl.BlockSpec((tm, d), lambda i, order_ref: (i, 0))
    grid_spec = pltpu.PrefetchScalarGridSpec(
        num_scalar_prefetch=1, grid=(s // tm,),
        in_specs=[a_spec(q) for q in range(nq)]
        + [pl.BlockSpec((d, sh), lambda i, order_ref: (0, 0), pipeline_mode=pl.Buffered(1))] * nq
        + [row, row, pl.BlockSpec((1, d), lambda i, order_ref: (0, 0))] + [_hbm()] * n_after,
        out_specs=[row, pl.BlockSpec((SUBLANES, d), lambda i, order_ref: (0, 0))])
    return _call(body, name="in_proj_bwd_norm", grid_spec=grid_spec,
                 out_shape=[jax.ShapeDtypeStruct((s, d), F32), jax.ShapeDtypeStruct((SUBLANES, d), F32)],
                 compiler_params=_params(("arbitrary",), VMEM_LIMIT_BIG))(
                     order, *([dz] * nq), *ws, xin, add, gain, *after)


def _seg_mean(sq, segb):
    parts = []
    for cgrp in range(sq.shape[1] // SEG):
        blk = sq[:, cgrp * SEG:(cgrp + 1) * SEG]
        hi = blk.astype(BF16)
        r1 = blk - hi.astype(F32)
        mid = r1.astype(BF16)
        lo = (r1 - mid.astype(F32)).astype(BF16)
        acc = jnp.dot(hi, segb, preferred_element_type=F32)
        acc += jnp.dot(mid, segb, preferred_element_type=F32)
        acc += jnp.dot(lo, segb, preferred_element_type=F32)
        parts.append(acc)
    out = parts[0] if len(parts) == 1 else jnp.concatenate(parts, axis=1)
    return out * (1.0 / HEAD_DIM)


def _rope(v, cos, sa, sb):
    parts = []
    for cgrp in range(v.shape[1] // LANES):
        blk = v[:, cgrp * LANES:(cgrp + 1) * LANES]
        parts.append(blk * cos + pltpu.roll(blk, LANES - 8, 1) * sa + pltpu.roll(blk, 8, 1) * sb)
    return parts[0] if len(parts) == 1 else jnp.concatenate(parts, axis=1)


def _rope_t(dv, cos, sa, sb):
    parts = []
    for cgrp in range(dv.shape[1] // LANES):
        blk = dv[:, cgrp * LANES:(cgrp + 1) * LANES]
        parts.append(blk * cos + pltpu.roll(blk * sa, 8, 1) + pltpu.roll(blk * sb, LANES - 8, 1))
    return parts[0] if len(parts) == 1 else jnp.concatenate(parts, axis=1)


def _tile_lanes(vec, width):
    reps = width // LANES
    return vec if reps == 1 else jnp.tile(vec, (1, reps))


def qk_prep_fwd(z, tabs, qg, kg, segb, aw, after=()):
    s = z.shape[0]
    tm = _tile(s, 512)
    wq = aw + 2 * KV_WIDTH

    def body(z_ref, cos_ref, sa_ref, sb_ref, qg_ref, kg_ref, seg_ref, qs_ref, ks_ref, vb_ref):
        zz = z_ref[...]
        q, k, v = zz[:, :aw], zz[:, aw:aw + KV_WIDTH], zz[:, aw + KV_WIDTH:]
        cos, sa, sb, segm = cos_ref[...], sa_ref[...], sb_ref[...], seg_ref[...]
        rq = lax.rsqrt(_seg_mean(q * q, segm) + EPS)
        qn = (q * rq) * _tile_lanes(qg_ref[...], aw)
        qs_ref[...] = (_rope(qn, cos, sa, sb) * (HEAD_DIM ** -0.5)).astype(BF16)
        rk = lax.rsqrt(_seg_mean(k * k, segm) + EPS)
        kn = (k * rk) * _tile_lanes(kg_ref[...], KV_WIDTH)
        ks_ref[...] = _rope(kn, cos, sa, sb).astype(BF16)
        vb_ref[...] = v.astype(BF16)

    tab = pl.BlockSpec((tm, LANES), lambda i: (i, 0))
    vec = pl.BlockSpec((1, LANES), lambda i: (0, 0))
    return _call_after(body, after, name="qk_prep_fwd", grid=(s // tm,),
                       in_specs=[pl.BlockSpec((tm, wq), lambda i: (i, 0)), tab, tab, tab, vec, vec,
                                 pl.BlockSpec((SEG, SEG), lambda i: (0, 0))],
                       out_specs=[pl.BlockSpec((tm, aw), lambda i: (i, 0)),
                                  pl.BlockSpec((tm, KV_WIDTH), lambda i: (i, 0)),
                                  pl.BlockSpec((tm, KV_WIDTH), lambda i: (i, 0))],
                       out_shape=[jax.ShapeDtypeStruct((s, aw), BF16), jax.ShapeDtypeStruct((s, KV_WIDTH), BF16),
                                  jax.ShapeDtypeStruct((s, KV_WIDTH), BF16)],
                       compiler_params=_params(("parallel",)))(z, *tabs, qg, kg, segb)


def qk_prep_bwd(z, dqs, dks, dvs, d_gate, tabs, qg, kg, segb, aw):
    s, in_w = z.shape
    tm = _tile(s, 512)
    wq = aw + 2 * KV_WIDTH

    def body(z_ref, dq_ref, dk_ref, dv_ref, dga_ref, cos_ref, sa_ref, sb_ref, qg_ref, kg_ref, seg_ref,
             dz_ref, acc_ref):
        i = pl.program_id(0)
        zz = z_ref[...]
        q, k = zz[:, :aw], zz[:, aw:aw + KV_WIDTH]
        cos, sa, sb, segm = cos_ref[...], sa_ref[...], sb_ref[...], seg_ref[...]

        def one(xv, dout, gvec, width):
            g = _tile_lanes(gvec, width)
            r = lax.rsqrt(_seg_mean(xv * xv, segm) + EPS)
            xhat = xv * r
            dn = _rope_t(dout, cos, sa, sb)
            gd = dn * g
            dx = r * (gd - xhat * _seg_mean(xhat * gd, segm))
            contrib = jnp.sum(dn * xhat, axis=0, keepdims=True)
            folded = contrib[:, :LANES]
            for cgrp in range(1, width // LANES):
                folded = folded + contrib[:, cgrp * LANES:(cgrp + 1) * LANES]
            return dx, folded + pltpu.roll(folded, HEAD_DIM, 1)

        dq, gq = one(q, dq_ref[...] * (HEAD_DIM ** -0.5), qg_ref[...], aw)
        dk, gk = one(k, dk_ref[...], kg_ref[...], KV_WIDTH)
        dz_ref[:, :aw] = dq.astype(BF16)
        dz_ref[:, aw:aw + KV_WIDTH] = dk.astype(BF16)
        dz_ref[:, aw + KV_WIDTH:wq] = dv_ref[...].astype(BF16)
        dz_ref[:, wq:] = dga_ref[...]

        @pl.when(i == 0)
        def _():
            acc_ref[...] = jnp.zeros_like(acc_ref)

        acc_ref[0:1, :] += gq
        acc_ref[1:2, :] += gk

    tab = pl.BlockSpec((tm, LANES), lambda i: (i, 0))
    vec = pl.BlockSpec((1, LANES), lambda i: (0, 0))
    kvb = pl.BlockSpec((tm, KV_WIDTH), lambda i: (i, 0))
    awb = pl.BlockSpec((tm, aw), lambda i: (i, 0))
    return _call(body, name="qk_prep_bwd", grid=(s // tm,),
                 in_specs=[pl.BlockSpec((tm, wq), lambda i: (i, 0)), awb, kvb, kvb, awb, tab, tab, tab, vec, vec,
                           pl.BlockSpec((SEG, SEG), lambda i: (0, 0))],
                 out_specs=[pl.BlockSpec((tm, wq + aw), lambda i: (i, 0)),
                            pl.BlockSpec((SUBLANES, LANES), lambda i: (0, 0))],
                 out_shape=[jax.ShapeDtypeStruct((s, in_w), BF16), jax.ShapeDtypeStruct((SUBLANES, LANES), F32)],
                 compiler_params=_params(("arbitrary",)))(z, dqs, dks, dvs, d_gate, *tabs, qg, kg, segb)


def _placed(band, lane_idx):
    out = {}
    for kh in range(N_KV_HEADS):
        grp = band[:, (kh // 2) * LANES:(kh // 2 + 1) * LANES]
        for half in (0, 1):
            t = grp if half == kh % 2 else pltpu.roll(grp, HEAD_DIM, 1)
            keep = (lane_idx >= half * HEAD_DIM) & (lane_idx < (half + 1) * HEAD_DIM)
            out[kh, half] = jnp.where(keep, t, jnp.zeros_like(t))
    return out


def _softmax_with_sink(sc, valid, sink):
    sc = jnp.where(valid, sc, NEG_INF)
    m = jnp.maximum(jnp.max(sc, axis=-1, keepdims=True), sink)
    e = jnp.exp(sc - m)
    es = jnp.exp(sink - m)
    den = jnp.sum(e, axis=-1, keepdims=True) + es
    return e / den, es / den


def _stack_halves(placed, kh):
    return jnp.concatenate([placed[kh, 0], placed[kh, 1]], axis=0)


def _valid_mask(n):
    r_i = lax.broadcasted_iota(jnp.int32, (BLOCK, 2 * BLOCK), 0)
    j_i = lax.broadcasted_iota(jnp.int32, (BLOCK, 2 * BLOCK), 1)
    return (j_i > r_i) & (j_i <= r_i + BLOCK) & ((n > 0) | (j_i >= BLOCK))


def attn_fwd(qs, ks, vb, z, sinks, aw, d, ga_off):
    s = qs.shape[0]
    nb = s // BLOCK
    nq = aw // HEAD_DIM
    grp_sz = nq // N_KV_HEADS
    n_ga = aw // COLT

    def body(q_ref, ko_ref, kp_ref, vo_ref, vp_ref, *rest):
        ga_refs = rest[:n_ga]
        sink_ref, attn_ref, mix_ref = rest[n_ga:]
        n = pl.program_id(0)
        lane_idx = lax.broadcasted_iota(jnp.int32, (2 * BLOCK, LANES), 1)
        kpl = _placed(jnp.concatenate([kp_ref[...], ko_ref[...]], axis=0), lane_idx)
        vpl = _placed(jnp.concatenate([vp_ref[...], vo_ref[...]], axis=0), lane_idx)
        valid = _valid_mask(n)
        groups = range(nq // 2)
        kcat = [_stack_halves(kpl, kh) for kh in range(N_KV_HEADS)]
        vcat = [_stack_halves(vpl, kh) for kh in range(N_KV_HEADS)]
        scs = [lax.dot_general(q_ref[:, c * LANES:(c + 1) * LANES], kcat[2 * c // grp_sz], NT,
                               preferred_element_type=F32) for c in groups]
        probs = [jnp.concatenate(
            [_softmax_with_sink(scs[c][:, half * 2 * BLOCK:(half + 1) * 2 * BLOCK], valid,
                                sink_ref[0, 2 * c + half])[0].astype(BF16) for half in (0, 1)], axis=1)
                 for c in groups]
        for c in groups:
            acc = jnp.dot(probs[c], vcat[2 * c // grp_sz], preferred_element_type=F32)
            attn_ref[:, c * LANES:(c + 1) * LANES] = acc
            col = c * LANES
            ga = ga_refs[col // COLT][:, col % COLT:col % COLT + LANES]
            mix_ref[:, c * LANES:(c + 1) * LANES] = (acc * (ga * _sigmoid(ga))).astype(BF16)

    own = lambda n: (n, 0)
    prev = lambda n: (jnp.maximum(n - 1, 0), 0)
    kvs = lambda imap: pl.BlockSpec((BLOCK, KV_WIDTH), imap)
    ga_specs = [pl.BlockSpec((BLOCK, COLT), functools.partial(lambda n, j: (n, ga_off + j), j=j))
                for j in range(n_ga)]
    return _call(body, name="attn_fwd", grid=(nb,),
                 in_specs=[pl.BlockSpec((BLOCK, aw), own), kvs(own), kvs(prev), kvs(own), kvs(prev)]
                 + ga_specs + [pl.BlockSpec(memory_space=pltpu.SMEM)],
                 out_specs=[pl.BlockSpec((BLOCK, aw), own), pl.BlockSpec((BLOCK, aw), own)],
                 out_shape=[jax.ShapeDtypeStruct((s, aw), F32), jax.ShapeDtypeStruct((s, d), BF16)],
                 compiler_params=_params(("parallel",)))(qs, ks, ks, vb, vb, *([z] * n_ga), sinks)


def out_proj_bwd_gate(d_x1b, wo, attn, z, aw, ga_off, after=()):
    s, d = d_x1b.shape
    tm = _tile(s, 256)
    n_ga = aw // COLT

    def body(dx_ref, w_ref, at_ref, *rest):
        ga_refs, (do_ref, dga_ref, dmc_ref) = rest[:n_ga], rest[n_ga:]
        dm = lax.dot_general(dx_ref[...], w_ref[...], NT, preferred_element_type=F32)
        dmc_ref[...] = dm[:, aw:]
        for j in range(n_ga):
            cols = slice(j * COLT, (j + 1) * COLT)
            ga = ga_refs[j][...]
            sig = _sigmoid(ga)
            dma = dm[:, cols]
            do_ref[:, cols] = (dma * (ga * sig)).astype(BF16)
            dga_ref[:, cols] = ((dma * at_ref[:, cols]) * (sig * (1.0 + ga * (1.0 - sig)))).astype(BF16)

    row = lambda width: pl.BlockSpec((tm, width), lambda i: (i, 0))
    ga_specs = [pl.BlockSpec((tm, COLT), functools.partial(lambda i, j: (i, ga_off + j), j=j)) for j in range(n_ga)]
    return _call_after(body, after, name="out_proj_bwd_gate", grid=(s // tm,),
                       in_specs=[row(d), pl.BlockSpec((d, d), lambda i: (0, 0), pipeline_mode=pl.Buffered(1)),
                                 row(aw)] + ga_specs,
                       out_specs=[row(aw), row(aw), row(d - aw)],
                       out_shape=[jax.ShapeDtypeStruct((s, aw), BF16), jax.ShapeDtypeStruct((s, aw), BF16),
                                  jax.ShapeDtypeStruct((s, d - aw), F32)],
                       compiler_params=_params(("parallel",)))(d_x1b, wo, attn, *([z] * n_ga))


def attn_bwd(qs, ks, vb, d_o, sinks, aw):
    s = qs.shape[0]
    nb = s // BLOCK
    nq = aw // HEAD_DIM
    grp_sz = nq // N_KV_HEADS

    def body(q_ref, ko_ref, kp_ref, vo_ref, vp_ref, do_ref, sink_ref, dq_ref, dk_ref, dv_ref, ds_ref,
             ck_ref, cv_ref):
        n = pl.program_id(0)

        @pl.when(n == 0)
        def _():
            ds_ref[...] = jnp.zeros_like(ds_ref)
            dk_ref[...] = jnp.zeros_like(dk_ref)
            dv_ref[...] = jnp.zeros_like(dv_ref)

        @pl.when(n < nb)
        def _():
            lane_idx = lax.broadcasted_iota(jnp.int32, (2 * BLOCK, LANES), 1)
            lane_row = lax.broadcasted_iota(jnp.int32, (1, LANES), 1)
            kpl = _placed(jnp.concatenate([kp_ref[...], ko_ref[...]], axis=0), lane_idx)
            vpl = _placed(jnp.concatenate([vp_ref[...], vo_ref[...]], axis=0), lane_idx)
            valid = _valid_mask(n)
            ds_row = jnp.zeros((1, LANES), F32)
            wide = 2 * BLOCK
            groups = range(nq // 2)
            per_kv = grp_sz // 2
            kcat = [_stack_halves(kpl, kh) for kh in range(N_KV_HEADS)]
            vcat = [_stack_halves(vpl, kh) for kh in range(N_KV_HEADS)]
            qg = [q_ref[:, c * LANES:(c + 1) * LANES] for c in groups]
            dog = [do_ref[:, c * LANES:(c + 1) * LANES] for c in groups]
            scs = [lax.dot_general(qg[c], kcat[c // per_kv], NT, preferred_element_type=F32) for c in groups]
            dps = [lax.dot_general(dog[c], vcat[c // per_kv], NT, preferred_element_type=F32) for c in groups]
            ds_pairs, p_pairs = [], []
            for c in groups:
                probs, dss = [], []
                for half in (0, 1):
                    h = 2 * c + half
                    cols = slice(half * wide, (half + 1) * wide)
                    p, p_sink = _softmax_with_sink(scs[c][:, cols], valid, sink_ref[0, h])
                    delta = jnp.sum(p * dps[c][:, cols], axis=-1, keepdims=True)
                    dss.append((p * (dps[c][:, cols] - delta)).astype(BF16))
                    probs.append(p.astype(BF16))
                    dsink = -jnp.sum(p_sink * delta, axis=0, keepdims=True)
                    ds_row += jnp.where(lane_row == h, dsink, 0.0)
                ds_pairs.append(jnp.concatenate(dss, axis=1))
                p_pairs.append(jnp.concatenate(probs, axis=1))
            for c in groups:
                dq_ref[:, c * LANES:(c + 1) * LANES] = jnp.dot(ds_pairs[c], kcat[c // per_kv],
                                                               preferred_element_type=F32)
            dkts = [lax.dot_general(qg[c], ds_pairs[c], TN, preferred_element_type=F32) for c in groups]
            dvts = [lax.dot_general(dog[c], p_pairs[c], TN, preferred_element_type=F32) for c in groups]
            dkt, dvt = [], []
            for kh in range(N_KV_HEADS):
                for parts, out in ((dkts, dkt), (dvts, dvt)):
                    tot = parts[kh * per_kv]
                    for c in range(kh * per_kv + 1, (kh + 1) * per_kv):
                        tot = tot + parts[c]
                    out.append(tot[:HEAD_DIM, :wide] + tot[HEAD_DIM:, wide:])
            ds_ref[0:1, :] += ds_row
            back = lambda t: jnp.concatenate(
                [jnp.concatenate(t[2 * g:2 * g + 2], axis=0).T for g in range(N_KV_HEADS // 2)], axis=1)
            dk_band, dv_band = back(dkt), back(dvt)

            @pl.when(n > 0)
            def _():
                dk_ref[...] = ck_ref[...] + dk_band[:BLOCK]
                dv_ref[...] = cv_ref[...] + dv_band[:BLOCK]

            ck_ref[...] = dk_band[BLOCK:]
            cv_ref[...] = dv_band[BLOCK:]

        @pl.when(n == nb)
        def _():
            dk_ref[...] = ck_ref[...]
            dv_ref[...] = cv_ref[...]

    own = lambda n: (jnp.minimum(n, nb - 1), 0)
    prev = lambda n: (jnp.clip(n - 1, 0, nb - 1), 0)
    done = lambda n: (jnp.maximum(n - 1, 0), 0)
    kvs = lambda imap: pl.BlockSpec((BLOCK, KV_WIDTH), imap)
    return _call(body, name="attn_bwd", grid=(nb + 1,),
                 in_specs=[pl.BlockSpec((BLOCK, aw), own), kvs(own), kvs(prev), kvs(own), kvs(prev),
                           pl.BlockSpec((BLOCK, aw), own), pl.BlockSpec(memory_space=pltpu.SMEM)],
                 out_specs=[pl.BlockSpec((BLOCK, aw), own), kvs(done), kvs(done),
                            pl.BlockSpec((SUBLANES, LANES), lambda n: (0, 0))],
                 out_shape=[jax.ShapeDtypeStruct((s, aw), F32), jax.ShapeDtypeStruct((s, KV_WIDTH), F32),
                            jax.ShapeDtypeStruct((s, KV_WIDTH), F32), jax.ShapeDtypeStruct((SUBLANES, LANES), F32)],
                 scratch_shapes=[pltpu.VMEM((BLOCK, KV_WIDTH), F32), pltpu.VMEM((BLOCK, KV_WIDTH), F32)],
                 compiler_params=_params(("arbitrary",)))(qs, ks, ks, vb, vb, d_o, sinks)


def conv_fwd(z, conv_w, mix, aw, cw, b_off):
    s = z.shape[0]
    tm = _tile(s, 1024)
    nseg = cw // COLT
    hb = tm // SUBLANES

    def body(b_ref, c_ref, h_ref, g_ref, cp_ref, hp_ref, w_ref, mix_in, mix_ref):
        i = pl.program_id(0)
        u = c_ref[...] * h_ref[...]
        up = jnp.where(i > 0, cp_ref[...] * hp_ref[...], 0.0)
        ext = jnp.concatenate([up, u], axis=0)
        um1 = pltpu.roll(ext, 1, 0)[SUBLANES:]
        um2 = pltpu.roll(ext, 2, 0)[SUBLANES:]
        w = w_ref[...]
        cv = w[0:1] * um2 + w[1:2] * um1 + w[2:3] * u
        g = g_ref[...]
        mix_ref[...] = ((b_ref[...] * cv) * (g * _sigmoid(g))).astype(BF16)

    seg = lambda k: pl.BlockSpec((tm, COLT), functools.partial(lambda i, j, k: (i, b_off + k * nseg + j), k=k))
    halo = lambda k: pl.BlockSpec(
        (SUBLANES, COLT), functools.partial(lambda i, j, k: (jnp.maximum(i * hb - 1, 0), b_off + k * nseg + j), k=k))
    return _call(body, name="conv_fwd", grid=(s // tm, nseg),
                 in_specs=[seg(0), seg(1), seg(2), seg(3), halo(1), halo(2),
                           pl.BlockSpec((SUBLANES, COLT), lambda i, j: (0, j)), _hbm()],
                 out_specs=pl.BlockSpec((tm, COLT), lambda i, j: (i, aw // COLT + j)),
                 out_shape=jax.ShapeDtypeStruct(mix.shape, BF16),
                 input_output_aliases={7: 0},
                 compiler_params=_params(("parallel", "parallel")))(z, z, z, z, z, z, conv_w, mix)


def conv_bwd(z, d_mix, conv_w, dz, aw, cw, b_off):
    s = z.shape[0]
    tm = _tile(s, 512)
    nseg = cw // COLT
    hb = tm // SUBLANES
    n_row = s // tm
    last_h = s // SUBLANES - 1
    n_step = nseg * n_row

    def body(b_ref, c_ref, h_ref, g_ref, cp_ref, hp_ref, bn_ref, gn_ref, dm_ref, dmn_ref, w_ref, dz_in,
             dz_ref, acc_ref, stash_ref, sems):
        j = pl.program_id(0)
        i = pl.program_id(1)
        step = j * n_row + i
        slot = step % 2

        def copies(from_slot, at_step):
            jj, ii = at_step // n_row, at_step % n_row
            rows = pl.ds(pl.multiple_of(ii * tm, tm), tm)
            return [pltpu.make_async_copy(
                stash_ref.at[from_slot, q],
                dz_ref.at[rows, pl.ds(pl.multiple_of((b_off + q * nseg + jj) * COLT, COLT), COLT)],
                sems.at[from_slot, q]) for q in range(4)]

        @pl.when(step >= 2)
        def _():
            for cp in copies(slot, step - 2):
                cp.wait()

        cc, hh, bb, g = c_ref[...], h_ref[...], b_ref[...], g_ref[...]
        w = w_ref[...]
        u = cc * hh
        up = jnp.where(i > 0, cp_ref[...] * hp_ref[...], 0.0)
        ext = jnp.concatenate([up, u], axis=0)
        um1 = pltpu.roll(ext, 1, 0)[SUBLANES:]
        um2 = pltpu.roll(ext, 2, 0)[SUBLANES:]
        cv = w[0:1] * um2 + w[1:2] * um1 + w[2:3] * u
        sig = _sigmoid(g)
        sg = g * sig
        dm = dm_ref[...]
        d_cv = (dm * bb) * sg
        gn = gn_ref[...]
        d_cv_next = jnp.where(i < n_row - 1, (dmn_ref[...] * bn_ref[...]) * (gn * _sigmoid(gn)), 0.0)
        ext2 = jnp.concatenate([d_cv, d_cv_next], axis=0)
        dp1 = pltpu.roll(ext2, tm + SUBLANES - 1, 0)[:tm]
        dp2 = pltpu.roll(ext2, tm + SUBLANES - 2, 0)[:tm]
        d_u = w[2:3] * d_cv + w[1:2] * dp1 + w[0:1] * dp2
        stash_ref[slot, 0] = ((dm * cv) * sg).astype(BF16)
        stash_ref[slot, 1] = (d_u * hh).astype(BF16)
        stash_ref[slot, 2] = (d_u * cc).astype(BF16)
        stash_ref[slot, 3] = (((dm * bb) * cv) * (sig * (1.0 + g * (1.0 - sig)))).astype(BF16)
        for cp in copies(slot, step):
            cp.start()

        @pl.when(i == 0)
        def _():
            acc_ref[...] = jnp.zeros_like(acc_ref)

        acc_ref[0:1, :] += jnp.sum(d_cv * um2, axis=0, keepdims=True)
        acc_ref[1:2, :] += jnp.sum(d_cv * um1, axis=0, keepdims=True)
        acc_ref[2:3, :] += jnp.sum(d_cv * u, axis=0, keepdims=True)

        @pl.when(step == n_step - 1)
        def _():
            if n_step >= 2:
                for cp in copies(1 - slot, step - 1):
                    cp.wait()
            for cp in copies(slot, step):
                cp.wait()

    seg = lambda q: pl.BlockSpec((tm, COLT), functools.partial(lambda j, i, q: (i, b_off + q * nseg + j), q=q))
    halo_p = lambda q: pl.BlockSpec(
        (SUBLANES, COLT),
        functools.partial(lambda j, i, q: (jnp.maximum(i * hb - 1, 0), b_off + q * nseg + j), q=q))
    halo_n = lambda q: pl.BlockSpec(
        (SUBLANES, COLT),
        functools.partial(lambda j, i, q: (jnp.minimum((i + 1) * hb, last_h), b_off + q * nseg + j), q=q))
    return _call(body, name="conv_bwd", grid=(nseg, n_row),
                 in_specs=[seg(0), seg(1), seg(2), seg(3), halo_p(1), halo_p(2), halo_n(0), halo_n(3),
                           pl.BlockSpec((tm, COLT), lambda j, i: (i, j)),
                           pl.BlockSpec((SUBLANES, COLT), lambda j, i: (jnp.minimum((i + 1) * hb, last_h), j)),
                           pl.BlockSpec((SUBLANES, COLT), lambda j, i: (0, j)), _hbm()],
                 out_specs=[_hbm(), pl.BlockSpec((SUBLANES, COLT), lambda j, i: (0, j))],
                 out_shape=[jax.ShapeDtypeStruct(dz.shape, BF16), jax.ShapeDtypeStruct((SUBLANES, cw), F32)],
                 input_output_aliases={11: 0},
                 scratch_shapes=[pltpu.VMEM((2, 4, tm, COLT), BF16), pltpu.SemaphoreType.DMA((2, 4))],
                 compiler_params=_params(("arbitrary", "arbitrary")))(
                     z, z, z, z, z, z, z, z, d_mix, d_mix, conv_w, dz)


def _place():
    x, y, c = lax.axis_index("x"), lax.axis_index("y"), lax.axis_index("c")
    chips = [(1 - x, y), (x, 1 - y), (1 - x, 1 - y)]
    return x, y, c, chips


def _remote(src, dst, send_sem, recv_sem, device):
    return pltpu.make_async_remote_copy(src_ref=src, dst_ref=dst, send_sem=send_sem, recv_sem=recv_sem,
                                        device_id=device, device_id_type=MESH)


def plan_gather_ici(n_split):
    def plan(refs, send, recv):
        x, y, c, chips = _place()
        me = 2 * x + y
        mine, theirs = [], []
        for w, ref in enumerate(refs):
            for j, (cx, cy) in enumerate(chips):
                def part(slot):
                    if w >= n_split:
                        return ref.at[slot]
                    hr = ref.shape[1] // 2
                    return ref.at[slot, pl.ds(c * hr, hr)]
                k = 3 * w + j
                mine.append(_remote(part(me), part(me), send.at[k], recv.at[k], (cx, cy, c)))
                theirs.append(_remote(part(2 * cx + cy), part(2 * cx + cy), send.at[k], recv.at[k], (cx, cy, c)))
        return mine, theirs
    return plan


def plan_shard_ici(j):
    def plan(refs, send, recv):
        _, _, c, chips = _place()
        own, land = refs
        hr = own.shape[0] // 2
        rows = pl.ds(c * hr, hr)
        cp = _remote(own.at[rows], land.at[rows], send.at[0], recv.at[0], (*chips[j], c))
        return [cp], [cp]
    return plan


def plan_shard_relay(refs, send, recv):
    _, _, c, chips = _place()
    land_x, land_y, land_far = refs
    hr = land_far.shape[0] // 2
    quarter = lambda q: pl.ds(c * hr + q * (hr // 2), hr // 2)
    mine = [_remote(land_y.at[quarter(0)], land_far.at[quarter(0)], send.at[0], recv.at[0], (*chips[0], c)),
            _remote(land_x.at[quarter(1)], land_far.at[quarter(1)], send.at[1], recv.at[1], (*chips[1], c))]
    return mine, mine


def plan_shard_pass(refs, send, recv):
    x, y, c, _ = _place()
    mine, theirs = [], []
    for w, land in enumerate(refs):
        hr = land.shape[0] // 2
        half = lambda core: land.at[pl.ds(core * hr, hr)]
        mine.append(_remote(half(c), half(c), send.at[w], recv.at[w], (x, y, 1 - c)))
        theirs.append(_remote(half(1 - c), half(1 - c), send.at[w], recv.at[w], (x, y, 1 - c)))
    return mine, theirs


def plan_gather_pass(refs, send, recv):
    x, y, c, chips = _place()
    mine, theirs = [], []
    for w, ref in enumerate(refs):
        hr = ref.shape[1] // 2
        for j, (cx, cy) in enumerate(chips):
            half = lambda core: ref.at[2 * cx + cy, pl.ds(core * hr, hr)]
            k = 3 * w + j
            mine.append(_remote(half(c), half(c), send.at[k], recv.at[k], (x, y, 1 - c)))
            theirs.append(_remote(half(1 - c), half(1 - c), send.at[k], recv.at[k], (x, y, 1 - c)))
    return mine, theirs


def plan_swap(refs, send, recv):
    x, y, c, _ = _place()
    nw = len(refs) // 2
    mine = []
    for w in range(nw):
        hr = refs[w].shape[1] // 2
        mine.append(_remote(refs[w].at[:, pl.ds((1 - c) * hr, hr), :], refs[nw + w], send.at[w], recv.at[w],
                            (x, y, 1 - c)))
    return mine, mine


def plan_scatter(refs, send, recv):
    x, y, c, chips = _place()
    me = 2 * x + y
    nw = len(refs) // 2
    mine, theirs = [], []
    for w in range(nw):
        for j, (cx, cy) in enumerate(chips):
            k = 3 * w + j
            mine.append(_remote(refs[w].at[2 * cx + cy], refs[nw + w].at[me], send.at[k], recv.at[k], (cx, cy, c)))
            theirs.append(_remote(refs[w].at[me], refs[nw + w].at[2 * cx + cy], send.at[k], recv.at[k], (cx, cy, c)))
    return mine, theirs


def plan_join(refs, send, recv):
    x, y, c, _ = _place()
    mine, theirs = [], []
    for w, ref in enumerate(refs):
        hr = ref.shape[0] // 2
        half = lambda core: ref.at[pl.ds(core * hr, hr)]
        mine.append(_remote(half(c), half(c), send.at[w], recv.at[w], (x, y, 1 - c)))
        theirs.append(_remote(half(1 - c), half(1 - c), send.at[w], recv.at[w], (x, y, 1 - c)))
    return mine, theirs


def exchange(name, plan, arrays, n, after=()):
    na = len(arrays)

    def body(*refs):
        mine, theirs = plan(refs[:na], refs[2 * na], refs[2 * na + 1])
        for cp in mine:
            cp.start()
        for cp in theirs:
            cp.wait_recv()
        for cp in mine:
            cp.wait_send()

    return _call_after(body, after, name=name, in_specs=[_hbm()] * na, out_specs=[_hbm()] * na,
                       out_shape=[jax.ShapeDtypeStruct(a.shape, a.dtype) for a in arrays],
                       input_output_aliases={i: i for i in range(na)},
                       scratch_shapes=[pltpu.SemaphoreType.DMA((n,)), pltpu.SemaphoreType.DMA((n,))])(*arrays)


def exchange_start(name, groups, arrays, after=()):
    na, ng = len(arrays), len(groups)

    def body(*refs):
        for g, (plan, idx, _) in enumerate(groups):
            mine, _ = plan([refs[i] for i in idx], refs[na + 2 * g], refs[na + 2 * g + 1])
            for cp in mine:
                cp.start()
        refs[-1][...] = jnp.zeros_like(refs[-1])

    hbm = pl.BlockSpec(memory_space=pltpu.HBM)
    sem = pl.BlockSpec(memory_space=pltpu.SEMAPHORE)
    sem_types = [pltpu.SemaphoreType.DMA((n,)) for _, _, n in groups for _ in (0, 1)]
    outs = _call_after(body, after, name=name, in_specs=[hbm] * na,
                       out_specs=[sem] * (2 * ng) + [hbm] * na + [pl.BlockSpec(memory_space=pltpu.VMEM)],
                       out_shape=sem_types + [pltpu.HBM(a.shape, a.dtype) for a in arrays]
                       + [jax.ShapeDtypeStruct((SUBLANES, LANES), F32)],
                       input_output_aliases={i: i + 2 * ng for i in range(na)},
                       compiler_params=pltpu.CompilerParams(
                           has_side_effects=pltpu.SideEffectType.DATAFLOW_SIDE_EFFECTING))(
                               *[pltpu.with_memory_space_constraint(a, pltpu.HBM) for a in arrays])
    sems = [(outs[2 * g], outs[2 * g + 1]) for g in range(ng)]
    return sems, list(outs[2 * ng:-1]), outs[-1]


def exchange_wait(name, plan, sems, arrays, after):
    send_sems, recv_sems = sems
    na = len(arrays)
    after = tuple(after) if isinstance(after, (tuple, list)) else (after,)

    def body(*refs):
        mine, theirs = plan(refs[:na], refs[na], refs[na + 1])
        for cp in mine:
            cp.wait_send()
        for cp in theirs:
            cp.wait_recv()

    hbm = pl.BlockSpec(memory_space=pltpu.HBM)
    sem = pl.BlockSpec(memory_space=pltpu.SEMAPHORE)
    return _call(body, name=name, in_specs=[hbm] * na + [sem, sem] + [_hbm()] * len(after),
                 out_specs=[hbm] * na, out_shape=[pltpu.HBM(a.shape, a.dtype) for a in arrays],
                 input_output_aliases={i: i for i in range(na)},
                 compiler_params=pltpu.CompilerParams(
                     has_side_effects=pltpu.SideEffectType.DATAFLOW_SIDE_EFFECTING))(
                         *arrays, send_sems, recv_sems, *after)


def plan_allgather_small(refs, send, recv):
    x, y, c, _ = _place()
    buf, = refs
    mine, theirs = [], []
    flips = [(fx, fy, fc) for fx in (0, 1) for fy in (0, 1) for fc in (0, 1)][1:]
    for k, (fx, fy, fc) in enumerate(flips):
        peer = (x ^ fx, y ^ fy, c ^ fc)
        slot = lambda px, py, pc: buf.at[4 * px + 2 * py + pc]
        mine.append(_remote(slot(x, y, c), slot(x, y, c), send.at[k], recv.at[k], peer))
        theirs.append(_remote(slot(*peer), slot(*peer), send.at[k], recv.at[k], peer))
    return mine, theirs


def add_sibling(grad, got, core, name):
    _, r, c = grad.shape
    hr = r // 2
    tr = _tile(hr, 512)
    nblk = hr // tr

    def body(core_ref, g_ref, o_ref, out_ref):
        out_ref[...] = (g_ref[...].astype(F32) + o_ref[...].astype(F32)).astype(BF16)

    grid_spec = pltpu.PrefetchScalarGridSpec(
        num_scalar_prefetch=1, grid=(N_CHIPS, nblk),
        in_specs=[pl.BlockSpec((None, tr, c), lambda t, i, core_ref: (t, core_ref[0] * nblk + i, 0)),
                  pl.BlockSpec((None, tr, c), lambda t, i, core_ref: (t, i, 0))],
        out_specs=pl.BlockSpec((None, tr, c), lambda t, i, core_ref: (t, i, 0)))
    return _call(body, name=name, grid_spec=grid_spec,
                 out_shape=jax.ShapeDtypeStruct((N_CHIPS, hr, c), BF16),
                 compiler_params=_params(("parallel", "parallel")))(core, grad, got)


def sum_chips(mine, owned, place, name):
    _, hr, c = mine.shape
    tr = _tile(hr, 512)
    nblk = hr // tr

    def body(place_ref, m_ref, o1_ref, o2_ref, o3_ref, out_ref):
        acc = m_ref[...].astype(F32)
        for o_ref in (o1_ref, o2_ref, o3_ref):
            acc = acc + o_ref[...].astype(F32)
        out_ref[...] = acc

    other = lambda k: pl.BlockSpec((None, tr, c), lambda i, place_ref: ((place_ref[0] + k) % N_CHIPS, i, 0))
    grid_spec = pltpu.PrefetchScalarGridSpec(
        num_scalar_prefetch=1, grid=(nblk,),
        in_specs=[other(0), other(1), other(2), other(3)],
        out_specs=pl.BlockSpec((tr, c), lambda i, place_ref: (place_ref[1] * nblk + i, 0)))
    return _call(body, name=name, grid_spec=grid_spec,
                 out_shape=jax.ShapeDtypeStruct((2 * hr, c), F32),
                 compiler_params=_params(("parallel",)))(place, mine, owned, owned, owned)


def sum_devices(gathered):
    _, rows, width = gathered.shape

    def body(g_ref, out_ref):
        acc = g_ref[0]
        for dev in range(1, 8):
            acc = acc + g_ref[dev]
        out_ref[...] = acc
        tail = acc[SUBLANES:]
        out_ref[SUBLANES:, :] = jnp.broadcast_to(jnp.sum(tail, axis=1, keepdims=True), tail.shape)

    return _call(body, name="sum_devices",
                 in_specs=[pl.BlockSpec(memory_space=pltpu.VMEM)],
                 out_specs=pl.BlockSpec(memory_space=pltpu.VMEM),
                 out_shape=jax.ShapeDtypeStruct((rows, width), F32))(gathered)


def _rope_tables(s):
    half = ROT_DIM // 2
    inv_freq = jnp.power(jnp.float32(ROPE_THETA), -jnp.arange(half, dtype=F32) * 2.0 / ROT_DIM)
    freq64 = jnp.concatenate([inv_freq, inv_freq, jnp.zeros((HEAD_DIM - ROT_DIM,), F32)])
    freq = jnp.concatenate([freq64, freq64])[None, :]
    dim = (jnp.arange(LANES) % HEAD_DIM)[None, :]
    ang = jnp.arange(s).astype(F32)[:, None] * freq
    cos, sin = jnp.cos(ang), jnp.sin(ang)
    return cos, jnp.where(dim < half, -sin, 0.0), jnp.where((dim >= half) & (dim < ROT_DIM), sin, 0.0)


def _pad_rows(a, rows):
    return jnp.pad(a, ((0, rows - a.shape[0]), (0, 0)))


def _pad_cols(a, cols):
    return jnp.pad(a, ((0, 0), (0, cols - a.shape[1])))


def kernel(x, p, norm_gain, w_in, q_norm_gain, k_norm_gain, attn_sinks, conv_w, w_out, ple_gate_norm_gain, w_ple_gate, b_ple_gate, w_ple_proj, ple_norm_gain, loss_target, m_norm_gain, m_w_in, m_q_norm_gain, m_k_norm_gain, m_attn_sinks, m_conv_w, m_w_out, m_ple_gate_norm_gain, m_w_ple_gate, m_b_ple_gate, m_w_ple_proj, m_ple_norm_gain, v_norm_gain, v_w_in, v_q_norm_gain, v_k_norm_gain, v_attn_sinks, v_conv_w, v_w_out, v_ple_gate_norm_gain, v_w_ple_gate, v_b_ple_gate, v_w_ple_proj, v_ple_norm_gain):
    x2, p2, tgt = x[0], p[0, 0], loss_target[0]
    s, d = x2.shape
    ple = p2.shape[1]
    aw = d // 2
    cw = d - aw
    nq = aw // HEAD_DIM
    sh = w_in.shape[2]
    in_w = N_CHIPS * sh
    dq = d // N_CHIPS
    cq = cw // N_CHIPS
    ga_off = (aw + 2 * KV_WIDTH) // COLT
    b_off = (2 * aw + 2 * KV_WIDTH) // COLT
    assert in_w == 2 * aw + 2 * KV_WIDTH + 4 * cw and aw % COLT == 0 and cw % COLT == 0
    assert s % BLOCK == 0 and sh % LANES == 0 and nq % (2 * N_KV_HEADS) == 0

    core = lax.axis_index("c").astype(jnp.int32).reshape(1)
    chip = 2 * lax.axis_index("x") + lax.axis_index("y")

    chip1 = chip.astype(jnp.int32).reshape(1)
    place = jnp.concatenate([chip1, core])
    w_in_own = cast_bf16(w_in[0], "cast_w_in")
    rest = [cast_into_slot(w_out[0], chip1, "cast_w_out"), cast_into_slot(w_ple_gate[0], chip1, "cast_w_pg"),
            cast_into_slot(w_ple_proj[0], chip1, "cast_w_pp"),
            lax.dynamic_update_slice(jnp.zeros((N_CHIPS, SUBLANES, cq), F32),
                                     _pad_rows(conv_w[0], SUBLANES)[None], (chip, 0, 0))]
    order = jnp.stack([chip, chip ^ 2, chip ^ 1, chip ^ 3]).astype(jnp.int32)
    wi_sems, wi_arrs, wi_token = exchange_start(
        "gather_wi_start", [(plan_shard_ici(j), [0, 1 + j], 1) for j in range(2)],
        [w_in_own] + [lax.empty((d, sh), BF16) for _ in range(3)])

    tn = _tile(d, 1024)
    ts = _tile(s, 2048)
    tabs = _rope_tables(s)
    qg = jnp.tile(q_norm_gain, (1, LANES // HEAD_DIM))
    kg = jnp.tile(k_norm_gain, (1, LANES // HEAD_DIM))
    seg_i = jnp.arange(SEG) // HEAD_DIM
    segb = (seg_i[:, None] == seg_i[None, :]).astype(BF16)
    h, z = norm_in_proj(x2, norm_gain, wi_arrs[0], order, in_w)
    own, land_x = exchange_wait("gather_wi_wait0", plan_shard_ici(0), wi_sems[0], [wi_arrs[0], wi_arrs[1]],
                                (z,) + tuple(tabs) + tuple(rest[:2]))
    own, land_y = exchange_wait("gather_wi_wait1", plan_shard_ici(1), wi_sems[1], [own, wi_arrs[2]], land_x)
    relay_sems, relayed, relay_token = exchange_start(
        "gather_wi_relay_start", [(plan_shard_relay, [0, 1, 2], 2)], [land_x, land_y, wi_arrs[3]])
    rest_sems, rest, rest_token = exchange_start(
        "gather_rest_start", [(plan_gather_ici(3), [0, 1, 2, 3], 12)], rest, after=(relay_token,))
    land_x, land_y = exchange("gather_wi_pass01", plan_shard_pass, relayed[:2], 2, after=(rest_token,))
    z = in_proj_part(h, land_x, z, order, 1, in_w)
    z = in_proj_part(h, land_y, z, order, 2, in_w)
    land_x, land_y, land_far = exchange_wait("gather_wi_relay_wait", plan_shard_relay, relay_sems[0],
                                             [land_x, land_y, relayed[2]], z)
    land_far, = exchange("gather_wi_pass2", plan_shard_pass, [land_far], 1)
    z = in_proj_part(h, land_far, z, order, 3, in_w)
    w_shards = [own, land_x, land_y, land_far]
    wo_all, wg_all, wp_all, conv_all = exchange_wait("gather_rest_wait", plan_gather_ici(3), rest_sems[0], rest, z)
    pass_sems, passed, pass_token = exchange_start(
        "gather_rest_pass_start", [(plan_gather_pass, [0, 1, 2], 9)], [wo_all, wg_all, wp_all])
    conv_full = conv_all.transpose(1, 0, 2).reshape(SUBLANES, cw)
    qs, ks, vb = qk_prep_fwd(z, tabs, qg, kg, segb, aw, after=(pass_token,))
    attn, mix = attn_fwd(qs, ks, vb, z, attn_sinks, aw, d, ga_off)
    mix = conv_fwd(z, conv_full, mix, aw, cw, b_off)
    wo_all, wg_all, wp_all = exchange_wait("gather_rest_pass_wait", plan_gather_pass, pass_sems[0], passed, mix)
    wo_full = wo_all.reshape(d, d)
    wg_full = wg_all.reshape(d, d)
    x1, hg = out_proj_norm(mix, wo_full, x2, ple_gate_norm_gain)
    pq = d // N_CHIPS

    d_gl, d_pe, dy, acc_head = head_fwd_bwd(x1, hg, wg_full, p2, wp_all, tgt, b_ple_gate, ple_norm_gain)
    wgrad = lambda a_cols, shape3, o_spec, b_cols, name, a, b, grid: matmul(
        a, b, grid=grid,
        a_spec=pl.BlockSpec((ts, a_cols), lambda i, j, k: (k, i)),
        b_spec=pl.BlockSpec((ts, b_cols), lambda i, j, k: (k, j)),
        o_spec=o_spec, out_shape=jax.ShapeDtypeStruct(shape3, BF16), dims=TN, name=name)
    g_wg = wgrad(tn, (d, d), pl.BlockSpec((tn, tn), lambda i, j, k: (i, j)), tn, "mm_g_wg", hg, d_gl,
                 (d // tn, d // tn, s // ts))
    g_wp = wgrad(ple, (N_CHIPS, ple, pq), pl.BlockSpec((None, ple, pq), lambda i, j, k: (j, 0, 0)), pq,
                 "mm_g_wp", p2, d_pe, (1, N_CHIPS, s // ts))
    d_x1, d_x1b, acc_g = gate_proj_bwd_norm(d_gl, wg_full, x1, dy, ple_gate_norm_gain)
    g_wo = wgrad(tn, (d, d), pl.BlockSpec((tn, tn), lambda i, j, k: (i, j)), tn, "mm_g_wo", mix, d_x1b,
                 (d // tn, d // tn, s // ts))
    def reduce_start(tag, grads):
        n = len(grads)
        lands = [lax.empty((N_CHIPS, a.shape[1] // 2, a.shape[2]), BF16) for a in grads]
        swapped = exchange("swap_" + tag, plan_swap, list(grads) + lands, n)
        parts = [add_sibling(g, o, core, "add_sibling_%s%d" % (tag, i))
                 for i, (g, o) in enumerate(zip(swapped[:n], swapped[n:]))]
        return exchange_start("scatter_%s_start" % tag, [(plan_scatter, list(range(2 * n)), 3 * n)],
                              parts + [lax.empty(a.shape, BF16) for a in parts])

    def reduce_sum(tag, handle, after):
        sems, arrays, _ = handle
        n = len(arrays) // 2
        got = exchange_wait("scatter_%s_wait" % tag, plan_scatter, sems[0], arrays, after)
        return [sum_chips(pt, o, place, "sum_chips_%s%d" % (tag, i))
                for i, (pt, o) in enumerate(zip(got[:n], got[n:]))]

    early = reduce_start("early", [g_wo.reshape(N_CHIPS, dq, d), g_wg.reshape(N_CHIPS, dq, d), g_wp])
    d_o, d_gate, d_mix_conv = out_proj_bwd_gate(d_x1b, wo_full, attn, z, aw, ga_off, after=(early[-1],))
    dqs, dks, dvs, acc_sink = attn_bwd(qs, ks, vb, d_o, attn_sinks, aw)
    dz, acc_qk = qk_prep_bwd(z, dqs, dks, dvs, d_gate, tabs, qg, kg, segb, aw)
    dz, acc_conv = conv_bwd(z, d_mix_conv, conv_full, dz, aw, cw, b_off)
    join_sems, joining, join_token = exchange_start(
        "join_early_start", [(plan_join, [0, 1, 2], 3)], reduce_sum("early", early, dz))
    g_wi = matmul(h, dz, grid=(d // tn, N_CHIPS, 1),
                  a_spec=pl.BlockSpec((s, tn), lambda i, j, k: (0, i), pipeline_mode=pl.Buffered(1)),
                  b_spec=pl.BlockSpec((s, sh), lambda i, j, k: (0, j)),
                  o_spec=pl.BlockSpec((None, tn, sh), lambda i, j, k: (j, i, 0)),
                  out_shape=jax.ShapeDtypeStruct((N_CHIPS, d, sh), BF16), dims=TN, name="mm_g_wi",
                  after=(join_token,), vmem=VMEM_LIMIT_BIG)
    full_wo, full_wg, full_wp = exchange_wait("join_early_wait", plan_join, join_sems[0], joining, g_wi)
    late = reduce_start("late", [g_wi])
    grad_x, acc_x = in_proj_bwd_norm(dz, w_shards, order, x2, d_x1, norm_gain, after=(late[-1],))

    wsm = max(d, cw)
    misc = jnp.concatenate([acc_qk[0:1, :HEAD_DIM], acc_qk[1:2, :HEAD_DIM], acc_sink[0:1, :nq]], axis=1)
    small = jnp.concatenate([
        _pad_cols(acc_x[0:1], wsm), _pad_cols(acc_g[0:1], wsm), _pad_cols(acc_head[0:1], wsm),
        _pad_cols(acc_head[1:2], wsm), _pad_cols(misc, wsm), _pad_cols(acc_conv[0:3], wsm),
        _pad_rows(_pad_cols(acc_head[2:3], wsm), SUBLANES)], axis=0)
    device = 2 * chip + lax.axis_index("c")
    small_sems, small_bufs, small_token = exchange_start(
        "small_start", [(plan_allgather_small, [0], 7)],
        [lax.dynamic_update_slice(jnp.zeros((8,) + small.shape, F32), small[None], (device, 0, 0))])
    last_sems, last_halves, last_token = exchange_start(
        "join_late_start", [(plan_join, [0], 1)], reduce_sum("late", late, small_token))
    big, after = {}, (last_token,)
    for nm, g, w, m, v in (("w_out", full_wo, w_out, m_w_out, v_w_out),
                           ("w_ple_gate", full_wg, w_ple_gate, m_w_ple_gate, v_w_ple_gate),
                           ("w_ple_proj", full_wp, w_ple_proj, m_w_ple_proj, v_w_ple_proj)):
        stepped = adamw(g, w[0], m[0], v[0], "adamw_" + nm, after=after)
        big[nm], after = [o[None] for o in stepped], (stepped[1],)
    full_wi, = exchange_wait("join_late_wait", plan_join, last_sems[0], last_halves, after[0])
    big["w_in"] = [o[None] for o in adamw(full_wi, w_in[0], m_w_in[0], v_w_in[0], "adamw_w_in")]

    gathered, = exchange_wait("small_wait", plan_allgather_small, small_sems[0], small_bufs, big["w_in"][1])
    tot = sum_devices(gathered)
    loss = tot[SUBLANES, 0]

    def pack(vals):
        ng, pg, bg, eg, qgv, kgv, sk, cv = vals
        misc_v = jnp.concatenate([qgv, kgv, sk], axis=1)
        return jnp.concatenate([_pad_cols(ng, wsm), _pad_cols(pg, wsm), _pad_cols(bg, wsm), _pad_cols(eg, wsm),
                                _pad_cols(misc_v, wsm), _pad_cols(cv[0], wsm)], axis=0)

    g_small = jnp.concatenate(
        [tot[0:5], _pad_cols(lax.dynamic_slice(tot[5:8], (0, chip * cq), (3, cq)), wsm)], axis=0)
    w_small = pack((norm_gain, ple_gate_norm_gain, b_ple_gate, ple_norm_gain, q_norm_gain, k_norm_gain,
                    attn_sinks, conv_w))
    m_small = pack((m_norm_gain, m_ple_gate_norm_gain, m_b_ple_gate, m_ple_norm_gain, m_q_norm_gain,
                    m_k_norm_gain, m_attn_sinks, m_conv_w))
    v_small = pack((v_norm_gain, v_ple_gate_norm_gain, v_b_ple_gate, v_ple_norm_gain, v_q_norm_gain,
                    v_k_norm_gain, v_attn_sinks, v_conv_w))
    sm = adamw(g_small, w_small, m_small, v_small, "adamw_small")

    def unpack(a):
        return {"norm_gain": a[0:1, :d], "ple_gate_norm_gain": a[1:2, :d], "b_ple_gate": a[2:3, :d],
                "ple_norm_gain": a[3:4, :d], "q_norm_gain": a[4:5, :HEAD_DIM],
                "k_norm_gain": a[4:5, HEAD_DIM:2 * HEAD_DIM],
                "attn_sinks": a[4:5, 2 * HEAD_DIM:2 * HEAD_DIM + nq], "conv_w": a[5:8, :cq][None]}

    order = ("norm_gain", "w_in", "q_norm_gain", "k_norm_gain", "attn_sinks", "conv_w", "w_out",
             "ple_gate_norm_gain", "w_ple_gate", "b_ple_gate", "w_ple_proj", "ple_norm_gain")
    outs = [loss, grad_x[None]]
    for kind in range(4):
        table = unpack(sm[kind])
        for nm in order:
            outs.append(big[nm][kind] if nm in big else table[nm])
    return tuple(outs)
```

```python
import functools

import jax
import jax.numpy as jnp
from jax import lax
from jax.experimental import pallas as pl
from jax.experimental.pallas import tpu as pltpu

F32 = jnp.float32
BF16 = jnp.bfloat16
MESH = pl.DeviceIdType.MESH

HEAD_DIM = 64
N_KV_HEADS = 4
KV_WIDTH = N_KV_HEADS * HEAD_DIM
BLOCK = 128
ROT_DIM = 16
ROPE_THETA = 500000.0
EPS = 1e-6
NEG_INF = -1e30
N_CHIPS = 4
LANES = 128
SUBLANES = 8
COLT = 512
SEG = 256
VMEM_LIMIT = 48 * 1024 * 1024
VMEM_LIMIT_BIG = 60 * 1024 * 1024

ADAM_LR = 0.001
ADAM_B1 = 0.9
ADAM_B2 = 0.999
ADAM_EPS = 1e-08
ADAM_WD = 0.01
ADAM_STEP = 10

NN = (((1,), (0,)), ((), ()))
NT = (((1,), (1,)), ((), ()))
TN = (((0,), (0,)), ((), ()))


def _call(body, **kw):
    return pl.pallas_call(body, **kw)


def _call_after(body, after, **kw):
    n_in, n_after = len(kw["in_specs"]), len(after)
    kw["in_specs"] = list(kw["in_specs"]) + [pl.BlockSpec(memory_space=pl.ANY)] * n_after

    def body_after(*refs):
        body(*refs[:n_in], *refs[n_in + n_after:])

    call = _call(body_after, **kw)
    return lambda *args: call(*args, *after)


def _params(sem, vmem=VMEM_LIMIT):
    return pltpu.CompilerParams(dimension_semantics=sem, vmem_limit_bytes=vmem)


def _tile(n, pref):
    return pref if n % pref == 0 else n


def _sigmoid(v):
    return 1.0 / (1.0 + jnp.exp(-v))


def _hbm():
    return pl.BlockSpec(memory_space=pl.ANY)


def cast_bf16(a, name):
    r, c = a.shape
    tr = _tile(r, 512)

    def body(a_ref, o_ref):
        o_ref[...] = a_ref[...].astype(BF16)

    return _call(body, name=name, grid=(r // tr,),
                 in_specs=[pl.BlockSpec((tr, c), lambda i: (i, 0))],
                 out_specs=pl.BlockSpec((tr, c), lambda i: (i, 0)),
                 out_shape=jax.ShapeDtypeStruct((r, c), BF16),
                 compiler_params=_params(("parallel",)))(a)


def cast_into_slot(a, chip, name):
    r, c = a.shape
    tr = _tile(r, 512)

    def body(chip_ref, a_ref, o_ref):
        o_ref[...] = a_ref[...].astype(BF16)

    grid_spec = pltpu.PrefetchScalarGridSpec(
        num_scalar_prefetch=1, grid=(r // tr,),
        in_specs=[pl.BlockSpec((tr, c), lambda i, chip_ref: (i, 0))],
        out_specs=pl.BlockSpec((None, tr, c), lambda i, chip_ref: (chip_ref[0], i, 0)))
    return _call(body, name=name, grid_spec=grid_spec,
                 out_shape=jax.ShapeDtypeStruct((N_CHIPS, r, c), BF16),
                 compiler_params=_params(("parallel",)))(chip, a)


def out_proj_norm(mix, wo, x, gain):
    s, d = x.shape
    tm = _tile(s, 512)

    def body(m_ref, w_ref, x_ref, g_ref, x1_ref, hg_ref):
        x1 = x_ref[...] + jnp.dot(m_ref[...], w_ref[...], preferred_element_type=F32)
        x1_ref[...] = x1
        r = lax.rsqrt(jnp.mean(x1 * x1, axis=-1, keepdims=True) + EPS)
        hg_ref[...] = ((x1 * r) * g_ref[...]).astype(BF16)

    row = pl.BlockSpec((tm, d), lambda i: (i, 0))
    return _call(body, name="out_proj_norm", grid=(s // tm,),
                 in_specs=[row, pl.BlockSpec((d, d), lambda i: (0, 0), pipeline_mode=pl.Buffered(1)), row,
                           pl.BlockSpec((1, d), lambda i: (0, 0))],
                 out_specs=[row, row],
                 out_shape=[jax.ShapeDtypeStruct((s, d), F32), jax.ShapeDtypeStruct((s, d), BF16)],
                 compiler_params=_params(("parallel",), VMEM_LIMIT_BIG))(mix, wo, x, gain)


def gate_proj_bwd_norm(d_gl, wg, xin, add, gain):
    s, d = xin.shape
    tm = _tile(s, 256)

    def body(dg_ref, w_ref, x_ref, a_ref, g_ref, dx_ref, dxb_ref, acc_ref):
        i = pl.program_id(0)
        dyv = lax.dot_general(dg_ref[...], w_ref[...], NT, preferred_element_type=F32)
        xf = x_ref[...]
        r = lax.rsqrt(jnp.mean(xf * xf, axis=-1, keepdims=True) + EPS)
        xhat = xf * r
        gd = dyv * g_ref[...]
        dx = a_ref[...] + r * (gd - xhat * jnp.mean(xhat * gd, axis=-1, keepdims=True))
        dx_ref[...] = dx
        dxb_ref[...] = dx.astype(BF16)

        @pl.when(i == 0)
        def _():
            acc_ref[...] = jnp.zeros_like(acc_ref)

        acc_ref[0:1, :] += jnp.sum(dyv * xhat, axis=0, keepdims=True)

    row = pl.BlockSpec((tm, d), lambda i: (i, 0))
    return _call(body, name="gate_proj_bwd_norm", grid=(s // tm,),
                 in_specs=[row, pl.BlockSpec((d, d), lambda i: (0, 0), pipeline_mode=pl.Buffered(1)), row, row,
                           pl.BlockSpec((1, d), lambda i: (0, 0))],
                 out_specs=[row, row, pl.BlockSpec((SUBLANES, d), lambda i: (0, 0))],
                 out_shape=[jax.ShapeDtypeStruct((s, d), F32), jax.ShapeDtypeStruct((s, d), BF16),
                            jax.ShapeDtypeStruct((SUBLANES, d), F32)],
                 compiler_params=_params(("arbitrary",), VMEM_LIMIT_BIG))(d_gl, wg, xin, add, gain)


def head_fwd_bwd(x1, hg, wg, p, wp, tgt, bias, ple_gain):
    s, d = x1.shape
    tm = _tile(s, 256)

    def body(x1_ref, hg_ref, wg_ref, p_ref, wp_ref, t_ref, b_ref, g_ref, dgl_ref, dpe_ref, dy_ref, acc_ref):
        i = pl.program_id(0)
        gl = jnp.dot(hg_ref[...], wg_ref[...], preferred_element_type=F32)
        pb = p_ref[...].astype(BF16)
        pev = jnp.concatenate([jnp.dot(pb, wp_ref[q], preferred_element_type=F32) for q in range(N_CHIPS)],
                              axis=1)
        r = lax.rsqrt(jnp.mean(pev * pev, axis=-1, keepdims=True) + EPS)
        pehat = pev * r
        gain = g_ref[...]
        e = pehat * gain
        gate = _sigmoid(gl + b_ref[...])
        diff = (x1_ref[...] + gate * e) - t_ref[...]
        dy = diff * (1.0 / d)
        dy_ref[...] = dy
        d_gl = (dy * e) * (gate * (1.0 - gate))
        dgl_ref[...] = d_gl.astype(BF16)
        d_e = dy * gate
        gd = d_e * gain
        d_pe = r * (gd - pehat * jnp.mean(pehat * gd, axis=-1, keepdims=True))
        dpe_ref[...] = d_pe.astype(BF16)

        @pl.when(i == 0)
        def _():
            acc_ref[...] = jnp.zeros_like(acc_ref)

        acc_ref[0:1, :] += jnp.sum(d_gl, axis=0, keepdims=True)
        acc_ref[1:2, :] += jnp.sum(d_e * pehat, axis=0, keepdims=True)
        acc_ref[2:3, :] += jnp.sum(diff * diff, axis=0, keepdims=True) * (0.5 / d)

    row = pl.BlockSpec((tm, d), lambda i: (i, 0))
    vec = pl.BlockSpec((1, d), lambda i: (0, 0))
    return _call(body, name="head_fwd_bwd", grid=(s // tm,),
                 in_specs=[row, row, pl.BlockSpec((d, d), lambda i: (0, 0), pipeline_mode=pl.Buffered(1)),
                           pl.BlockSpec((tm, p.shape[1]), lambda i: (i, 0)),
                           pl.BlockSpec(wp.shape, lambda i: (0, 0, 0)), row, vec, vec],
                 out_specs=[row, row, row, pl.BlockSpec((SUBLANES, d), lambda i: (0, 0))],
                 out_shape=[jax.ShapeDtypeStruct((s, d), BF16), jax.ShapeDtypeStruct((s, d), BF16),
                            jax.ShapeDtypeStruct((s, d), F32), jax.ShapeDtypeStruct((SUBLANES, d), F32)],
                 compiler_params=_params(("arbitrary",), VMEM_LIMIT_BIG))(
                     x1, hg, wg, p, wp, tgt, bias, ple_gain)


def adamw(g, w, m, v, name, after=()):
    r, c = g.shape
    tr = _tile(r, 256)

    def body(g_ref, w_ref, m_ref, v_ref, go_ref, d_ref, mo_ref, vo_ref):
        gv = g_ref[...]
        mn = ADAM_B1 * m_ref[...] + (1.0 - ADAM_B1) * gv
        vn = ADAM_B2 * v_ref[...] + (1.0 - ADAM_B2) * (gv * gv)
        m_hat = mn / (1.0 - ADAM_B1 ** ADAM_STEP)
        v_hat = vn / (1.0 - ADAM_B2 ** ADAM_STEP)
        go_ref[...] = gv
        d_ref[...] = -ADAM_LR * (m_hat / (jnp.sqrt(v_hat) + ADAM_EPS) + ADAM_WD * w_ref[...])
        mo_ref[...] = mn
        vo_ref[...] = vn

    blk = pl.BlockSpec((tr, c), lambda i: (i, 0))
    shp = jax.ShapeDtypeStruct((r, c), F32)
    return _call_after(body, after, name=name, grid=(r // tr,), in_specs=[blk] * 4, out_specs=[blk] * 4,
                       out_shape=[shp] * 4, compiler_params=_params(("parallel",)))(g, w, m, v)


def matmul(a, b, *, grid, a_spec, b_spec, o_spec, out_shape, dims, name, res=None, res_spec=None, after=(),
           vmem=VMEM_LIMIT):
    nk = grid[2]
    acc_shape = tuple(d for d in o_spec.block_shape if d is not None)

    def body(*refs):
        a_ref, b_ref = refs[:2]
        r_ref = refs[2] if res is not None else None
        o_ref = refs[3] if res is not None else refs[2]
        part = lax.dot_general(a_ref[...].astype(BF16), b_ref[...].astype(BF16), dims, preferred_element_type=F32)

        def finish(out):
            if res is not None:
                out = r_ref[...] + out
            o_ref[...] = out.astype(o_ref.dtype)

        if nk == 1:
            finish(part)
            return
        acc_ref = refs[-1]
        k = pl.program_id(2)

        @pl.when(k == 0)
        def _():
            acc_ref[...] = part

        @pl.when((k > 0) & (k < nk - 1))
        def _():
            acc_ref[...] += part

        @pl.when(k == nk - 1)
        def _():
            finish(acc_ref[...] + part)

    in_specs = [a_spec, b_spec] + ([res_spec] if res is not None else [])
    args = (a, b) + ((res,) if res is not None else ())
    return _call_after(body, after, name=name, grid=grid, in_specs=in_specs, out_specs=o_spec,
                       out_shape=out_shape, scratch_shapes=[pltpu.VMEM(acc_shape, F32)] if nk > 1 else [],
                       compiler_params=_params(("parallel", "parallel", "arbitrary"), vmem))(*args)


def norm_in_proj(x, gain, w, order, in_w):
    s, d = x.shape
    sh = w.shape[1]
    tm = _tile(s, 512)

    def body(order_ref, x_ref, g_ref, w_ref, h_ref, z_ref):
        xf = x_ref[...]
        r = lax.rsqrt(jnp.mean(xf * xf, axis=-1, keepdims=True) + EPS)
        hb = ((xf * r) * g_ref[...]).astype(BF16)
        h_ref[...] = hb
        z_ref[...] = jnp.dot(hb, w_ref[...], preferred_element_type=F32)

    grid_spec = pltpu.PrefetchScalarGridSpec(
        num_scalar_prefetch=1, grid=(s // tm,),
        in_specs=[pl.BlockSpec((tm, d), lambda i, order_ref: (i, 0)),
                  pl.BlockSpec((1, d), lambda i, order_ref: (0, 0)),
                  pl.BlockSpec((d, sh), lambda i, order_ref: (0, 0), pipeline_mode=pl.Buffered(1))],
        out_specs=[pl.BlockSpec((tm, d), lambda i, order_ref: (i, 0)),
                   pl.BlockSpec((tm, sh), lambda i, order_ref: (i, order_ref[0]))])
    return _call(body, name="norm_in_proj", grid_spec=grid_spec,
                 out_shape=[jax.ShapeDtypeStruct((s, d), BF16), jax.ShapeDtypeStruct((s, in_w), F32)],
                 compiler_params=_params(("parallel",)))(order, x, gain, w)


def in_proj_part(h, w, z_prev, order, q, in_w):
    s, d = h.shape
    sh = w.shape[1]
    tm = _tile(s, 512)

    def body(order_ref, h_ref, w_ref, *rest):
        rest[-1][...] = jnp.dot(h_ref[...], w_ref[...], preferred_element_type=F32)

    in_specs = [pl.BlockSpec((tm, d), lambda i, order_ref: (i, 0)),
                pl.BlockSpec((d, sh), lambda i, order_ref: (0, 0))]
    args = [order, h, w]
    if z_prev is not None:
        in_specs.append(_hbm())
        args.append(z_prev)
    grid_spec = pltpu.PrefetchScalarGridSpec(
        num_scalar_prefetch=1, grid=(s // tm,), in_specs=in_specs,
        out_specs=pl.BlockSpec((tm, sh), lambda i, order_ref: (i, order_ref[q])))
    return _call(body, name="mm_z%d" % q, grid_spec=grid_spec,
                 out_shape=jax.ShapeDtypeStruct((s, in_w), F32),
                 input_output_aliases={3: 0} if z_prev is not None else {},
                 compiler_params=_params(("parallel",)))(*args)


def in_proj_wgrad_half(h, dz, g_prev, half, after):
    s, d = h.shape
    sh = dz.shape[1] // N_CHIPS
    hr = d // 2

    def body(half_ref, h_ref, dz_ref, *rest):
        rest[-1][...] = lax.dot_general(h_ref[...], dz_ref[...], TN, preferred_element_type=F32).astype(BF16)

    extra = ([g_prev] if g_prev is not None else []) + list(after)
    grid_spec = pltpu.PrefetchScalarGridSpec(
        num_scalar_prefetch=1, grid=(N_CHIPS,),
        in_specs=[pl.BlockSpec((s, hr), lambda j, half_ref: (0, half_ref[0]), pipeline_mode=pl.Buffered(1)),
                  pl.BlockSpec((s, sh), lambda j, half_ref: (0, j))] + [_hbm()] * len(extra),
        out_specs=pl.BlockSpec((None, hr, sh), lambda j, half_ref: (j, half_ref[0], 0)))
    return _call(body, name="mm_g_wi_%s" % ("keep" if g_prev is not None else "send"), grid_spec=grid_spec,
                 out_shape=jax.ShapeDtypeStruct((N_CHIPS, d, sh), BF16),
                 input_output_aliases={3: 0} if g_prev is not None else {},
                 compiler_params=_params(("parallel",), VMEM_LIMIT_BIG))(half, h, dz, *extra)


def in_proj_bwd_norm(dz, ws, order, xin, add, gain, after):
    s, d = xin.shape
    sh = ws[0].shape[1]
    tm = _tile(s, 256)
    nq, n_after = len(ws), len(after)

    def body(order_ref, *refs):
        a_refs, b_refs = refs[:nq], refs[nq:2 * nq]
        x_ref, add_ref, g_ref = refs[2 * nq:2 * nq + 3]
        dx_ref, acc_ref = refs[2 * nq + 3 + n_after:]
        i = pl.program_id(0)
        dyv = lax.dot_general(a_refs[0][...], b_refs[0][...], NT, preferred_element_type=F32)
        for q in range(1, nq):
            dyv += lax.dot_general(a_refs[q][...], b_refs[q][...], NT, preferred_element_type=F32)
        xf = x_ref[...]
        r = lax.rsqrt(jnp.mean(xf * xf, axis=-1, keepdims=True) + EPS)
        xhat = xf * r
        gd = dyv * g_ref[...]
        dx_ref[...] = add_ref[...] + r * (gd - xhat * jnp.mean(xhat * gd, axis=-1, keepdims=True))

        @pl.when(i == 0)
        def _():
            acc_ref[...] = jnp.zeros_like(acc_ref)

        acc_ref[0:1, :] += jnp.sum(dyv * xhat, axis=0, keepdims=True)

    a_spec = lambda q: pl.BlockSpec((tm, sh), functools.partial(lambda i, order_ref, q: (i, order_ref[q]), q=q))
    row = pl.BlockSpec((tm, d), lambda i, order_ref: (i, 0))
    grid_spec = pltpu.PrefetchScalarGridSpec(
        num_scalar_prefetch=1, grid=(s // tm,),
        in_specs=[a_spec(q) for q in range(nq)]
        + [pl.BlockSpec((d, sh), lambda i, order_ref: (0, 0), pipeline_mode=pl.Buffered(1))] * nq
        + [row, row, pl.BlockSpec((1, d), lambda i, order_ref: (0, 0))] + [_hbm()] * n_after,
        out_specs=[row, pl.BlockSpec((SUBLANES, d), lambda i, order_ref: (0, 0))])
    return _call(body, name="in_proj_bwd_norm", grid_spec=grid_spec,
                 out_shape=[jax.ShapeDtypeStruct((s, d), F32), jax.ShapeDtypeStruct((SUBLANES, d), F32)],
                 compiler_params=_params(("arbitrary",), VMEM_LIMIT_BIG))(
                     order, *([dz] * nq), *ws, xin, add, gain, *after)


def _seg_mean(sq, segb):
    parts = []
    for cgrp in range(sq.shape[1] // SEG):
        blk = sq[:, cgrp * SEG:(cgrp + 1) * SEG]
        hi = blk.astype(BF16)
        r1 = blk - hi.astype(F32)
        mid = r1.astype(BF16)
        lo = (r1 - mid.astype(F32)).astype(BF16)
        acc = jnp.dot(hi, segb, preferred_element_type=F32)
        acc += jnp.dot(mid, segb, preferred_element_type=F32)
        acc += jnp.dot(lo, segb, preferred_element_type=F32)
        parts.append(acc)
    out = parts[0] if len(parts) == 1 else jnp.concatenate(parts, axis=1)
    return out * (1.0 / HEAD_DIM)


def _rope(v, cos, sa, sb):
    parts = []
    for cgrp in range(v.shape[1] // LANES):
        blk = v[:, cgrp * LANES:(cgrp + 1) * LANES]
        parts.append(blk * cos + pltpu.roll(blk, LANES - 8, 1) * sa + pltpu.roll(blk, 8, 1) * sb)
    return parts[0] if len(parts) == 1 else jnp.concatenate(parts, axis=1)


def _rope_t(dv, cos, sa, sb):
    parts = []
    for cgrp in range(dv.shape[1] // LANES):
        blk = dv[:, cgrp * LANES:(cgrp + 1) * LANES]
        parts.append(blk * cos + pltpu.roll(blk * sa, 8, 1) + pltpu.roll(blk * sb, LANES - 8, 1))
    return parts[0] if len(parts) == 1 else jnp.concatenate(parts, axis=1)


def _tile_lanes(vec, width):
    reps = width // LANES
    return vec if reps == 1 else jnp.tile(vec, (1, reps))


def qk_prep_fwd(z, tabs, qg, kg, segb, aw, after=()):
    s = z.shape[0]
    tm = _tile(s, 512)
    wq = aw + 2 * KV_WIDTH

    def body(z_ref, cos_ref, sa_ref, sb_ref, qg_ref, kg_ref, seg_ref, qs_ref, ks_ref, vb_ref):
        zz = z_ref[...]
        q, k, v = zz[:, :aw], zz[:, aw:aw + KV_WIDTH], zz[:, aw + KV_WIDTH:]
        cos, sa, sb, segm = cos_ref[...], sa_ref[...], sb_ref[...], seg_ref[...]
        rq = lax.rsqrt(_seg_mean(q * q, segm) + EPS)
        qn = (q * rq) * _tile_lanes(qg_ref[...], aw)
        qs_ref[...] = (_rope(qn, cos, sa, sb) * (HEAD_DIM ** -0.5)).astype(BF16)
        rk = lax.rsqrt(_seg_mean(k * k, segm) + EPS)
        kn = (k * rk) * _tile_lanes(kg_ref[...], KV_WIDTH)
        ks_ref[...] = _rope(kn, cos, sa, sb).astype(BF16)
        vb_ref[...] = v.astype(BF16)

    tab = pl.BlockSpec((tm, LANES), lambda i: (i, 0))
    vec = pl.BlockSpec((1, LANES), lambda i: (0, 0))
    return _call_after(body, after, name="qk_prep_fwd", grid=(s // tm,),
                       in_specs=[pl.BlockSpec((tm, wq), lambda i: (i, 0)), tab, tab, tab, vec, vec,
                                 pl.BlockSpec((SEG, SEG), lambda i: (0, 0))],
                       out_specs=[pl.BlockSpec((tm, aw), lambda i: (i, 0)),
                                  pl.BlockSpec((tm, KV_WIDTH), lambda i: (i, 0)),
                                  pl.BlockSpec((tm, KV_WIDTH), lambda i: (i, 0))],
                       out_shape=[jax.ShapeDtypeStruct((s, aw), BF16), jax.ShapeDtypeStruct((s, KV_WIDTH), BF16),
                                  jax.ShapeDtypeStruct((s, KV_WIDTH), BF16)],
                       compiler_params=_params(("parallel",)))(z, *tabs, qg, kg, segb)


def qk_prep_bwd(z, dqs, dks, dvs, d_gate, tabs, qg, kg, segb, aw):
    s, in_w = z.shape
    tm = _tile(s, 512)
    wq = aw + 2 * KV_WIDTH

    def body(z_ref, dq_ref, dk_ref, dv_ref, dga_ref, cos_ref, sa_ref, sb_ref, qg_ref, kg_ref, seg_ref,
             dz_ref, acc_ref):
        i = pl.program_id(0)
        zz = z_ref[...]
        q, k = zz[:, :aw], zz[:, aw:aw + KV_WIDTH]
        cos, sa, sb, segm = cos_ref[...], sa_ref[...], sb_ref[...], seg_ref[...]

        def one(xv, dout, gvec, width):
            g = _tile_lanes(gvec, width)
            r = lax.rsqrt(_seg_mean(xv * xv, segm) + EPS)
            xhat = xv * r
            dn = _rope_t(dout, cos, sa, sb)
            gd = dn * g
            dx = r * (gd - xhat * _seg_mean(xhat * gd, segm))
            contrib = jnp.sum(dn * xhat, axis=0, keepdims=True)
            folded = contrib[:, :LANES]
            for cgrp in range(1, width // LANES):
                folded = folded + contrib[:, cgrp * LANES:(cgrp + 1) * LANES]
            return dx, folded + pltpu.roll(folded, HEAD_DIM, 1)

        dq, gq = one(q, dq_ref[...] * (HEAD_DIM ** -0.5), qg_ref[...], aw)
        dk, gk = one(k, dk_ref[...], kg_ref[...], KV_WIDTH)
        dz_ref[:, :aw] = dq.astype(BF16)
        dz_ref[:, aw:aw + KV_WIDTH] = dk.astype(BF16)
        dz_ref[:, aw + KV_WIDTH:wq] = dv_ref[...].astype(BF16)
        dz_ref[:, wq:] = dga_ref[...]

        @pl.when(i == 0)
        def _():
            acc_ref[...] = jnp.zeros_like(acc_ref)

        acc_ref[0:1, :] += gq
        acc_ref[1:2, :] += gk

    tab = pl.BlockSpec((tm, LANES), lambda i: (i, 0))
    vec = pl.BlockSpec((1, LANES), lambda i: (0, 0))
    kvb = pl.BlockSpec((tm, KV_WIDTH), lambda i: (i, 0))
    awb = pl.BlockSpec((tm, aw), lambda i: (i, 0))
    return _call(body, name="qk_prep_bwd", grid=(s // tm,),
                 in_specs=[pl.BlockSpec((tm, wq), lambda i: (i, 0)), awb, kvb, kvb, awb, tab, tab, tab, vec, vec,
                           pl.BlockSpec((SEG, SEG), lambda i: (0, 0))],
                 out_specs=[pl.BlockSpec((tm, wq + aw), lambda i: (i, 0)),
                            pl.BlockSpec((SUBLANES, LANES), lambda i: (0, 0))],
                 out_shape=[jax.ShapeDtypeStruct((s, in_w), BF16), jax.ShapeDtypeStruct((SUBLANES, LANES), F32)],
                 compiler_params=_params(("arbitrary",)))(z, dqs, dks, dvs, d_gate, *tabs, qg, kg, segb)


def _placed(band, lane_idx):
    out = {}
    for kh in range(N_KV_HEADS):
        grp = band[:, (kh // 2) * LANES:(kh // 2 + 1) * LANES]
        for half in (0, 1):
            t = grp if half == kh % 2 else pltpu.roll(grp, HEAD_DIM, 1)
            keep = (lane_idx >= half * HEAD_DIM) & (lane_idx < (half + 1) * HEAD_DIM)
            out[kh, half] = jnp.where(keep, t, jnp.zeros_like(t))
    return out


def _softmax_with_sink(sc, valid, sink):
    sc = jnp.where(valid, sc, NEG_INF)
    m = jnp.maximum(jnp.max(sc, axis=-1, keepdims=True), sink)
    e = jnp.exp(sc - m)
    es = jnp.exp(sink - m)
    den = jnp.sum(e, axis=-1, keepdims=True) + es
    return e / den, es / den


def _stack_halves(placed, kh):
    return jnp.concatenate([placed[kh, 0], placed[kh, 1]], axis=0)


def _valid_mask(n):
    r_i = lax.broadcasted_iota(jnp.int32, (BLOCK, 2 * BLOCK), 0)
    j_i = lax.broadcasted_iota(jnp.int32, (BLOCK, 2 * BLOCK), 1)
    return (j_i > r_i) & (j_i <= r_i + BLOCK) & ((n > 0) | (j_i >= BLOCK))


def attn_fwd(qs, ks, vb, z, sinks, aw, d, ga_off):
    s = qs.shape[0]
    nb = s // BLOCK
    nq = aw // HEAD_DIM
    grp_sz = nq // N_KV_HEADS
    n_ga = aw // COLT

    def body(q_ref, ko_ref, kp_ref, vo_ref, vp_ref, *rest):
        ga_refs = rest[:n_ga]
        sink_ref, attn_ref, mix_ref = rest[n_ga:]
        n = pl.program_id(0)
        lane_idx = lax.broadcasted_iota(jnp.int32, (2 * BLOCK, LANES), 1)
        kpl = _placed(jnp.concatenate([kp_ref[...], ko_ref[...]], axis=0), lane_idx)
        vpl = _placed(jnp.concatenate([vp_ref[...], vo_ref[...]], axis=0), lane_idx)
        valid = _valid_mask(n)
        groups = range(nq // 2)
        kcat = [_stack_halves(kpl, kh) for kh in range(N_KV_HEADS)]
        vcat = [_stack_halves(vpl, kh) for kh in range(N_KV_HEADS)]
        scs = [lax.dot_general(q_ref[:, c * LANES:(c + 1) * LANES], kcat[2 * c // grp_sz], NT,
                               preferred_element_type=F32) for c in groups]
        probs = [jnp.concatenate(
            [_softmax_with_sink(scs[c][:, half * 2 * BLOCK:(half + 1) * 2 * BLOCK], valid,
                                sink_ref[0, 2 * c + half])[0].astype(BF16) for half in (0, 1)], axis=1)
                 for c in groups]
        for c in groups:
            acc = jnp.dot(probs[c], vcat[2 * c // grp_sz], preferred_element_type=F32)
            attn_ref[:, c * LANES:(c + 1) * LANES] = acc
            col = c * LANES
            ga = ga_refs[col // COLT][:, col % COLT:col % COLT + LANES]
            mix_ref[:, c * LANES:(c + 1) * LANES] = (acc * (ga * _sigmoid(ga))).astype(BF16)

    own = lambda n: (n, 0)
    prev = lambda n: (jnp.maximum(n - 1, 0), 0)
    kvs = lambda imap: pl.BlockSpec((BLOCK, KV_WIDTH), imap)
    ga_specs = [pl.BlockSpec((BLOCK, COLT), functools.partial(lambda n, j: (n, ga_off + j), j=j))
                for j in range(n_ga)]
    return _call(body, name="attn_fwd", grid=(nb,),
                 in_specs=[pl.BlockSpec((BLOCK, aw), own), kvs(own), kvs(prev), kvs(own), kvs(prev)]
                 + ga_specs + [pl.BlockSpec(memory_space=pltpu.SMEM)],
                 out_specs=[pl.BlockSpec((BLOCK, aw), own), pl.BlockSpec((BLOCK, aw), own)],
                 out_shape=[jax.ShapeDtypeStruct((s, aw), F32), jax.ShapeDtypeStruct((s, d), BF16)],
                 compiler_params=_params(("parallel",)))(qs, ks, ks, vb, vb, *([z] * n_ga), sinks)


def out_proj_bwd_gate(d_x1b, wo, attn, z, aw, ga_off, after=()):
    s, d = d_x1b.shape
    tm = _tile(s, 512)
    n_ga = aw // COLT

    def body(dx_ref, w_ref, at_ref, *rest):
        ga_refs, (do_ref, dga_ref, dmc_ref) = rest[:n_ga], rest[n_ga:]
        dm = lax.dot_general(dx_ref[...], w_ref[...], NT, preferred_element_type=F32)
        dmc_ref[...] = dm[:, aw:]
        for j in range(n_ga):
            cols = slice(j * COLT, (j + 1) * COLT)
            ga = ga_refs[j][...]
            sig = _sigmoid(ga)
            dma = dm[:, cols]
            do_ref[:, cols] = (dma * (ga * sig)).astype(BF16)
            dga_ref[:, cols] = ((dma * at_ref[:, cols]) * (sig * (1.0 + ga * (1.0 - sig)))).astype(BF16)

    row = lambda width: pl.BlockSpec((tm, width), lambda i: (i, 0))
    ga_specs = [pl.BlockSpec((tm, COLT), functools.partial(lambda i, j: (i, ga_off + j), j=j)) for j in range(n_ga)]
    return _call_after(body, after, name="out_proj_bwd_gate", grid=(s // tm,),
                       in_specs=[row(d), pl.BlockSpec((d, d), lambda i: (0, 0), pipeline_mode=pl.Buffered(1)),
                                 row(aw)] + ga_specs,
                       out_specs=[row(aw), row(aw), row(d - aw)],
                       out_shape=[jax.ShapeDtypeStruct((s, aw), BF16), jax.ShapeDtypeStruct((s, aw), BF16),
                                  jax.ShapeDtypeStruct((s, d - aw), F32)],
                       compiler_params=_params(("parallel",)))(d_x1b, wo, attn, *([z] * n_ga))


def attn_bwd(qs, ks, vb, d_o, sinks, aw):
    s = qs.shape[0]
    nb = s // BLOCK
    nq = aw // HEAD_DIM
    grp_sz = nq // N_KV_HEADS

    def body(q_ref, ko_ref, kp_ref, vo_ref, vp_ref, do_ref, sink_ref, dq_ref, dk_ref, dv_ref, ds_ref,
             ck_ref, cv_ref):
        n = pl.program_id(0)

        @pl.when(n == 0)
        def _():
            ds_ref[...] = jnp.zeros_like(ds_ref)
            dk_ref[...] = jnp.zeros_like(dk_ref)
            dv_ref[...] = jnp.zeros_like(dv_ref)

        @pl.when(n < nb)
        def _():
            lane_idx = lax.broadcasted_iota(jnp.int32, (2 * BLOCK, LANES), 1)
            lane_row = lax.broadcasted_iota(jnp.int32, (1, LANES), 1)
            kpl = _placed(jnp.concatenate([kp_ref[...], ko_ref[...]], axis=0), lane_idx)
            vpl = _placed(jnp.concatenate([vp_ref[...], vo_ref[...]], axis=0), lane_idx)
            valid = _valid_mask(n)
            ds_row = jnp.zeros((1, LANES), F32)
            wide = 2 * BLOCK
            groups = range(nq // 2)
            per_kv = grp_sz // 2
            kcat = [_stack_halves(kpl, kh) for kh in range(N_KV_HEADS)]
            vcat = [_stack_halves(vpl, kh) for kh in range(N_KV_HEADS)]
            qg = [q_ref[:, c * LANES:(c + 1) * LANES] for c in groups]
            dog = [do_ref[:, c * LANES:(c + 1) * LANES] for c in groups]
            scs = [lax.dot_general(qg[c], kcat[c // per_kv], NT, preferred_element_type=F32) for c in groups]
            dps = [lax.dot_general(dog[c], vcat[c // per_kv], NT, preferred_element_type=F32) for c in groups]
            ds_pairs, p_pairs = [], []
            for c in groups:
                probs, dss = [], []
                for half in (0, 1):
                    h = 2 * c + half
                    cols = slice(half * wide, (half + 1) * wide)
                    p, p_sink = _softmax_with_sink(scs[c][:, cols], valid, sink_ref[0, h])
                    delta = jnp.sum(p * dps[c][:, cols], axis=-1, keepdims=True)
                    dss.append((p * (dps[c][:, cols] - delta)).astype(BF16))
                    probs.append(p.astype(BF16))
                    dsink = -jnp.sum(p_sink * delta, axis=0, keepdims=True)
                    ds_row += jnp.where(lane_row == h, dsink, 0.0)
                ds_pairs.append(jnp.concatenate(dss, axis=1))
                p_pairs.append(jnp.concatenate(probs, axis=1))
            for c in groups:
                dq_ref[:, c * LANES:(c + 1) * LANES] = jnp.dot(ds_pairs[c], kcat[c // per_kv],
                                                               preferred_element_type=F32)
            dkts = [lax.dot_general(qg[c], ds_pairs[c], TN, preferred_element_type=F32) for c in groups]
            dvts = [lax.dot_general(dog[c], p_pairs[c], TN, preferred_element_type=F32) for c in groups]
            dkt, dvt = [], []
            for kh in range(N_KV_HEADS):
                for parts, out in ((dkts, dkt), (dvts, dvt)):
                    tot = parts[kh * per_kv]
                    for c in range(kh * per_kv + 1, (kh + 1) * per_kv):
                        tot = tot + parts[c]
                    out.append(tot[:HEAD_DIM, :wide] + tot[HEAD_DIM:, wide:])
            ds_ref[0:1, :] += ds_row
            back = lambda t: jnp.concatenate(
                [jnp.concatenate(t[2 * g:2 * g + 2], axis=0).T for g in range(N_KV_HEADS // 2)], axis=1)
            dk_band, dv_band = back(dkt), back(dvt)

            @pl.when(n > 0)
            def _():
                dk_ref[...] = ck_ref[...] + dk_band[:BLOCK]
                dv_ref[...] = cv_ref[...] + dv_band[:BLOCK]

            ck_ref[...] = dk_band[BLOCK:]
            cv_ref[...] = dv_band[BLOCK:]

        @pl.when(n == nb)
        def _():
            dk_ref[...] = ck_ref[...]
            dv_ref[...] = cv_ref[...]

    own = lambda n: (jnp.minimum(n, nb - 1), 0)
    prev = lambda n: (jnp.clip(n - 1, 0, nb - 1), 0)
    done = lambda n: (jnp.maximum(n - 1, 0), 0)
    kvs = lambda imap: pl.BlockSpec((BLOCK, KV_WIDTH), imap)
    return _call(body, name="attn_bwd", grid=(nb + 1,),
                 in_specs=[pl.BlockSpec((BLOCK, aw), own), kvs(own), kvs(prev), kvs(own), kvs(prev),
                           pl.BlockSpec((BLOCK, aw), own), pl.BlockSpec(memory_space=pltpu.SMEM)],
                 out_specs=[pl.BlockSpec((BLOCK, aw), own), kvs(done), kvs(done),
                            pl.BlockSpec((SUBLANES, LANES), lambda n: (0, 0))],
                 out_shape=[jax.ShapeDtypeStruct((s, aw), F32), jax.ShapeDtypeStruct((s, KV_WIDTH), F32),
                            jax.ShapeDtypeStruct((s, KV_WIDTH), F32), jax.ShapeDtypeStruct((SUBLANES, LANES), F32)],
                 scratch_shapes=[pltpu.VMEM((BLOCK, KV_WIDTH), F32), pltpu.VMEM((BLOCK, KV_WIDTH), F32)],
                 compiler_params=_params(("arbitrary",)))(qs, ks, ks, vb, vb, d_o, sinks)


def conv_fwd(z, conv_w, mix, aw, cw, b_off):
    s = z.shape[0]
    tm = _tile(s, 1024)
    nseg = cw // COLT
    hb = tm // SUBLANES

    def body(b_ref, c_ref, h_ref, g_ref, cp_ref, hp_ref, w_ref, mix_in, mix_ref):
        i = pl.program_id(0)
        u = c_ref[...] * h_ref[...]
        up = jnp.where(i > 0, cp_ref[...] * hp_ref[...], 0.0)
        ext = jnp.concatenate([up, u], axis=0)
        um1 = pltpu.roll(ext, 1, 0)[SUBLANES:]
        um2 = pltpu.roll(ext, 2, 0)[SUBLANES:]
        w = w_ref[...]
        cv = w[0:1] * um2 + w[1:2] * um1 + w[2:3] * u
        g = g_ref[...]
        mix_ref[...] = ((b_ref[...] * cv) * (g * _sigmoid(g))).astype(BF16)

    seg = lambda k: pl.BlockSpec((tm, COLT), functools.partial(lambda i, j, k: (i, b_off + k * nseg + j), k=k))
    halo = lambda k: pl.BlockSpec(
        (SUBLANES, COLT), functools.partial(lambda i, j, k: (jnp.maximum(i * hb - 1, 0), b_off + k * nseg + j), k=k))
    return _call(body, name="conv_fwd", grid=(s // tm, nseg),
                 in_specs=[seg(0), seg(1), seg(2), seg(3), halo(1), halo(2),
                           pl.BlockSpec((SUBLANES, COLT), lambda i, j: (0, j)), _hbm()],
                 out_specs=pl.BlockSpec((tm, COLT), lambda i, j: (i, aw // COLT + j)),
                 out_shape=jax.ShapeDtypeStruct(mix.shape, BF16),
                 input_output_aliases={7: 0},
                 compiler_params=_params(("parallel", "parallel")))(z, z, z, z, z, z, conv_w, mix)


def conv_bwd(z, d_mix, conv_w, dz, aw, cw, b_off):
    s = z.shape[0]
    tm = _tile(s, 512)
    nseg = cw // COLT
    hb = tm // SUBLANES
    n_row = s // tm
    last_h = s // SUBLANES - 1
    n_step = nseg * n_row

    def body(b_ref, c_ref, h_ref, g_ref, cp_ref, hp_ref, bn_ref, gn_ref, dm_ref, dmn_ref, w_ref, dz_in,
             dz_ref, acc_ref, stash_ref, sems):
        j = pl.program_id(0)
        i = pl.program_id(1)
        step = j * n_row + i
        slot = step % 2

        def copies(from_slot, at_step):
            jj, ii = at_step // n_row, at_step % n_row
            rows = pl.ds(pl.multiple_of(ii * tm, tm), tm)
            return [pltpu.make_async_copy(
                stash_ref.at[from_slot, q],
                dz_ref.at[rows, pl.ds(pl.multiple_of((b_off + q * nseg + jj) * COLT, COLT), COLT)],
                sems.at[from_slot, q]) for q in range(4)]

        @pl.when(step >= 2)
        def _():
            for cp in copies(slot, step - 2):
                cp.wait()

        cc, hh, bb, g = c_ref[...], h_ref[...], b_ref[...], g_ref[...]
        w = w_ref[...]
        u = cc * hh
        up = jnp.where(i > 0, cp_ref[...] * hp_ref[...], 0.0)
        ext = jnp.concatenate([up, u], axis=0)
        um1 = pltpu.roll(ext, 1, 0)[SUBLANES:]
        um2 = pltpu.roll(ext, 2, 0)[SUBLANES:]
        cv = w[0:1] * um2 + w[1:2] * um1 + w[2:3] * u
        sig = _sigmoid(g)
        sg = g * sig
        dm = dm_ref[...]
        d_cv = (dm * bb) * sg
        gn = gn_ref[...]
        d_cv_next = jnp.where(i < n_row - 1, (dmn_ref[...] * bn_ref[...]) * (gn * _sigmoid(gn)), 0.0)
        ext2 = jnp.concatenate([d_cv, d_cv_next], axis=0)
        dp1 = pltpu.roll(ext2, tm + SUBLANES - 1, 0)[:tm]
        dp2 = pltpu.roll(ext2, tm + SUBLANES - 2, 0)[:tm]
        d_u = w[2:3] * d_cv + w[1:2] * dp1 + w[0:1] * dp2
        stash_ref[slot, 0] = ((dm * cv) * sg).astype(BF16)
        stash_ref[slot, 1] = (d_u * hh).astype(BF16)
        stash_ref[slot, 2] = (d_u * cc).astype(BF16)
        stash_ref[slot, 3] = (((dm * bb) * cv) * (sig * (1.0 + g * (1.0 - sig)))).astype(BF16)
        for cp in copies(slot, step):
            cp.start()

        @pl.when(i == 0)
        def _():
            acc_ref[...] = jnp.zeros_like(acc_ref)

        acc_ref[0:1, :] += jnp.sum(d_cv * um2, axis=0, keepdims=True)
        acc_ref[1:2, :] += jnp.sum(d_cv * um1, axis=0, keepdims=True)
        acc_ref[2:3, :] += jnp.sum(d_cv * u, axis=0, keepdims=True)

        @pl.when(step == n_step - 1)
        def _():
            if n_step >= 2:
                for cp in copies(1 - slot, step - 1):
                    cp.wait()
            for cp in copies(slot, step):
                cp.wait()

    seg = lambda q: pl.BlockSpec((tm, COLT), functools.partial(lambda j, i, q: (i, b_off + q * nseg + j), q=q))
    halo_p = lambda q: pl.BlockSpec(
        (SUBLANES, COLT),
        functools.partial(lambda j, i, q: (jnp.maximum(i * hb - 1, 0), b_off + q * nseg + j), q=q))
    halo_n = lambda q: pl.BlockSpec(
        (SUBLANES, COLT),
        functools.partial(lambda j, i, q: (jnp.minimum((i + 1) * hb, last_h), b_off + q * nseg + j), q=q))
    return _call(body, name="conv_bwd", grid=(nseg, n_row),
                 in_specs=[seg(0), seg(1), seg(2), seg(3), halo_p(1), halo_p(2), halo_n(0), halo_n(3),
                           pl.BlockSpec((tm, COLT), lambda j, i: (i, j)),
                           pl.BlockSpec((SUBLANES, COLT), lambda j, i: (jnp.minimum((i + 1) * hb, last_h), j)),
                           pl.BlockSpec((SUBLANES, COLT), lambda j, i: (0, j)), _hbm()],
                 out_specs=[_hbm(), pl.BlockSpec((SUBLANES, COLT), lambda j, i: (0, j))],
                 out_shape=[jax.ShapeDtypeStruct(dz.shape, BF16), jax.ShapeDtypeStruct((SUBLANES, cw), F32)],
                 input_output_aliases={11: 0},
                 scratch_shapes=[pltpu.VMEM((2, 4, tm, COLT), BF16), pltpu.SemaphoreType.DMA((2, 4))],
                 compiler_params=_params(("arbitrary", "arbitrary")))(
                     z, z, z, z, z, z, z, z, d_mix, d_mix, conv_w, dz)


def _place():
    x, y, c = lax.axis_index("x"), lax.axis_index("y"), lax.axis_index("c")
    chips = [(1 - x, y), (x, 1 - y), (1 - x, 1 - y)]
    return x, y, c, chips


def _remote(src, dst, send_sem, recv_sem, device):
    return pltpu.make_async_remote_copy(src_ref=src, dst_ref=dst, send_sem=send_sem, recv_sem=recv_sem,
                                        device_id=device, device_id_type=MESH)


def plan_gather_ici(n_split):
    def plan(refs, send, recv):
        x, y, c, chips = _place()
        me = 2 * x + y
        mine, theirs = [], []
        for w, ref in enumerate(refs):
            for j, (cx, cy) in enumerate(chips):
                def part(slot):
                    if w >= n_split:
                        return ref.at[slot]
                    hr = ref.shape[1] // 2
                    return ref.at[slot, pl.ds(c * hr, hr)]
                k = 3 * w + j
                mine.append(_remote(part(me), part(me), send.at[k], recv.at[k], (cx, cy, c)))
                theirs.append(_remote(part(2 * cx + cy), part(2 * cx + cy), send.at[k], recv.at[k], (cx, cy, c)))
        return mine, theirs
    return plan


def plan_shard_ici(j):
    def plan(refs, send, recv):
        _, _, c, chips = _place()
        own, land = refs
        hr = own.shape[0] // 2
        rows = pl.ds(c * hr, hr)
        cp = _remote(own.at[rows], land.at[rows], send.at[0], recv.at[0], (*chips[j], c))
        return [cp], [cp]
    return plan


def plan_shard_relay(refs, send, recv):
    _, _, c, chips = _place()
    land_x, land_y, land_far = refs
    hr = land_far.shape[0] // 2
    quarter = lambda q: pl.ds(c * hr + q * (hr // 2), hr // 2)
    mine = [_remote(land_y.at[quarter(0)], land_far.at[quarter(0)], send.at[0], recv.at[0], (*chips[0], c)),
            _remote(land_x.at[quarter(1)], land_far.at[quarter(1)], send.at[1], recv.at[1], (*chips[1], c))]
    return mine, mine


def plan_shard_pass(refs, send, recv):
    x, y, c, _ = _place()
    mine, theirs = [], []
    for w, land in enumerate(refs):
        hr = land.shape[0] // 2
        half = lambda core: land.at[pl.ds(core * hr, hr)]
        mine.append(_remote(half(c), half(c), send.at[w], recv.at[w], (x, y, 1 - c)))
        theirs.append(_remote(half(1 - c), half(1 - c), send.at[w], recv.at[w], (x, y, 1 - c)))
    return mine, theirs


def plan_gather_pass(refs, send, recv):
    x, y, c, chips = _place()
    mine, theirs = [], []
    for w, ref in enumerate(refs):
        hr = ref.shape[1] // 2
        for j, (cx, cy) in enumerate(chips):
            half = lambda core: ref.at[2 * cx + cy, pl.ds(core * hr, hr)]
            k = 3 * w + j
            mine.append(_remote(half(c), half(c), send.at[k], recv.at[k], (x, y, 1 - c)))
            theirs.append(_remote(half(1 - c), half(1 - c), send.at[k], recv.at[k], (x, y, 1 - c)))
    return mine, theirs


def plan_swap(refs, send, recv):
    x, y, c, _ = _place()
    nw = len(refs) // 2
    mine = []
    for w in range(nw):
        hr = refs[w].shape[1] // 2
        mine.append(_remote(refs[w].at[:, pl.ds((1 - c) * hr, hr), :], refs[nw + w], send.at[w], recv.at[w],
                            (x, y, 1 - c)))
    return mine, mine


def plan_scatter(refs, send, recv):
    x, y, c, chips = _place()
    me = 2 * x + y
    nw = len(refs) // 2
    mine, theirs = [], []
    for w in range(nw):
        for j, (cx, cy) in enumerate(chips):
            k = 3 * w + j
            mine.append(_remote(refs[w].at[2 * cx + cy], refs[nw + w].at[me], send.at[k], recv.at[k], (cx, cy, c)))
            theirs.append(_remote(refs[w].at[me], refs[nw + w].at[2 * cx + cy], send.at[k], recv.at[k], (cx, cy, c)))
    return mine, theirs


def plan_join(refs, send, recv):
    x, y, c, _ = _place()
    mine, theirs = [], []
    for w, ref in enumerate(refs):
        hr = ref.shape[0] // 2
        half = lambda core: ref.at[pl.ds(core * hr, hr)]
        mine.append(_remote(half(c), half(c), send.at[w], recv.at[w], (x, y, 1 - c)))
        theirs.append(_remote(half(1 - c), half(1 - c), send.at[w], recv.at[w], (x, y, 1 - c)))
    return mine, theirs


def exchange(name, plan, arrays, n, after=()):
    na = len(arrays)

    def body(*refs):
        mine, theirs = plan(refs[:na], refs[2 * na], refs[2 * na + 1])
        for cp in mine:
            cp.start()
        for cp in theirs:
            cp.wait_recv()
        for cp in mine:
            cp.wait_send()

    return _call_after(body, after, name=name, in_specs=[_hbm()] * na, out_specs=[_hbm()] * na,
                       out_shape=[jax.ShapeDtypeStruct(a.shape, a.dtype) for a in arrays],
                       input_output_aliases={i: i for i in range(na)},
                       scratch_shapes=[pltpu.SemaphoreType.DMA((n,)), pltpu.SemaphoreType.DMA((n,))])(*arrays)


def exchange_start(name, groups, arrays, after=()):
    na, ng = len(arrays), len(groups)

    def body(*refs):
        for g, (plan, idx, _) in enumerate(groups):
            mine, _ = plan([refs[i] for i in idx], refs[na + 2 * g], refs[na + 2 * g + 1])
            for cp in mine:
                cp.start()
        refs[-1][...] = jnp.zeros_like(refs[-1])

    hbm = pl.BlockSpec(memory_space=pltpu.HBM)
    sem = pl.BlockSpec(memory_space=pltpu.SEMAPHORE)
    sem_types = [pltpu.SemaphoreType.DMA((n,)) for _, _, n in groups for _ in (0, 1)]
    outs = _call_after(body, after, name=name, in_specs=[hbm] * na,
                       out_specs=[sem] * (2 * ng) + [hbm] * na + [pl.BlockSpec(memory_space=pltpu.VMEM)],
                       out_shape=sem_types + [pltpu.HBM(a.shape, a.dtype) for a in arrays]
                       + [jax.ShapeDtypeStruct((SUBLANES, LANES), F32)],
                       input_output_aliases={i: i + 2 * ng for i in range(na)},
                       compiler_params=pltpu.CompilerParams(
                           has_side_effects=pltpu.SideEffectType.DATAFLOW_SIDE_EFFECTING))(
                               *[pltpu.with_memory_space_constraint(a, pltpu.HBM) for a in arrays])
    sems = [(outs[2 * g], outs[2 * g + 1]) for g in range(ng)]
    return sems, list(outs[2 * ng:-1]), outs[-1]


def exchange_wait(name, plan, sems, arrays, after):
    send_sems, recv_sems = sems
    na = len(arrays)
    after = tuple(after) if isinstance(after, (tuple, list)) else (after,)

    def body(*refs):
        mine, theirs = plan(refs[:na], refs[na], refs[na + 1])
        for cp in mine:
            cp.wait_send()
        for cp in theirs:
            cp.wait_recv()

    hbm = pl.BlockSpec(memory_space=pltpu.HBM)
    sem = pl.BlockSpec(memory_space=pltpu.SEMAPHORE)
    return _call(body, name=name, in_specs=[hbm] * na + [sem, sem] + [_hbm()] * len(after),
                 out_specs=[hbm] * na, out_shape=[pltpu.HBM(a.shape, a.dtype) for a in arrays],
                 input_output_aliases={i: i for i in range(na)},
                 compiler_params=pltpu.CompilerParams(
                     has_side_effects=pltpu.SideEffectType.DATAFLOW_SIDE_EFFECTING))(
                         *arrays, send_sems, recv_sems, *after)


def plan_allgather_small(refs, send, recv):
    x, y, c, _ = _place()
    buf, = refs
    mine, theirs = [], []
    flips = [(fx, fy, fc) for fx in (0, 1) for fy in (0, 1) for fc in (0, 1)][1:]
    for k, (fx, fy, fc) in enumerate(flips):
        peer = (x ^ fx, y ^ fy, c ^ fc)
        slot = lambda px, py, pc: buf.at[4 * px + 2 * py + pc]
        mine.append(_remote(slot(x, y, c), slot(x, y, c), send.at[k], recv.at[k], peer))
        theirs.append(_remote(slot(*peer), slot(*peer), send.at[k], recv.at[k], peer))
    return mine, theirs


def add_sibling(grad, got, core, name):
    _, r, c = grad.shape
    hr = r // 2
    tr = _tile(hr, 512)
    nblk = hr // tr

    def body(core_ref, g_ref, o_ref, out_ref):
        out_ref[...] = (g_ref[...].astype(F32) + o_ref[...].astype(F32)).astype(BF16)

    grid_spec = pltpu.PrefetchScalarGridSpec(
        num_scalar_prefetch=1, grid=(N_CHIPS, nblk),
        in_specs=[pl.BlockSpec((None, tr, c), lambda t, i, core_ref: (t, core_ref[0] * nblk + i, 0)),
                  pl.BlockSpec((None, tr, c), lambda t, i, core_ref: (t, i, 0))],
        out_specs=pl.BlockSpec((None, tr, c), lambda t, i, core_ref: (t, i, 0)))
    return _call(body, name=name, grid_spec=grid_spec,
                 out_shape=jax.ShapeDtypeStruct((N_CHIPS, hr, c), BF16),
                 compiler_params=_params(("parallel", "parallel")))(core, grad, got)


def sum_chips(mine, owned, place, name):
    _, hr, c = mine.shape
    tr = _tile(hr, 512)
    nblk = hr // tr

    def body(place_ref, m_ref, o1_ref, o2_ref, o3_ref, out_ref):
        acc = m_ref[...].astype(F32)
        for o_ref in (o1_ref, o2_ref, o3_ref):
            acc = acc + o_ref[...].astype(F32)
        out_ref[...] = acc

    other = lambda k: pl.BlockSpec((None, tr, c), lambda i, place_ref: ((place_ref[0] + k) % N_CHIPS, i, 0))
    grid_spec = pltpu.PrefetchScalarGridSpec(
        num_scalar_prefetch=1, grid=(nblk,),
        in_specs=[other(0), other(1), other(2), other(3)],
        out_specs=pl.BlockSpec((tr, c), lambda i, place_ref: (place_ref[1] * nblk + i, 0)))
    return _call(body, name=name, grid_spec=grid_spec,
                 out_shape=jax.ShapeDtypeStruct((2 * hr, c), F32),
                 compiler_params=_params(("parallel",)))(place, mine, owned, owned, owned)


def sum_devices(gathered):
    _, rows, width = gathered.shape

    def body(g_ref, out_ref):
        acc = g_ref[0]
        for dev in range(1, 8):
            acc = acc + g_ref[dev]
        out_ref[...] = acc
        tail = acc[SUBLANES:]
        out_ref[SUBLANES:, :] = jnp.broadcast_to(jnp.sum(tail, axis=1, keepdims=True), tail.shape)

    return _call(body, name="sum_devices",
                 in_specs=[pl.BlockSpec(memory_space=pltpu.VMEM)],
                 out_specs=pl.BlockSpec(memory_space=pltpu.VMEM),
                 out_shape=jax.ShapeDtypeStruct((rows, width), F32))(gathered)


def _rope_tables(s):
    half = ROT_DIM // 2
    inv_freq = jnp.power(jnp.float32(ROPE_THETA), -jnp.arange(half, dtype=F32) * 2.0 / ROT_DIM)
    freq64 = jnp.concatenate([inv_freq, inv_freq, jnp.zeros((HEAD_DIM - ROT_DIM,), F32)])
    freq = jnp.concatenate([freq64, freq64])[None, :]
    dim = (jnp.arange(LANES) % HEAD_DIM)[None, :]
    ang = jnp.arange(s).astype(F32)[:, None] * freq
    cos, sin = jnp.cos(ang), jnp.sin(ang)
    return cos, jnp.where(dim < half, -sin, 0.0), jnp.where((dim >= half) & (dim < ROT_DIM), sin, 0.0)


def _pad_rows(a, rows):
    return jnp.pad(a, ((0, rows - a.shape[0]), (0, 0)))


def _pad_cols(a, cols):
    return jnp.pad(a, ((0, 0), (0, cols - a.shape[1])))


def kernel(x, p, norm_gain, w_in, q_norm_gain, k_norm_gain, attn_sinks, conv_w, w_out, ple_gate_norm_gain, w_ple_gate, b_ple_gate, w_ple_proj, ple_norm_gain, loss_target, m_norm_gain, m_w_in, m_q_norm_gain, m_k_norm_gain, m_attn_sinks, m_conv_w, m_w_out, m_ple_gate_norm_gain, m_w_ple_gate, m_b_ple_gate, m_w_ple_proj, m_ple_norm_gain, v_norm_gain, v_w_in, v_q_norm_gain, v_k_norm_gain, v_attn_sinks, v_conv_w, v_w_out, v_ple_gate_norm_gain, v_w_ple_gate, v_b_ple_gate, v_w_ple_proj, v_ple_norm_gain):
    x2, p2, tgt = x[0], p[0, 0], loss_target[0]
    s, d = x2.shape
    ple = p2.shape[1]
    aw = d // 2
    cw = d - aw
    nq = aw // HEAD_DIM
    sh = w_in.shape[2]
    in_w = N_CHIPS * sh
    dq = d // N_CHIPS
    cq = cw // N_CHIPS
    ga_off = (aw + 2 * KV_WIDTH) // COLT
    b_off = (2 * aw + 2 * KV_WIDTH) // COLT
    assert in_w == 2 * aw + 2 * KV_WIDTH + 4 * cw and aw % COLT == 0 and cw % COLT == 0
    assert s % BLOCK == 0 and sh % LANES == 0 and nq % (2 * N_KV_HEADS) == 0

    core = lax.axis_index("c").astype(jnp.int32).reshape(1)
    chip = 2 * lax.axis_index("x") + lax.axis_index("y")

    chip1 = chip.astype(jnp.int32).reshape(1)
    place = jnp.concatenate([chip1, core])
    w_in_own = cast_bf16(w_in[0], "cast_w_in")
    rest = [cast_into_slot(w_out[0], chip1, "cast_w_out"), cast_into_slot(w_ple_gate[0], chip1, "cast_w_pg"),
            cast_into_slot(w_ple_proj[0], chip1, "cast_w_pp"),
            lax.dynamic_update_slice(jnp.zeros((N_CHIPS, SUBLANES, cq), F32),
                                     _pad_rows(conv_w[0], SUBLANES)[None], (chip, 0, 0))]
    order = jnp.stack([chip, chip ^ 2, chip ^ 1, chip ^ 3]).astype(jnp.int32)
    wi_sems, wi_arrs, wi_token = exchange_start(
        "gather_wi_start", [(plan_shard_ici(j), [0, 1 + j], 1) for j in range(2)],
        [w_in_own] + [lax.empty((d, sh), BF16) for _ in range(3)])

    tn = _tile(d, 1024)
    ts = _tile(s, 2048)
    tabs = _rope_tables(s)
    qg = jnp.tile(q_norm_gain, (1, LANES // HEAD_DIM))
    kg = jnp.tile(k_norm_gain, (1, LANES // HEAD_DIM))
    seg_i = jnp.arange(SEG) // HEAD_DIM
    segb = (seg_i[:, None] == seg_i[None, :]).astype(BF16)
    h, z = norm_in_proj(x2, norm_gain, wi_arrs[0], order, in_w)
    own, land_x = exchange_wait("gather_wi_wait0", plan_shard_ici(0), wi_sems[0], [wi_arrs[0], wi_arrs[1]],
                                (z,) + tuple(tabs) + tuple(rest[:2]))
    own, land_y = exchange_wait("gather_wi_wait1", plan_shard_ici(1), wi_sems[1], [own, wi_arrs[2]], land_x)
    relay_sems, relayed, relay_token = exchange_start(
        "gather_wi_relay_start", [(plan_shard_relay, [0, 1, 2], 2)], [land_x, land_y, wi_arrs[3]])
    rest_sems, rest, rest_token = exchange_start(
        "gather_rest_start", [(plan_gather_ici(3), [0, 1, 2, 3], 12)], rest, after=(relay_token,))
    land_x, land_y = exchange("gather_wi_pass01", plan_shard_pass, relayed[:2], 2, after=(rest_token,))
    z = in_proj_part(h, land_x, z, order, 1, in_w)
    z = in_proj_part(h, land_y, z, order, 2, in_w)
    land_x, land_y, land_far = exchange_wait("gather_wi_relay_wait", plan_shard_relay, relay_sems[0],
                                             [land_x, land_y, relayed[2]], z)
    land_far, = exchange("gather_wi_pass2", plan_shard_pass, [land_far], 1)
    z = in_proj_part(h, land_far, z, order, 3, in_w)
    w_shards = [own, land_x, land_y, land_far]
    wo_all, wg_all, wp_all, conv_all = exchange_wait("gather_rest_wait", plan_gather_ici(3), rest_sems[0], rest, z)
    pass_sems, passed, pass_token = exchange_start(
        "gather_rest_pass_start", [(plan_gather_pass, [0, 1, 2], 9)], [wo_all, wg_all, wp_all])
    conv_full = conv_all.transpose(1, 0, 2).reshape(SUBLANES, cw)
    qs, ks, vb = qk_prep_fwd(z, tabs, qg, kg, segb, aw, after=(pass_token,))
    attn, mix = attn_fwd(qs, ks, vb, z, attn_sinks, aw, d, ga_off)
    mix = conv_fwd(z, conv_full, mix, aw, cw, b_off)
    wo_all, wg_all, wp_all = exchange_wait("gather_rest_pass_wait", plan_gather_pass, pass_sems[0], passed, mix)
    wo_full = wo_all.reshape(d, d)
    wg_full = wg_all.reshape(d, d)
    x1, hg = out_proj_norm(mix, wo_full, x2, ple_gate_norm_gain)
    pq = d // N_CHIPS

    d_gl, d_pe, dy, acc_head = head_fwd_bwd(x1, hg, wg_full, p2, wp_all, tgt, b_ple_gate, ple_norm_gain)
    wgrad = lambda a_cols, shape3, o_spec, b_cols, name, a, b, grid: matmul(
        a, b, grid=grid,
        a_spec=pl.BlockSpec((ts, a_cols), lambda i, j, k: (k, i)),
        b_spec=pl.BlockSpec((ts, b_cols), lambda i, j, k: (k, j)),
        o_spec=o_spec, out_shape=jax.ShapeDtypeStruct(shape3, BF16), dims=TN, name=name)
    g_wg = wgrad(tn, (d, d), pl.BlockSpec((tn, tn), lambda i, j, k: (i, j)), tn, "mm_g_wg", hg, d_gl,
                 (d // tn, d // tn, s // ts))
    g_wp = wgrad(ple, (N_CHIPS, ple, pq), pl.BlockSpec((None, ple, pq), lambda i, j, k: (j, 0, 0)), pq,
                 "mm_g_wp", p2, d_pe, (1, N_CHIPS, s // ts))
    d_x1, d_x1b, acc_g = gate_proj_bwd_norm(d_gl, wg_full, x1, dy, ple_gate_norm_gain)
    g_wo = wgrad(tn, (d, d), pl.BlockSpec((tn, tn), lambda i, j, k: (i, j)), tn, "mm_g_wo", mix, d_x1b,
                 (d // tn, d // tn, s // ts))
    def reduce_start(tag, grads):
        n = len(grads)
        lands = [lax.empty((N_CHIPS, a.shape[1] // 2, a.shape[2]), BF16) for a in grads]
        swapped = exchange("swap_" + tag, plan_swap, list(grads) + lands, n)
        parts = [add_sibling(g, o, core, "add_sibling_%s%d" % (tag, i))
                 for i, (g, o) in enumerate(zip(swapped[:n], swapped[n:]))]
        return exchange_start("scatter_%s_start" % tag, [(plan_scatter, list(range(2 * n)), 3 * n)],
                              parts + [lax.empty(a.shape, BF16) for a in parts])

    def reduce_sum(tag, handle, after):
        sems, arrays, _ = handle
        n = len(arrays) // 2
        got = exchange_wait("scatter_%s_wait" % tag, plan_scatter, sems[0], arrays, after)
        return [sum_chips(pt, o, place, "sum_chips_%s%d" % (tag, i))
                for i, (pt, o) in enumerate(zip(got[:n], got[n:]))]

    early = reduce_start("early", [g_wo.reshape(N_CHIPS, dq, d), g_wg.reshape(N_CHIPS, dq, d), g_wp])
    d_o, d_gate, d_mix_conv = out_proj_bwd_gate(d_x1b, wo_full, attn, z, aw, ga_off, after=(early[-1],))
    dqs, dks, dvs, acc_sink = attn_bwd(qs, ks, vb, d_o, attn_sinks, aw)
    dz, acc_qk = qk_prep_bwd(z, dqs, dks, dvs, d_gate, tabs, qg, kg, segb, aw)
    dz, acc_conv = conv_bwd(z, d_mix_conv, conv_full, dz, aw, cw, b_off)
    join_sems, joining, join_token = exchange_start(
        "join_early_start", [(plan_join, [0, 1, 2], 3)], reduce_sum("early", early, dz))
    g_send = in_proj_wgrad_half(h, dz, None, 1 - core, after=(join_token,))
    swap_sems, swapping, swap_token = exchange_start(
        "swap_late_start", [(plan_swap, [0, 1], 1)], [g_send, lax.empty((N_CHIPS, d // 2, sh), BF16)])
    g_wi = in_proj_wgrad_half(h, dz, swapping[0], core, after=(swap_token,))
    full_wo, full_wg, full_wp = exchange_wait("join_early_wait", plan_join, join_sems[0], joining, g_wi)
    g_wi, got_wi = exchange_wait("swap_late_wait", plan_swap, swap_sems[0], [g_wi, swapping[1]], full_wo)
    part_wi = add_sibling(g_wi, got_wi, core, "add_sibling_late0")
    late = exchange_start("scatter_late_start", [(plan_scatter, [0, 1], 3)],
                          [part_wi, lax.empty(part_wi.shape, BF16)])
    grad_x, acc_x = in_proj_bwd_norm(dz, w_shards, order, x2, d_x1, norm_gain, after=(late[-1],))

    wsm = max(d, cw)
    misc = jnp.concatenate([acc_qk[0:1, :HEAD_DIM], acc_qk[1:2, :HEAD_DIM], acc_sink[0:1, :nq]], axis=1)
    small = jnp.concatenate([
        _pad_cols(acc_x[0:1], wsm), _pad_cols(acc_g[0:1], wsm), _pad_cols(acc_head[0:1], wsm),
        _pad_cols(acc_head[1:2], wsm), _pad_cols(misc, wsm), _pad_cols(acc_conv[0:3], wsm),
        _pad_rows(_pad_cols(acc_head[2:3], wsm), SUBLANES)], axis=0)
    device = 2 * chip + lax.axis_index("c")
    small_sems, small_bufs, small_token = exchange_start(
        "small_start", [(plan_allgather_small, [0], 7)],
        [lax.dynamic_update_slice(jnp.zeros((8,) + small.shape, F32), small[None], (device, 0, 0))])
    last_sems, last_halves, last_token = exchange_start(
        "join_late_start", [(plan_join, [0], 1)], reduce_sum("late", late, small_token))
    big, after = {}, (last_token,)
    for nm, g, w, m, v in (("w_out", full_wo, w_out, m_w_out, v_w_out),
                           ("w_ple_gate", full_wg, w_ple_gate, m_w_ple_gate, v_w_ple_gate),
                           ("w_ple_proj", full_wp, w_ple_proj, m_w_ple_proj, v_w_ple_proj)):
        stepped = adamw(g, w[0], m[0], v[0], "adamw_" + nm, after=after)
        big[nm], after = [o[None] for o in stepped], (stepped[1],)
    full_wi, = exchange_wait("join_late_wait", plan_join, last_sems[0], last_halves, after[0])
    big["w_in"] = [o[None] for o in adamw(full_wi, w_in[0], m_w_in[0], v_w_in[0], "adamw_w_in")]

    gathered, = exchange_wait("small_wait", plan_allgather_small, small_sems[0], small_bufs, big["w_in"][1])
    tot = sum_devices(gathered)
    loss = tot[SUBLANES, 0]

    def pack(vals):
        ng, pg, bg, eg, qgv, kgv, sk, cv = vals
        misc_v = jnp.concatenate([qgv, kgv, sk], axis=1)
        return jnp.concatenate([_pad_cols(ng, wsm), _pad_cols(pg, wsm), _pad_cols(bg, wsm), _pad_cols(eg, wsm),
                                _pad_cols(misc_v, wsm), _pad_cols(cv[0], wsm)], axis=0)

    g_small = jnp.concatenate(
        [tot[0:5], _pad_cols(lax.dynamic_slice(tot[5:8], (0, chip * cq), (3, cq)), wsm)], axis=0)
    w_small = pack((norm_gain, ple_gate_norm_gain, b_ple_gate, ple_norm_gain, q_norm_gain, k_norm_gain,
                    attn_sinks, conv_w))
    m_small = pack((m_norm_gain, m_ple_gate_norm_gain, m_b_ple_gate, m_ple_norm_gain, m_q_norm_gain,
                    m_k_norm_gain, m_attn_sinks, m_conv_w))
    v_small = pack((v_norm_gain, v_ple_gate_norm_gain, v_b_ple_gate, v_ple_norm_gain, v_q_norm_gain,
                    v_k_norm_gain, v_attn_sinks, v_conv_w))
    sm = adamw(g_small, w_small, m_small, v_small, "adamw_small")

    def unpack(a):
        return {"norm_gain": a[0:1, :d], "ple_gate_norm_gain": a[1:2, :d], "b_ple_gate": a[2:3, :d],
                "ple_norm_gain": a[3:4, :d], "q_norm_gain": a[4:5, :HEAD_DIM],
                "k_norm_gain": a[4:5, HEAD_DIM:2 * HEAD_DIM],
                "attn_sinks": a[4:5, 2 * HEAD_DIM:2 * HEAD_DIM + nq], "conv_w": a[5:8, :cq][None]}

    order = ("norm_gain", "w_in", "q_norm_gain", "k_norm_gain", "attn_sinks", "conv_w", "w_out",
             "ple_gate_norm_gain", "w_ple_gate", "b_ple_gate", "w_ple_proj", "ple_norm_gain")
    outs = [loss, grad_x[None]]
    for kind in range(4):
        table = unpack(sm[kind])
        for nm in order:
            outs.append(big[nm][kind] if nm in big else table[nm])
    return tuple(outs)
```

```python
import functools

import jax
import jax.numpy as jnp
from jax import lax
from jax.experimental import pallas as pl
from jax.experimental.pallas import tpu as pltpu

F32 = jnp.float32
BF16 = jnp.bfloat16
MESH = pl.DeviceIdType.MESH

HEAD_DIM = 64
N_KV_HEADS = 4
KV_WIDTH = N_KV_HEADS * HEAD_DIM
BLOCK = 128
ROT_DIM = 16
ROPE_THETA = 500000.0
EPS = 1e-6
NEG_INF = -1e30
N_CHIPS = 4
LANES = 128
SUBLANES = 8
COLT = 512
SEG = 256
VMEM_LIMIT = 48 * 1024 * 1024
VMEM_LIMIT_BIG = 60 * 1024 * 1024

ADAM_LR = 0.001
ADAM_B1 = 0.9
ADAM_B2 = 0.999
ADAM_EPS = 1e-08
ADAM_WD = 0.01
ADAM_STEP = 10

NN = (((1,), (0,)), ((), ()))
NT = (((1,), (1,)), ((), ()))
TN = (((0,), (0,)), ((), ()))


def _call(body, **kw):
    return pl.pallas_call(body, **kw)


def _call_after(body, after, **kw):
    n_in, n_after = len(kw["in_specs"]), len(after)
    kw["in_specs"] = list(kw["in_specs"]) + [pl.BlockSpec(memory_space=pl.ANY)] * n_after

    def body_after(*refs):
        body(*refs[:n_in], *refs[n_in + n_after:])

    call = _call(body_after, **kw)
    return lambda *args: call(*args, *after)


def _params(sem, vmem=VMEM_LIMIT):
    return pltpu.CompilerParams(dimension_semantics=sem, vmem_limit_bytes=vmem)


def _tile(n, pref):
    return pref if n % pref == 0 else n


def _sigmoid(v):
    return 1.0 / (1.0 + jnp.exp(-v))


def _hbm():
    return pl.BlockSpec(memory_space=pl.ANY)


def cast_bf16(a, name):
    r, c = a.shape
    tr = _tile(r, 512)

    def body(a_ref, o_ref):
        o_ref[...] = a_ref[...].astype(BF16)

    return _call(body, name=name, grid=(r // tr,),
                 in_specs=[pl.BlockSpec((tr, c), lambda i: (i, 0))],
                 out_specs=pl.BlockSpec((tr, c), lambda i: (i, 0)),
                 out_shape=pltpu.HBM((r, c), BF16),
                 compiler_params=_params(("parallel",)))(a)


def cast_into_slot(a, chip, name):
    r, c = a.shape
    tr = _tile(r, 512)

    def body(chip_ref, a_ref, o_ref):
        o_ref[...] = a_ref[...].astype(BF16)

    grid_spec = pltpu.PrefetchScalarGridSpec(
        num_scalar_prefetch=1, grid=(r // tr,),
        in_specs=[pl.BlockSpec((tr, c), lambda i, chip_ref: (i, 0))],
        out_specs=pl.BlockSpec((None, tr, c), lambda i, chip_ref: (chip_ref[0], i, 0)))
    return _call(body, name=name, grid_spec=grid_spec,
                 out_shape=pltpu.HBM((N_CHIPS, r, c), BF16),
                 compiler_params=_params(("parallel",)))(chip, a)


def out_proj_norm(mix, wo, x, gain):
    s, d = x.shape
    tm = _tile(s, 512)

    def body(m_ref, w_ref, x_ref, g_ref, x1_ref, hg_ref):
        x1 = x_ref[...] + jnp.dot(m_ref[...], w_ref[...], preferred_element_type=F32)
        x1_ref[...] = x1
        r = lax.rsqrt(jnp.mean(x1 * x1, axis=-1, keepdims=True) + EPS)
        hg_ref[...] = ((x1 * r) * g_ref[...]).astype(BF16)

    row = pl.BlockSpec((tm, d), lambda i: (i, 0))
    return _call(body, name="out_proj_norm", grid=(s // tm,),
                 in_specs=[row, pl.BlockSpec((d, d), lambda i: (0, 0), pipeline_mode=pl.Buffered(1)), row,
                           pl.BlockSpec((1, d), lambda i: (0, 0))],
                 out_specs=[row, row],
                 out_shape=[jax.ShapeDtypeStruct((s, d), F32), jax.ShapeDtypeStruct((s, d), BF16)],
                 compiler_params=_params(("parallel",), VMEM_LIMIT_BIG))(mix, wo, x, gain)


def gate_proj_bwd_norm(d_gl, wg, xin, add, gain):
    s, d = xin.shape
    tm = _tile(s, 512)

    def body(dg_ref, w_ref, x_ref, a_ref, g_ref, dx_ref, dxb_ref, acc_ref):
        i = pl.program_id(0)
        dyv = lax.dot_general(dg_ref[...], w_ref[...], NT, preferred_element_type=F32)
        xf = x_ref[...]
        r = lax.rsqrt(jnp.mean(xf * xf, axis=-1, keepdims=True) + EPS)
        xhat = xf * r
        gd = dyv * g_ref[...]
        dx = a_ref[...] + r * (gd - xhat * jnp.mean(xhat * gd, axis=-1, keepdims=True))
        dx_ref[...] = dx
        dxb_ref[...] = dx.astype(BF16)

        @pl.when(i == 0)
        def _():
            acc_ref[...] = jnp.zeros_like(acc_ref)

        acc_ref[0:1, :] += jnp.sum(dyv * xhat, axis=0, keepdims=True)

    row = pl.BlockSpec((tm, d), lambda i: (i, 0))
    return _call(body, name="gate_proj_bwd_norm", grid=(s // tm,),
                 in_specs=[row, pl.BlockSpec((d, d), lambda i: (0, 0), pipeline_mode=pl.Buffered(1)), row, row,
                           pl.BlockSpec((1, d), lambda i: (0, 0))],
                 out_specs=[row, row, pl.BlockSpec((SUBLANES, d), lambda i: (0, 0))],
                 out_shape=[jax.ShapeDtypeStruct((s, d), F32), jax.ShapeDtypeStruct((s, d), BF16),
                            jax.ShapeDtypeStruct((SUBLANES, d), F32)],
                 compiler_params=_params(("arbitrary",), VMEM_LIMIT_BIG))(d_gl, wg, xin, add, gain)


def head_fwd_bwd(x1, hg, wg, p, wp, tgt, bias, ple_gain):
    s, d = x1.shape
    tm = _tile(s, 256)

    def body(x1_ref, hg_ref, wg_ref, p_ref, wp_ref, t_ref, b_ref, g_ref, dgl_ref, dpe_ref, dy_ref, acc_ref):
        i = pl.program_id(0)
        gl = jnp.dot(hg_ref[...], wg_ref[...], preferred_element_type=F32)
        pb = p_ref[...].astype(BF16)
        pev = jnp.concatenate([jnp.dot(pb, wp_ref[q], preferred_element_type=F32) for q in range(N_CHIPS)],
                              axis=1)
        r = lax.rsqrt(jnp.mean(pev * pev, axis=-1, keepdims=True) + EPS)
        pehat = pev * r
        gain = g_ref[...]
        e = pehat * gain
        gate = _sigmoid(gl + b_ref[...])
        diff = (x1_ref[...] + gate * e) - t_ref[...]
        dy = diff * (1.0 / d)
        dy_ref[...] = dy
        d_gl = (dy * e) * (gate * (1.0 - gate))
        dgl_ref[...] = d_gl.astype(BF16)
        d_e = dy * gate
        gd = d_e * gain
        d_pe = r * (gd - pehat * jnp.mean(pehat * gd, axis=-1, keepdims=True))
        dpe_ref[...] = d_pe.astype(BF16)

        @pl.when(i == 0)
        def _():
            acc_ref[...] = jnp.zeros_like(acc_ref)

        acc_ref[0:1, :] += jnp.sum(d_gl, axis=0, keepdims=True)
        acc_ref[1:2, :] += jnp.sum(d_e * pehat, axis=0, keepdims=True)
        acc_ref[2:3, :] += jnp.sum(diff * diff, axis=0, keepdims=True) * (0.5 / d)

    row = pl.BlockSpec((tm, d), lambda i: (i, 0))
    vec = pl.BlockSpec((1, d), lambda i: (0, 0))
    return _call(body, name="head_fwd_bwd", grid=(s // tm,),
                 in_specs=[row, row, pl.BlockSpec((d, d), lambda i: (0, 0), pipeline_mode=pl.Buffered(1)),
                           pl.BlockSpec((tm, p.shape[1]), lambda i: (i, 0)),
                           pl.BlockSpec(wp.shape, lambda i: (0, 0, 0)), row, vec, vec],
                 out_specs=[row, row, row, pl.BlockSpec((SUBLANES, d), lambda i: (0, 0))],
                 out_shape=[jax.ShapeDtypeStruct((s, d), BF16), jax.ShapeDtypeStruct((s, d), BF16),
                            jax.ShapeDtypeStruct((s, d), F32), jax.ShapeDtypeStruct((SUBLANES, d), F32)],
                 compiler_params=_params(("arbitrary",), VMEM_LIMIT_BIG))(
                     x1, hg, wg, p, wp, tgt, bias, ple_gain)


def adamw(g, w, m, v, name, after=()):
    r, c = g.shape
    tr = _tile(r, 256)

    def body(g_ref, w_ref, m_ref, v_ref, go_ref, d_ref, mo_ref, vo_ref):
        gv = g_ref[...]
        mn = ADAM_B1 * m_ref[...] + (1.0 - ADAM_B1) * gv
        vn = ADAM_B2 * v_ref[...] + (1.0 - ADAM_B2) * (gv * gv)
        m_hat = mn / (1.0 - ADAM_B1 ** ADAM_STEP)
        v_hat = vn / (1.0 - ADAM_B2 ** ADAM_STEP)
        go_ref[...] = gv
        d_ref[...] = -ADAM_LR * (m_hat / (jnp.sqrt(v_hat) + ADAM_EPS) + ADAM_WD * w_ref[...])
        mo_ref[...] = mn
        vo_ref[...] = vn

    blk = pl.BlockSpec((tr, c), lambda i: (i, 0))
    shp = jax.ShapeDtypeStruct((r, c), F32)
    return _call_after(body, after, name=name, grid=(r // tr,), in_specs=[blk] * 4, out_specs=[blk] * 4,
                       out_shape=[shp] * 4, compiler_params=_params(("parallel",)))(g, w, m, v)


def matmul(a, b, *, grid, a_spec, b_spec, o_spec, out_shape, dims, name, res=None, res_spec=None, after=(),
           vmem=VMEM_LIMIT):
    nk = grid[2]
    acc_shape = tuple(d for d in o_spec.block_shape if d is not None)

    def body(*refs):
        a_ref, b_ref = refs[:2]
        r_ref = refs[2] if res is not None else None
        o_ref = refs[3] if res is not None else refs[2]
        part = lax.dot_general(a_ref[...].astype(BF16), b_ref[...].astype(BF16), dims, preferred_element_type=F32)

        def finish(out):
            if res is not None:
                out = r_ref[...] + out
            o_ref[...] = out.astype(o_ref.dtype)

        if nk == 1:
            finish(part)
            return
        acc_ref = refs[-1]
        k = pl.program_id(2)

        @pl.when(k == 0)
        def _():
            acc_ref[...] = part

        @pl.when((k > 0) & (k < nk - 1))
        def _():
            acc_ref[...] += part

        @pl.when(k == nk - 1)
        def _():
            finish(acc_ref[...] + part)

    in_specs = [a_spec, b_spec] + ([res_spec] if res is not None else [])
    args = (a, b) + ((res,) if res is not None else ())
    return _call_after(body, after, name=name, grid=grid, in_specs=in_specs, out_specs=o_spec,
                       out_shape=out_shape, scratch_shapes=[pltpu.VMEM(acc_shape, F32)] if nk > 1 else [],
                       compiler_params=_params(("parallel", "parallel", "arbitrary"), vmem))(*args)


def norm_in_proj(x, gain, w, order, in_w):
    s, d = x.shape
    sh = w.shape[1]
    tm = _tile(s, 512)

    def body(order_ref, x_ref, g_ref, w_ref, h_ref, z_ref):
        xf = x_ref[...]
        r = lax.rsqrt(jnp.mean(xf * xf, axis=-1, keepdims=True) + EPS)
        hb = ((xf * r) * g_ref[...]).astype(BF16)
        h_ref[...] = hb
        z_ref[...] = jnp.dot(hb, w_ref[...], preferred_element_type=F32)

    grid_spec = pltpu.PrefetchScalarGridSpec(
        num_scalar_prefetch=1, grid=(s // tm,),
        in_specs=[pl.BlockSpec((tm, d), lambda i, order_ref: (i, 0)),
                  pl.BlockSpec((1, d), lambda i, order_ref: (0, 0)),
                  pl.BlockSpec((d, sh), lambda i, order_ref: (0, 0), pipeline_mode=pl.Buffered(1))],
        out_specs=[pl.BlockSpec((tm, d), lambda i, order_ref: (i, 0)),
                   pl.BlockSpec((tm, sh), lambda i, order_ref: (i, order_ref[0]))])
    return _call(body, name="norm_in_proj", grid_spec=grid_spec,
                 out_shape=[jax.ShapeDtypeStruct((s, d), BF16), jax.ShapeDtypeStruct((s, in_w), F32)],
                 compiler_params=_params(("parallel",)))(order, x, gain, w)


def in_proj_part(h, w, z_prev, order, q, in_w):
    s, d = h.shape
    sh = w.shape[1]
    tm = _tile(s, 512)

    def body(order_ref, h_ref, w_ref, *rest):
        rest[-1][...] = jnp.dot(h_ref[...], w_ref[...], preferred_element_type=F32)

    in_specs = [pl.BlockSpec((tm, d), lambda i, order_ref: (i, 0)),
                pl.BlockSpec((d, sh), lambda i, order_ref: (0, 0))]
    args = [order, h, w]
    if z_prev is not None:
        in_specs.append(_hbm())
        args.append(z_prev)
    grid_spec = pltpu.PrefetchScalarGridSpec(
        num_scalar_prefetch=1, grid=(s // tm,), in_specs=in_specs,
        out_specs=pl.BlockSpec((tm, sh), lambda i, order_ref: (i, order_ref[q])))
    return _call(body, name="mm_z%d" % q, grid_spec=grid_spec,
                 out_shape=jax.ShapeDtypeStruct((s, in_w), F32),
                 input_output_aliases={3: 0} if z_prev is not None else {},
                 compiler_params=_params(("parallel",)))(*args)


def in_proj_wgrad_half(h, dz, g_prev, half, after):
    s, d = h.shape
    sh = dz.shape[1] // N_CHIPS
    hr = d // 2

    def body(half_ref, h_ref, dz_ref, *rest):
        rest[-1][...] = lax.dot_general(h_ref[...], dz_ref[...], TN, preferred_element_type=F32).astype(BF16)

    extra = ([g_prev] if g_prev is not None else []) + list(after)
    grid_spec = pltpu.PrefetchScalarGridSpec(
        num_scalar_prefetch=1, grid=(N_CHIPS,),
        in_specs=[pl.BlockSpec((s, hr), lambda j, half_ref: (0, half_ref[0]), pipeline_mode=pl.Buffered(1)),
                  pl.BlockSpec((s, sh), lambda j, half_ref: (0, j))] + [_hbm()] * len(extra),
        out_specs=pl.BlockSpec((None, hr, sh), lambda j, half_ref: (j, half_ref[0], 0)))
    return _call(body, name="mm_g_wi_%s" % ("keep" if g_prev is not None else "send"), grid_spec=grid_spec,
                 out_shape=jax.ShapeDtypeStruct((N_CHIPS, d, sh), BF16),
                 input_output_aliases={3: 0} if g_prev is not None else {},
                 compiler_params=_params(("parallel",), VMEM_LIMIT_BIG))(half, h, dz, *extra)


def in_proj_bwd_norm(dz, ws, order, xin, add, gain, after):
    s, d = xin.shape
    sh = ws[0].shape[1]
    tm = _tile(s, 256)
    nq, n_after = len(ws), len(after)

    def body(order_ref, *refs):
        a_refs, b_refs = refs[:nq], refs[nq:2 * nq]
        x_ref, add_ref, g_ref = refs[2 * nq:2 * nq + 3]
        dx_ref, acc_ref = refs[2 * nq + 3 + n_after:]
        i = pl.program_id(0)
        dyv = lax.dot_general(a_refs[0][...], b_refs[0][...], NT, preferred_element_type=F32)
        for q in range(1, nq):
            dyv += lax.dot_general(a_refs[q][...], b_refs[q][...], NT, preferred_element_type=F32)
        xf = x_ref[...]
        r = lax.rsqrt(jnp.mean(xf * xf, axis=-1, keepdims=True) + EPS)
        xhat = xf * r
        gd = dyv * g_ref[...]
        dx_ref[...] = add_ref[...] + r * (gd - xhat * jnp.mean(xhat * gd, axis=-1, keepdims=True))

        @pl.when(i == 0)
        def _():
            acc_ref[...] = jnp.zeros_like(acc_ref)

        acc_ref[0:1, :] += jnp.sum(dyv * xhat, axis=0, keepdims=True)

    a_spec = lambda q: pl.BlockSpec((tm, sh), functools.partial(lambda i, order_ref, q: (i, order_ref[q]), q=q))
    row = pl.BlockSpec((tm, d), lambda i, order_ref: (i, 0))
    grid_spec = pltpu.PrefetchScalarGridSpec(
        num_scalar_prefetch=1, grid=(s // tm,),
        in_specs=[a_spec(q) for q in range(nq)]
        + [pl.BlockSpec((d, sh), lambda i, order_ref: (0, 0), pipeline_mode=pl.Buffered(1))] * nq
        + [row, row, pl.BlockSpec((1, d), lambda i, order_ref: (0, 0))] + [_hbm()] * n_after,
        out_specs=[row, pl.BlockSpec((SUBLANES, d), lambda i, order_ref: (0, 0))])
    return _call(body, name="in_proj_bwd_norm", grid_spec=grid_spec,
                 out_shape=[jax.ShapeDtypeStruct((s, d), F32), jax.ShapeDtypeStruct((SUBLANES, d), F32)],
                 compiler_params=_params(("arbitrary",), VMEM_LIMIT_BIG))(
                     order, *([dz] * nq), *ws, xin, add, gain, *after)


def _seg_mean(sq, segb):
    parts = []
    for cgrp in range(sq.shape[1] // SEG):
        blk = sq[:, cgrp * SEG:(cgrp + 1) * SEG]
        hi = blk.astype(BF16)
        r1 = blk - hi.astype(F32)
        mid = r1.astype(BF16)
        lo = (r1 - mid.astype(F32)).astype(BF16)
        acc = jnp.dot(hi, segb, preferred_element_type=F32)
        acc += jnp.dot(mid, segb, preferred_element_type=F32)
        acc += jnp.dot(lo, segb, preferred_element_type=F32)
        parts.append(acc)
    out = parts[0] if len(parts) == 1 else jnp.concatenate(parts, axis=1)
    return out * (1.0 / HEAD_DIM)


def _rope(v, cos, sa, sb):
    parts = []
    for cgrp in range(v.shape[1] // LANES):
        blk = v[:, cgrp * LANES:(cgrp + 1) * LANES]
        parts.append(blk * cos + pltpu.roll(blk, LANES - 8, 1) * sa + pltpu.roll(blk, 8, 1) * sb)
    return parts[0] if len(parts) == 1 else jnp.concatenate(parts, axis=1)


def _rope_t(dv, cos, sa, sb):
    parts = []
    for cgrp in range(dv.shape[1] // LANES):
        blk = dv[:, cgrp * LANES:(cgrp + 1) * LANES]
        parts.append(blk * cos + pltpu.roll(blk * sa, 8, 1) + pltpu.roll(blk * sb, LANES - 8, 1))
    return parts[0] if len(parts) == 1 else jnp.concatenate(parts, axis=1)


def _tile_lanes(vec, width):
    reps = width // LANES
    return vec if reps == 1 else jnp.tile(vec, (1, reps))


def qk_prep_fwd(z, tabs, qg, kg, segb, aw, after=()):
    s = z.shape[0]
    tm = _tile(s, 512)
    wq = aw + 2 * KV_WIDTH

    def body(z_ref, cos_ref, sa_ref, sb_ref, qg_ref, kg_ref, seg_ref, qs_ref, ks_ref, vb_ref):
        zz = z_ref[...]
        q, k, v = zz[:, :aw], zz[:, aw:aw + KV_WIDTH], zz[:, aw + KV_WIDTH:]
        cos, sa, sb, segm = cos_ref[...], sa_ref[...], sb_ref[...], seg_ref[...]
        rq = lax.rsqrt(_seg_mean(q * q, segm) + EPS)
        qn = (q * rq) * _tile_lanes(qg_ref[...], aw)
        qs_ref[...] = (_rope(qn, cos, sa, sb) * (HEAD_DIM ** -0.5)).astype(BF16)
        rk = lax.rsqrt(_seg_mean(k * k, segm) + EPS)
        kn = (k * rk) * _tile_lanes(kg_ref[...], KV_WIDTH)
        ks_ref[...] = _rope(kn, cos, sa, sb).astype(BF16)
        vb_ref[...] = v.astype(BF16)

    tab = pl.BlockSpec((tm, LANES), lambda i: (i, 0))
    vec = pl.BlockSpec((1, LANES), lambda i: (0, 0))
    return _call_after(body, after, name="qk_prep_fwd", grid=(s // tm,),
                       in_specs=[pl.BlockSpec((tm, wq), lambda i: (i, 0)), tab, tab, tab, vec, vec,
                                 pl.BlockSpec((SEG, SEG), lambda i: (0, 0))],
                       out_specs=[pl.BlockSpec((tm, aw), lambda i: (i, 0)),
                                  pl.BlockSpec((tm, KV_WIDTH), lambda i: (i, 0)),
                                  pl.BlockSpec((tm, KV_WIDTH), lambda i: (i, 0))],
                       out_shape=[jax.ShapeDtypeStruct((s, aw), BF16), jax.ShapeDtypeStruct((s, KV_WIDTH), BF16),
                                  jax.ShapeDtypeStruct((s, KV_WIDTH), BF16)],
                       compiler_params=_params(("parallel",)))(z, *tabs, qg, kg, segb)


def qk_prep_bwd(z, dqs, dks, dvs, d_gate, tabs, qg, kg, segb, aw):
    s, in_w = z.shape
    tm = _tile(s, 512)
    wq = aw + 2 * KV_WIDTH

    def body(z_ref, dq_ref, dk_ref, dv_ref, dga_ref, cos_ref, sa_ref, sb_ref, qg_ref, kg_ref, seg_ref,
             dz_ref, acc_ref):
        i = pl.program_id(0)
        zz = z_ref[...]
        q, k = zz[:, :aw], zz[:, aw:aw + KV_WIDTH]
        cos, sa, sb, segm = cos_ref[...], sa_ref[...], sb_ref[...], seg_ref[...]

        def one(xv, dout, gvec, width):
            g = _tile_lanes(gvec, width)
            r = lax.rsqrt(_seg_mean(xv * xv, segm) + EPS)
            xhat = xv * r
            dn = _rope_t(dout, cos, sa, sb)
            gd = dn * g
            dx = r * (gd - xhat * _seg_mean(xhat * gd, segm))
            contrib = jnp.sum(dn * xhat, axis=0, keepdims=True)
            folded = contrib[:, :LANES]
            for cgrp in range(1, width // LANES):
                folded = folded + contrib[:, cgrp * LANES:(cgrp + 1) * LANES]
            return dx, folded + pltpu.roll(folded, HEAD_DIM, 1)

        dq, gq = one(q, dq_ref[...] * (HEAD_DIM ** -0.5), qg_ref[...], aw)
        dk, gk = one(k, dk_ref[...], kg_ref[...], KV_WIDTH)
        dz_ref[:, :aw] = dq.astype(BF16)
        dz_ref[:, aw:aw + KV_WIDTH] = dk.astype(BF16)
        dz_ref[:, aw + KV_WIDTH:wq] = dv_ref[...].astype(BF16)
        dz_ref[:, wq:] = dga_ref[...]

        @pl.when(i == 0)
        def _():
            acc_ref[...] = jnp.zeros_like(acc_ref)

        acc_ref[0:1, :] += gq
        acc_ref[1:2, :] += gk

    tab = pl.BlockSpec((tm, LANES), lambda i: (i, 0))
    vec = pl.BlockSpec((1, LANES), lambda i: (0, 0))
    kvb = pl.BlockSpec((tm, KV_WIDTH), lambda i: (i, 0))
    awb = pl.BlockSpec((tm, aw), lambda i: (i, 0))
    return _call(body, name="qk_prep_bwd", grid=(s // tm,),
                 in_specs=[pl.BlockSpec((tm, wq), lambda i: (i, 0)), awb, kvb, kvb, awb, tab, tab, tab, vec, vec,
                           pl.BlockSpec((SEG, SEG), lambda i: (0, 0))],
                 out_specs=[pl.BlockSpec((tm, wq + aw), lambda i: (i, 0)),
                            pl.BlockSpec((SUBLANES, LANES), lambda i: (0, 0))],
                 out_shape=[jax.ShapeDtypeStruct((s, in_w), BF16), jax.ShapeDtypeStruct((SUBLANES, LANES), F32)],
                 compiler_params=_params(("arbitrary",)))(z, dqs, dks, dvs, d_gate, *tabs, qg, kg, segb)


def _placed(band, lane_idx):
    out = {}
    for kh in range(N_KV_HEADS):
        grp = band[:, (kh // 2) * LANES:(kh // 2 + 1) * LANES]
        for half in (0, 1):
            t = grp if half == kh % 2 else pltpu.roll(grp, HEAD_DIM, 1)
            keep = (lane_idx >= half * HEAD_DIM) & (lane_idx < (half + 1) * HEAD_DIM)
            out[kh, half] = jnp.where(keep, t, jnp.zeros_like(t))
    return out


def _softmax_with_sink(sc, valid, sink):
    sc = jnp.where(valid, sc, NEG_INF)
    m = jnp.maximum(jnp.max(sc, axis=-1, keepdims=True), sink)
    e = jnp.exp(sc - m)
    es = jnp.exp(sink - m)
    den = jnp.sum(e, axis=-1, keepdims=True) + es
    return e / den, es / den


def _stack_halves(placed, kh):
    return jnp.concatenate([placed[kh, 0], placed[kh, 1]], axis=0)


def _valid_mask(n):
    r_i = lax.broadcasted_iota(jnp.int32, (BLOCK, 2 * BLOCK), 0)
    j_i = lax.broadcasted_iota(jnp.int32, (BLOCK, 2 * BLOCK), 1)
    return (j_i > r_i) & (j_i <= r_i + BLOCK) & ((n > 0) | (j_i >= BLOCK))


def attn_fwd(qs, ks, vb, z, sinks, aw, d, ga_off):
    s = qs.shape[0]
    nb = s // BLOCK
    nq = aw // HEAD_DIM
    grp_sz = nq // N_KV_HEADS
    n_ga = aw // COLT

    def body(q_ref, ko_ref, kp_ref, vo_ref, vp_ref, *rest):
        ga_refs = rest[:n_ga]
        sink_ref, attn_ref, mix_ref = rest[n_ga:]
        n = pl.program_id(0)
        lane_idx = lax.broadcasted_iota(jnp.int32, (2 * BLOCK, LANES), 1)
        kpl = _placed(jnp.concatenate([kp_ref[...], ko_ref[...]], axis=0), lane_idx)
        vpl = _placed(jnp.concatenate([vp_ref[...], vo_ref[...]], axis=0), lane_idx)
        valid = _valid_mask(n)
        groups = range(nq // 2)
        kcat = [_stack_halves(kpl, kh) for kh in range(N_KV_HEADS)]
        vcat = [_stack_halves(vpl, kh) for kh in range(N_KV_HEADS)]
        scs = [lax.dot_general(q_ref[:, c * LANES:(c + 1) * LANES], kcat[2 * c // grp_sz], NT,
                               preferred_element_type=F32) for c in groups]
        probs = [jnp.concatenate(
            [_softmax_with_sink(scs[c][:, half * 2 * BLOCK:(half + 1) * 2 * BLOCK], valid,
                                sink_ref[0, 2 * c + half])[0].astype(BF16) for half in (0, 1)], axis=1)
                 for c in groups]
        for c in groups:
            acc = jnp.dot(probs[c], vcat[2 * c // grp_sz], preferred_element_type=F32)
            attn_ref[:, c * LANES:(c + 1) * LANES] = acc
            col = c * LANES
            ga = ga_refs[col // COLT][:, col % COLT:col % COLT + LANES]
            mix_ref[:, c * LANES:(c + 1) * LANES] = (acc * (ga * _sigmoid(ga))).astype(BF16)

    own = lambda n: (n, 0)
    prev = lambda n: (jnp.maximum(n - 1, 0), 0)
    kvs = lambda imap: pl.BlockSpec((BLOCK, KV_WIDTH), imap)
    ga_specs = [pl.BlockSpec((BLOCK, COLT), functools.partial(lambda n, j: (n, ga_off + j), j=j))
                for j in range(n_ga)]
    return _call(body, name="attn_fwd", grid=(nb,),
                 in_specs=[pl.BlockSpec((BLOCK, aw), own), kvs(own), kvs(prev), kvs(own), kvs(prev)]
                 + ga_specs + [pl.BlockSpec(memory_space=pltpu.SMEM)],
                 out_specs=[pl.BlockSpec((BLOCK, aw), own), pl.BlockSpec((BLOCK, aw), own)],
                 out_shape=[jax.ShapeDtypeStruct((s, aw), F32), jax.ShapeDtypeStruct((s, d), BF16)],
                 compiler_params=_params(("parallel",)))(qs, ks, ks, vb, vb, *([z] * n_ga), sinks)


def out_proj_bwd_gate(d_x1b, wo, attn, z, aw, ga_off, after=()):
    s, d = d_x1b.shape
    tm = _tile(s, 512)
    n_ga = aw // COLT

    def body(dx_ref, w_ref, at_ref, *rest):
        ga_refs, (do_ref, dga_ref, dmc_ref) = rest[:n_ga], rest[n_ga:]
        dm = lax.dot_general(dx_ref[...], w_ref[...], NT, preferred_element_type=F32)
        dmc_ref[...] = dm[:, aw:]
        for j in range(n_ga):
            cols = slice(j * COLT, (j + 1) * COLT)
            ga = ga_refs[j][...]
            sig = _sigmoid(ga)
            dma = dm[:, cols]
            do_ref[:, cols] = (dma * (ga * sig)).astype(BF16)
            dga_ref[:, cols] = ((dma * at_ref[:, cols]) * (sig * (1.0 + ga * (1.0 - sig)))).astype(BF16)

    row = lambda width: pl.BlockSpec((tm, width), lambda i: (i, 0))
    ga_specs = [pl.BlockSpec((tm, COLT), functools.partial(lambda i, j: (i, ga_off + j), j=j)) for j in range(n_ga)]
    return _call_after(body, after, name="out_proj_bwd_gate", grid=(s // tm,),
                       in_specs=[row(d), pl.BlockSpec((d, d), lambda i: (0, 0), pipeline_mode=pl.Buffered(1)),
                                 row(aw)] + ga_specs,
                       out_specs=[row(aw), row(aw), row(d - aw)],
                       out_shape=[jax.ShapeDtypeStruct((s, aw), BF16), jax.ShapeDtypeStruct((s, aw), BF16),
                                  jax.ShapeDtypeStruct((s, d - aw), F32)],
                       compiler_params=_params(("parallel",)))(d_x1b, wo, attn, *([z] * n_ga))


def attn_bwd(qs, ks, vb, d_o, sinks, aw):
    s = qs.shape[0]
    nb = s // BLOCK
    nq = aw // HEAD_DIM
    grp_sz = nq // N_KV_HEADS

    def body(q_ref, ko_ref, kp_ref, vo_ref, vp_ref, do_ref, sink_ref, dq_ref, dk_ref, dv_ref, ds_ref,
             ck_ref, cv_ref):
        n = pl.program_id(0)

        @pl.when(n == 0)
        def _():
            ds_ref[...] = jnp.zeros_like(ds_ref)
            dk_ref[...] = jnp.zeros_like(dk_ref)
            dv_ref[...] = jnp.zeros_like(dv_ref)

        @pl.when(n < nb)
        def _():
            lane_idx = lax.broadcasted_iota(jnp.int32, (2 * BLOCK, LANES), 1)
            lane_row = lax.broadcasted_iota(jnp.int32, (1, LANES), 1)
            kpl = _placed(jnp.concatenate([kp_ref[...], ko_ref[...]], axis=0), lane_idx)
            vpl = _placed(jnp.concatenate([vp_ref[...], vo_ref[...]], axis=0), lane_idx)
            valid = _valid_mask(n)
            ds_row = jnp.zeros((1, LANES), F32)
            wide = 2 * BLOCK
            groups = range(nq // 2)
            per_kv = grp_sz // 2
            kcat = [_stack_halves(kpl, kh) for kh in range(N_KV_HEADS)]
            vcat = [_stack_halves(vpl, kh) for kh in range(N_KV_HEADS)]
            qg = [q_ref[:, c * LANES:(c + 1) * LANES] for c in groups]
            dog = [do_ref[:, c * LANES:(c + 1) * LANES] for c in groups]
            scs = [lax.dot_general(qg[c], kcat[c // per_kv], NT, preferred_element_type=F32) for c in groups]
            dps = [lax.dot_general(dog[c], vcat[c // per_kv], NT, preferred_element_type=F32) for c in groups]
            ds_pairs, p_pairs = [], []
            for c in groups:
                probs, dss = [], []
                for half in (0, 1):
                    h = 2 * c + half
                    cols = slice(half * wide, (half + 1) * wide)
                    p, p_sink = _softmax_with_sink(scs[c][:, cols], valid, sink_ref[0, h])
                    delta = jnp.sum(p * dps[c][:, cols], axis=-1, keepdims=True)
                    dss.append((p * (dps[c][:, cols] - delta)).astype(BF16))
                    probs.append(p.astype(BF16))
                    dsink = -jnp.sum(p_sink * delta, axis=0, keepdims=True)
                    ds_row += jnp.where(lane_row == h, dsink, 0.0)
                ds_pairs.append(jnp.concatenate(dss, axis=1))
                p_pairs.append(jnp.concatenate(probs, axis=1))
            for c in groups:
                dq_ref[:, c * LANES:(c + 1) * LANES] = jnp.dot(ds_pairs[c], kcat[c // per_kv],
                                                               preferred_element_type=F32)
            dkts = [lax.dot_general(qg[c], ds_pairs[c], TN, preferred_element_type=F32) for c in groups]
            dvts = [lax.dot_general(dog[c], p_pairs[c], TN, preferred_element_type=F32) for c in groups]
            dkt, dvt = [], []
            for kh in range(N_KV_HEADS):
                for parts, out in ((dkts, dkt), (dvts, dvt)):
                    tot = parts[kh * per_kv]
                    for c in range(kh * per_kv + 1, (kh + 1) * per_kv):
                        tot = tot + parts[c]
                    out.append(tot[:HEAD_DIM, :wide] + tot[HEAD_DIM:, wide:])
            ds_ref[0:1, :] += ds_row
            back = lambda t: jnp.concatenate(
                [jnp.concatenate(t[2 * g:2 * g + 2], axis=0).T for g in range(N_KV_HEADS // 2)], axis=1)
            dk_band, dv_band = back(dkt), back(dvt)

            @pl.when(n > 0)
            def _():
                dk_ref[...] = ck_ref[...] + dk_band[:BLOCK]
                dv_ref[...] = cv_ref[...] + dv_band[:BLOCK]

            ck_ref[...] = dk_band[BLOCK:]
            cv_ref[...] = dv_band[BLOCK:]

        @pl.when(n == nb)
        def _():
            dk_ref[...] = ck_ref[...]
            dv_ref[...] = cv_ref[...]

    own = lambda n: (jnp.minimum(n, nb - 1), 0)
    prev = lambda n: (jnp.clip(n - 1, 0, nb - 1), 0)
    done = lambda n: (jnp.maximum(n - 1, 0), 0)
    kvs = lambda imap: pl.BlockSpec((BLOCK, KV_WIDTH), imap)
    return _call(body, name="attn_bwd", grid=(nb + 1,),
                 in_specs=[pl.BlockSpec((BLOCK, aw), own), kvs(own), kvs(prev), kvs(own), kvs(prev),
                           pl.BlockSpec((BLOCK, aw), own), pl.BlockSpec(memory_space=pltpu.SMEM)],
                 out_specs=[pl.BlockSpec((BLOCK, aw), own), kvs(done), kvs(done),
                            pl.BlockSpec((SUBLANES, LANES), lambda n: (0, 0))],
                 out_shape=[jax.ShapeDtypeStruct((s, aw), F32), jax.ShapeDtypeStruct((s, KV_WIDTH), F32),
                            jax.ShapeDtypeStruct((s, KV_WIDTH), F32), jax.ShapeDtypeStruct((SUBLANES, LANES), F32)],
                 scratch_shapes=[pltpu.VMEM((BLOCK, KV_WIDTH), F32), pltpu.VMEM((BLOCK, KV_WIDTH), F32)],
                 compiler_params=_params(("arbitrary",)))(qs, ks, ks, vb, vb, d_o, sinks)


def conv_fwd(z, conv_w, mix, aw, cw, b_off):
    s = z.shape[0]
    tm = _tile(s, 1024)
    nseg = cw // COLT
    hb = tm // SUBLANES

    def body(b_ref, c_ref, h_ref, g_ref, cp_ref, hp_ref, w_ref, mix_in, mix_ref):
        i = pl.program_id(0)
        u = c_ref[...] * h_ref[...]
        up = jnp.where(i > 0, cp_ref[...] * hp_ref[...], 0.0)
        ext = jnp.concatenate([up, u], axis=0)
        um1 = pltpu.roll(ext, 1, 0)[SUBLANES:]
        um2 = pltpu.roll(ext, 2, 0)[SUBLANES:]
        w = w_ref[...]
        cv = w[0:1] * um2 + w[1:2] * um1 + w[2:3] * u
        g = g_ref[...]
        mix_ref[...] = ((b_ref[...] * cv) * (g * _sigmoid(g))).astype(BF16)

    seg = lambda k: pl.BlockSpec((tm, COLT), functools.partial(lambda i, j, k: (i, b_off + k * nseg + j), k=k))
    halo = lambda k: pl.BlockSpec(
        (SUBLANES, COLT), functools.partial(lambda i, j, k: (jnp.maximum(i * hb - 1, 0), b_off + k * nseg + j), k=k))
    return _call(body, name="conv_fwd", grid=(s // tm, nseg),
                 in_specs=[seg(0), seg(1), seg(2), seg(3), halo(1), halo(2),
                           pl.BlockSpec((SUBLANES, COLT), lambda i, j: (0, j)), _hbm()],
                 out_specs=pl.BlockSpec((tm, COLT), lambda i, j: (i, aw // COLT + j)),
                 out_shape=jax.ShapeDtypeStruct(mix.shape, BF16),
                 input_output_aliases={7: 0},
                 compiler_params=_params(("parallel", "parallel")))(z, z, z, z, z, z, conv_w, mix)


def conv_bwd(z, d_mix, conv_w, dz, aw, cw, b_off):
    s = z.shape[0]
    tm = _tile(s, 512)
    nseg = cw // COLT
    hb = tm // SUBLANES
    n_row = s // tm
    last_h = s // SUBLANES - 1
    n_step = nseg * n_row

    def body(b_ref, c_ref, h_ref, g_ref, cp_ref, hp_ref, bn_ref, gn_ref, dm_ref, dmn_ref, w_ref, dz_in,
             dz_ref, acc_ref, stash_ref, sems):
        j = pl.program_id(0)
        i = pl.program_id(1)
        step = j * n_row + i
        slot = step % 2

        def copies(from_slot, at_step):
            jj, ii = at_step // n_row, at_step % n_row
            rows = pl.ds(pl.multiple_of(ii * tm, tm), tm)
            return [pltpu.make_async_copy(
                stash_ref.at[from_slot, q],
                dz_ref.at[rows, pl.ds(pl.multiple_of((b_off + q * nseg + jj) * COLT, COLT), COLT)],
                sems.at[from_slot, q]) for q in range(4)]

        @pl.when(step >= 2)
        def _():
            for cp in copies(slot, step - 2):
                cp.wait()

        cc, hh, bb, g = c_ref[...], h_ref[...], b_ref[...], g_ref[...]
        w = w_ref[...]
        u = cc * hh
        up = jnp.where(i > 0, cp_ref[...] * hp_ref[...], 0.0)
        ext = jnp.concatenate([up, u], axis=0)
        um1 = pltpu.roll(ext, 1, 0)[SUBLANES:]
        um2 = pltpu.roll(ext, 2, 0)[SUBLANES:]
        cv = w[0:1] * um2 + w[1:2] * um1 + w[2:3] * u
        sig = _sigmoid(g)
        sg = g * sig
        dm = dm_ref[...]
        d_cv = (dm * bb) * sg
        gn = gn_ref[...]
        d_cv_next = jnp.where(i < n_row - 1, (dmn_ref[...] * bn_ref[...]) * (gn * _sigmoid(gn)), 0.0)
        ext2 = jnp.concatenate([d_cv, d_cv_next], axis=0)
        dp1 = pltpu.roll(ext2, tm + SUBLANES - 1, 0)[:tm]
        dp2 = pltpu.roll(ext2, tm + SUBLANES - 2, 0)[:tm]
        d_u = w[2:3] * d_cv + w[1:2] * dp1 + w[0:1] * dp2
        stash_ref[slot, 0] = ((dm * cv) * sg).astype(BF16)
        stash_ref[slot, 1] = (d_u * hh).astype(BF16)
        stash_ref[slot, 2] = (d_u * cc).astype(BF16)
        stash_ref[slot, 3] = (((dm * bb) * cv) * (sig * (1.0 + g * (1.0 - sig)))).astype(BF16)
        for cp in copies(slot, step):
            cp.start()

        @pl.when(i == 0)
        def _():
            acc_ref[...] = jnp.zeros_like(acc_ref)

        acc_ref[0:1, :] += jnp.sum(d_cv * um2, axis=0, keepdims=True)
        acc_ref[1:2, :] += jnp.sum(d_cv * um1, axis=0, keepdims=True)
        acc_ref[2:3, :] += jnp.sum(d_cv * u, axis=0, keepdims=True)

        @pl.when(step == n_step - 1)
        def _():
            if n_step >= 2:
                for cp in copies(1 - slot, step - 1):
                    cp.wait()
            for cp in copies(slot, step):
                cp.wait()

    seg = lambda q: pl.BlockSpec((tm, COLT), functools.partial(lambda j, i, q: (i, b_off + q * nseg + j), q=q))
    halo_p = lambda q: pl.BlockSpec(
        (SUBLANES, COLT),
        functools.partial(lambda j, i, q: (jnp.maximum(i * hb - 1, 0), b_off + q * nseg + j), q=q))
    halo_n = lambda q: pl.BlockSpec(
        (SUBLANES, COLT),
        functools.partial(lambda j, i, q: (jnp.minimum((i + 1) * hb, last_h), b_off + q * nseg + j), q=q))
    return _call(body, name="conv_bwd", grid=(nseg, n_row),
                 in_specs=[seg(0), seg(1), seg(2), seg(3), halo_p(1), halo_p(2), halo_n(0), halo_n(3),
                           pl.BlockSpec((tm, COLT), lambda j, i: (i, j)),
                           pl.BlockSpec((SUBLANES, COLT), lambda j, i: (jnp.minimum((i + 1) * hb, last_h), j)),
                           pl.BlockSpec((SUBLANES, COLT), lambda j, i: (0, j)), _hbm()],
                 out_specs=[_hbm(), pl.BlockSpec((SUBLANES, COLT), lambda j, i: (0, j))],
                 out_shape=[jax.ShapeDtypeStruct(dz.shape, BF16), jax.ShapeDtypeStruct((SUBLANES, cw), F32)],
                 input_output_aliases={11: 0},
                 scratch_shapes=[pltpu.VMEM((2, 4, tm, COLT), BF16), pltpu.SemaphoreType.DMA((2, 4))],
                 compiler_params=_params(("arbitrary", "arbitrary")))(
                     z, z, z, z, z, z, z, z, d_mix, d_mix, conv_w, dz)


def _place():
    x, y, c = lax.axis_index("x"), lax.axis_index("y"), lax.axis_index("c")
    chips = [(1 - x, y), (x, 1 - y), (1 - x, 1 - y)]
    return x, y, c, chips


def _remote(src, dst, send_sem, recv_sem, device):
    return pltpu.make_async_remote_copy(src_ref=src, dst_ref=dst, send_sem=send_sem, recv_sem=recv_sem,
                                        device_id=device, device_id_type=MESH)


def plan_gather_ici(n_split):
    def plan(refs, send, recv):
        x, y, c, chips = _place()
        me = 2 * x + y
        mine, theirs = [], []
        for w, ref in enumerate(refs):
            for j, (cx, cy) in enumerate(chips):
                def part(slot):
                    if w >= n_split:
                        return ref.at[slot]
                    hr = ref.shape[1] // 2
                    return ref.at[slot, pl.ds(c * hr, hr)]
                k = 3 * w + j
                mine.append(_remote(part(me), part(me), send.at[k], recv.at[k], (cx, cy, c)))
                theirs.append(_remote(part(2 * cx + cy), part(2 * cx + cy), send.at[k], recv.at[k], (cx, cy, c)))
        return mine, theirs
    return plan


def plan_shard_ici(j):
    def plan(refs, send, recv):
        _, _, c, chips = _place()
        own, land = refs
        hr = own.shape[0] // 2
        rows = pl.ds(c * hr, hr)
        cp = _remote(own.at[rows], land.at[rows], send.at[0], recv.at[0], (*chips[j], c))
        return [cp], [cp]
    return plan


def plan_shard_relay(refs, send, recv):
    _, _, c, chips = _place()
    land_x, land_y, land_far = refs
    hr = land_far.shape[0] // 2
    quarter = lambda q: pl.ds(c * hr + q * (hr // 2), hr // 2)
    mine = [_remote(land_y.at[quarter(0)], land_far.at[quarter(0)], send.at[0], recv.at[0], (*chips[0], c)),
            _remote(land_x.at[quarter(1)], land_far.at[quarter(1)], send.at[1], recv.at[1], (*chips[1], c))]
    return mine, mine


def plan_shard_pass(refs, send, recv):
    x, y, c, _ = _place()
    mine, theirs = [], []
    for w, land in enumerate(refs):
        hr = land.shape[0] // 2
        half = lambda core: land.at[pl.ds(core * hr, hr)]
        mine.append(_remote(half(c), half(c), send.at[w], recv.at[w], (x, y, 1 - c)))
        theirs.append(_remote(half(1 - c), half(1 - c), send.at[w], recv.at[w], (x, y, 1 - c)))
    return mine, theirs


def plan_gather_pass(refs, send, recv):
    x, y, c, chips = _place()
    mine, theirs = [], []
    for w, ref in enumerate(refs):
        hr = ref.shape[1] // 2
        for j, (cx, cy) in enumerate(chips):
            half = lambda core: ref.at[2 * cx + cy, pl.ds(core * hr, hr)]
            k = 3 * w + j
            mine.append(_remote(half(c), half(c), send.at[k], recv.at[k], (x, y, 1 - c)))
            theirs.append(_remote(half(1 - c), half(1 - c), send.at[k], recv.at[k], (x, y, 1 - c)))
    return mine, theirs


def plan_swap(refs, send, recv):
    x, y, c, _ = _place()
    nw = len(refs) // 2
    mine = []
    for w in range(nw):
        hr = refs[w].shape[1] // 2
        mine.append(_remote(refs[w].at[:, pl.ds((1 - c) * hr, hr), :], refs[nw + w], send.at[w], recv.at[w],
                            (x, y, 1 - c)))
    return mine, mine


def plan_scatter(refs, send, recv):
    x, y, c, chips = _place()
    me = 2 * x + y
    nw = len(refs) // 2
    mine, theirs = [], []
    for w in range(nw):
        for j, (cx, cy) in enumerate(chips):
            k = 3 * w + j
            mine.append(_remote(refs[w].at[2 * cx + cy], refs[nw + w].at[me], send.at[k], recv.at[k], (cx, cy, c)))
            theirs.append(_remote(refs[w].at[me], refs[nw + w].at[2 * cx + cy], send.at[k], recv.at[k], (cx, cy, c)))
    return mine, theirs


def plan_join(refs, send, recv):
    x, y, c, _ = _place()
    mine, theirs = [], []
    for w, ref in enumerate(refs):
        hr = ref.shape[0] // 2
        half = lambda core: ref.at[pl.ds(core * hr, hr)]
        mine.append(_remote(half(c), half(c), send.at[w], recv.at[w], (x, y, 1 - c)))
        theirs.append(_remote(half(1 - c), half(1 - c), send.at[w], recv.at[w], (x, y, 1 - c)))
    return mine, theirs


def exchange(name, plan, arrays, n, after=()):
    na = len(arrays)

    def body(*refs):
        mine, theirs = plan(refs[:na], refs[2 * na], refs[2 * na + 1])
        for cp in mine:
            cp.start()
        for cp in theirs:
            cp.wait_recv()
        for cp in mine:
            cp.wait_send()

    return _call_after(body, after, name=name, in_specs=[_hbm()] * na, out_specs=[_hbm()] * na,
                       out_shape=[jax.ShapeDtypeStruct(a.shape, a.dtype) for a in arrays],
                       input_output_aliases={i: i for i in range(na)},
                       scratch_shapes=[pltpu.SemaphoreType.DMA((n,)), pltpu.SemaphoreType.DMA((n,))])(*arrays)


def exchange_start(name, groups, arrays, after=()):
    na, ng = len(arrays), len(groups)

    def body(*refs):
        for g, (plan, idx, _) in enumerate(groups):
            mine, _ = plan([refs[i] for i in idx], refs[na + 2 * g], refs[na + 2 * g + 1])
            for cp in mine:
                cp.start()
        refs[-1][...] = jnp.zeros_like(refs[-1])

    hbm = pl.BlockSpec(memory_space=pltpu.HBM)
    sem = pl.BlockSpec(memory_space=pltpu.SEMAPHORE)
    sem_types = [pltpu.SemaphoreType.DMA((n,)) for _, _, n in groups for _ in (0, 1)]
    outs = _call_after(body, after, name=name, in_specs=[hbm] * na,
                       out_specs=[sem] * (2 * ng) + [hbm] * na + [pl.BlockSpec(memory_space=pltpu.VMEM)],
                       out_shape=sem_types + [pltpu.HBM(a.shape, a.dtype) for a in arrays]
                       + [jax.ShapeDtypeStruct((SUBLANES, LANES), F32)],
                       input_output_aliases={i: i + 2 * ng for i in range(na)},
                       compiler_params=pltpu.CompilerParams(
                           has_side_effects=pltpu.SideEffectType.DATAFLOW_SIDE_EFFECTING))(
                               *[pltpu.with_memory_space_constraint(a, pltpu.HBM) for a in arrays])
    sems = [(outs[2 * g], outs[2 * g + 1]) for g in range(ng)]
    return sems, list(outs[2 * ng:-1]), outs[-1]


def exchange_wait(name, plan, sems, arrays, after):
    send_sems, recv_sems = sems
    na = len(arrays)
    after = tuple(after) if isinstance(after, (tuple, list)) else (after,)

    def body(*refs):
        mine, theirs = plan(refs[:na], refs[na], refs[na + 1])
        for cp in mine:
            cp.wait_send()
        for cp in theirs:
            cp.wait_recv()

    hbm = pl.BlockSpec(memory_space=pltpu.HBM)
    sem = pl.BlockSpec(memory_space=pltpu.SEMAPHORE)
    return _call(body, name=name, in_specs=[hbm] * na + [sem, sem] + [_hbm()] * len(after),
                 out_specs=[hbm] * na, out_shape=[pltpu.HBM(a.shape, a.dtype) for a in arrays],
                 input_output_aliases={i: i for i in range(na)},
                 compiler_params=pltpu.CompilerParams(
                     has_side_effects=pltpu.SideEffectType.DATAFLOW_SIDE_EFFECTING))(
                         *arrays, send_sems, recv_sems, *after)


def plan_allgather_small(refs, send, recv):
    x, y, c, _ = _place()
    buf, = refs
    mine, theirs = [], []
    flips = [(fx, fy, fc) for fx in (0, 1) for fy in (0, 1) for fc in (0, 1)][1:]
    for k, (fx, fy, fc) in enumerate(flips):
        peer = (x ^ fx, y ^ fy, c ^ fc)
        slot = lambda px, py, pc: buf.at[4 * px + 2 * py + pc]
        mine.append(_remote(slot(x, y, c), slot(x, y, c), send.at[k], recv.at[k], peer))
        theirs.append(_remote(slot(*peer), slot(*peer), send.at[k], recv.at[k], peer))
    return mine, theirs


def add_sibling(grad, got, core, name):
    _, r, c = grad.shape
    hr = r // 2
    tr = _tile(hr, 512)
    nblk = hr // tr

    def body(core_ref, g_ref, o_ref, out_ref):
        out_ref[...] = (g_ref[...].astype(F32) + o_ref[...].astype(F32)).astype(BF16)

    grid_spec = pltpu.PrefetchScalarGridSpec(
        num_scalar_prefetch=1, grid=(N_CHIPS, nblk),
        in_specs=[pl.BlockSpec((None, tr, c), lambda t, i, core_ref: (t, core_ref[0] * nblk + i, 0)),
                  pl.BlockSpec((None, tr, c), lambda t, i, core_ref: (t, i, 0))],
        out_specs=pl.BlockSpec((None, tr, c), lambda t, i, core_ref: (t, i, 0)))
    return _call(body, name=name, grid_spec=grid_spec,
                 out_shape=pltpu.HBM((N_CHIPS, hr, c), BF16),
                 compiler_params=_params(("parallel", "parallel")))(core, grad, got)


def sum_chips(mine, owned, place, name):
    _, hr, c = mine.shape
    tr = _tile(hr, 512)
    nblk = hr // tr

    def body(place_ref, m_ref, o1_ref, o2_ref, o3_ref, out_ref):
        acc = m_ref[...].astype(F32)
        for o_ref in (o1_ref, o2_ref, o3_ref):
            acc = acc + o_ref[...].astype(F32)
        out_ref[...] = acc

    other = lambda k: pl.BlockSpec((None, tr, c), lambda i, place_ref: ((place_ref[0] + k) % N_CHIPS, i, 0))
    grid_spec = pltpu.PrefetchScalarGridSpec(
        num_scalar_prefetch=1, grid=(nblk,),
        in_specs=[other(0), other(1), other(2), other(3)],
        out_specs=pl.BlockSpec((tr, c), lambda i, place_ref: (place_ref[1] * nblk + i, 0)))
    return _call(body, name=name, grid_spec=grid_spec,
                 out_shape=pltpu.HBM((2 * hr, c), F32),
                 compiler_params=_params(("parallel",)))(place, mine, owned, owned, owned)


def sum_devices(gathered):
    _, rows, width = gathered.shape

    def body(g_ref, out_ref):
        acc = g_ref[0]
        for dev in range(1, 8):
            acc = acc + g_ref[dev]
        out_ref[...] = acc
        tail = acc[SUBLANES:]
        out_ref[SUBLANES:, :] = jnp.broadcast_to(jnp.sum(tail, axis=1, keepdims=True), tail.shape)

    return _call(body, name="sum_devices",
                 in_specs=[pl.BlockSpec(memory_space=pltpu.VMEM)],
                 out_specs=pl.BlockSpec(memory_space=pltpu.VMEM),
                 out_shape=jax.ShapeDtypeStruct((rows, width), F32))(gathered)


def _rope_tables(s):
    half = ROT_DIM // 2
    inv_freq = jnp.power(jnp.float32(ROPE_THETA), -jnp.arange(half, dtype=F32) * 2.0 / ROT_DIM)
    freq64 = jnp.concatenate([inv_freq, inv_freq, jnp.zeros((HEAD_DIM - ROT_DIM,), F32)])
    freq = jnp.concatenate([freq64, freq64])[None, :]
    dim = (jnp.arange(LANES) % HEAD_DIM)[None, :]
    ang = jnp.arange(s).astype(F32)[:, None] * freq
    cos, sin = jnp.cos(ang), jnp.sin(ang)
    return cos, jnp.where(dim < half, -sin, 0.0), jnp.where((dim >= half) & (dim < ROT_DIM), sin, 0.0)


def _pad_rows(a, rows):
    return jnp.pad(a, ((0, rows - a.shape[0]), (0, 0)))


def _pad_cols(a, cols):
    return jnp.pad(a, ((0, 0), (0, cols - a.shape[1])))


def kernel(x, p, norm_gain, w_in, q_norm_gain, k_norm_gain, attn_sinks, conv_w, w_out, ple_gate_norm_gain, w_ple_gate, b_ple_gate, w_ple_proj, ple_norm_gain, loss_target, m_norm_gain, m_w_in, m_q_norm_gain, m_k_norm_gain, m_attn_sinks, m_conv_w, m_w_out, m_ple_gate_norm_gain, m_w_ple_gate, m_b_ple_gate, m_w_ple_proj, m_ple_norm_gain, v_norm_gain, v_w_in, v_q_norm_gain, v_k_norm_gain, v_attn_sinks, v_conv_w, v_w_out, v_ple_gate_norm_gain, v_w_ple_gate, v_b_ple_gate, v_w_ple_proj, v_ple_norm_gain):
    x2, p2, tgt = x[0], p[0, 0], loss_target[0]
    s, d = x2.shape
    ple = p2.shape[1]
    aw = d // 2
    cw = d - aw
    nq = aw // HEAD_DIM
    sh = w_in.shape[2]
    in_w = N_CHIPS * sh
    dq = d // N_CHIPS
    cq = cw // N_CHIPS
    ga_off = (aw + 2 * KV_WIDTH) // COLT
    b_off = (2 * aw + 2 * KV_WIDTH) // COLT
    assert in_w == 2 * aw + 2 * KV_WIDTH + 4 * cw and aw % COLT == 0 and cw % COLT == 0
    assert s % BLOCK == 0 and sh % LANES == 0 and nq % (2 * N_KV_HEADS) == 0

    core = lax.axis_index("c").astype(jnp.int32).reshape(1)
    chip = 2 * lax.axis_index("x") + lax.axis_index("y")

    chip1 = chip.astype(jnp.int32).reshape(1)
    place = jnp.concatenate([chip1, core])
    w_in_own = cast_bf16(w_in[0], "cast_w_in")
    rest = [cast_into_slot(w_out[0], chip1, "cast_w_out"), cast_into_slot(w_ple_gate[0], chip1, "cast_w_pg"),
            cast_into_slot(w_ple_proj[0], chip1, "cast_w_pp"),
            lax.dynamic_update_slice(jnp.zeros((N_CHIPS, SUBLANES, cq), F32),
                                     _pad_rows(conv_w[0], SUBLANES)[None], (chip, 0, 0))]
    order = jnp.stack([chip, chip ^ 2, chip ^ 1, chip ^ 3]).astype(jnp.int32)
    wi_sems, wi_arrs, wi_token = exchange_start(
        "gather_wi_start", [(plan_shard_ici(j), [0, 1 + j], 1) for j in range(2)],
        [w_in_own] + [lax.empty((d, sh), BF16) for _ in range(3)])

    tn = _tile(d, 1024)
    ts = _tile(s, 2048)
    tabs = _rope_tables(s)
    qg = jnp.tile(q_norm_gain, (1, LANES // HEAD_DIM))
    kg = jnp.tile(k_norm_gain, (1, LANES // HEAD_DIM))
    seg_i = jnp.arange(SEG) // HEAD_DIM
    segb = (seg_i[:, None] == seg_i[None, :]).astype(BF16)
    h, z = norm_in_proj(x2, norm_gain, wi_arrs[0], order, in_w)
    own, land_x = exchange_wait("gather_wi_wait0", plan_shard_ici(0), wi_sems[0], [wi_arrs[0], wi_arrs[1]],
                                (z,) + tuple(tabs) + tuple(rest[:2]))
    own, land_y = exchange_wait("gather_wi_wait1", plan_shard_ici(1), wi_sems[1], [own, wi_arrs[2]], land_x)
    relay_sems, relayed, relay_token = exchange_start(
        "gather_wi_relay_start", [(plan_shard_relay, [0, 1, 2], 2)], [land_x, land_y, wi_arrs[3]])
    rest_sems, rest, rest_token = exchange_start(
        "gather_rest_start", [(plan_gather_ici(3), [0, 1, 2, 3], 12)], rest, after=(relay_token,))
    land_x, land_y = exchange("gather_wi_pass01", plan_shard_pass, relayed[:2], 2, after=(rest_token,))
    z = in_proj_part(h, land_x, z, order, 1, in_w)
    z = in_proj_part(h, land_y, z, order, 2, in_w)
    land_x, land_y, land_far = exchange_wait("gather_wi_relay_wait", plan_shard_relay, relay_sems[0],
                                             [land_x, land_y, relayed[2]], z)
    land_far, = exchange("gather_wi_pass2", plan_shard_pass, [land_far], 1)
    z = in_proj_part(h, land_far, z, order, 3, in_w)
    w_shards = [own, land_x, land_y, land_far]
    wo_all, wg_all, wp_all, conv_all = exchange_wait("gather_rest_wait", plan_gather_ici(3), rest_sems[0], rest, z)
    pass_sems, passed, pass_token = exchange_start(
        "gather_rest_pass_start", [(plan_gather_pass, [0, 1, 2], 9)], [wo_all, wg_all, wp_all])
    conv_full = conv_all.transpose(1, 0, 2).reshape(SUBLANES, cw)
    qs, ks, vb = qk_prep_fwd(z, tabs, qg, kg, segb, aw, after=(pass_token,))
    attn, mix = attn_fwd(qs, ks, vb, z, attn_sinks, aw, d, ga_off)
    mix = conv_fwd(z, conv_full, mix, aw, cw, b_off)
    wo_all, wg_all, wp_all = exchange_wait("gather_rest_pass_wait", plan_gather_pass, pass_sems[0], passed, mix)
    wo_full = wo_all.reshape(d, d)
    wg_full = wg_all.reshape(d, d)
    x1, hg = out_proj_norm(mix, wo_full, x2, ple_gate_norm_gain)
    pq = d // N_CHIPS

    d_gl, d_pe, dy, acc_head = head_fwd_bwd(x1, hg, wg_full, p2, wp_all, tgt, b_ple_gate, ple_norm_gain)
    wgrad = lambda a_cols, shape3, o_spec, b_cols, name, a, b, grid: matmul(
        a, b, grid=grid,
        a_spec=pl.BlockSpec((ts, a_cols), lambda i, j, k: (k, i)),
        b_spec=pl.BlockSpec((ts, b_cols), lambda i, j, k: (k, j)),
        o_spec=o_spec, out_shape=jax.ShapeDtypeStruct(shape3, BF16), dims=TN, name=name)
    g_wg = wgrad(tn, (d, d), pl.BlockSpec((tn, tn), lambda i, j, k: (i, j)), tn, "mm_g_wg", hg, d_gl,
                 (d // tn, d // tn, s // ts))
    g_wp = wgrad(ple, (N_CHIPS, ple, pq), pl.BlockSpec((None, ple, pq), lambda i, j, k: (j, 0, 0)), pq,
                 "mm_g_wp", p2, d_pe, (1, N_CHIPS, s // ts))
    d_x1, d_x1b, acc_g = gate_proj_bwd_norm(d_gl, wg_full, x1, dy, ple_gate_norm_gain)
    g_wo = wgrad(tn, (d, d), pl.BlockSpec((tn, tn), lambda i, j, k: (i, j)), tn, "mm_g_wo", mix, d_x1b,
                 (d // tn, d // tn, s // ts))
    def reduce_start(tag, grads):
        n = len(grads)
        lands = [lax.empty((N_CHIPS, a.shape[1] // 2, a.shape[2]), BF16) for a in grads]
        swapped = exchange("swap_" + tag, plan_swap, list(grads) + lands, n)
        parts = [add_sibling(g, o, core, "add_sibling_%s%d" % (tag, i))
                 for i, (g, o) in enumerate(zip(swapped[:n], swapped[n:]))]
        return exchange_start("scatter_%s_start" % tag, [(plan_scatter, list(range(2 * n)), 3 * n)],
                              parts + [lax.empty(a.shape, BF16) for a in parts])

    def reduce_sum(tag, handle, after):
        sems, arrays, _ = handle
        n = len(arrays) // 2
        got = exchange_wait("scatter_%s_wait" % tag, plan_scatter, sems[0], arrays, after)
        return [sum_chips(pt, o, place, "sum_chips_%s%d" % (tag, i))
                for i, (pt, o) in enumerate(zip(got[:n], got[n:]))]

    early = reduce_start("early", [g_wo.reshape(N_CHIPS, dq, d), g_wg.reshape(N_CHIPS, dq, d), g_wp])
    d_o, d_gate, d_mix_conv = out_proj_bwd_gate(d_x1b, wo_full, attn, z, aw, ga_off, after=(early[-1],))
    dqs, dks, dvs, acc_sink = attn_bwd(qs, ks, vb, d_o, attn_sinks, aw)
    dz, acc_qk = qk_prep_bwd(z, dqs, dks, dvs, d_gate, tabs, qg, kg, segb, aw)
    dz, acc_conv = conv_bwd(z, d_mix_conv, conv_full, dz, aw, cw, b_off)
    join_sems, joining, join_token = exchange_start(
        "join_early_start", [(plan_join, [0, 1, 2], 3)], reduce_sum("early", early, dz))
    g_send = in_proj_wgrad_half(h, dz, None, 1 - core, after=(join_token,))
    swap_sems, swapping, swap_token = exchange_start(
        "swap_late_start", [(plan_swap, [0, 1], 1)], [g_send, lax.empty((N_CHIPS, d // 2, sh), BF16)])
    g_wi = in_proj_wgrad_half(h, dz, swapping[0], core, after=(swap_token,))
    full_wo, full_wg, full_wp = exchange_wait("join_early_wait", plan_join, join_sems[0], joining, g_wi)
    g_wi, got_wi = exchange_wait("swap_late_wait", plan_swap, swap_sems[0], [g_wi, swapping[1]], full_wo)
    part_wi = add_sibling(g_wi, got_wi, core, "add_sibling_late0")
    late = exchange_start("scatter_late_start", [(plan_scatter, [0, 1], 3)],
                          [part_wi, lax.empty(part_wi.shape, BF16)])
    grad_x, acc_x = in_proj_bwd_norm(dz, w_shards, order, x2, d_x1, norm_gain, after=(late[-1],))

    wsm = max(d, cw)
    misc = jnp.concatenate([acc_qk[0:1, :HEAD_DIM], acc_qk[1:2, :HEAD_DIM], acc_sink[0:1, :nq]], axis=1)
    small = jnp.concatenate([
        _pad_cols(acc_x[0:1], wsm), _pad_cols(acc_g[0:1], wsm), _pad_cols(acc_head[0:1], wsm),
        _pad_cols(acc_head[1:2], wsm), _pad_cols(misc, wsm), _pad_cols(acc_conv[0:3], wsm),
        _pad_rows(_pad_cols(acc_head[2:3], wsm), SUBLANES)], axis=0)
    device = 2 * chip + lax.axis_index("c")
    small_sems, small_bufs, small_token = exchange_start(
        "small_start", [(plan_allgather_small, [0], 7)],
        [lax.dynamic_update_slice(jnp.zeros((8,) + small.shape, F32), small[None], (device, 0, 0))])
    last_sems, last_halves, last_token = exchange_start(
        "join_late_start", [(plan_join, [0], 1)], reduce_sum("late", late, small_token))
    big, after = {}, (last_token,)
    for nm, g, w, m, v in (("w_out", full_wo, w_out, m_w_out, v_w_out),
                           ("w_ple_gate", full_wg, w_ple_gate, m_w_ple_gate, v_w_ple_gate),
                           ("w_ple_proj", full_wp, w_ple_proj, m_w_ple_proj, v_w_ple_proj)):
        stepped = adamw(g, w[0], m[0], v[0], "adamw_" + nm, after=after)
        big[nm], after = [o[None] for o in stepped], (stepped[1],)
    full_wi, = exchange_wait("join_late_wait", plan_join, last_sems[0], last_halves, after[0])
    big["w_in"] = [o[None] for o in adamw(full_wi, w_in[0], m_w_in[0], v_w_in[0], "adamw_w_in")]

    gathered, = exchange_wait("small_wait", plan_allgather_small, small_sems[0], small_bufs, big["w_in"][1])
    tot = sum_devices(gathered)
    loss = tot[SUBLANES, 0]

    def pack(vals):
        ng, pg, bg, eg, qgv, kgv, sk, cv = vals
        misc_v = jnp.concatenate([qgv, kgv, sk], axis=1)
        return jnp.concatenate([_pad_cols(ng, wsm), _pad_cols(pg, wsm), _pad_cols(bg, wsm), _pad_cols(eg, wsm),
                                _pad_cols(misc_v, wsm), _pad_cols(cv[0], wsm)], axis=0)

    g_small = jnp.concatenate(
        [tot[0:5], _pad_cols(lax.dynamic_slice(tot[5:8], (0, chip * cq), (3, cq)), wsm)], axis=0)
    w_small = pack((norm_gain, ple_gate_norm_gain, b_ple_gate, ple_norm_gain, q_norm_gain, k_norm_gain,
                    attn_sinks, conv_w))
    m_small = pack((m_norm_gain, m_ple_gate_norm_gain, m_b_ple_gate, m_ple_norm_gain, m_q_norm_gain,
                    m_k_norm_gain, m_attn_sinks, m_conv_w))
    v_small = pack((v_norm_gain, v_ple_gate_norm_gain, v_b_ple_gate, v_ple_norm_gain, v_q_norm_gain,
                    v_k_norm_gain, v_attn_sinks, v_conv_w))
    sm = adamw(g_small, w_small, m_small, v_small, "adamw_small")

    def unpack(a):
        return {"norm_gain": a[0:1, :d], "ple_gate_norm_gain": a[1:2, :d], "b_ple_gate": a[2:3, :d],
                "ple_norm_gain": a[3:4, :d], "q_norm_gain": a[4:5, :HEAD_DIM],
                "k_norm_gain": a[4:5, HEAD_DIM:2 * HEAD_DIM],
                "attn_sinks": a[4:5, 2 * HEAD_DIM:2 * HEAD_DIM + nq], "conv_w": a[5:8, :cq][None]}

    order = ("norm_gain", "w_in", "q_norm_gain", "k_norm_gain", "attn_sinks", "conv_w", "w_out",
             "ple_gate_norm_gain", "w_ple_gate", "b_ple_gate", "w_ple_proj", "ple_norm_gain")
    outs = [loss, grad_x[None]]
    for kind in range(4):
        table = unpack(sm[kind])
        for nm in order:
            outs.append(big[nm][kind] if nm in big else table[nm])
    return tuple(outs)
```

```python
import functools

import jax
import jax.numpy as jnp
from jax import lax
from jax.experimental import pallas as pl
from jax.experimental.pallas import tpu as pltpu

F32 = jnp.float32
BF16 = jnp.bfloat16
MESH = pl.DeviceIdType.MESH

HEAD_DIM = 64
N_KV_HEADS = 4
KV_WIDTH = N_KV_HEADS * HEAD_DIM
BLOCK = 128
ROT_DIM = 16
ROPE_THETA = 500000.0
EPS = 1e-6
NEG_INF = -1e30
N_CHIPS = 4
LANES = 128
SUBLANES = 8
COLT = 512
SEG = 256
VMEM_LIMIT = 48 * 1024 * 1024
VMEM_LIMIT_BIG = 60 * 1024 * 1024

ADAM_LR = 0.001
ADAM_B1 = 0.9
ADAM_B2 = 0.999
ADAM_EPS = 1e-08
ADAM_WD = 0.01
ADAM_STEP = 10

NN = (((1,), (0,)), ((), ()))
NT = (((1,), (1,)), ((), ()))
TN = (((0,), (0,)), ((), ()))


def _call(body, **kw):
    return pl.pallas_call(body, **kw)


def _call_after(body, after, **kw):
    n_in, n_after = len(kw["in_specs"]), len(after)
    kw["in_specs"] = list(kw["in_specs"]) + [pl.BlockSpec(memory_space=pl.ANY)] * n_after

    def body_after(*refs):
        body(*refs[:n_in], *refs[n_in + n_after:])

    call = _call(body_after, **kw)
    return lambda *args: call(*args, *after)


def _params(sem, vmem=VMEM_LIMIT):
    return pltpu.CompilerParams(dimension_semantics=sem, vmem_limit_bytes=vmem)


def _tile(n, pref):
    return pref if n % pref == 0 else n


def _sigmoid(v):
    return 1.0 / (1.0 + jnp.exp(-v))


def _hbm():
    return pl.BlockSpec(memory_space=pl.ANY)


def cast_bf16(a, name):
    r, c = a.shape
    tr = _tile(r, 512)

    def body(a_ref, o_ref):
        o_ref[...] = a_ref[...].astype(BF16)

    return _call(body, name=name, grid=(r // tr,),
                 in_specs=[pl.BlockSpec((tr, c), lambda i: (i, 0))],
                 out_specs=pl.BlockSpec((tr, c), lambda i: (i, 0)),
                 out_shape=jax.ShapeDtypeStruct((r, c), BF16),
                 compiler_params=_params(("parallel",)))(a)


def cast_into_slot(a, chip, name):
    r, c = a.shape
    tr = _tile(r, 512)

    def body(chip_ref, a_ref, o_ref):
        o_ref[...] = a_ref[...].astype(BF16)

    grid_spec = pltpu.PrefetchScalarGridSpec(
        num_scalar_prefetch=1, grid=(r // tr,),
        in_specs=[pl.BlockSpec((tr, c), lambda i, chip_ref: (i, 0))],
        out_specs=pl.BlockSpec((None, tr, c), lambda i, chip_ref: (chip_ref[0], i, 0)))
    return _call(body, name=name, grid_spec=grid_spec,
                 out_shape=jax.ShapeDtypeStruct((N_CHIPS, r, c), BF16),
                 compiler_params=_params(("parallel",)))(chip, a)


def out_proj_norm(mix, wo, x, gain):
    s, d = x.shape
    tm = _tile(s, 512)

    def body(m_ref, w_ref, x_ref, g_ref, x1_ref, hg_ref):
        x1 = x_ref[...] + jnp.dot(m_ref[...], w_ref[...], preferred_element_type=F32)
        x1_ref[...] = x1
        r = lax.rsqrt(jnp.mean(x1 * x1, axis=-1, keepdims=True) + EPS)
        hg_ref[...] = ((x1 * r) * g_ref[...]).astype(BF16)

    row = pl.BlockSpec((tm, d), lambda i: (i, 0))
    return _call(body, name="out_proj_norm", grid=(s // tm,),
                 in_specs=[row, pl.BlockSpec((d, d), lambda i: (0, 0), pipeline_mode=pl.Buffered(1)), row,
                           pl.BlockSpec((1, d), lambda i: (0, 0))],
                 out_specs=[row, row],
                 out_shape=[jax.ShapeDtypeStruct((s, d), F32), jax.ShapeDtypeStruct((s, d), BF16)],
                 compiler_params=_params(("parallel",), VMEM_LIMIT_BIG))(mix, wo, x, gain)


def gate_proj_bwd_norm(d_gl, wg, xin, add, gain):
    s, d = xin.shape
    tm = _tile(s, 256)

    def body(dg_ref, w_ref, x_ref, a_ref, g_ref, dx_ref, dxb_ref, acc_ref):
        i = pl.program_id(0)
        dyv = lax.dot_general(dg_ref[...], w_ref[...], NT, preferred_element_type=F32)
        xf = x_ref[...]
        r = lax.rsqrt(jnp.mean(xf * xf, axis=-1, keepdims=True) + EPS)
        xhat = xf * r
        gd = dyv * g_ref[...]
        dx = a_ref[...] + r * (gd - xhat * jnp.mean(xhat * gd, axis=-1, keepdims=True))
        dx_ref[...] = dx
        dxb_ref[...] = dx.astype(BF16)

        @pl.when(i == 0)
        def _():
            acc_ref[...] = jnp.zeros_like(acc_ref)

        acc_ref[0:1, :] += jnp.sum(dyv * xhat, axis=0, keepdims=True)

    row = pl.BlockSpec((tm, d), lambda i: (i, 0))
    return _call(body, name="gate_proj_bwd_norm", grid=(s // tm,),
                 in_specs=[row, pl.BlockSpec((d, d), lambda i: (0, 0), pipeline_mode=pl.Buffered(1)), row, row,
                           pl.BlockSpec((1, d), lambda i: (0, 0))],
                 out_specs=[row, row, pl.BlockSpec((SUBLANES, d), lambda i: (0, 0))],
                 out_shape=[jax.ShapeDtypeStruct((s, d), F32), jax.ShapeDtypeStruct((s, d), BF16),
                            jax.ShapeDtypeStruct((SUBLANES, d), F32)],
                 compiler_params=_params(("arbitrary",), VMEM_LIMIT_BIG))(d_gl, wg, xin, add, gain)


def head_fwd_bwd(x1, hg, wg, p, wp, tgt, bias, ple_gain):
    s, d = x1.shape
    tm = _tile(s, 256)

    def body(x1_ref, hg_ref, wg_ref, p_ref, wp_ref, t_ref, b_ref, g_ref, dgl_ref, dpe_ref, dy_ref, acc_ref):
        i = pl.program_id(0)
        gl = jnp.dot(hg_ref[...], wg_ref[...], preferred_element_type=F32)
        pb = p_ref[...].astype(BF16)
        pev = jnp.concatenate([jnp.dot(pb, wp_ref[q], preferred_element_type=F32) for q in range(N_CHIPS)],
                              axis=1)
        r = lax.rsqrt(jnp.mean(pev * pev, axis=-1, keepdims=True) + EPS)
        pehat = pev * r
        gain = g_ref[...]
        e = pehat * gain
        gate = _sigmoid(gl + b_ref[...])
        diff = (x1_ref[...] + gate * e) - t_ref[...]
        dy = diff * (1.0 / d)
        dy_ref[...] = dy
        d_gl = (dy * e) * (gate * (1.0 - gate))
        dgl_ref[...] = d_gl.astype(BF16)
        d_e = dy * gate
        gd = d_e * gain
        d_pe = r * (gd - pehat * jnp.mean(pehat * gd, axis=-1, keepdims=True))
        dpe_ref[...] = d_pe.astype(BF16)

        @pl.when(i == 0)
        def _():
            acc_ref[...] = jnp.zeros_like(acc_ref)

        acc_ref[0:1, :] += jnp.sum(d_gl, axis=0, keepdims=True)
        acc_ref[1:2, :] += jnp.sum(d_e * pehat, axis=0, keepdims=True)
        acc_ref[2:3, :] += jnp.sum(diff * diff, axis=0, keepdims=True) * (0.5 / d)

    row = pl.BlockSpec((tm, d), lambda i: (i, 0))
    vec = pl.BlockSpec((1, d), lambda i: (0, 0))
    return _call(body, name="head_fwd_bwd", grid=(s // tm,),
                 in_specs=[row, row, pl.BlockSpec((d, d), lambda i: (0, 0), pipeline_mode=pl.Buffered(1)),
                           pl.BlockSpec((tm, p.shape[1]), lambda i: (i, 0)),
                           pl.BlockSpec(wp.shape, lambda i: (0, 0, 0)), row, vec, vec],
                 out_specs=[row, row, row, pl.BlockSpec((SUBLANES, d), lambda i: (0, 0))],
                 out_shape=[jax.ShapeDtypeStruct((s, d), BF16), jax.ShapeDtypeStruct((s, d), BF16),
                            jax.ShapeDtypeStruct((s, d), F32), jax.ShapeDtypeStruct((SUBLANES, d), F32)],
                 compiler_params=_params(("arbitrary",), VMEM_LIMIT_BIG))(
                     x1, hg, wg, p, wp, tgt, bias, ple_gain)


def adamw(g, w, m, v, name, after=()):
    r, c = g.shape
    tr = _tile(r, 256)

    def body(g_ref, w_ref, m_ref, v_ref, go_ref, d_ref, mo_ref, vo_ref):
        gv = g_ref[...]
        mn = ADAM_B1 * m_ref[...] + (1.0 - ADAM_B1) * gv
        vn = ADAM_B2 * v_ref[...] + (1.0 - ADAM_B2) * (gv * gv)
        m_hat = mn / (1.0 - ADAM_B1 ** ADAM_STEP)
        v_hat = vn / (1.0 - ADAM_B2 ** ADAM_STEP)
        go_ref[...] = gv
        d_ref[...] = -ADAM_LR * (m_hat / (jnp.sqrt(v_hat) + ADAM_EPS) + ADAM_WD * w_ref[...])
        mo_ref[...] = mn
        vo_ref[...] = vn

    blk = pl.BlockSpec((tr, c), lambda i: (i, 0))
    shp = jax.ShapeDtypeStruct((r, c), F32)
    return _call_after(body, after, name=name, grid=(r // tr,), in_specs=[blk] * 4, out_specs=[blk] * 4,
                       out_shape=[shp] * 4, compiler_params=_params(("parallel",)))(g, w, m, v)


def matmul(a, b, *, grid, a_spec, b_spec, o_spec, out_shape, dims, name, res=None, res_spec=None, after=(),
           vmem=VMEM_LIMIT):
    nk = grid[2]
    acc_shape = tuple(d for d in o_spec.block_shape if d is not None)

    def body(*refs):
        a_ref, b_ref = refs[:2]
        r_ref = refs[2] if res is not None else None
        o_ref = refs[3] if res is not None else refs[2]
        part = lax.dot_general(a_ref[...].astype(BF16), b_ref[...].astype(BF16), dims, preferred_element_type=F32)

        def finish(out):
            if res is not None:
                out = r_ref[...] + out
            o_ref[...] = out.astype(o_ref.dtype)

        if nk == 1:
            finish(part)
            return
        acc_ref = refs[-1]
        k = pl.program_id(2)

        @pl.when(k == 0)
        def _():
            acc_ref[...] = part

        @pl.when((k > 0) & (k < nk - 1))
        def _():
            acc_ref[...] += part

        @pl.when(k == nk - 1)
        def _():
            finish(acc_ref[...] + part)

    in_specs = [a_spec, b_spec] + ([res_spec] if res is not None else [])
    args = (a, b) + ((res,) if res is not None else ())
    return _call_after(body, after, name=name, grid=grid, in_specs=in_specs, out_specs=o_spec,
                       out_shape=out_shape, scratch_shapes=[pltpu.VMEM(acc_shape, F32)] if nk > 1 else [],
                       compiler_params=_params(("parallel", "parallel", "arbitrary"), vmem))(*args)


def norm_in_proj(x, gain, w, order, in_w):
    s, d = x.shape
    sh = w.shape[1]
    tm = _tile(s, 512)

    def body(order_ref, x_ref, g_ref, w_ref, h_ref, z_ref):
        xf = x_ref[...]
        r = lax.rsqrt(jnp.mean(xf * xf, axis=-1, keepdims=True) + EPS)
        hb = ((xf * r) * g_ref[...]).astype(BF16)
        h_ref[...] = hb
        z_ref[...] = jnp.dot(hb, w_ref[...], preferred_element_type=F32)

    grid_spec = pltpu.PrefetchScalarGridSpec(
        num_scalar_prefetch=1, grid=(s // tm,),
        in_specs=[pl.BlockSpec((tm, d), lambda i, order_ref: (i, 0)),
                  pl.BlockSpec((1, d), lambda i, order_ref: (0, 0)),
                  pl.BlockSpec((d, sh), lambda i, order_ref: (0, 0), pipeline_mode=pl.Buffered(1))],
        out_specs=[pl.BlockSpec((tm, d), lambda i, order_ref: (i, 0)),
                   pl.BlockSpec((tm, sh), lambda i, order_ref: (i, order_ref[0]))])
    return _call(body, name="norm_in_proj", grid_spec=grid_spec,
                 out_shape=[jax.ShapeDtypeStruct((s, d), BF16), jax.ShapeDtypeStruct((s, in_w), F32)],
                 compiler_params=_params(("parallel",)))(order, x, gain, w)


def in_proj_part(h, w, z_prev, order, q, in_w):
    s, d = h.shape
    sh = w.shape[1]
    tm = _tile(s, 512)

    def body(order_ref, h_ref, w_ref, *rest):
        rest[-1][...] = jnp.dot(h_ref[...], w_ref[...], preferred_element_type=F32)

    in_specs = [pl.BlockSpec((tm, d), lambda i, order_ref: (i, 0)),
                pl.BlockSpec((d, sh), lambda i, order_ref: (0, 0))]
    args = [order, h, w]
    if z_prev is not None:
        in_specs.append(_hbm())
        args.append(z_prev)
    grid_spec = pltpu.PrefetchScalarGridSpec(
        num_scalar_prefetch=1, grid=(s // tm,), in_specs=in_specs,
        out_specs=pl.BlockSpec((tm, sh), lambda i, order_ref: (i, order_ref[q])))
    return _call(body, name="mm_z%d" % q, grid_spec=grid_spec,
                 out_shape=jax.ShapeDtypeStruct((s, in_w), F32),
                 input_output_aliases={3: 0} if z_prev is not None else {},
                 compiler_params=_params(("parallel",)))(*args)


def in_proj_wgrad_half(h, dz, g_prev, half, after):
    s, d = h.shape
    sh = dz.shape[1] // N_CHIPS
    hr = d // 2

    def body(half_ref, h_ref, dz_ref, *rest):
        rest[-1][...] = lax.dot_general(h_ref[...], dz_ref[...], TN, preferred_element_type=F32).astype(BF16)

    extra = ([g_prev] if g_prev is not None else []) + list(after)
    grid_spec = pltpu.PrefetchScalarGridSpec(
        num_scalar_prefetch=1, grid=(N_CHIPS,),
        in_specs=[pl.BlockSpec((s, hr), lambda j, half_ref: (0, half_ref[0]), pipeline_mode=pl.Buffered(1)),
                  pl.BlockSpec((s, sh), lambda j, half_ref: (0, j))] + [_hbm()] * len(extra),
        out_specs=pl.BlockSpec((None, hr, sh), lambda j, half_ref: (j, half_ref[0], 0)))
    return _call(body, name="mm_g_wi_%s" % ("keep" if g_prev is not None else "send"), grid_spec=grid_spec,
                 out_shape=jax.ShapeDtypeStruct((N_CHIPS, d, sh), BF16),
                 input_output_aliases={3: 0} if g_prev is not None else {},
                 compiler_params=_params(("parallel",), VMEM_LIMIT_BIG))(half, h, dz, *extra)


def in_proj_bwd_norm(dz, ws, order, xin, add, gain, after):
    s, d = xin.shape
    sh = ws[0].shape[1]
    tm = _tile(s, 256)
    nq, n_after = len(ws), len(after)

    def body(order_ref, *refs):
        a_refs, b_refs = refs[:nq], refs[nq:2 * nq]
        x_ref, add_ref, g_ref = refs[2 * nq:2 * nq + 3]
        dx_ref, acc_ref = refs[2 * nq + 3 + n_after:]
        i = pl.program_id(0)
        dyv = lax.dot_general(a_refs[0][...], b_refs[0][...], NT, preferred_element_type=F32)
        for q in range(1, nq):
            dyv += lax.dot_general(a_refs[q][...], b_refs[q][...], NT, preferred_element_type=F32)
        xf = x_ref[...]
        r = lax.rsqrt(jnp.mean(xf * xf, axis=-1, keepdims=True) + EPS)
        xhat = xf * r
        gd = dyv * g_ref[...]
        dx_ref[...] = add_ref[...] + r * (gd - xhat * jnp.mean(xhat * gd, axis=-1, keepdims=True))

        @pl.when(i == 0)
        def _():
            acc_ref[...] = jnp.zeros_like(acc_ref)

        acc_ref[0:1, :] += jnp.sum(dyv * xhat, axis=0, keepdims=True)

    a_spec = lambda q: pl.BlockSpec((tm, sh), functools.partial(lambda i, order_ref, q: (i, order_ref[q]), q=q))
    row = pl.BlockSpec((tm, d), lambda i, order_ref: (i, 0))
    grid_spec = pltpu.PrefetchScalarGridSpec(
        num_scalar_prefetch=1, grid=(s // tm,),
        in_specs=[a_spec(q) for q in range(nq)]
        + [pl.BlockSpec((d, sh), lambda i, order_ref: (0, 0), pipeline_mode=pl.Buffered(1))] * nq
        + [row, row, pl.BlockSpec((1, d), lambda i, order_ref: (0, 0))] + [_hbm()] * n_after,
        out_specs=[row, pl.BlockSpec((SUBLANES, d), lambda i, order_ref: (0, 0))])
    return _call(body, name="in_proj_bwd_norm", grid_spec=grid_spec,
                 out_shape=[jax.ShapeDtypeStruct((s, d), F32), jax.ShapeDtypeStruct((SUBLANES, d), F32)],
                 compiler_params=_params(("arbitrary",), VMEM_LIMIT_BIG))(
                     order, *([dz] * nq), *ws, xin, add, gain, *after)


def _seg_mean(sq, segb):
    parts = []
    for cgrp in range(sq.shape[1] // SEG):
        blk = sq[:, cgrp * SEG:(cgrp + 1) * SEG]
        hi = blk.astype(BF16)
        r1 = blk - hi.astype(F32)
        mid = r1.astype(BF16)
        lo = (r1 - mid.astype(F32)).astype(BF16)
        acc = jnp.dot(hi, segb, preferred_element_type=F32)
        acc += jnp.dot(mid, segb, preferred_element_type=F32)
        acc += jnp.dot(lo, segb, preferred_element_type=F32)
        parts.append(acc)
    out = parts[0] if len(parts) == 1 else jnp.concatenate(parts, axis=1)
    return out * (1.0 / HEAD_DIM)


def _rope(v, cos, sa, sb):
    parts = []
    for cgrp in range(v.shape[1] // LANES):
        blk = v[:, cgrp * LANES:(cgrp + 1) * LANES]
        parts.append(blk * cos + pltpu.roll(blk, LANES - 8, 1) * sa + pltpu.roll(blk, 8, 1) * sb)
    return parts[0] if len(parts) == 1 else jnp.concatenate(parts, axis=1)


def _rope_t(dv, cos, sa, sb):
    parts = []
    for cgrp in range(dv.shape[1] // LANES):
        blk = dv[:, cgrp * LANES:(cgrp + 1) * LANES]
        parts.append(blk * cos + pltpu.roll(blk * sa, 8, 1) + pltpu.roll(blk * sb, LANES - 8, 1))
    return parts[0] if len(parts) == 1 else jnp.concatenate(parts, axis=1)


def _tile_lanes(vec, width):
    reps = width // LANES
    return vec if reps == 1 else jnp.tile(vec, (1, reps))


def qk_prep_fwd(z, tabs, qg, kg, segb, aw, after=()):
    s = z.shape[0]
    tm = _tile(s, 512)
    wq = aw + 2 * KV_WIDTH

    def body(z_ref, cos_ref, sa_ref, sb_ref, qg_ref, kg_ref, seg_ref, qs_ref, ks_ref, vb_ref):
        zz = z_ref[...]
        q, k, v = zz[:, :aw], zz[:, aw:aw + KV_WIDTH], zz[:, aw + KV_WIDTH:]
        cos, sa, sb, segm = cos_ref[...], sa_ref[...], sb_ref[...], seg_ref[...]
        rq = lax.rsqrt(_seg_mean(q * q, segm) + EPS)
        qn = (q * rq) * _tile_lanes(qg_ref[...], aw)
        qs_ref[...] = (_rope(qn, cos, sa, sb) * (HEAD_DIM ** -0.5)).astype(BF16)
        rk = lax.rsqrt(_seg_mean(k * k, segm) + EPS)
        kn = (k * rk) * _tile_lanes(kg_ref[...], KV_WIDTH)
        ks_ref[...] = _rope(kn, cos, sa, sb).astype(BF16)
        vb_ref[...] = v.astype(BF16)

    tab = pl.BlockSpec((tm, LANES), lambda i: (i, 0))
    vec = pl.BlockSpec((1, LANES), lambda i: (0, 0))
    return _call_after(body, after, name="qk_prep_fwd", grid=(s // tm,),
                       in_specs=[pl.BlockSpec((tm, wq), lambda i: (i, 0)), tab, tab, tab, vec, vec,
                                 pl.BlockSpec((SEG, SEG), lambda i: (0, 0))],
                       out_specs=[pl.BlockSpec((tm, aw), lambda i: (i, 0)),
                                  pl.BlockSpec((tm, KV_WIDTH), lambda i: (i, 0)),
                                  pl.BlockSpec((tm, KV_WIDTH), lambda i: (i, 0))],
                       out_shape=[jax.ShapeDtypeStruct((s, aw), BF16), jax.ShapeDtypeStruct((s, KV_WIDTH), BF16),
                                  jax.ShapeDtypeStruct((s, KV_WIDTH), BF16)],
                       compiler_params=_params(("parallel",)))(z, *tabs, qg, kg, segb)


def qk_prep_bwd(z, dqs, dks, dvs, d_gate, tabs, qg, kg, segb, aw):
    s, in_w = z.shape
    tm = _tile(s, 512)
    wq = aw + 2 * KV_WIDTH

    def body(z_ref, dq_ref, dk_ref, dv_ref, dga_ref, cos_ref, sa_ref, sb_ref, qg_ref, kg_ref, seg_ref,
             dz_ref, acc_ref):
        i = pl.program_id(0)
        zz = z_ref[...]
        q, k = zz[:, :aw], zz[:, aw:aw + KV_WIDTH]
        cos, sa, sb, segm = cos_ref[...], sa_ref[...], sb_ref[...], seg_ref[...]

        def one(xv, dout, gvec, width):
            g = _tile_lanes(gvec, width)
            r = lax.rsqrt(_seg_mean(xv * xv, segm) + EPS)
            xhat = xv * r
            dn = _rope_t(dout, cos, sa, sb)
            gd = dn * g
            dx = r * (gd - xhat * _seg_mean(xhat * gd, segm))
            contrib = jnp.sum(dn * xhat, axis=0, keepdims=True)
            folded = contrib[:, :LANES]
            for cgrp in range(1, width // LANES):
                folded = folded + contrib[:, cgrp * LANES:(cgrp + 1) * LANES]
            return dx, folded + pltpu.roll(folded, HEAD_DIM, 1)

        dq, gq = one(q, dq_ref[...] * (HEAD_DIM ** -0.5), qg_ref[...], aw)
        dk, gk = one(k, dk_ref[...], kg_ref[...], KV_WIDTH)
        dz_ref[:, :aw] = dq.astype(BF16)
        dz_ref[:, aw:aw + KV_WIDTH] = dk.astype(BF16)
        dz_ref[:, aw + KV_WIDTH:wq] = dv_ref[...].astype(BF16)
        dz_ref[:, wq:] = dga_ref[...]

        @pl.when(i == 0)
        def _():
            acc_ref[...] = jnp.zeros_like(acc_ref)

        acc_ref[0:1, :] += gq
        acc_ref[1:2, :] += gk

    tab = pl.BlockSpec((tm, LANES), lambda i: (i, 0))
    vec = pl.BlockSpec((1, LANES), lambda i: (0, 0))
    kvb = pl.BlockSpec((tm, KV_WIDTH), lambda i: (i, 0))
    awb = pl.BlockSpec((tm, aw), lambda i: (i, 0))
    return _call(body, name="qk_prep_bwd", grid=(s // tm,),
                 in_specs=[pl.BlockSpec((tm, wq), lambda i: (i, 0)), awb, kvb, kvb, awb, tab, tab, tab, vec, vec,
                           pl.BlockSpec((SEG, SEG), lambda i: (0, 0))],
                 out_specs=[pl.BlockSpec((tm, wq + aw), lambda i: (i, 0)),
                            pl.BlockSpec((SUBLANES, LANES), lambda i: (0, 0))],
                 out_shape=[jax.ShapeDtypeStruct((s, in_w), BF16), jax.ShapeDtypeStruct((SUBLANES, LANES), F32)],
                 compiler_params=_params(("arbitrary",)))(z, dqs, dks, dvs, d_gate, *tabs, qg, kg, segb)


def _placed(band, lane_idx):
    out = {}
    for kh in range(N_KV_HEADS):
        grp = band[:, (kh // 2) * LANES:(kh // 2 + 1) * LANES]
        for half in (0, 1):
            t = grp if half == kh % 2 else pltpu.roll(grp, HEAD_DIM, 1)
            keep = (lane_idx >= half * HEAD_DIM) & (lane_idx < (half + 1) * HEAD_DIM)
            out[kh, half] = jnp.where(keep, t, jnp.zeros_like(t))
    return out


def _softmax_with_sink(sc, valid, sink):
    sc = jnp.where(valid, sc, NEG_INF)
    m = jnp.maximum(jnp.max(sc, axis=-1, keepdims=True), sink)
    e = jnp.exp(sc - m)
    es = jnp.exp(sink - m)
    den = jnp.sum(e, axis=-1, keepdims=True) + es
    return e / den, es / den


def _stack_halves(placed, kh):
    return jnp.concatenate([placed[kh, 0], placed[kh, 1]], axis=0)


def _valid_mask(n):
    r_i = lax.broadcasted_iota(jnp.int32, (BLOCK, 2 * BLOCK), 0)
    j_i = lax.broadcasted_iota(jnp.int32, (BLOCK, 2 * BLOCK), 1)
    return (j_i > r_i) & (j_i <= r_i + BLOCK) & ((n > 0) | (j_i >= BLOCK))


def attn_fwd(qs, ks, vb, z, sinks, aw, d, ga_off):
    s = qs.shape[0]
    nb = s // BLOCK
    nq = aw // HEAD_DIM
    grp_sz = nq // N_KV_HEADS
    n_ga = aw // COLT

    def body(q_ref, ko_ref, kp_ref, vo_ref, vp_ref, *rest):
        ga_refs = rest[:n_ga]
        sink_ref, attn_ref, mix_ref = rest[n_ga:]
        n = pl.program_id(0)
        lane_idx = lax.broadcasted_iota(jnp.int32, (2 * BLOCK, LANES), 1)
        kpl = _placed(jnp.concatenate([kp_ref[...], ko_ref[...]], axis=0), lane_idx)
        vpl = _placed(jnp.concatenate([vp_ref[...], vo_ref[...]], axis=0), lane_idx)
        valid = _valid_mask(n)
        groups = range(nq // 2)
        kcat = [_stack_halves(kpl, kh) for kh in range(N_KV_HEADS)]
        vcat = [_stack_halves(vpl, kh) for kh in range(N_KV_HEADS)]
        scs = [lax.dot_general(q_ref[:, c * LANES:(c + 1) * LANES], kcat[2 * c // grp_sz], NT,
                               preferred_element_type=F32) for c in groups]
        probs = [jnp.concatenate(
            [_softmax_with_sink(scs[c][:, half * 2 * BLOCK:(half + 1) * 2 * BLOCK], valid,
                                sink_ref[0, 2 * c + half])[0].astype(BF16) for half in (0, 1)], axis=1)
                 for c in groups]
        for c in groups:
            acc = jnp.dot(probs[c], vcat[2 * c // grp_sz], preferred_element_type=F32)
            attn_ref[:, c * LANES:(c + 1) * LANES] = acc
            col = c * LANES
            ga = ga_refs[col // COLT][:, col % COLT:col % COLT + LANES]
            mix_ref[:, c * LANES:(c + 1) * LANES] = (acc * (ga * _sigmoid(ga))).astype(BF16)

    own = lambda n: (n, 0)
    prev = lambda n: (jnp.maximum(n - 1, 0), 0)
    kvs = lambda imap: pl.BlockSpec((BLOCK, KV_WIDTH), imap)
    ga_specs = [pl.BlockSpec((BLOCK, COLT), functools.partial(lambda n, j: (n, ga_off + j), j=j))
                for j in range(n_ga)]
    return _call(body, name="attn_fwd", grid=(nb,),
                 in_specs=[pl.BlockSpec((BLOCK, aw), own), kvs(own), kvs(prev), kvs(own), kvs(prev)]
                 + ga_specs + [pl.BlockSpec(memory_space=pltpu.SMEM)],
                 out_specs=[pl.BlockSpec((BLOCK, aw), own), pl.BlockSpec((BLOCK, aw), own)],
                 out_shape=[jax.ShapeDtypeStruct((s, aw), F32), jax.ShapeDtypeStruct((s, d), BF16)],
                 compiler_params=_params(("parallel",)))(qs, ks, ks, vb, vb, *([z] * n_ga), sinks)


def out_proj_bwd_gate(d_x1b, wo, attn, z, aw, ga_off, after=()):
    s, d = d_x1b.shape
    tm = _tile(s, 512)
    n_ga = aw // COLT

    def body(dx_ref, w_ref, at_ref, *rest):
        ga_refs, (do_ref, dga_ref, dmc_ref) = rest[:n_ga], rest[n_ga:]
        dm = lax.dot_general(dx_ref[...], w_ref[...], NT, preferred_element_type=F32)
        dmc_ref[...] = dm[:, aw:]
        for j in range(n_ga):
            cols = slice(j * COLT, (j + 1) * COLT)
            ga = ga_refs[j][...]
            sig = _sigmoid(ga)
            dma = dm[:, cols]
            do_ref[:, cols] = (dma * (ga * sig)).astype(BF16)
            dga_ref[:, cols] = ((dma * at_ref[:, cols]) * (sig * (1.0 + ga * (1.0 - sig)))).astype(BF16)

    row = lambda width: pl.BlockSpec((tm, width), lambda i: (i, 0))
    ga_specs = [pl.BlockSpec((tm, COLT), functools.partial(lambda i, j: (i, ga_off + j), j=j)) for j in range(n_ga)]
    return _call_after(body, after, name="out_proj_bwd_gate", grid=(s // tm,),
                       in_specs=[row(d), pl.BlockSpec((d, d), lambda i: (0, 0), pipeline_mode=pl.Buffered(1)),
                                 row(aw)] + ga_specs,
                       out_specs=[row(aw), row(aw), row(d - aw)],
                       out_shape=[jax.ShapeDtypeStruct((s, aw), BF16), jax.ShapeDtypeStruct((s, aw), BF16),
                                  jax.ShapeDtypeStruct((s, d - aw), F32)],
                       compiler_params=_params(("parallel",)))(d_x1b, wo, attn, *([z] * n_ga))


def attn_bwd(qs, ks, vb, d_o, sinks, aw, after=()):
    s = qs.shape[0]
    nb = s // BLOCK
    nq = aw // HEAD_DIM
    grp_sz = nq // N_KV_HEADS

    def body(q_ref, ko_ref, kp_ref, vo_ref, vp_ref, do_ref, sink_ref, dq_ref, dk_ref, dv_ref, ds_ref,
             ck_ref, cv_ref):
        n = pl.program_id(0)

        @pl.when(n == 0)
        def _():
            ds_ref[...] = jnp.zeros_like(ds_ref)
            dk_ref[...] = jnp.zeros_like(dk_ref)
            dv_ref[...] = jnp.zeros_like(dv_ref)

        @pl.when(n < nb)
        def _():
            lane_idx = lax.broadcasted_iota(jnp.int32, (2 * BLOCK, LANES), 1)
            lane_row = lax.broadcasted_iota(jnp.int32, (1, LANES), 1)
            kpl = _placed(jnp.concatenate([kp_ref[...], ko_ref[...]], axis=0), lane_idx)
            vpl = _placed(jnp.concatenate([vp_ref[...], vo_ref[...]], axis=0), lane_idx)
            valid = _valid_mask(n)
            ds_row = jnp.zeros((1, LANES), F32)
            wide = 2 * BLOCK
            groups = range(nq // 2)
            per_kv = grp_sz // 2
            kcat = [_stack_halves(kpl, kh) for kh in range(N_KV_HEADS)]
            vcat = [_stack_halves(vpl, kh) for kh in range(N_KV_HEADS)]
            qg = [q_ref[:, c * LANES:(c + 1) * LANES] for c in groups]
            dog = [do_ref[:, c * LANES:(c + 1) * LANES] for c in groups]
            scs = [lax.dot_general(qg[c], kcat[c // per_kv], NT, preferred_element_type=F32) for c in groups]
            dps = [lax.dot_general(dog[c], vcat[c // per_kv], NT, preferred_element_type=F32) for c in groups]
            ds_pairs, p_pairs = [], []
            for c in groups:
                probs, dss = [], []
                for half in (0, 1):
                    h = 2 * c + half
                    cols = slice(half * wide, (half + 1) * wide)
                    p, p_sink = _softmax_with_sink(scs[c][:, cols], valid, sink_ref[0, h])
                    delta = jnp.sum(p * dps[c][:, cols], axis=-1, keepdims=True)
                    dss.append((p * (dps[c][:, cols] - delta)).astype(BF16))
                    probs.append(p.astype(BF16))
                    dsink = -jnp.sum(p_sink * delta, axis=0, keepdims=True)
                    ds_row += jnp.where(lane_row == h, dsink, 0.0)
                ds_pairs.append(jnp.concatenate(dss, axis=1))
                p_pairs.append(jnp.concatenate(probs, axis=1))
            for c in groups:
                dq_ref[:, c * LANES:(c + 1) * LANES] = jnp.dot(ds_pairs[c], kcat[c // per_kv],
                                                               preferred_element_type=F32)
            dkts = [lax.dot_general(qg[c], ds_pairs[c], TN, preferred_element_type=F32) for c in groups]
            dvts = [lax.dot_general(dog[c], p_pairs[c], TN, preferred_element_type=F32) for c in groups]
            dkt, dvt = [], []
            for kh in range(N_KV_HEADS):
                for parts, out in ((dkts, dkt), (dvts, dvt)):
                    tot = parts[kh * per_kv]
                    for c in range(kh * per_kv + 1, (kh + 1) * per_kv):
                        tot = tot + parts[c]
                    out.append(tot[:HEAD_DIM, :wide] + tot[HEAD_DIM:, wide:])
            ds_ref[0:1, :] += ds_row
            back = lambda t: jnp.concatenate(
                [jnp.concatenate(t[2 * g:2 * g + 2], axis=0).T for g in range(N_KV_HEADS // 2)], axis=1)
            dk_band, dv_band = back(dkt), back(dvt)

            @pl.when(n > 0)
            def _():
                dk_ref[...] = ck_ref[...] + dk_band[:BLOCK]
                dv_ref[...] = cv_ref[...] + dv_band[:BLOCK]

            ck_ref[...] = dk_band[BLOCK:]
            cv_ref[...] = dv_band[BLOCK:]

        @pl.when(n == nb)
        def _():
            dk_ref[...] = ck_ref[...]
            dv_ref[...] = cv_ref[...]

    own = lambda n: (jnp.minimum(n, nb - 1), 0)
    prev = lambda n: (jnp.clip(n - 1, 0, nb - 1), 0)
    done = lambda n: (jnp.maximum(n - 1, 0), 0)
    kvs = lambda imap: pl.BlockSpec((BLOCK, KV_WIDTH), imap)
    return _call_after(body, after, name="attn_bwd", grid=(nb + 1,),
                       in_specs=[pl.BlockSpec((BLOCK, aw), own), kvs(own), kvs(prev), kvs(own), kvs(prev),
                                 pl.BlockSpec((BLOCK, aw), own), pl.BlockSpec(memory_space=pltpu.SMEM)],
                       out_specs=[pl.BlockSpec((BLOCK, aw), own), kvs(done), kvs(done),
                                  pl.BlockSpec((SUBLANES, LANES), lambda n: (0, 0))],
                       out_shape=[jax.ShapeDtypeStruct((s, aw), F32), jax.ShapeDtypeStruct((s, KV_WIDTH), F32),
                                  jax.ShapeDtypeStruct((s, KV_WIDTH), F32),
                                  jax.ShapeDtypeStruct((SUBLANES, LANES), F32)],
                       scratch_shapes=[pltpu.VMEM((BLOCK, KV_WIDTH), F32), pltpu.VMEM((BLOCK, KV_WIDTH), F32)],
                       compiler_params=_params(("arbitrary",)))(qs, ks, ks, vb, vb, d_o, sinks)


def conv_fwd(z, conv_w, mix, aw, cw, b_off):
    s = z.shape[0]
    tm = _tile(s, 1024)
    nseg = cw // COLT
    hb = tm // SUBLANES

    def body(b_ref, c_ref, h_ref, g_ref, cp_ref, hp_ref, w_ref, mix_in, mix_ref):
        i = pl.program_id(0)
        u = c_ref[...] * h_ref[...]
        up = jnp.where(i > 0, cp_ref[...] * hp_ref[...], 0.0)
        ext = jnp.concatenate([up, u], axis=0)
        um1 = pltpu.roll(ext, 1, 0)[SUBLANES:]
        um2 = pltpu.roll(ext, 2, 0)[SUBLANES:]
        w = w_ref[...]
        cv = w[0:1] * um2 + w[1:2] * um1 + w[2:3] * u
        g = g_ref[...]
        mix_ref[...] = ((b_ref[...] * cv) * (g * _sigmoid(g))).astype(BF16)

    seg = lambda k: pl.BlockSpec((tm, COLT), functools.partial(lambda i, j, k: (i, b_off + k * nseg + j), k=k))
    halo = lambda k: pl.BlockSpec(
        (SUBLANES, COLT), functools.partial(lambda i, j, k: (jnp.maximum(i * hb - 1, 0), b_off + k * nseg + j), k=k))
    return _call(body, name="conv_fwd", grid=(s // tm, nseg),
                 in_specs=[seg(0), seg(1), seg(2), seg(3), halo(1), halo(2),
                           pl.BlockSpec((SUBLANES, COLT), lambda i, j: (0, j)), _hbm()],
                 out_specs=pl.BlockSpec((tm, COLT), lambda i, j: (i, aw // COLT + j)),
                 out_shape=jax.ShapeDtypeStruct(mix.shape, BF16),
                 input_output_aliases={7: 0},
                 compiler_params=_params(("parallel", "parallel")))(z, z, z, z, z, z, conv_w, mix)


def conv_bwd(z, d_mix, conv_w, dz, aw, cw, b_off):
    s = z.shape[0]
    tm = _tile(s, 512)
    nseg = cw // COLT
    hb = tm // SUBLANES
    n_row = s // tm
    last_h = s // SUBLANES - 1
    n_step = nseg * n_row

    def body(b_ref, c_ref, h_ref, g_ref, cp_ref, hp_ref, bn_ref, gn_ref, dm_ref, dmn_ref, w_ref, dz_in,
             dz_ref, acc_ref, stash_ref, sems):
        j = pl.program_id(0)
        i = pl.program_id(1)
        step = j * n_row + i
        slot = step % 2

        def copies(from_slot, at_step):
            jj, ii = at_step // n_row, at_step % n_row
            rows = pl.ds(pl.multiple_of(ii * tm, tm), tm)
            return [pltpu.make_async_copy(
                stash_ref.at[from_slot, q],
                dz_ref.at[rows, pl.ds(pl.multiple_of((b_off + q * nseg + jj) * COLT, COLT), COLT)],
                sems.at[from_slot, q]) for q in range(4)]

        @pl.when(step >= 2)
        def _():
            for cp in copies(slot, step - 2):
                cp.wait()

        cc, hh, bb, g = c_ref[...], h_ref[...], b_ref[...], g_ref[...]
        w = w_ref[...]
        u = cc * hh
        up = jnp.where(i > 0, cp_ref[...] * hp_ref[...], 0.0)
        ext = jnp.concatenate([up, u], axis=0)
        um1 = pltpu.roll(ext, 1, 0)[SUBLANES:]
        um2 = pltpu.roll(ext, 2, 0)[SUBLANES:]
        cv = w[0:1] * um2 + w[1:2] * um1 + w[2:3] * u
        sig = _sigmoid(g)
        sg = g * sig
        dm = dm_ref[...]
        d_cv = (dm * bb) * sg
        gn = gn_ref[...]
        d_cv_next = jnp.where(i < n_row - 1, (dmn_ref[...] * bn_ref[...]) * (gn * _sigmoid(gn)), 0.0)
        ext2 = jnp.concatenate([d_cv, d_cv_next], axis=0)
        dp1 = pltpu.roll(ext2, tm + SUBLANES - 1, 0)[:tm]
        dp2 = pltpu.roll(ext2, tm + SUBLANES - 2, 0)[:tm]
        d_u = w[2:3] * d_cv + w[1:2] * dp1 + w[0:1] * dp2
        stash_ref[slot, 0] = ((dm * cv) * sg).astype(BF16)
        stash_ref[slot, 1] = (d_u * hh).astype(BF16)
        stash_ref[slot, 2] = (d_u * cc).astype(BF16)
        stash_ref[slot, 3] = (((dm * bb) * cv) * (sig * (1.0 + g * (1.0 - sig)))).astype(BF16)
        for cp in copies(slot, step):
            cp.start()

        @pl.when(i == 0)
        def _():
            acc_ref[...] = jnp.zeros_like(acc_ref)

        acc_ref[0:1, :] += jnp.sum(d_cv * um2, axis=0, keepdims=True)
        acc_ref[1:2, :] += jnp.sum(d_cv * um1, axis=0, keepdims=True)
        acc_ref[2:3, :] += jnp.sum(d_cv * u, axis=0, keepdims=True)

        @pl.when(step == n_step - 1)
        def _():
            if n_step >= 2:
                for cp in copies(1 - slot, step - 1):
                    cp.wait()
            for cp in copies(slot, step):
                cp.wait()

    seg = lambda q: pl.BlockSpec((tm, COLT), functools.partial(lambda j, i, q: (i, b_off + q * nseg + j), q=q))
    halo_p = lambda q: pl.BlockSpec(
        (SUBLANES, COLT),
        functools.partial(lambda j, i, q: (jnp.maximum(i * hb - 1, 0), b_off + q * nseg + j), q=q))
    halo_n = lambda q: pl.BlockSpec(
        (SUBLANES, COLT),
        functools.partial(lambda j, i, q: (jnp.minimum((i + 1) * hb, last_h), b_off + q * nseg + j), q=q))
    return _call(body, name="conv_bwd", grid=(nseg, n_row),
                 in_specs=[seg(0), seg(1), seg(2), seg(3), halo_p(1), halo_p(2), halo_n(0), halo_n(3),
                           pl.BlockSpec((tm, COLT), lambda j, i: (i, j)),
                           pl.BlockSpec((SUBLANES, COLT), lambda j, i: (jnp.minimum((i + 1) * hb, last_h), j)),
                           pl.BlockSpec((SUBLANES, COLT), lambda j, i: (0, j)), _hbm()],
                 out_specs=[_hbm(), pl.BlockSpec((SUBLANES, COLT), lambda j, i: (0, j))],
                 out_shape=[jax.ShapeDtypeStruct(dz.shape, BF16), jax.ShapeDtypeStruct((SUBLANES, cw), F32)],
                 input_output_aliases={11: 0},
                 scratch_shapes=[pltpu.VMEM((2, 4, tm, COLT), BF16), pltpu.SemaphoreType.DMA((2, 4))],
                 compiler_params=_params(("arbitrary", "arbitrary")))(
                     z, z, z, z, z, z, z, z, d_mix, d_mix, conv_w, dz)


def _place():
    x, y, c = lax.axis_index("x"), lax.axis_index("y"), lax.axis_index("c")
    chips = [(1 - x, y), (x, 1 - y), (1 - x, 1 - y)]
    return x, y, c, chips


def _remote(src, dst, send_sem, recv_sem, device):
    return pltpu.make_async_remote_copy(src_ref=src, dst_ref=dst, send_sem=send_sem, recv_sem=recv_sem,
                                        device_id=device, device_id_type=MESH)


def plan_gather_ici(n_split):
    def plan(refs, send, recv):
        x, y, c, chips = _place()
        me = 2 * x + y
        mine, theirs = [], []
        for w, ref in enumerate(refs):
            for j, (cx, cy) in enumerate(chips):
                def part(slot):
                    if w >= n_split:
                        return ref.at[slot]
                    hr = ref.shape[1] // 2
                    return ref.at[slot, pl.ds(c * hr, hr)]
                k = 3 * w + j
                mine.append(_remote(part(me), part(me), send.at[k], recv.at[k], (cx, cy, c)))
                theirs.append(_remote(part(2 * cx + cy), part(2 * cx + cy), send.at[k], recv.at[k], (cx, cy, c)))
        return mine, theirs
    return plan


def plan_shard_ici(j):
    def plan(refs, send, recv):
        _, _, c, chips = _place()
        own, land = refs
        hr = own.shape[0] // 2
        rows = pl.ds(c * hr, hr)
        cp = _remote(own.at[rows], land.at[rows], send.at[0], recv.at[0], (*chips[j], c))
        return [cp], [cp]
    return plan


def plan_shard_relay(refs, send, recv):
    _, _, c, chips = _place()
    land_x, land_y, land_far = refs
    hr = land_far.shape[0] // 2
    quarter = lambda q: pl.ds(c * hr + q * (hr // 2), hr // 2)
    mine = [_remote(land_y.at[quarter(0)], land_far.at[quarter(0)], send.at[0], recv.at[0], (*chips[0], c)),
            _remote(land_x.at[quarter(1)], land_far.at[quarter(1)], send.at[1], recv.at[1], (*chips[1], c))]
    return mine, mine


def plan_shard_pass(refs, send, recv):
    x, y, c, _ = _place()
    mine, theirs = [], []
    for w, land in enumerate(refs):
        hr = land.shape[0] // 2
        half = lambda core: land.at[pl.ds(core * hr, hr)]
        mine.append(_remote(half(c), half(c), send.at[w], recv.at[w], (x, y, 1 - c)))
        theirs.append(_remote(half(1 - c), half(1 - c), send.at[w], recv.at[w], (x, y, 1 - c)))
    return mine, theirs


def plan_gather_pass(refs, send, recv):
    x, y, c, chips = _place()
    mine, theirs = [], []
    for w, ref in enumerate(refs):
        hr = ref.shape[1] // 2
        for j, (cx, cy) in enumerate(chips):
            half = lambda core: ref.at[2 * cx + cy, pl.ds(core * hr, hr)]
            k = 3 * w + j
            mine.append(_remote(half(c), half(c), send.at[k], recv.at[k], (x, y, 1 - c)))
            theirs.append(_remote(half(1 - c), half(1 - c), send.at[k], recv.at[k], (x, y, 1 - c)))
    return mine, theirs


def plan_swap(refs, send, recv):
    x, y, c, _ = _place()
    nw = len(refs) // 2
    mine = []
    for w in range(nw):
        hr = refs[w].shape[1] // 2
        mine.append(_remote(refs[w].at[:, pl.ds((1 - c) * hr, hr), :], refs[nw + w], send.at[w], recv.at[w],
                            (x, y, 1 - c)))
    return mine, mine


def plan_scatter(refs, send, recv):
    x, y, c, chips = _place()
    me = 2 * x + y
    nw = len(refs) // 2
    mine, theirs = [], []
    for w in range(nw):
        for j, (cx, cy) in enumerate(chips):
            k = 3 * w + j
            mine.append(_remote(refs[w].at[2 * cx + cy], refs[nw + w].at[me], send.at[k], recv.at[k], (cx, cy, c)))
            theirs.append(_remote(refs[w].at[me], refs[nw + w].at[2 * cx + cy], send.at[k], recv.at[k], (cx, cy, c)))
    return mine, theirs


def plan_join(refs, send, recv):
    x, y, c, _ = _place()
    mine, theirs = [], []
    for w, ref in enumerate(refs):
        hr = ref.shape[0] // 2
        half = lambda core: ref.at[pl.ds(core * hr, hr)]
        mine.append(_remote(half(c), half(c), send.at[w], recv.at[w], (x, y, 1 - c)))
        theirs.append(_remote(half(1 - c), half(1 - c), send.at[w], recv.at[w], (x, y, 1 - c)))
    return mine, theirs


def exchange(name, plan, arrays, n, after=()):
    na = len(arrays)

    def body(*refs):
        mine, theirs = plan(refs[:na], refs[2 * na], refs[2 * na + 1])
        for cp in mine:
            cp.start()
        for cp in theirs:
            cp.wait_recv()
        for cp in mine:
            cp.wait_send()

    return _call_after(body, after, name=name, in_specs=[_hbm()] * na, out_specs=[_hbm()] * na,
                       out_shape=[jax.ShapeDtypeStruct(a.shape, a.dtype) for a in arrays],
                       input_output_aliases={i: i for i in range(na)},
                       scratch_shapes=[pltpu.SemaphoreType.DMA((n,)), pltpu.SemaphoreType.DMA((n,))])(*arrays)


def exchange_start(name, groups, arrays, after=()):
    na, ng = len(arrays), len(groups)

    def body(*refs):
        for g, (plan, idx, _) in enumerate(groups):
            mine, _ = plan([refs[i] for i in idx], refs[na + 2 * g], refs[na + 2 * g + 1])
            for cp in mine:
                cp.start()
        refs[-1][...] = jnp.zeros_like(refs[-1])

    hbm = pl.BlockSpec(memory_space=pltpu.HBM)
    sem = pl.BlockSpec(memory_space=pltpu.SEMAPHORE)
    sem_types = [pltpu.SemaphoreType.DMA((n,)) for _, _, n in groups for _ in (0, 1)]
    outs = _call_after(body, after, name=name, in_specs=[hbm] * na,
                       out_specs=[sem] * (2 * ng) + [hbm] * na + [pl.BlockSpec(memory_space=pltpu.VMEM)],
                       out_shape=sem_types + [pltpu.HBM(a.shape, a.dtype) for a in arrays]
                       + [jax.ShapeDtypeStruct((SUBLANES, LANES), F32)],
                       input_output_aliases={i: i + 2 * ng for i in range(na)},
                       compiler_params=pltpu.CompilerParams(
                           has_side_effects=pltpu.SideEffectType.DATAFLOW_SIDE_EFFECTING))(
                               *[pltpu.with_memory_space_constraint(a, pltpu.HBM) for a in arrays])
    sems = [(outs[2 * g], outs[2 * g + 1]) for g in range(ng)]
    return sems, list(outs[2 * ng:-1]), outs[-1]


def exchange_wait(name, plan, sems, arrays, after):
    send_sems, recv_sems = sems
    na = len(arrays)
    after = tuple(after) if isinstance(after, (tuple, list)) else (after,)

    def body(*refs):
        mine, theirs = plan(refs[:na], refs[na], refs[na + 1])
        for cp in mine:
            cp.wait_send()
        for cp in theirs:
            cp.wait_recv()

    hbm = pl.BlockSpec(memory_space=pltpu.HBM)
    sem = pl.BlockSpec(memory_space=pltpu.SEMAPHORE)
    return _call(body, name=name, in_specs=[hbm] * na + [sem, sem] + [_hbm()] * len(after),
                 out_specs=[hbm] * na, out_shape=[pltpu.HBM(a.shape, a.dtype) for a in arrays],
                 input_output_aliases={i: i for i in range(na)},
                 compiler_params=pltpu.CompilerParams(
                     has_side_effects=pltpu.SideEffectType.DATAFLOW_SIDE_EFFECTING))(
                         *arrays, send_sems, recv_sems, *after)


def plan_allgather_small(refs, send, recv):
    x, y, c, _ = _place()
    buf, = refs
    mine, theirs = [], []
    flips = [(fx, fy, fc) for fx in (0, 1) for fy in (0, 1) for fc in (0, 1)][1:]
    for k, (fx, fy, fc) in enumerate(flips):
        peer = (x ^ fx, y ^ fy, c ^ fc)
        slot = lambda px, py, pc: buf.at[4 * px + 2 * py + pc]
        mine.append(_remote(slot(x, y, c), slot(x, y, c), send.at[k], recv.at[k], peer))
        theirs.append(_remote(slot(*peer), slot(*peer), send.at[k], recv.at[k], peer))
    return mine, theirs


def add_sibling(grad, got, core, name):
    _, r, c = grad.shape
    hr = r // 2
    tr = _tile(hr, 512)
    nblk = hr // tr

    def body(core_ref, g_ref, o_ref, out_ref):
        out_ref[...] = (g_ref[...].astype(F32) + o_ref[...].astype(F32)).astype(BF16)

    grid_spec = pltpu.PrefetchScalarGridSpec(
        num_scalar_prefetch=1, grid=(N_CHIPS, nblk),
        in_specs=[pl.BlockSpec((None, tr, c), lambda t, i, core_ref: (t, core_ref[0] * nblk + i, 0)),
                  pl.BlockSpec((None, tr, c), lambda t, i, core_ref: (t, i, 0))],
        out_specs=pl.BlockSpec((None, tr, c), lambda t, i, core_ref: (t, i, 0)))
    return _call(body, name=name, grid_spec=grid_spec,
                 out_shape=jax.ShapeDtypeStruct((N_CHIPS, hr, c), BF16),
                 compiler_params=_params(("parallel", "parallel")))(core, grad, got)


def sum_chips(mine, owned, place, name):
    _, hr, c = mine.shape
    tr = _tile(hr, 512)
    nblk = hr // tr

    def body(place_ref, m_ref, o1_ref, o2_ref, o3_ref, out_ref):
        acc = m_ref[...].astype(F32)
        for o_ref in (o1_ref, o2_ref, o3_ref):
            acc = acc + o_ref[...].astype(F32)
        out_ref[...] = acc

    other = lambda k: pl.BlockSpec((None, tr, c), lambda i, place_ref: ((place_ref[0] + k) % N_CHIPS, i, 0))
    grid_spec = pltpu.PrefetchScalarGridSpec(
        num_scalar_prefetch=1, grid=(nblk,),
        in_specs=[other(0), other(1), other(2), other(3)],
        out_specs=pl.BlockSpec((tr, c), lambda i, place_ref: (place_ref[1] * nblk + i, 0)))
    return _call(body, name=name, grid_spec=grid_spec,
                 out_shape=jax.ShapeDtypeStruct((2 * hr, c), F32),
                 compiler_params=_params(("parallel",)))(place, mine, owned, owned, owned)


def sum_devices(gathered):
    _, rows, width = gathered.shape

    def body(g_ref, out_ref):
        acc = g_ref[0]
        for dev in range(1, 8):
            acc = acc + g_ref[dev]
        out_ref[...] = acc
        tail = acc[SUBLANES:]
        out_ref[SUBLANES:, :] = jnp.broadcast_to(jnp.sum(tail, axis=1, keepdims=True), tail.shape)

    return _call(body, name="sum_devices",
                 in_specs=[pl.BlockSpec(memory_space=pltpu.VMEM)],
                 out_specs=pl.BlockSpec(memory_space=pltpu.VMEM),
                 out_shape=jax.ShapeDtypeStruct((rows, width), F32))(gathered)


def _rope_tables(s):
    half = ROT_DIM // 2
    inv_freq = jnp.power(jnp.float32(ROPE_THETA), -jnp.arange(half, dtype=F32) * 2.0 / ROT_DIM)
    freq64 = jnp.concatenate([inv_freq, inv_freq, jnp.zeros((HEAD_DIM - ROT_DIM,), F32)])
    freq = jnp.concatenate([freq64, freq64])[None, :]
    dim = (jnp.arange(LANES) % HEAD_DIM)[None, :]
    ang = jnp.arange(s).astype(F32)[:, None] * freq
    cos, sin = jnp.cos(ang), jnp.sin(ang)
    return cos, jnp.where(dim < half, -sin, 0.0), jnp.where((dim >= half) & (dim < ROT_DIM), sin, 0.0)


def _pad_rows(a, rows):
    return jnp.pad(a, ((0, rows - a.shape[0]), (0, 0)))


def _pad_cols(a, cols):
    return jnp.pad(a, ((0, 0), (0, cols - a.shape[1])))


def kernel(x, p, norm_gain, w_in, q_norm_gain, k_norm_gain, attn_sinks, conv_w, w_out, ple_gate_norm_gain, w_ple_gate, b_ple_gate, w_ple_proj, ple_norm_gain, loss_target, m_norm_gain, m_w_in, m_q_norm_gain, m_k_norm_gain, m_attn_sinks, m_conv_w, m_w_out, m_ple_gate_norm_gain, m_w_ple_gate, m_b_ple_gate, m_w_ple_proj, m_ple_norm_gain, v_norm_gain, v_w_in, v_q_norm_gain, v_k_norm_gain, v_attn_sinks, v_conv_w, v_w_out, v_ple_gate_norm_gain, v_w_ple_gate, v_b_ple_gate, v_w_ple_proj, v_ple_norm_gain):
    x2, p2, tgt = x[0], p[0, 0], loss_target[0]
    s, d = x2.shape
    ple = p2.shape[1]
    aw = d // 2
    cw = d - aw
    nq = aw // HEAD_DIM
    sh = w_in.shape[2]
    in_w = N_CHIPS * sh
    dq = d // N_CHIPS
    cq = cw // N_CHIPS
    ga_off = (aw + 2 * KV_WIDTH) // COLT
    b_off = (2 * aw + 2 * KV_WIDTH) // COLT
    assert in_w == 2 * aw + 2 * KV_WIDTH + 4 * cw and aw % COLT == 0 and cw % COLT == 0
    assert s % BLOCK == 0 and sh % LANES == 0 and nq % (2 * N_KV_HEADS) == 0

    core = lax.axis_index("c").astype(jnp.int32).reshape(1)
    chip = 2 * lax.axis_index("x") + lax.axis_index("y")

    chip1 = chip.astype(jnp.int32).reshape(1)
    place = jnp.concatenate([chip1, core])
    w_in_own = cast_bf16(w_in[0], "cast_w_in")
    rest = [cast_into_slot(w_out[0], chip1, "cast_w_out"), cast_into_slot(w_ple_gate[0], chip1, "cast_w_pg"),
            cast_into_slot(w_ple_proj[0], chip1, "cast_w_pp"),
            lax.dynamic_update_slice(jnp.zeros((N_CHIPS, SUBLANES, cq), F32),
                                     _pad_rows(conv_w[0], SUBLANES)[None], (chip, 0, 0))]
    order = jnp.stack([chip, chip ^ 2, chip ^ 1, chip ^ 3]).astype(jnp.int32)
    wi_sems, wi_arrs, wi_token = exchange_start(
        "gather_wi_start", [(plan_shard_ici(j), [0, 1 + j], 1) for j in range(2)],
        [w_in_own] + [lax.empty((d, sh), BF16) for _ in range(3)])

    tn = _tile(d, 1024)
    ts = _tile(s, 2048)
    tabs = _rope_tables(s)
    qg = jnp.tile(q_norm_gain, (1, LANES // HEAD_DIM))
    kg = jnp.tile(k_norm_gain, (1, LANES // HEAD_DIM))
    seg_i = jnp.arange(SEG) // HEAD_DIM
    segb = (seg_i[:, None] == seg_i[None, :]).astype(BF16)
    h, z = norm_in_proj(x2, norm_gain, wi_arrs[0], order, in_w)
    own, land_x = exchange_wait("gather_wi_wait0", plan_shard_ici(0), wi_sems[0], [wi_arrs[0], wi_arrs[1]],
                                (z,) + tuple(tabs) + tuple(rest[:2]))
    own, land_y = exchange_wait("gather_wi_wait1", plan_shard_ici(1), wi_sems[1], [own, wi_arrs[2]], land_x)
    relay_sems, relayed, relay_token = exchange_start(
        "gather_wi_relay_start", [(plan_shard_relay, [0, 1, 2], 2)], [land_x, land_y, wi_arrs[3]])
    rest_sems, rest, rest_token = exchange_start(
        "gather_rest_start", [(plan_gather_ici(3), [0, 1, 2, 3], 12)], rest, after=(relay_token,))
    land_x, land_y = exchange("gather_wi_pass01", plan_shard_pass, relayed[:2], 2, after=(rest_token,))
    z = in_proj_part(h, land_x, z, order, 1, in_w)
    z = in_proj_part(h, land_y, z, order, 2, in_w)
    land_x, land_y, land_far = exchange_wait("gather_wi_relay_wait", plan_shard_relay, relay_sems[0],
                                             [land_x, land_y, relayed[2]], z)
    land_far, = exchange("gather_wi_pass2", plan_shard_pass, [land_far], 1)
    z = in_proj_part(h, land_far, z, order, 3, in_w)
    w_shards = [own, land_x, land_y, land_far]
    wo_all, wg_all, wp_all, conv_all = exchange_wait("gather_rest_wait", plan_gather_ici(3), rest_sems[0], rest, z)
    pass_sems, passed, pass_token = exchange_start(
        "gather_rest_pass_start", [(plan_gather_pass, [0, 1, 2], 9)], [wo_all, wg_all, wp_all])
    conv_full = conv_all.transpose(1, 0, 2).reshape(SUBLANES, cw)
    qs, ks, vb = qk_prep_fwd(z, tabs, qg, kg, segb, aw, after=(pass_token,))
    attn, mix = attn_fwd(qs, ks, vb, z, attn_sinks, aw, d, ga_off)
    mix = conv_fwd(z, conv_full, mix, aw, cw, b_off)
    wo_all, wg_all, wp_all = exchange_wait("gather_rest_pass_wait", plan_gather_pass, pass_sems[0], passed, mix)
    wo_full = wo_all.reshape(d, d)
    wg_full = wg_all.reshape(d, d)
    x1, hg = out_proj_norm(mix, wo_full, x2, ple_gate_norm_gain)
    pq = d // N_CHIPS

    d_gl, d_pe, dy, acc_head = head_fwd_bwd(x1, hg, wg_full, p2, wp_all, tgt, b_ple_gate, ple_norm_gain)
    wgrad = lambda a_cols, shape3, o_spec, b_cols, name, a, b, grid: matmul(
        a, b, grid=grid,
        a_spec=pl.BlockSpec((ts, a_cols), lambda i, j, k: (k, i)),
        b_spec=pl.BlockSpec((ts, b_cols), lambda i, j, k: (k, j)),
        o_spec=o_spec, out_shape=jax.ShapeDtypeStruct(shape3, BF16), dims=TN, name=name)
    g_wg = wgrad(tn, (d, d), pl.BlockSpec((tn, tn), lambda i, j, k: (i, j)), tn, "mm_g_wg", hg, d_gl,
                 (d // tn, d // tn, s // ts))
    g_wp = wgrad(ple, (N_CHIPS, ple, pq), pl.BlockSpec((None, ple, pq), lambda i, j, k: (j, 0, 0)), pq,
                 "mm_g_wp", p2, d_pe, (1, N_CHIPS, s // ts))
    d_x1, d_x1b, acc_g = gate_proj_bwd_norm(d_gl, wg_full, x1, dy, ple_gate_norm_gain)
    g_wo = wgrad(tn, (d, d), pl.BlockSpec((tn, tn), lambda i, j, k: (i, j)), tn, "mm_g_wo", mix, d_x1b,
                 (d // tn, d // tn, s // ts))
    def reduce_sum(tag, handle, after):
        sems, arrays, _ = handle
        n = len(arrays) // 2
        got = exchange_wait("scatter_%s_wait" % tag, plan_scatter, sems[0], arrays, after)
        return [sum_chips(pt, o, place, "sum_chips_%s%d" % (tag, i))
                for i, (pt, o) in enumerate(zip(got[:n], got[n:]))]

    early_grads = [g_wo.reshape(N_CHIPS, dq, d), g_wg.reshape(N_CHIPS, dq, d), g_wp]
    eswap_sems, eswapping, eswap_token = exchange_start(
        "swap_early_start", [(plan_swap, list(range(6)), 3)],
        early_grads + [lax.empty((N_CHIPS, a.shape[1] // 2, a.shape[2]), BF16) for a in early_grads])
    d_o, d_gate, d_mix_conv = out_proj_bwd_gate(d_x1b, wo_full, attn, z, aw, ga_off, after=(eswap_token,))
    eswapped = exchange_wait("swap_early_wait", plan_swap, eswap_sems[0], eswapping, d_o)
    early_parts = [add_sibling(g, o, core, "add_sibling_early%d" % i)
                   for i, (g, o) in enumerate(zip(eswapped[:3], eswapped[3:]))]
    early = exchange_start("scatter_early_start", [(plan_scatter, list(range(6)), 9)],
                           early_parts + [lax.empty(a.shape, BF16) for a in early_parts])
    dqs, dks, dvs, acc_sink = attn_bwd(qs, ks, vb, d_o, attn_sinks, aw, after=(early[-1],))
    dz, acc_qk = qk_prep_bwd(z, dqs, dks, dvs, d_gate, tabs, qg, kg, segb, aw)
    dz, acc_conv = conv_bwd(z, d_mix_conv, conv_full, dz, aw, cw, b_off)
    join_sems, joining, join_token = exchange_start(
        "join_early_start", [(plan_join, [0, 1, 2], 3)], reduce_sum("early", early, dz))
    g_send = in_proj_wgrad_half(h, dz, None, 1 - core, after=(join_token,))
    swap_sems, swapping, swap_token = exchange_start(
        "swap_late_start", [(plan_swap, [0, 1], 1)], [g_send, lax.empty((N_CHIPS, d // 2, sh), BF16)])
    g_wi = in_proj_wgrad_half(h, dz, swapping[0], core, after=(swap_token,))
    full_wo, full_wg, full_wp = exchange_wait("join_early_wait", plan_join, join_sems[0], joining, g_wi)
    g_wi, got_wi = exchange_wait("swap_late_wait", plan_swap, swap_sems[0], [g_wi, swapping[1]], full_wo)
    part_wi = add_sibling(g_wi, got_wi, core, "add_sibling_late0")
    late = exchange_start("scatter_late_start", [(plan_scatter, [0, 1], 3)],
                          [part_wi, lax.empty(part_wi.shape, BF16)])
    grad_x, acc_x = in_proj_bwd_norm(dz, w_shards, order, x2, d_x1, norm_gain, after=(late[-1],))

    wsm = max(d, cw)
    misc = jnp.concatenate([acc_qk[0:1, :HEAD_DIM], acc_qk[1:2, :HEAD_DIM], acc_sink[0:1, :nq]], axis=1)
    small = jnp.concatenate([
        _pad_cols(acc_x[0:1], wsm), _pad_cols(acc_g[0:1], wsm), _pad_cols(acc_head[0:1], wsm),
        _pad_cols(acc_head[1:2], wsm), _pad_cols(misc, wsm), _pad_cols(acc_conv[0:3], wsm),
        _pad_rows(_pad_cols(acc_head[2:3], wsm), SUBLANES)], axis=0)
    device = 2 * chip + lax.axis_index("c")
    small_sems, small_bufs, small_token = exchange_start(
        "small_start", [(plan_allgather_small, [0], 7)],
        [lax.dynamic_update_slice(jnp.zeros((8,) + small.shape, F32), small[None], (device, 0, 0))])
    last_sems, last_halves, last_token = exchange_start(
        "join_late_start", [(plan_join, [0], 1)], reduce_sum("late", late, small_token))
    big, after = {}, (last_token,)
    for nm, g, w, m, v in (("w_out", full_wo, w_out, m_w_out, v_w_out),
                           ("w_ple_gate", full_wg, w_ple_gate, m_w_ple_gate, v_w_ple_gate),
                           ("w_ple_proj", full_wp, w_ple_proj, m_w_ple_proj, v_w_ple_proj)):
        stepped = adamw(g, w[0], m[0], v[0], "adamw_" + nm, after=after)
        big[nm], after = [o[None] for o in stepped], (stepped[1],)
    full_wi, = exchange_wait("join_late_wait", plan_join, last_sems[0], last_halves, after[0])
    big["w_in"] = [o[None] for o in adamw(full_wi, w_in[0], m_w_in[0], v_w_in[0], "adamw_w_in")]

    gathered, = exchange_wait("small_wait", plan_allgather_small, small_sems[0], small_bufs, big["w_in"][1])
    tot = sum_devices(gathered)
    loss = tot[SUBLANES, 0]

    def pack(vals):
        ng, pg, bg, eg, qgv, kgv, sk, cv = vals
        misc_v = jnp.concatenate([qgv, kgv, sk], axis=1)
        return jnp.concatenate([_pad_cols(ng, wsm), _pad_cols(pg, wsm), _pad_cols(bg, wsm), _pad_cols(eg, wsm),
                                _pad_cols(misc_v, wsm), _pad_cols(cv[0], wsm)], axis=0)

    g_small = jnp.concatenate(
        [tot[0:5], _pad_cols(lax.dynamic_slice(tot[5:8], (0, chip * cq), (3, cq)), wsm)], axis=0)
    w_small = pack((norm_gain, ple_gate_norm_gain, b_ple_gate, ple_norm_gain, q_norm_gain, k_norm_gain,
                    attn_sinks, conv_w))
    m_small = pack((m_norm_gain, m_ple_gate_norm_gain, m_b_ple_gate, m_ple_norm_gain, m_q_norm_gain,
                    m_k_norm_gain, m_attn_sinks, m_conv_w))
    v_small = pack((v_norm_gain, v_ple_gate_norm_gain, v_b_ple_gate, v_ple_norm_gain, v_q_norm_gain,
                    v_k_norm_gain, v_attn_sinks, v_conv_w))
    sm = adamw(g_small, w_small, m_small, v_small, "adamw_small")

    def unpack(a):
        return {"norm_gain": a[0:1, :d], "ple_gate_norm_gain": a[1:2, :d], "b_ple_gate": a[2:3, :d],
                "ple_norm_gain": a[3:4, :d], "q_norm_gain": a[4:5, :HEAD_DIM],
                "k_norm_gain": a[4:5, HEAD_DIM:2 * HEAD_DIM],
                "attn_sinks": a[4:5, 2 * HEAD_DIM:2 * HEAD_DIM + nq], "conv_w": a[5:8, :cq][None]}

    order = ("norm_gain", "w_in", "q_norm_gain", "k_norm_gain", "attn_sinks", "conv_w", "w_out",
             "ple_gate_norm_gain", "w_ple_gate", "b_ple_gate", "w_ple_proj", "ple_norm_gain")
    outs = [loss, grad_x[None]]
    for kind in range(4):
        table = unpack(sm[kind])
        for nm in order:
            outs.append(big[nm][kind] if nm in big else table[nm])
    return tuple(outs)
```

```python
import functools

import jax
import jax.numpy as jnp
from jax import lax
from jax.experimental import pallas as pl
from jax.experimental.pallas import tpu as pltpu

F32 = jnp.float32
BF16 = jnp.bfloat16
MESH = pl.DeviceIdType.MESH

HEAD_DIM = 64
N_KV_HEADS = 4
KV_WIDTH = N_KV_HEADS * HEAD_DIM
BLOCK = 128
ROT_DIM = 16
ROPE_THETA = 500000.0
EPS = 1e-6
NEG_INF = -1e30
N_CHIPS = 4
LANES = 128
SUBLANES = 8
COLT = 512
SEG = 256
VMEM_LIMIT = 48 * 1024 * 1024
VMEM_LIMIT_BIG = 60 * 1024 * 1024

ADAM_LR = 0.001
ADAM_B1 = 0.9
ADAM_B2 = 0.999
ADAM_EPS = 1e-08
ADAM_WD = 0.01
ADAM_STEP = 10

NN = (((1,), (0,)), ((), ()))
NT = (((1,), (1,)), ((), ()))
TN = (((0,), (0,)), ((), ()))


def _call(body, **kw):
    return pl.pallas_call(body, **kw)


def _call_after(body, after, **kw):
    n_in, n_after = len(kw["in_specs"]), len(after)
    kw["in_specs"] = list(kw["in_specs"]) + [pl.BlockSpec(memory_space=pl.ANY)] * n_after

    def body_after(*refs):
        body(*refs[:n_in], *refs[n_in + n_after:])

    call = _call(body_after, **kw)
    return lambda *args: call(*args, *after)


def _params(sem, vmem=VMEM_LIMIT):
    return pltpu.CompilerParams(dimension_semantics=sem, vmem_limit_bytes=vmem)


def _tile(n, pref):
    return pref if n % pref == 0 else n


def _sigmoid(v):
    return 1.0 / (1.0 + jnp.exp(-v))


def _hbm():
    return pl.BlockSpec(memory_space=pl.ANY)


def cast_bf16(a, name):
    r, c = a.shape
    tr = _tile(r, 512)

    def body(a_ref, o_ref):
        o_ref[...] = a_ref[...].astype(BF16)

    return _call(body, name=name, grid=(r // tr,),
                 in_specs=[pl.BlockSpec((tr, c), lambda i: (i, 0))],
                 out_specs=pl.BlockSpec((tr, c), lambda i: (i, 0)),
                 out_shape=jax.ShapeDtypeStruct((r, c), BF16),
                 compiler_params=_params(("parallel",)))(a)


def cast_into_slot(a, chip, name):
    r, c = a.shape
    tr = _tile(r, 512)

    def body(chip_ref, a_ref, o_ref):
        o_ref[...] = a_ref[...].astype(BF16)

    grid_spec = pltpu.PrefetchScalarGridSpec(
        num_scalar_prefetch=1, grid=(r // tr,),
        in_specs=[pl.BlockSpec((tr, c), lambda i, chip_ref: (i, 0))],
        out_specs=pl.BlockSpec((None, tr, c), lambda i, chip_ref: (chip_ref[0], i, 0)))
    return _call(body, name=name, grid_spec=grid_spec,
                 out_shape=jax.ShapeDtypeStruct((N_CHIPS, r, c), BF16),
                 compiler_params=_params(("parallel",)))(chip, a)


def out_proj_norm(mix, wo, x, gain):
    s, d = x.shape
    tm = _tile(s, 512)

    def body(m_ref, w_ref, x_ref, g_ref, x1_ref, hg_ref):
        x1 = x_ref[...] + jnp.dot(m_ref[...], w_ref[...], preferred_element_type=F32)
        x1_ref[...] = x1
        r = lax.rsqrt(jnp.mean(x1 * x1, axis=-1, keepdims=True) + EPS)
        hg_ref[...] = ((x1 * r) * g_ref[...]).astype(BF16)

    row = pl.BlockSpec((tm, d), lambda i: (i, 0))
    return _call(body, name="out_proj_norm", grid=(s // tm,),
                 in_specs=[row, pl.BlockSpec((d, d), lambda i: (0, 0), pipeline_mode=pl.Buffered(1)), row,
                           pl.BlockSpec((1, d), lambda i: (0, 0))],
                 out_specs=[row, row],
                 out_shape=[jax.ShapeDtypeStruct((s, d), F32), jax.ShapeDtypeStruct((s, d), BF16)],
                 compiler_params=_params(("parallel",), VMEM_LIMIT_BIG))(mix, wo, x, gain)


def gate_proj_bwd_norm(d_gl, wg, xin, add, gain):
    s, d = xin.shape
    tm = _tile(s, 256)

    def body(dg_ref, w_ref, x_ref, a_ref, g_ref, dx_ref, dxb_ref, acc_ref):
        i = pl.program_id(0)
        dyv = lax.dot_general(dg_ref[...], w_ref[...], NT, preferred_element_type=F32)
        xf = x_ref[...]
        r = lax.rsqrt(jnp.mean(xf * xf, axis=-1, keepdims=True) + EPS)
        xhat = xf * r
        gd = dyv * g_ref[...]
        dx = a_ref[...] + r * (gd - xhat * jnp.mean(xhat * gd, axis=-1, keepdims=True))
        dx_ref[...] = dx
        dxb_ref[...] = dx.astype(BF16)

        @pl.when(i == 0)
        def _():
            acc_ref[...] = jnp.zeros_like(acc_ref)

        acc_ref[0:1, :] += jnp.sum(dyv * xhat, axis=0, keepdims=True)

    row = pl.BlockSpec((tm, d), lambda i: (i, 0))
    return _call(body, name="gate_proj_bwd_norm", grid=(s // tm,),
                 in_specs=[row, pl.BlockSpec((d, d), lambda i: (0, 0), pipeline_mode=pl.Buffered(1)), row, row,
                           pl.BlockSpec((1, d), lambda i: (0, 0))],
                 out_specs=[row, row, pl.BlockSpec((SUBLANES, d), lambda i: (0, 0))],
                 out_shape=[jax.ShapeDtypeStruct((s, d), F32), jax.ShapeDtypeStruct((s, d), BF16),
                            jax.ShapeDtypeStruct((SUBLANES, d), F32)],
                 compiler_params=_params(("arbitrary",), VMEM_LIMIT_BIG))(d_gl, wg, xin, add, gain)


def head_fwd_bwd(x1, hg, wg, p, wp, tgt, bias, ple_gain):
    s, d = x1.shape
    tm = _tile(s, 256)

    def body(x1_ref, hg_ref, wg_ref, p_ref, wp_ref, t_ref, b_ref, g_ref, dgl_ref, dpe_ref, dy_ref, acc_ref):
        i = pl.program_id(0)
        gl = jnp.dot(hg_ref[...], wg_ref[...], preferred_element_type=F32)
        pb = p_ref[...].astype(BF16)
        pev = jnp.concatenate([jnp.dot(pb, wp_ref[q], preferred_element_type=F32) for q in range(N_CHIPS)],
                              axis=1)
        r = lax.rsqrt(jnp.mean(pev * pev, axis=-1, keepdims=True) + EPS)
        pehat = pev * r
        gain = g_ref[...]
        e = pehat * gain
        gate = _sigmoid(gl + b_ref[...])
        diff = (x1_ref[...] + gate * e) - t_ref[...]
        dy = diff * (1.0 / d)
        dy_ref[...] = dy
        d_gl = (dy * e) * (gate * (1.0 - gate))
        dgl_ref[...] = d_gl.astype(BF16)
        d_e = dy * gate
        gd = d_e * gain
        d_pe = r * (gd - pehat * jnp.mean(pehat * gd, axis=-1, keepdims=True))
        dpe_ref[...] = d_pe.astype(BF16)

        @pl.when(i == 0)
        def _():
            acc_ref[...] = jnp.zeros_like(acc_ref)

        acc_ref[0:1, :] += jnp.sum(d_gl, axis=0, keepdims=True)
        acc_ref[1:2, :] += jnp.sum(d_e * pehat, axis=0, keepdims=True)
        acc_ref[2:3, :] += jnp.sum(diff * diff, axis=0, keepdims=True) * (0.5 / d)

    row = pl.BlockSpec((tm, d), lambda i: (i, 0))
    vec = pl.BlockSpec((1, d), lambda i: (0, 0))
    return _call(body, name="head_fwd_bwd", grid=(s // tm,),
                 in_specs=[row, row, pl.BlockSpec((d, d), lambda i: (0, 0), pipeline_mode=pl.Buffered(1)),
                           pl.BlockSpec((tm, p.shape[1]), lambda i: (i, 0)),
                           pl.BlockSpec(wp.shape, lambda i: (0, 0, 0)), row, vec, vec],
                 out_specs=[row, row, row, pl.BlockSpec((SUBLANES, d), lambda i: (0, 0))],
                 out_shape=[jax.ShapeDtypeStruct((s, d), BF16), jax.ShapeDtypeStruct((s, d), BF16),
                            jax.ShapeDtypeStruct((s, d), F32), jax.ShapeDtypeStruct((SUBLANES, d), F32)],
                 compiler_params=_params(("arbitrary",), VMEM_LIMIT_BIG))(
                     x1, hg, wg, p, wp, tgt, bias, ple_gain)


def adamw(g, w, m, v, name, after=()):
    r, c = g.shape
    tr = _tile(r, 256)

    def body(g_ref, w_ref, m_ref, v_ref, go_ref, d_ref, mo_ref, vo_ref):
        gv = g_ref[...]
        mn = ADAM_B1 * m_ref[...] + (1.0 - ADAM_B1) * gv
        vn = ADAM_B2 * v_ref[...] + (1.0 - ADAM_B2) * (gv * gv)
        m_hat = mn / (1.0 - ADAM_B1 ** ADAM_STEP)
        v_hat = vn / (1.0 - ADAM_B2 ** ADAM_STEP)
        go_ref[...] = gv
        d_ref[...] = -ADAM_LR * (m_hat / (jnp.sqrt(v_hat) + ADAM_EPS) + ADAM_WD * w_ref[...])
        mo_ref[...] = mn
        vo_ref[...] = vn

    blk = pl.BlockSpec((tr, c), lambda i: (i, 0))
    shp = jax.ShapeDtypeStruct((r, c), F32)
    return _call_after(body, after, name=name, grid=(r // tr,), in_specs=[blk] * 4, out_specs=[blk] * 4,
                       out_shape=[shp] * 4, compiler_params=_params(("parallel",)))(g, w, m, v)


def matmul(a, b, *, grid, a_spec, b_spec, o_spec, out_shape, dims, name):
    nk = grid[2]
    acc_shape = tuple(d for d in o_spec.block_shape if d is not None)

    def body(a_ref, b_ref, o_ref, *scratch):
        part = lax.dot_general(a_ref[...].astype(BF16), b_ref[...].astype(BF16), dims, preferred_element_type=F32)
        if nk == 1:
            o_ref[...] = part.astype(o_ref.dtype)
            return
        acc_ref, = scratch
        k = pl.program_id(2)

        @pl.when(k == 0)
        def _():
            acc_ref[...] = part

        @pl.when((k > 0) & (k < nk - 1))
        def _():
            acc_ref[...] += part

        @pl.when(k == nk - 1)
        def _():
            o_ref[...] = (acc_ref[...] + part).astype(o_ref.dtype)

    return _call(body, name=name, grid=grid, in_specs=[a_spec, b_spec], out_specs=o_spec, out_shape=out_shape,
                 scratch_shapes=[pltpu.VMEM(acc_shape, F32)] if nk > 1 else [],
                 compiler_params=_params(("parallel", "parallel", "arbitrary")))(a, b)


def norm_in_proj(x, gain, w, order, in_w):
    s, d = x.shape
    sh = w.shape[1]
    tm = _tile(s, 512)

    def body(order_ref, x_ref, g_ref, w_ref, h_ref, z_ref):
        xf = x_ref[...]
        r = lax.rsqrt(jnp.mean(xf * xf, axis=-1, keepdims=True) + EPS)
        hb = ((xf * r) * g_ref[...]).astype(BF16)
        h_ref[...] = hb
        z_ref[...] = jnp.dot(hb, w_ref[...], preferred_element_type=F32)

    grid_spec = pltpu.PrefetchScalarGridSpec(
        num_scalar_prefetch=1, grid=(s // tm,),
        in_specs=[pl.BlockSpec((tm, d), lambda i, order_ref: (i, 0)),
                  pl.BlockSpec((1, d), lambda i, order_ref: (0, 0)),
                  pl.BlockSpec((d, sh), lambda i, order_ref: (0, 0), pipeline_mode=pl.Buffered(1))],
        out_specs=[pl.BlockSpec((tm, d), lambda i, order_ref: (i, 0)),
                   pl.BlockSpec((tm, sh), lambda i, order_ref: (i, order_ref[0]))])
    return _call(body, name="norm_in_proj", grid_spec=grid_spec,
                 out_shape=[jax.ShapeDtypeStruct((s, d), BF16), jax.ShapeDtypeStruct((s, in_w), F32)],
                 compiler_params=_params(("parallel",)))(order, x, gain, w)


def in_proj_part(h, w, z_prev, order, q, in_w):
    s, d = h.shape
    sh = w.shape[1]
    tm = _tile(s, 512)

    def body(order_ref, h_ref, w_ref, *rest):
        rest[-1][...] = jnp.dot(h_ref[...], w_ref[...], preferred_element_type=F32)

    in_specs = [pl.BlockSpec((tm, d), lambda i, order_ref: (i, 0)),
                pl.BlockSpec((d, sh), lambda i, order_ref: (0, 0))]
    args = [order, h, w]
    if z_prev is not None:
        in_specs.append(_hbm())
        args.append(z_prev)
    grid_spec = pltpu.PrefetchScalarGridSpec(
        num_scalar_prefetch=1, grid=(s // tm,), in_specs=in_specs,
        out_specs=pl.BlockSpec((tm, sh), lambda i, order_ref: (i, order_ref[q])))
    return _call(body, name="mm_z%d" % q, grid_spec=grid_spec,
                 out_shape=jax.ShapeDtypeStruct((s, in_w), F32),
                 input_output_aliases={3: 0} if z_prev is not None else {},
                 compiler_params=_params(("parallel",)))(*args)


def in_proj_wgrad_half(h, dz, g_prev, half, after):
    s, d = h.shape
    sh = dz.shape[1] // N_CHIPS
    hr = d // 2

    def body(half_ref, h_ref, dz_ref, *rest):
        rest[-1][...] = lax.dot_general(h_ref[...], dz_ref[...], TN, preferred_element_type=F32).astype(BF16)

    extra = ([g_prev] if g_prev is not None else []) + list(after)
    grid_spec = pltpu.PrefetchScalarGridSpec(
        num_scalar_prefetch=1, grid=(N_CHIPS,),
        in_specs=[pl.BlockSpec((s, hr), lambda j, half_ref: (0, half_ref[0]), pipeline_mode=pl.Buffered(1)),
                  pl.BlockSpec((s, sh), lambda j, half_ref: (0, j))] + [_hbm()] * len(extra),
        out_specs=pl.BlockSpec((None, hr, sh), lambda j, half_ref: (j, half_ref[0], 0)))
    return _call(body, name="mm_g_wi_%s" % ("keep" if g_prev is not None else "send"), grid_spec=grid_spec,
                 out_shape=jax.ShapeDtypeStruct((N_CHIPS, d, sh), BF16),
                 input_output_aliases={3: 0} if g_prev is not None else {},
                 compiler_params=_params(("parallel",), VMEM_LIMIT_BIG))(half, h, dz, *extra)


def in_proj_bwd_norm(dz, ws, order, xin, add, gain, after):
    s, d = xin.shape
    sh = ws[0].shape[1]
    tm = _tile(s, 256)
    nq, n_after = len(ws), len(after)

    def body(order_ref, *refs):
        a_refs, b_refs = refs[:nq], refs[nq:2 * nq]
        x_ref, add_ref, g_ref = refs[2 * nq:2 * nq + 3]
        dx_ref, acc_ref = refs[2 * nq + 3 + n_after:]
        i = pl.program_id(0)
        dyv = lax.dot_general(a_refs[0][...], b_refs[0][...], NT, preferred_element_type=F32)
        for q in range(1, nq):
            dyv += lax.dot_general(a_refs[q][...], b_refs[q][...], NT, preferred_element_type=F32)
        xf = x_ref[...]
        r = lax.rsqrt(jnp.mean(xf * xf, axis=-1, keepdims=True) + EPS)
        xhat = xf * r
        gd = dyv * g_ref[...]
        dx_ref[...] = add_ref[...] + r * (gd - xhat * jnp.mean(xhat * gd, axis=-1, keepdims=True))

        @pl.when(i == 0)
        def _():
            acc_ref[...] = jnp.zeros_like(acc_ref)

        acc_ref[0:1, :] += jnp.sum(dyv * xhat, axis=0, keepdims=True)

    a_spec = lambda q: pl.BlockSpec((tm, sh), functools.partial(lambda i, order_ref, q: (i, order_ref[q]), q=q))
    row = pl.BlockSpec((tm, d), lambda i, order_ref: (i, 0))
    grid_spec = pltpu.PrefetchScalarGridSpec(
        num_scalar_prefetch=1, grid=(s // tm,),
        in_specs=[a_spec(q) for q in range(nq)]
        + [pl.BlockSpec((d, sh), lambda i, order_ref: (0, 0), pipeline_mode=pl.Buffered(1))] * nq
        + [row, row, pl.BlockSpec((1, d), lambda i, order_ref: (0, 0))] + [_hbm()] * n_after,
        out_specs=[row, pl.BlockSpec((SUBLANES, d), lambda i, order_ref: (0, 0))])
    return _call(body, name="in_proj_bwd_norm", grid_spec=grid_spec,
                 out_shape=[jax.ShapeDtypeStruct((s, d), F32), jax.ShapeDtypeStruct((SUBLANES, d), F32)],
                 compiler_params=_params(("arbitrary",), VMEM_LIMIT_BIG))(
                     order, *([dz] * nq), *ws, xin, add, gain, *after)


def _seg_mean(sq, segb):
    parts = []
    for cgrp in range(sq.shape[1] // SEG):
        blk = sq[:, cgrp * SEG:(cgrp + 1) * SEG]
        hi = blk.astype(BF16)
        r1 = blk - hi.astype(F32)
        mid = r1.astype(BF16)
        lo = (r1 - mid.astype(F32)).astype(BF16)
        acc = jnp.dot(hi, segb, preferred_element_type=F32)
        acc += jnp.dot(mid, segb, preferred_element_type=F32)
        acc += jnp.dot(lo, segb, preferred_element_type=F32)
        parts.append(acc)
    out = parts[0] if len(parts) == 1 else jnp.concatenate(parts, axis=1)
    return out * (1.0 / HEAD_DIM)


def _rope(v, cos, sa, sb):
    parts = []
    for cgrp in range(v.shape[1] // LANES):
        blk = v[:, cgrp * LANES:(cgrp + 1) * LANES]
        parts.append(blk * cos + pltpu.roll(blk, LANES - 8, 1) * sa + pltpu.roll(blk, 8, 1) * sb)
    return parts[0] if len(parts) == 1 else jnp.concatenate(parts, axis=1)


def _rope_t(dv, cos, sa, sb):
    parts = []
    for cgrp in range(dv.shape[1] // LANES):
        blk = dv[:, cgrp * LANES:(cgrp + 1) * LANES]
        parts.append(blk * cos + pltpu.roll(blk * sa, 8, 1) + pltpu.roll(blk * sb, LANES - 8, 1))
    return parts[0] if len(parts) == 1 else jnp.concatenate(parts, axis=1)


def _tile_lanes(vec, width):
    reps = width // LANES
    return vec if reps == 1 else jnp.tile(vec, (1, reps))


def qk_prep_fwd(z, tabs, qg, kg, segb, aw, after=()):
    s = z.shape[0]
    tm = _tile(s, 512)
    wq = aw + 2 * KV_WIDTH

    def body(z_ref, cos_ref, sa_ref, sb_ref, qg_ref, kg_ref, seg_ref, qs_ref, ks_ref, vb_ref):
        zz = z_ref[...]
        q, k, v = zz[:, :aw], zz[:, aw:aw + KV_WIDTH], zz[:, aw + KV_WIDTH:]
        cos, sa, sb, segm = cos_ref[...], sa_ref[...], sb_ref[...], seg_ref[...]
        rq = lax.rsqrt(_seg_mean(q * q, segm) + EPS)
        qn = (q * rq) * _tile_lanes(qg_ref[...], aw)
        qs_ref[...] = (_rope(qn, cos, sa, sb) * (HEAD_DIM ** -0.5)).astype(BF16)
        rk = lax.rsqrt(_seg_mean(k * k, segm) + EPS)
        kn = (k * rk) * _tile_lanes(kg_ref[...], KV_WIDTH)
        ks_ref[...] = _rope(kn, cos, sa, sb).astype(BF16)
        vb_ref[...] = v.astype(BF16)

    tab = pl.BlockSpec((tm, LANES), lambda i: (i, 0))
    vec = pl.BlockSpec((1, LANES), lambda i: (0, 0))
    return _call_after(body, after, name="qk_prep_fwd", grid=(s // tm,),
                       in_specs=[pl.BlockSpec((tm, wq), lambda i: (i, 0)), tab, tab, tab, vec, vec,
                                 pl.BlockSpec((SEG, SEG), lambda i: (0, 0))],
                       out_specs=[pl.BlockSpec((tm, aw), lambda i: (i, 0)),
                                  pl.BlockSpec((tm, KV_WIDTH), lambda i: (i, 0)),
                                  pl.BlockSpec((tm, KV_WIDTH), lambda i: (i, 0))],
                       out_shape=[jax.ShapeDtypeStruct((s, aw), BF16), jax.ShapeDtypeStruct((s, KV_WIDTH), BF16),
                                  jax.ShapeDtypeStruct((s, KV_WIDTH), BF16)],
                       compiler_params=_params(("parallel",)))(z, *tabs, qg, kg, segb)


def qk_prep_bwd(z, dqs, dks, dvs, d_gate, tabs, qg, kg, segb, aw):
    s, in_w = z.shape
    tm = _tile(s, 512)
    wq = aw + 2 * KV_WIDTH

    def body(z_ref, dq_ref, dk_ref, dv_ref, dga_ref, cos_ref, sa_ref, sb_ref, qg_ref, kg_ref, seg_ref,
             dz_ref, acc_ref):
        i = pl.program_id(0)
        zz = z_ref[...]
        q, k = zz[:, :aw], zz[:, aw:aw + KV_WIDTH]
        cos, sa, sb, segm = cos_ref[...], sa_ref[...], sb_ref[...], seg_ref[...]

        def one(xv, dout, gvec, width):
            g = _tile_lanes(gvec, width)
            r = lax.rsqrt(_seg_mean(xv * xv, segm) + EPS)
            xhat = xv * r
            dn = _rope_t(dout, cos, sa, sb)
            gd = dn * g
            dx = r * (gd - xhat * _seg_mean(xhat * gd, segm))
            contrib = jnp.sum(dn * xhat, axis=0, keepdims=True)
            folded = contrib[:, :LANES]
            for cgrp in range(1, width // LANES):
                folded = folded + contrib[:, cgrp * LANES:(cgrp + 1) * LANES]
            return dx, folded + pltpu.roll(folded, HEAD_DIM, 1)

        dq, gq = one(q, dq_ref[...] * (HEAD_DIM ** -0.5), qg_ref[...], aw)
        dk, gk = one(k, dk_ref[...], kg_ref[...], KV_WIDTH)
        dz_ref[:, :aw] = dq.astype(BF16)
        dz_ref[:, aw:aw + KV_WIDTH] = dk.astype(BF16)
        dz_ref[:, aw + KV_WIDTH:wq] = dv_ref[...].astype(BF16)
        dz_ref[:, wq:] = dga_ref[...]

        @pl.when(i == 0)
        def _():
            acc_ref[...] = jnp.zeros_like(acc_ref)

        acc_ref[0:1, :] += gq
        acc_ref[1:2, :] += gk

    tab = pl.BlockSpec((tm, LANES), lambda i: (i, 0))
    vec = pl.BlockSpec((1, LANES), lambda i: (0, 0))
    kvb = pl.BlockSpec((tm, KV_WIDTH), lambda i: (i, 0))
    awb = pl.BlockSpec((tm, aw), lambda i: (i, 0))
    return _call(body, name="qk_prep_bwd", grid=(s // tm,),
                 in_specs=[pl.BlockSpec((tm, wq), lambda i: (i, 0)), awb, kvb, kvb, awb, tab, tab, tab, vec, vec,
                           pl.BlockSpec((SEG, SEG), lambda i: (0, 0))],
                 out_specs=[pl.BlockSpec((tm, wq + aw), lambda i: (i, 0)),
                            pl.BlockSpec((SUBLANES, LANES), lambda i: (0, 0))],
                 out_shape=[jax.ShapeDtypeStruct((s, in_w), BF16), jax.ShapeDtypeStruct((SUBLANES, LANES), F32)],
                 compiler_params=_params(("arbitrary",)))(z, dqs, dks, dvs, d_gate, *tabs, qg, kg, segb)


def _placed(band, lane_idx):
    out = {}
    for kh in range(N_KV_HEADS):
        grp = band[:, (kh // 2) * LANES:(kh // 2 + 1) * LANES]
        for half in (0, 1):
            t = grp if half == kh % 2 else pltpu.roll(grp, HEAD_DIM, 1)
            keep = (lane_idx >= half * HEAD_DIM) & (lane_idx < (half + 1) * HEAD_DIM)
            out[kh, half] = jnp.where(keep, t, jnp.zeros_like(t))
    return out


def _softmax_with_sink(sc, valid, sink):
    sc = jnp.where(valid, sc, NEG_INF)
    m = jnp.maximum(jnp.max(sc, axis=-1, keepdims=True), sink)
    e = jnp.exp(sc - m)
    es = jnp.exp(sink - m)
    den = jnp.sum(e, axis=-1, keepdims=True) + es
    return e / den, es / den


def _stack_halves(placed, kh):
    return jnp.concatenate([placed[kh, 0], placed[kh, 1]], axis=0)


def _valid_mask(n):
    r_i = lax.broadcasted_iota(jnp.int32, (BLOCK, 2 * BLOCK), 0)
    j_i = lax.broadcasted_iota(jnp.int32, (BLOCK, 2 * BLOCK), 1)
    return (j_i > r_i) & (j_i <= r_i + BLOCK) & ((n > 0) | (j_i >= BLOCK))


def attn_fwd(qs, ks, vb, z, sinks, aw, d, ga_off):
    s = qs.shape[0]
    nb = s // BLOCK
    nq = aw // HEAD_DIM
    grp_sz = nq // N_KV_HEADS
    n_ga = aw // COLT

    def body(q_ref, ko_ref, kp_ref, vo_ref, vp_ref, *rest):
        ga_refs = rest[:n_ga]
        sink_ref, attn_ref, mix_ref = rest[n_ga:]
        n = pl.program_id(0)
        lane_idx = lax.broadcasted_iota(jnp.int32, (2 * BLOCK, LANES), 1)
        kpl = _placed(jnp.concatenate([kp_ref[...], ko_ref[...]], axis=0), lane_idx)
        vpl = _placed(jnp.concatenate([vp_ref[...], vo_ref[...]], axis=0), lane_idx)
        valid = _valid_mask(n)
        groups = range(nq // 2)
        kcat = [_stack_halves(kpl, kh) for kh in range(N_KV_HEADS)]
        vcat = [_stack_halves(vpl, kh) for kh in range(N_KV_HEADS)]
        scs = [lax.dot_general(q_ref[:, c * LANES:(c + 1) * LANES], kcat[2 * c // grp_sz], NT,
                               preferred_element_type=F32) for c in groups]
        probs = [jnp.concatenate(
            [_softmax_with_sink(scs[c][:, half * 2 * BLOCK:(half + 1) * 2 * BLOCK], valid,
                                sink_ref[0, 2 * c + half])[0].astype(BF16) for half in (0, 1)], axis=1)
                 for c in groups]
        for c in groups:
            acc = jnp.dot(probs[c], vcat[2 * c // grp_sz], preferred_element_type=F32)
            attn_ref[:, c * LANES:(c + 1) * LANES] = acc
            col = c * LANES
            ga = ga_refs[col // COLT][:, col % COLT:col % COLT + LANES]
            mix_ref[:, c * LANES:(c + 1) * LANES] = (acc * (ga * _sigmoid(ga))).astype(BF16)

    own = lambda n: (n, 0)
    prev = lambda n: (jnp.maximum(n - 1, 0), 0)
    kvs = lambda imap: pl.BlockSpec((BLOCK, KV_WIDTH), imap)
    ga_specs = [pl.BlockSpec((BLOCK, COLT), functools.partial(lambda n, j: (n, ga_off + j), j=j))
                for j in range(n_ga)]
    return _call(body, name="attn_fwd", grid=(nb,),
                 in_specs=[pl.BlockSpec((BLOCK, aw), own), kvs(own), kvs(prev), kvs(own), kvs(prev)]
                 + ga_specs + [pl.BlockSpec(memory_space=pltpu.SMEM)],
                 out_specs=[pl.BlockSpec((BLOCK, aw), own), pl.BlockSpec((BLOCK, aw), own)],
                 out_shape=[jax.ShapeDtypeStruct((s, aw), F32), jax.ShapeDtypeStruct((s, d), BF16)],
                 compiler_params=_params(("parallel",)))(qs, ks, ks, vb, vb, *([z] * n_ga), sinks)


def out_proj_bwd_gate(d_x1b, wo, attn, z, aw, ga_off, after=()):
    s, d = d_x1b.shape
    tm = _tile(s, 512)
    n_ga = aw // COLT

    def body(dx_ref, w_ref, at_ref, *rest):
        ga_refs, (do_ref, dga_ref, dmc_ref) = rest[:n_ga], rest[n_ga:]
        dm = lax.dot_general(dx_ref[...], w_ref[...], NT, preferred_element_type=F32)
        dmc_ref[...] = dm[:, aw:]
        for j in range(n_ga):
            cols = slice(j * COLT, (j + 1) * COLT)
            ga = ga_refs[j][...]
            sig = _sigmoid(ga)
            dma = dm[:, cols]
            do_ref[:, cols] = (dma * (ga * sig)).astype(BF16)
            dga_ref[:, cols] = ((dma * at_ref[:, cols]) * (sig * (1.0 + ga * (1.0 - sig)))).astype(BF16)

    row = lambda width: pl.BlockSpec((tm, width), lambda i: (i, 0))
    ga_specs = [pl.BlockSpec((tm, COLT), functools.partial(lambda i, j: (i, ga_off + j), j=j)) for j in range(n_ga)]
    return _call_after(body, after, name="out_proj_bwd_gate", grid=(s // tm,),
                       in_specs=[row(d), pl.BlockSpec((d, d), lambda i: (0, 0), pipeline_mode=pl.Buffered(1)),
                                 row(aw)] + ga_specs,
                       out_specs=[row(aw), row(aw), row(d - aw)],
                       out_shape=[jax.ShapeDtypeStruct((s, aw), BF16), jax.ShapeDtypeStruct((s, aw), BF16),
                                  jax.ShapeDtypeStruct((s, d - aw), F32)],
                       compiler_params=_params(("parallel",)))(d_x1b, wo, attn, *([z] * n_ga))


def attn_bwd(qs, ks, vb, d_o, sinks, aw, after=()):
    s = qs.shape[0]
    nb = s // BLOCK
    nq = aw // HEAD_DIM
    grp_sz = nq // N_KV_HEADS

    def body(q_ref, ko_ref, kp_ref, vo_ref, vp_ref, do_ref, sink_ref, dq_ref, dk_ref, dv_ref, ds_ref,
             ck_ref, cv_ref):
        n = pl.program_id(0)

        @pl.when(n == 0)
        def _():
            ds_ref[...] = jnp.zeros_like(ds_ref)
            dk_ref[...] = jnp.zeros_like(dk_ref)
            dv_ref[...] = jnp.zeros_like(dv_ref)

        @pl.when(n < nb)
        def _():
            lane_idx = lax.broadcasted_iota(jnp.int32, (2 * BLOCK, LANES), 1)
            lane_row = lax.broadcasted_iota(jnp.int32, (1, LANES), 1)
            kpl = _placed(jnp.concatenate([kp_ref[...], ko_ref[...]], axis=0), lane_idx)
            vpl = _placed(jnp.concatenate([vp_ref[...], vo_ref[...]], axis=0), lane_idx)
            valid = _valid_mask(n)
            ds_row = jnp.zeros((1, LANES), F32)
            wide = 2 * BLOCK
            groups = range(nq // 2)
            per_kv = grp_sz // 2
            kcat = [_stack_halves(kpl, kh) for kh in range(N_KV_HEADS)]
            vcat = [_stack_halves(vpl, kh) for kh in range(N_KV_HEADS)]
            qg = [q_ref[:, c * LANES:(c + 1) * LANES] for c in groups]
            dog = [do_ref[:, c * LANES:(c + 1) * LANES] for c in groups]
            scs = [lax.dot_general(qg[c], kcat[c // per_kv], NT, preferred_element_type=F32) for c in groups]
            dps = [lax.dot_general(dog[c], vcat[c // per_kv], NT, preferred_element_type=F32) for c in groups]
            ds_pairs, p_pairs = [], []
            for c in groups:
                probs, dss = [], []
                for half in (0, 1):
                    h = 2 * c + half
                    cols = slice(half * wide, (half + 1) * wide)
                    p, p_sink = _softmax_with_sink(scs[c][:, cols], valid, sink_ref[0, h])
                    delta = jnp.sum(p * dps[c][:, cols], axis=-1, keepdims=True)
                    dss.append((p * (dps[c][:, cols] - delta)).astype(BF16))
                    probs.append(p.astype(BF16))
                    dsink = -jnp.sum(p_sink * delta, axis=0, keepdims=True)
                    ds_row += jnp.where(lane_row == h, dsink, 0.0)
                ds_pairs.append(jnp.concatenate(dss, axis=1))
                p_pairs.append(jnp.concatenate(probs, axis=1))
            for c in groups:
                dq_ref[:, c * LANES:(c + 1) * LANES] = jnp.dot(ds_pairs[c], kcat[c // per_kv],
                                                               preferred_element_type=F32)
            dkts = [lax.dot_general(qg[c], ds_pairs[c], TN, preferred_element_type=F32) for c in groups]
            dvts = [lax.dot_general(dog[c], p_pairs[c], TN, preferred_element_type=F32) for c in groups]
            dkt, dvt = [], []
            for kh in range(N_KV_HEADS):
                for parts, out in ((dkts, dkt), (dvts, dvt)):
                    tot = parts[kh * per_kv]
                    for c in range(kh * per_kv + 1, (kh + 1) * per_kv):
                        tot = tot + parts[c]
                    out.append(tot[:HEAD_DIM, :wide] + tot[HEAD_DIM:, wide:])
            ds_ref[0:1, :] += ds_row
            back = lambda t: jnp.concatenate(
                [jnp.concatenate(t[2 * g:2 * g + 2], axis=0).T for g in range(N_KV_HEADS // 2)], axis=1)
            dk_band, dv_band = back(dkt), back(dvt)

            @pl.when(n > 0)
            def _():
                dk_ref[...] = ck_ref[...] + dk_band[:BLOCK]
                dv_ref[...] = cv_ref[...] + dv_band[:BLOCK]

            ck_ref[...] = dk_band[BLOCK:]
            cv_ref[...] = dv_band[BLOCK:]

        @pl.when(n == nb)
        def _():
            dk_ref[...] = ck_ref[...]
            dv_ref[...] = cv_ref[...]

    own = lambda n: (jnp.minimum(n, nb - 1), 0)
    prev = lambda n: (jnp.clip(n - 1, 0, nb - 1), 0)
    done = lambda n: (jnp.maximum(n - 1, 0), 0)
    kvs = lambda imap: pl.BlockSpec((BLOCK, KV_WIDTH), imap)
    return _call_after(body, after, name="attn_bwd", grid=(nb + 1,),
                       in_specs=[pl.BlockSpec((BLOCK, aw), own), kvs(own), kvs(prev), kvs(own), kvs(prev),
                                 pl.BlockSpec((BLOCK, aw), own), pl.BlockSpec(memory_space=pltpu.SMEM)],
                       out_specs=[pl.BlockSpec((BLOCK, aw), own), kvs(done), kvs(done),
                                  pl.BlockSpec((SUBLANES, LANES), lambda n: (0, 0))],
                       out_shape=[jax.ShapeDtypeStruct((s, aw), F32), jax.ShapeDtypeStruct((s, KV_WIDTH), F32),
                                  jax.ShapeDtypeStruct((s, KV_WIDTH), F32),
                                  jax.ShapeDtypeStruct((SUBLANES, LANES), F32)],
                       scratch_shapes=[pltpu.VMEM((BLOCK, KV_WIDTH), F32), pltpu.VMEM((BLOCK, KV_WIDTH), F32)],
                       compiler_params=_params(("arbitrary",)))(qs, ks, ks, vb, vb, d_o, sinks)


def conv_fwd(z, conv_w, mix, aw, cw, b_off):
    s = z.shape[0]
    tm = _tile(s, 1024)
    nseg = cw // COLT
    hb = tm // SUBLANES

    def body(b_ref, c_ref, h_ref, g_ref, cp_ref, hp_ref, w_ref, mix_in, mix_ref):
        i = pl.program_id(0)
        u = c_ref[...] * h_ref[...]
        up = jnp.where(i > 0, cp_ref[...] * hp_ref[...], 0.0)
        ext = jnp.concatenate([up, u], axis=0)
        um1 = pltpu.roll(ext, 1, 0)[SUBLANES:]
        um2 = pltpu.roll(ext, 2, 0)[SUBLANES:]
        w = w_ref[...]
        cv = w[0:1] * um2 + w[1:2] * um1 + w[2:3] * u
        g = g_ref[...]
        mix_ref[...] = ((b_ref[...] * cv) * (g * _sigmoid(g))).astype(BF16)

    seg = lambda k: pl.BlockSpec((tm, COLT), functools.partial(lambda i, j, k: (i, b_off + k * nseg + j), k=k))
    halo = lambda k: pl.BlockSpec(
        (SUBLANES, COLT), functools.partial(lambda i, j, k: (jnp.maximum(i * hb - 1, 0), b_off + k * nseg + j), k=k))
    return _call(body, name="conv_fwd", grid=(s // tm, nseg),
                 in_specs=[seg(0), seg(1), seg(2), seg(3), halo(1), halo(2),
                           pl.BlockSpec((SUBLANES, COLT), lambda i, j: (0, j)), _hbm()],
                 out_specs=pl.BlockSpec((tm, COLT), lambda i, j: (i, aw // COLT + j)),
                 out_shape=jax.ShapeDtypeStruct(mix.shape, BF16),
                 input_output_aliases={7: 0},
                 compiler_params=_params(("parallel", "parallel")))(z, z, z, z, z, z, conv_w, mix)


def conv_bwd(z, d_mix, conv_w, dz, aw, cw, b_off):
    s = z.shape[0]
    tm = _tile(s, 512)
    nseg = cw // COLT
    hb = tm // SUBLANES
    n_row = s // tm
    last_h = s // SUBLANES - 1
    n_step = nseg * n_row

    def body(b_ref, c_ref, h_ref, g_ref, cp_ref, hp_ref, bn_ref, gn_ref, dm_ref, dmn_ref, w_ref, dz_in,
             dz_ref, acc_ref, stash_ref, sems):
        j = pl.program_id(0)
        i = pl.program_id(1)
        step = j * n_row + i
        slot = step % 2

        def copies(from_slot, at_step):
            jj, ii = at_step // n_row, at_step % n_row
            rows = pl.ds(pl.multiple_of(ii * tm, tm), tm)
            return [pltpu.make_async_copy(
                stash_ref.at[from_slot, q],
                dz_ref.at[rows, pl.ds(pl.multiple_of((b_off + q * nseg + jj) * COLT, COLT), COLT)],
                sems.at[from_slot, q]) for q in range(4)]

        @pl.when(step >= 2)
        def _():
            for cp in copies(slot, step - 2):
                cp.wait()

        cc, hh, bb, g = c_ref[...], h_ref[...], b_ref[...], g_ref[...]
        w = w_ref[...]
        u = cc * hh
        up = jnp.where(i > 0, cp_ref[...] * hp_ref[...], 0.0)
        ext = jnp.concatenate([up, u], axis=0)
        um1 = pltpu.roll(ext, 1, 0)[SUBLANES:]
        um2 = pltpu.roll(ext, 2, 0)[SUBLANES:]
        cv = w[0:1] * um2 + w[1:2] * um1 + w[2:3] * u
        sig = _sigmoid(g)
        sg = g * sig
        dm = dm_ref[...]
        d_cv = (dm * bb) * sg
        gn = gn_ref[...]
        d_cv_next = jnp.where(i < n_row - 1, (dmn_ref[...] * bn_ref[...]) * (gn * _sigmoid(gn)), 0.0)
        ext2 = jnp.concatenate([d_cv, d_cv_next], axis=0)
        dp1 = pltpu.roll(ext2, tm + SUBLANES - 1, 0)[:tm]
        dp2 = pltpu.roll(ext2, tm + SUBLANES - 2, 0)[:tm]
        d_u = w[2:3] * d_cv + w[1:2] * dp1 + w[0:1] * dp2
        stash_ref[slot, 0] = ((dm * cv) * sg).astype(BF16)
        stash_ref[slot, 1] = (d_u * hh).astype(BF16)
        stash_ref[slot, 2] = (d_u * cc).astype(BF16)
        stash_ref[slot, 3] = (((dm * bb) * cv) * (sig * (1.0 + g * (1.0 - sig)))).astype(BF16)
        for cp in copies(slot, step):
            cp.start()

        @pl.when(i == 0)
        def _():
            acc_ref[...] = jnp.zeros_like(acc_ref)

        acc_ref[0:1, :] += jnp.sum(d_cv * um2, axis=0, keepdims=True)
        acc_ref[1:2, :] += jnp.sum(d_cv * um1, axis=0, keepdims=True)
        acc_ref[2:3, :] += jnp.sum(d_cv * u, axis=0, keepdims=True)

        @pl.when(step == n_step - 1)
        def _():
            if n_step >= 2:
                for cp in copies(1 - slot, step - 1):
                    cp.wait()
            for cp in copies(slot, step):
                cp.wait()

    seg = lambda q: pl.BlockSpec((tm, COLT), functools.partial(lambda j, i, q: (i, b_off + q * nseg + j), q=q))
    halo_p = lambda q: pl.BlockSpec(
        (SUBLANES, COLT),
        functools.partial(lambda j, i, q: (jnp.maximum(i * hb - 1, 0), b_off + q * nseg + j), q=q))
    halo_n = lambda q: pl.BlockSpec(
        (SUBLANES, COLT),
        functools.partial(lambda j, i, q: (jnp.minimum((i + 1) * hb, last_h), b_off + q * nseg + j), q=q))
    return _call(body, name="conv_bwd", grid=(nseg, n_row),
                 in_specs=[seg(0), seg(1), seg(2), seg(3), halo_p(1), halo_p(2), halo_n(0), halo_n(3),
                           pl.BlockSpec((tm, COLT), lambda j, i: (i, j)),
                           pl.BlockSpec((SUBLANES, COLT), lambda j, i: (jnp.minimum((i + 1) * hb, last_h), j)),
                           pl.BlockSpec((SUBLANES, COLT), lambda j, i: (0, j)), _hbm()],
                 out_specs=[_hbm(), pl.BlockSpec((SUBLANES, COLT), lambda j, i: (0, j))],
                 out_shape=[jax.ShapeDtypeStruct(dz.shape, BF16), jax.ShapeDtypeStruct((SUBLANES, cw), F32)],
                 input_output_aliases={11: 0},
                 scratch_shapes=[pltpu.VMEM((2, 4, tm, COLT), BF16), pltpu.SemaphoreType.DMA((2, 4))],
                 compiler_params=_params(("arbitrary", "arbitrary")))(
                     z, z, z, z, z, z, z, z, d_mix, d_mix, conv_w, dz)


def _place():
    x, y, c = lax.axis_index("x"), lax.axis_index("y"), lax.axis_index("c")
    chips = [(1 - x, y), (x, 1 - y), (1 - x, 1 - y)]
    return x, y, c, chips


def _remote(src, dst, send_sem, recv_sem, device):
    return pltpu.make_async_remote_copy(src_ref=src, dst_ref=dst, send_sem=send_sem, recv_sem=recv_sem,
                                        device_id=device, device_id_type=MESH)


def plan_gather_ici(n_split):
    def plan(refs, send, recv):
        x, y, c, chips = _place()
        me = 2 * x + y
        mine, theirs = [], []
        for w, ref in enumerate(refs):
            for j, (cx, cy) in enumerate(chips):
                def part(slot):
                    if w >= n_split:
                        return ref.at[slot]
                    hr = ref.shape[1] // 2
                    return ref.at[slot, pl.ds(c * hr, hr)]
                k = 3 * w + j
                mine.append(_remote(part(me), part(me), send.at[k], recv.at[k], (cx, cy, c)))
                theirs.append(_remote(part(2 * cx + cy), part(2 * cx + cy), send.at[k], recv.at[k], (cx, cy, c)))
        return mine, theirs
    return plan


def plan_shard_ici(j):
    def plan(refs, send, recv):
        _, _, c, chips = _place()
        own, land = refs
        hr = own.shape[0] // 2
        rows = pl.ds(c * hr, hr)
        cp = _remote(own.at[rows], land.at[rows], send.at[0], recv.at[0], (*chips[j], c))
        return [cp], [cp]
    return plan


def plan_shard_relay(refs, send, recv):
    _, _, c, chips = _place()
    land_x, land_y, land_far = refs
    hr = land_far.shape[0] // 2
    quarter = lambda q: pl.ds(c * hr + q * (hr // 2), hr // 2)
    mine = [_remote(land_y.at[quarter(0)], land_far.at[quarter(0)], send.at[0], recv.at[0], (*chips[0], c)),
            _remote(land_x.at[quarter(1)], land_far.at[quarter(1)], send.at[1], recv.at[1], (*chips[1], c))]
    return mine, mine


def plan_shard_pass(refs, send, recv):
    x, y, c, _ = _place()
    mine, theirs = [], []
    for w, land in enumerate(refs):
        hr = land.shape[0] // 2
        half = lambda core: land.at[pl.ds(core * hr, hr)]
        mine.append(_remote(half(c), half(c), send.at[w], recv.at[w], (x, y, 1 - c)))
        theirs.append(_remote(half(1 - c), half(1 - c), send.at[w], recv.at[w], (x, y, 1 - c)))
    return mine, theirs


def plan_gather_pass(refs, send, recv):
    x, y, c, chips = _place()
    mine, theirs = [], []
    for w, ref in enumerate(refs):
        hr = ref.shape[1] // 2
        for j, (cx, cy) in enumerate(chips):
            half = lambda core: ref.at[2 * cx + cy, pl.ds(core * hr, hr)]
            k = 3 * w + j
            mine.append(_remote(half(c), half(c), send.at[k], recv.at[k], (x, y, 1 - c)))
            theirs.append(_remote(half(1 - c), half(1 - c), send.at[k], recv.at[k], (x, y, 1 - c)))
    return mine, theirs


def plan_swap(refs, send, recv):
    x, y, c, _ = _place()
    nw = len(refs) // 2
    mine = []
    for w in range(nw):
        hr = refs[w].shape[1] // 2
        mine.append(_remote(refs[w].at[:, pl.ds((1 - c) * hr, hr), :], refs[nw + w], send.at[w], recv.at[w],
                            (x, y, 1 - c)))
    return mine, mine


def plan_scatter(refs, send, recv):
    x, y, c, chips = _place()
    me = 2 * x + y
    nw = len(refs) // 2
    mine, theirs = [], []
    for w in range(nw):
        for j, (cx, cy) in enumerate(chips):
            k = 3 * w + j
            mine.append(_remote(refs[w].at[2 * cx + cy], refs[nw + w].at[me], send.at[k], recv.at[k], (cx, cy, c)))
            theirs.append(_remote(refs[w].at[me], refs[nw + w].at[2 * cx + cy], send.at[k], recv.at[k], (cx, cy, c)))
    return mine, theirs


def plan_join(refs, send, recv):
    x, y, c, _ = _place()
    mine, theirs = [], []
    for w, ref in enumerate(refs):
        hr = ref.shape[0] // 2
        half = lambda core: ref.at[pl.ds(core * hr, hr)]
        mine.append(_remote(half(c), half(c), send.at[w], recv.at[w], (x, y, 1 - c)))
        theirs.append(_remote(half(1 - c), half(1 - c), send.at[w], recv.at[w], (x, y, 1 - c)))
    return mine, theirs


def exchange(name, plan, arrays, n, after=()):
    na = len(arrays)

    def body(*refs):
        mine, theirs = plan(refs[:na], refs[2 * na], refs[2 * na + 1])
        for cp in mine:
            cp.start()
        for cp in theirs:
            cp.wait_recv()
        for cp in mine:
            cp.wait_send()

    return _call_after(body, after, name=name, in_specs=[_hbm()] * na, out_specs=[_hbm()] * na,
                       out_shape=[jax.ShapeDtypeStruct(a.shape, a.dtype) for a in arrays],
                       input_output_aliases={i: i for i in range(na)},
                       scratch_shapes=[pltpu.SemaphoreType.DMA((n,)), pltpu.SemaphoreType.DMA((n,))])(*arrays)


def exchange_start(name, groups, arrays, after=()):
    na, ng = len(arrays), len(groups)

    def body(*refs):
        for g, (plan, idx, _) in enumerate(groups):
            mine, _ = plan([refs[i] for i in idx], refs[na + 2 * g], refs[na + 2 * g + 1])
            for cp in mine:
                cp.start()
        refs[-1][...] = jnp.zeros_like(refs[-1])

    hbm = pl.BlockSpec(memory_space=pltpu.HBM)
    sem = pl.BlockSpec(memory_space=pltpu.SEMAPHORE)
    sem_types = [pltpu.SemaphoreType.DMA((n,)) for _, _, n in groups for _ in (0, 1)]
    outs = _call_after(body, after, name=name, in_specs=[hbm] * na,
                       out_specs=[sem] * (2 * ng) + [hbm] * na + [pl.BlockSpec(memory_space=pltpu.VMEM)],
                       out_shape=sem_types + [pltpu.HBM(a.shape, a.dtype) for a in arrays]
                       + [jax.ShapeDtypeStruct((SUBLANES, LANES), F32)],
                       input_output_aliases={i: i + 2 * ng for i in range(na)},
                       compiler_params=pltpu.CompilerParams(
                           has_side_effects=pltpu.SideEffectType.DATAFLOW_SIDE_EFFECTING))(
                               *[pltpu.with_memory_space_constraint(a, pltpu.HBM) for a in arrays])
    sems = [(outs[2 * g], outs[2 * g + 1]) for g in range(ng)]
    return sems, list(outs[2 * ng:-1]), outs[-1]


def exchange_wait(name, plan, sems, arrays, after):
    send_sems, recv_sems = sems
    na = len(arrays)
    after = tuple(after) if isinstance(after, (tuple, list)) else (after,)

    def body(*refs):
        mine, theirs = plan(refs[:na], refs[na], refs[na + 1])
        for cp in mine:
            cp.wait_send()
        for cp in theirs:
            cp.wait_recv()

    hbm = pl.BlockSpec(memory_space=pltpu.HBM)
    sem = pl.BlockSpec(memory_space=pltpu.SEMAPHORE)
    return _call(body, name=name, in_specs=[hbm] * na + [sem, sem] + [_hbm()] * len(after),
                 out_specs=[hbm] * na, out_shape=[pltpu.HBM(a.shape, a.dtype) for a in arrays],
                 input_output_aliases={i: i for i in range(na)},
                 compiler_params=pltpu.CompilerParams(
                     has_side_effects=pltpu.SideEffectType.DATAFLOW_SIDE_EFFECTING))(
                         *arrays, send_sems, recv_sems, *after)


def plan_allgather_small(refs, send, recv):
    x, y, c, _ = _place()
    buf, = refs
    mine, theirs = [], []
    flips = [(fx, fy, fc) for fx in (0, 1) for fy in (0, 1) for fc in (0, 1)][1:]
    for k, (fx, fy, fc) in enumerate(flips):
        peer = (x ^ fx, y ^ fy, c ^ fc)
        slot = lambda px, py, pc: buf.at[4 * px + 2 * py + pc]
        mine.append(_remote(slot(x, y, c), slot(x, y, c), send.at[k], recv.at[k], peer))
        theirs.append(_remote(slot(*peer), slot(*peer), send.at[k], recv.at[k], peer))
    return mine, theirs


def add_sibling(grad, got, core, name):
    _, r, c = grad.shape
    hr = r // 2
    tr = _tile(hr, 512)
    nblk = hr // tr

    def body(core_ref, g_ref, o_ref, out_ref):
        out_ref[...] = (g_ref[...].astype(F32) + o_ref[...].astype(F32)).astype(BF16)

    grid_spec = pltpu.PrefetchScalarGridSpec(
        num_scalar_prefetch=1, grid=(N_CHIPS, nblk),
        in_specs=[pl.BlockSpec((None, tr, c), lambda t, i, core_ref: (t, core_ref[0] * nblk + i, 0)),
                  pl.BlockSpec((None, tr, c), lambda t, i, core_ref: (t, i, 0))],
        out_specs=pl.BlockSpec((None, tr, c), lambda t, i, core_ref: (t, i, 0)))
    return _call(body, name=name, grid_spec=grid_spec,
                 out_shape=jax.ShapeDtypeStruct((N_CHIPS, hr, c), BF16),
                 compiler_params=_params(("parallel", "parallel")))(core, grad, got)


def sum_chips(mine, owned, place, name):
    _, hr, c = mine.shape
    tr = _tile(hr, 512)
    nblk = hr // tr

    def body(place_ref, m_ref, o1_ref, o2_ref, o3_ref, out_ref):
        acc = m_ref[...].astype(F32)
        for o_ref in (o1_ref, o2_ref, o3_ref):
            acc = acc + o_ref[...].astype(F32)
        out_ref[...] = acc

    other = lambda k: pl.BlockSpec((None, tr, c), lambda i, place_ref: ((place_ref[0] + k) % N_CHIPS, i, 0))
    grid_spec = pltpu.PrefetchScalarGridSpec(
        num_scalar_prefetch=1, grid=(nblk,),
        in_specs=[other(0), other(1), other(2), other(3)],
        out_specs=pl.BlockSpec((tr, c), lambda i, place_ref: (place_ref[1] * nblk + i, 0)))
    return _call(body, name=name, grid_spec=grid_spec,
                 out_shape=jax.ShapeDtypeStruct((2 * hr, c), F32),
                 compiler_params=_params(("parallel",)))(place, mine, owned, owned, owned)


def sum_devices(gathered):
    _, rows, width = gathered.shape

    def body(g_ref, out_ref):
        acc = g_ref[0]
        for dev in range(1, 8):
            acc = acc + g_ref[dev]
        out_ref[...] = acc
        tail = acc[SUBLANES:]
        out_ref[SUBLANES:, :] = jnp.broadcast_to(jnp.sum(tail, axis=1, keepdims=True), tail.shape)

    return _call(body, name="sum_devices",
                 in_specs=[pl.BlockSpec(memory_space=pltpu.VMEM)],
                 out_specs=pl.BlockSpec(memory_space=pltpu.VMEM),
                 out_shape=jax.ShapeDtypeStruct((rows, width), F32))(gathered)


def _rope_tables(s):
    half = ROT_DIM // 2
    inv_freq = jnp.power(jnp.float32(ROPE_THETA), -jnp.arange(half, dtype=F32) * 2.0 / ROT_DIM)
    freq64 = jnp.concatenate([inv_freq, inv_freq, jnp.zeros((HEAD_DIM - ROT_DIM,), F32)])
    freq = jnp.concatenate([freq64, freq64])[None, :]
    dim = (jnp.arange(LANES) % HEAD_DIM)[None, :]
    ang = jnp.arange(s).astype(F32)[:, None] * freq
    cos, sin = jnp.cos(ang), jnp.sin(ang)
    return cos, jnp.where(dim < half, -sin, 0.0), jnp.where((dim >= half) & (dim < ROT_DIM), sin, 0.0)


def _pad_rows(a, rows):
    return jnp.pad(a, ((0, rows - a.shape[0]), (0, 0)))


def _pad_cols(a, cols):
    return jnp.pad(a, ((0, 0), (0, cols - a.shape[1])))


def kernel(x, p, norm_gain, w_in, q_norm_gain, k_norm_gain, attn_sinks, conv_w, w_out, ple_gate_norm_gain, w_ple_gate, b_ple_gate, w_ple_proj, ple_norm_gain, loss_target, m_norm_gain, m_w_in, m_q_norm_gain, m_k_norm_gain, m_attn_sinks, m_conv_w, m_w_out, m_ple_gate_norm_gain, m_w_ple_gate, m_b_ple_gate, m_w_ple_proj, m_ple_norm_gain, v_norm_gain, v_w_in, v_q_norm_gain, v_k_norm_gain, v_attn_sinks, v_conv_w, v_w_out, v_ple_gate_norm_gain, v_w_ple_gate, v_b_ple_gate, v_w_ple_proj, v_ple_norm_gain):
    x2, p2, tgt = x[0], p[0, 0], loss_target[0]
    s, d = x2.shape
    ple = p2.shape[1]
    aw = d // 2
    cw = d - aw
    nq = aw // HEAD_DIM
    sh = w_in.shape[2]
    in_w = N_CHIPS * sh
    dq = d // N_CHIPS
    cq = cw // N_CHIPS
    ga_off = (aw + 2 * KV_WIDTH) // COLT
    b_off = (2 * aw + 2 * KV_WIDTH) // COLT
    assert in_w == 2 * aw + 2 * KV_WIDTH + 4 * cw and aw % COLT == 0 and cw % COLT == 0
    assert s % BLOCK == 0 and sh % LANES == 0 and nq % (2 * N_KV_HEADS) == 0

    core = lax.axis_index("c").astype(jnp.int32).reshape(1)
    chip = 2 * lax.axis_index("x") + lax.axis_index("y")

    chip1 = chip.astype(jnp.int32).reshape(1)
    place = jnp.concatenate([chip1, core])
    w_in_own = cast_bf16(w_in[0], "cast_w_in")
    rest = [cast_into_slot(w_out[0], chip1, "cast_w_out"), cast_into_slot(w_ple_gate[0], chip1, "cast_w_pg"),
            cast_into_slot(w_ple_proj[0], chip1, "cast_w_pp"),
            lax.dynamic_update_slice(jnp.zeros((N_CHIPS, SUBLANES, cq), F32),
                                     _pad_rows(conv_w[0], SUBLANES)[None], (chip, 0, 0))]
    order = jnp.stack([chip, chip ^ 2, chip ^ 1, chip ^ 3]).astype(jnp.int32)
    wi_sems, wi_arrs, wi_token = exchange_start(
        "gather_wi_start", [(plan_shard_ici(j), [0, 1 + j], 1) for j in range(2)],
        [w_in_own] + [lax.empty((d, sh), BF16) for _ in range(3)])

    tn = _tile(d, 1024)
    ts = _tile(s, 2048)
    tabs = _rope_tables(s)
    qg = jnp.tile(q_norm_gain, (1, LANES // HEAD_DIM))
    kg = jnp.tile(k_norm_gain, (1, LANES // HEAD_DIM))
    seg_i = jnp.arange(SEG) // HEAD_DIM
    segb = (seg_i[:, None] == seg_i[None, :]).astype(BF16)
    h, z = norm_in_proj(x2, norm_gain, wi_arrs[0], order, in_w)
    own, land_x = exchange_wait("gather_wi_wait0", plan_shard_ici(0), wi_sems[0], [wi_arrs[0], wi_arrs[1]],
                                (z,) + tuple(tabs) + tuple(rest[:2]))
    own, land_y = exchange_wait("gather_wi_wait1", plan_shard_ici(1), wi_sems[1], [own, wi_arrs[2]], land_x)
    (relay_sems, rest_sems), started, rest_token = exchange_start(
        "gather_relay_rest_start", [(plan_shard_relay, [0, 1, 2], 2), (plan_gather_ici(3), [3, 4, 5, 6], 12)],
        [land_x, land_y, wi_arrs[3]] + rest)
    relayed, rest = started[:3], started[3:]
    land_x, land_y = exchange("gather_wi_pass01", plan_shard_pass, relayed[:2], 2, after=(rest_token,))
    z = in_proj_part(h, land_x, z, order, 1, in_w)
    z = in_proj_part(h, land_y, z, order, 2, in_w)
    land_x, land_y, land_far = exchange_wait("gather_wi_relay_wait", plan_shard_relay, relay_sems,
                                             [land_x, land_y, relayed[2]], z)
    land_far, = exchange("gather_wi_pass2", plan_shard_pass, [land_far], 1)
    z = in_proj_part(h, land_far, z, order, 3, in_w)
    w_shards = [own, land_x, land_y, land_far]
    wo_all, wg_all, wp_all, conv_all = exchange_wait("gather_rest_wait", plan_gather_ici(3), rest_sems, rest, z)
    pass_sems, passed, pass_token = exchange_start(
        "gather_rest_pass_start", [(plan_gather_pass, [0, 1, 2], 9)], [wo_all, wg_all, wp_all])
    conv_full = conv_all.transpose(1, 0, 2).reshape(SUBLANES, cw)
    qs, ks, vb = qk_prep_fwd(z, tabs, qg, kg, segb, aw, after=(pass_token,))
    attn, mix = attn_fwd(qs, ks, vb, z, attn_sinks, aw, d, ga_off)
    mix = conv_fwd(z, conv_full, mix, aw, cw, b_off)
    wo_all, wg_all, wp_all = exchange_wait("gather_rest_pass_wait", plan_gather_pass, pass_sems[0], passed, mix)
    wo_full = wo_all.reshape(d, d)
    wg_full = wg_all.reshape(d, d)
    x1, hg = out_proj_norm(mix, wo_full, x2, ple_gate_norm_gain)
    pq = d // N_CHIPS

    d_gl, d_pe, dy, acc_head = head_fwd_bwd(x1, hg, wg_full, p2, wp_all, tgt, b_ple_gate, ple_norm_gain)
    wgrad = lambda a_cols, shape3, o_spec, b_cols, name, a, b, grid: matmul(
        a, b, grid=grid,
        a_spec=pl.BlockSpec((ts, a_cols), lambda i, j, k: (k, i)),
        b_spec=pl.BlockSpec((ts, b_cols), lambda i, j, k: (k, j)),
        o_spec=o_spec, out_shape=jax.ShapeDtypeStruct(shape3, BF16), dims=TN, name=name)
    g_wg = wgrad(tn, (d, d), pl.BlockSpec((tn, tn), lambda i, j, k: (i, j)), tn, "mm_g_wg", hg, d_gl,
                 (d // tn, d // tn, s // ts))
    g_wp = wgrad(ple, (N_CHIPS, ple, pq), pl.BlockSpec((None, ple, pq), lambda i, j, k: (j, 0, 0)), pq,
                 "mm_g_wp", p2, d_pe, (1, N_CHIPS, s // ts))
    d_x1, d_x1b, acc_g = gate_proj_bwd_norm(d_gl, wg_full, x1, dy, ple_gate_norm_gain)
    g_wo = wgrad(tn, (d, d), pl.BlockSpec((tn, tn), lambda i, j, k: (i, j)), tn, "mm_g_wo", mix, d_x1b,
                 (d // tn, d // tn, s // ts))
    def reduce_sum(tag, handle, after):
        sems, arrays, _ = handle
        n = len(arrays) // 2
        got = exchange_wait("scatter_%s_wait" % tag, plan_scatter, sems[0], arrays, after)
        return [sum_chips(pt, o, place, "sum_chips_%s%d" % (tag, i))
                for i, (pt, o) in enumerate(zip(got[:n], got[n:]))]

    early_grads = [g_wo.reshape(N_CHIPS, dq, d), g_wg.reshape(N_CHIPS, dq, d), g_wp]
    eswap_sems, eswapping, eswap_token = exchange_start(
        "swap_early_start", [(plan_swap, list(range(6)), 3)],
        early_grads + [lax.empty((N_CHIPS, a.shape[1] // 2, a.shape[2]), BF16) for a in early_grads])
    d_o, d_gate, d_mix_conv = out_proj_bwd_gate(d_x1b, wo_full, attn, z, aw, ga_off, after=(eswap_token,))
    eswapped = exchange_wait("swap_early_wait", plan_swap, eswap_sems[0], eswapping, d_o)
    early_parts = [add_sibling(g, o, core, "add_sibling_early%d" % i)
                   for i, (g, o) in enumerate(zip(eswapped[:3], eswapped[3:]))]
    early = exchange_start("scatter_early_start", [(plan_scatter, list(range(6)), 9)],
                           early_parts + [lax.empty(a.shape, BF16) for a in early_parts])
    dqs, dks, dvs, acc_sink = attn_bwd(qs, ks, vb, d_o, attn_sinks, aw, after=(early[-1],))
    dz, acc_qk = qk_prep_bwd(z, dqs, dks, dvs, d_gate, tabs, qg, kg, segb, aw)
    dz, acc_conv = conv_bwd(z, d_mix_conv, conv_full, dz, aw, cw, b_off)
    join_sems, joining, join_token = exchange_start(
        "join_early_start", [(plan_join, [0, 1, 2], 3)], reduce_sum("early", early, dz))
    g_send = in_proj_wgrad_half(h, dz, None, 1 - core, after=(join_token,))
    swap_sems, swapping, swap_token = exchange_start(
        "swap_late_start", [(plan_swap, [0, 1], 1)], [g_send, lax.empty((N_CHIPS, d // 2, sh), BF16)])
    g_wi = in_proj_wgrad_half(h, dz, swapping[0], core, after=(swap_token,))
    full_wo, full_wg, full_wp = exchange_wait("join_early_wait", plan_join, join_sems[0], joining, g_wi)
    g_wi, got_wi = exchange_wait("swap_late_wait", plan_swap, swap_sems[0], [g_wi, swapping[1]], full_wo)
    part_wi = add_sibling(g_wi, got_wi, core, "add_sibling_late0")
    late = exchange_start("scatter_late_start", [(plan_scatter, [0, 1], 3)],
                          [part_wi, lax.empty(part_wi.shape, BF16)])
    grad_x, acc_x = in_proj_bwd_norm(dz, w_shards, order, x2, d_x1, norm_gain, after=(late[-1],))

    wsm = max(d, cw)
    misc = jnp.concatenate([acc_qk[0:1, :HEAD_DIM], acc_qk[1:2, :HEAD_DIM], acc_sink[0:1, :nq]], axis=1)
    small = jnp.concatenate([
        _pad_cols(acc_x[0:1], wsm), _pad_cols(acc_g[0:1], wsm), _pad_cols(acc_head[0:1], wsm),
        _pad_cols(acc_head[1:2], wsm), _pad_cols(misc, wsm), _pad_cols(acc_conv[0:3], wsm),
        _pad_rows(_pad_cols(acc_head[2:3], wsm), SUBLANES)], axis=0)
    device = 2 * chip + lax.axis_index("c")
    (small_sems, last_sems), started, last_token = exchange_start(
        "small_join_late_start", [(plan_allgather_small, [0], 7), (plan_join, [1], 1)],
        [lax.dynamic_update_slice(jnp.zeros((8,) + small.shape, F32), small[None], (device, 0, 0))]
        + reduce_sum("late", late, grad_x))
    small_bufs, last_halves = started[:1], started[1:]
    big, after = {}, (last_token,)
    for nm, g, w, m, v in (("w_out", full_wo, w_out, m_w_out, v_w_out),
                           ("w_ple_gate", full_wg, w_ple_gate, m_w_ple_gate, v_w_ple_gate),
                           ("w_ple_proj", full_wp, w_ple_proj, m_w_ple_proj, v_w_ple_proj)):
        stepped = adamw(g, w[0], m[0], v[0], "adamw_" + nm, after=after)
        big[nm], after = [o[None] for o in stepped], (stepped[1],)
    full_wi, = exchange_wait("join_late_wait", plan_join, last_sems, last_halves, after[0])
    big["w_in"] = [o[None] for o in adamw(full_wi, w_in[0], m_w_in[0], v_w_in[0], "adamw_w_in")]

    gathered, = exchange_wait("small_wait", plan_allgather_small, small_sems, small_bufs, big["w_in"][1])
    tot = sum_devices(gathered)
    loss = tot[SUBLANES, 0]

    def pack(vals):
        ng, pg, bg, eg, qgv, kgv, sk, cv = vals
        misc_v = jnp.concatenate([qgv, kgv, sk], axis=1)
        return jnp.concatenate([_pad_cols(ng, wsm), _pad_cols(pg, wsm), _pad_cols(bg, wsm), _pad_cols(eg, wsm),
                                _pad_cols(misc_v, wsm), _pad_cols(cv[0], wsm)], axis=0)

    g_small = jnp.concatenate(
        [tot[0:5], _pad_cols(lax.dynamic_slice(tot[5:8], (0, chip * cq), (3, cq)), wsm)], axis=0)
    w_small = pack((norm_gain, ple_gate_norm_gain, b_ple_gate, ple_norm_gain, q_norm_gain, k_norm_gain,
                    attn_sinks, conv_w))
    m_small = pack((m_norm_gain, m_ple_gate_norm_gain, m_b_ple_gate, m_ple_norm_gain, m_q_norm_gain,
                    m_k_norm_gain, m_attn_sinks, m_conv_w))
    v_small = pack((v_norm_gain, v_ple_gate_norm_gain, v_b_ple_gate, v_ple_norm_gain, v_q_norm_gain,
                    v_k_norm_gain, v_attn_sinks, v_conv_w))
    sm = adamw(g_small, w_small, m_small, v_small, "adamw_small")

    def unpack(a):
        return {"norm_gain": a[0:1, :d], "ple_gate_norm_gain": a[1:2, :d], "b_ple_gate": a[2:3, :d],
                "ple_norm_gain": a[3:4, :d], "q_norm_gain": a[4:5, :HEAD_DIM],
                "k_norm_gain": a[4:5, HEAD_DIM:2 * HEAD_DIM],
                "attn_sinks": a[4:5, 2 * HEAD_DIM:2 * HEAD_DIM + nq], "conv_w": a[5:8, :cq][None]}

    order = ("norm_gain", "w_in", "q_norm_gain", "k_norm_gain", "attn_sinks", "conv_w", "w_out",
             "ple_gate_norm_gain", "w_ple_gate", "b_ple_gate", "w_ple_proj", "ple_norm_gain")
    outs = [loss, grad_x[None]]
    for kind in range(4):
        table = unpack(sm[kind])
        for nm in order:
            outs.append(big[nm][kind] if nm in big else table[nm])
    return tuple(outs)
```

```python
import functools

import jax
import jax.numpy as jnp
from jax import lax
from jax.experimental import pallas as pl
from jax.experimental.pallas import tpu as pltpu

F32 = jnp.float32
BF16 = jnp.bfloat16
MESH = pl.DeviceIdType.MESH

HEAD_DIM = 64
N_KV_HEADS = 4
KV_WIDTH = N_KV_HEADS * HEAD_DIM
BLOCK = 128
ROT_DIM = 16
ROPE_THETA = 500000.0
EPS = 1e-6
NEG_INF = -1e30
N_CHIPS = 4
LANES = 128
SUBLANES = 8
COLT = 512
SEG = 256
VMEM_LIMIT = 48 * 1024 * 1024
VMEM_LIMIT_BIG = 60 * 1024 * 1024

ADAM_LR = 0.001
ADAM_B1 = 0.9
ADAM_B2 = 0.999
ADAM_EPS = 1e-08
ADAM_WD = 0.01
ADAM_STEP = 10

NN = (((1,), (0,)), ((), ()))
NT = (((1,), (1,)), ((), ()))
TN = (((0,), (0,)), ((), ()))


def _call(body, **kw):
    return pl.pallas_call(body, **kw)


def _call_after(body, after, **kw):
    n_in, n_after = len(kw["in_specs"]), len(after)
    kw["in_specs"] = list(kw["in_specs"]) + [pl.BlockSpec(memory_space=pl.ANY)] * n_after

    def body_after(*refs):
        body(*refs[:n_in], *refs[n_in + n_after:])

    call = _call(body_after, **kw)
    return lambda *args: call(*args, *after)


def _params(sem, vmem=VMEM_LIMIT):
    return pltpu.CompilerParams(dimension_semantics=sem, vmem_limit_bytes=vmem)


def _tile(n, pref):
    return pref if n % pref == 0 else n


def _sigmoid(v):
    return 1.0 / (1.0 + jnp.exp(-v))


def _hbm():
    return pl.BlockSpec(memory_space=pl.ANY)


def cast_bf16(a, name):
    r, c = a.shape
    tr = _tile(r, 512)

    def body(a_ref, o_ref):
        o_ref[...] = a_ref[...].astype(BF16)

    return _call(body, name=name, grid=(r // tr,),
                 in_specs=[pl.BlockSpec((tr, c), lambda i: (i, 0))],
                 out_specs=pl.BlockSpec((tr, c), lambda i: (i, 0)),
                 out_shape=jax.ShapeDtypeStruct((r, c), BF16),
                 compiler_params=_params(("parallel",)))(a)


def cast_into_slot(a, chip, name):
    r, c = a.shape
    tr = _tile(r, 512)

    def body(chip_ref, a_ref, o_ref):
        o_ref[...] = a_ref[...].astype(BF16)

    grid_spec = pltpu.PrefetchScalarGridSpec(
        num_scalar_prefetch=1, grid=(r // tr,),
        in_specs=[pl.BlockSpec((tr, c), lambda i, chip_ref: (i, 0))],
        out_specs=pl.BlockSpec((None, tr, c), lambda i, chip_ref: (chip_ref[0], i, 0)))
    return _call(body, name=name, grid_spec=grid_spec,
                 out_shape=jax.ShapeDtypeStruct((N_CHIPS, r, c), BF16),
                 compiler_params=_params(("parallel",)))(chip, a)


def out_proj_norm(mix, wo, x, gain):
    s, d = x.shape
    tm = _tile(s, 512)

    def body(m_ref, w_ref, x_ref, g_ref, x1_ref, hg_ref):
        x1 = x_ref[...] + jnp.dot(m_ref[...], w_ref[...], preferred_element_type=F32)
        x1_ref[...] = x1
        r = lax.rsqrt(jnp.mean(x1 * x1, axis=-1, keepdims=True) + EPS)
        hg_ref[...] = ((x1 * r) * g_ref[...]).astype(BF16)

    row = pl.BlockSpec((tm, d), lambda i: (i, 0))
    return _call(body, name="out_proj_norm", grid=(s // tm,),
                 in_specs=[row, pl.BlockSpec((d, d), lambda i: (0, 0), pipeline_mode=pl.Buffered(1)), row,
                           pl.BlockSpec((1, d), lambda i: (0, 0))],
                 out_specs=[row, row],
                 out_shape=[jax.ShapeDtypeStruct((s, d), F32), jax.ShapeDtypeStruct((s, d), BF16)],
                 compiler_params=_params(("parallel",), VMEM_LIMIT_BIG))(mix, wo, x, gain)


def gate_proj_bwd_norm(d_gl, wg, xin, add, gain):
    s, d = xin.shape
    tm = _tile(s, 256)

    def body(dg_ref, w_ref, x_ref, a_ref, g_ref, dx_ref, dxb_ref, acc_ref):
        i = pl.program_id(0)
        dyv = lax.dot_general(dg_ref[...], w_ref[...], NT, preferred_element_type=F32)
        xf = x_ref[...]
        r = lax.rsqrt(jnp.mean(xf * xf, axis=-1, keepdims=True) + EPS)
        xhat = xf * r
        gd = dyv * g_ref[...]
        dx = a_ref[...] + r * (gd - xhat * jnp.mean(xhat * gd, axis=-1, keepdims=True))
        dx_ref[...] = dx
        dxb_ref[...] = dx.astype(BF16)

        @pl.when(i == 0)
        def _():
            acc_ref[...] = jnp.zeros_like(acc_ref)

        acc_ref[0:1, :] += jnp.sum(dyv * xhat, axis=0, keepdims=True)

    row = pl.BlockSpec((tm, d), lambda i: (i, 0))
    return _call(body, name="gate_proj_bwd_norm", grid=(s // tm,),
                 in_specs=[row, pl.BlockSpec((d, d), lambda i: (0, 0), pipeline_mode=pl.Buffered(1)), row, row,
                           pl.BlockSpec((1, d), lambda i: (0, 0))],
                 out_specs=[row, row, pl.BlockSpec((SUBLANES, d), lambda i: (0, 0))],
                 out_shape=[jax.ShapeDtypeStruct((s, d), F32), jax.ShapeDtypeStruct((s, d), BF16),
                            jax.ShapeDtypeStruct((SUBLANES, d), F32)],
                 compiler_params=_params(("arbitrary",), VMEM_LIMIT_BIG))(d_gl, wg, xin, add, gain)


def head_fwd_bwd(x1, hg, wg, p, wp, tgt, bias, ple_gain):
    s, d = x1.shape
    tm = _tile(s, 256)

    def body(x1_ref, hg_ref, wg_ref, p_ref, wp_ref, t_ref, b_ref, g_ref, dgl_ref, dpe_ref, dy_ref, acc_ref):
        i = pl.program_id(0)
        gl = jnp.dot(hg_ref[...], wg_ref[...], preferred_element_type=F32)
        pb = p_ref[...].astype(BF16)
        pev = jnp.concatenate([jnp.dot(pb, wp_ref[q], preferred_element_type=F32) for q in range(N_CHIPS)],
                              axis=1)
        r = lax.rsqrt(jnp.mean(pev * pev, axis=-1, keepdims=True) + EPS)
        pehat = pev * r
        gain = g_ref[...]
        e = pehat * gain
        gate = _sigmoid(gl + b_ref[...])
        diff = (x1_ref[...] + gate * e) - t_ref[...]
        dy = diff * (1.0 / d)
        dy_ref[...] = dy
        d_gl = (dy * e) * (gate * (1.0 - gate))
        dgl_ref[...] = d_gl.astype(BF16)
        d_e = dy * gate
        gd = d_e * gain
        d_pe = r * (gd - pehat * jnp.mean(pehat * gd, axis=-1, keepdims=True))
        dpe_ref[...] = d_pe.astype(BF16)

        @pl.when(i == 0)
        def _():
            acc_ref[...] = jnp.zeros_like(acc_ref)

        acc_ref[0:1, :] += jnp.sum(d_gl, axis=0, keepdims=True)
        acc_ref[1:2, :] += jnp.sum(d_e * pehat, axis=0, keepdims=True)
        acc_ref[2:3, :] += jnp.sum(diff * diff, axis=0, keepdims=True) * (0.5 / d)

    row = pl.BlockSpec((tm, d), lambda i: (i, 0))
    vec = pl.BlockSpec((1, d), lambda i: (0, 0))
    return _call(body, name="head_fwd_bwd", grid=(s // tm,),
                 in_specs=[row, row, pl.BlockSpec((d, d), lambda i: (0, 0), pipeline_mode=pl.Buffered(1)),
                           pl.BlockSpec((tm, p.shape[1]), lambda i: (i, 0)),
                           pl.BlockSpec(wp.shape, lambda i: (0, 0, 0)), row, vec, vec],
                 out_specs=[row, row, row, pl.BlockSpec((SUBLANES, d), lambda i: (0, 0))],
                 out_shape=[jax.ShapeDtypeStruct((s, d), BF16), jax.ShapeDtypeStruct((s, d), BF16),
                            jax.ShapeDtypeStruct((s, d), F32), jax.ShapeDtypeStruct((SUBLANES, d), F32)],
                 compiler_params=_params(("arbitrary",), VMEM_LIMIT_BIG))(
                     x1, hg, wg, p, wp, tgt, bias, ple_gain)


def adamw(g, w, m, v, name, after=()):
    r, c = g.shape
    tr = _tile(r, min(256, max(SUBLANES, r // 8)))

    def body(g_ref, w_ref, m_ref, v_ref, go_ref, d_ref, mo_ref, vo_ref):
        gv = g_ref[...]
        mn = ADAM_B1 * m_ref[...] + (1.0 - ADAM_B1) * gv
        vn = ADAM_B2 * v_ref[...] + (1.0 - ADAM_B2) * (gv * gv)
        m_hat = mn / (1.0 - ADAM_B1 ** ADAM_STEP)
        v_hat = vn / (1.0 - ADAM_B2 ** ADAM_STEP)
        go_ref[...] = gv
        d_ref[...] = -ADAM_LR * (m_hat / (jnp.sqrt(v_hat) + ADAM_EPS) + ADAM_WD * w_ref[...])
        mo_ref[...] = mn
        vo_ref[...] = vn

    blk = pl.BlockSpec((tr, c), lambda i: (i, 0))
    shp = jax.ShapeDtypeStruct((r, c), F32)
    return _call_after(body, after, name=name, grid=(r // tr,), in_specs=[blk] * 4, out_specs=[blk] * 4,
                       out_shape=[shp] * 4, compiler_params=_params(("parallel",)))(g, w, m, v)


def matmul(a, b, *, grid, a_spec, b_spec, o_spec, out_shape, dims, name):
    nk = grid[2]
    acc_shape = tuple(d for d in o_spec.block_shape if d is not None)

    def body(a_ref, b_ref, o_ref, *scratch):
        part = lax.dot_general(a_ref[...].astype(BF16), b_ref[...].astype(BF16), dims, preferred_element_type=F32)
        if nk == 1:
            o_ref[...] = part.astype(o_ref.dtype)
            return
        acc_ref, = scratch
        k = pl.program_id(2)

        @pl.when(k == 0)
        def _():
            acc_ref[...] = part

        @pl.when((k > 0) & (k < nk - 1))
        def _():
            acc_ref[...] += part

        @pl.when(k == nk - 1)
        def _():
            o_ref[...] = (acc_ref[...] + part).astype(o_ref.dtype)

    return _call(body, name=name, grid=grid, in_specs=[a_spec, b_spec], out_specs=o_spec, out_shape=out_shape,
                 scratch_shapes=[pltpu.VMEM(acc_shape, F32)] if nk > 1 else [],
                 compiler_params=_params(("parallel", "parallel", "arbitrary")))(a, b)


def norm_in_proj(x, gain, w, order, in_w):
    s, d = x.shape
    sh = w.shape[1]
    tm = _tile(s, 512)

    def body(order_ref, x_ref, g_ref, w_ref, h_ref, z_ref):
        xf = x_ref[...]
        r = lax.rsqrt(jnp.mean(xf * xf, axis=-1, keepdims=True) + EPS)
        hb = ((xf * r) * g_ref[...]).astype(BF16)
        h_ref[...] = hb
        z_ref[...] = jnp.dot(hb, w_ref[...], preferred_element_type=F32)

    grid_spec = pltpu.PrefetchScalarGridSpec(
        num_scalar_prefetch=1, grid=(s // tm,),
        in_specs=[pl.BlockSpec((tm, d), lambda i, order_ref: (i, 0)),
                  pl.BlockSpec((1, d), lambda i, order_ref: (0, 0)),
                  pl.BlockSpec((d, sh), lambda i, order_ref: (0, 0), pipeline_mode=pl.Buffered(1))],
        out_specs=[pl.BlockSpec((tm, d), lambda i, order_ref: (i, 0)),
                   pl.BlockSpec((tm, sh), lambda i, order_ref: (i, order_ref[0]))])
    return _call(body, name="norm_in_proj", grid_spec=grid_spec,
                 out_shape=[jax.ShapeDtypeStruct((s, d), BF16), jax.ShapeDtypeStruct((s, in_w), F32)],
                 compiler_params=_params(("parallel",)))(order, x, gain, w)


def in_proj_part(h, w, z_prev, order, q, in_w):
    s, d = h.shape
    sh = w.shape[1]
    tm = _tile(s, 512)

    def body(order_ref, h_ref, w_ref, *rest):
        rest[-1][...] = jnp.dot(h_ref[...], w_ref[...], preferred_element_type=F32)

    in_specs = [pl.BlockSpec((tm, d), lambda i, order_ref: (i, 0)),
                pl.BlockSpec((d, sh), lambda i, order_ref: (0, 0))]
    args = [order, h, w]
    if z_prev is not None:
        in_specs.append(_hbm())
        args.append(z_prev)
    grid_spec = pltpu.PrefetchScalarGridSpec(
        num_scalar_prefetch=1, grid=(s // tm,), in_specs=in_specs,
        out_specs=pl.BlockSpec((tm, sh), lambda i, order_ref: (i, order_ref[q])))
    return _call(body, name="mm_z%d" % q, grid_spec=grid_spec,
                 out_shape=jax.ShapeDtypeStruct((s, in_w), F32),
                 input_output_aliases={3: 0} if z_prev is not None else {},
                 compiler_params=_params(("parallel",)))(*args)


def in_proj_wgrad_half(h, dz, g_prev, half, after):
    s, d = h.shape
    sh = dz.shape[1] // N_CHIPS
    hr = d // 2

    def body(half_ref, h_ref, dz_ref, *rest):
        rest[-1][...] = lax.dot_general(h_ref[...], dz_ref[...], TN, preferred_element_type=F32).astype(BF16)

    extra = ([g_prev] if g_prev is not None else []) + list(after)
    grid_spec = pltpu.PrefetchScalarGridSpec(
        num_scalar_prefetch=1, grid=(N_CHIPS,),
        in_specs=[pl.BlockSpec((s, hr), lambda j, half_ref: (0, half_ref[0]), pipeline_mode=pl.Buffered(1)),
                  pl.BlockSpec((s, sh), lambda j, half_ref: (0, j))] + [_hbm()] * len(extra),
        out_specs=pl.BlockSpec((None, hr, sh), lambda j, half_ref: (j, half_ref[0], 0)))
    return _call(body, name="mm_g_wi_%s" % ("keep" if g_prev is not None else "send"), grid_spec=grid_spec,
                 out_shape=jax.ShapeDtypeStruct((N_CHIPS, d, sh), BF16),
                 input_output_aliases={3: 0} if g_prev is not None else {},
                 compiler_params=_params(("parallel",), VMEM_LIMIT_BIG))(half, h, dz, *extra)


def in_proj_bwd_norm(dz, ws, order, xin, add, gain, after):
    s, d = xin.shape
    sh = ws[0].shape[1]
    tm = _tile(s, 256)
    nq, n_after = len(ws), len(after)

    def body(order_ref, *refs):
        a_refs, b_refs = refs[:nq], refs[nq:2 * nq]
        x_ref, add_ref, g_ref = refs[2 * nq:2 * nq + 3]
        dx_ref, acc_ref = refs[2 * nq + 3 + n_after:]
        i = pl.program_id(0)
        dyv = lax.dot_general(a_refs[0][...], b_refs[0][...], NT, preferred_element_type=F32)
        for q in range(1, nq):
            dyv += lax.dot_general(a_refs[q][...], b_refs[q][...], NT, preferred_element_type=F32)
        xf = x_ref[...]
        r = lax.rsqrt(jnp.mean(xf * xf, axis=-1, keepdims=True) + EPS)
        xhat = xf * r
        gd = dyv * g_ref[...]
        dx_ref[...] = add_ref[...] + r * (gd - xhat * jnp.mean(xhat * gd, axis=-1, keepdims=True))

        @pl.when(i == 0)
        def _():
            acc_ref[...] = jnp.zeros_like(acc_ref)

        acc_ref[0:1, :] += jnp.sum(dyv * xhat, axis=0, keepdims=True)

    a_spec = lambda q: pl.BlockSpec((tm, sh), functools.partial(lambda i, order_ref, q: (i, order_ref[q]), q=q))
    row = pl.BlockSpec((tm, d), lambda i, order_ref: (i, 0))
    grid_spec = pltpu.PrefetchScalarGridSpec(
        num_scalar_prefetch=1, grid=(s // tm,),
        in_specs=[a_spec(q) for q in range(nq)]
        + [pl.BlockSpec((d, sh), lambda i, order_ref: (0, 0), pipeline_mode=pl.Buffered(1))] * nq
        + [row, row, pl.BlockSpec((1, d), lambda i, order_ref: (0, 0))] + [_hbm()] * n_after,
        out_specs=[row, pl.BlockSpec((SUBLANES, d), lambda i, order_ref: (0, 0))])
    return _call(body, name="in_proj_bwd_norm", grid_spec=grid_spec,
                 out_shape=[jax.ShapeDtypeStruct((s, d), F32), jax.ShapeDtypeStruct((SUBLANES, d), F32)],
                 compiler_params=_params(("arbitrary",), VMEM_LIMIT_BIG))(
                     order, *([dz] * nq), *ws, xin, add, gain, *after)


def _seg_mean(sq, segb):
    parts = []
    for cgrp in range(sq.shape[1] // SEG):
        blk = sq[:, cgrp * SEG:(cgrp + 1) * SEG]
        hi = blk.astype(BF16)
        r1 = blk - hi.astype(F32)
        mid = r1.astype(BF16)
        lo = (r1 - mid.astype(F32)).astype(BF16)
        acc = jnp.dot(hi, segb, preferred_element_type=F32)
        acc += jnp.dot(mid, segb, preferred_element_type=F32)
        acc += jnp.dot(lo, segb, preferred_element_type=F32)
        parts.append(acc)
    out = parts[0] if len(parts) == 1 else jnp.concatenate(parts, axis=1)
    return out * (1.0 / HEAD_DIM)


def _rope(v, cos, sa, sb):
    parts = []
    for cgrp in range(v.shape[1] // LANES):
        blk = v[:, cgrp * LANES:(cgrp + 1) * LANES]
        parts.append(blk * cos + pltpu.roll(blk, LANES - 8, 1) * sa + pltpu.roll(blk, 8, 1) * sb)
    return parts[0] if len(parts) == 1 else jnp.concatenate(parts, axis=1)


def _rope_t(dv, cos, sa, sb):
    parts = []
    for cgrp in range(dv.shape[1] // LANES):
        blk = dv[:, cgrp * LANES:(cgrp + 1) * LANES]
        parts.append(blk * cos + pltpu.roll(blk * sa, 8, 1) + pltpu.roll(blk * sb, LANES - 8, 1))
    return parts[0] if len(parts) == 1 else jnp.concatenate(parts, axis=1)


def _tile_lanes(vec, width):
    reps = width // LANES
    return vec if reps == 1 else jnp.tile(vec, (1, reps))


def qk_prep_fwd(z, tabs, qg, kg, segb, aw, after=()):
    s = z.shape[0]
    tm = _tile(s, 512)
    wq = aw + 2 * KV_WIDTH

    def body(z_ref, cos_ref, sa_ref, sb_ref, qg_ref, kg_ref, seg_ref, qs_ref, ks_ref, vb_ref):
        zz = z_ref[...]
        q, k, v = zz[:, :aw], zz[:, aw:aw + KV_WIDTH], zz[:, aw + KV_WIDTH:]
        cos, sa, sb, segm = cos_ref[...], sa_ref[...], sb_ref[...], seg_ref[...]
        rq = lax.rsqrt(_seg_mean(q * q, segm) + EPS)
        qn = (q * rq) * _tile_lanes(qg_ref[...], aw)
        qs_ref[...] = (_rope(qn, cos, sa, sb) * (HEAD_DIM ** -0.5)).astype(BF16)
        rk = lax.rsqrt(_seg_mean(k * k, segm) + EPS)
        kn = (k * rk) * _tile_lanes(kg_ref[...], KV_WIDTH)
        ks_ref[...] = _rope(kn, cos, sa, sb).astype(BF16)
        vb_ref[...] = v.astype(BF16)

    tab = pl.BlockSpec((tm, LANES), lambda i: (i, 0))
    vec = pl.BlockSpec((1, LANES), lambda i: (0, 0))
    return _call_after(body, after, name="qk_prep_fwd", grid=(s // tm,),
                       in_specs=[pl.BlockSpec((tm, wq), lambda i: (i, 0)), tab, tab, tab, vec, vec,
                                 pl.BlockSpec((SEG, SEG), lambda i: (0, 0))],
                       out_specs=[pl.BlockSpec((tm, aw), lambda i: (i, 0)),
                                  pl.BlockSpec((tm, KV_WIDTH), lambda i: (i, 0)),
                                  pl.BlockSpec((tm, KV_WIDTH), lambda i: (i, 0))],
                       out_shape=[jax.ShapeDtypeStruct((s, aw), BF16), jax.ShapeDtypeStruct((s, KV_WIDTH), BF16),
                                  jax.ShapeDtypeStruct((s, KV_WIDTH), BF16)],
                       compiler_params=_params(("parallel",)))(z, *tabs, qg, kg, segb)


def qk_prep_bwd(z, dqs, dks, dvs, d_gate, tabs, qg, kg, segb, aw):
    s, in_w = z.shape
    tm = _tile(s, 512)
    wq = aw + 2 * KV_WIDTH

    def body(z_ref, dq_ref, dk_ref, dv_ref, dga_ref, cos_ref, sa_ref, sb_ref, qg_ref, kg_ref, seg_ref,
             dz_ref, acc_ref):
        i = pl.program_id(0)
        zz = z_ref[...]
        q, k = zz[:, :aw], zz[:, aw:aw + KV_WIDTH]
        cos, sa, sb, segm = cos_ref[...], sa_ref[...], sb_ref[...], seg_ref[...]

        def one(xv, dout, gvec, width):
            g = _tile_lanes(gvec, width)
            r = lax.rsqrt(_seg_mean(xv * xv, segm) + EPS)
            xhat = xv * r
            dn = _rope_t(dout, cos, sa, sb)
            gd = dn * g
            dx = r * (gd - xhat * _seg_mean(xhat * gd, segm))
            contrib = jnp.sum(dn * xhat, axis=0, keepdims=True)
            folded = contrib[:, :LANES]
            for cgrp in range(1, width // LANES):
                folded = folded + contrib[:, cgrp * LANES:(cgrp + 1) * LANES]
            return dx, folded + pltpu.roll(folded, HEAD_DIM, 1)

        dq, gq = one(q, dq_ref[...] * (HEAD_DIM ** -0.5), qg_ref[...], aw)
        dk, gk = one(k, dk_ref[...], kg_ref[...], KV_WIDTH)
        dz_ref[:, :aw] = dq.astype(BF16)
        dz_ref[:, aw:aw + KV_WIDTH] = dk.astype(BF16)
        dz_ref[:, aw + KV_WIDTH:wq] = dv_ref[...].astype(BF16)
        dz_ref[:, wq:] = dga_ref[...]

        @pl.when(i == 0)
        def _():
            acc_ref[...] = jnp.zeros_like(acc_ref)

        acc_ref[0:1, :] += gq
        acc_ref[1:2, :] += gk

    tab = pl.BlockSpec((tm, LANES), lambda i: (i, 0))
    vec = pl.BlockSpec((1, LANES), lambda i: (0, 0))
    kvb = pl.BlockSpec((tm, KV_WIDTH), lambda i: (i, 0))
    awb = pl.BlockSpec((tm, aw), lambda i: (i, 0))
    return _call(body, name="qk_prep_bwd", grid=(s // tm,),
                 in_specs=[pl.BlockSpec((tm, wq), lambda i: (i, 0)), awb, kvb, kvb, awb, tab, tab, tab, vec, vec,
                           pl.BlockSpec((SEG, SEG), lambda i: (0, 0))],
                 out_specs=[pl.BlockSpec((tm, wq + aw), lambda i: (i, 0)),
                            pl.BlockSpec((SUBLANES, LANES), lambda i: (0, 0))],
                 out_shape=[jax.ShapeDtypeStruct((s, in_w), BF16), jax.ShapeDtypeStruct((SUBLANES, LANES), F32)],
                 compiler_params=_params(("arbitrary",)))(z, dqs, dks, dvs, d_gate, *tabs, qg, kg, segb)


def _placed(band, lane_idx):
    out = {}
    for kh in range(N_KV_HEADS):
        grp = band[:, (kh // 2) * LANES:(kh // 2 + 1) * LANES]
        for half in (0, 1):
            t = grp if half == kh % 2 else pltpu.roll(grp, HEAD_DIM, 1)
            keep = (lane_idx >= half * HEAD_DIM) & (lane_idx < (half + 1) * HEAD_DIM)
            out[kh, half] = jnp.where(keep, t, jnp.zeros_like(t))
    return out


def _softmax_with_sink(sc, valid, sink):
    sc = jnp.where(valid, sc, NEG_INF)
    m = jnp.maximum(jnp.max(sc, axis=-1, keepdims=True), sink)
    e = jnp.exp(sc - m)
    es = jnp.exp(sink - m)
    den = jnp.sum(e, axis=-1, keepdims=True) + es
    return e / den, es / den


def _stack_halves(placed, kh):
    return jnp.concatenate([placed[kh, 0], placed[kh, 1]], axis=0)


def _valid_mask(n):
    r_i = lax.broadcasted_iota(jnp.int32, (BLOCK, 2 * BLOCK), 0)
    j_i = lax.broadcasted_iota(jnp.int32, (BLOCK, 2 * BLOCK), 1)
    return (j_i > r_i) & (j_i <= r_i + BLOCK) & ((n > 0) | (j_i >= BLOCK))


def attn_fwd(qs, ks, vb, z, sinks, aw, d, ga_off):
    s = qs.shape[0]
    nb = s // BLOCK
    nq = aw // HEAD_DIM
    grp_sz = nq // N_KV_HEADS
    n_ga = aw // COLT

    def body(q_ref, ko_ref, kp_ref, vo_ref, vp_ref, *rest):
        ga_refs = rest[:n_ga]
        sink_ref, attn_ref, mix_ref = rest[n_ga:]
        n = pl.program_id(0)
        lane_idx = lax.broadcasted_iota(jnp.int32, (2 * BLOCK, LANES), 1)
        kpl = _placed(jnp.concatenate([kp_ref[...], ko_ref[...]], axis=0), lane_idx)
        vpl = _placed(jnp.concatenate([vp_ref[...], vo_ref[...]], axis=0), lane_idx)
        valid = _valid_mask(n)
        groups = range(nq // 2)
        kcat = [_stack_halves(kpl, kh) for kh in range(N_KV_HEADS)]
        vcat = [_stack_halves(vpl, kh) for kh in range(N_KV_HEADS)]
        scs = [lax.dot_general(q_ref[:, c * LANES:(c + 1) * LANES], kcat[2 * c // grp_sz], NT,
                               preferred_element_type=F32) for c in groups]
        probs = [jnp.concatenate(
            [_softmax_with_sink(scs[c][:, half * 2 * BLOCK:(half + 1) * 2 * BLOCK], valid,
                                sink_ref[0, 2 * c + half])[0].astype(BF16) for half in (0, 1)], axis=1)
                 for c in groups]
        for c in groups:
            acc = jnp.dot(probs[c], vcat[2 * c // grp_sz], preferred_element_type=F32)
            attn_ref[:, c * LANES:(c + 1) * LANES] = acc
            col = c * LANES
            ga = ga_refs[col // COLT][:, col % COLT:col % COLT + LANES]
            mix_ref[:, c * LANES:(c + 1) * LANES] = (acc * (ga * _sigmoid(ga))).astype(BF16)

    own = lambda n: (n, 0)
    prev = lambda n: (jnp.maximum(n - 1, 0), 0)
    kvs = lambda imap: pl.BlockSpec((BLOCK, KV_WIDTH), imap)
    ga_specs = [pl.BlockSpec((BLOCK, COLT), functools.partial(lambda n, j: (n, ga_off + j), j=j))
                for j in range(n_ga)]
    return _call(body, name="attn_fwd", grid=(nb,),
                 in_specs=[pl.BlockSpec((BLOCK, aw), own), kvs(own), kvs(prev), kvs(own), kvs(prev)]
                 + ga_specs + [pl.BlockSpec(memory_space=pltpu.SMEM)],
                 out_specs=[pl.BlockSpec((BLOCK, aw), own), pl.BlockSpec((BLOCK, aw), own)],
                 out_shape=[jax.ShapeDtypeStruct((s, aw), F32), jax.ShapeDtypeStruct((s, d), BF16)],
                 compiler_params=_params(("parallel",)))(qs, ks, ks, vb, vb, *([z] * n_ga), sinks)


def out_proj_bwd_gate(d_x1b, wo, attn, z, aw, ga_off, after=()):
    s, d = d_x1b.shape
    tm = _tile(s, 512)
    n_ga = aw // COLT

    def body(dx_ref, w_ref, at_ref, *rest):
        ga_refs, (do_ref, dga_ref, dmc_ref) = rest[:n_ga], rest[n_ga:]
        dm = lax.dot_general(dx_ref[...], w_ref[...], NT, preferred_element_type=F32)
        dmc_ref[...] = dm[:, aw:]
        for j in range(n_ga):
            cols = slice(j * COLT, (j + 1) * COLT)
            ga = ga_refs[j][...]
            sig = _sigmoid(ga)
            dma = dm[:, cols]
            do_ref[:, cols] = (dma * (ga * sig)).astype(BF16)
            dga_ref[:, cols] = ((dma * at_ref[:, cols]) * (sig * (1.0 + ga * (1.0 - sig)))).astype(BF16)

    row = lambda width: pl.BlockSpec((tm, width), lambda i: (i, 0))
    ga_specs = [pl.BlockSpec((tm, COLT), functools.partial(lambda i, j: (i, ga_off + j), j=j)) for j in range(n_ga)]
    return _call_after(body, after, name="out_proj_bwd_gate", grid=(s // tm,),
                       in_specs=[row(d), pl.BlockSpec((d, d), lambda i: (0, 0), pipeline_mode=pl.Buffered(1)),
                                 row(aw)] + ga_specs,
                       out_specs=[row(aw), row(aw), row(d - aw)],
                       out_shape=[jax.ShapeDtypeStruct((s, aw), BF16), jax.ShapeDtypeStruct((s, aw), BF16),
                                  jax.ShapeDtypeStruct((s, d - aw), F32)],
                       compiler_params=_params(("parallel",)))(d_x1b, wo, attn, *([z] * n_ga))


def attn_bwd(qs, ks, vb, d_o, sinks, aw, after=()):
    s = qs.shape[0]
    nb = s // BLOCK
    nq = aw // HEAD_DIM
    grp_sz = nq // N_KV_HEADS

    def body(q_ref, ko_ref, kp_ref, vo_ref, vp_ref, do_ref, sink_ref, dq_ref, dk_ref, dv_ref, ds_ref,
             ck_ref, cv_ref):
        n = pl.program_id(0)

        @pl.when(n == 0)
        def _():
            ds_ref[...] = jnp.zeros_like(ds_ref)
            dk_ref[...] = jnp.zeros_like(dk_ref)
            dv_ref[...] = jnp.zeros_like(dv_ref)

        @pl.when(n < nb)
        def _():
            lane_idx = lax.broadcasted_iota(jnp.int32, (2 * BLOCK, LANES), 1)
            lane_row = lax.broadcasted_iota(jnp.int32, (1, LANES), 1)
            kpl = _placed(jnp.concatenate([kp_ref[...], ko_ref[...]], axis=0), lane_idx)
            vpl = _placed(jnp.concatenate([vp_ref[...], vo_ref[...]], axis=0), lane_idx)
            valid = _valid_mask(n)
            ds_row = jnp.zeros((1, LANES), F32)
            wide = 2 * BLOCK
            groups = range(nq // 2)
            per_kv = grp_sz // 2
            kcat = [_stack_halves(kpl, kh) for kh in range(N_KV_HEADS)]
            vcat = [_stack_halves(vpl, kh) for kh in range(N_KV_HEADS)]
            qg = [q_ref[:, c * LANES:(c + 1) * LANES] for c in groups]
            dog = [do_ref[:, c * LANES:(c + 1) * LANES] for c in groups]
            scs = [lax.dot_general(qg[c], kcat[c // per_kv], NT, preferred_element_type=F32) for c in groups]
            dps = [lax.dot_general(dog[c], vcat[c // per_kv], NT, preferred_element_type=F32) for c in groups]
            ds_pairs, p_pairs = [], []
            for c in groups:
                probs, dss = [], []
                for half in (0, 1):
                    h = 2 * c + half
                    cols = slice(half * wide, (half + 1) * wide)
                    p, p_sink = _softmax_with_sink(scs[c][:, cols], valid, sink_ref[0, h])
                    delta = jnp.sum(p * dps[c][:, cols], axis=-1, keepdims=True)
                    dss.append((p * (dps[c][:, cols] - delta)).astype(BF16))
                    probs.append(p.astype(BF16))
                    dsink = -jnp.sum(p_sink * delta, axis=0, keepdims=True)
                    ds_row += jnp.where(lane_row == h, dsink, 0.0)
                ds_pairs.append(jnp.concatenate(dss, axis=1))
                p_pairs.append(jnp.concatenate(probs, axis=1))
            for c in groups:
                dq_ref[:, c * LANES:(c + 1) * LANES] = jnp.dot(ds_pairs[c], kcat[c // per_kv],
                                                               preferred_element_type=F32)
            dkts = [lax.dot_general(qg[c], ds_pairs[c], TN, preferred_element_type=F32) for c in groups]
            dvts = [lax.dot_general(dog[c], p_pairs[c], TN, preferred_element_type=F32) for c in groups]
            dkt, dvt = [], []
            for kh in range(N_KV_HEADS):
                for parts, out in ((dkts, dkt), (dvts, dvt)):
                    tot = parts[kh * per_kv]
                    for c in range(kh * per_kv + 1, (kh + 1) * per_kv):
                        tot = tot + parts[c]
                    out.append(tot[:HEAD_DIM, :wide] + tot[HEAD_DIM:, wide:])
            ds_ref[0:1, :] += ds_row
            back = lambda t: jnp.concatenate(
                [jnp.concatenate(t[2 * g:2 * g + 2], axis=0).T for g in range(N_KV_HEADS // 2)], axis=1)
            dk_band, dv_band = back(dkt), back(dvt)

            @pl.when(n > 0)
            def _():
                dk_ref[...] = ck_ref[...] + dk_band[:BLOCK]
                dv_ref[...] = cv_ref[...] + dv_band[:BLOCK]

            ck_ref[...] = dk_band[BLOCK:]
            cv_ref[...] = dv_band[BLOCK:]

        @pl.when(n == nb)
        def _():
            dk_ref[...] = ck_ref[...]
            dv_ref[...] = cv_ref[...]

    own = lambda n: (jnp.minimum(n, nb - 1), 0)
    prev = lambda n: (jnp.clip(n - 1, 0, nb - 1), 0)
    done = lambda n: (jnp.maximum(n - 1, 0), 0)
    kvs = lambda imap: pl.BlockSpec((BLOCK, KV_WIDTH), imap)
    return _call_after(body, after, name="attn_bwd", grid=(nb + 1,),
                       in_specs=[pl.BlockSpec((BLOCK, aw), own), kvs(own), kvs(prev), kvs(own), kvs(prev),
                                 pl.BlockSpec((BLOCK, aw), own), pl.BlockSpec(memory_space=pltpu.SMEM)],
                       out_specs=[pl.BlockSpec((BLOCK, aw), own), kvs(done), kvs(done),
                                  pl.BlockSpec((SUBLANES, LANES), lambda n: (0, 0))],
                       out_shape=[jax.ShapeDtypeStruct((s, aw), F32), jax.ShapeDtypeStruct((s, KV_WIDTH), F32),
                                  jax.ShapeDtypeStruct((s, KV_WIDTH), F32),
                                  jax.ShapeDtypeStruct((SUBLANES, LANES), F32)],
                       scratch_shapes=[pltpu.VMEM((BLOCK, KV_WIDTH), F32), pltpu.VMEM((BLOCK, KV_WIDTH), F32)],
                       compiler_params=_params(("arbitrary",)))(qs, ks, ks, vb, vb, d_o, sinks)


def conv_fwd(z, conv_w, mix, aw, cw, b_off):
    s = z.shape[0]
    tm = _tile(s, 1024)
    nseg = cw // COLT
    hb = tm // SUBLANES

    def body(b_ref, c_ref, h_ref, g_ref, cp_ref, hp_ref, w_ref, mix_in, mix_ref):
        i = pl.program_id(0)
        u = c_ref[...] * h_ref[...]
        up = jnp.where(i > 0, cp_ref[...] * hp_ref[...], 0.0)
        ext = jnp.concatenate([up, u], axis=0)
        um1 = pltpu.roll(ext, 1, 0)[SUBLANES:]
        um2 = pltpu.roll(ext, 2, 0)[SUBLANES:]
        w = w_ref[...]
        cv = w[0:1] * um2 + w[1:2] * um1 + w[2:3] * u
        g = g_ref[...]
        mix_ref[...] = ((b_ref[...] * cv) * (g * _sigmoid(g))).astype(BF16)

    seg = lambda k: pl.BlockSpec((tm, COLT), functools.partial(lambda i, j, k: (i, b_off + k * nseg + j), k=k))
    halo = lambda k: pl.BlockSpec(
        (SUBLANES, COLT), functools.partial(lambda i, j, k: (jnp.maximum(i * hb - 1, 0), b_off + k * nseg + j), k=k))
    return _call(body, name="conv_fwd", grid=(s // tm, nseg),
                 in_specs=[seg(0), seg(1), seg(2), seg(3), halo(1), halo(2),
                           pl.BlockSpec((SUBLANES, COLT), lambda i, j: (0, j)), _hbm()],
                 out_specs=pl.BlockSpec((tm, COLT), lambda i, j: (i, aw // COLT + j)),
                 out_shape=jax.ShapeDtypeStruct(mix.shape, BF16),
                 input_output_aliases={7: 0},
                 compiler_params=_params(("parallel", "parallel")))(z, z, z, z, z, z, conv_w, mix)


def conv_bwd(z, d_mix, conv_w, dz, aw, cw, b_off):
    s = z.shape[0]
    tm = _tile(s, 512)
    nseg = cw // COLT
    hb = tm // SUBLANES
    n_row = s // tm
    last_h = s // SUBLANES - 1
    n_step = nseg * n_row

    def body(b_ref, c_ref, h_ref, g_ref, cp_ref, hp_ref, bn_ref, gn_ref, dm_ref, dmn_ref, w_ref, dz_in,
             dz_ref, acc_ref, stash_ref, sems):
        j = pl.program_id(0)
        i = pl.program_id(1)
        step = j * n_row + i
        slot = step % 2

        def copies(from_slot, at_step):
            jj, ii = at_step // n_row, at_step % n_row
            rows = pl.ds(pl.multiple_of(ii * tm, tm), tm)
            return [pltpu.make_async_copy(
                stash_ref.at[from_slot, q],
                dz_ref.at[rows, pl.ds(pl.multiple_of((b_off + q * nseg + jj) * COLT, COLT), COLT)],
                sems.at[from_slot, q]) for q in range(4)]

        @pl.when(step >= 2)
        def _():
            for cp in copies(slot, step - 2):
                cp.wait()

        cc, hh, bb, g = c_ref[...], h_ref[...], b_ref[...], g_ref[...]
        w = w_ref[...]
        u = cc * hh
        up = jnp.where(i > 0, cp_ref[...] * hp_ref[...], 0.0)
        ext = jnp.concatenate([up, u], axis=0)
        um1 = pltpu.roll(ext, 1, 0)[SUBLANES:]
        um2 = pltpu.roll(ext, 2, 0)[SUBLANES:]
        cv = w[0:1] * um2 + w[1:2] * um1 + w[2:3] * u
        sig = _sigmoid(g)
        sg = g * sig
        dm = dm_ref[...]
        d_cv = (dm * bb) * sg
        gn = gn_ref[...]
        d_cv_next = jnp.where(i < n_row - 1, (dmn_ref[...] * bn_ref[...]) * (gn * _sigmoid(gn)), 0.0)
        ext2 = jnp.concatenate([d_cv, d_cv_next], axis=0)
        dp1 = pltpu.roll(ext2, tm + SUBLANES - 1, 0)[:tm]
        dp2 = pltpu.roll(ext2, tm + SUBLANES - 2, 0)[:tm]
        d_u = w[2:3] * d_cv + w[1:2] * dp1 + w[0:1] * dp2
        stash_ref[slot, 0] = ((dm * cv) * sg).astype(BF16)
        stash_ref[slot, 1] = (d_u * hh).astype(BF16)
        stash_ref[slot, 2] = (d_u * cc).astype(BF16)
        stash_ref[slot, 3] = (((dm * bb) * cv) * (sig * (1.0 + g * (1.0 - sig)))).astype(BF16)
        for cp in copies(slot, step):
            cp.start()

        @pl.when(i == 0)
        def _():
            acc_ref[...] = jnp.zeros_like(acc_ref)

        acc_ref[0:1, :] += jnp.sum(d_cv * um2, axis=0, keepdims=True)
        acc_ref[1:2, :] += jnp.sum(d_cv * um1, axis=0, keepdims=True)
        acc_ref[2:3, :] += jnp.sum(d_cv * u, axis=0, keepdims=True)

        @pl.when(step == n_step - 1)
        def _():
            if n_step >= 2:
                for cp in copies(1 - slot, step - 1):
                    cp.wait()
            for cp in copies(slot, step):
                cp.wait()

    seg = lambda q: pl.BlockSpec((tm, COLT), functools.partial(lambda j, i, q: (i, b_off + q * nseg + j), q=q))
    halo_p = lambda q: pl.BlockSpec(
        (SUBLANES, COLT),
        functools.partial(lambda j, i, q: (jnp.maximum(i * hb - 1, 0), b_off + q * nseg + j), q=q))
    halo_n = lambda q: pl.BlockSpec(
        (SUBLANES, COLT),
        functools.partial(lambda j, i, q: (jnp.minimum((i + 1) * hb, last_h), b_off + q * nseg + j), q=q))
    return _call(body, name="conv_bwd", grid=(nseg, n_row),
                 in_specs=[seg(0), seg(1), seg(2), seg(3), halo_p(1), halo_p(2), halo_n(0), halo_n(3),
                           pl.BlockSpec((tm, COLT), lambda j, i: (i, j)),
                           pl.BlockSpec((SUBLANES, COLT), lambda j, i: (jnp.minimum((i + 1) * hb, last_h), j)),
                           pl.BlockSpec((SUBLANES, COLT), lambda j, i: (0, j)), _hbm()],
                 out_specs=[_hbm(), pl.BlockSpec((SUBLANES, COLT), lambda j, i: (0, j))],
                 out_shape=[jax.ShapeDtypeStruct(dz.shape, BF16), jax.ShapeDtypeStruct((SUBLANES, cw), F32)],
                 input_output_aliases={11: 0},
                 scratch_shapes=[pltpu.VMEM((2, 4, tm, COLT), BF16), pltpu.SemaphoreType.DMA((2, 4))],
                 compiler_params=_params(("arbitrary", "arbitrary")))(
                     z, z, z, z, z, z, z, z, d_mix, d_mix, conv_w, dz)


def _place():
    x, y, c = lax.axis_index("x"), lax.axis_index("y"), lax.axis_index("c")
    chips = [(1 - x, y), (x, 1 - y), (1 - x, 1 - y)]
    return x, y, c, chips


def _remote(src, dst, send_sem, recv_sem, device):
    return pltpu.make_async_remote_copy(src_ref=src, dst_ref=dst, send_sem=send_sem, recv_sem=recv_sem,
                                        device_id=device, device_id_type=MESH)


def plan_gather_ici(n_split):
    def plan(refs, send, recv):
        x, y, c, chips = _place()
        me = 2 * x + y
        mine, theirs = [], []
        for w, ref in enumerate(refs):
            for j, (cx, cy) in enumerate(chips):
                def part(slot):
                    if w >= n_split:
                        return ref.at[slot]
                    hr = ref.shape[1] // 2
                    return ref.at[slot, pl.ds(c * hr, hr)]
                k = 3 * w + j
                mine.append(_remote(part(me), part(me), send.at[k], recv.at[k], (cx, cy, c)))
                theirs.append(_remote(part(2 * cx + cy), part(2 * cx + cy), send.at[k], recv.at[k], (cx, cy, c)))
        return mine, theirs
    return plan


def plan_shard_ici(j):
    def plan(refs, send, recv):
        _, _, c, chips = _place()
        own, land = refs
        hr = own.shape[0] // 2
        rows = pl.ds(c * hr, hr)
        cp = _remote(own.at[rows], land.at[rows], send.at[0], recv.at[0], (*chips[j], c))
        return [cp], [cp]
    return plan


def plan_shard_relay(refs, send, recv):
    _, _, c, chips = _place()
    land_x, land_y, land_far = refs
    hr = land_far.shape[0] // 2
    quarter = lambda q: pl.ds(c * hr + q * (hr // 2), hr // 2)
    mine = [_remote(land_y.at[quarter(0)], land_far.at[quarter(0)], send.at[0], recv.at[0], (*chips[0], c)),
            _remote(land_x.at[quarter(1)], land_far.at[quarter(1)], send.at[1], recv.at[1], (*chips[1], c))]
    return mine, mine


def plan_shard_pass(refs, send, recv):
    x, y, c, _ = _place()
    mine, theirs = [], []
    for w, land in enumerate(refs):
        hr = land.shape[0] // 2
        half = lambda core: land.at[pl.ds(core * hr, hr)]
        mine.append(_remote(half(c), half(c), send.at[w], recv.at[w], (x, y, 1 - c)))
        theirs.append(_remote(half(1 - c), half(1 - c), send.at[w], recv.at[w], (x, y, 1 - c)))
    return mine, theirs


def plan_gather_pass(refs, send, recv):
    x, y, c, chips = _place()
    mine, theirs = [], []
    for w, ref in enumerate(refs):
        hr = ref.shape[1] // 2
        for j, (cx, cy) in enumerate(chips):
            half = lambda core: ref.at[2 * cx + cy, pl.ds(core * hr, hr)]
            k = 3 * w + j
            mine.append(_remote(half(c), half(c), send.at[k], recv.at[k], (x, y, 1 - c)))
            theirs.append(_remote(half(1 - c), half(1 - c), send.at[k], recv.at[k], (x, y, 1 - c)))
    return mine, theirs


def plan_swap(refs, send, recv):
    x, y, c, _ = _place()
    nw = len(refs) // 2
    mine = []
    for w in range(nw):
        hr = refs[w].shape[1] // 2
        mine.append(_remote(refs[w].at[:, pl.ds((1 - c) * hr, hr), :], refs[nw + w], send.at[w], recv.at[w],
                            (x, y, 1 - c)))
    return mine, mine


def plan_scatter(refs, send, recv):
    x, y, c, chips = _place()
    me = 2 * x + y
    nw = len(refs) // 2
    mine, theirs = [], []
    for w in range(nw):
        for j, (cx, cy) in enumerate(chips):
            k = 3 * w + j
            mine.append(_remote(refs[w].at[2 * cx + cy], refs[nw + w].at[me], send.at[k], recv.at[k], (cx, cy, c)))
            theirs.append(_remote(refs[w].at[me], refs[nw + w].at[2 * cx + cy], send.at[k], recv.at[k], (cx, cy, c)))
    return mine, theirs


def plan_join(refs, send, recv):
    x, y, c, _ = _place()
    mine, theirs = [], []
    for w, ref in enumerate(refs):
        hr = ref.shape[0] // 2
        half = lambda core: ref.at[pl.ds(core * hr, hr)]
        mine.append(_remote(half(c), half(c), send.at[w], recv.at[w], (x, y, 1 - c)))
        theirs.append(_remote(half(1 - c), half(1 - c), send.at[w], recv.at[w], (x, y, 1 - c)))
    return mine, theirs


def exchange(name, plan, arrays, n, after=()):
    na = len(arrays)

    def body(*refs):
        mine, theirs = plan(refs[:na], refs[2 * na], refs[2 * na + 1])
        for cp in mine:
            cp.start()
        for cp in theirs:
            cp.wait_recv()
        for cp in mine:
            cp.wait_send()

    return _call_after(body, after, name=name, in_specs=[_hbm()] * na, out_specs=[_hbm()] * na,
                       out_shape=[jax.ShapeDtypeStruct(a.shape, a.dtype) for a in arrays],
                       input_output_aliases={i: i for i in range(na)},
                       scratch_shapes=[pltpu.SemaphoreType.DMA((n,)), pltpu.SemaphoreType.DMA((n,))])(*arrays)


def exchange_start(name, groups, arrays, after=()):
    na, ng = len(arrays), len(groups)

    def body(*refs):
        for g, (plan, idx, _) in enumerate(groups):
            mine, _ = plan([refs[i] for i in idx], refs[na + 2 * g], refs[na + 2 * g + 1])
            for cp in mine:
                cp.start()
        refs[-1][...] = jnp.zeros_like(refs[-1])

    hbm = pl.BlockSpec(memory_space=pltpu.HBM)
    sem = pl.BlockSpec(memory_space=pltpu.SEMAPHORE)
    sem_types = [pltpu.SemaphoreType.DMA((n,)) for _, _, n in groups for _ in (0, 1)]
    outs = _call_after(body, after, name=name, in_specs=[hbm] * na,
                       out_specs=[sem] * (2 * ng) + [hbm] * na + [pl.BlockSpec(memory_space=pltpu.VMEM)],
                       out_shape=sem_types + [pltpu.HBM(a.shape, a.dtype) for a in arrays]
                       + [jax.ShapeDtypeStruct((SUBLANES, LANES), F32)],
                       input_output_aliases={i: i + 2 * ng for i in range(na)},
                       compiler_params=pltpu.CompilerParams(
                           has_side_effects=pltpu.SideEffectType.DATAFLOW_SIDE_EFFECTING))(
                               *[pltpu.with_memory_space_constraint(a, pltpu.HBM) for a in arrays])
    sems = [(outs[2 * g], outs[2 * g + 1]) for g in range(ng)]
    return sems, list(outs[2 * ng:-1]), outs[-1]


def exchange_wait(name, plan, sems, arrays, after):
    send_sems, recv_sems = sems
    na = len(arrays)
    after = tuple(after) if isinstance(after, (tuple, list)) else (after,)

    def body(*refs):
        mine, theirs = plan(refs[:na], refs[na], refs[na + 1])
        for cp in mine:
            cp.wait_send()
        for cp in theirs:
            cp.wait_recv()

    hbm = pl.BlockSpec(memory_space=pltpu.HBM)
    sem = pl.BlockSpec(memory_space=pltpu.SEMAPHORE)
    return _call(body, name=name, in_specs=[hbm] * na + [sem, sem] + [_hbm()] * len(after),
                 out_specs=[hbm] * na, out_shape=[pltpu.HBM(a.shape, a.dtype) for a in arrays],
                 input_output_aliases={i: i for i in range(na)},
                 compiler_params=pltpu.CompilerParams(
                     has_side_effects=pltpu.SideEffectType.DATAFLOW_SIDE_EFFECTING))(
                         *arrays, send_sems, recv_sems, *after)


def plan_allgather_small(refs, send, recv):
    x, y, c, _ = _place()
    buf, = refs
    mine, theirs = [], []
    flips = [(fx, fy, fc) for fx in (0, 1) for fy in (0, 1) for fc in (0, 1)][1:]
    for k, (fx, fy, fc) in enumerate(flips):
        peer = (x ^ fx, y ^ fy, c ^ fc)
        slot = lambda px, py, pc: buf.at[4 * px + 2 * py + pc]
        mine.append(_remote(slot(x, y, c), slot(x, y, c), send.at[k], recv.at[k], peer))
        theirs.append(_remote(slot(*peer), slot(*peer), send.at[k], recv.at[k], peer))
    return mine, theirs


def add_sibling(grad, got, core, name):
    _, r, c = grad.shape
    hr = r // 2
    tr = _tile(hr, 512)
    nblk = hr // tr

    def body(core_ref, g_ref, o_ref, out_ref):
        out_ref[...] = (g_ref[...].astype(F32) + o_ref[...].astype(F32)).astype(BF16)

    grid_spec = pltpu.PrefetchScalarGridSpec(
        num_scalar_prefetch=1, grid=(N_CHIPS, nblk),
        in_specs=[pl.BlockSpec((None, tr, c), lambda t, i, core_ref: (t, core_ref[0] * nblk + i, 0)),
                  pl.BlockSpec((None, tr, c), lambda t, i, core_ref: (t, i, 0))],
        out_specs=pl.BlockSpec((None, tr, c), lambda t, i, core_ref: (t, i, 0)))
    return _call(body, name=name, grid_spec=grid_spec,
                 out_shape=jax.ShapeDtypeStruct((N_CHIPS, hr, c), BF16),
                 compiler_params=_params(("parallel", "parallel")))(core, grad, got)


def sum_chips(mine, owned, place, name):
    _, hr, c = mine.shape
    tr = _tile(hr, 512)
    nblk = hr // tr

    def body(place_ref, m_ref, o1_ref, o2_ref, o3_ref, out_ref):
        acc = m_ref[...].astype(F32)
        for o_ref in (o1_ref, o2_ref, o3_ref):
            acc = acc + o_ref[...].astype(F32)
        out_ref[...] = acc

    other = lambda k: pl.BlockSpec((None, tr, c), lambda i, place_ref: ((place_ref[0] + k) % N_CHIPS, i, 0))
    grid_spec = pltpu.PrefetchScalarGridSpec(
        num_scalar_prefetch=1, grid=(nblk,),
        in_specs=[other(0), other(1), other(2), other(3)],
        out_specs=pl.BlockSpec((tr, c), lambda i, place_ref: (place_ref[1] * nblk + i, 0)))
    return _call(body, name=name, grid_spec=grid_spec,
                 out_shape=jax.ShapeDtypeStruct((2 * hr, c), F32),
                 compiler_params=_params(("parallel",)))(place, mine, owned, owned, owned)


def sum_devices(gathered):
    _, rows, width = gathered.shape

    def body(g_ref, out_ref):
        acc = g_ref[0]
        for dev in range(1, 8):
            acc = acc + g_ref[dev]
        out_ref[...] = acc
        tail = acc[SUBLANES:]
        out_ref[SUBLANES:, :] = jnp.broadcast_to(jnp.sum(tail, axis=1, keepdims=True), tail.shape)

    return _call(body, name="sum_devices",
                 in_specs=[pl.BlockSpec(memory_space=pltpu.VMEM)],
                 out_specs=pl.BlockSpec(memory_space=pltpu.VMEM),
                 out_shape=jax.ShapeDtypeStruct((rows, width), F32))(gathered)


def _rope_tables(s):
    half = ROT_DIM // 2
    inv_freq = jnp.power(jnp.float32(ROPE_THETA), -jnp.arange(half, dtype=F32) * 2.0 / ROT_DIM)
    freq64 = jnp.concatenate([inv_freq, inv_freq, jnp.zeros((HEAD_DIM - ROT_DIM,), F32)])
    freq = jnp.concatenate([freq64, freq64])[None, :]
    dim = (jnp.arange(LANES) % HEAD_DIM)[None, :]
    ang = jnp.arange(s).astype(F32)[:, None] * freq
    cos, sin = jnp.cos(ang), jnp.sin(ang)
    return cos, jnp.where(dim < half, -sin, 0.0), jnp.where((dim >= half) & (dim < ROT_DIM), sin, 0.0)


def _pad_rows(a, rows):
    return jnp.pad(a, ((0, rows - a.shape[0]), (0, 0)))


def _pad_cols(a, cols):
    return jnp.pad(a, ((0, 0), (0, cols - a.shape[1])))


def kernel(x, p, norm_gain, w_in, q_norm_gain, k_norm_gain, attn_sinks, conv_w, w_out, ple_gate_norm_gain, w_ple_gate, b_ple_gate, w_ple_proj, ple_norm_gain, loss_target, m_norm_gain, m_w_in, m_q_norm_gain, m_k_norm_gain, m_attn_sinks, m_conv_w, m_w_out, m_ple_gate_norm_gain, m_w_ple_gate, m_b_ple_gate, m_w_ple_proj, m_ple_norm_gain, v_norm_gain, v_w_in, v_q_norm_gain, v_k_norm_gain, v_attn_sinks, v_conv_w, v_w_out, v_ple_gate_norm_gain, v_w_ple_gate, v_b_ple_gate, v_w_ple_proj, v_ple_norm_gain):
    x2, p2, tgt = x[0], p[0, 0], loss_target[0]
    s, d = x2.shape
    ple = p2.shape[1]
    aw = d // 2
    cw = d - aw
    nq = aw // HEAD_DIM
    sh = w_in.shape[2]
    in_w = N_CHIPS * sh
    dq = d // N_CHIPS
    cq = cw // N_CHIPS
    ga_off = (aw + 2 * KV_WIDTH) // COLT
    b_off = (2 * aw + 2 * KV_WIDTH) // COLT
    assert in_w == 2 * aw + 2 * KV_WIDTH + 4 * cw and aw % COLT == 0 and cw % COLT == 0
    assert s % BLOCK == 0 and sh % LANES == 0 and nq % (2 * N_KV_HEADS) == 0

    core = lax.axis_index("c").astype(jnp.int32).reshape(1)
    chip = 2 * lax.axis_index("x") + lax.axis_index("y")

    chip1 = chip.astype(jnp.int32).reshape(1)
    place = jnp.concatenate([chip1, core])
    w_in_own = cast_bf16(w_in[0], "cast_w_in")
    rest = [cast_into_slot(w_out[0], chip1, "cast_w_out"), cast_into_slot(w_ple_gate[0], chip1, "cast_w_pg"),
            cast_into_slot(w_ple_proj[0], chip1, "cast_w_pp"),
            lax.dynamic_update_slice(jnp.zeros((N_CHIPS, SUBLANES, cq), F32),
                                     _pad_rows(conv_w[0], SUBLANES)[None], (chip, 0, 0))]
    order = jnp.stack([chip, chip ^ 2, chip ^ 1, chip ^ 3]).astype(jnp.int32)
    wi_sems, wi_arrs, wi_token = exchange_start(
        "gather_wi_start", [(plan_shard_ici(j), [0, 1 + j], 1) for j in range(2)],
        [w_in_own] + [lax.empty((d, sh), BF16) for _ in range(3)])

    tn = _tile(d, 1024)
    ts = _tile(s, 2048)
    tabs = _rope_tables(s)
    qg = jnp.tile(q_norm_gain, (1, LANES // HEAD_DIM))
    kg = jnp.tile(k_norm_gain, (1, LANES // HEAD_DIM))
    seg_i = jnp.arange(SEG) // HEAD_DIM
    segb = (seg_i[:, None] == seg_i[None, :]).astype(BF16)
    h, z = norm_in_proj(x2, norm_gain, wi_arrs[0], order, in_w)
    own, land_x = exchange_wait("gather_wi_wait0", plan_shard_ici(0), wi_sems[0], [wi_arrs[0], wi_arrs[1]],
                                (z,) + tuple(tabs) + tuple(rest[:2]))
    own, land_y = exchange_wait("gather_wi_wait1", plan_shard_ici(1), wi_sems[1], [own, wi_arrs[2]], land_x)
    (relay_sems, rest_sems), started, rest_token = exchange_start(
        "gather_relay_rest_start", [(plan_shard_relay, [0, 1, 2], 2), (plan_gather_ici(3), [3, 4, 5, 6], 12)],
        [land_x, land_y, wi_arrs[3]] + rest)
    relayed, rest = started[:3], started[3:]
    land_x, land_y = exchange("gather_wi_pass01", plan_shard_pass, relayed[:2], 2, after=(rest_token,))
    z = in_proj_part(h, land_x, z, order, 1, in_w)
    z = in_proj_part(h, land_y, z, order, 2, in_w)
    land_x, land_y, land_far = exchange_wait("gather_wi_relay_wait", plan_shard_relay, relay_sems,
                                             [land_x, land_y, relayed[2]], z)
    land_far, = exchange("gather_wi_pass2", plan_shard_pass, [land_far], 1)
    z = in_proj_part(h, land_far, z, order, 3, in_w)
    w_shards = [own, land_x, land_y, land_far]
    wo_all, wg_all, wp_all, conv_all = exchange_wait("gather_rest_wait", plan_gather_ici(3), rest_sems, rest, z)
    pass_sems, passed, pass_token = exchange_start(
        "gather_rest_pass_start", [(plan_gather_pass, [0, 1, 2], 9)], [wo_all, wg_all, wp_all])
    conv_full = conv_all.transpose(1, 0, 2).reshape(SUBLANES, cw)
    qs, ks, vb = qk_prep_fwd(z, tabs, qg, kg, segb, aw, after=(pass_token,))
    attn, mix = attn_fwd(qs, ks, vb, z, attn_sinks, aw, d, ga_off)
    mix = conv_fwd(z, conv_full, mix, aw, cw, b_off)
    wo_all, wg_all, wp_all = exchange_wait("gather_rest_pass_wait", plan_gather_pass, pass_sems[0], passed, mix)
    wo_full = wo_all.reshape(d, d)
    wg_full = wg_all.reshape(d, d)
    x1, hg = out_proj_norm(mix, wo_full, x2, ple_gate_norm_gain)
    pq = d // N_CHIPS

    d_gl, d_pe, dy, acc_head = head_fwd_bwd(x1, hg, wg_full, p2, wp_all, tgt, b_ple_gate, ple_norm_gain)
    wgrad = lambda a_cols, shape3, o_spec, b_cols, name, a, b, grid: matmul(
        a, b, grid=grid,
        a_spec=pl.BlockSpec((ts, a_cols), lambda i, j, k: (k, i)),
        b_spec=pl.BlockSpec((ts, b_cols), lambda i, j, k: (k, j)),
        o_spec=o_spec, out_shape=jax.ShapeDtypeStruct(shape3, BF16), dims=TN, name=name)
    g_wg = wgrad(tn, (d, d), pl.BlockSpec((tn, tn), lambda i, j, k: (i, j)), tn, "mm_g_wg", hg, d_gl,
                 (d // tn, d // tn, s // ts))
    g_wp = wgrad(ple, (N_CHIPS, ple, pq), pl.BlockSpec((None, ple, pq), lambda i, j, k: (j, 0, 0)), pq,
                 "mm_g_wp", p2, d_pe, (1, N_CHIPS, s // ts))
    d_x1, d_x1b, acc_g = gate_proj_bwd_norm(d_gl, wg_full, x1, dy, ple_gate_norm_gain)
    g_wo = wgrad(tn, (d, d), pl.BlockSpec((tn, tn), lambda i, j, k: (i, j)), tn, "mm_g_wo", mix, d_x1b,
                 (d // tn, d // tn, s // ts))
    def reduce_sum(tag, handle, after):
        sems, arrays, _ = handle
        n = len(arrays) // 2
        got = exchange_wait("scatter_%s_wait" % tag, plan_scatter, sems[0], arrays, after)
        return [sum_chips(pt, o, place, "sum_chips_%s%d" % (tag, i))
                for i, (pt, o) in enumerate(zip(got[:n], got[n:]))]

    early_grads = [g_wo.reshape(N_CHIPS, dq, d), g_wg.reshape(N_CHIPS, dq, d), g_wp]
    eswap_sems, eswapping, eswap_token = exchange_start(
        "swap_early_start", [(plan_swap, list(range(6)), 3)],
        early_grads + [lax.empty((N_CHIPS, a.shape[1] // 2, a.shape[2]), BF16) for a in early_grads])
    d_o, d_gate, d_mix_conv = out_proj_bwd_gate(d_x1b, wo_full, attn, z, aw, ga_off, after=(eswap_token,))
    eswapped = exchange_wait("swap_early_wait", plan_swap, eswap_sems[0], eswapping, d_o)
    early_parts = [add_sibling(g, o, core, "add_sibling_early%d" % i)
                   for i, (g, o) in enumerate(zip(eswapped[:3], eswapped[3:]))]
    early = exchange_start("scatter_early_start", [(plan_scatter, list(range(6)), 9)],
                           early_parts + [lax.empty(a.shape, BF16) for a in early_parts])
    dqs, dks, dvs, acc_sink = attn_bwd(qs, ks, vb, d_o, attn_sinks, aw, after=(early[-1],))
    dz, acc_qk = qk_prep_bwd(z, dqs, dks, dvs, d_gate, tabs, qg, kg, segb, aw)
    dz, acc_conv = conv_bwd(z, d_mix_conv, conv_full, dz, aw, cw, b_off)
    join_sems, joining, join_token = exchange_start(
        "join_early_start", [(plan_join, [0, 1, 2], 3)], reduce_sum("early", early, dz))
    g_send = in_proj_wgrad_half(h, dz, None, 1 - core, after=(join_token,))
    swap_sems, swapping, swap_token = exchange_start(
        "swap_late_start", [(plan_swap, [0, 1], 1)], [g_send, lax.empty((N_CHIPS, d // 2, sh), BF16)])
    g_wi = in_proj_wgrad_half(h, dz, swapping[0], core, after=(swap_token,))
    full_wo, full_wg, full_wp = exchange_wait("join_early_wait", plan_join, join_sems[0], joining, g_wi)
    g_wi, got_wi = exchange_wait("swap_late_wait", plan_swap, swap_sems[0], [g_wi, swapping[1]], full_wo)
    part_wi = add_sibling(g_wi, got_wi, core, "add_sibling_late0")
    late = exchange_start("scatter_late_start", [(plan_scatter, [0, 1], 3)],
                          [part_wi, lax.empty(part_wi.shape, BF16)])
    grad_x, acc_x = in_proj_bwd_norm(dz, w_shards, order, x2, d_x1, norm_gain, after=(late[-1],))

    wsm = max(d, cw)
    misc = jnp.concatenate([acc_qk[0:1, :HEAD_DIM], acc_qk[1:2, :HEAD_DIM], acc_sink[0:1, :nq]], axis=1)
    small = jnp.concatenate([
        _pad_cols(acc_x[0:1], wsm), _pad_cols(acc_g[0:1], wsm), _pad_cols(acc_head[0:1], wsm),
        _pad_cols(acc_head[1:2], wsm), _pad_cols(misc, wsm), _pad_cols(acc_conv[0:3], wsm),
        _pad_rows(_pad_cols(acc_head[2:3], wsm), SUBLANES)], axis=0)
    device = 2 * chip + lax.axis_index("c")
    (small_sems, last_sems), started, last_token = exchange_start(
        "small_join_late_start", [(plan_allgather_small, [0], 7), (plan_join, [1], 1)],
        [lax.dynamic_update_slice(jnp.zeros((8,) + small.shape, F32), small[None], (device, 0, 0))]
        + reduce_sum("late", late, grad_x))
    small_bufs, last_halves = started[:1], started[1:]
    big, after = {}, (last_token,)
    for nm, g, w, m, v in (("w_out", full_wo, w_out, m_w_out, v_w_out),
                           ("w_ple_gate", full_wg, w_ple_gate, m_w_ple_gate, v_w_ple_gate),
                           ("w_ple_proj", full_wp, w_ple_proj, m_w_ple_proj, v_w_ple_proj)):
        stepped = adamw(g, w[0], m[0], v[0], "adamw_" + nm, after=after)
        big[nm], after = [o[None] for o in stepped], (stepped[1],)
    full_wi, = exchange_wait("join_late_wait", plan_join, last_sems, last_halves, after[0])
    big["w_in"] = [o[None] for o in adamw(full_wi, w_in[0], m_w_in[0], v_w_in[0], "adamw_w_in")]

    gathered, = exchange_wait("small_wait", plan_allgather_small, small_sems, small_bufs, big["w_in"][1])
    tot = sum_devices(gathered)
    loss = tot[SUBLANES, 0]

    def pack(vals):
        ng, pg, bg, eg, qgv, kgv, sk, cv = vals
        misc_v = jnp.concatenate([qgv, kgv, sk], axis=1)
        return jnp.concatenate([_pad_cols(ng, wsm), _pad_cols(pg, wsm), _pad_cols(bg, wsm), _pad_cols(eg, wsm),
                                _pad_cols(misc_v, wsm), _pad_cols(cv[0], wsm)], axis=0)

    g_small = jnp.concatenate(
        [tot[0:5], _pad_cols(lax.dynamic_slice(tot[5:8], (0, chip * cq), (3, cq)), wsm)], axis=0)
    w_small = pack((norm_gain, ple_gate_norm_gain, b_ple_gate, ple_norm_gain, q_norm_gain, k_norm_gain,
                    attn_sinks, conv_w))
    m_small = pack((m_norm_gain, m_ple_gate_norm_gain, m_b_ple_gate, m_ple_norm_gain, m_q_norm_gain,
                    m_k_norm_gain, m_attn_sinks, m_conv_w))
    v_small = pack((v_norm_gain, v_ple_gate_norm_gain, v_b_ple_gate, v_ple_norm_gain, v_q_norm_gain,
                    v_k_norm_gain, v_attn_sinks, v_conv_w))
    sm = adamw(g_small, w_small, m_small, v_small, "adamw_small")

    def unpack(a):
        return {"norm_gain": a[0:1, :d], "ple_gate_norm_gain": a[1:2, :d], "b_ple_gate": a[2:3, :d],
                "ple_norm_gain": a[3:4, :d], "q_norm_gain": a[4:5, :HEAD_DIM],
                "k_norm_gain": a[4:5, HEAD_DIM:2 * HEAD_DIM],
                "attn_sinks": a[4:5, 2 * HEAD_DIM:2 * HEAD_DIM + nq], "conv_w": a[5:8, :cq][None]}

    order = ("norm_gain", "w_in", "q_norm_gain", "k_norm_gain", "attn_sinks", "conv_w", "w_out",
             "ple_gate_norm_gain", "w_ple_gate", "b_ple_gate", "w_ple_proj", "ple_norm_gain")
    outs = [loss, grad_x[None]]
    for kind in range(4):
        table = unpack(sm[kind])
        for nm in order:
            outs.append(big[nm][kind] if nm in big else table[nm])
    return tuple(outs)
```

```python
import functools

import jax
import jax.numpy as jnp
from jax import lax
from jax.experimental import pallas as pl
from jax.experimental.pallas import tpu as pltpu

F32 = jnp.float32
BF16 = jnp.bfloat16
MESH = pl.DeviceIdType.MESH

HEAD_DIM = 64
N_KV_HEADS = 4
KV_WIDTH = N_KV_HEADS * HEAD_DIM
BLOCK = 128
ROT_DIM = 16
ROPE_THETA = 500000.0
EPS = 1e-6
NEG_INF = -1e30
N_CHIPS = 4
LANES = 128
SUBLANES = 8
COLT = 512
SEG = 256
VMEM_LIMIT = 48 * 1024 * 1024
VMEM_LIMIT_BIG = 60 * 1024 * 1024

ADAM_LR = 0.001
ADAM_B1 = 0.9
ADAM_B2 = 0.999
ADAM_EPS = 1e-08
ADAM_WD = 0.01
ADAM_STEP = 10

NN = (((1,), (0,)), ((), ()))
NT = (((1,), (1,)), ((), ()))
TN = (((0,), (0,)), ((), ()))


def _call(body, **kw):
    return pl.pallas_call(body, **kw)


def _call_after(body, after, **kw):
    n_in, n_after = len(kw["in_specs"]), len(after)
    kw["in_specs"] = list(kw["in_specs"]) + [pl.BlockSpec(memory_space=pl.ANY)] * n_after

    def body_after(*refs):
        body(*refs[:n_in], *refs[n_in + n_after:])

    call = _call(body_after, **kw)
    return lambda *args: call(*args, *after)


def _params(sem, vmem=VMEM_LIMIT):
    return pltpu.CompilerParams(dimension_semantics=sem, vmem_limit_bytes=vmem)


def _tile(n, pref):
    return pref if n % pref == 0 else n


def _sigmoid(v):
    return 1.0 / (1.0 + jnp.exp(-v))


def _hbm():
    return pl.BlockSpec(memory_space=pl.ANY)


def cast_bf16(a, name):
    r, c = a.shape
    tr = _tile(r, 512)

    def body(a_ref, o_ref):
        o_ref[...] = a_ref[...].astype(BF16)

    return _call(body, name=name, grid=(r // tr,),
                 in_specs=[pl.BlockSpec((tr, c), lambda i: (i, 0))],
                 out_specs=pl.BlockSpec((tr, c), lambda i: (i, 0)),
                 out_shape=jax.ShapeDtypeStruct((r, c), BF16),
                 compiler_params=_params(("parallel",)))(a)


def cast_into_slot(a, chip, name):
    r, c = a.shape
    tr = _tile(r, 512)

    def body(chip_ref, a_ref, o_ref):
        o_ref[...] = a_ref[...].astype(BF16)

    grid_spec = pltpu.PrefetchScalarGridSpec(
        num_scalar_prefetch=1, grid=(r // tr,),
        in_specs=[pl.BlockSpec((tr, c), lambda i, chip_ref: (i, 0))],
        out_specs=pl.BlockSpec((None, tr, c), lambda i, chip_ref: (chip_ref[0], i, 0)))
    return _call(body, name=name, grid_spec=grid_spec,
                 out_shape=jax.ShapeDtypeStruct((N_CHIPS, r, c), BF16),
                 compiler_params=_params(("parallel",)))(chip, a)


def out_proj_norm(mix, wo, x, gain):
    s, d = x.shape
    tm = _tile(s, 512)

    def body(m_ref, w_ref, x_ref, g_ref, x1_ref, hg_ref):
        x1 = x_ref[...] + jnp.dot(m_ref[...], w_ref[...], preferred_element_type=F32)
        x1_ref[...] = x1
        r = lax.rsqrt(jnp.mean(x1 * x1, axis=-1, keepdims=True) + EPS)
        hg_ref[...] = ((x1 * r) * g_ref[...]).astype(BF16)

    row = pl.BlockSpec((tm, d), lambda i: (i, 0))
    return _call(body, name="out_proj_norm", grid=(s // tm,),
                 in_specs=[row, pl.BlockSpec((d, d), lambda i: (0, 0), pipeline_mode=pl.Buffered(1)), row,
                           pl.BlockSpec((1, d), lambda i: (0, 0))],
                 out_specs=[row, row],
                 out_shape=[jax.ShapeDtypeStruct((s, d), F32), jax.ShapeDtypeStruct((s, d), BF16)],
                 compiler_params=_params(("parallel",), VMEM_LIMIT_BIG))(mix, wo, x, gain)


def gate_proj_bwd_norm(d_gl, wg, xin, add, gain):
    s, d = xin.shape
    tm = _tile(s, 256)

    def body(dg_ref, w_ref, x_ref, a_ref, g_ref, dx_ref, dxb_ref, acc_ref):
        i = pl.program_id(0)
        dyv = lax.dot_general(dg_ref[...], w_ref[...], NT, preferred_element_type=F32)
        xf = x_ref[...]
        r = lax.rsqrt(jnp.mean(xf * xf, axis=-1, keepdims=True) + EPS)
        xhat = xf * r
        gd = dyv * g_ref[...]
        dx = a_ref[...] + r * (gd - xhat * jnp.mean(xhat * gd, axis=-1, keepdims=True))
        dx_ref[...] = dx
        dxb_ref[...] = dx.astype(BF16)

        @pl.when(i == 0)
        def _():
            acc_ref[...] = jnp.zeros_like(acc_ref)

        acc_ref[0:1, :] += jnp.sum(dyv * xhat, axis=0, keepdims=True)

    row = pl.BlockSpec((tm, d), lambda i: (i, 0))
    return _call(body, name="gate_proj_bwd_norm", grid=(s // tm,),
                 in_specs=[row, pl.BlockSpec((d, d), lambda i: (0, 0), pipeline_mode=pl.Buffered(1)), row, row,
                           pl.BlockSpec((1, d), lambda i: (0, 0))],
                 out_specs=[row, row, pl.BlockSpec((SUBLANES, d), lambda i: (0, 0))],
                 out_shape=[jax.ShapeDtypeStruct((s, d), F32), jax.ShapeDtypeStruct((s, d), BF16),
                            jax.ShapeDtypeStruct((SUBLANES, d), F32)],
                 compiler_params=_params(("arbitrary",), VMEM_LIMIT_BIG))(d_gl, wg, xin, add, gain)


def head_fwd_bwd(x1, hg, wg, p, wp, tgt, bias, ple_gain):
    s, d = x1.shape
    tm = _tile(s, 256)
    n_row = s // tm

    def body(x1_ref, hg_ref, wg_ref, p_ref, wp_ref, t_ref, b_ref, g_ref, dgl_ref, dy_ref, gwp_ref, acc_ref,
             gacc_ref):
        i = pl.program_id(0)
        gl = jnp.dot(hg_ref[...], wg_ref[...], preferred_element_type=F32)
        pb = p_ref[...].astype(BF16)
        pev = jnp.concatenate([jnp.dot(pb, wp_ref[q], preferred_element_type=F32) for q in range(N_CHIPS)],
                              axis=1)
        r = lax.rsqrt(jnp.mean(pev * pev, axis=-1, keepdims=True) + EPS)
        pehat = pev * r
        gain = g_ref[...]
        e = pehat * gain
        gate = _sigmoid(gl + b_ref[...])
        diff = (x1_ref[...] + gate * e) - t_ref[...]
        dy = diff * (1.0 / d)
        dy_ref[...] = dy
        d_gl = (dy * e) * (gate * (1.0 - gate))
        dgl_ref[...] = d_gl.astype(BF16)
        d_e = dy * gate
        gd = d_e * gain
        d_pe = r * (gd - pehat * jnp.mean(pehat * gd, axis=-1, keepdims=True))
        g_part = lax.dot_general(pb, d_pe.astype(BF16), TN, preferred_element_type=F32)

        @pl.when(i == 0)
        def _():
            acc_ref[...] = jnp.zeros_like(acc_ref)
            gacc_ref[...] = g_part

        @pl.when(i > 0)
        def _():
            gacc_ref[...] += g_part

        @pl.when(i == n_row - 1)
        def _():
            pq = d // N_CHIPS
            for q in range(N_CHIPS):
                gwp_ref[q] = gacc_ref[:, q * pq:(q + 1) * pq].astype(BF16)

        acc_ref[0:1, :] += jnp.sum(d_gl, axis=0, keepdims=True)
        acc_ref[1:2, :] += jnp.sum(d_e * pehat, axis=0, keepdims=True)
        acc_ref[2:3, :] += jnp.sum(diff * diff, axis=0, keepdims=True) * (0.5 / d)

    row = pl.BlockSpec((tm, d), lambda i: (i, 0))
    vec = pl.BlockSpec((1, d), lambda i: (0, 0))
    ple = p.shape[1]
    return _call(body, name="head_fwd_bwd", grid=(n_row,),
                 in_specs=[row, row, pl.BlockSpec((d, d), lambda i: (0, 0), pipeline_mode=pl.Buffered(1)),
                           pl.BlockSpec((tm, ple), lambda i: (i, 0)),
                           pl.BlockSpec(wp.shape, lambda i: (0, 0, 0)), row, vec, vec],
                 out_specs=[row, row, pl.BlockSpec(wp.shape, lambda i: (0, 0, 0)),
                            pl.BlockSpec((SUBLANES, d), lambda i: (0, 0))],
                 out_shape=[jax.ShapeDtypeStruct((s, d), BF16), jax.ShapeDtypeStruct((s, d), F32),
                            jax.ShapeDtypeStruct(wp.shape, BF16), jax.ShapeDtypeStruct((SUBLANES, d), F32)],
                 scratch_shapes=[pltpu.VMEM((ple, d), F32)],
                 compiler_params=_params(("arbitrary",), VMEM_LIMIT_BIG))(
                     x1, hg, wg, p, wp, tgt, bias, ple_gain)


def adamw(g, w, m, v, name, after=()):
    r, c = g.shape
    tr = _tile(r, 256)

    def body(g_ref, w_ref, m_ref, v_ref, go_ref, d_ref, mo_ref, vo_ref):
        gv = g_ref[...]
        mn = ADAM_B1 * m_ref[...] + (1.0 - ADAM_B1) * gv
        vn = ADAM_B2 * v_ref[...] + (1.0 - ADAM_B2) * (gv * gv)
        m_hat = mn / (1.0 - ADAM_B1 ** ADAM_STEP)
        v_hat = vn / (1.0 - ADAM_B2 ** ADAM_STEP)
        go_ref[...] = gv
        d_ref[...] = -ADAM_LR * (m_hat / (jnp.sqrt(v_hat) + ADAM_EPS) + ADAM_WD * w_ref[...])
        mo_ref[...] = mn
        vo_ref[...] = vn

    blk = pl.BlockSpec((tr, c), lambda i: (i, 0))
    shp = jax.ShapeDtypeStruct((r, c), F32)
    return _call_after(body, after, name=name, grid=(r // tr,), in_specs=[blk] * 4, out_specs=[blk] * 4,
                       out_shape=[shp] * 4, compiler_params=_params(("parallel",)))(g, w, m, v)


def matmul(a, b, *, grid, a_spec, b_spec, o_spec, out_shape, dims, name):
    nk = grid[2]
    acc_shape = tuple(d for d in o_spec.block_shape if d is not None)

    def body(a_ref, b_ref, o_ref, *scratch):
        part = lax.dot_general(a_ref[...].astype(BF16), b_ref[...].astype(BF16), dims, preferred_element_type=F32)
        if nk == 1:
            o_ref[...] = part.astype(o_ref.dtype)
            return
        acc_ref, = scratch
        k = pl.program_id(2)

        @pl.when(k == 0)
        def _():
            acc_ref[...] = part

        @pl.when((k > 0) & (k < nk - 1))
        def _():
            acc_ref[...] += part

        @pl.when(k == nk - 1)
        def _():
            o_ref[...] = (acc_ref[...] + part).astype(o_ref.dtype)

    return _call(body, name=name, grid=grid, in_specs=[a_spec, b_spec], out_specs=o_spec, out_shape=out_shape,
                 scratch_shapes=[pltpu.VMEM(acc_shape, F32)] if nk > 1 else [],
                 compiler_params=_params(("parallel", "parallel", "arbitrary")))(a, b)


def norm_in_proj(x, gain, w, order, in_w):
    s, d = x.shape
    sh = w.shape[1]
    tm = _tile(s, 512)

    def body(order_ref, x_ref, g_ref, w_ref, h_ref, z_ref):
        xf = x_ref[...]
        r = lax.rsqrt(jnp.mean(xf * xf, axis=-1, keepdims=True) + EPS)
        hb = ((xf * r) * g_ref[...]).astype(BF16)
        h_ref[...] = hb
        z_ref[...] = jnp.dot(hb, w_ref[...], preferred_element_type=F32)

    grid_spec = pltpu.PrefetchScalarGridSpec(
        num_scalar_prefetch=1, grid=(s // tm,),
        in_specs=[pl.BlockSpec((tm, d), lambda i, order_ref: (i, 0)),
                  pl.BlockSpec((1, d), lambda i, order_ref: (0, 0)),
                  pl.BlockSpec((d, sh), lambda i, order_ref: (0, 0), pipeline_mode=pl.Buffered(1))],
        out_specs=[pl.BlockSpec((tm, d), lambda i, order_ref: (i, 0)),
                   pl.BlockSpec((tm, sh), lambda i, order_ref: (i, order_ref[0]))])
    return _call(body, name="norm_in_proj", grid_spec=grid_spec,
                 out_shape=[jax.ShapeDtypeStruct((s, d), BF16), jax.ShapeDtypeStruct((s, in_w), F32)],
                 compiler_params=_params(("parallel",)))(order, x, gain, w)


def in_proj_part(h, w, z_prev, order, q, in_w):
    s, d = h.shape
    sh = w.shape[1]
    tm = _tile(s, 512)

    def body(order_ref, h_ref, w_ref, *rest):
        rest[-1][...] = jnp.dot(h_ref[...], w_ref[...], preferred_element_type=F32)

    in_specs = [pl.BlockSpec((tm, d), lambda i, order_ref: (i, 0)),
                pl.BlockSpec((d, sh), lambda i, order_ref: (0, 0))]
    args = [order, h, w]
    if z_prev is not None:
        in_specs.append(_hbm())
        args.append(z_prev)
    grid_spec = pltpu.PrefetchScalarGridSpec(
        num_scalar_prefetch=1, grid=(s // tm,), in_specs=in_specs,
        out_specs=pl.BlockSpec((tm, sh), lambda i, order_ref: (i, order_ref[q])))
    return _call(body, name="mm_z%d" % q, grid_spec=grid_spec,
                 out_shape=jax.ShapeDtypeStruct((s, in_w), F32),
                 input_output_aliases={3: 0} if z_prev is not None else {},
                 compiler_params=_params(("parallel",)))(*args)


def in_proj_wgrad_half(h, dz, g_prev, half, after):
    s, d = h.shape
    sh = dz.shape[1] // N_CHIPS
    hr = d // 2

    def body(half_ref, h_ref, dz_ref, *rest):
        rest[-1][...] = lax.dot_general(h_ref[...], dz_ref[...], TN, preferred_element_type=F32).astype(BF16)

    extra = ([g_prev] if g_prev is not None else []) + list(after)
    grid_spec = pltpu.PrefetchScalarGridSpec(
        num_scalar_prefetch=1, grid=(N_CHIPS,),
        in_specs=[pl.BlockSpec((s, hr), lambda j, half_ref: (0, half_ref[0]), pipeline_mode=pl.Buffered(1)),
                  pl.BlockSpec((s, sh), lambda j, half_ref: (0, j))] + [_hbm()] * len(extra),
        out_specs=pl.BlockSpec((None, hr, sh), lambda j, half_ref: (j, half_ref[0], 0)))
    return _call(body, name="mm_g_wi_%s" % ("keep" if g_prev is not None else "send"), grid_spec=grid_spec,
                 out_shape=jax.ShapeDtypeStruct((N_CHIPS, d, sh), BF16),
                 input_output_aliases={3: 0} if g_prev is not None else {},
                 compiler_params=_params(("parallel",), VMEM_LIMIT_BIG))(half, h, dz, *extra)


def in_proj_bwd_norm(dz, ws, order, xin, add, gain, after):
    s, d = xin.shape
    sh = ws[0].shape[1]
    tm = _tile(s, 256)
    nq, n_after = len(ws), len(after)

    def body(order_ref, *refs):
        a_refs, b_refs = refs[:nq], refs[nq:2 * nq]
        x_ref, add_ref, g_ref = refs[2 * nq:2 * nq + 3]
        dx_ref, acc_ref = refs[2 * nq + 3 + n_after:]
        i = pl.program_id(0)
        dyv = lax.dot_general(a_refs[0][...], b_refs[0][...], NT, preferred_element_type=F32)
        for q in range(1, nq):
            dyv += lax.dot_general(a_refs[q][...], b_refs[q][...], NT, preferred_element_type=F32)
        xf = x_ref[...]
        r = lax.rsqrt(jnp.mean(xf * xf, axis=-1, keepdims=True) + EPS)
        xhat = xf * r
        gd = dyv * g_ref[...]
        dx_ref[...] = add_ref[...] + r * (gd - xhat * jnp.mean(xhat * gd, axis=-1, keepdims=True))

        @pl.when(i == 0)
        def _():
            acc_ref[...] = jnp.zeros_like(acc_ref)

        acc_ref[0:1, :] += jnp.sum(dyv * xhat, axis=0, keepdims=True)

    a_spec = lambda q: pl.BlockSpec((tm, sh), functools.partial(lambda i, order_ref, q: (i, order_ref[q]), q=q))
    row = pl.BlockSpec((tm, d), lambda i, order_ref: (i, 0))
    grid_spec = pltpu.PrefetchScalarGridSpec(
        num_scalar_prefetch=1, grid=(s // tm,),
        in_specs=[a_spec(q) for q in range(nq)]
        + [pl.BlockSpec((d, sh), lambda i, order_ref: (0, 0), pipeline_mode=pl.Buffered(1))] * nq
        + [row, row, pl.BlockSpec((1, d), lambda i, order_ref: (0, 0))] + [_hbm()] * n_after,
        out_specs=[row, pl.BlockSpec((SUBLANES, d), lambda i, order_ref: (0, 0))])
    return _call(body, name="in_proj_bwd_norm", grid_spec=grid_spec,
                 out_shape=[jax.ShapeDtypeStruct((s, d), F32), jax.ShapeDtypeStruct((SUBLANES, d), F32)],
                 compiler_params=_params(("arbitrary",), VMEM_LIMIT_BIG))(
                     order, *([dz] * nq), *ws, xin, add, gain, *after)


def _seg_mean(sq, segb):
    parts = []
    for cgrp in range(sq.shape[1] // SEG):
        blk = sq[:, cgrp * SEG:(cgrp + 1) * SEG]
        hi = blk.astype(BF16)
        r1 = blk - hi.astype(F32)
        mid = r1.astype(BF16)
        lo = (r1 - mid.astype(F32)).astype(BF16)
        acc = jnp.dot(hi, segb, preferred_element_type=F32)
        acc += jnp.dot(mid, segb, preferred_element_type=F32)
        acc += jnp.dot(lo, segb, preferred_element_type=F32)
        parts.append(acc)
    out = parts[0] if len(parts) == 1 else jnp.concatenate(parts, axis=1)
    return out * (1.0 / HEAD_DIM)


def _rope(v, cos, sa, sb):
    parts = []
    for cgrp in range(v.shape[1] // LANES):
        blk = v[:, cgrp * LANES:(cgrp + 1) * LANES]
        parts.append(blk * cos + pltpu.roll(blk, LANES - 8, 1) * sa + pltpu.roll(blk, 8, 1) * sb)
    return parts[0] if len(parts) == 1 else jnp.concatenate(parts, axis=1)


def _rope_t(dv, cos, sa, sb):
    parts = []
    for cgrp in range(dv.shape[1] // LANES):
        blk = dv[:, cgrp * LANES:(cgrp + 1) * LANES]
        parts.append(blk * cos + pltpu.roll(blk * sa, 8, 1) + pltpu.roll(blk * sb, LANES - 8, 1))
    return parts[0] if len(parts) == 1 else jnp.concatenate(parts, axis=1)


def _tile_lanes(vec, width):
    reps = width // LANES
    return vec if reps == 1 else jnp.tile(vec, (1, reps))


def qk_prep_fwd(z, tabs, qg, kg, segb, aw, after=()):
    s = z.shape[0]
    tm = _tile(s, 512)
    wq = aw + 2 * KV_WIDTH

    def body(z_ref, cos_ref, sa_ref, sb_ref, qg_ref, kg_ref, seg_ref, qs_ref, ks_ref, vb_ref):
        zz = z_ref[...]
        q, k, v = zz[:, :aw], zz[:, aw:aw + KV_WIDTH], zz[:, aw + KV_WIDTH:]
        cos, sa, sb, segm = cos_ref[...], sa_ref[...], sb_ref[...], seg_ref[...]
        rq = lax.rsqrt(_seg_mean(q * q, segm) + EPS)
        qn = (q * rq) * _tile_lanes(qg_ref[...], aw)
        qs_ref[...] = (_rope(qn, cos, sa, sb) * (HEAD_DIM ** -0.5)).astype(BF16)
        rk = lax.rsqrt(_seg_mean(k * k, segm) + EPS)
        kn = (k * rk) * _tile_lanes(kg_ref[...], KV_WIDTH)
        ks_ref[...] = _rope(kn, cos, sa, sb).astype(BF16)
        vb_ref[...] = v.astype(BF16)

    tab = pl.BlockSpec((tm, LANES), lambda i: (i, 0))
    vec = pl.BlockSpec((1, LANES), lambda i: (0, 0))
    return _call_after(body, after, name="qk_prep_fwd", grid=(s // tm,),
                       in_specs=[pl.BlockSpec((tm, wq), lambda i: (i, 0)), tab, tab, tab, vec, vec,
                                 pl.BlockSpec((SEG, SEG), lambda i: (0, 0))],
                       out_specs=[pl.BlockSpec((tm, aw), lambda i: (i, 0)),
                                  pl.BlockSpec((tm, KV_WIDTH), lambda i: (i, 0)),
                                  pl.BlockSpec((tm, KV_WIDTH), lambda i: (i, 0))],
                       out_shape=[jax.ShapeDtypeStruct((s, aw), BF16), jax.ShapeDtypeStruct((s, KV_WIDTH), BF16),
                                  jax.ShapeDtypeStruct((s, KV_WIDTH), BF16)],
                       compiler_params=_params(("parallel",)))(z, *tabs, qg, kg, segb)


def qk_prep_bwd(z, dqs, dks, dvs, d_gate, tabs, qg, kg, segb, aw):
    s, in_w = z.shape
    tm = _tile(s, 512)
    wq = aw + 2 * KV_WIDTH

    def body(z_ref, dq_ref, dk_ref, dv_ref, dga_ref, cos_ref, sa_ref, sb_ref, qg_ref, kg_ref, seg_ref,
             dz_ref, acc_ref):
        i = pl.program_id(0)
        zz = z_ref[...]
        q, k = zz[:, :aw], zz[:, aw:aw + KV_WIDTH]
        cos, sa, sb, segm = cos_ref[...], sa_ref[...], sb_ref[...], seg_ref[...]

        def one(xv, dout, gvec, width):
            g = _tile_lanes(gvec, width)
            r = lax.rsqrt(_seg_mean(xv * xv, segm) + EPS)
            xhat = xv * r
            dn = _rope_t(dout, cos, sa, sb)
            gd = dn * g
            dx = r * (gd - xhat * _seg_mean(xhat * gd, segm))
            contrib = jnp.sum(dn * xhat, axis=0, keepdims=True)
            folded = contrib[:, :LANES]
            for cgrp in range(1, width // LANES):
                folded = folded + contrib[:, cgrp * LANES:(cgrp + 1) * LANES]
            return dx, folded + pltpu.roll(folded, HEAD_DIM, 1)

        dq, gq = one(q, dq_ref[...] * (HEAD_DIM ** -0.5), qg_ref[...], aw)
        dk, gk = one(k, dk_ref[...], kg_ref[...], KV_WIDTH)
        dz_ref[:, :aw] = dq.astype(BF16)
        dz_ref[:, aw:aw + KV_WIDTH] = dk.astype(BF16)
        dz_ref[:, aw + KV_WIDTH:wq] = dv_ref[...].astype(BF16)
        dz_ref[:, wq:] = dga_ref[...]

        @pl.when(i == 0)
        def _():
            acc_ref[...] = jnp.zeros_like(acc_ref)

        acc_ref[0:1, :] += gq
        acc_ref[1:2, :] += gk

    tab = pl.BlockSpec((tm, LANES), lambda i: (i, 0))
    vec = pl.BlockSpec((1, LANES), lambda i: (0, 0))
    kvb = pl.BlockSpec((tm, KV_WIDTH), lambda i: (i, 0))
    awb = pl.BlockSpec((tm, aw), lambda i: (i, 0))
    return _call(body, name="qk_prep_bwd", grid=(s // tm,),
                 in_specs=[pl.BlockSpec((tm, wq), lambda i: (i, 0)), awb, kvb, kvb, awb, tab, tab, tab, vec, vec,
                           pl.BlockSpec((SEG, SEG), lambda i: (0, 0))],
                 out_specs=[pl.BlockSpec((tm, wq + aw), lambda i: (i, 0)),
                            pl.BlockSpec((SUBLANES, LANES), lambda i: (0, 0))],
                 out_shape=[jax.ShapeDtypeStruct((s, in_w), BF16), jax.ShapeDtypeStruct((SUBLANES, LANES), F32)],
                 compiler_params=_params(("arbitrary",)))(z, dqs, dks, dvs, d_gate, *tabs, qg, kg, segb)


def _placed(band, lane_idx):
    out = {}
    for kh in range(N_KV_HEADS):
        grp = band[:, (kh // 2) * LANES:(kh // 2 + 1) * LANES]
        for half in (0, 1):
            t = grp if half == kh % 2 else pltpu.roll(grp, HEAD_DIM, 1)
            keep = (lane_idx >= half * HEAD_DIM) & (lane_idx < (half + 1) * HEAD_DIM)
            out[kh, half] = jnp.where(keep, t, jnp.zeros_like(t))
    return out


def _softmax_with_sink(sc, valid, sink):
    sc = jnp.where(valid, sc, NEG_INF)
    m = jnp.maximum(jnp.max(sc, axis=-1, keepdims=True), sink)
    e = jnp.exp(sc - m)
    es = jnp.exp(sink - m)
    den = jnp.sum(e, axis=-1, keepdims=True) + es
    return e / den, es / den


def _stack_halves(placed, kh):
    return jnp.concatenate([placed[kh, 0], placed[kh, 1]], axis=0)


def _valid_mask(n):
    r_i = lax.broadcasted_iota(jnp.int32, (BLOCK, 2 * BLOCK), 0)
    j_i = lax.broadcasted_iota(jnp.int32, (BLOCK, 2 * BLOCK), 1)
    return (j_i > r_i) & (j_i <= r_i + BLOCK) & ((n > 0) | (j_i >= BLOCK))


def attn_fwd(qs, ks, vb, z, sinks, aw, d, ga_off):
    s = qs.shape[0]
    nb = s // BLOCK
    nq = aw // HEAD_DIM
    grp_sz = nq // N_KV_HEADS
    n_ga = aw // COLT

    def body(q_ref, ko_ref, kp_ref, vo_ref, vp_ref, *rest):
        ga_refs = rest[:n_ga]
        sink_ref, attn_ref, mix_ref = rest[n_ga:]
        n = pl.program_id(0)
        lane_idx = lax.broadcasted_iota(jnp.int32, (2 * BLOCK, LANES), 1)
        kpl = _placed(jnp.concatenate([kp_ref[...], ko_ref[...]], axis=0), lane_idx)
        vpl = _placed(jnp.concatenate([vp_ref[...], vo_ref[...]], axis=0), lane_idx)
        valid = _valid_mask(n)
        groups = range(nq // 2)
        kcat = [_stack_halves(kpl, kh) for kh in range(N_KV_HEADS)]
        vcat = [_stack_halves(vpl, kh) for kh in range(N_KV_HEADS)]
        scs = [lax.dot_general(q_ref[:, c * LANES:(c + 1) * LANES], kcat[2 * c // grp_sz], NT,
                               preferred_element_type=F32) for c in groups]
        probs = [jnp.concatenate(
            [_softmax_with_sink(scs[c][:, half * 2 * BLOCK:(half + 1) * 2 * BLOCK], valid,
                                sink_ref[0, 2 * c + half])[0].astype(BF16) for half in (0, 1)], axis=1)
                 for c in groups]
        for c in groups:
            acc = jnp.dot(probs[c], vcat[2 * c // grp_sz], preferred_element_type=F32)
            attn_ref[:, c * LANES:(c + 1) * LANES] = acc
            col = c * LANES
            ga = ga_refs[col // COLT][:, col % COLT:col % COLT + LANES]
            mix_ref[:, c * LANES:(c + 1) * LANES] = (acc * (ga * _sigmoid(ga))).astype(BF16)

    own = lambda n: (n, 0)
    prev = lambda n: (jnp.maximum(n - 1, 0), 0)
    kvs = lambda imap: pl.BlockSpec((BLOCK, KV_WIDTH), imap)
    ga_specs = [pl.BlockSpec((BLOCK, COLT), functools.partial(lambda n, j: (n, ga_off + j), j=j))
                for j in range(n_ga)]
    return _call(body, name="attn_fwd", grid=(nb,),
                 in_specs=[pl.BlockSpec((BLOCK, aw), own), kvs(own), kvs(prev), kvs(own), kvs(prev)]
                 + ga_specs + [pl.BlockSpec(memory_space=pltpu.SMEM)],
                 out_specs=[pl.BlockSpec((BLOCK, aw), own), pl.BlockSpec((BLOCK, aw), own)],
                 out_shape=[jax.ShapeDtypeStruct((s, aw), F32), jax.ShapeDtypeStruct((s, d), BF16)],
                 compiler_params=_params(("parallel",)))(qs, ks, ks, vb, vb, *([z] * n_ga), sinks)


def out_proj_bwd_gate(d_x1b, wo, attn, z, aw, ga_off, after=()):
    s, d = d_x1b.shape
    tm = _tile(s, 512)
    n_ga = aw // COLT

    def body(dx_ref, w_ref, at_ref, *rest):
        ga_refs, (do_ref, dga_ref, dmc_ref) = rest[:n_ga], rest[n_ga:]
        dm = lax.dot_general(dx_ref[...], w_ref[...], NT, preferred_element_type=F32)
        dmc_ref[...] = dm[:, aw:]
        for j in range(n_ga):
            cols = slice(j * COLT, (j + 1) * COLT)
            ga = ga_refs[j][...]
            sig = _sigmoid(ga)
            dma = dm[:, cols]
            do_ref[:, cols] = (dma * (ga * sig)).astype(BF16)
            dga_ref[:, cols] = ((dma * at_ref[:, cols]) * (sig * (1.0 + ga * (1.0 - sig)))).astype(BF16)

    row = lambda width: pl.BlockSpec((tm, width), lambda i: (i, 0))
    ga_specs = [pl.BlockSpec((tm, COLT), functools.partial(lambda i, j: (i, ga_off + j), j=j)) for j in range(n_ga)]
    return _call_after(body, after, name="out_proj_bwd_gate", grid=(s // tm,),
                       in_specs=[row(d), pl.BlockSpec((d, d), lambda i: (0, 0), pipeline_mode=pl.Buffered(1)),
                                 row(aw)] + ga_specs,
                       out_specs=[row(aw), row(aw), row(d - aw)],
                       out_shape=[jax.ShapeDtypeStruct((s, aw), BF16), jax.ShapeDtypeStruct((s, aw), BF16),
                                  jax.ShapeDtypeStruct((s, d - aw), F32)],
                       compiler_params=_params(("parallel",)))(d_x1b, wo, attn, *([z] * n_ga))


def attn_bwd(qs, ks, vb, d_o, sinks, aw, after=()):
    s = qs.shape[0]
    nb = s // BLOCK
    nq = aw // HEAD_DIM
    grp_sz = nq // N_KV_HEADS

    def body(q_ref, ko_ref, kp_ref, vo_ref, vp_ref, do_ref, sink_ref, dq_ref, dk_ref, dv_ref, ds_ref,
             ck_ref, cv_ref):
        n = pl.program_id(0)

        @pl.when(n == 0)
        def _():
            ds_ref[...] = jnp.zeros_like(ds_ref)
            dk_ref[...] = jnp.zeros_like(dk_ref)
            dv_ref[...] = jnp.zeros_like(dv_ref)

        @pl.when(n < nb)
        def _():
            lane_idx = lax.broadcasted_iota(jnp.int32, (2 * BLOCK, LANES), 1)
            lane_row = lax.broadcasted_iota(jnp.int32, (1, LANES), 1)
            kpl = _placed(jnp.concatenate([kp_ref[...], ko_ref[...]], axis=0), lane_idx)
            vpl = _placed(jnp.concatenate([vp_ref[...], vo_ref[...]], axis=0), lane_idx)
            valid = _valid_mask(n)
            ds_row = jnp.zeros((1, LANES), F32)
            wide = 2 * BLOCK
            groups = range(nq // 2)
            per_kv = grp_sz // 2
            kcat = [_stack_halves(kpl, kh) for kh in range(N_KV_HEADS)]
            vcat = [_stack_halves(vpl, kh) for kh in range(N_KV_HEADS)]
            qg = [q_ref[:, c * LANES:(c + 1) * LANES] for c in groups]
            dog = [do_ref[:, c * LANES:(c + 1) * LANES] for c in groups]
            scs = [lax.dot_general(qg[c], kcat[c // per_kv], NT, preferred_element_type=F32) for c in groups]
            dps = [lax.dot_general(dog[c], vcat[c // per_kv], NT, preferred_element_type=F32) for c in groups]
            ds_pairs, p_pairs = [], []
            for c in groups:
                probs, dss = [], []
                for half in (0, 1):
                    h = 2 * c + half
                    cols = slice(half * wide, (half + 1) * wide)
                    p, p_sink = _softmax_with_sink(scs[c][:, cols], valid, sink_ref[0, h])
                    delta = jnp.sum(p * dps[c][:, cols], axis=-1, keepdims=True)
                    dss.append((p * (dps[c][:, cols] - delta)).astype(BF16))
                    probs.append(p.astype(BF16))
                    dsink = -jnp.sum(p_sink * delta, axis=0, keepdims=True)
                    ds_row += jnp.where(lane_row == h, dsink, 0.0)
                ds_pairs.append(jnp.concatenate(dss, axis=1))
                p_pairs.append(jnp.concatenate(probs, axis=1))
            for c in groups:
                dq_ref[:, c * LANES:(c + 1) * LANES] = jnp.dot(ds_pairs[c], kcat[c // per_kv],
                                                               preferred_element_type=F32)
            dkts = [lax.dot_general(qg[c], ds_pairs[c], TN, preferred_element_type=F32) for c in groups]
            dvts = [lax.dot_general(dog[c], p_pairs[c], TN, preferred_element_type=F32) for c in groups]
            dkt, dvt = [], []
            for kh in range(N_KV_HEADS):
                for parts, out in ((dkts, dkt), (dvts, dvt)):
                    tot = parts[kh * per_kv]
                    for c in range(kh * per_kv + 1, (kh + 1) * per_kv):
                        tot = tot + parts[c]
                    out.append(tot[:HEAD_DIM, :wide] + tot[HEAD_DIM:, wide:])
            ds_ref[0:1, :] += ds_row
            back = lambda t: jnp.concatenate(
                [jnp.concatenate(t[2 * g:2 * g + 2], axis=0).T for g in range(N_KV_HEADS // 2)], axis=1)
            dk_band, dv_band = back(dkt), back(dvt)

            @pl.when(n > 0)
            def _():
                dk_ref[...] = ck_ref[...] + dk_band[:BLOCK]
                dv_ref[...] = cv_ref[...] + dv_band[:BLOCK]

            ck_ref[...] = dk_band[BLOCK:]
            cv_ref[...] = dv_band[BLOCK:]

        @pl.when(n == nb)
        def _():
            dk_ref[...] = ck_ref[...]
            dv_ref[...] = cv_ref[...]

    own = lambda n: (jnp.minimum(n, nb - 1), 0)
    prev = lambda n: (jnp.clip(n - 1, 0, nb - 1), 0)
    done = lambda n: (jnp.maximum(n - 1, 0), 0)
    kvs = lambda imap: pl.BlockSpec((BLOCK, KV_WIDTH), imap)
    return _call_after(body, after, name="attn_bwd", grid=(nb + 1,),
                       in_specs=[pl.BlockSpec((BLOCK, aw), own), kvs(own), kvs(prev), kvs(own), kvs(prev),
                                 pl.BlockSpec((BLOCK, aw), own), pl.BlockSpec(memory_space=pltpu.SMEM)],
                       out_specs=[pl.BlockSpec((BLOCK, aw), own), kvs(done), kvs(done),
                                  pl.BlockSpec((SUBLANES, LANES), lambda n: (0, 0))],
                       out_shape=[jax.ShapeDtypeStruct((s, aw), F32), jax.ShapeDtypeStruct((s, KV_WIDTH), F32),
                                  jax.ShapeDtypeStruct((s, KV_WIDTH), F32),
                                  jax.ShapeDtypeStruct((SUBLANES, LANES), F32)],
                       scratch_shapes=[pltpu.VMEM((BLOCK, KV_WIDTH), F32), pltpu.VMEM((BLOCK, KV_WIDTH), F32)],
                       compiler_params=_params(("arbitrary",)))(qs, ks, ks, vb, vb, d_o, sinks)


def conv_fwd(z, conv_w, mix, aw, cw, b_off):
    s = z.shape[0]
    tm = _tile(s, 1024)
    nseg = cw // COLT
    hb = tm // SUBLANES

    def body(b_ref, c_ref, h_ref, g_ref, cp_ref, hp_ref, w_ref, mix_in, mix_ref):
        i = pl.program_id(0)
        u = c_ref[...] * h_ref[...]
        up = jnp.where(i > 0, cp_ref[...] * hp_ref[...], 0.0)
        ext = jnp.concatenate([up, u], axis=0)
        um1 = pltpu.roll(ext, 1, 0)[SUBLANES:]
        um2 = pltpu.roll(ext, 2, 0)[SUBLANES:]
        w = w_ref[...]
        cv = w[0:1] * um2 + w[1:2] * um1 + w[2:3] * u
        g = g_ref[...]
        mix_ref[...] = ((b_ref[...] * cv) * (g * _sigmoid(g))).astype(BF16)

    seg = lambda k: pl.BlockSpec((tm, COLT), functools.partial(lambda i, j, k: (i, b_off + k * nseg + j), k=k))
    halo = lambda k: pl.BlockSpec(
        (SUBLANES, COLT), functools.partial(lambda i, j, k: (jnp.maximum(i * hb - 1, 0), b_off + k * nseg + j), k=k))
    return _call(body, name="conv_fwd", grid=(s // tm, nseg),
                 in_specs=[seg(0), seg(1), seg(2), seg(3), halo(1), halo(2),
                           pl.BlockSpec((SUBLANES, COLT), lambda i, j: (0, j)), _hbm()],
                 out_specs=pl.BlockSpec((tm, COLT), lambda i, j: (i, aw // COLT + j)),
                 out_shape=jax.ShapeDtypeStruct(mix.shape, BF16),
                 input_output_aliases={7: 0},
                 compiler_params=_params(("parallel", "parallel")))(z, z, z, z, z, z, conv_w, mix)


def conv_bwd(z, d_mix, conv_w, dz, aw, cw, b_off):
    s = z.shape[0]
    tm = _tile(s, 512)
    nseg = cw // COLT
    hb = tm // SUBLANES
    n_row = s // tm
    last_h = s // SUBLANES - 1
    n_step = nseg * n_row

    def body(b_ref, c_ref, h_ref, g_ref, cp_ref, hp_ref, bn_ref, gn_ref, dm_ref, dmn_ref, w_ref, dz_in,
             dz_ref, acc_ref, stash_ref, sems):
        j = pl.program_id(0)
        i = pl.program_id(1)
        step = j * n_row + i
        slot = step % 2

        def copies(from_slot, at_step):
            jj, ii = at_step // n_row, at_step % n_row
            rows = pl.ds(pl.multiple_of(ii * tm, tm), tm)
            return [pltpu.make_async_copy(
                stash_ref.at[from_slot, q],
                dz_ref.at[rows, pl.ds(pl.multiple_of((b_off + q * nseg + jj) * COLT, COLT), COLT)],
                sems.at[from_slot, q]) for q in range(4)]

        @pl.when(step >= 2)
        def _():
            for cp in copies(slot, step - 2):
                cp.wait()

        cc, hh, bb, g = c_ref[...], h_ref[...], b_ref[...], g_ref[...]
        w = w_ref[...]
        u = cc * hh
        up = jnp.where(i > 0, cp_ref[...] * hp_ref[...], 0.0)
        ext = jnp.concatenate([up, u], axis=0)
        um1 = pltpu.roll(ext, 1, 0)[SUBLANES:]
        um2 = pltpu.roll(ext, 2, 0)[SUBLANES:]
        cv = w[0:1] * um2 + w[1:2] * um1 + w[2:3] * u
        sig = _sigmoid(g)
        sg = g * sig
        dm = dm_ref[...]
        d_cv = (dm * bb) * sg
        gn = gn_ref[...]
        d_cv_next = jnp.where(i < n_row - 1, (dmn_ref[...] * bn_ref[...]) * (gn * _sigmoid(gn)), 0.0)
        ext2 = jnp.concatenate([d_cv, d_cv_next], axis=0)
        dp1 = pltpu.roll(ext2, tm + SUBLANES - 1, 0)[:tm]
        dp2 = pltpu.roll(ext2, tm + SUBLANES - 2, 0)[:tm]
        d_u = w[2:3] * d_cv + w[1:2] * dp1 + w[0:1] * dp2
        stash_ref[slot, 0] = ((dm * cv) * sg).astype(BF16)
        stash_ref[slot, 1] = (d_u * hh).astype(BF16)
        stash_ref[slot, 2] = (d_u * cc).astype(BF16)
        stash_ref[slot, 3] = (((dm * bb) * cv) * (sig * (1.0 + g * (1.0 - sig)))).astype(BF16)
        for cp in copies(slot, step):
            cp.start()

        @pl.when(i == 0)
        def _():
            acc_ref[...] = jnp.zeros_like(acc_ref)

        acc_ref[0:1, :] += jnp.sum(d_cv * um2, axis=0, keepdims=True)
        acc_ref[1:2, :] += jnp.sum(d_cv * um1, axis=0, keepdims=True)
        acc_ref[2:3, :] += jnp.sum(d_cv * u, axis=0, keepdims=True)

        @pl.when(step == n_step - 1)
        def _():
            if n_step >= 2:
                for cp in copies(1 - slot, step - 1):
                    cp.wait()
            for cp in copies(slot, step):
                cp.wait()

    seg = lambda q: pl.BlockSpec((tm, COLT), functools.partial(lambda j, i, q: (i, b_off + q * nseg + j), q=q))
    halo_p = lambda q: pl.BlockSpec(
        (SUBLANES, COLT),
        functools.partial(lambda j, i, q: (jnp.maximum(i * hb - 1, 0), b_off + q * nseg + j), q=q))
    halo_n = lambda q: pl.BlockSpec(
        (SUBLANES, COLT),
        functools.partial(lambda j, i, q: (jnp.minimum((i + 1) * hb, last_h), b_off + q * nseg + j), q=q))
    return _call(body, name="conv_bwd", grid=(nseg, n_row),
                 in_specs=[seg(0), seg(1), seg(2), seg(3), halo_p(1), halo_p(2), halo_n(0), halo_n(3),
                           pl.BlockSpec((tm, COLT), lambda j, i: (i, j)),
                           pl.BlockSpec((SUBLANES, COLT), lambda j, i: (jnp.minimum((i + 1) * hb, last_h), j)),
                           pl.BlockSpec((SUBLANES, COLT), lambda j, i: (0, j)), _hbm()],
                 out_specs=[_hbm(), pl.BlockSpec((SUBLANES, COLT), lambda j, i: (0, j))],
                 out_shape=[jax.ShapeDtypeStruct(dz.shape, BF16), jax.ShapeDtypeStruct((SUBLANES, cw), F32)],
                 input_output_aliases={11: 0},
                 scratch_shapes=[pltpu.VMEM((2, 4, tm, COLT), BF16), pltpu.SemaphoreType.DMA((2, 4))],
                 compiler_params=_params(("arbitrary", "arbitrary")))(
                     z, z, z, z, z, z, z, z, d_mix, d_mix, conv_w, dz)


def _place():
    x, y, c = lax.axis_index("x"), lax.axis_index("y"), lax.axis_index("c")
    chips = [(1 - x, y), (x, 1 - y), (1 - x, 1 - y)]
    return x, y, c, chips


def _remote(src, dst, send_sem, recv_sem, device):
    return pltpu.make_async_remote_copy(src_ref=src, dst_ref=dst, send_sem=send_sem, recv_sem=recv_sem,
                                        device_id=device, device_id_type=MESH)


def plan_gather_ici(n_split):
    def plan(refs, send, recv):
        x, y, c, chips = _place()
        me = 2 * x + y
        mine, theirs = [], []
        for w, ref in enumerate(refs):
            for j, (cx, cy) in enumerate(chips):
                def part(slot):
                    if w >= n_split:
                        return ref.at[slot]
                    hr = ref.shape[1] // 2
                    return ref.at[slot, pl.ds(c * hr, hr)]
                k = 3 * w + j
                mine.append(_remote(part(me), part(me), send.at[k], recv.at[k], (cx, cy, c)))
                theirs.append(_remote(part(2 * cx + cy), part(2 * cx + cy), send.at[k], recv.at[k], (cx, cy, c)))
        return mine, theirs
    return plan


def plan_shard_ici(j):
    def plan(refs, send, recv):
        _, _, c, chips = _place()
        own, land = refs
        hr = own.shape[0] // 2
        rows = pl.ds(c * hr, hr)
        cp = _remote(own.at[rows], land.at[rows], send.at[0], recv.at[0], (*chips[j], c))
        return [cp], [cp]
    return plan


def plan_shard_relay(refs, send, recv):
    _, _, c, chips = _place()
    land_x, land_y, land_far = refs
    hr = land_far.shape[0] // 2
    quarter = lambda q: pl.ds(c * hr + q * (hr // 2), hr // 2)
    mine = [_remote(land_y.at[quarter(0)], land_far.at[quarter(0)], send.at[0], recv.at[0], (*chips[0], c)),
            _remote(land_x.at[quarter(1)], land_far.at[quarter(1)], send.at[1], recv.at[1], (*chips[1], c))]
    return mine, mine


def plan_shard_pass(refs, send, recv):
    x, y, c, _ = _place()
    mine, theirs = [], []
    for w, land in enumerate(refs):
        hr = land.shape[0] // 2
        half = lambda core: land.at[pl.ds(core * hr, hr)]
        mine.append(_remote(half(c), half(c), send.at[w], recv.at[w], (x, y, 1 - c)))
        theirs.append(_remote(half(1 - c), half(1 - c), send.at[w], recv.at[w], (x, y, 1 - c)))
    return mine, theirs


def plan_gather_pass(refs, send, recv):
    x, y, c, chips = _place()
    mine, theirs = [], []
    for w, ref in enumerate(refs):
        hr = ref.shape[1] // 2
        for j, (cx, cy) in enumerate(chips):
            half = lambda core: ref.at[2 * cx + cy, pl.ds(core * hr, hr)]
            k = 3 * w + j
            mine.append(_remote(half(c), half(c), send.at[k], recv.at[k], (x, y, 1 - c)))
            theirs.append(_remote(half(1 - c), half(1 - c), send.at[k], recv.at[k], (x, y, 1 - c)))
    return mine, theirs


def plan_swap(refs, send, recv):
    x, y, c, _ = _place()
    nw = len(refs) // 2
    mine = []
    for w in range(nw):
        hr = refs[w].shape[1] // 2
        mine.append(_remote(refs[w].at[:, pl.ds((1 - c) * hr, hr), :], refs[nw + w], send.at[w], recv.at[w],
                            (x, y, 1 - c)))
    return mine, mine


def plan_scatter(refs, send, recv):
    x, y, c, chips = _place()
    me = 2 * x + y
    nw = len(refs) // 2
    mine, theirs = [], []
    for w in range(nw):
        for j, (cx, cy) in enumerate(chips):
            k = 3 * w + j
            mine.append(_remote(refs[w].at[2 * cx + cy], refs[nw + w].at[me], send.at[k], recv.at[k], (cx, cy, c)))
            theirs.append(_remote(refs[w].at[me], refs[nw + w].at[2 * cx + cy], send.at[k], recv.at[k], (cx, cy, c)))
    return mine, theirs


def plan_join(refs, send, recv):
    x, y, c, _ = _place()
    mine, theirs = [], []
    for w, ref in enumerate(refs):
        hr = ref.shape[0] // 2
        half = lambda core: ref.at[pl.ds(core * hr, hr)]
        mine.append(_remote(half(c), half(c), send.at[w], recv.at[w], (x, y, 1 - c)))
        theirs.append(_remote(half(1 - c), half(1 - c), send.at[w], recv.at[w], (x, y, 1 - c)))
    return mine, theirs


def exchange(name, plan, arrays, n, after=()):
    na = len(arrays)

    def body(*refs):
        mine, theirs = plan(refs[:na], refs[2 * na], refs[2 * na + 1])
        for cp in mine:
            cp.start()
        for cp in theirs:
            cp.wait_recv()
        for cp in mine:
            cp.wait_send()

    return _call_after(body, after, name=name, in_specs=[_hbm()] * na, out_specs=[_hbm()] * na,
                       out_shape=[jax.ShapeDtypeStruct(a.shape, a.dtype) for a in arrays],
                       input_output_aliases={i: i for i in range(na)},
                       scratch_shapes=[pltpu.SemaphoreType.DMA((n,)), pltpu.SemaphoreType.DMA((n,))])(*arrays)


def exchange_start(name, groups, arrays, after=()):
    na, ng = len(arrays), len(groups)

    def body(*refs):
        for g, (plan, idx, _) in enumerate(groups):
            mine, _ = plan([refs[i] for i in idx], refs[na + 2 * g], refs[na + 2 * g + 1])
            for cp in mine:
                cp.start()
        refs[-1][...] = jnp.zeros_like(refs[-1])

    hbm = pl.BlockSpec(memory_space=pltpu.HBM)
    sem = pl.BlockSpec(memory_space=pltpu.SEMAPHORE)
    sem_types = [pltpu.SemaphoreType.DMA((n,)) for _, _, n in groups for _ in (0, 1)]
    outs = _call_after(body, after, name=name, in_specs=[hbm] * na,
                       out_specs=[sem] * (2 * ng) + [hbm] * na + [pl.BlockSpec(memory_space=pltpu.VMEM)],
                       out_shape=sem_types + [pltpu.HBM(a.shape, a.dtype) for a in arrays]
                       + [jax.ShapeDtypeStruct((SUBLANES, LANES), F32)],
                       input_output_aliases={i: i + 2 * ng for i in range(na)},
                       compiler_params=pltpu.CompilerParams(
                           has_side_effects=pltpu.SideEffectType.DATAFLOW_SIDE_EFFECTING))(
                               *[pltpu.with_memory_space_constraint(a, pltpu.HBM) for a in arrays])
    sems = [(outs[2 * g], outs[2 * g + 1]) for g in range(ng)]
    return sems, list(outs[2 * ng:-1]), outs[-1]


def exchange_wait(name, plan, sems, arrays, after):
    send_sems, recv_sems = sems
    na = len(arrays)
    after = tuple(after) if isinstance(after, (tuple, list)) else (after,)

    def body(*refs):
        mine, theirs = plan(refs[:na], refs[na], refs[na + 1])
        for cp in mine:
            cp.wait_send()
        for cp in theirs:
            cp.wait_recv()

    hbm = pl.BlockSpec(memory_space=pltpu.HBM)
    sem = pl.BlockSpec(memory_space=pltpu.SEMAPHORE)
    return _call(body, name=name, in_specs=[hbm] * na + [sem, sem] + [_hbm()] * len(after),
                 out_specs=[hbm] * na, out_shape=[pltpu.HBM(a.shape, a.dtype) for a in arrays],
                 input_output_aliases={i: i for i in range(na)},
                 compiler_params=pltpu.CompilerParams(
                     has_side_effects=pltpu.SideEffectType.DATAFLOW_SIDE_EFFECTING))(
                         *arrays, send_sems, recv_sems, *after)


def plan_allgather_small(refs, send, recv):
    x, y, c, _ = _place()
    buf, = refs
    mine, theirs = [], []
    flips = [(fx, fy, fc) for fx in (0, 1) for fy in (0, 1) for fc in (0, 1)][1:]
    for k, (fx, fy, fc) in enumerate(flips):
        peer = (x ^ fx, y ^ fy, c ^ fc)
        slot = lambda px, py, pc: buf.at[4 * px + 2 * py + pc]
        mine.append(_remote(slot(x, y, c), slot(x, y, c), send.at[k], recv.at[k], peer))
        theirs.append(_remote(slot(*peer), slot(*peer), send.at[k], recv.at[k], peer))
    return mine, theirs


def add_sibling(grad, got, core, name):
    _, r, c = grad.shape
    hr = r // 2
    tr = _tile(hr, 512)
    nblk = hr // tr

    def body(core_ref, g_ref, o_ref, out_ref):
        out_ref[...] = (g_ref[...].astype(F32) + o_ref[...].astype(F32)).astype(BF16)

    grid_spec = pltpu.PrefetchScalarGridSpec(
        num_scalar_prefetch=1, grid=(N_CHIPS, nblk),
        in_specs=[pl.BlockSpec((None, tr, c), lambda t, i, core_ref: (t, core_ref[0] * nblk + i, 0)),
                  pl.BlockSpec((None, tr, c), lambda t, i, core_ref: (t, i, 0))],
        out_specs=pl.BlockSpec((None, tr, c), lambda t, i, core_ref: (t, i, 0)))
    return _call(body, name=name, grid_spec=grid_spec,
                 out_shape=jax.ShapeDtypeStruct((N_CHIPS, hr, c), BF16),
                 compiler_params=_params(("parallel", "parallel")))(core, grad, got)


def sum_chips(mine, owned, place, name):
    _, hr, c = mine.shape
    tr = _tile(hr, 512)
    nblk = hr // tr

    def body(place_ref, m_ref, o1_ref, o2_ref, o3_ref, out_ref):
        acc = m_ref[...].astype(F32)
        for o_ref in (o1_ref, o2_ref, o3_ref):
            acc = acc + o_ref[...].astype(F32)
        out_ref[...] = acc

    other = lambda k: pl.BlockSpec((None, tr, c), lambda i, place_ref: ((place_ref[0] + k) % N_CHIPS, i, 0))
    grid_spec = pltpu.PrefetchScalarGridSpec(
        num_scalar_prefetch=1, grid=(nblk,),
        in_specs=[other(0), other(1), other(2), other(3)],
        out_specs=pl.BlockSpec((tr, c), lambda i, place_ref: (place_ref[1] * nblk + i, 0)))
    return _call(body, name=name, grid_spec=grid_spec,
                 out_shape=jax.ShapeDtypeStruct((2 * hr, c), F32),
                 compiler_params=_params(("parallel",)))(place, mine, owned, owned, owned)


def sum_devices(gathered):
    _, rows, width = gathered.shape

    def body(g_ref, out_ref):
        acc = g_ref[0]
        for dev in range(1, 8):
            acc = acc + g_ref[dev]
        out_ref[...] = acc
        tail = acc[SUBLANES:]
        out_ref[SUBLANES:, :] = jnp.broadcast_to(jnp.sum(tail, axis=1, keepdims=True), tail.shape)

    return _call(body, name="sum_devices",
                 in_specs=[pl.BlockSpec(memory_space=pltpu.VMEM)],
                 out_specs=pl.BlockSpec(memory_space=pltpu.VMEM),
                 out_shape=jax.ShapeDtypeStruct((rows, width), F32))(gathered)


def _rope_tables(s):
    half = ROT_DIM // 2
    inv_freq = jnp.power(jnp.float32(ROPE_THETA), -jnp.arange(half, dtype=F32) * 2.0 / ROT_DIM)
    freq64 = jnp.concatenate([inv_freq, inv_freq, jnp.zeros((HEAD_DIM - ROT_DIM,), F32)])
    freq = jnp.concatenate([freq64, freq64])[None, :]
    dim = (jnp.arange(LANES) % HEAD_DIM)[None, :]
    ang = jnp.arange(s).astype(F32)[:, None] * freq
    cos, sin = jnp.cos(ang), jnp.sin(ang)
    return cos, jnp.where(dim < half, -sin, 0.0), jnp.where((dim >= half) & (dim < ROT_DIM), sin, 0.0)


def _pad_rows(a, rows):
    return jnp.pad(a, ((0, rows - a.shape[0]), (0, 0)))


def _pad_cols(a, cols):
    return jnp.pad(a, ((0, 0), (0, cols - a.shape[1])))


def kernel(x, p, norm_gain, w_in, q_norm_gain, k_norm_gain, attn_sinks, conv_w, w_out, ple_gate_norm_gain, w_ple_gate, b_ple_gate, w_ple_proj, ple_norm_gain, loss_target, m_norm_gain, m_w_in, m_q_norm_gain, m_k_norm_gain, m_attn_sinks, m_conv_w, m_w_out, m_ple_gate_norm_gain, m_w_ple_gate, m_b_ple_gate, m_w_ple_proj, m_ple_norm_gain, v_norm_gain, v_w_in, v_q_norm_gain, v_k_norm_gain, v_attn_sinks, v_conv_w, v_w_out, v_ple_gate_norm_gain, v_w_ple_gate, v_b_ple_gate, v_w_ple_proj, v_ple_norm_gain):
    x2, p2, tgt = x[0], p[0, 0], loss_target[0]
    s, d = x2.shape
    aw = d // 2
    cw = d - aw
    nq = aw // HEAD_DIM
    sh = w_in.shape[2]
    in_w = N_CHIPS * sh
    dq = d // N_CHIPS
    cq = cw // N_CHIPS
    ga_off = (aw + 2 * KV_WIDTH) // COLT
    b_off = (2 * aw + 2 * KV_WIDTH) // COLT
    assert in_w == 2 * aw + 2 * KV_WIDTH + 4 * cw and aw % COLT == 0 and cw % COLT == 0
    assert s % BLOCK == 0 and sh % LANES == 0 and nq % (2 * N_KV_HEADS) == 0

    core = lax.axis_index("c").astype(jnp.int32).reshape(1)
    chip = 2 * lax.axis_index("x") + lax.axis_index("y")

    chip1 = chip.astype(jnp.int32).reshape(1)
    place = jnp.concatenate([chip1, core])
    w_in_own = cast_bf16(w_in[0], "cast_w_in")
    rest = [cast_into_slot(w_out[0], chip1, "cast_w_out"), cast_into_slot(w_ple_gate[0], chip1, "cast_w_pg"),
            cast_into_slot(w_ple_proj[0], chip1, "cast_w_pp"),
            lax.dynamic_update_slice(jnp.zeros((N_CHIPS, SUBLANES, cq), F32),
                                     _pad_rows(conv_w[0], SUBLANES)[None], (chip, 0, 0))]
    order = jnp.stack([chip, chip ^ 2, chip ^ 1, chip ^ 3]).astype(jnp.int32)
    wi_sems, wi_arrs, wi_token = exchange_start(
        "gather_wi_start", [(plan_shard_ici(j), [0, 1 + j], 1) for j in range(2)],
        [w_in_own] + [lax.empty((d, sh), BF16) for _ in range(3)])

    tn = _tile(d, 1024)
    ts = _tile(s, 2048)
    tabs = _rope_tables(s)
    qg = jnp.tile(q_norm_gain, (1, LANES // HEAD_DIM))
    kg = jnp.tile(k_norm_gain, (1, LANES // HEAD_DIM))
    seg_i = jnp.arange(SEG) // HEAD_DIM
    segb = (seg_i[:, None] == seg_i[None, :]).astype(BF16)
    h, z = norm_in_proj(x2, norm_gain, wi_arrs[0], order, in_w)
    own, land_x = exchange_wait("gather_wi_wait0", plan_shard_ici(0), wi_sems[0], [wi_arrs[0], wi_arrs[1]],
                                (z,) + tuple(tabs) + tuple(rest[:2]))
    own, land_y = exchange_wait("gather_wi_wait1", plan_shard_ici(1), wi_sems[1], [own, wi_arrs[2]], land_x)
    (relay_sems, rest_sems), started, rest_token = exchange_start(
        "gather_relay_rest_start", [(plan_shard_relay, [0, 1, 2], 2), (plan_gather_ici(3), [3, 4, 5, 6], 12)],
        [land_x, land_y, wi_arrs[3]] + rest)
    relayed, rest = started[:3], started[3:]
    land_x, land_y = exchange("gather_wi_pass01", plan_shard_pass, relayed[:2], 2, after=(rest_token,))
    z = in_proj_part(h, land_x, z, order, 1, in_w)
    z = in_proj_part(h, land_y, z, order, 2, in_w)
    land_x, land_y, land_far = exchange_wait("gather_wi_relay_wait", plan_shard_relay, relay_sems,
                                             [land_x, land_y, relayed[2]], z)
    land_far, = exchange("gather_wi_pass2", plan_shard_pass, [land_far], 1)
    z = in_proj_part(h, land_far, z, order, 3, in_w)
    w_shards = [own, land_x, land_y, land_far]
    wo_all, wg_all, wp_all, conv_all = exchange_wait("gather_rest_wait", plan_gather_ici(3), rest_sems, rest, z)
    pass_sems, passed, pass_token = exchange_start(
        "gather_rest_pass_start", [(plan_gather_pass, [0, 1, 2], 9)], [wo_all, wg_all, wp_all])
    conv_full = conv_all.transpose(1, 0, 2).reshape(SUBLANES, cw)
    qs, ks, vb = qk_prep_fwd(z, tabs, qg, kg, segb, aw, after=(pass_token,))
    attn, mix = attn_fwd(qs, ks, vb, z, attn_sinks, aw, d, ga_off)
    mix = conv_fwd(z, conv_full, mix, aw, cw, b_off)
    wo_all, wg_all, wp_all = exchange_wait("gather_rest_pass_wait", plan_gather_pass, pass_sems[0], passed, mix)
    wo_full = wo_all.reshape(d, d)
    wg_full = wg_all.reshape(d, d)
    x1, hg = out_proj_norm(mix, wo_full, x2, ple_gate_norm_gain)

    d_gl, dy, g_wp, acc_head = head_fwd_bwd(x1, hg, wg_full, p2, wp_all, tgt, b_ple_gate, ple_norm_gain)
    wgrad = lambda name, a, b: matmul(
        a, b, grid=(d // tn, d // tn, s // ts),
        a_spec=pl.BlockSpec((ts, tn), lambda i, j, k: (k, i)),
        b_spec=pl.BlockSpec((ts, tn), lambda i, j, k: (k, j)),
        o_spec=pl.BlockSpec((tn, tn), lambda i, j, k: (i, j)),
        out_shape=jax.ShapeDtypeStruct((d, d), BF16), dims=TN, name=name)
    g_wg = wgrad("mm_g_wg", hg, d_gl)
    d_x1, d_x1b, acc_g = gate_proj_bwd_norm(d_gl, wg_full, x1, dy, ple_gate_norm_gain)
    g_wo = wgrad("mm_g_wo", mix, d_x1b)
    def reduce_sum(tag, handle, after):
        sems, arrays, _ = handle
        n = len(arrays) // 2
        got = exchange_wait("scatter_%s_wait" % tag, plan_scatter, sems[0], arrays, after)
        return [sum_chips(pt, o, place, "sum_chips_%s%d" % (tag, i))
                for i, (pt, o) in enumerate(zip(got[:n], got[n:]))]

    early_grads = [g_wo.reshape(N_CHIPS, dq, d), g_wg.reshape(N_CHIPS, dq, d), g_wp]
    eswap_sems, eswapping, eswap_token = exchange_start(
        "swap_early_start", [(plan_swap, list(range(6)), 3)],
        early_grads + [lax.empty((N_CHIPS, a.shape[1] // 2, a.shape[2]), BF16) for a in early_grads])
    d_o, d_gate, d_mix_conv = out_proj_bwd_gate(d_x1b, wo_full, attn, z, aw, ga_off, after=(eswap_token,))
    eswapped = exchange_wait("swap_early_wait", plan_swap, eswap_sems[0], eswapping, d_o)
    early_parts = [add_sibling(g, o, core, "add_sibling_early%d" % i)
                   for i, (g, o) in enumerate(zip(eswapped[:3], eswapped[3:]))]
    early = exchange_start("scatter_early_start", [(plan_scatter, list(range(6)), 9)],
                           early_parts + [lax.empty(a.shape, BF16) for a in early_parts])
    dqs, dks, dvs, acc_sink = attn_bwd(qs, ks, vb, d_o, attn_sinks, aw, after=(early[-1],))
    dz, acc_qk = qk_prep_bwd(z, dqs, dks, dvs, d_gate, tabs, qg, kg, segb, aw)
    dz, acc_conv = conv_bwd(z, d_mix_conv, conv_full, dz, aw, cw, b_off)
    join_sems, joining, join_token = exchange_start(
        "join_early_start", [(plan_join, [0, 1, 2], 3)], reduce_sum("early", early, dz))
    g_send = in_proj_wgrad_half(h, dz, None, 1 - core, after=(join_token,))
    swap_sems, swapping, swap_token = exchange_start(
        "swap_late_start", [(plan_swap, [0, 1], 1)], [g_send, lax.empty((N_CHIPS, d // 2, sh), BF16)])
    g_wi = in_proj_wgrad_half(h, dz, swapping[0], core, after=(swap_token,))
    full_wo, full_wg, full_wp = exchange_wait("join_early_wait", plan_join, join_sems[0], joining, g_wi)
    g_wi, got_wi = exchange_wait("swap_late_wait", plan_swap, swap_sems[0], [g_wi, swapping[1]], full_wo)
    part_wi = add_sibling(g_wi, got_wi, core, "add_sibling_late0")
    late = exchange_start("scatter_late_start", [(plan_scatter, [0, 1], 3)],
                          [part_wi, lax.empty(part_wi.shape, BF16)])
    grad_x, acc_x = in_proj_bwd_norm(dz, w_shards, order, x2, d_x1, norm_gain, after=(late[-1],))

    wsm = max(d, cw)
    misc = jnp.concatenate([acc_qk[0:1, :HEAD_DIM], acc_qk[1:2, :HEAD_DIM], acc_sink[0:1, :nq]], axis=1)
    small = jnp.concatenate([
        _pad_cols(acc_x[0:1], wsm), _pad_cols(acc_g[0:1], wsm), _pad_cols(acc_head[0:1], wsm),
        _pad_cols(acc_head[1:2], wsm), _pad_cols(misc, wsm), _pad_cols(acc_conv[0:3], wsm),
        _pad_rows(_pad_cols(acc_head[2:3], wsm), SUBLANES)], axis=0)
    device = 2 * chip + lax.axis_index("c")
    (small_sems, last_sems), started, last_token = exchange_start(
        "small_join_late_start", [(plan_allgather_small, [0], 7), (plan_join, [1], 1)],
        [lax.dynamic_update_slice(jnp.zeros((8,) + small.shape, F32), small[None], (device, 0, 0))]
        + reduce_sum("late", late, grad_x))
    small_bufs, last_halves = started[:1], started[1:]
    big, after = {}, (last_token,)
    for nm, g, w, m, v in (("w_out", full_wo, w_out, m_w_out, v_w_out),
                           ("w_ple_gate", full_wg, w_ple_gate, m_w_ple_gate, v_w_ple_gate),
                           ("w_ple_proj", full_wp, w_ple_proj, m_w_ple_proj, v_w_ple_proj)):
        stepped = adamw(g, w[0], m[0], v[0], "adamw_" + nm, after=after)
        big[nm], after = [o[None] for o in stepped], (stepped[1],)
    full_wi, = exchange_wait("join_late_wait", plan_join, last_sems, last_halves, after[0])
    big["w_in"] = [o[None] for o in adamw(full_wi, w_in[0], m_w_in[0], v_w_in[0], "adamw_w_in")]

    gathered, = exchange_wait("small_wait", plan_allgather_small, small_sems, small_bufs, big["w_in"][1])
    tot = sum_devices(gathered)
    loss = tot[SUBLANES, 0]

    def pack(vals):
        ng, pg, bg, eg, qgv, kgv, sk, cv = vals
        misc_v = jnp.concatenate([qgv, kgv, sk], axis=1)
        return jnp.concatenate([_pad_cols(ng, wsm), _pad_cols(pg, wsm), _pad_cols(bg, wsm), _pad_cols(eg, wsm),
                                _pad_cols(misc_v, wsm), _pad_cols(cv[0], wsm)], axis=0)

    g_small = jnp.concatenate(
        [tot[0:5], _pad_cols(lax.dynamic_slice(tot[5:8], (0, chip * cq), (3, cq)), wsm)], axis=0)
    w_small = pack((norm_gain, ple_gate_norm_gain, b_ple_gate, ple_norm_gain, q_norm_gain, k_norm_gain,
                    attn_sinks, conv_w))
    m_small = pack((m_norm_gain, m_ple_gate_norm_gain, m_b_ple_gate, m_ple_norm_gain, m_q_norm_gain,
                    m_k_norm_gain, m_attn_sinks, m_conv_w))
    v_small = pack((v_norm_gain, v_ple_gate_norm_gain, v_b_ple_gate, v_ple_norm_gain, v_q_norm_gain,
                    v_k_norm_gain, v_attn_sinks, v_conv_w))
    sm = adamw(g_small, w_small, m_small, v_small, "adamw_small")

    def unpack(a):
        return {"norm_gain": a[0:1, :d], "ple_gate_norm_gain": a[1:2, :d], "b_ple_gate": a[2:3, :d],
                "ple_norm_gain": a[3:4, :d], "q_norm_gain": a[4:5, :HEAD_DIM],
                "k_norm_gain": a[4:5, HEAD_DIM:2 * HEAD_DIM],
                "attn_sinks": a[4:5, 2 * HEAD_DIM:2 * HEAD_DIM + nq], "conv_w": a[5:8, :cq][None]}

    order = ("norm_gain", "w_in", "q_norm_gain", "k_norm_gain", "attn_sinks", "conv_w", "w_out",
             "ple_gate_norm_gain", "w_ple_gate", "b_ple_gate", "w_ple_proj", "ple_norm_gain")
    outs = [loss, grad_x[None]]
    for kind in range(4):
        table = unpack(sm[kind])
        for nm in order:
            outs.append(big[nm][kind] if nm in big else table[nm])
    return tuple(outs)
```

```python
import functools

import jax
import jax.numpy as jnp
from jax import lax
from jax.experimental import pallas as pl
from jax.experimental.pallas import tpu as pltpu

F32 = jnp.float32
BF16 = jnp.bfloat16
MESH = pl.DeviceIdType.MESH

HEAD_DIM = 64
N_KV_HEADS = 4
KV_WIDTH = N_KV_HEADS * HEAD_DIM
BLOCK = 128
ROT_DIM = 16
ROPE_THETA = 500000.0
EPS = 1e-6
NEG_INF = -1e30
N_CHIPS = 4
LANES = 128
SUBLANES = 8
COLT = 512
SEG = 256
VMEM_LIMIT = 48 * 1024 * 1024
VMEM_LIMIT_BIG = 60 * 1024 * 1024

ADAM_LR = 0.001
ADAM_B1 = 0.9
ADAM_B2 = 0.999
ADAM_EPS = 1e-08
ADAM_WD = 0.01
ADAM_STEP = 10

NN = (((1,), (0,)), ((), ()))
NT = (((1,), (1,)), ((), ()))
TN = (((0,), (0,)), ((), ()))


def _call(body, **kw):
    return pl.pallas_call(body, **kw)


def _call_after(body, after, **kw):
    n_in, n_after = len(kw["in_specs"]), len(after)
    kw["in_specs"] = list(kw["in_specs"]) + [pl.BlockSpec(memory_space=pl.ANY)] * n_after

    def body_after(*refs):
        body(*refs[:n_in], *refs[n_in + n_after:])

    call = _call(body_after, **kw)
    return lambda *args: call(*args, *after)


def _params(sem, vmem=VMEM_LIMIT):
    return pltpu.CompilerParams(dimension_semantics=sem, vmem_limit_bytes=vmem)


def _tile(n, pref):
    return pref if n % pref == 0 else n


def _sigmoid(v):
    return 1.0 / (1.0 + jnp.exp(-v))


def _hbm():
    return pl.BlockSpec(memory_space=pl.ANY)


def cast_bf16(a, name):
    r, c = a.shape
    tr = _tile(r, 512)

    def body(a_ref, o_ref):
        o_ref[...] = a_ref[...].astype(BF16)

    return _call(body, name=name, grid=(r // tr,),
                 in_specs=[pl.BlockSpec((tr, c), lambda i: (i, 0))],
                 out_specs=pl.BlockSpec((tr, c), lambda i: (i, 0)),
                 out_shape=jax.ShapeDtypeStruct((r, c), BF16),
                 compiler_params=_params(("parallel",)))(a)


def cast_into_slot(a, chip, name):
    r, c = a.shape
    tr = _tile(r, 512)

    def body(chip_ref, a_ref, o_ref):
        o_ref[...] = a_ref[...].astype(BF16)

    grid_spec = pltpu.PrefetchScalarGridSpec(
        num_scalar_prefetch=1, grid=(r // tr,),
        in_specs=[pl.BlockSpec((tr, c), lambda i, chip_ref: (i, 0))],
        out_specs=pl.BlockSpec((None, tr, c), lambda i, chip_ref: (chip_ref[0], i, 0)))
    return _call(body, name=name, grid_spec=grid_spec,
                 out_shape=jax.ShapeDtypeStruct((N_CHIPS, r, c), BF16),
                 compiler_params=_params(("parallel",)))(chip, a)


def out_proj_norm(mix, wo, x, gain):
    s, d = x.shape
    tm = _tile(s, 512)

    def body(m_ref, w_ref, x_ref, g_ref, x1_ref, hg_ref):
        x1 = x_ref[...] + jnp.dot(m_ref[...], w_ref[...], preferred_element_type=F32)
        x1_ref[...] = x1
        r = lax.rsqrt(jnp.mean(x1 * x1, axis=-1, keepdims=True) + EPS)
        hg_ref[...] = ((x1 * r) * g_ref[...]).astype(BF16)

    row = pl.BlockSpec((tm, d), lambda i: (i, 0))
    return _call(body, name="out_proj_norm", grid=(s // tm,),
                 in_specs=[row, pl.BlockSpec((d, d), lambda i: (0, 0), pipeline_mode=pl.Buffered(1)), row,
                           pl.BlockSpec((1, d), lambda i: (0, 0))],
                 out_specs=[row, row],
                 out_shape=[jax.ShapeDtypeStruct((s, d), F32), jax.ShapeDtypeStruct((s, d), BF16)],
                 compiler_params=_params(("parallel",), VMEM_LIMIT_BIG))(mix, wo, x, gain)


def gate_proj_bwd_norm(d_gl, wg, xin, add, gain):
    s, d = xin.shape
    tm = _tile(s, 256)

    def body(dg_ref, w_ref, x_ref, a_ref, g_ref, dx_ref, dxb_ref, acc_ref):
        i = pl.program_id(0)
        dyv = lax.dot_general(dg_ref[...], w_ref[...], NT, preferred_element_type=F32)
        xf = x_ref[...]
        r = lax.rsqrt(jnp.mean(xf * xf, axis=-1, keepdims=True) + EPS)
        xhat = xf * r
        gd = dyv * g_ref[...]
        dx = a_ref[...] + r * (gd - xhat * jnp.mean(xhat * gd, axis=-1, keepdims=True))
        dx_ref[...] = dx
        dxb_ref[...] = dx.astype(BF16)

        @pl.when(i == 0)
        def _():
            acc_ref[...] = jnp.zeros_like(acc_ref)

        acc_ref[0:1, :] += jnp.sum(dyv * xhat, axis=0, keepdims=True)

    row = pl.BlockSpec((tm, d), lambda i: (i, 0))
    return _call(body, name="gate_proj_bwd_norm", grid=(s // tm,),
                 in_specs=[row, pl.BlockSpec((d, d), lambda i: (0, 0), pipeline_mode=pl.Buffered(1)), row, row,
                           pl.BlockSpec((1, d), lambda i: (0, 0))],
                 out_specs=[row, row, pl.BlockSpec((SUBLANES, d), lambda i: (0, 0))],
                 out_shape=[jax.ShapeDtypeStruct((s, d), F32), jax.ShapeDtypeStruct((s, d), BF16),
                            jax.ShapeDtypeStruct((SUBLANES, d), F32)],
                 compiler_params=_params(("arbitrary",), VMEM_LIMIT_BIG))(d_gl, wg, xin, add, gain)


def head_fwd_bwd(x1, hg, wg, p, wp, tgt, bias, ple_gain):
    s, d = x1.shape
    tm = _tile(s, 256)
    n_row = s // tm

    def body(x1_ref, hg_ref, wg_ref, p_ref, wp_ref, t_ref, b_ref, g_ref, dgl_ref, dy_ref, gwp_ref, acc_ref,
             gacc_ref):
        i = pl.program_id(0)
        gl = jnp.dot(hg_ref[...], wg_ref[...], preferred_element_type=F32)
        pb = p_ref[...].astype(BF16)
        pev = jnp.concatenate([jnp.dot(pb, wp_ref[q], preferred_element_type=F32) for q in range(N_CHIPS)],
                              axis=1)
        r = lax.rsqrt(jnp.mean(pev * pev, axis=-1, keepdims=True) + EPS)
        pehat = pev * r
        gain = g_ref[...]
        e = pehat * gain
        gate = _sigmoid(gl + b_ref[...])
        diff = (x1_ref[...] + gate * e) - t_ref[...]
        dy = diff * (1.0 / d)
        dy_ref[...] = dy
        d_gl = (dy * e) * (gate * (1.0 - gate))
        dgl_ref[...] = d_gl.astype(BF16)
        d_e = dy * gate
        gd = d_e * gain
        d_pe = r * (gd - pehat * jnp.mean(pehat * gd, axis=-1, keepdims=True))
        g_part = lax.dot_general(pb, d_pe.astype(BF16), TN, preferred_element_type=F32)

        @pl.when(i == 0)
        def _():
            acc_ref[...] = jnp.zeros_like(acc_ref)
            gacc_ref[...] = g_part

        @pl.when(i > 0)
        def _():
            gacc_ref[...] += g_part

        @pl.when(i == n_row - 1)
        def _():
            pq = d // N_CHIPS
            for q in range(N_CHIPS):
                gwp_ref[q] = gacc_ref[:, q * pq:(q + 1) * pq].astype(BF16)

        acc_ref[0:1, :] += jnp.sum(d_gl, axis=0, keepdims=True)
        acc_ref[1:2, :] += jnp.sum(d_e * pehat, axis=0, keepdims=True)
        acc_ref[2:3, :] += jnp.sum(diff * diff, axis=0, keepdims=True) * (0.5 / d)

    row = pl.BlockSpec((tm, d), lambda i: (i, 0))
    vec = pl.BlockSpec((1, d), lambda i: (0, 0))
    ple = p.shape[1]
    return _call(body, name="head_fwd_bwd", grid=(n_row,),
                 in_specs=[row, row, pl.BlockSpec((d, d), lambda i: (0, 0), pipeline_mode=pl.Buffered(1)),
                           pl.BlockSpec((tm, ple), lambda i: (i, 0)),
                           pl.BlockSpec(wp.shape, lambda i: (0, 0, 0)), row, vec, vec],
                 out_specs=[row, row, pl.BlockSpec(wp.shape, lambda i: (0, 0, 0)),
                            pl.BlockSpec((SUBLANES, d), lambda i: (0, 0))],
                 out_shape=[jax.ShapeDtypeStruct((s, d), BF16), jax.ShapeDtypeStruct((s, d), F32),
                            jax.ShapeDtypeStruct(wp.shape, BF16), jax.ShapeDtypeStruct((SUBLANES, d), F32)],
                 scratch_shapes=[pltpu.VMEM((ple, d), F32)],
                 compiler_params=_params(("arbitrary",), VMEM_LIMIT_BIG))(
                     x1, hg, wg, p, wp, tgt, bias, ple_gain)


def _adamw_step(gv, w_ref, m_ref, v_ref, go_ref, d_ref, mo_ref, vo_ref):
    mn = ADAM_B1 * m_ref[...] + (1.0 - ADAM_B1) * gv
    vn = ADAM_B2 * v_ref[...] + (1.0 - ADAM_B2) * (gv * gv)
    m_hat = mn / (1.0 - ADAM_B1 ** ADAM_STEP)
    v_hat = vn / (1.0 - ADAM_B2 ** ADAM_STEP)
    go_ref[...] = gv
    d_ref[...] = -ADAM_LR * (m_hat / (jnp.sqrt(v_hat) + ADAM_EPS) + ADAM_WD * w_ref[...])
    mo_ref[...] = mn
    vo_ref[...] = vn


def adamw(g, w, m, v, name, after=()):
    r, c = g.shape
    tr = _tile(r, 256)

    def body(g_ref, w_ref, m_ref, v_ref, go_ref, d_ref, mo_ref, vo_ref):
        _adamw_step(g_ref[...], w_ref, m_ref, v_ref, go_ref, d_ref, mo_ref, vo_ref)

    blk = pl.BlockSpec((tr, c), lambda i: (i, 0))
    shp = jax.ShapeDtypeStruct((r, c), F32)
    return _call_after(body, after, name=name, grid=(r // tr,), in_specs=[blk] * 4, out_specs=[blk] * 4,
                       out_shape=[shp] * 4, compiler_params=_params(("parallel",)))(g, w, m, v)


def matmul(a, b, *, grid, a_spec, b_spec, o_spec, out_shape, dims, name):
    nk = grid[2]
    acc_shape = tuple(d for d in o_spec.block_shape if d is not None)

    def body(a_ref, b_ref, o_ref, *scratch):
        part = lax.dot_general(a_ref[...].astype(BF16), b_ref[...].astype(BF16), dims, preferred_element_type=F32)
        if nk == 1:
            o_ref[...] = part.astype(o_ref.dtype)
            return
        acc_ref, = scratch
        k = pl.program_id(2)

        @pl.when(k == 0)
        def _():
            acc_ref[...] = part

        @pl.when((k > 0) & (k < nk - 1))
        def _():
            acc_ref[...] += part

        @pl.when(k == nk - 1)
        def _():
            o_ref[...] = (acc_ref[...] + part).astype(o_ref.dtype)

    return _call(body, name=name, grid=grid, in_specs=[a_spec, b_spec], out_specs=o_spec, out_shape=out_shape,
                 scratch_shapes=[pltpu.VMEM(acc_shape, F32)] if nk > 1 else [],
                 compiler_params=_params(("parallel", "parallel", "arbitrary")))(a, b)


def norm_in_proj(x, gain, w, order, in_w):
    s, d = x.shape
    sh = w.shape[1]
    tm = _tile(s, 512)

    def body(order_ref, x_ref, g_ref, w_ref, h_ref, z_ref):
        xf = x_ref[...]
        r = lax.rsqrt(jnp.mean(xf * xf, axis=-1, keepdims=True) + EPS)
        hb = ((xf * r) * g_ref[...]).astype(BF16)
        h_ref[...] = hb
        z_ref[...] = jnp.dot(hb, w_ref[...], preferred_element_type=F32)

    grid_spec = pltpu.PrefetchScalarGridSpec(
        num_scalar_prefetch=1, grid=(s // tm,),
        in_specs=[pl.BlockSpec((tm, d), lambda i, order_ref: (i, 0)),
                  pl.BlockSpec((1, d), lambda i, order_ref: (0, 0)),
                  pl.BlockSpec((d, sh), lambda i, order_ref: (0, 0), pipeline_mode=pl.Buffered(1))],
        out_specs=[pl.BlockSpec((tm, d), lambda i, order_ref: (i, 0)),
                   pl.BlockSpec((tm, sh), lambda i, order_ref: (i, order_ref[0]))])
    return _call(body, name="norm_in_proj", grid_spec=grid_spec,
                 out_shape=[jax.ShapeDtypeStruct((s, d), BF16), jax.ShapeDtypeStruct((s, in_w), F32)],
                 compiler_params=_params(("parallel",)))(order, x, gain, w)


def in_proj_part(h, w, z_prev, order, q, in_w):
    s, d = h.shape
    sh = w.shape[1]
    tm = _tile(s, 512)

    def body(order_ref, h_ref, w_ref, *rest):
        rest[-1][...] = jnp.dot(h_ref[...], w_ref[...], preferred_element_type=F32)

    in_specs = [pl.BlockSpec((tm, d), lambda i, order_ref: (i, 0)),
                pl.BlockSpec((d, sh), lambda i, order_ref: (0, 0))]
    args = [order, h, w]
    if z_prev is not None:
        in_specs.append(_hbm())
        args.append(z_prev)
    grid_spec = pltpu.PrefetchScalarGridSpec(
        num_scalar_prefetch=1, grid=(s // tm,), in_specs=in_specs,
        out_specs=pl.BlockSpec((tm, sh), lambda i, order_ref: (i, order_ref[q])))
    return _call(body, name="mm_z%d" % q, grid_spec=grid_spec,
                 out_shape=jax.ShapeDtypeStruct((s, in_w), F32),
                 input_output_aliases={3: 0} if z_prev is not None else {},
                 compiler_params=_params(("parallel",)))(*args)


def in_proj_wgrad_half(h, dz, g_prev, half, after):
    s, d = h.shape
    sh = dz.shape[1] // N_CHIPS
    hr = d // 2

    def body(half_ref, h_ref, dz_ref, *rest):
        rest[-1][...] = lax.dot_general(h_ref[...], dz_ref[...], TN, preferred_element_type=F32).astype(BF16)

    extra = ([g_prev] if g_prev is not None else []) + list(after)
    grid_spec = pltpu.PrefetchScalarGridSpec(
        num_scalar_prefetch=1, grid=(N_CHIPS,),
        in_specs=[pl.BlockSpec((s, hr), lambda j, half_ref: (0, half_ref[0]), pipeline_mode=pl.Buffered(1)),
                  pl.BlockSpec((s, sh), lambda j, half_ref: (0, j))] + [_hbm()] * len(extra),
        out_specs=pl.BlockSpec((None, hr, sh), lambda j, half_ref: (j, half_ref[0], 0)))
    return _call(body, name="mm_g_wi_%s" % ("keep" if g_prev is not None else "send"), grid_spec=grid_spec,
                 out_shape=jax.ShapeDtypeStruct((N_CHIPS, d, sh), BF16),
                 input_output_aliases={3: 0} if g_prev is not None else {},
                 compiler_params=_params(("parallel",), VMEM_LIMIT_BIG))(half, h, dz, *extra)


def in_proj_bwd_norm(dz, ws, order, xin, add, gain, after):
    s, d = xin.shape
    sh = ws[0].shape[1]
    tm = _tile(s, 256)
    nq, n_after = len(ws), len(after)

    def body(order_ref, *refs):
        a_refs, b_refs = refs[:nq], refs[nq:2 * nq]
        x_ref, add_ref, g_ref = refs[2 * nq:2 * nq + 3]
        dx_ref, acc_ref = refs[2 * nq + 3 + n_after:]
        i = pl.program_id(0)
        dyv = lax.dot_general(a_refs[0][...], b_refs[0][...], NT, preferred_element_type=F32)
        for q in range(1, nq):
            dyv += lax.dot_general(a_refs[q][...], b_refs[q][...], NT, preferred_element_type=F32)
        xf = x_ref[...]
        r = lax.rsqrt(jnp.mean(xf * xf, axis=-1, keepdims=True) + EPS)
        xhat = xf * r
        gd = dyv * g_ref[...]
        dx_ref[...] = add_ref[...] + r * (gd - xhat * jnp.mean(xhat * gd, axis=-1, keepdims=True))

        @pl.when(i == 0)
        def _():
            acc_ref[...] = jnp.zeros_like(acc_ref)

        acc_ref[0:1, :] += jnp.sum(dyv * xhat, axis=0, keepdims=True)

    a_spec = lambda q: pl.BlockSpec((tm, sh), functools.partial(lambda i, order_ref, q: (i, order_ref[q]), q=q))
    row = pl.BlockSpec((tm, d), lambda i, order_ref: (i, 0))
    grid_spec = pltpu.PrefetchScalarGridSpec(
        num_scalar_prefetch=1, grid=(s // tm,),
        in_specs=[a_spec(q) for q in range(nq)]
        + [pl.BlockSpec((d, sh), lambda i, order_ref: (0, 0), pipeline_mode=pl.Buffered(1))] * nq
        + [row, row, pl.BlockSpec((1, d), lambda i, order_ref: (0, 0))] + [_hbm()] * n_after,
        out_specs=[row, pl.BlockSpec((SUBLANES, d), lambda i, order_ref: (0, 0))])
    return _call(body, name="in_proj_bwd_norm", grid_spec=grid_spec,
                 out_shape=[jax.ShapeDtypeStruct((s, d), F32), jax.ShapeDtypeStruct((SUBLANES, d), F32)],
                 compiler_params=_params(("arbitrary",), VMEM_LIMIT_BIG))(
                     order, *([dz] * nq), *ws, xin, add, gain, *after)


def _seg_mean(sq, segb):
    parts = []
    for cgrp in range(sq.shape[1] // SEG):
        blk = sq[:, cgrp * SEG:(cgrp + 1) * SEG]
        hi = blk.astype(BF16)
        r1 = blk - hi.astype(F32)
        mid = r1.astype(BF16)
        lo = (r1 - mid.astype(F32)).astype(BF16)
        acc = jnp.dot(hi, segb, preferred_element_type=F32)
        acc += jnp.dot(mid, segb, preferred_element_type=F32)
        acc += jnp.dot(lo, segb, preferred_element_type=F32)
        parts.append(acc)
    out = parts[0] if len(parts) == 1 else jnp.concatenate(parts, axis=1)
    return out * (1.0 / HEAD_DIM)


def _rope(v, cos, sa, sb):
    parts = []
    for cgrp in range(v.shape[1] // LANES):
        blk = v[:, cgrp * LANES:(cgrp + 1) * LANES]
        parts.append(blk * cos + pltpu.roll(blk, LANES - 8, 1) * sa + pltpu.roll(blk, 8, 1) * sb)
    return parts[0] if len(parts) == 1 else jnp.concatenate(parts, axis=1)


def _rope_t(dv, cos, sa, sb):
    parts = []
    for cgrp in range(dv.shape[1] // LANES):
        blk = dv[:, cgrp * LANES:(cgrp + 1) * LANES]
        parts.append(blk * cos + pltpu.roll(blk * sa, 8, 1) + pltpu.roll(blk * sb, LANES - 8, 1))
    return parts[0] if len(parts) == 1 else jnp.concatenate(parts, axis=1)


def _tile_lanes(vec, width):
    reps = width // LANES
    return vec if reps == 1 else jnp.tile(vec, (1, reps))


def qk_prep_fwd(z, tabs, qg, kg, segb, aw, after=()):
    s = z.shape[0]
    tm = _tile(s, 512)
    wq = aw + 2 * KV_WIDTH

    def body(z_ref, cos_ref, sa_ref, sb_ref, qg_ref, kg_ref, seg_ref, qs_ref, ks_ref, vb_ref):
        zz = z_ref[...]
        q, k, v = zz[:, :aw], zz[:, aw:aw + KV_WIDTH], zz[:, aw + KV_WIDTH:]
        cos, sa, sb, segm = cos_ref[...], sa_ref[...], sb_ref[...], seg_ref[...]
        rq = lax.rsqrt(_seg_mean(q * q, segm) + EPS)
        qn = (q * rq) * _tile_lanes(qg_ref[...], aw)
        qs_ref[...] = (_rope(qn, cos, sa, sb) * (HEAD_DIM ** -0.5)).astype(BF16)
        rk = lax.rsqrt(_seg_mean(k * k, segm) + EPS)
        kn = (k * rk) * _tile_lanes(kg_ref[...], KV_WIDTH)
        ks_ref[...] = _rope(kn, cos, sa, sb).astype(BF16)
        vb_ref[...] = v.astype(BF16)

    tab = pl.BlockSpec((tm, LANES), lambda i: (i, 0))
    vec = pl.BlockSpec((1, LANES), lambda i: (0, 0))
    return _call_after(body, after, name="qk_prep_fwd", grid=(s // tm,),
                       in_specs=[pl.BlockSpec((tm, wq), lambda i: (i, 0)), tab, tab, tab, vec, vec,
                                 pl.BlockSpec((SEG, SEG), lambda i: (0, 0))],
                       out_specs=[pl.BlockSpec((tm, aw), lambda i: (i, 0)),
                                  pl.BlockSpec((tm, KV_WIDTH), lambda i: (i, 0)),
                                  pl.BlockSpec((tm, KV_WIDTH), lambda i: (i, 0))],
                       out_shape=[jax.ShapeDtypeStruct((s, aw), BF16), jax.ShapeDtypeStruct((s, KV_WIDTH), BF16),
                                  jax.ShapeDtypeStruct((s, KV_WIDTH), BF16)],
                       compiler_params=_params(("parallel",)))(z, *tabs, qg, kg, segb)


def qk_prep_bwd(z, dqs, dks, dvs, d_gate, tabs, qg, kg, segb, aw):
    s, in_w = z.shape
    tm = _tile(s, 512)
    wq = aw + 2 * KV_WIDTH

    def body(z_ref, dq_ref, dk_ref, dv_ref, dga_ref, cos_ref, sa_ref, sb_ref, qg_ref, kg_ref, seg_ref,
             dz_ref, acc_ref):
        i = pl.program_id(0)
        zz = z_ref[...]
        q, k = zz[:, :aw], zz[:, aw:aw + KV_WIDTH]
        cos, sa, sb, segm = cos_ref[...], sa_ref[...], sb_ref[...], seg_ref[...]

        def one(xv, dout, gvec, width):
            g = _tile_lanes(gvec, width)
            r = lax.rsqrt(_seg_mean(xv * xv, segm) + EPS)
            xhat = xv * r
            dn = _rope_t(dout, cos, sa, sb)
            gd = dn * g
            dx = r * (gd - xhat * _seg_mean(xhat * gd, segm))
            contrib = jnp.sum(dn * xhat, axis=0, keepdims=True)
            folded = contrib[:, :LANES]
            for cgrp in range(1, width // LANES):
                folded = folded + contrib[:, cgrp * LANES:(cgrp + 1) * LANES]
            return dx, folded + pltpu.roll(folded, HEAD_DIM, 1)

        dq, gq = one(q, dq_ref[...] * (HEAD_DIM ** -0.5), qg_ref[...], aw)
        dk, gk = one(k, dk_ref[...], kg_ref[...], KV_WIDTH)
        dz_ref[:, :aw] = dq.astype(BF16)
        dz_ref[:, aw:aw + KV_WIDTH] = dk.astype(BF16)
        dz_ref[:, aw + KV_WIDTH:wq] = dv_ref[...].astype(BF16)
        dz_ref[:, wq:] = dga_ref[...]

        @pl.when(i == 0)
        def _():
            acc_ref[...] = jnp.zeros_like(acc_ref)

        acc_ref[0:1, :] += gq
        acc_ref[1:2, :] += gk

    tab = pl.BlockSpec((tm, LANES), lambda i: (i, 0))
    vec = pl.BlockSpec((1, LANES), lambda i: (0, 0))
    kvb = pl.BlockSpec((tm, KV_WIDTH), lambda i: (i, 0))
    awb = pl.BlockSpec((tm, aw), lambda i: (i, 0))
    return _call(body, name="qk_prep_bwd", grid=(s // tm,),
                 in_specs=[pl.BlockSpec((tm, wq), lambda i: (i, 0)), awb, kvb, kvb, awb, tab, tab, tab, vec, vec,
                           pl.BlockSpec((SEG, SEG), lambda i: (0, 0))],
                 out_specs=[pl.BlockSpec((tm, wq + aw), lambda i: (i, 0)),
                            pl.BlockSpec((SUBLANES, LANES), lambda i: (0, 0))],
                 out_shape=[jax.ShapeDtypeStruct((s, in_w), BF16), jax.ShapeDtypeStruct((SUBLANES, LANES), F32)],
                 compiler_params=_params(("arbitrary",)))(z, dqs, dks, dvs, d_gate, *tabs, qg, kg, segb)


def _placed(band, lane_idx):
    out = {}
    for kh in range(N_KV_HEADS):
        grp = band[:, (kh // 2) * LANES:(kh // 2 + 1) * LANES]
        for half in (0, 1):
            t = grp if half == kh % 2 else pltpu.roll(grp, HEAD_DIM, 1)
            keep = (lane_idx >= half * HEAD_DIM) & (lane_idx < (half + 1) * HEAD_DIM)
            out[kh, half] = jnp.where(keep, t, jnp.zeros_like(t))
    return out


def _softmax_with_sink(sc, valid, sink):
    sc = jnp.where(valid, sc, NEG_INF)
    m = jnp.maximum(jnp.max(sc, axis=-1, keepdims=True), sink)
    e = jnp.exp(sc - m)
    es = jnp.exp(sink - m)
    den = jnp.sum(e, axis=-1, keepdims=True) + es
    return e / den, es / den


def _stack_halves(placed, kh):
    return jnp.concatenate([placed[kh, 0], placed[kh, 1]], axis=0)


def _valid_mask(n):
    r_i = lax.broadcasted_iota(jnp.int32, (BLOCK, 2 * BLOCK), 0)
    j_i = lax.broadcasted_iota(jnp.int32, (BLOCK, 2 * BLOCK), 1)
    return (j_i > r_i) & (j_i <= r_i + BLOCK) & ((n > 0) | (j_i >= BLOCK))


def attn_fwd(qs, ks, vb, z, sinks, aw, d, ga_off):
    s = qs.shape[0]
    nb = s // BLOCK
    nq = aw // HEAD_DIM
    grp_sz = nq // N_KV_HEADS
    n_ga = aw // COLT

    def body(q_ref, ko_ref, kp_ref, vo_ref, vp_ref, *rest):
        ga_refs = rest[:n_ga]
        sink_ref, attn_ref, mix_ref = rest[n_ga:]
        n = pl.program_id(0)
        lane_idx = lax.broadcasted_iota(jnp.int32, (2 * BLOCK, LANES), 1)
        kpl = _placed(jnp.concatenate([kp_ref[...], ko_ref[...]], axis=0), lane_idx)
        vpl = _placed(jnp.concatenate([vp_ref[...], vo_ref[...]], axis=0), lane_idx)
        valid = _valid_mask(n)
        groups = range(nq // 2)
        kcat = [_stack_halves(kpl, kh) for kh in range(N_KV_HEADS)]
        vcat = [_stack_halves(vpl, kh) for kh in range(N_KV_HEADS)]
        scs = [lax.dot_general(q_ref[:, c * LANES:(c + 1) * LANES], kcat[2 * c // grp_sz], NT,
                               preferred_element_type=F32) for c in groups]
        probs = [jnp.concatenate(
            [_softmax_with_sink(scs[c][:, half * 2 * BLOCK:(half + 1) * 2 * BLOCK], valid,
                                sink_ref[0, 2 * c + half])[0].astype(BF16) for half in (0, 1)], axis=1)
                 for c in groups]
        for c in groups:
            acc = jnp.dot(probs[c], vcat[2 * c // grp_sz], preferred_element_type=F32)
            attn_ref[:, c * LANES:(c + 1) * LANES] = acc
            col = c * LANES
            ga = ga_refs[col // COLT][:, col % COLT:col % COLT + LANES]
            mix_ref[:, c * LANES:(c + 1) * LANES] = (acc * (ga * _sigmoid(ga))).astype(BF16)

    own = lambda n: (n, 0)
    prev = lambda n: (jnp.maximum(n - 1, 0), 0)
    kvs = lambda imap: pl.BlockSpec((BLOCK, KV_WIDTH), imap)
    ga_specs = [pl.BlockSpec((BLOCK, COLT), functools.partial(lambda n, j: (n, ga_off + j), j=j))
                for j in range(n_ga)]
    return _call(body, name="attn_fwd", grid=(nb,),
                 in_specs=[pl.BlockSpec((BLOCK, aw), own), kvs(own), kvs(prev), kvs(own), kvs(prev)]
                 + ga_specs + [pl.BlockSpec(memory_space=pltpu.SMEM)],
                 out_specs=[pl.BlockSpec((BLOCK, aw), own), pl.BlockSpec((BLOCK, aw), own)],
                 out_shape=[jax.ShapeDtypeStruct((s, aw), F32), jax.ShapeDtypeStruct((s, d), BF16)],
                 compiler_params=_params(("parallel",)))(qs, ks, ks, vb, vb, *([z] * n_ga), sinks)


def out_proj_bwd_gate(d_x1b, wo, attn, z, aw, ga_off, after=()):
    s, d = d_x1b.shape
    tm = _tile(s, 512)
    n_ga = aw // COLT

    def body(dx_ref, w_ref, at_ref, *rest):
        ga_refs, (do_ref, dga_ref, dmc_ref) = rest[:n_ga], rest[n_ga:]
        dm = lax.dot_general(dx_ref[...], w_ref[...], NT, preferred_element_type=F32)
        dmc_ref[...] = dm[:, aw:]
        for j in range(n_ga):
            cols = slice(j * COLT, (j + 1) * COLT)
            ga = ga_refs[j][...]
            sig = _sigmoid(ga)
            dma = dm[:, cols]
            do_ref[:, cols] = (dma * (ga * sig)).astype(BF16)
            dga_ref[:, cols] = ((dma * at_ref[:, cols]) * (sig * (1.0 + ga * (1.0 - sig)))).astype(BF16)

    row = lambda width: pl.BlockSpec((tm, width), lambda i: (i, 0))
    ga_specs = [pl.BlockSpec((tm, COLT), functools.partial(lambda i, j: (i, ga_off + j), j=j)) for j in range(n_ga)]
    return _call_after(body, after, name="out_proj_bwd_gate", grid=(s // tm,),
                       in_specs=[row(d), pl.BlockSpec((d, d), lambda i: (0, 0), pipeline_mode=pl.Buffered(1)),
                                 row(aw)] + ga_specs,
                       out_specs=[row(aw), row(aw), row(d - aw)],
                       out_shape=[jax.ShapeDtypeStruct((s, aw), BF16), jax.ShapeDtypeStruct((s, aw), BF16),
                                  jax.ShapeDtypeStruct((s, d - aw), F32)],
                       compiler_params=_params(("parallel",)))(d_x1b, wo, attn, *([z] * n_ga))


def attn_bwd(qs, ks, vb, d_o, sinks, aw, after=()):
    s = qs.shape[0]
    nb = s // BLOCK
    nq = aw // HEAD_DIM
    grp_sz = nq // N_KV_HEADS

    def body(q_ref, ko_ref, kp_ref, vo_ref, vp_ref, do_ref, sink_ref, dq_ref, dk_ref, dv_ref, ds_ref,
             ck_ref, cv_ref):
        n = pl.program_id(0)

        @pl.when(n == 0)
        def _():
            ds_ref[...] = jnp.zeros_like(ds_ref)
            dk_ref[...] = jnp.zeros_like(dk_ref)
            dv_ref[...] = jnp.zeros_like(dv_ref)

        @pl.when(n < nb)
        def _():
            lane_idx = lax.broadcasted_iota(jnp.int32, (2 * BLOCK, LANES), 1)
            lane_row = lax.broadcasted_iota(jnp.int32, (1, LANES), 1)
            kpl = _placed(jnp.concatenate([kp_ref[...], ko_ref[...]], axis=0), lane_idx)
            vpl = _placed(jnp.concatenate([vp_ref[...], vo_ref[...]], axis=0), lane_idx)
            valid = _valid_mask(n)
            ds_row = jnp.zeros((1, LANES), F32)
            wide = 2 * BLOCK
            groups = range(nq // 2)
            per_kv = grp_sz // 2
            kcat = [_stack_halves(kpl, kh) for kh in range(N_KV_HEADS)]
            vcat = [_stack_halves(vpl, kh) for kh in range(N_KV_HEADS)]
            qg = [q_ref[:, c * LANES:(c + 1) * LANES] for c in groups]
            dog = [do_ref[:, c * LANES:(c + 1) * LANES] for c in groups]
            scs = [lax.dot_general(qg[c], kcat[c // per_kv], NT, preferred_element_type=F32) for c in groups]
            dps = [lax.dot_general(dog[c], vcat[c // per_kv], NT, preferred_element_type=F32) for c in groups]
            ds_pairs, p_pairs = [], []
            for c in groups:
                probs, dss = [], []
                for half in (0, 1):
                    h = 2 * c + half
                    cols = slice(half * wide, (half + 1) * wide)
                    p, p_sink = _softmax_with_sink(scs[c][:, cols], valid, sink_ref[0, h])
                    delta = jnp.sum(p * dps[c][:, cols], axis=-1, keepdims=True)
                    dss.append((p * (dps[c][:, cols] - delta)).astype(BF16))
                    probs.append(p.astype(BF16))
                    dsink = -jnp.sum(p_sink * delta, axis=0, keepdims=True)
                    ds_row += jnp.where(lane_row == h, dsink, 0.0)
                ds_pairs.append(jnp.concatenate(dss, axis=1))
                p_pairs.append(jnp.concatenate(probs, axis=1))
            for c in groups:
                dq_ref[:, c * LANES:(c + 1) * LANES] = jnp.dot(ds_pairs[c], kcat[c // per_kv],
                                                               preferred_element_type=F32)
            dkts = [lax.dot_general(qg[c], ds_pairs[c], TN, preferred_element_type=F32) for c in groups]
            dvts = [lax.dot_general(dog[c], p_pairs[c], TN, preferred_element_type=F32) for c in groups]
            dkt, dvt = [], []
            for kh in range(N_KV_HEADS):
                for parts, out in ((dkts, dkt), (dvts, dvt)):
                    tot = parts[kh * per_kv]
                    for c in range(kh * per_kv + 1, (kh + 1) * per_kv):
                        tot = tot + parts[c]
                    out.append(tot[:HEAD_DIM, :wide] + tot[HEAD_DIM:, wide:])
            ds_ref[0:1, :] += ds_row
            back = lambda t: jnp.concatenate(
                [jnp.concatenate(t[2 * g:2 * g + 2], axis=0).T for g in range(N_KV_HEADS // 2)], axis=1)
            dk_band, dv_band = back(dkt), back(dvt)

            @pl.when(n > 0)
            def _():
                dk_ref[...] = ck_ref[...] + dk_band[:BLOCK]
                dv_ref[...] = cv_ref[...] + dv_band[:BLOCK]

            ck_ref[...] = dk_band[BLOCK:]
            cv_ref[...] = dv_band[BLOCK:]

        @pl.when(n == nb)
        def _():
            dk_ref[...] = ck_ref[...]
            dv_ref[...] = cv_ref[...]

    own = lambda n: (jnp.minimum(n, nb - 1), 0)
    prev = lambda n: (jnp.clip(n - 1, 0, nb - 1), 0)
    done = lambda n: (jnp.maximum(n - 1, 0), 0)
    kvs = lambda imap: pl.BlockSpec((BLOCK, KV_WIDTH), imap)
    return _call_after(body, after, name="attn_bwd", grid=(nb + 1,),
                       in_specs=[pl.BlockSpec((BLOCK, aw), own), kvs(own), kvs(prev), kvs(own), kvs(prev),
                                 pl.BlockSpec((BLOCK, aw), own), pl.BlockSpec(memory_space=pltpu.SMEM)],
                       out_specs=[pl.BlockSpec((BLOCK, aw), own), kvs(done), kvs(done),
                                  pl.BlockSpec((SUBLANES, LANES), lambda n: (0, 0))],
                       out_shape=[jax.ShapeDtypeStruct((s, aw), F32), jax.ShapeDtypeStruct((s, KV_WIDTH), F32),
                                  jax.ShapeDtypeStruct((s, KV_WIDTH), F32),
                                  jax.ShapeDtypeStruct((SUBLANES, LANES), F32)],
                       scratch_shapes=[pltpu.VMEM((BLOCK, KV_WIDTH), F32), pltpu.VMEM((BLOCK, KV_WIDTH), F32)],
                       compiler_params=_params(("arbitrary",)))(qs, ks, ks, vb, vb, d_o, sinks)


def conv_fwd(z, conv_w, mix, aw, cw, b_off):
    s = z.shape[0]
    tm = _tile(s, 1024)
    nseg = cw // COLT
    hb = tm // SUBLANES

    def body(b_ref, c_ref, h_ref, g_ref, cp_ref, hp_ref, w_ref, mix_in, mix_ref):
        i = pl.program_id(0)
        u = c_ref[...] * h_ref[...]
        up = jnp.where(i > 0, cp_ref[...] * hp_ref[...], 0.0)
        ext = jnp.concatenate([up, u], axis=0)
        um1 = pltpu.roll(ext, 1, 0)[SUBLANES:]
        um2 = pltpu.roll(ext, 2, 0)[SUBLANES:]
        w = w_ref[...]
        cv = w[0:1] * um2 + w[1:2] * um1 + w[2:3] * u
        g = g_ref[...]
        mix_ref[...] = ((b_ref[...] * cv) * (g * _sigmoid(g))).astype(BF16)

    seg = lambda k: pl.BlockSpec((tm, COLT), functools.partial(lambda i, j, k: (i, b_off + k * nseg + j), k=k))
    halo = lambda k: pl.BlockSpec(
        (SUBLANES, COLT), functools.partial(lambda i, j, k: (jnp.maximum(i * hb - 1, 0), b_off + k * nseg + j), k=k))
    return _call(body, name="conv_fwd", grid=(s // tm, nseg),
                 in_specs=[seg(0), seg(1), seg(2), seg(3), halo(1), halo(2),
                           pl.BlockSpec((SUBLANES, COLT), lambda i, j: (0, j)), _hbm()],
                 out_specs=pl.BlockSpec((tm, COLT), lambda i, j: (i, aw // COLT + j)),
                 out_shape=jax.ShapeDtypeStruct(mix.shape, BF16),
                 input_output_aliases={7: 0},
                 compiler_params=_params(("parallel", "parallel")))(z, z, z, z, z, z, conv_w, mix)


def conv_bwd(z, d_mix, conv_w, dz, aw, cw, b_off):
    s = z.shape[0]
    tm = _tile(s, 512)
    nseg = cw // COLT
    hb = tm // SUBLANES
    n_row = s // tm
    last_h = s // SUBLANES - 1
    n_step = nseg * n_row

    def body(b_ref, c_ref, h_ref, g_ref, cp_ref, hp_ref, bn_ref, gn_ref, dm_ref, dmn_ref, w_ref, dz_in,
             dz_ref, acc_ref, stash_ref, sems):
        j = pl.program_id(0)
        i = pl.program_id(1)
        step = j * n_row + i
        slot = step % 2

        def copies(from_slot, at_step):
            jj, ii = at_step // n_row, at_step % n_row
            rows = pl.ds(pl.multiple_of(ii * tm, tm), tm)
            return [pltpu.make_async_copy(
                stash_ref.at[from_slot, q],
                dz_ref.at[rows, pl.ds(pl.multiple_of((b_off + q * nseg + jj) * COLT, COLT), COLT)],
                sems.at[from_slot, q]) for q in range(4)]

        @pl.when(step >= 2)
        def _():
            for cp in copies(slot, step - 2):
                cp.wait()

        cc, hh, bb, g = c_ref[...], h_ref[...], b_ref[...], g_ref[...]
        w = w_ref[...]
        u = cc * hh
        up = jnp.where(i > 0, cp_ref[...] * hp_ref[...], 0.0)
        ext = jnp.concatenate([up, u], axis=0)
        um1 = pltpu.roll(ext, 1, 0)[SUBLANES:]
        um2 = pltpu.roll(ext, 2, 0)[SUBLANES:]
        cv = w[0:1] * um2 + w[1:2] * um1 + w[2:3] * u
        sig = _sigmoid(g)
        sg = g * sig
        dm = dm_ref[...]
        d_cv = (dm * bb) * sg
        gn = gn_ref[...]
        d_cv_next = jnp.where(i < n_row - 1, (dmn_ref[...] * bn_ref[...]) * (gn * _sigmoid(gn)), 0.0)
        ext2 = jnp.concatenate([d_cv, d_cv_next], axis=0)
        dp1 = pltpu.roll(ext2, tm + SUBLANES - 1, 0)[:tm]
        dp2 = pltpu.roll(ext2, tm + SUBLANES - 2, 0)[:tm]
        d_u = w[2:3] * d_cv + w[1:2] * dp1 + w[0:1] * dp2
        stash_ref[slot, 0] = ((dm * cv) * sg).astype(BF16)
        stash_ref[slot, 1] = (d_u * hh).astype(BF16)
        stash_ref[slot, 2] = (d_u * cc).astype(BF16)
        stash_ref[slot, 3] = (((dm * bb) * cv) * (sig * (1.0 + g * (1.0 - sig)))).astype(BF16)
        for cp in copies(slot, step):
            cp.start()

        @pl.when(i == 0)
        def _():
            acc_ref[...] = jnp.zeros_like(acc_ref)

        acc_ref[0:1, :] += jnp.sum(d_cv * um2, axis=0, keepdims=True)
        acc_ref[1:2, :] += jnp.sum(d_cv * um1, axis=0, keepdims=True)
        acc_ref[2:3, :] += jnp.sum(d_cv * u, axis=0, keepdims=True)

        @pl.when(step == n_step - 1)
        def _():
            if n_step >= 2:
                for cp in copies(1 - slot, step - 1):
                    cp.wait()
            for cp in copies(slot, step):
                cp.wait()

    seg = lambda q: pl.BlockSpec((tm, COLT), functools.partial(lambda j, i, q: (i, b_off + q * nseg + j), q=q))
    halo_p = lambda q: pl.BlockSpec(
        (SUBLANES, COLT),
        functools.partial(lambda j, i, q: (jnp.maximum(i * hb - 1, 0), b_off + q * nseg + j), q=q))
    halo_n = lambda q: pl.BlockSpec(
        (SUBLANES, COLT),
        functools.partial(lambda j, i, q: (jnp.minimum((i + 1) * hb, last_h), b_off + q * nseg + j), q=q))
    return _call(body, name="conv_bwd", grid=(nseg, n_row),
                 in_specs=[seg(0), seg(1), seg(2), seg(3), halo_p(1), halo_p(2), halo_n(0), halo_n(3),
                           pl.BlockSpec((tm, COLT), lambda j, i: (i, j)),
                           pl.BlockSpec((SUBLANES, COLT), lambda j, i: (jnp.minimum((i + 1) * hb, last_h), j)),
                           pl.BlockSpec((SUBLANES, COLT), lambda j, i: (0, j)), _hbm()],
                 out_specs=[_hbm(), pl.BlockSpec((SUBLANES, COLT), lambda j, i: (0, j))],
                 out_shape=[jax.ShapeDtypeStruct(dz.shape, BF16), jax.ShapeDtypeStruct((SUBLANES, cw), F32)],
                 input_output_aliases={11: 0},
                 scratch_shapes=[pltpu.VMEM((2, 4, tm, COLT), BF16), pltpu.SemaphoreType.DMA((2, 4))],
                 compiler_params=_params(("arbitrary", "arbitrary")))(
                     z, z, z, z, z, z, z, z, d_mix, d_mix, conv_w, dz)


def _place():
    x, y, c = lax.axis_index("x"), lax.axis_index("y"), lax.axis_index("c")
    chips = [(1 - x, y), (x, 1 - y), (1 - x, 1 - y)]
    return x, y, c, chips


def _remote(src, dst, send_sem, recv_sem, device):
    return pltpu.make_async_remote_copy(src_ref=src, dst_ref=dst, send_sem=send_sem, recv_sem=recv_sem,
                                        device_id=device, device_id_type=MESH)


def plan_gather_ici(n_split):
    def plan(refs, send, recv):
        x, y, c, chips = _place()
        me = 2 * x + y
        mine, theirs = [], []
        for w, ref in enumerate(refs):
            for j, (cx, cy) in enumerate(chips):
                def part(slot):
                    if w >= n_split:
                        return ref.at[slot]
                    hr = ref.shape[1] // 2
                    return ref.at[slot, pl.ds(c * hr, hr)]
                k = 3 * w + j
                mine.append(_remote(part(me), part(me), send.at[k], recv.at[k], (cx, cy, c)))
                theirs.append(_remote(part(2 * cx + cy), part(2 * cx + cy), send.at[k], recv.at[k], (cx, cy, c)))
        return mine, theirs
    return plan


def plan_shard_ici(j):
    def plan(refs, send, recv):
        _, _, c, chips = _place()
        own, land = refs
        hr = own.shape[0] // 2
        rows = pl.ds(c * hr, hr)
        cp = _remote(own.at[rows], land.at[rows], send.at[0], recv.at[0], (*chips[j], c))
        return [cp], [cp]
    return plan


def plan_shard_relay(refs, send, recv):
    _, _, c, chips = _place()
    land_x, land_y, land_far = refs
    hr = land_far.shape[0] // 2
    quarter = lambda q: pl.ds(c * hr + q * (hr // 2), hr // 2)
    mine = [_remote(land_y.at[quarter(0)], land_far.at[quarter(0)], send.at[0], recv.at[0], (*chips[0], c)),
            _remote(land_x.at[quarter(1)], land_far.at[quarter(1)], send.at[1], recv.at[1], (*chips[1], c))]
    return mine, mine


def plan_shard_pass(refs, send, recv):
    x, y, c, _ = _place()
    mine, theirs = [], []
    for w, land in enumerate(refs):
        hr = land.shape[0] // 2
        half = lambda core: land.at[pl.ds(core * hr, hr)]
        mine.append(_remote(half(c), half(c), send.at[w], recv.at[w], (x, y, 1 - c)))
        theirs.append(_remote(half(1 - c), half(1 - c), send.at[w], recv.at[w], (x, y, 1 - c)))
    return mine, theirs


def plan_gather_pass(refs, send, recv):
    x, y, c, chips = _place()
    mine, theirs = [], []
    for w, ref in enumerate(refs):
        hr = ref.shape[1] // 2
        for j, (cx, cy) in enumerate(chips):
            half = lambda core: ref.at[2 * cx + cy, pl.ds(core * hr, hr)]
            k = 3 * w + j
            mine.append(_remote(half(c), half(c), send.at[k], recv.at[k], (x, y, 1 - c)))
            theirs.append(_remote(half(1 - c), half(1 - c), send.at[k], recv.at[k], (x, y, 1 - c)))
    return mine, theirs


def plan_swap(refs, send, recv):
    x, y, c, _ = _place()
    nw = len(refs) // 2
    mine = []
    for w in range(nw):
        hr = refs[w].shape[1] // 2
        mine.append(_remote(refs[w].at[:, pl.ds((1 - c) * hr, hr), :], refs[nw + w], send.at[w], recv.at[w],
                            (x, y, 1 - c)))
    return mine, mine


def plan_scatter(refs, send, recv):
    x, y, c, chips = _place()
    me = 2 * x + y
    nw = len(refs) // 2
    mine, theirs = [], []
    for w in range(nw):
        for j, (cx, cy) in enumerate(chips):
            k = 3 * w + j
            mine.append(_remote(refs[w].at[2 * cx + cy], refs[nw + w].at[me], send.at[k], recv.at[k], (cx, cy, c)))
            theirs.append(_remote(refs[w].at[me], refs[nw + w].at[2 * cx + cy], send.at[k], recv.at[k], (cx, cy, c)))
    return mine, theirs


def plan_join(refs, send, recv):
    x, y, c, _ = _place()
    mine, theirs = [], []
    for w, ref in enumerate(refs):
        hr = ref.shape[0] // 2
        half = lambda core: ref.at[pl.ds(core * hr, hr)]
        mine.append(_remote(half(c), half(c), send.at[w], recv.at[w], (x, y, 1 - c)))
        theirs.append(_remote(half(1 - c), half(1 - c), send.at[w], recv.at[w], (x, y, 1 - c)))
    return mine, theirs


def exchange(name, plan, arrays, n, after=()):
    na = len(arrays)

    def body(*refs):
        mine, theirs = plan(refs[:na], refs[2 * na], refs[2 * na + 1])
        for cp in mine:
            cp.start()
        for cp in theirs:
            cp.wait_recv()
        for cp in mine:
            cp.wait_send()

    return _call_after(body, after, name=name, in_specs=[_hbm()] * na, out_specs=[_hbm()] * na,
                       out_shape=[jax.ShapeDtypeStruct(a.shape, a.dtype) for a in arrays],
                       input_output_aliases={i: i for i in range(na)},
                       scratch_shapes=[pltpu.SemaphoreType.DMA((n,)), pltpu.SemaphoreType.DMA((n,))])(*arrays)


def exchange_start(name, groups, arrays, after=()):
    na, ng = len(arrays), len(groups)

    def body(*refs):
        for g, (plan, idx, _) in enumerate(groups):
            mine, _ = plan([refs[i] for i in idx], refs[na + 2 * g], refs[na + 2 * g + 1])
            for cp in mine:
                cp.start()
        refs[-1][...] = jnp.zeros_like(refs[-1])

    hbm = pl.BlockSpec(memory_space=pltpu.HBM)
    sem = pl.BlockSpec(memory_space=pltpu.SEMAPHORE)
    sem_types = [pltpu.SemaphoreType.DMA((n,)) for _, _, n in groups for _ in (0, 1)]
    outs = _call_after(body, after, name=name, in_specs=[hbm] * na,
                       out_specs=[sem] * (2 * ng) + [hbm] * na + [pl.BlockSpec(memory_space=pltpu.VMEM)],
                       out_shape=sem_types + [pltpu.HBM(a.shape, a.dtype) for a in arrays]
                       + [jax.ShapeDtypeStruct((SUBLANES, LANES), F32)],
                       input_output_aliases={i: i + 2 * ng for i in range(na)},
                       compiler_params=pltpu.CompilerParams(
                           has_side_effects=pltpu.SideEffectType.DATAFLOW_SIDE_EFFECTING))(
                               *[pltpu.with_memory_space_constraint(a, pltpu.HBM) for a in arrays])
    sems = [(outs[2 * g], outs[2 * g + 1]) for g in range(ng)]
    return sems, list(outs[2 * ng:-1]), outs[-1]


def exchange_wait(name, plan, sems, arrays, after):
    send_sems, recv_sems = sems
    na = len(arrays)
    after = tuple(after) if isinstance(after, (tuple, list)) else (after,)

    def body(*refs):
        mine, theirs = plan(refs[:na], refs[na], refs[na + 1])
        for cp in mine:
            cp.wait_send()
        for cp in theirs:
            cp.wait_recv()

    hbm = pl.BlockSpec(memory_space=pltpu.HBM)
    sem = pl.BlockSpec(memory_space=pltpu.SEMAPHORE)
    return _call(body, name=name, in_specs=[hbm] * na + [sem, sem] + [_hbm()] * len(after),
                 out_specs=[hbm] * na, out_shape=[pltpu.HBM(a.shape, a.dtype) for a in arrays],
                 input_output_aliases={i: i for i in range(na)},
                 compiler_params=pltpu.CompilerParams(
                     has_side_effects=pltpu.SideEffectType.DATAFLOW_SIDE_EFFECTING))(
                         *arrays, send_sems, recv_sems, *after)


def plan_allgather_small(refs, send, recv):
    x, y, c, _ = _place()
    buf, = refs
    mine, theirs = [], []
    flips = [(fx, fy, fc) for fx in (0, 1) for fy in (0, 1) for fc in (0, 1)][1:]
    for k, (fx, fy, fc) in enumerate(flips):
        peer = (x ^ fx, y ^ fy, c ^ fc)
        slot = lambda px, py, pc: buf.at[4 * px + 2 * py + pc]
        mine.append(_remote(slot(x, y, c), slot(x, y, c), send.at[k], recv.at[k], peer))
        theirs.append(_remote(slot(*peer), slot(*peer), send.at[k], recv.at[k], peer))
    return mine, theirs


def add_sibling(grad, got, core, name):
    _, r, c = grad.shape
    hr = r // 2
    tr = _tile(hr, 512)
    nblk = hr // tr

    def body(core_ref, g_ref, o_ref, out_ref):
        out_ref[...] = (g_ref[...].astype(F32) + o_ref[...].astype(F32)).astype(BF16)

    grid_spec = pltpu.PrefetchScalarGridSpec(
        num_scalar_prefetch=1, grid=(N_CHIPS, nblk),
        in_specs=[pl.BlockSpec((None, tr, c), lambda t, i, core_ref: (t, core_ref[0] * nblk + i, 0)),
                  pl.BlockSpec((None, tr, c), lambda t, i, core_ref: (t, i, 0))],
        out_specs=pl.BlockSpec((None, tr, c), lambda t, i, core_ref: (t, i, 0)))
    return _call(body, name=name, grid_spec=grid_spec,
                 out_shape=jax.ShapeDtypeStruct((N_CHIPS, hr, c), BF16),
                 compiler_params=_params(("parallel", "parallel")))(core, grad, got)


def sum_chips(mine, owned, place, name):
    _, hr, c = mine.shape
    tr = _tile(hr, 512)
    nblk = hr // tr

    def body(place_ref, m_ref, o1_ref, o2_ref, o3_ref, out_ref):
        acc = m_ref[...].astype(F32)
        for o_ref in (o1_ref, o2_ref, o3_ref):
            acc = acc + o_ref[...].astype(F32)
        out_ref[...] = acc

    other = lambda k: pl.BlockSpec((None, tr, c), lambda i, place_ref: ((place_ref[0] + k) % N_CHIPS, i, 0))
    grid_spec = pltpu.PrefetchScalarGridSpec(
        num_scalar_prefetch=1, grid=(nblk,),
        in_specs=[other(0), other(1), other(2), other(3)],
        out_specs=pl.BlockSpec((tr, c), lambda i, place_ref: (place_ref[1] * nblk + i, 0)))
    return _call(body, name=name, grid_spec=grid_spec,
                 out_shape=jax.ShapeDtypeStruct((2 * hr, c), F32),
                 compiler_params=_params(("parallel",)))(place, mine, owned, owned, owned)


def small_step(gathered, chip, rows, params, conv_row, conv):
    n_dev, n_rows, _ = gathered.shape
    cr, cq = conv[0].shape
    groups = list(params) + [conv]
    n_par = len(groups)

    def body(chip_ref, g_ref, gc_ref, *refs):
        ins, outs, loss_ref = refs[:3 * n_par], refs[3 * n_par:7 * n_par], refs[7 * n_par]
        tot = g_ref[0]
        conv_g = gc_ref[0, conv_row:conv_row + cr, :]
        for dev in range(1, n_dev):
            tot = tot + g_ref[dev]
            conv_g = conv_g + gc_ref[dev, conv_row:conv_row + cr, :]
        loss_ref[...] = jnp.sum(tot[SUBLANES:SUBLANES + 1], axis=1, keepdims=True)
        grads = [tot[r:r + 1, :w] for r, w in rows] + [conv_g]
        for i, gv in enumerate(grads):
            _adamw_step(gv, *ins[3 * i:3 * i + 3], *outs[4 * i:4 * i + 4])

    def whole(a):
        return pl.BlockSpec(a.shape, lambda i, chip_ref, n=len(a.shape): (0,) * n)

    flat = [a for group in groups for a in group]
    out_shape = [jax.ShapeDtypeStruct(group[0].shape, F32) for group in groups for _ in range(4)]
    out_shape.append(jax.ShapeDtypeStruct((1, 1), F32))
    grid_spec = pltpu.PrefetchScalarGridSpec(
        num_scalar_prefetch=1, grid=(1,),
        in_specs=[whole(gathered), pl.BlockSpec((n_dev, n_rows, cq), lambda i, chip_ref: (0, 0, chip_ref[0]))]
        + [whole(a) for a in flat],
        out_specs=[whole(o) for o in out_shape])
    res = _call(body, name="small_step", grid_spec=grid_spec, out_shape=out_shape,
                compiler_params=_params(("arbitrary",)))(chip, gathered, gathered, *flat)
    return res[-1], [res[4 * i:4 * i + 4] for i in range(n_par)]


def _rope_tables(s):
    half = ROT_DIM // 2
    inv_freq = jnp.power(jnp.float32(ROPE_THETA), -jnp.arange(half, dtype=F32) * 2.0 / ROT_DIM)
    freq64 = jnp.concatenate([inv_freq, inv_freq, jnp.zeros((HEAD_DIM - ROT_DIM,), F32)])
    freq = jnp.concatenate([freq64, freq64])[None, :]
    dim = (jnp.arange(LANES) % HEAD_DIM)[None, :]
    ang = jnp.arange(s).astype(F32)[:, None] * freq
    cos, sin = jnp.cos(ang), jnp.sin(ang)
    return cos, jnp.where(dim < half, -sin, 0.0), jnp.where((dim >= half) & (dim < ROT_DIM), sin, 0.0)


def _pad_rows(a, rows):
    return jnp.pad(a, ((0, rows - a.shape[0]), (0, 0)))


def _pad_cols(a, cols):
    return jnp.pad(a, ((0, 0), (0, cols - a.shape[1])))


def kernel(x, p, norm_gain, w_in, q_norm_gain, k_norm_gain, attn_sinks, conv_w, w_out, ple_gate_norm_gain, w_ple_gate, b_ple_gate, w_ple_proj, ple_norm_gain, loss_target, m_norm_gain, m_w_in, m_q_norm_gain, m_k_norm_gain, m_attn_sinks, m_conv_w, m_w_out, m_ple_gate_norm_gain, m_w_ple_gate, m_b_ple_gate, m_w_ple_proj, m_ple_norm_gain, v_norm_gain, v_w_in, v_q_norm_gain, v_k_norm_gain, v_attn_sinks, v_conv_w, v_w_out, v_ple_gate_norm_gain, v_w_ple_gate, v_b_ple_gate, v_w_ple_proj, v_ple_norm_gain):
    x2, p2, tgt = x[0], p[0, 0], loss_target[0]
    s, d = x2.shape
    aw = d // 2
    cw = d - aw
    nq = aw // HEAD_DIM
    sh = w_in.shape[2]
    in_w = N_CHIPS * sh
    dq = d // N_CHIPS
    cq = cw // N_CHIPS
    ga_off = (aw + 2 * KV_WIDTH) // COLT
    b_off = (2 * aw + 2 * KV_WIDTH) // COLT
    assert in_w == 2 * aw + 2 * KV_WIDTH + 4 * cw and aw % COLT == 0 and cw % COLT == 0
    assert s % BLOCK == 0 and sh % LANES == 0 and nq % (2 * N_KV_HEADS) == 0

    core = lax.axis_index("c").astype(jnp.int32).reshape(1)
    chip = 2 * lax.axis_index("x") + lax.axis_index("y")

    chip1 = chip.astype(jnp.int32).reshape(1)
    place = jnp.concatenate([chip1, core])
    w_in_own = cast_bf16(w_in[0], "cast_w_in")
    rest = [cast_into_slot(w_out[0], chip1, "cast_w_out"), cast_into_slot(w_ple_gate[0], chip1, "cast_w_pg"),
            cast_into_slot(w_ple_proj[0], chip1, "cast_w_pp"),
            lax.dynamic_update_slice(jnp.zeros((N_CHIPS, SUBLANES, cq), F32),
                                     _pad_rows(conv_w[0], SUBLANES)[None], (chip, 0, 0))]
    order = jnp.stack([chip, chip ^ 2, chip ^ 1, chip ^ 3]).astype(jnp.int32)
    wi_sems, wi_arrs, wi_token = exchange_start(
        "gather_wi_start", [(plan_shard_ici(j), [0, 1 + j], 1) for j in range(2)],
        [w_in_own] + [lax.empty((d, sh), BF16) for _ in range(3)])

    tn = _tile(d, 1024)
    ts = _tile(s, 2048)
    tabs = _rope_tables(s)
    qg = jnp.tile(q_norm_gain, (1, LANES // HEAD_DIM))
    kg = jnp.tile(k_norm_gain, (1, LANES // HEAD_DIM))
    seg_i = jnp.arange(SEG) // HEAD_DIM
    segb = (seg_i[:, None] == seg_i[None, :]).astype(BF16)
    h, z = norm_in_proj(x2, norm_gain, wi_arrs[0], order, in_w)
    own, land_x = exchange_wait("gather_wi_wait0", plan_shard_ici(0), wi_sems[0], [wi_arrs[0], wi_arrs[1]],
                                (z,) + tuple(tabs) + tuple(rest[:2]))
    own, land_y = exchange_wait("gather_wi_wait1", plan_shard_ici(1), wi_sems[1], [own, wi_arrs[2]], land_x)
    (relay_sems, rest_sems), started, rest_token = exchange_start(
        "gather_relay_rest_start", [(plan_shard_relay, [0, 1, 2], 2), (plan_gather_ici(3), [3, 4, 5, 6], 12)],
        [land_x, land_y, wi_arrs[3]] + rest)
    relayed, rest = started[:3], started[3:]
    land_x, land_y = exchange("gather_wi_pass01", plan_shard_pass, relayed[:2], 2, after=(rest_token,))
    z = in_proj_part(h, land_x, z, order, 1, in_w)
    z = in_proj_part(h, land_y, z, order, 2, in_w)
    land_x, land_y, land_far = exchange_wait("gather_wi_relay_wait", plan_shard_relay, relay_sems,
                                             [land_x, land_y, relayed[2]], z)
    land_far, = exchange("gather_wi_pass2", plan_shard_pass, [land_far], 1)
    z = in_proj_part(h, land_far, z, order, 3, in_w)
    w_shards = [own, land_x, land_y, land_far]
    wo_all, wg_all, wp_all, conv_all = exchange_wait("gather_rest_wait", plan_gather_ici(3), rest_sems, rest, z)
    pass_sems, passed, pass_token = exchange_start(
        "gather_rest_pass_start", [(plan_gather_pass, [0, 1, 2], 9)], [wo_all, wg_all, wp_all])
    conv_full = conv_all.transpose(1, 0, 2).reshape(SUBLANES, cw)
    qs, ks, vb = qk_prep_fwd(z, tabs, qg, kg, segb, aw, after=(pass_token,))
    attn, mix = attn_fwd(qs, ks, vb, z, attn_sinks, aw, d, ga_off)
    mix = conv_fwd(z, conv_full, mix, aw, cw, b_off)
    wo_all, wg_all, wp_all = exchange_wait("gather_rest_pass_wait", plan_gather_pass, pass_sems[0], passed, mix)
    wo_full = wo_all.reshape(d, d)
    wg_full = wg_all.reshape(d, d)
    x1, hg = out_proj_norm(mix, wo_full, x2, ple_gate_norm_gain)

    d_gl, dy, g_wp, acc_head = head_fwd_bwd(x1, hg, wg_full, p2, wp_all, tgt, b_ple_gate, ple_norm_gain)
    wgrad = lambda name, a, b: matmul(
        a, b, grid=(d // tn, d // tn, s // ts),
        a_spec=pl.BlockSpec((ts, tn), lambda i, j, k: (k, i)),
        b_spec=pl.BlockSpec((ts, tn), lambda i, j, k: (k, j)),
        o_spec=pl.BlockSpec((tn, tn), lambda i, j, k: (i, j)),
        out_shape=jax.ShapeDtypeStruct((d, d), BF16), dims=TN, name=name)
    g_wg = wgrad("mm_g_wg", hg, d_gl)
    d_x1, d_x1b, acc_g = gate_proj_bwd_norm(d_gl, wg_full, x1, dy, ple_gate_norm_gain)
    g_wo = wgrad("mm_g_wo", mix, d_x1b)
    def reduce_sum(tag, handle, after):
        sems, arrays, _ = handle
        n = len(arrays) // 2
        got = exchange_wait("scatter_%s_wait" % tag, plan_scatter, sems[0], arrays, after)
        return [sum_chips(pt, o, place, "sum_chips_%s%d" % (tag, i))
                for i, (pt, o) in enumerate(zip(got[:n], got[n:]))]

    early_grads = [g_wo.reshape(N_CHIPS, dq, d), g_wg.reshape(N_CHIPS, dq, d), g_wp]
    eswap_sems, eswapping, eswap_token = exchange_start(
        "swap_early_start", [(plan_swap, list(range(6)), 3)],
        early_grads + [lax.empty((N_CHIPS, a.shape[1] // 2, a.shape[2]), BF16) for a in early_grads])
    d_o, d_gate, d_mix_conv = out_proj_bwd_gate(d_x1b, wo_full, attn, z, aw, ga_off, after=(eswap_token,))
    eswapped = exchange_wait("swap_early_wait", plan_swap, eswap_sems[0], eswapping, d_o)
    early_parts = [add_sibling(g, o, core, "add_sibling_early%d" % i)
                   for i, (g, o) in enumerate(zip(eswapped[:3], eswapped[3:]))]
    early = exchange_start("scatter_early_start", [(plan_scatter, list(range(6)), 9)],
                           early_parts + [lax.empty(a.shape, BF16) for a in early_parts])
    dqs, dks, dvs, acc_sink = attn_bwd(qs, ks, vb, d_o, attn_sinks, aw, after=(early[-1],))
    dz, acc_qk = qk_prep_bwd(z, dqs, dks, dvs, d_gate, tabs, qg, kg, segb, aw)
    dz, acc_conv = conv_bwd(z, d_mix_conv, conv_full, dz, aw, cw, b_off)
    join_sems, joining, join_token = exchange_start(
        "join_early_start", [(plan_join, [0, 1, 2], 3)], reduce_sum("early", early, dz))
    g_send = in_proj_wgrad_half(h, dz, None, 1 - core, after=(join_token,))
    swap_sems, swapping, swap_token = exchange_start(
        "swap_late_start", [(plan_swap, [0, 1], 1)], [g_send, lax.empty((N_CHIPS, d // 2, sh), BF16)])
    g_wi = in_proj_wgrad_half(h, dz, swapping[0], core, after=(swap_token,))
    full_wo, full_wg, full_wp = exchange_wait("join_early_wait", plan_join, join_sems[0], joining, g_wi)
    g_wi, got_wi = exchange_wait("swap_late_wait", plan_swap, swap_sems[0], [g_wi, swapping[1]], full_wo)
    part_wi = add_sibling(g_wi, got_wi, core, "add_sibling_late0")
    late = exchange_start("scatter_late_start", [(plan_scatter, [0, 1], 3)],
                          [part_wi, lax.empty(part_wi.shape, BF16)])
    grad_x, acc_x = in_proj_bwd_norm(dz, w_shards, order, x2, d_x1, norm_gain, after=(late[-1],))

    wsm = max(d, cw)
    small = _pad_rows(jnp.concatenate([
        _pad_cols(acc_x[0:1], wsm), _pad_cols(acc_g[0:1], wsm), _pad_cols(acc_head[0:1], wsm),
        _pad_cols(acc_head[1:2], wsm), _pad_cols(acc_qk[0:1, :HEAD_DIM], wsm), _pad_cols(acc_conv[0:3], wsm),
        _pad_cols(acc_head[2:3], wsm),
        _pad_cols(acc_qk[1:2, :HEAD_DIM], wsm), _pad_cols(acc_sink[0:1, :nq], wsm)], axis=0), 2 * SUBLANES)
    device = 2 * chip + lax.axis_index("c")
    (small_sems, last_sems), started, last_token = exchange_start(
        "small_join_late_start", [(plan_allgather_small, [0], 7), (plan_join, [1], 1)],
        [lax.dynamic_update_slice(jnp.zeros((8,) + small.shape, F32), small[None], (device, 0, 0))]
        + reduce_sum("late", late, grad_x))
    small_bufs, last_halves = started[:1], started[1:]
    big, after = {}, (last_token,)
    for nm, g, w, m, v in (("w_out", full_wo, w_out, m_w_out, v_w_out),
                           ("w_ple_gate", full_wg, w_ple_gate, m_w_ple_gate, v_w_ple_gate),
                           ("w_ple_proj", full_wp, w_ple_proj, m_w_ple_proj, v_w_ple_proj)):
        stepped = adamw(g, w[0], m[0], v[0], "adamw_" + nm, after=after)
        big[nm], after = [o[None] for o in stepped], (stepped[1],)
    full_wi, = exchange_wait("join_late_wait", plan_join, last_sems, last_halves, after[0])
    big["w_in"] = [o[None] for o in adamw(full_wi, w_in[0], m_w_in[0], v_w_in[0], "adamw_w_in")]

    gathered, = exchange_wait("small_wait", plan_allgather_small, small_sems, small_bufs, big["w_in"][1])
    rowwise = (("norm_gain", (0, d), (norm_gain, m_norm_gain, v_norm_gain)),
               ("ple_gate_norm_gain", (1, d), (ple_gate_norm_gain, m_ple_gate_norm_gain, v_ple_gate_norm_gain)),
               ("b_ple_gate", (2, d), (b_ple_gate, m_b_ple_gate, v_b_ple_gate)),
               ("ple_norm_gain", (3, d), (ple_norm_gain, m_ple_norm_gain, v_ple_norm_gain)),
               ("q_norm_gain", (4, HEAD_DIM), (q_norm_gain, m_q_norm_gain, v_q_norm_gain)),
               ("k_norm_gain", (SUBLANES + 1, HEAD_DIM), (k_norm_gain, m_k_norm_gain, v_k_norm_gain)),
               ("attn_sinks", (SUBLANES + 2, nq), (attn_sinks, m_attn_sinks, v_attn_sinks)))
    loss, stepped = small_step(gathered, chip1, [r for _, r, _ in rowwise], [t for _, _, t in rowwise],
                               5, (conv_w[0], m_conv_w[0], v_conv_w[0]))
    table = {nm: stepped[i] for i, (nm, _, _) in enumerate(rowwise)}
    table["conv_w"] = [o[None] for o in stepped[-1]]

    order = ("norm_gain", "w_in", "q_norm_gain", "k_norm_gain", "attn_sinks", "conv_w", "w_out",
             "ple_gate_norm_gain", "w_ple_gate", "b_ple_gate", "w_ple_proj", "ple_norm_gain")
    outs = [loss.reshape(()), grad_x[None]]
    for kind in range(4):
        for nm in order:
            outs.append(big[nm][kind] if nm in big else table[nm][kind])
    return tuple(outs)
```

```python
import functools

import jax
import jax.numpy as jnp
from jax import lax
from jax.experimental import pallas as pl
from jax.experimental.pallas import tpu as pltpu

F32 = jnp.float32
BF16 = jnp.bfloat16
MESH = pl.DeviceIdType.MESH

HEAD_DIM = 64
N_KV_HEADS = 4
KV_WIDTH = N_KV_HEADS * HEAD_DIM
BLOCK = 128
ROT_DIM = 16
ROPE_THETA = 500000.0
EPS = 1e-6
NEG_INF = -1e30
N_CHIPS = 4
LANES = 128
SUBLANES = 8
COLT = 512
SEG = 256
VMEM_LIMIT = 48 * 1024 * 1024
VMEM_LIMIT_BIG = 60 * 1024 * 1024

ADAM_LR = 0.001
ADAM_B1 = 0.9
ADAM_B2 = 0.999
ADAM_EPS = 1e-08
ADAM_WD = 0.01
ADAM_STEP = 10

NN = (((1,), (0,)), ((), ()))
NT = (((1,), (1,)), ((), ()))
TN = (((0,), (0,)), ((), ()))


def _call(body, **kw):
    return pl.pallas_call(body, **kw)


def _call_after(body, after, **kw):
    n_in, n_after = len(kw["in_specs"]), len(after)
    kw["in_specs"] = list(kw["in_specs"]) + [pl.BlockSpec(memory_space=pl.ANY)] * n_after

    def body_after(*refs):
        body(*refs[:n_in], *refs[n_in + n_after:])

    call = _call(body_after, **kw)
    return lambda *args: call(*args, *after)


def _params(sem, vmem=VMEM_LIMIT):
    return pltpu.CompilerParams(dimension_semantics=sem, vmem_limit_bytes=vmem)


def _tile(n, pref):
    return pref if n % pref == 0 else n


def _sigmoid(v):
    return 1.0 / (1.0 + jnp.exp(-v))


def _hbm():
    return pl.BlockSpec(memory_space=pl.ANY)


def cast_bf16(a, name):
    r, c = a.shape
    tr = _tile(r, 512)

    def body(a_ref, o_ref):
        o_ref[...] = a_ref[...].astype(BF16)

    return _call(body, name=name, grid=(r // tr,),
                 in_specs=[pl.BlockSpec((tr, c), lambda i: (i, 0))],
                 out_specs=pl.BlockSpec((tr, c), lambda i: (i, 0)),
                 out_shape=jax.ShapeDtypeStruct((r, c), BF16),
                 compiler_params=_params(("parallel",)))(a)


def cast_into_slot(a, chip, name):
    r, c = a.shape
    tr = _tile(r, 512)

    def body(chip_ref, a_ref, o_ref):
        o_ref[...] = a_ref[...].astype(BF16)

    grid_spec = pltpu.PrefetchScalarGridSpec(
        num_scalar_prefetch=1, grid=(r // tr,),
        in_specs=[pl.BlockSpec((tr, c), lambda i, chip_ref: (i, 0))],
        out_specs=pl.BlockSpec((None, tr, c), lambda i, chip_ref: (chip_ref[0], i, 0)))
    return _call(body, name=name, grid_spec=grid_spec,
                 out_shape=jax.ShapeDtypeStruct((N_CHIPS, r, c), BF16),
                 compiler_params=_params(("parallel",)))(chip, a)


def out_proj_norm(mix, wo, x, gain):
    s, d = x.shape
    tm = _tile(s, 512)

    def body(m_ref, w_ref, x_ref, g_ref, x1_ref, hg_ref):
        x1 = x_ref[...] + jnp.dot(m_ref[...], w_ref[...], preferred_element_type=F32)
        x1_ref[...] = x1
        r = lax.rsqrt(jnp.mean(x1 * x1, axis=-1, keepdims=True) + EPS)
        hg_ref[...] = ((x1 * r) * g_ref[...]).astype(BF16)

    row = pl.BlockSpec((tm, d), lambda i: (i, 0))
    return _call(body, name="out_proj_norm", grid=(s // tm,),
                 in_specs=[row, pl.BlockSpec((d, d), lambda i: (0, 0), pipeline_mode=pl.Buffered(1)), row,
                           pl.BlockSpec((1, d), lambda i: (0, 0))],
                 out_specs=[row, row],
                 out_shape=[jax.ShapeDtypeStruct((s, d), F32), jax.ShapeDtypeStruct((s, d), BF16)],
                 compiler_params=_params(("parallel",), VMEM_LIMIT_BIG))(mix, wo, x, gain)


def gate_proj_bwd_norm(d_gl, wg, xin, add, gain):
    s, d = xin.shape
    tm = _tile(s, 256)

    def body(dg_ref, w_ref, x_ref, a_ref, g_ref, dx_ref, dxb_ref, acc_ref):
        i = pl.program_id(0)
        dyv = lax.dot_general(dg_ref[...], w_ref[...], NT, preferred_element_type=F32)
        xf = x_ref[...]
        r = lax.rsqrt(jnp.mean(xf * xf, axis=-1, keepdims=True) + EPS)
        xhat = xf * r
        gd = dyv * g_ref[...]
        dx = a_ref[...] + r * (gd - xhat * jnp.mean(xhat * gd, axis=-1, keepdims=True))
        dx_ref[...] = dx
        dxb_ref[...] = dx.astype(BF16)

        @pl.when(i == 0)
        def _():
            acc_ref[...] = jnp.zeros_like(acc_ref)

        acc_ref[0:1, :] += jnp.sum(dyv * xhat, axis=0, keepdims=True)

    row = pl.BlockSpec((tm, d), lambda i: (i, 0))
    return _call(body, name="gate_proj_bwd_norm", grid=(s // tm,),
                 in_specs=[row, pl.BlockSpec((d, d), lambda i: (0, 0), pipeline_mode=pl.Buffered(1)), row, row,
                           pl.BlockSpec((1, d), lambda i: (0, 0))],
                 out_specs=[row, row, pl.BlockSpec((SUBLANES, d), lambda i: (0, 0))],
                 out_shape=[jax.ShapeDtypeStruct((s, d), F32), jax.ShapeDtypeStruct((s, d), BF16),
                            jax.ShapeDtypeStruct((SUBLANES, d), F32)],
                 compiler_params=_params(("arbitrary",), VMEM_LIMIT_BIG))(d_gl, wg, xin, add, gain)


def head_fwd_bwd(x1, hg, wg, p, wp, tgt, bias, ple_gain):
    s, d = x1.shape
    tm = _tile(s, 256)
    n_row = s // tm

    def body(x1_ref, hg_ref, wg_ref, p_ref, wp_ref, t_ref, b_ref, g_ref, dgl_ref, dy_ref, gwp_ref, acc_ref,
             gacc_ref):
        i = pl.program_id(0)
        gl = jnp.dot(hg_ref[...], wg_ref[...], preferred_element_type=F32)
        pb = p_ref[...].astype(BF16)
        pev = jnp.concatenate([jnp.dot(pb, wp_ref[q], preferred_element_type=F32) for q in range(N_CHIPS)],
                              axis=1)
        r = lax.rsqrt(jnp.mean(pev * pev, axis=-1, keepdims=True) + EPS)
        pehat = pev * r
        gain = g_ref[...]
        e = pehat * gain
        gate = _sigmoid(gl + b_ref[...])
        diff = (x1_ref[...] + gate * e) - t_ref[...]
        dy = diff * (1.0 / d)
        dy_ref[...] = dy
        d_gl = (dy * e) * (gate * (1.0 - gate))
        dgl_ref[...] = d_gl.astype(BF16)
        d_e = dy * gate
        gd = d_e * gain
        d_pe = r * (gd - pehat * jnp.mean(pehat * gd, axis=-1, keepdims=True))
        g_part = lax.dot_general(pb, d_pe.astype(BF16), TN, preferred_element_type=F32)

        @pl.when(i == 0)
        def _():
            acc_ref[...] = jnp.zeros_like(acc_ref)
            gacc_ref[...] = g_part

        @pl.when(i > 0)
        def _():
            gacc_ref[...] += g_part

        @pl.when(i == n_row - 1)
        def _():
            pq = d // N_CHIPS
            for q in range(N_CHIPS):
                gwp_ref[q] = gacc_ref[:, q * pq:(q + 1) * pq].astype(BF16)

        acc_ref[0:1, :] += jnp.sum(d_gl, axis=0, keepdims=True)
        acc_ref[1:2, :] += jnp.sum(d_e * pehat, axis=0, keepdims=True)
        acc_ref[2:3, :] += jnp.sum(diff * diff, axis=0, keepdims=True) * (0.5 / d)

    row = pl.BlockSpec((tm, d), lambda i: (i, 0))
    vec = pl.BlockSpec((1, d), lambda i: (0, 0))
    ple = p.shape[1]
    return _call(body, name="head_fwd_bwd", grid=(n_row,),
                 in_specs=[row, row, pl.BlockSpec((d, d), lambda i: (0, 0), pipeline_mode=pl.Buffered(1)),
                           pl.BlockSpec((tm, ple), lambda i: (i, 0)),
                           pl.BlockSpec(wp.shape, lambda i: (0, 0, 0)), row, vec, vec],
                 out_specs=[row, row, pl.BlockSpec(wp.shape, lambda i: (0, 0, 0)),
                            pl.BlockSpec((SUBLANES, d), lambda i: (0, 0))],
                 out_shape=[jax.ShapeDtypeStruct((s, d), BF16), jax.ShapeDtypeStruct((s, d), F32),
                            jax.ShapeDtypeStruct(wp.shape, BF16), jax.ShapeDtypeStruct((SUBLANES, d), F32)],
                 scratch_shapes=[pltpu.VMEM((ple, d), F32)],
                 compiler_params=_params(("arbitrary",), VMEM_LIMIT_BIG))(
                     x1, hg, wg, p, wp, tgt, bias, ple_gain)


def _adamw_step(gv, w_ref, m_ref, v_ref, go_ref, d_ref, mo_ref, vo_ref):
    mn = ADAM_B1 * m_ref[...] + (1.0 - ADAM_B1) * gv
    vn = ADAM_B2 * v_ref[...] + (1.0 - ADAM_B2) * (gv * gv)
    m_hat = mn / (1.0 - ADAM_B1 ** ADAM_STEP)
    v_hat = vn / (1.0 - ADAM_B2 ** ADAM_STEP)
    go_ref[...] = gv
    d_ref[...] = -ADAM_LR * (m_hat / (jnp.sqrt(v_hat) + ADAM_EPS) + ADAM_WD * w_ref[...])
    mo_ref[...] = mn
    vo_ref[...] = vn


def adamw(g, w, m, v, name, after=()):
    r, c = g.shape
    tr = _tile(r, 256)

    def body(g_ref, w_ref, m_ref, v_ref, go_ref, d_ref, mo_ref, vo_ref):
        _adamw_step(g_ref[...], w_ref, m_ref, v_ref, go_ref, d_ref, mo_ref, vo_ref)

    blk = pl.BlockSpec((tr, c), lambda i: (i, 0))
    shp = jax.ShapeDtypeStruct((r, c), F32)
    return _call_after(body, after, name=name, grid=(r // tr,), in_specs=[blk] * 4, out_specs=[blk] * 4,
                       out_shape=[shp] * 4, compiler_params=_params(("parallel",)))(g, w, m, v)


def matmul(a, b, *, grid, a_spec, b_spec, o_spec, out_shape, dims, name):
    nk = grid[2]
    acc_shape = tuple(d for d in o_spec.block_shape if d is not None)

    def body(a_ref, b_ref, o_ref, *scratch):
        part = lax.dot_general(a_ref[...].astype(BF16), b_ref[...].astype(BF16), dims, preferred_element_type=F32)
        if nk == 1:
            o_ref[...] = part.astype(o_ref.dtype)
            return
        acc_ref, = scratch
        k = pl.program_id(2)

        @pl.when(k == 0)
        def _():
            acc_ref[...] = part

        @pl.when((k > 0) & (k < nk - 1))
        def _():
            acc_ref[...] += part

        @pl.when(k == nk - 1)
        def _():
            o_ref[...] = (acc_ref[...] + part).astype(o_ref.dtype)

    return _call(body, name=name, grid=grid, in_specs=[a_spec, b_spec], out_specs=o_spec, out_shape=out_shape,
                 scratch_shapes=[pltpu.VMEM(acc_shape, F32)] if nk > 1 else [],
                 compiler_params=_params(("parallel", "parallel", "arbitrary")))(a, b)


def norm_in_proj(x, gain, w, order, in_w):
    s, d = x.shape
    sh = w.shape[1]
    tm = _tile(s, 512)

    def body(order_ref, x_ref, g_ref, w_ref, h_ref, z_ref):
        xf = x_ref[...]
        r = lax.rsqrt(jnp.mean(xf * xf, axis=-1, keepdims=True) + EPS)
        hb = ((xf * r) * g_ref[...]).astype(BF16)
        h_ref[...] = hb
        z_ref[...] = jnp.dot(hb, w_ref[...], preferred_element_type=F32)

    grid_spec = pltpu.PrefetchScalarGridSpec(
        num_scalar_prefetch=1, grid=(s // tm,),
        in_specs=[pl.BlockSpec((tm, d), lambda i, order_ref: (i, 0)),
                  pl.BlockSpec((1, d), lambda i, order_ref: (0, 0)),
                  pl.BlockSpec((d, sh), lambda i, order_ref: (0, 0), pipeline_mode=pl.Buffered(1))],
        out_specs=[pl.BlockSpec((tm, d), lambda i, order_ref: (i, 0)),
                   pl.BlockSpec((tm, sh), lambda i, order_ref: (i, order_ref[0]))])
    return _call(body, name="norm_in_proj", grid_spec=grid_spec,
                 out_shape=[jax.ShapeDtypeStruct((s, d), BF16), jax.ShapeDtypeStruct((s, in_w), F32)],
                 compiler_params=_params(("parallel",)))(order, x, gain, w)


def in_proj_part(h, w, z_prev, order, q, in_w):
    s, d = h.shape
    sh = w.shape[1]
    tm = _tile(s, 512)

    def body(order_ref, h_ref, w_ref, *rest):
        rest[-1][...] = jnp.dot(h_ref[...], w_ref[...], preferred_element_type=F32)

    in_specs = [pl.BlockSpec((tm, d), lambda i, order_ref: (i, 0)),
                pl.BlockSpec((d, sh), lambda i, order_ref: (0, 0))]
    args = [order, h, w]
    if z_prev is not None:
        in_specs.append(_hbm())
        args.append(z_prev)
    grid_spec = pltpu.PrefetchScalarGridSpec(
        num_scalar_prefetch=1, grid=(s // tm,), in_specs=in_specs,
        out_specs=pl.BlockSpec((tm, sh), lambda i, order_ref: (i, order_ref[q])))
    return _call(body, name="mm_z%d" % q, grid_spec=grid_spec,
                 out_shape=jax.ShapeDtypeStruct((s, in_w), F32),
                 input_output_aliases={3: 0} if z_prev is not None else {},
                 compiler_params=_params(("parallel",)))(*args)


def in_proj_wgrad_half(h, dz, g_prev, half, after):
    s, d = h.shape
    sh = dz.shape[1] // N_CHIPS
    hr = d // 2

    def body(half_ref, h_ref, dz_ref, *rest):
        rest[-1][...] = lax.dot_general(h_ref[...], dz_ref[...], TN, preferred_element_type=F32).astype(BF16)

    extra = ([g_prev] if g_prev is not None else []) + list(after)
    grid_spec = pltpu.PrefetchScalarGridSpec(
        num_scalar_prefetch=1, grid=(N_CHIPS,),
        in_specs=[pl.BlockSpec((s, hr), lambda j, half_ref: (0, half_ref[0]), pipeline_mode=pl.Buffered(1)),
                  pl.BlockSpec((s, sh), lambda j, half_ref: (0, j))] + [_hbm()] * len(extra),
        out_specs=pl.BlockSpec((None, hr, sh), lambda j, half_ref: (j, half_ref[0], 0)))
    return _call(body, name="mm_g_wi_%s" % ("keep" if g_prev is not None else "send"), grid_spec=grid_spec,
                 out_shape=jax.ShapeDtypeStruct((N_CHIPS, d, sh), BF16),
                 input_output_aliases={3: 0} if g_prev is not None else {},
                 compiler_params=_params(("parallel",), VMEM_LIMIT_BIG))(half, h, dz, *extra)


def in_proj_bwd_norm(dz, ws, order, xin, add, gain, after):
    s, d = xin.shape
    sh = ws[0].shape[1]
    tm = _tile(s, 256)
    nq, n_after = len(ws), len(after)

    def body(order_ref, *refs):
        a_refs, b_refs = refs[:nq], refs[nq:2 * nq]
        x_ref, add_ref, g_ref = refs[2 * nq:2 * nq + 3]
        dx_ref, acc_ref = refs[2 * nq + 3 + n_after:]
        i = pl.program_id(0)
        dyv = lax.dot_general(a_refs[0][...], b_refs[0][...], NT, preferred_element_type=F32)
        for q in range(1, nq):
            dyv += lax.dot_general(a_refs[q][...], b_refs[q][...], NT, preferred_element_type=F32)
        xf = x_ref[...]
        r = lax.rsqrt(jnp.mean(xf * xf, axis=-1, keepdims=True) + EPS)
        xhat = xf * r
        gd = dyv * g_ref[...]
        dx_ref[...] = add_ref[...] + r * (gd - xhat * jnp.mean(xhat * gd, axis=-1, keepdims=True))

        @pl.when(i == 0)
        def _():
            acc_ref[...] = jnp.zeros_like(acc_ref)

        acc_ref[0:1, :] += jnp.sum(dyv * xhat, axis=0, keepdims=True)

    a_spec = lambda q: pl.BlockSpec((tm, sh), functools.partial(lambda i, order_ref, q: (i, order_ref[q]), q=q))
    row = pl.BlockSpec((tm, d), lambda i, order_ref: (i, 0))
    grid_spec = pltpu.PrefetchScalarGridSpec(
        num_scalar_prefetch=1, grid=(s // tm,),
        in_specs=[a_spec(q) for q in range(nq)]
        + [pl.BlockSpec((d, sh), lambda i, order_ref: (0, 0), pipeline_mode=pl.Buffered(1))] * nq
        + [row, row, pl.BlockSpec((1, d), lambda i, order_ref: (0, 0))] + [_hbm()] * n_after,
        out_specs=[row, pl.BlockSpec((SUBLANES, d), lambda i, order_ref: (0, 0))])
    return _call(body, name="in_proj_bwd_norm", grid_spec=grid_spec,
                 out_shape=[jax.ShapeDtypeStruct((s, d), F32), jax.ShapeDtypeStruct((SUBLANES, d), F32)],
                 compiler_params=_params(("arbitrary",), VMEM_LIMIT_BIG))(
                     order, *([dz] * nq), *ws, xin, add, gain, *after)


def _seg_mean(sq, segb):
    parts = []
    for cgrp in range(sq.shape[1] // SEG):
        blk = sq[:, cgrp * SEG:(cgrp + 1) * SEG]
        hi = blk.astype(BF16)
        r1 = blk - hi.astype(F32)
        mid = r1.astype(BF16)
        lo = (r1 - mid.astype(F32)).astype(BF16)
        acc = jnp.dot(hi, segb, preferred_element_type=F32)
        acc += jnp.dot(mid, segb, preferred_element_type=F32)
        acc += jnp.dot(lo, segb, preferred_element_type=F32)
        parts.append(acc)
    out = parts[0] if len(parts) == 1 else jnp.concatenate(parts, axis=1)
    return out * (1.0 / HEAD_DIM)


def _rope(v, cos, sa, sb):
    parts = []
    for cgrp in range(v.shape[1] // LANES):
        blk = v[:, cgrp * LANES:(cgrp + 1) * LANES]
        parts.append(blk * cos + pltpu.roll(blk, LANES - 8, 1) * sa + pltpu.roll(blk, 8, 1) * sb)
    return parts[0] if len(parts) == 1 else jnp.concatenate(parts, axis=1)


def _rope_t(dv, cos, sa, sb):
    parts = []
    for cgrp in range(dv.shape[1] // LANES):
        blk = dv[:, cgrp * LANES:(cgrp + 1) * LANES]
        parts.append(blk * cos + pltpu.roll(blk * sa, 8, 1) + pltpu.roll(blk * sb, LANES - 8, 1))
    return parts[0] if len(parts) == 1 else jnp.concatenate(parts, axis=1)


def _tile_lanes(vec, width):
    reps = width // LANES
    return vec if reps == 1 else jnp.tile(vec, (1, reps))


def qk_prep_fwd(z, tabs, qg, kg, segb, aw, after=()):
    s = z.shape[0]
    tm = _tile(s, 512)
    wq = aw + 2 * KV_WIDTH

    def body(z_ref, cos_ref, sa_ref, sb_ref, qg_ref, kg_ref, seg_ref, qs_ref, ks_ref, vb_ref):
        zz = z_ref[...]
        q, k, v = zz[:, :aw], zz[:, aw:aw + KV_WIDTH], zz[:, aw + KV_WIDTH:]
        cos, sa, sb, segm = cos_ref[...], sa_ref[...], sb_ref[...], seg_ref[...]
        rq = lax.rsqrt(_seg_mean(q * q, segm) + EPS)
        qn = (q * rq) * _tile_lanes(qg_ref[...], aw)
        qs_ref[...] = (_rope(qn, cos, sa, sb) * (HEAD_DIM ** -0.5)).astype(BF16)
        rk = lax.rsqrt(_seg_mean(k * k, segm) + EPS)
        kn = (k * rk) * _tile_lanes(kg_ref[...], KV_WIDTH)
        ks_ref[...] = _rope(kn, cos, sa, sb).astype(BF16)
        vb_ref[...] = v.astype(BF16)

    tab = pl.BlockSpec((tm, LANES), lambda i: (i, 0))
    vec = pl.BlockSpec((1, LANES), lambda i: (0, 0))
    return _call_after(body, after, name="qk_prep_fwd", grid=(s // tm,),
                       in_specs=[pl.BlockSpec((tm, wq), lambda i: (i, 0)), tab, tab, tab, vec, vec,
                                 pl.BlockSpec((SEG, SEG), lambda i: (0, 0))],
                       out_specs=[pl.BlockSpec((tm, aw), lambda i: (i, 0)),
                                  pl.BlockSpec((tm, KV_WIDTH), lambda i: (i, 0)),
                                  pl.BlockSpec((tm, KV_WIDTH), lambda i: (i, 0))],
                       out_shape=[jax.ShapeDtypeStruct((s, aw), BF16), jax.ShapeDtypeStruct((s, KV_WIDTH), BF16),
                                  jax.ShapeDtypeStruct((s, KV_WIDTH), BF16)],
                       compiler_params=_params(("parallel",)))(z, *tabs, qg, kg, segb)


def qk_prep_bwd(z, dqs, dks, dvs, d_gate, tabs, qg, kg, segb, aw):
    s, in_w = z.shape
    tm = _tile(s, 512)
    wq = aw + 2 * KV_WIDTH

    def body(z_ref, dq_ref, dk_ref, dv_ref, dga_ref, cos_ref, sa_ref, sb_ref, qg_ref, kg_ref, seg_ref,
             dz_ref, acc_ref):
        i = pl.program_id(0)
        zz = z_ref[...]
        q, k = zz[:, :aw], zz[:, aw:aw + KV_WIDTH]
        cos, sa, sb, segm = cos_ref[...], sa_ref[...], sb_ref[...], seg_ref[...]

        def one(xv, dout, gvec, width):
            g = _tile_lanes(gvec, width)
            r = lax.rsqrt(_seg_mean(xv * xv, segm) + EPS)
            xhat = xv * r
            dn = _rope_t(dout, cos, sa, sb)
            gd = dn * g
            dx = r * (gd - xhat * _seg_mean(xhat * gd, segm))
            contrib = jnp.sum(dn * xhat, axis=0, keepdims=True)
            folded = contrib[:, :LANES]
            for cgrp in range(1, width // LANES):
                folded = folded + contrib[:, cgrp * LANES:(cgrp + 1) * LANES]
            return dx, folded + pltpu.roll(folded, HEAD_DIM, 1)

        dq, gq = one(q, dq_ref[...] * (HEAD_DIM ** -0.5), qg_ref[...], aw)
        dk, gk = one(k, dk_ref[...], kg_ref[...], KV_WIDTH)
        dz_ref[:, :aw] = dq.astype(BF16)
        dz_ref[:, aw:aw + KV_WIDTH] = dk.astype(BF16)
        dz_ref[:, aw + KV_WIDTH:wq] = dv_ref[...].astype(BF16)
        dz_ref[:, wq:] = dga_ref[...]

        @pl.when(i == 0)
        def _():
            acc_ref[...] = jnp.zeros_like(acc_ref)

        acc_ref[0:1, :] += gq
        acc_ref[1:2, :] += gk

    tab = pl.BlockSpec((tm, LANES), lambda i: (i, 0))
    vec = pl.BlockSpec((1, LANES), lambda i: (0, 0))
    kvb = pl.BlockSpec((tm, KV_WIDTH), lambda i: (i, 0))
    awb = pl.BlockSpec((tm, aw), lambda i: (i, 0))
    return _call(body, name="qk_prep_bwd", grid=(s // tm,),
                 in_specs=[pl.BlockSpec((tm, wq), lambda i: (i, 0)), awb, kvb, kvb, awb, tab, tab, tab, vec, vec,
                           pl.BlockSpec((SEG, SEG), lambda i: (0, 0))],
                 out_specs=[pl.BlockSpec((tm, wq + aw), lambda i: (i, 0)),
                            pl.BlockSpec((SUBLANES, LANES), lambda i: (0, 0))],
                 out_shape=[jax.ShapeDtypeStruct((s, in_w), BF16), jax.ShapeDtypeStruct((SUBLANES, LANES), F32)],
                 compiler_params=_params(("arbitrary",)))(z, dqs, dks, dvs, d_gate, *tabs, qg, kg, segb)


def _placed(band, lane_idx):
    out = {}
    for kh in range(N_KV_HEADS):
        grp = band[:, (kh // 2) * LANES:(kh // 2 + 1) * LANES]
        for half in (0, 1):
            t = grp if half == kh % 2 else pltpu.roll(grp, HEAD_DIM, 1)
            keep = (lane_idx >= half * HEAD_DIM) & (lane_idx < (half + 1) * HEAD_DIM)
            out[kh, half] = jnp.where(keep, t, jnp.zeros_like(t))
    return out


def _softmax_with_sink(sc, valid, sink):
    sc = jnp.where(valid, sc, NEG_INF)
    m = jnp.maximum(jnp.max(sc, axis=-1, keepdims=True), sink)
    e = jnp.exp(sc - m)
    es = jnp.exp(sink - m)
    den = jnp.sum(e, axis=-1, keepdims=True) + es
    return e / den, es / den


def _stack_halves(placed, kh):
    return jnp.concatenate([placed[kh, 0], placed[kh, 1]], axis=0)


def _valid_mask(n):
    r_i = lax.broadcasted_iota(jnp.int32, (BLOCK, 2 * BLOCK), 0)
    j_i = lax.broadcasted_iota(jnp.int32, (BLOCK, 2 * BLOCK), 1)
    return (j_i > r_i) & (j_i <= r_i + BLOCK) & ((n > 0) | (j_i >= BLOCK))


def attn_fwd(qs, ks, vb, z, sinks, aw, d, ga_off):
    s = qs.shape[0]
    nb = s // BLOCK
    nq = aw // HEAD_DIM
    grp_sz = nq // N_KV_HEADS
    n_ga = aw // COLT

    def body(q_ref, ko_ref, kp_ref, vo_ref, vp_ref, *rest):
        ga_refs = rest[:n_ga]
        sink_ref, attn_ref, mix_ref = rest[n_ga:]
        n = pl.program_id(0)
        lane_idx = lax.broadcasted_iota(jnp.int32, (2 * BLOCK, LANES), 1)
        kpl = _placed(jnp.concatenate([kp_ref[...], ko_ref[...]], axis=0), lane_idx)
        vpl = _placed(jnp.concatenate([vp_ref[...], vo_ref[...]], axis=0), lane_idx)
        valid = _valid_mask(n)
        groups = range(nq // 2)
        kcat = [_stack_halves(kpl, kh) for kh in range(N_KV_HEADS)]
        vcat = [_stack_halves(vpl, kh) for kh in range(N_KV_HEADS)]
        scs = [lax.dot_general(q_ref[:, c * LANES:(c + 1) * LANES], kcat[2 * c // grp_sz], NT,
                               preferred_element_type=F32) for c in groups]
        probs = [jnp.concatenate(
            [_softmax_with_sink(scs[c][:, half * 2 * BLOCK:(half + 1) * 2 * BLOCK], valid,
                                sink_ref[0, 2 * c + half])[0].astype(BF16) for half in (0, 1)], axis=1)
                 for c in groups]
        for c in groups:
            acc = jnp.dot(probs[c], vcat[2 * c // grp_sz], preferred_element_type=F32)
            attn_ref[:, c * LANES:(c + 1) * LANES] = acc
            col = c * LANES
            ga = ga_refs[col // COLT][:, col % COLT:col % COLT + LANES]
            mix_ref[:, c * LANES:(c + 1) * LANES] = (acc * (ga * _sigmoid(ga))).astype(BF16)

    own = lambda n: (n, 0)
    prev = lambda n: (jnp.maximum(n - 1, 0), 0)
    kvs = lambda imap: pl.BlockSpec((BLOCK, KV_WIDTH), imap)
    ga_specs = [pl.BlockSpec((BLOCK, COLT), functools.partial(lambda n, j: (n, ga_off + j), j=j))
                for j in range(n_ga)]
    return _call(body, name="attn_fwd", grid=(nb,),
                 in_specs=[pl.BlockSpec((BLOCK, aw), own), kvs(own), kvs(prev), kvs(own), kvs(prev)]
                 + ga_specs + [pl.BlockSpec(memory_space=pltpu.SMEM)],
                 out_specs=[pl.BlockSpec((BLOCK, aw), own), pl.BlockSpec((BLOCK, aw), own)],
                 out_shape=[jax.ShapeDtypeStruct((s, aw), F32), jax.ShapeDtypeStruct((s, d), BF16)],
                 compiler_params=_params(("parallel",)))(qs, ks, ks, vb, vb, *([z] * n_ga), sinks)


def out_proj_bwd_gate(d_x1b, wo, attn, z, aw, ga_off, after=()):
    s, d = d_x1b.shape
    tm = _tile(s, 512)
    n_ga = aw // COLT

    def body(dx_ref, w_ref, at_ref, *rest):
        ga_refs, (do_ref, dga_ref, dmc_ref) = rest[:n_ga], rest[n_ga:]
        dm = lax.dot_general(dx_ref[...], w_ref[...], NT, preferred_element_type=F32)
        dmc_ref[...] = dm[:, aw:]
        for j in range(n_ga):
            cols = slice(j * COLT, (j + 1) * COLT)
            ga = ga_refs[j][...]
            sig = _sigmoid(ga)
            dma = dm[:, cols]
            do_ref[:, cols] = (dma * (ga * sig)).astype(BF16)
            dga_ref[:, cols] = ((dma * at_ref[:, cols]) * (sig * (1.0 + ga * (1.0 - sig)))).astype(BF16)

    row = lambda width: pl.BlockSpec((tm, width), lambda i: (i, 0))
    ga_specs = [pl.BlockSpec((tm, COLT), functools.partial(lambda i, j: (i, ga_off + j), j=j)) for j in range(n_ga)]
    return _call_after(body, after, name="out_proj_bwd_gate", grid=(s // tm,),
                       in_specs=[row(d), pl.BlockSpec((d, d), lambda i: (0, 0), pipeline_mode=pl.Buffered(1)),
                                 row(aw)] + ga_specs,
                       out_specs=[row(aw), row(aw), row(d - aw)],
                       out_shape=[jax.ShapeDtypeStruct((s, aw), BF16), jax.ShapeDtypeStruct((s, aw), BF16),
                                  jax.ShapeDtypeStruct((s, d - aw), F32)],
                       compiler_params=_params(("parallel",)))(d_x1b, wo, attn, *([z] * n_ga))


def attn_bwd(qs, ks, vb, d_o, sinks, aw, after=()):
    s = qs.shape[0]
    nb = s // BLOCK
    nq = aw // HEAD_DIM
    grp_sz = nq // N_KV_HEADS

    def body(q_ref, ko_ref, kp_ref, vo_ref, vp_ref, do_ref, sink_ref, dq_ref, dk_ref, dv_ref, ds_ref,
             ck_ref, cv_ref):
        n = pl.program_id(0)

        @pl.when(n == 0)
        def _():
            ds_ref[...] = jnp.zeros_like(ds_ref)
            dk_ref[...] = jnp.zeros_like(dk_ref)
            dv_ref[...] = jnp.zeros_like(dv_ref)

        @pl.when(n < nb)
        def _():
            lane_idx = lax.broadcasted_iota(jnp.int32, (2 * BLOCK, LANES), 1)
            lane_row = lax.broadcasted_iota(jnp.int32, (1, LANES), 1)
            kpl = _placed(jnp.concatenate([kp_ref[...], ko_ref[...]], axis=0), lane_idx)
            vpl = _placed(jnp.concatenate([vp_ref[...], vo_ref[...]], axis=0), lane_idx)
            valid = _valid_mask(n)
            ds_row = jnp.zeros((1, LANES), F32)
            wide = 2 * BLOCK
            groups = range(nq // 2)
            per_kv = grp_sz // 2
            kcat = [_stack_halves(kpl, kh) for kh in range(N_KV_HEADS)]
            vcat = [_stack_halves(vpl, kh) for kh in range(N_KV_HEADS)]
            qg = [q_ref[:, c * LANES:(c + 1) * LANES] for c in groups]
            dog = [do_ref[:, c * LANES:(c + 1) * LANES] for c in groups]
            scs = [lax.dot_general(qg[c], kcat[c // per_kv], NT, preferred_element_type=F32) for c in groups]
            dps = [lax.dot_general(dog[c], vcat[c // per_kv], NT, preferred_element_type=F32) for c in groups]
            ds_pairs, p_pairs = [], []
            for c in groups:
                probs, dss = [], []
                for half in (0, 1):
                    h = 2 * c + half
                    cols = slice(half * wide, (half + 1) * wide)
                    p, p_sink = _softmax_with_sink(scs[c][:, cols], valid, sink_ref[0, h])
                    delta = jnp.sum(p * dps[c][:, cols], axis=-1, keepdims=True)
                    dss.append((p * (dps[c][:, cols] - delta)).astype(BF16))
                    probs.append(p.astype(BF16))
                    dsink = -jnp.sum(p_sink * delta, axis=0, keepdims=True)
                    ds_row += jnp.where(lane_row == h, dsink, 0.0)
                ds_pairs.append(jnp.concatenate(dss, axis=1))
                p_pairs.append(jnp.concatenate(probs, axis=1))
            for c in groups:
                dq_ref[:, c * LANES:(c + 1) * LANES] = jnp.dot(ds_pairs[c], kcat[c // per_kv],
                                                               preferred_element_type=F32)
            dkts = [lax.dot_general(qg[c], ds_pairs[c], TN, preferred_element_type=F32) for c in groups]
            dvts = [lax.dot_general(dog[c], p_pairs[c], TN, preferred_element_type=F32) for c in groups]
            dkt, dvt = [], []
            for kh in range(N_KV_HEADS):
                for parts, out in ((dkts, dkt), (dvts, dvt)):
                    tot = parts[kh * per_kv]
                    for c in range(kh * per_kv + 1, (kh + 1) * per_kv):
                        tot = tot + parts[c]
                    out.append(tot[:HEAD_DIM, :wide] + tot[HEAD_DIM:, wide:])
            ds_ref[0:1, :] += ds_row
            back = lambda t: jnp.concatenate(
                [jnp.concatenate(t[2 * g:2 * g + 2], axis=0).T for g in range(N_KV_HEADS // 2)], axis=1)
            dk_band, dv_band = back(dkt), back(dvt)

            @pl.when(n > 0)
            def _():
                dk_ref[...] = ck_ref[...] + dk_band[:BLOCK]
                dv_ref[...] = cv_ref[...] + dv_band[:BLOCK]

            ck_ref[...] = dk_band[BLOCK:]
            cv_ref[...] = dv_band[BLOCK:]

        @pl.when(n == nb)
        def _():
            dk_ref[...] = ck_ref[...]
            dv_ref[...] = cv_ref[...]

    own = lambda n: (jnp.minimum(n, nb - 1), 0)
    prev = lambda n: (jnp.clip(n - 1, 0, nb - 1), 0)
    done = lambda n: (jnp.maximum(n - 1, 0), 0)
    kvs = lambda imap: pl.BlockSpec((BLOCK, KV_WIDTH), imap)
    return _call_after(body, after, name="attn_bwd", grid=(nb + 1,),
                       in_specs=[pl.BlockSpec((BLOCK, aw), own), kvs(own), kvs(prev), kvs(own), kvs(prev),
                                 pl.BlockSpec((BLOCK, aw), own), pl.BlockSpec(memory_space=pltpu.SMEM)],
                       out_specs=[pl.BlockSpec((BLOCK, aw), own), kvs(done), kvs(done),
                                  pl.BlockSpec((SUBLANES, LANES), lambda n: (0, 0))],
                       out_shape=[jax.ShapeDtypeStruct((s, aw), F32), jax.ShapeDtypeStruct((s, KV_WIDTH), F32),
                                  jax.ShapeDtypeStruct((s, KV_WIDTH), F32),
                                  jax.ShapeDtypeStruct((SUBLANES, LANES), F32)],
                       scratch_shapes=[pltpu.VMEM((BLOCK, KV_WIDTH), F32), pltpu.VMEM((BLOCK, KV_WIDTH), F32)],
                       compiler_params=_params(("arbitrary",)))(qs, ks, ks, vb, vb, d_o, sinks)


def conv_fwd(z, conv_w, mix, aw, cw, b_off):
    s = z.shape[0]
    tm = _tile(s, 1024)
    nseg = cw // COLT
    hb = tm // SUBLANES

    def body(b_ref, c_ref, h_ref, g_ref, cp_ref, hp_ref, w_ref, mix_in, mix_ref):
        i = pl.program_id(0)
        u = c_ref[...] * h_ref[...]
        up = jnp.where(i > 0, cp_ref[...] * hp_ref[...], 0.0)
        ext = jnp.concatenate([up, u], axis=0)
        um1 = pltpu.roll(ext, 1, 0)[SUBLANES:]
        um2 = pltpu.roll(ext, 2, 0)[SUBLANES:]
        w = w_ref[...]
        cv = w[0:1] * um2 + w[1:2] * um1 + w[2:3] * u
        g = g_ref[...]
        mix_ref[...] = ((b_ref[...] * cv) * (g * _sigmoid(g))).astype(BF16)

    seg = lambda k: pl.BlockSpec((tm, COLT), functools.partial(lambda i, j, k: (i, b_off + k * nseg + j), k=k))
    halo = lambda k: pl.BlockSpec(
        (SUBLANES, COLT), functools.partial(lambda i, j, k: (jnp.maximum(i * hb - 1, 0), b_off + k * nseg + j), k=k))
    return _call(body, name="conv_fwd", grid=(s // tm, nseg),
                 in_specs=[seg(0), seg(1), seg(2), seg(3), halo(1), halo(2),
                           pl.BlockSpec((SUBLANES, COLT), lambda i, j: (0, j)), _hbm()],
                 out_specs=pl.BlockSpec((tm, COLT), lambda i, j: (i, aw // COLT + j)),
                 out_shape=jax.ShapeDtypeStruct(mix.shape, BF16),
                 input_output_aliases={7: 0},
                 compiler_params=_params(("parallel", "parallel")))(z, z, z, z, z, z, conv_w, mix)


def conv_bwd(z, d_mix, conv_w, dz, aw, cw, b_off):
    s = z.shape[0]
    tm = _tile(s, 512)
    nseg = cw // COLT
    hb = tm // SUBLANES
    n_row = s // tm
    last_h = s // SUBLANES - 1
    n_step = nseg * n_row

    def body(b_ref, c_ref, h_ref, g_ref, cp_ref, hp_ref, bn_ref, gn_ref, dm_ref, dmn_ref, w_ref, dz_in,
             dz_ref, acc_ref, stash_ref, sems):
        j = pl.program_id(0)
        i = pl.program_id(1)
        step = j * n_row + i
        slot = step % 2

        def copies(from_slot, at_step):
            jj, ii = at_step // n_row, at_step % n_row
            rows = pl.ds(pl.multiple_of(ii * tm, tm), tm)
            return [pltpu.make_async_copy(
                stash_ref.at[from_slot, q],
                dz_ref.at[rows, pl.ds(pl.multiple_of((b_off + q * nseg + jj) * COLT, COLT), COLT)],
                sems.at[from_slot, q]) for q in range(4)]

        @pl.when(step >= 2)
        def _():
            for cp in copies(slot, step - 2):
                cp.wait()

        cc, hh, bb, g = c_ref[...], h_ref[...], b_ref[...], g_ref[...]
        w = w_ref[...]
        u = cc * hh
        up = jnp.where(i > 0, cp_ref[...] * hp_ref[...], 0.0)
        ext = jnp.concatenate([up, u], axis=0)
        um1 = pltpu.roll(ext, 1, 0)[SUBLANES:]
        um2 = pltpu.roll(ext, 2, 0)[SUBLANES:]
        cv = w[0:1] * um2 + w[1:2] * um1 + w[2:3] * u
        sig = _sigmoid(g)
        sg = g * sig
        dm = dm_ref[...]
        d_cv = (dm * bb) * sg
        gn = gn_ref[...]
        d_cv_next = jnp.where(i < n_row - 1, (dmn_ref[...] * bn_ref[...]) * (gn * _sigmoid(gn)), 0.0)
        ext2 = jnp.concatenate([d_cv, d_cv_next], axis=0)
        dp1 = pltpu.roll(ext2, tm + SUBLANES - 1, 0)[:tm]
        dp2 = pltpu.roll(ext2, tm + SUBLANES - 2, 0)[:tm]
        d_u = w[2:3] * d_cv + w[1:2] * dp1 + w[0:1] * dp2
        stash_ref[slot, 0] = ((dm * cv) * sg).astype(BF16)
        stash_ref[slot, 1] = (d_u * hh).astype(BF16)
        stash_ref[slot, 2] = (d_u * cc).astype(BF16)
        stash_ref[slot, 3] = (((dm * bb) * cv) * (sig * (1.0 + g * (1.0 - sig)))).astype(BF16)
        for cp in copies(slot, step):
            cp.start()

        @pl.when(i == 0)
        def _():
            acc_ref[...] = jnp.zeros_like(acc_ref)

        acc_ref[0:1, :] += jnp.sum(d_cv * um2, axis=0, keepdims=True)
        acc_ref[1:2, :] += jnp.sum(d_cv * um1, axis=0, keepdims=True)
        acc_ref[2:3, :] += jnp.sum(d_cv * u, axis=0, keepdims=True)

        @pl.when(step == n_step - 1)
        def _():
            if n_step >= 2:
                for cp in copies(1 - slot, step - 1):
                    cp.wait()
            for cp in copies(slot, step):
                cp.wait()

    seg = lambda q: pl.BlockSpec((tm, COLT), functools.partial(lambda j, i, q: (i, b_off + q * nseg + j), q=q))
    halo_p = lambda q: pl.BlockSpec(
        (SUBLANES, COLT),
        functools.partial(lambda j, i, q: (jnp.maximum(i * hb - 1, 0), b_off + q * nseg + j), q=q))
    halo_n = lambda q: pl.BlockSpec(
        (SUBLANES, COLT),
        functools.partial(lambda j, i, q: (jnp.minimum((i + 1) * hb, last_h), b_off + q * nseg + j), q=q))
    return _call(body, name="conv_bwd", grid=(nseg, n_row),
                 in_specs=[seg(0), seg(1), seg(2), seg(3), halo_p(1), halo_p(2), halo_n(0), halo_n(3),
                           pl.BlockSpec((tm, COLT), lambda j, i: (i, j)),
                           pl.BlockSpec((SUBLANES, COLT), lambda j, i: (jnp.minimum((i + 1) * hb, last_h), j)),
                           pl.BlockSpec((SUBLANES, COLT), lambda j, i: (0, j)), _hbm()],
                 out_specs=[_hbm(), pl.BlockSpec((SUBLANES, COLT), lambda j, i: (0, j))],
                 out_shape=[jax.ShapeDtypeStruct(dz.shape, BF16), jax.ShapeDtypeStruct((SUBLANES, cw), F32)],
                 input_output_aliases={11: 0},
                 scratch_shapes=[pltpu.VMEM((2, 4, tm, COLT), BF16), pltpu.SemaphoreType.DMA((2, 4))],
                 compiler_params=_params(("arbitrary", "arbitrary")))(
                     z, z, z, z, z, z, z, z, d_mix, d_mix, conv_w, dz)


def _place():
    x, y, c = lax.axis_index("x"), lax.axis_index("y"), lax.axis_index("c")
    chips = [(1 - x, y), (x, 1 - y), (1 - x, 1 - y)]
    return x, y, c, chips


def _remote(src, dst, send_sem, recv_sem, device):
    return pltpu.make_async_remote_copy(src_ref=src, dst_ref=dst, send_sem=send_sem, recv_sem=recv_sem,
                                        device_id=device, device_id_type=MESH)


def plan_gather_ici(n_split):
    def plan(refs, send, recv):
        x, y, c, chips = _place()
        me = 2 * x + y
        mine, theirs = [], []
        for w, ref in enumerate(refs):
            for j, (cx, cy) in enumerate(chips):
                def part(slot):
                    if w >= n_split:
                        return ref.at[slot]
                    hr = ref.shape[1] // 2
                    return ref.at[slot, pl.ds(c * hr, hr)]
                k = 3 * w + j
                mine.append(_remote(part(me), part(me), send.at[k], recv.at[k], (cx, cy, c)))
                theirs.append(_remote(part(2 * cx + cy), part(2 * cx + cy), send.at[k], recv.at[k], (cx, cy, c)))
        return mine, theirs
    return plan


def plan_shard_ici(j):
    def plan(refs, send, recv):
        _, _, c, chips = _place()
        own, land = refs
        hr = own.shape[0] // 2
        rows = pl.ds(c * hr, hr)
        cp = _remote(own.at[rows], land.at[rows], send.at[0], recv.at[0], (*chips[j], c))
        return [cp], [cp]
    return plan


def plan_shard_relay(refs, send, recv):
    _, _, c, chips = _place()
    land_x, land_y, land_far = refs
    hr = land_far.shape[0] // 2
    quarter = lambda q: pl.ds(c * hr + q * (hr // 2), hr // 2)
    mine = [_remote(land_y.at[quarter(0)], land_far.at[quarter(0)], send.at[0], recv.at[0], (*chips[0], c)),
            _remote(land_x.at[quarter(1)], land_far.at[quarter(1)], send.at[1], recv.at[1], (*chips[1], c))]
    return mine, mine


def plan_shard_pass(refs, send, recv):
    x, y, c, _ = _place()
    mine, theirs = [], []
    for w, land in enumerate(refs):
        hr = land.shape[0] // 2
        half = lambda core: land.at[pl.ds(core * hr, hr)]
        mine.append(_remote(half(c), half(c), send.at[w], recv.at[w], (x, y, 1 - c)))
        theirs.append(_remote(half(1 - c), half(1 - c), send.at[w], recv.at[w], (x, y, 1 - c)))
    return mine, theirs


def plan_gather_pass(refs, send, recv):
    x, y, c, chips = _place()
    mine, theirs = [], []
    for w, ref in enumerate(refs):
        hr = ref.shape[1] // 2
        for j, (cx, cy) in enumerate(chips):
            half = lambda core: ref.at[2 * cx + cy, pl.ds(core * hr, hr)]
            k = 3 * w + j
            mine.append(_remote(half(c), half(c), send.at[k], recv.at[k], (x, y, 1 - c)))
            theirs.append(_remote(half(1 - c), half(1 - c), send.at[k], recv.at[k], (x, y, 1 - c)))
    return mine, theirs


def plan_swap(refs, send, recv):
    x, y, c, _ = _place()
    nw = len(refs) // 2
    mine = []
    for w in range(nw):
        hr = refs[w].shape[1] // 2
        mine.append(_remote(refs[w].at[:, pl.ds((1 - c) * hr, hr), :], refs[nw + w], send.at[w], recv.at[w],
                            (x, y, 1 - c)))
    return mine, mine


def plan_scatter(refs, send, recv):
    x, y, c, chips = _place()
    me = 2 * x + y
    nw = len(refs) // 2
    mine, theirs = [], []
    for w in range(nw):
        for j, (cx, cy) in enumerate(chips):
            k = 3 * w + j
            mine.append(_remote(refs[w].at[2 * cx + cy], refs[nw + w].at[me], send.at[k], recv.at[k], (cx, cy, c)))
            theirs.append(_remote(refs[w].at[me], refs[nw + w].at[2 * cx + cy], send.at[k], recv.at[k], (cx, cy, c)))
    return mine, theirs


def plan_join(refs, send, recv):
    x, y, c, _ = _place()
    mine, theirs = [], []
    for w, ref in enumerate(refs):
        hr = ref.shape[0] // 2
        half = lambda core: ref.at[pl.ds(core * hr, hr)]
        mine.append(_remote(half(c), half(c), send.at[w], recv.at[w], (x, y, 1 - c)))
        theirs.append(_remote(half(1 - c), half(1 - c), send.at[w], recv.at[w], (x, y, 1 - c)))
    return mine, theirs


def exchange(name, plan, arrays, n, after=()):
    na = len(arrays)

    def body(*refs):
        mine, theirs = plan(refs[:na], refs[2 * na], refs[2 * na + 1])
        for cp in mine:
            cp.start()
        for cp in theirs:
            cp.wait_recv()
        for cp in mine:
            cp.wait_send()

    return _call_after(body, after, name=name, in_specs=[_hbm()] * na, out_specs=[_hbm()] * na,
                       out_shape=[jax.ShapeDtypeStruct(a.shape, a.dtype) for a in arrays],
                       input_output_aliases={i: i for i in range(na)},
                       scratch_shapes=[pltpu.SemaphoreType.DMA((n,)), pltpu.SemaphoreType.DMA((n,))])(*arrays)


def exchange_start(name, groups, arrays, after=()):
    na, ng = len(arrays), len(groups)

    def body(*refs):
        for g, (plan, idx, _) in enumerate(groups):
            mine, _ = plan([refs[i] for i in idx], refs[na + 2 * g], refs[na + 2 * g + 1])
            for cp in mine:
                cp.start()
        refs[-1][...] = jnp.zeros_like(refs[-1])

    hbm = pl.BlockSpec(memory_space=pltpu.HBM)
    sem = pl.BlockSpec(memory_space=pltpu.SEMAPHORE)
    sem_types = [pltpu.SemaphoreType.DMA((n,)) for _, _, n in groups for _ in (0, 1)]
    outs = _call_after(body, after, name=name, in_specs=[hbm] * na,
                       out_specs=[sem] * (2 * ng) + [hbm] * na + [pl.BlockSpec(memory_space=pltpu.VMEM)],
                       out_shape=sem_types + [pltpu.HBM(a.shape, a.dtype) for a in arrays]
                       + [jax.ShapeDtypeStruct((SUBLANES, LANES), F32)],
                       input_output_aliases={i: i + 2 * ng for i in range(na)},
                       compiler_params=pltpu.CompilerParams(
                           has_side_effects=pltpu.SideEffectType.DATAFLOW_SIDE_EFFECTING))(
                               *[pltpu.with_memory_space_constraint(a, pltpu.HBM) for a in arrays])
    sems = [(outs[2 * g], outs[2 * g + 1]) for g in range(ng)]
    return sems, list(outs[2 * ng:-1]), outs[-1]


def exchange_wait(name, plan, sems, arrays, after):
    send_sems, recv_sems = sems
    na = len(arrays)
    after = tuple(after) if isinstance(after, (tuple, list)) else (after,)

    def body(*refs):
        mine, theirs = plan(refs[:na], refs[na], refs[na + 1])
        for cp in mine:
            cp.wait_send()
        for cp in theirs:
            cp.wait_recv()

    hbm = pl.BlockSpec(memory_space=pltpu.HBM)
    sem = pl.BlockSpec(memory_space=pltpu.SEMAPHORE)
    return _call(body, name=name, in_specs=[hbm] * na + [sem, sem] + [_hbm()] * len(after),
                 out_specs=[hbm] * na, out_shape=[pltpu.HBM(a.shape, a.dtype) for a in arrays],
                 input_output_aliases={i: i for i in range(na)},
                 compiler_params=pltpu.CompilerParams(
                     has_side_effects=pltpu.SideEffectType.DATAFLOW_SIDE_EFFECTING))(
                         *arrays, send_sems, recv_sems, *after)


def plan_allgather_small(refs, send, recv):
    x, y, c, _ = _place()
    buf, = refs
    mine, theirs = [], []
    flips = [(fx, fy, fc) for fx in (0, 1) for fy in (0, 1) for fc in (0, 1)][1:]
    for k, (fx, fy, fc) in enumerate(flips):
        peer = (x ^ fx, y ^ fy, c ^ fc)
        slot = lambda px, py, pc: buf.at[4 * px + 2 * py + pc]
        mine.append(_remote(slot(x, y, c), slot(x, y, c), send.at[k], recv.at[k], peer))
        theirs.append(_remote(slot(*peer), slot(*peer), send.at[k], recv.at[k], peer))
    return mine, theirs


def add_sibling(grad, got, core, name):
    _, r, c = grad.shape
    hr = r // 2
    tr = _tile(hr, 512)
    nblk = hr // tr

    def body(core_ref, g_ref, o_ref, out_ref):
        out_ref[...] = (g_ref[...].astype(F32) + o_ref[...].astype(F32)).astype(BF16)

    grid_spec = pltpu.PrefetchScalarGridSpec(
        num_scalar_prefetch=1, grid=(N_CHIPS, nblk),
        in_specs=[pl.BlockSpec((None, tr, c), lambda t, i, core_ref: (t, core_ref[0] * nblk + i, 0)),
                  pl.BlockSpec((None, tr, c), lambda t, i, core_ref: (t, i, 0))],
        out_specs=pl.BlockSpec((None, tr, c), lambda t, i, core_ref: (t, i, 0)))
    return _call(body, name=name, grid_spec=grid_spec,
                 out_shape=jax.ShapeDtypeStruct((N_CHIPS, hr, c), BF16),
                 compiler_params=_params(("parallel", "parallel")))(core, grad, got)


def sum_chips(mine, owned, place, name):
    _, hr, c = mine.shape
    tr = _tile(hr, 512)
    nblk = hr // tr

    def body(place_ref, m_ref, o1_ref, o2_ref, o3_ref, out_ref):
        acc = m_ref[...].astype(F32)
        for o_ref in (o1_ref, o2_ref, o3_ref):
            acc = acc + o_ref[...].astype(F32)
        out_ref[...] = acc

    other = lambda k: pl.BlockSpec((None, tr, c), lambda i, place_ref: ((place_ref[0] + k) % N_CHIPS, i, 0))
    grid_spec = pltpu.PrefetchScalarGridSpec(
        num_scalar_prefetch=1, grid=(nblk,),
        in_specs=[other(0), other(1), other(2), other(3)],
        out_specs=pl.BlockSpec((tr, c), lambda i, place_ref: (place_ref[1] * nblk + i, 0)))
    return _call(body, name=name, grid_spec=grid_spec,
                 out_shape=jax.ShapeDtypeStruct((2 * hr, c), F32),
                 compiler_params=_params(("parallel",)))(place, mine, owned, owned, owned)


def small_step(gathered, chip, rows, params, conv_row, conv):
    n_dev, n_rows, _ = gathered.shape
    _, cr, cq = conv[0].shape
    groups = list(params) + [conv]
    n_par = len(groups)

    def body(chip_ref, g_ref, gc_ref, *refs):
        ins, outs, loss_ref = refs[:3 * n_par], refs[3 * n_par:7 * n_par], refs[7 * n_par]
        tot = g_ref[0]
        conv_g = gc_ref[0, conv_row:conv_row + cr, :]
        for dev in range(1, n_dev):
            tot = tot + g_ref[dev]
            conv_g = conv_g + gc_ref[dev, conv_row:conv_row + cr, :]
        loss_ref[...] = jnp.sum(tot[SUBLANES:SUBLANES + 1], axis=1, keepdims=True)
        grads = [tot[r:r + 1, :w] for r, w in rows] + [conv_g[None]]
        for i, gv in enumerate(grads):
            _adamw_step(gv, *ins[3 * i:3 * i + 3], *outs[4 * i:4 * i + 4])

    def whole(a):
        return pl.BlockSpec(a.shape, lambda i, chip_ref, n=len(a.shape): (0,) * n)

    flat = [a for group in groups for a in group]
    out_shape = [jax.ShapeDtypeStruct(group[0].shape, F32) for group in groups for _ in range(4)]
    out_shape.append(jax.ShapeDtypeStruct((1, 1), F32))
    grid_spec = pltpu.PrefetchScalarGridSpec(
        num_scalar_prefetch=1, grid=(1,),
        in_specs=[whole(gathered), pl.BlockSpec((n_dev, n_rows, cq), lambda i, chip_ref: (0, 0, chip_ref[0]))]
        + [whole(a) for a in flat],
        out_specs=[whole(o) for o in out_shape])
    res = _call(body, name="small_step", grid_spec=grid_spec, out_shape=out_shape,
                compiler_params=_params(("arbitrary",)))(chip, gathered, gathered, *flat)
    return res[-1], [res[4 * i:4 * i + 4] for i in range(n_par)]


def _rope_tables(s):
    half = ROT_DIM // 2
    inv_freq = jnp.power(jnp.float32(ROPE_THETA), -jnp.arange(half, dtype=F32) * 2.0 / ROT_DIM)
    freq64 = jnp.concatenate([inv_freq, inv_freq, jnp.zeros((HEAD_DIM - ROT_DIM,), F32)])
    freq = jnp.concatenate([freq64, freq64])[None, :]
    dim = (jnp.arange(LANES) % HEAD_DIM)[None, :]
    ang = jnp.arange(s).astype(F32)[:, None] * freq
    cos, sin = jnp.cos(ang), jnp.sin(ang)
    return cos, jnp.where(dim < half, -sin, 0.0), jnp.where((dim >= half) & (dim < ROT_DIM), sin, 0.0)


def _pad_rows(a, rows):
    return jnp.pad(a, ((0, rows - a.shape[0]), (0, 0)))


def _pad_cols(a, cols):
    return jnp.pad(a, ((0, 0), (0, cols - a.shape[1])))


def kernel(x, p, norm_gain, w_in, q_norm_gain, k_norm_gain, attn_sinks, conv_w, w_out, ple_gate_norm_gain, w_ple_gate, b_ple_gate, w_ple_proj, ple_norm_gain, loss_target, m_norm_gain, m_w_in, m_q_norm_gain, m_k_norm_gain, m_attn_sinks, m_conv_w, m_w_out, m_ple_gate_norm_gain, m_w_ple_gate, m_b_ple_gate, m_w_ple_proj, m_ple_norm_gain, v_norm_gain, v_w_in, v_q_norm_gain, v_k_norm_gain, v_attn_sinks, v_conv_w, v_w_out, v_ple_gate_norm_gain, v_w_ple_gate, v_b_ple_gate, v_w_ple_proj, v_ple_norm_gain):
    x2, p2, tgt = x[0], p[0, 0], loss_target[0]
    s, d = x2.shape
    aw = d // 2
    cw = d - aw
    nq = aw // HEAD_DIM
    sh = w_in.shape[2]
    in_w = N_CHIPS * sh
    dq = d // N_CHIPS
    cq = cw // N_CHIPS
    ga_off = (aw + 2 * KV_WIDTH) // COLT
    b_off = (2 * aw + 2 * KV_WIDTH) // COLT
    assert in_w == 2 * aw + 2 * KV_WIDTH + 4 * cw and aw % COLT == 0 and cw % COLT == 0
    assert s % BLOCK == 0 and sh % LANES == 0 and nq % (2 * N_KV_HEADS) == 0

    core = lax.axis_index("c").astype(jnp.int32).reshape(1)
    chip = 2 * lax.axis_index("x") + lax.axis_index("y")

    chip1 = chip.astype(jnp.int32).reshape(1)
    place = jnp.concatenate([chip1, core])
    w_in_own = cast_bf16(w_in[0], "cast_w_in")
    rest = [cast_into_slot(w_out[0], chip1, "cast_w_out"), cast_into_slot(w_ple_gate[0], chip1, "cast_w_pg"),
            cast_into_slot(w_ple_proj[0], chip1, "cast_w_pp"),
            lax.dynamic_update_slice(jnp.zeros((N_CHIPS, SUBLANES, cq), F32),
                                     _pad_rows(conv_w[0], SUBLANES)[None], (chip, 0, 0))]
    order = jnp.stack([chip, chip ^ 2, chip ^ 1, chip ^ 3]).astype(jnp.int32)
    wi_sems, wi_arrs, wi_token = exchange_start(
        "gather_wi_start", [(plan_shard_ici(j), [0, 1 + j], 1) for j in range(2)],
        [w_in_own] + [lax.empty((d, sh), BF16) for _ in range(3)])

    tn = _tile(d, 1024)
    ts = _tile(s, 2048)
    tabs = _rope_tables(s)
    qg = jnp.tile(q_norm_gain, (1, LANES // HEAD_DIM))
    kg = jnp.tile(k_norm_gain, (1, LANES // HEAD_DIM))
    seg_i = jnp.arange(SEG) // HEAD_DIM
    segb = (seg_i[:, None] == seg_i[None, :]).astype(BF16)
    h, z = norm_in_proj(x2, norm_gain, wi_arrs[0], order, in_w)
    own, land_x = exchange_wait("gather_wi_wait0", plan_shard_ici(0), wi_sems[0], [wi_arrs[0], wi_arrs[1]],
                                (z,) + tuple(tabs) + tuple(rest[:2]))
    own, land_y = exchange_wait("gather_wi_wait1", plan_shard_ici(1), wi_sems[1], [own, wi_arrs[2]], land_x)
    (relay_sems, rest_sems), started, rest_token = exchange_start(
        "gather_relay_rest_start", [(plan_shard_relay, [0, 1, 2], 2), (plan_gather_ici(3), [3, 4, 5, 6], 12)],
        [land_x, land_y, wi_arrs[3]] + rest)
    relayed, rest = started[:3], started[3:]
    land_x, land_y = exchange("gather_wi_pass01", plan_shard_pass, relayed[:2], 2, after=(rest_token,))
    z = in_proj_part(h, land_x, z, order, 1, in_w)
    z = in_proj_part(h, land_y, z, order, 2, in_w)
    land_x, land_y, land_far = exchange_wait("gather_wi_relay_wait", plan_shard_relay, relay_sems,
                                             [land_x, land_y, relayed[2]], z)
    land_far, = exchange("gather_wi_pass2", plan_shard_pass, [land_far], 1)
    z = in_proj_part(h, land_far, z, order, 3, in_w)
    w_shards = [own, land_x, land_y, land_far]
    wo_all, wg_all, wp_all, conv_all = exchange_wait("gather_rest_wait", plan_gather_ici(3), rest_sems, rest, z)
    pass_sems, passed, pass_token = exchange_start(
        "gather_rest_pass_start", [(plan_gather_pass, [0, 1, 2], 9)], [wo_all, wg_all, wp_all])
    conv_full = conv_all.transpose(1, 0, 2).reshape(SUBLANES, cw)
    qs, ks, vb = qk_prep_fwd(z, tabs, qg, kg, segb, aw, after=(pass_token,))
    attn, mix = attn_fwd(qs, ks, vb, z, attn_sinks, aw, d, ga_off)
    mix = conv_fwd(z, conv_full, mix, aw, cw, b_off)
    wo_all, wg_all, wp_all = exchange_wait("gather_rest_pass_wait", plan_gather_pass, pass_sems[0], passed, mix)
    wo_full = wo_all.reshape(d, d)
    wg_full = wg_all.reshape(d, d)
    x1, hg = out_proj_norm(mix, wo_full, x2, ple_gate_norm_gain)

    d_gl, dy, g_wp, acc_head = head_fwd_bwd(x1, hg, wg_full, p2, wp_all, tgt, b_ple_gate, ple_norm_gain)
    wgrad = lambda name, a, b: matmul(
        a, b, grid=(d // tn, d // tn, s // ts),
        a_spec=pl.BlockSpec((ts, tn), lambda i, j, k: (k, i)),
        b_spec=pl.BlockSpec((ts, tn), lambda i, j, k: (k, j)),
        o_spec=pl.BlockSpec((tn, tn), lambda i, j, k: (i, j)),
        out_shape=jax.ShapeDtypeStruct((d, d), BF16), dims=TN, name=name)
    g_wg = wgrad("mm_g_wg", hg, d_gl)
    d_x1, d_x1b, acc_g = gate_proj_bwd_norm(d_gl, wg_full, x1, dy, ple_gate_norm_gain)
    g_wo = wgrad("mm_g_wo", mix, d_x1b)
    def reduce_sum(tag, handle, after):
        sems, arrays, _ = handle
        n = len(arrays) // 2
        got = exchange_wait("scatter_%s_wait" % tag, plan_scatter, sems[0], arrays, after)
        return [sum_chips(pt, o, place, "sum_chips_%s%d" % (tag, i))
                for i, (pt, o) in enumerate(zip(got[:n], got[n:]))]

    early_grads = [g_wo.reshape(N_CHIPS, dq, d), g_wg.reshape(N_CHIPS, dq, d), g_wp]
    eswap_sems, eswapping, eswap_token = exchange_start(
        "swap_early_start", [(plan_swap, list(range(6)), 3)],
        early_grads + [lax.empty((N_CHIPS, a.shape[1] // 2, a.shape[2]), BF16) for a in early_grads])
    d_o, d_gate, d_mix_conv = out_proj_bwd_gate(d_x1b, wo_full, attn, z, aw, ga_off, after=(eswap_token,))
    eswapped = exchange_wait("swap_early_wait", plan_swap, eswap_sems[0], eswapping, d_o)
    early_parts = [add_sibling(g, o, core, "add_sibling_early%d" % i)
                   for i, (g, o) in enumerate(zip(eswapped[:3], eswapped[3:]))]
    early = exchange_start("scatter_early_start", [(plan_scatter, list(range(6)), 9)],
                           early_parts + [lax.empty(a.shape, BF16) for a in early_parts])
    dqs, dks, dvs, acc_sink = attn_bwd(qs, ks, vb, d_o, attn_sinks, aw, after=(early[-1],))
    dz, acc_qk = qk_prep_bwd(z, dqs, dks, dvs, d_gate, tabs, qg, kg, segb, aw)
    dz, acc_conv = conv_bwd(z, d_mix_conv, conv_full, dz, aw, cw, b_off)
    join_sems, joining, join_token = exchange_start(
        "join_early_start", [(plan_join, [0, 1, 2], 3)], reduce_sum("early", early, dz))
    g_send = in_proj_wgrad_half(h, dz, None, 1 - core, after=(join_token,))
    swap_sems, swapping, swap_token = exchange_start(
        "swap_late_start", [(plan_swap, [0, 1], 1)], [g_send, lax.empty((N_CHIPS, d // 2, sh), BF16)])
    g_wi = in_proj_wgrad_half(h, dz, swapping[0], core, after=(swap_token,))
    full_wo, full_wg, full_wp = exchange_wait("join_early_wait", plan_join, join_sems[0], joining, g_wi)
    g_wi, got_wi = exchange_wait("swap_late_wait", plan_swap, swap_sems[0], [g_wi, swapping[1]], full_wo)
    part_wi = add_sibling(g_wi, got_wi, core, "add_sibling_late0")
    late = exchange_start("scatter_late_start", [(plan_scatter, [0, 1], 3)],
                          [part_wi, lax.empty(part_wi.shape, BF16)])
    grad_x, acc_x = in_proj_bwd_norm(dz, w_shards, order, x2, d_x1, norm_gain, after=(late[-1],))

    wsm = max(d, cw)
    small = _pad_rows(jnp.concatenate([
        _pad_cols(acc_x[0:1], wsm), _pad_cols(acc_g[0:1], wsm), _pad_cols(acc_head[0:1], wsm),
        _pad_cols(acc_head[1:2], wsm), _pad_cols(acc_qk[0:1, :HEAD_DIM], wsm), _pad_cols(acc_conv[0:3], wsm),
        _pad_cols(acc_head[2:3], wsm),
        _pad_cols(acc_qk[1:2, :HEAD_DIM], wsm), _pad_cols(acc_sink[0:1, :nq], wsm)], axis=0), 2 * SUBLANES)
    device = 2 * chip + lax.axis_index("c")
    (small_sems, last_sems), started, last_token = exchange_start(
        "small_join_late_start", [(plan_allgather_small, [0], 7), (plan_join, [1], 1)],
        [lax.dynamic_update_slice(jnp.zeros((8,) + small.shape, F32), small[None], (device, 0, 0))]
        + reduce_sum("late", late, grad_x))
    small_bufs, last_halves = started[:1], started[1:]
    big, after = {}, (last_token,)
    for nm, g, w, m, v in (("w_out", full_wo, w_out, m_w_out, v_w_out),
                           ("w_ple_gate", full_wg, w_ple_gate, m_w_ple_gate, v_w_ple_gate),
                           ("w_ple_proj", full_wp, w_ple_proj, m_w_ple_proj, v_w_ple_proj)):
        stepped = adamw(g, w[0], m[0], v[0], "adamw_" + nm, after=after)
        big[nm], after = [o[None] for o in stepped], (stepped[1],)
    full_wi, = exchange_wait("join_late_wait", plan_join, last_sems, last_halves, after[0])
    big["w_in"] = [o[None] for o in adamw(full_wi, w_in[0], m_w_in[0], v_w_in[0], "adamw_w_in")]

    gathered, = exchange_wait("small_wait", plan_allgather_small, small_sems, small_bufs, big["w_in"][1])
    rowwise = (("norm_gain", (0, d), (norm_gain, m_norm_gain, v_norm_gain)),
               ("ple_gate_norm_gain", (1, d), (ple_gate_norm_gain, m_ple_gate_norm_gain, v_ple_gate_norm_gain)),
               ("b_ple_gate", (2, d), (b_ple_gate, m_b_ple_gate, v_b_ple_gate)),
               ("ple_norm_gain", (3, d), (ple_norm_gain, m_ple_norm_gain, v_ple_norm_gain)),
               ("q_norm_gain", (4, HEAD_DIM), (q_norm_gain, m_q_norm_gain, v_q_norm_gain)),
               ("k_norm_gain", (SUBLANES + 1, HEAD_DIM), (k_norm_gain, m_k_norm_gain, v_k_norm_gain)),
               ("attn_sinks", (SUBLANES + 2, nq), (attn_sinks, m_attn_sinks, v_attn_sinks)))
    loss, stepped = small_step(gathered, chip1, [r for _, r, _ in rowwise], [t for _, _, t in rowwise],
                               5, (conv_w, m_conv_w, v_conv_w))
    table = {nm: stepped[i] for i, (nm, _, _) in enumerate(rowwise)}
    table["conv_w"] = stepped[-1]

    order = ("norm_gain", "w_in", "q_norm_gain", "k_norm_gain", "attn_sinks", "conv_w", "w_out",
             "ple_gate_norm_gain", "w_ple_gate", "b_ple_gate", "w_ple_proj", "ple_norm_gain")
    outs = [loss.reshape(()), grad_x[None]]
    for kind in range(4):
        for nm in order:
            outs.append(big[nm][kind] if nm in big else table[nm][kind])
    return tuple(outs)
```

```python
import functools

import jax
import jax.numpy as jnp
from jax import lax
from jax.experimental import pallas as pl
from jax.experimental.pallas import tpu as pltpu

F32 = jnp.float32
BF16 = jnp.bfloat16
MESH = pl.DeviceIdType.MESH

HEAD_DIM = 64
N_KV_HEADS = 4
KV_WIDTH = N_KV_HEADS * HEAD_DIM
BLOCK = 128
ROT_DIM = 16
ROPE_THETA = 500000.0
EPS = 1e-6
NEG_INF = -1e30
N_CHIPS = 4
LANES = 128
SUBLANES = 8
COLT = 512
SEG = 256
VMEM_LIMIT = 48 * 1024 * 1024
VMEM_LIMIT_BIG = 60 * 1024 * 1024

ADAM_LR = 0.001
ADAM_B1 = 0.9
ADAM_B2 = 0.999
ADAM_EPS = 1e-08
ADAM_WD = 0.01
ADAM_STEP = 10

NN = (((1,), (0,)), ((), ()))
NT = (((1,), (1,)), ((), ()))
TN = (((0,), (0,)), ((), ()))


def _call(body, **kw):
    return pl.pallas_call(body, **kw)


def _call_after(body, after, **kw):
    n_in, n_after = len(kw["in_specs"]), len(after)
    kw["in_specs"] = list(kw["in_specs"]) + [pl.BlockSpec(memory_space=pl.ANY)] * n_after

    def body_after(*refs):
        body(*refs[:n_in], *refs[n_in + n_after:])

    call = _call(body_after, **kw)
    return lambda *args: call(*args, *after)


def _params(sem, vmem=VMEM_LIMIT):
    return pltpu.CompilerParams(dimension_semantics=sem, vmem_limit_bytes=vmem)


def _tile(n, pref):
    return pref if n % pref == 0 else n


def _sigmoid(v):
    return 1.0 / (1.0 + jnp.exp(-v))


def _hbm():
    return pl.BlockSpec(memory_space=pl.ANY)


def cast_bf16(a, name):
    r, c = a.shape
    tr = _tile(r, 512)

    def body(a_ref, o_ref):
        o_ref[...] = a_ref[...].astype(BF16)

    return _call(body, name=name, grid=(r // tr,),
                 in_specs=[pl.BlockSpec((tr, c), lambda i: (i, 0))],
                 out_specs=pl.BlockSpec((tr, c), lambda i: (i, 0)),
                 out_shape=jax.ShapeDtypeStruct((r, c), BF16),
                 compiler_params=_params(("parallel",)))(a)


def cast_into_slot(a, chip, name):
    r, c = a.shape
    tr = _tile(r, 512)

    def body(chip_ref, a_ref, o_ref):
        o_ref[...] = a_ref[...].astype(BF16)

    grid_spec = pltpu.PrefetchScalarGridSpec(
        num_scalar_prefetch=1, grid=(r // tr,),
        in_specs=[pl.BlockSpec((tr, c), lambda i, chip_ref: (i, 0))],
        out_specs=pl.BlockSpec((None, tr, c), lambda i, chip_ref: (chip_ref[0], i, 0)))
    return _call(body, name=name, grid_spec=grid_spec,
                 out_shape=jax.ShapeDtypeStruct((N_CHIPS, r, c), BF16),
                 compiler_params=_params(("parallel",)))(chip, a)


def out_proj_norm(mix, wo, x, gain):
    s, d = x.shape
    tm = _tile(s, 512)

    def body(m_ref, w_ref, x_ref, g_ref, x1_ref, hg_ref):
        x1 = x_ref[...] + jnp.dot(m_ref[...], w_ref[...], preferred_element_type=F32)
        x1_ref[...] = x1
        r = lax.rsqrt(jnp.mean(x1 * x1, axis=-1, keepdims=True) + EPS)
        hg_ref[...] = ((x1 * r) * g_ref[...]).astype(BF16)

    row = pl.BlockSpec((tm, d), lambda i: (i, 0))
    return _call(body, name="out_proj_norm", grid=(s // tm,),
                 in_specs=[row, pl.BlockSpec((d, d), lambda i: (0, 0), pipeline_mode=pl.Buffered(1)), row,
                           pl.BlockSpec((1, d), lambda i: (0, 0))],
                 out_specs=[row, row],
                 out_shape=[jax.ShapeDtypeStruct((s, d), F32), jax.ShapeDtypeStruct((s, d), BF16)],
                 compiler_params=_params(("parallel",), VMEM_LIMIT_BIG))(mix, wo, x, gain)


def gate_proj_bwd_norm(d_gl, wg, xin, add, gain):
    s, d = xin.shape
    tm = _tile(s, 256)

    def body(dg_ref, w_ref, x_ref, a_ref, g_ref, dx_ref, dxb_ref, acc_ref):
        i = pl.program_id(0)
        dyv = lax.dot_general(dg_ref[...], w_ref[...], NT, preferred_element_type=F32)
        xf = x_ref[...]
        r = lax.rsqrt(jnp.mean(xf * xf, axis=-1, keepdims=True) + EPS)
        xhat = xf * r
        gd = dyv * g_ref[...]
        dx = a_ref[...] + r * (gd - xhat * jnp.mean(xhat * gd, axis=-1, keepdims=True))
        dx_ref[...] = dx
        dxb_ref[...] = dx.astype(BF16)

        @pl.when(i == 0)
        def _():
            acc_ref[...] = jnp.zeros_like(acc_ref)

        acc_ref[0:1, :] += jnp.sum(dyv * xhat, axis=0, keepdims=True)

    row = pl.BlockSpec((tm, d), lambda i: (i, 0))
    return _call(body, name="gate_proj_bwd_norm", grid=(s // tm,),
                 in_specs=[row, pl.BlockSpec((d, d), lambda i: (0, 0), pipeline_mode=pl.Buffered(1)), row, row,
                           pl.BlockSpec((1, d), lambda i: (0, 0))],
                 out_specs=[row, row, pl.BlockSpec((SUBLANES, d), lambda i: (0, 0))],
                 out_shape=[jax.ShapeDtypeStruct((s, d), F32), jax.ShapeDtypeStruct((s, d), BF16),
                            jax.ShapeDtypeStruct((SUBLANES, d), F32)],
                 compiler_params=_params(("arbitrary",), VMEM_LIMIT_BIG))(d_gl, wg, xin, add, gain)


def head_fwd_bwd(x1, hg, wg, p, wp, tgt, bias, ple_gain):
    s, d = x1.shape
    tm = _tile(s, 256)
    n_row = s // tm

    def body(x1_ref, hg_ref, wg_ref, p_ref, wp_ref, t_ref, b_ref, g_ref, dgl_ref, dy_ref, gwp_ref, acc_ref,
             gacc_ref):
        i = pl.program_id(0)
        gl = jnp.dot(hg_ref[...], wg_ref[...], preferred_element_type=F32)
        pb = p_ref[...].astype(BF16)
        pev = jnp.concatenate([jnp.dot(pb, wp_ref[q], preferred_element_type=F32) for q in range(N_CHIPS)],
                              axis=1)
        r = lax.rsqrt(jnp.mean(pev * pev, axis=-1, keepdims=True) + EPS)
        pehat = pev * r
        gain = g_ref[...]
        e = pehat * gain
        gate = _sigmoid(gl + b_ref[...])
        diff = (x1_ref[...] + gate * e) - t_ref[...]
        dy = diff * (1.0 / d)
        dy_ref[...] = dy
        d_gl = (dy * e) * (gate * (1.0 - gate))
        dgl_ref[...] = d_gl.astype(BF16)
        d_e = dy * gate
        gd = d_e * gain
        d_pe = r * (gd - pehat * jnp.mean(pehat * gd, axis=-1, keepdims=True))
        g_part = lax.dot_general(pb, d_pe.astype(BF16), TN, preferred_element_type=F32)

        @pl.when(i == 0)
        def _():
            acc_ref[...] = jnp.zeros_like(acc_ref)
            gacc_ref[...] = g_part

        @pl.when(i > 0)
        def _():
            gacc_ref[...] += g_part

        @pl.when(i == n_row - 1)
        def _():
            pq = d // N_CHIPS
            for q in range(N_CHIPS):
                gwp_ref[q] = gacc_ref[:, q * pq:(q + 1) * pq].astype(BF16)

        acc_ref[0:1, :] += jnp.sum(d_gl, axis=0, keepdims=True)
        acc_ref[1:2, :] += jnp.sum(d_e * pehat, axis=0, keepdims=True)
        acc_ref[2:3, :] += jnp.sum(diff * diff, axis=0, keepdims=True) * (0.5 / d)

    row = pl.BlockSpec((tm, d), lambda i: (i, 0))
    vec = pl.BlockSpec((1, d), lambda i: (0, 0))
    ple = p.shape[1]
    return _call(body, name="head_fwd_bwd", grid=(n_row,),
                 in_specs=[row, row, pl.BlockSpec((d, d), lambda i: (0, 0), pipeline_mode=pl.Buffered(1)),
                           pl.BlockSpec((tm, ple), lambda i: (i, 0)),
                           pl.BlockSpec(wp.shape, lambda i: (0, 0, 0)), row, vec, vec],
                 out_specs=[row, row, pl.BlockSpec(wp.shape, lambda i: (0, 0, 0)),
                            pl.BlockSpec((SUBLANES, d), lambda i: (0, 0))],
                 out_shape=[jax.ShapeDtypeStruct((s, d), BF16), jax.ShapeDtypeStruct((s, d), F32),
                            jax.ShapeDtypeStruct(wp.shape, BF16), jax.ShapeDtypeStruct((SUBLANES, d), F32)],
                 scratch_shapes=[pltpu.VMEM((ple, d), F32)],
                 compiler_params=_params(("arbitrary",), VMEM_LIMIT_BIG))(
                     x1, hg, wg, p, wp, tgt, bias, ple_gain)


def _adamw_step(gv, w_ref, m_ref, v_ref, go_ref, d_ref, mo_ref, vo_ref):
    mn = ADAM_B1 * m_ref[...] + (1.0 - ADAM_B1) * gv
    vn = ADAM_B2 * v_ref[...] + (1.0 - ADAM_B2) * (gv * gv)
    m_hat = mn / (1.0 - ADAM_B1 ** ADAM_STEP)
    v_hat = vn / (1.0 - ADAM_B2 ** ADAM_STEP)
    go_ref[...] = gv
    d_ref[...] = -ADAM_LR * (m_hat / (jnp.sqrt(v_hat) + ADAM_EPS) + ADAM_WD * w_ref[...])
    mo_ref[...] = mn
    vo_ref[...] = vn


def adamw(g, w, m, v, name, after=()):
    r, c = g.shape
    tr = _tile(r, 256)

    def body(g_ref, w_ref, m_ref, v_ref, go_ref, d_ref, mo_ref, vo_ref):
        _adamw_step(g_ref[...], w_ref, m_ref, v_ref, go_ref, d_ref, mo_ref, vo_ref)

    blk = pl.BlockSpec((tr, c), lambda i: (i, 0))
    shp = jax.ShapeDtypeStruct((r, c), F32)
    return _call_after(body, after, name=name, grid=(r // tr,), in_specs=[blk] * 4, out_specs=[blk] * 4,
                       out_shape=[shp] * 4, compiler_params=_params(("parallel",)))(g, w, m, v)


def matmul(a, b, *, grid, a_spec, b_spec, o_spec, out_shape, dims, name):
    nk = grid[2]
    acc_shape = tuple(d for d in o_spec.block_shape if d is not None)

    def body(a_ref, b_ref, o_ref, *scratch):
        part = lax.dot_general(a_ref[...].astype(BF16), b_ref[...].astype(BF16), dims, preferred_element_type=F32)
        if nk == 1:
            o_ref[...] = part.astype(o_ref.dtype)
            return
        acc_ref, = scratch
        k = pl.program_id(2)

        @pl.when(k == 0)
        def _():
            acc_ref[...] = part

        @pl.when((k > 0) & (k < nk - 1))
        def _():
            acc_ref[...] += part

        @pl.when(k == nk - 1)
        def _():
            o_ref[...] = (acc_ref[...] + part).astype(o_ref.dtype)

    return _call(body, name=name, grid=grid, in_specs=[a_spec, b_spec], out_specs=o_spec, out_shape=out_shape,
                 scratch_shapes=[pltpu.VMEM(acc_shape, F32)] if nk > 1 else [],
                 compiler_params=_params(("parallel", "parallel", "arbitrary")))(a, b)


def norm_in_proj(x, gain, w, order, in_w):
    s, d = x.shape
    sh = w.shape[1]
    tm = _tile(s, 512)

    def body(order_ref, x_ref, g_ref, w_ref, h_ref, z_ref):
        xf = x_ref[...]
        r = lax.rsqrt(jnp.mean(xf * xf, axis=-1, keepdims=True) + EPS)
        hb = ((xf * r) * g_ref[...]).astype(BF16)
        h_ref[...] = hb
        z_ref[...] = jnp.dot(hb, w_ref[...], preferred_element_type=F32)

    grid_spec = pltpu.PrefetchScalarGridSpec(
        num_scalar_prefetch=1, grid=(s // tm,),
        in_specs=[pl.BlockSpec((tm, d), lambda i, order_ref: (i, 0)),
                  pl.BlockSpec((1, d), lambda i, order_ref: (0, 0)),
                  pl.BlockSpec((d, sh), lambda i, order_ref: (0, 0), pipeline_mode=pl.Buffered(1))],
        out_specs=[pl.BlockSpec((tm, d), lambda i, order_ref: (i, 0)),
                   pl.BlockSpec((tm, sh), lambda i, order_ref: (i, order_ref[0]))])
    return _call(body, name="norm_in_proj", grid_spec=grid_spec,
                 out_shape=[jax.ShapeDtypeStruct((s, d), BF16), jax.ShapeDtypeStruct((s, in_w), F32)],
                 compiler_params=_params(("parallel",)))(order, x, gain, w)


def in_proj_part(h, w, z_prev, order, q, in_w, after=()):
    s, d = h.shape
    sh = w.shape[1]
    tm = _tile(s, 512)

    def body(order_ref, h_ref, w_ref, *rest):
        rest[-1][...] = jnp.dot(h_ref[...], w_ref[...], preferred_element_type=F32)

    in_specs = [pl.BlockSpec((tm, d), lambda i, order_ref: (i, 0)),
                pl.BlockSpec((d, sh), lambda i, order_ref: (0, 0))]
    args = [order, h, w]
    if z_prev is not None:
        in_specs.append(_hbm())
        args.append(z_prev)
    in_specs += [_hbm()] * len(after)
    args += list(after)
    grid_spec = pltpu.PrefetchScalarGridSpec(
        num_scalar_prefetch=1, grid=(s // tm,), in_specs=in_specs,
        out_specs=pl.BlockSpec((tm, sh), lambda i, order_ref: (i, order_ref[q])))
    return _call(body, name="mm_z%d" % q, grid_spec=grid_spec,
                 out_shape=jax.ShapeDtypeStruct((s, in_w), F32),
                 input_output_aliases={3: 0} if z_prev is not None else {},
                 compiler_params=_params(("parallel",)))(*args)


def in_proj_wgrad_half(h, dz, g_prev, half, after):
    s, d = h.shape
    sh = dz.shape[1] // N_CHIPS
    hr = d // 2

    def body(half_ref, h_ref, dz_ref, *rest):
        rest[-1][...] = lax.dot_general(h_ref[...], dz_ref[...], TN, preferred_element_type=F32).astype(BF16)

    extra = ([g_prev] if g_prev is not None else []) + list(after)
    grid_spec = pltpu.PrefetchScalarGridSpec(
        num_scalar_prefetch=1, grid=(N_CHIPS,),
        in_specs=[pl.BlockSpec((s, hr), lambda j, half_ref: (0, half_ref[0]), pipeline_mode=pl.Buffered(1)),
                  pl.BlockSpec((s, sh), lambda j, half_ref: (0, j))] + [_hbm()] * len(extra),
        out_specs=pl.BlockSpec((None, hr, sh), lambda j, half_ref: (j, half_ref[0], 0)))
    return _call(body, name="mm_g_wi_%s" % ("keep" if g_prev is not None else "send"), grid_spec=grid_spec,
                 out_shape=jax.ShapeDtypeStruct((N_CHIPS, d, sh), BF16),
                 input_output_aliases={3: 0} if g_prev is not None else {},
                 compiler_params=_params(("parallel",), VMEM_LIMIT_BIG))(half, h, dz, *extra)


def in_proj_bwd_norm(dz, ws, order, xin, add, gain, after):
    s, d = xin.shape
    sh = ws[0].shape[1]
    tm = _tile(s, 256)
    nq, n_after = len(ws), len(after)

    def body(order_ref, *refs):
        a_refs, b_refs = refs[:nq], refs[nq:2 * nq]
        x_ref, add_ref, g_ref = refs[2 * nq:2 * nq + 3]
        dx_ref, acc_ref = refs[2 * nq + 3 + n_after:]
        i = pl.program_id(0)
        dyv = lax.dot_general(a_refs[0][...], b_refs[0][...], NT, preferred_element_type=F32)
        for q in range(1, nq):
            dyv += lax.dot_general(a_refs[q][...], b_refs[q][...], NT, preferred_element_type=F32)
        xf = x_ref[...]
        r = lax.rsqrt(jnp.mean(xf * xf, axis=-1, keepdims=True) + EPS)
        xhat = xf * r
        gd = dyv * g_ref[...]
        dx_ref[...] = add_ref[...] + r * (gd - xhat * jnp.mean(xhat * gd, axis=-1, keepdims=True))

        @pl.when(i == 0)
        def _():
            acc_ref[...] = jnp.zeros_like(acc_ref)

        acc_ref[0:1, :] += jnp.sum(dyv * xhat, axis=0, keepdims=True)

    a_spec = lambda q: pl.BlockSpec((tm, sh), functools.partial(lambda i, order_ref, q: (i, order_ref[q]), q=q))
    row = pl.BlockSpec((tm, d), lambda i, order_ref: (i, 0))
    grid_spec = pltpu.PrefetchScalarGridSpec(
        num_scalar_prefetch=1, grid=(s // tm,),
        in_specs=[a_spec(q) for q in range(nq)]
        + [pl.BlockSpec((d, sh), lambda i, order_ref: (0, 0), pipeline_mode=pl.Buffered(1))] * nq
        + [row, row, pl.BlockSpec((1, d), lambda i, order_ref: (0, 0))] + [_hbm()] * n_after,
        out_specs=[row, pl.BlockSpec((SUBLANES, d), lambda i, order_ref: (0, 0))])
    return _call(body, name="in_proj_bwd_norm", grid_spec=grid_spec,
                 out_shape=[jax.ShapeDtypeStruct((s, d), F32), jax.ShapeDtypeStruct((SUBLANES, d), F32)],
                 compiler_params=_params(("arbitrary",), VMEM_LIMIT_BIG))(
                     order, *([dz] * nq), *ws, xin, add, gain, *after)


def _seg_mean(sq, segb):
    parts = []
    for cgrp in range(sq.shape[1] // SEG):
        blk = sq[:, cgrp * SEG:(cgrp + 1) * SEG]
        hi = blk.astype(BF16)
        r1 = blk - hi.astype(F32)
        mid = r1.astype(BF16)
        lo = (r1 - mid.astype(F32)).astype(BF16)
        acc = jnp.dot(hi, segb, preferred_element_type=F32)
        acc += jnp.dot(mid, segb, preferred_element_type=F32)
        acc += jnp.dot(lo, segb, preferred_element_type=F32)
        parts.append(acc)
    out = parts[0] if len(parts) == 1 else jnp.concatenate(parts, axis=1)
    return out * (1.0 / HEAD_DIM)


def _rope(v, cos, sa, sb):
    parts = []
    for cgrp in range(v.shape[1] // LANES):
        blk = v[:, cgrp * LANES:(cgrp + 1) * LANES]
        parts.append(blk * cos + pltpu.roll(blk, LANES - 8, 1) * sa + pltpu.roll(blk, 8, 1) * sb)
    return parts[0] if len(parts) == 1 else jnp.concatenate(parts, axis=1)


def _rope_t(dv, cos, sa, sb):
    parts = []
    for cgrp in range(dv.shape[1] // LANES):
        blk = dv[:, cgrp * LANES:(cgrp + 1) * LANES]
        parts.append(blk * cos + pltpu.roll(blk * sa, 8, 1) + pltpu.roll(blk * sb, LANES - 8, 1))
    return parts[0] if len(parts) == 1 else jnp.concatenate(parts, axis=1)


def _tile_lanes(vec, width):
    reps = width // LANES
    return vec if reps == 1 else jnp.tile(vec, (1, reps))


def qk_prep_fwd(z, tabs, qg, kg, segb, aw, after=()):
    s = z.shape[0]
    tm = _tile(s, 512)
    wq = aw + 2 * KV_WIDTH

    def body(z_ref, cos_ref, sa_ref, sb_ref, qg_ref, kg_ref, seg_ref, qs_ref, ks_ref, vb_ref):
        zz = z_ref[...]
        q, k, v = zz[:, :aw], zz[:, aw:aw + KV_WIDTH], zz[:, aw + KV_WIDTH:]
        cos, sa, sb, segm = cos_ref[...], sa_ref[...], sb_ref[...], seg_ref[...]
        rq = lax.rsqrt(_seg_mean(q * q, segm) + EPS)
        qn = (q * rq) * _tile_lanes(qg_ref[...], aw)
        qs_ref[...] = (_rope(qn, cos, sa, sb) * (HEAD_DIM ** -0.5)).astype(BF16)
        rk = lax.rsqrt(_seg_mean(k * k, segm) + EPS)
        kn = (k * rk) * _tile_lanes(kg_ref[...], KV_WIDTH)
        ks_ref[...] = _rope(kn, cos, sa, sb).astype(BF16)
        vb_ref[...] = v.astype(BF16)

    tab = pl.BlockSpec((tm, LANES), lambda i: (i, 0))
    vec = pl.BlockSpec((1, LANES), lambda i: (0, 0))
    return _call_after(body, after, name="qk_prep_fwd", grid=(s // tm,),
                       in_specs=[pl.BlockSpec((tm, wq), lambda i: (i, 0)), tab, tab, tab, vec, vec,
                                 pl.BlockSpec((SEG, SEG), lambda i: (0, 0))],
                       out_specs=[pl.BlockSpec((tm, aw), lambda i: (i, 0)),
                                  pl.BlockSpec((tm, KV_WIDTH), lambda i: (i, 0)),
                                  pl.BlockSpec((tm, KV_WIDTH), lambda i: (i, 0))],
                       out_shape=[jax.ShapeDtypeStruct((s, aw), BF16), jax.ShapeDtypeStruct((s, KV_WIDTH), BF16),
                                  jax.ShapeDtypeStruct((s, KV_WIDTH), BF16)],
                       compiler_params=_params(("parallel",)))(z, *tabs, qg, kg, segb)


def qk_prep_bwd(z, dqs, dks, dvs, d_gate, tabs, qg, kg, segb, aw):
    s, in_w = z.shape
    tm = _tile(s, 512)
    wq = aw + 2 * KV_WIDTH

    def body(z_ref, dq_ref, dk_ref, dv_ref, dga_ref, cos_ref, sa_ref, sb_ref, qg_ref, kg_ref, seg_ref,
             dz_ref, acc_ref):
        i = pl.program_id(0)
        zz = z_ref[...]
        q, k = zz[:, :aw], zz[:, aw:aw + KV_WIDTH]
        cos, sa, sb, segm = cos_ref[...], sa_ref[...], sb_ref[...], seg_ref[...]

        def one(xv, dout, gvec, width):
            g = _tile_lanes(gvec, width)
            r = lax.rsqrt(_seg_mean(xv * xv, segm) + EPS)
            xhat = xv * r
            dn = _rope_t(dout, cos, sa, sb)
            gd = dn * g
            dx = r * (gd - xhat * _seg_mean(xhat * gd, segm))
            contrib = jnp.sum(dn * xhat, axis=0, keepdims=True)
            folded = contrib[:, :LANES]
            for cgrp in range(1, width // LANES):
                folded = folded + contrib[:, cgrp * LANES:(cgrp + 1) * LANES]
            return dx, folded + pltpu.roll(folded, HEAD_DIM, 1)

        dq, gq = one(q, dq_ref[...] * (HEAD_DIM ** -0.5), qg_ref[...], aw)
        dk, gk = one(k, dk_ref[...], kg_ref[...], KV_WIDTH)
        dz_ref[:, :aw] = dq.astype(BF16)
        dz_ref[:, aw:aw + KV_WIDTH] = dk.astype(BF16)
        dz_ref[:, aw + KV_WIDTH:wq] = dv_ref[...].astype(BF16)
        dz_ref[:, wq:] = dga_ref[...]

        @pl.when(i == 0)
        def _():
            acc_ref[...] = jnp.zeros_like(acc_ref)

        acc_ref[0:1, :] += gq
        acc_ref[1:2, :] += gk

    tab = pl.BlockSpec((tm, LANES), lambda i: (i, 0))
    vec = pl.BlockSpec((1, LANES), lambda i: (0, 0))
    kvb = pl.BlockSpec((tm, KV_WIDTH), lambda i: (i, 0))
    awb = pl.BlockSpec((tm, aw), lambda i: (i, 0))
    return _call(body, name="qk_prep_bwd", grid=(s // tm,),
                 in_specs=[pl.BlockSpec((tm, wq), lambda i: (i, 0)), awb, kvb, kvb, awb, tab, tab, tab, vec, vec,
                           pl.BlockSpec((SEG, SEG), lambda i: (0, 0))],
                 out_specs=[pl.BlockSpec((tm, wq + aw), lambda i: (i, 0)),
                            pl.BlockSpec((SUBLANES, LANES), lambda i: (0, 0))],
                 out_shape=[jax.ShapeDtypeStruct((s, in_w), BF16), jax.ShapeDtypeStruct((SUBLANES, LANES), F32)],
                 compiler_params=_params(("arbitrary",)))(z, dqs, dks, dvs, d_gate, *tabs, qg, kg, segb)


def _placed(band, lane_idx):
    out = {}
    for kh in range(N_KV_HEADS):
        grp = band[:, (kh // 2) * LANES:(kh // 2 + 1) * LANES]
        for half in (0, 1):
            t = grp if half == kh % 2 else pltpu.roll(grp, HEAD_DIM, 1)
            keep = (lane_idx >= half * HEAD_DIM) & (lane_idx < (half + 1) * HEAD_DIM)
            out[kh, half] = jnp.where(keep, t, jnp.zeros_like(t))
    return out


def _softmax_with_sink(sc, valid, sink):
    sc = jnp.where(valid, sc, NEG_INF)
    m = jnp.maximum(jnp.max(sc, axis=-1, keepdims=True), sink)
    e = jnp.exp(sc - m)
    es = jnp.exp(sink - m)
    den = jnp.sum(e, axis=-1, keepdims=True) + es
    return e / den, es / den


def _stack_halves(placed, kh):
    return jnp.concatenate([placed[kh, 0], placed[kh, 1]], axis=0)


def _valid_mask(n):
    r_i = lax.broadcasted_iota(jnp.int32, (BLOCK, 2 * BLOCK), 0)
    j_i = lax.broadcasted_iota(jnp.int32, (BLOCK, 2 * BLOCK), 1)
    return (j_i > r_i) & (j_i <= r_i + BLOCK) & ((n > 0) | (j_i >= BLOCK))


def attn_fwd(qs, ks, vb, z, sinks, aw, d, ga_off):
    s = qs.shape[0]
    nb = s // BLOCK
    nq = aw // HEAD_DIM
    grp_sz = nq // N_KV_HEADS
    n_ga = aw // COLT

    def body(q_ref, ko_ref, kp_ref, vo_ref, vp_ref, *rest):
        ga_refs = rest[:n_ga]
        sink_ref, attn_ref, mix_ref = rest[n_ga:]
        n = pl.program_id(0)
        lane_idx = lax.broadcasted_iota(jnp.int32, (2 * BLOCK, LANES), 1)
        kpl = _placed(jnp.concatenate([kp_ref[...], ko_ref[...]], axis=0), lane_idx)
        vpl = _placed(jnp.concatenate([vp_ref[...], vo_ref[...]], axis=0), lane_idx)
        valid = _valid_mask(n)
        groups = range(nq // 2)
        kcat = [_stack_halves(kpl, kh) for kh in range(N_KV_HEADS)]
        vcat = [_stack_halves(vpl, kh) for kh in range(N_KV_HEADS)]
        scs = [lax.dot_general(q_ref[:, c * LANES:(c + 1) * LANES], kcat[2 * c // grp_sz], NT,
                               preferred_element_type=F32) for c in groups]
        probs = [jnp.concatenate(
            [_softmax_with_sink(scs[c][:, half * 2 * BLOCK:(half + 1) * 2 * BLOCK], valid,
                                sink_ref[0, 2 * c + half])[0].astype(BF16) for half in (0, 1)], axis=1)
                 for c in groups]
        for c in groups:
            acc = jnp.dot(probs[c], vcat[2 * c // grp_sz], preferred_element_type=F32)
            attn_ref[:, c * LANES:(c + 1) * LANES] = acc
            col = c * LANES
            ga = ga_refs[col // COLT][:, col % COLT:col % COLT + LANES]
            mix_ref[:, c * LANES:(c + 1) * LANES] = (acc * (ga * _sigmoid(ga))).astype(BF16)

    own = lambda n: (n, 0)
    prev = lambda n: (jnp.maximum(n - 1, 0), 0)
    kvs = lambda imap: pl.BlockSpec((BLOCK, KV_WIDTH), imap)
    ga_specs = [pl.BlockSpec((BLOCK, COLT), functools.partial(lambda n, j: (n, ga_off + j), j=j))
                for j in range(n_ga)]
    return _call(body, name="attn_fwd", grid=(nb,),
                 in_specs=[pl.BlockSpec((BLOCK, aw), own), kvs(own), kvs(prev), kvs(own), kvs(prev)]
                 + ga_specs + [pl.BlockSpec(memory_space=pltpu.SMEM)],
                 out_specs=[pl.BlockSpec((BLOCK, aw), own), pl.BlockSpec((BLOCK, aw), own)],
                 out_shape=[jax.ShapeDtypeStruct((s, aw), F32), jax.ShapeDtypeStruct((s, d), BF16)],
                 compiler_params=_params(("parallel",)))(qs, ks, ks, vb, vb, *([z] * n_ga), sinks)


def out_proj_bwd_gate(d_x1b, wo, attn, z, aw, ga_off, after=()):
    s, d = d_x1b.shape
    tm = _tile(s, 512)
    n_ga = aw // COLT

    def body(dx_ref, w_ref, at_ref, *rest):
        ga_refs, (do_ref, dga_ref, dmc_ref) = rest[:n_ga], rest[n_ga:]
        dm = lax.dot_general(dx_ref[...], w_ref[...], NT, preferred_element_type=F32)
        dmc_ref[...] = dm[:, aw:]
        for j in range(n_ga):
            cols = slice(j * COLT, (j + 1) * COLT)
            ga = ga_refs[j][...]
            sig = _sigmoid(ga)
            dma = dm[:, cols]
            do_ref[:, cols] = (dma * (ga * sig)).astype(BF16)
            dga_ref[:, cols] = ((dma * at_ref[:, cols]) * (sig * (1.0 + ga * (1.0 - sig)))).astype(BF16)

    row = lambda width: pl.BlockSpec((tm, width), lambda i: (i, 0))
    ga_specs = [pl.BlockSpec((tm, COLT), functools.partial(lambda i, j: (i, ga_off + j), j=j)) for j in range(n_ga)]
    return _call_after(body, after, name="out_proj_bwd_gate", grid=(s // tm,),
                       in_specs=[row(d), pl.BlockSpec((d, d), lambda i: (0, 0), pipeline_mode=pl.Buffered(1)),
                                 row(aw)] + ga_specs,
                       out_specs=[row(aw), row(aw), row(d - aw)],
                       out_shape=[jax.ShapeDtypeStruct((s, aw), BF16), jax.ShapeDtypeStruct((s, aw), BF16),
                                  jax.ShapeDtypeStruct((s, d - aw), F32)],
                       compiler_params=_params(("parallel",)))(d_x1b, wo, attn, *([z] * n_ga))


def attn_bwd(qs, ks, vb, d_o, sinks, aw, after=()):
    s = qs.shape[0]
    nb = s // BLOCK
    nq = aw // HEAD_DIM
    grp_sz = nq // N_KV_HEADS

    def body(q_ref, ko_ref, kp_ref, vo_ref, vp_ref, do_ref, sink_ref, dq_ref, dk_ref, dv_ref, ds_ref,
             ck_ref, cv_ref):
        n = pl.program_id(0)

        @pl.when(n == 0)
        def _():
            ds_ref[...] = jnp.zeros_like(ds_ref)
            dk_ref[...] = jnp.zeros_like(dk_ref)
            dv_ref[...] = jnp.zeros_like(dv_ref)

        @pl.when(n < nb)
        def _():
            lane_idx = lax.broadcasted_iota(jnp.int32, (2 * BLOCK, LANES), 1)
            lane_row = lax.broadcasted_iota(jnp.int32, (1, LANES), 1)
            kpl = _placed(jnp.concatenate([kp_ref[...], ko_ref[...]], axis=0), lane_idx)
            vpl = _placed(jnp.concatenate([vp_ref[...], vo_ref[...]], axis=0), lane_idx)
            valid = _valid_mask(n)
            ds_row = jnp.zeros((1, LANES), F32)
            wide = 2 * BLOCK
            groups = range(nq // 2)
            per_kv = grp_sz // 2
            kcat = [_stack_halves(kpl, kh) for kh in range(N_KV_HEADS)]
            vcat = [_stack_halves(vpl, kh) for kh in range(N_KV_HEADS)]
            qg = [q_ref[:, c * LANES:(c + 1) * LANES] for c in groups]
            dog = [do_ref[:, c * LANES:(c + 1) * LANES] for c in groups]
            scs = [lax.dot_general(qg[c], kcat[c // per_kv], NT, preferred_element_type=F32) for c in groups]
            dps = [lax.dot_general(dog[c], vcat[c // per_kv], NT, preferred_element_type=F32) for c in groups]
            ds_pairs, p_pairs = [], []
            for c in groups:
                probs, dss = [], []
                for half in (0, 1):
                    h = 2 * c + half
                    cols = slice(half * wide, (half + 1) * wide)
                    p, p_sink = _softmax_with_sink(scs[c][:, cols], valid, sink_ref[0, h])
                    delta = jnp.sum(p * dps[c][:, cols], axis=-1, keepdims=True)
                    dss.append((p * (dps[c][:, cols] - delta)).astype(BF16))
                    probs.append(p.astype(BF16))
                    dsink = -jnp.sum(p_sink * delta, axis=0, keepdims=True)
                    ds_row += jnp.where(lane_row == h, dsink, 0.0)
                ds_pairs.append(jnp.concatenate(dss, axis=1))
                p_pairs.append(jnp.concatenate(probs, axis=1))
            for c in groups:
                dq_ref[:, c * LANES:(c + 1) * LANES] = jnp.dot(ds_pairs[c], kcat[c // per_kv],
                                                               preferred_element_type=F32)
            dkts = [lax.dot_general(qg[c], ds_pairs[c], TN, preferred_element_type=F32) for c in groups]
            dvts = [lax.dot_general(dog[c], p_pairs[c], TN, preferred_element_type=F32) for c in groups]
            dkt, dvt = [], []
            for kh in range(N_KV_HEADS):
                for parts, out in ((dkts, dkt), (dvts, dvt)):
                    tot = parts[kh * per_kv]
                    for c in range(kh * per_kv + 1, (kh + 1) * per_kv):
                        tot = tot + parts[c]
                    out.append(tot[:HEAD_DIM, :wide] + tot[HEAD_DIM:, wide:])
            ds_ref[0:1, :] += ds_row
            back = lambda t: jnp.concatenate(
                [jnp.concatenate(t[2 * g:2 * g + 2], axis=0).T for g in range(N_KV_HEADS // 2)], axis=1)
            dk_band, dv_band = back(dkt), back(dvt)

            @pl.when(n > 0)
            def _():
                dk_ref[...] = ck_ref[...] + dk_band[:BLOCK]
                dv_ref[...] = cv_ref[...] + dv_band[:BLOCK]

            ck_ref[...] = dk_band[BLOCK:]
            cv_ref[...] = dv_band[BLOCK:]

        @pl.when(n == nb)
        def _():
            dk_ref[...] = ck_ref[...]
            dv_ref[...] = cv_ref[...]

    own = lambda n: (jnp.minimum(n, nb - 1), 0)
    prev = lambda n: (jnp.clip(n - 1, 0, nb - 1), 0)
    done = lambda n: (jnp.maximum(n - 1, 0), 0)
    kvs = lambda imap: pl.BlockSpec((BLOCK, KV_WIDTH), imap)
    return _call_after(body, after, name="attn_bwd", grid=(nb + 1,),
                       in_specs=[pl.BlockSpec((BLOCK, aw), own), kvs(own), kvs(prev), kvs(own), kvs(prev),
                                 pl.BlockSpec((BLOCK, aw), own), pl.BlockSpec(memory_space=pltpu.SMEM)],
                       out_specs=[pl.BlockSpec((BLOCK, aw), own), kvs(done), kvs(done),
                                  pl.BlockSpec((SUBLANES, LANES), lambda n: (0, 0))],
                       out_shape=[jax.ShapeDtypeStruct((s, aw), F32), jax.ShapeDtypeStruct((s, KV_WIDTH), F32),
                                  jax.ShapeDtypeStruct((s, KV_WIDTH), F32),
                                  jax.ShapeDtypeStruct((SUBLANES, LANES), F32)],
                       scratch_shapes=[pltpu.VMEM((BLOCK, KV_WIDTH), F32), pltpu.VMEM((BLOCK, KV_WIDTH), F32)],
                       compiler_params=_params(("arbitrary",)))(qs, ks, ks, vb, vb, d_o, sinks)


def conv_fwd(z, conv_w, mix, aw, cw, b_off):
    s = z.shape[0]
    tm = _tile(s, 1024)
    nseg = cw // COLT
    hb = tm // SUBLANES

    def body(b_ref, c_ref, h_ref, g_ref, cp_ref, hp_ref, w_ref, mix_in, mix_ref):
        i = pl.program_id(0)
        u = c_ref[...] * h_ref[...]
        up = jnp.where(i > 0, cp_ref[...] * hp_ref[...], 0.0)
        ext = jnp.concatenate([up, u], axis=0)
        um1 = pltpu.roll(ext, 1, 0)[SUBLANES:]
        um2 = pltpu.roll(ext, 2, 0)[SUBLANES:]
        w = w_ref[...]
        cv = w[0:1] * um2 + w[1:2] * um1 + w[2:3] * u
        g = g_ref[...]
        mix_ref[...] = ((b_ref[...] * cv) * (g * _sigmoid(g))).astype(BF16)

    seg = lambda k: pl.BlockSpec((tm, COLT), functools.partial(lambda i, j, k: (i, b_off + k * nseg + j), k=k))
    halo = lambda k: pl.BlockSpec(
        (SUBLANES, COLT), functools.partial(lambda i, j, k: (jnp.maximum(i * hb - 1, 0), b_off + k * nseg + j), k=k))
    return _call(body, name="conv_fwd", grid=(s // tm, nseg),
                 in_specs=[seg(0), seg(1), seg(2), seg(3), halo(1), halo(2),
                           pl.BlockSpec((SUBLANES, COLT), lambda i, j: (0, j)), _hbm()],
                 out_specs=pl.BlockSpec((tm, COLT), lambda i, j: (i, aw // COLT + j)),
                 out_shape=jax.ShapeDtypeStruct(mix.shape, BF16),
                 input_output_aliases={7: 0},
                 compiler_params=_params(("parallel", "parallel")))(z, z, z, z, z, z, conv_w, mix)


def conv_bwd(z, d_mix, conv_w, dz, aw, cw, b_off):
    s = z.shape[0]
    tm = _tile(s, 512)
    nseg = cw // COLT
    hb = tm // SUBLANES
    n_row = s // tm
    last_h = s // SUBLANES - 1
    n_step = nseg * n_row

    def body(b_ref, c_ref, h_ref, g_ref, cp_ref, hp_ref, bn_ref, gn_ref, dm_ref, dmn_ref, w_ref, dz_in,
             dz_ref, acc_ref, stash_ref, sems):
        j = pl.program_id(0)
        i = pl.program_id(1)
        step = j * n_row + i
        slot = step % 2

        def copies(from_slot, at_step):
            jj, ii = at_step // n_row, at_step % n_row
            rows = pl.ds(pl.multiple_of(ii * tm, tm), tm)
            return [pltpu.make_async_copy(
                stash_ref.at[from_slot, q],
                dz_ref.at[rows, pl.ds(pl.multiple_of((b_off + q * nseg + jj) * COLT, COLT), COLT)],
                sems.at[from_slot, q]) for q in range(4)]

        @pl.when(step >= 2)
        def _():
            for cp in copies(slot, step - 2):
                cp.wait()

        cc, hh, bb, g = c_ref[...], h_ref[...], b_ref[...], g_ref[...]
        w = w_ref[...]
        u = cc * hh
        up = jnp.where(i > 0, cp_ref[...] * hp_ref[...], 0.0)
        ext = jnp.concatenate([up, u], axis=0)
        um1 = pltpu.roll(ext, 1, 0)[SUBLANES:]
        um2 = pltpu.roll(ext, 2, 0)[SUBLANES:]
        cv = w[0:1] * um2 + w[1:2] * um1 + w[2:3] * u
        sig = _sigmoid(g)
        sg = g * sig
        dm = dm_ref[...]
        d_cv = (dm * bb) * sg
        gn = gn_ref[...]
        d_cv_next = jnp.where(i < n_row - 1, (dmn_ref[...] * bn_ref[...]) * (gn * _sigmoid(gn)), 0.0)
        ext2 = jnp.concatenate([d_cv, d_cv_next], axis=0)
        dp1 = pltpu.roll(ext2, tm + SUBLANES - 1, 0)[:tm]
        dp2 = pltpu.roll(ext2, tm + SUBLANES - 2, 0)[:tm]
        d_u = w[2:3] * d_cv + w[1:2] * dp1 + w[0:1] * dp2
        stash_ref[slot, 0] = ((dm * cv) * sg).astype(BF16)
        stash_ref[slot, 1] = (d_u * hh).astype(BF16)
        stash_ref[slot, 2] = (d_u * cc).astype(BF16)
        stash_ref[slot, 3] = (((dm * bb) * cv) * (sig * (1.0 + g * (1.0 - sig)))).astype(BF16)
        for cp in copies(slot, step):
            cp.start()

        @pl.when(i == 0)
        def _():
            acc_ref[...] = jnp.zeros_like(acc_ref)

        acc_ref[0:1, :] += jnp.sum(d_cv * um2, axis=0, keepdims=True)
        acc_ref[1:2, :] += jnp.sum(d_cv * um1, axis=0, keepdims=True)
        acc_ref[2:3, :] += jnp.sum(d_cv * u, axis=0, keepdims=True)

        @pl.when(step == n_step - 1)
        def _():
            if n_step >= 2:
                for cp in copies(1 - slot, step - 1):
                    cp.wait()
            for cp in copies(slot, step):
                cp.wait()

    seg = lambda q: pl.BlockSpec((tm, COLT), functools.partial(lambda j, i, q: (i, b_off + q * nseg + j), q=q))
    halo_p = lambda q: pl.BlockSpec(
        (SUBLANES, COLT),
        functools.partial(lambda j, i, q: (jnp.maximum(i * hb - 1, 0), b_off + q * nseg + j), q=q))
    halo_n = lambda q: pl.BlockSpec(
        (SUBLANES, COLT),
        functools.partial(lambda j, i, q: (jnp.minimum((i + 1) * hb, last_h), b_off + q * nseg + j), q=q))
    return _call(body, name="conv_bwd", grid=(nseg, n_row),
                 in_specs=[seg(0), seg(1), seg(2), seg(3), halo_p(1), halo_p(2), halo_n(0), halo_n(3),
                           pl.BlockSpec((tm, COLT), lambda j, i: (i, j)),
                           pl.BlockSpec((SUBLANES, COLT), lambda j, i: (jnp.minimum((i + 1) * hb, last_h), j)),
                           pl.BlockSpec((SUBLANES, COLT), lambda j, i: (0, j)), _hbm()],
                 out_specs=[_hbm(), pl.BlockSpec((SUBLANES, COLT), lambda j, i: (0, j))],
                 out_shape=[jax.ShapeDtypeStruct(dz.shape, BF16), jax.ShapeDtypeStruct((SUBLANES, cw), F32)],
                 input_output_aliases={11: 0},
                 scratch_shapes=[pltpu.VMEM((2, 4, tm, COLT), BF16), pltpu.SemaphoreType.DMA((2, 4))],
                 compiler_params=_params(("arbitrary", "arbitrary")))(
                     z, z, z, z, z, z, z, z, d_mix, d_mix, conv_w, dz)


def _place():
    x, y, c = lax.axis_index("x"), lax.axis_index("y"), lax.axis_index("c")
    chips = [(1 - x, y), (x, 1 - y), (1 - x, 1 - y)]
    return x, y, c, chips


def _remote(src, dst, send_sem, recv_sem, device):
    return pltpu.make_async_remote_copy(src_ref=src, dst_ref=dst, send_sem=send_sem, recv_sem=recv_sem,
                                        device_id=device, device_id_type=MESH)


def plan_gather_ici(n_split):
    def plan(refs, send, recv):
        x, y, c, chips = _place()
        me = 2 * x + y
        mine, theirs = [], []
        for w, ref in enumerate(refs):
            for j, (cx, cy) in enumerate(chips):
                def part(slot):
                    if w >= n_split:
                        return ref.at[slot]
                    hr = ref.shape[1] // 2
                    return ref.at[slot, pl.ds(c * hr, hr)]
                k = 3 * w + j
                mine.append(_remote(part(me), part(me), send.at[k], recv.at[k], (cx, cy, c)))
                theirs.append(_remote(part(2 * cx + cy), part(2 * cx + cy), send.at[k], recv.at[k], (cx, cy, c)))
        return mine, theirs
    return plan


def plan_shard_ici(j):
    def plan(refs, send, recv):
        _, _, c, chips = _place()
        own, land = refs
        hr = own.shape[0] // 2
        rows = pl.ds(c * hr, hr)
        cp = _remote(own.at[rows], land.at[rows], send.at[0], recv.at[0], (*chips[j], c))
        return [cp], [cp]
    return plan


def plan_shard_relay(refs, send, recv):
    _, _, c, chips = _place()
    land_x, land_y, land_far = refs
    hr = land_far.shape[0] // 2
    quarter = lambda q: pl.ds(c * hr + q * (hr // 2), hr // 2)
    mine = [_remote(land_y.at[quarter(0)], land_far.at[quarter(0)], send.at[0], recv.at[0], (*chips[0], c)),
            _remote(land_x.at[quarter(1)], land_far.at[quarter(1)], send.at[1], recv.at[1], (*chips[1], c))]
    return mine, mine


def plan_shard_pass(refs, send, recv):
    x, y, c, _ = _place()
    mine, theirs = [], []
    for w, land in enumerate(refs):
        hr = land.shape[0] // 2
        half = lambda core: land.at[pl.ds(core * hr, hr)]
        mine.append(_remote(half(c), half(c), send.at[w], recv.at[w], (x, y, 1 - c)))
        theirs.append(_remote(half(1 - c), half(1 - c), send.at[w], recv.at[w], (x, y, 1 - c)))
    return mine, theirs


def plan_gather_pass(refs, send, recv):
    x, y, c, chips = _place()
    mine, theirs = [], []
    for w, ref in enumerate(refs):
        hr = ref.shape[1] // 2
        for j, (cx, cy) in enumerate(chips):
            half = lambda core: ref.at[2 * cx + cy, pl.ds(core * hr, hr)]
            k = 3 * w + j
            mine.append(_remote(half(c), half(c), send.at[k], recv.at[k], (x, y, 1 - c)))
            theirs.append(_remote(half(1 - c), half(1 - c), send.at[k], recv.at[k], (x, y, 1 - c)))
    return mine, theirs


def plan_swap(refs, send, recv):
    x, y, c, _ = _place()
    nw = len(refs) // 2
    mine = []
    for w in range(nw):
        hr = refs[w].shape[1] // 2
        mine.append(_remote(refs[w].at[:, pl.ds((1 - c) * hr, hr), :], refs[nw + w], send.at[w], recv.at[w],
                            (x, y, 1 - c)))
    return mine, mine


def plan_scatter(refs, send, recv):
    x, y, c, chips = _place()
    me = 2 * x + y
    nw = len(refs) // 2
    mine, theirs = [], []
    for w in range(nw):
        for j, (cx, cy) in enumerate(chips):
            k = 3 * w + j
            mine.append(_remote(refs[w].at[2 * cx + cy], refs[nw + w].at[me], send.at[k], recv.at[k], (cx, cy, c)))
            theirs.append(_remote(refs[w].at[me], refs[nw + w].at[2 * cx + cy], send.at[k], recv.at[k], (cx, cy, c)))
    return mine, theirs


def plan_join(refs, send, recv):
    x, y, c, _ = _place()
    mine, theirs = [], []
    for w, ref in enumerate(refs):
        hr = ref.shape[0] // 2
        half = lambda core: ref.at[pl.ds(core * hr, hr)]
        mine.append(_remote(half(c), half(c), send.at[w], recv.at[w], (x, y, 1 - c)))
        theirs.append(_remote(half(1 - c), half(1 - c), send.at[w], recv.at[w], (x, y, 1 - c)))
    return mine, theirs


def exchange(name, plan, arrays, n, after=()):
    na = len(arrays)

    def body(*refs):
        mine, theirs = plan(refs[:na], refs[2 * na], refs[2 * na + 1])
        for cp in mine:
            cp.start()
        for cp in theirs:
            cp.wait_recv()
        for cp in mine:
            cp.wait_send()

    return _call_after(body, after, name=name, in_specs=[_hbm()] * na, out_specs=[_hbm()] * na,
                       out_shape=[jax.ShapeDtypeStruct(a.shape, a.dtype) for a in arrays],
                       input_output_aliases={i: i for i in range(na)},
                       scratch_shapes=[pltpu.SemaphoreType.DMA((n,)), pltpu.SemaphoreType.DMA((n,))])(*arrays)


def exchange_start(name, groups, arrays, after=()):
    na, ng = len(arrays), len(groups)

    def body(*refs):
        for g, (plan, idx, _) in enumerate(groups):
            mine, _ = plan([refs[i] for i in idx], refs[na + 2 * g], refs[na + 2 * g + 1])
            for cp in mine:
                cp.start()
        refs[-1][...] = jnp.zeros_like(refs[-1])

    hbm = pl.BlockSpec(memory_space=pltpu.HBM)
    sem = pl.BlockSpec(memory_space=pltpu.SEMAPHORE)
    sem_types = [pltpu.SemaphoreType.DMA((n,)) for _, _, n in groups for _ in (0, 1)]
    outs = _call_after(body, after, name=name, in_specs=[hbm] * na,
                       out_specs=[sem] * (2 * ng) + [hbm] * na + [pl.BlockSpec(memory_space=pltpu.VMEM)],
                       out_shape=sem_types + [pltpu.HBM(a.shape, a.dtype) for a in arrays]
                       + [jax.ShapeDtypeStruct((SUBLANES, LANES), F32)],
                       input_output_aliases={i: i + 2 * ng for i in range(na)},
                       compiler_params=pltpu.CompilerParams(
                           has_side_effects=pltpu.SideEffectType.DATAFLOW_SIDE_EFFECTING))(
                               *[pltpu.with_memory_space_constraint(a, pltpu.HBM) for a in arrays])
    sems = [(outs[2 * g], outs[2 * g + 1]) for g in range(ng)]
    return sems, list(outs[2 * ng:-1]), outs[-1]


def exchange_wait(name, plan, sems, arrays, after):
    send_sems, recv_sems = sems
    na = len(arrays)
    after = tuple(after) if isinstance(after, (tuple, list)) else (after,)

    def body(*refs):
        mine, theirs = plan(refs[:na], refs[na], refs[na + 1])
        for cp in mine:
            cp.wait_send()
        for cp in theirs:
            cp.wait_recv()

    hbm = pl.BlockSpec(memory_space=pltpu.HBM)
    sem = pl.BlockSpec(memory_space=pltpu.SEMAPHORE)
    return _call(body, name=name, in_specs=[hbm] * na + [sem, sem] + [_hbm()] * len(after),
                 out_specs=[hbm] * na, out_shape=[pltpu.HBM(a.shape, a.dtype) for a in arrays],
                 input_output_aliases={i: i for i in range(na)},
                 compiler_params=pltpu.CompilerParams(
                     has_side_effects=pltpu.SideEffectType.DATAFLOW_SIDE_EFFECTING))(
                         *arrays, send_sems, recv_sems, *after)


def plan_allgather_small(refs, send, recv):
    x, y, c, _ = _place()
    buf, = refs
    mine, theirs = [], []
    flips = [(fx, fy, fc) for fx in (0, 1) for fy in (0, 1) for fc in (0, 1)][1:]
    for k, (fx, fy, fc) in enumerate(flips):
        peer = (x ^ fx, y ^ fy, c ^ fc)
        slot = lambda px, py, pc: buf.at[4 * px + 2 * py + pc]
        mine.append(_remote(slot(x, y, c), slot(x, y, c), send.at[k], recv.at[k], peer))
        theirs.append(_remote(slot(*peer), slot(*peer), send.at[k], recv.at[k], peer))
    return mine, theirs


def add_sibling(grad, got, core, name):
    _, r, c = grad.shape
    hr = r // 2
    tr = _tile(hr, 512)
    nblk = hr // tr

    def body(core_ref, g_ref, o_ref, out_ref):
        out_ref[...] = (g_ref[...].astype(F32) + o_ref[...].astype(F32)).astype(BF16)

    grid_spec = pltpu.PrefetchScalarGridSpec(
        num_scalar_prefetch=1, grid=(N_CHIPS, nblk),
        in_specs=[pl.BlockSpec((None, tr, c), lambda t, i, core_ref: (t, core_ref[0] * nblk + i, 0)),
                  pl.BlockSpec((None, tr, c), lambda t, i, core_ref: (t, i, 0))],
        out_specs=pl.BlockSpec((None, tr, c), lambda t, i, core_ref: (t, i, 0)))
    return _call(body, name=name, grid_spec=grid_spec,
                 out_shape=jax.ShapeDtypeStruct((N_CHIPS, hr, c), BF16),
                 compiler_params=_params(("parallel", "parallel")))(core, grad, got)


def sum_chips(mine, owned, place, name):
    _, hr, c = mine.shape
    tr = _tile(hr, 512)
    nblk = hr // tr

    def body(place_ref, m_ref, o1_ref, o2_ref, o3_ref, out_ref):
        acc = m_ref[...].astype(F32)
        for o_ref in (o1_ref, o2_ref, o3_ref):
            acc = acc + o_ref[...].astype(F32)
        out_ref[...] = acc

    other = lambda k: pl.BlockSpec((None, tr, c), lambda i, place_ref: ((place_ref[0] + k) % N_CHIPS, i, 0))
    grid_spec = pltpu.PrefetchScalarGridSpec(
        num_scalar_prefetch=1, grid=(nblk,),
        in_specs=[other(0), other(1), other(2), other(3)],
        out_specs=pl.BlockSpec((tr, c), lambda i, place_ref: (place_ref[1] * nblk + i, 0)))
    return _call(body, name=name, grid_spec=grid_spec,
                 out_shape=jax.ShapeDtypeStruct((2 * hr, c), F32),
                 compiler_params=_params(("parallel",)))(place, mine, owned, owned, owned)


def small_step(gathered, chip, rows, params, conv_row, conv):
    n_dev, n_rows, _ = gathered.shape
    _, cr, cq = conv[0].shape
    groups = list(params) + [conv]
    n_par = len(groups)

    def body(chip_ref, g_ref, gc_ref, *refs):
        ins, outs, loss_ref = refs[:3 * n_par], refs[3 * n_par:7 * n_par], refs[7 * n_par]
        tot = g_ref[0]
        conv_g = gc_ref[0, conv_row:conv_row + cr, :]
        for dev in range(1, n_dev):
            tot = tot + g_ref[dev]
            conv_g = conv_g + gc_ref[dev, conv_row:conv_row + cr, :]
        loss_ref[...] = jnp.sum(tot[SUBLANES:SUBLANES + 1], axis=1, keepdims=True)
        grads = [tot[r:r + 1, :w] for r, w in rows] + [conv_g[None]]
        for i, gv in enumerate(grads):
            _adamw_step(gv, *ins[3 * i:3 * i + 3], *outs[4 * i:4 * i + 4])

    def whole(a):
        return pl.BlockSpec(a.shape, lambda i, chip_ref, n=len(a.shape): (0,) * n)

    flat = [a for group in groups for a in group]
    out_shape = [jax.ShapeDtypeStruct(group[0].shape, F32) for group in groups for _ in range(4)]
    out_shape.append(jax.ShapeDtypeStruct((1, 1), F32))
    grid_spec = pltpu.PrefetchScalarGridSpec(
        num_scalar_prefetch=1, grid=(1,),
        in_specs=[whole(gathered), pl.BlockSpec((n_dev, n_rows, cq), lambda i, chip_ref: (0, 0, chip_ref[0]))]
        + [whole(a) for a in flat],
        out_specs=[whole(o) for o in out_shape])
    res = _call(body, name="small_step", grid_spec=grid_spec, out_shape=out_shape,
                compiler_params=_params(("arbitrary",)))(chip, gathered, gathered, *flat)
    return res[-1], [res[4 * i:4 * i + 4] for i in range(n_par)]


def _rope_tables(s):
    half = ROT_DIM // 2
    inv_freq = jnp.power(jnp.float32(ROPE_THETA), -jnp.arange(half, dtype=F32) * 2.0 / ROT_DIM)
    freq64 = jnp.concatenate([inv_freq, inv_freq, jnp.zeros((HEAD_DIM - ROT_DIM,), F32)])
    freq = jnp.concatenate([freq64, freq64])[None, :]
    dim = (jnp.arange(LANES) % HEAD_DIM)[None, :]
    ang = jnp.arange(s).astype(F32)[:, None] * freq
    cos, sin = jnp.cos(ang), jnp.sin(ang)
    return cos, jnp.where(dim < half, -sin, 0.0), jnp.where((dim >= half) & (dim < ROT_DIM), sin, 0.0)


def _pad_rows(a, rows):
    return jnp.pad(a, ((0, rows - a.shape[0]), (0, 0)))


def _pad_cols(a, cols):
    return jnp.pad(a, ((0, 0), (0, cols - a.shape[1])))


def kernel(x, p, norm_gain, w_in, q_norm_gain, k_norm_gain, attn_sinks, conv_w, w_out, ple_gate_norm_gain, w_ple_gate, b_ple_gate, w_ple_proj, ple_norm_gain, loss_target, m_norm_gain, m_w_in, m_q_norm_gain, m_k_norm_gain, m_attn_sinks, m_conv_w, m_w_out, m_ple_gate_norm_gain, m_w_ple_gate, m_b_ple_gate, m_w_ple_proj, m_ple_norm_gain, v_norm_gain, v_w_in, v_q_norm_gain, v_k_norm_gain, v_attn_sinks, v_conv_w, v_w_out, v_ple_gate_norm_gain, v_w_ple_gate, v_b_ple_gate, v_w_ple_proj, v_ple_norm_gain):
    x2, p2, tgt = x[0], p[0, 0], loss_target[0]
    s, d = x2.shape
    aw = d // 2
    cw = d - aw
    nq = aw // HEAD_DIM
    sh = w_in.shape[2]
    in_w = N_CHIPS * sh
    dq = d // N_CHIPS
    cq = cw // N_CHIPS
    ga_off = (aw + 2 * KV_WIDTH) // COLT
    b_off = (2 * aw + 2 * KV_WIDTH) // COLT
    assert in_w == 2 * aw + 2 * KV_WIDTH + 4 * cw and aw % COLT == 0 and cw % COLT == 0
    assert s % BLOCK == 0 and sh % LANES == 0 and nq % (2 * N_KV_HEADS) == 0

    core = lax.axis_index("c").astype(jnp.int32).reshape(1)
    chip = 2 * lax.axis_index("x") + lax.axis_index("y")

    chip1 = chip.astype(jnp.int32).reshape(1)
    place = jnp.concatenate([chip1, core])
    w_in_own = cast_bf16(w_in[0], "cast_w_in")
    rest = [cast_into_slot(w_out[0], chip1, "cast_w_out"), cast_into_slot(w_ple_gate[0], chip1, "cast_w_pg"),
            cast_into_slot(w_ple_proj[0], chip1, "cast_w_pp"),
            lax.dynamic_update_slice(jnp.zeros((N_CHIPS, SUBLANES, cq), F32),
                                     _pad_rows(conv_w[0], SUBLANES)[None], (chip, 0, 0))]
    order = jnp.stack([chip, chip ^ 2, chip ^ 1, chip ^ 3]).astype(jnp.int32)
    wi_sems, wi_arrs, wi_token = exchange_start(
        "gather_wi_start", [(plan_shard_ici(j), [0, 1 + j], 1) for j in range(2)],
        [w_in_own] + [lax.empty((d, sh), BF16) for _ in range(3)])

    tn = _tile(d, 1024)
    ts = _tile(s, 2048)
    tabs = _rope_tables(s)
    qg = jnp.tile(q_norm_gain, (1, LANES // HEAD_DIM))
    kg = jnp.tile(k_norm_gain, (1, LANES // HEAD_DIM))
    seg_i = jnp.arange(SEG) // HEAD_DIM
    segb = (seg_i[:, None] == seg_i[None, :]).astype(BF16)
    h, z = norm_in_proj(x2, norm_gain, wi_arrs[0], order, in_w)
    own, land_x = exchange_wait("gather_wi_wait0", plan_shard_ici(0), wi_sems[0], [wi_arrs[0], wi_arrs[1]],
                                (z,) + tuple(tabs) + tuple(rest[:2]))
    own, land_y = exchange_wait("gather_wi_wait1", plan_shard_ici(1), wi_sems[1], [own, wi_arrs[2]], land_x)
    (relay_sems, rest_sems), started, rest_token = exchange_start(
        "gather_relay_rest_start", [(plan_shard_relay, [0, 1, 2], 2), (plan_gather_ici(3), [3, 4, 5, 6], 12)],
        [land_x, land_y, wi_arrs[3]] + rest)
    relayed, rest = started[:3], started[3:]
    land_x, = exchange("gather_wi_pass0", plan_shard_pass, relayed[:1], 1, after=(rest_token,))
    (y_sems,), (land_y,), y_token = exchange_start(
        "gather_wi_pass1_start", [(plan_shard_pass, [0], 1)], [relayed[1]], after=(land_x,))
    z = in_proj_part(h, land_x, z, order, 1, in_w, after=(y_token,))
    land_y, = exchange_wait("gather_wi_pass1_wait", plan_shard_pass, y_sems, [land_y], z)
    land_x, land_y, land_far = exchange_wait("gather_wi_relay_wait", plan_shard_relay, relay_sems,
                                             [land_x, land_y, relayed[2]], z)
    (far_sems,), (land_far,), far_token = exchange_start(
        "gather_wi_pass2_start", [(plan_shard_pass, [0], 1)], [land_far])
    z = in_proj_part(h, land_y, z, order, 2, in_w, after=(far_token,))
    land_far, = exchange_wait("gather_wi_pass2_wait", plan_shard_pass, far_sems, [land_far], z)
    z = in_proj_part(h, land_far, z, order, 3, in_w)
    w_shards = [own, land_x, land_y, land_far]
    wo_all, wg_all, wp_all, conv_all = exchange_wait("gather_rest_wait", plan_gather_ici(3), rest_sems, rest, z)
    pass_sems, passed, pass_token = exchange_start(
        "gather_rest_pass_start", [(plan_gather_pass, [0, 1, 2], 9)], [wo_all, wg_all, wp_all])
    conv_full = conv_all.transpose(1, 0, 2).reshape(SUBLANES, cw)
    qs, ks, vb = qk_prep_fwd(z, tabs, qg, kg, segb, aw, after=(pass_token,))
    attn, mix = attn_fwd(qs, ks, vb, z, attn_sinks, aw, d, ga_off)
    mix = conv_fwd(z, conv_full, mix, aw, cw, b_off)
    wo_all, wg_all, wp_all = exchange_wait("gather_rest_pass_wait", plan_gather_pass, pass_sems[0], passed, mix)
    wo_full = wo_all.reshape(d, d)
    wg_full = wg_all.reshape(d, d)
    x1, hg = out_proj_norm(mix, wo_full, x2, ple_gate_norm_gain)

    d_gl, dy, g_wp, acc_head = head_fwd_bwd(x1, hg, wg_full, p2, wp_all, tgt, b_ple_gate, ple_norm_gain)
    wgrad = lambda name, a, b: matmul(
        a, b, grid=(d // tn, d // tn, s // ts),
        a_spec=pl.BlockSpec((ts, tn), lambda i, j, k: (k, i)),
        b_spec=pl.BlockSpec((ts, tn), lambda i, j, k: (k, j)),
        o_spec=pl.BlockSpec((tn, tn), lambda i, j, k: (i, j)),
        out_shape=jax.ShapeDtypeStruct((d, d), BF16), dims=TN, name=name)
    g_wg = wgrad("mm_g_wg", hg, d_gl)
    d_x1, d_x1b, acc_g = gate_proj_bwd_norm(d_gl, wg_full, x1, dy, ple_gate_norm_gain)
    g_wo = wgrad("mm_g_wo", mix, d_x1b)
    def reduce_sum(tag, handle, after):
        sems, arrays, _ = handle
        n = len(arrays) // 2
        got = exchange_wait("scatter_%s_wait" % tag, plan_scatter, sems[0], arrays, after)
        return [sum_chips(pt, o, place, "sum_chips_%s%d" % (tag, i))
                for i, (pt, o) in enumerate(zip(got[:n], got[n:]))]

    early_grads = [g_wo.reshape(N_CHIPS, dq, d), g_wg.reshape(N_CHIPS, dq, d), g_wp]
    eswap_sems, eswapping, eswap_token = exchange_start(
        "swap_early_start", [(plan_swap, list(range(6)), 3)],
        early_grads + [lax.empty((N_CHIPS, a.shape[1] // 2, a.shape[2]), BF16) for a in early_grads])
    d_o, d_gate, d_mix_conv = out_proj_bwd_gate(d_x1b, wo_full, attn, z, aw, ga_off, after=(eswap_token,))
    eswapped = exchange_wait("swap_early_wait", plan_swap, eswap_sems[0], eswapping, d_o)
    early_parts = [add_sibling(g, o, core, "add_sibling_early%d" % i)
                   for i, (g, o) in enumerate(zip(eswapped[:3], eswapped[3:]))]
    early = exchange_start("scatter_early_start", [(plan_scatter, list(range(6)), 9)],
                           early_parts + [lax.empty(a.shape, BF16) for a in early_parts])
    dqs, dks, dvs, acc_sink = attn_bwd(qs, ks, vb, d_o, attn_sinks, aw, after=(early[-1],))
    dz, acc_qk = qk_prep_bwd(z, dqs, dks, dvs, d_gate, tabs, qg, kg, segb, aw)
    dz, acc_conv = conv_bwd(z, d_mix_conv, conv_full, dz, aw, cw, b_off)
    join_sems, joining, join_token = exchange_start(
        "join_early_start", [(plan_join, [0, 1, 2], 3)], reduce_sum("early", early, dz))
    g_send = in_proj_wgrad_half(h, dz, None, 1 - core, after=(join_token,))
    swap_sems, swapping, swap_token = exchange_start(
        "swap_late_start", [(plan_swap, [0, 1], 1)], [g_send, lax.empty((N_CHIPS, d // 2, sh), BF16)])
    g_wi = in_proj_wgrad_half(h, dz, swapping[0], core, after=(swap_token,))
    full_wo, full_wg, full_wp = exchange_wait("join_early_wait", plan_join, join_sems[0], joining, g_wi)
    g_wi, got_wi = exchange_wait("swap_late_wait", plan_swap, swap_sems[0], [g_wi, swapping[1]], full_wo)
    part_wi = add_sibling(g_wi, got_wi, core, "add_sibling_late0")
    late = exchange_start("scatter_late_start", [(plan_scatter, [0, 1], 3)],
                          [part_wi, lax.empty(part_wi.shape, BF16)])
    grad_x, acc_x = in_proj_bwd_norm(dz, w_shards, order, x2, d_x1, norm_gain, after=(late[-1],))

    wsm = max(d, cw)
    small = _pad_rows(jnp.concatenate([
        _pad_cols(acc_x[0:1], wsm), _pad_cols(acc_g[0:1], wsm), _pad_cols(acc_head[0:1], wsm),
        _pad_cols(acc_head[1:2], wsm), _pad_cols(acc_qk[0:1, :HEAD_DIM], wsm), _pad_cols(acc_conv[0:3], wsm),
        _pad_cols(acc_head[2:3], wsm),
        _pad_cols(acc_qk[1:2, :HEAD_DIM], wsm), _pad_cols(acc_sink[0:1, :nq], wsm)], axis=0), 2 * SUBLANES)
    device = 2 * chip + lax.axis_index("c")
    (small_sems, last_sems), started, last_token = exchange_start(
        "small_join_late_start", [(plan_allgather_small, [0], 7), (plan_join, [1], 1)],
        [lax.dynamic_update_slice(jnp.zeros((8,) + small.shape, F32), small[None], (device, 0, 0))]
        + reduce_sum("late", late, grad_x))
    small_bufs, last_halves = started[:1], started[1:]
    big, after = {}, (last_token,)
    for nm, g, w, m, v in (("w_out", full_wo, w_out, m_w_out, v_w_out),
                           ("w_ple_gate", full_wg, w_ple_gate, m_w_ple_gate, v_w_ple_gate),
                           ("w_ple_proj", full_wp, w_ple_proj, m_w_ple_proj, v_w_ple_proj)):
        stepped = adamw(g, w[0], m[0], v[0], "adamw_" + nm, after=after)
        big[nm], after = [o[None] for o in stepped], (stepped[1],)
    full_wi, = exchange_wait("join_late_wait", plan_join, last_sems, last_halves, after[0])
    big["w_in"] = [o[None] for o in adamw(full_wi, w_in[0], m_w_in[0], v_w_in[0], "adamw_w_in")]

    gathered, = exchange_wait("small_wait", plan_allgather_small, small_sems, small_bufs, big["w_in"][1])
    rowwise = (("norm_gain", (0, d), (norm_gain, m_norm_gain, v_norm_gain)),
               ("ple_gate_norm_gain", (1, d), (ple_gate_norm_gain, m_ple_gate_norm_gain, v_ple_gate_norm_gain)),
               ("b_ple_gate", (2, d), (b_ple_gate, m_b_ple_gate, v_b_ple_gate)),
               ("ple_norm_gain", (3, d), (ple_norm_gain, m_ple_norm_gain, v_ple_norm_gain)),
               ("q_norm_gain", (4, HEAD_DIM), (q_norm_gain, m_q_norm_gain, v_q_norm_gain)),
               ("k_norm_gain", (SUBLANES + 1, HEAD_DIM), (k_norm_gain, m_k_norm_gain, v_k_norm_gain)),
               ("attn_sinks", (SUBLANES + 2, nq), (attn_sinks, m_attn_sinks, v_attn_sinks)))
    loss, stepped = small_step(gathered, chip1, [r for _, r, _ in rowwise], [t for _, _, t in rowwise],
                               5, (conv_w, m_conv_w, v_conv_w))
    table = {nm: stepped[i] for i, (nm, _, _) in enumerate(rowwise)}
    table["conv_w"] = stepped[-1]

    order = ("norm_gain", "w_in", "q_norm_gain", "k_norm_gain", "attn_sinks", "conv_w", "w_out",
             "ple_gate_norm_gain", "w_ple_gate", "b_ple_gate", "w_ple_proj", "ple_norm_gain")
    outs = [loss.reshape(()), grad_x[None]]
    for kind in range(4):
        for nm in order:
            outs.append(big[nm][kind] if nm in big else table[nm][kind])
    return tuple(outs)
```

```python
import functools

import jax
import jax.numpy as jnp
from jax import lax
from jax.experimental import pallas as pl
from jax.experimental.pallas import tpu as pltpu

F32 = jnp.float32
BF16 = jnp.bfloat16
MESH = pl.DeviceIdType.MESH

HEAD_DIM = 64
N_KV_HEADS = 4
KV_WIDTH = N_KV_HEADS * HEAD_DIM
BLOCK = 128
ROT_DIM = 16
ROPE_THETA = 500000.0
EPS = 1e-6
NEG_INF = -1e30
N_CHIPS = 4
LANES = 128
SUBLANES = 8
COLT = 512
SEG = 256
VMEM_LIMIT = 48 * 1024 * 1024
VMEM_LIMIT_BIG = 60 * 1024 * 1024

ADAM_LR = 0.001
ADAM_B1 = 0.9
ADAM_B2 = 0.999
ADAM_EPS = 1e-08
ADAM_WD = 0.01
ADAM_STEP = 10

NN = (((1,), (0,)), ((), ()))
NT = (((1,), (1,)), ((), ()))
TN = (((0,), (0,)), ((), ()))


def _call(body, **kw):
    return pl.pallas_call(body, **kw)


def _call_after(body, after, **kw):
    n_in, n_after = len(kw["in_specs"]), len(after)
    kw["in_specs"] = list(kw["in_specs"]) + [pl.BlockSpec(memory_space=pl.ANY)] * n_after

    def body_after(*refs):
        body(*refs[:n_in], *refs[n_in + n_after:])

    call = _call(body_after, **kw)
    return lambda *args: call(*args, *after)


def _params(sem, vmem=VMEM_LIMIT):
    return pltpu.CompilerParams(dimension_semantics=sem, vmem_limit_bytes=vmem)


def _tile(n, pref):
    return pref if n % pref == 0 else n


def _sigmoid(v):
    return 1.0 / (1.0 + jnp.exp(-v))


def _hbm():
    return pl.BlockSpec(memory_space=pl.ANY)


def cast_bf16(a, name):
    r, c = a.shape
    tr = _tile(r, 512)

    def body(a_ref, o_ref):
        o_ref[...] = a_ref[...].astype(BF16)

    return _call(body, name=name, grid=(r // tr,),
                 in_specs=[pl.BlockSpec((tr, c), lambda i: (i, 0))],
                 out_specs=pl.BlockSpec((tr, c), lambda i: (i, 0)),
                 out_shape=jax.ShapeDtypeStruct((r, c), BF16),
                 compiler_params=_params(("parallel",)))(a)


def cast_into_slot(a, chip, name):
    r, c = a.shape
    tr = _tile(r, 512)

    def body(chip_ref, a_ref, o_ref):
        o_ref[...] = a_ref[...].astype(BF16)

    grid_spec = pltpu.PrefetchScalarGridSpec(
        num_scalar_prefetch=1, grid=(r // tr,),
        in_specs=[pl.BlockSpec((tr, c), lambda i, chip_ref: (i, 0))],
        out_specs=pl.BlockSpec((None, tr, c), lambda i, chip_ref: (chip_ref[0], i, 0)))
    return _call(body, name=name, grid_spec=grid_spec,
                 out_shape=jax.ShapeDtypeStruct((N_CHIPS, r, c), BF16),
                 compiler_params=_params(("parallel",)))(chip, a)


def out_proj_norm(mix, wo, x, gain):
    s, d = x.shape
    tm = _tile(s, 512)

    def body(m_ref, w_ref, x_ref, g_ref, x1_ref, hg_ref):
        x1 = x_ref[...] + jnp.dot(m_ref[...], w_ref[...], preferred_element_type=F32)
        x1_ref[...] = x1
        r = lax.rsqrt(jnp.mean(x1 * x1, axis=-1, keepdims=True) + EPS)
        hg_ref[...] = ((x1 * r) * g_ref[...]).astype(BF16)

    row = pl.BlockSpec((tm, d), lambda i: (i, 0))
    return _call(body, name="out_proj_norm", grid=(s // tm,),
                 in_specs=[row, pl.BlockSpec((d, d), lambda i: (0, 0), pipeline_mode=pl.Buffered(1)), row,
                           pl.BlockSpec((1, d), lambda i: (0, 0))],
                 out_specs=[row, row],
                 out_shape=[jax.ShapeDtypeStruct((s, d), F32), jax.ShapeDtypeStruct((s, d), BF16)],
                 compiler_params=_params(("parallel",), VMEM_LIMIT_BIG))(mix, wo, x, gain)


def gate_proj_bwd_norm(d_gl, wg, xin, add, gain):
    s, d = xin.shape
    tm = _tile(s, 256)

    def body(dg_ref, w_ref, x_ref, a_ref, g_ref, dx_ref, dxb_ref, acc_ref):
        i = pl.program_id(0)
        dyv = lax.dot_general(dg_ref[...], w_ref[...], NT, preferred_element_type=F32)
        xf = x_ref[...]
        r = lax.rsqrt(jnp.mean(xf * xf, axis=-1, keepdims=True) + EPS)
        xhat = xf * r
        gd = dyv * g_ref[...]
        dx = a_ref[...] + r * (gd - xhat * jnp.mean(xhat * gd, axis=-1, keepdims=True))
        dx_ref[...] = dx
        dxb_ref[...] = dx.astype(BF16)

        @pl.when(i == 0)
        def _():
            acc_ref[...] = jnp.zeros_like(acc_ref)

        acc_ref[0:1, :] += jnp.sum(dyv * xhat, axis=0, keepdims=True)

    row = pl.BlockSpec((tm, d), lambda i: (i, 0))
    return _call(body, name="gate_proj_bwd_norm", grid=(s // tm,),
                 in_specs=[row, pl.BlockSpec((d, d), lambda i: (0, 0), pipeline_mode=pl.Buffered(1)), row, row,
                           pl.BlockSpec((1, d), lambda i: (0, 0))],
                 out_specs=[row, row, pl.BlockSpec((SUBLANES, d), lambda i: (0, 0))],
                 out_shape=[jax.ShapeDtypeStruct((s, d), F32), jax.ShapeDtypeStruct((s, d), BF16),
                            jax.ShapeDtypeStruct((SUBLANES, d), F32)],
                 compiler_params=_params(("arbitrary",), VMEM_LIMIT_BIG))(d_gl, wg, xin, add, gain)


def head_fwd_bwd(x1, hg, wg, p, wp, tgt, bias, ple_gain):
    s, d = x1.shape
    tm = _tile(s, 256)
    n_row = s // tm

    def body(x1_ref, hg_ref, wg_ref, p_ref, wp_ref, t_ref, b_ref, g_ref, dgl_ref, dy_ref, gwp_ref, acc_ref,
             gacc_ref):
        i = pl.program_id(0)
        gl = jnp.dot(hg_ref[...], wg_ref[...], preferred_element_type=F32)
        pb = p_ref[...].astype(BF16)
        pev = jnp.concatenate([jnp.dot(pb, wp_ref[q], preferred_element_type=F32) for q in range(N_CHIPS)],
                              axis=1)
        r = lax.rsqrt(jnp.mean(pev * pev, axis=-1, keepdims=True) + EPS)
        pehat = pev * r
        gain = g_ref[...]
        e = pehat * gain
        gate = _sigmoid(gl + b_ref[...])
        diff = (x1_ref[...] + gate * e) - t_ref[...]
        dy = diff * (1.0 / d)
        dy_ref[...] = dy
        d_gl = (dy * e) * (gate * (1.0 - gate))
        dgl_ref[...] = d_gl.astype(BF16)
        d_e = dy * gate
        gd = d_e * gain
        d_pe = r * (gd - pehat * jnp.mean(pehat * gd, axis=-1, keepdims=True))
        g_part = lax.dot_general(pb, d_pe.astype(BF16), TN, preferred_element_type=F32)

        @pl.when(i == 0)
        def _():
            acc_ref[...] = jnp.zeros_like(acc_ref)
            gacc_ref[...] = g_part

        @pl.when(i > 0)
        def _():
            gacc_ref[...] += g_part

        @pl.when(i == n_row - 1)
        def _():
            pq = d // N_CHIPS
            for q in range(N_CHIPS):
                gwp_ref[q] = gacc_ref[:, q * pq:(q + 1) * pq].astype(BF16)

        acc_ref[0:1, :] += jnp.sum(d_gl, axis=0, keepdims=True)
        acc_ref[1:2, :] += jnp.sum(d_e * pehat, axis=0, keepdims=True)
        acc_ref[2:3, :] += jnp.sum(diff * diff, axis=0, keepdims=True) * (0.5 / d)

    row = pl.BlockSpec((tm, d), lambda i: (i, 0))
    vec = pl.BlockSpec((1, d), lambda i: (0, 0))
    ple = p.shape[1]
    return _call(body, name="head_fwd_bwd", grid=(n_row,),
                 in_specs=[row, row, pl.BlockSpec((d, d), lambda i: (0, 0), pipeline_mode=pl.Buffered(1)),
                           pl.BlockSpec((tm, ple), lambda i: (i, 0)),
                           pl.BlockSpec(wp.shape, lambda i: (0, 0, 0)), row, vec, vec],
                 out_specs=[row, row, pl.BlockSpec(wp.shape, lambda i: (0, 0, 0)),
                            pl.BlockSpec((SUBLANES, d), lambda i: (0, 0))],
                 out_shape=[jax.ShapeDtypeStruct((s, d), BF16), jax.ShapeDtypeStruct((s, d), F32),
                            jax.ShapeDtypeStruct(wp.shape, BF16), jax.ShapeDtypeStruct((SUBLANES, d), F32)],
                 scratch_shapes=[pltpu.VMEM((ple, d), F32)],
                 compiler_params=_params(("arbitrary",), VMEM_LIMIT_BIG))(
                     x1, hg, wg, p, wp, tgt, bias, ple_gain)


def _adamw_step(gv, w_ref, m_ref, v_ref, go_ref, d_ref, mo_ref, vo_ref):
    mn = ADAM_B1 * m_ref[...] + (1.0 - ADAM_B1) * gv
    vn = ADAM_B2 * v_ref[...] + (1.0 - ADAM_B2) * (gv * gv)
    m_hat = mn / (1.0 - ADAM_B1 ** ADAM_STEP)
    v_hat = vn / (1.0 - ADAM_B2 ** ADAM_STEP)
    go_ref[...] = gv
    d_ref[...] = -ADAM_LR * (m_hat / (jnp.sqrt(v_hat) + ADAM_EPS) + ADAM_WD * w_ref[...])
    mo_ref[...] = mn
    vo_ref[...] = vn


def adamw(g, w, m, v, name, after=()):
    r, c = g.shape
    tr = _tile(r, 256)

    def body(g_ref, w_ref, m_ref, v_ref, go_ref, d_ref, mo_ref, vo_ref):
        _adamw_step(g_ref[...], w_ref, m_ref, v_ref, go_ref, d_ref, mo_ref, vo_ref)

    blk = pl.BlockSpec((tr, c), lambda i: (i, 0))
    shp = jax.ShapeDtypeStruct((r, c), F32)
    return _call_after(body, after, name=name, grid=(r // tr,), in_specs=[blk] * 4, out_specs=[blk] * 4,
                       out_shape=[shp] * 4, compiler_params=_params(("parallel",)))(g, w, m, v)


def matmul(a, b, *, grid, a_spec, b_spec, o_spec, out_shape, dims, name):
    nk = grid[2]
    acc_shape = tuple(d for d in o_spec.block_shape if d is not None)

    def body(a_ref, b_ref, o_ref, *scratch):
        part = lax.dot_general(a_ref[...].astype(BF16), b_ref[...].astype(BF16), dims, preferred_element_type=F32)
        if nk == 1:
            o_ref[...] = part.astype(o_ref.dtype)
            return
        acc_ref, = scratch
        k = pl.program_id(2)

        @pl.when(k == 0)
        def _():
            acc_ref[...] = part

        @pl.when((k > 0) & (k < nk - 1))
        def _():
            acc_ref[...] += part

        @pl.when(k == nk - 1)
        def _():
            o_ref[...] = (acc_ref[...] + part).astype(o_ref.dtype)

    return _call(body, name=name, grid=grid, in_specs=[a_spec, b_spec], out_specs=o_spec, out_shape=out_shape,
                 scratch_shapes=[pltpu.VMEM(acc_shape, F32)] if nk > 1 else [],
                 compiler_params=_params(("parallel", "parallel", "arbitrary")))(a, b)


def norm_in_proj(x, gain, w, order, in_w):
    s, d = x.shape
    sh = w.shape[1]
    tm = _tile(s, 512)

    def body(order_ref, x_ref, g_ref, w_ref, h_ref, z_ref):
        xf = x_ref[...]
        r = lax.rsqrt(jnp.mean(xf * xf, axis=-1, keepdims=True) + EPS)
        hb = ((xf * r) * g_ref[...]).astype(BF16)
        h_ref[...] = hb
        z_ref[...] = jnp.dot(hb, w_ref[...], preferred_element_type=F32)

    grid_spec = pltpu.PrefetchScalarGridSpec(
        num_scalar_prefetch=1, grid=(s // tm,),
        in_specs=[pl.BlockSpec((tm, d), lambda i, order_ref: (i, 0)),
                  pl.BlockSpec((1, d), lambda i, order_ref: (0, 0)),
                  pl.BlockSpec((d, sh), lambda i, order_ref: (0, 0), pipeline_mode=pl.Buffered(1))],
        out_specs=[pl.BlockSpec((tm, d), lambda i, order_ref: (i, 0)),
                   pl.BlockSpec((tm, sh), lambda i, order_ref: (i, order_ref[0]))])
    return _call(body, name="norm_in_proj", grid_spec=grid_spec,
                 out_shape=[jax.ShapeDtypeStruct((s, d), BF16), jax.ShapeDtypeStruct((s, in_w), F32)],
                 compiler_params=_params(("parallel",)))(order, x, gain, w)


def in_proj_part(h, w, z_prev, order, q, in_w, after=()):
    s, d = h.shape
    sh = w.shape[1]
    tm = _tile(s, 512)

    def body(order_ref, h_ref, w_ref, *rest):
        rest[-1][...] = jnp.dot(h_ref[...], w_ref[...], preferred_element_type=F32)

    in_specs = [pl.BlockSpec((tm, d), lambda i, order_ref: (i, 0)),
                pl.BlockSpec((d, sh), lambda i, order_ref: (0, 0))]
    args = [order, h, w]
    if z_prev is not None:
        in_specs.append(_hbm())
        args.append(z_prev)
    in_specs += [_hbm()] * len(after)
    args += list(after)
    grid_spec = pltpu.PrefetchScalarGridSpec(
        num_scalar_prefetch=1, grid=(s // tm,), in_specs=in_specs,
        out_specs=pl.BlockSpec((tm, sh), lambda i, order_ref: (i, order_ref[q])))
    return _call(body, name="mm_z%d" % q, grid_spec=grid_spec,
                 out_shape=jax.ShapeDtypeStruct((s, in_w), F32),
                 input_output_aliases={3: 0} if z_prev is not None else {},
                 compiler_params=_params(("parallel",)))(*args)


def in_proj_wgrad_half(h, dz, g_prev, half, after):
    s, d = h.shape
    sh = dz.shape[1] // N_CHIPS
    hr = d // 2

    def body(half_ref, h_ref, dz_ref, *rest):
        rest[-1][...] = lax.dot_general(h_ref[...], dz_ref[...], TN, preferred_element_type=F32).astype(BF16)

    extra = ([g_prev] if g_prev is not None else []) + list(after)
    grid_spec = pltpu.PrefetchScalarGridSpec(
        num_scalar_prefetch=1, grid=(N_CHIPS,),
        in_specs=[pl.BlockSpec((s, hr), lambda j, half_ref: (0, half_ref[0]), pipeline_mode=pl.Buffered(1)),
                  pl.BlockSpec((s, sh), lambda j, half_ref: (0, j))] + [_hbm()] * len(extra),
        out_specs=pl.BlockSpec((None, hr, sh), lambda j, half_ref: (j, half_ref[0], 0)))
    return _call(body, name="mm_g_wi_%s" % ("keep" if g_prev is not None else "send"), grid_spec=grid_spec,
                 out_shape=jax.ShapeDtypeStruct((N_CHIPS, d, sh), BF16),
                 input_output_aliases={3: 0} if g_prev is not None else {},
                 compiler_params=_params(("parallel",), VMEM_LIMIT_BIG))(half, h, dz, *extra)


def in_proj_bwd_norm(dz, ws, order, xin, add, gain, after):
    s, d = xin.shape
    sh = ws[0].shape[1]
    tm = _tile(s, 256)
    nq, n_after = len(ws), len(after)

    def body(order_ref, *refs):
        a_refs, b_refs = refs[:nq], refs[nq:2 * nq]
        x_ref, add_ref, g_ref = refs[2 * nq:2 * nq + 3]
        dx_ref, acc_ref = refs[2 * nq + 3 + n_after:]
        i = pl.program_id(0)
        dyv = lax.dot_general(a_refs[0][...], b_refs[0][...], NT, preferred_element_type=F32)
        for q in range(1, nq):
            dyv += lax.dot_general(a_refs[q][...], b_refs[q][...], NT, preferred_element_type=F32)
        xf = x_ref[...]
        r = lax.rsqrt(jnp.mean(xf * xf, axis=-1, keepdims=True) + EPS)
        xhat = xf * r
        gd = dyv * g_ref[...]
        dx_ref[...] = add_ref[...] + r * (gd - xhat * jnp.mean(xhat * gd, axis=-1, keepdims=True))

        @pl.when(i == 0)
        def _():
            acc_ref[...] = jnp.zeros_like(acc_ref)

        acc_ref[0:1, :] += jnp.sum(dyv * xhat, axis=0, keepdims=True)

    a_spec = lambda q: pl.BlockSpec((tm, sh), functools.partial(lambda i, order_ref, q: (i, order_ref[q]), q=q))
    row = pl.BlockSpec((tm, d), lambda i, order_ref: (i, 0))
    grid_spec = pltpu.PrefetchScalarGridSpec(
        num_scalar_prefetch=1, grid=(s // tm,),
        in_specs=[a_spec(q) for q in range(nq)]
        + [pl.BlockSpec((d, sh), lambda i, order_ref: (0, 0), pipeline_mode=pl.Buffered(1))] * nq
        + [row, row, pl.BlockSpec((1, d), lambda i, order_ref: (0, 0))] + [_hbm()] * n_after,
        out_specs=[row, pl.BlockSpec((SUBLANES, d), lambda i, order_ref: (0, 0))])
    return _call(body, name="in_proj_bwd_norm", grid_spec=grid_spec,
                 out_shape=[jax.ShapeDtypeStruct((s, d), F32), jax.ShapeDtypeStruct((SUBLANES, d), F32)],
                 compiler_params=_params(("arbitrary",), VMEM_LIMIT_BIG))(
                     order, *([dz] * nq), *ws, xin, add, gain, *after)


def _seg_mean(sq, segb):
    parts = []
    for cgrp in range(sq.shape[1] // SEG):
        blk = sq[:, cgrp * SEG:(cgrp + 1) * SEG]
        hi = blk.astype(BF16)
        r1 = blk - hi.astype(F32)
        mid = r1.astype(BF16)
        lo = (r1 - mid.astype(F32)).astype(BF16)
        acc = jnp.dot(hi, segb, preferred_element_type=F32)
        acc += jnp.dot(mid, segb, preferred_element_type=F32)
        acc += jnp.dot(lo, segb, preferred_element_type=F32)
        parts.append(acc)
    out = parts[0] if len(parts) == 1 else jnp.concatenate(parts, axis=1)
    return out * (1.0 / HEAD_DIM)


def _rope(v, cos, sa, sb):
    parts = []
    for cgrp in range(v.shape[1] // LANES):
        blk = v[:, cgrp * LANES:(cgrp + 1) * LANES]
        parts.append(blk * cos + pltpu.roll(blk, LANES - 8, 1) * sa + pltpu.roll(blk, 8, 1) * sb)
    return parts[0] if len(parts) == 1 else jnp.concatenate(parts, axis=1)


def _rope_t(dv, cos, sa, sb):
    parts = []
    for cgrp in range(dv.shape[1] // LANES):
        blk = dv[:, cgrp * LANES:(cgrp + 1) * LANES]
        parts.append(blk * cos + pltpu.roll(blk * sa, 8, 1) + pltpu.roll(blk * sb, LANES - 8, 1))
    return parts[0] if len(parts) == 1 else jnp.concatenate(parts, axis=1)


def _tile_lanes(vec, width):
    reps = width // LANES
    return vec if reps == 1 else jnp.tile(vec, (1, reps))


def qk_prep_fwd(z, tabs, qg, kg, segb, aw, after=()):
    s = z.shape[0]
    tm = _tile(s, 512)
    wq = aw + 2 * KV_WIDTH

    def body(z_ref, cos_ref, sa_ref, sb_ref, qg_ref, kg_ref, seg_ref, qs_ref, ks_ref, vb_ref):
        zz = z_ref[...]
        q, k, v = zz[:, :aw], zz[:, aw:aw + KV_WIDTH], zz[:, aw + KV_WIDTH:]
        cos, sa, sb, segm = cos_ref[...], sa_ref[...], sb_ref[...], seg_ref[...]
        rq = lax.rsqrt(_seg_mean(q * q, segm) + EPS)
        qn = (q * rq) * _tile_lanes(qg_ref[...], aw)
        qs_ref[...] = (_rope(qn, cos, sa, sb) * (HEAD_DIM ** -0.5)).astype(BF16)
        rk = lax.rsqrt(_seg_mean(k * k, segm) + EPS)
        kn = (k * rk) * _tile_lanes(kg_ref[...], KV_WIDTH)
        ks_ref[...] = _rope(kn, cos, sa, sb).astype(BF16)
        vb_ref[...] = v.astype(BF16)

    tab = pl.BlockSpec((tm, LANES), lambda i: (i, 0))
    vec = pl.BlockSpec((1, LANES), lambda i: (0, 0))
    return _call_after(body, after, name="qk_prep_fwd", grid=(s // tm,),
                       in_specs=[pl.BlockSpec((tm, wq), lambda i: (i, 0)), tab, tab, tab, vec, vec,
                                 pl.BlockSpec((SEG, SEG), lambda i: (0, 0))],
                       out_specs=[pl.BlockSpec((tm, aw), lambda i: (i, 0)),
                                  pl.BlockSpec((tm, KV_WIDTH), lambda i: (i, 0)),
                                  pl.BlockSpec((tm, KV_WIDTH), lambda i: (i, 0))],
                       out_shape=[jax.ShapeDtypeStruct((s, aw), BF16), jax.ShapeDtypeStruct((s, KV_WIDTH), BF16),
                                  jax.ShapeDtypeStruct((s, KV_WIDTH), BF16)],
                       compiler_params=_params(("parallel",)))(z, *tabs, qg, kg, segb)


def qk_prep_bwd(z, dqs, dks, dvs, d_gate, tabs, qg, kg, segb, aw):
    s, in_w = z.shape
    tm = _tile(s, 512)
    wq = aw + 2 * KV_WIDTH

    def body(z_ref, dq_ref, dk_ref, dv_ref, dga_ref, cos_ref, sa_ref, sb_ref, qg_ref, kg_ref, seg_ref,
             dz_ref, acc_ref):
        i = pl.program_id(0)
        zz = z_ref[...]
        q, k = zz[:, :aw], zz[:, aw:aw + KV_WIDTH]
        cos, sa, sb, segm = cos_ref[...], sa_ref[...], sb_ref[...], seg_ref[...]

        def one(xv, dout, gvec, width):
            g = _tile_lanes(gvec, width)
            r = lax.rsqrt(_seg_mean(xv * xv, segm) + EPS)
            xhat = xv * r
            dn = _rope_t(dout, cos, sa, sb)
            gd = dn * g
            dx = r * (gd - xhat * _seg_mean(xhat * gd, segm))
            contrib = jnp.sum(dn * xhat, axis=0, keepdims=True)
            folded = contrib[:, :LANES]
            for cgrp in range(1, width // LANES):
                folded = folded + contrib[:, cgrp * LANES:(cgrp + 1) * LANES]
            return dx, folded + pltpu.roll(folded, HEAD_DIM, 1)

        dq, gq = one(q, dq_ref[...] * (HEAD_DIM ** -0.5), qg_ref[...], aw)
        dk, gk = one(k, dk_ref[...], kg_ref[...], KV_WIDTH)
        dz_ref[:, :aw] = dq.astype(BF16)
        dz_ref[:, aw:aw + KV_WIDTH] = dk.astype(BF16)
        dz_ref[:, aw + KV_WIDTH:wq] = dv_ref[...].astype(BF16)
        dz_ref[:, wq:] = dga_ref[...]

        @pl.when(i == 0)
        def _():
            acc_ref[...] = jnp.zeros_like(acc_ref)

        acc_ref[0:1, :] += gq
        acc_ref[1:2, :] += gk

    tab = pl.BlockSpec((tm, LANES), lambda i: (i, 0))
    vec = pl.BlockSpec((1, LANES), lambda i: (0, 0))
    kvb = pl.BlockSpec((tm, KV_WIDTH), lambda i: (i, 0))
    awb = pl.BlockSpec((tm, aw), lambda i: (i, 0))
    return _call(body, name="qk_prep_bwd", grid=(s // tm,),
                 in_specs=[pl.BlockSpec((tm, wq), lambda i: (i, 0)), awb, kvb, kvb, awb, tab, tab, tab, vec, vec,
                           pl.BlockSpec((SEG, SEG), lambda i: (0, 0))],
                 out_specs=[pl.BlockSpec((tm, wq + aw), lambda i: (i, 0)),
                            pl.BlockSpec((SUBLANES, LANES), lambda i: (0, 0))],
                 out_shape=[jax.ShapeDtypeStruct((s, in_w), BF16), jax.ShapeDtypeStruct((SUBLANES, LANES), F32)],
                 compiler_params=_params(("arbitrary",)))(z, dqs, dks, dvs, d_gate, *tabs, qg, kg, segb)


def _placed(band, lane_idx):
    out = {}
    for kh in range(N_KV_HEADS):
        grp = band[:, (kh // 2) * LANES:(kh // 2 + 1) * LANES]
        for half in (0, 1):
            t = grp if half == kh % 2 else pltpu.roll(grp, HEAD_DIM, 1)
            keep = (lane_idx >= half * HEAD_DIM) & (lane_idx < (half + 1) * HEAD_DIM)
            out[kh, half] = jnp.where(keep, t, jnp.zeros_like(t))
    return out


def _softmax_with_sink(sc, valid, sink):
    sc = jnp.where(valid, sc, NEG_INF)
    m = jnp.maximum(jnp.max(sc, axis=-1, keepdims=True), sink)
    e = jnp.exp(sc - m)
    es = jnp.exp(sink - m)
    den = jnp.sum(e, axis=-1, keepdims=True) + es
    return e / den, es / den


def _stack_halves(placed, kh):
    return jnp.concatenate([placed[kh, 0], placed[kh, 1]], axis=0)


def _valid_mask(n):
    r_i = lax.broadcasted_iota(jnp.int32, (BLOCK, 2 * BLOCK), 0)
    j_i = lax.broadcasted_iota(jnp.int32, (BLOCK, 2 * BLOCK), 1)
    return (j_i > r_i) & (j_i <= r_i + BLOCK) & ((n > 0) | (j_i >= BLOCK))


def attn_fwd(qs, ks, vb, z, sinks, aw, d, ga_off):
    s = qs.shape[0]
    nb = s // BLOCK
    nq = aw // HEAD_DIM
    grp_sz = nq // N_KV_HEADS
    n_ga = aw // COLT

    def body(q_ref, ko_ref, kp_ref, vo_ref, vp_ref, *rest):
        ga_refs = rest[:n_ga]
        sink_ref, attn_ref, mix_ref = rest[n_ga:]
        n = pl.program_id(0)
        lane_idx = lax.broadcasted_iota(jnp.int32, (2 * BLOCK, LANES), 1)
        kpl = _placed(jnp.concatenate([kp_ref[...], ko_ref[...]], axis=0), lane_idx)
        vpl = _placed(jnp.concatenate([vp_ref[...], vo_ref[...]], axis=0), lane_idx)
        valid = _valid_mask(n)
        groups = range(nq // 2)
        kcat = [_stack_halves(kpl, kh) for kh in range(N_KV_HEADS)]
        vcat = [_stack_halves(vpl, kh) for kh in range(N_KV_HEADS)]
        scs = [lax.dot_general(q_ref[:, c * LANES:(c + 1) * LANES], kcat[2 * c // grp_sz], NT,
                               preferred_element_type=F32) for c in groups]
        probs = [jnp.concatenate(
            [_softmax_with_sink(scs[c][:, half * 2 * BLOCK:(half + 1) * 2 * BLOCK], valid,
                                sink_ref[0, 2 * c + half])[0].astype(BF16) for half in (0, 1)], axis=1)
                 for c in groups]
        for c in groups:
            acc = jnp.dot(probs[c], vcat[2 * c // grp_sz], preferred_element_type=F32)
            attn_ref[:, c * LANES:(c + 1) * LANES] = acc
            col = c * LANES
            ga = ga_refs[col // COLT][:, col % COLT:col % COLT + LANES]
            mix_ref[:, c * LANES:(c + 1) * LANES] = (acc * (ga * _sigmoid(ga))).astype(BF16)

    own = lambda n: (n, 0)
    prev = lambda n: (jnp.maximum(n - 1, 0), 0)
    kvs = lambda imap: pl.BlockSpec((BLOCK, KV_WIDTH), imap)
    ga_specs = [pl.BlockSpec((BLOCK, COLT), functools.partial(lambda n, j: (n, ga_off + j), j=j))
                for j in range(n_ga)]
    return _call(body, name="attn_fwd", grid=(nb,),
                 in_specs=[pl.BlockSpec((BLOCK, aw), own), kvs(own), kvs(prev), kvs(own), kvs(prev)]
                 + ga_specs + [pl.BlockSpec(memory_space=pltpu.SMEM)],
                 out_specs=[pl.BlockSpec((BLOCK, aw), own), pl.BlockSpec((BLOCK, aw), own)],
                 out_shape=[jax.ShapeDtypeStruct((s, aw), F32), jax.ShapeDtypeStruct((s, d), BF16)],
                 compiler_params=_params(("parallel",)))(qs, ks, ks, vb, vb, *([z] * n_ga), sinks)


def out_proj_bwd_gate(d_x1b, wo, attn, z, aw, ga_off, after=()):
    s, d = d_x1b.shape
    tm = _tile(s, 512)
    n_ga = aw // COLT

    def body(dx_ref, w_ref, at_ref, *rest):
        ga_refs, (do_ref, dga_ref, dmc_ref) = rest[:n_ga], rest[n_ga:]
        dm = lax.dot_general(dx_ref[...], w_ref[...], NT, preferred_element_type=F32)
        dmc_ref[...] = dm[:, aw:]
        for j in range(n_ga):
            cols = slice(j * COLT, (j + 1) * COLT)
            ga = ga_refs[j][...]
            sig = _sigmoid(ga)
            dma = dm[:, cols]
            do_ref[:, cols] = (dma * (ga * sig)).astype(BF16)
            dga_ref[:, cols] = ((dma * at_ref[:, cols]) * (sig * (1.0 + ga * (1.0 - sig)))).astype(BF16)

    row = lambda width: pl.BlockSpec((tm, width), lambda i: (i, 0))
    ga_specs = [pl.BlockSpec((tm, COLT), functools.partial(lambda i, j: (i, ga_off + j), j=j)) for j in range(n_ga)]
    return _call_after(body, after, name="out_proj_bwd_gate", grid=(s // tm,),
                       in_specs=[row(d), pl.BlockSpec((d, d), lambda i: (0, 0), pipeline_mode=pl.Buffered(1)),
                                 row(aw)] + ga_specs,
                       out_specs=[row(aw), row(aw), row(d - aw)],
                       out_shape=[jax.ShapeDtypeStruct((s, aw), BF16), jax.ShapeDtypeStruct((s, aw), BF16),
                                  jax.ShapeDtypeStruct((s, d - aw), F32)],
                       compiler_params=_params(("parallel",)))(d_x1b, wo, attn, *([z] * n_ga))


def attn_bwd(qs, ks, vb, d_o, sinks, aw, after=()):
    s = qs.shape[0]
    nb = s // BLOCK
    nq = aw // HEAD_DIM
    grp_sz = nq // N_KV_HEADS

    def body(q_ref, ko_ref, kp_ref, vo_ref, vp_ref, do_ref, sink_ref, dq_ref, dk_ref, dv_ref, ds_ref,
             ck_ref, cv_ref):
        n = pl.program_id(0)

        @pl.when(n == 0)
        def _():
            ds_ref[...] = jnp.zeros_like(ds_ref)
            dk_ref[...] = jnp.zeros_like(dk_ref)
            dv_ref[...] = jnp.zeros_like(dv_ref)

        @pl.when(n < nb)
        def _():
            lane_idx = lax.broadcasted_iota(jnp.int32, (2 * BLOCK, LANES), 1)
            lane_row = lax.broadcasted_iota(jnp.int32, (1, LANES), 1)
            kpl = _placed(jnp.concatenate([kp_ref[...], ko_ref[...]], axis=0), lane_idx)
            vpl = _placed(jnp.concatenate([vp_ref[...], vo_ref[...]], axis=0), lane_idx)
            valid = _valid_mask(n)
            ds_row = jnp.zeros((1, LANES), F32)
            wide = 2 * BLOCK
            groups = range(nq // 2)
            per_kv = grp_sz // 2
            kcat = [_stack_halves(kpl, kh) for kh in range(N_KV_HEADS)]
            vcat = [_stack_halves(vpl, kh) for kh in range(N_KV_HEADS)]
            qg = [q_ref[:, c * LANES:(c + 1) * LANES] for c in groups]
            dog = [do_ref[:, c * LANES:(c + 1) * LANES] for c in groups]
            scs = [lax.dot_general(qg[c], kcat[c // per_kv], NT, preferred_element_type=F32) for c in groups]
            dps = [lax.dot_general(dog[c], vcat[c // per_kv], NT, preferred_element_type=F32) for c in groups]
            ds_pairs, p_pairs = [], []
            for c in groups:
                probs, dss = [], []
                for half in (0, 1):
                    h = 2 * c + half
                    cols = slice(half * wide, (half + 1) * wide)
                    p, p_sink = _softmax_with_sink(scs[c][:, cols], valid, sink_ref[0, h])
                    delta = jnp.sum(p * dps[c][:, cols], axis=-1, keepdims=True)
                    dss.append((p * (dps[c][:, cols] - delta)).astype(BF16))
                    probs.append(p.astype(BF16))
                    dsink = -jnp.sum(p_sink * delta, axis=0, keepdims=True)
                    ds_row += jnp.where(lane_row == h, dsink, 0.0)
                ds_pairs.append(jnp.concatenate(dss, axis=1))
                p_pairs.append(jnp.concatenate(probs, axis=1))
            for c in groups:
                dq_ref[:, c * LANES:(c + 1) * LANES] = jnp.dot(ds_pairs[c], kcat[c // per_kv],
                                                               preferred_element_type=F32)
            dkts = [lax.dot_general(qg[c], ds_pairs[c], TN, preferred_element_type=F32) for c in groups]
            dvts = [lax.dot_general(dog[c], p_pairs[c], TN, preferred_element_type=F32) for c in groups]
            dkt, dvt = [], []
            for kh in range(N_KV_HEADS):
                for parts, out in ((dkts, dkt), (dvts, dvt)):
                    tot = parts[kh * per_kv]
                    for c in range(kh * per_kv + 1, (kh + 1) * per_kv):
                        tot = tot + parts[c]
                    out.append(tot[:HEAD_DIM, :wide] + tot[HEAD_DIM:, wide:])
            ds_ref[0:1, :] += ds_row
            back = lambda t: jnp.concatenate(
                [jnp.concatenate(t[2 * g:2 * g + 2], axis=0).T for g in range(N_KV_HEADS // 2)], axis=1)
            dk_band, dv_band = back(dkt), back(dvt)

            @pl.when(n > 0)
            def _():
                dk_ref[...] = ck_ref[...] + dk_band[:BLOCK]
                dv_ref[...] = cv_ref[...] + dv_band[:BLOCK]

            ck_ref[...] = dk_band[BLOCK:]
            cv_ref[...] = dv_band[BLOCK:]

        @pl.when(n == nb)
        def _():
            dk_ref[...] = ck_ref[...]
            dv_ref[...] = cv_ref[...]

    own = lambda n: (jnp.minimum(n, nb - 1), 0)
    prev = lambda n: (jnp.clip(n - 1, 0, nb - 1), 0)
    done = lambda n: (jnp.maximum(n - 1, 0), 0)
    kvs = lambda imap: pl.BlockSpec((BLOCK, KV_WIDTH), imap)
    return _call_after(body, after, name="attn_bwd", grid=(nb + 1,),
                       in_specs=[pl.BlockSpec((BLOCK, aw), own), kvs(own), kvs(prev), kvs(own), kvs(prev),
                                 pl.BlockSpec((BLOCK, aw), own), pl.BlockSpec(memory_space=pltpu.SMEM)],
                       out_specs=[pl.BlockSpec((BLOCK, aw), own), kvs(done), kvs(done),
                                  pl.BlockSpec((SUBLANES, LANES), lambda n: (0, 0))],
                       out_shape=[jax.ShapeDtypeStruct((s, aw), F32), jax.ShapeDtypeStruct((s, KV_WIDTH), F32),
                                  jax.ShapeDtypeStruct((s, KV_WIDTH), F32),
                                  jax.ShapeDtypeStruct((SUBLANES, LANES), F32)],
                       scratch_shapes=[pltpu.VMEM((BLOCK, KV_WIDTH), F32), pltpu.VMEM((BLOCK, KV_WIDTH), F32)],
                       compiler_params=_params(("arbitrary",)))(qs, ks, ks, vb, vb, d_o, sinks)


def conv_fwd(z, conv_w, mix, aw, cw, b_off):
    s = z.shape[0]
    tm = _tile(s, 1024)
    nseg = cw // COLT
    hb = tm // SUBLANES

    def body(b_ref, c_ref, h_ref, g_ref, cp_ref, hp_ref, w_ref, mix_in, mix_ref):
        i = pl.program_id(0)
        u = c_ref[...] * h_ref[...]
        up = jnp.where(i > 0, cp_ref[...] * hp_ref[...], 0.0)
        ext = jnp.concatenate([up, u], axis=0)
        um1 = pltpu.roll(ext, 1, 0)[SUBLANES:]
        um2 = pltpu.roll(ext, 2, 0)[SUBLANES:]
        w = w_ref[...]
        cv = w[0:1] * um2 + w[1:2] * um1 + w[2:3] * u
        g = g_ref[...]
        mix_ref[...] = ((b_ref[...] * cv) * (g * _sigmoid(g))).astype(BF16)

    seg = lambda k: pl.BlockSpec((tm, COLT), functools.partial(lambda i, j, k: (i, b_off + k * nseg + j), k=k))
    halo = lambda k: pl.BlockSpec(
        (SUBLANES, COLT), functools.partial(lambda i, j, k: (jnp.maximum(i * hb - 1, 0), b_off + k * nseg + j), k=k))
    return _call(body, name="conv_fwd", grid=(s // tm, nseg),
                 in_specs=[seg(0), seg(1), seg(2), seg(3), halo(1), halo(2),
                           pl.BlockSpec((SUBLANES, COLT), lambda i, j: (0, j)), _hbm()],
                 out_specs=pl.BlockSpec((tm, COLT), lambda i, j: (i, aw // COLT + j)),
                 out_shape=jax.ShapeDtypeStruct(mix.shape, BF16),
                 input_output_aliases={7: 0},
                 compiler_params=_params(("parallel", "parallel")))(z, z, z, z, z, z, conv_w, mix)


def conv_bwd(z, d_mix, conv_w, dz, aw, cw, b_off):
    s = z.shape[0]
    tm = _tile(s, 512)
    nseg = cw // COLT
    hb = tm // SUBLANES
    n_row = s // tm
    last_h = s // SUBLANES - 1
    n_step = nseg * n_row

    def body(b_ref, c_ref, h_ref, g_ref, cp_ref, hp_ref, bn_ref, gn_ref, dm_ref, dmn_ref, w_ref, dz_in,
             dz_ref, acc_ref, stash_ref, sems):
        j = pl.program_id(0)
        i = pl.program_id(1)
        step = j * n_row + i
        slot = step % 2

        def copies(from_slot, at_step):
            jj, ii = at_step // n_row, at_step % n_row
            rows = pl.ds(pl.multiple_of(ii * tm, tm), tm)
            return [pltpu.make_async_copy(
                stash_ref.at[from_slot, q],
                dz_ref.at[rows, pl.ds(pl.multiple_of((b_off + q * nseg + jj) * COLT, COLT), COLT)],
                sems.at[from_slot, q]) for q in range(4)]

        @pl.when(step >= 2)
        def _():
            for cp in copies(slot, step - 2):
                cp.wait()

        cc, hh, bb, g = c_ref[...], h_ref[...], b_ref[...], g_ref[...]
        w = w_ref[...]
        u = cc * hh
        up = jnp.where(i > 0, cp_ref[...] * hp_ref[...], 0.0)
        ext = jnp.concatenate([up, u], axis=0)
        um1 = pltpu.roll(ext, 1, 0)[SUBLANES:]
        um2 = pltpu.roll(ext, 2, 0)[SUBLANES:]
        cv = w[0:1] * um2 + w[1:2] * um1 + w[2:3] * u
        sig = _sigmoid(g)
        sg = g * sig
        dm = dm_ref[...]
        d_cv = (dm * bb) * sg
        gn = gn_ref[...]
        d_cv_next = jnp.where(i < n_row - 1, (dmn_ref[...] * bn_ref[...]) * (gn * _sigmoid(gn)), 0.0)
        ext2 = jnp.concatenate([d_cv, d_cv_next], axis=0)
        dp1 = pltpu.roll(ext2, tm + SUBLANES - 1, 0)[:tm]
        dp2 = pltpu.roll(ext2, tm + SUBLANES - 2, 0)[:tm]
        d_u = w[2:3] * d_cv + w[1:2] * dp1 + w[0:1] * dp2
        stash_ref[slot, 0] = ((dm * cv) * sg).astype(BF16)
        stash_ref[slot, 1] = (d_u * hh).astype(BF16)
        stash_ref[slot, 2] = (d_u * cc).astype(BF16)
        stash_ref[slot, 3] = (((dm * bb) * cv) * (sig * (1.0 + g * (1.0 - sig)))).astype(BF16)
        for cp in copies(slot, step):
            cp.start()

        @pl.when(i == 0)
        def _():
            acc_ref[...] = jnp.zeros_like(acc_ref)

        acc_ref[0:1, :] += jnp.sum(d_cv * um2, axis=0, keepdims=True)
        acc_ref[1:2, :] += jnp.sum(d_cv * um1, axis=0, keepdims=True)
        acc_ref[2:3, :] += jnp.sum(d_cv * u, axis=0, keepdims=True)

        @pl.when(step == n_step - 1)
        def _():
            if n_step >= 2:
                for cp in copies(1 - slot, step - 1):
                    cp.wait()
            for cp in copies(slot, step):
                cp.wait()

    seg = lambda q: pl.BlockSpec((tm, COLT), functools.partial(lambda j, i, q: (i, b_off + q * nseg + j), q=q))
    halo_p = lambda q: pl.BlockSpec(
        (SUBLANES, COLT),
        functools.partial(lambda j, i, q: (jnp.maximum(i * hb - 1, 0), b_off + q * nseg + j), q=q))
    halo_n = lambda q: pl.BlockSpec(
        (SUBLANES, COLT),
        functools.partial(lambda j, i, q: (jnp.minimum((i + 1) * hb, last_h), b_off + q * nseg + j), q=q))
    return _call(body, name="conv_bwd", grid=(nseg, n_row),
                 in_specs=[seg(0), seg(1), seg(2), seg(3), halo_p(1), halo_p(2), halo_n(0), halo_n(3),
                           pl.BlockSpec((tm, COLT), lambda j, i: (i, j)),
                           pl.BlockSpec((SUBLANES, COLT), lambda j, i: (jnp.minimum((i + 1) * hb, last_h), j)),
                           pl.BlockSpec((SUBLANES, COLT), lambda j, i: (0, j)), _hbm()],
                 out_specs=[_hbm(), pl.BlockSpec((SUBLANES, COLT), lambda j, i: (0, j))],
                 out_shape=[jax.ShapeDtypeStruct(dz.shape, BF16), jax.ShapeDtypeStruct((SUBLANES, cw), F32)],
                 input_output_aliases={11: 0},
                 scratch_shapes=[pltpu.VMEM((2, 4, tm, COLT), BF16), pltpu.SemaphoreType.DMA((2, 4))],
                 compiler_params=_params(("arbitrary", "arbitrary")))(
                     z, z, z, z, z, z, z, z, d_mix, d_mix, conv_w, dz)


def _place():
    x, y, c = lax.axis_index("x"), lax.axis_index("y"), lax.axis_index("c")
    chips = [(1 - x, y), (x, 1 - y), (1 - x, 1 - y)]
    return x, y, c, chips


def _remote(src, dst, send_sem, recv_sem, device):
    return pltpu.make_async_remote_copy(src_ref=src, dst_ref=dst, send_sem=send_sem, recv_sem=recv_sem,
                                        device_id=device, device_id_type=MESH)


def plan_gather_ici(n_split):
    def plan(refs, send, recv):
        x, y, c, chips = _place()
        me = 2 * x + y
        mine, theirs = [], []
        for w, ref in enumerate(refs):
            for j, (cx, cy) in enumerate(chips):
                def part(slot):
                    if w >= n_split:
                        return ref.at[slot]
                    hr = ref.shape[1] // 2
                    return ref.at[slot, pl.ds(c * hr, hr)]
                k = 3 * w + j
                mine.append(_remote(part(me), part(me), send.at[k], recv.at[k], (cx, cy, c)))
                theirs.append(_remote(part(2 * cx + cy), part(2 * cx + cy), send.at[k], recv.at[k], (cx, cy, c)))
        return mine, theirs
    return plan


def plan_shard_ici(j):
    def plan(refs, send, recv):
        _, _, c, chips = _place()
        own, land = refs
        hr = own.shape[0] // 2
        rows = pl.ds(c * hr, hr)
        cp = _remote(own.at[rows], land.at[rows], send.at[0], recv.at[0], (*chips[j], c))
        return [cp], [cp]
    return plan


def plan_shard_relay(refs, send, recv):
    _, _, c, chips = _place()
    land_x, land_y, land_far = refs
    hr = land_far.shape[0] // 2
    quarter = lambda q: pl.ds(c * hr + q * (hr // 2), hr // 2)
    mine = [_remote(land_y.at[quarter(0)], land_far.at[quarter(0)], send.at[0], recv.at[0], (*chips[0], c)),
            _remote(land_x.at[quarter(1)], land_far.at[quarter(1)], send.at[1], recv.at[1], (*chips[1], c))]
    return mine, mine


def plan_shard_pass(refs, send, recv):
    x, y, c, _ = _place()
    mine, theirs = [], []
    for w, land in enumerate(refs):
        hr = land.shape[0] // 2
        half = lambda core: land.at[pl.ds(core * hr, hr)]
        mine.append(_remote(half(c), half(c), send.at[w], recv.at[w], (x, y, 1 - c)))
        theirs.append(_remote(half(1 - c), half(1 - c), send.at[w], recv.at[w], (x, y, 1 - c)))
    return mine, theirs


def plan_gather_pass(refs, send, recv):
    x, y, c, chips = _place()
    mine, theirs = [], []
    for w, ref in enumerate(refs):
        hr = ref.shape[1] // 2
        for j, (cx, cy) in enumerate(chips):
            half = lambda core: ref.at[2 * cx + cy, pl.ds(core * hr, hr)]
            k = 3 * w + j
            mine.append(_remote(half(c), half(c), send.at[k], recv.at[k], (x, y, 1 - c)))
            theirs.append(_remote(half(1 - c), half(1 - c), send.at[k], recv.at[k], (x, y, 1 - c)))
    return mine, theirs


def plan_swap(refs, send, recv):
    x, y, c, _ = _place()
    nw = len(refs) // 2
    mine = []
    for w in range(nw):
        hr = refs[w].shape[1] // 2
        mine.append(_remote(refs[w].at[:, pl.ds((1 - c) * hr, hr), :], refs[nw + w], send.at[w], recv.at[w],
                            (x, y, 1 - c)))
    return mine, mine


def plan_scatter(refs, send, recv):
    x, y, c, chips = _place()
    me = 2 * x + y
    nw = len(refs) // 2
    mine, theirs = [], []
    for w in range(nw):
        for j, (cx, cy) in enumerate(chips):
            k = 3 * w + j
            mine.append(_remote(refs[w].at[2 * cx + cy], refs[nw + w].at[me], send.at[k], recv.at[k], (cx, cy, c)))
            theirs.append(_remote(refs[w].at[me], refs[nw + w].at[2 * cx + cy], send.at[k], recv.at[k], (cx, cy, c)))
    return mine, theirs


def plan_join(refs, send, recv):
    x, y, c, _ = _place()
    mine, theirs = [], []
    for w, ref in enumerate(refs):
        hr = ref.shape[0] // 2
        half = lambda core: ref.at[pl.ds(core * hr, hr)]
        mine.append(_remote(half(c), half(c), send.at[w], recv.at[w], (x, y, 1 - c)))
        theirs.append(_remote(half(1 - c), half(1 - c), send.at[w], recv.at[w], (x, y, 1 - c)))
    return mine, theirs


def exchange(name, plan, arrays, n, after=()):
    na = len(arrays)

    def body(*refs):
        mine, theirs = plan(refs[:na], refs[2 * na], refs[2 * na + 1])
        for cp in mine:
            cp.start()
        for cp in theirs:
            cp.wait_recv()
        for cp in mine:
            cp.wait_send()

    return _call_after(body, after, name=name, in_specs=[_hbm()] * na, out_specs=[_hbm()] * na,
                       out_shape=[jax.ShapeDtypeStruct(a.shape, a.dtype) for a in arrays],
                       input_output_aliases={i: i for i in range(na)},
                       scratch_shapes=[pltpu.SemaphoreType.DMA((n,)), pltpu.SemaphoreType.DMA((n,))])(*arrays)


def exchange_start(name, groups, arrays, after=()):
    na, ng = len(arrays), len(groups)

    def body(*refs):
        for g, (plan, idx, _) in enumerate(groups):
            mine, _ = plan([refs[i] for i in idx], refs[na + 2 * g], refs[na + 2 * g + 1])
            for cp in mine:
                cp.start()
        refs[-1][...] = jnp.zeros_like(refs[-1])

    hbm = pl.BlockSpec(memory_space=pltpu.HBM)
    sem = pl.BlockSpec(memory_space=pltpu.SEMAPHORE)
    sem_types = [pltpu.SemaphoreType.DMA((n,)) for _, _, n in groups for _ in (0, 1)]
    outs = _call_after(body, after, name=name, in_specs=[hbm] * na,
                       out_specs=[sem] * (2 * ng) + [hbm] * na + [pl.BlockSpec(memory_space=pltpu.VMEM)],
                       out_shape=sem_types + [pltpu.HBM(a.shape, a.dtype) for a in arrays]
                       + [jax.ShapeDtypeStruct((SUBLANES, LANES), F32)],
                       input_output_aliases={i: i + 2 * ng for i in range(na)},
                       compiler_params=pltpu.CompilerParams(
                           has_side_effects=pltpu.SideEffectType.DATAFLOW_SIDE_EFFECTING))(
                               *[pltpu.with_memory_space_constraint(a, pltpu.HBM) for a in arrays])
    sems = [(outs[2 * g], outs[2 * g + 1]) for g in range(ng)]
    return sems, list(outs[2 * ng:-1]), outs[-1]


def exchange_wait(name, plan, sems, arrays, after):
    send_sems, recv_sems = sems
    na = len(arrays)
    after = tuple(after) if isinstance(after, (tuple, list)) else (after,)

    def body(*refs):
        mine, theirs = plan(refs[:na], refs[na], refs[na + 1])
        for cp in mine:
            cp.wait_send()
        for cp in theirs:
            cp.wait_recv()

    hbm = pl.BlockSpec(memory_space=pltpu.HBM)
    sem = pl.BlockSpec(memory_space=pltpu.SEMAPHORE)
    return _call(body, name=name, in_specs=[hbm] * na + [sem, sem] + [_hbm()] * len(after),
                 out_specs=[hbm] * na, out_shape=[pltpu.HBM(a.shape, a.dtype) for a in arrays],
                 input_output_aliases={i: i for i in range(na)},
                 compiler_params=pltpu.CompilerParams(
                     has_side_effects=pltpu.SideEffectType.DATAFLOW_SIDE_EFFECTING))(
                         *arrays, send_sems, recv_sems, *after)


def plan_allgather_small(refs, send, recv):
    x, y, c, _ = _place()
    buf, = refs
    mine, theirs = [], []
    flips = [(fx, fy, fc) for fx in (0, 1) for fy in (0, 1) for fc in (0, 1)][1:]
    for k, (fx, fy, fc) in enumerate(flips):
        peer = (x ^ fx, y ^ fy, c ^ fc)
        slot = lambda px, py, pc: buf.at[4 * px + 2 * py + pc]
        mine.append(_remote(slot(x, y, c), slot(x, y, c), send.at[k], recv.at[k], peer))
        theirs.append(_remote(slot(*peer), slot(*peer), send.at[k], recv.at[k], peer))
    return mine, theirs


def add_sibling(grad, got, core, name):
    _, r, c = grad.shape
    hr = r // 2
    tr = _tile(hr, 512)
    nblk = hr // tr

    def body(core_ref, g_ref, o_ref, out_ref):
        out_ref[...] = (g_ref[...].astype(F32) + o_ref[...].astype(F32)).astype(BF16)

    grid_spec = pltpu.PrefetchScalarGridSpec(
        num_scalar_prefetch=1, grid=(N_CHIPS, nblk),
        in_specs=[pl.BlockSpec((None, tr, c), lambda t, i, core_ref: (t, core_ref[0] * nblk + i, 0)),
                  pl.BlockSpec((None, tr, c), lambda t, i, core_ref: (t, i, 0))],
        out_specs=pl.BlockSpec((None, tr, c), lambda t, i, core_ref: (t, i, 0)))
    return _call(body, name=name, grid_spec=grid_spec,
                 out_shape=jax.ShapeDtypeStruct((N_CHIPS, hr, c), BF16),
                 compiler_params=_params(("parallel", "parallel")))(core, grad, got)


def sum_chips(mine, owned, place, name):
    _, hr, c = mine.shape
    tr = _tile(hr, 512)
    nblk = hr // tr

    def body(place_ref, m_ref, o1_ref, o2_ref, o3_ref, out_ref):
        acc = m_ref[...].astype(F32)
        for o_ref in (o1_ref, o2_ref, o3_ref):
            acc = acc + o_ref[...].astype(F32)
        out_ref[...] = acc

    other = lambda k: pl.BlockSpec((None, tr, c), lambda i, place_ref: ((place_ref[0] + k) % N_CHIPS, i, 0))
    grid_spec = pltpu.PrefetchScalarGridSpec(
        num_scalar_prefetch=1, grid=(nblk,),
        in_specs=[other(0), other(1), other(2), other(3)],
        out_specs=pl.BlockSpec((tr, c), lambda i, place_ref: (place_ref[1] * nblk + i, 0)))
    return _call(body, name=name, grid_spec=grid_spec,
                 out_shape=jax.ShapeDtypeStruct((2 * hr, c), F32),
                 compiler_params=_params(("parallel",)))(place, mine, owned, owned, owned)


def small_step(gathered, chip, rows, params, conv_row, conv):
    n_dev, n_rows, _ = gathered.shape
    _, cr, cq = conv[0].shape
    groups = list(params) + [conv]
    n_par = len(groups)

    def body(chip_ref, g_ref, gc_ref, *refs):
        ins, outs, loss_ref = refs[:3 * n_par], refs[3 * n_par:7 * n_par], refs[7 * n_par]
        tot = g_ref[0]
        conv_g = gc_ref[0, conv_row:conv_row + cr, :]
        for dev in range(1, n_dev):
            tot = tot + g_ref[dev]
            conv_g = conv_g + gc_ref[dev, conv_row:conv_row + cr, :]
        loss_ref[...] = jnp.sum(tot[SUBLANES:SUBLANES + 1], axis=1, keepdims=True)
        grads = [tot[r:r + 1, :w] for r, w in rows] + [conv_g[None]]
        for i, gv in enumerate(grads):
            _adamw_step(gv, *ins[3 * i:3 * i + 3], *outs[4 * i:4 * i + 4])

    def whole(a):
        return pl.BlockSpec(a.shape, lambda i, chip_ref, n=len(a.shape): (0,) * n)

    flat = [a for group in groups for a in group]
    out_shape = [jax.ShapeDtypeStruct(group[0].shape, F32) for group in groups for _ in range(4)]
    out_shape.append(jax.ShapeDtypeStruct((1, 1), F32))
    grid_spec = pltpu.PrefetchScalarGridSpec(
        num_scalar_prefetch=1, grid=(1,),
        in_specs=[whole(gathered), pl.BlockSpec((n_dev, n_rows, cq), lambda i, chip_ref: (0, 0, chip_ref[0]))]
        + [whole(a) for a in flat],
        out_specs=[whole(o) for o in out_shape])
    res = _call(body, name="small_step", grid_spec=grid_spec, out_shape=out_shape,
                compiler_params=_params(("arbitrary",)))(chip, gathered, gathered, *flat)
    return res[-1], [res[4 * i:4 * i + 4] for i in range(n_par)]


def _rope_tables(s):
    half = ROT_DIM // 2
    inv_freq = jnp.power(jnp.float32(ROPE_THETA), -jnp.arange(half, dtype=F32) * 2.0 / ROT_DIM)
    freq64 = jnp.concatenate([inv_freq, inv_freq, jnp.zeros((HEAD_DIM - ROT_DIM,), F32)])
    freq = jnp.concatenate([freq64, freq64])[None, :]
    dim = (jnp.arange(LANES) % HEAD_DIM)[None, :]
    ang = jnp.arange(s).astype(F32)[:, None] * freq
    cos, sin = jnp.cos(ang), jnp.sin(ang)
    return cos, jnp.where(dim < half, -sin, 0.0), jnp.where((dim >= half) & (dim < ROT_DIM), sin, 0.0)


def _pad_rows(a, rows):
    return jnp.pad(a, ((0, rows - a.shape[0]), (0, 0)))


def _pad_cols(a, cols):
    return jnp.pad(a, ((0, 0), (0, cols - a.shape[1])))


def kernel(x, p, norm_gain, w_in, q_norm_gain, k_norm_gain, attn_sinks, conv_w, w_out, ple_gate_norm_gain, w_ple_gate, b_ple_gate, w_ple_proj, ple_norm_gain, loss_target, m_norm_gain, m_w_in, m_q_norm_gain, m_k_norm_gain, m_attn_sinks, m_conv_w, m_w_out, m_ple_gate_norm_gain, m_w_ple_gate, m_b_ple_gate, m_w_ple_proj, m_ple_norm_gain, v_norm_gain, v_w_in, v_q_norm_gain, v_k_norm_gain, v_attn_sinks, v_conv_w, v_w_out, v_ple_gate_norm_gain, v_w_ple_gate, v_b_ple_gate, v_w_ple_proj, v_ple_norm_gain):
    x2, p2, tgt = x[0], p[0, 0], loss_target[0]
    s, d = x2.shape
    aw = d // 2
    cw = d - aw
    nq = aw // HEAD_DIM
    sh = w_in.shape[2]
    in_w = N_CHIPS * sh
    dq = d // N_CHIPS
    cq = cw // N_CHIPS
    ga_off = (aw + 2 * KV_WIDTH) // COLT
    b_off = (2 * aw + 2 * KV_WIDTH) // COLT
    assert in_w == 2 * aw + 2 * KV_WIDTH + 4 * cw and aw % COLT == 0 and cw % COLT == 0
    assert s % BLOCK == 0 and sh % LANES == 0 and nq % (2 * N_KV_HEADS) == 0

    core = lax.axis_index("c").astype(jnp.int32).reshape(1)
    chip = 2 * lax.axis_index("x") + lax.axis_index("y")

    chip1 = chip.astype(jnp.int32).reshape(1)
    place = jnp.concatenate([chip1, core])
    w_in_own = cast_bf16(w_in[0], "cast_w_in")
    rest = [cast_into_slot(w_out[0], chip1, "cast_w_out"), cast_into_slot(w_ple_gate[0], chip1, "cast_w_pg"),
            cast_into_slot(w_ple_proj[0], chip1, "cast_w_pp"),
            lax.dynamic_update_slice(jnp.zeros((N_CHIPS, SUBLANES, cq), F32),
                                     _pad_rows(conv_w[0], SUBLANES)[None], (chip, 0, 0))]
    order = jnp.stack([chip, chip ^ 2, chip ^ 1, chip ^ 3]).astype(jnp.int32)
    wi_sems, wi_arrs, wi_token = exchange_start(
        "gather_wi_start", [(plan_shard_ici(j), [0, 1 + j], 1) for j in range(2)],
        [w_in_own] + [lax.empty((d, sh), BF16) for _ in range(3)])

    tn = _tile(d, 1024)
    ts = _tile(s, 2048)
    tabs = _rope_tables(s)
    qg = jnp.tile(q_norm_gain, (1, LANES // HEAD_DIM))
    kg = jnp.tile(k_norm_gain, (1, LANES // HEAD_DIM))
    seg_i = jnp.arange(SEG) // HEAD_DIM
    segb = (seg_i[:, None] == seg_i[None, :]).astype(BF16)
    h, z = norm_in_proj(x2, norm_gain, wi_arrs[0], order, in_w)
    own, land_x = exchange_wait("gather_wi_wait0", plan_shard_ici(0), wi_sems[0], [wi_arrs[0], wi_arrs[1]],
                                (z,) + tuple(tabs) + tuple(rest[:2]))
    own, land_y = exchange_wait("gather_wi_wait1", plan_shard_ici(1), wi_sems[1], [own, wi_arrs[2]], land_x)
    (relay_sems, rest_sems, y_sems), started, rest_token = exchange_start(
        "gather_relay_rest_start", [(plan_shard_relay, [0, 1, 2], 2), (plan_gather_ici(3), [3, 4, 5, 6], 12),
                                    (plan_shard_pass, [1], 1)],
        [land_x, land_y, wi_arrs[3]] + rest)
    relayed, rest = started[:3], started[3:]
    land_x, = exchange("gather_wi_pass0", plan_shard_pass, relayed[:1], 1, after=(rest_token,))
    z = in_proj_part(h, land_x, z, order, 1, in_w)
    land_y, = exchange_wait("gather_wi_pass1_wait", plan_shard_pass, y_sems, [relayed[1]], z)
    land_x, land_y, land_far = exchange_wait("gather_wi_relay_wait", plan_shard_relay, relay_sems,
                                             [land_x, land_y, relayed[2]], z)
    (far_sems,), (land_far,), far_token = exchange_start(
        "gather_wi_pass2_start", [(plan_shard_pass, [0], 1)], [land_far])
    z = in_proj_part(h, land_y, z, order, 2, in_w, after=(far_token,))
    land_far, = exchange_wait("gather_wi_pass2_wait", plan_shard_pass, far_sems, [land_far], z)
    z = in_proj_part(h, land_far, z, order, 3, in_w)
    w_shards = [own, land_x, land_y, land_far]
    wo_all, wg_all, wp_all, conv_all = exchange_wait("gather_rest_wait", plan_gather_ici(3), rest_sems, rest, z)
    pass_sems, passed, pass_token = exchange_start(
        "gather_rest_pass_start", [(plan_gather_pass, [0, 1, 2], 9)], [wo_all, wg_all, wp_all])
    conv_full = conv_all.transpose(1, 0, 2).reshape(SUBLANES, cw)
    qs, ks, vb = qk_prep_fwd(z, tabs, qg, kg, segb, aw, after=(pass_token,))
    attn, mix = attn_fwd(qs, ks, vb, z, attn_sinks, aw, d, ga_off)
    mix = conv_fwd(z, conv_full, mix, aw, cw, b_off)
    wo_all, wg_all, wp_all = exchange_wait("gather_rest_pass_wait", plan_gather_pass, pass_sems[0], passed, mix)
    wo_full = wo_all.reshape(d, d)
    wg_full = wg_all.reshape(d, d)
    x1, hg = out_proj_norm(mix, wo_full, x2, ple_gate_norm_gain)

    d_gl, dy, g_wp, acc_head = head_fwd_bwd(x1, hg, wg_full, p2, wp_all, tgt, b_ple_gate, ple_norm_gain)
    wgrad = lambda name, a, b: matmul(
        a, b, grid=(d // tn, d // tn, s // ts),
        a_spec=pl.BlockSpec((ts, tn), lambda i, j, k: (k, i)),
        b_spec=pl.BlockSpec((ts, tn), lambda i, j, k: (k, j)),
        o_spec=pl.BlockSpec((tn, tn), lambda i, j, k: (i, j)),
        out_shape=jax.ShapeDtypeStruct((d, d), BF16), dims=TN, name=name)
    g_wg = wgrad("mm_g_wg", hg, d_gl)
    d_x1, d_x1b, acc_g = gate_proj_bwd_norm(d_gl, wg_full, x1, dy, ple_gate_norm_gain)
    g_wo = wgrad("mm_g_wo", mix, d_x1b)
    def reduce_sum(tag, handle, after):
        sems, arrays, _ = handle
        n = len(arrays) // 2
        got = exchange_wait("scatter_%s_wait" % tag, plan_scatter, sems[0], arrays, after)
        return [sum_chips(pt, o, place, "sum_chips_%s%d" % (tag, i))
                for i, (pt, o) in enumerate(zip(got[:n], got[n:]))]

    early_grads = [g_wo.reshape(N_CHIPS, dq, d), g_wg.reshape(N_CHIPS, dq, d), g_wp]
    eswap_sems, eswapping, eswap_token = exchange_start(
        "swap_early_start", [(plan_swap, list(range(6)), 3)],
        early_grads + [lax.empty((N_CHIPS, a.shape[1] // 2, a.shape[2]), BF16) for a in early_grads])
    d_o, d_gate, d_mix_conv = out_proj_bwd_gate(d_x1b, wo_full, attn, z, aw, ga_off, after=(eswap_token,))
    eswapped = exchange_wait("swap_early_wait", plan_swap, eswap_sems[0], eswapping, d_o)
    early_parts = [add_sibling(g, o, core, "add_sibling_early%d" % i)
                   for i, (g, o) in enumerate(zip(eswapped[:3], eswapped[3:]))]
    early = exchange_start("scatter_early_start", [(plan_scatter, list(range(6)), 9)],
                           early_parts + [lax.empty(a.shape, BF16) for a in early_parts])
    dqs, dks, dvs, acc_sink = attn_bwd(qs, ks, vb, d_o, attn_sinks, aw, after=(early[-1],))
    dz, acc_qk = qk_prep_bwd(z, dqs, dks, dvs, d_gate, tabs, qg, kg, segb, aw)
    dz, acc_conv = conv_bwd(z, d_mix_conv, conv_full, dz, aw, cw, b_off)
    join_sems, joining, join_token = exchange_start(
        "join_early_start", [(plan_join, [0, 1, 2], 3)], reduce_sum("early", early, dz))
    g_send = in_proj_wgrad_half(h, dz, None, 1 - core, after=(join_token,))
    swap_sems, swapping, swap_token = exchange_start(
        "swap_late_start", [(plan_swap, [0, 1], 1)], [g_send, lax.empty((N_CHIPS, d // 2, sh), BF16)])
    g_wi = in_proj_wgrad_half(h, dz, swapping[0], core, after=(swap_token,))
    full_wo, full_wg, full_wp = exchange_wait("join_early_wait", plan_join, join_sems[0], joining, g_wi)
    g_wi, got_wi = exchange_wait("swap_late_wait", plan_swap, swap_sems[0], [g_wi, swapping[1]], full_wo)
    part_wi = add_sibling(g_wi, got_wi, core, "add_sibling_late0")
    late = exchange_start("scatter_late_start", [(plan_scatter, [0, 1], 3)],
                          [part_wi, lax.empty(part_wi.shape, BF16)])
    grad_x, acc_x = in_proj_bwd_norm(dz, w_shards, order, x2, d_x1, norm_gain, after=(late[-1],))

    wsm = max(d, cw)
    small = _pad_rows(jnp.concatenate([
        _pad_cols(acc_x[0:1], wsm), _pad_cols(acc_g[0:1], wsm), _pad_cols(acc_head[0:1], wsm),
        _pad_cols(acc_head[1:2], wsm), _pad_cols(acc_qk[0:1, :HEAD_DIM], wsm), _pad_cols(acc_conv[0:3], wsm),
        _pad_cols(acc_head[2:3], wsm),
        _pad_cols(acc_qk[1:2, :HEAD_DIM], wsm), _pad_cols(acc_sink[0:1, :nq], wsm)], axis=0), 2 * SUBLANES)
    device = 2 * chip + lax.axis_index("c")
    (small_sems, last_sems), started, last_token = exchange_start(
        "small_join_late_start", [(plan_allgather_small, [0], 7), (plan_join, [1], 1)],
        [lax.dynamic_update_slice(jnp.zeros((8,) + small.shape, F32), small[None], (device, 0, 0))]
        + reduce_sum("late", late, grad_x))
    small_bufs, last_halves = started[:1], started[1:]
    big, after = {}, (last_token,)
    for nm, g, w, m, v in (("w_out", full_wo, w_out, m_w_out, v_w_out),
                           ("w_ple_gate", full_wg, w_ple_gate, m_w_ple_gate, v_w_ple_gate),
                           ("w_ple_proj", full_wp, w_ple_proj, m_w_ple_proj, v_w_ple_proj)):
        stepped = adamw(g, w[0], m[0], v[0], "adamw_" + nm, after=after)
        big[nm], after = [o[None] for o in stepped], (stepped[1],)
    full_wi, = exchange_wait("join_late_wait", plan_join, last_sems, last_halves, after[0])
    big["w_in"] = [o[None] for o in adamw(full_wi, w_in[0], m_w_in[0], v_w_in[0], "adamw_w_in")]

    gathered, = exchange_wait("small_wait", plan_allgather_small, small_sems, small_bufs, big["w_in"][1])
    rowwise = (("norm_gain", (0, d), (norm_gain, m_norm_gain, v_norm_gain)),
               ("ple_gate_norm_gain", (1, d), (ple_gate_norm_gain, m_ple_gate_norm_gain, v_ple_gate_norm_gain)),
               ("b_ple_gate", (2, d), (b_ple_gate, m_b_ple_gate, v_b_ple_gate)),
               ("ple_norm_gain", (3, d), (ple_norm_gain, m_ple_norm_gain, v_ple_norm_gain)),
               ("q_norm_gain", (4, HEAD_DIM), (q_norm_gain, m_q_norm_gain, v_q_norm_gain)),
               ("k_norm_gain", (SUBLANES + 1, HEAD_DIM), (k_norm_gain, m_k_norm_gain, v_k_norm_gain)),
               ("attn_sinks", (SUBLANES + 2, nq), (attn_sinks, m_attn_sinks, v_attn_sinks)))
    loss, stepped = small_step(gathered, chip1, [r for _, r, _ in rowwise], [t for _, _, t in rowwise],
                               5, (conv_w, m_conv_w, v_conv_w))
    table = {nm: stepped[i] for i, (nm, _, _) in enumerate(rowwise)}
    table["conv_w"] = stepped[-1]

    order = ("norm_gain", "w_in", "q_norm_gain", "k_norm_gain", "attn_sinks", "conv_w", "w_out",
             "ple_gate_norm_gain", "w_ple_gate", "b_ple_gate", "w_ple_proj", "ple_norm_gain")
    outs = [loss.reshape(()), grad_x[None]]
    for kind in range(4):
        for nm in order:
            outs.append(big[nm][kind] if nm in big else table[nm][kind])
    return tuple(outs)
```
